```python
import jax, jax.numpy as jnp
from jax import lax
import numpy as np

D_MODEL = 1024
BATCH = 8
SEQ = 2048
DEPTH = 1

N_META = 16
HEAD_DIM = 64
MIX_WIDTH = D_MODEL
W_A = MIX_WIDTH // 2
W_B = MIX_WIDTH - W_A
N_HEADS_A = W_A // HEAD_DIM
N_HEADS_B = W_B // HEAD_DIM
CONV_A_WIDTH = 3
CONV_B_WIDTH = 31
IN_PROJ_WIDTH = 3 * W_A + 2 * W_B
D_FF = ((8 * D_MODEL // 3 + 255) // 256) * 256
RMS_EPS = 1e-6
LN_EPS = 1e-5

kernel_name = "hybrid_shortconv_conformer_block"


def rms_norm(x, g):
    xf = x.astype(jnp.float32)
    y = xf * lax.rsqrt(jnp.mean(xf * xf, axis=-1, keepdims=True) + RMS_EPS)
    return (y * g.astype(jnp.float32)).astype(x.dtype)


def layer_norm(x, g, b):
    xf = x.astype(jnp.float32)
    mu = jnp.mean(xf, axis=-1, keepdims=True)
    var = jnp.mean(jnp.square(xf - mu), axis=-1, keepdims=True)
    y = (xf - mu) * lax.rsqrt(var + LN_EPS)
    return (y * g.astype(jnp.float32) + b.astype(jnp.float32)).astype(x.dtype)


def causal_depthwise_conv(u, w):
    k = w.shape[0]
    return lax.conv_general_dilated(
        u, w[:, None, :].astype(u.dtype),
        window_strides=(1,), padding=[(k - 1, 0)],
        dimension_numbers=("NWC", "WIO", "NWC"),
        feature_group_count=u.shape[-1])


def mixer(xn, w_in, conv_a_w, conv_b_w, conv_b_bias, ln_b_gain, ln_b_bias, w_out):
    h = jnp.einsum("btd,de->bte", xn, w_in)
    b_gate, c_gate, h_a, glu_val, glu_gate = jnp.split(
        h, [W_A, 2 * W_A, 3 * W_A, 3 * W_A + W_B], axis=-1)
    y_a = b_gate * causal_depthwise_conv(c_gate * h_a, conv_a_w)
    g = glu_val * jax.nn.sigmoid(glu_gate)
    z = causal_depthwise_conv(g, conv_b_w) + conv_b_bias.astype(g.dtype)
    y_b = jax.nn.silu(layer_norm(z, ln_b_gain, ln_b_bias))
    y = jnp.concatenate([y_a, y_b], axis=-1)
    return jnp.einsum("bte,ed->btd", y, w_out)


def swiglu(xn, w_gate, w_up, w_down):
    a = jnp.einsum("btd,df->btf", xn, w_gate)
    u = jnp.einsum("btd,df->btf", xn, w_up)
    return jnp.einsum("btf,fd->btd", jax.nn.silu(a) * u, w_down)


def _fwd_setup_inputs(seed: int = 0) -> dict:
    key = jax.random.key(seed)
    ks = jax.random.split(key, 20)
    f32 = jnp.float32

    def nrm(k, shape, scale):
        return jax.random.normal(k, shape, f32) * scale

    def gain(k, n):
        return jnp.ones((DEPTH, n), f32) + 0.05 * jax.random.normal(k, (DEPTH, n), f32)

    return {
        "x": jax.random.normal(ks[0], (BATCH, SEQ, D_MODEL), f32),
        "meta_tokens": nrm(ks[1], (N_META, D_MODEL), 1.0),
        "pre_mix_norm": gain(ks[2], D_MODEL),
        "w_in": nrm(ks[3], (DEPTH, D_MODEL, IN_PROJ_WIDTH), D_MODEL ** -0.5),
        "conv_a_w": nrm(ks[4], (DEPTH, CONV_A_WIDTH, W_A), CONV_A_WIDTH ** -0.5),
        "conv_b_w": nrm(ks[5], (DEPTH, CONV_B_WIDTH, W_B), CONV_B_WIDTH ** -0.5),
        "conv_b_bias": nrm(ks[6], (DEPTH, W_B), 0.02),
        "ln_b_gain": gain(ks[7], W_B),
        "ln_b_bias": nrm(ks[8], (DEPTH, W_B), 0.02),
        "w_out": nrm(ks[9], (DEPTH, MIX_WIDTH, D_MODEL), MIX_WIDTH ** -0.5),
        "post_mix_norm": gain(ks[10], D_MODEL),
        "pre_ffn_norm": gain(ks[11], D_MODEL),
        "w_gate": nrm(ks[12], (DEPTH, D_MODEL, D_FF), D_MODEL ** -0.5),
        "w_up": nrm(ks[13], (DEPTH, D_MODEL, D_FF), D_MODEL ** -0.5),
        "w_down": nrm(ks[14], (DEPTH, D_FF, D_MODEL), D_FF ** -0.5),
        "post_ffn_norm": gain(ks[15], D_MODEL),
    }


def _fwd_reference(x, meta_tokens, pre_mix_norm, w_in, conv_a_w, conv_b_w, conv_b_bias,
              ln_b_gain, ln_b_bias, w_out, post_mix_norm, pre_ffn_norm,
              w_gate, w_up, w_down, post_ffn_norm):
    b = x.shape[0]
    meta = jnp.broadcast_to(meta_tokens[None].astype(x.dtype), (b, N_META, x.shape[-1]))
    h = jnp.concatenate([meta, x], axis=1)
    for l in range(DEPTH):
        mix = mixer(rms_norm(h, pre_mix_norm[l]), w_in[l], conv_a_w[l], conv_b_w[l],
                    conv_b_bias[l], ln_b_gain[l], ln_b_bias[l], w_out[l])
        h = h + rms_norm(mix, post_mix_norm[l])
        ff = swiglu(rms_norm(h, pre_ffn_norm[l]), w_gate[l], w_up[l], w_down[l])
        h = h + rms_norm(ff, post_ffn_norm[l])
    return h[:, N_META:, :]


import jax as _jax
import jax.numpy as _jnp

TWIN_FORMAT = 'train_step'
FWD_PARAMS = ['x', 'meta_tokens', 'pre_mix_norm', 'w_in', 'conv_a_w', 'conv_b_w', 'conv_b_bias', 'ln_b_gain', 'ln_b_bias', 'w_out', 'post_mix_norm', 'pre_ffn_norm', 'w_gate', 'w_up', 'w_down', 'post_ffn_norm']
TWIN_WEIGHTS = ['meta_tokens', 'pre_mix_norm', 'w_in', 'conv_a_w', 'conv_b_w', 'conv_b_bias', 'ln_b_gain', 'ln_b_bias', 'w_out', 'post_mix_norm', 'pre_ffn_norm', 'w_gate', 'w_up', 'w_down', 'post_ffn_norm']
TWIN_DIFF_INPUT = 'x'
TWIN_INPUTS = ['x', 'meta_tokens', 'pre_mix_norm', 'w_in', 'conv_a_w', 'conv_b_w', 'conv_b_bias', 'ln_b_gain', 'ln_b_bias', 'w_out', 'post_mix_norm', 'pre_ffn_norm', 'w_gate', 'w_up', 'w_down', 'post_ffn_norm', 'loss_target', 'm_meta_tokens', 'm_pre_mix_norm', 'm_w_in', 'm_conv_a_w', 'm_conv_b_w', 'm_conv_b_bias', 'm_ln_b_gain', 'm_ln_b_bias', 'm_w_out', 'm_post_mix_norm', 'm_pre_ffn_norm', 'm_w_gate', 'm_w_up', 'm_w_down', 'm_post_ffn_norm', 'v_meta_tokens', 'v_pre_mix_norm', 'v_w_in', 'v_conv_a_w', 'v_conv_b_w', 'v_conv_b_bias', 'v_ln_b_gain', 'v_ln_b_bias', 'v_w_out', 'v_post_mix_norm', 'v_pre_ffn_norm', 'v_w_gate', 'v_w_up', 'v_w_down', 'v_post_ffn_norm']
TWIN_OUTPUTS = ['loss', 'grad_x', 'grad_meta_tokens', 'grad_pre_mix_norm', 'grad_w_in', 'grad_conv_a_w', 'grad_conv_b_w', 'grad_conv_b_bias', 'grad_ln_b_gain', 'grad_ln_b_bias', 'grad_w_out', 'grad_post_mix_norm', 'grad_pre_ffn_norm', 'grad_w_gate', 'grad_w_up', 'grad_w_down', 'grad_post_ffn_norm', 'delta_meta_tokens', 'delta_pre_mix_norm', 'delta_w_in', 'delta_conv_a_w', 'delta_conv_b_w', 'delta_conv_b_bias', 'delta_ln_b_gain', 'delta_ln_b_bias', 'delta_w_out', 'delta_post_mix_norm', 'delta_pre_ffn_norm', 'delta_w_gate', 'delta_w_up', 'delta_w_down', 'delta_post_ffn_norm', 'new_m_meta_tokens', 'new_m_pre_mix_norm', 'new_m_w_in', 'new_m_conv_a_w', 'new_m_conv_b_w', 'new_m_conv_b_bias', 'new_m_ln_b_gain', 'new_m_ln_b_bias', 'new_m_w_out', 'new_m_post_mix_norm', 'new_m_pre_ffn_norm', 'new_m_w_gate', 'new_m_w_up', 'new_m_w_down', 'new_m_post_ffn_norm', 'new_v_meta_tokens', 'new_v_pre_mix_norm', 'new_v_w_in', 'new_v_conv_a_w', 'new_v_conv_b_w', 'new_v_conv_b_bias', 'new_v_ln_b_gain', 'new_v_ln_b_bias', 'new_v_w_out', 'new_v_post_mix_norm', 'new_v_pre_ffn_norm', 'new_v_w_gate', 'new_v_w_up', 'new_v_w_down', 'new_v_post_ffn_norm']
TWIN_LEAF_KINDS = {'loss': 'loss', 'grad_x': 'grad_x', 'grad_meta_tokens': 'grad_w', 'grad_pre_mix_norm': 'grad_w', 'grad_w_in': 'grad_w', 'grad_conv_a_w': 'grad_w', 'grad_conv_b_w': 'grad_w', 'grad_conv_b_bias': 'grad_w', 'grad_ln_b_gain': 'grad_w', 'grad_ln_b_bias': 'grad_w', 'grad_w_out': 'grad_w', 'grad_post_mix_norm': 'grad_w', 'grad_pre_ffn_norm': 'grad_w', 'grad_w_gate': 'grad_w', 'grad_w_up': 'grad_w', 'grad_w_down': 'grad_w', 'grad_post_ffn_norm': 'grad_w', 'delta_meta_tokens': 'delta_w', 'delta_pre_mix_norm': 'delta_w', 'delta_w_in': 'delta_w', 'delta_conv_a_w': 'delta_w', 'delta_conv_b_w': 'delta_w', 'delta_conv_b_bias': 'delta_w', 'delta_ln_b_gain': 'delta_w', 'delta_ln_b_bias': 'delta_w', 'delta_w_out': 'delta_w', 'delta_post_mix_norm': 'delta_w', 'delta_pre_ffn_norm': 'delta_w', 'delta_w_gate': 'delta_w', 'delta_w_up': 'delta_w', 'delta_w_down': 'delta_w', 'delta_post_ffn_norm': 'delta_w', 'new_m_meta_tokens': 'new_m', 'new_m_pre_mix_norm': 'new_m', 'new_m_w_in': 'new_m', 'new_m_conv_a_w': 'new_m', 'new_m_conv_b_w': 'new_m', 'new_m_conv_b_bias': 'new_m', 'new_m_ln_b_gain': 'new_m', 'new_m_ln_b_bias': 'new_m', 'new_m_w_out': 'new_m', 'new_m_post_mix_norm': 'new_m', 'new_m_pre_ffn_norm': 'new_m', 'new_m_w_gate': 'new_m', 'new_m_w_up': 'new_m', 'new_m_w_down': 'new_m', 'new_m_post_ffn_norm': 'new_m', 'new_v_meta_tokens': 'new_v', 'new_v_pre_mix_norm': 'new_v', 'new_v_w_in': 'new_v', 'new_v_conv_a_w': 'new_v', 'new_v_conv_b_w': 'new_v', 'new_v_conv_b_bias': 'new_v', 'new_v_ln_b_gain': 'new_v', 'new_v_ln_b_bias': 'new_v', 'new_v_w_out': 'new_v', 'new_v_post_mix_norm': 'new_v', 'new_v_pre_ffn_norm': 'new_v', 'new_v_w_gate': 'new_v', 'new_v_w_up': 'new_v', 'new_v_w_down': 'new_v', 'new_v_post_ffn_norm': 'new_v'}


def _forward(args):
    return _fwd_reference(*[args[k] for k in FWD_PARAMS])


def _output_shape():
    out = _jax.eval_shape(lambda: _forward(_fwd_setup_inputs(0)))
    return out.shape, out.dtype

N_MICROBATCH = 1
ADAM_LR = 0.001
ADAM_B1 = 0.9
ADAM_B2 = 0.999
ADAM_EPS = 1e-08
ADAM_WD = 0.01
ADAM_STEP = 10
PER_EXAMPLE_BATCH_AXIS = {'x': 0, 'loss_target': 0}
SHARED_INPUTS = []
_WEIGHT_DTYPES = {'meta_tokens': _jnp.float32, 'pre_mix_norm': _jnp.float32, 'w_in': _jnp.float32, 'conv_a_w': _jnp.float32, 'conv_b_w': _jnp.float32, 'conv_b_bias': _jnp.float32, 'ln_b_gain': _jnp.float32, 'ln_b_bias': _jnp.float32, 'w_out': _jnp.float32, 'post_mix_norm': _jnp.float32, 'pre_ffn_norm': _jnp.float32, 'w_gate': _jnp.float32, 'w_up': _jnp.float32, 'w_down': _jnp.float32, 'post_ffn_norm': _jnp.float32}
MOMENT_SCALE = {'meta_tokens': 1.199685e-02, 'pre_mix_norm': 4.975920e-01, 'w_in': 3.157328e-01, 'conv_a_w': 4.104151e-01, 'conv_b_w': 2.542010e-01, 'conv_b_bias': 1.485950e+00, 'ln_b_gain': 5.990191e-01, 'ln_b_bias': 8.161793e-01, 'w_out': 4.088060e-01, 'post_mix_norm': 1.601473e+01, 'pre_ffn_norm': 4.463435e-01, 'w_gate': 1.525650e-01, 'w_up': 2.058578e-01, 'w_down': 3.401903e-01, 'post_ffn_norm': 1.603598e+01}


def _to_microbatches(a, axis):
    t = _jnp.moveaxis(a, axis, 0)
    t = t.reshape((N_MICROBATCH, t.shape[0] // N_MICROBATCH) + t.shape[1:])
    return _jnp.moveaxis(t, 1, axis + 1)


def setup_inputs(seed: int = 0) -> dict:
    inp = _fwd_setup_inputs(seed)
    key = _jax.random.fold_in(_jax.random.key(seed), 7919)
    shape, _ = _output_shape()
    out = dict(inp)
    out["loss_target"] = _jax.random.normal(_jax.random.fold_in(key, 0), shape, _jnp.float32)
    for i, name in enumerate(TWIN_WEIGHTS):
        w = inp[name].astype(_jnp.float32)
        if MOMENT_SCALE is None:
            s = _jnp.sqrt(_jnp.mean(_jnp.square(w)) + 1e-30)
        else:
            s = MOMENT_SCALE[name]
        km, kv = _jax.random.split(_jax.random.fold_in(key, i + 1))
        out[name] = w
        out["m_" + name] = s * _jax.random.normal(km, w.shape, _jnp.float32)
        out["v_" + name] = (s * s) * _jax.random.uniform(kv, w.shape, _jnp.float32, 0.5, 1.5)
    if N_MICROBATCH > 1:
        for name, axis in PER_EXAMPLE_BATCH_AXIS.items():
            out[name] = _to_microbatches(out[name], axis)
    return {'x': out['x'], 'meta_tokens': out['meta_tokens'], 'pre_mix_norm': out['pre_mix_norm'], 'w_in': out['w_in'], 'conv_a_w': out['conv_a_w'], 'conv_b_w': out['conv_b_w'], 'conv_b_bias': out['conv_b_bias'], 'ln_b_gain': out['ln_b_gain'], 'ln_b_bias': out['ln_b_bias'], 'w_out': out['w_out'], 'post_mix_norm': out['post_mix_norm'], 'pre_ffn_norm': out['pre_ffn_norm'], 'w_gate': out['w_gate'], 'w_up': out['w_up'], 'w_down': out['w_down'], 'post_ffn_norm': out['post_ffn_norm'], 'loss_target': out['loss_target'], 'm_meta_tokens': out['m_meta_tokens'], 'm_pre_mix_norm': out['m_pre_mix_norm'], 'm_w_in': out['m_w_in'], 'm_conv_a_w': out['m_conv_a_w'], 'm_conv_b_w': out['m_conv_b_w'], 'm_conv_b_bias': out['m_conv_b_bias'], 'm_ln_b_gain': out['m_ln_b_gain'], 'm_ln_b_bias': out['m_ln_b_bias'], 'm_w_out': out['m_w_out'], 'm_post_mix_norm': out['m_post_mix_norm'], 'm_pre_ffn_norm': out['m_pre_ffn_norm'], 'm_w_gate': out['m_w_gate'], 'm_w_up': out['m_w_up'], 'm_w_down': out['m_w_down'], 'm_post_ffn_norm': out['m_post_ffn_norm'], 'v_meta_tokens': out['v_meta_tokens'], 'v_pre_mix_norm': out['v_pre_mix_norm'], 'v_w_in': out['v_w_in'], 'v_conv_a_w': out['v_conv_a_w'], 'v_conv_b_w': out['v_conv_b_w'], 'v_conv_b_bias': out['v_conv_b_bias'], 'v_ln_b_gain': out['v_ln_b_gain'], 'v_ln_b_bias': out['v_ln_b_bias'], 'v_w_out': out['v_w_out'], 'v_post_mix_norm': out['v_post_mix_norm'], 'v_pre_ffn_norm': out['v_pre_ffn_norm'], 'v_w_gate': out['v_w_gate'], 'v_w_up': out['v_w_up'], 'v_w_down': out['v_w_down'], 'v_post_ffn_norm': out['v_post_ffn_norm']}


def _loss(weights, diff, rest, loss_target):
    with _jax.named_scope("forward"):
        args = {**rest, TWIN_DIFF_INPUT: diff, **{k: w.astype(_WEIGHT_DTYPES[k]) for k, w in weights.items()}}
        y = _forward(args)
    with _jax.named_scope("loss_head"):
        err = _jnp.square(y.astype(_jnp.float32) - loss_target)
        return 0.5 * _jnp.sum(_jnp.mean(err, axis=-1)) if err.ndim else 0.5 * err


def _adamw(w, g, m, v):
    m = ADAM_B1 * m + (1.0 - ADAM_B1) * g
    v = ADAM_B2 * v + (1.0 - ADAM_B2) * _jnp.square(g)
    m_hat = m / (1.0 - ADAM_B1 ** ADAM_STEP)
    v_hat = v / (1.0 - ADAM_B2 ** ADAM_STEP)
    delta = -ADAM_LR * (m_hat / (_jnp.sqrt(v_hat) + ADAM_EPS) + ADAM_WD * w)
    return delta, m, v


def reference(x, meta_tokens, pre_mix_norm, w_in, conv_a_w, conv_b_w, conv_b_bias, ln_b_gain, ln_b_bias, w_out, post_mix_norm, pre_ffn_norm, w_gate, w_up, w_down, post_ffn_norm, loss_target, m_meta_tokens, m_pre_mix_norm, m_w_in, m_conv_a_w, m_conv_b_w, m_conv_b_bias, m_ln_b_gain, m_ln_b_bias, m_w_out, m_post_mix_norm, m_pre_ffn_norm, m_w_gate, m_w_up, m_w_down, m_post_ffn_norm, v_meta_tokens, v_pre_mix_norm, v_w_in, v_conv_a_w, v_conv_b_w, v_conv_b_bias, v_ln_b_gain, v_ln_b_bias, v_w_out, v_post_mix_norm, v_pre_ffn_norm, v_w_gate, v_w_up, v_w_down, v_post_ffn_norm):
    given = dict(x=x, meta_tokens=meta_tokens, pre_mix_norm=pre_mix_norm, w_in=w_in, conv_a_w=conv_a_w, conv_b_w=conv_b_w, conv_b_bias=conv_b_bias, ln_b_gain=ln_b_gain, ln_b_bias=ln_b_bias, w_out=w_out, post_mix_norm=post_mix_norm, pre_ffn_norm=pre_ffn_norm, w_gate=w_gate, w_up=w_up, w_down=w_down, post_ffn_norm=post_ffn_norm, loss_target=loss_target, m_meta_tokens=m_meta_tokens, m_pre_mix_norm=m_pre_mix_norm, m_w_in=m_w_in, m_conv_a_w=m_conv_a_w, m_conv_b_w=m_conv_b_w, m_conv_b_bias=m_conv_b_bias, m_ln_b_gain=m_ln_b_gain, m_ln_b_bias=m_ln_b_bias, m_w_out=m_w_out, m_post_mix_norm=m_post_mix_norm, m_pre_ffn_norm=m_pre_ffn_norm, m_w_gate=m_w_gate, m_w_up=m_w_up, m_w_down=m_w_down, m_post_ffn_norm=m_post_ffn_norm, v_meta_tokens=v_meta_tokens, v_pre_mix_norm=v_pre_mix_norm, v_w_in=v_w_in, v_conv_a_w=v_conv_a_w, v_conv_b_w=v_conv_b_w, v_conv_b_bias=v_conv_b_bias, v_ln_b_gain=v_ln_b_gain, v_ln_b_bias=v_ln_b_bias, v_w_out=v_w_out, v_post_mix_norm=v_post_mix_norm, v_pre_ffn_norm=v_pre_ffn_norm, v_w_gate=v_w_gate, v_w_up=v_w_up, v_w_down=v_w_down, v_post_ffn_norm=v_post_ffn_norm)
    weights = {n: given[n] for n in TWIN_WEIGHTS}
    shared = {n: given[n] for n in SHARED_INPUTS}
    per_example = {n: given[n] for n in ['x']}
    grad_fn = _jax.value_and_grad(_loss, argnums=(0, 1))

    def one_microbatch(ex, loss_target):
        ex = dict(ex)
        diff = ex.pop(TWIN_DIFF_INPUT)
        return grad_fn(weights, diff, {**shared, **ex}, loss_target)

    if N_MICROBATCH == 1:
        loss, (grad_w, grad_x) = one_microbatch(per_example, given["loss_target"])
    else:
        def body(carry, xs):
            loss_sum, grad_sum = carry
            l_k, (gw_k, gx_k) = one_microbatch(xs[0], xs[1])
            with _jax.named_scope("update"):
                return (loss_sum + l_k, _jax.tree.map(_jnp.add, grad_sum, gw_k)), gx_k

        init = (_jnp.zeros((), _jnp.float32), _jax.tree.map(_jnp.zeros_like, weights))
        (loss, grad_w), grad_x = _jax.lax.scan(body, init, (per_example, given["loss_target"]))
    with _jax.named_scope("update"):
        delta_w, new_m, new_v = {}, {}, {}
        for n in TWIN_WEIGHTS:
            delta_w[n], new_m[n], new_v[n] = _adamw(weights[n], grad_w[n], given["m_" + n], given["v_" + n])
    return (loss, grad_x, *[grad_w[n] for n in TWIN_WEIGHTS], *[delta_w[n] for n in TWIN_WEIGHTS],
            *[new_m[n] for n in TWIN_WEIGHTS], *[new_v[n] for n in TWIN_WEIGHTS])
```

```python
import functools

import jax
import jax.numpy as jnp
from jax import lax
from jax.experimental import pallas as pl
from jax.experimental.pallas import tpu as pltpu

F32 = jnp.float32
BF16 = jnp.bfloat16
MESH = pl.DeviceIdType.MESH

N_META = 16
N_DEV = 8
RMS_EPS = 1e-6
LN_EPS = 1e-5
ADAM_LR = 0.001
ADAM_B1 = 0.9
ADAM_B2 = 0.999
ADAM_EPS = 1e-08
ADAM_WD = 0.01
ADAM_STEP = 10

LANE = 128
SUBLANE = 8
ROW_ALIGN = 128
N_ROW_BLOCKS = 4
CONV_HALO = 32
CONV_CHUNK = 64
N_CHUNK = 512
WGRAD_TILE = 256
V7X_VMEM_BYTES = 64 * 1024 * 1024
VMEM_LIMIT = V7X_VMEM_BYTES - 6 * 1024 * 1024
SMALL_ROWS = 64
SM_ROWS = 56


def _cparams(n_grid_axes=0):
    sem = ("arbitrary",) * n_grid_axes if n_grid_axes else None
    return pltpu.CompilerParams(dimension_semantics=sem, vmem_limit_bytes=VMEM_LIMIT)


def _mesh_pos():
    return lax.axis_index("x"), lax.axis_index("y"), lax.axis_index("c")


def _dev_index(px, py, pc):
    return 4 * px + 2 * py + pc


def _full(shape):
    return pl.BlockSpec(shape, lambda *_: (0,) * len(shape))


def _resident(shape):
    return pl.BlockSpec(shape, lambda *_: (0,) * len(shape), pipeline_mode=pl.Buffered(1))


def _dot_nt(a, w):
    return lax.dot_general(a, w, (((1,), (1,)), ((), ())), preferred_element_type=F32)


def _dot_nn(a, w):
    return jnp.dot(a, w, preferred_element_type=F32)


def _chunks(n, c):
    out, o = [], 0
    while o < n:
        out.append((o, min(c, n - o)))
        o += c
    return out


def _rstd(h):
    return lax.rsqrt(jnp.mean(h * h, axis=-1, keepdims=True) + RMS_EPS)


def _rms_bwd(dyh, yh, r):
    return r * (dyh - yh * jnp.mean(dyh * yh, axis=-1, keepdims=True))


def _silu_grad(a, sig):
    return sig * (1.0 + a * (1.0 - sig))


def _acc_rows(ref, val, first):
    s = jnp.sum(val, axis=0, keepdims=True)

    @pl.when(first)
    def _():
        ref[...] = s

    @pl.when(jnp.logical_not(first))
    def _():
        ref[...] += s


def _pack_weights(w_in, w_out, w_gate, w_up, w_down):
    d = w_out.shape[1]
    rows = [w_in.shape[1], w_out.shape[0], w_gate.shape[1], w_up.shape[1], w_down.shape[0]]

    def body(in_ref, out_ref, gate_ref, up_ref, down_ref, o_ref):
        o = 0
        for ref, transposed, r in ((in_ref, True, rows[0]), (out_ref, False, rows[1]), (gate_ref, True, rows[2]),
                                   (up_ref, True, rows[3]), (down_ref, False, rows[4])):
            v = ref[...]
            o_ref[pl.ds(o, r), :] = (v.T if transposed else v).astype(BF16)
            o += r

    return pl.pallas_call(body, out_shape=jax.ShapeDtypeStruct((sum(rows), d), BF16), name="pack_weights",
                          compiler_params=_cparams())(w_in, w_out, w_gate, w_up, w_down)


def _all_gather(wl, sm, rows):
    d = wl.shape[1]
    r_loc = sum(rows)
    lo = [sum(rows[:p]) for p in range(5)]

    def body(wl_ref, sm_ref, o0, o1, o2, o3, o4, sa_ref, send_sems, recv_sems, ssend, srecv, local_sems):
        outs = (o0, o1, o2, o3, o4)
        x, y, c = _mesh_pos()
        me, sibling = (x, y, c), (x, y, 1 - c)
        chips = [(1 - x, y), (x, 1 - y), (1 - x, 1 - y)]

        def block_copies(k, blk, to, from_local):
            j = _dev_index(*blk)
            cps = []
            for p in range(5):
                dst = outs[p].at[pl.ds(pl.multiple_of(j * rows[p], 16), rows[p]), :]
                src = wl_ref.at[pl.ds(lo[p], rows[p]), :] if from_local else dst
                cps.append(pltpu.make_async_remote_copy(src_ref=src, dst_ref=dst, send_sem=send_sems.at[k],
                                                        recv_sem=recv_sems.at[k], device_id=to, device_id_type=MESH))
            return cps

        def block_wait(k):
            whole = wl_ref.at[pl.ds(0, r_loc), :]
            return pltpu.make_async_remote_copy(src_ref=whole, dst_ref=whole, send_sem=send_sems.at[k],
                                                recv_sem=recv_sems.at[k], device_id=me, device_id_type=MESH)

        def small_copy(k, to):
            j = _dev_index(*me)
            return pltpu.make_async_remote_copy(src_ref=sm_ref, dst_ref=sa_ref.at[j], send_sem=ssend.at[k],
                                                recv_sem=srecv.at[k], device_id=to, device_id_type=MESH)

        jme = _dev_index(*me)
        mine = [pltpu.make_async_copy(wl_ref.at[pl.ds(lo[p], rows[p]), :],
                                      outs[p].at[pl.ds(pl.multiple_of(jme * rows[p], 16), rows[p]), :], local_sems.at[0])
                for p in range(5)]
        mine.append(pltpu.make_async_copy(sm_ref, sa_ref.at[jme], local_sems.at[1]))
        for cp in mine:
            cp.start()
        peers = [sibling] + [(*chip, c) for chip in chips] + [(*chip, 1 - c) for chip in chips]
        smalls = [small_copy(k, to) for k, to in enumerate(peers)]
        for cp in smalls:
            cp.start()
        for cp in block_copies(0, me, sibling, True):
            cp.start()
        for j, chip in enumerate(chips):
            for cp in block_copies(1 + j, me, (*chip, c), True):
                cp.start()
        for j, chip in enumerate(chips):
            block_wait(1 + j).wait_recv()
            for cp in block_copies(4 + j, (*chip, c), sibling, False):
                cp.start()
        block_wait(0).wait_recv()
        for j in range(3):
            block_wait(4 + j).wait_recv()
        for k in range(7):
            block_wait(k).wait_send()
        for cp in smalls:
            cp.wait_recv()
        for cp in smalls:
            cp.wait_send()
        pltpu.make_async_copy(wl_ref, wl_ref, local_sems.at[0]).wait()
        mine[-1].wait()

    any_spec = pl.BlockSpec(memory_space=pl.ANY)
    out_shape = [jax.ShapeDtypeStruct((N_DEV * r, d), BF16) for r in rows]
    out_shape.append(jax.ShapeDtypeStruct((N_DEV,) + sm.shape, F32))
    return pl.pallas_call(
        body, out_shape=out_shape, in_specs=[any_spec, any_spec], out_specs=[any_spec] * 6,
        scratch_shapes=[pltpu.SemaphoreType.DMA((7,)), pltpu.SemaphoreType.DMA((7,)),
                        pltpu.SemaphoreType.DMA((7,)), pltpu.SemaphoreType.DMA((7,)),
                        pltpu.SemaphoreType.DMA((2,))],
        name="all_gather_weights", compiler_params=_cparams())(wl, sm)


def _prep(x2, tgt2, meta_full, t_rows, x0):
    s, d = x2.shape

    def body(x_ref, tgt_ref, meta_ref, h0_ref, tp_ref, head, zhead, sems):
        head[...] = jnp.zeros_like(head)
        head[pl.ds(x0 - N_META, N_META), :] = meta_ref[...]
        zhead[...] = jnp.zeros_like(zhead)
        cps = [pltpu.make_async_copy(x_ref, h0_ref.at[pl.ds(x0, s), :], sems.at[0]),
               pltpu.make_async_copy(tgt_ref, tp_ref.at[pl.ds(x0, s), :], sems.at[1]),
               pltpu.make_async_copy(head, h0_ref.at[pl.ds(0, x0), :], sems.at[2]),
               pltpu.make_async_copy(zhead, tp_ref.at[pl.ds(0, x0), :], sems.at[3])]
        for cp in cps:
            cp.start()
        for cp in cps:
            cp.wait()

    any_spec = pl.BlockSpec(memory_space=pl.ANY)
    return pl.pallas_call(
        body, out_shape=[jax.ShapeDtypeStruct((t_rows, d), F32)] * 2,
        in_specs=[any_spec, any_spec, pl.BlockSpec(memory_space=pltpu.VMEM)], out_specs=[any_spec, any_spec],
        scratch_shapes=[pltpu.VMEM((x0, d), F32), pltpu.VMEM((x0, d), F32), pltpu.SemaphoreType.DMA((4,))],
        name="prep", compiler_params=_cparams())(x2, tgt2, meta_full)


def _in_proj(h0, g1, win_t, tm):
    t_rows, d = h0.shape
    e = win_t.shape[0]

    def body(h_ref, g_ref, w_ref, xn_ref, hin_ref):
        h = h_ref[...]
        xn = ((h * _rstd(h)) * g_ref[...]).astype(BF16)
        xn_ref[...] = xn
        for o, n in _chunks(e, N_CHUNK):
            hin_ref[:, pl.ds(o, n)] = _dot_nt(xn, w_ref[pl.ds(o, n), :])

    return pl.pallas_call(
        body, grid=(t_rows // tm,),
        in_specs=[pl.BlockSpec((tm, d), lambda i: (i, 0)), _full((1, d)), _resident((e, d))],
        out_specs=[pl.BlockSpec((tm, d), lambda i: (i, 0)), pl.BlockSpec((tm, e), lambda i: (i, 0))],
        out_shape=[jax.ShapeDtypeStruct((t_rows, d), BF16), jax.ShapeDtypeStruct((t_rows, e), F32)],
        name="in_proj", compiler_params=_cparams(1))(h0, g1, win_t)


def _tap_slot(off):
    return off % SUBLANE, (off // SUBLANE) * SUBLANE


def _fill_shifted(sh_ref, base_ref, residues, n_rows):
    for r in residues:
        sh_ref[r] = base_ref[pl.ds(r, n_rows), :]


def _row_loop(t_rows, chunk, fn, carry=None):
    def step(i, c):
        return fn(pl.multiple_of(i * chunk, chunk), c)

    return lax.fori_loop(0, t_rows // chunk, step, carry)


def _mix_conv_fwd(hin, wa, wb, bb, wa_w):
    t_rows = hin.shape[0]
    nt = wa_w // LANE
    ka, kb = wa.shape[0], wb.shape[0]
    nr = CONV_HALO + t_rows

    def body(bg_ref, cg_ref, ha_ref, val_ref, gt_ref, wa_ref, wb_ref, bb_ref, ya_ref, z_ref, base, sh):
        base[pl.ds(0, CONV_HALO), :] = jnp.zeros((CONV_HALO, LANE), F32)
        base[pl.ds(nr, SUBLANE), :] = jnp.zeros((SUBLANE, LANE), F32)

        def conv(w_ref, k_taps, b, n):
            acc = None
            for k in range(k_taps):
                r, q = _tap_slot(CONV_HALO - (k_taps - 1) + k)
                term = w_ref[pl.ds(k, 1), :] * sh[r, pl.ds(b + q, n), :]
                acc = term if acc is None else acc + term
            return acc

        def fill_a(b, c):
            base[pl.ds(CONV_HALO + b, CONV_CHUNK), :] = cg_ref[pl.ds(b, CONV_CHUNK), :] * ha_ref[pl.ds(b, CONV_CHUNK), :]
            return c

        _row_loop(t_rows, CONV_CHUNK, fill_a)
        _fill_shifted(sh, base, sorted({_tap_slot(CONV_HALO - (ka - 1) + k)[0] for k in range(ka)}), nr)

        def out_a(b, c):
            ya_ref[pl.ds(b, CONV_CHUNK), :] = (bg_ref[pl.ds(b, CONV_CHUNK), :] * conv(wa_ref, ka, b, CONV_CHUNK)).astype(BF16)
            return c

        _row_loop(t_rows, CONV_CHUNK, out_a)

        def fill_b(b, c):
            base[pl.ds(CONV_HALO + b, CONV_CHUNK), :] = (val_ref[pl.ds(b, CONV_CHUNK), :]
                                                          * jax.nn.sigmoid(gt_ref[pl.ds(b, CONV_CHUNK), :]))
            return c

        _row_loop(t_rows, CONV_CHUNK, fill_b)
        _fill_shifted(sh, base, range(SUBLANE), nr)

        def out_b(b, c):
            z_ref[pl.ds(b, CONV_CHUNK), :] = conv(wb_ref, kb, b, CONV_CHUNK) + bb_ref[...]
            return c

        _row_loop(t_rows, CONV_CHUNK, out_b)

    def col(g):
        return pl.BlockSpec((t_rows, LANE), lambda i, g=g: (0, g * nt + i))

    tile = lambda rows: pl.BlockSpec((rows, LANE), lambda i: (0, i))
    return pl.pallas_call(
        body, grid=(nt,),
        in_specs=[col(0), col(1), col(2), col(3), col(4), tile(ka), tile(kb), tile(1)],
        out_specs=[tile(t_rows), tile(t_rows)],
        out_shape=[jax.ShapeDtypeStruct((t_rows, wa_w), BF16), jax.ShapeDtypeStruct((t_rows, wa_w), F32)],
        scratch_shapes=[pltpu.VMEM((nr + SUBLANE, LANE), F32), pltpu.VMEM((SUBLANE, nr, LANE), F32)],
        name="mix_conv_fwd", compiler_params=_cparams(1))(hin, hin, hin, hin, hin, wa, wb, bb)


def _ln_parts(z, lg, lb):
    mu = jnp.mean(z, axis=-1, keepdims=True)
    zc = z - mu
    rstd = lax.rsqrt(jnp.mean(zc * zc, axis=-1, keepdims=True) + LN_EPS)
    zh = zc * rstd
    return zh, rstd, zh * lg + lb


def _mix_ln_fwd(ya, z, lg, lb, tm):
    t_rows, w = z.shape

    def body(ya_ref, z_ref, lg_ref, lb_ref, y_ref):
        _, _, ln = _ln_parts(z_ref[...], lg_ref[...], lb_ref[...])
        y_ref[:, pl.ds(0, w)] = ya_ref[...]
        y_ref[:, pl.ds(w, w)] = (ln * jax.nn.sigmoid(ln)).astype(BF16)

    blk = pl.BlockSpec((tm, w), lambda i: (i, 0))
    return pl.pallas_call(
        body, grid=(t_rows // tm,), in_specs=[blk, blk, _full((1, w)), _full((1, w))],
        out_specs=pl.BlockSpec((tm, 2 * w), lambda i: (i, 0)),
        out_shape=jax.ShapeDtypeStruct((t_rows, 2 * w), BF16), name="mix_ln_fwd", compiler_params=_cparams(1))(ya, z, lg, lb)


def _out_proj(y, w_out, h0, g2, g3, tm):
    t_rows, d = h0.shape

    def body(y_ref, w_ref, h0_ref, g2_ref, g3_ref, mix_ref, h1_ref, xn2_ref):
        mix = _dot_nn(y_ref[...], w_ref[...])
        mix_ref[...] = mix
        h1 = h0_ref[...] + (mix * _rstd(mix)) * g2_ref[...]
        h1_ref[...] = h1
        xn2_ref[...] = ((h1 * _rstd(h1)) * g3_ref[...]).astype(BF16)

    blk = pl.BlockSpec((tm, d), lambda i: (i, 0))
    return pl.pallas_call(
        body, grid=(t_rows // tm,), in_specs=[blk, _resident(w_out.shape), blk, _full((1, d)), _full((1, d))],
        out_specs=[blk, blk, blk],
        out_shape=[jax.ShapeDtypeStruct((t_rows, d), F32), jax.ShapeDtypeStruct((t_rows, d), F32),
                   jax.ShapeDtypeStruct((t_rows, d), BF16)],
        name="out_proj", compiler_params=_cparams(1))(y, w_out, h0, g2, g3)


def _gate_up(xn2, wg_t, wu_t, tm):
    t_rows, d = xn2.shape
    f = wg_t.shape[0]

    def body(x_ref, wg_ref, wu_ref, a_ref, u_ref, s_ref):
        xn = x_ref[...]
        for o, n in _chunks(f, N_CHUNK):
            a = _dot_nt(xn, wg_ref[pl.ds(o, n), :])
            u = _dot_nt(xn, wu_ref[pl.ds(o, n), :])
            a_ref[:, pl.ds(o, n)] = a.astype(BF16)
            u_ref[:, pl.ds(o, n)] = u.astype(BF16)
            s_ref[:, pl.ds(o, n)] = ((a * jax.nn.sigmoid(a)) * u).astype(BF16)

    blk = pl.BlockSpec((tm, f), lambda i: (i, 0))
    return pl.pallas_call(
        body, grid=(t_rows // tm,),
        in_specs=[pl.BlockSpec((tm, d), lambda i: (i, 0)), _resident((f, d)), _resident((f, d))],
        out_specs=[blk, blk, blk], out_shape=[jax.ShapeDtypeStruct((t_rows, f), BF16)] * 3,
        name="gate_up", compiler_params=_cparams(1))(xn2, wg_t, wu_t)


def _down_loss(s, wd, h1, tgt, g4, tm, x0):
    t_rows, d = h1.shape
    f = wd.shape[0]

    def body(s_ref, w_ref, h1_ref, tgt_ref, g4_ref, dh2_ref, dff_ref, dg4_ref, loss_ref):
        i = pl.program_id(0)
        ff = _dot_nn(s_ref[...], w_ref[...])
        r4 = _rstd(ff)
        fh = ff * r4
        g4 = g4_ref[...]
        h2 = h1_ref[...] + fh * g4
        row = i * tm + lax.broadcasted_iota(jnp.int32, (tm, 1), 0)
        diff = jnp.where(row >= x0, h2 - tgt_ref[...], 0.0)
        dh2 = diff / d
        dh2_ref[...] = dh2
        dff_ref[...] = _rms_bwd(dh2 * g4, fh, r4).astype(BF16)
        _acc_rows(dg4_ref, dh2 * fh, i == 0)
        _acc_rows(loss_ref, diff * diff, i == 0)

    blk = pl.BlockSpec((tm, d), lambda i: (i, 0))
    return pl.pallas_call(
        body, grid=(t_rows // tm,),
        in_specs=[pl.BlockSpec((tm, f), lambda i: (i, 0)), _resident((f, d)), blk, blk, _full((1, d))],
        out_specs=[blk, blk, _full((1, d)), _full((1, d))],
        out_shape=[jax.ShapeDtypeStruct((t_rows, d), F32), jax.ShapeDtypeStruct((t_rows, d), BF16),
                   jax.ShapeDtypeStruct((1, d), F32), jax.ShapeDtypeStruct((1, d), F32)],
        name="down_loss", compiler_params=_cparams(1))(s, wd, h1, tgt, g4)


def _bwd_down(dff, wd, a, u, tm):
    t_rows, d = dff.shape
    f = wd.shape[0]

    def body(dff_ref, w_ref, a_ref, u_ref, da_ref, du_ref):
        dff_v = dff_ref[...]
        for o, n in _chunks(f, N_CHUNK):
            ds = _dot_nt(dff_v, w_ref[pl.ds(o, n), :])
            av = a_ref[:, pl.ds(o, n)].astype(F32)
            uv = u_ref[:, pl.ds(o, n)].astype(F32)
            sig = jax.nn.sigmoid(av)
            da_ref[:, pl.ds(o, n)] = (ds * uv * _silu_grad(av, sig)).astype(BF16)
            du_ref[:, pl.ds(o, n)] = (ds * (av * sig)).astype(BF16)

    blk = pl.BlockSpec((tm, f), lambda i: (i, 0))
    return pl.pallas_call(
        body, grid=(t_rows // tm,),
        in_specs=[pl.BlockSpec((tm, d), lambda i: (i, 0)), _resident((f, d)), blk, blk],
        out_specs=[blk, blk], out_shape=[jax.ShapeDtypeStruct((t_rows, f), BF16)] * 2,
        name="bwd_down", compiler_params=_cparams(1))(dff, wd, a, u)


def _wgrad(a, b, name):
    d = b.shape[1]
    t_rows = b.shape[0]
    stacked = a.ndim == 3
    n = a.shape[-1]
    groups = a.shape[0] if stacked else 1
    tiles = n // WGRAD_TILE

    def body(a_ref, b_ref, o_ref):
        o_ref[...] = lax.dot_general(a_ref[...], b_ref[...], (((0,), (0,)), ((), ())),
                                     preferred_element_type=F32).astype(BF16)

    if stacked:
        a_spec = pl.BlockSpec((None, t_rows, WGRAD_TILE), lambda g, i: (g, 0, i))
    else:
        a_spec = pl.BlockSpec((t_rows, WGRAD_TILE), lambda g, i: (0, i))
    return pl.pallas_call(
        body, grid=(groups, tiles), in_specs=[a_spec, pl.BlockSpec((t_rows, d), lambda g, i: (0, 0))],
        out_specs=pl.BlockSpec((WGRAD_TILE, d), lambda g, i: (g * tiles + i, 0)),
        out_shape=jax.ShapeDtypeStruct((groups * n, d), BF16), name=name, compiler_params=_cparams(2))(a, b)


def _bwd_ffn_in(da, du, wg_t, wu_t, h1, dh2, g3, tm):
    t_rows, d = h1.shape
    f = wg_t.shape[0]

    def body(da_ref, du_ref, wg_ref, wu_ref, h1_ref, dh2_ref, g3_ref, dh1_ref, dg3_ref):
        dxn2 = _dot_nn(da_ref[...], wg_ref[...]) + _dot_nn(du_ref[...], wu_ref[...])
        h1 = h1_ref[...]
        r3 = _rstd(h1)
        h1h = h1 * r3
        _acc_rows(dg3_ref, dxn2 * h1h, pl.program_id(0) == 0)
        dh1_ref[...] = dh2_ref[...] + _rms_bwd(dxn2 * g3_ref[...], h1h, r3)

    blk = pl.BlockSpec((tm, d), lambda i: (i, 0))
    blkf = pl.BlockSpec((tm, f), lambda i: (i, 0))
    return pl.pallas_call(
        body, grid=(t_rows // tm,),
        in_specs=[blkf, blkf, _resident((f, d)), _resident((f, d)), blk, blk, _full((1, d))],
        out_specs=[blk, _full((1, d))],
        out_shape=[jax.ShapeDtypeStruct((t_rows, d), F32), jax.ShapeDtypeStruct((1, d), F32)],
        name="bwd_ffn_in", compiler_params=_cparams(1))(da, du, wg_t, wu_t, h1, dh2, g3)


def _bwd_out_proj(dh1, mix, w_out, g2, tm):
    t_rows, d = dh1.shape

    def body(dh1_ref, mix_ref, w_ref, g2_ref, dmix_ref, dy_ref, dg2_ref):
        mix = mix_ref[...]
        r2 = _rstd(mix)
        mh = mix * r2
        dh1 = dh1_ref[...]
        _acc_rows(dg2_ref, dh1 * mh, pl.program_id(0) == 0)
        dmix = _rms_bwd(dh1 * g2_ref[...], mh, r2).astype(BF16)
        dmix_ref[...] = dmix
        dy_ref[...] = _dot_nt(dmix, w_ref[...])

    blk = pl.BlockSpec((tm, d), lambda i: (i, 0))
    return pl.pallas_call(
        body, grid=(t_rows // tm,), in_specs=[blk, blk, _resident(w_out.shape), _full((1, d))],
        out_specs=[blk, blk, _full((1, d))],
        out_shape=[jax.ShapeDtypeStruct((t_rows, d), BF16), jax.ShapeDtypeStruct((t_rows, d), F32),
                   jax.ShapeDtypeStruct((1, d), F32)],
        name="bwd_out_proj", compiler_params=_cparams(1))(dh1, mix, w_out, g2)


def _mix_ln_bwd(z, dy, lg, lb, tm):
    t_rows, w = z.shape

    def body(z_ref, dyb_ref, lg_ref, lb_ref, dz_ref, dlg_ref, dlb_ref, dbb_ref):
        first = pl.program_id(0) == 0
        lg = lg_ref[...]
        zh, rstd, ln = _ln_parts(z_ref[...], lg, lb_ref[...])
        dln = dyb_ref[...] * _silu_grad(ln, jax.nn.sigmoid(ln))
        _acc_rows(dlg_ref, dln * zh, first)
        _acc_rows(dlb_ref, dln, first)
        dzh = dln * lg
        dz = rstd * (dzh - jnp.mean(dzh, axis=-1, keepdims=True) - zh * jnp.mean(dzh * zh, axis=-1, keepdims=True))
        dz_ref[...] = dz
        _acc_rows(dbb_ref, dz, first)

    blk = pl.BlockSpec((tm, w), lambda i: (i, 0))
    vec = _full((1, w))
    return pl.pallas_call(
        body, grid=(t_rows // tm,), in_specs=[blk, pl.BlockSpec((tm, w), lambda i: (i, 1)), vec, vec],
        out_specs=[blk, vec, vec, vec],
        out_shape=[jax.ShapeDtypeStruct((t_rows, w), F32)] + [jax.ShapeDtypeStruct((1, w), F32)] * 3,
        name="mix_ln_bwd", compiler_params=_cparams(1))(z, dy, lg, lb)


def _mix_conv_bwd(hin, dy, dz, wa, wb, wa_w):
    t_rows = hin.shape[0]
    nt = wa_w // LANE
    ka, kb = wa.shape[0], wb.shape[0]
    nr = CONV_HALO + t_rows
    kb_rows = -(-kb // SUBLANE) * SUBLANE

    def body(bg_ref, cg_ref, ha_ref, val_ref, gt_ref, dya_ref, dz_ref, wa_ref, wb_ref,
             dh_ref, dwa_ref, dwb_ref, base, sh, based, shd, tmp):
        zeros = lambda n: jnp.zeros((n, LANE), F32)
        base[pl.ds(0, CONV_HALO), :] = zeros(CONV_HALO)
        base[pl.ds(nr, SUBLANE), :] = zeros(SUBLANE)
        based[pl.ds(t_rows, CONV_HALO + SUBLANE), :] = zeros(CONV_HALO + SUBLANE)

        def fwd_slot(k_taps, k):
            return _tap_slot(CONV_HALO - (k_taps - 1) + k)

        def bwd_slot(k_taps, k):
            return _tap_slot(k_taps - 1 - k)

        def conv(w_ref, k_taps, src, slot, b, n):
            acc = None
            for k in range(k_taps):
                r, q = slot(k_taps, k)
                term = w_ref[pl.ds(k, 1), :] * src[r, pl.ds(b + q, n), :]
                acc = term if acc is None else acc + term
            return acc

        def wgrad_loop(k_taps, d_of, dsrc_to_tmp):
            def step(b, accs):
                dv = d_of(b)
                new = []
                for k in range(k_taps):
                    r, q = fwd_slot(k_taps, k)
                    new.append(accs[k] + dv * sh[r, pl.ds(b + q, SUBLANE), :])
                tmp[pl.ds(b, SUBLANE), :] = dsrc_to_tmp(b)
                return tuple(new)

            return _row_loop(t_rows, SUBLANE, step, tuple(zeros(SUBLANE) for _ in range(k_taps)))

        def store_taps(ref, accs, rows):
            for k, acc in enumerate(accs):
                ref[pl.ds(k, 1), :] = jnp.sum(acc, axis=0, keepdims=True)
            if rows > len(accs):
                ref[pl.ds(len(accs), rows - len(accs)), :] = zeros(rows - len(accs))

        def fill_a(b, c):
            sl = pl.ds(b, CONV_CHUNK)
            base[pl.ds(CONV_HALO + b, CONV_CHUNK), :] = cg_ref[sl, :] * ha_ref[sl, :]
            based[sl, :] = dya_ref[sl, :] * bg_ref[sl, :]
            return c

        _row_loop(t_rows, CONV_CHUNK, fill_a)
        _fill_shifted(sh, base, sorted({fwd_slot(ka, k)[0] for k in range(ka)}), nr)
        _fill_shifted(shd, based, sorted({bwd_slot(ka, k)[0] for k in range(ka)}), nr)

        def d_bgate(b, c):
            sl = pl.ds(b, CONV_CHUNK)
            dh_ref[0, sl, :] = (dya_ref[sl, :] * conv(wa_ref, ka, sh, fwd_slot, b, CONV_CHUNK)).astype(BF16)
            return c

        _row_loop(t_rows, CONV_CHUNK, d_bgate)
        accs = wgrad_loop(ka, lambda b: based[pl.ds(b, SUBLANE), :],
                          lambda b: conv(wa_ref, ka, shd, bwd_slot, b, SUBLANE))
        store_taps(dwa_ref, accs, SUBLANE)

        def d_ch(b, c):
            sl = pl.ds(b, CONV_CHUNK)
            dua = tmp[sl, :]
            dh_ref[1, sl, :] = (dua * ha_ref[sl, :]).astype(BF16)
            dh_ref[2, sl, :] = (dua * cg_ref[sl, :]).astype(BF16)
            return c

        _row_loop(t_rows, CONV_CHUNK, d_ch)

        def fill_b(b, c):
            sl = pl.ds(b, CONV_CHUNK)
            base[pl.ds(CONV_HALO + b, CONV_CHUNK), :] = val_ref[sl, :] * jax.nn.sigmoid(gt_ref[sl, :])
            based[sl, :] = dz_ref[sl, :]
            return c

        _row_loop(t_rows, CONV_CHUNK, fill_b)
        _fill_shifted(sh, base, range(SUBLANE), nr)
        _fill_shifted(shd, based, range(SUBLANE), nr)
        accs = wgrad_loop(kb, lambda b: based[pl.ds(b, SUBLANE), :],
                          lambda b: conv(wb_ref, kb, shd, bwd_slot, b, SUBLANE))
        store_taps(dwb_ref, accs, kb_rows)

        def d_glu(b, c):
            sl = pl.ds(b, CONV_CHUNK)
            dgg = tmp[sl, :]
            sig = jax.nn.sigmoid(gt_ref[sl, :])
            dh_ref[3, sl, :] = (dgg * sig).astype(BF16)
            dh_ref[4, sl, :] = (dgg * val_ref[sl, :] * (sig * (1.0 - sig))).astype(BF16)
            return c

        _row_loop(t_rows, CONV_CHUNK, d_glu)

    def col(g):
        return pl.BlockSpec((t_rows, LANE), lambda i, g=g: (0, g * nt + i))

    tile = lambda rows: pl.BlockSpec((rows, LANE), lambda i: (0, i))
    return pl.pallas_call(
        body, grid=(nt,),
        in_specs=[col(0), col(1), col(2), col(3), col(4), tile(t_rows), tile(t_rows), tile(ka), tile(kb)],
        out_specs=[pl.BlockSpec((5, t_rows, LANE), lambda i: (0, 0, i)), tile(SUBLANE), tile(kb_rows)],
        out_shape=[jax.ShapeDtypeStruct((5, t_rows, wa_w), BF16), jax.ShapeDtypeStruct((SUBLANE, wa_w), F32),
                   jax.ShapeDtypeStruct((kb_rows, wa_w), F32)],
        scratch_shapes=[pltpu.VMEM((nr + SUBLANE, LANE), F32), pltpu.VMEM((SUBLANE, nr, LANE), F32),
                        pltpu.VMEM((nr + SUBLANE, LANE), F32), pltpu.VMEM((SUBLANE, nr, LANE), F32),
                        pltpu.VMEM((t_rows, LANE), F32)],
        name="mix_conv_bwd", compiler_params=_cparams(1))(hin, hin, hin, hin, hin, dy, dz, wa, wb)


def _bwd_in_proj(dh5, win_t, h0, dh1, g1, tm):
    t_rows, d = h0.shape
    groups, _, w = dh5.shape

    def body(dh_ref, w_ref, h0_ref, dh1_ref, g1_ref, dh0_ref, dg1_ref):
        dxn1 = None
        for g in range(groups):
            part = _dot_nn(dh_ref[g], w_ref[pl.ds(g * w, w), :])
            dxn1 = part if dxn1 is None else dxn1 + part
        h0 = h0_ref[...]
        r1 = _rstd(h0)
        h0h = h0 * r1
        _acc_rows(dg1_ref, dxn1 * h0h, pl.program_id(0) == 0)
        dh0_ref[...] = dh1_ref[...] + _rms_bwd(dxn1 * g1_ref[...], h0h, r1)

    blk = pl.BlockSpec((tm, d), lambda i: (i, 0))
    return pl.pallas_call(
        body, grid=(t_rows // tm,),
        in_specs=[pl.BlockSpec((groups, tm, w), lambda i: (0, i, 0)), _resident(win_t.shape), blk, blk, _full((1, d))],
        out_specs=[blk, _full((1, d))],
        out_shape=[jax.ShapeDtypeStruct((t_rows, d), F32), jax.ShapeDtypeStruct((1, d), F32)],
        name="bwd_in_proj", compiler_params=_cparams(1))(dh5, win_t, h0, dh1, g1)


def _reduce_grads(grads, rows, smalls):
    d = grads[0].shape[1]
    r_loc = sum(rows)
    lo = [sum(rows[:p]) for p in range(5)]
    add_chunk = 32
    (dmeta, dg1, dg2, dg3, dg4, dbb, dlg, dlb, lossv, dwa, dwb) = smalls
    half = d // 2
    kb_rows = dwb.shape[0]

    def body(g0, g1r, g2r, g3r, g4r, dmeta_ref, dg1_ref, dg2_ref, dg3_ref, dg4_ref, dbb_ref, dlg_ref, dlb_ref,
             loss_ref, dwa_ref, dwb_ref, recv_a,
             o0, o1, o2, o3, o4, ptot_ref,
             gbuf, abuf, acc, sendbuf, recv_b, pbuf, psib, chip_p,
             a_send, a_recv, b_send, b_recv, ps_send, ps_recv, pc_send, pc_recv, local_sems):
        gs = (g0, g1r, g2r, g3r, g4r)
        outs = (o0, o1, o2, o3, o4)
        x, y, c = _mesh_pos()
        me, sibling = (x, y, c), (x, y, 1 - c)
        chips = [(x, y), (1 - x, y), (x, 1 - y), (1 - x, 1 - y)]

        pbuf[...] = jnp.zeros_like(pbuf)
        pbuf[pl.ds(0, N_META), :] = dmeta_ref[...]
        for row, ref in ((16, dg1_ref), (17, dg2_ref), (18, dg3_ref), (19, dg4_ref)):
            pbuf[pl.ds(row, 1), :] = ref[...]
        pbuf[pl.ds(20, 1), pl.ds(0, half)] = dbb_ref[...]
        pbuf[pl.ds(20, 1), pl.ds(half, half)] = dlg_ref[...]
        pbuf[pl.ds(21, 1), pl.ds(0, half)] = dlb_ref[...]
        lv = loss_ref[...]
        pbuf[pl.ds(21, 1), pl.ds(half, half)] = lv[:, :half] + lv[:, half:]
        pbuf[pl.ds(24, SUBLANE), pl.ds(0, half)] = dwa_ref[...]
        pbuf[pl.ds(32, kb_rows), pl.ds(0, half)] = dwb_ref[...]
        to_sib = pltpu.make_async_remote_copy(src_ref=pbuf, dst_ref=psib, send_sem=ps_send.at[0], recv_sem=ps_recv.at[0],
                                              device_id=sibling, device_id_type=MESH)
        to_sib.start()

        def a_copies(r):
            j = _dev_index(*chips[r], 1 - c)
            return [pltpu.make_async_remote_copy(
                src_ref=gs[p].at[pl.ds(pl.multiple_of(j * rows[p], 16), rows[p]), :],
                dst_ref=recv_a.at[r, pl.ds(lo[p], rows[p]), :],
                send_sem=a_send.at[r], recv_sem=a_recv.at[r], device_id=sibling, device_id_type=MESH) for p in range(5)]

        def a_wait(r):
            whole = recv_a.at[r]
            return pltpu.make_async_remote_copy(src_ref=whole, dst_ref=whole, send_sem=a_send.at[r], recv_sem=a_recv.at[r],
                                                device_id=sibling, device_id_type=MESH)

        order = (1, 2, 3, 0)
        for r in order:
            for cp in a_copies(r):
                cp.start()

        to_sib.wait_recv()
        my_chip = 2 * x + y
        chip_p[my_chip] = pbuf[...] + psib[...]
        to_sib.wait_send()

        def p_copy(k):
            slot = chip_p.at[my_chip]
            return pltpu.make_async_remote_copy(src_ref=slot, dst_ref=slot, send_sem=pc_send.at[k], recv_sem=pc_recv.at[k],
                                                device_id=(*chips[k + 1], c), device_id_type=MESH)

        p_copies = [p_copy(k) for k in range(3)]
        for cp in p_copies:
            cp.start()

        def b_copy(k):
            return pltpu.make_async_remote_copy(src_ref=sendbuf.at[k], dst_ref=recv_b.at[k], send_sem=b_send.at[k],
                                                recv_sem=b_recv.at[k], device_id=(*chips[k + 1], c), device_id_type=MESH)

        b_copies = [b_copy(k) for k in range(3)]
        for r in order:
            j = _dev_index(*chips[r], c)
            loads = [pltpu.make_async_copy(gs[p].at[pl.ds(pl.multiple_of(j * rows[p], 16), rows[p]), :],
                                           gbuf.at[pl.ds(lo[p], rows[p]), :], local_sems.at[0]) for p in range(5)]
            for cp in loads:
                cp.start()
            a_wait(r).wait_recv()
            from_a = pltpu.make_async_copy(recv_a.at[r], abuf, local_sems.at[1])
            from_a.start()
            pltpu.make_async_copy(gbuf, gbuf, local_sems.at[0]).wait()
            from_a.wait()

            def add(b, carry, r=r):
                sl = pl.ds(b, add_chunk)
                s = gbuf[sl, :].astype(F32) + abuf[sl, :].astype(F32)
                if r == 0:
                    acc[sl, :] = s
                else:
                    sendbuf[r - 1, sl, :] = s.astype(BF16)
                return carry

            _row_loop(r_loc, add_chunk, add)
            if r != 0:
                b_copies[r - 1].start()

        for k in range(3):
            b_copies[k].wait_recv()

        def final(b, carry):
            sl = pl.ds(b, add_chunk)
            acc[sl, :] = ((acc[sl, :] + recv_b[0, sl, :].astype(F32)) + recv_b[1, sl, :].astype(F32)) + recv_b[2, sl, :].astype(F32)
            return carry

        _row_loop(r_loc, add_chunk, final)
        for p in range(5):
            outs[p][...] = acc[pl.ds(lo[p], rows[p]), :]

        for cp in p_copies:
            cp.wait_recv()
        ptot_ref[...] = ((chip_p[0] + chip_p[1]) + chip_p[2]) + chip_p[3]
        for cp in p_copies:
            cp.wait_send()
        for k in range(3):
            b_copies[k].wait_send()
        for r in order:
            a_wait(r).wait_send()

    any_spec = pl.BlockSpec(memory_space=pl.ANY)
    vmem = pl.BlockSpec(memory_space=pltpu.VMEM)
    out_shape = [jax.ShapeDtypeStruct((4, r_loc, d), BF16)]
    out_shape += [jax.ShapeDtypeStruct((r, d), F32) for r in rows]
    out_shape.append(jax.ShapeDtypeStruct((SMALL_ROWS, d), F32))
    res = pl.pallas_call(
        body, out_shape=out_shape, in_specs=[any_spec] * 5 + [vmem] * 11, out_specs=[any_spec] + [vmem] * 6,
        scratch_shapes=[pltpu.VMEM((r_loc, d), BF16), pltpu.VMEM((r_loc, d), BF16), pltpu.VMEM((r_loc, d), F32),
                        pltpu.VMEM((3, r_loc, d), BF16), pltpu.VMEM((3, r_loc, d), BF16),
                        pltpu.VMEM((SMALL_ROWS, d), F32), pltpu.VMEM((SMALL_ROWS, d), F32), pltpu.VMEM((4, SMALL_ROWS, d), F32),
                        pltpu.SemaphoreType.DMA((4,)), pltpu.SemaphoreType.DMA((4,)),
                        pltpu.SemaphoreType.DMA((3,)), pltpu.SemaphoreType.DMA((3,)),
                        pltpu.SemaphoreType.DMA((1,)), pltpu.SemaphoreType.DMA((1,)),
                        pltpu.SemaphoreType.DMA((3,)), pltpu.SemaphoreType.DMA((3,)),
                        pltpu.SemaphoreType.DMA((2,))],
        name="reduce_grads", compiler_params=_cparams())(*grads, *smalls)
    return res[1:6], res[6]


def _adamw(w, g, m, v):
    m = ADAM_B1 * m + (1.0 - ADAM_B1) * g
    v = ADAM_B2 * v + (1.0 - ADAM_B2) * jnp.square(g)
    m_hat = m / (1.0 - ADAM_B1 ** ADAM_STEP)
    v_hat = v / (1.0 - ADAM_B2 ** ADAM_STEP)
    delta = -ADAM_LR * (m_hat / (jnp.sqrt(v_hat) + ADAM_EPS) + ADAM_WD * w)
    return delta, m, v


def _adam_big(g_t, w, m, v, transposed, name):
    def body(g_ref, w_ref, m_ref, v_ref, go_ref, d_ref, mo_ref, vo_ref):
        g = g_ref[...].T if transposed else g_ref[...]
        go_ref[...] = g
        d_ref[...], mo_ref[...], vo_ref[...] = _adamw(w_ref[...], g, m_ref[...], v_ref[...])

    return pl.pallas_call(body, out_shape=[jax.ShapeDtypeStruct(w.shape, F32)] * 4, name=name,
                          compiler_params=_cparams())(g_t, w, m, v)


def _adam_small(gs, ws, ms, vs):
    n = len(gs)

    def body(*refs):
        ins, outs = refs[:4 * n], refs[4 * n:]
        for i in range(n):
            g = ins[i][...]
            delta, m, v = _adamw(ins[n + i][...], g, ins[2 * n + i][...], ins[3 * n + i][...])
            outs[i][...] = delta
            outs[n + i][...] = m
            outs[2 * n + i][...] = v

    shapes = [jax.ShapeDtypeStruct(w.shape, F32) for w in ws]
    return pl.pallas_call(body, out_shape=shapes * 3, name="adam_small", compiler_params=_cparams())(*gs, *ws, *ms, *vs)


def kernel(x, meta_tokens, pre_mix_norm, w_in, conv_a_w, conv_b_w, conv_b_bias, ln_b_gain, ln_b_bias, w_out, post_mix_norm, pre_ffn_norm, w_gate, w_up, w_down, post_ffn_norm, loss_target, m_meta_tokens, m_pre_mix_norm, m_w_in, m_conv_a_w, m_conv_b_w, m_conv_b_bias, m_ln_b_gain, m_ln_b_bias, m_w_out, m_post_mix_norm, m_pre_ffn_norm, m_w_gate, m_w_up, m_w_down, m_post_ffn_norm, v_meta_tokens, v_pre_mix_norm, v_w_in, v_conv_a_w, v_conv_b_w, v_conv_b_bias, v_ln_b_gain, v_ln_b_bias, v_w_out, v_post_mix_norm, v_pre_ffn_norm, v_w_gate, v_w_up, v_w_down, v_post_ffn_norm):
    _, seq, d = x.shape
    ka, ca_loc = conv_a_w.shape[1:]
    kb, cb_loc = conv_b_w.shape[1:]
    wa_w = ca_loc * N_DEV
    assert cb_loc == ca_loc and wa_w % LANE == 0 and w_in.shape[2] * N_DEV == 5 * wa_w
    pad = (-(N_META + seq)) % ROW_ALIGN
    x0 = pad + N_META
    t_rows = x0 + seq
    assert t_rows % (N_ROW_BLOCKS * 16) == 0 and t_rows % CONV_CHUNK == 0 and d % LANE == 0
    tm = t_rows // N_ROW_BLOCKS
    me = _dev_index(*_mesh_pos())

    w_loc = (w_in[0], w_out[0], w_gate[0], w_up[0], w_down[0])
    rows = [w_loc[0].shape[1], w_loc[1].shape[0], w_loc[2].shape[1], w_loc[3].shape[1], w_loc[4].shape[0]]
    wl = _pack_weights(*w_loc)
    sm = jnp.zeros((SM_ROWS, LANE), F32)
    sm = sm.at[0:N_META, :].set(meta_tokens)
    sm = sm.at[16:16 + ka, 0:ca_loc].set(conv_a_w[0])
    sm = sm.at[24:24 + kb, 0:cb_loc].set(conv_b_w[0])
    win_t, wout, wg_t, wu_t, wd, sm_all = _all_gather(wl, sm, rows)
    meta_full = jnp.transpose(sm_all[:, 0:N_META, :], (1, 0, 2)).reshape(N_META, d)
    wa = jnp.transpose(sm_all[:, 16:16 + ka, 0:ca_loc], (1, 0, 2)).reshape(ka, wa_w)
    wb = jnp.transpose(sm_all[:, 24:24 + kb, 0:cb_loc], (1, 0, 2)).reshape(kb, wa_w)

    h0, tgt = _prep(x[0], loss_target[0], meta_full, t_rows, x0)
    xn1, hin = _in_proj(h0, pre_mix_norm, win_t, tm)
    ya, z = _mix_conv_fwd(hin, wa, wb, conv_b_bias, wa_w)
    y = _mix_ln_fwd(ya, z, ln_b_gain, ln_b_bias, tm)
    mix, h1, xn2 = _out_proj(y, wout, h0, post_mix_norm, pre_ffn_norm, tm)
    a, u, s = _gate_up(xn2, wg_t, wu_t, tm)
    dh2, dff, dg4, lossv = _down_loss(s, wd, h1, tgt, post_ffn_norm, tm, x0)

    gwd = _wgrad(s, dff, "wgrad_down")
    da, du = _bwd_down(dff, wd, a, u, tm)
    gwg = _wgrad(da, xn2, "wgrad_gate")
    gwu = _wgrad(du, xn2, "wgrad_up")
    dh1, dg3 = _bwd_ffn_in(da, du, wg_t, wu_t, h1, dh2, pre_ffn_norm, tm)
    dmix, dy, dg2 = _bwd_out_proj(dh1, mix, wout, post_mix_norm, tm)
    gwo = _wgrad(y, dmix, "wgrad_out")
    dz, dlg, dlb, dbb = _mix_ln_bwd(z, dy, ln_b_gain, ln_b_bias, tm)
    dh5, dwa, dwb = _mix_conv_bwd(hin, dy, dz, wa, wb, wa_w)
    gwi = _wgrad(dh5, xn1, "wgrad_in")
    dh0, dg1 = _bwd_in_proj(dh5, win_t, h0, dh1, pre_mix_norm, tm)
    grad_x = dh0[x0:][None]
    dmeta = dh0[x0 - N_META:x0]

    big, ptot = _reduce_grads((gwi, gwo, gwg, gwu, gwd), rows, (dmeta, dg1, dg2, dg3, dg4, dbb, dlg, dlb, lossv, dwa, dwb))
    half = d // 2
    loss = (0.5 / d) * jnp.sum(ptot[21, half:])
    g_meta = lax.dynamic_slice(ptot, (0, me * (d // N_DEV)), (N_META, d // N_DEV))
    g_small = [g_meta, ptot[16:17], lax.dynamic_slice(ptot, (24, me * ca_loc), (ka, ca_loc))[None],
               lax.dynamic_slice(ptot, (32, me * cb_loc), (kb, cb_loc))[None],
               ptot[20:21, :half], ptot[20:21, half:], ptot[21:22, :half], ptot[17:18], ptot[18:19], ptot[19:20]]
    w_small = [meta_tokens, pre_mix_norm, conv_a_w, conv_b_w, conv_b_bias, ln_b_gain, ln_b_bias, post_mix_norm,
               pre_ffn_norm, post_ffn_norm]
    m_small = [m_meta_tokens, m_pre_mix_norm, m_conv_a_w, m_conv_b_w, m_conv_b_bias, m_ln_b_gain, m_ln_b_bias,
               m_post_mix_norm, m_pre_ffn_norm, m_post_ffn_norm]
    v_small = [v_meta_tokens, v_pre_mix_norm, v_conv_a_w, v_conv_b_w, v_conv_b_bias, v_ln_b_gain, v_ln_b_bias,
               v_post_mix_norm, v_pre_ffn_norm, v_post_ffn_norm]
    small = _adam_small(g_small, w_small, m_small, v_small)
    n_small = len(w_small)
    d_small, nm_small, nv_small = small[:n_small], small[n_small:2 * n_small], small[2 * n_small:]

    bigs = {}
    for name, g_t, w, m, v, tr in (("w_in", big[0], w_in, m_w_in, v_w_in, True), ("w_out", big[1], w_out, m_w_out, v_w_out, False),
                                   ("w_gate", big[2], w_gate, m_w_gate, v_w_gate, True), ("w_up", big[3], w_up, m_w_up, v_w_up, True),
                                   ("w_down", big[4], w_down, m_w_down, v_w_down, False)):
        bigs[name] = [o[None] for o in _adam_big(g_t, w[0], m[0], v[0], tr, "adam_" + name)]

    def ordered(pick_small, pick_big):
        sm_it = iter(range(n_small))
        out = []
        for name in ("s", "s", "w_in", "s", "s", "s", "s", "s", "w_out", "s", "s", "w_gate", "w_up", "w_down", "s"):
            out.append(pick_small(next(sm_it)) if name == "s" else pick_big(name))
        return out

    grads = ordered(lambda i: g_small[i], lambda n: bigs[n][0])
    deltas = ordered(lambda i: d_small[i], lambda n: bigs[n][1])
    new_m = ordered(lambda i: nm_small[i], lambda n: bigs[n][2])
    new_v = ordered(lambda i: nv_small[i], lambda n: bigs[n][3])
    return (loss, grad_x, *grads, *deltas, *new_m, *new_v)
```

```python
import functools

import jax
import jax.numpy as jnp
from jax import lax
from jax.experimental import pallas as pl
from jax.experimental.pallas import tpu as pltpu

F32 = jnp.float32
BF16 = jnp.bfloat16
MESH = pl.DeviceIdType.MESH

N_META = 16
N_DEV = 8
RMS_EPS = 1e-6
LN_EPS = 1e-5
ADAM_LR = 0.001
ADAM_B1 = 0.9
ADAM_B2 = 0.999
ADAM_EPS = 1e-08
ADAM_WD = 0.01
ADAM_STEP = 10

LANE = 128
SUBLANE = 8
ROW_ALIGN = 128
N_ROW_BLOCKS = 4
CONV_HALO = 32
CONV_CHUNK = 64
N_CHUNK = 512
WGRAD_TILE = 256
V7X_VMEM_BYTES = 64 * 1024 * 1024
VMEM_LIMIT = V7X_VMEM_BYTES - 6 * 1024 * 1024
SMALL_ROWS = 64
SM_ROWS = 56


def _cparams(n_grid_axes=0):
    sem = ("arbitrary",) * n_grid_axes if n_grid_axes else None
    return pltpu.CompilerParams(dimension_semantics=sem, vmem_limit_bytes=VMEM_LIMIT)


def _mesh_pos():
    return lax.axis_index("x"), lax.axis_index("y"), lax.axis_index("c")


def _dev_index(px, py, pc):
    return 4 * px + 2 * py + pc


def _full(shape):
    return pl.BlockSpec(shape, lambda *_: (0,) * len(shape))


def _resident(shape):
    return pl.BlockSpec(shape, lambda *_: (0,) * len(shape), pipeline_mode=pl.Buffered(1))


def _dot_nt(a, w):
    return lax.dot_general(a, w, (((1,), (1,)), ((), ())), preferred_element_type=F32)


def _dot_nn(a, w):
    return jnp.dot(a, w, preferred_element_type=F32)


def _chunks(n, c):
    out, o = [], 0
    while o < n:
        out.append((o, min(c, n - o)))
        o += c
    return out


def _rstd(h):
    return lax.rsqrt(jnp.mean(h * h, axis=-1, keepdims=True) + RMS_EPS)


def _rms_bwd(dyh, yh, r):
    return r * (dyh - yh * jnp.mean(dyh * yh, axis=-1, keepdims=True))


def _silu_grad(a, sig):
    return sig * (1.0 + a * (1.0 - sig))


def _acc_rows(ref, val, first):
    s = jnp.sum(val, axis=0, keepdims=True)

    @pl.when(first)
    def _():
        ref[...] = s

    @pl.when(jnp.logical_not(first))
    def _():
        ref[...] += s


def _pack_weights(shards):
    d = shards[0].shape[1]
    rows = [w.shape[0] for w in shards]

    def body(*refs):
        o_ref = refs[-1]
        o = 0
        for ref, r in zip(refs[:-1], rows):
            o_ref[pl.ds(o, r), :] = ref[...].astype(BF16)
            o += r

    return pl.pallas_call(body, out_shape=jax.ShapeDtypeStruct((sum(rows), d), BF16), name="pack_weights",
                          compiler_params=_cparams())(*shards)


def _all_gather(wl, sm, rows):
    d = wl.shape[1]
    r_loc = sum(rows)
    lo = [sum(rows[:p]) for p in range(5)]

    def body(wl_ref, sm_ref, o0, o1, o2, o3, o4, sa_ref, send_sems, recv_sems, ssend, srecv, local_sems):
        outs = (o0, o1, o2, o3, o4)
        x, y, c = _mesh_pos()
        me, sibling = (x, y, c), (x, y, 1 - c)
        chips = [(1 - x, y), (x, 1 - y), (1 - x, 1 - y)]

        def block_copies(k, blk, to, from_local):
            j = _dev_index(*blk)
            cps = []
            for p in range(5):
                dst = outs[p].at[pl.ds(pl.multiple_of(j * rows[p], 16), rows[p]), :]
                src = wl_ref.at[pl.ds(lo[p], rows[p]), :] if from_local else dst
                cps.append(pltpu.make_async_remote_copy(src_ref=src, dst_ref=dst, send_sem=send_sems.at[k],
                                                        recv_sem=recv_sems.at[k], device_id=to, device_id_type=MESH))
            return cps

        def block_wait(k):
            whole = wl_ref.at[pl.ds(0, r_loc), :]
            return pltpu.make_async_remote_copy(src_ref=whole, dst_ref=whole, send_sem=send_sems.at[k],
                                                recv_sem=recv_sems.at[k], device_id=me, device_id_type=MESH)

        def small_copy(k, to):
            j = _dev_index(*me)
            return pltpu.make_async_remote_copy(src_ref=sm_ref, dst_ref=sa_ref.at[j], send_sem=ssend.at[k],
                                                recv_sem=srecv.at[k], device_id=to, device_id_type=MESH)

        jme = _dev_index(*me)
        mine = [pltpu.make_async_copy(wl_ref.at[pl.ds(lo[p], rows[p]), :],
                                      outs[p].at[pl.ds(pl.multiple_of(jme * rows[p], 16), rows[p]), :], local_sems.at[0])
                for p in range(5)]
        mine.append(pltpu.make_async_copy(sm_ref, sa_ref.at[jme], local_sems.at[1]))
        for cp in mine:
            cp.start()
        peers = [sibling] + [(*chip, c) for chip in chips] + [(*chip, 1 - c) for chip in chips]
        smalls = [small_copy(k, to) for k, to in enumerate(peers)]
        for cp in smalls:
            cp.start()
        for cp in block_copies(0, me, sibling, True):
            cp.start()
        for j, chip in enumerate(chips):
            for cp in block_copies(1 + j, me, (*chip, c), True):
                cp.start()
        for j, chip in enumerate(chips):
            block_wait(1 + j).wait_recv()
            for cp in block_copies(4 + j, (*chip, c), sibling, False):
                cp.start()
        block_wait(0).wait_recv()
        for j in range(3):
            block_wait(4 + j).wait_recv()
        for k in range(7):
            block_wait(k).wait_send()
        for cp in smalls:
            cp.wait_recv()
        for cp in smalls:
            cp.wait_send()
        pltpu.make_async_copy(wl_ref, wl_ref, local_sems.at[0]).wait()
        mine[-1].wait()

    any_spec = pl.BlockSpec(memory_space=pl.ANY)
    out_shape = [jax.ShapeDtypeStruct((N_DEV * r, d), BF16) for r in rows]
    out_shape.append(jax.ShapeDtypeStruct((N_DEV,) + sm.shape, F32))
    return pl.pallas_call(
        body, out_shape=out_shape, in_specs=[any_spec, any_spec], out_specs=[any_spec] * 6,
        scratch_shapes=[pltpu.SemaphoreType.DMA((7,)), pltpu.SemaphoreType.DMA((7,)),
                        pltpu.SemaphoreType.DMA((7,)), pltpu.SemaphoreType.DMA((7,)),
                        pltpu.SemaphoreType.DMA((2,))],
        name="all_gather_weights", compiler_params=_cparams())(wl, sm)


def _prep(x2, tgt2, meta_full, t_rows, x0):
    s, d = x2.shape
    assert x0 == ROW_ALIGN and s % ROW_ALIGN == 0

    def body(x_ref, tgt_ref, meta_ref, h0_ref, tp_ref):
        i = pl.program_id(0)

        @pl.when(i == 0)
        def _():
            h0_ref[...] = jnp.zeros_like(h0_ref)
            h0_ref[pl.ds(x0 - N_META, N_META), :] = meta_ref[...]
            tp_ref[...] = jnp.zeros_like(tp_ref)

        @pl.when(i > 0)
        def _():
            h0_ref[...] = x_ref[...]
            tp_ref[...] = tgt_ref[...]

    src = pl.BlockSpec((ROW_ALIGN, d), lambda i: (jnp.maximum(i - 1, 0), 0))
    dst = pl.BlockSpec((ROW_ALIGN, d), lambda i: (i, 0))
    return pl.pallas_call(
        body, grid=(t_rows // ROW_ALIGN,), in_specs=[src, src, _full((N_META, d))], out_specs=[dst, dst],
        out_shape=[jax.ShapeDtypeStruct((t_rows, d), F32)] * 2, name="prep", compiler_params=_cparams(1))(x2, tgt2, meta_full)


def _in_proj(h0, g1, win_t, tm):
    t_rows, d = h0.shape
    e = win_t.shape[0]

    def body(h_ref, g_ref, w_ref, xn_ref, hin_ref):
        h = h_ref[...]
        xn = ((h * _rstd(h)) * g_ref[...]).astype(BF16)
        xn_ref[...] = xn
        for o, n in _chunks(e, N_CHUNK):
            hin_ref[:, pl.ds(o, n)] = _dot_nt(xn, w_ref[pl.ds(o, n), :])

    return pl.pallas_call(
        body, grid=(t_rows // tm,),
        in_specs=[pl.BlockSpec((tm, d), lambda i: (i, 0)), _full((1, d)), _resident((e, d))],
        out_specs=[pl.BlockSpec((tm, d), lambda i: (i, 0)), pl.BlockSpec((tm, e), lambda i: (i, 0))],
        out_shape=[jax.ShapeDtypeStruct((t_rows, d), BF16), jax.ShapeDtypeStruct((t_rows, e), F32)],
        name="in_proj", compiler_params=_cparams(1))(h0, g1, win_t)


def _tap_slot(off):
    return off % SUBLANE, (off // SUBLANE) * SUBLANE


def _fill_shifted(sh_ref, base_ref, residues, n_rows):
    for r in residues:
        sh_ref[r] = base_ref[pl.ds(r, n_rows), :]


def _row_loop(t_rows, chunk, fn, carry=None):
    def step(i, c):
        return fn(pl.multiple_of(i * chunk, chunk), c)

    return lax.fori_loop(0, t_rows // chunk, step, carry)


def _mix_conv_fwd(hin, wa, wb, bb, wa_w):
    t_rows = hin.shape[0]
    nt = wa_w // LANE
    ka, kb = wa.shape[0], wb.shape[0]
    nr = CONV_HALO + t_rows

    def body(bg_ref, cg_ref, ha_ref, val_ref, gt_ref, wa_ref, wb_ref, bb_ref, ya_ref, z_ref, base, sh):
        base[pl.ds(0, CONV_HALO), :] = jnp.zeros((CONV_HALO, LANE), F32)
        base[pl.ds(nr, SUBLANE), :] = jnp.zeros((SUBLANE, LANE), F32)

        def conv(w_ref, k_taps, b, n):
            acc = None
            for k in range(k_taps):
                r, q = _tap_slot(CONV_HALO - (k_taps - 1) + k)
                term = w_ref[pl.ds(k, 1), :] * sh[r, pl.ds(b + q, n), :]
                acc = term if acc is None else acc + term
            return acc

        def fill_a(b, c):
            base[pl.ds(CONV_HALO + b, CONV_CHUNK), :] = cg_ref[pl.ds(b, CONV_CHUNK), :] * ha_ref[pl.ds(b, CONV_CHUNK), :]
            return c

        _row_loop(t_rows, CONV_CHUNK, fill_a)
        _fill_shifted(sh, base, sorted({_tap_slot(CONV_HALO - (ka - 1) + k)[0] for k in range(ka)}), nr)

        def out_a(b, c):
            ya_ref[pl.ds(b, CONV_CHUNK), :] = (bg_ref[pl.ds(b, CONV_CHUNK), :] * conv(wa_ref, ka, b, CONV_CHUNK)).astype(BF16)
            return c

        _row_loop(t_rows, CONV_CHUNK, out_a)

        def fill_b(b, c):
            base[pl.ds(CONV_HALO + b, CONV_CHUNK), :] = (val_ref[pl.ds(b, CONV_CHUNK), :]
                                                          * jax.nn.sigmoid(gt_ref[pl.ds(b, CONV_CHUNK), :]))
            return c

        _row_loop(t_rows, CONV_CHUNK, fill_b)
        _fill_shifted(sh, base, range(SUBLANE), nr)

        def out_b(b, c):
            z_ref[pl.ds(b, CONV_CHUNK), :] = conv(wb_ref, kb, b, CONV_CHUNK) + bb_ref[...]
            return c

        _row_loop(t_rows, CONV_CHUNK, out_b)

    def col(g):
        return pl.BlockSpec((t_rows, LANE), lambda i, g=g: (0, g * nt + i))

    tile = lambda rows: pl.BlockSpec((rows, LANE), lambda i: (0, i))
    return pl.pallas_call(
        body, grid=(nt,),
        in_specs=[col(0), col(1), col(2), col(3), col(4), tile(ka), tile(kb), tile(1)],
        out_specs=[tile(t_rows), tile(t_rows)],
        out_shape=[jax.ShapeDtypeStruct((t_rows, wa_w), BF16), jax.ShapeDtypeStruct((t_rows, wa_w), F32)],
        scratch_shapes=[pltpu.VMEM((nr + SUBLANE, LANE), F32), pltpu.VMEM((SUBLANE, nr, LANE), F32)],
        name="mix_conv_fwd", compiler_params=_cparams(1))(hin, hin, hin, hin, hin, wa, wb, bb)


def _ln_parts(z, lg, lb):
    mu = jnp.mean(z, axis=-1, keepdims=True)
    zc = z - mu
    rstd = lax.rsqrt(jnp.mean(zc * zc, axis=-1, keepdims=True) + LN_EPS)
    zh = zc * rstd
    return zh, rstd, zh * lg + lb


def _mix_ln_fwd(ya, z, lg, lb, tm):
    t_rows, w = z.shape

    def body(ya_ref, z_ref, lg_ref, lb_ref, y_ref):
        _, _, ln = _ln_parts(z_ref[...], lg_ref[...], lb_ref[...])
        y_ref[:, pl.ds(0, w)] = ya_ref[...]
        y_ref[:, pl.ds(w, w)] = (ln * jax.nn.sigmoid(ln)).astype(BF16)

    blk = pl.BlockSpec((tm, w), lambda i: (i, 0))
    return pl.pallas_call(
        body, grid=(t_rows // tm,), in_specs=[blk, blk, _full((1, w)), _full((1, w))],
        out_specs=pl.BlockSpec((tm, 2 * w), lambda i: (i, 0)),
        out_shape=jax.ShapeDtypeStruct((t_rows, 2 * w), BF16), name="mix_ln_fwd", compiler_params=_cparams(1))(ya, z, lg, lb)


def _out_proj(y, w_out, h0, g2, g3, tm):
    t_rows, d = h0.shape

    def body(y_ref, w_ref, h0_ref, g2_ref, g3_ref, mix_ref, h1_ref, xn2_ref):
        mix = _dot_nn(y_ref[...], w_ref[...])
        mix_ref[...] = mix
        h1 = h0_ref[...] + (mix * _rstd(mix)) * g2_ref[...]
        h1_ref[...] = h1
        xn2_ref[...] = ((h1 * _rstd(h1)) * g3_ref[...]).astype(BF16)

    blk = pl.BlockSpec((tm, d), lambda i: (i, 0))
    return pl.pallas_call(
        body, grid=(t_rows // tm,), in_specs=[blk, _resident(w_out.shape), blk, _full((1, d)), _full((1, d))],
        out_specs=[blk, blk, blk],
        out_shape=[jax.ShapeDtypeStruct((t_rows, d), F32), jax.ShapeDtypeStruct((t_rows, d), F32),
                   jax.ShapeDtypeStruct((t_rows, d), BF16)],
        name="out_proj", compiler_params=_cparams(1))(y, w_out, h0, g2, g3)


def _gate_up(xn2, wg_t, wu_t, tm):
    t_rows, d = xn2.shape
    f = wg_t.shape[0]

    def body(x_ref, wg_ref, wu_ref, a_ref, u_ref, s_ref):
        xn = x_ref[...]
        for o, n in _chunks(f, N_CHUNK):
            a = _dot_nt(xn, wg_ref[pl.ds(o, n), :])
            u = _dot_nt(xn, wu_ref[pl.ds(o, n), :])
            a_ref[:, pl.ds(o, n)] = a.astype(BF16)
            u_ref[:, pl.ds(o, n)] = u.astype(BF16)
            s_ref[:, pl.ds(o, n)] = ((a * jax.nn.sigmoid(a)) * u).astype(BF16)

    blk = pl.BlockSpec((tm, f), lambda i: (i, 0))
    return pl.pallas_call(
        body, grid=(t_rows // tm,),
        in_specs=[pl.BlockSpec((tm, d), lambda i: (i, 0)), _resident((f, d)), _resident((f, d))],
        out_specs=[blk, blk, blk], out_shape=[jax.ShapeDtypeStruct((t_rows, f), BF16)] * 3,
        name="gate_up", compiler_params=_cparams(1))(xn2, wg_t, wu_t)


def _down_loss(s, wd, h1, tgt, g4, tm, x0):
    t_rows, d = h1.shape
    f = wd.shape[0]

    def body(s_ref, w_ref, h1_ref, tgt_ref, g4_ref, dh2_ref, dff_ref, dg4_ref, loss_ref):
        i = pl.program_id(0)
        ff = _dot_nn(s_ref[...], w_ref[...])
        r4 = _rstd(ff)
        fh = ff * r4
        g4 = g4_ref[...]
        h2 = h1_ref[...] + fh * g4
        row = i * tm + lax.broadcasted_iota(jnp.int32, (tm, 1), 0)
        diff = jnp.where(row >= x0, h2 - tgt_ref[...], 0.0)
        dh2 = diff / d
        dh2_ref[...] = dh2
        dff_ref[...] = _rms_bwd(dh2 * g4, fh, r4).astype(BF16)
        _acc_rows(dg4_ref, dh2 * fh, i == 0)
        _acc_rows(loss_ref, diff * diff, i == 0)

    blk = pl.BlockSpec((tm, d), lambda i: (i, 0))
    return pl.pallas_call(
        body, grid=(t_rows // tm,),
        in_specs=[pl.BlockSpec((tm, f), lambda i: (i, 0)), _resident((f, d)), blk, blk, _full((1, d))],
        out_specs=[blk, blk, _full((1, d)), _full((1, d))],
        out_shape=[jax.ShapeDtypeStruct((t_rows, d), F32), jax.ShapeDtypeStruct((t_rows, d), BF16),
                   jax.ShapeDtypeStruct((1, d), F32), jax.ShapeDtypeStruct((1, d), F32)],
        name="down_loss", compiler_params=_cparams(1))(s, wd, h1, tgt, g4)


def _bwd_down(dff, wd, a, u, tm):
    t_rows, d = dff.shape
    f = wd.shape[0]

    def body(dff_ref, w_ref, a_ref, u_ref, da_ref, du_ref):
        dff_v = dff_ref[...]
        for o, n in _chunks(f, N_CHUNK):
            ds = _dot_nt(dff_v, w_ref[pl.ds(o, n), :])
            av = a_ref[:, pl.ds(o, n)].astype(F32)
            uv = u_ref[:, pl.ds(o, n)].astype(F32)
            sig = jax.nn.sigmoid(av)
            da_ref[:, pl.ds(o, n)] = (ds * uv * _silu_grad(av, sig)).astype(BF16)
            du_ref[:, pl.ds(o, n)] = (ds * (av * sig)).astype(BF16)

    blk = pl.BlockSpec((tm, f), lambda i: (i, 0))
    return pl.pallas_call(
        body, grid=(t_rows // tm,),
        in_specs=[pl.BlockSpec((tm, d), lambda i: (i, 0)), _resident((f, d)), blk, blk],
        out_specs=[blk, blk], out_shape=[jax.ShapeDtypeStruct((t_rows, f), BF16)] * 2,
        name="bwd_down", compiler_params=_cparams(1))(dff, wd, a, u)


def _wgrad(a, b, name):
    d = b.shape[1]
    t_rows = b.shape[0]
    stacked = a.ndim == 3
    n = a.shape[-1]
    groups = a.shape[0] if stacked else 1
    tiles = n // WGRAD_TILE

    def body(a_ref, b_ref, o_ref):
        o_ref[...] = lax.dot_general(a_ref[...], b_ref[...], (((0,), (0,)), ((), ())),
                                     preferred_element_type=F32).astype(BF16)

    if stacked:
        a_spec = pl.BlockSpec((None, t_rows, WGRAD_TILE), lambda g, i: (g, 0, i))
    else:
        a_spec = pl.BlockSpec((t_rows, WGRAD_TILE), lambda g, i: (0, i))
    return pl.pallas_call(
        body, grid=(groups, tiles), in_specs=[a_spec, pl.BlockSpec((t_rows, d), lambda g, i: (0, 0))],
        out_specs=pl.BlockSpec((WGRAD_TILE, d), lambda g, i: (g * tiles + i, 0)),
        out_shape=jax.ShapeDtypeStruct((groups * n, d), BF16), name=name, compiler_params=_cparams(2))(a, b)


def _bwd_ffn_in(da, du, wg_t, wu_t, h1, dh2, g3, tm):
    t_rows, d = h1.shape
    f = wg_t.shape[0]

    def body(da_ref, du_ref, wg_ref, wu_ref, h1_ref, dh2_ref, g3_ref, dh1_ref, dg3_ref):
        dxn2 = _dot_nn(da_ref[...], wg_ref[...]) + _dot_nn(du_ref[...], wu_ref[...])
        h1 = h1_ref[...]
        r3 = _rstd(h1)
        h1h = h1 * r3
        _acc_rows(dg3_ref, dxn2 * h1h, pl.program_id(0) == 0)
        dh1_ref[...] = dh2_ref[...] + _rms_bwd(dxn2 * g3_ref[...], h1h, r3)

    blk = pl.BlockSpec((tm, d), lambda i: (i, 0))
    blkf = pl.BlockSpec((tm, f), lambda i: (i, 0))
    return pl.pallas_call(
        body, grid=(t_rows // tm,),
        in_specs=[blkf, blkf, _resident((f, d)), _resident((f, d)), blk, blk, _full((1, d))],
        out_specs=[blk, _full((1, d))],
        out_shape=[jax.ShapeDtypeStruct((t_rows, d), F32), jax.ShapeDtypeStruct((1, d), F32)],
        name="bwd_ffn_in", compiler_params=_cparams(1))(da, du, wg_t, wu_t, h1, dh2, g3)


def _bwd_out_proj(dh1, mix, w_out, g2, tm):
    t_rows, d = dh1.shape

    def body(dh1_ref, mix_ref, w_ref, g2_ref, dmix_ref, dy_ref, dg2_ref):
        mix = mix_ref[...]
        r2 = _rstd(mix)
        mh = mix * r2
        dh1 = dh1_ref[...]
        _acc_rows(dg2_ref, dh1 * mh, pl.program_id(0) == 0)
        dmix = _rms_bwd(dh1 * g2_ref[...], mh, r2).astype(BF16)
        dmix_ref[...] = dmix
        dy_ref[...] = _dot_nt(dmix, w_ref[...])

    blk = pl.BlockSpec((tm, d), lambda i: (i, 0))
    return pl.pallas_call(
        body, grid=(t_rows // tm,), in_specs=[blk, blk, _resident(w_out.shape), _full((1, d))],
        out_specs=[blk, blk, _full((1, d))],
        out_shape=[jax.ShapeDtypeStruct((t_rows, d), BF16), jax.ShapeDtypeStruct((t_rows, d), F32),
                   jax.ShapeDtypeStruct((1, d), F32)],
        name="bwd_out_proj", compiler_params=_cparams(1))(dh1, mix, w_out, g2)


def _mix_ln_bwd(z, dy, lg, lb, tm):
    t_rows, w = z.shape

    def body(z_ref, dyb_ref, lg_ref, lb_ref, dz_ref, dlg_ref, dlb_ref, dbb_ref):
        first = pl.program_id(0) == 0
        lg = lg_ref[...]
        zh, rstd, ln = _ln_parts(z_ref[...], lg, lb_ref[...])
        dln = dyb_ref[...] * _silu_grad(ln, jax.nn.sigmoid(ln))
        _acc_rows(dlg_ref, dln * zh, first)
        _acc_rows(dlb_ref, dln, first)
        dzh = dln * lg
        dz = rstd * (dzh - jnp.mean(dzh, axis=-1, keepdims=True) - zh * jnp.mean(dzh * zh, axis=-1, keepdims=True))
        dz_ref[...] = dz
        _acc_rows(dbb_ref, dz, first)

    blk = pl.BlockSpec((tm, w), lambda i: (i, 0))
    vec = _full((1, w))
    return pl.pallas_call(
        body, grid=(t_rows // tm,), in_specs=[blk, pl.BlockSpec((tm, w), lambda i: (i, 1)), vec, vec],
        out_specs=[blk, vec, vec, vec],
        out_shape=[jax.ShapeDtypeStruct((t_rows, w), F32)] + [jax.ShapeDtypeStruct((1, w), F32)] * 3,
        name="mix_ln_bwd", compiler_params=_cparams(1))(z, dy, lg, lb)


def _mix_conv_bwd(hin, dy, dz, wa, wb, wa_w):
    t_rows = hin.shape[0]
    nt = wa_w // LANE
    ka, kb = wa.shape[0], wb.shape[0]
    nr = CONV_HALO + t_rows
    kb_rows = -(-kb // SUBLANE) * SUBLANE

    def body(bg_ref, cg_ref, ha_ref, val_ref, gt_ref, dya_ref, dz_ref, wa_ref, wb_ref,
             dh_ref, dwa_ref, dwb_ref, base, sh, based, shd, tmp):
        zeros = lambda n: jnp.zeros((n, LANE), F32)
        base[pl.ds(0, CONV_HALO), :] = zeros(CONV_HALO)
        base[pl.ds(nr, SUBLANE), :] = zeros(SUBLANE)
        based[pl.ds(t_rows, CONV_HALO + SUBLANE), :] = zeros(CONV_HALO + SUBLANE)

        def fwd_slot(k_taps, k):
            return _tap_slot(CONV_HALO - (k_taps - 1) + k)

        def bwd_slot(k_taps, k):
            return _tap_slot(k_taps - 1 - k)

        def conv(w_ref, k_taps, src, slot, b, n):
            acc = None
            for k in range(k_taps):
                r, q = slot(k_taps, k)
                term = w_ref[pl.ds(k, 1), :] * src[r, pl.ds(b + q, n), :]
                acc = term if acc is None else acc + term
            return acc

        def wgrad_loop(k_taps, d_of, dsrc_to_tmp):
            def step(b, accs):
                dv = d_of(b)
                new = []
                for k in range(k_taps):
                    r, q = fwd_slot(k_taps, k)
                    new.append(accs[k] + dv * sh[r, pl.ds(b + q, SUBLANE), :])
                tmp[pl.ds(b, SUBLANE), :] = dsrc_to_tmp(b)
                return tuple(new)

            return _row_loop(t_rows, SUBLANE, step, tuple(zeros(SUBLANE) for _ in range(k_taps)))

        def store_taps(ref, accs, rows):
            for k, acc in enumerate(accs):
                ref[pl.ds(k, 1), :] = jnp.sum(acc, axis=0, keepdims=True)
            if rows > len(accs):
                ref[pl.ds(len(accs), rows - len(accs)), :] = zeros(rows - len(accs))

        def fill_a(b, c):
            sl = pl.ds(b, CONV_CHUNK)
            base[pl.ds(CONV_HALO + b, CONV_CHUNK), :] = cg_ref[sl, :] * ha_ref[sl, :]
            based[sl, :] = dya_ref[sl, :] * bg_ref[sl, :]
            return c

        _row_loop(t_rows, CONV_CHUNK, fill_a)
        _fill_shifted(sh, base, sorted({fwd_slot(ka, k)[0] for k in range(ka)}), nr)
        _fill_shifted(shd, based, sorted({bwd_slot(ka, k)[0] for k in range(ka)}), nr)

        def d_bgate(b, c):
            sl = pl.ds(b, CONV_CHUNK)
            dh_ref[0, sl, :] = (dya_ref[sl, :] * conv(wa_ref, ka, sh, fwd_slot, b, CONV_CHUNK)).astype(BF16)
            return c

        _row_loop(t_rows, CONV_CHUNK, d_bgate)
        accs = wgrad_loop(ka, lambda b: based[pl.ds(b, SUBLANE), :],
                          lambda b: conv(wa_ref, ka, shd, bwd_slot, b, SUBLANE))
        store_taps(dwa_ref, accs, SUBLANE)

        def d_ch(b, c):
            sl = pl.ds(b, CONV_CHUNK)
            dua = tmp[sl, :]
            dh_ref[1, sl, :] = (dua * ha_ref[sl, :]).astype(BF16)
            dh_ref[2, sl, :] = (dua * cg_ref[sl, :]).astype(BF16)
            return c

        _row_loop(t_rows, CONV_CHUNK, d_ch)

        def fill_b(b, c):
            sl = pl.ds(b, CONV_CHUNK)
            base[pl.ds(CONV_HALO + b, CONV_CHUNK), :] = val_ref[sl, :] * jax.nn.sigmoid(gt_ref[sl, :])
            based[sl, :] = dz_ref[sl, :]
            return c

        _row_loop(t_rows, CONV_CHUNK, fill_b)
        _fill_shifted(sh, base, range(SUBLANE), nr)
        _fill_shifted(shd, based, range(SUBLANE), nr)
        accs = wgrad_loop(kb, lambda b: based[pl.ds(b, SUBLANE), :],
                          lambda b: conv(wb_ref, kb, shd, bwd_slot, b, SUBLANE))
        store_taps(dwb_ref, accs, kb_rows)

        def d_glu(b, c):
            sl = pl.ds(b, CONV_CHUNK)
            dgg = tmp[sl, :]
            sig = jax.nn.sigmoid(gt_ref[sl, :])
            dh_ref[3, sl, :] = (dgg * sig).astype(BF16)
            dh_ref[4, sl, :] = (dgg * val_ref[sl, :] * (sig * (1.0 - sig))).astype(BF16)
            return c

        _row_loop(t_rows, CONV_CHUNK, d_glu)

    def col(g):
        return pl.BlockSpec((t_rows, LANE), lambda i, g=g: (0, g * nt + i))

    tile = lambda rows: pl.BlockSpec((rows, LANE), lambda i: (0, i))
    return pl.pallas_call(
        body, grid=(nt,),
        in_specs=[col(0), col(1), col(2), col(3), col(4), tile(t_rows), tile(t_rows), tile(ka), tile(kb)],
        out_specs=[pl.BlockSpec((5, t_rows, LANE), lambda i: (0, 0, i)), tile(SUBLANE), tile(kb_rows)],
        out_shape=[jax.ShapeDtypeStruct((5, t_rows, wa_w), BF16), jax.ShapeDtypeStruct((SUBLANE, wa_w), F32),
                   jax.ShapeDtypeStruct((kb_rows, wa_w), F32)],
        scratch_shapes=[pltpu.VMEM((nr + SUBLANE, LANE), F32), pltpu.VMEM((SUBLANE, nr, LANE), F32),
                        pltpu.VMEM((nr + SUBLANE, LANE), F32), pltpu.VMEM((SUBLANE, nr, LANE), F32),
                        pltpu.VMEM((t_rows, LANE), F32)],
        name="mix_conv_bwd", compiler_params=_cparams(1))(hin, hin, hin, hin, hin, dy, dz, wa, wb)


def _bwd_in_proj(dh5, win_t, h0, dh1, g1, tm):
    t_rows, d = h0.shape
    groups, _, w = dh5.shape

    def body(dh_ref, w_ref, h0_ref, dh1_ref, g1_ref, dh0_ref, dg1_ref):
        dxn1 = None
        for g in range(groups):
            part = _dot_nn(dh_ref[g], w_ref[pl.ds(g * w, w), :])
            dxn1 = part if dxn1 is None else dxn1 + part
        h0 = h0_ref[...]
        r1 = _rstd(h0)
        h0h = h0 * r1
        _acc_rows(dg1_ref, dxn1 * h0h, pl.program_id(0) == 0)
        dh0_ref[...] = dh1_ref[...] + _rms_bwd(dxn1 * g1_ref[...], h0h, r1)

    blk = pl.BlockSpec((tm, d), lambda i: (i, 0))
    return pl.pallas_call(
        body, grid=(t_rows // tm,),
        in_specs=[pl.BlockSpec((groups, tm, w), lambda i: (0, i, 0)), _resident(win_t.shape), blk, blk, _full((1, d))],
        out_specs=[blk, _full((1, d))],
        out_shape=[jax.ShapeDtypeStruct((t_rows, d), F32), jax.ShapeDtypeStruct((1, d), F32)],
        name="bwd_in_proj", compiler_params=_cparams(1))(dh5, win_t, h0, dh1, g1)


def _reduce_grads(grads, rows, smalls):
    d = grads[0].shape[1]
    r_loc = sum(rows)
    lo = [sum(rows[:p]) for p in range(5)]
    add_chunk = 32
    (dmeta, dg1, dg2, dg3, dg4, dbb, dlg, dlb, lossv, dwa, dwb) = smalls
    half = d // 2
    kb_rows = dwb.shape[0]

    def body(g0, g1r, g2r, g3r, g4r, dmeta_ref, dg1_ref, dg2_ref, dg3_ref, dg4_ref, dbb_ref, dlg_ref, dlb_ref,
             loss_ref, dwa_ref, dwb_ref, recv_a,
             o0, o1, o2, o3, o4, ptot_ref,
             gbuf, abuf, acc, sendbuf, recv_b, pbuf, psib, chip_p,
             a_send, a_recv, b_send, b_recv, ps_send, ps_recv, pc_send, pc_recv, local_sems):
        gs = (g0, g1r, g2r, g3r, g4r)
        outs = (o0, o1, o2, o3, o4)
        x, y, c = _mesh_pos()
        me, sibling = (x, y, c), (x, y, 1 - c)
        chips = [(x, y), (1 - x, y), (x, 1 - y), (1 - x, 1 - y)]

        pbuf[...] = jnp.zeros_like(pbuf)
        pbuf[pl.ds(0, N_META), :] = dmeta_ref[...]
        for row, ref in ((16, dg1_ref), (17, dg2_ref), (18, dg3_ref), (19, dg4_ref)):
            pbuf[pl.ds(row, 1), :] = ref[...]
        pbuf[pl.ds(20, 1), pl.ds(0, half)] = dbb_ref[...]
        pbuf[pl.ds(20, 1), pl.ds(half, half)] = dlg_ref[...]
        pbuf[pl.ds(21, 1), pl.ds(0, half)] = dlb_ref[...]
        lv = loss_ref[...]
        pbuf[pl.ds(21, 1), pl.ds(half, half)] = lv[:, :half] + lv[:, half:]
        pbuf[pl.ds(24, SUBLANE), pl.ds(0, half)] = dwa_ref[...]
        pbuf[pl.ds(32, kb_rows), pl.ds(0, half)] = dwb_ref[...]
        to_sib = pltpu.make_async_remote_copy(src_ref=pbuf, dst_ref=psib, send_sem=ps_send.at[0], recv_sem=ps_recv.at[0],
                                              device_id=sibling, device_id_type=MESH)
        to_sib.start()

        def a_copies(r):
            j = _dev_index(*chips[r], 1 - c)
            return [pltpu.make_async_remote_copy(
                src_ref=gs[p].at[pl.ds(pl.multiple_of(j * rows[p], 16), rows[p]), :],
                dst_ref=recv_a.at[r, pl.ds(lo[p], rows[p]), :],
                send_sem=a_send.at[r], recv_sem=a_recv.at[r], device_id=sibling, device_id_type=MESH) for p in range(5)]

        def a_wait(r):
            whole = recv_a.at[r]
            return pltpu.make_async_remote_copy(src_ref=whole, dst_ref=whole, send_sem=a_send.at[r], recv_sem=a_recv.at[r],
                                                device_id=sibling, device_id_type=MESH)

        order = (1, 2, 3, 0)
        for r in order:
            for cp in a_copies(r):
                cp.start()

        to_sib.wait_recv()
        my_chip = 2 * x + y
        chip_p[my_chip] = pbuf[...] + psib[...]
        to_sib.wait_send()

        def p_copy(k):
            slot = chip_p.at[my_chip]
            return pltpu.make_async_remote_copy(src_ref=slot, dst_ref=slot, send_sem=pc_send.at[k], recv_sem=pc_recv.at[k],
                                                device_id=(*chips[k + 1], c), device_id_type=MESH)

        p_copies = [p_copy(k) for k in range(3)]
        for cp in p_copies:
            cp.start()

        def b_copy(k):
            return pltpu.make_async_remote_copy(src_ref=sendbuf.at[k], dst_ref=recv_b.at[k], send_sem=b_send.at[k],
                                                recv_sem=b_recv.at[k], device_id=(*chips[k + 1], c), device_id_type=MESH)

        b_copies = [b_copy(k) for k in range(3)]
        for r in order:
            j = _dev_index(*chips[r], c)
            loads = [pltpu.make_async_copy(gs[p].at[pl.ds(pl.multiple_of(j * rows[p], 16), rows[p]), :],
                                           gbuf.at[pl.ds(lo[p], rows[p]), :], local_sems.at[0]) for p in range(5)]
            for cp in loads:
                cp.start()
            a_wait(r).wait_recv()
            from_a = pltpu.make_async_copy(recv_a.at[r], abuf, local_sems.at[1])
            from_a.start()
            pltpu.make_async_copy(gbuf, gbuf, local_sems.at[0]).wait()
            from_a.wait()

            def add(b, carry, r=r):
                sl = pl.ds(b, add_chunk)
                s = gbuf[sl, :].astype(F32) + abuf[sl, :].astype(F32)
                if r == 0:
                    acc[sl, :] = s
                else:
                    sendbuf[r - 1, sl, :] = s.astype(BF16)
                return carry

            _row_loop(r_loc, add_chunk, add)
            if r != 0:
                b_copies[r - 1].start()

        for k in range(3):
            b_copies[k].wait_recv()

        def final(b, carry):
            sl = pl.ds(b, add_chunk)
            acc[sl, :] = ((acc[sl, :] + recv_b[0, sl, :].astype(F32)) + recv_b[1, sl, :].astype(F32)) + recv_b[2, sl, :].astype(F32)
            return carry

        _row_loop(r_loc, add_chunk, final)
        for p in range(5):
            outs[p][...] = acc[pl.ds(lo[p], rows[p]), :]

        for cp in p_copies:
            cp.wait_recv()
        ptot_ref[...] = ((chip_p[0] + chip_p[1]) + chip_p[2]) + chip_p[3]
        for cp in p_copies:
            cp.wait_send()
        for k in range(3):
            b_copies[k].wait_send()
        for r in order:
            a_wait(r).wait_send()

    any_spec = pl.BlockSpec(memory_space=pl.ANY)
    vmem = pl.BlockSpec(memory_space=pltpu.VMEM)
    out_shape = [jax.ShapeDtypeStruct((4, r_loc, d), BF16)]
    out_shape += [jax.ShapeDtypeStruct((r, d), F32) for r in rows]
    out_shape.append(jax.ShapeDtypeStruct((SMALL_ROWS, d), F32))
    res = pl.pallas_call(
        body, out_shape=out_shape, in_specs=[any_spec] * 5 + [vmem] * 11, out_specs=[any_spec] + [vmem] * 6,
        scratch_shapes=[pltpu.VMEM((r_loc, d), BF16), pltpu.VMEM((r_loc, d), BF16), pltpu.VMEM((r_loc, d), F32),
                        pltpu.VMEM((3, r_loc, d), BF16), pltpu.VMEM((3, r_loc, d), BF16),
                        pltpu.VMEM((SMALL_ROWS, d), F32), pltpu.VMEM((SMALL_ROWS, d), F32), pltpu.VMEM((4, SMALL_ROWS, d), F32),
                        pltpu.SemaphoreType.DMA((4,)), pltpu.SemaphoreType.DMA((4,)),
                        pltpu.SemaphoreType.DMA((3,)), pltpu.SemaphoreType.DMA((3,)),
                        pltpu.SemaphoreType.DMA((1,)), pltpu.SemaphoreType.DMA((1,)),
                        pltpu.SemaphoreType.DMA((3,)), pltpu.SemaphoreType.DMA((3,)),
                        pltpu.SemaphoreType.DMA((2,))],
        name="reduce_grads", compiler_params=_cparams())(*grads, *smalls)
    return res[1:6], res[6]


def _adamw(w, g, m, v):
    m = ADAM_B1 * m + (1.0 - ADAM_B1) * g
    v = ADAM_B2 * v + (1.0 - ADAM_B2) * jnp.square(g)
    m_hat = m / (1.0 - ADAM_B1 ** ADAM_STEP)
    v_hat = v / (1.0 - ADAM_B2 ** ADAM_STEP)
    delta = -ADAM_LR * (m_hat / (jnp.sqrt(v_hat) + ADAM_EPS) + ADAM_WD * w)
    return delta, m, v


def _adam_big(g, w, m, v, name):
    def body(g_ref, w_ref, m_ref, v_ref, d_ref, mo_ref, vo_ref):
        d_ref[...], mo_ref[...], vo_ref[...] = _adamw(w_ref[...], g_ref[...], m_ref[...], v_ref[...])

    return pl.pallas_call(body, out_shape=[jax.ShapeDtypeStruct(w.shape, F32)] * 3, name=name,
                          compiler_params=_cparams())(g, w, m, v)


def _adam_small(gs, ws, ms, vs):
    n = len(gs)

    def body(*refs):
        ins, outs = refs[:4 * n], refs[4 * n:]
        for i in range(n):
            g = ins[i][...]
            delta, m, v = _adamw(ins[n + i][...], g, ins[2 * n + i][...], ins[3 * n + i][...])
            outs[i][...] = delta
            outs[n + i][...] = m
            outs[2 * n + i][...] = v

    shapes = [jax.ShapeDtypeStruct(w.shape, F32) for w in ws]
    return pl.pallas_call(body, out_shape=shapes * 3, name="adam_small", compiler_params=_cparams())(*gs, *ws, *ms, *vs)


def kernel(x, meta_tokens, pre_mix_norm, w_in, conv_a_w, conv_b_w, conv_b_bias, ln_b_gain, ln_b_bias, w_out, post_mix_norm, pre_ffn_norm, w_gate, w_up, w_down, post_ffn_norm, loss_target, m_meta_tokens, m_pre_mix_norm, m_w_in, m_conv_a_w, m_conv_b_w, m_conv_b_bias, m_ln_b_gain, m_ln_b_bias, m_w_out, m_post_mix_norm, m_pre_ffn_norm, m_w_gate, m_w_up, m_w_down, m_post_ffn_norm, v_meta_tokens, v_pre_mix_norm, v_w_in, v_conv_a_w, v_conv_b_w, v_conv_b_bias, v_ln_b_gain, v_ln_b_bias, v_w_out, v_post_mix_norm, v_pre_ffn_norm, v_w_gate, v_w_up, v_w_down, v_post_ffn_norm):
    _, seq, d = x.shape
    ka, ca_loc = conv_a_w.shape[1:]
    kb, cb_loc = conv_b_w.shape[1:]
    wa_w = ca_loc * N_DEV
    assert cb_loc == ca_loc and wa_w % LANE == 0 and w_in.shape[2] * N_DEV == 5 * wa_w
    pad = (-(N_META + seq)) % ROW_ALIGN
    x0 = pad + N_META
    t_rows = x0 + seq
    assert t_rows % (N_ROW_BLOCKS * 16) == 0 and t_rows % CONV_CHUNK == 0 and d % LANE == 0
    tm = t_rows // N_ROW_BLOCKS
    me = _dev_index(*_mesh_pos())

    def as_rows(w_in_like, w_out_like, w_gate_like, w_up_like, w_down_like):
        return (w_in_like[0].T, w_out_like[0], w_gate_like[0].T, w_up_like[0].T, w_down_like[0])

    w_loc = as_rows(w_in, w_out, w_gate, w_up, w_down)
    rows = [w.shape[0] for w in w_loc]
    wl = _pack_weights(w_loc)
    sm = jnp.zeros((SM_ROWS, LANE), F32)
    sm = sm.at[0:N_META, :].set(meta_tokens)
    sm = sm.at[16:16 + ka, 0:ca_loc].set(conv_a_w[0])
    sm = sm.at[24:24 + kb, 0:cb_loc].set(conv_b_w[0])
    win_t, wout, wg_t, wu_t, wd, sm_all = _all_gather(wl, sm, rows)
    meta_full = jnp.transpose(sm_all[:, 0:N_META, :], (1, 0, 2)).reshape(N_META, d)
    wa = jnp.transpose(sm_all[:, 16:16 + ka, 0:ca_loc], (1, 0, 2)).reshape(ka, wa_w)
    wb = jnp.transpose(sm_all[:, 24:24 + kb, 0:cb_loc], (1, 0, 2)).reshape(kb, wa_w)

    h0, tgt = _prep(x[0], loss_target[0], meta_full, t_rows, x0)
    xn1, hin = _in_proj(h0, pre_mix_norm, win_t, tm)
    ya, z = _mix_conv_fwd(hin, wa, wb, conv_b_bias, wa_w)
    y = _mix_ln_fwd(ya, z, ln_b_gain, ln_b_bias, tm)
    mix, h1, xn2 = _out_proj(y, wout, h0, post_mix_norm, pre_ffn_norm, tm)
    a, u, s = _gate_up(xn2, wg_t, wu_t, tm)
    dh2, dff, dg4, lossv = _down_loss(s, wd, h1, tgt, post_ffn_norm, tm, x0)

    gwd = _wgrad(s, dff, "wgrad_down")
    da, du = _bwd_down(dff, wd, a, u, tm)
    gwg = _wgrad(da, xn2, "wgrad_gate")
    gwu = _wgrad(du, xn2, "wgrad_up")
    dh1, dg3 = _bwd_ffn_in(da, du, wg_t, wu_t, h1, dh2, pre_ffn_norm, tm)
    dmix, dy, dg2 = _bwd_out_proj(dh1, mix, wout, post_mix_norm, tm)
    gwo = _wgrad(y, dmix, "wgrad_out")
    dz, dlg, dlb, dbb = _mix_ln_bwd(z, dy, ln_b_gain, ln_b_bias, tm)
    dh5, dwa, dwb = _mix_conv_bwd(hin, dy, dz, wa, wb, wa_w)
    gwi = _wgrad(dh5, xn1, "wgrad_in")
    dh0, dg1 = _bwd_in_proj(dh5, win_t, h0, dh1, pre_mix_norm, tm)
    grad_x = dh0[x0:][None]
    dmeta = dh0[x0 - N_META:x0]

    big, ptot = _reduce_grads((gwi, gwo, gwg, gwu, gwd), rows, (dmeta, dg1, dg2, dg3, dg4, dbb, dlg, dlb, lossv, dwa, dwb))
    half = d // 2
    loss = (0.5 / d) * jnp.sum(ptot[21, half:])
    g_meta = lax.dynamic_slice(ptot, (0, me * (d // N_DEV)), (N_META, d // N_DEV))
    g_small = [g_meta, ptot[16:17], lax.dynamic_slice(ptot, (24, me * ca_loc), (ka, ca_loc))[None],
               lax.dynamic_slice(ptot, (32, me * cb_loc), (kb, cb_loc))[None],
               ptot[20:21, :half], ptot[20:21, half:], ptot[21:22, :half], ptot[17:18], ptot[18:19], ptot[19:20]]
    w_small = [meta_tokens, pre_mix_norm, conv_a_w, conv_b_w, conv_b_bias, ln_b_gain, ln_b_bias, post_mix_norm,
               pre_ffn_norm, post_ffn_norm]
    m_small = [m_meta_tokens, m_pre_mix_norm, m_conv_a_w, m_conv_b_w, m_conv_b_bias, m_ln_b_gain, m_ln_b_bias,
               m_post_mix_norm, m_pre_ffn_norm, m_post_ffn_norm]
    v_small = [v_meta_tokens, v_pre_mix_norm, v_conv_a_w, v_conv_b_w, v_conv_b_bias, v_ln_b_gain, v_ln_b_bias,
               v_post_mix_norm, v_pre_ffn_norm, v_post_ffn_norm]
    small = _adam_small(g_small, w_small, m_small, v_small)
    n_small = len(w_small)
    d_small, nm_small, nv_small = small[:n_small], small[n_small:2 * n_small], small[2 * n_small:]

    m_loc = as_rows(m_w_in, m_w_out, m_w_gate, m_w_up, m_w_down)
    v_loc = as_rows(v_w_in, v_w_out, v_w_gate, v_w_up, v_w_down)
    bigs = {}
    for p, (name, tr) in enumerate((("w_in", True), ("w_out", False), ("w_gate", True), ("w_up", True), ("w_down", False))):
        res = [big[p]] + list(_adam_big(big[p], w_loc[p], m_loc[p], v_loc[p], "adam_" + name))
        bigs[name] = [(o.T if tr else o)[None] for o in res]

    def ordered(pick_small, pick_big):
        sm_it = iter(range(n_small))
        out = []
        for name in ("s", "s", "w_in", "s", "s", "s", "s", "s", "w_out", "s", "s", "w_gate", "w_up", "w_down", "s"):
            out.append(pick_small(next(sm_it)) if name == "s" else pick_big(name))
        return out

    grads = ordered(lambda i: g_small[i], lambda n: bigs[n][0])
    deltas = ordered(lambda i: d_small[i], lambda n: bigs[n][1])
    new_m = ordered(lambda i: nm_small[i], lambda n: bigs[n][2])
    new_v = ordered(lambda i: nv_small[i], lambda n: bigs[n][3])
    return (loss, grad_x, *grads, *deltas, *new_m, *new_v)
```

```python
import jax
import jax.numpy as jnp
from jax import lax
from jax.experimental import pallas as pl
from jax.experimental.pallas import tpu as pltpu

F32 = jnp.float32
BF16 = jnp.bfloat16
MESH = pl.DeviceIdType.MESH

N_META = 16
N_DEV = 8
RMS_EPS = 1e-6
LN_EPS = 1e-5
ADAM_LR = 0.001
ADAM_B1 = 0.9
ADAM_B2 = 0.999
ADAM_EPS = 1e-08
ADAM_WD = 0.01
ADAM_STEP = 10

LANE = 128
SUBLANE = 8
BF16_ROWS = 16
ROW_ALIGN = 128
N_ROW_BLOCKS = 4
CONV_HALO = 32
CONV_CHUNK = 64
N_CHUNK = 512
WGRAD_TILE = 256
ADD_CHUNK = 32
V7X_VMEM_BYTES = 64 * 1024 * 1024
VMEM_LIMIT = V7X_VMEM_BYTES - 6 * 1024 * 1024
SMALL_ROWS = 64
SM_ROWS = 56
N_BIG = 5

ANY = pl.BlockSpec(memory_space=pl.ANY)
VMEM = pl.BlockSpec(memory_space=pltpu.VMEM)


def _cparams(n_grid_axes=0):
    sem = ("arbitrary",) * n_grid_axes if n_grid_axes else None
    return pltpu.CompilerParams(dimension_semantics=sem, vmem_limit_bytes=VMEM_LIMIT)


def _mesh_pos():
    return lax.axis_index("x"), lax.axis_index("y"), lax.axis_index("c")


def _dev_index(px, py, pc):
    return 4 * px + 2 * py + pc


def _other_chips(x, y):
    return [(1 - x, y), (x, 1 - y), (1 - x, 1 - y)]


def _full(shape):
    return pl.BlockSpec(shape, lambda *_: (0,) * len(shape))


def _resident(shape):
    return pl.BlockSpec(shape, lambda *_: (0,) * len(shape), pipeline_mode=pl.Buffered(1))


def _dot_nt(a, w):
    return lax.dot_general(a, w, (((1,), (1,)), ((), ())), preferred_element_type=F32)


def _dot_nn(a, w):
    return jnp.dot(a, w, preferred_element_type=F32)


def _chunks(n, c):
    out, o = [], 0
    while o < n:
        out.append((o, min(c, n - o)))
        o += c
    return out


def _rstd(h):
    return lax.rsqrt(jnp.mean(h * h, axis=-1, keepdims=True) + RMS_EPS)


def _rms_bwd(dyh, yh, r):
    return r * (dyh - yh * jnp.mean(dyh * yh, axis=-1, keepdims=True))


def _silu_grad(a, sig):
    return sig * (1.0 + a * (1.0 - sig))


def _acc_rows(ref, val, first):
    s = jnp.sum(val, axis=0, keepdims=True)

    @pl.when(first)
    def _():
        ref[...] = s

    @pl.when(jnp.logical_not(first))
    def _():
        ref[...] += s


def _row_loop(t_rows, chunk, fn, carry=None):
    def step(i, c):
        return fn(pl.multiple_of(i * chunk, chunk), c)

    return lax.fori_loop(0, t_rows // chunk, step, carry)


def _remote(src, dst, send_sem, recv_sem, to):
    return pltpu.make_async_remote_copy(src_ref=src, dst_ref=dst, send_sem=send_sem, recv_sem=recv_sem,
                                        device_id=to, device_id_type=MESH)


class _Comm:
    def __init__(self, inputs, out_shapes, aliases, scratch, start, finish):
        self.inputs, self.out_shapes, self.aliases, self.scratch = list(inputs), list(out_shapes), dict(aliases), list(scratch)
        self.start, self.finish = start, finish


def _merge_comms(comms):
    inputs, out_shapes, aliases, scratch, spans = [], [], {}, [], []
    for cm in comms:
        spans.append((len(inputs), len(out_shapes), len(scratch), cm))
        aliases.update({len(inputs) + k: len(out_shapes) + v for k, v in cm.aliases.items()})
        inputs += cm.inputs
        out_shapes += cm.out_shapes
        scratch += cm.scratch

    def run(which):
        def fn(ins, outs, scr):
            for i0, o0, s0, cm in spans:
                getattr(cm, which)(ins[i0:i0 + len(cm.inputs)], outs[o0:o0 + len(cm.out_shapes)], scr[s0:s0 + len(cm.scratch)])
        return fn

    return _Comm(inputs, out_shapes, aliases, scratch, run("start"), run("finish"))


def _host_call(body, *, grid, in_specs, out_specs, out_shape, args, name, scratch_shapes=(), comm=None):
    if comm is None:
        res = pl.pallas_call(body, grid=grid, in_specs=list(in_specs), out_specs=list(out_specs), out_shape=list(out_shape),
                             scratch_shapes=list(scratch_shapes), name=name, compiler_params=_cparams(len(grid)))(*args)
        return list(res), []
    n_in, n_out, n_scr = len(args), len(out_shape), len(scratch_shapes)
    c_in, c_out = len(comm.inputs), len(comm.out_shapes)

    def hosted(*refs):
        ins, c_ins = refs[:n_in], refs[n_in:n_in + c_in]
        o0 = n_in + c_in
        outs, c_outs = refs[o0:o0 + n_out], refs[o0 + n_out:o0 + n_out + c_out]
        s0 = o0 + n_out + c_out
        scr, c_scr = refs[s0:s0 + n_scr], refs[s0 + n_scr:]
        if not grid:
            comm.start(c_ins, c_outs, c_scr)
            body(*ins, *outs, *scr)
            comm.finish(c_ins, c_outs, c_scr)
            return
        first = last = None
        for a, n in enumerate(grid):
            f, l = pl.program_id(a) == 0, pl.program_id(a) == n - 1
            first = f if first is None else jnp.logical_and(first, f)
            last = l if last is None else jnp.logical_and(last, l)

        @pl.when(first)
        def _():
            comm.start(c_ins, c_outs, c_scr)

        body(*ins, *outs, *scr)

        @pl.when(last)
        def _():
            comm.finish(c_ins, c_outs, c_scr)

    res = pl.pallas_call(
        hosted, grid=grid, in_specs=list(in_specs) + [ANY] * c_in, out_specs=list(out_specs) + [ANY] * c_out,
        out_shape=list(out_shape) + comm.out_shapes, scratch_shapes=list(scratch_shapes) + comm.scratch,
        input_output_aliases={n_in + k: n_out + v for k, v in comm.aliases.items()},
        name=name, compiler_params=_cparams(len(grid)))(*args, *comm.inputs)
    return list(res[:n_out]), list(res[n_out:])


def _gather_jobs(x, y, c, jobs, rows, lo, src_ref, dests, send_sems, recv_sems):
    sibling = (x, y, 1 - c)
    chips = _other_chips(x, y)
    first, forward = [], []
    for n, (p, r0, nr) in enumerate(jobs):
        def rows_of(px, py, pc, p=p, r0=r0, nr=nr):
            return dests[p].at[pl.ds(pl.multiple_of(_dev_index(px, py, pc) * rows[p] + r0, BF16_ROWS), nr), :]

        src = src_ref.at[pl.ds(lo[p] + r0, nr), :]
        sem = lambda k, n=n: (send_sems.at[7 * n + k], recv_sems.at[7 * n + k])
        first.append([_remote(src, rows_of(x, y, c), *sem(0), sibling)]
                     + [_remote(src, rows_of(x, y, c), *sem(1 + j), (*chip, c)) for j, chip in enumerate(chips)])
        forward.append([_remote(rows_of(*chip, c), rows_of(*chip, c), *sem(4 + j), sibling) for j, chip in enumerate(chips)])
    return first, forward


def _gather_start(first):
    for cps in first:
        for cp in cps:
            cp.start()


def _gather_finish(first, forward):
    for cps, fws in zip(first, forward):
        for j in range(3):
            cps[1 + j].wait_recv()
            fws[j].start()
    for cps, fws in zip(first, forward):
        cps[0].wait_recv()
        for fw in fws:
            fw.wait_recv()
        for cp in cps + fws:
            cp.wait_send()


def _gather_comm(wl, dests, jobs, rows):
    lo = [sum(rows[:p]) for p in range(N_BIG)]
    ps = sorted(dests)
    slot = {p: i for i, p in enumerate(ps)}

    def descs(ins, outs, scr):
        x, y, c = _mesh_pos()
        return _gather_jobs(x, y, c, jobs, rows, lo, ins[0], {p: outs[slot[p]] for p in ps}, scr[0], scr[1])

    def start(ins, outs, scr):
        _gather_start(descs(ins, outs, scr)[0])

    def finish(ins, outs, scr):
        _gather_finish(*descs(ins, outs, scr))

    n_sems = 7 * len(jobs)
    return _Comm([wl] + [dests[p] for p in ps], [jax.ShapeDtypeStruct(dests[p].shape, BF16) for p in ps],
                 {1 + i: i for i in range(len(ps))},
                 [pltpu.SemaphoreType.DMA((n_sems,)), pltpu.SemaphoreType.DMA((n_sems,))], start, finish)


def _pair_comm(g, r):
    d = g.shape[1]

    def descs(ins, outs, scr):
        x, y, c = _mesh_pos()
        chips = [(x, y)] + _other_chips(x, y)
        return [_remote(ins[0].at[pl.ds(pl.multiple_of(_dev_index(*chip, 1 - c) * r, BF16_ROWS), r), :], outs[0].at[k],
                        scr[0].at[k], scr[1].at[k], (x, y, 1 - c)) for k, chip in enumerate(chips)]

    def start(ins, outs, scr):
        for cp in descs(ins, outs, scr):
            cp.start()

    def finish(ins, outs, scr):
        cps = descs(ins, outs, scr)
        for cp in cps:
            cp.wait_recv()
        for cp in cps:
            cp.wait_send()

    return _Comm([g], [jax.ShapeDtypeStruct((4, r, d), BF16)], {},
                 [pltpu.SemaphoreType.DMA((4,)), pltpu.SemaphoreType.DMA((4,))], start, finish)


def _chip_comm(g, pair, r):
    d = g.shape[1]

    def descs(outs, scr):
        x, y, c = _mesh_pos()
        return [_remote(scr[2].at[k], outs[0].at[k], scr[3].at[k], scr[4].at[k], (*chip, c))
                for k, chip in enumerate(_other_chips(x, y))]

    def start(ins, outs, scr):
        x, y, c = _mesh_pos()
        gbuf, abuf, sendbuf, local = scr[0], scr[1], scr[2], scr[5]
        cps = descs(outs, scr)
        for k, chip in enumerate(_other_chips(x, y)):
            j = _dev_index(*chip, c)
            loads = [pltpu.make_async_copy(ins[0].at[pl.ds(pl.multiple_of(j * r, BF16_ROWS), r), :], gbuf, local.at[0]),
                     pltpu.make_async_copy(ins[1].at[1 + k], abuf, local.at[1])]
            for cp in loads:
                cp.start()
            for cp in loads:
                cp.wait()

            def add(b, carry, k=k):
                sl = pl.ds(b, ADD_CHUNK)
                sendbuf[k, sl, :] = (gbuf[sl, :].astype(F32) + abuf[sl, :].astype(F32)).astype(BF16)
                return carry

            _row_loop(r, ADD_CHUNK, add)
            cps[k].start()

    def finish(ins, outs, scr):
        cps = descs(outs, scr)
        for cp in cps:
            cp.wait_recv()
        for cp in cps:
            cp.wait_send()

    return _Comm([g, pair], [jax.ShapeDtypeStruct((3, r, d), BF16)], {},
                 [pltpu.VMEM((r, d), BF16), pltpu.VMEM((r, d), BF16), pltpu.VMEM((3, r, d), BF16),
                  pltpu.SemaphoreType.DMA((3,)), pltpu.SemaphoreType.DMA((3,)), pltpu.SemaphoreType.DMA((2,))], start, finish)


def _gather_first(shards, sm, jobs):
    d = shards[0].shape[1]
    rows = [w.shape[0] for w in shards]
    lo = [sum(rows[:p]) for p in range(N_BIG)]
    n_sems = 7 * len(jobs)

    def body(s0, s1, s2, s3, s4, sm_ref, wl_ref, o0, o1, o2, o3, o4, sa_ref, wl_v, send_sems, recv_sems, ssend, srecv, local_sems):
        dests = (o0, o1, o2, o3, o4)
        x, y, c = _mesh_pos()
        me = (x, y, c)
        jme = _dev_index(*me)
        for p, ref in enumerate((s0, s1, s2, s3, s4)):
            wl_v[pl.ds(lo[p], rows[p]), :] = ref[...].astype(BF16)
        first, forward = _gather_jobs(x, y, c, jobs, rows, lo, wl_v, dict(enumerate(dests)), send_sems, recv_sems)
        _gather_start(first)
        peers = [(x, y, 1 - c)] + [(*chip, pc) for pc in (c, 1 - c) for chip in _other_chips(x, y)]
        smalls = [_remote(sm_ref, sa_ref.at[jme], ssend.at[k], srecv.at[k], to) for k, to in enumerate(peers)]
        for cp in smalls:
            cp.start()
        mine = [pltpu.make_async_copy(wl_v.at[pl.ds(lo[p], rows[p]), :],
                                      dests[p].at[pl.ds(pl.multiple_of(jme * rows[p], BF16_ROWS), rows[p]), :], local_sems.at[p])
                for p in range(N_BIG)]
        mine.append(pltpu.make_async_copy(wl_v, wl_ref, local_sems.at[N_BIG]))
        mine.append(pltpu.make_async_copy(sm_ref, sa_ref.at[jme], local_sems.at[N_BIG + 1]))
        for cp in mine:
            cp.start()
        _gather_finish(first, forward)
        for cp in smalls:
            cp.wait_recv()
        for cp in smalls:
            cp.wait_send()
        for cp in mine:
            cp.wait()

    out_shape = [jax.ShapeDtypeStruct((sum(rows), d), BF16)]
    out_shape += [jax.ShapeDtypeStruct((N_DEV * r, d), BF16) for r in rows]
    out_shape.append(jax.ShapeDtypeStruct((N_DEV,) + sm.shape, F32))
    res = pl.pallas_call(
        body, out_shape=out_shape, in_specs=[VMEM] * 6, out_specs=[ANY] * 7,
        scratch_shapes=[pltpu.VMEM((sum(rows), d), BF16), pltpu.SemaphoreType.DMA((n_sems,)), pltpu.SemaphoreType.DMA((n_sems,)),
                        pltpu.SemaphoreType.DMA((7,)), pltpu.SemaphoreType.DMA((7,)), pltpu.SemaphoreType.DMA((N_BIG + 2,))],
        name="gather_first", compiler_params=_cparams())(*shards, sm)
    return res[0], list(res[1:1 + N_BIG]), res[-1]


def _prep(x2, tgt2, meta_full, t_rows, x0):
    s, d = x2.shape
    assert x0 == ROW_ALIGN and s % ROW_ALIGN == 0

    def body(x_ref, tgt_ref, meta_ref, h0_ref, tp_ref):
        i = pl.program_id(0)

        @pl.when(i == 0)
        def _():
            h0_ref[...] = jnp.zeros_like(h0_ref)
            h0_ref[pl.ds(x0 - N_META, N_META), :] = meta_ref[...]
            tp_ref[...] = jnp.zeros_like(tp_ref)

        @pl.when(i > 0)
        def _():
            h0_ref[...] = x_ref[...]
            tp_ref[...] = tgt_ref[...]

    src = pl.BlockSpec((ROW_ALIGN, d), lambda i: (jnp.maximum(i - 1, 0), 0))
    dst = pl.BlockSpec((ROW_ALIGN, d), lambda i: (i, 0))
    res, _ = _host_call(body, grid=(t_rows // ROW_ALIGN,), in_specs=[src, src, _full((N_META, d))], out_specs=[dst, dst],
                        out_shape=[jax.ShapeDtypeStruct((t_rows, d), F32)] * 2, args=(x2, tgt2, meta_full), name="prep")
    return res


def _in_proj(h0, g1, win_t, tm, comm):
    t_rows, d = h0.shape
    e = win_t.shape[0]

    def body(h_ref, g_ref, w_ref, xn_ref, hin_ref):
        h = h_ref[...]
        xn = ((h * _rstd(h)) * g_ref[...]).astype(BF16)
        xn_ref[...] = xn
        for o, n in _chunks(e, N_CHUNK):
            hin_ref[:, pl.ds(o, n)] = _dot_nt(xn, w_ref[pl.ds(o, n), :])

    return _host_call(
        body, grid=(t_rows // tm,),
        in_specs=[pl.BlockSpec((tm, d), lambda i: (i, 0)), _full((1, d)), _resident((e, d))],
        out_specs=[pl.BlockSpec((tm, d), lambda i: (i, 0)), pl.BlockSpec((tm, e), lambda i: (i, 0))],
        out_shape=[jax.ShapeDtypeStruct((t_rows, d), BF16), jax.ShapeDtypeStruct((t_rows, e), F32)],
        args=(h0, g1, win_t), name="in_proj", comm=comm)


def _tap_slot(off):
    return off % SUBLANE, (off // SUBLANE) * SUBLANE


def _fill_shifted(sh_ref, base_ref, residues, n_rows):
    for r in residues:
        sh_ref[r] = base_ref[pl.ds(r, n_rows), :]


def _mix_conv_fwd(hin, wa, wb, bb, wa_w, comm):
    t_rows = hin.shape[0]
    nt = wa_w // LANE
    ka, kb = wa.shape[0], wb.shape[0]
    nr = CONV_HALO + t_rows

    def body(bg_ref, cg_ref, ha_ref, val_ref, gt_ref, wa_ref, wb_ref, bb_ref, ya_ref, z_ref, base, sh):
        base[pl.ds(0, CONV_HALO), :] = jnp.zeros((CONV_HALO, LANE), F32)
        base[pl.ds(nr, SUBLANE), :] = jnp.zeros((SUBLANE, LANE), F32)

        def conv(w_ref, k_taps, b, n):
            acc = None
            for k in range(k_taps):
                r, q = _tap_slot(CONV_HALO - (k_taps - 1) + k)
                term = w_ref[pl.ds(k, 1), :] * sh[r, pl.ds(b + q, n), :]
                acc = term if acc is None else acc + term
            return acc

        def fill_a(b, c):
            base[pl.ds(CONV_HALO + b, CONV_CHUNK), :] = cg_ref[pl.ds(b, CONV_CHUNK), :] * ha_ref[pl.ds(b, CONV_CHUNK), :]
            return c

        _row_loop(t_rows, CONV_CHUNK, fill_a)
        _fill_shifted(sh, base, sorted({_tap_slot(CONV_HALO - (ka - 1) + k)[0] for k in range(ka)}), nr)

        def out_a(b, c):
            ya_ref[pl.ds(b, CONV_CHUNK), :] = (bg_ref[pl.ds(b, CONV_CHUNK), :] * conv(wa_ref, ka, b, CONV_CHUNK)).astype(BF16)
            return c

        _row_loop(t_rows, CONV_CHUNK, out_a)

        def fill_b(b, c):
            base[pl.ds(CONV_HALO + b, CONV_CHUNK), :] = (val_ref[pl.ds(b, CONV_CHUNK), :]
                                                          * jax.nn.sigmoid(gt_ref[pl.ds(b, CONV_CHUNK), :]))
            return c

        _row_loop(t_rows, CONV_CHUNK, fill_b)
        _fill_shifted(sh, base, range(SUBLANE), nr)

        def out_b(b, c):
            z_ref[pl.ds(b, CONV_CHUNK), :] = conv(wb_ref, kb, b, CONV_CHUNK) + bb_ref[...]
            return c

        _row_loop(t_rows, CONV_CHUNK, out_b)

    def col(g):
        return pl.BlockSpec((t_rows, LANE), lambda i, g=g: (0, g * nt + i))

    tile = lambda rows: pl.BlockSpec((rows, LANE), lambda i: (0, i))
    return _host_call(
        body, grid=(nt,),
        in_specs=[col(0), col(1), col(2), col(3), col(4), tile(ka), tile(kb), tile(1)],
        out_specs=[tile(t_rows), tile(t_rows)],
        out_shape=[jax.ShapeDtypeStruct((t_rows, wa_w), BF16), jax.ShapeDtypeStruct((t_rows, wa_w), F32)],
        scratch_shapes=[pltpu.VMEM((nr + SUBLANE, LANE), F32), pltpu.VMEM((SUBLANE, nr, LANE), F32)],
        args=(hin, hin, hin, hin, hin, wa, wb, bb), name="mix_conv_fwd", comm=comm)


def _ln_parts(z, lg, lb):
    mu = jnp.mean(z, axis=-1, keepdims=True)
    zc = z - mu
    rstd = lax.rsqrt(jnp.mean(zc * zc, axis=-1, keepdims=True) + LN_EPS)
    zh = zc * rstd
    return zh, rstd, zh * lg + lb


def _mix_ln_fwd(ya, z, lg, lb, tm):
    t_rows, w = z.shape

    def body(ya_ref, z_ref, lg_ref, lb_ref, y_ref):
        _, _, ln = _ln_parts(z_ref[...], lg_ref[...], lb_ref[...])
        y_ref[:, pl.ds(0, w)] = ya_ref[...]
        y_ref[:, pl.ds(w, w)] = (ln * jax.nn.sigmoid(ln)).astype(BF16)

    blk = pl.BlockSpec((tm, w), lambda i: (i, 0))
    res, _ = _host_call(body, grid=(t_rows // tm,), in_specs=[blk, blk, _full((1, w)), _full((1, w))],
                        out_specs=[pl.BlockSpec((tm, 2 * w), lambda i: (i, 0))],
                        out_shape=[jax.ShapeDtypeStruct((t_rows, 2 * w), BF16)], args=(ya, z, lg, lb), name="mix_ln_fwd")
    return res[0]


def _out_proj(y, w_out, h0, g2, g3, tm, comm):
    t_rows, d = h0.shape

    def body(y_ref, w_ref, h0_ref, g2_ref, g3_ref, mix_ref, h1_ref, xn2_ref):
        mix = _dot_nn(y_ref[...], w_ref[...])
        mix_ref[...] = mix
        h1 = h0_ref[...] + (mix * _rstd(mix)) * g2_ref[...]
        h1_ref[...] = h1
        xn2_ref[...] = ((h1 * _rstd(h1)) * g3_ref[...]).astype(BF16)

    blk = pl.BlockSpec((tm, d), lambda i: (i, 0))
    return _host_call(
        body, grid=(t_rows // tm,), in_specs=[blk, _resident(w_out.shape), blk, _full((1, d)), _full((1, d))],
        out_specs=[blk, blk, blk],
        out_shape=[jax.ShapeDtypeStruct((t_rows, d), F32), jax.ShapeDtypeStruct((t_rows, d), F32),
                   jax.ShapeDtypeStruct((t_rows, d), BF16)],
        args=(y, w_out, h0, g2, g3), name="out_proj", comm=comm)


def _gate_up(xn2, wg_t, wu_t, tm, comm):
    t_rows, d = xn2.shape
    f = wg_t.shape[0]

    def body(x_ref, wg_ref, wu_ref, a_ref, u_ref, s_ref):
        xn = x_ref[...]
        for o, n in _chunks(f, N_CHUNK):
            a = _dot_nt(xn, wg_ref[pl.ds(o, n), :])
            u = _dot_nt(xn, wu_ref[pl.ds(o, n), :])
            a_ref[:, pl.ds(o, n)] = a.astype(BF16)
            u_ref[:, pl.ds(o, n)] = u.astype(BF16)
            s_ref[:, pl.ds(o, n)] = ((a * jax.nn.sigmoid(a)) * u).astype(BF16)

    blk = pl.BlockSpec((tm, f), lambda i: (i, 0))
    return _host_call(
        body, grid=(t_rows // tm,),
        in_specs=[pl.BlockSpec((tm, d), lambda i: (i, 0)), _resident((f, d)), _resident((f, d))],
        out_specs=[blk, blk, blk], out_shape=[jax.ShapeDtypeStruct((t_rows, f), BF16)] * 3,
        args=(xn2, wg_t, wu_t), name="gate_up", comm=comm)


def _down_loss(s, wd, h1, tgt, g4, tm, x0):
    t_rows, d = h1.shape
    f = wd.shape[0]

    def body(s_ref, w_ref, h1_ref, tgt_ref, g4_ref, dh2_ref, dff_ref, dg4_ref, loss_ref):
        i = pl.program_id(0)
        ff = _dot_nn(s_ref[...], w_ref[...])
        r4 = _rstd(ff)
        fh = ff * r4
        g4 = g4_ref[...]
        h2 = h1_ref[...] + fh * g4
        row = i * tm + lax.broadcasted_iota(jnp.int32, (tm, 1), 0)
        diff = jnp.where(row >= x0, h2 - tgt_ref[...], 0.0)
        dh2 = diff / d
        dh2_ref[...] = dh2
        dff_ref[...] = _rms_bwd(dh2 * g4, fh, r4).astype(BF16)
        _acc_rows(dg4_ref, dh2 * fh, i == 0)
        _acc_rows(loss_ref, diff * diff, i == 0)

    blk = pl.BlockSpec((tm, d), lambda i: (i, 0))
    res, _ = _host_call(
        body, grid=(t_rows // tm,),
        in_specs=[pl.BlockSpec((tm, f), lambda i: (i, 0)), _resident((f, d)), blk, blk, _full((1, d))],
        out_specs=[blk, blk, _full((1, d)), _full((1, d))],
        out_shape=[jax.ShapeDtypeStruct((t_rows, d), F32), jax.ShapeDtypeStruct((t_rows, d), BF16),
                   jax.ShapeDtypeStruct((1, d), F32), jax.ShapeDtypeStruct((1, d), F32)],
        args=(s, wd, h1, tgt, g4), name="down_loss")
    return res


def _bwd_down(dff, wd, a, u, tm, comm):
    t_rows, d = dff.shape
    f = wd.shape[0]

    def body(dff_ref, w_ref, a_ref, u_ref, da_ref, du_ref):
        dff_v = dff_ref[...]
        for o, n in _chunks(f, N_CHUNK):
            ds = _dot_nt(dff_v, w_ref[pl.ds(o, n), :])
            av = a_ref[:, pl.ds(o, n)].astype(F32)
            uv = u_ref[:, pl.ds(o, n)].astype(F32)
            sig = jax.nn.sigmoid(av)
            da_ref[:, pl.ds(o, n)] = (ds * uv * _silu_grad(av, sig)).astype(BF16)
            du_ref[:, pl.ds(o, n)] = (ds * (av * sig)).astype(BF16)

    blk = pl.BlockSpec((tm, f), lambda i: (i, 0))
    return _host_call(
        body, grid=(t_rows // tm,),
        in_specs=[pl.BlockSpec((tm, d), lambda i: (i, 0)), _resident((f, d)), blk, blk],
        out_specs=[blk, blk], out_shape=[jax.ShapeDtypeStruct((t_rows, f), BF16)] * 2,
        args=(dff, wd, a, u), name="bwd_down", comm=comm)


def _wgrad(a, b, name, comm=None):
    d = b.shape[1]
    t_rows = b.shape[0]
    stacked = a.ndim == 3
    n = a.shape[-1]
    groups = a.shape[0] if stacked else 1
    tiles = n // WGRAD_TILE

    def body(a_ref, b_ref, o_ref):
        o_ref[...] = lax.dot_general(a_ref[...], b_ref[...], (((0,), (0,)), ((), ())),
                                     preferred_element_type=F32).astype(BF16)

    if stacked:
        a_spec = pl.BlockSpec((None, t_rows, WGRAD_TILE), lambda g, i: (g, 0, i))
    else:
        a_spec = pl.BlockSpec((t_rows, WGRAD_TILE), lambda g, i: (0, i))
    res, extra = _host_call(
        body, grid=(groups, tiles), in_specs=[a_spec, pl.BlockSpec((t_rows, d), lambda g, i: (0, 0))],
        out_specs=[pl.BlockSpec((WGRAD_TILE, d), lambda g, i: (g * tiles + i, 0))],
        out_shape=[jax.ShapeDtypeStruct((groups * n, d), BF16)], args=(a, b), name=name, comm=comm)
    return res[0], extra


def _bwd_ffn_in(da, du, wg_t, wu_t, h1, dh2, g3, tm, comm):
    t_rows, d = h1.shape
    f = wg_t.shape[0]

    def body(da_ref, du_ref, wg_ref, wu_ref, h1_ref, dh2_ref, g3_ref, dh1_ref, dg3_ref):
        dxn2 = _dot_nn(da_ref[...], wg_ref[...]) + _dot_nn(du_ref[...], wu_ref[...])
        h1 = h1_ref[...]
        r3 = _rstd(h1)
        h1h = h1 * r3
        _acc_rows(dg3_ref, dxn2 * h1h, pl.program_id(0) == 0)
        dh1_ref[...] = dh2_ref[...] + _rms_bwd(dxn2 * g3_ref[...], h1h, r3)

    blk = pl.BlockSpec((tm, d), lambda i: (i, 0))
    blkf = pl.BlockSpec((tm, f), lambda i: (i, 0))
    return _host_call(
        body, grid=(t_rows // tm,),
        in_specs=[blkf, blkf, _resident((f, d)), _resident((f, d)), blk, blk, _full((1, d))],
        out_specs=[blk, _full((1, d))],
        out_shape=[jax.ShapeDtypeStruct((t_rows, d), F32), jax.ShapeDtypeStruct((1, d), F32)],
        args=(da, du, wg_t, wu_t, h1, dh2, g3), name="bwd_ffn_in", comm=comm)


def _bwd_out_proj(dh1, mix, w_out, g2, tm):
    t_rows, d = dh1.shape

    def body(dh1_ref, mix_ref, w_ref, g2_ref, dmix_ref, dy_ref, dg2_ref):
        mix = mix_ref[...]
        r2 = _rstd(mix)
        mh = mix * r2
        dh1 = dh1_ref[...]
        _acc_rows(dg2_ref, dh1 * mh, pl.program_id(0) == 0)
        dmix = _rms_bwd(dh1 * g2_ref[...], mh, r2).astype(BF16)
        dmix_ref[...] = dmix
        dy_ref[...] = _dot_nt(dmix, w_ref[...])

    blk = pl.BlockSpec((tm, d), lambda i: (i, 0))
    res, _ = _host_call(
        body, grid=(t_rows // tm,), in_specs=[blk, blk, _resident(w_out.shape), _full((1, d))],
        out_specs=[blk, blk, _full((1, d))],
        out_shape=[jax.ShapeDtypeStruct((t_rows, d), BF16), jax.ShapeDtypeStruct((t_rows, d), F32),
                   jax.ShapeDtypeStruct((1, d), F32)],
        args=(dh1, mix, w_out, g2), name="bwd_out_proj")
    return res


def _mix_ln_bwd(z, dy, lg, lb, tm):
    t_rows, w = z.shape

    def body(z_ref, dyb_ref, lg_ref, lb_ref, dz_ref, dlg_ref, dlb_ref, dbb_ref):
        first = pl.program_id(0) == 0
        lg = lg_ref[...]
        zh, rstd, ln = _ln_parts(z_ref[...], lg, lb_ref[...])
        dln = dyb_ref[...] * _silu_grad(ln, jax.nn.sigmoid(ln))
        _acc_rows(dlg_ref, dln * zh, first)
        _acc_rows(dlb_ref, dln, first)
        dzh = dln * lg
        dz = rstd * (dzh - jnp.mean(dzh, axis=-1, keepdims=True) - zh * jnp.mean(dzh * zh, axis=-1, keepdims=True))
        dz_ref[...] = dz
        _acc_rows(dbb_ref, dz, first)

    blk = pl.BlockSpec((tm, w), lambda i: (i, 0))
    vec = _full((1, w))
    res, _ = _host_call(
        body, grid=(t_rows // tm,), in_specs=[blk, pl.BlockSpec((tm, w), lambda i: (i, 1)), vec, vec],
        out_specs=[blk, vec, vec, vec],
        out_shape=[jax.ShapeDtypeStruct((t_rows, w), F32)] + [jax.ShapeDtypeStruct((1, w), F32)] * 3,
        args=(z, dy, lg, lb), name="mix_ln_bwd")
    return res


def _mix_conv_bwd(hin, dy, dz, wa, wb, wa_w, comm):
    t_rows = hin.shape[0]
    nt = wa_w // LANE
    ka, kb = wa.shape[0], wb.shape[0]
    nr = CONV_HALO + t_rows
    kb_rows = -(-kb // SUBLANE) * SUBLANE

    def body(bg_ref, cg_ref, ha_ref, val_ref, gt_ref, dya_ref, dz_ref, wa_ref, wb_ref,
             dh_ref, dwa_ref, dwb_ref, base, sh, based, shd, tmp):
        zeros = lambda n: jnp.zeros((n, LANE), F32)
        base[pl.ds(0, CONV_HALO), :] = zeros(CONV_HALO)
        base[pl.ds(nr, SUBLANE), :] = zeros(SUBLANE)
        based[pl.ds(t_rows, CONV_HALO + SUBLANE), :] = zeros(CONV_HALO + SUBLANE)

        def fwd_slot(k_taps, k):
            return _tap_slot(CONV_HALO - (k_taps - 1) + k)

        def bwd_slot(k_taps, k):
            return _tap_slot(k_taps - 1 - k)

        def conv(w_ref, k_taps, src, slot, b, n):
            acc = None
            for k in range(k_taps):
                r, q = slot(k_taps, k)
                term = w_ref[pl.ds(k, 1), :] * src[r, pl.ds(b + q, n), :]
                acc = term if acc is None else acc + term
            return acc

        def wgrad_loop(k_taps, d_of, dsrc_to_tmp):
            def step(b, accs):
                dv = d_of(b)
                new = []
                for k in range(k_taps):
                    r, q = fwd_slot(k_taps, k)
                    new.append(accs[k] + dv * sh[r, pl.ds(b + q, SUBLANE), :])
                tmp[pl.ds(b, SUBLANE), :] = dsrc_to_tmp(b)
                return tuple(new)

            return _row_loop(t_rows, SUBLANE, step, tuple(zeros(SUBLANE) for _ in range(k_taps)))

        def store_taps(ref, accs, rows):
            for k, acc in enumerate(accs):
                ref[pl.ds(k, 1), :] = jnp.sum(acc, axis=0, keepdims=True)
            if rows > len(accs):
                ref[pl.ds(len(accs), rows - len(accs)), :] = zeros(rows - len(accs))

        def fill_a(b, c):
            sl = pl.ds(b, CONV_CHUNK)
            base[pl.ds(CONV_HALO + b, CONV_CHUNK), :] = cg_ref[sl, :] * ha_ref[sl, :]
            based[sl, :] = dya_ref[sl, :] * bg_ref[sl, :]
            return c

        _row_loop(t_rows, CONV_CHUNK, fill_a)
        _fill_shifted(sh, base, sorted({fwd_slot(ka, k)[0] for k in range(ka)}), nr)
        _fill_shifted(shd, based, sorted({bwd_slot(ka, k)[0] for k in range(ka)}), nr)

        def d_bgate(b, c):
            sl = pl.ds(b, CONV_CHUNK)
            dh_ref[0, sl, :] = (dya_ref[sl, :] * conv(wa_ref, ka, sh, fwd_slot, b, CONV_CHUNK)).astype(BF16)
            return c

        _row_loop(t_rows, CONV_CHUNK, d_bgate)
        accs = wgrad_loop(ka, lambda b: based[pl.ds(b, SUBLANE), :],
                          lambda b: conv(wa_ref, ka, shd, bwd_slot, b, SUBLANE))
        store_taps(dwa_ref, accs, SUBLANE)

        def d_ch(b, c):
            sl = pl.ds(b, CONV_CHUNK)
            dua = tmp[sl, :]
            dh_ref[1, sl, :] = (dua * ha_ref[sl, :]).astype(BF16)
            dh_ref[2, sl, :] = (dua * cg_ref[sl, :]).astype(BF16)
            return c

        _row_loop(t_rows, CONV_CHUNK, d_ch)

        def fill_b(b, c):
            sl = pl.ds(b, CONV_CHUNK)
            base[pl.ds(CONV_HALO + b, CONV_CHUNK), :] = val_ref[sl, :] * jax.nn.sigmoid(gt_ref[sl, :])
            based[sl, :] = dz_ref[sl, :]
            return c

        _row_loop(t_rows, CONV_CHUNK, fill_b)
        _fill_shifted(sh, base, range(SUBLANE), nr)
        _fill_shifted(shd, based, range(SUBLANE), nr)
        accs = wgrad_loop(kb, lambda b: based[pl.ds(b, SUBLANE), :],
                          lambda b: conv(wb_ref, kb, shd, bwd_slot, b, SUBLANE))
        store_taps(dwb_ref, accs, kb_rows)

        def d_glu(b, c):
            sl = pl.ds(b, CONV_CHUNK)
            dgg = tmp[sl, :]
            sig = jax.nn.sigmoid(gt_ref[sl, :])
            dh_ref[3, sl, :] = (dgg * sig).astype(BF16)
            dh_ref[4, sl, :] = (dgg * val_ref[sl, :] * (sig * (1.0 - sig))).astype(BF16)
            return c

        _row_loop(t_rows, CONV_CHUNK, d_glu)

    def col(g):
        return pl.BlockSpec((t_rows, LANE), lambda i, g=g: (0, g * nt + i))

    tile = lambda rows: pl.BlockSpec((rows, LANE), lambda i: (0, i))
    return _host_call(
        body, grid=(nt,),
        in_specs=[col(0), col(1), col(2), col(3), col(4), tile(t_rows), tile(t_rows), tile(ka), tile(kb)],
        out_specs=[pl.BlockSpec((5, t_rows, LANE), lambda i: (0, 0, i)), tile(SUBLANE), tile(kb_rows)],
        out_shape=[jax.ShapeDtypeStruct((5, t_rows, wa_w), BF16), jax.ShapeDtypeStruct((SUBLANE, wa_w), F32),
                   jax.ShapeDtypeStruct((kb_rows, wa_w), F32)],
        scratch_shapes=[pltpu.VMEM((nr + SUBLANE, LANE), F32), pltpu.VMEM((SUBLANE, nr, LANE), F32),
                        pltpu.VMEM((nr + SUBLANE, LANE), F32), pltpu.VMEM((SUBLANE, nr, LANE), F32),
                        pltpu.VMEM((t_rows, LANE), F32)],
        args=(hin, hin, hin, hin, hin, dy, dz, wa, wb), name="mix_conv_bwd", comm=comm)


def _bwd_in_proj(dh5, win_t, h0, dh1, g1, tm, comm):
    t_rows, d = h0.shape
    groups, _, w = dh5.shape

    def body(dh_ref, w_ref, h0_ref, dh1_ref, g1_ref, dh0_ref, dg1_ref):
        dxn1 = None
        for g in range(groups):
            part = _dot_nn(dh_ref[g], w_ref[pl.ds(g * w, w), :])
            dxn1 = part if dxn1 is None else dxn1 + part
        h0 = h0_ref[...]
        r1 = _rstd(h0)
        h0h = h0 * r1
        _acc_rows(dg1_ref, dxn1 * h0h, pl.program_id(0) == 0)
        dh0_ref[...] = dh1_ref[...] + _rms_bwd(dxn1 * g1_ref[...], h0h, r1)

    blk = pl.BlockSpec((tm, d), lambda i: (i, 0))
    return _host_call(
        body, grid=(t_rows // tm,),
        in_specs=[pl.BlockSpec((groups, tm, w), lambda i: (0, i, 0)), _resident(win_t.shape), blk, blk, _full((1, d))],
        out_specs=[blk, _full((1, d))],
        out_shape=[jax.ShapeDtypeStruct((t_rows, d), F32), jax.ShapeDtypeStruct((1, d), F32)],
        args=(dh5, win_t, h0, dh1, g1), name="bwd_in_proj", comm=comm)


def _reduce_small(smalls, d, comm):
    (dmeta, dg1, dg2, dg3, dg4, dbb, dlg, dlb, lossv, dwa, dwb) = smalls
    half = d // 2
    kb_rows = dwb.shape[0]

    def body(dmeta_ref, dg1_ref, dg2_ref, dg3_ref, dg4_ref, dbb_ref, dlg_ref, dlb_ref, loss_ref, dwa_ref, dwb_ref,
             ptot_ref, pbuf, psib, chip_p, ps_send, ps_recv, pc_send, pc_recv):
        x, y, c = _mesh_pos()
        pbuf[...] = jnp.zeros_like(pbuf)
        pbuf[pl.ds(0, N_META), :] = dmeta_ref[...]
        for row, ref in ((16, dg1_ref), (17, dg2_ref), (18, dg3_ref), (19, dg4_ref)):
            pbuf[pl.ds(row, 1), :] = ref[...]
        pbuf[pl.ds(20, 1), pl.ds(0, half)] = dbb_ref[...]
        pbuf[pl.ds(20, 1), pl.ds(half, half)] = dlg_ref[...]
        pbuf[pl.ds(21, 1), pl.ds(0, half)] = dlb_ref[...]
        lv = loss_ref[...]
        pbuf[pl.ds(21, 1), pl.ds(half, half)] = lv[:, :half] + lv[:, half:]
        pbuf[pl.ds(24, SUBLANE), pl.ds(0, half)] = dwa_ref[...]
        pbuf[pl.ds(32, kb_rows), pl.ds(0, half)] = dwb_ref[...]
        to_sib = _remote(pbuf, psib, ps_send.at[0], ps_recv.at[0], (x, y, 1 - c))
        to_sib.start()
        to_sib.wait_recv()
        my_chip = 2 * x + y
        chip_p[my_chip] = pbuf[...] + psib[...]
        to_sib.wait_send()
        slot = chip_p.at[my_chip]
        cps = [_remote(slot, slot, pc_send.at[k], pc_recv.at[k], (*chip, c)) for k, chip in enumerate(_other_chips(x, y))]
        for cp in cps:
            cp.start()
        for cp in cps:
            cp.wait_recv()
        ptot_ref[...] = ((chip_p[0] + chip_p[1]) + chip_p[2]) + chip_p[3]
        for cp in cps:
            cp.wait_send()

    return _host_call(
        body, grid=(), in_specs=[VMEM] * 11, out_specs=[VMEM], out_shape=[jax.ShapeDtypeStruct((SMALL_ROWS, d), F32)],
        scratch_shapes=[pltpu.VMEM((SMALL_ROWS, d), F32), pltpu.VMEM((SMALL_ROWS, d), F32), pltpu.VMEM((4, SMALL_ROWS, d), F32),
                        pltpu.SemaphoreType.DMA((1,)), pltpu.SemaphoreType.DMA((1,)),
                        pltpu.SemaphoreType.DMA((3,)), pltpu.SemaphoreType.DMA((3,))],
        args=smalls, name="reduce_small", comm=comm)


def _adamw(w, g, m, v):
    m = ADAM_B1 * m + (1.0 - ADAM_B1) * g
    v = ADAM_B2 * v + (1.0 - ADAM_B2) * jnp.square(g)
    m_hat = m / (1.0 - ADAM_B1 ** ADAM_STEP)
    v_hat = v / (1.0 - ADAM_B2 ** ADAM_STEP)
    delta = -ADAM_LR * (m_hat / (jnp.sqrt(v_hat) + ADAM_EPS) + ADAM_WD * w)
    return delta, m, v


def _adam_big(g, pair, part, w, m, v, name):
    r, d = w.shape

    def body(g_ref, pair_ref, part_ref, w_ref, m_ref, v_ref, go_ref, d_ref, mo_ref, vo_ref, gbuf, abuf, pbuf, sems):
        j = _dev_index(*_mesh_pos())
        loads = [pltpu.make_async_copy(g_ref.at[pl.ds(pl.multiple_of(j * r, BF16_ROWS), r), :], gbuf, sems.at[0]),
                 pltpu.make_async_copy(pair_ref.at[0], abuf, sems.at[1]),
                 pltpu.make_async_copy(part_ref, pbuf, sems.at[2])]
        for cp in loads:
            cp.start()
        for cp in loads:
            cp.wait()
        g = gbuf[...].astype(F32) + abuf[...].astype(F32)
        for k in range(3):
            g = g + pbuf[k].astype(F32)
        go_ref[...] = g
        d_ref[...], mo_ref[...], vo_ref[...] = _adamw(w_ref[...], g, m_ref[...], v_ref[...])

    return pl.pallas_call(
        body, out_shape=[jax.ShapeDtypeStruct((r, d), F32)] * 4, in_specs=[ANY, ANY, ANY, VMEM, VMEM, VMEM],
        scratch_shapes=[pltpu.VMEM((r, d), BF16), pltpu.VMEM((r, d), BF16), pltpu.VMEM((3, r, d), BF16),
                        pltpu.SemaphoreType.DMA((3,))],
        name=name, compiler_params=_cparams())(g, pair, part, w, m, v)


def _adam_small(gs, ws, ms, vs):
    n = len(gs)

    def body(*refs):
        ins, outs = refs[:4 * n], refs[4 * n:]
        for i in range(n):
            g = ins[i][...]
            delta, m, v = _adamw(ins[n + i][...], g, ins[2 * n + i][...], ins[3 * n + i][...])
            outs[i][...] = delta
            outs[n + i][...] = m
            outs[2 * n + i][...] = v

    shapes = [jax.ShapeDtypeStruct(w.shape, F32) for w in ws]
    return pl.pallas_call(body, out_shape=shapes * 3, name="adam_small", compiler_params=_cparams())(*gs, *ws, *ms, *vs)


def kernel(x, meta_tokens, pre_mix_norm, w_in, conv_a_w, conv_b_w, conv_b_bias, ln_b_gain, ln_b_bias, w_out, post_mix_norm, pre_ffn_norm, w_gate, w_up, w_down, post_ffn_norm, loss_target, m_meta_tokens, m_pre_mix_norm, m_w_in, m_conv_a_w, m_conv_b_w, m_conv_b_bias, m_ln_b_gain, m_ln_b_bias, m_w_out, m_post_mix_norm, m_pre_ffn_norm, m_w_gate, m_w_up, m_w_down, m_post_ffn_norm, v_meta_tokens, v_pre_mix_norm, v_w_in, v_conv_a_w, v_conv_b_w, v_conv_b_bias, v_ln_b_gain, v_ln_b_bias, v_w_out, v_post_mix_norm, v_pre_ffn_norm, v_w_gate, v_w_up, v_w_down, v_post_ffn_norm):
    _, seq, d = x.shape
    ka, ca_loc = conv_a_w.shape[1:]
    kb, cb_loc = conv_b_w.shape[1:]
    wa_w = ca_loc * N_DEV
    assert cb_loc == ca_loc and wa_w % LANE == 0 and w_in.shape[2] * N_DEV == 5 * wa_w
    pad = (-(N_META + seq)) % ROW_ALIGN
    x0 = pad + N_META
    t_rows = x0 + seq
    assert t_rows % (N_ROW_BLOCKS * BF16_ROWS) == 0 and t_rows % CONV_CHUNK == 0 and d % LANE == 0
    tm = t_rows // N_ROW_BLOCKS
    me = _dev_index(*_mesh_pos())

    def as_rows(w_in_like, w_out_like, w_gate_like, w_up_like, w_down_like):
        return (w_in_like[0].T, w_out_like[0], w_gate_like[0].T, w_up_like[0].T, w_down_like[0])

    w_loc = as_rows(w_in, w_out, w_gate, w_up, w_down)
    rows = [w.shape[0] for w in w_loc]
    assert all(r % ADD_CHUNK == 0 for r in rows)
    P_IN, P_OUT, P_GATE, P_UP, P_DOWN = range(N_BIG)
    gate_cut = (rows[P_GATE] * 3 // 8) // BF16_ROWS * BF16_ROWS
    up_cut = (rows[P_UP] // 4) // BF16_ROWS * BF16_ROWS
    jobs_first = [(P_IN, 0, rows[P_IN])]
    jobs_in_proj = [(P_OUT, 0, rows[P_OUT]), (P_GATE, 0, gate_cut)]
    jobs_conv = [(P_GATE, gate_cut, rows[P_GATE] - gate_cut), (P_UP, 0, up_cut)]
    jobs_out_proj = [(P_UP, up_cut, rows[P_UP] - up_cut)]
    jobs_gate_up = [(P_DOWN, 0, rows[P_DOWN])]

    sm = jnp.zeros((SM_ROWS, LANE), F32)
    sm = sm.at[0:N_META, :].set(meta_tokens)
    sm = sm.at[16:16 + ka, 0:ca_loc].set(conv_a_w[0])
    sm = sm.at[24:24 + kb, 0:cb_loc].set(conv_b_w[0])
    wl, wfull, sm_all = _gather_first(w_loc, sm, jobs_first)
    meta_full = jnp.transpose(sm_all[:, 0:N_META, :], (1, 0, 2)).reshape(N_META, d)
    wa = jnp.transpose(sm_all[:, 16:16 + ka, 0:ca_loc], (1, 0, 2)).reshape(ka, wa_w)
    wb = jnp.transpose(sm_all[:, 24:24 + kb, 0:cb_loc], (1, 0, 2)).reshape(kb, wa_w)

    def gather(jobs):
        return _gather_comm(wl, {p: wfull[p] for p in sorted({j[0] for j in jobs})}, jobs, rows)

    def gathered(jobs, extra):
        for p, arr in zip(sorted({j[0] for j in jobs}), extra):
            wfull[p] = arr

    h0, tgt = _prep(x[0], loss_target[0], meta_full, t_rows, x0)
    (xn1, hin), extra = _in_proj(h0, pre_mix_norm, wfull[P_IN], tm, gather(jobs_in_proj))
    gathered(jobs_in_proj, extra)
    (ya, z), extra = _mix_conv_fwd(hin, wa, wb, conv_b_bias, wa_w, gather(jobs_conv))
    gathered(jobs_conv, extra)
    y = _mix_ln_fwd(ya, z, ln_b_gain, ln_b_bias, tm)
    (mix, h1, xn2), extra = _out_proj(y, wfull[P_OUT], h0, post_mix_norm, pre_ffn_norm, tm, gather(jobs_out_proj))
    gathered(jobs_out_proj, extra)
    (a, u, s), extra = _gate_up(xn2, wfull[P_GATE], wfull[P_UP], tm, gather(jobs_gate_up))
    gathered(jobs_gate_up, extra)
    dh2, dff, dg4, lossv = _down_loss(s, wfull[P_DOWN], h1, tgt, post_ffn_norm, tm, x0)

    gwd, _ = _wgrad(s, dff, "wgrad_down")
    (da, du), (pair_d,) = _bwd_down(dff, wfull[P_DOWN], a, u, tm, _pair_comm(gwd, rows[P_DOWN]))
    gwg, _ = _wgrad(da, xn2, "wgrad_gate")
    gwu, _ = _wgrad(du, xn2, "wgrad_up")
    (dh1, dg3), (part_d, pair_g, pair_u) = _bwd_ffn_in(
        da, du, wfull[P_GATE], wfull[P_UP], h1, dh2, pre_ffn_norm, tm,
        _merge_comms([_chip_comm(gwd, pair_d, rows[P_DOWN]), _pair_comm(gwg, rows[P_GATE]), _pair_comm(gwu, rows[P_UP])]))
    dmix, dy, dg2 = _bwd_out_proj(dh1, mix, wfull[P_OUT], post_mix_norm, tm)
    gwo, _ = _wgrad(y, dmix, "wgrad_out")
    dz, dlg, dlb, dbb = _mix_ln_bwd(z, dy, ln_b_gain, ln_b_bias, tm)
    (dh5, dwa, dwb), (part_g, part_u, pair_o) = _mix_conv_bwd(
        hin, dy, dz, wa, wb, wa_w,
        _merge_comms([_chip_comm(gwg, pair_g, rows[P_GATE]), _chip_comm(gwu, pair_u, rows[P_UP]), _pair_comm(gwo, rows[P_OUT])]))
    gwi, (part_o,) = _wgrad(dh5, xn1, "wgrad_in", _chip_comm(gwo, pair_o, rows[P_OUT]))
    (dh0, dg1), (pair_i,) = _bwd_in_proj(dh5, wfull[P_IN], h0, dh1, pre_mix_norm, tm, _pair_comm(gwi, rows[P_IN]))
    grad_x = dh0[x0:][None]
    dmeta = dh0[x0 - N_META:x0]
    (ptot,), (part_i,) = _reduce_small((dmeta, dg1, dg2, dg3, dg4, dbb, dlg, dlb, lossv, dwa, dwb), d,
                                       _chip_comm(gwi, pair_i, rows[P_IN]))

    half = d // 2
    loss = (0.5 / d) * jnp.sum(ptot[21, half:])
    g_meta = lax.dynamic_slice(ptot, (0, me * (d // N_DEV)), (N_META, d // N_DEV))
    g_small = [g_meta, ptot[16:17], lax.dynamic_slice(ptot, (24, me * ca_loc), (ka, ca_loc))[None],
               lax.dynamic_slice(ptot, (32, me * cb_loc), (kb, cb_loc))[None],
               ptot[20:21, :half], ptot[20:21, half:], ptot[21:22, :half], ptot[17:18], ptot[18:19], ptot[19:20]]
    w_small = [meta_tokens, pre_mix_norm, conv_a_w, conv_b_w, conv_b_bias, ln_b_gain, ln_b_bias, post_mix_norm,
               pre_ffn_norm, post_ffn_norm]
    m_small = [m_meta_tokens, m_pre_mix_norm, m_conv_a_w, m_conv_b_w, m_conv_b_bias, m_ln_b_gain, m_ln_b_bias,
               m_post_mix_norm, m_pre_ffn_norm, m_post_ffn_norm]
    v_small = [v_meta_tokens, v_pre_mix_norm, v_conv_a_w, v_conv_b_w, v_conv_b_bias, v_ln_b_gain, v_ln_b_bias,
               v_post_mix_norm, v_pre_ffn_norm, v_post_ffn_norm]
    small = _adam_small(g_small, w_small, m_small, v_small)
    n_small = len(w_small)
    d_small, nm_small, nv_small = small[:n_small], small[n_small:2 * n_small], small[2 * n_small:]

    m_loc = as_rows(m_w_in, m_w_out, m_w_gate, m_w_up, m_w_down)
    v_loc = as_rows(v_w_in, v_w_out, v_w_gate, v_w_up, v_w_down)
    full_grads = (gwi, gwo, gwg, gwu, gwd)
    pairs = (pair_i, pair_o, pair_g, pair_u, pair_d)
    parts = (part_i, part_o, part_g, part_u, part_d)
    bigs = {}
    for p, (name, tr) in enumerate((("w_in", True), ("w_out", False), ("w_gate", True), ("w_up", True), ("w_down", False))):
        res = _adam_big(full_grads[p], pairs[p], parts[p], w_loc[p], m_loc[p], v_loc[p], "adam_" + name)
        bigs[name] = [(o.T if tr else o)[None] for o in res]

    def ordered(pick_small, pick_big):
        sm_it = iter(range(n_small))
        out = []
        for name in ("s", "s", "w_in", "s", "s", "s", "s", "s", "w_out", "s", "s", "w_gate", "w_up", "w_down", "s"):
            out.append(pick_small(next(sm_it)) if name == "s" else pick_big(name))
        return out

    grads = ordered(lambda i: g_small[i], lambda n: bigs[n][0])
    deltas = ordered(lambda i: d_small[i], lambda n: bigs[n][1])
    new_m = ordered(lambda i: nm_small[i], lambda n: bigs[n][2])
    new_v = ordered(lambda i: nv_small[i], lambda n: bigs[n][3])
    return (loss, grad_x, *grads, *deltas, *new_m, *new_v)
```

```python
import jax
import jax.numpy as jnp
from jax import lax
from jax.experimental import pallas as pl
from jax.experimental.pallas import tpu as pltpu

F32 = jnp.float32
BF16 = jnp.bfloat16
MESH = pl.DeviceIdType.MESH

N_META = 16
N_DEV = 8
RMS_EPS = 1e-6
LN_EPS = 1e-5
ADAM_LR = 0.001
ADAM_B1 = 0.9
ADAM_B2 = 0.999
ADAM_EPS = 1e-08
ADAM_WD = 0.01
ADAM_STEP = 10

LANE = 128
SUBLANE = 8
BF16_ROWS = 16
ROW_ALIGN = 128
N_ROW_BLOCKS = 4
CONV_HALO = 32
CONV_CHUNK = 64
WGRAD_ROWS = 32
N_CHUNK = 512
WGRAD_TILE = 256
ADD_CHUNK = 32
V7X_VMEM_BYTES = 64 * 1024 * 1024
VMEM_LIMIT = V7X_VMEM_BYTES - 6 * 1024 * 1024
SMALL_ROWS = 64
SM_ROWS = 56
N_BIG = 5

ANY = pl.BlockSpec(memory_space=pl.ANY)
VMEM = pl.BlockSpec(memory_space=pltpu.VMEM)


def _cparams(n_grid_axes=0):
    sem = ("arbitrary",) * n_grid_axes if n_grid_axes else None
    return pltpu.CompilerParams(dimension_semantics=sem, vmem_limit_bytes=VMEM_LIMIT)


def _mesh_pos():
    return lax.axis_index("x"), lax.axis_index("y"), lax.axis_index("c")


def _dev_index(px, py, pc):
    return 4 * px + 2 * py + pc


def _other_chips(x, y):
    return [(1 - x, y), (x, 1 - y), (1 - x, 1 - y)]


def _full(shape):
    return pl.BlockSpec(shape, lambda *_: (0,) * len(shape))


def _resident(shape):
    return pl.BlockSpec(shape, lambda *_: (0,) * len(shape), pipeline_mode=pl.Buffered(1))


def _dot_nt(a, w):
    return lax.dot_general(a, w, (((1,), (1,)), ((), ())), preferred_element_type=F32)


def _dot_nn(a, w):
    return jnp.dot(a, w, preferred_element_type=F32)


def _chunks(n, c):
    out, o = [], 0
    while o < n:
        out.append((o, min(c, n - o)))
        o += c
    return out


def _rstd(h):
    return lax.rsqrt(jnp.mean(h * h, axis=-1, keepdims=True) + RMS_EPS)


def _rms_bwd(dyh, yh, r):
    return r * (dyh - yh * jnp.mean(dyh * yh, axis=-1, keepdims=True))


def _silu_grad(a, sig):
    return sig * (1.0 + a * (1.0 - sig))


def _acc_rows(ref, val, first):
    s = jnp.sum(val, axis=0, keepdims=True)

    @pl.when(first)
    def _():
        ref[...] = s

    @pl.when(jnp.logical_not(first))
    def _():
        ref[...] += s


def _row_loop(t_rows, chunk, fn, carry=None):
    def step(i, c):
        return fn(pl.multiple_of(i * chunk, chunk), c)

    return lax.fori_loop(0, t_rows // chunk, step, carry)


def _remote(src, dst, send_sem, recv_sem, to):
    return pltpu.make_async_remote_copy(src_ref=src, dst_ref=dst, send_sem=send_sem, recv_sem=recv_sem,
                                        device_id=to, device_id_type=MESH)


class _Comm:
    def __init__(self, inputs, out_shapes, aliases, scratch, start, finish, mid=None):
        self.inputs, self.out_shapes, self.aliases, self.scratch = list(inputs), list(out_shapes), dict(aliases), list(scratch)
        self.start, self.finish, self.mid = start, finish, mid


def _merge_comms(comms):
    inputs, out_shapes, aliases, scratch, spans = [], [], {}, [], []
    for cm in comms:
        spans.append((len(inputs), len(out_shapes), len(scratch), cm))
        aliases.update({len(inputs) + k: len(out_shapes) + v for k, v in cm.aliases.items()})
        inputs += cm.inputs
        out_shapes += cm.out_shapes
        scratch += cm.scratch

    def run(which):
        def fn(ins, outs, scr):
            for i0, o0, s0, cm in spans:
                if getattr(cm, which) is not None:
                    getattr(cm, which)(ins[i0:i0 + len(cm.inputs)], outs[o0:o0 + len(cm.out_shapes)], scr[s0:s0 + len(cm.scratch)])
        return fn

    mid = run("mid") if any(cm.mid is not None for cm in comms) else None
    return _Comm(inputs, out_shapes, aliases, scratch, run("start"), run("finish"), mid)


def _host_call(body, *, grid, in_specs, out_specs, out_shape, args, name, scratch_shapes=(), comm=None):
    if comm is None:
        res = pl.pallas_call(body, grid=grid, in_specs=list(in_specs), out_specs=list(out_specs), out_shape=list(out_shape),
                             scratch_shapes=list(scratch_shapes), name=name, compiler_params=_cparams(len(grid)))(*args)
        return list(res), []
    n_in, n_out, n_scr = len(args), len(out_shape), len(scratch_shapes)
    c_in, c_out = len(comm.inputs), len(comm.out_shapes)

    def hosted(*refs):
        ins, c_ins = refs[:n_in], refs[n_in:n_in + c_in]
        o0 = n_in + c_in
        outs, c_outs = refs[o0:o0 + n_out], refs[o0 + n_out:o0 + n_out + c_out]
        s0 = o0 + n_out + c_out
        scr, c_scr = refs[s0:s0 + n_scr], refs[s0 + n_scr:]
        if not grid:
            comm.start(c_ins, c_outs, c_scr)
            body(*ins, *outs, *scr)
            comm.finish(c_ins, c_outs, c_scr)
            return
        first = last = None
        for a, n in enumerate(grid):
            f, l = pl.program_id(a) == 0, pl.program_id(a) == n - 1
            first = f if first is None else jnp.logical_and(first, f)
            last = l if last is None else jnp.logical_and(last, l)

        @pl.when(first)
        def _():
            comm.start(c_ins, c_outs, c_scr)

        if comm.mid is not None:
            assert len(grid) == 1 and grid[0] >= 3

            @pl.when(pl.program_id(0) == 1)
            def _():
                comm.mid(c_ins, c_outs, c_scr)

        body(*ins, *outs, *scr)

        @pl.when(last)
        def _():
            comm.finish(c_ins, c_outs, c_scr)

    res = pl.pallas_call(
        hosted, grid=grid, in_specs=list(in_specs) + [ANY] * c_in, out_specs=list(out_specs) + [ANY] * c_out,
        out_shape=list(out_shape) + comm.out_shapes, scratch_shapes=list(scratch_shapes) + comm.scratch,
        input_output_aliases={n_in + k: n_out + v for k, v in comm.aliases.items()},
        name=name, compiler_params=_cparams(len(grid)))(*args, *comm.inputs)
    return list(res[:n_out]), list(res[n_out:])


def _gather_jobs(x, y, c, jobs, rows, lo, src_ref, dests, send_sems, recv_sems):
    sibling = (x, y, 1 - c)
    chips = _other_chips(x, y)
    first, forward = [], []
    for n, (p, r0, nr) in enumerate(jobs):
        def rows_of(px, py, pc, p=p, r0=r0, nr=nr):
            return dests[p].at[pl.ds(pl.multiple_of(_dev_index(px, py, pc) * rows[p] + r0, BF16_ROWS), nr), :]

        src = src_ref.at[pl.ds(lo[p] + r0, nr), :]
        sem = lambda k, n=n: (send_sems.at[7 * n + k], recv_sems.at[7 * n + k])
        first.append([_remote(src, rows_of(x, y, c), *sem(0), sibling)]
                     + [_remote(src, rows_of(x, y, c), *sem(1 + j), (*chip, c)) for j, chip in enumerate(chips)])
        forward.append([_remote(rows_of(*chip, c), rows_of(*chip, c), *sem(4 + j), sibling) for j, chip in enumerate(chips)])
    return first, forward


def _gather_start(first):
    for cps in first:
        for cp in cps:
            cp.start()


def _gather_finish(first, forward):
    for cps, fws in zip(first, forward):
        for j in range(3):
            cps[1 + j].wait_recv()
            fws[j].start()
    for cps, fws in zip(first, forward):
        cps[0].wait_recv()
        for fw in fws:
            fw.wait_recv()
        for cp in cps + fws:
            cp.wait_send()


def _gather_comm(wl, dests, jobs, rows):
    lo = [sum(rows[:p]) for p in range(N_BIG)]
    ps = sorted(dests)
    slot = {p: i for i, p in enumerate(ps)}

    def descs(ins, outs, scr):
        x, y, c = _mesh_pos()
        return _gather_jobs(x, y, c, jobs, rows, lo, ins[0], {p: outs[slot[p]] for p in ps}, scr[0], scr[1])

    def start(ins, outs, scr):
        _gather_start(descs(ins, outs, scr)[0])

    def finish(ins, outs, scr):
        _gather_finish(*descs(ins, outs, scr))

    n_sems = 7 * len(jobs)
    return _Comm([wl] + [dests[p] for p in ps], [jax.ShapeDtypeStruct(dests[p].shape, BF16) for p in ps],
                 {1 + i: i for i in range(len(ps))},
                 [pltpu.SemaphoreType.DMA((n_sems,)), pltpu.SemaphoreType.DMA((n_sems,))], start, finish)


def _pair_comm(g, r):
    d = g.shape[1]

    def descs(ins, outs, scr):
        x, y, c = _mesh_pos()
        chips = [(x, y)] + _other_chips(x, y)
        return [_remote(ins[0].at[pl.ds(pl.multiple_of(_dev_index(*chip, 1 - c) * r, BF16_ROWS), r), :], outs[0].at[k],
                        scr[0].at[k], scr[1].at[k], (x, y, 1 - c)) for k, chip in enumerate(chips)]

    def start(ins, outs, scr):
        for cp in descs(ins, outs, scr):
            cp.start()

    def finish(ins, outs, scr):
        cps = descs(ins, outs, scr)
        for cp in cps:
            cp.wait_recv()
        for cp in cps:
            cp.wait_send()

    comm = _Comm([g], [jax.ShapeDtypeStruct((4, r, d), BF16)], {},
                 [pltpu.SemaphoreType.DMA((4,)), pltpu.SemaphoreType.DMA((4,))], start, finish)
    comm.descs = descs
    return comm


def _chip_comm(g, pair, r):
    d = g.shape[1]
    return _Comm([g, pair], [jax.ShapeDtypeStruct((3, r, d), BF16)], {}, _chip_scratch(r, d),
                 lambda ins, outs, scr: _chip_send(ins[0], ins[1], outs[0], scr, r),
                 lambda ins, outs, scr: _chip_wait(outs[0], scr))


def _chip_scratch(r, d):
    return [pltpu.VMEM((r, d), BF16), pltpu.VMEM((r, d), BF16), pltpu.VMEM((3, r, d), BF16),
            pltpu.SemaphoreType.DMA((3,)), pltpu.SemaphoreType.DMA((3,)), pltpu.SemaphoreType.DMA((2,))]


def _chip_copies(part_ref, scr):
    x, y, c = _mesh_pos()
    return [_remote(scr[2].at[k], part_ref.at[k], scr[3].at[k], scr[4].at[k], (*chip, c))
            for k, chip in enumerate(_other_chips(x, y))]


def _chip_send(g_ref, pair_ref, part_ref, scr, r):
    x, y, c = _mesh_pos()
    gbuf, abuf, sendbuf, local = scr[0], scr[1], scr[2], scr[5]
    cps = _chip_copies(part_ref, scr)
    for k, chip in enumerate(_other_chips(x, y)):
        j = _dev_index(*chip, c)
        loads = [pltpu.make_async_copy(g_ref.at[pl.ds(pl.multiple_of(j * r, BF16_ROWS), r), :], gbuf, local.at[0]),
                 pltpu.make_async_copy(pair_ref.at[1 + k], abuf, local.at[1])]
        for cp in loads:
            cp.start()
        for cp in loads:
            cp.wait()

        def add(b, carry, k=k):
            sl = pl.ds(b, ADD_CHUNK)
            sendbuf[k, sl, :] = (gbuf[sl, :].astype(F32) + abuf[sl, :].astype(F32)).astype(BF16)
            return carry

        _row_loop(r, ADD_CHUNK, add)
        cps[k].start()


def _chip_wait(part_ref, scr):
    cps = _chip_copies(part_ref, scr)
    for cp in cps:
        cp.wait_recv()
    for cp in cps:
        cp.wait_send()


def _pair_chip_comm(g, r):
    d = g.shape[1]
    pair = _pair_comm(g, r)
    n_pair = len(pair.scratch)

    def mid(ins, outs, scr):
        for cp in pair.descs(ins, outs[:1], scr[:n_pair]):
            cp.wait_recv()
        _chip_send(ins[0], outs[0], outs[1], scr[n_pair:], r)

    def finish(ins, outs, scr):
        _chip_wait(outs[1], scr[n_pair:])
        for cp in pair.descs(ins, outs[:1], scr[:n_pair]):
            cp.wait_send()

    return _Comm([g], pair.out_shapes + [jax.ShapeDtypeStruct((3, r, d), BF16)], {}, pair.scratch + _chip_scratch(r, d),
                 lambda ins, outs, scr: pair.start(ins, outs[:1], scr[:n_pair]), finish, mid)


def _gather_first(shards, sm, jobs):
    d = shards[0].shape[1]
    rows = [w.shape[0] for w in shards]
    lo = [sum(rows[:p]) for p in range(N_BIG)]
    n_sems = 7 * len(jobs)

    def body(s0, s1, s2, s3, s4, sm_ref, wl_ref, o0, o1, o2, o3, o4, sa_ref, wl_v, send_sems, recv_sems, ssend, srecv, local_sems):
        dests = (o0, o1, o2, o3, o4)
        x, y, c = _mesh_pos()
        me = (x, y, c)
        jme = _dev_index(*me)
        for p, ref in enumerate((s0, s1, s2, s3, s4)):
            wl_v[pl.ds(lo[p], rows[p]), :] = ref[...].astype(BF16)
        first, forward = _gather_jobs(x, y, c, jobs, rows, lo, wl_v, dict(enumerate(dests)), send_sems, recv_sems)
        _gather_start(first)
        peers = [(x, y, 1 - c)] + [(*chip, pc) for pc in (c, 1 - c) for chip in _other_chips(x, y)]
        smalls = [_remote(sm_ref, sa_ref.at[jme], ssend.at[k], srecv.at[k], to) for k, to in enumerate(peers)]
        for cp in smalls:
            cp.start()
        mine = [pltpu.make_async_copy(wl_v.at[pl.ds(lo[p], rows[p]), :],
                                      dests[p].at[pl.ds(pl.multiple_of(jme * rows[p], BF16_ROWS), rows[p]), :], local_sems.at[p])
                for p in range(N_BIG)]
        mine.append(pltpu.make_async_copy(wl_v, wl_ref, local_sems.at[N_BIG]))
        mine.append(pltpu.make_async_copy(sm_ref, sa_ref.at[jme], local_sems.at[N_BIG + 1]))
        for cp in mine:
            cp.start()
        _gather_finish(first, forward)
        for cp in smalls:
            cp.wait_recv()
        for cp in smalls:
            cp.wait_send()
        for cp in mine:
            cp.wait()

    out_shape = [jax.ShapeDtypeStruct((sum(rows), d), BF16)]
    out_shape += [jax.ShapeDtypeStruct((N_DEV * r, d), BF16) for r in rows]
    out_shape.append(jax.ShapeDtypeStruct((N_DEV,) + sm.shape, F32))
    res = pl.pallas_call(
        body, out_shape=out_shape, in_specs=[VMEM] * 6, out_specs=[ANY] * 7,
        scratch_shapes=[pltpu.VMEM((sum(rows), d), BF16), pltpu.SemaphoreType.DMA((n_sems,)), pltpu.SemaphoreType.DMA((n_sems,)),
                        pltpu.SemaphoreType.DMA((7,)), pltpu.SemaphoreType.DMA((7,)), pltpu.SemaphoreType.DMA((N_BIG + 2,))],
        name="gather_first", compiler_params=_cparams())(*shards, sm)
    return res[0], list(res[1:1 + N_BIG]), res[-1]


def _prep(x2, tgt2, meta_full, t_rows, x0):
    s, d = x2.shape
    assert x0 == ROW_ALIGN and s % ROW_ALIGN == 0

    def body(x_ref, tgt_ref, meta_ref, h0_ref, tp_ref):
        i = pl.program_id(0)

        @pl.when(i == 0)
        def _():
            h0_ref[...] = jnp.zeros_like(h0_ref)
            h0_ref[pl.ds(x0 - N_META, N_META), :] = meta_ref[...]
            tp_ref[...] = jnp.zeros_like(tp_ref)

        @pl.when(i > 0)
        def _():
            h0_ref[...] = x_ref[...]
            tp_ref[...] = tgt_ref[...]

    src = pl.BlockSpec((ROW_ALIGN, d), lambda i: (jnp.maximum(i - 1, 0), 0))
    dst = pl.BlockSpec((ROW_ALIGN, d), lambda i: (i, 0))
    res, _ = _host_call(body, grid=(t_rows // ROW_ALIGN,), in_specs=[src, src, _full((N_META, d))], out_specs=[dst, dst],
                        out_shape=[jax.ShapeDtypeStruct((t_rows, d), F32)] * 2, args=(x2, tgt2, meta_full), name="prep")
    return res


def _in_proj(h0, g1, win_t, tm, comm):
    t_rows, d = h0.shape
    e = win_t.shape[0]

    def body(h_ref, g_ref, w_ref, xn_ref, hin_ref):
        h = h_ref[...]
        xn = ((h * _rstd(h)) * g_ref[...]).astype(BF16)
        xn_ref[...] = xn
        for o, n in _chunks(e, N_CHUNK):
            hin_ref[:, pl.ds(o, n)] = _dot_nt(xn, w_ref[pl.ds(o, n), :])

    return _host_call(
        body, grid=(t_rows // tm,),
        in_specs=[pl.BlockSpec((tm, d), lambda i: (i, 0)), _full((1, d)), _resident((e, d))],
        out_specs=[pl.BlockSpec((tm, d), lambda i: (i, 0)), pl.BlockSpec((tm, e), lambda i: (i, 0))],
        out_shape=[jax.ShapeDtypeStruct((t_rows, d), BF16), jax.ShapeDtypeStruct((t_rows, e), F32)],
        args=(h0, g1, win_t), name="in_proj", comm=comm)


def _tap_slot(off):
    return off % SUBLANE, (off // SUBLANE) * SUBLANE


def _fill_shifted(sh_ref, base_ref, residues, n_rows):
    for r in residues:
        sh_ref[r] = base_ref[pl.ds(r, n_rows), :]


def _mix_conv_fwd(hin, wa, wb, bb, wa_w, comm):
    t_rows = hin.shape[0]
    nt = wa_w // LANE
    ka, kb = wa.shape[0], wb.shape[0]
    nr = CONV_HALO + t_rows

    def body(bg_ref, cg_ref, ha_ref, val_ref, gt_ref, wa_ref, wb_ref, bb_ref, ya_ref, z_ref, base, sh):
        base[pl.ds(0, CONV_HALO), :] = jnp.zeros((CONV_HALO, LANE), F32)
        base[pl.ds(nr, SUBLANE), :] = jnp.zeros((SUBLANE, LANE), F32)

        def conv(w_ref, k_taps, b, n):
            acc = None
            for k in range(k_taps):
                r, q = _tap_slot(CONV_HALO - (k_taps - 1) + k)
                term = w_ref[pl.ds(k, 1), :] * sh[r, pl.ds(b + q, n), :]
                acc = term if acc is None else acc + term
            return acc

        def fill_a(b, c):
            base[pl.ds(CONV_HALO + b, CONV_CHUNK), :] = cg_ref[pl.ds(b, CONV_CHUNK), :] * ha_ref[pl.ds(b, CONV_CHUNK), :]
            return c

        _row_loop(t_rows, CONV_CHUNK, fill_a)
        _fill_shifted(sh, base, sorted({_tap_slot(CONV_HALO - (ka - 1) + k)[0] for k in range(ka)}), nr)

        def out_a(b, c):
            ya_ref[pl.ds(b, CONV_CHUNK), :] = (bg_ref[pl.ds(b, CONV_CHUNK), :] * conv(wa_ref, ka, b, CONV_CHUNK)).astype(BF16)
            return c

        _row_loop(t_rows, CONV_CHUNK, out_a)

        def fill_b(b, c):
            base[pl.ds(CONV_HALO + b, CONV_CHUNK), :] = (val_ref[pl.ds(b, CONV_CHUNK), :]
                                                          * jax.nn.sigmoid(gt_ref[pl.ds(b, CONV_CHUNK), :]))
            return c

        _row_loop(t_rows, CONV_CHUNK, fill_b)
        _fill_shifted(sh, base, range(SUBLANE), nr)

        def out_b(b, c):
            z_ref[pl.ds(b, CONV_CHUNK), :] = conv(wb_ref, kb, b, CONV_CHUNK) + bb_ref[...]
            return c

        _row_loop(t_rows, CONV_CHUNK, out_b)

    def col(g):
        return pl.BlockSpec((t_rows, LANE), lambda i, g=g: (0, g * nt + i))

    tile = lambda rows: pl.BlockSpec((rows, LANE), lambda i: (0, i))
    return _host_call(
        body, grid=(nt,),
        in_specs=[col(0), col(1), col(2), col(3), col(4), tile(ka), tile(kb), tile(1)],
        out_specs=[tile(t_rows), tile(t_rows)],
        out_shape=[jax.ShapeDtypeStruct((t_rows, wa_w), BF16), jax.ShapeDtypeStruct((t_rows, wa_w), F32)],
        scratch_shapes=[pltpu.VMEM((nr + SUBLANE, LANE), F32), pltpu.VMEM((SUBLANE, nr, LANE), F32)],
        args=(hin, hin, hin, hin, hin, wa, wb, bb), name="mix_conv_fwd", comm=comm)


def _ln_parts(z, lg, lb):
    mu = jnp.mean(z, axis=-1, keepdims=True)
    zc = z - mu
    rstd = lax.rsqrt(jnp.mean(zc * zc, axis=-1, keepdims=True) + LN_EPS)
    zh = zc * rstd
    return zh, rstd, zh * lg + lb


def _mix_ln_fwd(ya, z, lg, lb, tm):
    t_rows, w = z.shape

    def body(ya_ref, z_ref, lg_ref, lb_ref, y_ref):
        _, _, ln = _ln_parts(z_ref[...], lg_ref[...], lb_ref[...])
        y_ref[:, pl.ds(0, w)] = ya_ref[...]
        y_ref[:, pl.ds(w, w)] = (ln * jax.nn.sigmoid(ln)).astype(BF16)

    blk = pl.BlockSpec((tm, w), lambda i: (i, 0))
    res, _ = _host_call(body, grid=(t_rows // tm,), in_specs=[blk, blk, _full((1, w)), _full((1, w))],
                        out_specs=[pl.BlockSpec((tm, 2 * w), lambda i: (i, 0))],
                        out_shape=[jax.ShapeDtypeStruct((t_rows, 2 * w), BF16)], args=(ya, z, lg, lb), name="mix_ln_fwd")
    return res[0]


def _out_proj(y, w_out, h0, g2, g3, tm, comm):
    t_rows, d = h0.shape

    def body(y_ref, w_ref, h0_ref, g2_ref, g3_ref, mix_ref, h1_ref, xn2_ref):
        mix = _dot_nn(y_ref[...], w_ref[...])
        mix_ref[...] = mix
        h1 = h0_ref[...] + (mix * _rstd(mix)) * g2_ref[...]
        h1_ref[...] = h1
        xn2_ref[...] = ((h1 * _rstd(h1)) * g3_ref[...]).astype(BF16)

    blk = pl.BlockSpec((tm, d), lambda i: (i, 0))
    return _host_call(
        body, grid=(t_rows // tm,), in_specs=[blk, _resident(w_out.shape), blk, _full((1, d)), _full((1, d))],
        out_specs=[blk, blk, blk],
        out_shape=[jax.ShapeDtypeStruct((t_rows, d), F32), jax.ShapeDtypeStruct((t_rows, d), F32),
                   jax.ShapeDtypeStruct((t_rows, d), BF16)],
        args=(y, w_out, h0, g2, g3), name="out_proj", comm=comm)


def _gate_up(xn2, wg_t, wu_t, tm, comm):
    t_rows, d = xn2.shape
    f = wg_t.shape[0]

    def body(x_ref, wg_ref, wu_ref, a_ref, u_ref, s_ref):
        xn = x_ref[...]
        for o, n in _chunks(f, N_CHUNK):
            a = _dot_nt(xn, wg_ref[pl.ds(o, n), :])
            u = _dot_nt(xn, wu_ref[pl.ds(o, n), :])
            a_ref[:, pl.ds(o, n)] = a.astype(BF16)
            u_ref[:, pl.ds(o, n)] = u.astype(BF16)
            s_ref[:, pl.ds(o, n)] = ((a * jax.nn.sigmoid(a)) * u).astype(BF16)

    blk = pl.BlockSpec((tm, f), lambda i: (i, 0))
    return _host_call(
        body, grid=(t_rows // tm,),
        in_specs=[pl.BlockSpec((tm, d), lambda i: (i, 0)), _resident((f, d)), _resident((f, d))],
        out_specs=[blk, blk, blk], out_shape=[jax.ShapeDtypeStruct((t_rows, f), BF16)] * 3,
        args=(xn2, wg_t, wu_t), name="gate_up", comm=comm)


def _down_loss(s, wd, h1, tgt, g4, tm, x0):
    t_rows, d = h1.shape
    f = wd.shape[0]

    def body(s_ref, w_ref, h1_ref, tgt_ref, g4_ref, dh2_ref, dff_ref, dg4_ref, loss_ref):
        i = pl.program_id(0)
        ff = _dot_nn(s_ref[...], w_ref[...])
        r4 = _rstd(ff)
        fh = ff * r4
        g4 = g4_ref[...]
        h2 = h1_ref[...] + fh * g4
        row = i * tm + lax.broadcasted_iota(jnp.int32, (tm, 1), 0)
        diff = jnp.where(row >= x0, h2 - tgt_ref[...], 0.0)
        dh2 = diff / d
        dh2_ref[...] = dh2
        dff_ref[...] = _rms_bwd(dh2 * g4, fh, r4).astype(BF16)
        _acc_rows(dg4_ref, dh2 * fh, i == 0)
        _acc_rows(loss_ref, diff * diff, i == 0)

    blk = pl.BlockSpec((tm, d), lambda i: (i, 0))
    res, _ = _host_call(
        body, grid=(t_rows // tm,),
        in_specs=[pl.BlockSpec((tm, f), lambda i: (i, 0)), _resident((f, d)), blk, blk, _full((1, d))],
        out_specs=[blk, blk, _full((1, d)), _full((1, d))],
        out_shape=[jax.ShapeDtypeStruct((t_rows, d), F32), jax.ShapeDtypeStruct((t_rows, d), BF16),
                   jax.ShapeDtypeStruct((1, d), F32), jax.ShapeDtypeStruct((1, d), F32)],
        args=(s, wd, h1, tgt, g4), name="down_loss")
    return res


def _bwd_down(dff, wd, a, u, tm, comm):
    t_rows, d = dff.shape
    f = wd.shape[0]

    def body(dff_ref, w_ref, a_ref, u_ref, da_ref, du_ref):
        dff_v = dff_ref[...]
        for o, n in _chunks(f, N_CHUNK):
            ds = _dot_nt(dff_v, w_ref[pl.ds(o, n), :])
            av = a_ref[:, pl.ds(o, n)].astype(F32)
            uv = u_ref[:, pl.ds(o, n)].astype(F32)
            sig = jax.nn.sigmoid(av)
            da_ref[:, pl.ds(o, n)] = (ds * uv * _silu_grad(av, sig)).astype(BF16)
            du_ref[:, pl.ds(o, n)] = (ds * (av * sig)).astype(BF16)

    blk = pl.BlockSpec((tm, f), lambda i: (i, 0))
    return _host_call(
        body, grid=(t_rows // tm,),
        in_specs=[pl.BlockSpec((tm, d), lambda i: (i, 0)), _resident((f, d)), blk, blk],
        out_specs=[blk, blk], out_shape=[jax.ShapeDtypeStruct((t_rows, f), BF16)] * 2,
        args=(dff, wd, a, u), name="bwd_down", comm=comm)


def _wgrad(a, b, name, comm=None):
    d = b.shape[1]
    t_rows = b.shape[0]
    stacked = a.ndim == 3
    n = a.shape[-1]
    groups = a.shape[0] if stacked else 1
    tiles = n // WGRAD_TILE

    def body(a_ref, b_ref, o_ref):
        o_ref[...] = lax.dot_general(a_ref[...], b_ref[...], (((0,), (0,)), ((), ())),
                                     preferred_element_type=F32).astype(BF16)

    if stacked:
        a_spec = pl.BlockSpec((None, t_rows, WGRAD_TILE), lambda g, i: (g, 0, i))
    else:
        a_spec = pl.BlockSpec((t_rows, WGRAD_TILE), lambda g, i: (0, i))
    res, extra = _host_call(
        body, grid=(groups, tiles), in_specs=[a_spec, pl.BlockSpec((t_rows, d), lambda g, i: (0, 0))],
        out_specs=[pl.BlockSpec((WGRAD_TILE, d), lambda g, i: (g * tiles + i, 0))],
        out_shape=[jax.ShapeDtypeStruct((groups * n, d), BF16)], args=(a, b), name=name, comm=comm)
    return res[0], extra


def _bwd_ffn_in(da, du, wg_t, wu_t, h1, dh2, g3, tm, comm):
    t_rows, d = h1.shape
    f = wg_t.shape[0]

    def body(da_ref, du_ref, wg_ref, wu_ref, h1_ref, dh2_ref, g3_ref, dh1_ref, dg3_ref):
        dxn2 = _dot_nn(da_ref[...], wg_ref[...]) + _dot_nn(du_ref[...], wu_ref[...])
        h1 = h1_ref[...]
        r3 = _rstd(h1)
        h1h = h1 * r3
        _acc_rows(dg3_ref, dxn2 * h1h, pl.program_id(0) == 0)
        dh1_ref[...] = dh2_ref[...] + _rms_bwd(dxn2 * g3_ref[...], h1h, r3)

    blk = pl.BlockSpec((tm, d), lambda i: (i, 0))
    blkf = pl.BlockSpec((tm, f), lambda i: (i, 0))
    return _host_call(
        body, grid=(t_rows // tm,),
        in_specs=[blkf, blkf, _resident((f, d)), _resident((f, d)), blk, blk, _full((1, d))],
        out_specs=[blk, _full((1, d))],
        out_shape=[jax.ShapeDtypeStruct((t_rows, d), F32), jax.ShapeDtypeStruct((1, d), F32)],
        args=(da, du, wg_t, wu_t, h1, dh2, g3), name="bwd_ffn_in", comm=comm)


def _bwd_out_proj(dh1, mix, w_out, g2, tm):
    t_rows, d = dh1.shape

    def body(dh1_ref, mix_ref, w_ref, g2_ref, dmix_ref, dy_ref, dg2_ref):
        mix = mix_ref[...]
        r2 = _rstd(mix)
        mh = mix * r2
        dh1 = dh1_ref[...]
        _acc_rows(dg2_ref, dh1 * mh, pl.program_id(0) == 0)
        dmix = _rms_bwd(dh1 * g2_ref[...], mh, r2).astype(BF16)
        dmix_ref[...] = dmix
        dy_ref[...] = _dot_nt(dmix, w_ref[...])

    blk = pl.BlockSpec((tm, d), lambda i: (i, 0))
    res, _ = _host_call(
        body, grid=(t_rows // tm,), in_specs=[blk, blk, _resident(w_out.shape), _full((1, d))],
        out_specs=[blk, blk, _full((1, d))],
        out_shape=[jax.ShapeDtypeStruct((t_rows, d), BF16), jax.ShapeDtypeStruct((t_rows, d), F32),
                   jax.ShapeDtypeStruct((1, d), F32)],
        args=(dh1, mix, w_out, g2), name="bwd_out_proj")
    return res


def _mix_ln_bwd(z, dy, lg, lb, tm):
    t_rows, w = z.shape

    def body(z_ref, dyb_ref, lg_ref, lb_ref, dz_ref, dlg_ref, dlb_ref, dbb_ref):
        first = pl.program_id(0) == 0
        lg = lg_ref[...]
        zh, rstd, ln = _ln_parts(z_ref[...], lg, lb_ref[...])
        dln = dyb_ref[...] * _silu_grad(ln, jax.nn.sigmoid(ln))
        _acc_rows(dlg_ref, dln * zh, first)
        _acc_rows(dlb_ref, dln, first)
        dzh = dln * lg
        dz = rstd * (dzh - jnp.mean(dzh, axis=-1, keepdims=True) - zh * jnp.mean(dzh * zh, axis=-1, keepdims=True))
        dz_ref[...] = dz
        _acc_rows(dbb_ref, dz, first)

    blk = pl.BlockSpec((tm, w), lambda i: (i, 0))
    vec = _full((1, w))
    res, _ = _host_call(
        body, grid=(t_rows // tm,), in_specs=[blk, pl.BlockSpec((tm, w), lambda i: (i, 1)), vec, vec],
        out_specs=[blk, vec, vec, vec],
        out_shape=[jax.ShapeDtypeStruct((t_rows, w), F32)] + [jax.ShapeDtypeStruct((1, w), F32)] * 3,
        args=(z, dy, lg, lb), name="mix_ln_bwd")
    return res


def _mix_conv_bwd(hin, dy, dz, wa, wb, wa_w, comm):
    t_rows = hin.shape[0]
    nt = wa_w // LANE
    ka, kb = wa.shape[0], wb.shape[0]
    nr = CONV_HALO + t_rows
    kb_rows = -(-kb // SUBLANE) * SUBLANE

    def body(bg_ref, cg_ref, ha_ref, val_ref, gt_ref, dya_ref, dz_ref, wa_ref, wb_ref,
             dh_ref, dwa_ref, dwb_ref, base, sh, based, shd, tmp, wbc):
        zeros = lambda n: jnp.zeros((n, LANE), F32)
        base[pl.ds(0, CONV_HALO), :] = zeros(CONV_HALO)
        base[pl.ds(nr, SUBLANE), :] = zeros(SUBLANE)
        based[pl.ds(t_rows, CONV_HALO + SUBLANE), :] = zeros(CONV_HALO + SUBLANE)

        def fwd_slot(k_taps, k):
            return _tap_slot(CONV_HALO - (k_taps - 1) + k)

        def bwd_slot(k_taps, k):
            return _tap_slot(k_taps - 1 - k)

        def conv(w_ref, k_taps, src, slot, b, n):
            acc = None
            for k in range(k_taps):
                r, q = slot(k_taps, k)
                term = w_ref[pl.ds(k, 1), :] * src[r, pl.ds(b + q, n), :]
                acc = term if acc is None else acc + term
            return acc

        def by_residue(k_taps, slot):
            groups = {}
            for k in range(k_taps):
                r, q = slot(k_taps, k)
                groups.setdefault(r, []).append((k, q // SUBLANE))
            return groups

        def wgrad_loop(w_ref, k_taps):
            n_sub = WGRAD_ROWS // SUBLANE
            for k in range(k_taps):
                wbc[k] = jnp.broadcast_to(w_ref[pl.ds(k, 1), :], (SUBLANE, LANE))
            fwd, bwd = by_residue(k_taps, fwd_slot), by_residue(k_taps, bwd_slot)

            def window(src, r, taps, b):
                span = n_sub + max(qi for _, qi in taps)
                return [src[r, pl.ds(b + SUBLANE * i, SUBLANE), :] for i in range(span)]

            def step(b, accs):
                accs = list(accs)
                dv = [based[pl.ds(b + SUBLANE * j, SUBLANE), :] for j in range(n_sub)]
                for r, taps in fwd.items():
                    win = window(sh, r, taps, b)
                    for k, qi in taps:
                        t = dv[0] * win[qi]
                        for j in range(1, n_sub):
                            t = t + dv[j] * win[qi + j]
                        accs[k] = accs[k] + t
                outs = [None] * n_sub
                for r, taps in bwd.items():
                    win = window(shd, r, taps, b)
                    for k, qi in taps:
                        wk = wbc[k]
                        for j in range(n_sub):
                            term = wk * win[qi + j]
                            outs[j] = term if outs[j] is None else outs[j] + term
                for j in range(n_sub):
                    tmp[pl.ds(b + SUBLANE * j, SUBLANE), :] = outs[j]
                return tuple(accs)

            return _row_loop(t_rows, WGRAD_ROWS, step, tuple(zeros(SUBLANE) for _ in range(k_taps)))

        def store_taps(ref, accs, rows):
            for k, acc in enumerate(accs):
                ref[pl.ds(k, 1), :] = jnp.sum(acc, axis=0, keepdims=True)
            if rows > len(accs):
                ref[pl.ds(len(accs), rows - len(accs)), :] = zeros(rows - len(accs))

        def fill_a(b, c):
            sl = pl.ds(b, CONV_CHUNK)
            base[pl.ds(CONV_HALO + b, CONV_CHUNK), :] = cg_ref[sl, :] * ha_ref[sl, :]
            based[sl, :] = dya_ref[sl, :] * bg_ref[sl, :]
            return c

        _row_loop(t_rows, CONV_CHUNK, fill_a)
        _fill_shifted(sh, base, sorted({fwd_slot(ka, k)[0] for k in range(ka)}), nr)
        _fill_shifted(shd, based, sorted({bwd_slot(ka, k)[0] for k in range(ka)}), nr)

        def d_bgate(b, c):
            sl = pl.ds(b, CONV_CHUNK)
            dh_ref[0, sl, :] = (dya_ref[sl, :] * conv(wa_ref, ka, sh, fwd_slot, b, CONV_CHUNK)).astype(BF16)
            return c

        _row_loop(t_rows, CONV_CHUNK, d_bgate)
        store_taps(dwa_ref, wgrad_loop(wa_ref, ka), SUBLANE)

        def d_ch(b, c):
            sl = pl.ds(b, CONV_CHUNK)
            dua = tmp[sl, :]
            dh_ref[1, sl, :] = (dua * ha_ref[sl, :]).astype(BF16)
            dh_ref[2, sl, :] = (dua * cg_ref[sl, :]).astype(BF16)
            return c

        _row_loop(t_rows, CONV_CHUNK, d_ch)

        def fill_b(b, c):
            sl = pl.ds(b, CONV_CHUNK)
            base[pl.ds(CONV_HALO + b, CONV_CHUNK), :] = val_ref[sl, :] * jax.nn.sigmoid(gt_ref[sl, :])
            based[sl, :] = dz_ref[sl, :]
            return c

        _row_loop(t_rows, CONV_CHUNK, fill_b)
        _fill_shifted(sh, base, range(SUBLANE), nr)
        _fill_shifted(shd, based, range(SUBLANE), nr)
        store_taps(dwb_ref, wgrad_loop(wb_ref, kb), kb_rows)

        def d_glu(b, c):
            sl = pl.ds(b, CONV_CHUNK)
            dgg = tmp[sl, :]
            sig = jax.nn.sigmoid(gt_ref[sl, :])
            dh_ref[3, sl, :] = (dgg * sig).astype(BF16)
            dh_ref[4, sl, :] = (dgg * val_ref[sl, :] * (sig * (1.0 - sig))).astype(BF16)
            return c

        _row_loop(t_rows, CONV_CHUNK, d_glu)

    def col(g):
        return pl.BlockSpec((t_rows, LANE), lambda i, g=g: (0, g * nt + i))

    tile = lambda rows: pl.BlockSpec((rows, LANE), lambda i: (0, i))
    return _host_call(
        body, grid=(nt,),
        in_specs=[col(0), col(1), col(2), col(3), col(4), tile(t_rows), tile(t_rows), tile(ka), tile(kb)],
        out_specs=[pl.BlockSpec((5, t_rows, LANE), lambda i: (0, 0, i)), tile(SUBLANE), tile(kb_rows)],
        out_shape=[jax.ShapeDtypeStruct((5, t_rows, wa_w), BF16), jax.ShapeDtypeStruct((SUBLANE, wa_w), F32),
                   jax.ShapeDtypeStruct((kb_rows, wa_w), F32)],
        scratch_shapes=[pltpu.VMEM((nr + SUBLANE, LANE), F32), pltpu.VMEM((SUBLANE, nr, LANE), F32),
                        pltpu.VMEM((nr + SUBLANE, LANE), F32), pltpu.VMEM((SUBLANE, nr, LANE), F32),
                        pltpu.VMEM((t_rows, LANE), F32), pltpu.VMEM((kb_rows, SUBLANE, LANE), F32)],
        args=(hin, hin, hin, hin, hin, dy, dz, wa, wb), name="mix_conv_bwd", comm=comm)


def _bwd_in_proj(dh5, win_t, h0, dh1, g1, tm, comm):
    t_rows, d = h0.shape
    groups, _, w = dh5.shape

    def body(dh_ref, w_ref, h0_ref, dh1_ref, g1_ref, dh0_ref, dg1_ref):
        dxn1 = None
        for g in range(groups):
            part = _dot_nn(dh_ref[g], w_ref[pl.ds(g * w, w), :])
            dxn1 = part if dxn1 is None else dxn1 + part
        h0 = h0_ref[...]
        r1 = _rstd(h0)
        h0h = h0 * r1
        _acc_rows(dg1_ref, dxn1 * h0h, pl.program_id(0) == 0)
        dh0_ref[...] = dh1_ref[...] + _rms_bwd(dxn1 * g1_ref[...], h0h, r1)

    blk = pl.BlockSpec((tm, d), lambda i: (i, 0))
    return _host_call(
        body, grid=(t_rows // tm,),
        in_specs=[pl.BlockSpec((groups, tm, w), lambda i: (0, i, 0)), _resident(win_t.shape), blk, blk, _full((1, d))],
        out_specs=[blk, _full((1, d))],
        out_shape=[jax.ShapeDtypeStruct((t_rows, d), F32), jax.ShapeDtypeStruct((1, d), F32)],
        args=(dh5, win_t, h0, dh1, g1), name="bwd_in_proj", comm=comm)


def _reduce_small(smalls, d, comm):
    (dmeta, dg1, dg2, dg3, dg4, dbb, dlg, dlb, lossv, dwa, dwb) = smalls
    half = d // 2
    kb_rows = dwb.shape[0]

    def body(dmeta_ref, dg1_ref, dg2_ref, dg3_ref, dg4_ref, dbb_ref, dlg_ref, dlb_ref, loss_ref, dwa_ref, dwb_ref,
             ptot_ref, pbuf, psib, chip_p, ps_send, ps_recv, pc_send, pc_recv):
        x, y, c = _mesh_pos()
        pbuf[...] = jnp.zeros_like(pbuf)
        pbuf[pl.ds(0, N_META), :] = dmeta_ref[...]
        for row, ref in ((16, dg1_ref), (17, dg2_ref), (18, dg3_ref), (19, dg4_ref)):
            pbuf[pl.ds(row, 1), :] = ref[...]
        pbuf[pl.ds(20, 1), pl.ds(0, half)] = dbb_ref[...]
        pbuf[pl.ds(20, 1), pl.ds(half, half)] = dlg_ref[...]
        pbuf[pl.ds(21, 1), pl.ds(0, half)] = dlb_ref[...]
        lv = loss_ref[...]
        pbuf[pl.ds(21, 1), pl.ds(half, half)] = lv[:, :half] + lv[:, half:]
        pbuf[pl.ds(24, SUBLANE), pl.ds(0, half)] = dwa_ref[...]
        pbuf[pl.ds(32, kb_rows), pl.ds(0, half)] = dwb_ref[...]
        to_sib = _remote(pbuf, psib, ps_send.at[0], ps_recv.at[0], (x, y, 1 - c))
        to_sib.start()
        to_sib.wait_recv()
        my_chip = 2 * x + y
        chip_p[my_chip] = pbuf[...] + psib[...]
        to_sib.wait_send()
        slot = chip_p.at[my_chip]
        cps = [_remote(slot, slot, pc_send.at[k], pc_recv.at[k], (*chip, c)) for k, chip in enumerate(_other_chips(x, y))]
        for cp in cps:
            cp.start()
        for cp in cps:
            cp.wait_recv()
        ptot_ref[...] = ((chip_p[0] + chip_p[1]) + chip_p[2]) + chip_p[3]
        for cp in cps:
            cp.wait_send()

    return _host_call(
        body, grid=(), in_specs=[VMEM] * 11, out_specs=[VMEM], out_shape=[jax.ShapeDtypeStruct((SMALL_ROWS, d), F32)],
        scratch_shapes=[pltpu.VMEM((SMALL_ROWS, d), F32), pltpu.VMEM((SMALL_ROWS, d), F32), pltpu.VMEM((4, SMALL_ROWS, d), F32),
                        pltpu.SemaphoreType.DMA((1,)), pltpu.SemaphoreType.DMA((1,)),
                        pltpu.SemaphoreType.DMA((3,)), pltpu.SemaphoreType.DMA((3,))],
        args=smalls, name="reduce_small", comm=comm)


def _adamw(w, g, m, v):
    m = ADAM_B1 * m + (1.0 - ADAM_B1) * g
    v = ADAM_B2 * v + (1.0 - ADAM_B2) * jnp.square(g)
    m_hat = m / (1.0 - ADAM_B1 ** ADAM_STEP)
    v_hat = v / (1.0 - ADAM_B2 ** ADAM_STEP)
    delta = -ADAM_LR * (m_hat / (jnp.sqrt(v_hat) + ADAM_EPS) + ADAM_WD * w)
    return delta, m, v


def _adam_big(g, pair, part, w, m, v, name):
    r, d = w.shape

    def body(g_ref, pair_ref, part_ref, w_ref, m_ref, v_ref, go_ref, d_ref, mo_ref, vo_ref, gbuf, abuf, pbuf, sems):
        j = _dev_index(*_mesh_pos())
        loads = [pltpu.make_async_copy(g_ref.at[pl.ds(pl.multiple_of(j * r, BF16_ROWS), r), :], gbuf, sems.at[0]),
                 pltpu.make_async_copy(pair_ref.at[0], abuf, sems.at[1]),
                 pltpu.make_async_copy(part_ref, pbuf, sems.at[2])]
        for cp in loads:
            cp.start()
        for cp in loads:
            cp.wait()
        g = gbuf[...].astype(F32) + abuf[...].astype(F32)
        for k in range(3):
            g = g + pbuf[k].astype(F32)
        go_ref[...] = g
        d_ref[...], mo_ref[...], vo_ref[...] = _adamw(w_ref[...], g, m_ref[...], v_ref[...])

    return pl.pallas_call(
        body, out_shape=[jax.ShapeDtypeStruct((r, d), F32)] * 4, in_specs=[ANY, ANY, ANY, VMEM, VMEM, VMEM],
        scratch_shapes=[pltpu.VMEM((r, d), BF16), pltpu.VMEM((r, d), BF16), pltpu.VMEM((3, r, d), BF16),
                        pltpu.SemaphoreType.DMA((3,))],
        name=name, compiler_params=_cparams())(g, pair, part, w, m, v)


def _adam_small(gs, ws, ms, vs):
    n = len(gs)

    def body(*refs):
        ins, outs = refs[:4 * n], refs[4 * n:]
        for i in range(n):
            g = ins[i][...]
            delta, m, v = _adamw(ins[n + i][...], g, ins[2 * n + i][...], ins[3 * n + i][...])
            outs[i][...] = delta
            outs[n + i][...] = m
            outs[2 * n + i][...] = v

    shapes = [jax.ShapeDtypeStruct(w.shape, F32) for w in ws]
    return pl.pallas_call(body, out_shape=shapes * 3, name="adam_small", compiler_params=_cparams())(*gs, *ws, *ms, *vs)


def kernel(x, meta_tokens, pre_mix_norm, w_in, conv_a_w, conv_b_w, conv_b_bias, ln_b_gain, ln_b_bias, w_out, post_mix_norm, pre_ffn_norm, w_gate, w_up, w_down, post_ffn_norm, loss_target, m_meta_tokens, m_pre_mix_norm, m_w_in, m_conv_a_w, m_conv_b_w, m_conv_b_bias, m_ln_b_gain, m_ln_b_bias, m_w_out, m_post_mix_norm, m_pre_ffn_norm, m_w_gate, m_w_up, m_w_down, m_post_ffn_norm, v_meta_tokens, v_pre_mix_norm, v_w_in, v_conv_a_w, v_conv_b_w, v_conv_b_bias, v_ln_b_gain, v_ln_b_bias, v_w_out, v_post_mix_norm, v_pre_ffn_norm, v_w_gate, v_w_up, v_w_down, v_post_ffn_norm):
    _, seq, d = x.shape
    ka, ca_loc = conv_a_w.shape[1:]
    kb, cb_loc = conv_b_w.shape[1:]
    wa_w = ca_loc * N_DEV
    assert cb_loc == ca_loc and wa_w % LANE == 0 and w_in.shape[2] * N_DEV == 5 * wa_w
    pad = (-(N_META + seq)) % ROW_ALIGN
    x0 = pad + N_META
    t_rows = x0 + seq
    assert t_rows % (N_ROW_BLOCKS * BF16_ROWS) == 0 and t_rows % CONV_CHUNK == 0 and d % LANE == 0
    tm = t_rows // N_ROW_BLOCKS
    me = _dev_index(*_mesh_pos())

    def as_rows(w_in_like, w_out_like, w_gate_like, w_up_like, w_down_like):
        return (w_in_like[0].T, w_out_like[0], w_gate_like[0].T, w_up_like[0].T, w_down_like[0])

    w_loc = as_rows(w_in, w_out, w_gate, w_up, w_down)
    rows = [w.shape[0] for w in w_loc]
    assert all(r % ADD_CHUNK == 0 for r in rows)
    P_IN, P_OUT, P_GATE, P_UP, P_DOWN = range(N_BIG)
    gate_cut = (rows[P_GATE] * 3 // 8) // BF16_ROWS * BF16_ROWS
    up_cut = (rows[P_UP] // 4) // BF16_ROWS * BF16_ROWS
    jobs_first = [(P_IN, 0, rows[P_IN])]
    jobs_in_proj = [(P_OUT, 0, rows[P_OUT]), (P_GATE, 0, gate_cut)]
    jobs_conv = [(P_GATE, gate_cut, rows[P_GATE] - gate_cut), (P_UP, 0, up_cut)]
    jobs_out_proj = [(P_UP, up_cut, rows[P_UP] - up_cut)]
    jobs_gate_up = [(P_DOWN, 0, rows[P_DOWN])]

    sm = jnp.zeros((SM_ROWS, LANE), F32)
    sm = sm.at[0:N_META, :].set(meta_tokens)
    sm = sm.at[16:16 + ka, 0:ca_loc].set(conv_a_w[0])
    sm = sm.at[24:24 + kb, 0:cb_loc].set(conv_b_w[0])
    wl, wfull, sm_all = _gather_first(w_loc, sm, jobs_first)
    meta_full = jnp.transpose(sm_all[:, 0:N_META, :], (1, 0, 2)).reshape(N_META, d)
    wa = jnp.transpose(sm_all[:, 16:16 + ka, 0:ca_loc], (1, 0, 2)).reshape(ka, wa_w)
    wb = jnp.transpose(sm_all[:, 24:24 + kb, 0:cb_loc], (1, 0, 2)).reshape(kb, wa_w)

    def gather(jobs):
        return _gather_comm(wl, {p: wfull[p] for p in sorted({j[0] for j in jobs})}, jobs, rows)

    def gathered(jobs, extra):
        for p, arr in zip(sorted({j[0] for j in jobs}), extra):
            wfull[p] = arr

    h0, tgt = _prep(x[0], loss_target[0], meta_full, t_rows, x0)
    (xn1, hin), extra = _in_proj(h0, pre_mix_norm, wfull[P_IN], tm, gather(jobs_in_proj))
    gathered(jobs_in_proj, extra)
    (ya, z), extra = _mix_conv_fwd(hin, wa, wb, conv_b_bias, wa_w, gather(jobs_conv))
    gathered(jobs_conv, extra)
    y = _mix_ln_fwd(ya, z, ln_b_gain, ln_b_bias, tm)
    (mix, h1, xn2), extra = _out_proj(y, wfull[P_OUT], h0, post_mix_norm, pre_ffn_norm, tm, gather(jobs_out_proj))
    gathered(jobs_out_proj, extra)
    (a, u, s), extra = _gate_up(xn2, wfull[P_GATE], wfull[P_UP], tm, gather(jobs_gate_up))
    gathered(jobs_gate_up, extra)
    dh2, dff, dg4, lossv = _down_loss(s, wfull[P_DOWN], h1, tgt, post_ffn_norm, tm, x0)

    gwd, _ = _wgrad(s, dff, "wgrad_down")
    (da, du), (pair_d,) = _bwd_down(dff, wfull[P_DOWN], a, u, tm, _pair_comm(gwd, rows[P_DOWN]))
    gwg, _ = _wgrad(da, xn2, "wgrad_gate")
    gwu, _ = _wgrad(du, xn2, "wgrad_up")
    (dh1, dg3), (part_d, pair_g, pair_u) = _bwd_ffn_in(
        da, du, wfull[P_GATE], wfull[P_UP], h1, dh2, pre_ffn_norm, tm,
        _merge_comms([_chip_comm(gwd, pair_d, rows[P_DOWN]), _pair_comm(gwg, rows[P_GATE]), _pair_comm(gwu, rows[P_UP])]))
    dmix, dy, dg2 = _bwd_out_proj(dh1, mix, wfull[P_OUT], post_mix_norm, tm)
    gwo, _ = _wgrad(y, dmix, "wgrad_out")
    dz, dlg, dlb, dbb = _mix_ln_bwd(z, dy, ln_b_gain, ln_b_bias, tm)
    (dh5, dwa, dwb), (part_g, part_u, pair_o) = _mix_conv_bwd(
        hin, dy, dz, wa, wb, wa_w,
        _merge_comms([_chip_comm(gwg, pair_g, rows[P_GATE]), _chip_comm(gwu, pair_u, rows[P_UP]), _pair_comm(gwo, rows[P_OUT])]))
    gwi, (part_o,) = _wgrad(dh5, xn1, "wgrad_in", _chip_comm(gwo, pair_o, rows[P_OUT]))
    (dh0, dg1), (pair_i, part_i) = _bwd_in_proj(dh5, wfull[P_IN], h0, dh1, pre_mix_norm, tm, _pair_chip_comm(gwi, rows[P_IN]))
    grad_x = dh0[x0:][None]
    dmeta = dh0[x0 - N_META:x0]
    (ptot,), _ = _reduce_small((dmeta, dg1, dg2, dg3, dg4, dbb, dlg, dlb, lossv, dwa, dwb), d, None)

    half = d // 2
    loss = (0.5 / d) * jnp.sum(ptot[21, half:])
    g_meta = lax.dynamic_slice(ptot, (0, me * (d // N_DEV)), (N_META, d // N_DEV))
    g_small = [g_meta, ptot[16:17], lax.dynamic_slice(ptot, (24, me * ca_loc), (ka, ca_loc))[None],
               lax.dynamic_slice(ptot, (32, me * cb_loc), (kb, cb_loc))[None],
               ptot[20:21, :half], ptot[20:21, half:], ptot[21:22, :half], ptot[17:18], ptot[18:19], ptot[19:20]]
    w_small = [meta_tokens, pre_mix_norm, conv_a_w, conv_b_w, conv_b_bias, ln_b_gain, ln_b_bias, post_mix_norm,
               pre_ffn_norm, post_ffn_norm]
    m_small = [m_meta_tokens, m_pre_mix_norm, m_conv_a_w, m_conv_b_w, m_conv_b_bias, m_ln_b_gain, m_ln_b_bias,
               m_post_mix_norm, m_pre_ffn_norm, m_post_ffn_norm]
    v_small = [v_meta_tokens, v_pre_mix_norm, v_conv_a_w, v_conv_b_w, v_conv_b_bias, v_ln_b_gain, v_ln_b_bias,
               v_post_mix_norm, v_pre_ffn_norm, v_post_ffn_norm]
    small = _adam_small(g_small, w_small, m_small, v_small)
    n_small = len(w_small)
    d_small, nm_small, nv_small = small[:n_small], small[n_small:2 * n_small], small[2 * n_small:]

    m_loc = as_rows(m_w_in, m_w_out, m_w_gate, m_w_up, m_w_down)
    v_loc = as_rows(v_w_in, v_w_out, v_w_gate, v_w_up, v_w_down)
    full_grads = (gwi, gwo, gwg, gwu, gwd)
    pairs = (pair_i, pair_o, pair_g, pair_u, pair_d)
    parts = (part_i, part_o, part_g, part_u, part_d)
    bigs = {}
    for p, (name, tr) in enumerate((("w_in", True), ("w_out", False), ("w_gate", True), ("w_up", True), ("w_down", False))):
        res = _adam_big(full_grads[p], pairs[p], parts[p], w_loc[p], m_loc[p], v_loc[p], "adam_" + name)
        bigs[name] = [(o.T if tr else o)[None] for o in res]

    def ordered(pick_small, pick_big):
        sm_it = iter(range(n_small))
        out = []
        for name in ("s", "s", "w_in", "s", "s", "s", "s", "s", "w_out", "s", "s", "w_gate", "w_up", "w_down", "s"):
            out.append(pick_small(next(sm_it)) if name == "s" else pick_big(name))
        return out

    grads = ordered(lambda i: g_small[i], lambda n: bigs[n][0])
    deltas = ordered(lambda i: d_small[i], lambda n: bigs[n][1])
    new_m = ordered(lambda i: nm_small[i], lambda n: bigs[n][2])
    new_v = ordered(lambda i: nv_small[i], lambda n: bigs[n][3])
    return (loss, grad_x, *grads, *deltas, *new_m, *new_v)
```

```python
import jax
import jax.numpy as jnp
from jax import lax
from jax.experimental import pallas as pl
from jax.experimental.pallas import tpu as pltpu

F32 = jnp.float32
BF16 = jnp.bfloat16
MESH = pl.DeviceIdType.MESH

N_META = 16
N_DEV = 8
RMS_EPS = 1e-6
LN_EPS = 1e-5
ADAM_LR = 0.001
ADAM_B1 = 0.9
ADAM_B2 = 0.999
ADAM_EPS = 1e-08
ADAM_WD = 0.01
ADAM_STEP = 10

LANE = 128
SUBLANE = 8
BF16_ROWS = 16
ROW_ALIGN = 128
N_ROW_BLOCKS = 4
CONV_HALO = 32
CONV_CHUNK = 64
WGRAD_ROWS = 32
N_CHUNK = 512
WGRAD_TILE = 256
ADD_CHUNK = 32
ADAM_COL_BLOCKS = 4
V7X_VMEM_BYTES = 64 * 1024 * 1024
VMEM_LIMIT = V7X_VMEM_BYTES - 6 * 1024 * 1024
SMALL_ROWS = 64
SM_ROWS = 56
N_BIG = 5

ANY = pl.BlockSpec(memory_space=pl.ANY)
VMEM = pl.BlockSpec(memory_space=pltpu.VMEM)


def _cparams(n_grid_axes=0):
    sem = ("arbitrary",) * n_grid_axes if n_grid_axes else None
    return pltpu.CompilerParams(dimension_semantics=sem, vmem_limit_bytes=VMEM_LIMIT)


def _mesh_pos():
    return lax.axis_index("x"), lax.axis_index("y"), lax.axis_index("c")


def _dev_index(px, py, pc):
    return 4 * px + 2 * py + pc


def _other_chips(x, y):
    return [(1 - x, y), (x, 1 - y), (1 - x, 1 - y)]


def _full(shape):
    return pl.BlockSpec(shape, lambda *_: (0,) * len(shape))


def _resident(shape):
    return pl.BlockSpec(shape, lambda *_: (0,) * len(shape), pipeline_mode=pl.Buffered(1))


def _dot_nt(a, w):
    return lax.dot_general(a, w, (((1,), (1,)), ((), ())), preferred_element_type=F32)


def _dot_nn(a, w):
    return jnp.dot(a, w, preferred_element_type=F32)


def _chunks(n, c):
    out, o = [], 0
    while o < n:
        out.append((o, min(c, n - o)))
        o += c
    return out


def _rstd(h):
    return lax.rsqrt(jnp.mean(h * h, axis=-1, keepdims=True) + RMS_EPS)


def _rms_bwd(dyh, yh, r):
    return r * (dyh - yh * jnp.mean(dyh * yh, axis=-1, keepdims=True))


def _silu_grad(a, sig):
    return sig * (1.0 + a * (1.0 - sig))


def _acc_rows(ref, val, first):
    s = jnp.sum(val, axis=0, keepdims=True)

    @pl.when(first)
    def _():
        ref[...] = s

    @pl.when(jnp.logical_not(first))
    def _():
        ref[...] += s


def _row_loop(t_rows, chunk, fn, carry=None):
    def step(i, c):
        return fn(pl.multiple_of(i * chunk, chunk), c)

    return lax.fori_loop(0, t_rows // chunk, step, carry)


def _remote(src, dst, send_sem, recv_sem, to):
    return pltpu.make_async_remote_copy(src_ref=src, dst_ref=dst, send_sem=send_sem, recv_sem=recv_sem,
                                        device_id=to, device_id_type=MESH)


class _Comm:
    def __init__(self, inputs, out_shapes, aliases, scratch, start, finish, mid=None):
        self.inputs, self.out_shapes, self.aliases, self.scratch = list(inputs), list(out_shapes), dict(aliases), list(scratch)
        self.start, self.finish, self.mid = start, finish, mid


def _merge_comms(comms):
    inputs, out_shapes, aliases, scratch, spans = [], [], {}, [], []
    for cm in comms:
        spans.append((len(inputs), len(out_shapes), len(scratch), cm))
        aliases.update({len(inputs) + k: len(out_shapes) + v for k, v in cm.aliases.items()})
        inputs += cm.inputs
        out_shapes += cm.out_shapes
        scratch += cm.scratch

    def run(which):
        def fn(ins, outs, scr):
            for i0, o0, s0, cm in spans:
                if getattr(cm, which) is not None:
                    getattr(cm, which)(ins[i0:i0 + len(cm.inputs)], outs[o0:o0 + len(cm.out_shapes)], scr[s0:s0 + len(cm.scratch)])
        return fn

    mid = run("mid") if any(cm.mid is not None for cm in comms) else None
    return _Comm(inputs, out_shapes, aliases, scratch, run("start"), run("finish"), mid)


def _host_call(body, *, grid, in_specs, out_specs, out_shape, args, name, scratch_shapes=(), comm=None):
    if comm is None:
        res = pl.pallas_call(body, grid=grid, in_specs=list(in_specs), out_specs=list(out_specs), out_shape=list(out_shape),
                             scratch_shapes=list(scratch_shapes), name=name, compiler_params=_cparams(len(grid)))(*args)
        return list(res), []
    n_in, n_out, n_scr = len(args), len(out_shape), len(scratch_shapes)
    c_in, c_out = len(comm.inputs), len(comm.out_shapes)

    def hosted(*refs):
        ins, c_ins = refs[:n_in], refs[n_in:n_in + c_in]
        o0 = n_in + c_in
        outs, c_outs = refs[o0:o0 + n_out], refs[o0 + n_out:o0 + n_out + c_out]
        s0 = o0 + n_out + c_out
        scr, c_scr = refs[s0:s0 + n_scr], refs[s0 + n_scr:]
        if not grid:
            comm.start(c_ins, c_outs, c_scr)
            body(*ins, *outs, *scr)
            comm.finish(c_ins, c_outs, c_scr)
            return
        first = last = None
        for a, n in enumerate(grid):
            f, l = pl.program_id(a) == 0, pl.program_id(a) == n - 1
            first = f if first is None else jnp.logical_and(first, f)
            last = l if last is None else jnp.logical_and(last, l)

        @pl.when(first)
        def _():
            comm.start(c_ins, c_outs, c_scr)

        if comm.mid is not None:
            assert len(grid) == 1 and grid[0] >= 3

            @pl.when(pl.program_id(0) == 1)
            def _():
                comm.mid(c_ins, c_outs, c_scr)

        body(*ins, *outs, *scr)

        @pl.when(last)
        def _():
            comm.finish(c_ins, c_outs, c_scr)

    res = pl.pallas_call(
        hosted, grid=grid, in_specs=list(in_specs) + [ANY] * c_in, out_specs=list(out_specs) + [ANY] * c_out,
        out_shape=list(out_shape) + comm.out_shapes, scratch_shapes=list(scratch_shapes) + comm.scratch,
        input_output_aliases={n_in + k: n_out + v for k, v in comm.aliases.items()},
        name=name, compiler_params=_cparams(len(grid)))(*args, *comm.inputs)
    return list(res[:n_out]), list(res[n_out:])


def _gather_jobs(x, y, c, jobs, rows, lo, src_ref, dests, send_sems, recv_sems):
    sibling = (x, y, 1 - c)
    chips = _other_chips(x, y)
    first, forward = [], []
    for n, (p, r0, nr) in enumerate(jobs):
        def rows_of(px, py, pc, p=p, r0=r0, nr=nr):
            return dests[p].at[pl.ds(pl.multiple_of(_dev_index(px, py, pc) * rows[p] + r0, BF16_ROWS), nr), :]

        src = src_ref.at[pl.ds(lo[p] + r0, nr), :]
        sem = lambda k, n=n: (send_sems.at[7 * n + k], recv_sems.at[7 * n + k])
        first.append([_remote(src, rows_of(x, y, c), *sem(0), sibling)]
                     + [_remote(src, rows_of(x, y, c), *sem(1 + j), (*chip, c)) for j, chip in enumerate(chips)])
        forward.append([_remote(rows_of(*chip, c), rows_of(*chip, c), *sem(4 + j), sibling) for j, chip in enumerate(chips)])
    return first, forward


def _gather_start(first):
    for cps in first:
        for cp in cps:
            cp.start()


def _gather_finish(first, forward):
    for cps, fws in zip(first, forward):
        for j in range(3):
            cps[1 + j].wait_recv()
            fws[j].start()
    for cps, fws in zip(first, forward):
        cps[0].wait_recv()
        for fw in fws:
            fw.wait_recv()
        for cp in cps + fws:
            cp.wait_send()


def _gather_comm(wl, dests, jobs, rows):
    lo = [sum(rows[:p]) for p in range(N_BIG)]
    ps = sorted(dests)
    slot = {p: i for i, p in enumerate(ps)}

    def descs(ins, outs, scr):
        x, y, c = _mesh_pos()
        return _gather_jobs(x, y, c, jobs, rows, lo, ins[0], {p: outs[slot[p]] for p in ps}, scr[0], scr[1])

    def start(ins, outs, scr):
        _gather_start(descs(ins, outs, scr)[0])

    def finish(ins, outs, scr):
        _gather_finish(*descs(ins, outs, scr))

    n_sems = 7 * len(jobs)
    return _Comm([wl] + [dests[p] for p in ps], [jax.ShapeDtypeStruct(dests[p].shape, BF16) for p in ps],
                 {1 + i: i for i in range(len(ps))},
                 [pltpu.SemaphoreType.DMA((n_sems,)), pltpu.SemaphoreType.DMA((n_sems,))], start, finish)


def _pair_comm(g, r):
    d = g.shape[1]

    def descs(ins, outs, scr):
        x, y, c = _mesh_pos()
        chips = [(x, y)] + _other_chips(x, y)
        return [_remote(ins[0].at[pl.ds(pl.multiple_of(_dev_index(*chip, 1 - c) * r, BF16_ROWS), r), :], outs[0].at[k],
                        scr[0].at[k], scr[1].at[k], (x, y, 1 - c)) for k, chip in enumerate(chips)]

    def start(ins, outs, scr):
        for cp in descs(ins, outs, scr):
            cp.start()

    def finish(ins, outs, scr):
        cps = descs(ins, outs, scr)
        for cp in cps:
            cp.wait_recv()
        for cp in cps:
            cp.wait_send()

    comm = _Comm([g], [jax.ShapeDtypeStruct((4, r, d), BF16)], {},
                 [pltpu.SemaphoreType.DMA((4,)), pltpu.SemaphoreType.DMA((4,))], start, finish)
    comm.descs = descs
    return comm


def _chip_comm(g, pair, r):
    d = g.shape[1]
    return _Comm([g, pair], [jax.ShapeDtypeStruct((3, r, d), BF16)], {}, _chip_scratch(r, d),
                 lambda ins, outs, scr: _chip_send(ins[0], ins[1], outs[0], scr, r),
                 lambda ins, outs, scr: _chip_wait(outs[0], scr))


def _chip_scratch(r, d):
    return [pltpu.VMEM((r, d), BF16), pltpu.VMEM((r, d), BF16), pltpu.VMEM((3, r, d), BF16),
            pltpu.SemaphoreType.DMA((3,)), pltpu.SemaphoreType.DMA((3,)), pltpu.SemaphoreType.DMA((2,))]


def _chip_copies(part_ref, scr):
    x, y, c = _mesh_pos()
    return [_remote(scr[2].at[k], part_ref.at[k], scr[3].at[k], scr[4].at[k], (*chip, c))
            for k, chip in enumerate(_other_chips(x, y))]


def _chip_send(g_ref, pair_ref, part_ref, scr, r):
    x, y, c = _mesh_pos()
    gbuf, abuf, sendbuf, local = scr[0], scr[1], scr[2], scr[5]
    cps = _chip_copies(part_ref, scr)
    for k, chip in enumerate(_other_chips(x, y)):
        j = _dev_index(*chip, c)
        loads = [pltpu.make_async_copy(g_ref.at[pl.ds(pl.multiple_of(j * r, BF16_ROWS), r), :], gbuf, local.at[0]),
                 pltpu.make_async_copy(pair_ref.at[1 + k], abuf, local.at[1])]
        for cp in loads:
            cp.start()
        for cp in loads:
            cp.wait()

        def add(b, carry, k=k):
            sl = pl.ds(b, ADD_CHUNK)
            sendbuf[k, sl, :] = (gbuf[sl, :].astype(F32) + abuf[sl, :].astype(F32)).astype(BF16)
            return carry

        _row_loop(r, ADD_CHUNK, add)
        cps[k].start()


def _chip_wait(part_ref, scr):
    cps = _chip_copies(part_ref, scr)
    for cp in cps:
        cp.wait_recv()
    for cp in cps:
        cp.wait_send()


def _pair_chip_comm(g, r):
    d = g.shape[1]
    pair = _pair_comm(g, r)
    n_pair = len(pair.scratch)

    def mid(ins, outs, scr):
        for cp in pair.descs(ins, outs[:1], scr[:n_pair]):
            cp.wait_recv()
        _chip_send(ins[0], outs[0], outs[1], scr[n_pair:], r)

    def finish(ins, outs, scr):
        _chip_wait(outs[1], scr[n_pair:])
        for cp in pair.descs(ins, outs[:1], scr[:n_pair]):
            cp.wait_send()

    return _Comm([g], pair.out_shapes + [jax.ShapeDtypeStruct((3, r, d), BF16)], {}, pair.scratch + _chip_scratch(r, d),
                 lambda ins, outs, scr: pair.start(ins, outs[:1], scr[:n_pair]), finish, mid)


def _gather_first(shards, sm, jobs):
    d = shards[0].shape[1]
    rows = [w.shape[0] for w in shards]
    lo = [sum(rows[:p]) for p in range(N_BIG)]
    n_sems = 7 * len(jobs)

    def body(s0, s1, s2, s3, s4, sm_ref, wl_ref, o0, o1, o2, o3, o4, sa_ref, wl_v, send_sems, recv_sems, ssend, srecv, local_sems):
        dests = (o0, o1, o2, o3, o4)
        x, y, c = _mesh_pos()
        me = (x, y, c)
        jme = _dev_index(*me)
        for p, ref in enumerate((s0, s1, s2, s3, s4)):
            wl_v[pl.ds(lo[p], rows[p]), :] = ref[...].astype(BF16)
        first, forward = _gather_jobs(x, y, c, jobs, rows, lo, wl_v, dict(enumerate(dests)), send_sems, recv_sems)
        _gather_start(first)
        peers = [(x, y, 1 - c)] + [(*chip, pc) for pc in (c, 1 - c) for chip in _other_chips(x, y)]
        smalls = [_remote(sm_ref, sa_ref.at[jme], ssend.at[k], srecv.at[k], to) for k, to in enumerate(peers)]
        for cp in smalls:
            cp.start()
        mine = [pltpu.make_async_copy(wl_v.at[pl.ds(lo[p], rows[p]), :],
                                      dests[p].at[pl.ds(pl.multiple_of(jme * rows[p], BF16_ROWS), rows[p]), :], local_sems.at[p])
                for p in range(N_BIG)]
        mine.append(pltpu.make_async_copy(wl_v, wl_ref, local_sems.at[N_BIG]))
        mine.append(pltpu.make_async_copy(sm_ref, sa_ref.at[jme], local_sems.at[N_BIG + 1]))
        for cp in mine:
            cp.start()
        _gather_finish(first, forward)
        for cp in smalls:
            cp.wait_recv()
        for cp in smalls:
            cp.wait_send()
        for cp in mine:
            cp.wait()

    out_shape = [jax.ShapeDtypeStruct((sum(rows), d), BF16)]
    out_shape += [jax.ShapeDtypeStruct((N_DEV * r, d), BF16) for r in rows]
    out_shape.append(jax.ShapeDtypeStruct((N_DEV,) + sm.shape, F32))
    res = pl.pallas_call(
        body, out_shape=out_shape, in_specs=[VMEM] * 6, out_specs=[ANY] * 7,
        scratch_shapes=[pltpu.VMEM((sum(rows), d), BF16), pltpu.SemaphoreType.DMA((n_sems,)), pltpu.SemaphoreType.DMA((n_sems,)),
                        pltpu.SemaphoreType.DMA((7,)), pltpu.SemaphoreType.DMA((7,)), pltpu.SemaphoreType.DMA((N_BIG + 2,))],
        name="gather_first", compiler_params=_cparams())(*shards, sm)
    return res[0], list(res[1:1 + N_BIG]), res[-1]


def _prep(x2, tgt2, meta_full, t_rows, x0, comm):
    s, d = x2.shape
    assert x0 == ROW_ALIGN and s % ROW_ALIGN == 0

    def body(x_ref, tgt_ref, meta_ref, h0_ref, tp_ref):
        i = pl.program_id(0)

        @pl.when(i == 0)
        def _():
            h0_ref[...] = jnp.zeros_like(h0_ref)
            h0_ref[pl.ds(x0 - N_META, N_META), :] = meta_ref[...]
            tp_ref[...] = jnp.zeros_like(tp_ref)

        @pl.when(i > 0)
        def _():
            h0_ref[...] = x_ref[...]
            tp_ref[...] = tgt_ref[...]

    src = pl.BlockSpec((ROW_ALIGN, d), lambda i: (jnp.maximum(i - 1, 0), 0))
    dst = pl.BlockSpec((ROW_ALIGN, d), lambda i: (i, 0))
    return _host_call(body, grid=(t_rows // ROW_ALIGN,), in_specs=[src, src, _full((N_META, d))], out_specs=[dst, dst],
                      out_shape=[jax.ShapeDtypeStruct((t_rows, d), F32)] * 2, args=(x2, tgt2, meta_full), name="prep", comm=comm)


def _in_proj(h0, g1, win_t, tm, comm):
    t_rows, d = h0.shape
    e = win_t.shape[0]

    def body(h_ref, g_ref, w_ref, xn_ref, hin_ref):
        h = h_ref[...]
        xn = ((h * _rstd(h)) * g_ref[...]).astype(BF16)
        xn_ref[...] = xn
        for o, n in _chunks(e, N_CHUNK):
            hin_ref[:, pl.ds(o, n)] = _dot_nt(xn, w_ref[pl.ds(o, n), :])

    return _host_call(
        body, grid=(t_rows // tm,),
        in_specs=[pl.BlockSpec((tm, d), lambda i: (i, 0)), _full((1, d)), _resident((e, d))],
        out_specs=[pl.BlockSpec((tm, d), lambda i: (i, 0)), pl.BlockSpec((tm, e), lambda i: (i, 0))],
        out_shape=[jax.ShapeDtypeStruct((t_rows, d), BF16), jax.ShapeDtypeStruct((t_rows, e), F32)],
        args=(h0, g1, win_t), name="in_proj", comm=comm)


def _tap_slot(off):
    return off % SUBLANE, (off // SUBLANE) * SUBLANE


def _fill_shifted(sh_ref, base_ref, residues, n_rows):
    for r in residues:
        sh_ref[r] = base_ref[pl.ds(r, n_rows), :]


def _mix_conv_fwd(hin, wa, wb, bb, wa_w, comm):
    t_rows = hin.shape[0]
    nt = wa_w // LANE
    ka, kb = wa.shape[0], wb.shape[0]
    nr = CONV_HALO + t_rows

    def body(bg_ref, cg_ref, ha_ref, val_ref, gt_ref, wa_ref, wb_ref, bb_ref, ya_ref, z_ref, base, sh):
        base[pl.ds(0, CONV_HALO), :] = jnp.zeros((CONV_HALO, LANE), F32)
        base[pl.ds(nr, SUBLANE), :] = jnp.zeros((SUBLANE, LANE), F32)

        def conv(w_ref, k_taps, b, n):
            acc = None
            for k in range(k_taps):
                r, q = _tap_slot(CONV_HALO - (k_taps - 1) + k)
                term = w_ref[pl.ds(k, 1), :] * sh[r, pl.ds(b + q, n), :]
                acc = term if acc is None else acc + term
            return acc

        def fill_a(b, c):
            base[pl.ds(CONV_HALO + b, CONV_CHUNK), :] = cg_ref[pl.ds(b, CONV_CHUNK), :] * ha_ref[pl.ds(b, CONV_CHUNK), :]
            return c

        _row_loop(t_rows, CONV_CHUNK, fill_a)
        _fill_shifted(sh, base, sorted({_tap_slot(CONV_HALO - (ka - 1) + k)[0] for k in range(ka)}), nr)

        def out_a(b, c):
            ya_ref[pl.ds(b, CONV_CHUNK), :] = (bg_ref[pl.ds(b, CONV_CHUNK), :] * conv(wa_ref, ka, b, CONV_CHUNK)).astype(BF16)
            return c

        _row_loop(t_rows, CONV_CHUNK, out_a)

        def fill_b(b, c):
            base[pl.ds(CONV_HALO + b, CONV_CHUNK), :] = (val_ref[pl.ds(b, CONV_CHUNK), :]
                                                          * jax.nn.sigmoid(gt_ref[pl.ds(b, CONV_CHUNK), :]))
            return c

        _row_loop(t_rows, CONV_CHUNK, fill_b)
        _fill_shifted(sh, base, range(SUBLANE), nr)

        def out_b(b, c):
            z_ref[pl.ds(b, CONV_CHUNK), :] = conv(wb_ref, kb, b, CONV_CHUNK) + bb_ref[...]
            return c

        _row_loop(t_rows, CONV_CHUNK, out_b)

    def col(g):
        return pl.BlockSpec((t_rows, LANE), lambda i, g=g: (0, g * nt + i))

    tile = lambda rows: pl.BlockSpec((rows, LANE), lambda i: (0, i))
    return _host_call(
        body, grid=(nt,),
        in_specs=[col(0), col(1), col(2), col(3), col(4), tile(ka), tile(kb), tile(1)],
        out_specs=[tile(t_rows), tile(t_rows)],
        out_shape=[jax.ShapeDtypeStruct((t_rows, wa_w), BF16), jax.ShapeDtypeStruct((t_rows, wa_w), F32)],
        scratch_shapes=[pltpu.VMEM((nr + SUBLANE, LANE), F32), pltpu.VMEM((SUBLANE, nr, LANE), F32)],
        args=(hin, hin, hin, hin, hin, wa, wb, bb), name="mix_conv_fwd", comm=comm)


def _ln_parts(z, lg, lb):
    mu = jnp.mean(z, axis=-1, keepdims=True)
    zc = z - mu
    rstd = lax.rsqrt(jnp.mean(zc * zc, axis=-1, keepdims=True) + LN_EPS)
    zh = zc * rstd
    return zh, rstd, zh * lg + lb


def _mix_ln_fwd(ya, z, lg, lb, tm):
    t_rows, w = z.shape

    def body(ya_ref, z_ref, lg_ref, lb_ref, y_ref):
        _, _, ln = _ln_parts(z_ref[...], lg_ref[...], lb_ref[...])
        y_ref[:, pl.ds(0, w)] = ya_ref[...]
        y_ref[:, pl.ds(w, w)] = (ln * jax.nn.sigmoid(ln)).astype(BF16)

    blk = pl.BlockSpec((tm, w), lambda i: (i, 0))
    res, _ = _host_call(body, grid=(t_rows // tm,), in_specs=[blk, blk, _full((1, w)), _full((1, w))],
                        out_specs=[pl.BlockSpec((tm, 2 * w), lambda i: (i, 0))],
                        out_shape=[jax.ShapeDtypeStruct((t_rows, 2 * w), BF16)], args=(ya, z, lg, lb), name="mix_ln_fwd")
    return res[0]


def _out_proj(y, w_out, h0, g2, g3, tm, comm):
    t_rows, d = h0.shape

    def body(y_ref, w_ref, h0_ref, g2_ref, g3_ref, mix_ref, h1_ref, xn2_ref):
        mix = _dot_nn(y_ref[...], w_ref[...])
        mix_ref[...] = mix
        h1 = h0_ref[...] + (mix * _rstd(mix)) * g2_ref[...]
        h1_ref[...] = h1
        xn2_ref[...] = ((h1 * _rstd(h1)) * g3_ref[...]).astype(BF16)

    blk = pl.BlockSpec((tm, d), lambda i: (i, 0))
    return _host_call(
        body, grid=(t_rows // tm,), in_specs=[blk, _resident(w_out.shape), blk, _full((1, d)), _full((1, d))],
        out_specs=[blk, blk, blk],
        out_shape=[jax.ShapeDtypeStruct((t_rows, d), F32), jax.ShapeDtypeStruct((t_rows, d), F32),
                   jax.ShapeDtypeStruct((t_rows, d), BF16)],
        args=(y, w_out, h0, g2, g3), name="out_proj", comm=comm)


def _gate_up(xn2, wg_t, wu_t, tm, comm):
    t_rows, d = xn2.shape
    f = wg_t.shape[0]

    def body(x_ref, wg_ref, wu_ref, a_ref, u_ref, s_ref):
        xn = x_ref[...]
        for o, n in _chunks(f, N_CHUNK):
            a = _dot_nt(xn, wg_ref[pl.ds(o, n), :])
            u = _dot_nt(xn, wu_ref[pl.ds(o, n), :])
            a_ref[:, pl.ds(o, n)] = a.astype(BF16)
            u_ref[:, pl.ds(o, n)] = u.astype(BF16)
            s_ref[:, pl.ds(o, n)] = ((a * jax.nn.sigmoid(a)) * u).astype(BF16)

    blk = pl.BlockSpec((tm, f), lambda i: (i, 0))
    return _host_call(
        body, grid=(t_rows // tm,),
        in_specs=[pl.BlockSpec((tm, d), lambda i: (i, 0)), _resident((f, d)), _resident((f, d))],
        out_specs=[blk, blk, blk], out_shape=[jax.ShapeDtypeStruct((t_rows, f), BF16)] * 3,
        args=(xn2, wg_t, wu_t), name="gate_up", comm=comm)


def _down_loss(s, wd, h1, tgt, g4, tm, x0):
    t_rows, d = h1.shape
    f = wd.shape[0]

    def body(s_ref, w_ref, h1_ref, tgt_ref, g4_ref, dh2_ref, dff_ref, dg4_ref, loss_ref):
        i = pl.program_id(0)
        ff = _dot_nn(s_ref[...], w_ref[...])
        r4 = _rstd(ff)
        fh = ff * r4
        g4 = g4_ref[...]
        h2 = h1_ref[...] + fh * g4
        row = i * tm + lax.broadcasted_iota(jnp.int32, (tm, 1), 0)
        diff = jnp.where(row >= x0, h2 - tgt_ref[...], 0.0)
        dh2 = diff / d
        dh2_ref[...] = dh2
        dff_ref[...] = _rms_bwd(dh2 * g4, fh, r4).astype(BF16)
        _acc_rows(dg4_ref, dh2 * fh, i == 0)
        _acc_rows(loss_ref, diff * diff, i == 0)

    blk = pl.BlockSpec((tm, d), lambda i: (i, 0))
    res, _ = _host_call(
        body, grid=(t_rows // tm,),
        in_specs=[pl.BlockSpec((tm, f), lambda i: (i, 0)), _resident((f, d)), blk, blk, _full((1, d))],
        out_specs=[blk, blk, _full((1, d)), _full((1, d))],
        out_shape=[jax.ShapeDtypeStruct((t_rows, d), F32), jax.ShapeDtypeStruct((t_rows, d), BF16),
                   jax.ShapeDtypeStruct((1, d), F32), jax.ShapeDtypeStruct((1, d), F32)],
        args=(s, wd, h1, tgt, g4), name="down_loss")
    return res


def _bwd_down(dff, wd, a, u, tm, comm):
    t_rows, d = dff.shape
    f = wd.shape[0]

    def body(dff_ref, w_ref, a_ref, u_ref, da_ref, du_ref):
        dff_v = dff_ref[...]
        for o, n in _chunks(f, N_CHUNK):
            ds = _dot_nt(dff_v, w_ref[pl.ds(o, n), :])
            av = a_ref[:, pl.ds(o, n)].astype(F32)
            uv = u_ref[:, pl.ds(o, n)].astype(F32)
            sig = jax.nn.sigmoid(av)
            da_ref[:, pl.ds(o, n)] = (ds * uv * _silu_grad(av, sig)).astype(BF16)
            du_ref[:, pl.ds(o, n)] = (ds * (av * sig)).astype(BF16)

    blk = pl.BlockSpec((tm, f), lambda i: (i, 0))
    return _host_call(
        body, grid=(t_rows // tm,),
        in_specs=[pl.BlockSpec((tm, d), lambda i: (i, 0)), _resident((f, d)), blk, blk],
        out_specs=[blk, blk], out_shape=[jax.ShapeDtypeStruct((t_rows, f), BF16)] * 2,
        args=(dff, wd, a, u), name="bwd_down", comm=comm)


def _wgrad(a, b, name, comm=None):
    d = b.shape[1]
    t_rows = b.shape[0]
    stacked = a.ndim == 3
    n = a.shape[-1]
    groups = a.shape[0] if stacked else 1
    tiles = n // WGRAD_TILE

    def body(a_ref, b_ref, o_ref):
        o_ref[...] = lax.dot_general(a_ref[...], b_ref[...], (((0,), (0,)), ((), ())),
                                     preferred_element_type=F32).astype(BF16)

    if stacked:
        a_spec = pl.BlockSpec((None, t_rows, WGRAD_TILE), lambda g, i: (g, 0, i))
    else:
        a_spec = pl.BlockSpec((t_rows, WGRAD_TILE), lambda g, i: (0, i))
    res, extra = _host_call(
        body, grid=(groups, tiles), in_specs=[a_spec, pl.BlockSpec((t_rows, d), lambda g, i: (0, 0))],
        out_specs=[pl.BlockSpec((WGRAD_TILE, d), lambda g, i: (g * tiles + i, 0))],
        out_shape=[jax.ShapeDtypeStruct((groups * n, d), BF16)], args=(a, b), name=name, comm=comm)
    return res[0], extra


def _bwd_ffn_in(da, du, wg_t, wu_t, h1, dh2, g3, tm, comm):
    t_rows, d = h1.shape
    f = wg_t.shape[0]

    def body(da_ref, du_ref, wg_ref, wu_ref, h1_ref, dh2_ref, g3_ref, dh1_ref, dg3_ref):
        dxn2 = _dot_nn(da_ref[...], wg_ref[...]) + _dot_nn(du_ref[...], wu_ref[...])
        h1 = h1_ref[...]
        r3 = _rstd(h1)
        h1h = h1 * r3
        _acc_rows(dg3_ref, dxn2 * h1h, pl.program_id(0) == 0)
        dh1_ref[...] = dh2_ref[...] + _rms_bwd(dxn2 * g3_ref[...], h1h, r3)

    blk = pl.BlockSpec((tm, d), lambda i: (i, 0))
    blkf = pl.BlockSpec((tm, f), lambda i: (i, 0))
    return _host_call(
        body, grid=(t_rows // tm,),
        in_specs=[blkf, blkf, _resident((f, d)), _resident((f, d)), blk, blk, _full((1, d))],
        out_specs=[blk, _full((1, d))],
        out_shape=[jax.ShapeDtypeStruct((t_rows, d), F32), jax.ShapeDtypeStruct((1, d), F32)],
        args=(da, du, wg_t, wu_t, h1, dh2, g3), name="bwd_ffn_in", comm=comm)


def _bwd_out_proj(dh1, mix, w_out, g2, tm):
    t_rows, d = dh1.shape

    def body(dh1_ref, mix_ref, w_ref, g2_ref, dmix_ref, dy_ref, dg2_ref):
        mix = mix_ref[...]
        r2 = _rstd(mix)
        mh = mix * r2
        dh1 = dh1_ref[...]
        _acc_rows(dg2_ref, dh1 * mh, pl.program_id(0) == 0)
        dmix = _rms_bwd(dh1 * g2_ref[...], mh, r2).astype(BF16)
        dmix_ref[...] = dmix
        dy_ref[...] = _dot_nt(dmix, w_ref[...])

    blk = pl.BlockSpec((tm, d), lambda i: (i, 0))
    res, _ = _host_call(
        body, grid=(t_rows // tm,), in_specs=[blk, blk, _resident(w_out.shape), _full((1, d))],
        out_specs=[blk, blk, _full((1, d))],
        out_shape=[jax.ShapeDtypeStruct((t_rows, d), BF16), jax.ShapeDtypeStruct((t_rows, d), F32),
                   jax.ShapeDtypeStruct((1, d), F32)],
        args=(dh1, mix, w_out, g2), name="bwd_out_proj")
    return res


def _mix_ln_bwd(z, dy, lg, lb, tm):
    t_rows, w = z.shape

    def body(z_ref, dyb_ref, lg_ref, lb_ref, dz_ref, dlg_ref, dlb_ref, dbb_ref):
        first = pl.program_id(0) == 0
        lg = lg_ref[...]
        zh, rstd, ln = _ln_parts(z_ref[...], lg, lb_ref[...])
        dln = dyb_ref[...] * _silu_grad(ln, jax.nn.sigmoid(ln))
        _acc_rows(dlg_ref, dln * zh, first)
        _acc_rows(dlb_ref, dln, first)
        dzh = dln * lg
        dz = rstd * (dzh - jnp.mean(dzh, axis=-1, keepdims=True) - zh * jnp.mean(dzh * zh, axis=-1, keepdims=True))
        dz_ref[...] = dz
        _acc_rows(dbb_ref, dz, first)

    blk = pl.BlockSpec((tm, w), lambda i: (i, 0))
    vec = _full((1, w))
    res, _ = _host_call(
        body, grid=(t_rows // tm,), in_specs=[blk, pl.BlockSpec((tm, w), lambda i: (i, 1)), vec, vec],
        out_specs=[blk, vec, vec, vec],
        out_shape=[jax.ShapeDtypeStruct((t_rows, w), F32)] + [jax.ShapeDtypeStruct((1, w), F32)] * 3,
        args=(z, dy, lg, lb), name="mix_ln_bwd")
    return res


def _mix_conv_bwd(hin, dy, dz, wa, wb, wa_w, comm):
    t_rows = hin.shape[0]
    nt = wa_w // LANE
    ka, kb = wa.shape[0], wb.shape[0]
    nr = CONV_HALO + t_rows
    kb_rows = -(-kb // SUBLANE) * SUBLANE

    def body(bg_ref, cg_ref, ha_ref, val_ref, gt_ref, dya_ref, dz_ref, wa_ref, wb_ref,
             dh_ref, dwa_ref, dwb_ref, base, sh, based, shd, tmp, wbc):
        zeros = lambda n: jnp.zeros((n, LANE), F32)
        base[pl.ds(0, CONV_HALO), :] = zeros(CONV_HALO)
        base[pl.ds(nr, SUBLANE), :] = zeros(SUBLANE)
        based[pl.ds(t_rows, CONV_HALO + SUBLANE), :] = zeros(CONV_HALO + SUBLANE)

        def fwd_slot(k_taps, k):
            return _tap_slot(CONV_HALO - (k_taps - 1) + k)

        def bwd_slot(k_taps, k):
            return _tap_slot(k_taps - 1 - k)

        def conv(w_ref, k_taps, src, slot, b, n):
            acc = None
            for k in range(k_taps):
                r, q = slot(k_taps, k)
                term = w_ref[pl.ds(k, 1), :] * src[r, pl.ds(b + q, n), :]
                acc = term if acc is None else acc + term
            return acc

        def by_residue(k_taps, slot):
            groups = {}
            for k in range(k_taps):
                r, q = slot(k_taps, k)
                groups.setdefault(r, []).append((k, q // SUBLANE))
            return groups

        def wgrad_loop(w_ref, k_taps):
            n_sub = WGRAD_ROWS // SUBLANE
            for k in range(k_taps):
                wbc[k] = jnp.broadcast_to(w_ref[pl.ds(k, 1), :], (SUBLANE, LANE))
            fwd, bwd = by_residue(k_taps, fwd_slot), by_residue(k_taps, bwd_slot)

            def window(src, r, taps, b):
                span = n_sub + max(qi for _, qi in taps)
                return [src[r, pl.ds(b + SUBLANE * i, SUBLANE), :] for i in range(span)]

            def step(b, accs):
                accs = list(accs)
                dv = [based[pl.ds(b + SUBLANE * j, SUBLANE), :] for j in range(n_sub)]
                for r, taps in fwd.items():
                    win = window(sh, r, taps, b)
                    for k, qi in taps:
                        t = dv[0] * win[qi]
                        for j in range(1, n_sub):
                            t = t + dv[j] * win[qi + j]
                        accs[k] = accs[k] + t
                outs = [None] * n_sub
                for r, taps in bwd.items():
                    win = window(shd, r, taps, b)
                    for k, qi in taps:
                        wk = wbc[k]
                        for j in range(n_sub):
                            term = wk * win[qi + j]
                            outs[j] = term if outs[j] is None else outs[j] + term
                for j in range(n_sub):
                    tmp[pl.ds(b + SUBLANE * j, SUBLANE), :] = outs[j]
                return tuple(accs)

            return _row_loop(t_rows, WGRAD_ROWS, step, tuple(zeros(SUBLANE) for _ in range(k_taps)))

        def store_taps(ref, accs, rows):
            for k, acc in enumerate(accs):
                ref[pl.ds(k, 1), :] = jnp.sum(acc, axis=0, keepdims=True)
            if rows > len(accs):
                ref[pl.ds(len(accs), rows - len(accs)), :] = zeros(rows - len(accs))

        def fill_a(b, c):
            sl = pl.ds(b, CONV_CHUNK)
            base[pl.ds(CONV_HALO + b, CONV_CHUNK), :] = cg_ref[sl, :] * ha_ref[sl, :]
            based[sl, :] = dya_ref[sl, :] * bg_ref[sl, :]
            return c

        _row_loop(t_rows, CONV_CHUNK, fill_a)
        _fill_shifted(sh, base, sorted({fwd_slot(ka, k)[0] for k in range(ka)}), nr)
        _fill_shifted(shd, based, sorted({bwd_slot(ka, k)[0] for k in range(ka)}), nr)

        def d_bgate(b, c):
            sl = pl.ds(b, CONV_CHUNK)
            dh_ref[0, sl, :] = (dya_ref[sl, :] * conv(wa_ref, ka, sh, fwd_slot, b, CONV_CHUNK)).astype(BF16)
            return c

        _row_loop(t_rows, CONV_CHUNK, d_bgate)
        store_taps(dwa_ref, wgrad_loop(wa_ref, ka), SUBLANE)

        def d_ch(b, c):
            sl = pl.ds(b, CONV_CHUNK)
            dua = tmp[sl, :]
            dh_ref[1, sl, :] = (dua * ha_ref[sl, :]).astype(BF16)
            dh_ref[2, sl, :] = (dua * cg_ref[sl, :]).astype(BF16)
            return c

        _row_loop(t_rows, CONV_CHUNK, d_ch)

        def fill_b(b, c):
            sl = pl.ds(b, CONV_CHUNK)
            base[pl.ds(CONV_HALO + b, CONV_CHUNK), :] = val_ref[sl, :] * jax.nn.sigmoid(gt_ref[sl, :])
            based[sl, :] = dz_ref[sl, :]
            return c

        _row_loop(t_rows, CONV_CHUNK, fill_b)
        _fill_shifted(sh, base, range(SUBLANE), nr)
        _fill_shifted(shd, based, range(SUBLANE), nr)
        store_taps(dwb_ref, wgrad_loop(wb_ref, kb), kb_rows)

        def d_glu(b, c):
            sl = pl.ds(b, CONV_CHUNK)
            dgg = tmp[sl, :]
            sig = jax.nn.sigmoid(gt_ref[sl, :])
            dh_ref[3, sl, :] = (dgg * sig).astype(BF16)
            dh_ref[4, sl, :] = (dgg * val_ref[sl, :] * (sig * (1.0 - sig))).astype(BF16)
            return c

        _row_loop(t_rows, CONV_CHUNK, d_glu)

    def col(g):
        return pl.BlockSpec((t_rows, LANE), lambda i, g=g: (0, g * nt + i))

    tile = lambda rows: pl.BlockSpec((rows, LANE), lambda i: (0, i))
    return _host_call(
        body, grid=(nt,),
        in_specs=[col(0), col(1), col(2), col(3), col(4), tile(t_rows), tile(t_rows), tile(ka), tile(kb)],
        out_specs=[pl.BlockSpec((5, t_rows, LANE), lambda i: (0, 0, i)), tile(SUBLANE), tile(kb_rows)],
        out_shape=[jax.ShapeDtypeStruct((5, t_rows, wa_w), BF16), jax.ShapeDtypeStruct((SUBLANE, wa_w), F32),
                   jax.ShapeDtypeStruct((kb_rows, wa_w), F32)],
        scratch_shapes=[pltpu.VMEM((nr + SUBLANE, LANE), F32), pltpu.VMEM((SUBLANE, nr, LANE), F32),
                        pltpu.VMEM((nr + SUBLANE, LANE), F32), pltpu.VMEM((SUBLANE, nr, LANE), F32),
                        pltpu.VMEM((t_rows, LANE), F32), pltpu.VMEM((kb_rows, SUBLANE, LANE), F32)],
        args=(hin, hin, hin, hin, hin, dy, dz, wa, wb), name="mix_conv_bwd", comm=comm)


def _bwd_in_proj(dh5, win_t, h0, dh1, g1, tm, comm):
    t_rows, d = h0.shape
    groups, _, w = dh5.shape

    def body(dh_ref, w_ref, h0_ref, dh1_ref, g1_ref, dh0_ref, dg1_ref):
        dxn1 = None
        for g in range(groups):
            part = _dot_nn(dh_ref[g], w_ref[pl.ds(g * w, w), :])
            dxn1 = part if dxn1 is None else dxn1 + part
        h0 = h0_ref[...]
        r1 = _rstd(h0)
        h0h = h0 * r1
        _acc_rows(dg1_ref, dxn1 * h0h, pl.program_id(0) == 0)
        dh0_ref[...] = dh1_ref[...] + _rms_bwd(dxn1 * g1_ref[...], h0h, r1)

    blk = pl.BlockSpec((tm, d), lambda i: (i, 0))
    return _host_call(
        body, grid=(t_rows // tm,),
        in_specs=[pl.BlockSpec((groups, tm, w), lambda i: (0, i, 0)), _resident(win_t.shape), blk, blk, _full((1, d))],
        out_specs=[blk, _full((1, d))],
        out_shape=[jax.ShapeDtypeStruct((t_rows, d), F32), jax.ShapeDtypeStruct((1, d), F32)],
        args=(dh5, win_t, h0, dh1, g1), name="bwd_in_proj", comm=comm)


def _reduce_small(smalls, d, comm):
    (dmeta, dg1, dg2, dg3, dg4, dbb, dlg, dlb, lossv, dwa, dwb) = smalls
    half = d // 2
    kb_rows = dwb.shape[0]

    def body(dmeta_ref, dg1_ref, dg2_ref, dg3_ref, dg4_ref, dbb_ref, dlg_ref, dlb_ref, loss_ref, dwa_ref, dwb_ref,
             ptot_ref, pbuf, psib, chip_p, ps_send, ps_recv, pc_send, pc_recv):
        x, y, c = _mesh_pos()
        pbuf[...] = jnp.zeros_like(pbuf)
        pbuf[pl.ds(0, N_META), :] = dmeta_ref[...]
        for row, ref in ((16, dg1_ref), (17, dg2_ref), (18, dg3_ref), (19, dg4_ref)):
            pbuf[pl.ds(row, 1), :] = ref[...]
        pbuf[pl.ds(20, 1), pl.ds(0, half)] = dbb_ref[...]
        pbuf[pl.ds(20, 1), pl.ds(half, half)] = dlg_ref[...]
        pbuf[pl.ds(21, 1), pl.ds(0, half)] = dlb_ref[...]
        lv = loss_ref[...]
        pbuf[pl.ds(21, 1), pl.ds(half, half)] = lv[:, :half] + lv[:, half:]
        pbuf[pl.ds(24, SUBLANE), pl.ds(0, half)] = dwa_ref[...]
        pbuf[pl.ds(32, kb_rows), pl.ds(0, half)] = dwb_ref[...]
        to_sib = _remote(pbuf, psib, ps_send.at[0], ps_recv.at[0], (x, y, 1 - c))
        to_sib.start()
        to_sib.wait_recv()
        my_chip = 2 * x + y
        chip_p[my_chip] = pbuf[...] + psib[...]
        to_sib.wait_send()
        slot = chip_p.at[my_chip]
        cps = [_remote(slot, slot, pc_send.at[k], pc_recv.at[k], (*chip, c)) for k, chip in enumerate(_other_chips(x, y))]
        for cp in cps:
            cp.start()
        for cp in cps:
            cp.wait_recv()
        ptot_ref[...] = ((chip_p[0] + chip_p[1]) + chip_p[2]) + chip_p[3]
        for cp in cps:
            cp.wait_send()

    return _host_call(
        body, grid=(), in_specs=[VMEM] * 11, out_specs=[VMEM], out_shape=[jax.ShapeDtypeStruct((SMALL_ROWS, d), F32)],
        scratch_shapes=[pltpu.VMEM((SMALL_ROWS, d), F32), pltpu.VMEM((SMALL_ROWS, d), F32), pltpu.VMEM((4, SMALL_ROWS, d), F32),
                        pltpu.SemaphoreType.DMA((1,)), pltpu.SemaphoreType.DMA((1,)),
                        pltpu.SemaphoreType.DMA((3,)), pltpu.SemaphoreType.DMA((3,))],
        args=smalls, name="reduce_small", comm=comm)


def _adamw(w, g, m, v):
    m = ADAM_B1 * m + (1.0 - ADAM_B1) * g
    v = ADAM_B2 * v + (1.0 - ADAM_B2) * jnp.square(g)
    m_hat = m / (1.0 - ADAM_B1 ** ADAM_STEP)
    v_hat = v / (1.0 - ADAM_B2 ** ADAM_STEP)
    delta = -ADAM_LR * (m_hat / (jnp.sqrt(v_hat) + ADAM_EPS) + ADAM_WD * w)
    return delta, m, v


def _adam_big(g, pair, part, w, m, v, name):
    r, d = w.shape
    cols = d // ADAM_COL_BLOCKS

    def body(me_ref, g_ref, pair_ref, part_ref, w_ref, m_ref, v_ref, go_ref, d_ref, mo_ref, vo_ref):
        g = g_ref[...].astype(F32) + pair_ref[...].astype(F32)
        for k in range(3):
            g = g + part_ref[k].astype(F32)
        go_ref[...] = g
        d_ref[...], mo_ref[...], vo_ref[...] = _adamw(w_ref[...], g, m_ref[...], v_ref[...])

    blk = pl.BlockSpec((r, cols), lambda i, me_ref: (0, i))
    grid_spec = pltpu.PrefetchScalarGridSpec(
        num_scalar_prefetch=1, grid=(ADAM_COL_BLOCKS,),
        in_specs=[pl.BlockSpec((r, cols), lambda i, me_ref: (me_ref[0], i)),
                  pl.BlockSpec((None, r, cols), lambda i, me_ref: (0, 0, i)),
                  pl.BlockSpec((3, r, cols), lambda i, me_ref: (0, 0, i)), blk, blk, blk],
        out_specs=[blk, blk, blk, blk])
    me = jnp.reshape(_dev_index(*_mesh_pos()), (1,)).astype(jnp.int32)
    return pl.pallas_call(body, out_shape=[jax.ShapeDtypeStruct((r, d), F32)] * 4, grid_spec=grid_spec, name=name,
                          compiler_params=_cparams(1))(me, g, pair, part, w, m, v)


def _adam_small(gs, ws, ms, vs):
    n = len(gs)

    def body(*refs):
        ins, outs = refs[:4 * n], refs[4 * n:]
        for i in range(n):
            g = ins[i][...]
            delta, m, v = _adamw(ins[n + i][...], g, ins[2 * n + i][...], ins[3 * n + i][...])
            outs[i][...] = delta
            outs[n + i][...] = m
            outs[2 * n + i][...] = v

    shapes = [jax.ShapeDtypeStruct(w.shape, F32) for w in ws]
    return pl.pallas_call(body, out_shape=shapes * 3, name="adam_small", compiler_params=_cparams())(*gs, *ws, *ms, *vs)


def kernel(x, meta_tokens, pre_mix_norm, w_in, conv_a_w, conv_b_w, conv_b_bias, ln_b_gain, ln_b_bias, w_out, post_mix_norm, pre_ffn_norm, w_gate, w_up, w_down, post_ffn_norm, loss_target, m_meta_tokens, m_pre_mix_norm, m_w_in, m_conv_a_w, m_conv_b_w, m_conv_b_bias, m_ln_b_gain, m_ln_b_bias, m_w_out, m_post_mix_norm, m_pre_ffn_norm, m_w_gate, m_w_up, m_w_down, m_post_ffn_norm, v_meta_tokens, v_pre_mix_norm, v_w_in, v_conv_a_w, v_conv_b_w, v_conv_b_bias, v_ln_b_gain, v_ln_b_bias, v_w_out, v_post_mix_norm, v_pre_ffn_norm, v_w_gate, v_w_up, v_w_down, v_post_ffn_norm):
    _, seq, d = x.shape
    ka, ca_loc = conv_a_w.shape[1:]
    kb, cb_loc = conv_b_w.shape[1:]
    wa_w = ca_loc * N_DEV
    assert cb_loc == ca_loc and wa_w % LANE == 0 and w_in.shape[2] * N_DEV == 5 * wa_w
    pad = (-(N_META + seq)) % ROW_ALIGN
    x0 = pad + N_META
    t_rows = x0 + seq
    assert t_rows % (N_ROW_BLOCKS * BF16_ROWS) == 0 and t_rows % CONV_CHUNK == 0 and d % LANE == 0
    tm = t_rows // N_ROW_BLOCKS
    me = _dev_index(*_mesh_pos())

    def as_rows(w_in_like, w_out_like, w_gate_like, w_up_like, w_down_like):
        return (w_in_like[0].T, w_out_like[0], w_gate_like[0].T, w_up_like[0].T, w_down_like[0])

    w_loc = as_rows(w_in, w_out, w_gate, w_up, w_down)
    rows = [w.shape[0] for w in w_loc]
    assert all(r % ADD_CHUNK == 0 for r in rows)
    P_IN, P_OUT, P_GATE, P_UP, P_DOWN = range(N_BIG)
    gate_cut = (rows[P_GATE] // 2) // BF16_ROWS * BF16_ROWS
    up_cut = (rows[P_UP] // 5) // BF16_ROWS * BF16_ROWS
    jobs_first = [(P_IN, 0, rows[P_IN])]
    jobs_prep = [(P_OUT, 0, rows[P_OUT])]
    jobs_in_proj = [(P_GATE, 0, gate_cut)]
    jobs_conv = [(P_GATE, gate_cut, rows[P_GATE] - gate_cut), (P_UP, 0, up_cut)]
    jobs_out_proj = [(P_UP, up_cut, rows[P_UP] - up_cut)]
    jobs_gate_up = [(P_DOWN, 0, rows[P_DOWN])]

    sm = jnp.zeros((SM_ROWS, LANE), F32)
    sm = sm.at[0:N_META, :].set(meta_tokens)
    sm = sm.at[16:16 + ka, 0:ca_loc].set(conv_a_w[0])
    sm = sm.at[24:24 + kb, 0:cb_loc].set(conv_b_w[0])
    wl, wfull, sm_all = _gather_first(w_loc, sm, jobs_first)
    meta_full = jnp.transpose(sm_all[:, 0:N_META, :], (1, 0, 2)).reshape(N_META, d)
    wa = jnp.transpose(sm_all[:, 16:16 + ka, 0:ca_loc], (1, 0, 2)).reshape(ka, wa_w)
    wb = jnp.transpose(sm_all[:, 24:24 + kb, 0:cb_loc], (1, 0, 2)).reshape(kb, wa_w)

    def gather(jobs):
        return _gather_comm(wl, {p: wfull[p] for p in sorted({j[0] for j in jobs})}, jobs, rows)

    def gathered(jobs, extra):
        for p, arr in zip(sorted({j[0] for j in jobs}), extra):
            wfull[p] = arr

    (h0, tgt), extra = _prep(x[0], loss_target[0], meta_full, t_rows, x0, gather(jobs_prep))
    gathered(jobs_prep, extra)
    (xn1, hin), extra = _in_proj(h0, pre_mix_norm, wfull[P_IN], tm, gather(jobs_in_proj))
    gathered(jobs_in_proj, extra)
    (ya, z), extra = _mix_conv_fwd(hin, wa, wb, conv_b_bias, wa_w, gather(jobs_conv))
    gathered(jobs_conv, extra)
    y = _mix_ln_fwd(ya, z, ln_b_gain, ln_b_bias, tm)
    (mix, h1, xn2), extra = _out_proj(y, wfull[P_OUT], h0, post_mix_norm, pre_ffn_norm, tm, gather(jobs_out_proj))
    gathered(jobs_out_proj, extra)
    (a, u, s), extra = _gate_up(xn2, wfull[P_GATE], wfull[P_UP], tm, gather(jobs_gate_up))
    gathered(jobs_gate_up, extra)
    dh2, dff, dg4, lossv = _down_loss(s, wfull[P_DOWN], h1, tgt, post_ffn_norm, tm, x0)

    gwd, _ = _wgrad(s, dff, "wgrad_down")
    (da, du), (pair_d,) = _bwd_down(dff, wfull[P_DOWN], a, u, tm, _pair_comm(gwd, rows[P_DOWN]))
    gwg, _ = _wgrad(da, xn2, "wgrad_gate")
    gwu, _ = _wgrad(du, xn2, "wgrad_up")
    (dh1, dg3), (part_d, pair_g, pair_u) = _bwd_ffn_in(
        da, du, wfull[P_GATE], wfull[P_UP], h1, dh2, pre_ffn_norm, tm,
        _merge_comms([_chip_comm(gwd, pair_d, rows[P_DOWN]), _pair_comm(gwg, rows[P_GATE]), _pair_comm(gwu, rows[P_UP])]))
    dmix, dy, dg2 = _bwd_out_proj(dh1, mix, wfull[P_OUT], post_mix_norm, tm)
    gwo, _ = _wgrad(y, dmix, "wgrad_out")
    dz, dlg, dlb, dbb = _mix_ln_bwd(z, dy, ln_b_gain, ln_b_bias, tm)
    (dh5, dwa, dwb), (part_g, part_u, pair_o) = _mix_conv_bwd(
        hin, dy, dz, wa, wb, wa_w,
        _merge_comms([_chip_comm(gwg, pair_g, rows[P_GATE]), _chip_comm(gwu, pair_u, rows[P_UP]), _pair_comm(gwo, rows[P_OUT])]))
    gwi, (part_o,) = _wgrad(dh5, xn1, "wgrad_in", _chip_comm(gwo, pair_o, rows[P_OUT]))
    (dh0, dg1), (pair_i, part_i) = _bwd_in_proj(dh5, wfull[P_IN], h0, dh1, pre_mix_norm, tm, _pair_chip_comm(gwi, rows[P_IN]))
    grad_x = dh0[x0:][None]
    dmeta = dh0[x0 - N_META:x0]
    (ptot,), _ = _reduce_small((dmeta, dg1, dg2, dg3, dg4, dbb, dlg, dlb, lossv, dwa, dwb), d, None)

    half = d // 2
    loss = (0.5 / d) * jnp.sum(ptot[21, half:])
    g_meta = lax.dynamic_slice(ptot, (0, me * (d // N_DEV)), (N_META, d // N_DEV))
    g_small = [g_meta, ptot[16:17], lax.dynamic_slice(ptot, (24, me * ca_loc), (ka, ca_loc))[None],
               lax.dynamic_slice(ptot, (32, me * cb_loc), (kb, cb_loc))[None],
               ptot[20:21, :half], ptot[20:21, half:], ptot[21:22, :half], ptot[17:18], ptot[18:19], ptot[19:20]]
    w_small = [meta_tokens, pre_mix_norm, conv_a_w, conv_b_w, conv_b_bias, ln_b_gain, ln_b_bias, post_mix_norm,
               pre_ffn_norm, post_ffn_norm]
    m_small = [m_meta_tokens, m_pre_mix_norm, m_conv_a_w, m_conv_b_w, m_conv_b_bias, m_ln_b_gain, m_ln_b_bias,
               m_post_mix_norm, m_pre_ffn_norm, m_post_ffn_norm]
    v_small = [v_meta_tokens, v_pre_mix_norm, v_conv_a_w, v_conv_b_w, v_conv_b_bias, v_ln_b_gain, v_ln_b_bias,
               v_post_mix_norm, v_pre_ffn_norm, v_post_ffn_norm]
    small = _adam_small(g_small, w_small, m_small, v_small)
    n_small = len(w_small)
    d_small, nm_small, nv_small = small[:n_small], small[n_small:2 * n_small], small[2 * n_small:]

    m_loc = as_rows(m_w_in, m_w_out, m_w_gate, m_w_up, m_w_down)
    v_loc = as_rows(v_w_in, v_w_out, v_w_gate, v_w_up, v_w_down)
    full_grads = (gwi, gwo, gwg, gwu, gwd)
    pairs = (pair_i, pair_o, pair_g, pair_u, pair_d)
    parts = (part_i, part_o, part_g, part_u, part_d)
    bigs = {}
    for p, (name, tr) in enumerate((("w_in", True), ("w_out", False), ("w_gate", True), ("w_up", True), ("w_down", False))):
        res = _adam_big(full_grads[p], pairs[p], parts[p], w_loc[p], m_loc[p], v_loc[p], "adam_" + name)
        bigs[name] = [(o.T if tr else o)[None] for o in res]

    def ordered(pick_small, pick_big):
        sm_it = iter(range(n_small))
        out = []
        for name in ("s", "s", "w_in", "s", "s", "s", "s", "s", "w_out", "s", "s", "w_gate", "w_up", "w_down", "s"):
            out.append(pick_small(next(sm_it)) if name == "s" else pick_big(name))
        return out

    grads = ordered(lambda i: g_small[i], lambda n: bigs[n][0])
    deltas = ordered(lambda i: d_small[i], lambda n: bigs[n][1])
    new_m = ordered(lambda i: nm_small[i], lambda n: bigs[n][2])
    new_v = ordered(lambda i: nv_small[i], lambda n: bigs[n][3])
    return (loss, grad_x, *grads, *deltas, *new_m, *new_v)
```

```python
import jax
import jax.numpy as jnp
from jax import lax
from jax.experimental import pallas as pl
from jax.experimental.pallas import tpu as pltpu

F32 = jnp.float32
BF16 = jnp.bfloat16
MESH = pl.DeviceIdType.MESH

N_META = 16
N_DEV = 8
RMS_EPS = 1e-6
LN_EPS = 1e-5
ADAM_LR = 0.001
ADAM_B1 = 0.9
ADAM_B2 = 0.999
ADAM_EPS = 1e-08
ADAM_WD = 0.01
ADAM_STEP = 10

LANE = 128
SUBLANE = 8
BF16_ROWS = 16
ROW_ALIGN = 128
N_ROW_BLOCKS = 4
CONV_HALO = 32
CONV_CHUNK = 64
WGRAD_ROWS = 32
N_CHUNK = 512
WGRAD_TILE = 256
ADD_CHUNK = 32
ADAM_COL_BLOCKS = 4
V7X_VMEM_BYTES = 64 * 1024 * 1024
VMEM_LIMIT = V7X_VMEM_BYTES - 6 * 1024 * 1024
SMALL_ROWS = 64
SM_ROWS = 56
N_BIG = 5

ANY = pl.BlockSpec(memory_space=pl.ANY)
VMEM = pl.BlockSpec(memory_space=pltpu.VMEM)


def _cparams(n_grid_axes=0):
    sem = ("arbitrary",) * n_grid_axes if n_grid_axes else None
    return pltpu.CompilerParams(dimension_semantics=sem, vmem_limit_bytes=VMEM_LIMIT)


def _mesh_pos():
    return lax.axis_index("x"), lax.axis_index("y"), lax.axis_index("c")


def _dev_index(px, py, pc):
    return 4 * px + 2 * py + pc


def _other_chips(x, y):
    return [(1 - x, y), (x, 1 - y), (1 - x, 1 - y)]


def _full(shape):
    return pl.BlockSpec(shape, lambda *_: (0,) * len(shape))


def _resident(shape):
    return pl.BlockSpec(shape, lambda *_: (0,) * len(shape), pipeline_mode=pl.Buffered(1))


def _dot_nt(a, w):
    return lax.dot_general(a, w, (((1,), (1,)), ((), ())), preferred_element_type=F32)


def _dot_nn(a, w):
    return jnp.dot(a, w, preferred_element_type=F32)


def _chunks(n, c):
    out, o = [], 0
    while o < n:
        out.append((o, min(c, n - o)))
        o += c
    return out


def _rstd(h):
    return lax.rsqrt(jnp.mean(h * h, axis=-1, keepdims=True) + RMS_EPS)


def _rms_bwd(dyh, yh, r):
    return r * (dyh - yh * jnp.mean(dyh * yh, axis=-1, keepdims=True))


def _silu_grad(a, sig):
    return sig * (1.0 + a * (1.0 - sig))


def _acc_rows(ref, val, first):
    s = jnp.sum(val, axis=0, keepdims=True)

    @pl.when(first)
    def _():
        ref[...] = s

    @pl.when(jnp.logical_not(first))
    def _():
        ref[...] += s


def _row_loop(t_rows, chunk, fn, carry=None):
    def step(i, c):
        return fn(pl.multiple_of(i * chunk, chunk), c)

    return lax.fori_loop(0, t_rows // chunk, step, carry)


def _remote(src, dst, send_sem, recv_sem, to):
    return pltpu.make_async_remote_copy(src_ref=src, dst_ref=dst, send_sem=send_sem, recv_sem=recv_sem,
                                        device_id=to, device_id_type=MESH)


class _Comm:
    def __init__(self, inputs, out_shapes, aliases, scratch, start, finish, mid=None):
        self.inputs, self.out_shapes, self.aliases, self.scratch = list(inputs), list(out_shapes), dict(aliases), list(scratch)
        self.start, self.finish, self.mid = start, finish, mid


def _merge_comms(comms):
    inputs, out_shapes, aliases, scratch, spans = [], [], {}, [], []
    for cm in comms:
        spans.append((len(inputs), len(out_shapes), len(scratch), cm))
        aliases.update({len(inputs) + k: len(out_shapes) + v for k, v in cm.aliases.items()})
        inputs += cm.inputs
        out_shapes += cm.out_shapes
        scratch += cm.scratch

    def run(which):
        def fn(ins, outs, scr):
            for i0, o0, s0, cm in spans:
                if getattr(cm, which) is not None:
                    getattr(cm, which)(ins[i0:i0 + len(cm.inputs)], outs[o0:o0 + len(cm.out_shapes)], scr[s0:s0 + len(cm.scratch)])
        return fn

    mid = run("mid") if any(cm.mid is not None for cm in comms) else None
    return _Comm(inputs, out_shapes, aliases, scratch, run("start"), run("finish"), mid)


def _host_call(body, *, grid, in_specs, out_specs, out_shape, args, name, scratch_shapes=(), comm=None):
    if comm is None:
        res = pl.pallas_call(body, grid=grid, in_specs=list(in_specs), out_specs=list(out_specs), out_shape=list(out_shape),
                             scratch_shapes=list(scratch_shapes), name=name, compiler_params=_cparams(len(grid)))(*args)
        return list(res), []
    n_in, n_out, n_scr = len(args), len(out_shape), len(scratch_shapes)
    c_in, c_out = len(comm.inputs), len(comm.out_shapes)

    def hosted(*refs):
        ins, c_ins = refs[:n_in], refs[n_in:n_in + c_in]
        o0 = n_in + c_in
        outs, c_outs = refs[o0:o0 + n_out], refs[o0 + n_out:o0 + n_out + c_out]
        s0 = o0 + n_out + c_out
        scr, c_scr = refs[s0:s0 + n_scr], refs[s0 + n_scr:]
        if not grid:
            comm.start(c_ins, c_outs, c_scr)
            body(*ins, *outs, *scr)
            comm.finish(c_ins, c_outs, c_scr)
            return
        first = last = None
        for a, n in enumerate(grid):
            f, l = pl.program_id(a) == 0, pl.program_id(a) == n - 1
            first = f if first is None else jnp.logical_and(first, f)
            last = l if last is None else jnp.logical_and(last, l)

        @pl.when(first)
        def _():
            comm.start(c_ins, c_outs, c_scr)

        if comm.mid is not None:
            assert len(grid) == 1 and grid[0] >= 3

            @pl.when(pl.program_id(0) == 1)
            def _():
                comm.mid(c_ins, c_outs, c_scr)

        body(*ins, *outs, *scr)

        @pl.when(last)
        def _():
            comm.finish(c_ins, c_outs, c_scr)

    res = pl.pallas_call(
        hosted, grid=grid, in_specs=list(in_specs) + [ANY] * c_in, out_specs=list(out_specs) + [ANY] * c_out,
        out_shape=list(out_shape) + comm.out_shapes, scratch_shapes=list(scratch_shapes) + comm.scratch,
        input_output_aliases={n_in + k: n_out + v for k, v in comm.aliases.items()},
        name=name, compiler_params=_cparams(len(grid)))(*args, *comm.inputs)
    return list(res[:n_out]), list(res[n_out:])


GATHER_SEMS = 10


class _Gather:
    def __init__(self, jobs, rows, lo, src_ref, dests, send_sems, recv_sems):
        x, y, c = _mesh_pos()
        me, sib = (x, y, c), (x, y, 1 - c)
        nx, ny, dg = (1 - x, y, c), (x, 1 - y, c), (1 - x, 1 - y, c)
        self.relayed, self.direct, self.relay, self.to_sib, self.sib_fwd = [], [], [], [], []
        for n, (p, r0, nr) in enumerate(jobs):
            assert nr % (2 * BF16_ROWS) == 0
            half = nr // 2

            def rows_of(dev, h, p=p, r0=r0, nr=nr, half=half):
                off, cnt = (r0, nr) if h is None else (r0 + h * half, half)
                return dests[p].at[pl.ds(pl.multiple_of(_dev_index(*dev) * rows[p] + off, BF16_ROWS), cnt), :]

            def mine(h, p=p, r0=r0, nr=nr, half=half):
                off, cnt = (r0, nr) if h is None else (r0 + h * half, half)
                return src_ref.at[pl.ds(lo[p] + off, cnt), :]

            sem = lambda k, n=n: (send_sems.at[GATHER_SEMS * n + k], recv_sems.at[GATHER_SEMS * n + k])
            self.relayed.append([_remote(mine(0), rows_of(me, 0), *sem(0), nx), _remote(mine(1), rows_of(me, 1), *sem(3), ny)])
            self.direct.append([_remote(mine(1), rows_of(me, 1), *sem(1), nx), _remote(mine(0), rows_of(me, 0), *sem(2), ny)])
            self.relay.append([_remote(rows_of(nx, 0), rows_of(nx, 0), *sem(4), ny), _remote(rows_of(ny, 1), rows_of(ny, 1), *sem(5), nx)])
            self.to_sib.append(_remote(mine(None), rows_of(me, None), *sem(6), sib))
            self.sib_fwd.append([_remote(rows_of(dev, None), rows_of(dev, None), *sem(7 + i), sib) for i, dev in enumerate((nx, ny, dg))])

    def start(self):
        for group in (self.relayed, self.direct):
            for cps in group:
                for cp in cps:
                    cp.start()
        for cp in self.to_sib:
            cp.start()

    def mid(self):
        for first, relay in zip(self.relayed, self.relay):
            for arrived, onward in zip(first, relay):
                arrived.wait_recv()
                onward.start()

    def finish(self):
        for direct, relay, fwd in zip(self.direct, self.relay, self.sib_fwd):
            for k in range(2):
                direct[k].wait_recv()
                fwd[k].start()
            for cp in relay:
                cp.wait_recv()
            fwd[2].start()
        for n in range(len(self.to_sib)):
            self.to_sib[n].wait_recv()
            for cp in self.sib_fwd[n]:
                cp.wait_recv()
            for cp in self.relayed[n] + self.direct[n] + self.relay[n] + [self.to_sib[n]] + self.sib_fwd[n]:
                cp.wait_send()


def _gather_comm(wl, dests, jobs, rows):
    lo = [sum(rows[:p]) for p in range(N_BIG)]
    ps = sorted(dests)
    slot = {p: i for i, p in enumerate(ps)}

    def gather(ins, outs, scr):
        return _Gather(jobs, rows, lo, ins[0], {p: outs[slot[p]] for p in ps}, scr[0], scr[1])

    n_sems = GATHER_SEMS * len(jobs)
    return _Comm([wl] + [dests[p] for p in ps], [jax.ShapeDtypeStruct(dests[p].shape, BF16) for p in ps],
                 {1 + i: i for i in range(len(ps))},
                 [pltpu.SemaphoreType.DMA((n_sems,)), pltpu.SemaphoreType.DMA((n_sems,))],
                 lambda *a: gather(*a).start(), lambda *a: gather(*a).finish(), lambda *a: gather(*a).mid())


def _pair_comm(g, r):
    d = g.shape[1]

    def descs(ins, outs, scr):
        x, y, c = _mesh_pos()
        chips = [(x, y)] + _other_chips(x, y)
        return [_remote(ins[0].at[pl.ds(pl.multiple_of(_dev_index(*chip, 1 - c) * r, BF16_ROWS), r), :], outs[0].at[k],
                        scr[0].at[k], scr[1].at[k], (x, y, 1 - c)) for k, chip in enumerate(chips)]

    def start(ins, outs, scr):
        for cp in descs(ins, outs, scr):
            cp.start()

    def finish(ins, outs, scr):
        cps = descs(ins, outs, scr)
        for cp in cps:
            cp.wait_recv()
        for cp in cps:
            cp.wait_send()

    comm = _Comm([g], [jax.ShapeDtypeStruct((4, r, d), BF16)], {},
                 [pltpu.SemaphoreType.DMA((4,)), pltpu.SemaphoreType.DMA((4,))], start, finish)
    comm.descs = descs
    return comm


def _chip_comm(g, pair, r):
    d = g.shape[1]
    return _Comm([g, pair], [jax.ShapeDtypeStruct((3, r, d), BF16)], {}, _chip_scratch(r, d),
                 lambda ins, outs, scr: _chip_send(ins[0], ins[1], outs[0], scr, r),
                 lambda ins, outs, scr: _chip_wait(outs[0], scr))


def _chip_scratch(r, d):
    return [pltpu.VMEM((r, d), BF16), pltpu.VMEM((r, d), BF16), pltpu.VMEM((3, r, d), BF16),
            pltpu.SemaphoreType.DMA((3,)), pltpu.SemaphoreType.DMA((3,)), pltpu.SemaphoreType.DMA((2,))]


def _chip_copies(part_ref, scr):
    x, y, c = _mesh_pos()
    return [_remote(scr[2].at[k], part_ref.at[k], scr[3].at[k], scr[4].at[k], (*chip, c))
            for k, chip in enumerate(_other_chips(x, y))]


def _chip_send(g_ref, pair_ref, part_ref, scr, r):
    x, y, c = _mesh_pos()
    gbuf, abuf, sendbuf, local = scr[0], scr[1], scr[2], scr[5]
    cps = _chip_copies(part_ref, scr)
    for k, chip in enumerate(_other_chips(x, y)):
        j = _dev_index(*chip, c)
        loads = [pltpu.make_async_copy(g_ref.at[pl.ds(pl.multiple_of(j * r, BF16_ROWS), r), :], gbuf, local.at[0]),
                 pltpu.make_async_copy(pair_ref.at[1 + k], abuf, local.at[1])]
        for cp in loads:
            cp.start()
        for cp in loads:
            cp.wait()

        def add(b, carry, k=k):
            sl = pl.ds(b, ADD_CHUNK)
            sendbuf[k, sl, :] = (gbuf[sl, :].astype(F32) + abuf[sl, :].astype(F32)).astype(BF16)
            return carry

        _row_loop(r, ADD_CHUNK, add)
        cps[k].start()


def _chip_wait(part_ref, scr):
    cps = _chip_copies(part_ref, scr)
    for cp in cps:
        cp.wait_recv()
    for cp in cps:
        cp.wait_send()


def _pair_chip_comm(g, r):
    d = g.shape[1]
    pair = _pair_comm(g, r)
    n_pair = len(pair.scratch)

    def mid(ins, outs, scr):
        for cp in pair.descs(ins, outs[:1], scr[:n_pair]):
            cp.wait_recv()
        _chip_send(ins[0], outs[0], outs[1], scr[n_pair:], r)

    def finish(ins, outs, scr):
        _chip_wait(outs[1], scr[n_pair:])
        for cp in pair.descs(ins, outs[:1], scr[:n_pair]):
            cp.wait_send()

    return _Comm([g], pair.out_shapes + [jax.ShapeDtypeStruct((3, r, d), BF16)], {}, pair.scratch + _chip_scratch(r, d),
                 lambda ins, outs, scr: pair.start(ins, outs[:1], scr[:n_pair]), finish, mid)


def _gather_first(shards, sm, jobs):
    d = shards[0].shape[1]
    rows = [w.shape[0] for w in shards]
    lo = [sum(rows[:p]) for p in range(N_BIG)]
    n_sems = GATHER_SEMS * len(jobs)

    def body(s0, s1, s2, s3, s4, sm_ref, wl_ref, o0, o1, o2, o3, o4, sa_ref, wl_v, send_sems, recv_sems, ssend, srecv, local_sems):
        dests = (o0, o1, o2, o3, o4)
        x, y, c = _mesh_pos()
        me = (x, y, c)
        jme = _dev_index(*me)
        for p, ref in enumerate((s0, s1, s2, s3, s4)):
            wl_v[pl.ds(lo[p], rows[p]), :] = ref[...].astype(BF16)
        gather = _Gather(jobs, rows, lo, wl_v, dict(enumerate(dests)), send_sems, recv_sems)
        gather.start()
        peers = [(x, y, 1 - c)] + [(*chip, pc) for pc in (c, 1 - c) for chip in _other_chips(x, y)]
        smalls = [_remote(sm_ref, sa_ref.at[jme], ssend.at[k], srecv.at[k], to) for k, to in enumerate(peers)]
        for cp in smalls:
            cp.start()
        mine = [pltpu.make_async_copy(wl_v.at[pl.ds(lo[p], rows[p]), :],
                                      dests[p].at[pl.ds(pl.multiple_of(jme * rows[p], BF16_ROWS), rows[p]), :], local_sems.at[p])
                for p in range(N_BIG)]
        mine.append(pltpu.make_async_copy(wl_v, wl_ref, local_sems.at[N_BIG]))
        mine.append(pltpu.make_async_copy(sm_ref, sa_ref.at[jme], local_sems.at[N_BIG + 1]))
        for cp in mine:
            cp.start()
        gather.mid()
        gather.finish()
        for cp in smalls:
            cp.wait_recv()
        for cp in smalls:
            cp.wait_send()
        for cp in mine:
            cp.wait()

    out_shape = [jax.ShapeDtypeStruct((sum(rows), d), BF16)]
    out_shape += [jax.ShapeDtypeStruct((N_DEV * r, d), BF16) for r in rows]
    out_shape.append(jax.ShapeDtypeStruct((N_DEV,) + sm.shape, F32))
    res = pl.pallas_call(
        body, out_shape=out_shape, in_specs=[VMEM] * 6, out_specs=[ANY] * 7,
        scratch_shapes=[pltpu.VMEM((sum(rows), d), BF16), pltpu.SemaphoreType.DMA((n_sems,)), pltpu.SemaphoreType.DMA((n_sems,)),
                        pltpu.SemaphoreType.DMA((7,)), pltpu.SemaphoreType.DMA((7,)), pltpu.SemaphoreType.DMA((N_BIG + 2,))],
        name="gather_first", compiler_params=_cparams())(*shards, sm)
    return res[0], list(res[1:1 + N_BIG]), res[-1]


def _prep(x2, tgt2, meta_full, t_rows, x0, comm):
    s, d = x2.shape
    assert x0 == ROW_ALIGN and s % ROW_ALIGN == 0

    def body(x_ref, tgt_ref, meta_ref, h0_ref, tp_ref):
        i = pl.program_id(0)

        @pl.when(i == 0)
        def _():
            h0_ref[...] = jnp.zeros_like(h0_ref)
            h0_ref[pl.ds(x0 - N_META, N_META), :] = meta_ref[...]
            tp_ref[...] = jnp.zeros_like(tp_ref)

        @pl.when(i > 0)
        def _():
            h0_ref[...] = x_ref[...]
            tp_ref[...] = tgt_ref[...]

    src = pl.BlockSpec((ROW_ALIGN, d), lambda i: (jnp.maximum(i - 1, 0), 0))
    dst = pl.BlockSpec((ROW_ALIGN, d), lambda i: (i, 0))
    return _host_call(body, grid=(t_rows // ROW_ALIGN,), in_specs=[src, src, _full((N_META, d))], out_specs=[dst, dst],
                      out_shape=[jax.ShapeDtypeStruct((t_rows, d), F32)] * 2, args=(x2, tgt2, meta_full), name="prep", comm=comm)


def _in_proj(h0, g1, win_t, tm, comm):
    t_rows, d = h0.shape
    e = win_t.shape[0]

    def body(h_ref, g_ref, w_ref, xn_ref, hin_ref):
        h = h_ref[...]
        xn = ((h * _rstd(h)) * g_ref[...]).astype(BF16)
        xn_ref[...] = xn
        for o, n in _chunks(e, N_CHUNK):
            hin_ref[:, pl.ds(o, n)] = _dot_nt(xn, w_ref[pl.ds(o, n), :])

    return _host_call(
        body, grid=(t_rows // tm,),
        in_specs=[pl.BlockSpec((tm, d), lambda i: (i, 0)), _full((1, d)), _resident((e, d))],
        out_specs=[pl.BlockSpec((tm, d), lambda i: (i, 0)), pl.BlockSpec((tm, e), lambda i: (i, 0))],
        out_shape=[jax.ShapeDtypeStruct((t_rows, d), BF16), jax.ShapeDtypeStruct((t_rows, e), F32)],
        args=(h0, g1, win_t), name="in_proj", comm=comm)


def _tap_slot(off):
    return off % SUBLANE, (off // SUBLANE) * SUBLANE


def _fill_shifted(sh_ref, base_ref, residues, n_rows):
    for r in residues:
        sh_ref[r] = base_ref[pl.ds(r, n_rows), :]


def _mix_conv_fwd(hin, wa, wb, bb, wa_w, comm):
    t_rows = hin.shape[0]
    nt = wa_w // LANE
    ka, kb = wa.shape[0], wb.shape[0]
    nr = CONV_HALO + t_rows

    def body(bg_ref, cg_ref, ha_ref, val_ref, gt_ref, wa_ref, wb_ref, bb_ref, ya_ref, z_ref, base, sh):
        base[pl.ds(0, CONV_HALO), :] = jnp.zeros((CONV_HALO, LANE), F32)
        base[pl.ds(nr, SUBLANE), :] = jnp.zeros((SUBLANE, LANE), F32)

        def conv(w_ref, k_taps, b, n):
            acc = None
            for k in range(k_taps):
                r, q = _tap_slot(CONV_HALO - (k_taps - 1) + k)
                term = w_ref[pl.ds(k, 1), :] * sh[r, pl.ds(b + q, n), :]
                acc = term if acc is None else acc + term
            return acc

        def fill_a(b, c):
            base[pl.ds(CONV_HALO + b, CONV_CHUNK), :] = cg_ref[pl.ds(b, CONV_CHUNK), :] * ha_ref[pl.ds(b, CONV_CHUNK), :]
            return c

        _row_loop(t_rows, CONV_CHUNK, fill_a)
        _fill_shifted(sh, base, sorted({_tap_slot(CONV_HALO - (ka - 1) + k)[0] for k in range(ka)}), nr)

        def out_a(b, c):
            ya_ref[pl.ds(b, CONV_CHUNK), :] = (bg_ref[pl.ds(b, CONV_CHUNK), :] * conv(wa_ref, ka, b, CONV_CHUNK)).astype(BF16)
            return c

        _row_loop(t_rows, CONV_CHUNK, out_a)

        def fill_b(b, c):
            base[pl.ds(CONV_HALO + b, CONV_CHUNK), :] = (val_ref[pl.ds(b, CONV_CHUNK), :]
                                                          * jax.nn.sigmoid(gt_ref[pl.ds(b, CONV_CHUNK), :]))
            return c

        _row_loop(t_rows, CONV_CHUNK, fill_b)
        _fill_shifted(sh, base, range(SUBLANE), nr)

        def out_b(b, c):
            z_ref[pl.ds(b, CONV_CHUNK), :] = conv(wb_ref, kb, b, CONV_CHUNK) + bb_ref[...]
            return c

        _row_loop(t_rows, CONV_CHUNK, out_b)

    def col(g):
        return pl.BlockSpec((t_rows, LANE), lambda i, g=g: (0, g * nt + i))

    tile = lambda rows: pl.BlockSpec((rows, LANE), lambda i: (0, i))
    return _host_call(
        body, grid=(nt,),
        in_specs=[col(0), col(1), col(2), col(3), col(4), tile(ka), tile(kb), tile(1)],
        out_specs=[tile(t_rows), tile(t_rows)],
        out_shape=[jax.ShapeDtypeStruct((t_rows, wa_w), BF16), jax.ShapeDtypeStruct((t_rows, wa_w), F32)],
        scratch_shapes=[pltpu.VMEM((nr + SUBLANE, LANE), F32), pltpu.VMEM((SUBLANE, nr, LANE), F32)],
        args=(hin, hin, hin, hin, hin, wa, wb, bb), name="mix_conv_fwd", comm=comm)


def _ln_parts(z, lg, lb):
    mu = jnp.mean(z, axis=-1, keepdims=True)
    zc = z - mu
    rstd = lax.rsqrt(jnp.mean(zc * zc, axis=-1, keepdims=True) + LN_EPS)
    zh = zc * rstd
    return zh, rstd, zh * lg + lb


def _mix_ln_fwd(ya, z, lg, lb, tm):
    t_rows, w = z.shape

    def body(ya_ref, z_ref, lg_ref, lb_ref, y_ref):
        _, _, ln = _ln_parts(z_ref[...], lg_ref[...], lb_ref[...])
        y_ref[:, pl.ds(0, w)] = ya_ref[...]
        y_ref[:, pl.ds(w, w)] = (ln * jax.nn.sigmoid(ln)).astype(BF16)

    blk = pl.BlockSpec((tm, w), lambda i: (i, 0))
    res, _ = _host_call(body, grid=(t_rows // tm,), in_specs=[blk, blk, _full((1, w)), _full((1, w))],
                        out_specs=[pl.BlockSpec((tm, 2 * w), lambda i: (i, 0))],
                        out_shape=[jax.ShapeDtypeStruct((t_rows, 2 * w), BF16)], args=(ya, z, lg, lb), name="mix_ln_fwd")
    return res[0]


def _out_proj(y, w_out, h0, g2, g3, tm, comm):
    t_rows, d = h0.shape

    def body(y_ref, w_ref, h0_ref, g2_ref, g3_ref, mix_ref, h1_ref, xn2_ref):
        mix = _dot_nn(y_ref[...], w_ref[...])
        mix_ref[...] = mix
        h1 = h0_ref[...] + (mix * _rstd(mix)) * g2_ref[...]
        h1_ref[...] = h1
        xn2_ref[...] = ((h1 * _rstd(h1)) * g3_ref[...]).astype(BF16)

    blk = pl.BlockSpec((tm, d), lambda i: (i, 0))
    return _host_call(
        body, grid=(t_rows // tm,), in_specs=[blk, _resident(w_out.shape), blk, _full((1, d)), _full((1, d))],
        out_specs=[blk, blk, blk],
        out_shape=[jax.ShapeDtypeStruct((t_rows, d), F32), jax.ShapeDtypeStruct((t_rows, d), F32),
                   jax.ShapeDtypeStruct((t_rows, d), BF16)],
        args=(y, w_out, h0, g2, g3), name="out_proj", comm=comm)


def _gate_up(xn2, wg_t, wu_t, tm, comm):
    t_rows, d = xn2.shape
    f = wg_t.shape[0]

    def body(x_ref, wg_ref, wu_ref, a_ref, u_ref, s_ref):
        xn = x_ref[...]
        for o, n in _chunks(f, N_CHUNK):
            a = _dot_nt(xn, wg_ref[pl.ds(o, n), :])
            u = _dot_nt(xn, wu_ref[pl.ds(o, n), :])
            a_ref[:, pl.ds(o, n)] = a.astype(BF16)
            u_ref[:, pl.ds(o, n)] = u.astype(BF16)
            s_ref[:, pl.ds(o, n)] = ((a * jax.nn.sigmoid(a)) * u).astype(BF16)

    blk = pl.BlockSpec((tm, f), lambda i: (i, 0))
    return _host_call(
        body, grid=(t_rows // tm,),
        in_specs=[pl.BlockSpec((tm, d), lambda i: (i, 0)), _resident((f, d)), _resident((f, d))],
        out_specs=[blk, blk, blk], out_shape=[jax.ShapeDtypeStruct((t_rows, f), BF16)] * 3,
        args=(xn2, wg_t, wu_t), name="gate_up", comm=comm)


def _down_loss(s, wd, h1, tgt, g4, tm, x0):
    t_rows, d = h1.shape
    f = wd.shape[0]

    def body(s_ref, w_ref, h1_ref, tgt_ref, g4_ref, dh2_ref, dff_ref, dg4_ref, loss_ref):
        i = pl.program_id(0)
        ff = _dot_nn(s_ref[...], w_ref[...])
        r4 = _rstd(ff)
        fh = ff * r4
        g4 = g4_ref[...]
        h2 = h1_ref[...] + fh * g4
        row = i * tm + lax.broadcasted_iota(jnp.int32, (tm, 1), 0)
        diff = jnp.where(row >= x0, h2 - tgt_ref[...], 0.0)
        dh2 = diff / d
        dh2_ref[...] = dh2
        dff_ref[...] = _rms_bwd(dh2 * g4, fh, r4).astype(BF16)
        _acc_rows(dg4_ref, dh2 * fh, i == 0)
        _acc_rows(loss_ref, diff * diff, i == 0)

    blk = pl.BlockSpec((tm, d), lambda i: (i, 0))
    res, _ = _host_call(
        body, grid=(t_rows // tm,),
        in_specs=[pl.BlockSpec((tm, f), lambda i: (i, 0)), _resident((f, d)), blk, blk, _full((1, d))],
        out_specs=[blk, blk, _full((1, d)), _full((1, d))],
        out_shape=[jax.ShapeDtypeStruct((t_rows, d), F32), jax.ShapeDtypeStruct((t_rows, d), BF16),
                   jax.ShapeDtypeStruct((1, d), F32), jax.ShapeDtypeStruct((1, d), F32)],
        args=(s, wd, h1, tgt, g4), name="down_loss")
    return res


def _bwd_down(dff, wd, a, u, tm, comm):
    t_rows, d = dff.shape
    f = wd.shape[0]

    def body(dff_ref, w_ref, a_ref, u_ref, da_ref, du_ref):
        dff_v = dff_ref[...]
        for o, n in _chunks(f, N_CHUNK):
            ds = _dot_nt(dff_v, w_ref[pl.ds(o, n), :])
            av = a_ref[:, pl.ds(o, n)].astype(F32)
            uv = u_ref[:, pl.ds(o, n)].astype(F32)
            sig = jax.nn.sigmoid(av)
            da_ref[:, pl.ds(o, n)] = (ds * uv * _silu_grad(av, sig)).astype(BF16)
            du_ref[:, pl.ds(o, n)] = (ds * (av * sig)).astype(BF16)

    blk = pl.BlockSpec((tm, f), lambda i: (i, 0))
    return _host_call(
        body, grid=(t_rows // tm,),
        in_specs=[pl.BlockSpec((tm, d), lambda i: (i, 0)), _resident((f, d)), blk, blk],
        out_specs=[blk, blk], out_shape=[jax.ShapeDtypeStruct((t_rows, f), BF16)] * 2,
        args=(dff, wd, a, u), name="bwd_down", comm=comm)


def _wgrad(a, b, name, comm=None):
    d = b.shape[1]
    t_rows = b.shape[0]
    stacked = a.ndim == 3
    n = a.shape[-1]
    groups = a.shape[0] if stacked else 1
    tiles = n // WGRAD_TILE

    def body(a_ref, b_ref, o_ref):
        o_ref[...] = lax.dot_general(a_ref[...], b_ref[...], (((0,), (0,)), ((), ())),
                                     preferred_element_type=F32).astype(BF16)

    if stacked:
        a_spec = pl.BlockSpec((None, t_rows, WGRAD_TILE), lambda g, i: (g, 0, i))
    else:
        a_spec = pl.BlockSpec((t_rows, WGRAD_TILE), lambda g, i: (0, i))
    res, extra = _host_call(
        body, grid=(groups, tiles), in_specs=[a_spec, pl.BlockSpec((t_rows, d), lambda g, i: (0, 0))],
        out_specs=[pl.BlockSpec((WGRAD_TILE, d), lambda g, i: (g * tiles + i, 0))],
        out_shape=[jax.ShapeDtypeStruct((groups * n, d), BF16)], args=(a, b), name=name, comm=comm)
    return res[0], extra


def _bwd_ffn_in(da, du, wg_t, wu_t, h1, dh2, g3, tm, comm):
    t_rows, d = h1.shape
    f = wg_t.shape[0]

    def body(da_ref, du_ref, wg_ref, wu_ref, h1_ref, dh2_ref, g3_ref, dh1_ref, dg3_ref):
        dxn2 = _dot_nn(da_ref[...], wg_ref[...]) + _dot_nn(du_ref[...], wu_ref[...])
        h1 = h1_ref[...]
        r3 = _rstd(h1)
        h1h = h1 * r3
        _acc_rows(dg3_ref, dxn2 * h1h, pl.program_id(0) == 0)
        dh1_ref[...] = dh2_ref[...] + _rms_bwd(dxn2 * g3_ref[...], h1h, r3)

    blk = pl.BlockSpec((tm, d), lambda i: (i, 0))
    blkf = pl.BlockSpec((tm, f), lambda i: (i, 0))
    return _host_call(
        body, grid=(t_rows // tm,),
        in_specs=[blkf, blkf, _resident((f, d)), _resident((f, d)), blk, blk, _full((1, d))],
        out_specs=[blk, _full((1, d))],
        out_shape=[jax.ShapeDtypeStruct((t_rows, d), F32), jax.ShapeDtypeStruct((1, d), F32)],
        args=(da, du, wg_t, wu_t, h1, dh2, g3), name="bwd_ffn_in", comm=comm)


def _bwd_out_proj(dh1, mix, w_out, g2, tm):
    t_rows, d = dh1.shape

    def body(dh1_ref, mix_ref, w_ref, g2_ref, dmix_ref, dy_ref, dg2_ref):
        mix = mix_ref[...]
        r2 = _rstd(mix)
        mh = mix * r2
        dh1 = dh1_ref[...]
        _acc_rows(dg2_ref, dh1 * mh, pl.program_id(0) == 0)
        dmix = _rms_bwd(dh1 * g2_ref[...], mh, r2).astype(BF16)
        dmix_ref[...] = dmix
        dy_ref[...] = _dot_nt(dmix, w_ref[...])

    blk = pl.BlockSpec((tm, d), lambda i: (i, 0))
    res, _ = _host_call(
        body, grid=(t_rows // tm,), in_specs=[blk, blk, _resident(w_out.shape), _full((1, d))],
        out_specs=[blk, blk, _full((1, d))],
        out_shape=[jax.ShapeDtypeStruct((t_rows, d), BF16), jax.ShapeDtypeStruct((t_rows, d), F32),
                   jax.ShapeDtypeStruct((1, d), F32)],
        args=(dh1, mix, w_out, g2), name="bwd_out_proj")
    return res


def _mix_ln_bwd(z, dy, lg, lb, tm):
    t_rows, w = z.shape

    def body(z_ref, dyb_ref, lg_ref, lb_ref, dz_ref, dlg_ref, dlb_ref, dbb_ref):
        first = pl.program_id(0) == 0
        lg = lg_ref[...]
        zh, rstd, ln = _ln_parts(z_ref[...], lg, lb_ref[...])
        dln = dyb_ref[...] * _silu_grad(ln, jax.nn.sigmoid(ln))
        _acc_rows(dlg_ref, dln * zh, first)
        _acc_rows(dlb_ref, dln, first)
        dzh = dln * lg
        dz = rstd * (dzh - jnp.mean(dzh, axis=-1, keepdims=True) - zh * jnp.mean(dzh * zh, axis=-1, keepdims=True))
        dz_ref[...] = dz
        _acc_rows(dbb_ref, dz, first)

    blk = pl.BlockSpec((tm, w), lambda i: (i, 0))
    vec = _full((1, w))
    res, _ = _host_call(
        body, grid=(t_rows // tm,), in_specs=[blk, pl.BlockSpec((tm, w), lambda i: (i, 1)), vec, vec],
        out_specs=[blk, vec, vec, vec],
        out_shape=[jax.ShapeDtypeStruct((t_rows, w), F32)] + [jax.ShapeDtypeStruct((1, w), F32)] * 3,
        args=(z, dy, lg, lb), name="mix_ln_bwd")
    return res


def _mix_conv_bwd(hin, dy, dz, wa, wb, wa_w, comm):
    t_rows = hin.shape[0]
    nt = wa_w // LANE
    ka, kb = wa.shape[0], wb.shape[0]
    nr = CONV_HALO + t_rows
    kb_rows = -(-kb // SUBLANE) * SUBLANE

    def body(bg_ref, cg_ref, ha_ref, val_ref, gt_ref, dya_ref, dz_ref, wa_ref, wb_ref,
             dh_ref, dwa_ref, dwb_ref, base, sh, based, shd, tmp, wbc):
        zeros = lambda n: jnp.zeros((n, LANE), F32)
        base[pl.ds(0, CONV_HALO), :] = zeros(CONV_HALO)
        base[pl.ds(nr, SUBLANE), :] = zeros(SUBLANE)
        based[pl.ds(t_rows, CONV_HALO + SUBLANE), :] = zeros(CONV_HALO + SUBLANE)

        def fwd_slot(k_taps, k):
            return _tap_slot(CONV_HALO - (k_taps - 1) + k)

        def bwd_slot(k_taps, k):
            return _tap_slot(k_taps - 1 - k)

        def conv(w_ref, k_taps, src, slot, b, n):
            acc = None
            for k in range(k_taps):
                r, q = slot(k_taps, k)
                term = w_ref[pl.ds(k, 1), :] * src[r, pl.ds(b + q, n), :]
                acc = term if acc is None else acc + term
            return acc

        def by_residue(k_taps, slot):
            groups = {}
            for k in range(k_taps):
                r, q = slot(k_taps, k)
                groups.setdefault(r, []).append((k, q // SUBLANE))
            return groups

        def wgrad_loop(w_ref, k_taps):
            n_sub = WGRAD_ROWS // SUBLANE
            for k in range(k_taps):
                wbc[k] = jnp.broadcast_to(w_ref[pl.ds(k, 1), :], (SUBLANE, LANE))
            fwd, bwd = by_residue(k_taps, fwd_slot), by_residue(k_taps, bwd_slot)

            def window(src, r, taps, b):
                span = n_sub + max(qi for _, qi in taps)
                return [src[r, pl.ds(b + SUBLANE * i, SUBLANE), :] for i in range(span)]

            def step(b, accs):
                accs = list(accs)
                dv = [based[pl.ds(b + SUBLANE * j, SUBLANE), :] for j in range(n_sub)]
                for r, taps in fwd.items():
                    win = window(sh, r, taps, b)
                    for k, qi in taps:
                        t = dv[0] * win[qi]
                        for j in range(1, n_sub):
                            t = t + dv[j] * win[qi + j]
                        accs[k] = accs[k] + t
                outs = [None] * n_sub
                for r, taps in bwd.items():
                    win = window(shd, r, taps, b)
                    for k, qi in taps:
                        wk = wbc[k]
                        for j in range(n_sub):
                            term = wk * win[qi + j]
                            outs[j] = term if outs[j] is None else outs[j] + term
                for j in range(n_sub):
                    tmp[pl.ds(b + SUBLANE * j, SUBLANE), :] = outs[j]
                return tuple(accs)

            return _row_loop(t_rows, WGRAD_ROWS, step, tuple(zeros(SUBLANE) for _ in range(k_taps)))

        def store_taps(ref, accs, rows):
            for k, acc in enumerate(accs):
                ref[pl.ds(k, 1), :] = jnp.sum(acc, axis=0, keepdims=True)
            if rows > len(accs):
                ref[pl.ds(len(accs), rows - len(accs)), :] = zeros(rows - len(accs))

        def fill_a(b, c):
            sl = pl.ds(b, CONV_CHUNK)
            base[pl.ds(CONV_HALO + b, CONV_CHUNK), :] = cg_ref[sl, :] * ha_ref[sl, :]
            based[sl, :] = dya_ref[sl, :] * bg_ref[sl, :]
            return c

        _row_loop(t_rows, CONV_CHUNK, fill_a)
        _fill_shifted(sh, base, sorted({fwd_slot(ka, k)[0] for k in range(ka)}), nr)
        _fill_shifted(shd, based, sorted({bwd_slot(ka, k)[0] for k in range(ka)}), nr)

        def d_bgate(b, c):
            sl = pl.ds(b, CONV_CHUNK)
            dh_ref[0, sl, :] = (dya_ref[sl, :] * conv(wa_ref, ka, sh, fwd_slot, b, CONV_CHUNK)).astype(BF16)
            return c

        _row_loop(t_rows, CONV_CHUNK, d_bgate)
        store_taps(dwa_ref, wgrad_loop(wa_ref, ka), SUBLANE)

        def d_ch(b, c):
            sl = pl.ds(b, CONV_CHUNK)
            dua = tmp[sl, :]
            dh_ref[1, sl, :] = (dua * ha_ref[sl, :]).astype(BF16)
            dh_ref[2, sl, :] = (dua * cg_ref[sl, :]).astype(BF16)
            return c

        _row_loop(t_rows, CONV_CHUNK, d_ch)

        def fill_b(b, c):
            sl = pl.ds(b, CONV_CHUNK)
            base[pl.ds(CONV_HALO + b, CONV_CHUNK), :] = val_ref[sl, :] * jax.nn.sigmoid(gt_ref[sl, :])
            based[sl, :] = dz_ref[sl, :]
            return c

        _row_loop(t_rows, CONV_CHUNK, fill_b)
        _fill_shifted(sh, base, range(SUBLANE), nr)
        _fill_shifted(shd, based, range(SUBLANE), nr)
        store_taps(dwb_ref, wgrad_loop(wb_ref, kb), kb_rows)

        def d_glu(b, c):
            sl = pl.ds(b, CONV_CHUNK)
            dgg = tmp[sl, :]
            sig = jax.nn.sigmoid(gt_ref[sl, :])
            dh_ref[3, sl, :] = (dgg * sig).astype(BF16)
            dh_ref[4, sl, :] = (dgg * val_ref[sl, :] * (sig * (1.0 - sig))).astype(BF16)
            return c

        _row_loop(t_rows, CONV_CHUNK, d_glu)

    def col(g):
        return pl.BlockSpec((t_rows, LANE), lambda i, g=g: (0, g * nt + i))

    tile = lambda rows: pl.BlockSpec((rows, LANE), lambda i: (0, i))
    return _host_call(
        body, grid=(nt,),
        in_specs=[col(0), col(1), col(2), col(3), col(4), tile(t_rows), tile(t_rows), tile(ka), tile(kb)],
        out_specs=[pl.BlockSpec((5, t_rows, LANE), lambda i: (0, 0, i)), tile(SUBLANE), tile(kb_rows)],
        out_shape=[jax.ShapeDtypeStruct((5, t_rows, wa_w), BF16), jax.ShapeDtypeStruct((SUBLANE, wa_w), F32),
                   jax.ShapeDtypeStruct((kb_rows, wa_w), F32)],
        scratch_shapes=[pltpu.VMEM((nr + SUBLANE, LANE), F32), pltpu.VMEM((SUBLANE, nr, LANE), F32),
                        pltpu.VMEM((nr + SUBLANE, LANE), F32), pltpu.VMEM((SUBLANE, nr, LANE), F32),
                        pltpu.VMEM((t_rows, LANE), F32), pltpu.VMEM((kb_rows, SUBLANE, LANE), F32)],
        args=(hin, hin, hin, hin, hin, dy, dz, wa, wb), name="mix_conv_bwd", comm=comm)


def _bwd_in_proj(dh5, win_t, h0, dh1, g1, tm, comm):
    t_rows, d = h0.shape
    groups, _, w = dh5.shape

    def body(dh_ref, w_ref, h0_ref, dh1_ref, g1_ref, dh0_ref, dg1_ref):
        dxn1 = None
        for g in range(groups):
            part = _dot_nn(dh_ref[g], w_ref[pl.ds(g * w, w), :])
            dxn1 = part if dxn1 is None else dxn1 + part
        h0 = h0_ref[...]
        r1 = _rstd(h0)
        h0h = h0 * r1
        _acc_rows(dg1_ref, dxn1 * h0h, pl.program_id(0) == 0)
        dh0_ref[...] = dh1_ref[...] + _rms_bwd(dxn1 * g1_ref[...], h0h, r1)

    blk = pl.BlockSpec((tm, d), lambda i: (i, 0))
    return _host_call(
        body, grid=(t_rows // tm,),
        in_specs=[pl.BlockSpec((groups, tm, w), lambda i: (0, i, 0)), _resident(win_t.shape), blk, blk, _full((1, d))],
        out_specs=[blk, _full((1, d))],
        out_shape=[jax.ShapeDtypeStruct((t_rows, d), F32), jax.ShapeDtypeStruct((1, d), F32)],
        args=(dh5, win_t, h0, dh1, g1), name="bwd_in_proj", comm=comm)


def _reduce_small(smalls, d, comm):
    (dmeta, dg1, dg2, dg3, dg4, dbb, dlg, dlb, lossv, dwa, dwb) = smalls
    half = d // 2
    kb_rows = dwb.shape[0]

    def body(dmeta_ref, dg1_ref, dg2_ref, dg3_ref, dg4_ref, dbb_ref, dlg_ref, dlb_ref, loss_ref, dwa_ref, dwb_ref,
             ptot_ref, pbuf, psib, chip_p, ps_send, ps_recv, pc_send, pc_recv):
        x, y, c = _mesh_pos()
        pbuf[...] = jnp.zeros_like(pbuf)
        pbuf[pl.ds(0, N_META), :] = dmeta_ref[...]
        for row, ref in ((16, dg1_ref), (17, dg2_ref), (18, dg3_ref), (19, dg4_ref)):
            pbuf[pl.ds(row, 1), :] = ref[...]
        pbuf[pl.ds(20, 1), pl.ds(0, half)] = dbb_ref[...]
        pbuf[pl.ds(20, 1), pl.ds(half, half)] = dlg_ref[...]
        pbuf[pl.ds(21, 1), pl.ds(0, half)] = dlb_ref[...]
        lv = loss_ref[...]
        pbuf[pl.ds(21, 1), pl.ds(half, half)] = lv[:, :half] + lv[:, half:]
        pbuf[pl.ds(24, SUBLANE), pl.ds(0, half)] = dwa_ref[...]
        pbuf[pl.ds(32, kb_rows), pl.ds(0, half)] = dwb_ref[...]
        to_sib = _remote(pbuf, psib, ps_send.at[0], ps_recv.at[0], (x, y, 1 - c))
        to_sib.start()
        to_sib.wait_recv()
        my_chip = 2 * x + y
        chip_p[my_chip] = pbuf[...] + psib[...]
        to_sib.wait_send()
        slot = chip_p.at[my_chip]
        cps = [_remote(slot, slot, pc_send.at[k], pc_recv.at[k], (*chip, c)) for k, chip in enumerate(_other_chips(x, y))]
        for cp in cps:
            cp.start()
        for cp in cps:
            cp.wait_recv()
        ptot_ref[...] = ((chip_p[0] + chip_p[1]) + chip_p[2]) + chip_p[3]
        for cp in cps:
            cp.wait_send()

    return _host_call(
        body, grid=(), in_specs=[VMEM] * 11, out_specs=[VMEM], out_shape=[jax.ShapeDtypeStruct((SMALL_ROWS, d), F32)],
        scratch_shapes=[pltpu.VMEM((SMALL_ROWS, d), F32), pltpu.VMEM((SMALL_ROWS, d), F32), pltpu.VMEM((4, SMALL_ROWS, d), F32),
                        pltpu.SemaphoreType.DMA((1,)), pltpu.SemaphoreType.DMA((1,)),
                        pltpu.SemaphoreType.DMA((3,)), pltpu.SemaphoreType.DMA((3,))],
        args=smalls, name="reduce_small", comm=comm)


def _adamw(w, g, m, v):
    m = ADAM_B1 * m + (1.0 - ADAM_B1) * g
    v = ADAM_B2 * v + (1.0 - ADAM_B2) * jnp.square(g)
    m_hat = m / (1.0 - ADAM_B1 ** ADAM_STEP)
    v_hat = v / (1.0 - ADAM_B2 ** ADAM_STEP)
    delta = -ADAM_LR * (m_hat / (jnp.sqrt(v_hat) + ADAM_EPS) + ADAM_WD * w)
    return delta, m, v


def _adam_big(g, pair, part, w, m, v, name):
    r, d = w.shape
    cols = d // ADAM_COL_BLOCKS

    def body(me_ref, g_ref, pair_ref, part_ref, w_ref, m_ref, v_ref, go_ref, d_ref, mo_ref, vo_ref):
        g = g_ref[...].astype(F32) + pair_ref[...].astype(F32)
        for k in range(3):
            g = g + part_ref[k].astype(F32)
        go_ref[...] = g
        d_ref[...], mo_ref[...], vo_ref[...] = _adamw(w_ref[...], g, m_ref[...], v_ref[...])

    blk = pl.BlockSpec((r, cols), lambda i, me_ref: (0, i))
    grid_spec = pltpu.PrefetchScalarGridSpec(
        num_scalar_prefetch=1, grid=(ADAM_COL_BLOCKS,),
        in_specs=[pl.BlockSpec((r, cols), lambda i, me_ref: (me_ref[0], i)),
                  pl.BlockSpec((None, r, cols), lambda i, me_ref: (0, 0, i)),
                  pl.BlockSpec((3, r, cols), lambda i, me_ref: (0, 0, i)), blk, blk, blk],
        out_specs=[blk, blk, blk, blk])
    me = jnp.reshape(_dev_index(*_mesh_pos()), (1,)).astype(jnp.int32)
    return pl.pallas_call(body, out_shape=[jax.ShapeDtypeStruct((r, d), F32)] * 4, grid_spec=grid_spec, name=name,
                          compiler_params=_cparams(1))(me, g, pair, part, w, m, v)


def _adam_small(gs, ws, ms, vs):
    n = len(gs)

    def body(*refs):
        ins, outs = refs[:4 * n], refs[4 * n:]
        for i in range(n):
            g = ins[i][...]
            delta, m, v = _adamw(ins[n + i][...], g, ins[2 * n + i][...], ins[3 * n + i][...])
            outs[i][...] = delta
            outs[n + i][...] = m
            outs[2 * n + i][...] = v

    shapes = [jax.ShapeDtypeStruct(w.shape, F32) for w in ws]
    return pl.pallas_call(body, out_shape=shapes * 3, name="adam_small", compiler_params=_cparams())(*gs, *ws, *ms, *vs)


def kernel(x, meta_tokens, pre_mix_norm, w_in, conv_a_w, conv_b_w, conv_b_bias, ln_b_gain, ln_b_bias, w_out, post_mix_norm, pre_ffn_norm, w_gate, w_up, w_down, post_ffn_norm, loss_target, m_meta_tokens, m_pre_mix_norm, m_w_in, m_conv_a_w, m_conv_b_w, m_conv_b_bias, m_ln_b_gain, m_ln_b_bias, m_w_out, m_post_mix_norm, m_pre_ffn_norm, m_w_gate, m_w_up, m_w_down, m_post_ffn_norm, v_meta_tokens, v_pre_mix_norm, v_w_in, v_conv_a_w, v_conv_b_w, v_conv_b_bias, v_ln_b_gain, v_ln_b_bias, v_w_out, v_post_mix_norm, v_pre_ffn_norm, v_w_gate, v_w_up, v_w_down, v_post_ffn_norm):
    _, seq, d = x.shape
    ka, ca_loc = conv_a_w.shape[1:]
    kb, cb_loc = conv_b_w.shape[1:]
    wa_w = ca_loc * N_DEV
    assert cb_loc == ca_loc and wa_w % LANE == 0 and w_in.shape[2] * N_DEV == 5 * wa_w
    pad = (-(N_META + seq)) % ROW_ALIGN
    x0 = pad + N_META
    t_rows = x0 + seq
    assert t_rows % (N_ROW_BLOCKS * BF16_ROWS) == 0 and t_rows % CONV_CHUNK == 0 and d % LANE == 0
    tm = t_rows // N_ROW_BLOCKS
    me = _dev_index(*_mesh_pos())

    def as_rows(w_in_like, w_out_like, w_gate_like, w_up_like, w_down_like):
        return (w_in_like[0].T, w_out_like[0], w_gate_like[0].T, w_up_like[0].T, w_down_like[0])

    w_loc = as_rows(w_in, w_out, w_gate, w_up, w_down)
    rows = [w.shape[0] for w in w_loc]
    assert all(r % ADD_CHUNK == 0 for r in rows)
    P_IN, P_OUT, P_GATE, P_UP, P_DOWN = range(N_BIG)
    gate_cut = (rows[P_GATE] // 2) // (2 * BF16_ROWS) * (2 * BF16_ROWS)
    up_cut = (rows[P_UP] // 5) // (2 * BF16_ROWS) * (2 * BF16_ROWS)
    jobs_first = [(P_IN, 0, rows[P_IN])]
    jobs_prep = [(P_OUT, 0, rows[P_OUT])]
    jobs_in_proj = [(P_GATE, 0, gate_cut)]
    jobs_conv = [(P_GATE, gate_cut, rows[P_GATE] - gate_cut), (P_UP, 0, up_cut)]
    jobs_out_proj = [(P_UP, up_cut, rows[P_UP] - up_cut)]
    jobs_gate_up = [(P_DOWN, 0, rows[P_DOWN])]

    sm = jnp.zeros((SM_ROWS, LANE), F32)
    sm = sm.at[0:N_META, :].set(meta_tokens)
    sm = sm.at[16:16 + ka, 0:ca_loc].set(conv_a_w[0])
    sm = sm.at[24:24 + kb, 0:cb_loc].set(conv_b_w[0])
    wl, wfull, sm_all = _gather_first(w_loc, sm, jobs_first)
    meta_full = jnp.transpose(sm_all[:, 0:N_META, :], (1, 0, 2)).reshape(N_META, d)
    wa = jnp.transpose(sm_all[:, 16:16 + ka, 0:ca_loc], (1, 0, 2)).reshape(ka, wa_w)
    wb = jnp.transpose(sm_all[:, 24:24 + kb, 0:cb_loc], (1, 0, 2)).reshape(kb, wa_w)

    def gather(jobs):
        return _gather_comm(wl, {p: wfull[p] for p in sorted({j[0] for j in jobs})}, jobs, rows)

    def gathered(jobs, extra):
        for p, arr in zip(sorted({j[0] for j in jobs}), extra):
            wfull[p] = arr

    (h0, tgt), extra = _prep(x[0], loss_target[0], meta_full, t_rows, x0, gather(jobs_prep))
    gathered(jobs_prep, extra)
    (xn1, hin), extra = _in_proj(h0, pre_mix_norm, wfull[P_IN], tm, gather(jobs_in_proj))
    gathered(jobs_in_proj, extra)
    (ya, z), extra = _mix_conv_fwd(hin, wa, wb, conv_b_bias, wa_w, gather(jobs_conv))
    gathered(jobs_conv, extra)
    y = _mix_ln_fwd(ya, z, ln_b_gain, ln_b_bias, tm)
    (mix, h1, xn2), extra = _out_proj(y, wfull[P_OUT], h0, post_mix_norm, pre_ffn_norm, tm, gather(jobs_out_proj))
    gathered(jobs_out_proj, extra)
    (a, u, s), extra = _gate_up(xn2, wfull[P_GATE], wfull[P_UP], tm, gather(jobs_gate_up))
    gathered(jobs_gate_up, extra)
    dh2, dff, dg4, lossv = _down_loss(s, wfull[P_DOWN], h1, tgt, post_ffn_norm, tm, x0)

    gwd, _ = _wgrad(s, dff, "wgrad_down")
    (da, du), (pair_d,) = _bwd_down(dff, wfull[P_DOWN], a, u, tm, _pair_comm(gwd, rows[P_DOWN]))
    gwg, _ = _wgrad(da, xn2, "wgrad_gate")
    gwu, _ = _wgrad(du, xn2, "wgrad_up")
    (dh1, dg3), (part_d, pair_g, pair_u) = _bwd_ffn_in(
        da, du, wfull[P_GATE], wfull[P_UP], h1, dh2, pre_ffn_norm, tm,
        _merge_comms([_chip_comm(gwd, pair_d, rows[P_DOWN]), _pair_comm(gwg, rows[P_GATE]), _pair_comm(gwu, rows[P_UP])]))
    dmix, dy, dg2 = _bwd_out_proj(dh1, mix, wfull[P_OUT], post_mix_norm, tm)
    gwo, _ = _wgrad(y, dmix, "wgrad_out")
    dz, dlg, dlb, dbb = _mix_ln_bwd(z, dy, ln_b_gain, ln_b_bias, tm)
    (dh5, dwa, dwb), (part_g, part_u, pair_o) = _mix_conv_bwd(
        hin, dy, dz, wa, wb, wa_w,
        _merge_comms([_chip_comm(gwg, pair_g, rows[P_GATE]), _chip_comm(gwu, pair_u, rows[P_UP]), _pair_comm(gwo, rows[P_OUT])]))
    gwi, (part_o,) = _wgrad(dh5, xn1, "wgrad_in", _chip_comm(gwo, pair_o, rows[P_OUT]))
    (dh0, dg1), (pair_i, part_i) = _bwd_in_proj(dh5, wfull[P_IN], h0, dh1, pre_mix_norm, tm, _pair_chip_comm(gwi, rows[P_IN]))
    grad_x = dh0[x0:][None]
    dmeta = dh0[x0 - N_META:x0]
    (ptot,), _ = _reduce_small((dmeta, dg1, dg2, dg3, dg4, dbb, dlg, dlb, lossv, dwa, dwb), d, None)

    half = d // 2
    loss = (0.5 / d) * jnp.sum(ptot[21, half:])
    g_meta = lax.dynamic_slice(ptot, (0, me * (d // N_DEV)), (N_META, d // N_DEV))
    g_small = [g_meta, ptot[16:17], lax.dynamic_slice(ptot, (24, me * ca_loc), (ka, ca_loc))[None],
               lax.dynamic_slice(ptot, (32, me * cb_loc), (kb, cb_loc))[None],
               ptot[20:21, :half], ptot[20:21, half:], ptot[21:22, :half], ptot[17:18], ptot[18:19], ptot[19:20]]
    w_small = [meta_tokens, pre_mix_norm, conv_a_w, conv_b_w, conv_b_bias, ln_b_gain, ln_b_bias, post_mix_norm,
               pre_ffn_norm, post_ffn_norm]
    m_small = [m_meta_tokens, m_pre_mix_norm, m_conv_a_w, m_conv_b_w, m_conv_b_bias, m_ln_b_gain, m_ln_b_bias,
               m_post_mix_norm, m_pre_ffn_norm, m_post_ffn_norm]
    v_small = [v_meta_tokens, v_pre_mix_norm, v_conv_a_w, v_conv_b_w, v_conv_b_bias, v_ln_b_gain, v_ln_b_bias,
               v_post_mix_norm, v_pre_ffn_norm, v_post_ffn_norm]
    small = _adam_small(g_small, w_small, m_small, v_small)
    n_small = len(w_small)
    d_small, nm_small, nv_small = small[:n_small], small[n_small:2 * n_small], small[2 * n_small:]

    m_loc = as_rows(m_w_in, m_w_out, m_w_gate, m_w_up, m_w_down)
    v_loc = as_rows(v_w_in, v_w_out, v_w_gate, v_w_up, v_w_down)
    full_grads = (gwi, gwo, gwg, gwu, gwd)
    pairs = (pair_i, pair_o, pair_g, pair_u, pair_d)
    parts = (part_i, part_o, part_g, part_u, part_d)
    bigs = {}
    for p, (name, tr) in enumerate((("w_in", True), ("w_out", False), ("w_gate", True), ("w_up", True), ("w_down", False))):
        res = _adam_big(full_grads[p], pairs[p], parts[p], w_loc[p], m_loc[p], v_loc[p], "adam_" + name)
        bigs[name] = [(o.T if tr else o)[None] for o in res]

    def ordered(pick_small, pick_big):
        sm_it = iter(range(n_small))
        out = []
        for name in ("s", "s", "w_in", "s", "s", "s", "s", "s", "w_out", "s", "s", "w_gate", "w_up", "w_down", "s"):
            out.append(pick_small(next(sm_it)) if name == "s" else pick_big(name))
        return out

    grads = ordered(lambda i: g_small[i], lambda n: bigs[n][0])
    deltas = ordered(lambda i: d_small[i], lambda n: bigs[n][1])
    new_m = ordered(lambda i: nm_small[i], lambda n: bigs[n][2])
    new_v = ordered(lambda i: nv_small[i], lambda n: bigs[n][3])
    return (loss, grad_x, *grads, *deltas, *new_m, *new_v)
```

```python
import jax
import jax.numpy as jnp
from jax import lax
from jax.experimental import pallas as pl
from jax.experimental.pallas import tpu as pltpu

F32 = jnp.float32
BF16 = jnp.bfloat16
MESH = pl.DeviceIdType.MESH

N_META = 16
N_DEV = 8
RMS_EPS = 1e-6
LN_EPS = 1e-5
ADAM_LR = 0.001
ADAM_B1 = 0.9
ADAM_B2 = 0.999
ADAM_EPS = 1e-08
ADAM_WD = 0.01
ADAM_STEP = 10

LANE = 128
SUBLANE = 8
BF16_ROWS = 16
ROW_ALIGN = 128
N_ROW_BLOCKS = 4
CONV_HALO = 32
CONV_CHUNK = 64
WGRAD_ROWS = 32
N_CHUNK = 512
WGRAD_TILE = 256
ADD_CHUNK = 32
ADAM_COL_BLOCKS = 4
V7X_VMEM_BYTES = 64 * 1024 * 1024
VMEM_LIMIT = V7X_VMEM_BYTES - 6 * 1024 * 1024
SMALL_ROWS = 64
SM_ROWS = 56
N_BIG = 5

ANY = pl.BlockSpec(memory_space=pl.ANY)
VMEM = pl.BlockSpec(memory_space=pltpu.VMEM)


def _cparams(n_grid_axes=0):
    sem = ("arbitrary",) * n_grid_axes if n_grid_axes else None
    return pltpu.CompilerParams(dimension_semantics=sem, vmem_limit_bytes=VMEM_LIMIT)


def _mesh_pos():
    return lax.axis_index("x"), lax.axis_index("y"), lax.axis_index("c")


def _dev_index(px, py, pc):
    return 4 * px + 2 * py + pc


def _other_chips(x, y):
    return [(1 - x, y), (x, 1 - y), (1 - x, 1 - y)]


def _full(shape):
    return pl.BlockSpec(shape, lambda *_: (0,) * len(shape))


def _resident(shape):
    return pl.BlockSpec(shape, lambda *_: (0,) * len(shape), pipeline_mode=pl.Buffered(1))


def _dot_nt(a, w):
    return lax.dot_general(a, w, (((1,), (1,)), ((), ())), preferred_element_type=F32)


def _dot_nn(a, w):
    return jnp.dot(a, w, preferred_element_type=F32)


def _chunks(n, c):
    out, o = [], 0
    while o < n:
        out.append((o, min(c, n - o)))
        o += c
    return out


def _rstd(h):
    return lax.rsqrt(jnp.mean(h * h, axis=-1, keepdims=True) + RMS_EPS)


def _rms_bwd(dyh, yh, r):
    return r * (dyh - yh * jnp.mean(dyh * yh, axis=-1, keepdims=True))


def _silu_grad(a, sig):
    return sig * (1.0 + a * (1.0 - sig))


def _acc_rows(ref, val, first):
    s = jnp.sum(val, axis=0, keepdims=True)

    @pl.when(first)
    def _():
        ref[...] = s

    @pl.when(jnp.logical_not(first))
    def _():
        ref[...] += s


def _row_loop(t_rows, chunk, fn, carry=None):
    def step(i, c):
        return fn(pl.multiple_of(i * chunk, chunk), c)

    return lax.fori_loop(0, t_rows // chunk, step, carry)


def _remote(src, dst, send_sem, recv_sem, to):
    return pltpu.make_async_remote_copy(src_ref=src, dst_ref=dst, send_sem=send_sem, recv_sem=recv_sem,
                                        device_id=to, device_id_type=MESH)


class _SplitRemote:
    def __init__(self, src, dst, send_sem, recv_sem, to, rows, n_chunks):
        units = rows // BF16_ROWS
        n_chunks = max(1, min(n_chunks, units))
        sizes = [(units // n_chunks + (i < units % n_chunks)) * BF16_ROWS for i in range(n_chunks)]
        self.whole = _remote(src, dst, send_sem, recv_sem, to)
        self.parts, o = [], 0
        for n in sizes:
            self.parts.append(_remote(src.at[pl.ds(o, n), :], dst.at[pl.ds(o, n), :], send_sem, recv_sem, to))
            o += n

    def start(self):
        for cp in self.parts:
            cp.start()

    def wait_recv(self):
        self.whole.wait_recv()

    def wait_send(self):
        self.whole.wait_send()


class _Comm:
    def __init__(self, inputs, out_shapes, aliases, scratch, start, finish, mid=None):
        self.inputs, self.out_shapes, self.aliases, self.scratch = list(inputs), list(out_shapes), dict(aliases), list(scratch)
        self.start, self.finish, self.mid = start, finish, mid


def _merge_comms(comms):
    inputs, out_shapes, aliases, scratch, spans = [], [], {}, [], []
    for cm in comms:
        spans.append((len(inputs), len(out_shapes), len(scratch), cm))
        aliases.update({len(inputs) + k: len(out_shapes) + v for k, v in cm.aliases.items()})
        inputs += cm.inputs
        out_shapes += cm.out_shapes
        scratch += cm.scratch

    def run(which):
        def fn(ins, outs, scr):
            for i0, o0, s0, cm in spans:
                if getattr(cm, which) is not None:
                    getattr(cm, which)(ins[i0:i0 + len(cm.inputs)], outs[o0:o0 + len(cm.out_shapes)], scr[s0:s0 + len(cm.scratch)])
        return fn

    mid = run("mid") if any(cm.mid is not None for cm in comms) else None
    return _Comm(inputs, out_shapes, aliases, scratch, run("start"), run("finish"), mid)


def _host_call(body, *, grid, in_specs, out_specs, out_shape, args, name, scratch_shapes=(), comm=None):
    if comm is None:
        res = pl.pallas_call(body, grid=grid, in_specs=list(in_specs), out_specs=list(out_specs), out_shape=list(out_shape),
                             scratch_shapes=list(scratch_shapes), name=name, compiler_params=_cparams(len(grid)))(*args)
        return list(res), []
    n_in, n_out, n_scr = len(args), len(out_shape), len(scratch_shapes)
    c_in, c_out = len(comm.inputs), len(comm.out_shapes)

    def hosted(*refs):
        ins, c_ins = refs[:n_in], refs[n_in:n_in + c_in]
        o0 = n_in + c_in
        outs, c_outs = refs[o0:o0 + n_out], refs[o0 + n_out:o0 + n_out + c_out]
        s0 = o0 + n_out + c_out
        scr, c_scr = refs[s0:s0 + n_scr], refs[s0 + n_scr:]
        if not grid:
            comm.start(c_ins, c_outs, c_scr)
            body(*ins, *outs, *scr)
            comm.finish(c_ins, c_outs, c_scr)
            return
        first = last = None
        for a, n in enumerate(grid):
            f, l = pl.program_id(a) == 0, pl.program_id(a) == n - 1
            first = f if first is None else jnp.logical_and(first, f)
            last = l if last is None else jnp.logical_and(last, l)

        @pl.when(first)
        def _():
            comm.start(c_ins, c_outs, c_scr)

        if comm.mid is not None:
            assert len(grid) == 1 and grid[0] >= 3

            @pl.when(pl.program_id(0) == 1)
            def _():
                comm.mid(c_ins, c_outs, c_scr)

        body(*ins, *outs, *scr)

        @pl.when(last)
        def _():
            comm.finish(c_ins, c_outs, c_scr)

    res = pl.pallas_call(
        hosted, grid=grid, in_specs=list(in_specs) + [ANY] * c_in, out_specs=list(out_specs) + [ANY] * c_out,
        out_shape=list(out_shape) + comm.out_shapes, scratch_shapes=list(scratch_shapes) + comm.scratch,
        input_output_aliases={n_in + k: n_out + v for k, v in comm.aliases.items()},
        name=name, compiler_params=_cparams(len(grid)))(*args, *comm.inputs)
    return list(res[:n_out]), list(res[n_out:])


GATHER_SEMS = 10
D2D_CHUNKS = 8


class _Gather:
    def __init__(self, jobs, rows, lo, src_ref, dests, send_sems, recv_sems):
        x, y, c = _mesh_pos()
        me, sib = (x, y, c), (x, y, 1 - c)
        nx, ny, dg = (1 - x, y, c), (x, 1 - y, c), (1 - x, 1 - y, c)
        self.relayed, self.direct, self.relay, self.to_sib, self.sib_fwd = [], [], [], [], []
        for n, (p, r0, nr) in enumerate(jobs):
            assert nr % (2 * BF16_ROWS) == 0
            half = nr // 2

            def rows_of(dev, h, p=p, r0=r0, nr=nr, half=half):
                off, cnt = (r0, nr) if h is None else (r0 + h * half, half)
                return dests[p].at[pl.ds(pl.multiple_of(_dev_index(*dev) * rows[p] + off, BF16_ROWS), cnt), :]

            def mine(h, p=p, r0=r0, nr=nr, half=half):
                off, cnt = (r0, nr) if h is None else (r0 + h * half, half)
                return src_ref.at[pl.ds(lo[p] + off, cnt), :]

            sem = lambda k, n=n: (send_sems.at[GATHER_SEMS * n + k], recv_sems.at[GATHER_SEMS * n + k])
            self.relayed.append([_remote(mine(0), rows_of(me, 0), *sem(0), nx), _remote(mine(1), rows_of(me, 1), *sem(3), ny)])
            self.direct.append([_remote(mine(1), rows_of(me, 1), *sem(1), nx), _remote(mine(0), rows_of(me, 0), *sem(2), ny)])
            self.relay.append([_remote(rows_of(nx, 0), rows_of(nx, 0), *sem(4), ny), _remote(rows_of(ny, 1), rows_of(ny, 1), *sem(5), nx)])
            self.to_sib.append(_SplitRemote(mine(None), rows_of(me, None), *sem(6), sib, nr, D2D_CHUNKS))
            self.sib_fwd.append([_SplitRemote(rows_of(dev, None), rows_of(dev, None), *sem(7 + i), sib, nr, D2D_CHUNKS)
                                 for i, dev in enumerate((nx, ny, dg))])

    def start(self):
        for group in (self.relayed, self.direct):
            for cps in group:
                for cp in cps:
                    cp.start()
        for cp in self.to_sib:
            cp.start()

    def mid(self):
        for first, relay in zip(self.relayed, self.relay):
            for arrived, onward in zip(first, relay):
                arrived.wait_recv()
                onward.start()

    def finish(self):
        for direct, relay, fwd in zip(self.direct, self.relay, self.sib_fwd):
            for k in range(2):
                direct[k].wait_recv()
                fwd[k].start()
            for cp in relay:
                cp.wait_recv()
            fwd[2].start()
        for n in range(len(self.to_sib)):
            self.to_sib[n].wait_recv()
            for cp in self.sib_fwd[n]:
                cp.wait_recv()
            for cp in self.relayed[n] + self.direct[n] + self.relay[n] + [self.to_sib[n]] + self.sib_fwd[n]:
                cp.wait_send()


def _gather_comm(wl, dests, jobs, rows):
    lo = [sum(rows[:p]) for p in range(N_BIG)]
    ps = sorted(dests)
    slot = {p: i for i, p in enumerate(ps)}

    def gather(ins, outs, scr):
        return _Gather(jobs, rows, lo, ins[0], {p: outs[slot[p]] for p in ps}, scr[0], scr[1])

    n_sems = GATHER_SEMS * len(jobs)
    return _Comm([wl] + [dests[p] for p in ps], [jax.ShapeDtypeStruct(dests[p].shape, BF16) for p in ps],
                 {1 + i: i for i in range(len(ps))},
                 [pltpu.SemaphoreType.DMA((n_sems,)), pltpu.SemaphoreType.DMA((n_sems,))],
                 lambda *a: gather(*a).start(), lambda *a: gather(*a).finish(), lambda *a: gather(*a).mid())


def _pair_comm(g, r):
    d = g.shape[1]

    def descs(ins, outs, scr):
        x, y, c = _mesh_pos()
        chips = [(x, y)] + _other_chips(x, y)
        return [_SplitRemote(ins[0].at[pl.ds(pl.multiple_of(_dev_index(*chip, 1 - c) * r, BF16_ROWS), r), :], outs[0].at[k],
                             scr[0].at[k], scr[1].at[k], (x, y, 1 - c), r, D2D_CHUNKS) for k, chip in enumerate(chips)]

    def start(ins, outs, scr):
        for cp in descs(ins, outs, scr):
            cp.start()

    def finish(ins, outs, scr):
        cps = descs(ins, outs, scr)
        for cp in cps:
            cp.wait_recv()
        for cp in cps:
            cp.wait_send()

    comm = _Comm([g], [jax.ShapeDtypeStruct((4, r, d), BF16)], {},
                 [pltpu.SemaphoreType.DMA((4,)), pltpu.SemaphoreType.DMA((4,))], start, finish)
    comm.descs = descs
    return comm


def _chip_comm(g, pair, r):
    d = g.shape[1]
    return _Comm([g, pair], [jax.ShapeDtypeStruct((3, r, d), BF16)], {}, _chip_scratch(r, d),
                 lambda ins, outs, scr: _chip_send(ins[0], ins[1], outs[0], scr, r),
                 lambda ins, outs, scr: _chip_wait(outs[0], scr))


def _chip_scratch(r, d):
    return [pltpu.VMEM((r, d), BF16), pltpu.VMEM((r, d), BF16), pltpu.VMEM((3, r, d), BF16),
            pltpu.SemaphoreType.DMA((3,)), pltpu.SemaphoreType.DMA((3,)), pltpu.SemaphoreType.DMA((2,))]


def _chip_copies(part_ref, scr):
    x, y, c = _mesh_pos()
    return [_remote(scr[2].at[k], part_ref.at[k], scr[3].at[k], scr[4].at[k], (*chip, c))
            for k, chip in enumerate(_other_chips(x, y))]


def _chip_send(g_ref, pair_ref, part_ref, scr, r):
    x, y, c = _mesh_pos()
    gbuf, abuf, sendbuf, local = scr[0], scr[1], scr[2], scr[5]
    cps = _chip_copies(part_ref, scr)
    for k, chip in enumerate(_other_chips(x, y)):
        j = _dev_index(*chip, c)
        loads = [pltpu.make_async_copy(g_ref.at[pl.ds(pl.multiple_of(j * r, BF16_ROWS), r), :], gbuf, local.at[0]),
                 pltpu.make_async_copy(pair_ref.at[1 + k], abuf, local.at[1])]
        for cp in loads:
            cp.start()
        for cp in loads:
            cp.wait()

        def add(b, carry, k=k):
            sl = pl.ds(b, ADD_CHUNK)
            sendbuf[k, sl, :] = (gbuf[sl, :].astype(F32) + abuf[sl, :].astype(F32)).astype(BF16)
            return carry

        _row_loop(r, ADD_CHUNK, add)
        cps[k].start()


def _chip_wait(part_ref, scr):
    cps = _chip_copies(part_ref, scr)
    for cp in cps:
        cp.wait_recv()
    for cp in cps:
        cp.wait_send()


def _pair_chip_comm(g, r):
    d = g.shape[1]
    pair = _pair_comm(g, r)
    n_pair = len(pair.scratch)

    def mid(ins, outs, scr):
        for cp in pair.descs(ins, outs[:1], scr[:n_pair]):
            cp.wait_recv()
        _chip_send(ins[0], outs[0], outs[1], scr[n_pair:], r)

    def finish(ins, outs, scr):
        _chip_wait(outs[1], scr[n_pair:])
        for cp in pair.descs(ins, outs[:1], scr[:n_pair]):
            cp.wait_send()

    return _Comm([g], pair.out_shapes + [jax.ShapeDtypeStruct((3, r, d), BF16)], {}, pair.scratch + _chip_scratch(r, d),
                 lambda ins, outs, scr: pair.start(ins, outs[:1], scr[:n_pair]), finish, mid)


def _gather_first(shards, sm, jobs):
    d = shards[0].shape[1]
    rows = [w.shape[0] for w in shards]
    lo = [sum(rows[:p]) for p in range(N_BIG)]
    n_sems = GATHER_SEMS * len(jobs)

    def body(s0, s1, s2, s3, s4, sm_ref, wl_ref, o0, o1, o2, o3, o4, sa_ref, wl_v, send_sems, recv_sems, ssend, srecv, local_sems):
        dests = (o0, o1, o2, o3, o4)
        x, y, c = _mesh_pos()
        me = (x, y, c)
        jme = _dev_index(*me)
        for p, ref in enumerate((s0, s1, s2, s3, s4)):
            wl_v[pl.ds(lo[p], rows[p]), :] = ref[...].astype(BF16)
        gather = _Gather(jobs, rows, lo, wl_v, dict(enumerate(dests)), send_sems, recv_sems)
        gather.start()
        peers = [(x, y, 1 - c)] + [(*chip, pc) for pc in (c, 1 - c) for chip in _other_chips(x, y)]
        smalls = [_remote(sm_ref, sa_ref.at[jme], ssend.at[k], srecv.at[k], to) for k, to in enumerate(peers)]
        for cp in smalls:
            cp.start()
        mine = [pltpu.make_async_copy(wl_v.at[pl.ds(lo[p], rows[p]), :],
                                      dests[p].at[pl.ds(pl.multiple_of(jme * rows[p], BF16_ROWS), rows[p]), :], local_sems.at[p])
                for p in range(N_BIG)]
        mine.append(pltpu.make_async_copy(wl_v, wl_ref, local_sems.at[N_BIG]))
        mine.append(pltpu.make_async_copy(sm_ref, sa_ref.at[jme], local_sems.at[N_BIG + 1]))
        for cp in mine:
            cp.start()
        gather.mid()
        gather.finish()
        for cp in smalls:
            cp.wait_recv()
        for cp in smalls:
            cp.wait_send()
        for cp in mine:
            cp.wait()

    out_shape = [jax.ShapeDtypeStruct((sum(rows), d), BF16)]
    out_shape += [jax.ShapeDtypeStruct((N_DEV * r, d), BF16) for r in rows]
    out_shape.append(jax.ShapeDtypeStruct((N_DEV,) + sm.shape, F32))
    res = pl.pallas_call(
        body, out_shape=out_shape, in_specs=[VMEM] * 6, out_specs=[ANY] * 7,
        scratch_shapes=[pltpu.VMEM((sum(rows), d), BF16), pltpu.SemaphoreType.DMA((n_sems,)), pltpu.SemaphoreType.DMA((n_sems,)),
                        pltpu.SemaphoreType.DMA((7,)), pltpu.SemaphoreType.DMA((7,)), pltpu.SemaphoreType.DMA((N_BIG + 2,))],
        name="gather_first", compiler_params=_cparams())(*shards, sm)
    return res[0], list(res[1:1 + N_BIG]), res[-1]


def _prep(x2, tgt2, meta_full, t_rows, x0, comm):
    s, d = x2.shape
    assert x0 == ROW_ALIGN and s % ROW_ALIGN == 0

    def body(x_ref, tgt_ref, meta_ref, h0_ref, tp_ref):
        i = pl.program_id(0)

        @pl.when(i == 0)
        def _():
            h0_ref[...] = jnp.zeros_like(h0_ref)
            h0_ref[pl.ds(x0 - N_META, N_META), :] = meta_ref[...]
            tp_ref[...] = jnp.zeros_like(tp_ref)

        @pl.when(i > 0)
        def _():
            h0_ref[...] = x_ref[...]
            tp_ref[...] = tgt_ref[...]

    src = pl.BlockSpec((ROW_ALIGN, d), lambda i: (jnp.maximum(i - 1, 0), 0))
    dst = pl.BlockSpec((ROW_ALIGN, d), lambda i: (i, 0))
    return _host_call(body, grid=(t_rows // ROW_ALIGN,), in_specs=[src, src, _full((N_META, d))], out_specs=[dst, dst],
                      out_shape=[jax.ShapeDtypeStruct((t_rows, d), F32)] * 2, args=(x2, tgt2, meta_full), name="prep", comm=comm)


def _in_proj(h0, g1, win_t, tm, comm):
    t_rows, d = h0.shape
    e = win_t.shape[0]

    def body(h_ref, g_ref, w_ref, xn_ref, hin_ref):
        h = h_ref[...]
        xn = ((h * _rstd(h)) * g_ref[...]).astype(BF16)
        xn_ref[...] = xn
        for o, n in _chunks(e, N_CHUNK):
            hin_ref[:, pl.ds(o, n)] = _dot_nt(xn, w_ref[pl.ds(o, n), :])

    return _host_call(
        body, grid=(t_rows // tm,),
        in_specs=[pl.BlockSpec((tm, d), lambda i: (i, 0)), _full((1, d)), _resident((e, d))],
        out_specs=[pl.BlockSpec((tm, d), lambda i: (i, 0)), pl.BlockSpec((tm, e), lambda i: (i, 0))],
        out_shape=[jax.ShapeDtypeStruct((t_rows, d), BF16), jax.ShapeDtypeStruct((t_rows, e), F32)],
        args=(h0, g1, win_t), name="in_proj", comm=comm)


def _tap_slot(off):
    return off % SUBLANE, (off // SUBLANE) * SUBLANE


def _fill_shifted(sh_ref, base_ref, residues, n_rows):
    for r in residues:
        sh_ref[r] = base_ref[pl.ds(r, n_rows), :]


def _mix_conv_fwd(hin, wa, wb, bb, wa_w, comm):
    t_rows = hin.shape[0]
    nt = wa_w // LANE
    ka, kb = wa.shape[0], wb.shape[0]
    nr = CONV_HALO + t_rows

    def body(bg_ref, cg_ref, ha_ref, val_ref, gt_ref, wa_ref, wb_ref, bb_ref, ya_ref, z_ref, base, sh):
        base[pl.ds(0, CONV_HALO), :] = jnp.zeros((CONV_HALO, LANE), F32)
        base[pl.ds(nr, SUBLANE), :] = jnp.zeros((SUBLANE, LANE), F32)

        def conv(w_ref, k_taps, b, n):
            acc = None
            for k in range(k_taps):
                r, q = _tap_slot(CONV_HALO - (k_taps - 1) + k)
                term = w_ref[pl.ds(k, 1), :] * sh[r, pl.ds(b + q, n), :]
                acc = term if acc is None else acc + term
            return acc

        def fill_a(b, c):
            base[pl.ds(CONV_HALO + b, CONV_CHUNK), :] = cg_ref[pl.ds(b, CONV_CHUNK), :] * ha_ref[pl.ds(b, CONV_CHUNK), :]
            return c

        _row_loop(t_rows, CONV_CHUNK, fill_a)
        _fill_shifted(sh, base, sorted({_tap_slot(CONV_HALO - (ka - 1) + k)[0] for k in range(ka)}), nr)

        def out_a(b, c):
            ya_ref[pl.ds(b, CONV_CHUNK), :] = (bg_ref[pl.ds(b, CONV_CHUNK), :] * conv(wa_ref, ka, b, CONV_CHUNK)).astype(BF16)
            return c

        _row_loop(t_rows, CONV_CHUNK, out_a)

        def fill_b(b, c):
            base[pl.ds(CONV_HALO + b, CONV_CHUNK), :] = (val_ref[pl.ds(b, CONV_CHUNK), :]
                                                          * jax.nn.sigmoid(gt_ref[pl.ds(b, CONV_CHUNK), :]))
            return c

        _row_loop(t_rows, CONV_CHUNK, fill_b)
        _fill_shifted(sh, base, range(SUBLANE), nr)

        def out_b(b, c):
            z_ref[pl.ds(b, CONV_CHUNK), :] = conv(wb_ref, kb, b, CONV_CHUNK) + bb_ref[...]
            return c

        _row_loop(t_rows, CONV_CHUNK, out_b)

    def col(g):
        return pl.BlockSpec((t_rows, LANE), lambda i, g=g: (0, g * nt + i))

    tile = lambda rows: pl.BlockSpec((rows, LANE), lambda i: (0, i))
    return _host_call(
        body, grid=(nt,),
        in_specs=[col(0), col(1), col(2), col(3), col(4), tile(ka), tile(kb), tile(1)],
        out_specs=[tile(t_rows), tile(t_rows)],
        out_shape=[jax.ShapeDtypeStruct((t_rows, wa_w), BF16), jax.ShapeDtypeStruct((t_rows, wa_w), F32)],
        scratch_shapes=[pltpu.VMEM((nr + SUBLANE, LANE), F32), pltpu.VMEM((SUBLANE, nr, LANE), F32)],
        args=(hin, hin, hin, hin, hin, wa, wb, bb), name="mix_conv_fwd", comm=comm)


def _ln_parts(z, lg, lb):
    mu = jnp.mean(z, axis=-1, keepdims=True)
    zc = z - mu
    rstd = lax.rsqrt(jnp.mean(zc * zc, axis=-1, keepdims=True) + LN_EPS)
    zh = zc * rstd
    return zh, rstd, zh * lg + lb


def _mix_ln_fwd(ya, z, lg, lb, tm):
    t_rows, w = z.shape

    def body(ya_ref, z_ref, lg_ref, lb_ref, y_ref):
        _, _, ln = _ln_parts(z_ref[...], lg_ref[...], lb_ref[...])
        y_ref[:, pl.ds(0, w)] = ya_ref[...]
        y_ref[:, pl.ds(w, w)] = (ln * jax.nn.sigmoid(ln)).astype(BF16)

    blk = pl.BlockSpec((tm, w), lambda i: (i, 0))
    res, _ = _host_call(body, grid=(t_rows // tm,), in_specs=[blk, blk, _full((1, w)), _full((1, w))],
                        out_specs=[pl.BlockSpec((tm, 2 * w), lambda i: (i, 0))],
                        out_shape=[jax.ShapeDtypeStruct((t_rows, 2 * w), BF16)], args=(ya, z, lg, lb), name="mix_ln_fwd")
    return res[0]


def _out_proj(y, w_out, h0, g2, g3, tm, comm):
    t_rows, d = h0.shape

    def body(y_ref, w_ref, h0_ref, g2_ref, g3_ref, mix_ref, h1_ref, xn2_ref):
        mix = _dot_nn(y_ref[...], w_ref[...])
        mix_ref[...] = mix
        h1 = h0_ref[...] + (mix * _rstd(mix)) * g2_ref[...]
        h1_ref[...] = h1
        xn2_ref[...] = ((h1 * _rstd(h1)) * g3_ref[...]).astype(BF16)

    blk = pl.BlockSpec((tm, d), lambda i: (i, 0))
    return _host_call(
        body, grid=(t_rows // tm,), in_specs=[blk, _resident(w_out.shape), blk, _full((1, d)), _full((1, d))],
        out_specs=[blk, blk, blk],
        out_shape=[jax.ShapeDtypeStruct((t_rows, d), F32), jax.ShapeDtypeStruct((t_rows, d), F32),
                   jax.ShapeDtypeStruct((t_rows, d), BF16)],
        args=(y, w_out, h0, g2, g3), name="out_proj", comm=comm)


def _gate_up(xn2, wg_t, wu_t, tm, comm):
    t_rows, d = xn2.shape
    f = wg_t.shape[0]

    def body(x_ref, wg_ref, wu_ref, a_ref, u_ref, s_ref):
        xn = x_ref[...]
        for o, n in _chunks(f, N_CHUNK):
            a = _dot_nt(xn, wg_ref[pl.ds(o, n), :])
            u = _dot_nt(xn, wu_ref[pl.ds(o, n), :])
            a_ref[:, pl.ds(o, n)] = a.astype(BF16)
            u_ref[:, pl.ds(o, n)] = u.astype(BF16)
            s_ref[:, pl.ds(o, n)] = ((a * jax.nn.sigmoid(a)) * u).astype(BF16)

    blk = pl.BlockSpec((tm, f), lambda i: (i, 0))
    return _host_call(
        body, grid=(t_rows // tm,),
        in_specs=[pl.BlockSpec((tm, d), lambda i: (i, 0)), _resident((f, d)), _resident((f, d))],
        out_specs=[blk, blk, blk], out_shape=[jax.ShapeDtypeStruct((t_rows, f), BF16)] * 3,
        args=(xn2, wg_t, wu_t), name="gate_up", comm=comm)


def _down_loss(s, wd, h1, tgt, g4, tm, x0):
    t_rows, d = h1.shape
    f = wd.shape[0]

    def body(s_ref, w_ref, h1_ref, tgt_ref, g4_ref, dh2_ref, dff_ref, dg4_ref, loss_ref):
        i = pl.program_id(0)
        ff = _dot_nn(s_ref[...], w_ref[...])
        r4 = _rstd(ff)
        fh = ff * r4
        g4 = g4_ref[...]
        h2 = h1_ref[...] + fh * g4
        row = i * tm + lax.broadcasted_iota(jnp.int32, (tm, 1), 0)
        diff = jnp.where(row >= x0, h2 - tgt_ref[...], 0.0)
        dh2 = diff / d
        dh2_ref[...] = dh2
        dff_ref[...] = _rms_bwd(dh2 * g4, fh, r4).astype(BF16)
        _acc_rows(dg4_ref, dh2 * fh, i == 0)
        _acc_rows(loss_ref, diff * diff, i == 0)

    blk = pl.BlockSpec((tm, d), lambda i: (i, 0))
    res, _ = _host_call(
        body, grid=(t_rows // tm,),
        in_specs=[pl.BlockSpec((tm, f), lambda i: (i, 0)), _resident((f, d)), blk, blk, _full((1, d))],
        out_specs=[blk, blk, _full((1, d)), _full((1, d))],
        out_shape=[jax.ShapeDtypeStruct((t_rows, d), F32), jax.ShapeDtypeStruct((t_rows, d), BF16),
                   jax.ShapeDtypeStruct((1, d), F32), jax.ShapeDtypeStruct((1, d), F32)],
        args=(s, wd, h1, tgt, g4), name="down_loss")
    return res


def _bwd_down(dff, wd, a, u, tm, comm):
    t_rows, d = dff.shape
    f = wd.shape[0]

    def body(dff_ref, w_ref, a_ref, u_ref, da_ref, du_ref):
        dff_v = dff_ref[...]
        for o, n in _chunks(f, N_CHUNK):
            ds = _dot_nt(dff_v, w_ref[pl.ds(o, n), :])
            av = a_ref[:, pl.ds(o, n)].astype(F32)
            uv = u_ref[:, pl.ds(o, n)].astype(F32)
            sig = jax.nn.sigmoid(av)
            da_ref[:, pl.ds(o, n)] = (ds * uv * _silu_grad(av, sig)).astype(BF16)
            du_ref[:, pl.ds(o, n)] = (ds * (av * sig)).astype(BF16)

    blk = pl.BlockSpec((tm, f), lambda i: (i, 0))
    return _host_call(
        body, grid=(t_rows // tm,),
        in_specs=[pl.BlockSpec((tm, d), lambda i: (i, 0)), _resident((f, d)), blk, blk],
        out_specs=[blk, blk], out_shape=[jax.ShapeDtypeStruct((t_rows, f), BF16)] * 2,
        args=(dff, wd, a, u), name="bwd_down", comm=comm)


def _wgrad(a, b, name, comm=None):
    d = b.shape[1]
    t_rows = b.shape[0]
    stacked = a.ndim == 3
    n = a.shape[-1]
    groups = a.shape[0] if stacked else 1
    tiles = n // WGRAD_TILE

    def body(a_ref, b_ref, o_ref):
        o_ref[...] = lax.dot_general(a_ref[...], b_ref[...], (((0,), (0,)), ((), ())),
                                     preferred_element_type=F32).astype(BF16)

    if stacked:
        a_spec = pl.BlockSpec((None, t_rows, WGRAD_TILE), lambda g, i: (g, 0, i))
    else:
        a_spec = pl.BlockSpec((t_rows, WGRAD_TILE), lambda g, i: (0, i))
    res, extra = _host_call(
        body, grid=(groups, tiles), in_specs=[a_spec, pl.BlockSpec((t_rows, d), lambda g, i: (0, 0))],
        out_specs=[pl.BlockSpec((WGRAD_TILE, d), lambda g, i: (g * tiles + i, 0))],
        out_shape=[jax.ShapeDtypeStruct((groups * n, d), BF16)], args=(a, b), name=name, comm=comm)
    return res[0], extra


def _bwd_ffn_in(da, du, wg_t, wu_t, h1, dh2, g3, tm, comm):
    t_rows, d = h1.shape
    f = wg_t.shape[0]

    def body(da_ref, du_ref, wg_ref, wu_ref, h1_ref, dh2_ref, g3_ref, dh1_ref, dg3_ref):
        dxn2 = _dot_nn(da_ref[...], wg_ref[...]) + _dot_nn(du_ref[...], wu_ref[...])
        h1 = h1_ref[...]
        r3 = _rstd(h1)
        h1h = h1 * r3
        _acc_rows(dg3_ref, dxn2 * h1h, pl.program_id(0) == 0)
        dh1_ref[...] = dh2_ref[...] + _rms_bwd(dxn2 * g3_ref[...], h1h, r3)

    blk = pl.BlockSpec((tm, d), lambda i: (i, 0))
    blkf = pl.BlockSpec((tm, f), lambda i: (i, 0))
    return _host_call(
        body, grid=(t_rows // tm,),
        in_specs=[blkf, blkf, _resident((f, d)), _resident((f, d)), blk, blk, _full((1, d))],
        out_specs=[blk, _full((1, d))],
        out_shape=[jax.ShapeDtypeStruct((t_rows, d), F32), jax.ShapeDtypeStruct((1, d), F32)],
        args=(da, du, wg_t, wu_t, h1, dh2, g3), name="bwd_ffn_in", comm=comm)


def _bwd_out_proj(dh1, mix, w_out, g2, tm):
    t_rows, d = dh1.shape

    def body(dh1_ref, mix_ref, w_ref, g2_ref, dmix_ref, dy_ref, dg2_ref):
        mix = mix_ref[...]
        r2 = _rstd(mix)
        mh = mix * r2
        dh1 = dh1_ref[...]
        _acc_rows(dg2_ref, dh1 * mh, pl.program_id(0) == 0)
        dmix = _rms_bwd(dh1 * g2_ref[...], mh, r2).astype(BF16)
        dmix_ref[...] = dmix
        dy_ref[...] = _dot_nt(dmix, w_ref[...])

    blk = pl.BlockSpec((tm, d), lambda i: (i, 0))
    res, _ = _host_call(
        body, grid=(t_rows // tm,), in_specs=[blk, blk, _resident(w_out.shape), _full((1, d))],
        out_specs=[blk, blk, _full((1, d))],
        out_shape=[jax.ShapeDtypeStruct((t_rows, d), BF16), jax.ShapeDtypeStruct((t_rows, d), F32),
                   jax.ShapeDtypeStruct((1, d), F32)],
        args=(dh1, mix, w_out, g2), name="bwd_out_proj")
    return res


def _mix_ln_bwd(z, dy, lg, lb, tm):
    t_rows, w = z.shape

    def body(z_ref, dyb_ref, lg_ref, lb_ref, dz_ref, dlg_ref, dlb_ref, dbb_ref):
        first = pl.program_id(0) == 0
        lg = lg_ref[...]
        zh, rstd, ln = _ln_parts(z_ref[...], lg, lb_ref[...])
        dln = dyb_ref[...] * _silu_grad(ln, jax.nn.sigmoid(ln))
        _acc_rows(dlg_ref, dln * zh, first)
        _acc_rows(dlb_ref, dln, first)
        dzh = dln * lg
        dz = rstd * (dzh - jnp.mean(dzh, axis=-1, keepdims=True) - zh * jnp.mean(dzh * zh, axis=-1, keepdims=True))
        dz_ref[...] = dz
        _acc_rows(dbb_ref, dz, first)

    blk = pl.BlockSpec((tm, w), lambda i: (i, 0))
    vec = _full((1, w))
    res, _ = _host_call(
        body, grid=(t_rows // tm,), in_specs=[blk, pl.BlockSpec((tm, w), lambda i: (i, 1)), vec, vec],
        out_specs=[blk, vec, vec, vec],
        out_shape=[jax.ShapeDtypeStruct((t_rows, w), F32)] + [jax.ShapeDtypeStruct((1, w), F32)] * 3,
        args=(z, dy, lg, lb), name="mix_ln_bwd")
    return res


def _mix_conv_bwd(hin, dy, dz, wa, wb, wa_w, comm):
    t_rows = hin.shape[0]
    nt = wa_w // LANE
    ka, kb = wa.shape[0], wb.shape[0]
    nr = CONV_HALO + t_rows
    kb_rows = -(-kb // SUBLANE) * SUBLANE

    def body(bg_ref, cg_ref, ha_ref, val_ref, gt_ref, dya_ref, dz_ref, wa_ref, wb_ref,
             dh_ref, dwa_ref, dwb_ref, base, sh, based, shd, tmp, wbc):
        zeros = lambda n: jnp.zeros((n, LANE), F32)
        base[pl.ds(0, CONV_HALO), :] = zeros(CONV_HALO)
        base[pl.ds(nr, SUBLANE), :] = zeros(SUBLANE)
        based[pl.ds(t_rows, CONV_HALO + SUBLANE), :] = zeros(CONV_HALO + SUBLANE)

        def fwd_slot(k_taps, k):
            return _tap_slot(CONV_HALO - (k_taps - 1) + k)

        def bwd_slot(k_taps, k):
            return _tap_slot(k_taps - 1 - k)

        def conv(w_ref, k_taps, src, slot, b, n):
            acc = None
            for k in range(k_taps):
                r, q = slot(k_taps, k)
                term = w_ref[pl.ds(k, 1), :] * src[r, pl.ds(b + q, n), :]
                acc = term if acc is None else acc + term
            return acc

        def by_residue(k_taps, slot):
            groups = {}
            for k in range(k_taps):
                r, q = slot(k_taps, k)
                groups.setdefault(r, []).append((k, q // SUBLANE))
            return groups

        def wgrad_loop(w_ref, k_taps):
            n_sub = WGRAD_ROWS // SUBLANE
            for k in range(k_taps):
                wbc[k] = jnp.broadcast_to(w_ref[pl.ds(k, 1), :], (SUBLANE, LANE))
            fwd, bwd = by_residue(k_taps, fwd_slot), by_residue(k_taps, bwd_slot)

            def window(src, r, taps, b):
                span = n_sub + max(qi for _, qi in taps)
                return [src[r, pl.ds(b + SUBLANE * i, SUBLANE), :] for i in range(span)]

            def step(b, accs):
                accs = list(accs)
                dv = [based[pl.ds(b + SUBLANE * j, SUBLANE), :] for j in range(n_sub)]
                for r, taps in fwd.items():
                    win = window(sh, r, taps, b)
                    for k, qi in taps:
                        t = dv[0] * win[qi]
                        for j in range(1, n_sub):
                            t = t + dv[j] * win[qi + j]
                        accs[k] = accs[k] + t
                outs = [None] * n_sub
                for r, taps in bwd.items():
                    win = window(shd, r, taps, b)
                    for k, qi in taps:
                        wk = wbc[k]
                        for j in range(n_sub):
                            term = wk * win[qi + j]
                            outs[j] = term if outs[j] is None else outs[j] + term
                for j in range(n_sub):
                    tmp[pl.ds(b + SUBLANE * j, SUBLANE), :] = outs[j]
                return tuple(accs)

            return _row_loop(t_rows, WGRAD_ROWS, step, tuple(zeros(SUBLANE) for _ in range(k_taps)))

        def store_taps(ref, accs, rows):
            for k, acc in enumerate(accs):
                ref[pl.ds(k, 1), :] = jnp.sum(acc, axis=0, keepdims=True)
            if rows > len(accs):
                ref[pl.ds(len(accs), rows - len(accs)), :] = zeros(rows - len(accs))

        def fill_a(b, c):
            sl = pl.ds(b, CONV_CHUNK)
            base[pl.ds(CONV_HALO + b, CONV_CHUNK), :] = cg_ref[sl, :] * ha_ref[sl, :]
            based[sl, :] = dya_ref[sl, :] * bg_ref[sl, :]
            return c

        _row_loop(t_rows, CONV_CHUNK, fill_a)
        _fill_shifted(sh, base, sorted({fwd_slot(ka, k)[0] for k in range(ka)}), nr)
        _fill_shifted(shd, based, sorted({bwd_slot(ka, k)[0] for k in range(ka)}), nr)

        def d_bgate(b, c):
            sl = pl.ds(b, CONV_CHUNK)
            dh_ref[0, sl, :] = (dya_ref[sl, :] * conv(wa_ref, ka, sh, fwd_slot, b, CONV_CHUNK)).astype(BF16)
            return c

        _row_loop(t_rows, CONV_CHUNK, d_bgate)
        store_taps(dwa_ref, wgrad_loop(wa_ref, ka), SUBLANE)

        def d_ch(b, c):
            sl = pl.ds(b, CONV_CHUNK)
            dua = tmp[sl, :]
            dh_ref[1, sl, :] = (dua * ha_ref[sl, :]).astype(BF16)
            dh_ref[2, sl, :] = (dua * cg_ref[sl, :]).astype(BF16)
            return c

        _row_loop(t_rows, CONV_CHUNK, d_ch)

        def fill_b(b, c):
            sl = pl.ds(b, CONV_CHUNK)
            base[pl.ds(CONV_HALO + b, CONV_CHUNK), :] = val_ref[sl, :] * jax.nn.sigmoid(gt_ref[sl, :])
            based[sl, :] = dz_ref[sl, :]
            return c

        _row_loop(t_rows, CONV_CHUNK, fill_b)
        _fill_shifted(sh, base, range(SUBLANE), nr)
        _fill_shifted(shd, based, range(SUBLANE), nr)
        store_taps(dwb_ref, wgrad_loop(wb_ref, kb), kb_rows)

        def d_glu(b, c):
            sl = pl.ds(b, CONV_CHUNK)
            dgg = tmp[sl, :]
            sig = jax.nn.sigmoid(gt_ref[sl, :])
            dh_ref[3, sl, :] = (dgg * sig).astype(BF16)
            dh_ref[4, sl, :] = (dgg * val_ref[sl, :] * (sig * (1.0 - sig))).astype(BF16)
            return c

        _row_loop(t_rows, CONV_CHUNK, d_glu)

    def col(g):
        return pl.BlockSpec((t_rows, LANE), lambda i, g=g: (0, g * nt + i))

    tile = lambda rows: pl.BlockSpec((rows, LANE), lambda i: (0, i))
    return _host_call(
        body, grid=(nt,),
        in_specs=[col(0), col(1), col(2), col(3), col(4), tile(t_rows), tile(t_rows), tile(ka), tile(kb)],
        out_specs=[pl.BlockSpec((5, t_rows, LANE), lambda i: (0, 0, i)), tile(SUBLANE), tile(kb_rows)],
        out_shape=[jax.ShapeDtypeStruct((5, t_rows, wa_w), BF16), jax.ShapeDtypeStruct((SUBLANE, wa_w), F32),
                   jax.ShapeDtypeStruct((kb_rows, wa_w), F32)],
        scratch_shapes=[pltpu.VMEM((nr + SUBLANE, LANE), F32), pltpu.VMEM((SUBLANE, nr, LANE), F32),
                        pltpu.VMEM((nr + SUBLANE, LANE), F32), pltpu.VMEM((SUBLANE, nr, LANE), F32),
                        pltpu.VMEM((t_rows, LANE), F32), pltpu.VMEM((kb_rows, SUBLANE, LANE), F32)],
        args=(hin, hin, hin, hin, hin, dy, dz, wa, wb), name="mix_conv_bwd", comm=comm)


def _bwd_in_proj(dh5, win_t, h0, dh1, g1, tm, comm):
    t_rows, d = h0.shape
    groups, _, w = dh5.shape

    def body(dh_ref, w_ref, h0_ref, dh1_ref, g1_ref, dh0_ref, dg1_ref):
        dxn1 = None
        for g in range(groups):
            part = _dot_nn(dh_ref[g], w_ref[pl.ds(g * w, w), :])
            dxn1 = part if dxn1 is None else dxn1 + part
        h0 = h0_ref[...]
        r1 = _rstd(h0)
        h0h = h0 * r1
        _acc_rows(dg1_ref, dxn1 * h0h, pl.program_id(0) == 0)
        dh0_ref[...] = dh1_ref[...] + _rms_bwd(dxn1 * g1_ref[...], h0h, r1)

    blk = pl.BlockSpec((tm, d), lambda i: (i, 0))
    return _host_call(
        body, grid=(t_rows // tm,),
        in_specs=[pl.BlockSpec((groups, tm, w), lambda i: (0, i, 0)), _resident(win_t.shape), blk, blk, _full((1, d))],
        out_specs=[blk, _full((1, d))],
        out_shape=[jax.ShapeDtypeStruct((t_rows, d), F32), jax.ShapeDtypeStruct((1, d), F32)],
        args=(dh5, win_t, h0, dh1, g1), name="bwd_in_proj", comm=comm)


def _reduce_small(smalls, d, comm):
    (dmeta, dg1, dg2, dg3, dg4, dbb, dlg, dlb, lossv, dwa, dwb) = smalls
    half = d // 2
    kb_rows = dwb.shape[0]

    def body(dmeta_ref, dg1_ref, dg2_ref, dg3_ref, dg4_ref, dbb_ref, dlg_ref, dlb_ref, loss_ref, dwa_ref, dwb_ref,
             ptot_ref, pbuf, psib, chip_p, ps_send, ps_recv, pc_send, pc_recv):
        x, y, c = _mesh_pos()
        pbuf[...] = jnp.zeros_like(pbuf)
        pbuf[pl.ds(0, N_META), :] = dmeta_ref[...]
        for row, ref in ((16, dg1_ref), (17, dg2_ref), (18, dg3_ref), (19, dg4_ref)):
            pbuf[pl.ds(row, 1), :] = ref[...]
        pbuf[pl.ds(20, 1), pl.ds(0, half)] = dbb_ref[...]
        pbuf[pl.ds(20, 1), pl.ds(half, half)] = dlg_ref[...]
        pbuf[pl.ds(21, 1), pl.ds(0, half)] = dlb_ref[...]
        lv = loss_ref[...]
        pbuf[pl.ds(21, 1), pl.ds(half, half)] = lv[:, :half] + lv[:, half:]
        pbuf[pl.ds(24, SUBLANE), pl.ds(0, half)] = dwa_ref[...]
        pbuf[pl.ds(32, kb_rows), pl.ds(0, half)] = dwb_ref[...]
        to_sib = _remote(pbuf, psib, ps_send.at[0], ps_recv.at[0], (x, y, 1 - c))
        to_sib.start()
        to_sib.wait_recv()
        my_chip = 2 * x + y
        chip_p[my_chip] = pbuf[...] + psib[...]
        to_sib.wait_send()
        slot = chip_p.at[my_chip]
        cps = [_remote(slot, slot, pc_send.at[k], pc_recv.at[k], (*chip, c)) for k, chip in enumerate(_other_chips(x, y))]
        for cp in cps:
            cp.start()
        for cp in cps:
            cp.wait_recv()
        ptot_ref[...] = ((chip_p[0] + chip_p[1]) + chip_p[2]) + chip_p[3]
        for cp in cps:
            cp.wait_send()

    return _host_call(
        body, grid=(), in_specs=[VMEM] * 11, out_specs=[VMEM], out_shape=[jax.ShapeDtypeStruct((SMALL_ROWS, d), F32)],
        scratch_shapes=[pltpu.VMEM((SMALL_ROWS, d), F32), pltpu.VMEM((SMALL_ROWS, d), F32), pltpu.VMEM((4, SMALL_ROWS, d), F32),
                        pltpu.SemaphoreType.DMA((1,)), pltpu.SemaphoreType.DMA((1,)),
                        pltpu.SemaphoreType.DMA((3,)), pltpu.SemaphoreType.DMA((3,))],
        args=smalls, name="reduce_small", comm=comm)


def _adamw(w, g, m, v):
    m = ADAM_B1 * m + (1.0 - ADAM_B1) * g
    v = ADAM_B2 * v + (1.0 - ADAM_B2) * jnp.square(g)
    m_hat = m / (1.0 - ADAM_B1 ** ADAM_STEP)
    v_hat = v / (1.0 - ADAM_B2 ** ADAM_STEP)
    delta = -ADAM_LR * (m_hat / (jnp.sqrt(v_hat) + ADAM_EPS) + ADAM_WD * w)
    return delta, m, v


def _adam_big(g, pair, part, w, m, v, name):
    r, d = w.shape
    cols = d // ADAM_COL_BLOCKS

    def body(me_ref, g_ref, pair_ref, part_ref, w_ref, m_ref, v_ref, go_ref, d_ref, mo_ref, vo_ref):
        g = g_ref[...].astype(F32) + pair_ref[...].astype(F32)
        for k in range(3):
            g = g + part_ref[k].astype(F32)
        go_ref[...] = g
        d_ref[...], mo_ref[...], vo_ref[...] = _adamw(w_ref[...], g, m_ref[...], v_ref[...])

    blk = pl.BlockSpec((r, cols), lambda i, me_ref: (0, i))
    grid_spec = pltpu.PrefetchScalarGridSpec(
        num_scalar_prefetch=1, grid=(ADAM_COL_BLOCKS,),
        in_specs=[pl.BlockSpec((r, cols), lambda i, me_ref: (me_ref[0], i)),
                  pl.BlockSpec((None, r, cols), lambda i, me_ref: (0, 0, i)),
                  pl.BlockSpec((3, r, cols), lambda i, me_ref: (0, 0, i)), blk, blk, blk],
        out_specs=[blk, blk, blk, blk])
    me = jnp.reshape(_dev_index(*_mesh_pos()), (1,)).astype(jnp.int32)
    return pl.pallas_call(body, out_shape=[jax.ShapeDtypeStruct((r, d), F32)] * 4, grid_spec=grid_spec, name=name,
                          compiler_params=_cparams(1))(me, g, pair, part, w, m, v)


def _adam_small(gs, ws, ms, vs):
    n = len(gs)

    def body(*refs):
        ins, outs = refs[:4 * n], refs[4 * n:]
        for i in range(n):
            g = ins[i][...]
            delta, m, v = _adamw(ins[n + i][...], g, ins[2 * n + i][...], ins[3 * n + i][...])
            outs[i][...] = delta
            outs[n + i][...] = m
            outs[2 * n + i][...] = v

    shapes = [jax.ShapeDtypeStruct(w.shape, F32) for w in ws]
    return pl.pallas_call(body, out_shape=shapes * 3, name="adam_small", compiler_params=_cparams())(*gs, *ws, *ms, *vs)


def kernel(x, meta_tokens, pre_mix_norm, w_in, conv_a_w, conv_b_w, conv_b_bias, ln_b_gain, ln_b_bias, w_out, post_mix_norm, pre_ffn_norm, w_gate, w_up, w_down, post_ffn_norm, loss_target, m_meta_tokens, m_pre_mix_norm, m_w_in, m_conv_a_w, m_conv_b_w, m_conv_b_bias, m_ln_b_gain, m_ln_b_bias, m_w_out, m_post_mix_norm, m_pre_ffn_norm, m_w_gate, m_w_up, m_w_down, m_post_ffn_norm, v_meta_tokens, v_pre_mix_norm, v_w_in, v_conv_a_w, v_conv_b_w, v_conv_b_bias, v_ln_b_gain, v_ln_b_bias, v_w_out, v_post_mix_norm, v_pre_ffn_norm, v_w_gate, v_w_up, v_w_down, v_post_ffn_norm):
    _, seq, d = x.shape
    ka, ca_loc = conv_a_w.shape[1:]
    kb, cb_loc = conv_b_w.shape[1:]
    wa_w = ca_loc * N_DEV
    assert cb_loc == ca_loc and wa_w % LANE == 0 and w_in.shape[2] * N_DEV == 5 * wa_w
    pad = (-(N_META + seq)) % ROW_ALIGN
    x0 = pad + N_META
    t_rows = x0 + seq
    assert t_rows % (N_ROW_BLOCKS * BF16_ROWS) == 0 and t_rows % CONV_CHUNK == 0 and d % LANE == 0
    tm = t_rows // N_ROW_BLOCKS
    me = _dev_index(*_mesh_pos())

    def as_rows(w_in_like, w_out_like, w_gate_like, w_up_like, w_down_like):
        return (w_in_like[0].T, w_out_like[0], w_gate_like[0].T, w_up_like[0].T, w_down_like[0])

    w_loc = as_rows(w_in, w_out, w_gate, w_up, w_down)
    rows = [w.shape[0] for w in w_loc]
    assert all(r % ADD_CHUNK == 0 for r in rows)
    P_IN, P_OUT, P_GATE, P_UP, P_DOWN = range(N_BIG)
    gate_cut = (rows[P_GATE] // 2) // (2 * BF16_ROWS) * (2 * BF16_ROWS)
    up_cut = (rows[P_UP] // 5) // (2 * BF16_ROWS) * (2 * BF16_ROWS)
    jobs_first = [(P_IN, 0, rows[P_IN])]
    jobs_prep = [(P_OUT, 0, rows[P_OUT])]
    jobs_in_proj = [(P_GATE, 0, gate_cut)]
    jobs_conv = [(P_GATE, gate_cut, rows[P_GATE] - gate_cut), (P_UP, 0, up_cut)]
    jobs_out_proj = [(P_UP, up_cut, rows[P_UP] - up_cut)]
    jobs_gate_up = [(P_DOWN, 0, rows[P_DOWN])]

    sm = jnp.zeros((SM_ROWS, LANE), F32)
    sm = sm.at[0:N_META, :].set(meta_tokens)
    sm = sm.at[16:16 + ka, 0:ca_loc].set(conv_a_w[0])
    sm = sm.at[24:24 + kb, 0:cb_loc].set(conv_b_w[0])
    wl, wfull, sm_all = _gather_first(w_loc, sm, jobs_first)
    meta_full = jnp.transpose(sm_all[:, 0:N_META, :], (1, 0, 2)).reshape(N_META, d)
    wa = jnp.transpose(sm_all[:, 16:16 + ka, 0:ca_loc], (1, 0, 2)).reshape(ka, wa_w)
    wb = jnp.transpose(sm_all[:, 24:24 + kb, 0:cb_loc], (1, 0, 2)).reshape(kb, wa_w)

    def gather(jobs):
        return _gather_comm(wl, {p: wfull[p] for p in sorted({j[0] for j in jobs})}, jobs, rows)

    def gathered(jobs, extra):
        for p, arr in zip(sorted({j[0] for j in jobs}), extra):
            wfull[p] = arr

    (h0, tgt), extra = _prep(x[0], loss_target[0], meta_full, t_rows, x0, gather(jobs_prep))
    gathered(jobs_prep, extra)
    (xn1, hin), extra = _in_proj(h0, pre_mix_norm, wfull[P_IN], tm, gather(jobs_in_proj))
    gathered(jobs_in_proj, extra)
    (ya, z), extra = _mix_conv_fwd(hin, wa, wb, conv_b_bias, wa_w, gather(jobs_conv))
    gathered(jobs_conv, extra)
    y = _mix_ln_fwd(ya, z, ln_b_gain, ln_b_bias, tm)
    (mix, h1, xn2), extra = _out_proj(y, wfull[P_OUT], h0, post_mix_norm, pre_ffn_norm, tm, gather(jobs_out_proj))
    gathered(jobs_out_proj, extra)
    (a, u, s), extra = _gate_up(xn2, wfull[P_GATE], wfull[P_UP], tm, gather(jobs_gate_up))
    gathered(jobs_gate_up, extra)
    dh2, dff, dg4, lossv = _down_loss(s, wfull[P_DOWN], h1, tgt, post_ffn_norm, tm, x0)

    gwd, _ = _wgrad(s, dff, "wgrad_down")
    (da, du), (pair_d,) = _bwd_down(dff, wfull[P_DOWN], a, u, tm, _pair_comm(gwd, rows[P_DOWN]))
    gwg, _ = _wgrad(da, xn2, "wgrad_gate")
    gwu, _ = _wgrad(du, xn2, "wgrad_up")
    (dh1, dg3), (part_d, pair_g, pair_u) = _bwd_ffn_in(
        da, du, wfull[P_GATE], wfull[P_UP], h1, dh2, pre_ffn_norm, tm,
        _merge_comms([_chip_comm(gwd, pair_d, rows[P_DOWN]), _pair_comm(gwg, rows[P_GATE]), _pair_comm(gwu, rows[P_UP])]))
    dmix, dy, dg2 = _bwd_out_proj(dh1, mix, wfull[P_OUT], post_mix_norm, tm)
    gwo, _ = _wgrad(y, dmix, "wgrad_out")
    dz, dlg, dlb, dbb = _mix_ln_bwd(z, dy, ln_b_gain, ln_b_bias, tm)
    (dh5, dwa, dwb), (part_g, part_u, pair_o) = _mix_conv_bwd(
        hin, dy, dz, wa, wb, wa_w,
        _merge_comms([_chip_comm(gwg, pair_g, rows[P_GATE]), _chip_comm(gwu, pair_u, rows[P_UP]), _pair_comm(gwo, rows[P_OUT])]))
    gwi, (part_o,) = _wgrad(dh5, xn1, "wgrad_in", _chip_comm(gwo, pair_o, rows[P_OUT]))
    (dh0, dg1), (pair_i, part_i) = _bwd_in_proj(dh5, wfull[P_IN], h0, dh1, pre_mix_norm, tm, _pair_chip_comm(gwi, rows[P_IN]))
    grad_x = dh0[x0:][None]
    dmeta = dh0[x0 - N_META:x0]
    (ptot,), _ = _reduce_small((dmeta, dg1, dg2, dg3, dg4, dbb, dlg, dlb, lossv, dwa, dwb), d, None)

    half = d // 2
    loss = (0.5 / d) * jnp.sum(ptot[21, half:])
    g_meta = lax.dynamic_slice(ptot, (0, me * (d // N_DEV)), (N_META, d // N_DEV))
    g_small = [g_meta, ptot[16:17], lax.dynamic_slice(ptot, (24, me * ca_loc), (ka, ca_loc))[None],
               lax.dynamic_slice(ptot, (32, me * cb_loc), (kb, cb_loc))[None],
               ptot[20:21, :half], ptot[20:21, half:], ptot[21:22, :half], ptot[17:18], ptot[18:19], ptot[19:20]]
    w_small = [meta_tokens, pre_mix_norm, conv_a_w, conv_b_w, conv_b_bias, ln_b_gain, ln_b_bias, post_mix_norm,
               pre_ffn_norm, post_ffn_norm]
    m_small = [m_meta_tokens, m_pre_mix_norm, m_conv_a_w, m_conv_b_w, m_conv_b_bias, m_ln_b_gain, m_ln_b_bias,
               m_post_mix_norm, m_pre_ffn_norm, m_post_ffn_norm]
    v_small = [v_meta_tokens, v_pre_mix_norm, v_conv_a_w, v_conv_b_w, v_conv_b_bias, v_ln_b_gain, v_ln_b_bias,
               v_post_mix_norm, v_pre_ffn_norm, v_post_ffn_norm]
    small = _adam_small(g_small, w_small, m_small, v_small)
    n_small = len(w_small)
    d_small, nm_small, nv_small = small[:n_small], small[n_small:2 * n_small], small[2 * n_small:]

    m_loc = as_rows(m_w_in, m_w_out, m_w_gate, m_w_up, m_w_down)
    v_loc = as_rows(v_w_in, v_w_out, v_w_gate, v_w_up, v_w_down)
    full_grads = (gwi, gwo, gwg, gwu, gwd)
    pairs = (pair_i, pair_o, pair_g, pair_u, pair_d)
    parts = (part_i, part_o, part_g, part_u, part_d)
    bigs = {}
    for p, (name, tr) in enumerate((("w_in", True), ("w_out", False), ("w_gate", True), ("w_up", True), ("w_down", False))):
        res = _adam_big(full_grads[p], pairs[p], parts[p], w_loc[p], m_loc[p], v_loc[p], "adam_" + name)
        bigs[name] = [(o.T if tr else o)[None] for o in res]

    def ordered(pick_small, pick_big):
        sm_it = iter(range(n_small))
        out = []
        for name in ("s", "s", "w_in", "s", "s", "s", "s", "s", "w_out", "s", "s", "w_gate", "w_up", "w_down", "s"):
            out.append(pick_small(next(sm_it)) if name == "s" else pick_big(name))
        return out

    grads = ordered(lambda i: g_small[i], lambda n: bigs[n][0])
    deltas = ordered(lambda i: d_small[i], lambda n: bigs[n][1])
    new_m = ordered(lambda i: nm_small[i], lambda n: bigs[n][2])
    new_v = ordered(lambda i: nv_small[i], lambda n: bigs[n][3])
    return (loss, grad_x, *grads, *deltas, *new_m, *new_v)
```

```python
import jax
import jax.numpy as jnp
from jax import lax
from jax.experimental import pallas as pl
from jax.experimental.pallas import tpu as pltpu

F32 = jnp.float32
BF16 = jnp.bfloat16
MESH = pl.DeviceIdType.MESH

N_META = 16
N_DEV = 8
RMS_EPS = 1e-6
LN_EPS = 1e-5
ADAM_LR = 0.001
ADAM_B1 = 0.9
ADAM_B2 = 0.999
ADAM_EPS = 1e-08
ADAM_WD = 0.01
ADAM_STEP = 10

LANE = 128
SUBLANE = 8
BF16_ROWS = 16
ROW_ALIGN = 128
N_ROW_BLOCKS = 4
CONV_HALO = 32
CONV_CHUNK = 64
WGRAD_ROWS = 32
N_CHUNK = 512
WGRAD_TILE_MAX = 1408
ADD_CHUNK = 32
ADAM_COL_BLOCKS = 4
V7X_VMEM_BYTES = 64 * 1024 * 1024
VMEM_LIMIT = V7X_VMEM_BYTES - 6 * 1024 * 1024
SMALL_ROWS = 64
SM_ROWS = 56
N_BIG = 5

ANY = pl.BlockSpec(memory_space=pl.ANY)
VMEM = pl.BlockSpec(memory_space=pltpu.VMEM)


def _cparams(n_grid_axes=0):
    sem = ("arbitrary",) * n_grid_axes if n_grid_axes else None
    return pltpu.CompilerParams(dimension_semantics=sem, vmem_limit_bytes=VMEM_LIMIT)


def _mesh_pos():
    return lax.axis_index("x"), lax.axis_index("y"), lax.axis_index("c")


def _dev_index(px, py, pc):
    return 4 * px + 2 * py + pc


def _other_chips(x, y):
    return [(1 - x, y), (x, 1 - y), (1 - x, 1 - y)]


def _full(shape):
    return pl.BlockSpec(shape, lambda *_: (0,) * len(shape))


def _resident(shape):
    return pl.BlockSpec(shape, lambda *_: (0,) * len(shape), pipeline_mode=pl.Buffered(1))


def _dot_nt(a, w):
    return lax.dot_general(a, w, (((1,), (1,)), ((), ())), preferred_element_type=F32)


def _dot_nn(a, w):
    return jnp.dot(a, w, preferred_element_type=F32)


def _chunks(n, c):
    out, o = [], 0
    while o < n:
        out.append((o, min(c, n - o)))
        o += c
    return out


def _rstd(h):
    return lax.rsqrt(jnp.mean(h * h, axis=-1, keepdims=True) + RMS_EPS)


def _rms_bwd(dyh, yh, r):
    return r * (dyh - yh * jnp.mean(dyh * yh, axis=-1, keepdims=True))


def _silu_grad(a, sig):
    return sig * (1.0 + a * (1.0 - sig))


def _acc_rows(ref, val, first):
    s = jnp.sum(val, axis=0, keepdims=True)

    @pl.when(first)
    def _():
        ref[...] = s

    @pl.when(jnp.logical_not(first))
    def _():
        ref[...] += s


def _row_loop(t_rows, chunk, fn, carry=None):
    def step(i, c):
        return fn(pl.multiple_of(i * chunk, chunk), c)

    return lax.fori_loop(0, t_rows // chunk, step, carry)


def _remote(src, dst, send_sem, recv_sem, to):
    return pltpu.make_async_remote_copy(src_ref=src, dst_ref=dst, send_sem=send_sem, recv_sem=recv_sem,
                                        device_id=to, device_id_type=MESH)


class _SplitRemote:
    def __init__(self, src, dst, send_sem, recv_sem, to, rows, n_chunks):
        units = rows // BF16_ROWS
        n_chunks = max(1, min(n_chunks, units))
        sizes = [(units // n_chunks + (i < units % n_chunks)) * BF16_ROWS for i in range(n_chunks)]
        self.whole = _remote(src, dst, send_sem, recv_sem, to)
        self.parts, o = [], 0
        for n in sizes:
            self.parts.append(_remote(src.at[pl.ds(o, n), :], dst.at[pl.ds(o, n), :], send_sem, recv_sem, to))
            o += n

    def start(self):
        for cp in self.parts:
            cp.start()

    def wait_recv(self):
        self.whole.wait_recv()

    def wait_send(self):
        self.whole.wait_send()


class _Comm:
    def __init__(self, inputs, out_shapes, aliases, scratch, start, finish):
        self.inputs, self.out_shapes, self.aliases, self.scratch = list(inputs), list(out_shapes), dict(aliases), list(scratch)
        self.start, self.finish = start, finish


def _merge_comms(comms):
    inputs, out_shapes, aliases, scratch, spans = [], [], {}, [], []
    for cm in comms:
        spans.append((len(inputs), len(out_shapes), len(scratch), cm))
        aliases.update({len(inputs) + k: len(out_shapes) + v for k, v in cm.aliases.items()})
        inputs += cm.inputs
        out_shapes += cm.out_shapes
        scratch += cm.scratch

    def run(which):
        def fn(ins, outs, scr):
            for i0, o0, s0, cm in spans:
                getattr(cm, which)(ins[i0:i0 + len(cm.inputs)], outs[o0:o0 + len(cm.out_shapes)], scr[s0:s0 + len(cm.scratch)])
        return fn

    return _Comm(inputs, out_shapes, aliases, scratch, run("start"), run("finish"))


def _host_call(body, *, grid, in_specs, out_specs, out_shape, args, name, scratch_shapes=(), comm=None, after=()):
    if comm is None:
        comm = _Comm([], [], {}, [], lambda *_: None, lambda *_: None)
    n_in, n_out, n_scr = len(args), len(out_shape), len(scratch_shapes)
    c_in, c_out = len(comm.inputs), len(comm.out_shapes)
    n_after = len(after)

    def hosted(*refs):
        ins, c_ins = refs[:n_in], refs[n_in:n_in + c_in]
        o0 = n_in + c_in + n_after
        outs, c_outs = refs[o0:o0 + n_out], refs[o0 + n_out:o0 + n_out + c_out]
        s0 = o0 + n_out + c_out
        scr, c_scr = refs[s0:s0 + n_scr], refs[s0 + n_scr:]
        if not grid:
            comm.start(c_ins, c_outs, c_scr)
            body(*ins, *outs, *scr)
            comm.finish(c_ins, c_outs, c_scr)
            return
        first = last = None
        for a, n in enumerate(grid):
            f, l = pl.program_id(a) == 0, pl.program_id(a) == n - 1
            first = f if first is None else jnp.logical_and(first, f)
            last = l if last is None else jnp.logical_and(last, l)

        @pl.when(first)
        def _():
            comm.start(c_ins, c_outs, c_scr)

        body(*ins, *outs, *scr)

        @pl.when(last)
        def _():
            comm.finish(c_ins, c_outs, c_scr)

    res = pl.pallas_call(
        hosted, grid=grid, in_specs=list(in_specs) + [ANY] * (c_in + n_after), out_specs=list(out_specs) + [ANY] * c_out,
        out_shape=list(out_shape) + comm.out_shapes, scratch_shapes=list(scratch_shapes) + comm.scratch,
        input_output_aliases={n_in + k: n_out + v for k, v in comm.aliases.items()},
        name=name, compiler_params=_cparams(len(grid)))(*args, *comm.inputs, *after)
    return list(res[:n_out]), list(res[n_out:])


GATHER_SEMS = 10
D2D_CHUNKS = 8


class _Gather:
    def __init__(self, jobs, rows, lo, src_ref, dests, send_sems, recv_sems):
        x, y, c = _mesh_pos()
        me, sib = (x, y, c), (x, y, 1 - c)
        nx, ny, dg = (1 - x, y, c), (x, 1 - y, c), (1 - x, 1 - y, c)
        self.relayed, self.direct, self.relay, self.to_sib, self.sib_fwd = [], [], [], [], []
        for n, (p, r0, nr) in enumerate(jobs):
            assert nr % (2 * BF16_ROWS) == 0
            half = nr // 2

            def rows_of(dev, h, p=p, r0=r0, nr=nr, half=half):
                off, cnt = (r0, nr) if h is None else (r0 + h * half, half)
                return dests[p].at[pl.ds(pl.multiple_of(_dev_index(*dev) * rows[p] + off, BF16_ROWS), cnt), :]

            def mine(h, p=p, r0=r0, nr=nr, half=half):
                off, cnt = (r0, nr) if h is None else (r0 + h * half, half)
                return src_ref.at[pl.ds(lo[p] + off, cnt), :]

            sem = lambda k, n=n: (send_sems.at[GATHER_SEMS * n + k], recv_sems.at[GATHER_SEMS * n + k])
            self.relayed.append([_remote(mine(0), rows_of(me, 0), *sem(0), nx), _remote(mine(1), rows_of(me, 1), *sem(3), ny)])
            self.direct.append([_remote(mine(1), rows_of(me, 1), *sem(1), nx), _remote(mine(0), rows_of(me, 0), *sem(2), ny)])
            self.relay.append([_remote(rows_of(nx, 0), rows_of(nx, 0), *sem(4), ny), _remote(rows_of(ny, 1), rows_of(ny, 1), *sem(5), nx)])
            self.to_sib.append(_SplitRemote(mine(None), rows_of(me, None), *sem(6), sib, nr, D2D_CHUNKS))
            self.sib_fwd.append([_SplitRemote(rows_of(dev, None), rows_of(dev, None), *sem(7 + i), sib, nr, D2D_CHUNKS)
                                 for i, dev in enumerate((nx, ny, dg))])

    def start(self):
        for group in (self.relayed, self.direct):
            for cps in group:
                for cp in cps:
                    cp.start()
        for cp in self.to_sib:
            cp.start()

    def mid(self):
        for first, relay in zip(self.relayed, self.relay):
            for arrived, onward in zip(first, relay):
                arrived.wait_recv()
                onward.start()

    def finish(self):
        for direct, relay, fwd in zip(self.direct, self.relay, self.sib_fwd):
            for k in range(2):
                direct[k].wait_recv()
                fwd[k].start()
            for cp in relay:
                cp.wait_recv()
            fwd[2].start()
        for n in range(len(self.to_sib)):
            self.to_sib[n].wait_recv()
            for cp in self.sib_fwd[n]:
                cp.wait_recv()
            for cp in self.relayed[n] + self.direct[n] + self.relay[n] + [self.to_sib[n]] + self.sib_fwd[n]:
                cp.wait_send()


HBM = pl.BlockSpec(memory_space=pltpu.HBM)
SEM = pl.BlockSpec(memory_space=pltpu.SEMAPHORE)
FLOWS = pltpu.SideEffectType.DATAFLOW_SIDE_EFFECTING


def _in_hbm(a):
    return pltpu.with_memory_space_constraint(a, pltpu.HBM)


def _gather_start(wl, dests, ps, rows):
    lo = [sum(rows[:p]) for p in range(N_BIG)]
    n = len(ps)

    def body(*refs):
        wl_ref, dest_refs = refs[0], refs[1:1 + n]
        sends, recvs = refs[1 + n:1 + 2 * n], refs[1 + 2 * n:1 + 3 * n]
        token = refs[-1]
        x, y, c = _mesh_pos()
        jme = _dev_index(x, y, c)
        for i, p in enumerate(ps):
            mine = dest_refs[i].at[pl.ds(pl.multiple_of(jme * rows[p], BF16_ROWS), rows[p]), :]
            for chip in _other_chips(x, y):
                _remote(wl_ref.at[pl.ds(lo[p], rows[p]), :], mine, sends[i], recvs[i], (*chip, c)).start()
        token[...] = jnp.zeros_like(token)

    thru = [pltpu.HBM(wl.shape, wl.dtype)] + [pltpu.HBM(dests[p].shape, BF16) for p in ps]
    res = pl.pallas_call(
        body, name="gather_start",
        out_shape=tuple([pltpu.SemaphoreType.DMA(())] * (2 * n) + thru + [jax.ShapeDtypeStruct((SUBLANE, LANE), F32)]),
        in_specs=[HBM] * (1 + n), out_specs=tuple([SEM] * (2 * n) + [HBM] * (1 + n) + [VMEM]),
        input_output_aliases={i: 2 * n + i for i in range(1 + n)},
        compiler_params=pltpu.CompilerParams(has_side_effects=FLOWS))(_in_hbm(wl), *[_in_hbm(dests[p]) for p in ps])
    sems = [(res[i], res[n + i]) for i in range(n)]
    return sems, res[2 * n], list(res[2 * n + 1:3 * n + 1]), res[-1]


def _gather_wait(wl, dest, sems, after, r, name):
    def body(wl_ref, dest_ref, send_sem, recv_sem, after_ref, wl_out, dest_out):
        x, y, c = _mesh_pos()
        three = dest_ref.at[pl.ds(0, 3 * r), :]
        cp = _remote(three, three, send_sem, recv_sem, (x, y, 1 - c))
        cp.wait_send()
        cp.wait_recv()

    res = pl.pallas_call(
        body, name=name, out_shape=(pltpu.HBM(wl.shape, wl.dtype), pltpu.HBM(dest.shape, dest.dtype)),
        in_specs=[HBM, HBM, SEM, SEM, ANY], out_specs=(HBM, HBM), input_output_aliases={0: 0, 1: 1},
        compiler_params=pltpu.CompilerParams(has_side_effects=FLOWS))(wl, dest, sems[0], sems[1], after)
    return res[0], res[1]


def _forward_comm(dest, r):
    def descs(ins, outs, scr):
        x, y, c = _mesh_pos()
        cps = []
        for k, chip in enumerate(_other_chips(x, y)):
            blk = outs[0].at[pl.ds(pl.multiple_of(_dev_index(*chip, c) * r, BF16_ROWS), r), :]
            cps.append(_SplitRemote(blk, blk, scr[0].at[k], scr[1].at[k], (x, y, 1 - c), r, D2D_CHUNKS))
        return cps

    def start(ins, outs, scr):
        for cp in descs(ins, outs, scr):
            cp.start()

    def finish(ins, outs, scr):
        cps = descs(ins, outs, scr)
        for cp in cps:
            cp.wait_recv()
        for cp in cps:
            cp.wait_send()

    return _Comm([dest], [jax.ShapeDtypeStruct(dest.shape, dest.dtype)], {0: 0},
                 [pltpu.SemaphoreType.DMA((3,)), pltpu.SemaphoreType.DMA((3,))], start, finish)


def _pair_comm(g, r):
    d = g.shape[1]

    def descs(ins, outs, scr):
        x, y, c = _mesh_pos()
        chips = [(x, y)] + _other_chips(x, y)
        return [_SplitRemote(ins[0].at[pl.ds(pl.multiple_of(_dev_index(*chip, 1 - c) * r, BF16_ROWS), r), :], outs[0].at[k],
                             scr[0].at[k], scr[1].at[k], (x, y, 1 - c), r, D2D_CHUNKS) for k, chip in enumerate(chips)]

    def start(ins, outs, scr):
        for cp in descs(ins, outs, scr):
            cp.start()

    def finish(ins, outs, scr):
        cps = descs(ins, outs, scr)
        for cp in cps:
            cp.wait_recv()
        for cp in cps:
            cp.wait_send()

    comm = _Comm([g], [jax.ShapeDtypeStruct((4, r, d), BF16)], {},
                 [pltpu.SemaphoreType.DMA((4,)), pltpu.SemaphoreType.DMA((4,))], start, finish)
    return comm


def _pair_sum(g, pair, r, name):
    d = g.shape[1]
    x, y, c = _mesh_pos()
    idx = jnp.stack([_dev_index(*chip, c) for chip in _other_chips(x, y)]).astype(jnp.int32)

    def body(idx_ref, g_ref, p_ref, o_ref):
        o_ref[...] = (g_ref[...].astype(F32) + p_ref[...].astype(F32)).astype(BF16)

    grid_spec = pltpu.PrefetchScalarGridSpec(
        num_scalar_prefetch=1, grid=(3,),
        in_specs=[pl.BlockSpec((r, d), lambda k, idx_ref: (idx_ref[k], 0)),
                  pl.BlockSpec((None, r, d), lambda k, idx_ref: (1 + k, 0, 0))],
        out_specs=pl.BlockSpec((None, r, d), lambda k, idx_ref: (k, 0, 0)))
    return pl.pallas_call(body, out_shape=jax.ShapeDtypeStruct((3, r, d), BF16), grid_spec=grid_spec, name=name,
                          compiler_params=_cparams(1))(idx, g, pair)


def _chip_start(sums, name):
    def body(sums_ref, land_ref, send_sem, recv_sem, sums_thru, land_thru, token):
        x, y, c = _mesh_pos()
        for k, chip in enumerate(_other_chips(x, y)):
            _remote(sums_ref.at[k], land_ref.at[k], send_sem, recv_sem, (*chip, c)).start()
        token[...] = jnp.zeros_like(token)

    zone = pltpu.HBM(sums.shape, sums.dtype)
    res = pl.pallas_call(
        body, name=name, out_shape=(pltpu.SemaphoreType.DMA(()), pltpu.SemaphoreType.DMA(()), zone, zone,
                                    jax.ShapeDtypeStruct((SUBLANE, LANE), F32)),
        in_specs=[HBM, HBM], out_specs=(SEM, SEM, HBM, HBM, VMEM), input_output_aliases={0: 2, 1: 3},
        compiler_params=pltpu.CompilerParams(has_side_effects=FLOWS))(_in_hbm(sums), _in_hbm(lax.empty(sums.shape, sums.dtype)))
    return (res[0], res[1]), res[2], res[3], res[4]


def _chip_wait(sums, land, sems, after, name):
    def body(sums_ref, land_ref, send_sem, recv_sem, after_ref, sums_out, land_out):
        x, y, c = _mesh_pos()
        cp = _remote(sums_ref, land_ref, send_sem, recv_sem, (x, y, 1 - c))
        cp.wait_send()
        cp.wait_recv()

    res = pl.pallas_call(
        body, name=name, out_shape=(pltpu.HBM(sums.shape, sums.dtype), pltpu.HBM(land.shape, land.dtype)),
        in_specs=[HBM, HBM, SEM, SEM, ANY], out_specs=(HBM, HBM), input_output_aliases={0: 0, 1: 1},
        compiler_params=pltpu.CompilerParams(has_side_effects=FLOWS))(sums, land, sems[0], sems[1], after)
    return res[1]


def _gather_first(shards, sm, jobs):
    d = shards[0].shape[1]
    rows = [w.shape[0] for w in shards]
    lo = [sum(rows[:p]) for p in range(N_BIG)]
    n_sems = GATHER_SEMS * len(jobs)

    def body(s0, s1, s2, s3, s4, sm_ref, wl_ref, o0, o1, o2, o3, o4, sa_ref, wl_v, send_sems, recv_sems, ssend, srecv, local_sems):
        dests = (o0, o1, o2, o3, o4)
        x, y, c = _mesh_pos()
        me = (x, y, c)
        jme = _dev_index(*me)
        for p, ref in enumerate((s0, s1, s2, s3, s4)):
            wl_v[pl.ds(lo[p], rows[p]), :] = ref[...].astype(BF16)
        gather = _Gather(jobs, rows, lo, wl_v, dict(enumerate(dests)), send_sems, recv_sems)
        gather.start()
        peers = [(x, y, 1 - c)] + [(*chip, pc) for pc in (c, 1 - c) for chip in _other_chips(x, y)]
        smalls = [_remote(sm_ref, sa_ref.at[jme], ssend.at[k], srecv.at[k], to) for k, to in enumerate(peers)]
        for cp in smalls:
            cp.start()
        mine = [pltpu.make_async_copy(wl_v.at[pl.ds(lo[p], rows[p]), :],
                                      dests[p].at[pl.ds(pl.multiple_of(jme * rows[p], BF16_ROWS), rows[p]), :], local_sems.at[p])
                for p in range(N_BIG)]
        mine.append(pltpu.make_async_copy(wl_v, wl_ref, local_sems.at[N_BIG]))
        mine.append(pltpu.make_async_copy(sm_ref, sa_ref.at[jme], local_sems.at[N_BIG + 1]))
        for cp in mine:
            cp.start()
        later = [p for p in range(N_BIG) if p not in {j[0] for j in jobs}]
        own = [_SplitRemote(wl_v.at[pl.ds(lo[p], rows[p]), :],
                            dests[p].at[pl.ds(pl.multiple_of(jme * rows[p], BF16_ROWS), rows[p]), :],
                            ssend.at[7 + i], srecv.at[7 + i], (x, y, 1 - c), rows[p], D2D_CHUNKS) for i, p in enumerate(later)]
        for cp in own:
            cp.start()
        gather.mid()
        gather.finish()
        for cp in smalls + own:
            cp.wait_recv()
        for cp in smalls + own:
            cp.wait_send()
        for cp in mine:
            cp.wait()

    out_shape = [jax.ShapeDtypeStruct((sum(rows), d), BF16)]
    out_shape += [jax.ShapeDtypeStruct((N_DEV * r, d), BF16) for r in rows]
    out_shape.append(jax.ShapeDtypeStruct((N_DEV,) + sm.shape, F32))
    res = pl.pallas_call(
        body, out_shape=out_shape, in_specs=[VMEM] * 6, out_specs=[ANY] * 7,
        scratch_shapes=[pltpu.VMEM((sum(rows), d), BF16), pltpu.SemaphoreType.DMA((n_sems,)), pltpu.SemaphoreType.DMA((n_sems,)),
                        pltpu.SemaphoreType.DMA((7 + N_BIG,)), pltpu.SemaphoreType.DMA((7 + N_BIG,)),
                        pltpu.SemaphoreType.DMA((N_BIG + 2,))],
        name="gather_first", compiler_params=_cparams())(*shards, sm)
    return res[0], list(res[1:1 + N_BIG]), res[-1]


def _prep(x2, tgt2, meta_full, t_rows, x0, after):
    s, d = x2.shape
    assert x0 == ROW_ALIGN and s % ROW_ALIGN == 0

    def body(x_ref, tgt_ref, meta_ref, h0_ref, tp_ref):
        i = pl.program_id(0)

        @pl.when(i == 0)
        def _():
            h0_ref[...] = jnp.zeros_like(h0_ref)
            h0_ref[pl.ds(x0 - N_META, N_META), :] = meta_ref[...]
            tp_ref[...] = jnp.zeros_like(tp_ref)

        @pl.when(i > 0)
        def _():
            h0_ref[...] = x_ref[...]
            tp_ref[...] = tgt_ref[...]

    src = pl.BlockSpec((ROW_ALIGN, d), lambda i: (jnp.maximum(i - 1, 0), 0))
    dst = pl.BlockSpec((ROW_ALIGN, d), lambda i: (i, 0))
    res, _ = _host_call(body, grid=(t_rows // ROW_ALIGN,), in_specs=[src, src, _full((N_META, d))], out_specs=[dst, dst],
                        out_shape=[jax.ShapeDtypeStruct((t_rows, d), F32)] * 2, args=(x2, tgt2, meta_full), name="prep", after=after)
    return res


def _in_proj(h0, g1, win_t, tm, comm):
    t_rows, d = h0.shape
    e = win_t.shape[0]

    def body(h_ref, g_ref, w_ref, xn_ref, hin_ref):
        h = h_ref[...]
        xn = ((h * _rstd(h)) * g_ref[...]).astype(BF16)
        xn_ref[...] = xn
        for o, n in _chunks(e, N_CHUNK):
            hin_ref[:, pl.ds(o, n)] = _dot_nt(xn, w_ref[pl.ds(o, n), :])

    return _host_call(
        body, grid=(t_rows // tm,),
        in_specs=[pl.BlockSpec((tm, d), lambda i: (i, 0)), _full((1, d)), _resident((e, d))],
        out_specs=[pl.BlockSpec((tm, d), lambda i: (i, 0)), pl.BlockSpec((tm, e), lambda i: (i, 0))],
        out_shape=[jax.ShapeDtypeStruct((t_rows, d), BF16), jax.ShapeDtypeStruct((t_rows, e), F32)],
        args=(h0, g1, win_t), name="in_proj", comm=comm)


def _tap_slot(off):
    return off % SUBLANE, (off // SUBLANE) * SUBLANE


def _fill_shifted(sh_ref, base_ref, residues, n_rows):
    for r in residues:
        sh_ref[r] = base_ref[pl.ds(r, n_rows), :]


def _mix_conv_fwd(hin, wa, wb, bb, wa_w, comm):
    t_rows = hin.shape[0]
    nt = wa_w // LANE
    ka, kb = wa.shape[0], wb.shape[0]
    nr = CONV_HALO + t_rows

    def body(bg_ref, cg_ref, ha_ref, val_ref, gt_ref, wa_ref, wb_ref, bb_ref, ya_ref, z_ref, base, sh):
        base[pl.ds(0, CONV_HALO), :] = jnp.zeros((CONV_HALO, LANE), F32)
        base[pl.ds(nr, SUBLANE), :] = jnp.zeros((SUBLANE, LANE), F32)

        def conv(w_ref, k_taps, b, n):
            acc = None
            for k in range(k_taps):
                r, q = _tap_slot(CONV_HALO - (k_taps - 1) + k)
                term = w_ref[pl.ds(k, 1), :] * sh[r, pl.ds(b + q, n), :]
                acc = term if acc is None else acc + term
            return acc

        def fill_a(b, c):
            base[pl.ds(CONV_HALO + b, CONV_CHUNK), :] = cg_ref[pl.ds(b, CONV_CHUNK), :] * ha_ref[pl.ds(b, CONV_CHUNK), :]
            return c

        _row_loop(t_rows, CONV_CHUNK, fill_a)
        _fill_shifted(sh, base, sorted({_tap_slot(CONV_HALO - (ka - 1) + k)[0] for k in range(ka)}), nr)

        def out_a(b, c):
            ya_ref[pl.ds(b, CONV_CHUNK), :] = (bg_ref[pl.ds(b, CONV_CHUNK), :] * conv(wa_ref, ka, b, CONV_CHUNK)).astype(BF16)
            return c

        _row_loop(t_rows, CONV_CHUNK, out_a)

        def fill_b(b, c):
            base[pl.ds(CONV_HALO + b, CONV_CHUNK), :] = (val_ref[pl.ds(b, CONV_CHUNK), :]
                                                          * jax.nn.sigmoid(gt_ref[pl.ds(b, CONV_CHUNK), :]))
            return c

        _row_loop(t_rows, CONV_CHUNK, fill_b)
        _fill_shifted(sh, base, range(SUBLANE), nr)

        def out_b(b, c):
            z_ref[pl.ds(b, CONV_CHUNK), :] = conv(wb_ref, kb, b, CONV_CHUNK) + bb_ref[...]
            return c

        _row_loop(t_rows, CONV_CHUNK, out_b)

    def col(g):
        return pl.BlockSpec((t_rows, LANE), lambda i, g=g: (0, g * nt + i))

    tile = lambda rows: pl.BlockSpec((rows, LANE), lambda i: (0, i))
    return _host_call(
        body, grid=(nt,),
        in_specs=[col(0), col(1), col(2), col(3), col(4), tile(ka), tile(kb), tile(1)],
        out_specs=[tile(t_rows), tile(t_rows)],
        out_shape=[jax.ShapeDtypeStruct((t_rows, wa_w), BF16), jax.ShapeDtypeStruct((t_rows, wa_w), F32)],
        scratch_shapes=[pltpu.VMEM((nr + SUBLANE, LANE), F32), pltpu.VMEM((SUBLANE, nr, LANE), F32)],
        args=(hin, hin, hin, hin, hin, wa, wb, bb), name="mix_conv_fwd", comm=comm)


def _ln_parts(z, lg, lb):
    mu = jnp.mean(z, axis=-1, keepdims=True)
    zc = z - mu
    rstd = lax.rsqrt(jnp.mean(zc * zc, axis=-1, keepdims=True) + LN_EPS)
    zh = zc * rstd
    return zh, rstd, zh * lg + lb


def _mix_ln_fwd(ya, z, lg, lb, tm):
    t_rows, w = z.shape

    def body(ya_ref, z_ref, lg_ref, lb_ref, y_ref):
        _, _, ln = _ln_parts(z_ref[...], lg_ref[...], lb_ref[...])
        y_ref[:, pl.ds(0, w)] = ya_ref[...]
        y_ref[:, pl.ds(w, w)] = (ln * jax.nn.sigmoid(ln)).astype(BF16)

    blk = pl.BlockSpec((tm, w), lambda i: (i, 0))
    res, _ = _host_call(body, grid=(t_rows // tm,), in_specs=[blk, blk, _full((1, w)), _full((1, w))],
                        out_specs=[pl.BlockSpec((tm, 2 * w), lambda i: (i, 0))],
                        out_shape=[jax.ShapeDtypeStruct((t_rows, 2 * w), BF16)], args=(ya, z, lg, lb), name="mix_ln_fwd")
    return res[0]


def _out_proj(y, w_out, h0, g2, g3, tm, comm):
    t_rows, d = h0.shape

    def body(y_ref, w_ref, h0_ref, g2_ref, g3_ref, mix_ref, h1_ref, xn2_ref):
        mix = _dot_nn(y_ref[...], w_ref[...])
        mix_ref[...] = mix
        h1 = h0_ref[...] + (mix * _rstd(mix)) * g2_ref[...]
        h1_ref[...] = h1
        xn2_ref[...] = ((h1 * _rstd(h1)) * g3_ref[...]).astype(BF16)

    blk = pl.BlockSpec((tm, d), lambda i: (i, 0))
    return _host_call(
        body, grid=(t_rows // tm,), in_specs=[blk, _resident(w_out.shape), blk, _full((1, d)), _full((1, d))],
        out_specs=[blk, blk, blk],
        out_shape=[jax.ShapeDtypeStruct((t_rows, d), F32), jax.ShapeDtypeStruct((t_rows, d), F32),
                   jax.ShapeDtypeStruct((t_rows, d), BF16)],
        args=(y, w_out, h0, g2, g3), name="out_proj", comm=comm)


def _gate_up(xn2, wg_t, wu_t, tm, comm):
    t_rows, d = xn2.shape
    f = wg_t.shape[0]

    def body(x_ref, wg_ref, wu_ref, a_ref, u_ref, s_ref):
        xn = x_ref[...]
        for o, n in _chunks(f, N_CHUNK):
            a = _dot_nt(xn, wg_ref[pl.ds(o, n), :])
            u = _dot_nt(xn, wu_ref[pl.ds(o, n), :])
            a_ref[:, pl.ds(o, n)] = a.astype(BF16)
            u_ref[:, pl.ds(o, n)] = u.astype(BF16)
            s_ref[:, pl.ds(o, n)] = ((a * jax.nn.sigmoid(a)) * u).astype(BF16)

    blk = pl.BlockSpec((tm, f), lambda i: (i, 0))
    return _host_call(
        body, grid=(t_rows // tm,),
        in_specs=[pl.BlockSpec((tm, d), lambda i: (i, 0)), _resident((f, d)), _resident((f, d))],
        out_specs=[blk, blk, blk], out_shape=[jax.ShapeDtypeStruct((t_rows, f), BF16)] * 3,
        args=(xn2, wg_t, wu_t), name="gate_up", comm=comm)


def _down_loss(s, wd, h1, tgt, g4, tm, x0):
    t_rows, d = h1.shape
    f = wd.shape[0]

    def body(s_ref, w_ref, h1_ref, tgt_ref, g4_ref, dh2_ref, dff_ref, dg4_ref, loss_ref):
        i = pl.program_id(0)
        ff = _dot_nn(s_ref[...], w_ref[...])
        r4 = _rstd(ff)
        fh = ff * r4
        g4 = g4_ref[...]
        h2 = h1_ref[...] + fh * g4
        row = i * tm + lax.broadcasted_iota(jnp.int32, (tm, 1), 0)
        diff = jnp.where(row >= x0, h2 - tgt_ref[...], 0.0)
        dh2 = diff / d
        dh2_ref[...] = dh2
        dff_ref[...] = _rms_bwd(dh2 * g4, fh, r4).astype(BF16)
        _acc_rows(dg4_ref, dh2 * fh, i == 0)
        _acc_rows(loss_ref, diff * diff, i == 0)

    blk = pl.BlockSpec((tm, d), lambda i: (i, 0))
    res, _ = _host_call(
        body, grid=(t_rows // tm,),
        in_specs=[pl.BlockSpec((tm, f), lambda i: (i, 0)), _resident((f, d)), blk, blk, _full((1, d))],
        out_specs=[blk, blk, _full((1, d)), _full((1, d))],
        out_shape=[jax.ShapeDtypeStruct((t_rows, d), F32), jax.ShapeDtypeStruct((t_rows, d), BF16),
                   jax.ShapeDtypeStruct((1, d), F32), jax.ShapeDtypeStruct((1, d), F32)],
        args=(s, wd, h1, tgt, g4), name="down_loss")
    return res


def _bwd_down(dff, wd, a, u, tm, comm):
    t_rows, d = dff.shape
    f = wd.shape[0]

    def body(dff_ref, w_ref, a_ref, u_ref, da_ref, du_ref):
        dff_v = dff_ref[...]
        for o, n in _chunks(f, N_CHUNK):
            ds = _dot_nt(dff_v, w_ref[pl.ds(o, n), :])
            av = a_ref[:, pl.ds(o, n)].astype(F32)
            uv = u_ref[:, pl.ds(o, n)].astype(F32)
            sig = jax.nn.sigmoid(av)
            da_ref[:, pl.ds(o, n)] = (ds * uv * _silu_grad(av, sig)).astype(BF16)
            du_ref[:, pl.ds(o, n)] = (ds * (av * sig)).astype(BF16)

    blk = pl.BlockSpec((tm, f), lambda i: (i, 0))
    return _host_call(
        body, grid=(t_rows // tm,),
        in_specs=[pl.BlockSpec((tm, d), lambda i: (i, 0)), _resident((f, d)), blk, blk],
        out_specs=[blk, blk], out_shape=[jax.ShapeDtypeStruct((t_rows, f), BF16)] * 2,
        args=(dff, wd, a, u), name="bwd_down", comm=comm)


def _wgrad(a, b, name, after=()):
    d = b.shape[1]
    t_rows = b.shape[0]
    stacked = a.ndim == 3
    n = a.shape[-1]
    groups = a.shape[0] if stacked else 1
    steps = 1 if stacked else 2
    tile = max(t for t in range(LANE, min(n // steps, WGRAD_TILE_MAX) + 1, LANE) if n % t == 0)
    tiles = n // tile

    def body(a_ref, b_ref, o_ref):
        o_ref[...] = lax.dot_general(a_ref[...], b_ref[...], (((0,), (0,)), ((), ())),
                                     preferred_element_type=F32).astype(BF16)

    if stacked:
        a_spec = pl.BlockSpec((None, t_rows, tile), lambda g, i: (g, 0, i))
    else:
        a_spec = pl.BlockSpec((t_rows, tile), lambda g, i: (0, i))
    res, _ = _host_call(
        body, grid=(groups, tiles), in_specs=[a_spec, _resident((t_rows, d))],
        out_specs=[pl.BlockSpec((tile, d), lambda g, i: (g * tiles + i, 0))],
        out_shape=[jax.ShapeDtypeStruct((groups * n, d), BF16)], args=(a, b), name=name, after=after)
    return res[0]


def _bwd_ffn_in(da, du, wg_t, wu_t, h1, dh2, g3, tm, comm):
    t_rows, d = h1.shape
    f = wg_t.shape[0]

    def body(da_ref, du_ref, wg_ref, wu_ref, h1_ref, dh2_ref, g3_ref, dh1_ref, dg3_ref):
        dxn2 = _dot_nn(da_ref[...], wg_ref[...]) + _dot_nn(du_ref[...], wu_ref[...])
        h1 = h1_ref[...]
        r3 = _rstd(h1)
        h1h = h1 * r3
        _acc_rows(dg3_ref, dxn2 * h1h, pl.program_id(0) == 0)
        dh1_ref[...] = dh2_ref[...] + _rms_bwd(dxn2 * g3_ref[...], h1h, r3)

    blk = pl.BlockSpec((tm, d), lambda i: (i, 0))
    blkf = pl.BlockSpec((tm, f), lambda i: (i, 0))
    return _host_call(
        body, grid=(t_rows // tm,),
        in_specs=[blkf, blkf, _resident((f, d)), _resident((f, d)), blk, blk, _full((1, d))],
        out_specs=[blk, _full((1, d))],
        out_shape=[jax.ShapeDtypeStruct((t_rows, d), F32), jax.ShapeDtypeStruct((1, d), F32)],
        args=(da, du, wg_t, wu_t, h1, dh2, g3), name="bwd_ffn_in", comm=comm)


def _bwd_out_proj(dh1, mix, w_out, g2, tm, after):
    t_rows, d = dh1.shape

    def body(dh1_ref, mix_ref, w_ref, g2_ref, dmix_ref, dy_ref, dg2_ref):
        mix = mix_ref[...]
        r2 = _rstd(mix)
        mh = mix * r2
        dh1 = dh1_ref[...]
        _acc_rows(dg2_ref, dh1 * mh, pl.program_id(0) == 0)
        dmix = _rms_bwd(dh1 * g2_ref[...], mh, r2).astype(BF16)
        dmix_ref[...] = dmix
        dy_ref[...] = _dot_nt(dmix, w_ref[...])

    blk = pl.BlockSpec((tm, d), lambda i: (i, 0))
    res, _ = _host_call(
        body, grid=(t_rows // tm,), in_specs=[blk, blk, _resident(w_out.shape), _full((1, d))],
        out_specs=[blk, blk, _full((1, d))],
        out_shape=[jax.ShapeDtypeStruct((t_rows, d), BF16), jax.ShapeDtypeStruct((t_rows, d), F32),
                   jax.ShapeDtypeStruct((1, d), F32)],
        args=(dh1, mix, w_out, g2), name="bwd_out_proj", after=after)
    return res


def _mix_ln_bwd(z, dy, lg, lb, tm):
    t_rows, w = z.shape

    def body(z_ref, dyb_ref, lg_ref, lb_ref, dz_ref, dlg_ref, dlb_ref, dbb_ref):
        first = pl.program_id(0) == 0
        lg = lg_ref[...]
        zh, rstd, ln = _ln_parts(z_ref[...], lg, lb_ref[...])
        dln = dyb_ref[...] * _silu_grad(ln, jax.nn.sigmoid(ln))
        _acc_rows(dlg_ref, dln * zh, first)
        _acc_rows(dlb_ref, dln, first)
        dzh = dln * lg
        dz = rstd * (dzh - jnp.mean(dzh, axis=-1, keepdims=True) - zh * jnp.mean(dzh * zh, axis=-1, keepdims=True))
        dz_ref[...] = dz
        _acc_rows(dbb_ref, dz, first)

    blk = pl.BlockSpec((tm, w), lambda i: (i, 0))
    vec = _full((1, w))
    res, _ = _host_call(
        body, grid=(t_rows // tm,), in_specs=[blk, pl.BlockSpec((tm, w), lambda i: (i, 1)), vec, vec],
        out_specs=[blk, vec, vec, vec],
        out_shape=[jax.ShapeDtypeStruct((t_rows, w), F32)] + [jax.ShapeDtypeStruct((1, w), F32)] * 3,
        args=(z, dy, lg, lb), name="mix_ln_bwd")
    return res


def _mix_conv_bwd(hin, dy, dz, wa, wb, wa_w, comm):
    t_rows = hin.shape[0]
    nt = wa_w // LANE
    ka, kb = wa.shape[0], wb.shape[0]
    nr = CONV_HALO + t_rows
    kb_rows = -(-kb // SUBLANE) * SUBLANE

    def body(bg_ref, cg_ref, ha_ref, val_ref, gt_ref, dya_ref, dz_ref, wa_ref, wb_ref,
             dh_ref, dwa_ref, dwb_ref, base, sh, based, shd, tmp, wbc):
        zeros = lambda n: jnp.zeros((n, LANE), F32)
        base[pl.ds(0, CONV_HALO), :] = zeros(CONV_HALO)
        base[pl.ds(nr, SUBLANE), :] = zeros(SUBLANE)
        based[pl.ds(t_rows, CONV_HALO + SUBLANE), :] = zeros(CONV_HALO + SUBLANE)

        def fwd_slot(k_taps, k):
            return _tap_slot(CONV_HALO - (k_taps - 1) + k)

        def bwd_slot(k_taps, k):
            return _tap_slot(k_taps - 1 - k)

        def conv(w_ref, k_taps, src, slot, b, n):
            acc = None
            for k in range(k_taps):
                r, q = slot(k_taps, k)
                term = w_ref[pl.ds(k, 1), :] * src[r, pl.ds(b + q, n), :]
                acc = term if acc is None else acc + term
            return acc

        def by_residue(k_taps, slot):
            groups = {}
            for k in range(k_taps):
                r, q = slot(k_taps, k)
                groups.setdefault(r, []).append((k, q // SUBLANE))
            return groups

        def wgrad_loop(w_ref, k_taps):
            n_sub = WGRAD_ROWS // SUBLANE
            for k in range(k_taps):
                wbc[k] = jnp.broadcast_to(w_ref[pl.ds(k, 1), :], (SUBLANE, LANE))
            fwd, bwd = by_residue(k_taps, fwd_slot), by_residue(k_taps, bwd_slot)

            def window(src, r, taps, b):
                span = n_sub + max(qi for _, qi in taps)
                return [src[r, pl.ds(b + SUBLANE * i, SUBLANE), :] for i in range(span)]

            def step(b, accs):
                accs = list(accs)
                dv = [based[pl.ds(b + SUBLANE * j, SUBLANE), :] for j in range(n_sub)]
                for r, taps in fwd.items():
                    win = window(sh, r, taps, b)
                    for k, qi in taps:
                        t = dv[0] * win[qi]
                        for j in range(1, n_sub):
                            t = t + dv[j] * win[qi + j]
                        accs[k] = accs[k] + t
                outs = [None] * n_sub
                for r, taps in bwd.items():
                    win = window(shd, r, taps, b)
                    for k, qi in taps:
                        wk = wbc[k]
                        for j in range(n_sub):
                            term = wk * win[qi + j]
                            outs[j] = term if outs[j] is None else outs[j] + term
                for j in range(n_sub):
                    tmp[pl.ds(b + SUBLANE * j, SUBLANE), :] = outs[j]
                return tuple(accs)

            return _row_loop(t_rows, WGRAD_ROWS, step, tuple(zeros(SUBLANE) for _ in range(k_taps)))

        def store_taps(ref, accs, rows):
            for k, acc in enumerate(accs):
                ref[pl.ds(k, 1), :] = jnp.sum(acc, axis=0, keepdims=True)
            if rows > len(accs):
                ref[pl.ds(len(accs), rows - len(accs)), :] = zeros(rows - len(accs))

        def fill_a(b, c):
            sl = pl.ds(b, CONV_CHUNK)
            base[pl.ds(CONV_HALO + b, CONV_CHUNK), :] = cg_ref[sl, :] * ha_ref[sl, :]
            based[sl, :] = dya_ref[sl, :] * bg_ref[sl, :]
            return c

        _row_loop(t_rows, CONV_CHUNK, fill_a)
        _fill_shifted(sh, base, sorted({fwd_slot(ka, k)[0] for k in range(ka)}), nr)
        _fill_shifted(shd, based, sorted({bwd_slot(ka, k)[0] for k in range(ka)}), nr)

        def d_bgate(b, c):
            sl = pl.ds(b, CONV_CHUNK)
            dh_ref[0, sl, :] = (dya_ref[sl, :] * conv(wa_ref, ka, sh, fwd_slot, b, CONV_CHUNK)).astype(BF16)
            return c

        _row_loop(t_rows, CONV_CHUNK, d_bgate)
        store_taps(dwa_ref, wgrad_loop(wa_ref, ka), SUBLANE)

        def d_ch(b, c):
            sl = pl.ds(b, CONV_CHUNK)
            dua = tmp[sl, :]
            dh_ref[1, sl, :] = (dua * ha_ref[sl, :]).astype(BF16)
            dh_ref[2, sl, :] = (dua * cg_ref[sl, :]).astype(BF16)
            return c

        _row_loop(t_rows, CONV_CHUNK, d_ch)

        def fill_b(b, c):
            sl = pl.ds(b, CONV_CHUNK)
            base[pl.ds(CONV_HALO + b, CONV_CHUNK), :] = val_ref[sl, :] * jax.nn.sigmoid(gt_ref[sl, :])
            based[sl, :] = dz_ref[sl, :]
            return c

        _row_loop(t_rows, CONV_CHUNK, fill_b)
        _fill_shifted(sh, base, range(SUBLANE), nr)
        _fill_shifted(shd, based, range(SUBLANE), nr)
        store_taps(dwb_ref, wgrad_loop(wb_ref, kb), kb_rows)

        def d_glu(b, c):
            sl = pl.ds(b, CONV_CHUNK)
            dgg = tmp[sl, :]
            sig = jax.nn.sigmoid(gt_ref[sl, :])
            dh_ref[3, sl, :] = (dgg * sig).astype(BF16)
            dh_ref[4, sl, :] = (dgg * val_ref[sl, :] * (sig * (1.0 - sig))).astype(BF16)
            return c

        _row_loop(t_rows, CONV_CHUNK, d_glu)

    def col(g):
        return pl.BlockSpec((t_rows, LANE), lambda i, g=g: (0, g * nt + i))

    tile = lambda rows: pl.BlockSpec((rows, LANE), lambda i: (0, i))
    return _host_call(
        body, grid=(nt,),
        in_specs=[col(0), col(1), col(2), col(3), col(4), tile(t_rows), tile(t_rows), tile(ka), tile(kb)],
        out_specs=[pl.BlockSpec((5, t_rows, LANE), lambda i: (0, 0, i)), tile(SUBLANE), tile(kb_rows)],
        out_shape=[jax.ShapeDtypeStruct((5, t_rows, wa_w), BF16), jax.ShapeDtypeStruct((SUBLANE, wa_w), F32),
                   jax.ShapeDtypeStruct((kb_rows, wa_w), F32)],
        scratch_shapes=[pltpu.VMEM((nr + SUBLANE, LANE), F32), pltpu.VMEM((SUBLANE, nr, LANE), F32),
                        pltpu.VMEM((nr + SUBLANE, LANE), F32), pltpu.VMEM((SUBLANE, nr, LANE), F32),
                        pltpu.VMEM((t_rows, LANE), F32), pltpu.VMEM((kb_rows, SUBLANE, LANE), F32)],
        args=(hin, hin, hin, hin, hin, dy, dz, wa, wb), name="mix_conv_bwd", comm=comm)


def _bwd_in_proj(dh5, win_t, h0, dh1, g1, tm, comm):
    t_rows, d = h0.shape
    groups, _, w = dh5.shape

    def body(dh_ref, w_ref, h0_ref, dh1_ref, g1_ref, dh0_ref, dg1_ref):
        dxn1 = None
        for g in range(groups):
            part = _dot_nn(dh_ref[g], w_ref[pl.ds(g * w, w), :])
            dxn1 = part if dxn1 is None else dxn1 + part
        h0 = h0_ref[...]
        r1 = _rstd(h0)
        h0h = h0 * r1
        _acc_rows(dg1_ref, dxn1 * h0h, pl.program_id(0) == 0)
        dh0_ref[...] = dh1_ref[...] + _rms_bwd(dxn1 * g1_ref[...], h0h, r1)

    blk = pl.BlockSpec((tm, d), lambda i: (i, 0))
    return _host_call(
        body, grid=(t_rows // tm,),
        in_specs=[pl.BlockSpec((groups, tm, w), lambda i: (0, i, 0)), _resident(win_t.shape), blk, blk, _full((1, d))],
        out_specs=[blk, _full((1, d))],
        out_shape=[jax.ShapeDtypeStruct((t_rows, d), F32), jax.ShapeDtypeStruct((1, d), F32)],
        args=(dh5, win_t, h0, dh1, g1), name="bwd_in_proj", comm=comm)


def _reduce_small(smalls, d, after):
    (dmeta, dg1, dg2, dg3, dg4, dbb, dlg, dlb, lossv, dwa, dwb) = smalls
    half = d // 2
    kb_rows = dwb.shape[0]

    def body(dmeta_ref, dg1_ref, dg2_ref, dg3_ref, dg4_ref, dbb_ref, dlg_ref, dlb_ref, loss_ref, dwa_ref, dwb_ref,
             ptot_ref, pbuf, psib, chip_p, ps_send, ps_recv, pc_send, pc_recv):
        x, y, c = _mesh_pos()
        pbuf[...] = jnp.zeros_like(pbuf)
        pbuf[pl.ds(0, N_META), :] = dmeta_ref[...]
        for row, ref in ((16, dg1_ref), (17, dg2_ref), (18, dg3_ref), (19, dg4_ref)):
            pbuf[pl.ds(row, 1), :] = ref[...]
        pbuf[pl.ds(20, 1), pl.ds(0, half)] = dbb_ref[...]
        pbuf[pl.ds(20, 1), pl.ds(half, half)] = dlg_ref[...]
        pbuf[pl.ds(21, 1), pl.ds(0, half)] = dlb_ref[...]
        lv = loss_ref[...]
        pbuf[pl.ds(21, 1), pl.ds(half, half)] = lv[:, :half] + lv[:, half:]
        pbuf[pl.ds(24, SUBLANE), pl.ds(0, half)] = dwa_ref[...]
        pbuf[pl.ds(32, kb_rows), pl.ds(0, half)] = dwb_ref[...]
        to_sib = _remote(pbuf, psib, ps_send.at[0], ps_recv.at[0], (x, y, 1 - c))
        to_sib.start()
        to_sib.wait_recv()
        my_chip = 2 * x + y
        chip_p[my_chip] = pbuf[...] + psib[...]
        to_sib.wait_send()
        slot = chip_p.at[my_chip]
        cps = [_remote(slot, slot, pc_send.at[k], pc_recv.at[k], (*chip, c)) for k, chip in enumerate(_other_chips(x, y))]
        for cp in cps:
            cp.start()
        for cp in cps:
            cp.wait_recv()
        ptot_ref[...] = ((chip_p[0] + chip_p[1]) + chip_p[2]) + chip_p[3]
        for cp in cps:
            cp.wait_send()

    return _host_call(
        body, grid=(), in_specs=[VMEM] * 11, out_specs=[VMEM], out_shape=[jax.ShapeDtypeStruct((SMALL_ROWS, d), F32)],
        scratch_shapes=[pltpu.VMEM((SMALL_ROWS, d), F32), pltpu.VMEM((SMALL_ROWS, d), F32), pltpu.VMEM((4, SMALL_ROWS, d), F32),
                        pltpu.SemaphoreType.DMA((1,)), pltpu.SemaphoreType.DMA((1,)),
                        pltpu.SemaphoreType.DMA((3,)), pltpu.SemaphoreType.DMA((3,))],
        args=smalls, name="reduce_small", after=after)


def _adamw(w, g, m, v):
    m = ADAM_B1 * m + (1.0 - ADAM_B1) * g
    v = ADAM_B2 * v + (1.0 - ADAM_B2) * jnp.square(g)
    m_hat = m / (1.0 - ADAM_B1 ** ADAM_STEP)
    v_hat = v / (1.0 - ADAM_B2 ** ADAM_STEP)
    delta = -ADAM_LR * (m_hat / (jnp.sqrt(v_hat) + ADAM_EPS) + ADAM_WD * w)
    return delta, m, v


def _adam_big(g, pair, part, w, m, v, name):
    r, d = w.shape
    cols = d // ADAM_COL_BLOCKS

    def body(me_ref, g_ref, pair_ref, part_ref, w_ref, m_ref, v_ref, go_ref, d_ref, mo_ref, vo_ref):
        g = g_ref[...].astype(F32) + pair_ref[...].astype(F32)
        for k in range(3):
            g = g + part_ref[k].astype(F32)
        go_ref[...] = g
        d_ref[...], mo_ref[...], vo_ref[...] = _adamw(w_ref[...], g, m_ref[...], v_ref[...])

    blk = pl.BlockSpec((r, cols), lambda i, me_ref: (0, i))
    grid_spec = pltpu.PrefetchScalarGridSpec(
        num_scalar_prefetch=1, grid=(ADAM_COL_BLOCKS,),
        in_specs=[pl.BlockSpec((r, cols), lambda i, me_ref: (me_ref[0], i)),
                  pl.BlockSpec((None, r, cols), lambda i, me_ref: (0, 0, i)),
                  pl.BlockSpec((3, r, cols), lambda i, me_ref: (0, 0, i)), blk, blk, blk],
        out_specs=[blk, blk, blk, blk])
    me = jnp.reshape(_dev_index(*_mesh_pos()), (1,)).astype(jnp.int32)
    return pl.pallas_call(body, out_shape=[jax.ShapeDtypeStruct((r, d), F32)] * 4, grid_spec=grid_spec, name=name,
                          compiler_params=_cparams(1))(me, g, pair, part, w, m, v)


def _adam_small(gs, ws, ms, vs):
    n = len(gs)

    def body(*refs):
        ins, outs = refs[:4 * n], refs[4 * n:]
        for i in range(n):
            g = ins[i][...]
            delta, m, v = _adamw(ins[n + i][...], g, ins[2 * n + i][...], ins[3 * n + i][...])
            outs[i][...] = delta
            outs[n + i][...] = m
            outs[2 * n + i][...] = v

    shapes = [jax.ShapeDtypeStruct(w.shape, F32) for w in ws]
    return pl.pallas_call(body, out_shape=shapes * 3, name="adam_small", compiler_params=_cparams())(*gs, *ws, *ms, *vs)


def kernel(x, meta_tokens, pre_mix_norm, w_in, conv_a_w, conv_b_w, conv_b_bias, ln_b_gain, ln_b_bias, w_out, post_mix_norm, pre_ffn_norm, w_gate, w_up, w_down, post_ffn_norm, loss_target, m_meta_tokens, m_pre_mix_norm, m_w_in, m_conv_a_w, m_conv_b_w, m_conv_b_bias, m_ln_b_gain, m_ln_b_bias, m_w_out, m_post_mix_norm, m_pre_ffn_norm, m_w_gate, m_w_up, m_w_down, m_post_ffn_norm, v_meta_tokens, v_pre_mix_norm, v_w_in, v_conv_a_w, v_conv_b_w, v_conv_b_bias, v_ln_b_gain, v_ln_b_bias, v_w_out, v_post_mix_norm, v_pre_ffn_norm, v_w_gate, v_w_up, v_w_down, v_post_ffn_norm):
    _, seq, d = x.shape
    ka, ca_loc = conv_a_w.shape[1:]
    kb, cb_loc = conv_b_w.shape[1:]
    wa_w = ca_loc * N_DEV
    assert cb_loc == ca_loc and wa_w % LANE == 0 and w_in.shape[2] * N_DEV == 5 * wa_w
    pad = (-(N_META + seq)) % ROW_ALIGN
    x0 = pad + N_META
    t_rows = x0 + seq
    assert t_rows % (N_ROW_BLOCKS * BF16_ROWS) == 0 and t_rows % CONV_CHUNK == 0 and d % LANE == 0
    tm = t_rows // N_ROW_BLOCKS
    me = _dev_index(*_mesh_pos())

    def as_rows(w_in_like, w_out_like, w_gate_like, w_up_like, w_down_like):
        return (w_in_like[0].T, w_out_like[0], w_gate_like[0].T, w_up_like[0].T, w_down_like[0])

    w_loc = as_rows(w_in, w_out, w_gate, w_up, w_down)
    rows = [w.shape[0] for w in w_loc]
    assert all(r % ADD_CHUNK == 0 for r in rows)
    P_IN, P_OUT, P_GATE, P_UP, P_DOWN = range(N_BIG)

    sm = jnp.zeros((SM_ROWS, LANE), F32)
    sm = sm.at[0:N_META, :].set(meta_tokens)
    sm = sm.at[16:16 + ka, 0:ca_loc].set(conv_a_w[0])
    sm = sm.at[24:24 + kb, 0:cb_loc].set(conv_b_w[0])
    wl, wfull, sm_all = _gather_first(w_loc, sm, [(P_IN, 0, rows[P_IN])])
    meta_full = jnp.transpose(sm_all[:, 0:N_META, :], (1, 0, 2)).reshape(N_META, d)
    wa = jnp.transpose(sm_all[:, 16:16 + ka, 0:ca_loc], (1, 0, 2)).reshape(ka, wa_w)
    wb = jnp.transpose(sm_all[:, 24:24 + kb, 0:cb_loc], (1, 0, 2)).reshape(kb, wa_w)

    later = (P_OUT, P_GATE, P_UP, P_DOWN)
    sems, wl, started, token = _gather_start(wl, wfull, later, rows)
    for p, arr in zip(later, started):
        wfull[p] = arr

    def arrived(p, after, name):
        nonlocal wl
        wl, wfull[p] = _gather_wait(wl, wfull[p], sems[later.index(p)], after, rows[p], name)
        return _forward_comm(wfull[p], rows[p])

    h0, tgt = _prep(x[0], loss_target[0], meta_full, t_rows, x0, [token])
    (xn1, hin), (wfull[P_OUT],) = _in_proj(h0, pre_mix_norm, wfull[P_IN], tm, arrived(P_OUT, h0, "gather_wait_out"))
    (ya, z), (wfull[P_GATE],) = _mix_conv_fwd(hin, wa, wb, conv_b_bias, wa_w, arrived(P_GATE, hin, "gather_wait_gate"))
    y = _mix_ln_fwd(ya, z, ln_b_gain, ln_b_bias, tm)
    (mix, h1, xn2), (wfull[P_UP],) = _out_proj(y, wfull[P_OUT], h0, post_mix_norm, pre_ffn_norm, tm,
                                               arrived(P_UP, y, "gather_wait_up"))
    (a, u, s), (wfull[P_DOWN],) = _gate_up(xn2, wfull[P_GATE], wfull[P_UP], tm, arrived(P_DOWN, xn2, "gather_wait_down"))
    dh2, dff, dg4, lossv = _down_loss(s, wfull[P_DOWN], h1, tgt, post_ffn_norm, tm, x0)

    def to_chips(g, pair, p, tag):
        sums = _pair_sum(g, pair, rows[p], "pair_sum_" + tag)
        return _chip_start(sums, "chip_start_" + tag)

    gwd = _wgrad(s, dff, "wgrad_down")
    (da, du), (pair_d,) = _bwd_down(dff, wfull[P_DOWN], a, u, tm, _pair_comm(gwd, rows[P_DOWN]))
    flight_d = to_chips(gwd, pair_d, P_DOWN, "down")
    gwg = _wgrad(da, xn2, "wgrad_gate", [flight_d[3]])
    gwu = _wgrad(du, xn2, "wgrad_up")
    (dh1, dg3), (pair_g, pair_u) = _bwd_ffn_in(da, du, wfull[P_GATE], wfull[P_UP], h1, dh2, pre_ffn_norm, tm,
                                               _merge_comms([_pair_comm(gwg, rows[P_GATE]), _pair_comm(gwu, rows[P_UP])]))
    flight_g = to_chips(gwg, pair_g, P_GATE, "gate")
    flight_u = to_chips(gwu, pair_u, P_UP, "up")
    dmix, dy, dg2 = _bwd_out_proj(dh1, mix, wfull[P_OUT], post_mix_norm, tm, [flight_g[3], flight_u[3]])
    gwo = _wgrad(y, dmix, "wgrad_out")
    dz, dlg, dlb, dbb = _mix_ln_bwd(z, dy, ln_b_gain, ln_b_bias, tm)
    (dh5, dwa, dwb), (pair_o,) = _mix_conv_bwd(hin, dy, dz, wa, wb, wa_w, _pair_comm(gwo, rows[P_OUT]))
    flight_o = to_chips(gwo, pair_o, P_OUT, "out")
    gwi = _wgrad(dh5, xn1, "wgrad_in", [flight_o[3]])
    (dh0, dg1), (pair_i,) = _bwd_in_proj(dh5, wfull[P_IN], h0, dh1, pre_mix_norm, tm, _pair_comm(gwi, rows[P_IN]))
    flight_i = to_chips(gwi, pair_i, P_IN, "in")
    grad_x = dh0[x0:][None]
    dmeta = dh0[x0 - N_META:x0]
    (ptot,), _ = _reduce_small((dmeta, dg1, dg2, dg3, dg4, dbb, dlg, dlb, lossv, dwa, dwb), d, [flight_i[3]])

    def landed(flight, after, tag):
        sems_p, sums, land, _ = flight
        return _chip_wait(sums, land, sems_p, after, "chip_wait_" + tag)

    part_d = landed(flight_d, ptot, "down")
    part_g = landed(flight_g, ptot, "gate")
    part_u = landed(flight_u, ptot, "up")
    part_o = landed(flight_o, ptot, "out")

    half = d // 2
    loss = (0.5 / d) * jnp.sum(ptot[21, half:])
    g_meta = lax.dynamic_slice(ptot, (0, me * (d // N_DEV)), (N_META, d // N_DEV))
    g_small = [g_meta, ptot[16:17], lax.dynamic_slice(ptot, (24, me * ca_loc), (ka, ca_loc))[None],
               lax.dynamic_slice(ptot, (32, me * cb_loc), (kb, cb_loc))[None],
               ptot[20:21, :half], ptot[20:21, half:], ptot[21:22, :half], ptot[17:18], ptot[18:19], ptot[19:20]]
    w_small = [meta_tokens, pre_mix_norm, conv_a_w, conv_b_w, conv_b_bias, ln_b_gain, ln_b_bias, post_mix_norm,
               pre_ffn_norm, post_ffn_norm]
    m_small = [m_meta_tokens, m_pre_mix_norm, m_conv_a_w, m_conv_b_w, m_conv_b_bias, m_ln_b_gain, m_ln_b_bias,
               m_post_mix_norm, m_pre_ffn_norm, m_post_ffn_norm]
    v_small = [v_meta_tokens, v_pre_mix_norm, v_conv_a_w, v_conv_b_w, v_conv_b_bias, v_ln_b_gain, v_ln_b_bias,
               v_post_mix_norm, v_pre_ffn_norm, v_post_ffn_norm]
    small = _adam_small(g_small, w_small, m_small, v_small)
    n_small = len(w_small)
    d_small, nm_small, nv_small = small[:n_small], small[n_small:2 * n_small], small[2 * n_small:]

    m_loc = as_rows(m_w_in, m_w_out, m_w_gate, m_w_up, m_w_down)
    v_loc = as_rows(v_w_in, v_w_out, v_w_gate, v_w_up, v_w_down)
    full_grads = {P_IN: gwi, P_OUT: gwo, P_GATE: gwg, P_UP: gwu, P_DOWN: gwd}
    pairs = {P_IN: pair_i, P_OUT: pair_o, P_GATE: pair_g, P_UP: pair_u, P_DOWN: pair_d}
    parts = {P_OUT: part_o, P_GATE: part_g, P_UP: part_u, P_DOWN: part_d}
    names = {P_IN: "w_in", P_OUT: "w_out", P_GATE: "w_gate", P_UP: "w_up", P_DOWN: "w_down"}
    bigs = {}
    res = None
    for p in (P_DOWN, P_GATE, P_UP, P_OUT, P_IN):
        if p == P_IN:
            parts[p] = landed(flight_i, res[1], "in")
        res = _adam_big(full_grads[p], pairs[p], parts[p], w_loc[p], m_loc[p], v_loc[p], "adam_" + names[p])
        bigs[names[p]] = [(o.T if p in (P_IN, P_GATE, P_UP) else o)[None] for o in res]

    def ordered(pick_small, pick_big):
        sm_it = iter(range(n_small))
        out = []
        for name in ("s", "s", "w_in", "s", "s", "s", "s", "s", "w_out", "s", "s", "w_gate", "w_up", "w_down", "s"):
            out.append(pick_small(next(sm_it)) if name == "s" else pick_big(name))
        return out

    grads = ordered(lambda i: g_small[i], lambda n: bigs[n][0])
    deltas = ordered(lambda i: d_small[i], lambda n: bigs[n][1])
    new_m = ordered(lambda i: nm_small[i], lambda n: bigs[n][2])
    new_v = ordered(lambda i: nv_small[i], lambda n: bigs[n][3])
    return (loss, grad_x, *grads, *deltas, *new_m, *new_v)
```

```python
import jax
import jax.numpy as jnp
from jax import lax
from jax.experimental import pallas as pl
from jax.experimental.pallas import tpu as pltpu

F32 = jnp.float32
BF16 = jnp.bfloat16
MESH = pl.DeviceIdType.MESH

N_META = 16
N_DEV = 8
RMS_EPS = 1e-6
LN_EPS = 1e-5
ADAM_LR = 0.001
ADAM_B1 = 0.9
ADAM_B2 = 0.999
ADAM_EPS = 1e-08
ADAM_WD = 0.01
ADAM_STEP = 10

LANE = 128
SUBLANE = 8
BF16_ROWS = 16
ROW_ALIGN = 128
N_ROW_BLOCKS = 4
CONV_HALO = 32
CONV_CHUNK = 64
WGRAD_ROWS = 32
N_CHUNK = 512
WGRAD_TILE_MAX = 1408
ADD_CHUNK = 32
ADAM_COL_BLOCKS = 4
V7X_VMEM_BYTES = 64 * 1024 * 1024
VMEM_LIMIT = V7X_VMEM_BYTES - 6 * 1024 * 1024
SMALL_ROWS = 64
SM_ROWS = 56
N_BIG = 5

ANY = pl.BlockSpec(memory_space=pl.ANY)
VMEM = pl.BlockSpec(memory_space=pltpu.VMEM)


def _cparams(n_grid_axes=0):
    sem = ("arbitrary",) * n_grid_axes if n_grid_axes else None
    return pltpu.CompilerParams(dimension_semantics=sem, vmem_limit_bytes=VMEM_LIMIT)


def _mesh_pos():
    return lax.axis_index("x"), lax.axis_index("y"), lax.axis_index("c")


def _dev_index(px, py, pc):
    return 4 * px + 2 * py + pc


def _other_chips(x, y):
    return [(1 - x, y), (x, 1 - y), (1 - x, 1 - y)]


def _full(shape):
    return pl.BlockSpec(shape, lambda *_: (0,) * len(shape))


def _resident(shape):
    return pl.BlockSpec(shape, lambda *_: (0,) * len(shape), pipeline_mode=pl.Buffered(1))


def _dot_nt(a, w):
    return lax.dot_general(a, w, (((1,), (1,)), ((), ())), preferred_element_type=F32)


def _dot_nn(a, w):
    return jnp.dot(a, w, preferred_element_type=F32)


def _chunks(n, c):
    out, o = [], 0
    while o < n:
        out.append((o, min(c, n - o)))
        o += c
    return out


def _rstd(h):
    return lax.rsqrt(jnp.mean(h * h, axis=-1, keepdims=True) + RMS_EPS)


def _rms_bwd(dyh, yh, r):
    return r * (dyh - yh * jnp.mean(dyh * yh, axis=-1, keepdims=True))


def _silu_grad(a, sig):
    return sig * (1.0 + a * (1.0 - sig))


def _acc_rows(ref, val, first):
    s = jnp.sum(val, axis=0, keepdims=True)

    @pl.when(first)
    def _():
        ref[...] = s

    @pl.when(jnp.logical_not(first))
    def _():
        ref[...] += s


def _row_loop(t_rows, chunk, fn, carry=None):
    def step(i, c):
        return fn(pl.multiple_of(i * chunk, chunk), c)

    return lax.fori_loop(0, t_rows // chunk, step, carry)


def _remote(src, dst, send_sem, recv_sem, to):
    return pltpu.make_async_remote_copy(src_ref=src, dst_ref=dst, send_sem=send_sem, recv_sem=recv_sem,
                                        device_id=to, device_id_type=MESH)


class _SplitRemote:
    def __init__(self, src, dst, send_sem, recv_sem, to, rows, n_chunks):
        units = rows // BF16_ROWS
        n_chunks = max(1, min(n_chunks, units))
        sizes = [(units // n_chunks + (i < units % n_chunks)) * BF16_ROWS for i in range(n_chunks)]
        self.whole = _remote(src, dst, send_sem, recv_sem, to)
        self.parts, o = [], 0
        for n in sizes:
            self.parts.append(_remote(src.at[pl.ds(o, n), :], dst.at[pl.ds(o, n), :], send_sem, recv_sem, to))
            o += n

    def start(self):
        for cp in self.parts:
            cp.start()

    def wait_recv(self):
        self.whole.wait_recv()

    def wait_send(self):
        self.whole.wait_send()


class _Comm:
    def __init__(self, inputs, out_shapes, aliases, scratch, start, finish):
        self.inputs, self.out_shapes, self.aliases, self.scratch = list(inputs), list(out_shapes), dict(aliases), list(scratch)
        self.start, self.finish = start, finish


def _merge_comms(comms):
    inputs, out_shapes, aliases, scratch, spans = [], [], {}, [], []
    for cm in comms:
        spans.append((len(inputs), len(out_shapes), len(scratch), cm))
        aliases.update({len(inputs) + k: len(out_shapes) + v for k, v in cm.aliases.items()})
        inputs += cm.inputs
        out_shapes += cm.out_shapes
        scratch += cm.scratch

    def run(which):
        def fn(ins, outs, scr):
            for i0, o0, s0, cm in spans:
                getattr(cm, which)(ins[i0:i0 + len(cm.inputs)], outs[o0:o0 + len(cm.out_shapes)], scr[s0:s0 + len(cm.scratch)])
        return fn

    return _Comm(inputs, out_shapes, aliases, scratch, run("start"), run("finish"))


def _host_call(body, *, grid, in_specs, out_specs, out_shape, args, name, scratch_shapes=(), comm=None, after=()):
    if comm is None:
        comm = _Comm([], [], {}, [], lambda *_: None, lambda *_: None)
    n_in, n_out, n_scr = len(args), len(out_shape), len(scratch_shapes)
    c_in, c_out = len(comm.inputs), len(comm.out_shapes)
    n_after = len(after)

    def hosted(*refs):
        ins, c_ins = refs[:n_in], refs[n_in:n_in + c_in]
        o0 = n_in + c_in + n_after
        outs, c_outs = refs[o0:o0 + n_out], refs[o0 + n_out:o0 + n_out + c_out]
        s0 = o0 + n_out + c_out
        scr, c_scr = refs[s0:s0 + n_scr], refs[s0 + n_scr:]
        if not grid:
            comm.start(c_ins, c_outs, c_scr)
            body(*ins, *outs, *scr)
            comm.finish(c_ins, c_outs, c_scr)
            return
        first = last = None
        for a, n in enumerate(grid):
            f, l = pl.program_id(a) == 0, pl.program_id(a) == n - 1
            first = f if first is None else jnp.logical_and(first, f)
            last = l if last is None else jnp.logical_and(last, l)

        @pl.when(first)
        def _():
            comm.start(c_ins, c_outs, c_scr)

        body(*ins, *outs, *scr)

        @pl.when(last)
        def _():
            comm.finish(c_ins, c_outs, c_scr)

    res = pl.pallas_call(
        hosted, grid=grid, in_specs=list(in_specs) + [ANY] * (c_in + n_after), out_specs=list(out_specs) + [ANY] * c_out,
        out_shape=list(out_shape) + comm.out_shapes, scratch_shapes=list(scratch_shapes) + comm.scratch,
        input_output_aliases={n_in + k: n_out + v for k, v in comm.aliases.items()},
        name=name, compiler_params=_cparams(len(grid)))(*args, *comm.inputs, *after)
    return list(res[:n_out]), list(res[n_out:])


GATHER_SEMS = 10
D2D_CHUNKS = 8


class _Gather:
    def __init__(self, jobs, rows, lo, src_ref, dests, send_sems, recv_sems):
        x, y, c = _mesh_pos()
        me, sib = (x, y, c), (x, y, 1 - c)
        nx, ny, dg = (1 - x, y, c), (x, 1 - y, c), (1 - x, 1 - y, c)
        self.relayed, self.direct, self.relay, self.to_sib, self.sib_fwd = [], [], [], [], []
        for n, (p, r0, nr) in enumerate(jobs):
            assert nr % (2 * BF16_ROWS) == 0
            half = nr // 2

            def rows_of(dev, h, p=p, r0=r0, nr=nr, half=half):
                off, cnt = (r0, nr) if h is None else (r0 + h * half, half)
                return dests[p].at[pl.ds(pl.multiple_of(_dev_index(*dev) * rows[p] + off, BF16_ROWS), cnt), :]

            def mine(h, p=p, r0=r0, nr=nr, half=half):
                off, cnt = (r0, nr) if h is None else (r0 + h * half, half)
                return src_ref.at[pl.ds(lo[p] + off, cnt), :]

            sem = lambda k, n=n: (send_sems.at[GATHER_SEMS * n + k], recv_sems.at[GATHER_SEMS * n + k])
            self.relayed.append([_remote(mine(0), rows_of(me, 0), *sem(0), nx), _remote(mine(1), rows_of(me, 1), *sem(3), ny)])
            self.direct.append([_remote(mine(1), rows_of(me, 1), *sem(1), nx), _remote(mine(0), rows_of(me, 0), *sem(2), ny)])
            self.relay.append([_remote(rows_of(nx, 0), rows_of(nx, 0), *sem(4), ny), _remote(rows_of(ny, 1), rows_of(ny, 1), *sem(5), nx)])
            self.to_sib.append(_SplitRemote(mine(None), rows_of(me, None), *sem(6), sib, nr, D2D_CHUNKS))
            self.sib_fwd.append([_SplitRemote(rows_of(dev, None), rows_of(dev, None), *sem(7 + i), sib, nr, D2D_CHUNKS)
                                 for i, dev in enumerate((nx, ny, dg))])

    def start(self):
        for group in (self.relayed, self.direct):
            for cps in group:
                for cp in cps:
                    cp.start()
        for cp in self.to_sib:
            cp.start()

    def mid(self):
        for first, relay in zip(self.relayed, self.relay):
            for arrived, onward in zip(first, relay):
                arrived.wait_recv()
                onward.start()

    def finish(self):
        for direct, relay, fwd in zip(self.direct, self.relay, self.sib_fwd):
            for k in range(2):
                direct[k].wait_recv()
                fwd[k].start()
            for cp in relay:
                cp.wait_recv()
            fwd[2].start()
        for n in range(len(self.to_sib)):
            self.to_sib[n].wait_recv()
            for cp in self.sib_fwd[n]:
                cp.wait_recv()
            for cp in self.relayed[n] + self.direct[n] + self.relay[n] + [self.to_sib[n]] + self.sib_fwd[n]:
                cp.wait_send()


HBM = pl.BlockSpec(memory_space=pltpu.HBM)
SEM = pl.BlockSpec(memory_space=pltpu.SEMAPHORE)
FLOWS = pltpu.SideEffectType.DATAFLOW_SIDE_EFFECTING


def _in_hbm(a):
    return pltpu.with_memory_space_constraint(a, pltpu.HBM)


def _gather_start(wl, dests, ps, rows):
    lo = [sum(rows[:p]) for p in range(N_BIG)]
    n = len(ps)

    def body(*refs):
        wl_ref, dest_refs = refs[0], refs[1:1 + n]
        sends, recvs = refs[1 + n:1 + 2 * n], refs[1 + 2 * n:1 + 3 * n]
        token = refs[-1]
        x, y, c = _mesh_pos()
        jme = _dev_index(x, y, c)
        for i, p in enumerate(ps):
            mine = dest_refs[i].at[pl.ds(pl.multiple_of(jme * rows[p], BF16_ROWS), rows[p]), :]
            for chip in _other_chips(x, y):
                _remote(wl_ref.at[pl.ds(lo[p], rows[p]), :], mine, sends[i], recvs[i], (*chip, c)).start()
        token[...] = jnp.zeros_like(token)

    thru = [pltpu.HBM(wl.shape, wl.dtype)] + [pltpu.HBM(dests[p].shape, BF16) for p in ps]
    res = pl.pallas_call(
        body, name="gather_start",
        out_shape=tuple([pltpu.SemaphoreType.DMA(())] * (2 * n) + thru + [jax.ShapeDtypeStruct((SUBLANE, LANE), F32)]),
        in_specs=[HBM] * (1 + n), out_specs=tuple([SEM] * (2 * n) + [HBM] * (1 + n) + [VMEM]),
        input_output_aliases={i: 2 * n + i for i in range(1 + n)},
        compiler_params=pltpu.CompilerParams(has_side_effects=FLOWS))(_in_hbm(wl), *[_in_hbm(dests[p]) for p in ps])
    sems = [(res[i], res[n + i]) for i in range(n)]
    return sems, res[2 * n], list(res[2 * n + 1:3 * n + 1]), res[-1]


def _gather_wait(wl, dest, sems, after, r, name):
    def body(wl_ref, dest_ref, send_sem, recv_sem, after_ref, wl_out, dest_out):
        x, y, c = _mesh_pos()
        three = dest_ref.at[pl.ds(0, 3 * r), :]
        cp = _remote(three, three, send_sem, recv_sem, (x, y, 1 - c))
        cp.wait_send()
        cp.wait_recv()

    res = pl.pallas_call(
        body, name=name, out_shape=(pltpu.HBM(wl.shape, wl.dtype), pltpu.HBM(dest.shape, dest.dtype)),
        in_specs=[HBM, HBM, SEM, SEM, ANY], out_specs=(HBM, HBM), input_output_aliases={0: 0, 1: 1},
        compiler_params=pltpu.CompilerParams(has_side_effects=FLOWS))(wl, dest, sems[0], sems[1], after)
    return res[0], res[1]


def _forward_comm(dest, r):
    def descs(ins, outs, scr):
        x, y, c = _mesh_pos()
        cps = []
        for k, chip in enumerate(_other_chips(x, y)):
            blk = outs[0].at[pl.ds(pl.multiple_of(_dev_index(*chip, c) * r, BF16_ROWS), r), :]
            cps.append(_SplitRemote(blk, blk, scr[0].at[k], scr[1].at[k], (x, y, 1 - c), r, D2D_CHUNKS))
        return cps

    def start(ins, outs, scr):
        for cp in descs(ins, outs, scr):
            cp.start()

    def finish(ins, outs, scr):
        cps = descs(ins, outs, scr)
        for cp in cps:
            cp.wait_recv()
        for cp in cps:
            cp.wait_send()

    return _Comm([dest], [jax.ShapeDtypeStruct(dest.shape, dest.dtype)], {0: 0},
                 [pltpu.SemaphoreType.DMA((3,)), pltpu.SemaphoreType.DMA((3,))], start, finish)


def _pair_comm(g, r):
    d = g.shape[1]

    def descs(ins, outs, scr):
        x, y, c = _mesh_pos()
        chips = [(x, y)] + _other_chips(x, y)
        return [_SplitRemote(ins[0].at[pl.ds(pl.multiple_of(_dev_index(*chip, 1 - c) * r, BF16_ROWS), r), :], outs[0].at[k],
                             scr[0].at[k], scr[1].at[k], (x, y, 1 - c), r, D2D_CHUNKS) for k, chip in enumerate(chips)]

    def start(ins, outs, scr):
        for cp in descs(ins, outs, scr):
            cp.start()

    def finish(ins, outs, scr):
        cps = descs(ins, outs, scr)
        for cp in cps:
            cp.wait_recv()
        for cp in cps:
            cp.wait_send()

    comm = _Comm([g], [jax.ShapeDtypeStruct((4, r, d), BF16)], {},
                 [pltpu.SemaphoreType.DMA((4,)), pltpu.SemaphoreType.DMA((4,))], start, finish)
    return comm


def _pair_sum(g, pair, r, name, after=()):
    d = g.shape[1]
    x, y, c = _mesh_pos()
    idx = jnp.stack([_dev_index(*chip, c) for chip in _other_chips(x, y)]).astype(jnp.int32)

    def body(idx_ref, g_ref, p_ref, *rest):
        rest[len(after)][...] = (g_ref[...].astype(F32) + p_ref[...].astype(F32)).astype(BF16)

    grid_spec = pltpu.PrefetchScalarGridSpec(
        num_scalar_prefetch=1, grid=(3,),
        in_specs=[pl.BlockSpec((r, d), lambda k, idx_ref: (idx_ref[k], 0)),
                  pl.BlockSpec((None, r, d), lambda k, idx_ref: (1 + k, 0, 0))] + [ANY] * len(after),
        out_specs=pl.BlockSpec((None, r, d), lambda k, idx_ref: (k, 0, 0)))
    return pl.pallas_call(body, out_shape=jax.ShapeDtypeStruct((3, r, d), BF16), grid_spec=grid_spec, name=name,
                          compiler_params=_cparams(1))(idx, g, pair, *after)


def _chip_start(sums, name):
    n = len(sums)

    def body(*refs):
        srcs, lands = refs[:n], refs[n:2 * n]
        sends, recvs = refs[2 * n:3 * n], refs[3 * n:4 * n]
        x, y, c = _mesh_pos()
        for i in range(n):
            for k, chip in enumerate(_other_chips(x, y)):
                _remote(srcs[i].at[k], lands[i].at[k], sends[i], recvs[i], (*chip, c)).start()
        refs[-1][...] = jnp.zeros_like(refs[-1])

    zones = [pltpu.HBM(s.shape, s.dtype) for s in sums]
    res = pl.pallas_call(
        body, name=name,
        out_shape=tuple([pltpu.SemaphoreType.DMA(())] * (2 * n) + zones + zones + [jax.ShapeDtypeStruct((SUBLANE, LANE), F32)]),
        in_specs=[HBM] * (2 * n), out_specs=tuple([SEM] * (2 * n) + [HBM] * (2 * n) + [VMEM]),
        input_output_aliases={i: 2 * n + i for i in range(2 * n)},
        compiler_params=pltpu.CompilerParams(has_side_effects=FLOWS))(
            *[_in_hbm(s) for s in sums], *[_in_hbm(lax.empty(s.shape, s.dtype)) for s in sums])
    flights = [((res[i], res[n + i]), res[2 * n + i], res[3 * n + i]) for i in range(n)]
    return flights, res[-1]


def _chip_wait(sums, land, sems, after, name):
    def body(sums_ref, land_ref, send_sem, recv_sem, after_ref, sums_out, land_out):
        x, y, c = _mesh_pos()
        cp = _remote(sums_ref, land_ref, send_sem, recv_sem, (x, y, 1 - c))
        cp.wait_send()
        cp.wait_recv()

    res = pl.pallas_call(
        body, name=name, out_shape=(pltpu.HBM(sums.shape, sums.dtype), pltpu.HBM(land.shape, land.dtype)),
        in_specs=[HBM, HBM, SEM, SEM, ANY], out_specs=(HBM, HBM), input_output_aliases={0: 0, 1: 1},
        compiler_params=pltpu.CompilerParams(has_side_effects=FLOWS))(sums, land, sems[0], sems[1], after)
    return res[1]


def _copy_through(src_ref, dst_ref, dst_row0, n_rows, bufs, sems_in, sems_out):
    n = n_rows // ROW_ALIGN
    ins = [pltpu.make_async_copy(src_ref.at[pl.ds(k * ROW_ALIGN, ROW_ALIGN), :], bufs.at[k % 2], sems_in.at[k % 2]) for k in range(n)]
    outs = [pltpu.make_async_copy(bufs.at[k % 2], dst_ref.at[pl.ds(dst_row0 + k * ROW_ALIGN, ROW_ALIGN), :], sems_out.at[k % 2])
            for k in range(n)]
    ins[0].start()
    for k in range(n):
        ins[k].wait()
        if k + 1 < n:
            if k >= 1:
                outs[k - 1].wait()
            ins[k + 1].start()
        outs[k].start()
    for k in range(max(0, n - 2), n):
        outs[k].wait()


def _gather_first(shards, sm, jobs, x2, tgt2, t_rows, x0):
    d = shards[0].shape[1]
    rows = [w.shape[0] for w in shards]
    lo = [sum(rows[:p]) for p in range(N_BIG)]
    n_sems = GATHER_SEMS * len(jobs)
    seq = x2.shape[0]
    assert x0 == ROW_ALIGN and seq % ROW_ALIGN == 0 and d == N_DEV * LANE

    def body(s0, s1, s2, s3, s4, sm_ref, x_ref, tgt_ref, wl_ref, o0, o1, o2, o3, o4, sa_ref, h0_ref, tp_ref,
             wl_v, bufs, sa_v, send_sems, recv_sems, ssend, srecv, local_sems, sems_in, sems_out):
        dests = (o0, o1, o2, o3, o4)
        x, y, c = _mesh_pos()
        me = (x, y, c)
        jme = _dev_index(*me)
        for p, ref in enumerate((s0, s1, s2, s3, s4)):
            wl_v[pl.ds(lo[p], rows[p]), :] = ref[...].astype(BF16)
        gather = _Gather(jobs, rows, lo, wl_v, dict(enumerate(dests)), send_sems, recv_sems)
        gather.start()
        peers = [(x, y, 1 - c)] + [(*chip, pc) for pc in (c, 1 - c) for chip in _other_chips(x, y)]
        smalls = [_remote(sm_ref, sa_ref.at[jme], ssend.at[k], srecv.at[k], to) for k, to in enumerate(peers)]
        for cp in smalls:
            cp.start()
        mine = [pltpu.make_async_copy(wl_v.at[pl.ds(lo[p], rows[p]), :],
                                      dests[p].at[pl.ds(pl.multiple_of(jme * rows[p], BF16_ROWS), rows[p]), :], local_sems.at[p])
                for p in range(N_BIG)]
        mine.append(pltpu.make_async_copy(wl_v, wl_ref, local_sems.at[N_BIG]))
        mine.append(pltpu.make_async_copy(sm_ref, sa_ref.at[jme], local_sems.at[N_BIG + 1]))
        for cp in mine:
            cp.start()
        later = [p for p in range(N_BIG) if p not in {j[0] for j in jobs}]
        own = [_SplitRemote(wl_v.at[pl.ds(lo[p], rows[p]), :],
                            dests[p].at[pl.ds(pl.multiple_of(jme * rows[p], BF16_ROWS), rows[p]), :],
                            ssend.at[7 + i], srecv.at[7 + i], (x, y, 1 - c), rows[p], D2D_CHUNKS) for i, p in enumerate(later)]
        for cp in own:
            cp.start()
        _copy_through(x_ref, h0_ref, x0, seq, bufs.at[0], sems_in, sems_out)
        gather.mid()
        _copy_through(tgt_ref, tp_ref, x0, seq, bufs.at[1], sems_in, sems_out)
        for cp in smalls + own:
            cp.wait_recv()
        mine[-1].wait()
        to_v = pltpu.make_async_copy(sa_ref, sa_v, local_sems.at[N_BIG + 1])
        to_v.start()
        to_v.wait()
        head, zeros = bufs.at[0, 0], bufs.at[1, 0]
        head[...] = jnp.zeros_like(head)
        zeros[...] = jnp.zeros_like(zeros)
        for j in range(N_DEV):
            head[pl.ds(x0 - N_META, N_META), pl.ds(j * LANE, LANE)] = sa_v[j, pl.ds(0, N_META), :]
        heads = [pltpu.make_async_copy(head, h0_ref.at[pl.ds(0, x0), :], sems_out.at[0]),
                 pltpu.make_async_copy(zeros, tp_ref.at[pl.ds(0, x0), :], sems_out.at[1])]
        for cp in heads:
            cp.start()
        gather.finish()
        for cp in smalls + own:
            cp.wait_send()
        for cp in mine[:-1] + heads:
            cp.wait()

    out_shape = [jax.ShapeDtypeStruct((sum(rows), d), BF16)]
    out_shape += [jax.ShapeDtypeStruct((N_DEV * r, d), BF16) for r in rows]
    out_shape.append(jax.ShapeDtypeStruct((N_DEV,) + sm.shape, F32))
    out_shape += [jax.ShapeDtypeStruct((t_rows, d), F32)] * 2
    res = pl.pallas_call(
        body, out_shape=out_shape, in_specs=[VMEM] * 6 + [ANY] * 2, out_specs=[ANY] * 9,
        scratch_shapes=[pltpu.VMEM((sum(rows), d), BF16), pltpu.VMEM((2, 2, ROW_ALIGN, d), F32), pltpu.VMEM((N_DEV,) + sm.shape, F32),
                        pltpu.SemaphoreType.DMA((n_sems,)), pltpu.SemaphoreType.DMA((n_sems,)),
                        pltpu.SemaphoreType.DMA((7 + N_BIG,)), pltpu.SemaphoreType.DMA((7 + N_BIG,)),
                        pltpu.SemaphoreType.DMA((N_BIG + 2,)), pltpu.SemaphoreType.DMA((2,)), pltpu.SemaphoreType.DMA((2,))],
        name="gather_first", compiler_params=_cparams())(*shards, sm, x2, tgt2)
    return res[0], list(res[1:1 + N_BIG]), res[1 + N_BIG], res[2 + N_BIG], res[3 + N_BIG]


def _in_proj(h0, g1, win_t, tm, comm):
    t_rows, d = h0.shape
    e = win_t.shape[0]

    def body(h_ref, g_ref, w_ref, xn_ref, hin_ref):
        h = h_ref[...]
        xn = ((h * _rstd(h)) * g_ref[...]).astype(BF16)
        xn_ref[...] = xn
        for o, n in _chunks(e, N_CHUNK):
            hin_ref[:, pl.ds(o, n)] = _dot_nt(xn, w_ref[pl.ds(o, n), :])

    return _host_call(
        body, grid=(t_rows // tm,),
        in_specs=[pl.BlockSpec((tm, d), lambda i: (i, 0)), _full((1, d)), _resident((e, d))],
        out_specs=[pl.BlockSpec((tm, d), lambda i: (i, 0)), pl.BlockSpec((tm, e), lambda i: (i, 0))],
        out_shape=[jax.ShapeDtypeStruct((t_rows, d), BF16), jax.ShapeDtypeStruct((t_rows, e), F32)],
        args=(h0, g1, win_t), name="in_proj", comm=comm)


def _tap_slot(off):
    return off % SUBLANE, (off // SUBLANE) * SUBLANE


def _fill_shifted(sh_ref, base_ref, residues, n_rows):
    for r in residues:
        sh_ref[r] = base_ref[pl.ds(r, n_rows), :]


def _mix_conv_fwd(hin, wa, wb, bb, wa_w, comm):
    t_rows = hin.shape[0]
    nt = wa_w // LANE
    ka, kb = wa.shape[0], wb.shape[0]
    nr = CONV_HALO + t_rows

    def body(bg_ref, cg_ref, ha_ref, val_ref, gt_ref, wa_ref, wb_ref, bb_ref, ya_ref, z_ref, base, sh):
        base[pl.ds(0, CONV_HALO), :] = jnp.zeros((CONV_HALO, LANE), F32)
        base[pl.ds(nr, SUBLANE), :] = jnp.zeros((SUBLANE, LANE), F32)

        def conv(w_ref, k_taps, b, n):
            acc = None
            for k in range(k_taps):
                r, q = _tap_slot(CONV_HALO - (k_taps - 1) + k)
                term = w_ref[pl.ds(k, 1), :] * sh[r, pl.ds(b + q, n), :]
                acc = term if acc is None else acc + term
            return acc

        def fill_a(b, c):
            base[pl.ds(CONV_HALO + b, CONV_CHUNK), :] = cg_ref[pl.ds(b, CONV_CHUNK), :] * ha_ref[pl.ds(b, CONV_CHUNK), :]
            return c

        _row_loop(t_rows, CONV_CHUNK, fill_a)
        _fill_shifted(sh, base, sorted({_tap_slot(CONV_HALO - (ka - 1) + k)[0] for k in range(ka)}), nr)

        def out_a(b, c):
            ya_ref[pl.ds(b, CONV_CHUNK), :] = (bg_ref[pl.ds(b, CONV_CHUNK), :] * conv(wa_ref, ka, b, CONV_CHUNK)).astype(BF16)
            return c

        _row_loop(t_rows, CONV_CHUNK, out_a)

        def fill_b(b, c):
            base[pl.ds(CONV_HALO + b, CONV_CHUNK), :] = (val_ref[pl.ds(b, CONV_CHUNK), :]
                                                          * jax.nn.sigmoid(gt_ref[pl.ds(b, CONV_CHUNK), :]))
            return c

        _row_loop(t_rows, CONV_CHUNK, fill_b)
        _fill_shifted(sh, base, range(SUBLANE), nr)

        def out_b(b, c):
            z_ref[pl.ds(b, CONV_CHUNK), :] = conv(wb_ref, kb, b, CONV_CHUNK) + bb_ref[...]
            return c

        _row_loop(t_rows, CONV_CHUNK, out_b)

    def col(g):
        return pl.BlockSpec((t_rows, LANE), lambda i, g=g: (0, g * nt + i))

    tile = lambda rows: pl.BlockSpec((rows, LANE), lambda i: (0, i))
    return _host_call(
        body, grid=(nt,),
        in_specs=[col(0), col(1), col(2), col(3), col(4), tile(ka), tile(kb), tile(1)],
        out_specs=[tile(t_rows), tile(t_rows)],
        out_shape=[jax.ShapeDtypeStruct((t_rows, wa_w), BF16), jax.ShapeDtypeStruct((t_rows, wa_w), F32)],
        scratch_shapes=[pltpu.VMEM((nr + SUBLANE, LANE), F32), pltpu.VMEM((SUBLANE, nr, LANE), F32)],
        args=(hin, hin, hin, hin, hin, wa, wb, bb), name="mix_conv_fwd", comm=comm)


def _ln_parts(z, lg, lb):
    mu = jnp.mean(z, axis=-1, keepdims=True)
    zc = z - mu
    rstd = lax.rsqrt(jnp.mean(zc * zc, axis=-1, keepdims=True) + LN_EPS)
    zh = zc * rstd
    return zh, rstd, zh * lg + lb


def _mix_ln_fwd(ya, z, lg, lb, tm):
    t_rows, w = z.shape

    def body(ya_ref, z_ref, lg_ref, lb_ref, y_ref):
        _, _, ln = _ln_parts(z_ref[...], lg_ref[...], lb_ref[...])
        y_ref[:, pl.ds(0, w)] = ya_ref[...]
        y_ref[:, pl.ds(w, w)] = (ln * jax.nn.sigmoid(ln)).astype(BF16)

    blk = pl.BlockSpec((tm, w), lambda i: (i, 0))
    res, _ = _host_call(body, grid=(t_rows // tm,), in_specs=[blk, blk, _full((1, w)), _full((1, w))],
                        out_specs=[pl.BlockSpec((tm, 2 * w), lambda i: (i, 0))],
                        out_shape=[jax.ShapeDtypeStruct((t_rows, 2 * w), BF16)], args=(ya, z, lg, lb), name="mix_ln_fwd")
    return res[0]


def _out_proj(y, w_out, h0, g2, g3, tm, comm):
    t_rows, d = h0.shape

    def body(y_ref, w_ref, h0_ref, g2_ref, g3_ref, mix_ref, h1_ref, xn2_ref):
        mix = _dot_nn(y_ref[...], w_ref[...])
        mix_ref[...] = mix
        h1 = h0_ref[...] + (mix * _rstd(mix)) * g2_ref[...]
        h1_ref[...] = h1
        xn2_ref[...] = ((h1 * _rstd(h1)) * g3_ref[...]).astype(BF16)

    blk = pl.BlockSpec((tm, d), lambda i: (i, 0))
    return _host_call(
        body, grid=(t_rows // tm,), in_specs=[blk, _resident(w_out.shape), blk, _full((1, d)), _full((1, d))],
        out_specs=[blk, blk, blk],
        out_shape=[jax.ShapeDtypeStruct((t_rows, d), F32), jax.ShapeDtypeStruct((t_rows, d), F32),
                   jax.ShapeDtypeStruct((t_rows, d), BF16)],
        args=(y, w_out, h0, g2, g3), name="out_proj", comm=comm)


def _gate_up(xn2, wg_t, wu_t, tm, comm):
    t_rows, d = xn2.shape
    f = wg_t.shape[0]

    def body(x_ref, wg_ref, wu_ref, a_ref, u_ref, s_ref):
        xn = x_ref[...]
        for o, n in _chunks(f, N_CHUNK):
            a = _dot_nt(xn, wg_ref[pl.ds(o, n), :])
            u = _dot_nt(xn, wu_ref[pl.ds(o, n), :])
            a_ref[:, pl.ds(o, n)] = a.astype(BF16)
            u_ref[:, pl.ds(o, n)] = u.astype(BF16)
            s_ref[:, pl.ds(o, n)] = ((a * jax.nn.sigmoid(a)) * u).astype(BF16)

    blk = pl.BlockSpec((tm, f), lambda i: (i, 0))
    return _host_call(
        body, grid=(t_rows // tm,),
        in_specs=[pl.BlockSpec((tm, d), lambda i: (i, 0)), _resident((f, d)), _resident((f, d))],
        out_specs=[blk, blk, blk], out_shape=[jax.ShapeDtypeStruct((t_rows, f), BF16)] * 3,
        args=(xn2, wg_t, wu_t), name="gate_up", comm=comm)


def _down_loss(s, wd, h1, tgt, g4, tm, x0):
    t_rows, d = h1.shape
    f = wd.shape[0]

    def body(s_ref, w_ref, h1_ref, tgt_ref, g4_ref, dh2_ref, dff_ref, dg4_ref, loss_ref):
        i = pl.program_id(0)
        ff = _dot_nn(s_ref[...], w_ref[...])
        r4 = _rstd(ff)
        fh = ff * r4
        g4 = g4_ref[...]
        h2 = h1_ref[...] + fh * g4
        row = i * tm + lax.broadcasted_iota(jnp.int32, (tm, 1), 0)
        diff = jnp.where(row >= x0, h2 - tgt_ref[...], 0.0)
        dh2 = diff / d
        dh2_ref[...] = dh2
        dff_ref[...] = _rms_bwd(dh2 * g4, fh, r4).astype(BF16)
        _acc_rows(dg4_ref, dh2 * fh, i == 0)
        _acc_rows(loss_ref, diff * diff, i == 0)

    blk = pl.BlockSpec((tm, d), lambda i: (i, 0))
    res, _ = _host_call(
        body, grid=(t_rows // tm,),
        in_specs=[pl.BlockSpec((tm, f), lambda i: (i, 0)), _resident((f, d)), blk, blk, _full((1, d))],
        out_specs=[blk, blk, _full((1, d)), _full((1, d))],
        out_shape=[jax.ShapeDtypeStruct((t_rows, d), F32), jax.ShapeDtypeStruct((t_rows, d), BF16),
                   jax.ShapeDtypeStruct((1, d), F32), jax.ShapeDtypeStruct((1, d), F32)],
        args=(s, wd, h1, tgt, g4), name="down_loss")
    return res


def _bwd_down(dff, wd, a, u, tm, comm):
    t_rows, d = dff.shape
    f = wd.shape[0]

    def body(dff_ref, w_ref, a_ref, u_ref, da_ref, du_ref):
        dff_v = dff_ref[...]
        for o, n in _chunks(f, N_CHUNK):
            ds = _dot_nt(dff_v, w_ref[pl.ds(o, n), :])
            av = a_ref[:, pl.ds(o, n)].astype(F32)
            uv = u_ref[:, pl.ds(o, n)].astype(F32)
            sig = jax.nn.sigmoid(av)
            da_ref[:, pl.ds(o, n)] = (ds * uv * _silu_grad(av, sig)).astype(BF16)
            du_ref[:, pl.ds(o, n)] = (ds * (av * sig)).astype(BF16)

    blk = pl.BlockSpec((tm, f), lambda i: (i, 0))
    return _host_call(
        body, grid=(t_rows // tm,),
        in_specs=[pl.BlockSpec((tm, d), lambda i: (i, 0)), _resident((f, d)), blk, blk],
        out_specs=[blk, blk], out_shape=[jax.ShapeDtypeStruct((t_rows, f), BF16)] * 2,
        args=(dff, wd, a, u), name="bwd_down", comm=comm)


def _wgrad(a, b, name, after=()):
    d = b.shape[1]
    t_rows = b.shape[0]
    stacked = a.ndim == 3
    n = a.shape[-1]
    groups = a.shape[0] if stacked else 1
    steps = 1 if stacked else 2
    tile = max(t for t in range(LANE, min(n // steps, WGRAD_TILE_MAX) + 1, LANE) if n % t == 0)
    tiles = n // tile

    def body(a_ref, b_ref, o_ref):
        o_ref[...] = lax.dot_general(a_ref[...], b_ref[...], (((0,), (0,)), ((), ())),
                                     preferred_element_type=F32).astype(BF16)

    if stacked:
        a_spec = pl.BlockSpec((None, t_rows, tile), lambda g, i: (g, 0, i))
    else:
        a_spec = pl.BlockSpec((t_rows, tile), lambda g, i: (0, i))
    res, _ = _host_call(
        body, grid=(groups, tiles), in_specs=[a_spec, _resident((t_rows, d))],
        out_specs=[pl.BlockSpec((tile, d), lambda g, i: (g * tiles + i, 0))],
        out_shape=[jax.ShapeDtypeStruct((groups * n, d), BF16)], args=(a, b), name=name, after=after)
    return res[0]


def _bwd_ffn_in(da, du, wg_t, wu_t, h1, dh2, g3, tm, comm):
    t_rows, d = h1.shape
    f = wg_t.shape[0]

    def body(da_ref, du_ref, wg_ref, wu_ref, h1_ref, dh2_ref, g3_ref, dh1_ref, dg3_ref):
        dxn2 = _dot_nn(da_ref[...], wg_ref[...]) + _dot_nn(du_ref[...], wu_ref[...])
        h1 = h1_ref[...]
        r3 = _rstd(h1)
        h1h = h1 * r3
        _acc_rows(dg3_ref, dxn2 * h1h, pl.program_id(0) == 0)
        dh1_ref[...] = dh2_ref[...] + _rms_bwd(dxn2 * g3_ref[...], h1h, r3)

    blk = pl.BlockSpec((tm, d), lambda i: (i, 0))
    blkf = pl.BlockSpec((tm, f), lambda i: (i, 0))
    return _host_call(
        body, grid=(t_rows // tm,),
        in_specs=[blkf, blkf, _resident((f, d)), _resident((f, d)), blk, blk, _full((1, d))],
        out_specs=[blk, _full((1, d))],
        out_shape=[jax.ShapeDtypeStruct((t_rows, d), F32), jax.ShapeDtypeStruct((1, d), F32)],
        args=(da, du, wg_t, wu_t, h1, dh2, g3), name="bwd_ffn_in", comm=comm)


def _bwd_out_proj(dh1, mix, w_out, g2, tm, after):
    t_rows, d = dh1.shape

    def body(dh1_ref, mix_ref, w_ref, g2_ref, dmix_ref, dy_ref, dg2_ref):
        mix = mix_ref[...]
        r2 = _rstd(mix)
        mh = mix * r2
        dh1 = dh1_ref[...]
        _acc_rows(dg2_ref, dh1 * mh, pl.program_id(0) == 0)
        dmix = _rms_bwd(dh1 * g2_ref[...], mh, r2).astype(BF16)
        dmix_ref[...] = dmix
        dy_ref[...] = _dot_nt(dmix, w_ref[...])

    blk = pl.BlockSpec((tm, d), lambda i: (i, 0))
    res, _ = _host_call(
        body, grid=(t_rows // tm,), in_specs=[blk, blk, _resident(w_out.shape), _full((1, d))],
        out_specs=[blk, blk, _full((1, d))],
        out_shape=[jax.ShapeDtypeStruct((t_rows, d), BF16), jax.ShapeDtypeStruct((t_rows, d), F32),
                   jax.ShapeDtypeStruct((1, d), F32)],
        args=(dh1, mix, w_out, g2), name="bwd_out_proj", after=after)
    return res


def _mix_ln_bwd(z, dy, lg, lb, tm):
    t_rows, w = z.shape

    def body(z_ref, dyb_ref, lg_ref, lb_ref, dz_ref, dlg_ref, dlb_ref, dbb_ref):
        first = pl.program_id(0) == 0
        lg = lg_ref[...]
        zh, rstd, ln = _ln_parts(z_ref[...], lg, lb_ref[...])
        dln = dyb_ref[...] * _silu_grad(ln, jax.nn.sigmoid(ln))
        _acc_rows(dlg_ref, dln * zh, first)
        _acc_rows(dlb_ref, dln, first)
        dzh = dln * lg
        dz = rstd * (dzh - jnp.mean(dzh, axis=-1, keepdims=True) - zh * jnp.mean(dzh * zh, axis=-1, keepdims=True))
        dz_ref[...] = dz
        _acc_rows(dbb_ref, dz, first)

    blk = pl.BlockSpec((tm, w), lambda i: (i, 0))
    vec = _full((1, w))
    res, _ = _host_call(
        body, grid=(t_rows // tm,), in_specs=[blk, pl.BlockSpec((tm, w), lambda i: (i, 1)), vec, vec],
        out_specs=[blk, vec, vec, vec],
        out_shape=[jax.ShapeDtypeStruct((t_rows, w), F32)] + [jax.ShapeDtypeStruct((1, w), F32)] * 3,
        args=(z, dy, lg, lb), name="mix_ln_bwd")
    return res


def _mix_conv_bwd(hin, dy, dz, wa, wb, wa_w, comm):
    t_rows = hin.shape[0]
    nt = wa_w // LANE
    ka, kb = wa.shape[0], wb.shape[0]
    nr = CONV_HALO + t_rows
    kb_rows = -(-kb // SUBLANE) * SUBLANE

    def body(bg_ref, cg_ref, ha_ref, val_ref, gt_ref, dya_ref, dz_ref, wa_ref, wb_ref,
             dh_ref, dwa_ref, dwb_ref, base, sh, based, shd, tmp, wbc):
        zeros = lambda n: jnp.zeros((n, LANE), F32)
        base[pl.ds(0, CONV_HALO), :] = zeros(CONV_HALO)
        base[pl.ds(nr, SUBLANE), :] = zeros(SUBLANE)
        based[pl.ds(t_rows, CONV_HALO + SUBLANE), :] = zeros(CONV_HALO + SUBLANE)

        def fwd_slot(k_taps, k):
            return _tap_slot(CONV_HALO - (k_taps - 1) + k)

        def bwd_slot(k_taps, k):
            return _tap_slot(k_taps - 1 - k)

        def conv(w_ref, k_taps, src, slot, b, n):
            acc = None
            for k in range(k_taps):
                r, q = slot(k_taps, k)
                term = w_ref[pl.ds(k, 1), :] * src[r, pl.ds(b + q, n), :]
                acc = term if acc is None else acc + term
            return acc

        def by_residue(k_taps, slot):
            groups = {}
            for k in range(k_taps):
                r, q = slot(k_taps, k)
                groups.setdefault(r, []).append((k, q // SUBLANE))
            return groups

        def wgrad_loop(w_ref, k_taps):
            n_sub = WGRAD_ROWS // SUBLANE
            for k in range(k_taps):
                wbc[k] = jnp.broadcast_to(w_ref[pl.ds(k, 1), :], (SUBLANE, LANE))
            fwd, bwd = by_residue(k_taps, fwd_slot), by_residue(k_taps, bwd_slot)

            def window(src, r, taps, b):
                span = n_sub + max(qi for _, qi in taps)
                return [src[r, pl.ds(b + SUBLANE * i, SUBLANE), :] for i in range(span)]

            def step(b, accs):
                accs = list(accs)
                dv = [based[pl.ds(b + SUBLANE * j, SUBLANE), :] for j in range(n_sub)]
                for r, taps in fwd.items():
                    win = window(sh, r, taps, b)
                    for k, qi in taps:
                        t = dv[0] * win[qi]
                        for j in range(1, n_sub):
                            t = t + dv[j] * win[qi + j]
                        accs[k] = accs[k] + t
                outs = [None] * n_sub
                for r, taps in bwd.items():
                    win = window(shd, r, taps, b)
                    for k, qi in taps:
                        wk = wbc[k]
                        for j in range(n_sub):
                            term = wk * win[qi + j]
                            outs[j] = term if outs[j] is None else outs[j] + term
                for j in range(n_sub):
                    tmp[pl.ds(b + SUBLANE * j, SUBLANE), :] = outs[j]
                return tuple(accs)

            return _row_loop(t_rows, WGRAD_ROWS, step, tuple(zeros(SUBLANE) for _ in range(k_taps)))

        def store_taps(ref, accs, rows):
            for k, acc in enumerate(accs):
                ref[pl.ds(k, 1), :] = jnp.sum(acc, axis=0, keepdims=True)
            if rows > len(accs):
                ref[pl.ds(len(accs), rows - len(accs)), :] = zeros(rows - len(accs))

        def fill_a(b, c):
            sl = pl.ds(b, CONV_CHUNK)
            base[pl.ds(CONV_HALO + b, CONV_CHUNK), :] = cg_ref[sl, :] * ha_ref[sl, :]
            based[sl, :] = dya_ref[sl, :] * bg_ref[sl, :]
            return c

        _row_loop(t_rows, CONV_CHUNK, fill_a)
        _fill_shifted(sh, base, sorted({fwd_slot(ka, k)[0] for k in range(ka)}), nr)
        _fill_shifted(shd, based, sorted({bwd_slot(ka, k)[0] for k in range(ka)}), nr)

        def d_bgate(b, c):
            sl = pl.ds(b, CONV_CHUNK)
            dh_ref[0, sl, :] = (dya_ref[sl, :] * conv(wa_ref, ka, sh, fwd_slot, b, CONV_CHUNK)).astype(BF16)
            return c

        _row_loop(t_rows, CONV_CHUNK, d_bgate)
        store_taps(dwa_ref, wgrad_loop(wa_ref, ka), SUBLANE)

        def d_ch(b, c):
            sl = pl.ds(b, CONV_CHUNK)
            dua = tmp[sl, :]
            dh_ref[1, sl, :] = (dua * ha_ref[sl, :]).astype(BF16)
            dh_ref[2, sl, :] = (dua * cg_ref[sl, :]).astype(BF16)
            return c

        _row_loop(t_rows, CONV_CHUNK, d_ch)

        def fill_b(b, c):
            sl = pl.ds(b, CONV_CHUNK)
            base[pl.ds(CONV_HALO + b, CONV_CHUNK), :] = val_ref[sl, :] * jax.nn.sigmoid(gt_ref[sl, :])
            based[sl, :] = dz_ref[sl, :]
            return c

        _row_loop(t_rows, CONV_CHUNK, fill_b)
        _fill_shifted(sh, base, range(SUBLANE), nr)
        _fill_shifted(shd, based, range(SUBLANE), nr)
        store_taps(dwb_ref, wgrad_loop(wb_ref, kb), kb_rows)

        def d_glu(b, c):
            sl = pl.ds(b, CONV_CHUNK)
            dgg = tmp[sl, :]
            sig = jax.nn.sigmoid(gt_ref[sl, :])
            dh_ref[3, sl, :] = (dgg * sig).astype(BF16)
            dh_ref[4, sl, :] = (dgg * val_ref[sl, :] * (sig * (1.0 - sig))).astype(BF16)
            return c

        _row_loop(t_rows, CONV_CHUNK, d_glu)

    def col(g):
        return pl.BlockSpec((t_rows, LANE), lambda i, g=g: (0, g * nt + i))

    tile = lambda rows: pl.BlockSpec((rows, LANE), lambda i: (0, i))
    return _host_call(
        body, grid=(nt,),
        in_specs=[col(0), col(1), col(2), col(3), col(4), tile(t_rows), tile(t_rows), tile(ka), tile(kb)],
        out_specs=[pl.BlockSpec((5, t_rows, LANE), lambda i: (0, 0, i)), tile(SUBLANE), tile(kb_rows)],
        out_shape=[jax.ShapeDtypeStruct((5, t_rows, wa_w), BF16), jax.ShapeDtypeStruct((SUBLANE, wa_w), F32),
                   jax.ShapeDtypeStruct((kb_rows, wa_w), F32)],
        scratch_shapes=[pltpu.VMEM((nr + SUBLANE, LANE), F32), pltpu.VMEM((SUBLANE, nr, LANE), F32),
                        pltpu.VMEM((nr + SUBLANE, LANE), F32), pltpu.VMEM((SUBLANE, nr, LANE), F32),
                        pltpu.VMEM((t_rows, LANE), F32), pltpu.VMEM((kb_rows, SUBLANE, LANE), F32)],
        args=(hin, hin, hin, hin, hin, dy, dz, wa, wb), name="mix_conv_bwd", comm=comm)


def _bwd_in_proj(dh5, win_t, h0, dh1, g1, tm, comm):
    t_rows, d = h0.shape
    groups, _, w = dh5.shape

    def body(dh_ref, w_ref, h0_ref, dh1_ref, g1_ref, dh0_ref, dg1_ref):
        dxn1 = None
        for g in range(groups):
            part = _dot_nn(dh_ref[g], w_ref[pl.ds(g * w, w), :])
            dxn1 = part if dxn1 is None else dxn1 + part
        h0 = h0_ref[...]
        r1 = _rstd(h0)
        h0h = h0 * r1
        _acc_rows(dg1_ref, dxn1 * h0h, pl.program_id(0) == 0)
        dh0_ref[...] = dh1_ref[...] + _rms_bwd(dxn1 * g1_ref[...], h0h, r1)

    blk = pl.BlockSpec((tm, d), lambda i: (i, 0))
    return _host_call(
        body, grid=(t_rows // tm,),
        in_specs=[pl.BlockSpec((groups, tm, w), lambda i: (0, i, 0)), _resident(win_t.shape), blk, blk, _full((1, d))],
        out_specs=[blk, _full((1, d))],
        out_shape=[jax.ShapeDtypeStruct((t_rows, d), F32), jax.ShapeDtypeStruct((1, d), F32)],
        args=(dh5, win_t, h0, dh1, g1), name="bwd_in_proj", comm=comm)


def _reduce_small(smalls, d, after):
    (dmeta, dg1, dg2, dg3, dg4, dbb, dlg, dlb, lossv, dwa, dwb) = smalls
    half = d // 2
    kb_rows = dwb.shape[0]

    def body(dmeta_ref, dg1_ref, dg2_ref, dg3_ref, dg4_ref, dbb_ref, dlg_ref, dlb_ref, loss_ref, dwa_ref, dwb_ref,
             ptot_ref, pbuf, psib, chip_p, ps_send, ps_recv, pc_send, pc_recv):
        x, y, c = _mesh_pos()
        pbuf[...] = jnp.zeros_like(pbuf)
        pbuf[pl.ds(0, N_META), :] = dmeta_ref[...]
        for row, ref in ((16, dg1_ref), (17, dg2_ref), (18, dg3_ref), (19, dg4_ref)):
            pbuf[pl.ds(row, 1), :] = ref[...]
        pbuf[pl.ds(20, 1), pl.ds(0, half)] = dbb_ref[...]
        pbuf[pl.ds(20, 1), pl.ds(half, half)] = dlg_ref[...]
        pbuf[pl.ds(21, 1), pl.ds(0, half)] = dlb_ref[...]
        lv = loss_ref[...]
        pbuf[pl.ds(21, 1), pl.ds(half, half)] = lv[:, :half] + lv[:, half:]
        pbuf[pl.ds(24, SUBLANE), pl.ds(0, half)] = dwa_ref[...]
        pbuf[pl.ds(32, kb_rows), pl.ds(0, half)] = dwb_ref[...]
        to_sib = _remote(pbuf, psib, ps_send.at[0], ps_recv.at[0], (x, y, 1 - c))
        to_sib.start()
        to_sib.wait_recv()
        my_chip = 2 * x + y
        chip_p[my_chip] = pbuf[...] + psib[...]
        to_sib.wait_send()
        slot = chip_p.at[my_chip]
        cps = [_remote(slot, slot, pc_send.at[k], pc_recv.at[k], (*chip, c)) for k, chip in enumerate(_other_chips(x, y))]
        for cp in cps:
            cp.start()
        for cp in cps:
            cp.wait_recv()
        ptot_ref[...] = ((chip_p[0] + chip_p[1]) + chip_p[2]) + chip_p[3]
        for cp in cps:
            cp.wait_send()

    return _host_call(
        body, grid=(), in_specs=[VMEM] * 11, out_specs=[VMEM], out_shape=[jax.ShapeDtypeStruct((SMALL_ROWS, d), F32)],
        scratch_shapes=[pltpu.VMEM((SMALL_ROWS, d), F32), pltpu.VMEM((SMALL_ROWS, d), F32), pltpu.VMEM((4, SMALL_ROWS, d), F32),
                        pltpu.SemaphoreType.DMA((1,)), pltpu.SemaphoreType.DMA((1,)),
                        pltpu.SemaphoreType.DMA((3,)), pltpu.SemaphoreType.DMA((3,))],
        args=smalls, name="reduce_small", after=after)


def _adamw(w, g, m, v):
    m = ADAM_B1 * m + (1.0 - ADAM_B1) * g
    v = ADAM_B2 * v + (1.0 - ADAM_B2) * jnp.square(g)
    m_hat = m / (1.0 - ADAM_B1 ** ADAM_STEP)
    v_hat = v / (1.0 - ADAM_B2 ** ADAM_STEP)
    delta = -ADAM_LR * (m_hat / (jnp.sqrt(v_hat) + ADAM_EPS) + ADAM_WD * w)
    return delta, m, v


def _adam_big(g, pair, part, w, m, v, name):
    r, d = w.shape
    cols = d // ADAM_COL_BLOCKS

    def body(me_ref, g_ref, pair_ref, part_ref, w_ref, m_ref, v_ref, go_ref, d_ref, mo_ref, vo_ref):
        g = g_ref[...].astype(F32) + pair_ref[...].astype(F32)
        for k in range(3):
            g = g + part_ref[k].astype(F32)
        go_ref[...] = g
        d_ref[...], mo_ref[...], vo_ref[...] = _adamw(w_ref[...], g, m_ref[...], v_ref[...])

    blk = pl.BlockSpec((r, cols), lambda i, me_ref: (0, i))
    grid_spec = pltpu.PrefetchScalarGridSpec(
        num_scalar_prefetch=1, grid=(ADAM_COL_BLOCKS,),
        in_specs=[pl.BlockSpec((r, cols), lambda i, me_ref: (me_ref[0], i)),
                  pl.BlockSpec((None, r, cols), lambda i, me_ref: (0, 0, i)),
                  pl.BlockSpec((3, r, cols), lambda i, me_ref: (0, 0, i)), blk, blk, blk],
        out_specs=[blk, blk, blk, blk])
    me = jnp.reshape(_dev_index(*_mesh_pos()), (1,)).astype(jnp.int32)
    return pl.pallas_call(body, out_shape=[jax.ShapeDtypeStruct((r, d), F32)] * 4, grid_spec=grid_spec, name=name,
                          compiler_params=_cparams(1))(me, g, pair, part, w, m, v)


def _adam_small(gs, ws, ms, vs):
    n = len(gs)

    def body(*refs):
        ins, outs = refs[:4 * n], refs[4 * n:]
        for i in range(n):
            g = ins[i][...]
            delta, m, v = _adamw(ins[n + i][...], g, ins[2 * n + i][...], ins[3 * n + i][...])
            outs[i][...] = delta
            outs[n + i][...] = m
            outs[2 * n + i][...] = v

    shapes = [jax.ShapeDtypeStruct(w.shape, F32) for w in ws]
    return pl.pallas_call(body, out_shape=shapes * 3, name="adam_small", compiler_params=_cparams())(*gs, *ws, *ms, *vs)


def kernel(x, meta_tokens, pre_mix_norm, w_in, conv_a_w, conv_b_w, conv_b_bias, ln_b_gain, ln_b_bias, w_out, post_mix_norm, pre_ffn_norm, w_gate, w_up, w_down, post_ffn_norm, loss_target, m_meta_tokens, m_pre_mix_norm, m_w_in, m_conv_a_w, m_conv_b_w, m_conv_b_bias, m_ln_b_gain, m_ln_b_bias, m_w_out, m_post_mix_norm, m_pre_ffn_norm, m_w_gate, m_w_up, m_w_down, m_post_ffn_norm, v_meta_tokens, v_pre_mix_norm, v_w_in, v_conv_a_w, v_conv_b_w, v_conv_b_bias, v_ln_b_gain, v_ln_b_bias, v_w_out, v_post_mix_norm, v_pre_ffn_norm, v_w_gate, v_w_up, v_w_down, v_post_ffn_norm):
    _, seq, d = x.shape
    ka, ca_loc = conv_a_w.shape[1:]
    kb, cb_loc = conv_b_w.shape[1:]
    wa_w = ca_loc * N_DEV
    assert cb_loc == ca_loc and wa_w % LANE == 0 and w_in.shape[2] * N_DEV == 5 * wa_w
    pad = (-(N_META + seq)) % ROW_ALIGN
    x0 = pad + N_META
    t_rows = x0 + seq
    assert t_rows % (N_ROW_BLOCKS * BF16_ROWS) == 0 and t_rows % CONV_CHUNK == 0 and d % LANE == 0
    tm = t_rows // N_ROW_BLOCKS
    me = _dev_index(*_mesh_pos())

    def as_rows(w_in_like, w_out_like, w_gate_like, w_up_like, w_down_like):
        return (w_in_like[0].T, w_out_like[0], w_gate_like[0].T, w_up_like[0].T, w_down_like[0])

    w_loc = as_rows(w_in, w_out, w_gate, w_up, w_down)
    rows = [w.shape[0] for w in w_loc]
    assert all(r % ADD_CHUNK == 0 for r in rows)
    P_IN, P_OUT, P_GATE, P_UP, P_DOWN = range(N_BIG)

    sm = jnp.zeros((SM_ROWS, LANE), F32)
    sm = sm.at[0:N_META, :].set(meta_tokens)
    sm = sm.at[16:16 + ka, 0:ca_loc].set(conv_a_w[0])
    sm = sm.at[24:24 + kb, 0:cb_loc].set(conv_b_w[0])
    wl, wfull, sm_all, h0, tgt = _gather_first(w_loc, sm, [(P_IN, 0, rows[P_IN])], x[0], loss_target[0], t_rows, x0)
    wa =jnp.transpose(sm_all[:, 16:16 + ka, 0:ca_loc], (1, 0, 2)).reshape(ka, wa_w)
    wb = jnp.transpose(sm_all[:, 24:24 + kb, 0:cb_loc], (1, 0, 2)).reshape(kb, wa_w)

    later = (P_OUT, P_GATE, P_UP, P_DOWN)
    sems, wl, started, _ = _gather_start(wl, wfull, later, rows)
    for p, arr in zip(later, started):
        wfull[p] = arr

    def arrived(p, after, name):
        nonlocal wl
        wl, wfull[p] = _gather_wait(wl, wfull[p], sems[later.index(p)], after, rows[p], name)
        return _forward_comm(wfull[p], rows[p])

    (xn1, hin), _ = _in_proj(h0, pre_mix_norm, wfull[P_IN], tm, None)
    (ya, z), (wfull[P_OUT], wfull[P_GATE]) = _mix_conv_fwd(
        hin, wa, wb, conv_b_bias, wa_w,
        _merge_comms([arrived(P_OUT, hin, "gather_wait_out"), arrived(P_GATE, hin, "gather_wait_gate")]))
    y = _mix_ln_fwd(ya, z, ln_b_gain, ln_b_bias, tm)
    (mix, h1, xn2), (wfull[P_UP],) = _out_proj(y, wfull[P_OUT], h0, post_mix_norm, pre_ffn_norm, tm,
                                               arrived(P_UP, y, "gather_wait_up"))
    (a, u, s), (wfull[P_DOWN],) = _gate_up(xn2, wfull[P_GATE], wfull[P_UP], tm, arrived(P_DOWN, xn2, "gather_wait_down"))
    dh2, dff, dg4, lossv = _down_loss(s, wfull[P_DOWN], h1, tgt, post_ffn_norm, tm, x0)

    gwd = _wgrad(s, dff, "wgrad_down")
    (da, du), (pair_d,) = _bwd_down(dff, wfull[P_DOWN], a, u, tm, _pair_comm(gwd, rows[P_DOWN]))
    (flight_d,), token = _chip_start([_pair_sum(gwd, pair_d, rows[P_DOWN], "pair_sum_down")], "chip_start_down")
    gwg = _wgrad(da, xn2, "wgrad_gate", [token])
    gwu = _wgrad(du, xn2, "wgrad_up")
    (dh1, dg3), (pair_g, pair_u) = _bwd_ffn_in(da, du, wfull[P_GATE], wfull[P_UP], h1, dh2, pre_ffn_norm, tm,
                                               _merge_comms([_pair_comm(gwg, rows[P_GATE]), _pair_comm(gwu, rows[P_UP])]))
    (flight_g, flight_u), token = _chip_start([_pair_sum(gwg, pair_g, rows[P_GATE], "pair_sum_gate"),
                                               _pair_sum(gwu, pair_u, rows[P_UP], "pair_sum_up")], "chip_start_gate_up")
    dmix, dy, dg2 = _bwd_out_proj(dh1, mix, wfull[P_OUT], post_mix_norm, tm, [token])
    gwo = _wgrad(y, dmix, "wgrad_out")
    dz, dlg, dlb, dbb = _mix_ln_bwd(z, dy, ln_b_gain, ln_b_bias, tm)
    (dh5, dwa, dwb), (pair_o,) = _mix_conv_bwd(hin, dy, dz, wa, wb, wa_w, _pair_comm(gwo, rows[P_OUT]))
    (flight_o,), token = _chip_start([_pair_sum(gwo, pair_o, rows[P_OUT], "pair_sum_out")], "chip_start_out")
    gwi = _wgrad(dh5, xn1, "wgrad_in", [token])
    (dh0, dg1), (pair_i,) = _bwd_in_proj(dh5, wfull[P_IN], h0, dh1, pre_mix_norm, tm, _pair_comm(gwi, rows[P_IN]))
    grad_x = dh0[x0:][None]
    dmeta = dh0[x0 - N_META:x0]
    (ptot,), _ = _reduce_small((dmeta, dg1, dg2, dg3, dg4, dbb, dlg, dlb, lossv, dwa, dwb), d, [])
    (flight_i,), token = _chip_start([_pair_sum(gwi, pair_i, rows[P_IN], "pair_sum_in", [ptot])], "chip_start_in")

    def landed(flight, after, tag):
        sems_p, sums, land = flight
        return _chip_wait(sums, land, sems_p, after, "chip_wait_" + tag)

    part_d = landed(flight_d, token, "down")
    part_g = landed(flight_g, token, "gate")
    part_u = landed(flight_u, token, "up")
    part_o = landed(flight_o, token, "out")

    half = d // 2
    loss = (0.5 / d) * jnp.sum(ptot[21, half:])
    g_meta = lax.dynamic_slice(ptot, (0, me * (d // N_DEV)), (N_META, d // N_DEV))
    g_small = [g_meta, ptot[16:17], lax.dynamic_slice(ptot, (24, me * ca_loc), (ka, ca_loc))[None],
               lax.dynamic_slice(ptot, (32, me * cb_loc), (kb, cb_loc))[None],
               ptot[20:21, :half], ptot[20:21, half:], ptot[21:22, :half], ptot[17:18], ptot[18:19], ptot[19:20]]
    w_small = [meta_tokens, pre_mix_norm, conv_a_w, conv_b_w, conv_b_bias, ln_b_gain, ln_b_bias, post_mix_norm,
               pre_ffn_norm, post_ffn_norm]
    m_small = [m_meta_tokens, m_pre_mix_norm, m_conv_a_w, m_conv_b_w, m_conv_b_bias, m_ln_b_gain, m_ln_b_bias,
               m_post_mix_norm, m_pre_ffn_norm, m_post_ffn_norm]
    v_small = [v_meta_tokens, v_pre_mix_norm, v_conv_a_w, v_conv_b_w, v_conv_b_bias, v_ln_b_gain, v_ln_b_bias,
               v_post_mix_norm, v_pre_ffn_norm, v_post_ffn_norm]
    small = _adam_small(g_small, w_small, m_small, v_small)
    n_small = len(w_small)
    d_small, nm_small, nv_small = small[:n_small], small[n_small:2 * n_small], small[2 * n_small:]

    m_loc = as_rows(m_w_in, m_w_out, m_w_gate, m_w_up, m_w_down)
    v_loc = as_rows(v_w_in, v_w_out, v_w_gate, v_w_up, v_w_down)
    full_grads = {P_IN: gwi, P_OUT: gwo, P_GATE: gwg, P_UP: gwu, P_DOWN: gwd}
    pairs = {P_IN: pair_i, P_OUT: pair_o, P_GATE: pair_g, P_UP: pair_u, P_DOWN: pair_d}
    parts = {P_OUT: part_o, P_GATE: part_g, P_UP: part_u, P_DOWN: part_d}
    names = {P_IN: "w_in", P_OUT: "w_out", P_GATE: "w_gate", P_UP: "w_up", P_DOWN: "w_down"}
    bigs = {}
    res = None
    for p in (P_DOWN, P_GATE, P_UP, P_OUT, P_IN):
        if p == P_IN:
            parts[p] = landed(flight_i, res[1], "in")
        res = _adam_big(full_grads[p], pairs[p], parts[p], w_loc[p], m_loc[p], v_loc[p], "adam_" + names[p])
        bigs[names[p]] = [(o.T if p in (P_IN, P_GATE, P_UP) else o)[None] for o in res]

    def ordered(pick_small, pick_big):
        sm_it = iter(range(n_small))
        out = []
        for name in ("s", "s", "w_in", "s", "s", "s", "s", "s", "w_out", "s", "s", "w_gate", "w_up", "w_down", "s"):
            out.append(pick_small(next(sm_it)) if name == "s" else pick_big(name))
        return out

    grads = ordered(lambda i: g_small[i], lambda n: bigs[n][0])
    deltas = ordered(lambda i: d_small[i], lambda n: bigs[n][1])
    new_m = ordered(lambda i: nm_small[i], lambda n: bigs[n][2])
    new_v = ordered(lambda i: nv_small[i], lambda n: bigs[n][3])
    return (loss, grad_x, *grads, *deltas, *new_m, *new_v)
```

```python
import jax
import jax.numpy as jnp
from jax import lax
from jax.experimental import pallas as pl
from jax.experimental.pallas import tpu as pltpu

F32 = jnp.float32
BF16 = jnp.bfloat16
MESH = pl.DeviceIdType.MESH

N_META = 16
N_DEV = 8
RMS_EPS = 1e-6
LN_EPS = 1e-5
ADAM_LR = 0.001
ADAM_B1 = 0.9
ADAM_B2 = 0.999
ADAM_EPS = 1e-08
ADAM_WD = 0.01
ADAM_STEP = 10

LANE = 128
SUBLANE = 8
BF16_ROWS = 16
ROW_ALIGN = 128
N_ROW_BLOCKS = 4
CONV_HALO = 32
CONV_CHUNK = 64
WGRAD_ROWS = 32
N_CHUNK = 512
WGRAD_TILE_MAX = 1408
ADD_CHUNK = 32
ADAM_COL_BLOCKS = 4
COPY_PIECES = 4
V7X_VMEM_BYTES = 64 * 1024 * 1024
VMEM_LIMIT = V7X_VMEM_BYTES - 6 * 1024 * 1024
SMALL_ROWS = 64
SM_ROWS = 56
N_BIG = 5

ANY = pl.BlockSpec(memory_space=pl.ANY)
VMEM = pl.BlockSpec(memory_space=pltpu.VMEM)


def _cparams(n_grid_axes=0):
    sem = ("arbitrary",) * n_grid_axes if n_grid_axes else None
    return pltpu.CompilerParams(dimension_semantics=sem, vmem_limit_bytes=VMEM_LIMIT)


def _mesh_pos():
    return lax.axis_index("x"), lax.axis_index("y"), lax.axis_index("c")


def _dev_index(px, py, pc):
    return 4 * px + 2 * py + pc


def _other_chips(x, y):
    return [(1 - x, y), (x, 1 - y), (1 - x, 1 - y)]


def _full(shape):
    return pl.BlockSpec(shape, lambda *_: (0,) * len(shape))


def _resident(shape):
    return pl.BlockSpec(shape, lambda *_: (0,) * len(shape), pipeline_mode=pl.Buffered(1))


def _dot_nt(a, w):
    return lax.dot_general(a, w, (((1,), (1,)), ((), ())), preferred_element_type=F32)


def _dot_nn(a, w):
    return jnp.dot(a, w, preferred_element_type=F32)


def _chunks(n, c):
    out, o = [], 0
    while o < n:
        out.append((o, min(c, n - o)))
        o += c
    return out


def _rstd(h):
    return lax.rsqrt(jnp.mean(h * h, axis=-1, keepdims=True) + RMS_EPS)


def _rms_bwd(dyh, yh, r):
    return r * (dyh - yh * jnp.mean(dyh * yh, axis=-1, keepdims=True))


def _silu_grad(a, sig):
    return sig * (1.0 + a * (1.0 - sig))


def _acc_rows(ref, val, first):
    s = jnp.sum(val, axis=0, keepdims=True)

    @pl.when(first)
    def _():
        ref[...] = s

    @pl.when(jnp.logical_not(first))
    def _():
        ref[...] += s


def _row_loop(t_rows, chunk, fn, carry=None):
    def step(i, c):
        return fn(pl.multiple_of(i * chunk, chunk), c)

    return lax.fori_loop(0, t_rows // chunk, step, carry)


def _remote(src, dst, send_sem, recv_sem, to):
    return pltpu.make_async_remote_copy(src_ref=src, dst_ref=dst, send_sem=send_sem, recv_sem=recv_sem,
                                        device_id=to, device_id_type=MESH)


class _SplitRemote:
    def __init__(self, src, dst, send_sem, recv_sem, to, rows, n_chunks):
        units = rows // BF16_ROWS
        n_chunks = max(1, min(n_chunks, units))
        sizes = [(units // n_chunks + (i < units % n_chunks)) * BF16_ROWS for i in range(n_chunks)]
        self.whole = _remote(src, dst, send_sem, recv_sem, to)
        self.parts, o = [], 0
        for n in sizes:
            self.parts.append(_remote(src.at[pl.ds(o, n), :], dst.at[pl.ds(o, n), :], send_sem, recv_sem, to))
            o += n

    def start(self):
        for cp in self.parts:
            cp.start()

    def wait_recv(self):
        self.whole.wait_recv()

    def wait_send(self):
        self.whole.wait_send()


class _Comm:
    def __init__(self, inputs, out_shapes, aliases, scratch, start, finish):
        self.inputs, self.out_shapes, self.aliases, self.scratch = list(inputs), list(out_shapes), dict(aliases), list(scratch)
        self.start, self.finish = start, finish


def _merge_comms(comms):
    inputs, out_shapes, aliases, scratch, spans = [], [], {}, [], []
    for cm in comms:
        spans.append((len(inputs), len(out_shapes), len(scratch), cm))
        aliases.update({len(inputs) + k: len(out_shapes) + v for k, v in cm.aliases.items()})
        inputs += cm.inputs
        out_shapes += cm.out_shapes
        scratch += cm.scratch

    def run(which):
        def fn(ins, outs, scr):
            for i0, o0, s0, cm in spans:
                getattr(cm, which)(ins[i0:i0 + len(cm.inputs)], outs[o0:o0 + len(cm.out_shapes)], scr[s0:s0 + len(cm.scratch)])
        return fn

    return _Comm(inputs, out_shapes, aliases, scratch, run("start"), run("finish"))


def _host_call(body, *, grid, in_specs, out_specs, out_shape, args, name, scratch_shapes=(), comm=None, after=()):
    if comm is None:
        comm = _Comm([], [], {}, [], lambda *_: None, lambda *_: None)
    n_in, n_out, n_scr = len(args), len(out_shape), len(scratch_shapes)
    c_in, c_out = len(comm.inputs), len(comm.out_shapes)
    n_after = len(after)

    def hosted(*refs):
        ins, c_ins = refs[:n_in], refs[n_in:n_in + c_in]
        o0 = n_in + c_in + n_after
        outs, c_outs = refs[o0:o0 + n_out], refs[o0 + n_out:o0 + n_out + c_out]
        s0 = o0 + n_out + c_out
        scr, c_scr = refs[s0:s0 + n_scr], refs[s0 + n_scr:]
        if not grid:
            comm.start(c_ins, c_outs, c_scr)
            body(*ins, *outs, *scr)
            comm.finish(c_ins, c_outs, c_scr)
            return
        first = last = None
        for a, n in enumerate(grid):
            f, l = pl.program_id(a) == 0, pl.program_id(a) == n - 1
            first = f if first is None else jnp.logical_and(first, f)
            last = l if last is None else jnp.logical_and(last, l)

        @pl.when(first)
        def _():
            comm.start(c_ins, c_outs, c_scr)

        body(*ins, *outs, *scr)

        @pl.when(last)
        def _():
            comm.finish(c_ins, c_outs, c_scr)

    res = pl.pallas_call(
        hosted, grid=grid, in_specs=list(in_specs) + [ANY] * (c_in + n_after), out_specs=list(out_specs) + [ANY] * c_out,
        out_shape=list(out_shape) + comm.out_shapes, scratch_shapes=list(scratch_shapes) + comm.scratch,
        input_output_aliases={n_in + k: n_out + v for k, v in comm.aliases.items()},
        name=name, compiler_params=_cparams(len(grid)))(*args, *comm.inputs, *after)
    return list(res[:n_out]), list(res[n_out:])


GATHER_SEMS = 10
D2D_CHUNKS = 8


class _Gather:
    def __init__(self, jobs, rows, lo, src_ref, dests, send_sems, recv_sems):
        x, y, c = _mesh_pos()
        me, sib = (x, y, c), (x, y, 1 - c)
        nx, ny, dg = (1 - x, y, c), (x, 1 - y, c), (1 - x, 1 - y, c)
        self.relayed, self.direct, self.relay, self.to_sib, self.sib_fwd = [], [], [], [], []
        for n, (p, r0, nr) in enumerate(jobs):
            assert nr % (2 * BF16_ROWS) == 0
            half = nr // 2

            def rows_of(dev, h, p=p, r0=r0, nr=nr, half=half):
                off, cnt = (r0, nr) if h is None else (r0 + h * half, half)
                return dests[p].at[pl.ds(pl.multiple_of(_dev_index(*dev) * rows[p] + off, BF16_ROWS), cnt), :]

            def mine(h, p=p, r0=r0, nr=nr, half=half):
                off, cnt = (r0, nr) if h is None else (r0 + h * half, half)
                return src_ref.at[pl.ds(lo[p] + off, cnt), :]

            sem = lambda k, n=n: (send_sems.at[GATHER_SEMS * n + k], recv_sems.at[GATHER_SEMS * n + k])
            self.relayed.append([_remote(mine(0), rows_of(me, 0), *sem(0), nx), _remote(mine(1), rows_of(me, 1), *sem(3), ny)])
            self.direct.append([_remote(mine(1), rows_of(me, 1), *sem(1), nx), _remote(mine(0), rows_of(me, 0), *sem(2), ny)])
            self.relay.append([_remote(rows_of(nx, 0), rows_of(nx, 0), *sem(4), ny), _remote(rows_of(ny, 1), rows_of(ny, 1), *sem(5), nx)])
            self.to_sib.append(_SplitRemote(mine(None), rows_of(me, None), *sem(6), sib, nr, D2D_CHUNKS))
            self.sib_fwd.append([_SplitRemote(rows_of(dev, None), rows_of(dev, None), *sem(7 + i), sib, nr, D2D_CHUNKS)
                                 for i, dev in enumerate((nx, ny, dg))])

    def start(self):
        for group in (self.relayed, self.direct):
            for cps in group:
                for cp in cps:
                    cp.start()
        for cp in self.to_sib:
            cp.start()

    def mid(self):
        for first, relay in zip(self.relayed, self.relay):
            for arrived, onward in zip(first, relay):
                arrived.wait_recv()
                onward.start()

    def finish(self):
        for direct, relay, fwd in zip(self.direct, self.relay, self.sib_fwd):
            for k in range(2):
                direct[k].wait_recv()
                fwd[k].start()
            for cp in relay:
                cp.wait_recv()
            fwd[2].start()
        for n in range(len(self.to_sib)):
            self.to_sib[n].wait_recv()
            for cp in self.sib_fwd[n]:
                cp.wait_recv()
            for cp in self.relayed[n] + self.direct[n] + self.relay[n] + [self.to_sib[n]] + self.sib_fwd[n]:
                cp.wait_send()


HBM = pl.BlockSpec(memory_space=pltpu.HBM)
SEM = pl.BlockSpec(memory_space=pltpu.SEMAPHORE)
FLOWS = pltpu.SideEffectType.DATAFLOW_SIDE_EFFECTING


def _in_hbm(a):
    return pltpu.with_memory_space_constraint(a, pltpu.HBM)


def _gather_start(wl, dests, ps, rows):
    lo = [sum(rows[:p]) for p in range(N_BIG)]
    n = len(ps)

    def body(*refs):
        wl_ref, dest_refs = refs[0], refs[1:1 + n]
        sends, recvs = refs[1 + n:1 + 2 * n], refs[1 + 2 * n:1 + 3 * n]
        token = refs[-1]
        x, y, c = _mesh_pos()
        jme = _dev_index(x, y, c)
        for i, p in enumerate(ps):
            mine = dest_refs[i].at[pl.ds(pl.multiple_of(jme * rows[p], BF16_ROWS), rows[p]), :]
            for chip in _other_chips(x, y):
                _remote(wl_ref.at[pl.ds(lo[p], rows[p]), :], mine, sends[i], recvs[i], (*chip, c)).start()
        token[...] = jnp.zeros_like(token)

    thru = [pltpu.HBM(wl.shape, wl.dtype)] + [pltpu.HBM(dests[p].shape, BF16) for p in ps]
    res = pl.pallas_call(
        body, name="gather_start",
        out_shape=tuple([pltpu.SemaphoreType.DMA(())] * (2 * n) + thru + [jax.ShapeDtypeStruct((SUBLANE, LANE), F32)]),
        in_specs=[HBM] * (1 + n), out_specs=tuple([SEM] * (2 * n) + [HBM] * (1 + n) + [VMEM]),
        input_output_aliases={i: 2 * n + i for i in range(1 + n)},
        compiler_params=pltpu.CompilerParams(has_side_effects=FLOWS))(_in_hbm(wl), *[_in_hbm(dests[p]) for p in ps])
    sems = [(res[i], res[n + i]) for i in range(n)]
    return sems, res[2 * n], list(res[2 * n + 1:3 * n + 1]), res[-1]


def _gather_wait(wl, dest, sems, after, r, name):
    def body(wl_ref, dest_ref, send_sem, recv_sem, after_ref, wl_out, dest_out):
        x, y, c = _mesh_pos()
        three = dest_ref.at[pl.ds(0, 3 * r), :]
        cp = _remote(three, three, send_sem, recv_sem, (x, y, 1 - c))
        cp.wait_send()
        cp.wait_recv()

    res = pl.pallas_call(
        body, name=name, out_shape=(pltpu.HBM(wl.shape, wl.dtype), pltpu.HBM(dest.shape, dest.dtype)),
        in_specs=[HBM, HBM, SEM, SEM, ANY], out_specs=(HBM, HBM), input_output_aliases={0: 0, 1: 1},
        compiler_params=pltpu.CompilerParams(has_side_effects=FLOWS))(wl, dest, sems[0], sems[1], after)
    return res[0], res[1]


def _forward_comm(dest, r):
    def descs(ins, outs, scr):
        x, y, c = _mesh_pos()
        cps = []
        for k, chip in enumerate(_other_chips(x, y)):
            blk = outs[0].at[pl.ds(pl.multiple_of(_dev_index(*chip, c) * r, BF16_ROWS), r), :]
            cps.append(_SplitRemote(blk, blk, scr[0].at[k], scr[1].at[k], (x, y, 1 - c), r, D2D_CHUNKS))
        return cps

    def start(ins, outs, scr):
        for cp in descs(ins, outs, scr):
            cp.start()

    def finish(ins, outs, scr):
        cps = descs(ins, outs, scr)
        for cp in cps:
            cp.wait_recv()
        for cp in cps:
            cp.wait_send()

    return _Comm([dest], [jax.ShapeDtypeStruct(dest.shape, dest.dtype)], {0: 0},
                 [pltpu.SemaphoreType.DMA((3,)), pltpu.SemaphoreType.DMA((3,))], start, finish)


def _pair_comm(g, r):
    d = g.shape[1]

    def descs(ins, outs, scr):
        x, y, c = _mesh_pos()
        chips = [(x, y)] + _other_chips(x, y)
        return [_SplitRemote(ins[0].at[pl.ds(pl.multiple_of(_dev_index(*chip, 1 - c) * r, BF16_ROWS), r), :], outs[0].at[k],
                             scr[0].at[k], scr[1].at[k], (x, y, 1 - c), r, D2D_CHUNKS) for k, chip in enumerate(chips)]

    def start(ins, outs, scr):
        for cp in descs(ins, outs, scr):
            cp.start()

    def finish(ins, outs, scr):
        cps = descs(ins, outs, scr)
        for cp in cps:
            cp.wait_recv()
        for cp in cps:
            cp.wait_send()

    comm = _Comm([g], [jax.ShapeDtypeStruct((4, r, d), BF16)], {},
                 [pltpu.SemaphoreType.DMA((4,)), pltpu.SemaphoreType.DMA((4,))], start, finish)
    return comm


def _pair_sum(g, pair, r, name, after=()):
    d = g.shape[1]
    x, y, c = _mesh_pos()
    idx = jnp.stack([_dev_index(*chip, c) for chip in _other_chips(x, y)]).astype(jnp.int32)

    def body(idx_ref, g_ref, p_ref, *rest):
        rest[len(after)][...] = (g_ref[...].astype(F32) + p_ref[...].astype(F32)).astype(BF16)

    grid_spec = pltpu.PrefetchScalarGridSpec(
        num_scalar_prefetch=1, grid=(3,),
        in_specs=[pl.BlockSpec((r, d), lambda k, idx_ref: (idx_ref[k], 0)),
                  pl.BlockSpec((None, r, d), lambda k, idx_ref: (1 + k, 0, 0))] + [ANY] * len(after),
        out_specs=pl.BlockSpec((None, r, d), lambda k, idx_ref: (k, 0, 0)))
    return pl.pallas_call(body, out_shape=jax.ShapeDtypeStruct((3, r, d), BF16), grid_spec=grid_spec, name=name,
                          compiler_params=_cparams(1))(idx, g, pair, *after)


def _chip_start(sums, name):
    n = len(sums)

    def body(*refs):
        srcs, lands = refs[:n], refs[n:2 * n]
        sends, recvs = refs[2 * n:3 * n], refs[3 * n:4 * n]
        x, y, c = _mesh_pos()
        for i in range(n):
            for k, chip in enumerate(_other_chips(x, y)):
                _remote(srcs[i].at[k], lands[i].at[k], sends[i], recvs[i], (*chip, c)).start()
        refs[-1][...] = jnp.zeros_like(refs[-1])

    zones = [pltpu.HBM(s.shape, s.dtype) for s in sums]
    res = pl.pallas_call(
        body, name=name,
        out_shape=tuple([pltpu.SemaphoreType.DMA(())] * (2 * n) + zones + zones + [jax.ShapeDtypeStruct((SUBLANE, LANE), F32)]),
        in_specs=[HBM] * (2 * n), out_specs=tuple([SEM] * (2 * n) + [HBM] * (2 * n) + [VMEM]),
        input_output_aliases={i: 2 * n + i for i in range(2 * n)},
        compiler_params=pltpu.CompilerParams(has_side_effects=FLOWS))(
            *[_in_hbm(s) for s in sums], *[_in_hbm(lax.empty(s.shape, s.dtype)) for s in sums])
    flights = [((res[i], res[n + i]), res[2 * n + i], res[3 * n + i]) for i in range(n)]
    return flights, res[-1]


def _chip_wait(sums, land, sems, after, name):
    def body(sums_ref, land_ref, send_sem, recv_sem, after_ref, sums_out, land_out):
        x, y, c = _mesh_pos()
        cp = _remote(sums_ref, land_ref, send_sem, recv_sem, (x, y, 1 - c))
        cp.wait_send()
        cp.wait_recv()

    res = pl.pallas_call(
        body, name=name, out_shape=(pltpu.HBM(sums.shape, sums.dtype), pltpu.HBM(land.shape, land.dtype)),
        in_specs=[HBM, HBM, SEM, SEM, ANY], out_specs=(HBM, HBM), input_output_aliases={0: 0, 1: 1},
        compiler_params=pltpu.CompilerParams(has_side_effects=FLOWS))(sums, land, sems[0], sems[1], after)
    return res[1]


class _CopyThrough:
    def __init__(self, src_ref, dst_ref, dst_row0, n_rows, buf, sem_in, sem_out):
        rc = n_rows // COPY_PIECES
        piece = lambda ref, o: ref.at[pl.ds(o, rc), :]
        self.loads = [pltpu.make_async_copy(piece(src_ref, k * rc), piece(buf, k * rc), sem_in) for k in range(COPY_PIECES)]
        self.stores = [pltpu.make_async_copy(piece(buf, k * rc), piece(dst_ref, dst_row0 + k * rc), sem_out) for k in range(COPY_PIECES)]
        self.all_in = pltpu.make_async_copy(src_ref, buf, sem_in)
        self.all_out = pltpu.make_async_copy(buf, dst_ref.at[pl.ds(dst_row0, n_rows), :], sem_out)

    def load(self):
        for cp in self.loads:
            cp.start()

    def store(self):
        self.all_in.wait()
        for cp in self.stores:
            cp.start()

    def done(self):
        self.all_out.wait()


def _gather_first(shards, sm, jobs, x2, tgt2, t_rows, x0):
    d = shards[0].shape[1]
    rows = [w.shape[0] for w in shards]
    lo = [sum(rows[:p]) for p in range(N_BIG)]
    n_sems = GATHER_SEMS * len(jobs)
    seq = x2.shape[0]
    assert x0 == ROW_ALIGN and seq % ROW_ALIGN == 0 and d == N_DEV * LANE

    def body(s0, s1, s2, s3, s4, sm_ref, x_ref, tgt_ref, wl_ref, o0, o1, o2, o3, o4, sa_ref, h0_ref, tp_ref,
             wl_v, x_v, tgt_v, heads_v, sa_v, send_sems, recv_sems, ssend, srecv, local_sems, sems_in, sems_out):
        dests = (o0, o1, o2, o3, o4)
        x, y, c = _mesh_pos()
        me = (x, y, c)
        jme = _dev_index(*me)
        padded = [_CopyThrough(x_ref, h0_ref, x0, seq, x_v, sems_in.at[0], sems_out.at[0]),
                  _CopyThrough(tgt_ref, tp_ref, x0, seq, tgt_v, sems_in.at[1], sems_out.at[1])]
        for cp in padded:
            cp.load()
        for p, ref in enumerate((s0, s1, s2, s3, s4)):
            wl_v[pl.ds(lo[p], rows[p]), :] = ref[...].astype(BF16)
        gather = _Gather(jobs, rows, lo, wl_v, dict(enumerate(dests)), send_sems, recv_sems)
        gather.start()
        peers = [(x, y, 1 - c)] + [(*chip, pc) for pc in (c, 1 - c) for chip in _other_chips(x, y)]
        smalls = [_remote(sm_ref, sa_ref.at[jme], ssend.at[k], srecv.at[k], to) for k, to in enumerate(peers)]
        for cp in smalls:
            cp.start()
        mine = [pltpu.make_async_copy(wl_v.at[pl.ds(lo[p], rows[p]), :],
                                      dests[p].at[pl.ds(pl.multiple_of(jme * rows[p], BF16_ROWS), rows[p]), :], local_sems.at[p])
                for p in range(N_BIG)]
        mine.append(pltpu.make_async_copy(wl_v, wl_ref, local_sems.at[N_BIG]))
        mine.append(pltpu.make_async_copy(sm_ref, sa_ref.at[jme], local_sems.at[N_BIG + 1]))
        for cp in mine:
            cp.start()
        later = [p for p in range(N_BIG) if p not in {j[0] for j in jobs}]
        own = [_SplitRemote(wl_v.at[pl.ds(lo[p], rows[p]), :],
                            dests[p].at[pl.ds(pl.multiple_of(jme * rows[p], BF16_ROWS), rows[p]), :],
                            ssend.at[7 + i], srecv.at[7 + i], (x, y, 1 - c), rows[p], D2D_CHUNKS) for i, p in enumerate(later)]
        for cp in own:
            cp.start()
        for cp in padded:
            cp.store()
        gather.mid()
        for cp in smalls + own:
            cp.wait_recv()
        mine[-1].wait()
        to_v = pltpu.make_async_copy(sa_ref, sa_v, local_sems.at[N_BIG + 1])
        to_v.start()
        to_v.wait()
        head, zeros = heads_v.at[0], heads_v.at[1]
        head[...] = jnp.zeros_like(head)
        zeros[...] = jnp.zeros_like(zeros)
        for j in range(N_DEV):
            head[pl.ds(x0 - N_META, N_META), pl.ds(j * LANE, LANE)] = sa_v[j, pl.ds(0, N_META), :]
        heads = [pltpu.make_async_copy(head, h0_ref.at[pl.ds(0, x0), :], local_sems.at[N_BIG + 1]),
                 pltpu.make_async_copy(zeros, tp_ref.at[pl.ds(0, x0), :], local_sems.at[N_BIG + 2])]
        for cp in heads:
            cp.start()
        gather.finish()
        for cp in smalls + own:
            cp.wait_send()
        for cp in mine[:-1] + heads:
            cp.wait()
        for cp in padded:
            cp.done()

    out_shape = [jax.ShapeDtypeStruct((sum(rows), d), BF16)]
    out_shape += [jax.ShapeDtypeStruct((N_DEV * r, d), BF16) for r in rows]
    out_shape.append(jax.ShapeDtypeStruct((N_DEV,) + sm.shape, F32))
    out_shape += [jax.ShapeDtypeStruct((t_rows, d), F32)] * 2
    res = pl.pallas_call(
        body, out_shape=out_shape, in_specs=[VMEM] * 6 + [ANY] * 2, out_specs=[ANY] * 9,
        scratch_shapes=[pltpu.VMEM((sum(rows), d), BF16), pltpu.VMEM((seq, d), F32), pltpu.VMEM((seq, d), F32),
                        pltpu.VMEM((2, ROW_ALIGN, d), F32), pltpu.VMEM((N_DEV,) + sm.shape, F32),
                        pltpu.SemaphoreType.DMA((n_sems,)), pltpu.SemaphoreType.DMA((n_sems,)),
                        pltpu.SemaphoreType.DMA((7 + N_BIG,)), pltpu.SemaphoreType.DMA((7 + N_BIG,)),
                        pltpu.SemaphoreType.DMA((N_BIG + 3,)), pltpu.SemaphoreType.DMA((2,)), pltpu.SemaphoreType.DMA((2,))],
        name="gather_first", compiler_params=_cparams())(*shards, sm, x2, tgt2)
    return res[0], list(res[1:1 + N_BIG]), res[1 + N_BIG], res[2 + N_BIG], res[3 + N_BIG]


def _in_proj(h0, g1, win_t, tm, comm):
    t_rows, d = h0.shape
    e = win_t.shape[0]

    def body(h_ref, g_ref, w_ref, xn_ref, hin_ref):
        h = h_ref[...]
        xn = ((h * _rstd(h)) * g_ref[...]).astype(BF16)
        xn_ref[...] = xn
        for o, n in _chunks(e, N_CHUNK):
            hin_ref[:, pl.ds(o, n)] = _dot_nt(xn, w_ref[pl.ds(o, n), :])

    return _host_call(
        body, grid=(t_rows // tm,),
        in_specs=[pl.BlockSpec((tm, d), lambda i: (i, 0)), _full((1, d)), _resident((e, d))],
        out_specs=[pl.BlockSpec((tm, d), lambda i: (i, 0)), pl.BlockSpec((tm, e), lambda i: (i, 0))],
        out_shape=[jax.ShapeDtypeStruct((t_rows, d), BF16), jax.ShapeDtypeStruct((t_rows, e), F32)],
        args=(h0, g1, win_t), name="in_proj", comm=comm)


def _tap_slot(off):
    return off % SUBLANE, (off // SUBLANE) * SUBLANE


def _fill_shifted(sh_ref, base_ref, residues, n_rows):
    for r in residues:
        sh_ref[r] = base_ref[pl.ds(r, n_rows), :]


def _mix_conv_fwd(hin, wa, wb, bb, wa_w, comm):
    t_rows = hin.shape[0]
    nt = wa_w // LANE
    ka, kb = wa.shape[0], wb.shape[0]
    nr = CONV_HALO + t_rows

    def body(bg_ref, cg_ref, ha_ref, val_ref, gt_ref, wa_ref, wb_ref, bb_ref, ya_ref, z_ref, base, sh):
        base[pl.ds(0, CONV_HALO), :] = jnp.zeros((CONV_HALO, LANE), F32)
        base[pl.ds(nr, SUBLANE), :] = jnp.zeros((SUBLANE, LANE), F32)

        def conv(w_ref, k_taps, b, n):
            acc = None
            for k in range(k_taps):
                r, q = _tap_slot(CONV_HALO - (k_taps - 1) + k)
                term = w_ref[pl.ds(k, 1), :] * sh[r, pl.ds(b + q, n), :]
                acc = term if acc is None else acc + term
            return acc

        def fill_a(b, c):
            base[pl.ds(CONV_HALO + b, CONV_CHUNK), :] = cg_ref[pl.ds(b, CONV_CHUNK), :] * ha_ref[pl.ds(b, CONV_CHUNK), :]
            return c

        _row_loop(t_rows, CONV_CHUNK, fill_a)
        _fill_shifted(sh, base, sorted({_tap_slot(CONV_HALO - (ka - 1) + k)[0] for k in range(ka)}), nr)

        def out_a(b, c):
            ya_ref[pl.ds(b, CONV_CHUNK), :] = (bg_ref[pl.ds(b, CONV_CHUNK), :] * conv(wa_ref, ka, b, CONV_CHUNK)).astype(BF16)
            return c

        _row_loop(t_rows, CONV_CHUNK, out_a)

        def fill_b(b, c):
            base[pl.ds(CONV_HALO + b, CONV_CHUNK), :] = (val_ref[pl.ds(b, CONV_CHUNK), :]
                                                          * jax.nn.sigmoid(gt_ref[pl.ds(b, CONV_CHUNK), :]))
            return c

        _row_loop(t_rows, CONV_CHUNK, fill_b)
        _fill_shifted(sh, base, range(SUBLANE), nr)

        def out_b(b, c):
            z_ref[pl.ds(b, CONV_CHUNK), :] = conv(wb_ref, kb, b, CONV_CHUNK) + bb_ref[...]
            return c

        _row_loop(t_rows, CONV_CHUNK, out_b)

    def col(g):
        return pl.BlockSpec((t_rows, LANE), lambda i, g=g: (0, g * nt + i))

    tile = lambda rows: pl.BlockSpec((rows, LANE), lambda i: (0, i))
    return _host_call(
        body, grid=(nt,),
        in_specs=[col(0), col(1), col(2), col(3), col(4), tile(ka), tile(kb), tile(1)],
        out_specs=[tile(t_rows), tile(t_rows)],
        out_shape=[jax.ShapeDtypeStruct((t_rows, wa_w), BF16), jax.ShapeDtypeStruct((t_rows, wa_w), F32)],
        scratch_shapes=[pltpu.VMEM((nr + SUBLANE, LANE), F32), pltpu.VMEM((SUBLANE, nr, LANE), F32)],
        args=(hin, hin, hin, hin, hin, wa, wb, bb), name="mix_conv_fwd", comm=comm)


def _ln_parts(z, lg, lb):
    mu = jnp.mean(z, axis=-1, keepdims=True)
    zc = z - mu
    rstd = lax.rsqrt(jnp.mean(zc * zc, axis=-1, keepdims=True) + LN_EPS)
    zh = zc * rstd
    return zh, rstd, zh * lg + lb


def _mix_ln_fwd(ya, z, lg, lb, tm, comm):
    t_rows, w = z.shape

    def body(ya_ref, z_ref, lg_ref, lb_ref, y_ref):
        _, _, ln = _ln_parts(z_ref[...], lg_ref[...], lb_ref[...])
        y_ref[:, pl.ds(0, w)] = ya_ref[...]
        y_ref[:, pl.ds(w, w)] = (ln * jax.nn.sigmoid(ln)).astype(BF16)

    blk = pl.BlockSpec((tm, w), lambda i: (i, 0))
    res, extra = _host_call(body, grid=(t_rows // tm,), in_specs=[blk, blk, _full((1, w)), _full((1, w))],
                            out_specs=[pl.BlockSpec((tm, 2 * w), lambda i: (i, 0))],
                            out_shape=[jax.ShapeDtypeStruct((t_rows, 2 * w), BF16)], args=(ya, z, lg, lb), name="mix_ln_fwd", comm=comm)
    return res[0], extra


def _out_proj(y, w_out, h0, g2, g3, tm, comm):
    t_rows, d = h0.shape

    def body(y_ref, w_ref, h0_ref, g2_ref, g3_ref, mix_ref, h1_ref, xn2_ref):
        mix = _dot_nn(y_ref[...], w_ref[...])
        mix_ref[...] = mix
        h1 = h0_ref[...] + (mix * _rstd(mix)) * g2_ref[...]
        h1_ref[...] = h1
        xn2_ref[...] = ((h1 * _rstd(h1)) * g3_ref[...]).astype(BF16)

    blk = pl.BlockSpec((tm, d), lambda i: (i, 0))
    return _host_call(
        body, grid=(t_rows // tm,), in_specs=[blk, _resident(w_out.shape), blk, _full((1, d)), _full((1, d))],
        out_specs=[blk, blk, blk],
        out_shape=[jax.ShapeDtypeStruct((t_rows, d), F32), jax.ShapeDtypeStruct((t_rows, d), F32),
                   jax.ShapeDtypeStruct((t_rows, d), BF16)],
        args=(y, w_out, h0, g2, g3), name="out_proj", comm=comm)


def _gate_up(xn2, wg_t, wu_t, tm, comm):
    t_rows, d = xn2.shape
    f = wg_t.shape[0]

    def body(x_ref, wg_ref, wu_ref, a_ref, u_ref, s_ref):
        xn = x_ref[...]
        for o, n in _chunks(f, N_CHUNK):
            a = _dot_nt(xn, wg_ref[pl.ds(o, n), :])
            u = _dot_nt(xn, wu_ref[pl.ds(o, n), :])
            a_ref[:, pl.ds(o, n)] = a.astype(BF16)
            u_ref[:, pl.ds(o, n)] = u.astype(BF16)
            s_ref[:, pl.ds(o, n)] = ((a * jax.nn.sigmoid(a)) * u).astype(BF16)

    blk = pl.BlockSpec((tm, f), lambda i: (i, 0))
    return _host_call(
        body, grid=(t_rows // tm,),
        in_specs=[pl.BlockSpec((tm, d), lambda i: (i, 0)), _resident((f, d)), _resident((f, d))],
        out_specs=[blk, blk, blk], out_shape=[jax.ShapeDtypeStruct((t_rows, f), BF16)] * 3,
        args=(xn2, wg_t, wu_t), name="gate_up", comm=comm)


def _down_loss(s, wd, h1, tgt, g4, tm, x0):
    t_rows, d = h1.shape
    f = wd.shape[0]

    def body(s_ref, w_ref, h1_ref, tgt_ref, g4_ref, dh2_ref, dff_ref, dg4_ref, loss_ref):
        i = pl.program_id(0)
        ff = _dot_nn(s_ref[...], w_ref[...])
        r4 = _rstd(ff)
        fh = ff * r4
        g4 = g4_ref[...]
        h2 = h1_ref[...] + fh * g4
        row = i * tm + lax.broadcasted_iota(jnp.int32, (tm, 1), 0)
        diff = jnp.where(row >= x0, h2 - tgt_ref[...], 0.0)
        dh2 = diff / d
        dh2_ref[...] = dh2
        dff_ref[...] = _rms_bwd(dh2 * g4, fh, r4).astype(BF16)
        _acc_rows(dg4_ref, dh2 * fh, i == 0)
        _acc_rows(loss_ref, diff * diff, i == 0)

    blk = pl.BlockSpec((tm, d), lambda i: (i, 0))
    res, _ = _host_call(
        body, grid=(t_rows // tm,),
        in_specs=[pl.BlockSpec((tm, f), lambda i: (i, 0)), _resident((f, d)), blk, blk, _full((1, d))],
        out_specs=[blk, blk, _full((1, d)), _full((1, d))],
        out_shape=[jax.ShapeDtypeStruct((t_rows, d), F32), jax.ShapeDtypeStruct((t_rows, d), BF16),
                   jax.ShapeDtypeStruct((1, d), F32), jax.ShapeDtypeStruct((1, d), F32)],
        args=(s, wd, h1, tgt, g4), name="down_loss")
    return res


def _bwd_down(dff, wd, a, u, tm, comm):
    t_rows, d = dff.shape
    f = wd.shape[0]

    def body(dff_ref, w_ref, a_ref, u_ref, da_ref, du_ref):
        dff_v = dff_ref[...]
        for o, n in _chunks(f, N_CHUNK):
            ds = _dot_nt(dff_v, w_ref[pl.ds(o, n), :])
            av = a_ref[:, pl.ds(o, n)].astype(F32)
            uv = u_ref[:, pl.ds(o, n)].astype(F32)
            sig = jax.nn.sigmoid(av)
            da_ref[:, pl.ds(o, n)] = (ds * uv * _silu_grad(av, sig)).astype(BF16)
            du_ref[:, pl.ds(o, n)] = (ds * (av * sig)).astype(BF16)

    blk = pl.BlockSpec((tm, f), lambda i: (i, 0))
    return _host_call(
        body, grid=(t_rows // tm,),
        in_specs=[pl.BlockSpec((tm, d), lambda i: (i, 0)), _resident((f, d)), blk, blk],
        out_specs=[blk, blk], out_shape=[jax.ShapeDtypeStruct((t_rows, f), BF16)] * 2,
        args=(dff, wd, a, u), name="bwd_down", comm=comm)


def _wgrad(a, b, name, after=()):
    d = b.shape[1]
    t_rows = b.shape[0]
    stacked = a.ndim == 3
    n = a.shape[-1]
    groups = a.shape[0] if stacked else 1
    steps = 1 if stacked else 2
    tile = max(t for t in range(LANE, min(n // steps, WGRAD_TILE_MAX) + 1, LANE) if n % t == 0)
    tiles = n // tile

    def body(a_ref, b_ref, o_ref):
        o_ref[...] = lax.dot_general(a_ref[...], b_ref[...], (((0,), (0,)), ((), ())),
                                     preferred_element_type=F32).astype(BF16)

    if stacked:
        a_spec = pl.BlockSpec((None, t_rows, tile), lambda g, i: (g, 0, i))
    else:
        a_spec = pl.BlockSpec((t_rows, tile), lambda g, i: (0, i))
    res, _ = _host_call(
        body, grid=(groups, tiles), in_specs=[a_spec, _resident((t_rows, d))],
        out_specs=[pl.BlockSpec((tile, d), lambda g, i: (g * tiles + i, 0))],
        out_shape=[jax.ShapeDtypeStruct((groups * n, d), BF16)], args=(a, b), name=name, after=after)
    return res[0]


def _bwd_ffn_in(da, du, wg_t, wu_t, h1, dh2, g3, tm, comm):
    t_rows, d = h1.shape
    f = wg_t.shape[0]

    def body(da_ref, du_ref, wg_ref, wu_ref, h1_ref, dh2_ref, g3_ref, dh1_ref, dg3_ref):
        dxn2 = _dot_nn(da_ref[...], wg_ref[...]) + _dot_nn(du_ref[...], wu_ref[...])
        h1 = h1_ref[...]
        r3 = _rstd(h1)
        h1h = h1 * r3
        _acc_rows(dg3_ref, dxn2 * h1h, pl.program_id(0) == 0)
        dh1_ref[...] = dh2_ref[...] + _rms_bwd(dxn2 * g3_ref[...], h1h, r3)

    blk = pl.BlockSpec((tm, d), lambda i: (i, 0))
    blkf = pl.BlockSpec((tm, f), lambda i: (i, 0))
    return _host_call(
        body, grid=(t_rows // tm,),
        in_specs=[blkf, blkf, _resident((f, d)), _resident((f, d)), blk, blk, _full((1, d))],
        out_specs=[blk, _full((1, d))],
        out_shape=[jax.ShapeDtypeStruct((t_rows, d), F32), jax.ShapeDtypeStruct((1, d), F32)],
        args=(da, du, wg_t, wu_t, h1, dh2, g3), name="bwd_ffn_in", comm=comm)


def _bwd_out_proj(dh1, mix, w_out, g2, tm, after):
    t_rows, d = dh1.shape

    def body(dh1_ref, mix_ref, w_ref, g2_ref, dmix_ref, dy_ref, dg2_ref):
        mix = mix_ref[...]
        r2 = _rstd(mix)
        mh = mix * r2
        dh1 = dh1_ref[...]
        _acc_rows(dg2_ref, dh1 * mh, pl.program_id(0) == 0)
        dmix = _rms_bwd(dh1 * g2_ref[...], mh, r2).astype(BF16)
        dmix_ref[...] = dmix
        dy_ref[...] = _dot_nt(dmix, w_ref[...])

    blk = pl.BlockSpec((tm, d), lambda i: (i, 0))
    res, _ = _host_call(
        body, grid=(t_rows // tm,), in_specs=[blk, blk, _resident(w_out.shape), _full((1, d))],
        out_specs=[blk, blk, _full((1, d))],
        out_shape=[jax.ShapeDtypeStruct((t_rows, d), BF16), jax.ShapeDtypeStruct((t_rows, d), F32),
                   jax.ShapeDtypeStruct((1, d), F32)],
        args=(dh1, mix, w_out, g2), name="bwd_out_proj", after=after)
    return res


def _mix_ln_bwd(z, dy, lg, lb, tm):
    t_rows, w = z.shape

    def body(z_ref, dyb_ref, lg_ref, lb_ref, dz_ref, dlg_ref, dlb_ref, dbb_ref):
        first = pl.program_id(0) == 0
        lg = lg_ref[...]
        zh, rstd, ln = _ln_parts(z_ref[...], lg, lb_ref[...])
        dln = dyb_ref[...] * _silu_grad(ln, jax.nn.sigmoid(ln))
        _acc_rows(dlg_ref, dln * zh, first)
        _acc_rows(dlb_ref, dln, first)
        dzh = dln * lg
        dz = rstd * (dzh - jnp.mean(dzh, axis=-1, keepdims=True) - zh * jnp.mean(dzh * zh, axis=-1, keepdims=True))
        dz_ref[...] = dz
        _acc_rows(dbb_ref, dz, first)

    blk = pl.BlockSpec((tm, w), lambda i: (i, 0))
    vec = _full((1, w))
    res, _ = _host_call(
        body, grid=(t_rows // tm,), in_specs=[blk, pl.BlockSpec((tm, w), lambda i: (i, 1)), vec, vec],
        out_specs=[blk, vec, vec, vec],
        out_shape=[jax.ShapeDtypeStruct((t_rows, w), F32)] + [jax.ShapeDtypeStruct((1, w), F32)] * 3,
        args=(z, dy, lg, lb), name="mix_ln_bwd")
    return res


def _mix_conv_bwd(hin, dy, dz, wa, wb, wa_w, comm):
    t_rows = hin.shape[0]
    nt = wa_w // LANE
    ka, kb = wa.shape[0], wb.shape[0]
    nr = CONV_HALO + t_rows
    kb_rows = -(-kb // SUBLANE) * SUBLANE

    def body(bg_ref, cg_ref, ha_ref, val_ref, gt_ref, dya_ref, dz_ref, wa_ref, wb_ref,
             dh_ref, dwa_ref, dwb_ref, base, sh, based, shd, tmp, wbc):
        zeros = lambda n: jnp.zeros((n, LANE), F32)
        base[pl.ds(0, CONV_HALO), :] = zeros(CONV_HALO)
        base[pl.ds(nr, SUBLANE), :] = zeros(SUBLANE)
        based[pl.ds(t_rows, CONV_HALO + SUBLANE), :] = zeros(CONV_HALO + SUBLANE)

        def fwd_slot(k_taps, k):
            return _tap_slot(CONV_HALO - (k_taps - 1) + k)

        def bwd_slot(k_taps, k):
            return _tap_slot(k_taps - 1 - k)

        def conv(w_ref, k_taps, src, slot, b, n):
            acc = None
            for k in range(k_taps):
                r, q = slot(k_taps, k)
                term = w_ref[pl.ds(k, 1), :] * src[r, pl.ds(b + q, n), :]
                acc = term if acc is None else acc + term
            return acc

        def by_residue(k_taps, slot):
            groups = {}
            for k in range(k_taps):
                r, q = slot(k_taps, k)
                groups.setdefault(r, []).append((k, q // SUBLANE))
            return groups

        def wgrad_loop(w_ref, k_taps):
            n_sub = WGRAD_ROWS // SUBLANE
            for k in range(k_taps):
                wbc[k] = jnp.broadcast_to(w_ref[pl.ds(k, 1), :], (SUBLANE, LANE))
            fwd, bwd = by_residue(k_taps, fwd_slot), by_residue(k_taps, bwd_slot)

            def window(src, r, taps, b):
                span = n_sub + max(qi for _, qi in taps)
                return [src[r, pl.ds(b + SUBLANE * i, SUBLANE), :] for i in range(span)]

            def step(b, accs):
                accs = list(accs)
                dv = [based[pl.ds(b + SUBLANE * j, SUBLANE), :] for j in range(n_sub)]
                for r, taps in fwd.items():
                    win = window(sh, r, taps, b)
                    for k, qi in taps:
                        t = dv[0] * win[qi]
                        for j in range(1, n_sub):
                            t = t + dv[j] * win[qi + j]
                        accs[k] = accs[k] + t
                outs = [None] * n_sub
                for r, taps in bwd.items():
                    win = window(shd, r, taps, b)
                    for k, qi in taps:
                        wk = wbc[k]
                        for j in range(n_sub):
                            term = wk * win[qi + j]
                            outs[j] = term if outs[j] is None else outs[j] + term
                for j in range(n_sub):
                    tmp[pl.ds(b + SUBLANE * j, SUBLANE), :] = outs[j]
                return tuple(accs)

            return _row_loop(t_rows, WGRAD_ROWS, step, tuple(zeros(SUBLANE) for _ in range(k_taps)))

        def store_taps(ref, accs, rows):
            for k, acc in enumerate(accs):
                ref[pl.ds(k, 1), :] = jnp.sum(acc, axis=0, keepdims=True)
            if rows > len(accs):
                ref[pl.ds(len(accs), rows - len(accs)), :] = zeros(rows - len(accs))

        def fill_a(b, c):
            sl = pl.ds(b, CONV_CHUNK)
            base[pl.ds(CONV_HALO + b, CONV_CHUNK), :] = cg_ref[sl, :] * ha_ref[sl, :]
            based[sl, :] = dya_ref[sl, :] * bg_ref[sl, :]
            return c

        _row_loop(t_rows, CONV_CHUNK, fill_a)
        _fill_shifted(sh, base, sorted({fwd_slot(ka, k)[0] for k in range(ka)}), nr)
        _fill_shifted(shd, based, sorted({bwd_slot(ka, k)[0] for k in range(ka)}), nr)

        def d_bgate(b, c):
            sl = pl.ds(b, CONV_CHUNK)
            dh_ref[0, sl, :] = (dya_ref[sl, :] * conv(wa_ref, ka, sh, fwd_slot, b, CONV_CHUNK)).astype(BF16)
            return c

        _row_loop(t_rows, CONV_CHUNK, d_bgate)
        store_taps(dwa_ref, wgrad_loop(wa_ref, ka), SUBLANE)

        def d_ch(b, c):
            sl = pl.ds(b, CONV_CHUNK)
            dua = tmp[sl, :]
            dh_ref[1, sl, :] = (dua * ha_ref[sl, :]).astype(BF16)
            dh_ref[2, sl, :] = (dua * cg_ref[sl, :]).astype(BF16)
            return c

        _row_loop(t_rows, CONV_CHUNK, d_ch)

        def fill_b(b, c):
            sl = pl.ds(b, CONV_CHUNK)
            base[pl.ds(CONV_HALO + b, CONV_CHUNK), :] = val_ref[sl, :] * jax.nn.sigmoid(gt_ref[sl, :])
            based[sl, :] = dz_ref[sl, :]
            return c

        _row_loop(t_rows, CONV_CHUNK, fill_b)
        _fill_shifted(sh, base, range(SUBLANE), nr)
        _fill_shifted(shd, based, range(SUBLANE), nr)
        store_taps(dwb_ref, wgrad_loop(wb_ref, kb), kb_rows)

        def d_glu(b, c):
            sl = pl.ds(b, CONV_CHUNK)
            dgg = tmp[sl, :]
            sig = jax.nn.sigmoid(gt_ref[sl, :])
            dh_ref[3, sl, :] = (dgg * sig).astype(BF16)
            dh_ref[4, sl, :] = (dgg * val_ref[sl, :] * (sig * (1.0 - sig))).astype(BF16)
            return c

        _row_loop(t_rows, CONV_CHUNK, d_glu)

    def col(g):
        return pl.BlockSpec((t_rows, LANE), lambda i, g=g: (0, g * nt + i))

    tile = lambda rows: pl.BlockSpec((rows, LANE), lambda i: (0, i))
    return _host_call(
        body, grid=(nt,),
        in_specs=[col(0), col(1), col(2), col(3), col(4), tile(t_rows), tile(t_rows), tile(ka), tile(kb)],
        out_specs=[pl.BlockSpec((5, t_rows, LANE), lambda i: (0, 0, i)), tile(SUBLANE), tile(kb_rows)],
        out_shape=[jax.ShapeDtypeStruct((5, t_rows, wa_w), BF16), jax.ShapeDtypeStruct((SUBLANE, wa_w), F32),
                   jax.ShapeDtypeStruct((kb_rows, wa_w), F32)],
        scratch_shapes=[pltpu.VMEM((nr + SUBLANE, LANE), F32), pltpu.VMEM((SUBLANE, nr, LANE), F32),
                        pltpu.VMEM((nr + SUBLANE, LANE), F32), pltpu.VMEM((SUBLANE, nr, LANE), F32),
                        pltpu.VMEM((t_rows, LANE), F32), pltpu.VMEM((kb_rows, SUBLANE, LANE), F32)],
        args=(hin, hin, hin, hin, hin, dy, dz, wa, wb), name="mix_conv_bwd", comm=comm)


def _bwd_in_proj(dh5, win_t, h0, dh1, g1, tm, comm):
    t_rows, d = h0.shape
    groups, _, w = dh5.shape

    def body(dh_ref, w_ref, h0_ref, dh1_ref, g1_ref, dh0_ref, dg1_ref):
        dxn1 = None
        for g in range(groups):
            part = _dot_nn(dh_ref[g], w_ref[pl.ds(g * w, w), :])
            dxn1 = part if dxn1 is None else dxn1 + part
        h0 = h0_ref[...]
        r1 = _rstd(h0)
        h0h = h0 * r1
        _acc_rows(dg1_ref, dxn1 * h0h, pl.program_id(0) == 0)
        dh0_ref[...] = dh1_ref[...] + _rms_bwd(dxn1 * g1_ref[...], h0h, r1)

    blk = pl.BlockSpec((tm, d), lambda i: (i, 0))
    return _host_call(
        body, grid=(t_rows // tm,),
        in_specs=[pl.BlockSpec((groups, tm, w), lambda i: (0, i, 0)), _resident(win_t.shape), blk, blk, _full((1, d))],
        out_specs=[blk, _full((1, d))],
        out_shape=[jax.ShapeDtypeStruct((t_rows, d), F32), jax.ShapeDtypeStruct((1, d), F32)],
        args=(dh5, win_t, h0, dh1, g1), name="bwd_in_proj", comm=comm)


def _reduce_small(smalls, d, after):
    (dmeta, dg1, dg2, dg3, dg4, dbb, dlg, dlb, lossv, dwa, dwb) = smalls
    half = d // 2
    kb_rows = dwb.shape[0]

    def body(dmeta_ref, dg1_ref, dg2_ref, dg3_ref, dg4_ref, dbb_ref, dlg_ref, dlb_ref, loss_ref, dwa_ref, dwb_ref,
             ptot_ref, pbuf, psib, chip_p, ps_send, ps_recv, pc_send, pc_recv):
        x, y, c = _mesh_pos()
        pbuf[...] = jnp.zeros_like(pbuf)
        pbuf[pl.ds(0, N_META), :] = dmeta_ref[...]
        for row, ref in ((16, dg1_ref), (17, dg2_ref), (18, dg3_ref), (19, dg4_ref)):
            pbuf[pl.ds(row, 1), :] = ref[...]
        pbuf[pl.ds(20, 1), pl.ds(0, half)] = dbb_ref[...]
        pbuf[pl.ds(20, 1), pl.ds(half, half)] = dlg_ref[...]
        pbuf[pl.ds(21, 1), pl.ds(0, half)] = dlb_ref[...]
        lv = loss_ref[...]
        pbuf[pl.ds(21, 1), pl.ds(half, half)] = lv[:, :half] + lv[:, half:]
        pbuf[pl.ds(24, SUBLANE), pl.ds(0, half)] = dwa_ref[...]
        pbuf[pl.ds(32, kb_rows), pl.ds(0, half)] = dwb_ref[...]
        to_sib = _remote(pbuf, psib, ps_send.at[0], ps_recv.at[0], (x, y, 1 - c))
        to_sib.start()
        to_sib.wait_recv()
        my_chip = 2 * x + y
        chip_p[my_chip] = pbuf[...] + psib[...]
        to_sib.wait_send()
        slot = chip_p.at[my_chip]
        cps = [_remote(slot, slot, pc_send.at[k], pc_recv.at[k], (*chip, c)) for k, chip in enumerate(_other_chips(x, y))]
        for cp in cps:
            cp.start()
        for cp in cps:
            cp.wait_recv()
        ptot_ref[...] = ((chip_p[0] + chip_p[1]) + chip_p[2]) + chip_p[3]
        for cp in cps:
            cp.wait_send()

    return _host_call(
        body, grid=(), in_specs=[VMEM] * 11, out_specs=[VMEM], out_shape=[jax.ShapeDtypeStruct((SMALL_ROWS, d), F32)],
        scratch_shapes=[pltpu.VMEM((SMALL_ROWS, d), F32), pltpu.VMEM((SMALL_ROWS, d), F32), pltpu.VMEM((4, SMALL_ROWS, d), F32),
                        pltpu.SemaphoreType.DMA((1,)), pltpu.SemaphoreType.DMA((1,)),
                        pltpu.SemaphoreType.DMA((3,)), pltpu.SemaphoreType.DMA((3,))],
        args=smalls, name="reduce_small", after=after)


def _adamw(w, g, m, v):
    m = ADAM_B1 * m + (1.0 - ADAM_B1) * g
    v = ADAM_B2 * v + (1.0 - ADAM_B2) * jnp.square(g)
    m_hat = m / (1.0 - ADAM_B1 ** ADAM_STEP)
    v_hat = v / (1.0 - ADAM_B2 ** ADAM_STEP)
    delta = -ADAM_LR * (m_hat / (jnp.sqrt(v_hat) + ADAM_EPS) + ADAM_WD * w)
    return delta, m, v


def _adam_big(g, pair, part, w, m, v, name):
    r, d = w.shape
    cols = d // ADAM_COL_BLOCKS

    def body(me_ref, g_ref, pair_ref, part_ref, w_ref, m_ref, v_ref, go_ref, d_ref, mo_ref, vo_ref):
        g = g_ref[...].astype(F32) + pair_ref[...].astype(F32)
        for k in range(3):
            g = g + part_ref[k].astype(F32)
        go_ref[...] = g
        d_ref[...], mo_ref[...], vo_ref[...] = _adamw(w_ref[...], g, m_ref[...], v_ref[...])

    blk = pl.BlockSpec((r, cols), lambda i, me_ref: (0, i))
    grid_spec = pltpu.PrefetchScalarGridSpec(
        num_scalar_prefetch=1, grid=(ADAM_COL_BLOCKS,),
        in_specs=[pl.BlockSpec((r, cols), lambda i, me_ref: (me_ref[0], i)),
                  pl.BlockSpec((None, r, cols), lambda i, me_ref: (0, 0, i)),
                  pl.BlockSpec((3, r, cols), lambda i, me_ref: (0, 0, i)), blk, blk, blk],
        out_specs=[blk, blk, blk, blk])
    me = jnp.reshape(_dev_index(*_mesh_pos()), (1,)).astype(jnp.int32)
    return pl.pallas_call(body, out_shape=[jax.ShapeDtypeStruct((r, d), F32)] * 4, grid_spec=grid_spec, name=name,
                          compiler_params=_cparams(1))(me, g, pair, part, w, m, v)


def _adam_small(gs, ws, ms, vs):
    n = len(gs)

    def body(*refs):
        ins, outs = refs[:4 * n], refs[4 * n:]
        for i in range(n):
            g = ins[i][...]
            delta, m, v = _adamw(ins[n + i][...], g, ins[2 * n + i][...], ins[3 * n + i][...])
            outs[i][...] = delta
            outs[n + i][...] = m
            outs[2 * n + i][...] = v

    shapes = [jax.ShapeDtypeStruct(w.shape, F32) for w in ws]
    return pl.pallas_call(body, out_shape=shapes * 3, name="adam_small", compiler_params=_cparams())(*gs, *ws, *ms, *vs)


def kernel(x, meta_tokens, pre_mix_norm, w_in, conv_a_w, conv_b_w, conv_b_bias, ln_b_gain, ln_b_bias, w_out, post_mix_norm, pre_ffn_norm, w_gate, w_up, w_down, post_ffn_norm, loss_target, m_meta_tokens, m_pre_mix_norm, m_w_in, m_conv_a_w, m_conv_b_w, m_conv_b_bias, m_ln_b_gain, m_ln_b_bias, m_w_out, m_post_mix_norm, m_pre_ffn_norm, m_w_gate, m_w_up, m_w_down, m_post_ffn_norm, v_meta_tokens, v_pre_mix_norm, v_w_in, v_conv_a_w, v_conv_b_w, v_conv_b_bias, v_ln_b_gain, v_ln_b_bias, v_w_out, v_post_mix_norm, v_pre_ffn_norm, v_w_gate, v_w_up, v_w_down, v_post_ffn_norm):
    _, seq, d = x.shape
    ka, ca_loc = conv_a_w.shape[1:]
    kb, cb_loc = conv_b_w.shape[1:]
    wa_w = ca_loc * N_DEV
    assert cb_loc == ca_loc and wa_w % LANE == 0 and w_in.shape[2] * N_DEV == 5 * wa_w
    pad = (-(N_META + seq)) % ROW_ALIGN
    x0 = pad + N_META
    t_rows = x0 + seq
    assert t_rows % (N_ROW_BLOCKS * BF16_ROWS) == 0 and t_rows % CONV_CHUNK == 0 and d % LANE == 0
    tm = t_rows // N_ROW_BLOCKS
    me = _dev_index(*_mesh_pos())

    def as_rows(w_in_like, w_out_like, w_gate_like, w_up_like, w_down_like):
        return (w_in_like[0].T, w_out_like[0], w_gate_like[0].T, w_up_like[0].T, w_down_like[0])

    w_loc = as_rows(w_in, w_out, w_gate, w_up, w_down)
    rows = [w.shape[0] for w in w_loc]
    assert all(r % ADD_CHUNK == 0 for r in rows)
    P_IN, P_OUT, P_GATE, P_UP, P_DOWN = range(N_BIG)

    sm = jnp.zeros((SM_ROWS, LANE), F32)
    sm = sm.at[0:N_META, :].set(meta_tokens)
    sm = sm.at[16:16 + ka, 0:ca_loc].set(conv_a_w[0])
    sm = sm.at[24:24 + kb, 0:cb_loc].set(conv_b_w[0])
    wl, wfull, sm_all, h0, tgt = _gather_first(w_loc, sm, [(P_IN, 0, rows[P_IN])], x[0], loss_target[0], t_rows, x0)
    wa =jnp.transpose(sm_all[:, 16:16 + ka, 0:ca_loc], (1, 0, 2)).reshape(ka, wa_w)
    wb = jnp.transpose(sm_all[:, 24:24 + kb, 0:cb_loc], (1, 0, 2)).reshape(kb, wa_w)

    later = (P_OUT, P_GATE, P_UP, P_DOWN)
    sems, wl, started, _ = _gather_start(wl, wfull, later, rows)
    for p, arr in zip(later, started):
        wfull[p] = arr

    def arrived(p, after, name):
        nonlocal wl
        wl, wfull[p] = _gather_wait(wl, wfull[p], sems[later.index(p)], after, rows[p], name)
        return _forward_comm(wfull[p], rows[p])

    (xn1, hin), _ = _in_proj(h0, pre_mix_norm, wfull[P_IN], tm, None)
    (ya, z), (wfull[P_OUT],) = _mix_conv_fwd(hin, wa, wb, conv_b_bias, wa_w, arrived(P_OUT, hin, "gather_wait_out"))
    y, (wfull[P_GATE],) = _mix_ln_fwd(ya, z, ln_b_gain, ln_b_bias, tm, arrived(P_GATE, z, "gather_wait_gate"))
    (mix, h1, xn2), (wfull[P_UP],) = _out_proj(y, wfull[P_OUT], h0, post_mix_norm, pre_ffn_norm, tm,
                                               arrived(P_UP, y, "gather_wait_up"))
    (a, u, s), (wfull[P_DOWN],) = _gate_up(xn2, wfull[P_GATE], wfull[P_UP], tm, arrived(P_DOWN, xn2, "gather_wait_down"))
    dh2, dff, dg4, lossv = _down_loss(s, wfull[P_DOWN], h1, tgt, post_ffn_norm, tm, x0)

    gwd = _wgrad(s, dff, "wgrad_down")
    (da, du), (pair_d,) = _bwd_down(dff, wfull[P_DOWN], a, u, tm, _pair_comm(gwd, rows[P_DOWN]))
    (flight_d,), token = _chip_start([_pair_sum(gwd, pair_d, rows[P_DOWN], "pair_sum_down")], "chip_start_down")
    gwg = _wgrad(da, xn2, "wgrad_gate", [token])
    gwu = _wgrad(du, xn2, "wgrad_up")
    (dh1, dg3), (pair_g, pair_u) = _bwd_ffn_in(da, du, wfull[P_GATE], wfull[P_UP], h1, dh2, pre_ffn_norm, tm,
                                               _merge_comms([_pair_comm(gwg, rows[P_GATE]), _pair_comm(gwu, rows[P_UP])]))
    (flight_g, flight_u), token = _chip_start([_pair_sum(gwg, pair_g, rows[P_GATE], "pair_sum_gate"),
                                               _pair_sum(gwu, pair_u, rows[P_UP], "pair_sum_up")], "chip_start_gate_up")
    dmix, dy, dg2 = _bwd_out_proj(dh1, mix, wfull[P_OUT], post_mix_norm, tm, [token])
    gwo = _wgrad(y, dmix, "wgrad_out")
    dz, dlg, dlb, dbb = _mix_ln_bwd(z, dy, ln_b_gain, ln_b_bias, tm)
    (dh5, dwa, dwb), (pair_o,) = _mix_conv_bwd(hin, dy, dz, wa, wb, wa_w, _pair_comm(gwo, rows[P_OUT]))
    (flight_o,), token = _chip_start([_pair_sum(gwo, pair_o, rows[P_OUT], "pair_sum_out")], "chip_start_out")
    gwi = _wgrad(dh5, xn1, "wgrad_in", [token])
    (dh0, dg1), (pair_i,) = _bwd_in_proj(dh5, wfull[P_IN], h0, dh1, pre_mix_norm, tm, _pair_comm(gwi, rows[P_IN]))
    grad_x = dh0[x0:][None]
    dmeta = dh0[x0 - N_META:x0]
    (ptot,), _ = _reduce_small((dmeta, dg1, dg2, dg3, dg4, dbb, dlg, dlb, lossv, dwa, dwb), d, [])
    (flight_i,), token = _chip_start([_pair_sum(gwi, pair_i, rows[P_IN], "pair_sum_in", [ptot])], "chip_start_in")

    def landed(flight, after, tag):
        sems_p, sums, land = flight
        return _chip_wait(sums, land, sems_p, after, "chip_wait_" + tag)

    part_d = landed(flight_d, token, "down")
    part_g = landed(flight_g, token, "gate")
    part_u = landed(flight_u, token, "up")
    part_o = landed(flight_o, token, "out")

    half = d // 2
    loss = (0.5 / d) * jnp.sum(ptot[21, half:])
    g_meta = lax.dynamic_slice(ptot, (0, me * (d // N_DEV)), (N_META, d // N_DEV))
    g_small = [g_meta, ptot[16:17], lax.dynamic_slice(ptot, (24, me * ca_loc), (ka, ca_loc))[None],
               lax.dynamic_slice(ptot, (32, me * cb_loc), (kb, cb_loc))[None],
               ptot[20:21, :half], ptot[20:21, half:], ptot[21:22, :half], ptot[17:18], ptot[18:19], ptot[19:20]]
    w_small = [meta_tokens, pre_mix_norm, conv_a_w, conv_b_w, conv_b_bias, ln_b_gain, ln_b_bias, post_mix_norm,
               pre_ffn_norm, post_ffn_norm]
    m_small = [m_meta_tokens, m_pre_mix_norm, m_conv_a_w, m_conv_b_w, m_conv_b_bias, m_ln_b_gain, m_ln_b_bias,
               m_post_mix_norm, m_pre_ffn_norm, m_post_ffn_norm]
    v_small = [v_meta_tokens, v_pre_mix_norm, v_conv_a_w, v_conv_b_w, v_conv_b_bias, v_ln_b_gain, v_ln_b_bias,
               v_post_mix_norm, v_pre_ffn_norm, v_post_ffn_norm]
    small = _adam_small(g_small, w_small, m_small, v_small)
    n_small = len(w_small)
    d_small, nm_small, nv_small = small[:n_small], small[n_small:2 * n_small], small[2 * n_small:]

    m_loc = as_rows(m_w_in, m_w_out, m_w_gate, m_w_up, m_w_down)
    v_loc = as_rows(v_w_in, v_w_out, v_w_gate, v_w_up, v_w_down)
    full_grads = {P_IN: gwi, P_OUT: gwo, P_GATE: gwg, P_UP: gwu, P_DOWN: gwd}
    pairs = {P_IN: pair_i, P_OUT: pair_o, P_GATE: pair_g, P_UP: pair_u, P_DOWN: pair_d}
    parts = {P_OUT: part_o, P_GATE: part_g, P_UP: part_u, P_DOWN: part_d}
    names = {P_IN: "w_in", P_OUT: "w_out", P_GATE: "w_gate", P_UP: "w_up", P_DOWN: "w_down"}
    bigs = {}
    res = None
    for p in (P_DOWN, P_GATE, P_UP, P_OUT, P_IN):
        if p == P_IN:
            parts[p] = landed(flight_i, res[1], "in")
        res = _adam_big(full_grads[p], pairs[p], parts[p], w_loc[p], m_loc[p], v_loc[p], "adam_" + names[p])
        bigs[names[p]] = [(o.T if p in (P_IN, P_GATE, P_UP) else o)[None] for o in res]

    def ordered(pick_small, pick_big):
        sm_it = iter(range(n_small))
        out = []
        for name in ("s", "s", "w_in", "s", "s", "s", "s", "s", "w_out", "s", "s", "w_gate", "w_up", "w_down", "s"):
            out.append(pick_small(next(sm_it)) if name == "s" else pick_big(name))
        return out

    grads = ordered(lambda i: g_small[i], lambda n: bigs[n][0])
    deltas = ordered(lambda i: d_small[i], lambda n: bigs[n][1])
    new_m = ordered(lambda i: nm_small[i], lambda n: bigs[n][2])
    new_v = ordered(lambda i: nv_small[i], lambda n: bigs[n][3])
    return (loss, grad_x, *grads, *deltas, *new_m, *new_v)
```

```python
import jax
import jax.numpy as jnp
from jax import lax
from jax.experimental import pallas as pl
from jax.experimental.pallas import tpu as pltpu

F32 = jnp.float32
BF16 = jnp.bfloat16
MESH = pl.DeviceIdType.MESH

N_META = 16
N_DEV = 8
RMS_EPS = 1e-6
LN_EPS = 1e-5
ADAM_LR = 0.001
ADAM_B1 = 0.9
ADAM_B2 = 0.999
ADAM_EPS = 1e-08
ADAM_WD = 0.01
ADAM_STEP = 10

LANE = 128
SUBLANE = 8
BF16_ROWS = 16
ROW_ALIGN = 128
N_ROW_BLOCKS = 4
CONV_HALO = 32
CONV_CHUNK = 64
WGRAD_ROWS = 32
N_CHUNK = 512
WGRAD_TILE_MAX = 1408
ADD_CHUNK = 32
ADAM_COL_BLOCKS = 4
COPY_PIECES = 4
V7X_VMEM_BYTES = 64 * 1024 * 1024
VMEM_LIMIT = V7X_VMEM_BYTES - 6 * 1024 * 1024
SMALL_ROWS = 64
SM_ROWS = 56
N_BIG = 5

ANY = pl.BlockSpec(memory_space=pl.ANY)
VMEM = pl.BlockSpec(memory_space=pltpu.VMEM)


def _cparams(n_grid_axes=0):
    sem = ("arbitrary",) * n_grid_axes if n_grid_axes else None
    return pltpu.CompilerParams(dimension_semantics=sem, vmem_limit_bytes=VMEM_LIMIT)


def _mesh_pos():
    return lax.axis_index("x"), lax.axis_index("y"), lax.axis_index("c")


def _dev_index(px, py, pc):
    return 4 * px + 2 * py + pc


def _other_chips(x, y):
    return [(1 - x, y), (x, 1 - y), (1 - x, 1 - y)]


def _full(shape):
    return pl.BlockSpec(shape, lambda *_: (0,) * len(shape))


def _resident(shape):
    return pl.BlockSpec(shape, lambda *_: (0,) * len(shape), pipeline_mode=pl.Buffered(1))


def _dot_nt(a, w):
    return lax.dot_general(a, w, (((1,), (1,)), ((), ())), preferred_element_type=F32)


def _dot_nn(a, w):
    return jnp.dot(a, w, preferred_element_type=F32)


def _chunks(n, c):
    out, o = [], 0
    while o < n:
        out.append((o, min(c, n - o)))
        o += c
    return out


def _rstd(h):
    return lax.rsqrt(jnp.mean(h * h, axis=-1, keepdims=True) + RMS_EPS)


def _rms_bwd(dyh, yh, r):
    return r * (dyh - yh * jnp.mean(dyh * yh, axis=-1, keepdims=True))


def _silu_grad(a, sig):
    return sig * (1.0 + a * (1.0 - sig))


def _acc_rows(ref, val, first):
    s = jnp.sum(val, axis=0, keepdims=True)

    @pl.when(first)
    def _():
        ref[...] = s

    @pl.when(jnp.logical_not(first))
    def _():
        ref[...] += s


def _row_loop(t_rows, chunk, fn, carry=None):
    def step(i, c):
        return fn(pl.multiple_of(i * chunk, chunk), c)

    return lax.fori_loop(0, t_rows // chunk, step, carry)


def _remote(src, dst, send_sem, recv_sem, to):
    return pltpu.make_async_remote_copy(src_ref=src, dst_ref=dst, send_sem=send_sem, recv_sem=recv_sem,
                                        device_id=to, device_id_type=MESH)


class _SplitRemote:
    def __init__(self, src, dst, send_sem, recv_sem, to, rows, n_chunks):
        units = rows // BF16_ROWS
        n_chunks = max(1, min(n_chunks, units))
        sizes = [(units // n_chunks + (i < units % n_chunks)) * BF16_ROWS for i in range(n_chunks)]
        self.whole = _remote(src, dst, send_sem, recv_sem, to)
        self.parts, o = [], 0
        for n in sizes:
            self.parts.append(_remote(src.at[pl.ds(o, n), :], dst.at[pl.ds(o, n), :], send_sem, recv_sem, to))
            o += n

    def start(self):
        for cp in self.parts:
            cp.start()

    def wait_recv(self):
        self.whole.wait_recv()

    def wait_send(self):
        self.whole.wait_send()


class _Comm:
    def __init__(self, inputs, out_shapes, aliases, scratch, start, finish):
        self.inputs, self.out_shapes, self.aliases, self.scratch = list(inputs), list(out_shapes), dict(aliases), list(scratch)
        self.start, self.finish = start, finish


def _merge_comms(comms):
    inputs, out_shapes, aliases, scratch, spans = [], [], {}, [], []
    for cm in comms:
        spans.append((len(inputs), len(out_shapes), len(scratch), cm))
        aliases.update({len(inputs) + k: len(out_shapes) + v for k, v in cm.aliases.items()})
        inputs += cm.inputs
        out_shapes += cm.out_shapes
        scratch += cm.scratch

    def run(which):
        def fn(ins, outs, scr):
            for i0, o0, s0, cm in spans:
                getattr(cm, which)(ins[i0:i0 + len(cm.inputs)], outs[o0:o0 + len(cm.out_shapes)], scr[s0:s0 + len(cm.scratch)])
        return fn

    return _Comm(inputs, out_shapes, aliases, scratch, run("start"), run("finish"))


def _host_call(body, *, grid, in_specs, out_specs, out_shape, args, name, scratch_shapes=(), comm=None, after=()):
    if comm is None:
        comm = _Comm([], [], {}, [], lambda *_: None, lambda *_: None)
    n_in, n_out, n_scr = len(args), len(out_shape), len(scratch_shapes)
    c_in, c_out = len(comm.inputs), len(comm.out_shapes)
    n_after = len(after)

    def hosted(*refs):
        ins, c_ins = refs[:n_in], refs[n_in:n_in + c_in]
        o0 = n_in + c_in + n_after
        outs, c_outs = refs[o0:o0 + n_out], refs[o0 + n_out:o0 + n_out + c_out]
        s0 = o0 + n_out + c_out
        scr, c_scr = refs[s0:s0 + n_scr], refs[s0 + n_scr:]
        if not grid:
            comm.start(c_ins, c_outs, c_scr)
            body(*ins, *outs, *scr)
            comm.finish(c_ins, c_outs, c_scr)
            return
        first = last = None
        for a, n in enumerate(grid):
            f, l = pl.program_id(a) == 0, pl.program_id(a) == n - 1
            first = f if first is None else jnp.logical_and(first, f)
            last = l if last is None else jnp.logical_and(last, l)

        @pl.when(first)
        def _():
            comm.start(c_ins, c_outs, c_scr)

        body(*ins, *outs, *scr)

        @pl.when(last)
        def _():
            comm.finish(c_ins, c_outs, c_scr)

    res = pl.pallas_call(
        hosted, grid=grid, in_specs=list(in_specs) + [ANY] * (c_in + n_after), out_specs=list(out_specs) + [ANY] * c_out,
        out_shape=list(out_shape) + comm.out_shapes, scratch_shapes=list(scratch_shapes) + comm.scratch,
        input_output_aliases={n_in + k: n_out + v for k, v in comm.aliases.items()},
        name=name, compiler_params=_cparams(len(grid)))(*args, *comm.inputs, *after)
    return list(res[:n_out]), list(res[n_out:])


GATHER_SEMS = 10
D2D_CHUNKS = 8


class _Gather:
    def __init__(self, jobs, rows, lo, src_ref, dests, send_sems, recv_sems):
        x, y, c = _mesh_pos()
        me, sib = (x, y, c), (x, y, 1 - c)
        nx, ny, dg = (1 - x, y, c), (x, 1 - y, c), (1 - x, 1 - y, c)
        self.relayed, self.direct, self.relay, self.to_sib, self.sib_fwd = [], [], [], [], []
        for n, (p, r0, nr) in enumerate(jobs):
            assert nr % (2 * BF16_ROWS) == 0
            half = nr // 2

            def rows_of(dev, h, p=p, r0=r0, nr=nr, half=half):
                off, cnt = (r0, nr) if h is None else (r0 + h * half, half)
                return dests[p].at[pl.ds(pl.multiple_of(_dev_index(*dev) * rows[p] + off, BF16_ROWS), cnt), :]

            def mine(h, p=p, r0=r0, nr=nr, half=half):
                off, cnt = (r0, nr) if h is None else (r0 + h * half, half)
                return src_ref.at[pl.ds(lo[p] + off, cnt), :]

            sem = lambda k, n=n: (send_sems.at[GATHER_SEMS * n + k], recv_sems.at[GATHER_SEMS * n + k])
            self.relayed.append([_remote(mine(0), rows_of(me, 0), *sem(0), nx), _remote(mine(1), rows_of(me, 1), *sem(3), ny)])
            self.direct.append([_remote(mine(1), rows_of(me, 1), *sem(1), nx), _remote(mine(0), rows_of(me, 0), *sem(2), ny)])
            self.relay.append([_remote(rows_of(nx, 0), rows_of(nx, 0), *sem(4), ny), _remote(rows_of(ny, 1), rows_of(ny, 1), *sem(5), nx)])
            self.to_sib.append(_SplitRemote(mine(None), rows_of(me, None), *sem(6), sib, nr, D2D_CHUNKS))
            self.sib_fwd.append([_SplitRemote(rows_of(dev, None), rows_of(dev, None), *sem(7 + i), sib, nr, D2D_CHUNKS)
                                 for i, dev in enumerate((nx, ny, dg))])

    def start(self):
        for group in (self.relayed, self.direct):
            for cps in group:
                for cp in cps:
                    cp.start()
        for cp in self.to_sib:
            cp.start()

    def mid(self):
        for first, relay in zip(self.relayed, self.relay):
            for arrived, onward in zip(first, relay):
                arrived.wait_recv()
                onward.start()

    def finish(self):
        for direct, relay, fwd in zip(self.direct, self.relay, self.sib_fwd):
            for k in range(2):
                direct[k].wait_recv()
                fwd[k].start()
            for cp in relay:
                cp.wait_recv()
            fwd[2].start()
        for n in range(len(self.to_sib)):
            self.to_sib[n].wait_recv()
            for cp in self.sib_fwd[n]:
                cp.wait_recv()
            for cp in self.relayed[n] + self.direct[n] + self.relay[n] + [self.to_sib[n]] + self.sib_fwd[n]:
                cp.wait_send()


HBM = pl.BlockSpec(memory_space=pltpu.HBM)
SEM = pl.BlockSpec(memory_space=pltpu.SEMAPHORE)
FLOWS = pltpu.SideEffectType.DATAFLOW_SIDE_EFFECTING


def _in_hbm(a):
    return pltpu.with_memory_space_constraint(a, pltpu.HBM)


def _gather_start(wl, dests, ps, rows):
    lo = [sum(rows[:p]) for p in range(N_BIG)]
    n = len(ps)

    def body(*refs):
        wl_ref, dest_refs = refs[0], refs[1:1 + n]
        sends, recvs = refs[1 + n:1 + 2 * n], refs[1 + 2 * n:1 + 3 * n]
        token = refs[-1]
        x, y, c = _mesh_pos()
        jme = _dev_index(x, y, c)
        for i, p in enumerate(ps):
            mine = dest_refs[i].at[pl.ds(pl.multiple_of(jme * rows[p], BF16_ROWS), rows[p]), :]
            for chip in _other_chips(x, y):
                _remote(wl_ref.at[pl.ds(lo[p], rows[p]), :], mine, sends[i], recvs[i], (*chip, c)).start()
        token[...] = jnp.zeros_like(token)

    thru = [pltpu.HBM(wl.shape, wl.dtype)] + [pltpu.HBM(dests[p].shape, BF16) for p in ps]
    res = pl.pallas_call(
        body, name="gather_start",
        out_shape=tuple([pltpu.SemaphoreType.DMA(())] * (2 * n) + thru + [jax.ShapeDtypeStruct((SUBLANE, LANE), F32)]),
        in_specs=[HBM] * (1 + n), out_specs=tuple([SEM] * (2 * n) + [HBM] * (1 + n) + [VMEM]),
        input_output_aliases={i: 2 * n + i for i in range(1 + n)},
        compiler_params=pltpu.CompilerParams(has_side_effects=FLOWS))(_in_hbm(wl), *[_in_hbm(dests[p]) for p in ps])
    sems = [(res[i], res[n + i]) for i in range(n)]
    return sems, res[2 * n], list(res[2 * n + 1:3 * n + 1]), res[-1]


def _gather_wait(wl, dest, sems, after, r, name):
    def body(wl_ref, dest_ref, send_sem, recv_sem, after_ref, wl_out, dest_out):
        x, y, c = _mesh_pos()
        three = dest_ref.at[pl.ds(0, 3 * r), :]
        cp = _remote(three, three, send_sem, recv_sem, (x, y, 1 - c))
        cp.wait_send()
        cp.wait_recv()

    res = pl.pallas_call(
        body, name=name, out_shape=(pltpu.HBM(wl.shape, wl.dtype), pltpu.HBM(dest.shape, dest.dtype)),
        in_specs=[HBM, HBM, SEM, SEM, ANY], out_specs=(HBM, HBM), input_output_aliases={0: 0, 1: 1},
        compiler_params=pltpu.CompilerParams(has_side_effects=FLOWS))(wl, dest, sems[0], sems[1], after)
    return res[0], res[1]


def _forward_comm(dest, r):
    def descs(ins, outs, scr):
        x, y, c = _mesh_pos()
        cps = []
        for k, chip in enumerate(_other_chips(x, y)):
            blk = outs[0].at[pl.ds(pl.multiple_of(_dev_index(*chip, c) * r, BF16_ROWS), r), :]
            cps.append(_SplitRemote(blk, blk, scr[0].at[k], scr[1].at[k], (x, y, 1 - c), r, D2D_CHUNKS))
        return cps

    def start(ins, outs, scr):
        for cp in descs(ins, outs, scr):
            cp.start()

    def finish(ins, outs, scr):
        cps = descs(ins, outs, scr)
        for cp in cps:
            cp.wait_recv()
        for cp in cps:
            cp.wait_send()

    return _Comm([dest], [jax.ShapeDtypeStruct(dest.shape, dest.dtype)], {0: 0},
                 [pltpu.SemaphoreType.DMA((3,)), pltpu.SemaphoreType.DMA((3,))], start, finish)


def _forward_now(dest, r, name):
    _, (dest,) = _host_call(lambda: None, grid=(), in_specs=[], out_specs=[], out_shape=[], args=(), name=name,
                            comm=_forward_comm(dest, r))
    return dest


def _pair_comm(g, r):
    d = g.shape[1]

    def descs(ins, outs, scr):
        x, y, c = _mesh_pos()
        chips = [(x, y)] + _other_chips(x, y)
        return [_SplitRemote(ins[0].at[pl.ds(pl.multiple_of(_dev_index(*chip, 1 - c) * r, BF16_ROWS), r), :], outs[0].at[k],
                             scr[0].at[k], scr[1].at[k], (x, y, 1 - c), r, D2D_CHUNKS) for k, chip in enumerate(chips)]

    def start(ins, outs, scr):
        for cp in descs(ins, outs, scr):
            cp.start()

    def finish(ins, outs, scr):
        cps = descs(ins, outs, scr)
        for cp in cps:
            cp.wait_recv()
        for cp in cps:
            cp.wait_send()

    comm = _Comm([g], [jax.ShapeDtypeStruct((4, r, d), BF16)], {},
                 [pltpu.SemaphoreType.DMA((4,)), pltpu.SemaphoreType.DMA((4,))], start, finish)
    return comm


def _pair_sum(g, pair, r, name, after=()):
    d = g.shape[1]

    def body(g_ref, p_ref, *rest):
        o_ref, gbuf, pbuf, sems = rest[len(after):]
        x, y, c = _mesh_pos()
        loads = [pltpu.make_async_copy(p_ref.at[pl.ds(1, 3)], pbuf, sems.at[3])]
        for k, chip in enumerate(_other_chips(x, y)):
            j = _dev_index(*chip, c)
            loads.append(pltpu.make_async_copy(g_ref.at[pl.ds(pl.multiple_of(j * r, BF16_ROWS), r), :], gbuf.at[k], sems.at[k]))
        for cp in loads:
            cp.start()
        for cp in loads:
            cp.wait()
        for k in range(3):
            o_ref[k] = (gbuf[k].astype(F32) + pbuf[k].astype(F32)).astype(BF16)

    return pl.pallas_call(
        body, out_shape=jax.ShapeDtypeStruct((3, r, d), BF16), in_specs=[ANY] * (2 + len(after)), out_specs=VMEM,
        scratch_shapes=[pltpu.VMEM((3, r, d), BF16), pltpu.VMEM((3, r, d), BF16), pltpu.SemaphoreType.DMA((4,))],
        name=name, compiler_params=_cparams())(g, pair, *after)


def _chip_start(sums, name):
    n = len(sums)

    def body(*refs):
        srcs, lands = refs[:n], refs[n:2 * n]
        sends, recvs = refs[2 * n:3 * n], refs[3 * n:4 * n]
        x, y, c = _mesh_pos()
        for i in range(n):
            for k, chip in enumerate(_other_chips(x, y)):
                _remote(srcs[i].at[k], lands[i].at[k], sends[i], recvs[i], (*chip, c)).start()
        refs[-1][...] = jnp.zeros_like(refs[-1])

    zones = [pltpu.HBM(s.shape, s.dtype) for s in sums]
    res = pl.pallas_call(
        body, name=name,
        out_shape=tuple([pltpu.SemaphoreType.DMA(())] * (2 * n) + zones + zones + [jax.ShapeDtypeStruct((SUBLANE, LANE), F32)]),
        in_specs=[HBM] * (2 * n), out_specs=tuple([SEM] * (2 * n) + [HBM] * (2 * n) + [VMEM]),
        input_output_aliases={i: 2 * n + i for i in range(2 * n)},
        compiler_params=pltpu.CompilerParams(has_side_effects=FLOWS))(
            *[_in_hbm(s) for s in sums], *[_in_hbm(lax.empty(s.shape, s.dtype)) for s in sums])
    flights = [((res[i], res[n + i]), res[2 * n + i], res[3 * n + i]) for i in range(n)]
    return flights, res[-1]


def _chip_wait(sums, land, sems, after, name):
    def body(sums_ref, land_ref, send_sem, recv_sem, after_ref, sums_out, land_out):
        x, y, c = _mesh_pos()
        cp = _remote(sums_ref, land_ref, send_sem, recv_sem, (x, y, 1 - c))
        cp.wait_send()
        cp.wait_recv()

    res = pl.pallas_call(
        body, name=name, out_shape=(pltpu.HBM(sums.shape, sums.dtype), pltpu.HBM(land.shape, land.dtype)),
        in_specs=[HBM, HBM, SEM, SEM, ANY], out_specs=(HBM, HBM), input_output_aliases={0: 0, 1: 1},
        compiler_params=pltpu.CompilerParams(has_side_effects=FLOWS))(sums, land, sems[0], sems[1], after)
    return res[1]


class _CopyThrough:
    def __init__(self, src_ref, dst_ref, dst_row0, n_rows, buf, sem_in, sem_out):
        rc = n_rows // COPY_PIECES
        piece = lambda ref, o: ref.at[pl.ds(o, rc), :]
        self.loads = [pltpu.make_async_copy(piece(src_ref, k * rc), piece(buf, k * rc), sem_in) for k in range(COPY_PIECES)]
        self.stores = [pltpu.make_async_copy(piece(buf, k * rc), piece(dst_ref, dst_row0 + k * rc), sem_out) for k in range(COPY_PIECES)]
        self.all_in = pltpu.make_async_copy(src_ref, buf, sem_in)
        self.all_out = pltpu.make_async_copy(buf, dst_ref.at[pl.ds(dst_row0, n_rows), :], sem_out)

    def load(self):
        for cp in self.loads:
            cp.start()

    def store(self):
        self.all_in.wait()
        for cp in self.stores:
            cp.start()

    def done(self):
        self.all_out.wait()


def _gather_first(shards, sm, jobs, x2, tgt2, t_rows, x0):
    d = shards[0].shape[1]
    rows = [w.shape[0] for w in shards]
    lo = [sum(rows[:p]) for p in range(N_BIG)]
    n_sems = GATHER_SEMS * len(jobs)
    seq = x2.shape[0]
    assert x0 == ROW_ALIGN and seq % ROW_ALIGN == 0 and d == N_DEV * LANE

    def body(s0, s1, s2, s3, s4, sm_ref, x_ref, tgt_ref, wl_ref, o0, o1, o2, o3, o4, sa_ref, h0_ref, tp_ref,
             wl_v, x_v, tgt_v, heads_v, sa_v, send_sems, recv_sems, ssend, srecv, local_sems, sems_in, sems_out):
        dests = (o0, o1, o2, o3, o4)
        x, y, c = _mesh_pos()
        me = (x, y, c)
        jme = _dev_index(*me)
        padded = [_CopyThrough(x_ref, h0_ref, x0, seq, x_v, sems_in.at[0], sems_out.at[0]),
                  _CopyThrough(tgt_ref, tp_ref, x0, seq, tgt_v, sems_in.at[1], sems_out.at[1])]
        for cp in padded:
            cp.load()
        for p, ref in enumerate((s0, s1, s2, s3, s4)):
            wl_v[pl.ds(lo[p], rows[p]), :] = ref[...].astype(BF16)
        gather = _Gather(jobs, rows, lo, wl_v, dict(enumerate(dests)), send_sems, recv_sems)
        gather.start()
        peers = [(x, y, 1 - c)] + [(*chip, pc) for pc in (c, 1 - c) for chip in _other_chips(x, y)]
        smalls = [_remote(sm_ref, sa_ref.at[jme], ssend.at[k], srecv.at[k], to) for k, to in enumerate(peers)]
        for cp in smalls:
            cp.start()
        mine = [pltpu.make_async_copy(wl_v.at[pl.ds(lo[p], rows[p]), :],
                                      dests[p].at[pl.ds(pl.multiple_of(jme * rows[p], BF16_ROWS), rows[p]), :], local_sems.at[p])
                for p in range(N_BIG)]
        mine.append(pltpu.make_async_copy(wl_v, wl_ref, local_sems.at[N_BIG]))
        mine.append(pltpu.make_async_copy(sm_ref, sa_ref.at[jme], local_sems.at[N_BIG + 1]))
        for cp in mine:
            cp.start()
        later = [p for p in range(N_BIG) if p not in {j[0] for j in jobs}]
        own = [_SplitRemote(wl_v.at[pl.ds(lo[p], rows[p]), :],
                            dests[p].at[pl.ds(pl.multiple_of(jme * rows[p], BF16_ROWS), rows[p]), :],
                            ssend.at[7 + i], srecv.at[7 + i], (x, y, 1 - c), rows[p], D2D_CHUNKS) for i, p in enumerate(later)]
        for cp in own:
            cp.start()
        for cp in padded:
            cp.store()
        gather.mid()
        for cp in smalls + own:
            cp.wait_recv()
        mine[-1].wait()
        to_v = pltpu.make_async_copy(sa_ref, sa_v, local_sems.at[N_BIG + 1])
        to_v.start()
        to_v.wait()
        head, zeros = heads_v.at[0], heads_v.at[1]
        head[...] = jnp.zeros_like(head)
        zeros[...] = jnp.zeros_like(zeros)
        for j in range(N_DEV):
            head[pl.ds(x0 - N_META, N_META), pl.ds(j * LANE, LANE)] = sa_v[j, pl.ds(0, N_META), :]
        heads = [pltpu.make_async_copy(head, h0_ref.at[pl.ds(0, x0), :], local_sems.at[N_BIG + 1]),
                 pltpu.make_async_copy(zeros, tp_ref.at[pl.ds(0, x0), :], local_sems.at[N_BIG + 2])]
        for cp in heads:
            cp.start()
        gather.finish()
        for cp in smalls + own:
            cp.wait_send()
        for cp in mine[:-1] + heads:
            cp.wait()
        for cp in padded:
            cp.done()

    out_shape = [jax.ShapeDtypeStruct((sum(rows), d), BF16)]
    out_shape += [jax.ShapeDtypeStruct((N_DEV * r, d), BF16) for r in rows]
    out_shape.append(jax.ShapeDtypeStruct((N_DEV,) + sm.shape, F32))
    out_shape += [jax.ShapeDtypeStruct((t_rows, d), F32)] * 2
    res = pl.pallas_call(
        body, out_shape=out_shape, in_specs=[VMEM] * 6 + [ANY] * 2, out_specs=[ANY] * 9,
        scratch_shapes=[pltpu.VMEM((sum(rows), d), BF16), pltpu.VMEM((seq, d), F32), pltpu.VMEM((seq, d), F32),
                        pltpu.VMEM((2, ROW_ALIGN, d), F32), pltpu.VMEM((N_DEV,) + sm.shape, F32),
                        pltpu.SemaphoreType.DMA((n_sems,)), pltpu.SemaphoreType.DMA((n_sems,)),
                        pltpu.SemaphoreType.DMA((7 + N_BIG,)), pltpu.SemaphoreType.DMA((7 + N_BIG,)),
                        pltpu.SemaphoreType.DMA((N_BIG + 3,)), pltpu.SemaphoreType.DMA((2,)), pltpu.SemaphoreType.DMA((2,))],
        name="gather_first", compiler_params=_cparams())(*shards, sm, x2, tgt2)
    return res[0], list(res[1:1 + N_BIG]), res[1 + N_BIG], res[2 + N_BIG], res[3 + N_BIG]


def _in_proj(h0, g1, win_t, tm, comm):
    t_rows, d = h0.shape
    e = win_t.shape[0]

    def body(h_ref, g_ref, w_ref, xn_ref, hin_ref):
        h = h_ref[...]
        xn = ((h * _rstd(h)) * g_ref[...]).astype(BF16)
        xn_ref[...] = xn
        for o, n in _chunks(e, N_CHUNK):
            hin_ref[:, pl.ds(o, n)] = _dot_nt(xn, w_ref[pl.ds(o, n), :])

    return _host_call(
        body, grid=(t_rows // tm,),
        in_specs=[pl.BlockSpec((tm, d), lambda i: (i, 0)), _full((1, d)), _resident((e, d))],
        out_specs=[pl.BlockSpec((tm, d), lambda i: (i, 0)), pl.BlockSpec((tm, e), lambda i: (i, 0))],
        out_shape=[jax.ShapeDtypeStruct((t_rows, d), BF16), jax.ShapeDtypeStruct((t_rows, e), F32)],
        args=(h0, g1, win_t), name="in_proj", comm=comm)


def _tap_slot(off):
    return off % SUBLANE, (off // SUBLANE) * SUBLANE


def _fill_shifted(sh_ref, base_ref, residues, n_rows):
    for r in residues:
        sh_ref[r] = base_ref[pl.ds(r, n_rows), :]


def _mix_conv_fwd(hin, wa, wb, bb, wa_w, comm):
    t_rows = hin.shape[0]
    nt = wa_w // LANE
    ka, kb = wa.shape[0], wb.shape[0]
    nr = CONV_HALO + t_rows

    def body(bg_ref, cg_ref, ha_ref, val_ref, gt_ref, wa_ref, wb_ref, bb_ref, ya_ref, z_ref, base, sh):
        base[pl.ds(0, CONV_HALO), :] = jnp.zeros((CONV_HALO, LANE), F32)
        base[pl.ds(nr, SUBLANE), :] = jnp.zeros((SUBLANE, LANE), F32)

        def conv(w_ref, k_taps, b, n):
            acc = None
            for k in range(k_taps):
                r, q = _tap_slot(CONV_HALO - (k_taps - 1) + k)
                term = w_ref[pl.ds(k, 1), :] * sh[r, pl.ds(b + q, n), :]
                acc = term if acc is None else acc + term
            return acc

        def fill_a(b, c):
            base[pl.ds(CONV_HALO + b, CONV_CHUNK), :] = cg_ref[pl.ds(b, CONV_CHUNK), :] * ha_ref[pl.ds(b, CONV_CHUNK), :]
            return c

        _row_loop(t_rows, CONV_CHUNK, fill_a)
        _fill_shifted(sh, base, sorted({_tap_slot(CONV_HALO - (ka - 1) + k)[0] for k in range(ka)}), nr)

        def out_a(b, c):
            ya_ref[pl.ds(b, CONV_CHUNK), :] = (bg_ref[pl.ds(b, CONV_CHUNK), :] * conv(wa_ref, ka, b, CONV_CHUNK)).astype(BF16)
            return c

        _row_loop(t_rows, CONV_CHUNK, out_a)

        def fill_b(b, c):
            base[pl.ds(CONV_HALO + b, CONV_CHUNK), :] = (val_ref[pl.ds(b, CONV_CHUNK), :]
                                                          * jax.nn.sigmoid(gt_ref[pl.ds(b, CONV_CHUNK), :]))
            return c

        _row_loop(t_rows, CONV_CHUNK, fill_b)
        _fill_shifted(sh, base, range(SUBLANE), nr)

        def out_b(b, c):
            z_ref[pl.ds(b, CONV_CHUNK), :] = conv(wb_ref, kb, b, CONV_CHUNK) + bb_ref[...]
            return c

        _row_loop(t_rows, CONV_CHUNK, out_b)

    def col(g):
        return pl.BlockSpec((t_rows, LANE), lambda i, g=g: (0, g * nt + i))

    tile = lambda rows: pl.BlockSpec((rows, LANE), lambda i: (0, i))
    return _host_call(
        body, grid=(nt,),
        in_specs=[col(0), col(1), col(2), col(3), col(4), tile(ka), tile(kb), tile(1)],
        out_specs=[tile(t_rows), tile(t_rows)],
        out_shape=[jax.ShapeDtypeStruct((t_rows, wa_w), BF16), jax.ShapeDtypeStruct((t_rows, wa_w), F32)],
        scratch_shapes=[pltpu.VMEM((nr + SUBLANE, LANE), F32), pltpu.VMEM((SUBLANE, nr, LANE), F32)],
        args=(hin, hin, hin, hin, hin, wa, wb, bb), name="mix_conv_fwd", comm=comm)


def _ln_parts(z, lg, lb):
    mu = jnp.mean(z, axis=-1, keepdims=True)
    zc = z - mu
    rstd = lax.rsqrt(jnp.mean(zc * zc, axis=-1, keepdims=True) + LN_EPS)
    zh = zc * rstd
    return zh, rstd, zh * lg + lb


def _mix_ln_fwd(ya, z, lg, lb, tm, comm):
    t_rows, w = z.shape

    def body(ya_ref, z_ref, lg_ref, lb_ref, y_ref):
        _, _, ln = _ln_parts(z_ref[...], lg_ref[...], lb_ref[...])
        y_ref[:, pl.ds(0, w)] = ya_ref[...]
        y_ref[:, pl.ds(w, w)] = (ln * jax.nn.sigmoid(ln)).astype(BF16)

    blk = pl.BlockSpec((tm, w), lambda i: (i, 0))
    res, extra = _host_call(body, grid=(t_rows // tm,), in_specs=[blk, blk, _full((1, w)), _full((1, w))],
                            out_specs=[pl.BlockSpec((tm, 2 * w), lambda i: (i, 0))],
                            out_shape=[jax.ShapeDtypeStruct((t_rows, 2 * w), BF16)], args=(ya, z, lg, lb), name="mix_ln_fwd", comm=comm)
    return res[0], extra


def _out_proj(y, w_out, h0, g2, g3, tm, comm):
    t_rows, d = h0.shape

    def body(y_ref, w_ref, h0_ref, g2_ref, g3_ref, mix_ref, h1_ref, xn2_ref):
        mix = _dot_nn(y_ref[...], w_ref[...])
        mix_ref[...] = mix
        h1 = h0_ref[...] + (mix * _rstd(mix)) * g2_ref[...]
        h1_ref[...] = h1
        xn2_ref[...] = ((h1 * _rstd(h1)) * g3_ref[...]).astype(BF16)

    blk = pl.BlockSpec((tm, d), lambda i: (i, 0))
    return _host_call(
        body, grid=(t_rows // tm,), in_specs=[blk, _resident(w_out.shape), blk, _full((1, d)), _full((1, d))],
        out_specs=[blk, blk, blk],
        out_shape=[jax.ShapeDtypeStruct((t_rows, d), F32), jax.ShapeDtypeStruct((t_rows, d), F32),
                   jax.ShapeDtypeStruct((t_rows, d), BF16)],
        args=(y, w_out, h0, g2, g3), name="out_proj", comm=comm)


def _gate_up(xn2, wg_t, wu_t, tm, comm):
    t_rows, d = xn2.shape
    f = wg_t.shape[0]

    def body(x_ref, wg_ref, wu_ref, a_ref, u_ref, s_ref):
        xn = x_ref[...]
        for o, n in _chunks(f, N_CHUNK):
            a = _dot_nt(xn, wg_ref[pl.ds(o, n), :])
            u = _dot_nt(xn, wu_ref[pl.ds(o, n), :])
            a_ref[:, pl.ds(o, n)] = a.astype(BF16)
            u_ref[:, pl.ds(o, n)] = u.astype(BF16)
            s_ref[:, pl.ds(o, n)] = ((a * jax.nn.sigmoid(a)) * u).astype(BF16)

    blk = pl.BlockSpec((tm, f), lambda i: (i, 0))
    return _host_call(
        body, grid=(t_rows // tm,),
        in_specs=[pl.BlockSpec((tm, d), lambda i: (i, 0)), _resident((f, d)), _resident((f, d))],
        out_specs=[blk, blk, blk], out_shape=[jax.ShapeDtypeStruct((t_rows, f), BF16)] * 3,
        args=(xn2, wg_t, wu_t), name="gate_up", comm=comm)


def _down_loss(s, wd, h1, tgt, g4, tm, x0):
    t_rows, d = h1.shape
    f = wd.shape[0]

    def body(s_ref, w_ref, h1_ref, tgt_ref, g4_ref, dh2_ref, dff_ref, dg4_ref, loss_ref):
        i = pl.program_id(0)
        ff = _dot_nn(s_ref[...], w_ref[...])
        r4 = _rstd(ff)
        fh = ff * r4
        g4 = g4_ref[...]
        h2 = h1_ref[...] + fh * g4
        row = i * tm + lax.broadcasted_iota(jnp.int32, (tm, 1), 0)
        diff = jnp.where(row >= x0, h2 - tgt_ref[...], 0.0)
        dh2 = diff / d
        dh2_ref[...] = dh2
        dff_ref[...] = _rms_bwd(dh2 * g4, fh, r4).astype(BF16)
        _acc_rows(dg4_ref, dh2 * fh, i == 0)
        _acc_rows(loss_ref, diff * diff, i == 0)

    blk = pl.BlockSpec((tm, d), lambda i: (i, 0))
    res, _ = _host_call(
        body, grid=(t_rows // tm,),
        in_specs=[pl.BlockSpec((tm, f), lambda i: (i, 0)), _resident((f, d)), blk, blk, _full((1, d))],
        out_specs=[blk, blk, _full((1, d)), _full((1, d))],
        out_shape=[jax.ShapeDtypeStruct((t_rows, d), F32), jax.ShapeDtypeStruct((t_rows, d), BF16),
                   jax.ShapeDtypeStruct((1, d), F32), jax.ShapeDtypeStruct((1, d), F32)],
        args=(s, wd, h1, tgt, g4), name="down_loss")
    return res


def _bwd_down(dff, wd, a, u, tm, comm):
    t_rows, d = dff.shape
    f = wd.shape[0]

    def body(dff_ref, w_ref, a_ref, u_ref, da_ref, du_ref):
        dff_v = dff_ref[...]
        for o, n in _chunks(f, N_CHUNK):
            ds = _dot_nt(dff_v, w_ref[pl.ds(o, n), :])
            av = a_ref[:, pl.ds(o, n)].astype(F32)
            uv = u_ref[:, pl.ds(o, n)].astype(F32)
            sig = jax.nn.sigmoid(av)
            da_ref[:, pl.ds(o, n)] = (ds * uv * _silu_grad(av, sig)).astype(BF16)
            du_ref[:, pl.ds(o, n)] = (ds * (av * sig)).astype(BF16)

    blk = pl.BlockSpec((tm, f), lambda i: (i, 0))
    return _host_call(
        body, grid=(t_rows // tm,),
        in_specs=[pl.BlockSpec((tm, d), lambda i: (i, 0)), _resident((f, d)), blk, blk],
        out_specs=[blk, blk], out_shape=[jax.ShapeDtypeStruct((t_rows, f), BF16)] * 2,
        args=(dff, wd, a, u), name="bwd_down", comm=comm)


def _wgrad(a, b, name, after=()):
    d = b.shape[1]
    t_rows = b.shape[0]
    stacked = a.ndim == 3
    n = a.shape[-1]
    groups = a.shape[0] if stacked else 1
    steps = 1 if stacked else 2
    tile = max(t for t in range(LANE, min(n // steps, WGRAD_TILE_MAX) + 1, LANE) if n % t == 0)
    tiles = n // tile

    def body(a_ref, b_ref, o_ref):
        o_ref[...] = lax.dot_general(a_ref[...], b_ref[...], (((0,), (0,)), ((), ())),
                                     preferred_element_type=F32).astype(BF16)

    if stacked:
        a_spec = pl.BlockSpec((None, t_rows, tile), lambda g, i: (g, 0, i))
    else:
        a_spec = pl.BlockSpec((t_rows, tile), lambda g, i: (0, i))
    res, _ = _host_call(
        body, grid=(groups, tiles), in_specs=[a_spec, _resident((t_rows, d))],
        out_specs=[pl.BlockSpec((tile, d), lambda g, i: (g * tiles + i, 0))],
        out_shape=[jax.ShapeDtypeStruct((groups * n, d), BF16)], args=(a, b), name=name, after=after)
    return res[0]


def _bwd_ffn_in(da, du, wg_t, wu_t, h1, dh2, g3, tm, comm):
    t_rows, d = h1.shape
    f = wg_t.shape[0]

    def body(da_ref, du_ref, wg_ref, wu_ref, h1_ref, dh2_ref, g3_ref, dh1_ref, dg3_ref):
        dxn2 = _dot_nn(da_ref[...], wg_ref[...]) + _dot_nn(du_ref[...], wu_ref[...])
        h1 = h1_ref[...]
        r3 = _rstd(h1)
        h1h = h1 * r3
        _acc_rows(dg3_ref, dxn2 * h1h, pl.program_id(0) == 0)
        dh1_ref[...] = dh2_ref[...] + _rms_bwd(dxn2 * g3_ref[...], h1h, r3)

    blk = pl.BlockSpec((tm, d), lambda i: (i, 0))
    blkf = pl.BlockSpec((tm, f), lambda i: (i, 0))
    return _host_call(
        body, grid=(t_rows // tm,),
        in_specs=[blkf, blkf, _resident((f, d)), _resident((f, d)), blk, blk, _full((1, d))],
        out_specs=[blk, _full((1, d))],
        out_shape=[jax.ShapeDtypeStruct((t_rows, d), F32), jax.ShapeDtypeStruct((1, d), F32)],
        args=(da, du, wg_t, wu_t, h1, dh2, g3), name="bwd_ffn_in", comm=comm)


def _bwd_out_proj(dh1, mix, w_out, g2, tm, after):
    t_rows, d = dh1.shape

    def body(dh1_ref, mix_ref, w_ref, g2_ref, dmix_ref, dy_ref, dg2_ref):
        mix = mix_ref[...]
        r2 = _rstd(mix)
        mh = mix * r2
        dh1 = dh1_ref[...]
        _acc_rows(dg2_ref, dh1 * mh, pl.program_id(0) == 0)
        dmix = _rms_bwd(dh1 * g2_ref[...], mh, r2).astype(BF16)
        dmix_ref[...] = dmix
        dy_ref[...] = _dot_nt(dmix, w_ref[...])

    blk = pl.BlockSpec((tm, d), lambda i: (i, 0))
    res, _ = _host_call(
        body, grid=(t_rows // tm,), in_specs=[blk, blk, _resident(w_out.shape), _full((1, d))],
        out_specs=[blk, blk, _full((1, d))],
        out_shape=[jax.ShapeDtypeStruct((t_rows, d), BF16), jax.ShapeDtypeStruct((t_rows, d), F32),
                   jax.ShapeDtypeStruct((1, d), F32)],
        args=(dh1, mix, w_out, g2), name="bwd_out_proj", after=after)
    return res


def _mix_ln_bwd(z, dy, lg, lb, tm):
    t_rows, w = z.shape

    def body(z_ref, dyb_ref, lg_ref, lb_ref, dz_ref, dlg_ref, dlb_ref, dbb_ref):
        first = pl.program_id(0) == 0
        lg = lg_ref[...]
        zh, rstd, ln = _ln_parts(z_ref[...], lg, lb_ref[...])
        dln = dyb_ref[...] * _silu_grad(ln, jax.nn.sigmoid(ln))
        _acc_rows(dlg_ref, dln * zh, first)
        _acc_rows(dlb_ref, dln, first)
        dzh = dln * lg
        dz = rstd * (dzh - jnp.mean(dzh, axis=-1, keepdims=True) - zh * jnp.mean(dzh * zh, axis=-1, keepdims=True))
        dz_ref[...] = dz
        _acc_rows(dbb_ref, dz, first)

    blk = pl.BlockSpec((tm, w), lambda i: (i, 0))
    vec = _full((1, w))
    res, _ = _host_call(
        body, grid=(t_rows // tm,), in_specs=[blk, pl.BlockSpec((tm, w), lambda i: (i, 1)), vec, vec],
        out_specs=[blk, vec, vec, vec],
        out_shape=[jax.ShapeDtypeStruct((t_rows, w), F32)] + [jax.ShapeDtypeStruct((1, w), F32)] * 3,
        args=(z, dy, lg, lb), name="mix_ln_bwd")
    return res


def _mix_conv_bwd(hin, dy, dz, wa, wb, wa_w, comm):
    t_rows = hin.shape[0]
    nt = wa_w // LANE
    ka, kb = wa.shape[0], wb.shape[0]
    nr = CONV_HALO + t_rows
    kb_rows = -(-kb // SUBLANE) * SUBLANE

    def body(bg_ref, cg_ref, ha_ref, val_ref, gt_ref, dya_ref, dz_ref, wa_ref, wb_ref,
             dh_ref, dwa_ref, dwb_ref, base, sh, based, shd, tmp, wbc):
        zeros = lambda n: jnp.zeros((n, LANE), F32)
        base[pl.ds(0, CONV_HALO), :] = zeros(CONV_HALO)
        base[pl.ds(nr, SUBLANE), :] = zeros(SUBLANE)
        based[pl.ds(t_rows, CONV_HALO + SUBLANE), :] = zeros(CONV_HALO + SUBLANE)

        def fwd_slot(k_taps, k):
            return _tap_slot(CONV_HALO - (k_taps - 1) + k)

        def bwd_slot(k_taps, k):
            return _tap_slot(k_taps - 1 - k)

        def conv(w_ref, k_taps, src, slot, b, n):
            acc = None
            for k in range(k_taps):
                r, q = slot(k_taps, k)
                term = w_ref[pl.ds(k, 1), :] * src[r, pl.ds(b + q, n), :]
                acc = term if acc is None else acc + term
            return acc

        def by_residue(k_taps, slot):
            groups = {}
            for k in range(k_taps):
                r, q = slot(k_taps, k)
                groups.setdefault(r, []).append((k, q // SUBLANE))
            return groups

        def wgrad_loop(w_ref, k_taps):
            n_sub = WGRAD_ROWS // SUBLANE
            for k in range(k_taps):
                wbc[k] = jnp.broadcast_to(w_ref[pl.ds(k, 1), :], (SUBLANE, LANE))
            fwd, bwd = by_residue(k_taps, fwd_slot), by_residue(k_taps, bwd_slot)

            def window(src, r, taps, b):
                span = n_sub + max(qi for _, qi in taps)
                return [src[r, pl.ds(b + SUBLANE * i, SUBLANE), :] for i in range(span)]

            def step(b, accs):
                accs = list(accs)
                dv = [based[pl.ds(b + SUBLANE * j, SUBLANE), :] for j in range(n_sub)]
                for r, taps in fwd.items():
                    win = window(sh, r, taps, b)
                    for k, qi in taps:
                        t = dv[0] * win[qi]
                        for j in range(1, n_sub):
                            t = t + dv[j] * win[qi + j]
                        accs[k] = accs[k] + t
                outs = [None] * n_sub
                for r, taps in bwd.items():
                    win = window(shd, r, taps, b)
                    for k, qi in taps:
                        wk = wbc[k]
                        for j in range(n_sub):
                            term = wk * win[qi + j]
                            outs[j] = term if outs[j] is None else outs[j] + term
                for j in range(n_sub):
                    tmp[pl.ds(b + SUBLANE * j, SUBLANE), :] = outs[j]
                return tuple(accs)

            return _row_loop(t_rows, WGRAD_ROWS, step, tuple(zeros(SUBLANE) for _ in range(k_taps)))

        def store_taps(ref, accs, rows):
            for k, acc in enumerate(accs):
                ref[pl.ds(k, 1), :] = jnp.sum(acc, axis=0, keepdims=True)
            if rows > len(accs):
                ref[pl.ds(len(accs), rows - len(accs)), :] = zeros(rows - len(accs))

        def fill_a(b, c):
            sl = pl.ds(b, CONV_CHUNK)
            base[pl.ds(CONV_HALO + b, CONV_CHUNK), :] = cg_ref[sl, :] * ha_ref[sl, :]
            based[sl, :] = dya_ref[sl, :] * bg_ref[sl, :]
            return c

        _row_loop(t_rows, CONV_CHUNK, fill_a)
        _fill_shifted(sh, base, sorted({fwd_slot(ka, k)[0] for k in range(ka)}), nr)
        _fill_shifted(shd, based, sorted({bwd_slot(ka, k)[0] for k in range(ka)}), nr)

        def d_bgate(b, c):
            sl = pl.ds(b, CONV_CHUNK)
            dh_ref[0, sl, :] = (dya_ref[sl, :] * conv(wa_ref, ka, sh, fwd_slot, b, CONV_CHUNK)).astype(BF16)
            return c

        _row_loop(t_rows, CONV_CHUNK, d_bgate)
        store_taps(dwa_ref, wgrad_loop(wa_ref, ka), SUBLANE)

        def d_ch(b, c):
            sl = pl.ds(b, CONV_CHUNK)
            dua = tmp[sl, :]
            dh_ref[1, sl, :] = (dua * ha_ref[sl, :]).astype(BF16)
            dh_ref[2, sl, :] = (dua * cg_ref[sl, :]).astype(BF16)
            return c

        _row_loop(t_rows, CONV_CHUNK, d_ch)

        def fill_b(b, c):
            sl = pl.ds(b, CONV_CHUNK)
            base[pl.ds(CONV_HALO + b, CONV_CHUNK), :] = val_ref[sl, :] * jax.nn.sigmoid(gt_ref[sl, :])
            based[sl, :] = dz_ref[sl, :]
            return c

        _row_loop(t_rows, CONV_CHUNK, fill_b)
        _fill_shifted(sh, base, range(SUBLANE), nr)
        _fill_shifted(shd, based, range(SUBLANE), nr)
        store_taps(dwb_ref, wgrad_loop(wb_ref, kb), kb_rows)

        def d_glu(b, c):
            sl = pl.ds(b, CONV_CHUNK)
            dgg = tmp[sl, :]
            sig = jax.nn.sigmoid(gt_ref[sl, :])
            dh_ref[3, sl, :] = (dgg * sig).astype(BF16)
            dh_ref[4, sl, :] = (dgg * val_ref[sl, :] * (sig * (1.0 - sig))).astype(BF16)
            return c

        _row_loop(t_rows, CONV_CHUNK, d_glu)

    def col(g):
        return pl.BlockSpec((t_rows, LANE), lambda i, g=g: (0, g * nt + i))

    tile = lambda rows: pl.BlockSpec((rows, LANE), lambda i: (0, i))
    return _host_call(
        body, grid=(nt,),
        in_specs=[col(0), col(1), col(2), col(3), col(4), tile(t_rows), tile(t_rows), tile(ka), tile(kb)],
        out_specs=[pl.BlockSpec((5, t_rows, LANE), lambda i: (0, 0, i)), tile(SUBLANE), tile(kb_rows)],
        out_shape=[jax.ShapeDtypeStruct((5, t_rows, wa_w), BF16), jax.ShapeDtypeStruct((SUBLANE, wa_w), F32),
                   jax.ShapeDtypeStruct((kb_rows, wa_w), F32)],
        scratch_shapes=[pltpu.VMEM((nr + SUBLANE, LANE), F32), pltpu.VMEM((SUBLANE, nr, LANE), F32),
                        pltpu.VMEM((nr + SUBLANE, LANE), F32), pltpu.VMEM((SUBLANE, nr, LANE), F32),
                        pltpu.VMEM((t_rows, LANE), F32), pltpu.VMEM((kb_rows, SUBLANE, LANE), F32)],
        args=(hin, hin, hin, hin, hin, dy, dz, wa, wb), name="mix_conv_bwd", comm=comm)


def _bwd_in_proj(dh5, win_t, h0, dh1, g1, tm, comm):
    t_rows, d = h0.shape
    groups, _, w = dh5.shape

    def body(dh_ref, w_ref, h0_ref, dh1_ref, g1_ref, dh0_ref, dg1_ref):
        dxn1 = None
        for g in range(groups):
            part = _dot_nn(dh_ref[g], w_ref[pl.ds(g * w, w), :])
            dxn1 = part if dxn1 is None else dxn1 + part
        h0 = h0_ref[...]
        r1 = _rstd(h0)
        h0h = h0 * r1
        _acc_rows(dg1_ref, dxn1 * h0h, pl.program_id(0) == 0)
        dh0_ref[...] = dh1_ref[...] + _rms_bwd(dxn1 * g1_ref[...], h0h, r1)

    blk = pl.BlockSpec((tm, d), lambda i: (i, 0))
    return _host_call(
        body, grid=(t_rows // tm,),
        in_specs=[pl.BlockSpec((groups, tm, w), lambda i: (0, i, 0)), _resident(win_t.shape), blk, blk, _full((1, d))],
        out_specs=[blk, _full((1, d))],
        out_shape=[jax.ShapeDtypeStruct((t_rows, d), F32), jax.ShapeDtypeStruct((1, d), F32)],
        args=(dh5, win_t, h0, dh1, g1), name="bwd_in_proj", comm=comm)


def _reduce_small(smalls, d, after):
    (dmeta, dg1, dg2, dg3, dg4, dbb, dlg, dlb, lossv, dwa, dwb) = smalls
    half = d // 2
    kb_rows = dwb.shape[0]

    def body(dmeta_ref, dg1_ref, dg2_ref, dg3_ref, dg4_ref, dbb_ref, dlg_ref, dlb_ref, loss_ref, dwa_ref, dwb_ref,
             ptot_ref, pbuf, psib, chip_p, ps_send, ps_recv, pc_send, pc_recv):
        x, y, c = _mesh_pos()
        pbuf[...] = jnp.zeros_like(pbuf)
        pbuf[pl.ds(0, N_META), :] = dmeta_ref[...]
        for row, ref in ((16, dg1_ref), (17, dg2_ref), (18, dg3_ref), (19, dg4_ref)):
            pbuf[pl.ds(row, 1), :] = ref[...]
        pbuf[pl.ds(20, 1), pl.ds(0, half)] = dbb_ref[...]
        pbuf[pl.ds(20, 1), pl.ds(half, half)] = dlg_ref[...]
        pbuf[pl.ds(21, 1), pl.ds(0, half)] = dlb_ref[...]
        lv = loss_ref[...]
        pbuf[pl.ds(21, 1), pl.ds(half, half)] = lv[:, :half] + lv[:, half:]
        pbuf[pl.ds(24, SUBLANE), pl.ds(0, half)] = dwa_ref[...]
        pbuf[pl.ds(32, kb_rows), pl.ds(0, half)] = dwb_ref[...]
        to_sib = _remote(pbuf, psib, ps_send.at[0], ps_recv.at[0], (x, y, 1 - c))
        to_sib.start()
        to_sib.wait_recv()
        my_chip = 2 * x + y
        chip_p[my_chip] = pbuf[...] + psib[...]
        to_sib.wait_send()
        slot = chip_p.at[my_chip]
        cps = [_remote(slot, slot, pc_send.at[k], pc_recv.at[k], (*chip, c)) for k, chip in enumerate(_other_chips(x, y))]
        for cp in cps:
            cp.start()
        for cp in cps:
            cp.wait_recv()
        ptot_ref[...] = ((chip_p[0] + chip_p[1]) + chip_p[2]) + chip_p[3]
        for cp in cps:
            cp.wait_send()

    return _host_call(
        body, grid=(), in_specs=[VMEM] * 11, out_specs=[VMEM], out_shape=[jax.ShapeDtypeStruct((SMALL_ROWS, d), F32)],
        scratch_shapes=[pltpu.VMEM((SMALL_ROWS, d), F32), pltpu.VMEM((SMALL_ROWS, d), F32), pltpu.VMEM((4, SMALL_ROWS, d), F32),
                        pltpu.SemaphoreType.DMA((1,)), pltpu.SemaphoreType.DMA((1,)),
                        pltpu.SemaphoreType.DMA((3,)), pltpu.SemaphoreType.DMA((3,))],
        args=smalls, name="reduce_small", after=after)


def _adamw(w, g, m, v):
    m = ADAM_B1 * m + (1.0 - ADAM_B1) * g
    v = ADAM_B2 * v + (1.0 - ADAM_B2) * jnp.square(g)
    m_hat = m / (1.0 - ADAM_B1 ** ADAM_STEP)
    v_hat = v / (1.0 - ADAM_B2 ** ADAM_STEP)
    delta = -ADAM_LR * (m_hat / (jnp.sqrt(v_hat) + ADAM_EPS) + ADAM_WD * w)
    return delta, m, v


def _adam_big(g, pair, part, w, m, v, name):
    r, d = w.shape
    cols = d // ADAM_COL_BLOCKS

    def body(me_ref, g_ref, pair_ref, part_ref, w_ref, m_ref, v_ref, go_ref, d_ref, mo_ref, vo_ref):
        g = g_ref[...].astype(F32) + pair_ref[...].astype(F32)
        for k in range(3):
            g = g + part_ref[k].astype(F32)
        go_ref[...] = g
        d_ref[...], mo_ref[...], vo_ref[...] = _adamw(w_ref[...], g, m_ref[...], v_ref[...])

    blk = pl.BlockSpec((r, cols), lambda i, me_ref: (0, i))
    grid_spec = pltpu.PrefetchScalarGridSpec(
        num_scalar_prefetch=1, grid=(ADAM_COL_BLOCKS,),
        in_specs=[pl.BlockSpec((r, cols), lambda i, me_ref: (me_ref[0], i)),
                  pl.BlockSpec((None, r, cols), lambda i, me_ref: (0, 0, i)),
                  pl.BlockSpec((3, r, cols), lambda i, me_ref: (0, 0, i)), blk, blk, blk],
        out_specs=[blk, blk, blk, blk])
    me = jnp.reshape(_dev_index(*_mesh_pos()), (1,)).astype(jnp.int32)
    return pl.pallas_call(body, out_shape=[jax.ShapeDtypeStruct((r, d), F32)] * 4, grid_spec=grid_spec, name=name,
                          compiler_params=_cparams(1))(me, g, pair, part, w, m, v)


def _adam_small(gs, ws, ms, vs):
    n = len(gs)

    def body(*refs):
        ins, outs = refs[:4 * n], refs[4 * n:]
        for i in range(n):
            g = ins[i][...]
            delta, m, v = _adamw(ins[n + i][...], g, ins[2 * n + i][...], ins[3 * n + i][...])
            outs[i][...] = delta
            outs[n + i][...] = m
            outs[2 * n + i][...] = v

    shapes = [jax.ShapeDtypeStruct(w.shape, F32) for w in ws]
    return pl.pallas_call(body, out_shape=shapes * 3, name="adam_small", compiler_params=_cparams())(*gs, *ws, *ms, *vs)


def kernel(x, meta_tokens, pre_mix_norm, w_in, conv_a_w, conv_b_w, conv_b_bias, ln_b_gain, ln_b_bias, w_out, post_mix_norm, pre_ffn_norm, w_gate, w_up, w_down, post_ffn_norm, loss_target, m_meta_tokens, m_pre_mix_norm, m_w_in, m_conv_a_w, m_conv_b_w, m_conv_b_bias, m_ln_b_gain, m_ln_b_bias, m_w_out, m_post_mix_norm, m_pre_ffn_norm, m_w_gate, m_w_up, m_w_down, m_post_ffn_norm, v_meta_tokens, v_pre_mix_norm, v_w_in, v_conv_a_w, v_conv_b_w, v_conv_b_bias, v_ln_b_gain, v_ln_b_bias, v_w_out, v_post_mix_norm, v_pre_ffn_norm, v_w_gate, v_w_up, v_w_down, v_post_ffn_norm):
    _, seq, d = x.shape
    ka, ca_loc = conv_a_w.shape[1:]
    kb, cb_loc = conv_b_w.shape[1:]
    wa_w = ca_loc * N_DEV
    assert cb_loc == ca_loc and wa_w % LANE == 0 and w_in.shape[2] * N_DEV == 5 * wa_w
    pad = (-(N_META + seq)) % ROW_ALIGN
    x0 = pad + N_META
    t_rows = x0 + seq
    assert t_rows % (N_ROW_BLOCKS * BF16_ROWS) == 0 and t_rows % CONV_CHUNK == 0 and d % LANE == 0
    tm = t_rows // N_ROW_BLOCKS
    me = _dev_index(*_mesh_pos())

    def as_rows(w_in_like, w_out_like, w_gate_like, w_up_like, w_down_like):
        return (w_in_like[0].T, w_out_like[0], w_gate_like[0].T, w_up_like[0].T, w_down_like[0])

    w_loc = as_rows(w_in, w_out, w_gate, w_up, w_down)
    rows = [w.shape[0] for w in w_loc]
    assert all(r % ADD_CHUNK == 0 for r in rows)
    P_IN, P_OUT, P_GATE, P_UP, P_DOWN = range(N_BIG)

    sm = jnp.zeros((SM_ROWS, LANE), F32)
    sm = sm.at[0:N_META, :].set(meta_tokens)
    sm = sm.at[16:16 + ka, 0:ca_loc].set(conv_a_w[0])
    sm = sm.at[24:24 + kb, 0:cb_loc].set(conv_b_w[0])
    wl, wfull, sm_all, h0, tgt = _gather_first(w_loc, sm, [(P_IN, 0, rows[P_IN])], x[0], loss_target[0], t_rows, x0)
    wa =jnp.transpose(sm_all[:, 16:16 + ka, 0:ca_loc], (1, 0, 2)).reshape(ka, wa_w)
    wb = jnp.transpose(sm_all[:, 24:24 + kb, 0:cb_loc], (1, 0, 2)).reshape(kb, wa_w)

    later = (P_OUT, P_GATE, P_UP, P_DOWN)
    sems, wl, started, _ = _gather_start(wl, wfull, later, rows)
    for p, arr in zip(later, started):
        wfull[p] = arr

    def arrived(p, after, name):
        nonlocal wl
        wl, wfull[p] = _gather_wait(wl, wfull[p], sems[later.index(p)], after, rows[p], name)
        return _forward_comm(wfull[p], rows[p])

    (xn1, hin), _ = _in_proj(h0, pre_mix_norm, wfull[P_IN], tm, None)
    (ya, z), (wfull[P_OUT],) = _mix_conv_fwd(hin, wa, wb, conv_b_bias, wa_w, arrived(P_OUT, hin, "gather_wait_out"))
    y, (wfull[P_GATE],) = _mix_ln_fwd(ya, z, ln_b_gain, ln_b_bias, tm, arrived(P_GATE, z, "gather_wait_gate"))
    (mix, h1, xn2), _ = _out_proj(y, wfull[P_OUT], h0, post_mix_norm, pre_ffn_norm, tm, None)
    arrived(P_UP, xn2, "gather_wait_up")
    wfull[P_UP] = _forward_now(wfull[P_UP], rows[P_UP], "forward_up")
    (a, u, s), _ = _gate_up(xn2, wfull[P_GATE], wfull[P_UP], tm, None)
    arrived(P_DOWN, s, "gather_wait_down")
    wfull[P_DOWN] = _forward_now(wfull[P_DOWN], rows[P_DOWN], "forward_down")
    dh2, dff, dg4, lossv = _down_loss(s, wfull[P_DOWN], h1, tgt, post_ffn_norm, tm, x0)

    gwd = _wgrad(s, dff, "wgrad_down")
    (da, du), (pair_d,) = _bwd_down(dff, wfull[P_DOWN], a, u, tm, _pair_comm(gwd, rows[P_DOWN]))
    (flight_d,), token = _chip_start([_pair_sum(gwd, pair_d, rows[P_DOWN], "pair_sum_down")], "chip_start_down")
    gwg = _wgrad(da, xn2, "wgrad_gate", [token])
    gwu = _wgrad(du, xn2, "wgrad_up")
    (dh1, dg3), (pair_g, pair_u) = _bwd_ffn_in(da, du, wfull[P_GATE], wfull[P_UP], h1, dh2, pre_ffn_norm, tm,
                                               _merge_comms([_pair_comm(gwg, rows[P_GATE]), _pair_comm(gwu, rows[P_UP])]))
    (flight_g, flight_u), token = _chip_start([_pair_sum(gwg, pair_g, rows[P_GATE], "pair_sum_gate"),
                                               _pair_sum(gwu, pair_u, rows[P_UP], "pair_sum_up")], "chip_start_gate_up")
    dmix, dy, dg2 = _bwd_out_proj(dh1, mix, wfull[P_OUT], post_mix_norm, tm, [token])
    gwo = _wgrad(y, dmix, "wgrad_out")
    dz, dlg, dlb, dbb = _mix_ln_bwd(z, dy, ln_b_gain, ln_b_bias, tm)
    (dh5, dwa, dwb), (pair_o,) = _mix_conv_bwd(hin, dy, dz, wa, wb, wa_w, _pair_comm(gwo, rows[P_OUT]))
    (flight_o,), token = _chip_start([_pair_sum(gwo, pair_o, rows[P_OUT], "pair_sum_out")], "chip_start_out")
    gwi = _wgrad(dh5, xn1, "wgrad_in", [token])
    (dh0, dg1), (pair_i,) = _bwd_in_proj(dh5, wfull[P_IN], h0, dh1, pre_mix_norm, tm, _pair_comm(gwi, rows[P_IN]))
    grad_x = dh0[x0:][None]
    dmeta = dh0[x0 - N_META:x0]
    (ptot,), _ = _reduce_small((dmeta, dg1, dg2, dg3, dg4, dbb, dlg, dlb, lossv, dwa, dwb), d, [])
    (flight_i,), token = _chip_start([_pair_sum(gwi, pair_i, rows[P_IN], "pair_sum_in", [ptot])], "chip_start_in")

    def landed(flight, after, tag):
        sems_p, sums, land = flight
        return _chip_wait(sums, land, sems_p, after, "chip_wait_" + tag)

    part_d = landed(flight_d, token, "down")
    part_g = landed(flight_g, token, "gate")
    part_u = landed(flight_u, token, "up")
    part_o = landed(flight_o, token, "out")

    half = d // 2
    loss = (0.5 / d) * jnp.sum(ptot[21, half:])
    g_meta = lax.dynamic_slice(ptot, (0, me * (d // N_DEV)), (N_META, d // N_DEV))
    g_small = [g_meta, ptot[16:17], lax.dynamic_slice(ptot, (24, me * ca_loc), (ka, ca_loc))[None],
               lax.dynamic_slice(ptot, (32, me * cb_loc), (kb, cb_loc))[None],
               ptot[20:21, :half], ptot[20:21, half:], ptot[21:22, :half], ptot[17:18], ptot[18:19], ptot[19:20]]
    w_small = [meta_tokens, pre_mix_norm, conv_a_w, conv_b_w, conv_b_bias, ln_b_gain, ln_b_bias, post_mix_norm,
               pre_ffn_norm, post_ffn_norm]
    m_small = [m_meta_tokens, m_pre_mix_norm, m_conv_a_w, m_conv_b_w, m_conv_b_bias, m_ln_b_gain, m_ln_b_bias,
               m_post_mix_norm, m_pre_ffn_norm, m_post_ffn_norm]
    v_small = [v_meta_tokens, v_pre_mix_norm, v_conv_a_w, v_conv_b_w, v_conv_b_bias, v_ln_b_gain, v_ln_b_bias,
               v_post_mix_norm, v_pre_ffn_norm, v_post_ffn_norm]
    small = _adam_small(g_small, w_small, m_small, v_small)
    n_small = len(w_small)
    d_small, nm_small, nv_small = small[:n_small], small[n_small:2 * n_small], small[2 * n_small:]

    m_loc = as_rows(m_w_in, m_w_out, m_w_gate, m_w_up, m_w_down)
    v_loc = as_rows(v_w_in, v_w_out, v_w_gate, v_w_up, v_w_down)
    full_grads = {P_IN: gwi, P_OUT: gwo, P_GATE: gwg, P_UP: gwu, P_DOWN: gwd}
    pairs = {P_IN: pair_i, P_OUT: pair_o, P_GATE: pair_g, P_UP: pair_u, P_DOWN: pair_d}
    parts = {P_OUT: part_o, P_GATE: part_g, P_UP: part_u, P_DOWN: part_d}
    names = {P_IN: "w_in", P_OUT: "w_out", P_GATE: "w_gate", P_UP: "w_up", P_DOWN: "w_down"}
    bigs = {}
    res = None
    for p in (P_DOWN, P_GATE, P_UP, P_OUT, P_IN):
        if p == P_IN:
            parts[p] = landed(flight_i, res[1], "in")
        res = _adam_big(full_grads[p], pairs[p], parts[p], w_loc[p], m_loc[p], v_loc[p], "adam_" + names[p])
        bigs[names[p]] = [(o.T if p in (P_IN, P_GATE, P_UP) else o)[None] for o in res]

    def ordered(pick_small, pick_big):
        sm_it = iter(range(n_small))
        out = []
        for name in ("s", "s", "w_in", "s", "s", "s", "s", "s", "w_out", "s", "s", "w_gate", "w_up", "w_down", "s"):
            out.append(pick_small(next(sm_it)) if name == "s" else pick_big(name))
        return out

    grads = ordered(lambda i: g_small[i], lambda n: bigs[n][0])
    deltas = ordered(lambda i: d_small[i], lambda n: bigs[n][1])
    new_m = ordered(lambda i: nm_small[i], lambda n: bigs[n][2])
    new_v = ordered(lambda i: nv_small[i], lambda n: bigs[n][3])
    return (loss, grad_x, *grads, *deltas, *new_m, *new_v)
```

```python
import jax
import jax.numpy as jnp
from jax import lax
from jax.experimental import pallas as pl
from jax.experimental.pallas import tpu as pltpu

F32 = jnp.float32
BF16 = jnp.bfloat16
MESH = pl.DeviceIdType.MESH

N_META = 16
N_DEV = 8
RMS_EPS = 1e-6
LN_EPS = 1e-5
ADAM_LR = 0.001
ADAM_B1 = 0.9
ADAM_B2 = 0.999
ADAM_EPS = 1e-08
ADAM_WD = 0.01
ADAM_STEP = 10

LANE = 128
SUBLANE = 8
BF16_ROWS = 16
ROW_ALIGN = 128
N_ROW_BLOCKS = 4
CONV_HALO = 32
CONV_CHUNK = 64
WGRAD_ROWS = 32
N_CHUNK = 512
WGRAD_TILE_MAX = 1408
ADD_CHUNK = 32
ADAM_COL_BLOCKS = 4
COPY_PIECES = 4
V7X_VMEM_BYTES = 64 * 1024 * 1024
VMEM_LIMIT = V7X_VMEM_BYTES - 6 * 1024 * 1024
SMALL_ROWS = 64
SM_ROWS = 56
N_BIG = 5

ANY = pl.BlockSpec(memory_space=pl.ANY)
VMEM = pl.BlockSpec(memory_space=pltpu.VMEM)


def _cparams(n_grid_axes=0):
    sem = ("arbitrary",) * n_grid_axes if n_grid_axes else None
    return pltpu.CompilerParams(dimension_semantics=sem, vmem_limit_bytes=VMEM_LIMIT)


def _mesh_pos():
    return lax.axis_index("x"), lax.axis_index("y"), lax.axis_index("c")


def _dev_index(px, py, pc):
    return 4 * px + 2 * py + pc


def _other_chips(x, y):
    return [(1 - x, y), (x, 1 - y), (1 - x, 1 - y)]


def _full(shape):
    return pl.BlockSpec(shape, lambda *_: (0,) * len(shape))


def _resident(shape):
    return pl.BlockSpec(shape, lambda *_: (0,) * len(shape), pipeline_mode=pl.Buffered(1))


def _dot_nt(a, w):
    return lax.dot_general(a, w, (((1,), (1,)), ((), ())), preferred_element_type=F32)


def _dot_nn(a, w):
    return jnp.dot(a, w, preferred_element_type=F32)


def _chunks(n, c):
    out, o = [], 0
    while o < n:
        out.append((o, min(c, n - o)))
        o += c
    return out


def _rstd(h):
    return lax.rsqrt(jnp.mean(h * h, axis=-1, keepdims=True) + RMS_EPS)


def _rms_bwd(dyh, yh, r):
    return r * (dyh - yh * jnp.mean(dyh * yh, axis=-1, keepdims=True))


def _silu_grad(a, sig):
    return sig * (1.0 + a * (1.0 - sig))


def _acc_rows(ref, val, first):
    s = jnp.sum(val, axis=0, keepdims=True)

    @pl.when(first)
    def _():
        ref[...] = s

    @pl.when(jnp.logical_not(first))
    def _():
        ref[...] += s


def _row_loop(t_rows, chunk, fn, carry=None):
    def step(i, c):
        return fn(pl.multiple_of(i * chunk, chunk), c)

    return lax.fori_loop(0, t_rows // chunk, step, carry)


def _remote(src, dst, send_sem, recv_sem, to):
    return pltpu.make_async_remote_copy(src_ref=src, dst_ref=dst, send_sem=send_sem, recv_sem=recv_sem,
                                        device_id=to, device_id_type=MESH)


class _SplitRemote:
    def __init__(self, src, dst, send_sem, recv_sem, to, rows, n_chunks):
        units = rows // BF16_ROWS
        n_chunks = max(1, min(n_chunks, units))
        sizes = [(units // n_chunks + (i < units % n_chunks)) * BF16_ROWS for i in range(n_chunks)]
        self.whole = _remote(src, dst, send_sem, recv_sem, to)
        self.parts, o = [], 0
        for n in sizes:
            self.parts.append(_remote(src.at[pl.ds(o, n), :], dst.at[pl.ds(o, n), :], send_sem, recv_sem, to))
            o += n

    def start(self):
        for cp in self.parts:
            cp.start()

    def wait_recv(self):
        self.whole.wait_recv()

    def wait_send(self):
        self.whole.wait_send()


class _Comm:
    def __init__(self, inputs, out_shapes, aliases, scratch, start, finish):
        self.inputs, self.out_shapes, self.aliases, self.scratch = list(inputs), list(out_shapes), dict(aliases), list(scratch)
        self.start, self.finish = start, finish


def _merge_comms(comms):
    inputs, out_shapes, aliases, scratch, spans = [], [], {}, [], []
    for cm in comms:
        spans.append((len(inputs), len(out_shapes), len(scratch), cm))
        aliases.update({len(inputs) + k: len(out_shapes) + v for k, v in cm.aliases.items()})
        inputs += cm.inputs
        out_shapes += cm.out_shapes
        scratch += cm.scratch

    def run(which):
        def fn(ins, outs, scr):
            for i0, o0, s0, cm in spans:
                getattr(cm, which)(ins[i0:i0 + len(cm.inputs)], outs[o0:o0 + len(cm.out_shapes)], scr[s0:s0 + len(cm.scratch)])
        return fn

    return _Comm(inputs, out_shapes, aliases, scratch, run("start"), run("finish"))


def _host_call(body, *, grid, in_specs, out_specs, out_shape, args, name, scratch_shapes=(), comm=None, after=()):
    talks = comm is not None
    if comm is None:
        comm = _Comm([], [], {}, [], lambda *_: None, lambda *_: None)
    n_in, n_out, n_scr = len(args), len(out_shape), len(scratch_shapes)
    c_in, c_out = len(comm.inputs), len(comm.out_shapes)
    n_after = len(after)

    def open_comm(c_ins, c_outs, c_scr):
        if talks:
            _pair_handshake()
        comm.start(c_ins, c_outs, c_scr)

    def hosted(*refs):
        ins, c_ins = refs[:n_in], refs[n_in:n_in + c_in]
        o0 = n_in + c_in + n_after
        outs, c_outs = refs[o0:o0 + n_out], refs[o0 + n_out:o0 + n_out + c_out]
        s0 = o0 + n_out + c_out
        scr, c_scr = refs[s0:s0 + n_scr], refs[s0 + n_scr:]
        if not grid:
            open_comm(c_ins, c_outs, c_scr)
            body(*ins, *outs, *scr)
            comm.finish(c_ins, c_outs, c_scr)
            return
        first = last = None
        for a, n in enumerate(grid):
            f, l = pl.program_id(a) == 0, pl.program_id(a) == n - 1
            first = f if first is None else jnp.logical_and(first, f)
            last = l if last is None else jnp.logical_and(last, l)

        @pl.when(first)
        def _():
            open_comm(c_ins, c_outs, c_scr)

        body(*ins, *outs, *scr)

        @pl.when(last)
        def _():
            comm.finish(c_ins, c_outs, c_scr)

    sem = ("arbitrary",) * len(grid) if grid else None
    params = pltpu.CompilerParams(dimension_semantics=sem, vmem_limit_bytes=VMEM_LIMIT,
                                  collective_id=PAIR_BARRIER_ID if talks else None)
    res = pl.pallas_call(
        hosted, grid=grid, in_specs=list(in_specs) + [ANY] * (c_in + n_after), out_specs=list(out_specs) + [ANY] * c_out,
        out_shape=list(out_shape) + comm.out_shapes, scratch_shapes=list(scratch_shapes) + comm.scratch,
        input_output_aliases={n_in + k: n_out + v for k, v in comm.aliases.items()},
        name=name, compiler_params=params)(*args, *comm.inputs, *after)
    return list(res[:n_out]), list(res[n_out:])


PAIR_BARRIER_ID = 0


def _pair_handshake():
    x, y, c = _mesh_pos()
    barrier = pltpu.get_barrier_semaphore()
    pl.semaphore_signal(barrier, inc=1, device_id=(x, y, 1 - c), device_id_type=MESH)
    pl.semaphore_wait(barrier, 1)


GATHER_SEMS = 10
D2D_CHUNKS = 8


class _Gather:
    def __init__(self, jobs, rows, lo, src_ref, dests, send_sems, recv_sems):
        x, y, c = _mesh_pos()
        me, sib = (x, y, c), (x, y, 1 - c)
        nx, ny, dg = (1 - x, y, c), (x, 1 - y, c), (1 - x, 1 - y, c)
        self.relayed, self.direct, self.relay, self.to_sib, self.sib_fwd = [], [], [], [], []
        for n, (p, r0, nr) in enumerate(jobs):
            assert nr % (2 * BF16_ROWS) == 0
            half = nr // 2

            def rows_of(dev, h, p=p, r0=r0, nr=nr, half=half):
                off, cnt = (r0, nr) if h is None else (r0 + h * half, half)
                return dests[p].at[pl.ds(pl.multiple_of(_dev_index(*dev) * rows[p] + off, BF16_ROWS), cnt), :]

            def mine(h, p=p, r0=r0, nr=nr, half=half):
                off, cnt = (r0, nr) if h is None else (r0 + h * half, half)
                return src_ref.at[pl.ds(lo[p] + off, cnt), :]

            sem = lambda k, n=n: (send_sems.at[GATHER_SEMS * n + k], recv_sems.at[GATHER_SEMS * n + k])
            self.relayed.append([_remote(mine(0), rows_of(me, 0), *sem(0), nx), _remote(mine(1), rows_of(me, 1), *sem(3), ny)])
            self.direct.append([_remote(mine(1), rows_of(me, 1), *sem(1), nx), _remote(mine(0), rows_of(me, 0), *sem(2), ny)])
            self.relay.append([_remote(rows_of(nx, 0), rows_of(nx, 0), *sem(4), ny), _remote(rows_of(ny, 1), rows_of(ny, 1), *sem(5), nx)])
            self.to_sib.append(_SplitRemote(mine(None), rows_of(me, None), *sem(6), sib, nr, D2D_CHUNKS))
            self.sib_fwd.append([_SplitRemote(rows_of(dev, None), rows_of(dev, None), *sem(7 + i), sib, nr, D2D_CHUNKS)
                                 for i, dev in enumerate((nx, ny, dg))])

    def start(self):
        for group in (self.relayed, self.direct):
            for cps in group:
                for cp in cps:
                    cp.start()
        for cp in self.to_sib:
            cp.start()

    def mid(self):
        for first, relay in zip(self.relayed, self.relay):
            for arrived, onward in zip(first, relay):
                arrived.wait_recv()
                onward.start()

    def finish(self):
        for direct, relay, fwd in zip(self.direct, self.relay, self.sib_fwd):
            for k in range(2):
                direct[k].wait_recv()
                fwd[k].start()
            for cp in relay:
                cp.wait_recv()
            fwd[2].start()
        for n in range(len(self.to_sib)):
            self.to_sib[n].wait_recv()
            for cp in self.sib_fwd[n]:
                cp.wait_recv()
            for cp in self.relayed[n] + self.direct[n] + self.relay[n] + [self.to_sib[n]] + self.sib_fwd[n]:
                cp.wait_send()


HBM = pl.BlockSpec(memory_space=pltpu.HBM)
SEM = pl.BlockSpec(memory_space=pltpu.SEMAPHORE)
FLOWS = pltpu.SideEffectType.DATAFLOW_SIDE_EFFECTING


def _in_hbm(a):
    return pltpu.with_memory_space_constraint(a, pltpu.HBM)


def _gather_start(wl, dests, ps, rows):
    lo = [sum(rows[:p]) for p in range(N_BIG)]
    n = len(ps)

    def body(*refs):
        wl_ref, dest_refs = refs[0], refs[1:1 + n]
        sends, recvs = refs[1 + n:1 + 2 * n], refs[1 + 2 * n:1 + 3 * n]
        token = refs[-1]
        x, y, c = _mesh_pos()
        jme = _dev_index(x, y, c)
        for i, p in enumerate(ps):
            mine = dest_refs[i].at[pl.ds(pl.multiple_of(jme * rows[p], BF16_ROWS), rows[p]), :]
            for chip in _other_chips(x, y):
                _remote(wl_ref.at[pl.ds(lo[p], rows[p]), :], mine, sends[i], recvs[i], (*chip, c)).start()
        token[...] = jnp.zeros_like(token)

    thru = [pltpu.HBM(wl.shape, wl.dtype)] + [pltpu.HBM(dests[p].shape, BF16) for p in ps]
    res = pl.pallas_call(
        body, name="gather_start",
        out_shape=tuple([pltpu.SemaphoreType.DMA(())] * (2 * n) + thru + [jax.ShapeDtypeStruct((SUBLANE, LANE), F32)]),
        in_specs=[HBM] * (1 + n), out_specs=tuple([SEM] * (2 * n) + [HBM] * (1 + n) + [VMEM]),
        input_output_aliases={i: 2 * n + i for i in range(1 + n)},
        compiler_params=pltpu.CompilerParams(has_side_effects=FLOWS))(_in_hbm(wl), *[_in_hbm(dests[p]) for p in ps])
    sems = [(res[i], res[n + i]) for i in range(n)]
    return sems, res[2 * n], list(res[2 * n + 1:3 * n + 1]), res[-1]


def _gather_wait(wl, dest, sems, after, r, name):
    def body(wl_ref, dest_ref, send_sem, recv_sem, after_ref, wl_out, dest_out):
        x, y, c = _mesh_pos()
        three = dest_ref.at[pl.ds(0, 3 * r), :]
        cp = _remote(three, three, send_sem, recv_sem, (x, y, 1 - c))
        cp.wait_send()
        cp.wait_recv()

    res = pl.pallas_call(
        body, name=name, out_shape=(pltpu.HBM(wl.shape, wl.dtype), pltpu.HBM(dest.shape, dest.dtype)),
        in_specs=[HBM, HBM, SEM, SEM, ANY], out_specs=(HBM, HBM), input_output_aliases={0: 0, 1: 1},
        compiler_params=pltpu.CompilerParams(has_side_effects=FLOWS))(wl, dest, sems[0], sems[1], after)
    return res[0], res[1]


def _forward_comm(dest, r):
    def descs(ins, outs, scr):
        x, y, c = _mesh_pos()
        cps = []
        for k, chip in enumerate(_other_chips(x, y)):
            blk = outs[0].at[pl.ds(pl.multiple_of(_dev_index(*chip, c) * r, BF16_ROWS), r), :]
            cps.append(_SplitRemote(blk, blk, scr[0].at[k], scr[1].at[k], (x, y, 1 - c), r, D2D_CHUNKS))
        return cps

    def start(ins, outs, scr):
        for cp in descs(ins, outs, scr):
            cp.start()

    def finish(ins, outs, scr):
        cps = descs(ins, outs, scr)
        for cp in cps:
            cp.wait_recv()
        for cp in cps:
            cp.wait_send()

    return _Comm([dest], [jax.ShapeDtypeStruct(dest.shape, dest.dtype)], {0: 0},
                 [pltpu.SemaphoreType.DMA((3,)), pltpu.SemaphoreType.DMA((3,))], start, finish)


def _forward_now(dest, r, name):
    _, (dest,) = _host_call(lambda: None, grid=(), in_specs=[], out_specs=[], out_shape=[], args=(), name=name,
                            comm=_forward_comm(dest, r))
    return dest


def _pair_comm(g, r):
    d = g.shape[1]

    def descs(ins, outs, scr):
        x, y, c = _mesh_pos()
        chips = [(x, y)] + _other_chips(x, y)
        return [_SplitRemote(ins[0].at[pl.ds(pl.multiple_of(_dev_index(*chip, 1 - c) * r, BF16_ROWS), r), :], outs[0].at[k],
                             scr[0].at[k], scr[1].at[k], (x, y, 1 - c), r, D2D_CHUNKS) for k, chip in enumerate(chips)]

    def start(ins, outs, scr):
        for cp in descs(ins, outs, scr):
            cp.start()

    def finish(ins, outs, scr):
        cps = descs(ins, outs, scr)
        for cp in cps:
            cp.wait_recv()
        for cp in cps:
            cp.wait_send()

    comm = _Comm([g], [jax.ShapeDtypeStruct((4, r, d), BF16)], {},
                 [pltpu.SemaphoreType.DMA((4,)), pltpu.SemaphoreType.DMA((4,))], start, finish)
    return comm


def _pair_sum(g, pair, r, name, after=()):
    d = g.shape[1]

    def body(g_ref, p_ref, *rest):
        o_ref, gbuf, pbuf, sems = rest[len(after):]
        x, y, c = _mesh_pos()
        loads = [pltpu.make_async_copy(p_ref.at[pl.ds(1, 3)], pbuf, sems.at[3])]
        for k, chip in enumerate(_other_chips(x, y)):
            j = _dev_index(*chip, c)
            loads.append(pltpu.make_async_copy(g_ref.at[pl.ds(pl.multiple_of(j * r, BF16_ROWS), r), :], gbuf.at[k], sems.at[k]))
        for cp in loads:
            cp.start()
        for cp in loads:
            cp.wait()
        for k in range(3):
            o_ref[k] = (gbuf[k].astype(F32) + pbuf[k].astype(F32)).astype(BF16)

    return pl.pallas_call(
        body, out_shape=jax.ShapeDtypeStruct((3, r, d), BF16), in_specs=[ANY] * (2 + len(after)), out_specs=VMEM,
        scratch_shapes=[pltpu.VMEM((3, r, d), BF16), pltpu.VMEM((3, r, d), BF16), pltpu.SemaphoreType.DMA((4,))],
        name=name, compiler_params=_cparams())(g, pair, *after)


def _chip_start(sums, name):
    n = len(sums)

    def body(*refs):
        srcs, lands = refs[:n], refs[n:2 * n]
        sends, recvs = refs[2 * n:3 * n], refs[3 * n:4 * n]
        x, y, c = _mesh_pos()
        for i in range(n):
            for k, chip in enumerate(_other_chips(x, y)):
                _remote(srcs[i].at[k], lands[i].at[k], sends[i], recvs[i], (*chip, c)).start()
        refs[-1][...] = jnp.zeros_like(refs[-1])

    zones = [pltpu.HBM(s.shape, s.dtype) for s in sums]
    res = pl.pallas_call(
        body, name=name,
        out_shape=tuple([pltpu.SemaphoreType.DMA(())] * (2 * n) + zones + zones + [jax.ShapeDtypeStruct((SUBLANE, LANE), F32)]),
        in_specs=[HBM] * (2 * n), out_specs=tuple([SEM] * (2 * n) + [HBM] * (2 * n) + [VMEM]),
        input_output_aliases={i: 2 * n + i for i in range(2 * n)},
        compiler_params=pltpu.CompilerParams(has_side_effects=FLOWS))(
            *[_in_hbm(s) for s in sums], *[_in_hbm(lax.empty(s.shape, s.dtype)) for s in sums])
    flights = [((res[i], res[n + i]), res[2 * n + i], res[3 * n + i]) for i in range(n)]
    return flights, res[-1]


def _chip_wait(sums, land, sems, after, name):
    def body(sums_ref, land_ref, send_sem, recv_sem, after_ref, sums_out, land_out):
        x, y, c = _mesh_pos()
        cp = _remote(sums_ref, land_ref, send_sem, recv_sem, (x, y, 1 - c))
        cp.wait_send()
        cp.wait_recv()

    res = pl.pallas_call(
        body, name=name, out_shape=(pltpu.HBM(sums.shape, sums.dtype), pltpu.HBM(land.shape, land.dtype)),
        in_specs=[HBM, HBM, SEM, SEM, ANY], out_specs=(HBM, HBM), input_output_aliases={0: 0, 1: 1},
        compiler_params=pltpu.CompilerParams(has_side_effects=FLOWS))(sums, land, sems[0], sems[1], after)
    return res[1]


class _CopyThrough:
    def __init__(self, src_ref, dst_ref, dst_row0, n_rows, buf, sem_in, sem_out):
        rc = n_rows // COPY_PIECES
        piece = lambda ref, o: ref.at[pl.ds(o, rc), :]
        self.loads = [pltpu.make_async_copy(piece(src_ref, k * rc), piece(buf, k * rc), sem_in) for k in range(COPY_PIECES)]
        self.stores = [pltpu.make_async_copy(piece(buf, k * rc), piece(dst_ref, dst_row0 + k * rc), sem_out) for k in range(COPY_PIECES)]
        self.all_in = pltpu.make_async_copy(src_ref, buf, sem_in)
        self.all_out = pltpu.make_async_copy(buf, dst_ref.at[pl.ds(dst_row0, n_rows), :], sem_out)

    def load(self):
        for cp in self.loads:
            cp.start()

    def store(self):
        self.all_in.wait()
        for cp in self.stores:
            cp.start()

    def done(self):
        self.all_out.wait()


def _gather_first(shards, sm, jobs, x2, tgt2, t_rows, x0):
    d = shards[0].shape[1]
    rows = [w.shape[0] for w in shards]
    lo = [sum(rows[:p]) for p in range(N_BIG)]
    n_sems = GATHER_SEMS * len(jobs)
    seq = x2.shape[0]
    assert x0 == ROW_ALIGN and seq % ROW_ALIGN == 0 and d == N_DEV * LANE

    def body(s0, s1, s2, s3, s4, sm_ref, x_ref, tgt_ref, wl_ref, o0, o1, o2, o3, o4, sa_ref, h0_ref, tp_ref,
             wl_v, x_v, tgt_v, heads_v, sa_v, send_sems, recv_sems, ssend, srecv, local_sems, sems_in, sems_out):
        dests = (o0, o1, o2, o3, o4)
        x, y, c = _mesh_pos()
        me = (x, y, c)
        jme = _dev_index(*me)
        padded = [_CopyThrough(x_ref, h0_ref, x0, seq, x_v, sems_in.at[0], sems_out.at[0]),
                  _CopyThrough(tgt_ref, tp_ref, x0, seq, tgt_v, sems_in.at[1], sems_out.at[1])]
        for cp in padded:
            cp.load()
        for p, ref in enumerate((s0, s1, s2, s3, s4)):
            wl_v[pl.ds(lo[p], rows[p]), :] = ref[...].astype(BF16)
        gather = _Gather(jobs, rows, lo, wl_v, dict(enumerate(dests)), send_sems, recv_sems)
        gather.start()
        peers = [(x, y, 1 - c)] + [(*chip, pc) for pc in (c, 1 - c) for chip in _other_chips(x, y)]
        smalls = [_remote(sm_ref, sa_ref.at[jme], ssend.at[k], srecv.at[k], to) for k, to in enumerate(peers)]
        for cp in smalls:
            cp.start()
        mine = [pltpu.make_async_copy(wl_v.at[pl.ds(lo[p], rows[p]), :],
                                      dests[p].at[pl.ds(pl.multiple_of(jme * rows[p], BF16_ROWS), rows[p]), :], local_sems.at[p])
                for p in range(N_BIG)]
        mine.append(pltpu.make_async_copy(wl_v, wl_ref, local_sems.at[N_BIG]))
        mine.append(pltpu.make_async_copy(sm_ref, sa_ref.at[jme], local_sems.at[N_BIG + 1]))
        for cp in mine:
            cp.start()
        later = [p for p in range(N_BIG) if p not in {j[0] for j in jobs}]
        own = [_SplitRemote(wl_v.at[pl.ds(lo[p], rows[p]), :],
                            dests[p].at[pl.ds(pl.multiple_of(jme * rows[p], BF16_ROWS), rows[p]), :],
                            ssend.at[7 + i], srecv.at[7 + i], (x, y, 1 - c), rows[p], D2D_CHUNKS) for i, p in enumerate(later)]
        for cp in own:
            cp.start()
        for cp in padded:
            cp.store()
        gather.mid()
        for cp in smalls + own:
            cp.wait_recv()
        mine[-1].wait()
        to_v = pltpu.make_async_copy(sa_ref, sa_v, local_sems.at[N_BIG + 1])
        to_v.start()
        to_v.wait()
        head, zeros = heads_v.at[0], heads_v.at[1]
        head[...] = jnp.zeros_like(head)
        zeros[...] = jnp.zeros_like(zeros)
        for j in range(N_DEV):
            head[pl.ds(x0 - N_META, N_META), pl.ds(j * LANE, LANE)] = sa_v[j, pl.ds(0, N_META), :]
        heads = [pltpu.make_async_copy(head, h0_ref.at[pl.ds(0, x0), :], local_sems.at[N_BIG + 1]),
                 pltpu.make_async_copy(zeros, tp_ref.at[pl.ds(0, x0), :], local_sems.at[N_BIG + 2])]
        for cp in heads:
            cp.start()
        gather.finish()
        for cp in smalls + own:
            cp.wait_send()
        for cp in mine[:-1] + heads:
            cp.wait()
        for cp in padded:
            cp.done()

    out_shape = [jax.ShapeDtypeStruct((sum(rows), d), BF16)]
    out_shape += [jax.ShapeDtypeStruct((N_DEV * r, d), BF16) for r in rows]
    out_shape.append(jax.ShapeDtypeStruct((N_DEV,) + sm.shape, F32))
    out_shape += [jax.ShapeDtypeStruct((t_rows, d), F32)] * 2
    res = pl.pallas_call(
        body, out_shape=out_shape, in_specs=[VMEM] * 6 + [ANY] * 2, out_specs=[ANY] * 9,
        scratch_shapes=[pltpu.VMEM((sum(rows), d), BF16), pltpu.VMEM((seq, d), F32), pltpu.VMEM((seq, d), F32),
                        pltpu.VMEM((2, ROW_ALIGN, d), F32), pltpu.VMEM((N_DEV,) + sm.shape, F32),
                        pltpu.SemaphoreType.DMA((n_sems,)), pltpu.SemaphoreType.DMA((n_sems,)),
                        pltpu.SemaphoreType.DMA((7 + N_BIG,)), pltpu.SemaphoreType.DMA((7 + N_BIG,)),
                        pltpu.SemaphoreType.DMA((N_BIG + 3,)), pltpu.SemaphoreType.DMA((2,)), pltpu.SemaphoreType.DMA((2,))],
        name="gather_first", compiler_params=_cparams())(*shards, sm, x2, tgt2)
    return res[0], list(res[1:1 + N_BIG]), res[1 + N_BIG], res[2 + N_BIG], res[3 + N_BIG]


def _in_proj(h0, g1, win_t, tm, comm):
    t_rows, d = h0.shape
    e = win_t.shape[0]

    def body(h_ref, g_ref, w_ref, xn_ref, hin_ref):
        h = h_ref[...]
        xn = ((h * _rstd(h)) * g_ref[...]).astype(BF16)
        xn_ref[...] = xn
        for o, n in _chunks(e, N_CHUNK):
            hin_ref[:, pl.ds(o, n)] = _dot_nt(xn, w_ref[pl.ds(o, n), :])

    return _host_call(
        body, grid=(t_rows // tm,),
        in_specs=[pl.BlockSpec((tm, d), lambda i: (i, 0)), _full((1, d)), _resident((e, d))],
        out_specs=[pl.BlockSpec((tm, d), lambda i: (i, 0)), pl.BlockSpec((tm, e), lambda i: (i, 0))],
        out_shape=[jax.ShapeDtypeStruct((t_rows, d), BF16), jax.ShapeDtypeStruct((t_rows, e), F32)],
        args=(h0, g1, win_t), name="in_proj", comm=comm)


def _tap_slot(off):
    return off % SUBLANE, (off // SUBLANE) * SUBLANE


def _fill_shifted(sh_ref, base_ref, residues, n_rows):
    for r in residues:
        sh_ref[r] = base_ref[pl.ds(r, n_rows), :]


def _mix_conv_fwd(hin, wa, wb, bb, wa_w, comm):
    t_rows = hin.shape[0]
    nt = wa_w // LANE
    ka, kb = wa.shape[0], wb.shape[0]
    nr = CONV_HALO + t_rows

    def body(bg_ref, cg_ref, ha_ref, val_ref, gt_ref, wa_ref, wb_ref, bb_ref, ya_ref, z_ref, base, sh):
        base[pl.ds(0, CONV_HALO), :] = jnp.zeros((CONV_HALO, LANE), F32)
        base[pl.ds(nr, SUBLANE), :] = jnp.zeros((SUBLANE, LANE), F32)

        def conv(w_ref, k_taps, b, n):
            acc = None
            for k in range(k_taps):
                r, q = _tap_slot(CONV_HALO - (k_taps - 1) + k)
                term = w_ref[pl.ds(k, 1), :] * sh[r, pl.ds(b + q, n), :]
                acc = term if acc is None else acc + term
            return acc

        def fill_a(b, c):
            base[pl.ds(CONV_HALO + b, CONV_CHUNK), :] = cg_ref[pl.ds(b, CONV_CHUNK), :] * ha_ref[pl.ds(b, CONV_CHUNK), :]
            return c

        _row_loop(t_rows, CONV_CHUNK, fill_a)
        _fill_shifted(sh, base, sorted({_tap_slot(CONV_HALO - (ka - 1) + k)[0] for k in range(ka)}), nr)

        def out_a(b, c):
            ya_ref[pl.ds(b, CONV_CHUNK), :] = (bg_ref[pl.ds(b, CONV_CHUNK), :] * conv(wa_ref, ka, b, CONV_CHUNK)).astype(BF16)
            return c

        _row_loop(t_rows, CONV_CHUNK, out_a)

        def fill_b(b, c):
            base[pl.ds(CONV_HALO + b, CONV_CHUNK), :] = (val_ref[pl.ds(b, CONV_CHUNK), :]
                                                          * jax.nn.sigmoid(gt_ref[pl.ds(b, CONV_CHUNK), :]))
            return c

        _row_loop(t_rows, CONV_CHUNK, fill_b)
        _fill_shifted(sh, base, range(SUBLANE), nr)

        def out_b(b, c):
            z_ref[pl.ds(b, CONV_CHUNK), :] = conv(wb_ref, kb, b, CONV_CHUNK) + bb_ref[...]
            return c

        _row_loop(t_rows, CONV_CHUNK, out_b)

    def col(g):
        return pl.BlockSpec((t_rows, LANE), lambda i, g=g: (0, g * nt + i))

    tile = lambda rows: pl.BlockSpec((rows, LANE), lambda i: (0, i))
    return _host_call(
        body, grid=(nt,),
        in_specs=[col(0), col(1), col(2), col(3), col(4), tile(ka), tile(kb), tile(1)],
        out_specs=[tile(t_rows), tile(t_rows)],
        out_shape=[jax.ShapeDtypeStruct((t_rows, wa_w), BF16), jax.ShapeDtypeStruct((t_rows, wa_w), F32)],
        scratch_shapes=[pltpu.VMEM((nr + SUBLANE, LANE), F32), pltpu.VMEM((SUBLANE, nr, LANE), F32)],
        args=(hin, hin, hin, hin, hin, wa, wb, bb), name="mix_conv_fwd", comm=comm)


def _ln_parts(z, lg, lb):
    mu = jnp.mean(z, axis=-1, keepdims=True)
    zc = z - mu
    rstd = lax.rsqrt(jnp.mean(zc * zc, axis=-1, keepdims=True) + LN_EPS)
    zh = zc * rstd
    return zh, rstd, zh * lg + lb


def _mix_ln_fwd(ya, z, lg, lb, tm, comm):
    t_rows, w = z.shape

    def body(ya_ref, z_ref, lg_ref, lb_ref, y_ref):
        _, _, ln = _ln_parts(z_ref[...], lg_ref[...], lb_ref[...])
        y_ref[:, pl.ds(0, w)] = ya_ref[...]
        y_ref[:, pl.ds(w, w)] = (ln * jax.nn.sigmoid(ln)).astype(BF16)

    blk = pl.BlockSpec((tm, w), lambda i: (i, 0))
    res, extra = _host_call(body, grid=(t_rows // tm,), in_specs=[blk, blk, _full((1, w)), _full((1, w))],
                            out_specs=[pl.BlockSpec((tm, 2 * w), lambda i: (i, 0))],
                            out_shape=[jax.ShapeDtypeStruct((t_rows, 2 * w), BF16)], args=(ya, z, lg, lb), name="mix_ln_fwd", comm=comm)
    return res[0], extra


def _out_proj(y, w_out, h0, g2, g3, tm, comm):
    t_rows, d = h0.shape

    def body(y_ref, w_ref, h0_ref, g2_ref, g3_ref, mix_ref, h1_ref, xn2_ref):
        mix = _dot_nn(y_ref[...], w_ref[...])
        mix_ref[...] = mix
        h1 = h0_ref[...] + (mix * _rstd(mix)) * g2_ref[...]
        h1_ref[...] = h1
        xn2_ref[...] = ((h1 * _rstd(h1)) * g3_ref[...]).astype(BF16)

    blk = pl.BlockSpec((tm, d), lambda i: (i, 0))
    return _host_call(
        body, grid=(t_rows // tm,), in_specs=[blk, _resident(w_out.shape), blk, _full((1, d)), _full((1, d))],
        out_specs=[blk, blk, blk],
        out_shape=[jax.ShapeDtypeStruct((t_rows, d), F32), jax.ShapeDtypeStruct((t_rows, d), F32),
                   jax.ShapeDtypeStruct((t_rows, d), BF16)],
        args=(y, w_out, h0, g2, g3), name="out_proj", comm=comm)


def _gate_up(xn2, wg_t, wu_t, tm, comm):
    t_rows, d = xn2.shape
    f = wg_t.shape[0]

    def body(x_ref, wg_ref, wu_ref, a_ref, u_ref, s_ref):
        xn = x_ref[...]
        for o, n in _chunks(f, N_CHUNK):
            a = _dot_nt(xn, wg_ref[pl.ds(o, n), :])
            u = _dot_nt(xn, wu_ref[pl.ds(o, n), :])
            a_ref[:, pl.ds(o, n)] = a.astype(BF16)
            u_ref[:, pl.ds(o, n)] = u.astype(BF16)
            s_ref[:, pl.ds(o, n)] = ((a * jax.nn.sigmoid(a)) * u).astype(BF16)

    blk = pl.BlockSpec((tm, f), lambda i: (i, 0))
    return _host_call(
        body, grid=(t_rows // tm,),
        in_specs=[pl.BlockSpec((tm, d), lambda i: (i, 0)), _resident((f, d)), _resident((f, d))],
        out_specs=[blk, blk, blk], out_shape=[jax.ShapeDtypeStruct((t_rows, f), BF16)] * 3,
        args=(xn2, wg_t, wu_t), name="gate_up", comm=comm)


def _down_loss(s, wd, h1, tgt, g4, tm, x0):
    t_rows, d = h1.shape
    f = wd.shape[0]

    def body(s_ref, w_ref, h1_ref, tgt_ref, g4_ref, dh2_ref, dff_ref, dg4_ref, loss_ref):
        i = pl.program_id(0)
        ff = _dot_nn(s_ref[...], w_ref[...])
        r4 = _rstd(ff)
        fh = ff * r4
        g4 = g4_ref[...]
        h2 = h1_ref[...] + fh * g4
        row = i * tm + lax.broadcasted_iota(jnp.int32, (tm, 1), 0)
        diff = jnp.where(row >= x0, h2 - tgt_ref[...], 0.0)
        dh2 = diff / d
        dh2_ref[...] = dh2
        dff_ref[...] = _rms_bwd(dh2 * g4, fh, r4).astype(BF16)
        _acc_rows(dg4_ref, dh2 * fh, i == 0)
        _acc_rows(loss_ref, diff * diff, i == 0)

    blk = pl.BlockSpec((tm, d), lambda i: (i, 0))
    res, _ = _host_call(
        body, grid=(t_rows // tm,),
        in_specs=[pl.BlockSpec((tm, f), lambda i: (i, 0)), _resident((f, d)), blk, blk, _full((1, d))],
        out_specs=[blk, blk, _full((1, d)), _full((1, d))],
        out_shape=[jax.ShapeDtypeStruct((t_rows, d), F32), jax.ShapeDtypeStruct((t_rows, d), BF16),
                   jax.ShapeDtypeStruct((1, d), F32), jax.ShapeDtypeStruct((1, d), F32)],
        args=(s, wd, h1, tgt, g4), name="down_loss")
    return res


def _bwd_down(dff, wd, a, u, tm, comm):
    t_rows, d = dff.shape
    f = wd.shape[0]

    def body(dff_ref, w_ref, a_ref, u_ref, da_ref, du_ref):
        dff_v = dff_ref[...]
        for o, n in _chunks(f, N_CHUNK):
            ds = _dot_nt(dff_v, w_ref[pl.ds(o, n), :])
            av = a_ref[:, pl.ds(o, n)].astype(F32)
            uv = u_ref[:, pl.ds(o, n)].astype(F32)
            sig = jax.nn.sigmoid(av)
            da_ref[:, pl.ds(o, n)] = (ds * uv * _silu_grad(av, sig)).astype(BF16)
            du_ref[:, pl.ds(o, n)] = (ds * (av * sig)).astype(BF16)

    blk = pl.BlockSpec((tm, f), lambda i: (i, 0))
    return _host_call(
        body, grid=(t_rows // tm,),
        in_specs=[pl.BlockSpec((tm, d), lambda i: (i, 0)), _resident((f, d)), blk, blk],
        out_specs=[blk, blk], out_shape=[jax.ShapeDtypeStruct((t_rows, f), BF16)] * 2,
        args=(dff, wd, a, u), name="bwd_down", comm=comm)


def _wgrad(a, b, name, after=()):
    d = b.shape[1]
    t_rows = b.shape[0]
    stacked = a.ndim == 3
    n = a.shape[-1]
    groups = a.shape[0] if stacked else 1
    steps = 1 if stacked else 2
    tile = max(t for t in range(LANE, min(n // steps, WGRAD_TILE_MAX) + 1, LANE) if n % t == 0)
    tiles = n // tile

    def body(a_ref, b_ref, o_ref):
        o_ref[...] = lax.dot_general(a_ref[...], b_ref[...], (((0,), (0,)), ((), ())),
                                     preferred_element_type=F32).astype(BF16)

    if stacked:
        a_spec = pl.BlockSpec((None, t_rows, tile), lambda g, i: (g, 0, i))
    else:
        a_spec = pl.BlockSpec((t_rows, tile), lambda g, i: (0, i))
    res, _ = _host_call(
        body, grid=(groups, tiles), in_specs=[a_spec, _resident((t_rows, d))],
        out_specs=[pl.BlockSpec((tile, d), lambda g, i: (g * tiles + i, 0))],
        out_shape=[jax.ShapeDtypeStruct((groups * n, d), BF16)], args=(a, b), name=name, after=after)
    return res[0]


def _bwd_ffn_in(da, du, wg_t, wu_t, h1, dh2, g3, tm, comm):
    t_rows, d = h1.shape
    f = wg_t.shape[0]

    def body(da_ref, du_ref, wg_ref, wu_ref, h1_ref, dh2_ref, g3_ref, dh1_ref, dg3_ref):
        dxn2 = _dot_nn(da_ref[...], wg_ref[...]) + _dot_nn(du_ref[...], wu_ref[...])
        h1 = h1_ref[...]
        r3 = _rstd(h1)
        h1h = h1 * r3
        _acc_rows(dg3_ref, dxn2 * h1h, pl.program_id(0) == 0)
        dh1_ref[...] = dh2_ref[...] + _rms_bwd(dxn2 * g3_ref[...], h1h, r3)

    blk = pl.BlockSpec((tm, d), lambda i: (i, 0))
    blkf = pl.BlockSpec((tm, f), lambda i: (i, 0))
    return _host_call(
        body, grid=(t_rows // tm,),
        in_specs=[blkf, blkf, _resident((f, d)), _resident((f, d)), blk, blk, _full((1, d))],
        out_specs=[blk, _full((1, d))],
        out_shape=[jax.ShapeDtypeStruct((t_rows, d), F32), jax.ShapeDtypeStruct((1, d), F32)],
        args=(da, du, wg_t, wu_t, h1, dh2, g3), name="bwd_ffn_in", comm=comm)


def _bwd_out_proj(dh1, mix, w_out, g2, tm, after):
    t_rows, d = dh1.shape

    def body(dh1_ref, mix_ref, w_ref, g2_ref, dmix_ref, dy_ref, dg2_ref):
        mix = mix_ref[...]
        r2 = _rstd(mix)
        mh = mix * r2
        dh1 = dh1_ref[...]
        _acc_rows(dg2_ref, dh1 * mh, pl.program_id(0) == 0)
        dmix = _rms_bwd(dh1 * g2_ref[...], mh, r2).astype(BF16)
        dmix_ref[...] = dmix
        dy_ref[...] = _dot_nt(dmix, w_ref[...])

    blk = pl.BlockSpec((tm, d), lambda i: (i, 0))
    res, _ = _host_call(
        body, grid=(t_rows // tm,), in_specs=[blk, blk, _resident(w_out.shape), _full((1, d))],
        out_specs=[blk, blk, _full((1, d))],
        out_shape=[jax.ShapeDtypeStruct((t_rows, d), BF16), jax.ShapeDtypeStruct((t_rows, d), F32),
                   jax.ShapeDtypeStruct((1, d), F32)],
        args=(dh1, mix, w_out, g2), name="bwd_out_proj", after=after)
    return res


def _mix_ln_bwd(z, dy, lg, lb, tm):
    t_rows, w = z.shape

    def body(z_ref, dyb_ref, lg_ref, lb_ref, dz_ref, dlg_ref, dlb_ref, dbb_ref):
        first = pl.program_id(0) == 0
        lg = lg_ref[...]
        zh, rstd, ln = _ln_parts(z_ref[...], lg, lb_ref[...])
        dln = dyb_ref[...] * _silu_grad(ln, jax.nn.sigmoid(ln))
        _acc_rows(dlg_ref, dln * zh, first)
        _acc_rows(dlb_ref, dln, first)
        dzh = dln * lg
        dz = rstd * (dzh - jnp.mean(dzh, axis=-1, keepdims=True) - zh * jnp.mean(dzh * zh, axis=-1, keepdims=True))
        dz_ref[...] = dz
        _acc_rows(dbb_ref, dz, first)

    blk = pl.BlockSpec((tm, w), lambda i: (i, 0))
    vec = _full((1, w))
    res, _ = _host_call(
        body, grid=(t_rows // tm,), in_specs=[blk, pl.BlockSpec((tm, w), lambda i: (i, 1)), vec, vec],
        out_specs=[blk, vec, vec, vec],
        out_shape=[jax.ShapeDtypeStruct((t_rows, w), F32)] + [jax.ShapeDtypeStruct((1, w), F32)] * 3,
        args=(z, dy, lg, lb), name="mix_ln_bwd")
    return res


def _mix_conv_bwd(hin, dy, dz, wa, wb, wa_w, comm):
    t_rows = hin.shape[0]
    nt = wa_w // LANE
    ka, kb = wa.shape[0], wb.shape[0]
    nr = CONV_HALO + t_rows
    kb_rows = -(-kb // SUBLANE) * SUBLANE

    def body(bg_ref, cg_ref, ha_ref, val_ref, gt_ref, dya_ref, dz_ref, wa_ref, wb_ref,
             dh_ref, dwa_ref, dwb_ref, base, sh, based, shd, tmp, wbc):
        zeros = lambda n: jnp.zeros((n, LANE), F32)
        base[pl.ds(0, CONV_HALO), :] = zeros(CONV_HALO)
        base[pl.ds(nr, SUBLANE), :] = zeros(SUBLANE)
        based[pl.ds(t_rows, CONV_HALO + SUBLANE), :] = zeros(CONV_HALO + SUBLANE)

        def fwd_slot(k_taps, k):
            return _tap_slot(CONV_HALO - (k_taps - 1) + k)

        def bwd_slot(k_taps, k):
            return _tap_slot(k_taps - 1 - k)

        def conv(w_ref, k_taps, src, slot, b, n):
            acc = None
            for k in range(k_taps):
                r, q = slot(k_taps, k)
                term = w_ref[pl.ds(k, 1), :] * src[r, pl.ds(b + q, n), :]
                acc = term if acc is None else acc + term
            return acc

        def by_residue(k_taps, slot):
            groups = {}
            for k in range(k_taps):
                r, q = slot(k_taps, k)
                groups.setdefault(r, []).append((k, q // SUBLANE))
            return groups

        def wgrad_loop(w_ref, k_taps):
            n_sub = WGRAD_ROWS // SUBLANE
            for k in range(k_taps):
                wbc[k] = jnp.broadcast_to(w_ref[pl.ds(k, 1), :], (SUBLANE, LANE))
            fwd, bwd = by_residue(k_taps, fwd_slot), by_residue(k_taps, bwd_slot)

            def window(src, r, taps, b):
                span = n_sub + max(qi for _, qi in taps)
                return [src[r, pl.ds(b + SUBLANE * i, SUBLANE), :] for i in range(span)]

            def step(b, accs):
                accs = list(accs)
                dv = [based[pl.ds(b + SUBLANE * j, SUBLANE), :] for j in range(n_sub)]
                for r, taps in fwd.items():
                    win = window(sh, r, taps, b)
                    for k, qi in taps:
                        t = dv[0] * win[qi]
                        for j in range(1, n_sub):
                            t = t + dv[j] * win[qi + j]
                        accs[k] = accs[k] + t
                outs = [None] * n_sub
                for r, taps in bwd.items():
                    win = window(shd, r, taps, b)
                    for k, qi in taps:
                        wk = wbc[k]
                        for j in range(n_sub):
                            term = wk * win[qi + j]
                            outs[j] = term if outs[j] is None else outs[j] + term
                for j in range(n_sub):
                    tmp[pl.ds(b + SUBLANE * j, SUBLANE), :] = outs[j]
                return tuple(accs)

            return _row_loop(t_rows, WGRAD_ROWS, step, tuple(zeros(SUBLANE) for _ in range(k_taps)))

        def store_taps(ref, accs, rows):
            for k, acc in enumerate(accs):
                ref[pl.ds(k, 1), :] = jnp.sum(acc, axis=0, keepdims=True)
            if rows > len(accs):
                ref[pl.ds(len(accs), rows - len(accs)), :] = zeros(rows - len(accs))

        def fill_a(b, c):
            sl = pl.ds(b, CONV_CHUNK)
            base[pl.ds(CONV_HALO + b, CONV_CHUNK), :] = cg_ref[sl, :] * ha_ref[sl, :]
            based[sl, :] = dya_ref[sl, :] * bg_ref[sl, :]
            return c

        _row_loop(t_rows, CONV_CHUNK, fill_a)
        _fill_shifted(sh, base, sorted({fwd_slot(ka, k)[0] for k in range(ka)}), nr)
        _fill_shifted(shd, based, sorted({bwd_slot(ka, k)[0] for k in range(ka)}), nr)

        def d_bgate(b, c):
            sl = pl.ds(b, CONV_CHUNK)
            dh_ref[0, sl, :] = (dya_ref[sl, :] * conv(wa_ref, ka, sh, fwd_slot, b, CONV_CHUNK)).astype(BF16)
            return c

        _row_loop(t_rows, CONV_CHUNK, d_bgate)
        store_taps(dwa_ref, wgrad_loop(wa_ref, ka), SUBLANE)

        def d_ch(b, c):
            sl = pl.ds(b, CONV_CHUNK)
            dua = tmp[sl, :]
            dh_ref[1, sl, :] = (dua * ha_ref[sl, :]).astype(BF16)
            dh_ref[2, sl, :] = (dua * cg_ref[sl, :]).astype(BF16)
            return c

        _row_loop(t_rows, CONV_CHUNK, d_ch)

        def fill_b(b, c):
            sl = pl.ds(b, CONV_CHUNK)
            base[pl.ds(CONV_HALO + b, CONV_CHUNK), :] = val_ref[sl, :] * jax.nn.sigmoid(gt_ref[sl, :])
            based[sl, :] = dz_ref[sl, :]
            return c

        _row_loop(t_rows, CONV_CHUNK, fill_b)
        _fill_shifted(sh, base, range(SUBLANE), nr)
        _fill_shifted(shd, based, range(SUBLANE), nr)
        store_taps(dwb_ref, wgrad_loop(wb_ref, kb), kb_rows)

        def d_glu(b, c):
            sl = pl.ds(b, CONV_CHUNK)
            dgg = tmp[sl, :]
            sig = jax.nn.sigmoid(gt_ref[sl, :])
            dh_ref[3, sl, :] = (dgg * sig).astype(BF16)
            dh_ref[4, sl, :] = (dgg * val_ref[sl, :] * (sig * (1.0 - sig))).astype(BF16)
            return c

        _row_loop(t_rows, CONV_CHUNK, d_glu)

    def col(g):
        return pl.BlockSpec((t_rows, LANE), lambda i, g=g: (0, g * nt + i))

    tile = lambda rows: pl.BlockSpec((rows, LANE), lambda i: (0, i))
    return _host_call(
        body, grid=(nt,),
        in_specs=[col(0), col(1), col(2), col(3), col(4), tile(t_rows), tile(t_rows), tile(ka), tile(kb)],
        out_specs=[pl.BlockSpec((5, t_rows, LANE), lambda i: (0, 0, i)), tile(SUBLANE), tile(kb_rows)],
        out_shape=[jax.ShapeDtypeStruct((5, t_rows, wa_w), BF16), jax.ShapeDtypeStruct((SUBLANE, wa_w), F32),
                   jax.ShapeDtypeStruct((kb_rows, wa_w), F32)],
        scratch_shapes=[pltpu.VMEM((nr + SUBLANE, LANE), F32), pltpu.VMEM((SUBLANE, nr, LANE), F32),
                        pltpu.VMEM((nr + SUBLANE, LANE), F32), pltpu.VMEM((SUBLANE, nr, LANE), F32),
                        pltpu.VMEM((t_rows, LANE), F32), pltpu.VMEM((kb_rows, SUBLANE, LANE), F32)],
        args=(hin, hin, hin, hin, hin, dy, dz, wa, wb), name="mix_conv_bwd", comm=comm)


def _bwd_in_proj(dh5, win_t, h0, dh1, g1, tm, comm):
    t_rows, d = h0.shape
    groups, _, w = dh5.shape

    def body(dh_ref, w_ref, h0_ref, dh1_ref, g1_ref, dh0_ref, dg1_ref):
        dxn1 = None
        for g in range(groups):
            part = _dot_nn(dh_ref[g], w_ref[pl.ds(g * w, w), :])
            dxn1 = part if dxn1 is None else dxn1 + part
        h0 = h0_ref[...]
        r1 = _rstd(h0)
        h0h = h0 * r1
        _acc_rows(dg1_ref, dxn1 * h0h, pl.program_id(0) == 0)
        dh0_ref[...] = dh1_ref[...] + _rms_bwd(dxn1 * g1_ref[...], h0h, r1)

    blk = pl.BlockSpec((tm, d), lambda i: (i, 0))
    return _host_call(
        body, grid=(t_rows // tm,),
        in_specs=[pl.BlockSpec((groups, tm, w), lambda i: (0, i, 0)), _resident(win_t.shape), blk, blk, _full((1, d))],
        out_specs=[blk, _full((1, d))],
        out_shape=[jax.ShapeDtypeStruct((t_rows, d), F32), jax.ShapeDtypeStruct((1, d), F32)],
        args=(dh5, win_t, h0, dh1, g1), name="bwd_in_proj", comm=comm)


def _reduce_small(smalls, d, after):
    (dmeta, dg1, dg2, dg3, dg4, dbb, dlg, dlb, lossv, dwa, dwb) = smalls
    half = d // 2
    kb_rows = dwb.shape[0]

    def body(dmeta_ref, dg1_ref, dg2_ref, dg3_ref, dg4_ref, dbb_ref, dlg_ref, dlb_ref, loss_ref, dwa_ref, dwb_ref,
             ptot_ref, pbuf, psib, chip_p, ps_send, ps_recv, pc_send, pc_recv):
        x, y, c = _mesh_pos()
        pbuf[...] = jnp.zeros_like(pbuf)
        pbuf[pl.ds(0, N_META), :] = dmeta_ref[...]
        for row, ref in ((16, dg1_ref), (17, dg2_ref), (18, dg3_ref), (19, dg4_ref)):
            pbuf[pl.ds(row, 1), :] = ref[...]
        pbuf[pl.ds(20, 1), pl.ds(0, half)] = dbb_ref[...]
        pbuf[pl.ds(20, 1), pl.ds(half, half)] = dlg_ref[...]
        pbuf[pl.ds(21, 1), pl.ds(0, half)] = dlb_ref[...]
        lv = loss_ref[...]
        pbuf[pl.ds(21, 1), pl.ds(half, half)] = lv[:, :half] + lv[:, half:]
        pbuf[pl.ds(24, SUBLANE), pl.ds(0, half)] = dwa_ref[...]
        pbuf[pl.ds(32, kb_rows), pl.ds(0, half)] = dwb_ref[...]
        to_sib = _remote(pbuf, psib, ps_send.at[0], ps_recv.at[0], (x, y, 1 - c))
        to_sib.start()
        to_sib.wait_recv()
        my_chip = 2 * x + y
        chip_p[my_chip] = pbuf[...] + psib[...]
        to_sib.wait_send()
        slot = chip_p.at[my_chip]
        cps = [_remote(slot, slot, pc_send.at[k], pc_recv.at[k], (*chip, c)) for k, chip in enumerate(_other_chips(x, y))]
        for cp in cps:
            cp.start()
        for cp in cps:
            cp.wait_recv()
        ptot_ref[...] = ((chip_p[0] + chip_p[1]) + chip_p[2]) + chip_p[3]
        for cp in cps:
            cp.wait_send()

    return _host_call(
        body, grid=(), in_specs=[VMEM] * 11, out_specs=[VMEM], out_shape=[jax.ShapeDtypeStruct((SMALL_ROWS, d), F32)],
        scratch_shapes=[pltpu.VMEM((SMALL_ROWS, d), F32), pltpu.VMEM((SMALL_ROWS, d), F32), pltpu.VMEM((4, SMALL_ROWS, d), F32),
                        pltpu.SemaphoreType.DMA((1,)), pltpu.SemaphoreType.DMA((1,)),
                        pltpu.SemaphoreType.DMA((3,)), pltpu.SemaphoreType.DMA((3,))],
        args=smalls, name="reduce_small", after=after)


def _adamw(w, g, m, v):
    m = ADAM_B1 * m + (1.0 - ADAM_B1) * g
    v = ADAM_B2 * v + (1.0 - ADAM_B2) * jnp.square(g)
    m_hat = m / (1.0 - ADAM_B1 ** ADAM_STEP)
    v_hat = v / (1.0 - ADAM_B2 ** ADAM_STEP)
    delta = -ADAM_LR * (m_hat / (jnp.sqrt(v_hat) + ADAM_EPS) + ADAM_WD * w)
    return delta, m, v


def _adam_big(g, pair, part, w, m, v, name):
    r, d = w.shape
    cols = d // ADAM_COL_BLOCKS

    def body(me_ref, g_ref, pair_ref, part_ref, w_ref, m_ref, v_ref, go_ref, d_ref, mo_ref, vo_ref):
        g = g_ref[...].astype(F32) + pair_ref[...].astype(F32)
        for k in range(3):
            g = g + part_ref[k].astype(F32)
        go_ref[...] = g
        d_ref[...], mo_ref[...], vo_ref[...] = _adamw(w_ref[...], g, m_ref[...], v_ref[...])

    blk = pl.BlockSpec((r, cols), lambda i, me_ref: (0, i))
    grid_spec = pltpu.PrefetchScalarGridSpec(
        num_scalar_prefetch=1, grid=(ADAM_COL_BLOCKS,),
        in_specs=[pl.BlockSpec((r, cols), lambda i, me_ref: (me_ref[0], i)),
                  pl.BlockSpec((None, r, cols), lambda i, me_ref: (0, 0, i)),
                  pl.BlockSpec((3, r, cols), lambda i, me_ref: (0, 0, i)), blk, blk, blk],
        out_specs=[blk, blk, blk, blk])
    me = jnp.reshape(_dev_index(*_mesh_pos()), (1,)).astype(jnp.int32)
    return pl.pallas_call(body, out_shape=[jax.ShapeDtypeStruct((r, d), F32)] * 4, grid_spec=grid_spec, name=name,
                          compiler_params=_cparams(1))(me, g, pair, part, w, m, v)


def _adam_small(gs, ws, ms, vs):
    n = len(gs)

    def body(*refs):
        ins, outs = refs[:4 * n], refs[4 * n:]
        for i in range(n):
            g = ins[i][...]
            delta, m, v = _adamw(ins[n + i][...], g, ins[2 * n + i][...], ins[3 * n + i][...])
            outs[i][...] = delta
            outs[n + i][...] = m
            outs[2 * n + i][...] = v

    shapes = [jax.ShapeDtypeStruct(w.shape, F32) for w in ws]
    return pl.pallas_call(body, out_shape=shapes * 3, name="adam_small", compiler_params=_cparams())(*gs, *ws, *ms, *vs)


def kernel(x, meta_tokens, pre_mix_norm, w_in, conv_a_w, conv_b_w, conv_b_bias, ln_b_gain, ln_b_bias, w_out, post_mix_norm, pre_ffn_norm, w_gate, w_up, w_down, post_ffn_norm, loss_target, m_meta_tokens, m_pre_mix_norm, m_w_in, m_conv_a_w, m_conv_b_w, m_conv_b_bias, m_ln_b_gain, m_ln_b_bias, m_w_out, m_post_mix_norm, m_pre_ffn_norm, m_w_gate, m_w_up, m_w_down, m_post_ffn_norm, v_meta_tokens, v_pre_mix_norm, v_w_in, v_conv_a_w, v_conv_b_w, v_conv_b_bias, v_ln_b_gain, v_ln_b_bias, v_w_out, v_post_mix_norm, v_pre_ffn_norm, v_w_gate, v_w_up, v_w_down, v_post_ffn_norm):
    _, seq, d = x.shape
    ka, ca_loc = conv_a_w.shape[1:]
    kb, cb_loc = conv_b_w.shape[1:]
    wa_w = ca_loc * N_DEV
    assert cb_loc == ca_loc and wa_w % LANE == 0 and w_in.shape[2] * N_DEV == 5 * wa_w
    pad = (-(N_META + seq)) % ROW_ALIGN
    x0 = pad + N_META
    t_rows = x0 + seq
    assert t_rows % (N_ROW_BLOCKS * BF16_ROWS) == 0 and t_rows % CONV_CHUNK == 0 and d % LANE == 0
    tm = t_rows // N_ROW_BLOCKS
    me = _dev_index(*_mesh_pos())

    def as_rows(w_in_like, w_out_like, w_gate_like, w_up_like, w_down_like):
        return (w_in_like[0].T, w_out_like[0], w_gate_like[0].T, w_up_like[0].T, w_down_like[0])

    w_loc = as_rows(w_in, w_out, w_gate, w_up, w_down)
    rows = [w.shape[0] for w in w_loc]
    assert all(r % ADD_CHUNK == 0 for r in rows)
    P_IN, P_OUT, P_GATE, P_UP, P_DOWN = range(N_BIG)

    sm = jnp.zeros((SM_ROWS, LANE), F32)
    sm = sm.at[0:N_META, :].set(meta_tokens)
    sm = sm.at[16:16 + ka, 0:ca_loc].set(conv_a_w[0])
    sm = sm.at[24:24 + kb, 0:cb_loc].set(conv_b_w[0])
    wl, wfull, sm_all, h0, tgt = _gather_first(w_loc, sm, [(P_IN, 0, rows[P_IN])], x[0], loss_target[0], t_rows, x0)
    wa =jnp.transpose(sm_all[:, 16:16 + ka, 0:ca_loc], (1, 0, 2)).reshape(ka, wa_w)
    wb = jnp.transpose(sm_all[:, 24:24 + kb, 0:cb_loc], (1, 0, 2)).reshape(kb, wa_w)

    later = (P_OUT, P_GATE, P_UP, P_DOWN)
    sems, wl, started, _ = _gather_start(wl, wfull, later, rows)
    for p, arr in zip(later, started):
        wfull[p] = arr

    def arrived(p, after, name):
        nonlocal wl
        wl, wfull[p] = _gather_wait(wl, wfull[p], sems[later.index(p)], after, rows[p], name)
        return _forward_comm(wfull[p], rows[p])

    (xn1, hin), _ = _in_proj(h0, pre_mix_norm, wfull[P_IN], tm, None)
    (ya, z), (wfull[P_OUT],) = _mix_conv_fwd(hin, wa, wb, conv_b_bias, wa_w, arrived(P_OUT, hin, "gather_wait_out"))
    y, (wfull[P_GATE],) = _mix_ln_fwd(ya, z, ln_b_gain, ln_b_bias, tm, arrived(P_GATE, z, "gather_wait_gate"))
    (mix, h1, xn2), _ = _out_proj(y, wfull[P_OUT], h0, post_mix_norm, pre_ffn_norm, tm, None)
    arrived(P_UP, xn2, "gather_wait_up")
    wfull[P_UP] = _forward_now(wfull[P_UP], rows[P_UP], "forward_up")
    (a, u, s), _ = _gate_up(xn2, wfull[P_GATE], wfull[P_UP], tm, None)
    arrived(P_DOWN, s, "gather_wait_down")
    wfull[P_DOWN] = _forward_now(wfull[P_DOWN], rows[P_DOWN], "forward_down")
    dh2, dff, dg4, lossv = _down_loss(s, wfull[P_DOWN], h1, tgt, post_ffn_norm, tm, x0)

    gwd = _wgrad(s, dff, "wgrad_down")
    (da, du), (pair_d,) = _bwd_down(dff, wfull[P_DOWN], a, u, tm, _pair_comm(gwd, rows[P_DOWN]))
    (flight_d,), token = _chip_start([_pair_sum(gwd, pair_d, rows[P_DOWN], "pair_sum_down")], "chip_start_down")
    gwg = _wgrad(da, xn2, "wgrad_gate", [token])
    gwu = _wgrad(du, xn2, "wgrad_up")
    (dh1, dg3), (pair_g, pair_u) = _bwd_ffn_in(da, du, wfull[P_GATE], wfull[P_UP], h1, dh2, pre_ffn_norm, tm,
                                               _merge_comms([_pair_comm(gwg, rows[P_GATE]), _pair_comm(gwu, rows[P_UP])]))
    (flight_g, flight_u), token = _chip_start([_pair_sum(gwg, pair_g, rows[P_GATE], "pair_sum_gate"),
                                               _pair_sum(gwu, pair_u, rows[P_UP], "pair_sum_up")], "chip_start_gate_up")
    dmix, dy, dg2 = _bwd_out_proj(dh1, mix, wfull[P_OUT], post_mix_norm, tm, [token])
    gwo = _wgrad(y, dmix, "wgrad_out")
    dz, dlg, dlb, dbb = _mix_ln_bwd(z, dy, ln_b_gain, ln_b_bias, tm)
    (dh5, dwa, dwb), (pair_o,) = _mix_conv_bwd(hin, dy, dz, wa, wb, wa_w, _pair_comm(gwo, rows[P_OUT]))
    (flight_o,), token = _chip_start([_pair_sum(gwo, pair_o, rows[P_OUT], "pair_sum_out")], "chip_start_out")
    gwi = _wgrad(dh5, xn1, "wgrad_in", [token])
    (dh0, dg1), (pair_i,) = _bwd_in_proj(dh5, wfull[P_IN], h0, dh1, pre_mix_norm, tm, _pair_comm(gwi, rows[P_IN]))
    grad_x = dh0[x0:][None]
    dmeta = dh0[x0 - N_META:x0]
    (ptot,), _ = _reduce_small((dmeta, dg1, dg2, dg3, dg4, dbb, dlg, dlb, lossv, dwa, dwb), d, [])
    (flight_i,), token = _chip_start([_pair_sum(gwi, pair_i, rows[P_IN], "pair_sum_in", [ptot])], "chip_start_in")

    def landed(flight, after, tag):
        sems_p, sums, land = flight
        return _chip_wait(sums, land, sems_p, after, "chip_wait_" + tag)

    part_d = landed(flight_d, token, "down")
    part_g = landed(flight_g, token, "gate")
    part_u = landed(flight_u, token, "up")
    part_o = landed(flight_o, token, "out")

    half = d // 2
    loss = (0.5 / d) * jnp.sum(ptot[21, half:])
    g_meta = lax.dynamic_slice(ptot, (0, me * (d // N_DEV)), (N_META, d // N_DEV))
    g_small = [g_meta, ptot[16:17], lax.dynamic_slice(ptot, (24, me * ca_loc), (ka, ca_loc))[None],
               lax.dynamic_slice(ptot, (32, me * cb_loc), (kb, cb_loc))[None],
               ptot[20:21, :half], ptot[20:21, half:], ptot[21:22, :half], ptot[17:18], ptot[18:19], ptot[19:20]]
    w_small = [meta_tokens, pre_mix_norm, conv_a_w, conv_b_w, conv_b_bias, ln_b_gain, ln_b_bias, post_mix_norm,
               pre_ffn_norm, post_ffn_norm]
    m_small = [m_meta_tokens, m_pre_mix_norm, m_conv_a_w, m_conv_b_w, m_conv_b_bias, m_ln_b_gain, m_ln_b_bias,
               m_post_mix_norm, m_pre_ffn_norm, m_post_ffn_norm]
    v_small = [v_meta_tokens, v_pre_mix_norm, v_conv_a_w, v_conv_b_w, v_conv_b_bias, v_ln_b_gain, v_ln_b_bias,
               v_post_mix_norm, v_pre_ffn_norm, v_post_ffn_norm]
    small = _adam_small(g_small, w_small, m_small, v_small)
    n_small = len(w_small)
    d_small, nm_small, nv_small = small[:n_small], small[n_small:2 * n_small], small[2 * n_small:]

    m_loc = as_rows(m_w_in, m_w_out, m_w_gate, m_w_up, m_w_down)
    v_loc = as_rows(v_w_in, v_w_out, v_w_gate, v_w_up, v_w_down)
    full_grads = {P_IN: gwi, P_OUT: gwo, P_GATE: gwg, P_UP: gwu, P_DOWN: gwd}
    pairs = {P_IN: pair_i, P_OUT: pair_o, P_GATE: pair_g, P_UP: pair_u, P_DOWN: pair_d}
    parts = {P_OUT: part_o, P_GATE: part_g, P_UP: part_u, P_DOWN: part_d}
    names = {P_IN: "w_in", P_OUT: "w_out", P_GATE: "w_gate", P_UP: "w_up", P_DOWN: "w_down"}
    bigs = {}
    res = None
    for p in (P_DOWN, P_GATE, P_UP, P_OUT, P_IN):
        if p == P_IN:
            parts[p] = landed(flight_i, res[1], "in")
        res = _adam_big(full_grads[p], pairs[p], parts[p], w_loc[p], m_loc[p], v_loc[p], "adam_" + names[p])
        bigs[names[p]] = [(o.T if p in (P_IN, P_GATE, P_UP) else o)[None] for o in res]

    def ordered(pick_small, pick_big):
        sm_it = iter(range(n_small))
        out = []
        for name in ("s", "s", "w_in", "s", "s", "s", "s", "s", "w_out", "s", "s", "w_gate", "w_up", "w_down", "s"):
            out.append(pick_small(next(sm_it)) if name == "s" else pick_big(name))
        return out

    grads = ordered(lambda i: g_small[i], lambda n: bigs[n][0])
    deltas = ordered(lambda i: d_small[i], lambda n: bigs[n][1])
    new_m = ordered(lambda i: nm_small[i], lambda n: bigs[n][2])
    new_v = ordered(lambda i: nv_small[i], lambda n: bigs[n][3])
    return (loss, grad_x, *grads, *deltas, *new_m, *new_v)
```

```python
import jax
import jax.numpy as jnp
from jax import lax
from jax.experimental import pallas as pl
from jax.experimental.pallas import tpu as pltpu

F32 = jnp.float32
BF16 = jnp.bfloat16
MESH = pl.DeviceIdType.MESH

N_META = 16
N_DEV = 8
RMS_EPS = 1e-6
LN_EPS = 1e-5
ADAM_LR = 0.001
ADAM_B1 = 0.9
ADAM_B2 = 0.999
ADAM_EPS = 1e-08
ADAM_WD = 0.01
ADAM_STEP = 10

LANE = 128
SUBLANE = 8
BF16_ROWS = 16
ROW_ALIGN = 128
N_ROW_BLOCKS = 4
CONV_HALO = 32
CONV_CHUNK = 64
WGRAD_ROWS = 32
N_CHUNK = 512
WGRAD_TILE_MAX = 1408
ADD_CHUNK = 32
ADAM_COL_BLOCKS = 4
COPY_PIECES = 4
V7X_VMEM_BYTES = 64 * 1024 * 1024
VMEM_LIMIT = V7X_VMEM_BYTES - 6 * 1024 * 1024
SMALL_ROWS = 64
SM_ROWS = 56
N_BIG = 5

ANY = pl.BlockSpec(memory_space=pl.ANY)
VMEM = pl.BlockSpec(memory_space=pltpu.VMEM)


def _cparams(n_grid_axes=0):
    sem = ("arbitrary",) * n_grid_axes if n_grid_axes else None
    return pltpu.CompilerParams(dimension_semantics=sem, vmem_limit_bytes=VMEM_LIMIT)


def _mesh_pos():
    return lax.axis_index("x"), lax.axis_index("y"), lax.axis_index("c")


def _dev_index(px, py, pc):
    return 4 * px + 2 * py + pc


def _other_chips(x, y):
    return [(1 - x, y), (x, 1 - y), (1 - x, 1 - y)]


def _full(shape):
    return pl.BlockSpec(shape, lambda *_: (0,) * len(shape))


def _resident(shape):
    return pl.BlockSpec(shape, lambda *_: (0,) * len(shape), pipeline_mode=pl.Buffered(1))


def _dot_nt(a, w):
    return lax.dot_general(a, w, (((1,), (1,)), ((), ())), preferred_element_type=F32)


def _dot_nn(a, w):
    return jnp.dot(a, w, preferred_element_type=F32)


def _chunks(n, c):
    out, o = [], 0
    while o < n:
        out.append((o, min(c, n - o)))
        o += c
    return out


def _rstd(h):
    return lax.rsqrt(jnp.mean(h * h, axis=-1, keepdims=True) + RMS_EPS)


def _rms_bwd(dyh, yh, r):
    return r * (dyh - yh * jnp.mean(dyh * yh, axis=-1, keepdims=True))


def _silu_grad(a, sig):
    return sig * (1.0 + a * (1.0 - sig))


def _acc_rows(ref, val, first):
    s = jnp.sum(val, axis=0, keepdims=True)

    @pl.when(first)
    def _():
        ref[...] = s

    @pl.when(jnp.logical_not(first))
    def _():
        ref[...] += s


def _row_loop(t_rows, chunk, fn, carry=None):
    def step(i, c):
        return fn(pl.multiple_of(i * chunk, chunk), c)

    return lax.fori_loop(0, t_rows // chunk, step, carry)


def _remote(src, dst, send_sem, recv_sem, to):
    return pltpu.make_async_remote_copy(src_ref=src, dst_ref=dst, send_sem=send_sem, recv_sem=recv_sem,
                                        device_id=to, device_id_type=MESH)


class _SplitRemote:
    def __init__(self, src, dst, send_sem, recv_sem, to, rows, n_chunks):
        units = rows // BF16_ROWS
        n_chunks = max(1, min(n_chunks, units))
        sizes = [(units // n_chunks + (i < units % n_chunks)) * BF16_ROWS for i in range(n_chunks)]
        self.whole = _remote(src, dst, send_sem, recv_sem, to)
        self.parts, o = [], 0
        for n in sizes:
            self.parts.append(_remote(src.at[pl.ds(o, n), :], dst.at[pl.ds(o, n), :], send_sem, recv_sem, to))
            o += n

    def start(self):
        for cp in self.parts:
            cp.start()

    def wait_recv(self):
        self.whole.wait_recv()

    def wait_send(self):
        self.whole.wait_send()


class _Comm:
    def __init__(self, inputs, out_shapes, aliases, scratch, start, finish):
        self.inputs, self.out_shapes, self.aliases, self.scratch = list(inputs), list(out_shapes), dict(aliases), list(scratch)
        self.start, self.finish = start, finish


def _merge_comms(comms):
    inputs, out_shapes, aliases, scratch, spans = [], [], {}, [], []
    for cm in comms:
        spans.append((len(inputs), len(out_shapes), len(scratch), cm))
        aliases.update({len(inputs) + k: len(out_shapes) + v for k, v in cm.aliases.items()})
        inputs += cm.inputs
        out_shapes += cm.out_shapes
        scratch += cm.scratch

    def run(which):
        def fn(ins, outs, scr):
            for i0, o0, s0, cm in spans:
                getattr(cm, which)(ins[i0:i0 + len(cm.inputs)], outs[o0:o0 + len(cm.out_shapes)], scr[s0:s0 + len(cm.scratch)])
        return fn

    return _Comm(inputs, out_shapes, aliases, scratch, run("start"), run("finish"))


def _host_call(body, *, grid, in_specs, out_specs, out_shape, args, name, scratch_shapes=(), comm=None, after=()):
    talks = comm is not None
    if comm is None:
        comm = _Comm([], [], {}, [], lambda *_: None, lambda *_: None)
    n_in, n_out, n_scr = len(args), len(out_shape), len(scratch_shapes)
    c_in, c_out = len(comm.inputs), len(comm.out_shapes)
    n_after = len(after)

    def open_comm(c_ins, c_outs, c_scr):
        if talks:
            _pair_handshake()
        comm.start(c_ins, c_outs, c_scr)

    def hosted(*refs):
        ins, c_ins = refs[:n_in], refs[n_in:n_in + c_in]
        o0 = n_in + c_in + n_after
        outs, c_outs = refs[o0:o0 + n_out], refs[o0 + n_out:o0 + n_out + c_out]
        s0 = o0 + n_out + c_out
        scr, c_scr = refs[s0:s0 + n_scr], refs[s0 + n_scr:]
        if not grid:
            open_comm(c_ins, c_outs, c_scr)
            body(*ins, *outs, *scr)
            comm.finish(c_ins, c_outs, c_scr)
            return
        first = last = None
        for a, n in enumerate(grid):
            f, l = pl.program_id(a) == 0, pl.program_id(a) == n - 1
            first = f if first is None else jnp.logical_and(first, f)
            last = l if last is None else jnp.logical_and(last, l)

        @pl.when(first)
        def _():
            open_comm(c_ins, c_outs, c_scr)

        body(*ins, *outs, *scr)

        @pl.when(last)
        def _():
            comm.finish(c_ins, c_outs, c_scr)

    sem = ("arbitrary",) * len(grid) if grid else None
    params = pltpu.CompilerParams(dimension_semantics=sem, vmem_limit_bytes=VMEM_LIMIT,
                                  collective_id=PAIR_BARRIER_ID if talks else None)
    res = pl.pallas_call(
        hosted, grid=grid, in_specs=list(in_specs) + [ANY] * (c_in + n_after), out_specs=list(out_specs) + [ANY] * c_out,
        out_shape=list(out_shape) + comm.out_shapes, scratch_shapes=list(scratch_shapes) + comm.scratch,
        input_output_aliases={n_in + k: n_out + v for k, v in comm.aliases.items()},
        name=name, compiler_params=params)(*args, *comm.inputs, *after)
    return list(res[:n_out]), list(res[n_out:])


PAIR_BARRIER_ID = 0
START_BARRIER_IDS = (1, 2, 3, 4, 5)


def _chips_handshake():
    x, y, c = _mesh_pos()
    barrier = pltpu.get_barrier_semaphore()
    for chip in _other_chips(x, y):
        pl.semaphore_signal(barrier, inc=1, device_id=(*chip, c), device_id_type=MESH)
    pl.semaphore_wait(barrier, 3)


def _pair_handshake():
    x, y, c = _mesh_pos()
    barrier = pltpu.get_barrier_semaphore()
    pl.semaphore_signal(barrier, inc=1, device_id=(x, y, 1 - c), device_id_type=MESH)
    pl.semaphore_wait(barrier, 1)


GATHER_SEMS = 10
D2D_CHUNKS = 8


class _Gather:
    def __init__(self, jobs, rows, lo, src_ref, dests, send_sems, recv_sems):
        x, y, c = _mesh_pos()
        me, sib = (x, y, c), (x, y, 1 - c)
        nx, ny, dg = (1 - x, y, c), (x, 1 - y, c), (1 - x, 1 - y, c)
        self.relayed, self.direct, self.relay, self.to_sib, self.sib_fwd = [], [], [], [], []
        for n, (p, r0, nr) in enumerate(jobs):
            assert nr % (2 * BF16_ROWS) == 0
            half = nr // 2

            def rows_of(dev, h, p=p, r0=r0, nr=nr, half=half):
                off, cnt = (r0, nr) if h is None else (r0 + h * half, half)
                return dests[p].at[pl.ds(pl.multiple_of(_dev_index(*dev) * rows[p] + off, BF16_ROWS), cnt), :]

            def mine(h, p=p, r0=r0, nr=nr, half=half):
                off, cnt = (r0, nr) if h is None else (r0 + h * half, half)
                return src_ref.at[pl.ds(lo[p] + off, cnt), :]

            sem = lambda k, n=n: (send_sems.at[GATHER_SEMS * n + k], recv_sems.at[GATHER_SEMS * n + k])
            self.relayed.append([_remote(mine(0), rows_of(me, 0), *sem(0), nx), _remote(mine(1), rows_of(me, 1), *sem(3), ny)])
            self.direct.append([_remote(mine(1), rows_of(me, 1), *sem(1), nx), _remote(mine(0), rows_of(me, 0), *sem(2), ny)])
            self.relay.append([_remote(rows_of(nx, 0), rows_of(nx, 0), *sem(4), ny), _remote(rows_of(ny, 1), rows_of(ny, 1), *sem(5), nx)])
            self.to_sib.append(_SplitRemote(mine(None), rows_of(me, None), *sem(6), sib, nr, D2D_CHUNKS))
            self.sib_fwd.append([_SplitRemote(rows_of(dev, None), rows_of(dev, None), *sem(7 + i), sib, nr, D2D_CHUNKS)
                                 for i, dev in enumerate((nx, ny, dg))])

    def start(self):
        for group in (self.relayed, self.direct):
            for cps in group:
                for cp in cps:
                    cp.start()
        for cp in self.to_sib:
            cp.start()

    def mid(self):
        for first, relay in zip(self.relayed, self.relay):
            for arrived, onward in zip(first, relay):
                arrived.wait_recv()
                onward.start()

    def finish(self):
        for direct, relay, fwd in zip(self.direct, self.relay, self.sib_fwd):
            for k in range(2):
                direct[k].wait_recv()
                fwd[k].start()
            for cp in relay:
                cp.wait_recv()
            fwd[2].start()
        for n in range(len(self.to_sib)):
            self.to_sib[n].wait_recv()
            for cp in self.sib_fwd[n]:
                cp.wait_recv()
            for cp in self.relayed[n] + self.direct[n] + self.relay[n] + [self.to_sib[n]] + self.sib_fwd[n]:
                cp.wait_send()


HBM = pl.BlockSpec(memory_space=pltpu.HBM)
SEM = pl.BlockSpec(memory_space=pltpu.SEMAPHORE)
FLOWS = pltpu.SideEffectType.DATAFLOW_SIDE_EFFECTING


def _in_hbm(a):
    return pltpu.with_memory_space_constraint(a, pltpu.HBM)


def _gather_start(wl, dests, ps, rows, barrier_id):
    lo = [sum(rows[:p]) for p in range(N_BIG)]
    n = len(ps)

    def body(*refs):
        wl_ref, dest_refs = refs[0], refs[1:1 + n]
        sends, recvs = refs[1 + n:1 + 2 * n], refs[1 + 2 * n:1 + 3 * n]
        token = refs[-1]
        _chips_handshake()
        x, y, c = _mesh_pos()
        jme = _dev_index(x, y, c)
        for i, p in enumerate(ps):
            mine = dest_refs[i].at[pl.ds(pl.multiple_of(jme * rows[p], BF16_ROWS), rows[p]), :]
            for chip in _other_chips(x, y):
                _remote(wl_ref.at[pl.ds(lo[p], rows[p]), :], mine, sends[i], recvs[i], (*chip, c)).start()
        token[...] = jnp.zeros_like(token)

    thru = [pltpu.HBM(wl.shape, wl.dtype)] + [pltpu.HBM(dests[p].shape, BF16) for p in ps]
    res = pl.pallas_call(
        body, name="gather_start",
        out_shape=tuple([pltpu.SemaphoreType.DMA(())] * (2 * n) + thru + [jax.ShapeDtypeStruct((SUBLANE, LANE), F32)]),
        in_specs=[HBM] * (1 + n), out_specs=tuple([SEM] * (2 * n) + [HBM] * (1 + n) + [VMEM]),
        input_output_aliases={i: 2 * n + i for i in range(1 + n)},
        compiler_params=pltpu.CompilerParams(has_side_effects=FLOWS, collective_id=barrier_id))(
            _in_hbm(wl), *[_in_hbm(dests[p]) for p in ps])
    sems = [(res[i], res[n + i]) for i in range(n)]
    return sems, res[2 * n], list(res[2 * n + 1:3 * n + 1]), res[-1]


def _gather_wait(wl, dest, sems, after, r, name):
    def body(wl_ref, dest_ref, send_sem, recv_sem, after_ref, wl_out, dest_out):
        x, y, c = _mesh_pos()
        three = dest_ref.at[pl.ds(0, 3 * r), :]
        cp = _remote(three, three, send_sem, recv_sem, (x, y, 1 - c))
        cp.wait_send()
        cp.wait_recv()

    res = pl.pallas_call(
        body, name=name, out_shape=(pltpu.HBM(wl.shape, wl.dtype), pltpu.HBM(dest.shape, dest.dtype)),
        in_specs=[HBM, HBM, SEM, SEM, ANY], out_specs=(HBM, HBM), input_output_aliases={0: 0, 1: 1},
        compiler_params=pltpu.CompilerParams(has_side_effects=FLOWS))(wl, dest, sems[0], sems[1], after)
    return res[0], res[1]


def _forward_comm(dest, r):
    def descs(ins, outs, scr):
        x, y, c = _mesh_pos()
        cps = []
        for k, chip in enumerate(_other_chips(x, y)):
            blk = outs[0].at[pl.ds(pl.multiple_of(_dev_index(*chip, c) * r, BF16_ROWS), r), :]
            cps.append(_SplitRemote(blk, blk, scr[0].at[k], scr[1].at[k], (x, y, 1 - c), r, D2D_CHUNKS))
        return cps

    def start(ins, outs, scr):
        for cp in descs(ins, outs, scr):
            cp.start()

    def finish(ins, outs, scr):
        cps = descs(ins, outs, scr)
        for cp in cps:
            cp.wait_recv()
        for cp in cps:
            cp.wait_send()

    return _Comm([dest], [jax.ShapeDtypeStruct(dest.shape, dest.dtype)], {0: 0},
                 [pltpu.SemaphoreType.DMA((3,)), pltpu.SemaphoreType.DMA((3,))], start, finish)


def _forward_now(dest, r, name):
    _, (dest,) = _host_call(lambda: None, grid=(), in_specs=[], out_specs=[], out_shape=[], args=(), name=name,
                            comm=_forward_comm(dest, r))
    return dest


def _pair_comm(g, r):
    d = g.shape[1]

    def descs(ins, outs, scr):
        x, y, c = _mesh_pos()
        chips = [(x, y)] + _other_chips(x, y)
        return [_SplitRemote(ins[0].at[pl.ds(pl.multiple_of(_dev_index(*chip, 1 - c) * r, BF16_ROWS), r), :], outs[0].at[k],
                             scr[0].at[k], scr[1].at[k], (x, y, 1 - c), r, D2D_CHUNKS) for k, chip in enumerate(chips)]

    def start(ins, outs, scr):
        for cp in descs(ins, outs, scr):
            cp.start()

    def finish(ins, outs, scr):
        cps = descs(ins, outs, scr)
        for cp in cps:
            cp.wait_recv()
        for cp in cps:
            cp.wait_send()

    comm = _Comm([g], [jax.ShapeDtypeStruct((4, r, d), BF16)], {},
                 [pltpu.SemaphoreType.DMA((4,)), pltpu.SemaphoreType.DMA((4,))], start, finish)
    return comm


def _pair_sum(g, pair, r, name, after=()):
    d = g.shape[1]

    def body(g_ref, p_ref, *rest):
        o_ref, gbuf, pbuf, sems = rest[len(after):]
        x, y, c = _mesh_pos()
        loads = [pltpu.make_async_copy(p_ref.at[pl.ds(1, 3)], pbuf, sems.at[3])]
        for k, chip in enumerate(_other_chips(x, y)):
            j = _dev_index(*chip, c)
            loads.append(pltpu.make_async_copy(g_ref.at[pl.ds(pl.multiple_of(j * r, BF16_ROWS), r), :], gbuf.at[k], sems.at[k]))
        for cp in loads:
            cp.start()
        for cp in loads:
            cp.wait()
        for k in range(3):
            o_ref[k] = (gbuf[k].astype(F32) + pbuf[k].astype(F32)).astype(BF16)

    return pl.pallas_call(
        body, out_shape=jax.ShapeDtypeStruct((3, r, d), BF16), in_specs=[ANY] * (2 + len(after)), out_specs=VMEM,
        scratch_shapes=[pltpu.VMEM((3, r, d), BF16), pltpu.VMEM((3, r, d), BF16), pltpu.SemaphoreType.DMA((4,))],
        name=name, compiler_params=_cparams())(g, pair, *after)


def _chip_start(sums, name, barrier_id):
    n = len(sums)

    def body(*refs):
        srcs, lands = refs[:n], refs[n:2 * n]
        sends, recvs = refs[2 * n:3 * n], refs[3 * n:4 * n]
        _chips_handshake()
        x, y, c = _mesh_pos()
        for i in range(n):
            for k, chip in enumerate(_other_chips(x, y)):
                _remote(srcs[i].at[k], lands[i].at[k], sends[i], recvs[i], (*chip, c)).start()
        refs[-1][...] = jnp.zeros_like(refs[-1])

    zones = [pltpu.HBM(s.shape, s.dtype) for s in sums]
    res = pl.pallas_call(
        body, name=name,
        out_shape=tuple([pltpu.SemaphoreType.DMA(())] * (2 * n) + zones + zones + [jax.ShapeDtypeStruct((SUBLANE, LANE), F32)]),
        in_specs=[HBM] * (2 * n), out_specs=tuple([SEM] * (2 * n) + [HBM] * (2 * n) + [VMEM]),
        input_output_aliases={i: 2 * n + i for i in range(2 * n)},
        compiler_params=pltpu.CompilerParams(has_side_effects=FLOWS, collective_id=barrier_id))(
            *[_in_hbm(s) for s in sums], *[_in_hbm(lax.empty(s.shape, s.dtype)) for s in sums])
    flights = [((res[i], res[n + i]), res[2 * n + i], res[3 * n + i]) for i in range(n)]
    return flights, res[-1]


def _chip_wait(sums, land, sems, after, name):
    def body(sums_ref, land_ref, send_sem, recv_sem, after_ref, sums_out, land_out):
        x, y, c = _mesh_pos()
        cp = _remote(sums_ref, land_ref, send_sem, recv_sem, (x, y, 1 - c))
        cp.wait_send()
        cp.wait_recv()

    res = pl.pallas_call(
        body, name=name, out_shape=(pltpu.HBM(sums.shape, sums.dtype), pltpu.HBM(land.shape, land.dtype)),
        in_specs=[HBM, HBM, SEM, SEM, ANY], out_specs=(HBM, HBM), input_output_aliases={0: 0, 1: 1},
        compiler_params=pltpu.CompilerParams(has_side_effects=FLOWS))(sums, land, sems[0], sems[1], after)
    return res[1]


class _CopyThrough:
    def __init__(self, src_ref, dst_ref, dst_row0, n_rows, buf, sem_in, sem_out):
        rc = n_rows // COPY_PIECES
        piece = lambda ref, o: ref.at[pl.ds(o, rc), :]
        self.loads = [pltpu.make_async_copy(piece(src_ref, k * rc), piece(buf, k * rc), sem_in) for k in range(COPY_PIECES)]
        self.stores = [pltpu.make_async_copy(piece(buf, k * rc), piece(dst_ref, dst_row0 + k * rc), sem_out) for k in range(COPY_PIECES)]
        self.all_in = pltpu.make_async_copy(src_ref, buf, sem_in)
        self.all_out = pltpu.make_async_copy(buf, dst_ref.at[pl.ds(dst_row0, n_rows), :], sem_out)

    def load(self):
        for cp in self.loads:
            cp.start()

    def store(self):
        self.all_in.wait()
        for cp in self.stores:
            cp.start()

    def done(self):
        self.all_out.wait()


def _gather_first(shards, sm, jobs, x2, tgt2, t_rows, x0):
    d = shards[0].shape[1]
    rows = [w.shape[0] for w in shards]
    lo = [sum(rows[:p]) for p in range(N_BIG)]
    n_sems = GATHER_SEMS * len(jobs)
    seq = x2.shape[0]
    assert x0 == ROW_ALIGN and seq % ROW_ALIGN == 0 and d == N_DEV * LANE

    def body(s0, s1, s2, s3, s4, sm_ref, x_ref, tgt_ref, wl_ref, o0, o1, o2, o3, o4, sa_ref, h0_ref, tp_ref,
             wl_v, x_v, tgt_v, heads_v, sa_v, send_sems, recv_sems, ssend, srecv, local_sems, sems_in, sems_out):
        dests = (o0, o1, o2, o3, o4)
        x, y, c = _mesh_pos()
        me = (x, y, c)
        jme = _dev_index(*me)
        padded = [_CopyThrough(x_ref, h0_ref, x0, seq, x_v, sems_in.at[0], sems_out.at[0]),
                  _CopyThrough(tgt_ref, tp_ref, x0, seq, tgt_v, sems_in.at[1], sems_out.at[1])]
        for cp in padded:
            cp.load()
        shard_refs = (s0, s1, s2, s3, s4)
        first = sorted({j[0] for j in jobs})
        for p in first + [p for p in range(N_BIG) if p not in first]:
            wl_v[pl.ds(lo[p], rows[p]), :] = shard_refs[p][...].astype(BF16)
            if p == first[-1]:
                gather = _Gather(jobs, rows, lo, wl_v, dict(enumerate(dests)), send_sems, recv_sems)
                gather.start()
        peers = [(x, y, 1 - c)] + [(*chip, pc) for pc in (c, 1 - c) for chip in _other_chips(x, y)]
        smalls = [_remote(sm_ref, sa_ref.at[jme], ssend.at[k], srecv.at[k], to) for k, to in enumerate(peers)]
        for cp in smalls:
            cp.start()
        mine = [pltpu.make_async_copy(wl_v.at[pl.ds(lo[p], rows[p]), :],
                                      dests[p].at[pl.ds(pl.multiple_of(jme * rows[p], BF16_ROWS), rows[p]), :], local_sems.at[p])
                for p in range(N_BIG)]
        mine.append(pltpu.make_async_copy(wl_v, wl_ref, local_sems.at[N_BIG]))
        mine.append(pltpu.make_async_copy(sm_ref, sa_ref.at[jme], local_sems.at[N_BIG + 1]))
        for cp in mine:
            cp.start()
        later = [p for p in range(N_BIG) if p not in {j[0] for j in jobs}]
        own = [_SplitRemote(wl_v.at[pl.ds(lo[p], rows[p]), :],
                            dests[p].at[pl.ds(pl.multiple_of(jme * rows[p], BF16_ROWS), rows[p]), :],
                            ssend.at[7 + i], srecv.at[7 + i], (x, y, 1 - c), rows[p], D2D_CHUNKS) for i, p in enumerate(later)]
        for cp in own:
            cp.start()
        for cp in padded:
            cp.store()
        gather.mid()
        for cp in smalls + own:
            cp.wait_recv()
        mine[-1].wait()
        to_v = pltpu.make_async_copy(sa_ref, sa_v, local_sems.at[N_BIG + 1])
        to_v.start()
        to_v.wait()
        head, zeros = heads_v.at[0], heads_v.at[1]
        head[...] = jnp.zeros_like(head)
        zeros[...] = jnp.zeros_like(zeros)
        for j in range(N_DEV):
            head[pl.ds(x0 - N_META, N_META), pl.ds(j * LANE, LANE)] = sa_v[j, pl.ds(0, N_META), :]
        heads = [pltpu.make_async_copy(head, h0_ref.at[pl.ds(0, x0), :], local_sems.at[N_BIG + 1]),
                 pltpu.make_async_copy(zeros, tp_ref.at[pl.ds(0, x0), :], local_sems.at[N_BIG + 2])]
        for cp in heads:
            cp.start()
        gather.finish()
        for cp in smalls + own:
            cp.wait_send()
        for cp in mine[:-1] + heads:
            cp.wait()
        for cp in padded:
            cp.done()

    out_shape = [jax.ShapeDtypeStruct((sum(rows), d), BF16)]
    out_shape += [jax.ShapeDtypeStruct((N_DEV * r, d), BF16) for r in rows]
    out_shape.append(jax.ShapeDtypeStruct((N_DEV,) + sm.shape, F32))
    out_shape += [jax.ShapeDtypeStruct((t_rows, d), F32)] * 2
    res = pl.pallas_call(
        body, out_shape=out_shape, in_specs=[VMEM] * 6 + [ANY] * 2, out_specs=[ANY] * 9,
        scratch_shapes=[pltpu.VMEM((sum(rows), d), BF16), pltpu.VMEM((seq, d), F32), pltpu.VMEM((seq, d), F32),
                        pltpu.VMEM((2, ROW_ALIGN, d), F32), pltpu.VMEM((N_DEV,) + sm.shape, F32),
                        pltpu.SemaphoreType.DMA((n_sems,)), pltpu.SemaphoreType.DMA((n_sems,)),
                        pltpu.SemaphoreType.DMA((7 + N_BIG,)), pltpu.SemaphoreType.DMA((7 + N_BIG,)),
                        pltpu.SemaphoreType.DMA((N_BIG + 3,)), pltpu.SemaphoreType.DMA((2,)), pltpu.SemaphoreType.DMA((2,))],
        name="gather_first", compiler_params=_cparams())(*shards, sm, x2, tgt2)
    return res[0], list(res[1:1 + N_BIG]), res[1 + N_BIG], res[2 + N_BIG], res[3 + N_BIG]


def _in_proj(h0, g1, win_t, tm, comm):
    t_rows, d = h0.shape
    e = win_t.shape[0]

    def body(h_ref, g_ref, w_ref, xn_ref, hin_ref):
        h = h_ref[...]
        xn = ((h * _rstd(h)) * g_ref[...]).astype(BF16)
        xn_ref[...] = xn
        for o, n in _chunks(e, N_CHUNK):
            hin_ref[:, pl.ds(o, n)] = _dot_nt(xn, w_ref[pl.ds(o, n), :])

    return _host_call(
        body, grid=(t_rows // tm,),
        in_specs=[pl.BlockSpec((tm, d), lambda i: (i, 0)), _full((1, d)), _resident((e, d))],
        out_specs=[pl.BlockSpec((tm, d), lambda i: (i, 0)), pl.BlockSpec((tm, e), lambda i: (i, 0))],
        out_shape=[jax.ShapeDtypeStruct((t_rows, d), BF16), jax.ShapeDtypeStruct((t_rows, e), F32)],
        args=(h0, g1, win_t), name="in_proj", comm=comm)


def _tap_slot(off):
    return off % SUBLANE, (off // SUBLANE) * SUBLANE


def _fill_shifted(sh_ref, base_ref, residues, n_rows):
    for r in residues:
        sh_ref[r] = base_ref[pl.ds(r, n_rows), :]


def _mix_conv_fwd(hin, wa, wb, bb, wa_w, comm):
    t_rows = hin.shape[0]
    nt = wa_w // LANE
    ka, kb = wa.shape[0], wb.shape[0]
    nr = CONV_HALO + t_rows

    def body(bg_ref, cg_ref, ha_ref, val_ref, gt_ref, wa_ref, wb_ref, bb_ref, ya_ref, z_ref, base, sh):
        base[pl.ds(0, CONV_HALO), :] = jnp.zeros((CONV_HALO, LANE), F32)
        base[pl.ds(nr, SUBLANE), :] = jnp.zeros((SUBLANE, LANE), F32)

        def conv(w_ref, k_taps, b, n):
            acc = None
            for k in range(k_taps):
                r, q = _tap_slot(CONV_HALO - (k_taps - 1) + k)
                term = w_ref[pl.ds(k, 1), :] * sh[r, pl.ds(b + q, n), :]
                acc = term if acc is None else acc + term
            return acc

        def fill_a(b, c):
            base[pl.ds(CONV_HALO + b, CONV_CHUNK), :] = cg_ref[pl.ds(b, CONV_CHUNK), :] * ha_ref[pl.ds(b, CONV_CHUNK), :]
            return c

        _row_loop(t_rows, CONV_CHUNK, fill_a)
        _fill_shifted(sh, base, sorted({_tap_slot(CONV_HALO - (ka - 1) + k)[0] for k in range(ka)}), nr)

        def out_a(b, c):
            ya_ref[pl.ds(b, CONV_CHUNK), :] = (bg_ref[pl.ds(b, CONV_CHUNK), :] * conv(wa_ref, ka, b, CONV_CHUNK)).astype(BF16)
            return c

        _row_loop(t_rows, CONV_CHUNK, out_a)

        def fill_b(b, c):
            base[pl.ds(CONV_HALO + b, CONV_CHUNK), :] = (val_ref[pl.ds(b, CONV_CHUNK), :]
                                                          * jax.nn.sigmoid(gt_ref[pl.ds(b, CONV_CHUNK), :]))
            return c

        _row_loop(t_rows, CONV_CHUNK, fill_b)
        _fill_shifted(sh, base, range(SUBLANE), nr)

        def out_b(b, c):
            z_ref[pl.ds(b, CONV_CHUNK), :] = conv(wb_ref, kb, b, CONV_CHUNK) + bb_ref[...]
            return c

        _row_loop(t_rows, CONV_CHUNK, out_b)

    def col(g):
        return pl.BlockSpec((t_rows, LANE), lambda i, g=g: (0, g * nt + i))

    tile = lambda rows: pl.BlockSpec((rows, LANE), lambda i: (0, i))
    return _host_call(
        body, grid=(nt,),
        in_specs=[col(0), col(1), col(2), col(3), col(4), tile(ka), tile(kb), tile(1)],
        out_specs=[tile(t_rows), tile(t_rows)],
        out_shape=[jax.ShapeDtypeStruct((t_rows, wa_w), BF16), jax.ShapeDtypeStruct((t_rows, wa_w), F32)],
        scratch_shapes=[pltpu.VMEM((nr + SUBLANE, LANE), F32), pltpu.VMEM((SUBLANE, nr, LANE), F32)],
        args=(hin, hin, hin, hin, hin, wa, wb, bb), name="mix_conv_fwd", comm=comm)


def _ln_parts(z, lg, lb):
    mu = jnp.mean(z, axis=-1, keepdims=True)
    zc = z - mu
    rstd = lax.rsqrt(jnp.mean(zc * zc, axis=-1, keepdims=True) + LN_EPS)
    zh = zc * rstd
    return zh, rstd, zh * lg + lb


def _mix_ln_fwd(ya, z, lg, lb, tm, comm):
    t_rows, w = z.shape

    def body(ya_ref, z_ref, lg_ref, lb_ref, y_ref):
        _, _, ln = _ln_parts(z_ref[...], lg_ref[...], lb_ref[...])
        y_ref[:, pl.ds(0, w)] = ya_ref[...]
        y_ref[:, pl.ds(w, w)] = (ln * jax.nn.sigmoid(ln)).astype(BF16)

    blk = pl.BlockSpec((tm, w), lambda i: (i, 0))
    res, extra = _host_call(body, grid=(t_rows // tm,), in_specs=[blk, blk, _full((1, w)), _full((1, w))],
                            out_specs=[pl.BlockSpec((tm, 2 * w), lambda i: (i, 0))],
                            out_shape=[jax.ShapeDtypeStruct((t_rows, 2 * w), BF16)], args=(ya, z, lg, lb), name="mix_ln_fwd", comm=comm)
    return res[0], extra


def _out_proj(y, w_out, h0, g2, g3, tm, comm):
    t_rows, d = h0.shape

    def body(y_ref, w_ref, h0_ref, g2_ref, g3_ref, mix_ref, h1_ref, xn2_ref):
        mix = _dot_nn(y_ref[...], w_ref[...])
        mix_ref[...] = mix
        h1 = h0_ref[...] + (mix * _rstd(mix)) * g2_ref[...]
        h1_ref[...] = h1
        xn2_ref[...] = ((h1 * _rstd(h1)) * g3_ref[...]).astype(BF16)

    blk = pl.BlockSpec((tm, d), lambda i: (i, 0))
    return _host_call(
        body, grid=(t_rows // tm,), in_specs=[blk, _resident(w_out.shape), blk, _full((1, d)), _full((1, d))],
        out_specs=[blk, blk, blk],
        out_shape=[jax.ShapeDtypeStruct((t_rows, d), F32), jax.ShapeDtypeStruct((t_rows, d), F32),
                   jax.ShapeDtypeStruct((t_rows, d), BF16)],
        args=(y, w_out, h0, g2, g3), name="out_proj", comm=comm)


def _gate_up(xn2, wg_t, wu_t, tm, comm):
    t_rows, d = xn2.shape
    f = wg_t.shape[0]

    def body(x_ref, wg_ref, wu_ref, a_ref, u_ref, s_ref):
        xn = x_ref[...]
        for o, n in _chunks(f, N_CHUNK):
            a = _dot_nt(xn, wg_ref[pl.ds(o, n), :])
            u = _dot_nt(xn, wu_ref[pl.ds(o, n), :])
            a_ref[:, pl.ds(o, n)] = a.astype(BF16)
            u_ref[:, pl.ds(o, n)] = u.astype(BF16)
            s_ref[:, pl.ds(o, n)] = ((a * jax.nn.sigmoid(a)) * u).astype(BF16)

    blk = pl.BlockSpec((tm, f), lambda i: (i, 0))
    return _host_call(
        body, grid=(t_rows // tm,),
        in_specs=[pl.BlockSpec((tm, d), lambda i: (i, 0)), _resident((f, d)), _resident((f, d))],
        out_specs=[blk, blk, blk], out_shape=[jax.ShapeDtypeStruct((t_rows, f), BF16)] * 3,
        args=(xn2, wg_t, wu_t), name="gate_up", comm=comm)


def _down_loss(s, wd, h1, tgt, g4, tm, x0):
    t_rows, d = h1.shape
    f = wd.shape[0]

    def body(s_ref, w_ref, h1_ref, tgt_ref, g4_ref, dh2_ref, dff_ref, dg4_ref, loss_ref):
        i = pl.program_id(0)
        ff = _dot_nn(s_ref[...], w_ref[...])
        r4 = _rstd(ff)
        fh = ff * r4
        g4 = g4_ref[...]
        h2 = h1_ref[...] + fh * g4
        row = i * tm + lax.broadcasted_iota(jnp.int32, (tm, 1), 0)
        diff = jnp.where(row >= x0, h2 - tgt_ref[...], 0.0)
        dh2 = diff / d
        dh2_ref[...] = dh2
        dff_ref[...] = _rms_bwd(dh2 * g4, fh, r4).astype(BF16)
        _acc_rows(dg4_ref, dh2 * fh, i == 0)
        _acc_rows(loss_ref, diff * diff, i == 0)

    blk = pl.BlockSpec((tm, d), lambda i: (i, 0))
    res, _ = _host_call(
        body, grid=(t_rows // tm,),
        in_specs=[pl.BlockSpec((tm, f), lambda i: (i, 0)), _resident((f, d)), blk, blk, _full((1, d))],
        out_specs=[blk, blk, _full((1, d)), _full((1, d))],
        out_shape=[jax.ShapeDtypeStruct((t_rows, d), F32), jax.ShapeDtypeStruct((t_rows, d), BF16),
                   jax.ShapeDtypeStruct((1, d), F32), jax.ShapeDtypeStruct((1, d), F32)],
        args=(s, wd, h1, tgt, g4), name="down_loss")
    return res


def _bwd_down(dff, wd, a, u, tm, comm):
    t_rows, d = dff.shape
    f = wd.shape[0]

    def body(dff_ref, w_ref, a_ref, u_ref, da_ref, du_ref):
        dff_v = dff_ref[...]
        for o, n in _chunks(f, N_CHUNK):
            ds = _dot_nt(dff_v, w_ref[pl.ds(o, n), :])
            av = a_ref[:, pl.ds(o, n)].astype(F32)
            uv = u_ref[:, pl.ds(o, n)].astype(F32)
            sig = jax.nn.sigmoid(av)
            da_ref[:, pl.ds(o, n)] = (ds * uv * _silu_grad(av, sig)).astype(BF16)
            du_ref[:, pl.ds(o, n)] = (ds * (av * sig)).astype(BF16)

    blk = pl.BlockSpec((tm, f), lambda i: (i, 0))
    return _host_call(
        body, grid=(t_rows // tm,),
        in_specs=[pl.BlockSpec((tm, d), lambda i: (i, 0)), _resident((f, d)), blk, blk],
        out_specs=[blk, blk], out_shape=[jax.ShapeDtypeStruct((t_rows, f), BF16)] * 2,
        args=(dff, wd, a, u), name="bwd_down", comm=comm)


def _wgrad(a, b, name, after=()):
    d = b.shape[1]
    t_rows = b.shape[0]
    stacked = a.ndim == 3
    n = a.shape[-1]
    groups = a.shape[0] if stacked else 1
    steps = 1 if stacked else 2
    tile = max(t for t in range(LANE, min(n // steps, WGRAD_TILE_MAX) + 1, LANE) if n % t == 0)
    tiles = n // tile

    def body(a_ref, b_ref, o_ref):
        o_ref[...] = lax.dot_general(a_ref[...], b_ref[...], (((0,), (0,)), ((), ())),
                                     preferred_element_type=F32).astype(BF16)

    if stacked:
        a_spec = pl.BlockSpec((None, t_rows, tile), lambda g, i: (g, 0, i))
    else:
        a_spec = pl.BlockSpec((t_rows, tile), lambda g, i: (0, i))
    res, _ = _host_call(
        body, grid=(groups, tiles), in_specs=[a_spec, _resident((t_rows, d))],
        out_specs=[pl.BlockSpec((tile, d), lambda g, i: (g * tiles + i, 0))],
        out_shape=[jax.ShapeDtypeStruct((groups * n, d), BF16)], args=(a, b), name=name, after=after)
    return res[0]


def _bwd_ffn_in(da, du, wg_t, wu_t, h1, dh2, g3, tm, comm):
    t_rows, d = h1.shape
    f = wg_t.shape[0]

    def body(da_ref, du_ref, wg_ref, wu_ref, h1_ref, dh2_ref, g3_ref, dh1_ref, dg3_ref):
        dxn2 = _dot_nn(da_ref[...], wg_ref[...]) + _dot_nn(du_ref[...], wu_ref[...])
        h1 = h1_ref[...]
        r3 = _rstd(h1)
        h1h = h1 * r3
        _acc_rows(dg3_ref, dxn2 * h1h, pl.program_id(0) == 0)
        dh1_ref[...] = dh2_ref[...] + _rms_bwd(dxn2 * g3_ref[...], h1h, r3)

    blk = pl.BlockSpec((tm, d), lambda i: (i, 0))
    blkf = pl.BlockSpec((tm, f), lambda i: (i, 0))
    return _host_call(
        body, grid=(t_rows // tm,),
        in_specs=[blkf, blkf, _resident((f, d)), _resident((f, d)), blk, blk, _full((1, d))],
        out_specs=[blk, _full((1, d))],
        out_shape=[jax.ShapeDtypeStruct((t_rows, d), F32), jax.ShapeDtypeStruct((1, d), F32)],
        args=(da, du, wg_t, wu_t, h1, dh2, g3), name="bwd_ffn_in", comm=comm)


def _bwd_out_proj(dh1, mix, w_out, g2, tm, after):
    t_rows, d = dh1.shape

    def body(dh1_ref, mix_ref, w_ref, g2_ref, dmix_ref, dy_ref, dg2_ref):
        mix = mix_ref[...]
        r2 = _rstd(mix)
        mh = mix * r2
        dh1 = dh1_ref[...]
        _acc_rows(dg2_ref, dh1 * mh, pl.program_id(0) == 0)
        dmix = _rms_bwd(dh1 * g2_ref[...], mh, r2).astype(BF16)
        dmix_ref[...] = dmix
        dy_ref[...] = _dot_nt(dmix, w_ref[...])

    blk = pl.BlockSpec((tm, d), lambda i: (i, 0))
    res, _ = _host_call(
        body, grid=(t_rows // tm,), in_specs=[blk, blk, _resident(w_out.shape), _full((1, d))],
        out_specs=[blk, blk, _full((1, d))],
        out_shape=[jax.ShapeDtypeStruct((t_rows, d), BF16), jax.ShapeDtypeStruct((t_rows, d), F32),
                   jax.ShapeDtypeStruct((1, d), F32)],
        args=(dh1, mix, w_out, g2), name="bwd_out_proj", after=after)
    return res


def _mix_ln_bwd(z, dy, lg, lb, tm):
    t_rows, w = z.shape

    def body(z_ref, dyb_ref, lg_ref, lb_ref, dz_ref, dlg_ref, dlb_ref, dbb_ref):
        first = pl.program_id(0) == 0
        lg = lg_ref[...]
        zh, rstd, ln = _ln_parts(z_ref[...], lg, lb_ref[...])
        dln = dyb_ref[...] * _silu_grad(ln, jax.nn.sigmoid(ln))
        _acc_rows(dlg_ref, dln * zh, first)
        _acc_rows(dlb_ref, dln, first)
        dzh = dln * lg
        dz = rstd * (dzh - jnp.mean(dzh, axis=-1, keepdims=True) - zh * jnp.mean(dzh * zh, axis=-1, keepdims=True))
        dz_ref[...] = dz
        _acc_rows(dbb_ref, dz, first)

    blk = pl.BlockSpec((tm, w), lambda i: (i, 0))
    vec = _full((1, w))
    res, _ = _host_call(
        body, grid=(t_rows // tm,), in_specs=[blk, pl.BlockSpec((tm, w), lambda i: (i, 1)), vec, vec],
        out_specs=[blk, vec, vec, vec],
        out_shape=[jax.ShapeDtypeStruct((t_rows, w), F32)] + [jax.ShapeDtypeStruct((1, w), F32)] * 3,
        args=(z, dy, lg, lb), name="mix_ln_bwd")
    return res


def _mix_conv_bwd(hin, dy, dz, wa, wb, wa_w, comm):
    t_rows = hin.shape[0]
    nt = wa_w // LANE
    ka, kb = wa.shape[0], wb.shape[0]
    nr = CONV_HALO + t_rows
    kb_rows = -(-kb // SUBLANE) * SUBLANE

    def body(bg_ref, cg_ref, ha_ref, val_ref, gt_ref, dya_ref, dz_ref, wa_ref, wb_ref,
             dh_ref, dwa_ref, dwb_ref, base, sh, based, shd, tmp, wbc):
        zeros = lambda n: jnp.zeros((n, LANE), F32)
        base[pl.ds(0, CONV_HALO), :] = zeros(CONV_HALO)
        base[pl.ds(nr, SUBLANE), :] = zeros(SUBLANE)
        based[pl.ds(t_rows, CONV_HALO + SUBLANE), :] = zeros(CONV_HALO + SUBLANE)

        def fwd_slot(k_taps, k):
            return _tap_slot(CONV_HALO - (k_taps - 1) + k)

        def bwd_slot(k_taps, k):
            return _tap_slot(k_taps - 1 - k)

        def conv(w_ref, k_taps, src, slot, b, n):
            acc = None
            for k in range(k_taps):
                r, q = slot(k_taps, k)
                term = w_ref[pl.ds(k, 1), :] * src[r, pl.ds(b + q, n), :]
                acc = term if acc is None else acc + term
            return acc

        def by_residue(k_taps, slot):
            groups = {}
            for k in range(k_taps):
                r, q = slot(k_taps, k)
                groups.setdefault(r, []).append((k, q // SUBLANE))
            return groups

        def wgrad_loop(w_ref, k_taps):
            n_sub = WGRAD_ROWS // SUBLANE
            for k in range(k_taps):
                wbc[k] = jnp.broadcast_to(w_ref[pl.ds(k, 1), :], (SUBLANE, LANE))
            fwd, bwd = by_residue(k_taps, fwd_slot), by_residue(k_taps, bwd_slot)

            def window(src, r, taps, b):
                span = n_sub + max(qi for _, qi in taps)
                return [src[r, pl.ds(b + SUBLANE * i, SUBLANE), :] for i in range(span)]

            def step(b, accs):
                accs = list(accs)
                dv = [based[pl.ds(b + SUBLANE * j, SUBLANE), :] for j in range(n_sub)]
                for r, taps in fwd.items():
                    win = window(sh, r, taps, b)
                    for k, qi in taps:
                        t = dv[0] * win[qi]
                        for j in range(1, n_sub):
                            t = t + dv[j] * win[qi + j]
                        accs[k] = accs[k] + t
                outs = [None] * n_sub
                for r, taps in bwd.items():
                    win = window(shd, r, taps, b)
                    for k, qi in taps:
                        wk = wbc[k]
                        for j in range(n_sub):
                            term = wk * win[qi + j]
                            outs[j] = term if outs[j] is None else outs[j] + term
                for j in range(n_sub):
                    tmp[pl.ds(b + SUBLANE * j, SUBLANE), :] = outs[j]
                return tuple(accs)

            return _row_loop(t_rows, WGRAD_ROWS, step, tuple(zeros(SUBLANE) for _ in range(k_taps)))

        def store_taps(ref, accs, rows):
            for k, acc in enumerate(accs):
                ref[pl.ds(k, 1), :] = jnp.sum(acc, axis=0, keepdims=True)
            if rows > len(accs):
                ref[pl.ds(len(accs), rows - len(accs)), :] = zeros(rows - len(accs))

        def fill_a(b, c):
            sl = pl.ds(b, CONV_CHUNK)
            base[pl.ds(CONV_HALO + b, CONV_CHUNK), :] = cg_ref[sl, :] * ha_ref[sl, :]
            based[sl, :] = dya_ref[sl, :] * bg_ref[sl, :]
            return c

        _row_loop(t_rows, CONV_CHUNK, fill_a)
        _fill_shifted(sh, base, sorted({fwd_slot(ka, k)[0] for k in range(ka)}), nr)
        _fill_shifted(shd, based, sorted({bwd_slot(ka, k)[0] for k in range(ka)}), nr)

        def d_bgate(b, c):
            sl = pl.ds(b, CONV_CHUNK)
            dh_ref[0, sl, :] = (dya_ref[sl, :] * conv(wa_ref, ka, sh, fwd_slot, b, CONV_CHUNK)).astype(BF16)
            return c

        _row_loop(t_rows, CONV_CHUNK, d_bgate)
        store_taps(dwa_ref, wgrad_loop(wa_ref, ka), SUBLANE)

        def d_ch(b, c):
            sl = pl.ds(b, CONV_CHUNK)
            dua = tmp[sl, :]
            dh_ref[1, sl, :] = (dua * ha_ref[sl, :]).astype(BF16)
            dh_ref[2, sl, :] = (dua * cg_ref[sl, :]).astype(BF16)
            return c

        _row_loop(t_rows, CONV_CHUNK, d_ch)

        def fill_b(b, c):
            sl = pl.ds(b, CONV_CHUNK)
            base[pl.ds(CONV_HALO + b, CONV_CHUNK), :] = val_ref[sl, :] * jax.nn.sigmoid(gt_ref[sl, :])
            based[sl, :] = dz_ref[sl, :]
            return c

        _row_loop(t_rows, CONV_CHUNK, fill_b)
        _fill_shifted(sh, base, range(SUBLANE), nr)
        _fill_shifted(shd, based, range(SUBLANE), nr)
        store_taps(dwb_ref, wgrad_loop(wb_ref, kb), kb_rows)

        def d_glu(b, c):
            sl = pl.ds(b, CONV_CHUNK)
            dgg = tmp[sl, :]
            sig = jax.nn.sigmoid(gt_ref[sl, :])
            dh_ref[3, sl, :] = (dgg * sig).astype(BF16)
            dh_ref[4, sl, :] = (dgg * val_ref[sl, :] * (sig * (1.0 - sig))).astype(BF16)
            return c

        _row_loop(t_rows, CONV_CHUNK, d_glu)

    def col(g):
        return pl.BlockSpec((t_rows, LANE), lambda i, g=g: (0, g * nt + i))

    tile = lambda rows: pl.BlockSpec((rows, LANE), lambda i: (0, i))
    return _host_call(
        body, grid=(nt,),
        in_specs=[col(0), col(1), col(2), col(3), col(4), tile(t_rows), tile(t_rows), tile(ka), tile(kb)],
        out_specs=[pl.BlockSpec((5, t_rows, LANE), lambda i: (0, 0, i)), tile(SUBLANE), tile(kb_rows)],
        out_shape=[jax.ShapeDtypeStruct((5, t_rows, wa_w), BF16), jax.ShapeDtypeStruct((SUBLANE, wa_w), F32),
                   jax.ShapeDtypeStruct((kb_rows, wa_w), F32)],
        scratch_shapes=[pltpu.VMEM((nr + SUBLANE, LANE), F32), pltpu.VMEM((SUBLANE, nr, LANE), F32),
                        pltpu.VMEM((nr + SUBLANE, LANE), F32), pltpu.VMEM((SUBLANE, nr, LANE), F32),
                        pltpu.VMEM((t_rows, LANE), F32), pltpu.VMEM((kb_rows, SUBLANE, LANE), F32)],
        args=(hin, hin, hin, hin, hin, dy, dz, wa, wb), name="mix_conv_bwd", comm=comm)


def _bwd_in_proj(dh5, win_t, h0, dh1, g1, tm, comm):
    t_rows, d = h0.shape
    groups, _, w = dh5.shape

    def body(dh_ref, w_ref, h0_ref, dh1_ref, g1_ref, dh0_ref, dg1_ref):
        dxn1 = None
        for g in range(groups):
            part = _dot_nn(dh_ref[g], w_ref[pl.ds(g * w, w), :])
            dxn1 = part if dxn1 is None else dxn1 + part
        h0 = h0_ref[...]
        r1 = _rstd(h0)
        h0h = h0 * r1
        _acc_rows(dg1_ref, dxn1 * h0h, pl.program_id(0) == 0)
        dh0_ref[...] = dh1_ref[...] + _rms_bwd(dxn1 * g1_ref[...], h0h, r1)

    blk = pl.BlockSpec((tm, d), lambda i: (i, 0))
    return _host_call(
        body, grid=(t_rows // tm,),
        in_specs=[pl.BlockSpec((groups, tm, w), lambda i: (0, i, 0)), _resident(win_t.shape), blk, blk, _full((1, d))],
        out_specs=[blk, _full((1, d))],
        out_shape=[jax.ShapeDtypeStruct((t_rows, d), F32), jax.ShapeDtypeStruct((1, d), F32)],
        args=(dh5, win_t, h0, dh1, g1), name="bwd_in_proj", comm=comm)


def _reduce_small(smalls, d, after):
    (dmeta, dg1, dg2, dg3, dg4, dbb, dlg, dlb, lossv, dwa, dwb) = smalls
    half = d // 2
    kb_rows = dwb.shape[0]

    def body(dmeta_ref, dg1_ref, dg2_ref, dg3_ref, dg4_ref, dbb_ref, dlg_ref, dlb_ref, loss_ref, dwa_ref, dwb_ref,
             ptot_ref, pbuf, psib, chip_p, ps_send, ps_recv, pc_send, pc_recv):
        x, y, c = _mesh_pos()
        pbuf[...] = jnp.zeros_like(pbuf)
        pbuf[pl.ds(0, N_META), :] = dmeta_ref[...]
        for row, ref in ((16, dg1_ref), (17, dg2_ref), (18, dg3_ref), (19, dg4_ref)):
            pbuf[pl.ds(row, 1), :] = ref[...]
        pbuf[pl.ds(20, 1), pl.ds(0, half)] = dbb_ref[...]
        pbuf[pl.ds(20, 1), pl.ds(half, half)] = dlg_ref[...]
        pbuf[pl.ds(21, 1), pl.ds(0, half)] = dlb_ref[...]
        lv = loss_ref[...]
        pbuf[pl.ds(21, 1), pl.ds(half, half)] = lv[:, :half] + lv[:, half:]
        pbuf[pl.ds(24, SUBLANE), pl.ds(0, half)] = dwa_ref[...]
        pbuf[pl.ds(32, kb_rows), pl.ds(0, half)] = dwb_ref[...]
        to_sib = _remote(pbuf, psib, ps_send.at[0], ps_recv.at[0], (x, y, 1 - c))
        to_sib.start()
        to_sib.wait_recv()
        my_chip = 2 * x + y
        chip_p[my_chip] = pbuf[...] + psib[...]
        to_sib.wait_send()
        slot = chip_p.at[my_chip]
        cps = [_remote(slot, slot, pc_send.at[k], pc_recv.at[k], (*chip, c)) for k, chip in enumerate(_other_chips(x, y))]
        for cp in cps:
            cp.start()
        for cp in cps:
            cp.wait_recv()
        ptot_ref[...] = ((chip_p[0] + chip_p[1]) + chip_p[2]) + chip_p[3]
        for cp in cps:
            cp.wait_send()

    return _host_call(
        body, grid=(), in_specs=[VMEM] * 11, out_specs=[VMEM], out_shape=[jax.ShapeDtypeStruct((SMALL_ROWS, d), F32)],
        scratch_shapes=[pltpu.VMEM((SMALL_ROWS, d), F32), pltpu.VMEM((SMALL_ROWS, d), F32), pltpu.VMEM((4, SMALL_ROWS, d), F32),
                        pltpu.SemaphoreType.DMA((1,)), pltpu.SemaphoreType.DMA((1,)),
                        pltpu.SemaphoreType.DMA((3,)), pltpu.SemaphoreType.DMA((3,))],
        args=smalls, name="reduce_small", after=after)


def _adamw(w, g, m, v):
    m = ADAM_B1 * m + (1.0 - ADAM_B1) * g
    v = ADAM_B2 * v + (1.0 - ADAM_B2) * jnp.square(g)
    m_hat = m / (1.0 - ADAM_B1 ** ADAM_STEP)
    v_hat = v / (1.0 - ADAM_B2 ** ADAM_STEP)
    delta = -ADAM_LR * (m_hat / (jnp.sqrt(v_hat) + ADAM_EPS) + ADAM_WD * w)
    return delta, m, v


def _adam_big(g, pair, part, w, m, v, name):
    r, d = w.shape
    cols = d // ADAM_COL_BLOCKS

    def body(me_ref, g_ref, pair_ref, part_ref, w_ref, m_ref, v_ref, go_ref, d_ref, mo_ref, vo_ref):
        g = g_ref[...].astype(F32) + pair_ref[...].astype(F32)
        for k in range(3):
            g = g + part_ref[k].astype(F32)
        go_ref[...] = g
        d_ref[...], mo_ref[...], vo_ref[...] = _adamw(w_ref[...], g, m_ref[...], v_ref[...])

    blk = pl.BlockSpec((r, cols), lambda i, me_ref: (0, i))
    grid_spec = pltpu.PrefetchScalarGridSpec(
        num_scalar_prefetch=1, grid=(ADAM_COL_BLOCKS,),
        in_specs=[pl.BlockSpec((r, cols), lambda i, me_ref: (me_ref[0], i)),
                  pl.BlockSpec((None, r, cols), lambda i, me_ref: (0, 0, i)),
                  pl.BlockSpec((3, r, cols), lambda i, me_ref: (0, 0, i)), blk, blk, blk],
        out_specs=[blk, blk, blk, blk])
    me = jnp.reshape(_dev_index(*_mesh_pos()), (1,)).astype(jnp.int32)
    return pl.pallas_call(body, out_shape=[jax.ShapeDtypeStruct((r, d), F32)] * 4, grid_spec=grid_spec, name=name,
                          compiler_params=_cparams(1))(me, g, pair, part, w, m, v)


def _adam_small(gs, ws, ms, vs):
    n = len(gs)

    def body(*refs):
        ins, outs = refs[:4 * n], refs[4 * n:]
        for i in range(n):
            g = ins[i][...]
            delta, m, v = _adamw(ins[n + i][...], g, ins[2 * n + i][...], ins[3 * n + i][...])
            outs[i][...] = delta
            outs[n + i][...] = m
            outs[2 * n + i][...] = v

    shapes = [jax.ShapeDtypeStruct(w.shape, F32) for w in ws]
    return pl.pallas_call(body, out_shape=shapes * 3, name="adam_small", compiler_params=_cparams())(*gs, *ws, *ms, *vs)


def kernel(x, meta_tokens, pre_mix_norm, w_in, conv_a_w, conv_b_w, conv_b_bias, ln_b_gain, ln_b_bias, w_out, post_mix_norm, pre_ffn_norm, w_gate, w_up, w_down, post_ffn_norm, loss_target, m_meta_tokens, m_pre_mix_norm, m_w_in, m_conv_a_w, m_conv_b_w, m_conv_b_bias, m_ln_b_gain, m_ln_b_bias, m_w_out, m_post_mix_norm, m_pre_ffn_norm, m_w_gate, m_w_up, m_w_down, m_post_ffn_norm, v_meta_tokens, v_pre_mix_norm, v_w_in, v_conv_a_w, v_conv_b_w, v_conv_b_bias, v_ln_b_gain, v_ln_b_bias, v_w_out, v_post_mix_norm, v_pre_ffn_norm, v_w_gate, v_w_up, v_w_down, v_post_ffn_norm):
    _, seq, d = x.shape
    ka, ca_loc = conv_a_w.shape[1:]
    kb, cb_loc = conv_b_w.shape[1:]
    wa_w = ca_loc * N_DEV
    assert cb_loc == ca_loc and wa_w % LANE == 0 and w_in.shape[2] * N_DEV == 5 * wa_w
    pad = (-(N_META + seq)) % ROW_ALIGN
    x0 = pad + N_META
    t_rows = x0 + seq
    assert t_rows % (N_ROW_BLOCKS * BF16_ROWS) == 0 and t_rows % CONV_CHUNK == 0 and d % LANE == 0
    tm = t_rows // N_ROW_BLOCKS
    me = _dev_index(*_mesh_pos())

    def as_rows(w_in_like, w_out_like, w_gate_like, w_up_like, w_down_like):
        return (w_in_like[0].T, w_out_like[0], w_gate_like[0].T, w_up_like[0].T, w_down_like[0])

    w_loc = as_rows(w_in, w_out, w_gate, w_up, w_down)
    rows = [w.shape[0] for w in w_loc]
    assert all(r % ADD_CHUNK == 0 for r in rows)
    P_IN, P_OUT, P_GATE, P_UP, P_DOWN = range(N_BIG)

    sm = jnp.zeros((SM_ROWS, LANE), F32)
    sm = sm.at[0:N_META, :].set(meta_tokens)
    sm = sm.at[16:16 + ka, 0:ca_loc].set(conv_a_w[0])
    sm = sm.at[24:24 + kb, 0:cb_loc].set(conv_b_w[0])
    wl, wfull, sm_all, h0, tgt = _gather_first(w_loc, sm, [(P_IN, 0, rows[P_IN])], x[0], loss_target[0], t_rows, x0)
    wa =jnp.transpose(sm_all[:, 16:16 + ka, 0:ca_loc], (1, 0, 2)).reshape(ka, wa_w)
    wb = jnp.transpose(sm_all[:, 24:24 + kb, 0:cb_loc], (1, 0, 2)).reshape(kb, wa_w)

    later = (P_OUT, P_GATE, P_UP, P_DOWN)
    sems, wl, started, _ = _gather_start(wl, wfull, later, rows, START_BARRIER_IDS[0])
    for p, arr in zip(later, started):
        wfull[p] = arr

    def arrived(p, after, name):
        nonlocal wl
        wl, wfull[p] = _gather_wait(wl, wfull[p], sems[later.index(p)], after, rows[p], name)
        return _forward_comm(wfull[p], rows[p])

    (xn1, hin), _ = _in_proj(h0, pre_mix_norm, wfull[P_IN], tm, None)
    (ya, z), (wfull[P_OUT],) = _mix_conv_fwd(hin, wa, wb, conv_b_bias, wa_w, arrived(P_OUT, hin, "gather_wait_out"))
    y, (wfull[P_GATE],) = _mix_ln_fwd(ya, z, ln_b_gain, ln_b_bias, tm, arrived(P_GATE, z, "gather_wait_gate"))
    (mix, h1, xn2), _ = _out_proj(y, wfull[P_OUT], h0, post_mix_norm, pre_ffn_norm, tm, None)
    arrived(P_UP, xn2, "gather_wait_up")
    wfull[P_UP] = _forward_now(wfull[P_UP], rows[P_UP], "forward_up")
    (a, u, s), _ = _gate_up(xn2, wfull[P_GATE], wfull[P_UP], tm, None)
    arrived(P_DOWN, s, "gather_wait_down")
    wfull[P_DOWN] = _forward_now(wfull[P_DOWN], rows[P_DOWN], "forward_down")
    dh2, dff, dg4, lossv = _down_loss(s, wfull[P_DOWN], h1, tgt, post_ffn_norm, tm, x0)

    gwd = _wgrad(s, dff, "wgrad_down")
    (da, du), (pair_d,) = _bwd_down(dff, wfull[P_DOWN], a, u, tm, _pair_comm(gwd, rows[P_DOWN]))
    (flight_d,), token = _chip_start([_pair_sum(gwd, pair_d, rows[P_DOWN], "pair_sum_down")], "chip_start_down", START_BARRIER_IDS[1])
    gwg = _wgrad(da, xn2, "wgrad_gate", [token])
    gwu = _wgrad(du, xn2, "wgrad_up")
    (dh1, dg3), (pair_g, pair_u) = _bwd_ffn_in(da, du, wfull[P_GATE], wfull[P_UP], h1, dh2, pre_ffn_norm, tm,
                                               _merge_comms([_pair_comm(gwg, rows[P_GATE]), _pair_comm(gwu, rows[P_UP])]))
    (flight_g, flight_u), token = _chip_start([_pair_sum(gwg, pair_g, rows[P_GATE], "pair_sum_gate"),
                                               _pair_sum(gwu, pair_u, rows[P_UP], "pair_sum_up")], "chip_start_gate_up",
                                              START_BARRIER_IDS[2])
    dmix, dy, dg2 = _bwd_out_proj(dh1, mix, wfull[P_OUT], post_mix_norm, tm, [token])
    gwo = _wgrad(y, dmix, "wgrad_out")
    dz, dlg, dlb, dbb = _mix_ln_bwd(z, dy, ln_b_gain, ln_b_bias, tm)
    (dh5, dwa, dwb), (pair_o,) = _mix_conv_bwd(hin, dy, dz, wa, wb, wa_w, _pair_comm(gwo, rows[P_OUT]))
    (flight_o,), token = _chip_start([_pair_sum(gwo, pair_o, rows[P_OUT], "pair_sum_out")], "chip_start_out", START_BARRIER_IDS[3])
    gwi = _wgrad(dh5, xn1, "wgrad_in", [token])
    (dh0, dg1), (pair_i,) = _bwd_in_proj(dh5, wfull[P_IN], h0, dh1, pre_mix_norm, tm, _pair_comm(gwi, rows[P_IN]))
    grad_x = dh0[x0:][None]
    dmeta = dh0[x0 - N_META:x0]
    (ptot,), _ = _reduce_small((dmeta, dg1, dg2, dg3, dg4, dbb, dlg, dlb, lossv, dwa, dwb), d, [])
    (flight_i,), token = _chip_start([_pair_sum(gwi, pair_i, rows[P_IN], "pair_sum_in", [ptot])], "chip_start_in",
                                     START_BARRIER_IDS[4])

    def landed(flight, after, tag):
        sems_p, sums, land = flight
        return _chip_wait(sums, land, sems_p, after, "chip_wait_" + tag)

    part_d = landed(flight_d, token, "down")
    part_g = landed(flight_g, token, "gate")
    part_u = landed(flight_u, token, "up")
    part_o = landed(flight_o, token, "out")

    half = d // 2
    loss = (0.5 / d) * jnp.sum(ptot[21, half:])
    g_meta = lax.dynamic_slice(ptot, (0, me * (d // N_DEV)), (N_META, d // N_DEV))
    g_small = [g_meta, ptot[16:17], lax.dynamic_slice(ptot, (24, me * ca_loc), (ka, ca_loc))[None],
               lax.dynamic_slice(ptot, (32, me * cb_loc), (kb, cb_loc))[None],
               ptot[20:21, :half], ptot[20:21, half:], ptot[21:22, :half], ptot[17:18], ptot[18:19], ptot[19:20]]
    w_small = [meta_tokens, pre_mix_norm, conv_a_w, conv_b_w, conv_b_bias, ln_b_gain, ln_b_bias, post_mix_norm,
               pre_ffn_norm, post_ffn_norm]
    m_small = [m_meta_tokens, m_pre_mix_norm, m_conv_a_w, m_conv_b_w, m_conv_b_bias, m_ln_b_gain, m_ln_b_bias,
               m_post_mix_norm, m_pre_ffn_norm, m_post_ffn_norm]
    v_small = [v_meta_tokens, v_pre_mix_norm, v_conv_a_w, v_conv_b_w, v_conv_b_bias, v_ln_b_gain, v_ln_b_bias,
               v_post_mix_norm, v_pre_ffn_norm, v_post_ffn_norm]
    small = _adam_small(g_small, w_small, m_small, v_small)
    n_small = len(w_small)
    d_small, nm_small, nv_small = small[:n_small], small[n_small:2 * n_small], small[2 * n_small:]

    m_loc = as_rows(m_w_in, m_w_out, m_w_gate, m_w_up, m_w_down)
    v_loc = as_rows(v_w_in, v_w_out, v_w_gate, v_w_up, v_w_down)
    full_grads = {P_IN: gwi, P_OUT: gwo, P_GATE: gwg, P_UP: gwu, P_DOWN: gwd}
    pairs = {P_IN: pair_i, P_OUT: pair_o, P_GATE: pair_g, P_UP: pair_u, P_DOWN: pair_d}
    parts = {P_OUT: part_o, P_GATE: part_g, P_UP: part_u, P_DOWN: part_d}
    names = {P_IN: "w_in", P_OUT: "w_out", P_GATE: "w_gate", P_UP: "w_up", P_DOWN: "w_down"}
    bigs = {}
    res = None
    for p in (P_DOWN, P_GATE, P_UP, P_OUT, P_IN):
        if p == P_IN:
            parts[p] = landed(flight_i, res[1], "in")
        res = _adam_big(full_grads[p], pairs[p], parts[p], w_loc[p], m_loc[p], v_loc[p], "adam_" + names[p])
        bigs[names[p]] = [(o.T if p in (P_IN, P_GATE, P_UP) else o)[None] for o in res]

    def ordered(pick_small, pick_big):
        sm_it = iter(range(n_small))
        out = []
        for name in ("s", "s", "w_in", "s", "s", "s", "s", "s", "w_out", "s", "s", "w_gate", "w_up", "w_down", "s"):
            out.append(pick_small(next(sm_it)) if name == "s" else pick_big(name))
        return out

    grads = ordered(lambda i: g_small[i], lambda n: bigs[n][0])
    deltas = ordered(lambda i: d_small[i], lambda n: bigs[n][1])
    new_m = ordered(lambda i: nm_small[i], lambda n: bigs[n][2])
    new_v = ordered(lambda i: nv_small[i], lambda n: bigs[n][3])
    return (loss, grad_x, *grads, *deltas, *new_m, *new_v)
```

```python
import jax
import jax.numpy as jnp
from jax import lax
from jax.experimental import pallas as pl
from jax.experimental.pallas import tpu as pltpu

F32 = jnp.float32
BF16 = jnp.bfloat16
MESH = pl.DeviceIdType.MESH

N_META = 16
N_DEV = 8
RMS_EPS = 1e-6
LN_EPS = 1e-5
ADAM_LR = 0.001
ADAM_B1 = 0.9
ADAM_B2 = 0.999
ADAM_EPS = 1e-08
ADAM_WD = 0.01
ADAM_STEP = 10

LANE = 128
SUBLANE = 8
BF16_ROWS = 16
ROW_ALIGN = 128
N_ROW_BLOCKS = 4
CONV_HALO = 32
CONV_CHUNK = 64
WGRAD_ROWS = 32
N_CHUNK = 512
WGRAD_TILE_MAX = 1408
ADD_CHUNK = 32
ADAM_COL_BLOCKS = 4
COPY_PIECES = 4
V7X_VMEM_BYTES = 64 * 1024 * 1024
VMEM_LIMIT = V7X_VMEM_BYTES - 6 * 1024 * 1024
SMALL_ROWS = 64
SM_ROWS = 56
N_BIG = 5

ANY = pl.BlockSpec(memory_space=pl.ANY)
VMEM = pl.BlockSpec(memory_space=pltpu.VMEM)


def _cparams(n_grid_axes=0):
    sem = ("arbitrary",) * n_grid_axes if n_grid_axes else None
    return pltpu.CompilerParams(dimension_semantics=sem, vmem_limit_bytes=VMEM_LIMIT)


def _mesh_pos():
    return lax.axis_index("x"), lax.axis_index("y"), lax.axis_index("c")


def _dev_index(px, py, pc):
    return 4 * px + 2 * py + pc


def _other_chips(x, y):
    return [(1 - x, y), (x, 1 - y), (1 - x, 1 - y)]


def _full(shape):
    return pl.BlockSpec(shape, lambda *_: (0,) * len(shape))


def _resident(shape):
    return pl.BlockSpec(shape, lambda *_: (0,) * len(shape), pipeline_mode=pl.Buffered(1))


def _dot_nt(a, w):
    return lax.dot_general(a, w, (((1,), (1,)), ((), ())), preferred_element_type=F32)


def _dot_nn(a, w):
    return jnp.dot(a, w, preferred_element_type=F32)


def _chunks(n, c):
    out, o = [], 0
    while o < n:
        out.append((o, min(c, n - o)))
        o += c
    return out


def _rstd(h):
    return lax.rsqrt(jnp.mean(h * h, axis=-1, keepdims=True) + RMS_EPS)


def _rms_bwd(dyh, yh, r):
    return r * (dyh - yh * jnp.mean(dyh * yh, axis=-1, keepdims=True))


def _silu_grad(a, sig):
    return sig * (1.0 + a * (1.0 - sig))


def _acc_rows(ref, val, first):
    s = jnp.sum(val, axis=0, keepdims=True)

    @pl.when(first)
    def _():
        ref[...] = s

    @pl.when(jnp.logical_not(first))
    def _():
        ref[...] += s


def _row_loop(t_rows, chunk, fn, carry=None):
    def step(i, c):
        return fn(pl.multiple_of(i * chunk, chunk), c)

    return lax.fori_loop(0, t_rows // chunk, step, carry)


def _remote(src, dst, send_sem, recv_sem, to):
    return pltpu.make_async_remote_copy(src_ref=src, dst_ref=dst, send_sem=send_sem, recv_sem=recv_sem,
                                        device_id=to, device_id_type=MESH)


class _SplitRemote:
    def __init__(self, src, dst, send_sem, recv_sem, to, rows, n_chunks):
        units = rows // BF16_ROWS
        n_chunks = max(1, min(n_chunks, units))
        sizes = [(units // n_chunks + (i < units % n_chunks)) * BF16_ROWS for i in range(n_chunks)]
        self.whole = _remote(src, dst, send_sem, recv_sem, to)
        self.parts, o = [], 0
        for n in sizes:
            self.parts.append(_remote(src.at[pl.ds(o, n), :], dst.at[pl.ds(o, n), :], send_sem, recv_sem, to))
            o += n

    def start(self):
        for cp in self.parts:
            cp.start()

    def wait_recv(self):
        self.whole.wait_recv()

    def wait_send(self):
        self.whole.wait_send()


class _Comm:
    def __init__(self, inputs, out_shapes, aliases, scratch, start, finish):
        self.inputs, self.out_shapes, self.aliases, self.scratch = list(inputs), list(out_shapes), dict(aliases), list(scratch)
        self.start, self.finish = start, finish


def _merge_comms(comms):
    inputs, out_shapes, aliases, scratch, spans = [], [], {}, [], []
    for cm in comms:
        spans.append((len(inputs), len(out_shapes), len(scratch), cm))
        aliases.update({len(inputs) + k: len(out_shapes) + v for k, v in cm.aliases.items()})
        inputs += cm.inputs
        out_shapes += cm.out_shapes
        scratch += cm.scratch

    def run(which):
        def fn(ins, outs, scr):
            for i0, o0, s0, cm in spans:
                getattr(cm, which)(ins[i0:i0 + len(cm.inputs)], outs[o0:o0 + len(cm.out_shapes)], scr[s0:s0 + len(cm.scratch)])
        return fn

    return _Comm(inputs, out_shapes, aliases, scratch, run("start"), run("finish"))


def _host_call(body, *, grid, in_specs, out_specs, out_shape, args, name, scratch_shapes=(), comm=None, after=()):
    talks = comm is not None
    if comm is None:
        comm = _Comm([], [], {}, [], lambda *_: None, lambda *_: None)
    n_in, n_out, n_scr = len(args), len(out_shape), len(scratch_shapes)
    c_in, c_out = len(comm.inputs), len(comm.out_shapes)
    n_after = len(after)

    def open_comm(c_ins, c_outs, c_scr):
        if talks:
            _pair_handshake()
        comm.start(c_ins, c_outs, c_scr)

    def hosted(*refs):
        ins, c_ins = refs[:n_in], refs[n_in:n_in + c_in]
        o0 = n_in + c_in + n_after
        outs, c_outs = refs[o0:o0 + n_out], refs[o0 + n_out:o0 + n_out + c_out]
        s0 = o0 + n_out + c_out
        scr, c_scr = refs[s0:s0 + n_scr], refs[s0 + n_scr:]
        if not grid:
            open_comm(c_ins, c_outs, c_scr)
            body(*ins, *outs, *scr)
            comm.finish(c_ins, c_outs, c_scr)
            return
        first = last = None
        for a, n in enumerate(grid):
            f, l = pl.program_id(a) == 0, pl.program_id(a) == n - 1
            first = f if first is None else jnp.logical_and(first, f)
            last = l if last is None else jnp.logical_and(last, l)

        @pl.when(first)
        def _():
            open_comm(c_ins, c_outs, c_scr)

        body(*ins, *outs, *scr)

        @pl.when(last)
        def _():
            comm.finish(c_ins, c_outs, c_scr)

    sem = ("arbitrary",) * len(grid) if grid else None
    params = pltpu.CompilerParams(dimension_semantics=sem, vmem_limit_bytes=VMEM_LIMIT,
                                  collective_id=PAIR_BARRIER_ID if talks else None)
    res = pl.pallas_call(
        hosted, grid=grid, in_specs=list(in_specs) + [ANY] * (c_in + n_after), out_specs=list(out_specs) + [ANY] * c_out,
        out_shape=list(out_shape) + comm.out_shapes, scratch_shapes=list(scratch_shapes) + comm.scratch,
        input_output_aliases={n_in + k: n_out + v for k, v in comm.aliases.items()},
        name=name, compiler_params=params)(*args, *comm.inputs, *after)
    return list(res[:n_out]), list(res[n_out:])


PAIR_BARRIER_ID = 0
START_BARRIER_IDS = (1, 2, 3, 4, 5)


def _chips_handshake():
    x, y, c = _mesh_pos()
    barrier = pltpu.get_barrier_semaphore()
    for chip in _other_chips(x, y):
        pl.semaphore_signal(barrier, inc=1, device_id=(*chip, c), device_id_type=MESH)
    pl.semaphore_wait(barrier, 3)


def _pair_handshake():
    x, y, c = _mesh_pos()
    barrier = pltpu.get_barrier_semaphore()
    pl.semaphore_signal(barrier, inc=1, device_id=(x, y, 1 - c), device_id_type=MESH)
    pl.semaphore_wait(barrier, 1)


GATHER_SEMS = 10
D2D_CHUNKS = 8


class _Gather:
    def __init__(self, jobs, rows, lo, src_ref, dests, send_sems, recv_sems):
        x, y, c = _mesh_pos()
        me, sib = (x, y, c), (x, y, 1 - c)
        nx, ny, dg = (1 - x, y, c), (x, 1 - y, c), (1 - x, 1 - y, c)
        self.relayed, self.direct, self.relay, self.to_sib, self.sib_fwd = [], [], [], [], []
        for n, (p, r0, nr) in enumerate(jobs):
            assert nr % (2 * BF16_ROWS) == 0
            half = nr // 2

            def rows_of(dev, h, p=p, r0=r0, nr=nr, half=half):
                off, cnt = (r0, nr) if h is None else (r0 + h * half, half)
                return dests[p].at[pl.ds(pl.multiple_of(_dev_index(*dev) * rows[p] + off, BF16_ROWS), cnt), :]

            def mine(h, p=p, r0=r0, nr=nr, half=half):
                off, cnt = (r0, nr) if h is None else (r0 + h * half, half)
                return src_ref.at[pl.ds(lo[p] + off, cnt), :]

            sem = lambda k, n=n: (send_sems.at[GATHER_SEMS * n + k], recv_sems.at[GATHER_SEMS * n + k])
            self.relayed.append([_remote(mine(0), rows_of(me, 0), *sem(0), nx), _remote(mine(1), rows_of(me, 1), *sem(3), ny)])
            self.direct.append([_remote(mine(1), rows_of(me, 1), *sem(1), nx), _remote(mine(0), rows_of(me, 0), *sem(2), ny)])
            self.relay.append([_remote(rows_of(nx, 0), rows_of(nx, 0), *sem(4), ny), _remote(rows_of(ny, 1), rows_of(ny, 1), *sem(5), nx)])
            self.to_sib.append(_SplitRemote(mine(None), rows_of(me, None), *sem(6), sib, nr, D2D_CHUNKS))
            self.sib_fwd.append([_SplitRemote(rows_of(dev, None), rows_of(dev, None), *sem(7 + i), sib, nr, D2D_CHUNKS)
                                 for i, dev in enumerate((nx, ny, dg))])

    def start(self):
        for group in (self.relayed, self.direct):
            for cps in group:
                for cp in cps:
                    cp.start()
        for cp in self.to_sib:
            cp.start()

    def mid(self):
        for first, relay in zip(self.relayed, self.relay):
            for arrived, onward in zip(first, relay):
                arrived.wait_recv()
                onward.start()

    def finish(self):
        for direct, relay, fwd in zip(self.direct, self.relay, self.sib_fwd):
            for k in range(2):
                direct[k].wait_recv()
                fwd[k].start()
            for cp in relay:
                cp.wait_recv()
            fwd[2].start()
        for n in range(len(self.to_sib)):
            self.to_sib[n].wait_recv()
            for cp in self.sib_fwd[n]:
                cp.wait_recv()
            for cp in self.relayed[n] + self.direct[n] + self.relay[n] + [self.to_sib[n]] + self.sib_fwd[n]:
                cp.wait_send()


HBM = pl.BlockSpec(memory_space=pltpu.HBM)
SEM = pl.BlockSpec(memory_space=pltpu.SEMAPHORE)
FLOWS = pltpu.SideEffectType.DATAFLOW_SIDE_EFFECTING


def _in_hbm(a):
    return pltpu.with_memory_space_constraint(a, pltpu.HBM)


def _gather_start(wl, dests, ps, rows, barrier_id):
    lo = [sum(rows[:p]) for p in range(N_BIG)]
    n = len(ps)

    def body(*refs):
        wl_ref, dest_refs = refs[0], refs[1:1 + n]
        sends, recvs = refs[1 + n:1 + 2 * n], refs[1 + 2 * n:1 + 3 * n]
        token = refs[-1]
        _chips_handshake()
        x, y, c = _mesh_pos()
        jme = _dev_index(x, y, c)
        for i, p in enumerate(ps):
            mine = dest_refs[i].at[pl.ds(pl.multiple_of(jme * rows[p], BF16_ROWS), rows[p]), :]
            for chip in _other_chips(x, y):
                _remote(wl_ref.at[pl.ds(lo[p], rows[p]), :], mine, sends[i], recvs[i], (*chip, c)).start()
        token[...] = jnp.zeros_like(token)

    thru = [pltpu.HBM(wl.shape, wl.dtype)] + [pltpu.HBM(dests[p].shape, BF16) for p in ps]
    res = pl.pallas_call(
        body, name="gather_start",
        out_shape=tuple([pltpu.SemaphoreType.DMA(())] * (2 * n) + thru + [jax.ShapeDtypeStruct((SUBLANE, LANE), F32)]),
        in_specs=[HBM] * (1 + n), out_specs=tuple([SEM] * (2 * n) + [HBM] * (1 + n) + [VMEM]),
        input_output_aliases={i: 2 * n + i for i in range(1 + n)},
        compiler_params=pltpu.CompilerParams(has_side_effects=FLOWS, collective_id=barrier_id))(
            _in_hbm(wl), *[_in_hbm(dests[p]) for p in ps])
    sems = [(res[i], res[n + i]) for i in range(n)]
    return sems, res[2 * n], list(res[2 * n + 1:3 * n + 1]), res[-1]


def _gather_wait(wl, dest, sems, after, r, name):
    def body(wl_ref, dest_ref, send_sem, recv_sem, after_ref, wl_out, dest_out):
        x, y, c = _mesh_pos()
        three = dest_ref.at[pl.ds(0, 3 * r), :]
        cp = _remote(three, three, send_sem, recv_sem, (x, y, 1 - c))
        cp.wait_send()
        cp.wait_recv()

    res = pl.pallas_call(
        body, name=name, out_shape=(pltpu.HBM(wl.shape, wl.dtype), pltpu.HBM(dest.shape, dest.dtype)),
        in_specs=[HBM, HBM, SEM, SEM, ANY], out_specs=(HBM, HBM), input_output_aliases={0: 0, 1: 1},
        compiler_params=pltpu.CompilerParams(has_side_effects=FLOWS))(wl, dest, sems[0], sems[1], after)
    return res[0], res[1]


def _forward_comm(dest, r):
    def descs(ins, outs, scr):
        x, y, c = _mesh_pos()
        cps = []
        for k, chip in enumerate(_other_chips(x, y)):
            blk = outs[0].at[pl.ds(pl.multiple_of(_dev_index(*chip, c) * r, BF16_ROWS), r), :]
            cps.append(_SplitRemote(blk, blk, scr[0].at[k], scr[1].at[k], (x, y, 1 - c), r, D2D_CHUNKS))
        return cps

    def start(ins, outs, scr):
        for cp in descs(ins, outs, scr):
            cp.start()

    def finish(ins, outs, scr):
        cps = descs(ins, outs, scr)
        for cp in cps:
            cp.wait_recv()
        for cp in cps:
            cp.wait_send()

    return _Comm([dest], [jax.ShapeDtypeStruct(dest.shape, dest.dtype)], {0: 0},
                 [pltpu.SemaphoreType.DMA((3,)), pltpu.SemaphoreType.DMA((3,))], start, finish)


def _forward_now(dest, r, name):
    _, (dest,) = _host_call(lambda: None, grid=(), in_specs=[], out_specs=[], out_shape=[], args=(), name=name,
                            comm=_forward_comm(dest, r))
    return dest


def _pair_comm(g, r):
    d = g.shape[1]

    def descs(ins, outs, scr):
        x, y, c = _mesh_pos()
        chips = [(x, y)] + _other_chips(x, y)
        return [_SplitRemote(ins[0].at[pl.ds(pl.multiple_of(_dev_index(*chip, 1 - c) * r, BF16_ROWS), r), :], outs[0].at[k],
                             scr[0].at[k], scr[1].at[k], (x, y, 1 - c), r, D2D_CHUNKS) for k, chip in enumerate(chips)]

    def start(ins, outs, scr):
        for cp in descs(ins, outs, scr):
            cp.start()

    def finish(ins, outs, scr):
        cps = descs(ins, outs, scr)
        for cp in cps:
            cp.wait_recv()
        for cp in cps:
            cp.wait_send()

    comm = _Comm([g], [jax.ShapeDtypeStruct((4, r, d), BF16)], {},
                 [pltpu.SemaphoreType.DMA((4,)), pltpu.SemaphoreType.DMA((4,))], start, finish)
    return comm


def _pair_sum(g, pair, r, name, after=()):
    d = g.shape[1]

    def body(g_ref, p_ref, *rest):
        o_ref, gbuf, pbuf, sems = rest[len(after):]
        x, y, c = _mesh_pos()
        loads = [pltpu.make_async_copy(p_ref.at[pl.ds(1, 3)], pbuf, sems.at[3])]
        for k, chip in enumerate(_other_chips(x, y)):
            j = _dev_index(*chip, c)
            loads.append(pltpu.make_async_copy(g_ref.at[pl.ds(pl.multiple_of(j * r, BF16_ROWS), r), :], gbuf.at[k], sems.at[k]))
        for cp in loads:
            cp.start()
        for cp in loads:
            cp.wait()
        for k in range(3):
            o_ref[k] = (gbuf[k].astype(F32) + pbuf[k].astype(F32)).astype(BF16)

    return pl.pallas_call(
        body, out_shape=jax.ShapeDtypeStruct((3, r, d), BF16), in_specs=[ANY] * (2 + len(after)), out_specs=VMEM,
        scratch_shapes=[pltpu.VMEM((3, r, d), BF16), pltpu.VMEM((3, r, d), BF16), pltpu.SemaphoreType.DMA((4,))],
        name=name, compiler_params=_cparams())(g, pair, *after)


def _chip_start(sums, name, barrier_id):
    n = len(sums)

    def body(*refs):
        srcs, lands = refs[:n], refs[n:2 * n]
        sends, recvs = refs[2 * n:3 * n], refs[3 * n:4 * n]
        _chips_handshake()
        x, y, c = _mesh_pos()
        for i in range(n):
            for k, chip in enumerate(_other_chips(x, y)):
                _remote(srcs[i].at[k], lands[i].at[k], sends[i], recvs[i], (*chip, c)).start()
        refs[-1][...] = jnp.zeros_like(refs[-1])

    zones = [pltpu.HBM(s.shape, s.dtype) for s in sums]
    res = pl.pallas_call(
        body, name=name,
        out_shape=tuple([pltpu.SemaphoreType.DMA(())] * (2 * n) + zones + zones + [jax.ShapeDtypeStruct((SUBLANE, LANE), F32)]),
        in_specs=[HBM] * (2 * n), out_specs=tuple([SEM] * (2 * n) + [HBM] * (2 * n) + [VMEM]),
        input_output_aliases={i: 2 * n + i for i in range(2 * n)},
        compiler_params=pltpu.CompilerParams(has_side_effects=FLOWS, collective_id=barrier_id))(
            *[_in_hbm(s) for s in sums], *[_in_hbm(lax.empty(s.shape, s.dtype)) for s in sums])
    flights = [((res[i], res[n + i]), res[2 * n + i], res[3 * n + i]) for i in range(n)]
    return flights, res[-1]


def _chip_wait(sums, land, sems, after, name):
    def body(sums_ref, land_ref, send_sem, recv_sem, after_ref, sums_out, land_out):
        x, y, c = _mesh_pos()
        cp = _remote(sums_ref, land_ref, send_sem, recv_sem, (x, y, 1 - c))
        cp.wait_send()
        cp.wait_recv()

    res = pl.pallas_call(
        body, name=name, out_shape=(pltpu.HBM(sums.shape, sums.dtype), pltpu.HBM(land.shape, land.dtype)),
        in_specs=[HBM, HBM, SEM, SEM, ANY], out_specs=(HBM, HBM), input_output_aliases={0: 0, 1: 1},
        compiler_params=pltpu.CompilerParams(has_side_effects=FLOWS))(sums, land, sems[0], sems[1], after)
    return res[1]


class _CopyThrough:
    def __init__(self, src_ref, dst_ref, dst_row0, n_rows, buf, sem_in, sem_out):
        rc = n_rows // COPY_PIECES
        piece = lambda ref, o: ref.at[pl.ds(o, rc), :]
        self.loads = [pltpu.make_async_copy(piece(src_ref, k * rc), piece(buf, k * rc), sem_in) for k in range(COPY_PIECES)]
        self.stores = [pltpu.make_async_copy(piece(buf, k * rc), piece(dst_ref, dst_row0 + k * rc), sem_out) for k in range(COPY_PIECES)]
        self.all_in = pltpu.make_async_copy(src_ref, buf, sem_in)
        self.all_out = pltpu.make_async_copy(buf, dst_ref.at[pl.ds(dst_row0, n_rows), :], sem_out)

    def load(self):
        for cp in self.loads:
            cp.start()

    def store(self):
        self.all_in.wait()
        for cp in self.stores:
            cp.start()

    def done(self):
        self.all_out.wait()


def _gather_first(shards, sm, jobs, x2, tgt2, t_rows, x0):
    d = shards[0].shape[1]
    rows = [w.shape[0] for w in shards]
    lo = [sum(rows[:p]) for p in range(N_BIG)]
    n_sems = GATHER_SEMS * len(jobs)
    seq = x2.shape[0]
    assert x0 == ROW_ALIGN and seq % ROW_ALIGN == 0 and d == N_DEV * LANE

    def body(s0, s1, s2, s3, s4, sm_ref, x_ref, tgt_ref, wl_ref, o0, o1, o2, o3, o4, sa_ref, h0_ref, tp_ref,
             wl_v, x_v, tgt_v, heads_v, sa_v, send_sems, recv_sems, ssend, srecv, local_sems, sems_in, sems_out):
        dests = (o0, o1, o2, o3, o4)
        x, y, c = _mesh_pos()
        me = (x, y, c)
        jme = _dev_index(*me)
        padded = [_CopyThrough(x_ref, h0_ref, x0, seq, x_v, sems_in.at[0], sems_out.at[0]),
                  _CopyThrough(tgt_ref, tp_ref, x0, seq, tgt_v, sems_in.at[1], sems_out.at[1])]
        for cp in padded:
            cp.load()
        shard_refs = (s0, s1, s2, s3, s4)
        first = sorted({j[0] for j in jobs})
        for p in first + [p for p in range(N_BIG) if p not in first]:
            wl_v[pl.ds(lo[p], rows[p]), :] = shard_refs[p][...].astype(BF16)
            if p == first[-1]:
                gather = _Gather(jobs, rows, lo, wl_v, dict(enumerate(dests)), send_sems, recv_sems)
                gather.start()
        peers = [(x, y, 1 - c)] + [(*chip, pc) for pc in (c, 1 - c) for chip in _other_chips(x, y)]
        smalls = [_remote(sm_ref, sa_ref.at[jme], ssend.at[k], srecv.at[k], to) for k, to in enumerate(peers)]
        for cp in smalls:
            cp.start()
        mine = [pltpu.make_async_copy(wl_v.at[pl.ds(lo[p], rows[p]), :],
                                      dests[p].at[pl.ds(pl.multiple_of(jme * rows[p], BF16_ROWS), rows[p]), :], local_sems.at[p])
                for p in range(N_BIG)]
        mine.append(pltpu.make_async_copy(wl_v, wl_ref, local_sems.at[N_BIG]))
        mine.append(pltpu.make_async_copy(sm_ref, sa_ref.at[jme], local_sems.at[N_BIG + 1]))
        for cp in mine:
            cp.start()
        later = [p for p in range(N_BIG) if p not in {j[0] for j in jobs}]
        own = [_SplitRemote(wl_v.at[pl.ds(lo[p], rows[p]), :],
                            dests[p].at[pl.ds(pl.multiple_of(jme * rows[p], BF16_ROWS), rows[p]), :],
                            ssend.at[7 + i], srecv.at[7 + i], (x, y, 1 - c), rows[p], D2D_CHUNKS) for i, p in enumerate(later)]
        for cp in own:
            cp.start()
        for cp in padded:
            cp.store()
        gather.mid()
        for cp in smalls + own:
            cp.wait_recv()
        mine[-1].wait()
        to_v = pltpu.make_async_copy(sa_ref, sa_v, local_sems.at[N_BIG + 1])
        to_v.start()
        to_v.wait()
        head, zeros = heads_v.at[0], heads_v.at[1]
        head[...] = jnp.zeros_like(head)
        zeros[...] = jnp.zeros_like(zeros)
        for j in range(N_DEV):
            head[pl.ds(x0 - N_META, N_META), pl.ds(j * LANE, LANE)] = sa_v[j, pl.ds(0, N_META), :]
        heads = [pltpu.make_async_copy(head, h0_ref.at[pl.ds(0, x0), :], local_sems.at[N_BIG + 1]),
                 pltpu.make_async_copy(zeros, tp_ref.at[pl.ds(0, x0), :], local_sems.at[N_BIG + 2])]
        for cp in heads:
            cp.start()
        gather.finish()
        for cp in smalls + own:
            cp.wait_send()
        for cp in mine[:-1] + heads:
            cp.wait()
        for cp in padded:
            cp.done()

    out_shape = [jax.ShapeDtypeStruct((sum(rows), d), BF16)]
    out_shape += [jax.ShapeDtypeStruct((N_DEV * r, d), BF16) for r in rows]
    out_shape.append(jax.ShapeDtypeStruct((N_DEV,) + sm.shape, F32))
    out_shape += [jax.ShapeDtypeStruct((t_rows, d), F32)] * 2
    res = pl.pallas_call(
        body, out_shape=out_shape, in_specs=[VMEM] * 6 + [ANY] * 2, out_specs=[ANY] * 9,
        scratch_shapes=[pltpu.VMEM((sum(rows), d), BF16), pltpu.VMEM((seq, d), F32), pltpu.VMEM((seq, d), F32),
                        pltpu.VMEM((2, ROW_ALIGN, d), F32), pltpu.VMEM((N_DEV,) + sm.shape, F32),
                        pltpu.SemaphoreType.DMA((n_sems,)), pltpu.SemaphoreType.DMA((n_sems,)),
                        pltpu.SemaphoreType.DMA((7 + N_BIG,)), pltpu.SemaphoreType.DMA((7 + N_BIG,)),
                        pltpu.SemaphoreType.DMA((N_BIG + 3,)), pltpu.SemaphoreType.DMA((2,)), pltpu.SemaphoreType.DMA((2,))],
        name="gather_first", compiler_params=_cparams())(*shards, sm, x2, tgt2)
    return res[0], list(res[1:1 + N_BIG]), res[1 + N_BIG], res[2 + N_BIG], res[3 + N_BIG]


def _in_proj(h0, g1, win_t, tm, comm):
    t_rows, d = h0.shape
    e = win_t.shape[0]

    def body(h_ref, g_ref, w_ref, xn_ref, hin_ref):
        h = h_ref[...]
        xn = ((h * _rstd(h)) * g_ref[...]).astype(BF16)
        xn_ref[...] = xn
        for o, n in _chunks(e, N_CHUNK):
            hin_ref[:, pl.ds(o, n)] = _dot_nt(xn, w_ref[pl.ds(o, n), :])

    return _host_call(
        body, grid=(t_rows // tm,),
        in_specs=[pl.BlockSpec((tm, d), lambda i: (i, 0)), _full((1, d)), _resident((e, d))],
        out_specs=[pl.BlockSpec((tm, d), lambda i: (i, 0)), pl.BlockSpec((tm, e), lambda i: (i, 0))],
        out_shape=[jax.ShapeDtypeStruct((t_rows, d), BF16), jax.ShapeDtypeStruct((t_rows, e), F32)],
        args=(h0, g1, win_t), name="in_proj", comm=comm)


def _tap_slot(off):
    return off % SUBLANE, (off // SUBLANE) * SUBLANE


def _fill_shifted(sh_ref, base_ref, residues, n_rows):
    for r in residues:
        if r:
            sh_ref[r] = base_ref[pl.ds(r, n_rows), :]


def _shifted_rows(pair, r, start, n):
    base_ref, sh_ref = pair
    return base_ref[pl.ds(start, n), :] if r == 0 else sh_ref[r, pl.ds(start, n), :]


def _mix_conv_fwd(hin, wa, wb, bb, wa_w, comm):
    t_rows = hin.shape[0]
    nt = wa_w // LANE
    ka, kb = wa.shape[0], wb.shape[0]
    nr = CONV_HALO + t_rows

    def body(bg_ref, cg_ref, ha_ref, val_ref, gt_ref, wa_ref, wb_ref, bb_ref, ya_ref, z_ref, base, sh):
        base[pl.ds(0, CONV_HALO), :] = jnp.zeros((CONV_HALO, LANE), F32)
        base[pl.ds(nr, SUBLANE), :] = jnp.zeros((SUBLANE, LANE), F32)

        def conv(w_ref, k_taps, b, n):
            acc = None
            for k in range(k_taps):
                r, q = _tap_slot(CONV_HALO - (k_taps - 1) + k)
                term = w_ref[pl.ds(k, 1), :] * _shifted_rows((base, sh), r, b + q, n)
                acc = term if acc is None else acc + term
            return acc

        def fill_a(b, c):
            base[pl.ds(CONV_HALO + b, CONV_CHUNK), :] = cg_ref[pl.ds(b, CONV_CHUNK), :] * ha_ref[pl.ds(b, CONV_CHUNK), :]
            return c

        _row_loop(t_rows, CONV_CHUNK, fill_a)
        _fill_shifted(sh, base, sorted({_tap_slot(CONV_HALO - (ka - 1) + k)[0] for k in range(ka)}), nr)

        def out_a(b, c):
            ya_ref[pl.ds(b, CONV_CHUNK), :] = (bg_ref[pl.ds(b, CONV_CHUNK), :] * conv(wa_ref, ka, b, CONV_CHUNK)).astype(BF16)
            return c

        _row_loop(t_rows, CONV_CHUNK, out_a)

        def fill_b(b, c):
            base[pl.ds(CONV_HALO + b, CONV_CHUNK), :] = (val_ref[pl.ds(b, CONV_CHUNK), :]
                                                          * jax.nn.sigmoid(gt_ref[pl.ds(b, CONV_CHUNK), :]))
            return c

        _row_loop(t_rows, CONV_CHUNK, fill_b)
        _fill_shifted(sh, base, range(SUBLANE), nr)

        def out_b(b, c):
            z_ref[pl.ds(b, CONV_CHUNK), :] = conv(wb_ref, kb, b, CONV_CHUNK) + bb_ref[...]
            return c

        _row_loop(t_rows, CONV_CHUNK, out_b)

    def col(g):
        return pl.BlockSpec((t_rows, LANE), lambda i, g=g: (0, g * nt + i))

    tile = lambda rows: pl.BlockSpec((rows, LANE), lambda i: (0, i))
    return _host_call(
        body, grid=(nt,),
        in_specs=[col(0), col(1), col(2), col(3), col(4), tile(ka), tile(kb), tile(1)],
        out_specs=[tile(t_rows), tile(t_rows)],
        out_shape=[jax.ShapeDtypeStruct((t_rows, wa_w), BF16), jax.ShapeDtypeStruct((t_rows, wa_w), F32)],
        scratch_shapes=[pltpu.VMEM((nr + SUBLANE, LANE), F32), pltpu.VMEM((SUBLANE, nr, LANE), F32)],
        args=(hin, hin, hin, hin, hin, wa, wb, bb), name="mix_conv_fwd", comm=comm)


def _ln_parts(z, lg, lb):
    mu = jnp.mean(z, axis=-1, keepdims=True)
    zc = z - mu
    rstd = lax.rsqrt(jnp.mean(zc * zc, axis=-1, keepdims=True) + LN_EPS)
    zh = zc * rstd
    return zh, rstd, zh * lg + lb


def _mix_ln_fwd(ya, z, lg, lb, tm, comm):
    t_rows, w = z.shape

    def body(ya_ref, z_ref, lg_ref, lb_ref, y_ref):
        _, _, ln = _ln_parts(z_ref[...], lg_ref[...], lb_ref[...])
        y_ref[:, pl.ds(0, w)] = ya_ref[...]
        y_ref[:, pl.ds(w, w)] = (ln * jax.nn.sigmoid(ln)).astype(BF16)

    blk = pl.BlockSpec((tm, w), lambda i: (i, 0))
    res, extra = _host_call(body, grid=(t_rows // tm,), in_specs=[blk, blk, _full((1, w)), _full((1, w))],
                            out_specs=[pl.BlockSpec((tm, 2 * w), lambda i: (i, 0))],
                            out_shape=[jax.ShapeDtypeStruct((t_rows, 2 * w), BF16)], args=(ya, z, lg, lb), name="mix_ln_fwd", comm=comm)
    return res[0], extra


def _out_proj(y, w_out, h0, g2, g3, tm, comm):
    t_rows, d = h0.shape

    def body(y_ref, w_ref, h0_ref, g2_ref, g3_ref, mix_ref, h1_ref, xn2_ref):
        mix = _dot_nn(y_ref[...], w_ref[...])
        mix_ref[...] = mix
        h1 = h0_ref[...] + (mix * _rstd(mix)) * g2_ref[...]
        h1_ref[...] = h1
        xn2_ref[...] = ((h1 * _rstd(h1)) * g3_ref[...]).astype(BF16)

    blk = pl.BlockSpec((tm, d), lambda i: (i, 0))
    return _host_call(
        body, grid=(t_rows // tm,), in_specs=[blk, _resident(w_out.shape), blk, _full((1, d)), _full((1, d))],
        out_specs=[blk, blk, blk],
        out_shape=[jax.ShapeDtypeStruct((t_rows, d), F32), jax.ShapeDtypeStruct((t_rows, d), F32),
                   jax.ShapeDtypeStruct((t_rows, d), BF16)],
        args=(y, w_out, h0, g2, g3), name="out_proj", comm=comm)


def _gate_up(xn2, wg_t, wu_t, tm, comm):
    t_rows, d = xn2.shape
    f = wg_t.shape[0]

    def body(x_ref, wg_ref, wu_ref, a_ref, u_ref, s_ref):
        xn = x_ref[...]
        for o, n in _chunks(f, N_CHUNK):
            a = _dot_nt(xn, wg_ref[pl.ds(o, n), :])
            u = _dot_nt(xn, wu_ref[pl.ds(o, n), :])
            a_ref[:, pl.ds(o, n)] = a.astype(BF16)
            u_ref[:, pl.ds(o, n)] = u.astype(BF16)
            s_ref[:, pl.ds(o, n)] = ((a * jax.nn.sigmoid(a)) * u).astype(BF16)

    blk = pl.BlockSpec((tm, f), lambda i: (i, 0))
    return _host_call(
        body, grid=(t_rows // tm,),
        in_specs=[pl.BlockSpec((tm, d), lambda i: (i, 0)), _resident((f, d)), _resident((f, d))],
        out_specs=[blk, blk, blk], out_shape=[jax.ShapeDtypeStruct((t_rows, f), BF16)] * 3,
        args=(xn2, wg_t, wu_t), name="gate_up", comm=comm)


def _down_loss(s, wd, h1, tgt, g4, tm, x0):
    t_rows, d = h1.shape
    f = wd.shape[0]

    def body(s_ref, w_ref, h1_ref, tgt_ref, g4_ref, dh2_ref, dff_ref, dg4_ref, loss_ref):
        i = pl.program_id(0)
        ff = _dot_nn(s_ref[...], w_ref[...])
        r4 = _rstd(ff)
        fh = ff * r4
        g4 = g4_ref[...]
        h2 = h1_ref[...] + fh * g4
        row = i * tm + lax.broadcasted_iota(jnp.int32, (tm, 1), 0)
        diff = jnp.where(row >= x0, h2 - tgt_ref[...], 0.0)
        dh2 = diff / d
        dh2_ref[...] = dh2
        dff_ref[...] = _rms_bwd(dh2 * g4, fh, r4).astype(BF16)
        _acc_rows(dg4_ref, dh2 * fh, i == 0)
        _acc_rows(loss_ref, diff * diff, i == 0)

    blk = pl.BlockSpec((tm, d), lambda i: (i, 0))
    res, _ = _host_call(
        body, grid=(t_rows // tm,),
        in_specs=[pl.BlockSpec((tm, f), lambda i: (i, 0)), _resident((f, d)), blk, blk, _full((1, d))],
        out_specs=[blk, blk, _full((1, d)), _full((1, d))],
        out_shape=[jax.ShapeDtypeStruct((t_rows, d), F32), jax.ShapeDtypeStruct((t_rows, d), BF16),
                   jax.ShapeDtypeStruct((1, d), F32), jax.ShapeDtypeStruct((1, d), F32)],
        args=(s, wd, h1, tgt, g4), name="down_loss")
    return res


def _bwd_down(dff, wd, a, u, tm, comm):
    t_rows, d = dff.shape
    f = wd.shape[0]

    def body(dff_ref, w_ref, a_ref, u_ref, da_ref, du_ref):
        dff_v = dff_ref[...]
        for o, n in _chunks(f, N_CHUNK):
            ds = _dot_nt(dff_v, w_ref[pl.ds(o, n), :]).astype(BF16)
            av = a_ref[:, pl.ds(o, n)]
            uv = u_ref[:, pl.ds(o, n)]
            sig = jax.nn.sigmoid(av)
            da_ref[:, pl.ds(o, n)] = ds * uv * _silu_grad(av, sig)
            du_ref[:, pl.ds(o, n)] = ds * (av * sig)

    blk = pl.BlockSpec((tm, f), lambda i: (i, 0))
    return _host_call(
        body, grid=(t_rows // tm,),
        in_specs=[pl.BlockSpec((tm, d), lambda i: (i, 0)), _resident((f, d)), blk, blk],
        out_specs=[blk, blk], out_shape=[jax.ShapeDtypeStruct((t_rows, f), BF16)] * 2,
        args=(dff, wd, a, u), name="bwd_down", comm=comm)


def _wgrad(a, b, name, after=()):
    d = b.shape[1]
    t_rows = b.shape[0]
    stacked = a.ndim == 3
    n = a.shape[-1]
    groups = a.shape[0] if stacked else 1
    steps = 1 if stacked else 2
    tile = max(t for t in range(LANE, min(n // steps, WGRAD_TILE_MAX) + 1, LANE) if n % t == 0)
    tiles = n // tile

    def body(a_ref, b_ref, o_ref):
        o_ref[...] = lax.dot_general(a_ref[...], b_ref[...], (((0,), (0,)), ((), ())),
                                     preferred_element_type=F32).astype(BF16)

    if stacked:
        a_spec = pl.BlockSpec((None, t_rows, tile), lambda g, i: (g, 0, i))
    else:
        a_spec = pl.BlockSpec((t_rows, tile), lambda g, i: (0, i))
    res, _ = _host_call(
        body, grid=(groups, tiles), in_specs=[a_spec, _resident((t_rows, d))],
        out_specs=[pl.BlockSpec((tile, d), lambda g, i: (g * tiles + i, 0))],
        out_shape=[jax.ShapeDtypeStruct((groups * n, d), BF16)], args=(a, b), name=name, after=after)
    return res[0]


def _bwd_ffn_in(da, du, wg_t, wu_t, h1, dh2, g3, tm, comm):
    t_rows, d = h1.shape
    f = wg_t.shape[0]

    def body(da_ref, du_ref, wg_ref, wu_ref, h1_ref, dh2_ref, g3_ref, dh1_ref, dg3_ref):
        dxn2 = _dot_nn(da_ref[...], wg_ref[...]) + _dot_nn(du_ref[...], wu_ref[...])
        h1 = h1_ref[...]
        r3 = _rstd(h1)
        h1h = h1 * r3
        _acc_rows(dg3_ref, dxn2 * h1h, pl.program_id(0) == 0)
        dh1_ref[...] = dh2_ref[...] + _rms_bwd(dxn2 * g3_ref[...], h1h, r3)

    blk = pl.BlockSpec((tm, d), lambda i: (i, 0))
    blkf = pl.BlockSpec((tm, f), lambda i: (i, 0))
    return _host_call(
        body, grid=(t_rows // tm,),
        in_specs=[blkf, blkf, _resident((f, d)), _resident((f, d)), blk, blk, _full((1, d))],
        out_specs=[blk, _full((1, d))],
        out_shape=[jax.ShapeDtypeStruct((t_rows, d), F32), jax.ShapeDtypeStruct((1, d), F32)],
        args=(da, du, wg_t, wu_t, h1, dh2, g3), name="bwd_ffn_in", comm=comm)


def _bwd_out_proj(dh1, mix, w_out, g2, tm, after):
    t_rows, d = dh1.shape

    def body(dh1_ref, mix_ref, w_ref, g2_ref, dmix_ref, dy_ref, dg2_ref):
        mix = mix_ref[...]
        r2 = _rstd(mix)
        mh = mix * r2
        dh1 = dh1_ref[...]
        _acc_rows(dg2_ref, dh1 * mh, pl.program_id(0) == 0)
        dmix = _rms_bwd(dh1 * g2_ref[...], mh, r2).astype(BF16)
        dmix_ref[...] = dmix
        dy_ref[...] = _dot_nt(dmix, w_ref[...])

    blk = pl.BlockSpec((tm, d), lambda i: (i, 0))
    res, _ = _host_call(
        body, grid=(t_rows // tm,), in_specs=[blk, blk, _resident(w_out.shape), _full((1, d))],
        out_specs=[blk, blk, _full((1, d))],
        out_shape=[jax.ShapeDtypeStruct((t_rows, d), BF16), jax.ShapeDtypeStruct((t_rows, d), F32),
                   jax.ShapeDtypeStruct((1, d), F32)],
        args=(dh1, mix, w_out, g2), name="bwd_out_proj", after=after)
    return res


def _mix_ln_bwd(z, dy, lg, lb, tm):
    t_rows, w = z.shape

    def body(z_ref, dyb_ref, lg_ref, lb_ref, dz_ref, dlg_ref, dlb_ref, dbb_ref):
        first = pl.program_id(0) == 0
        lg = lg_ref[...]
        zh, rstd, ln = _ln_parts(z_ref[...], lg, lb_ref[...])
        dln = dyb_ref[...] * _silu_grad(ln, jax.nn.sigmoid(ln))
        _acc_rows(dlg_ref, dln * zh, first)
        _acc_rows(dlb_ref, dln, first)
        dzh = dln * lg
        dz = rstd * (dzh - jnp.mean(dzh, axis=-1, keepdims=True) - zh * jnp.mean(dzh * zh, axis=-1, keepdims=True))
        dz_ref[...] = dz
        _acc_rows(dbb_ref, dz, first)

    blk = pl.BlockSpec((tm, w), lambda i: (i, 0))
    vec = _full((1, w))
    res, _ = _host_call(
        body, grid=(t_rows // tm,), in_specs=[blk, pl.BlockSpec((tm, w), lambda i: (i, 1)), vec, vec],
        out_specs=[blk, vec, vec, vec],
        out_shape=[jax.ShapeDtypeStruct((t_rows, w), F32)] + [jax.ShapeDtypeStruct((1, w), F32)] * 3,
        args=(z, dy, lg, lb), name="mix_ln_bwd")
    return res


def _mix_conv_bwd(hin, dy, dz, wa, wb, wa_w, comm):
    t_rows = hin.shape[0]
    nt = wa_w // LANE
    ka, kb = wa.shape[0], wb.shape[0]
    nr = CONV_HALO + t_rows
    kb_rows = -(-kb // SUBLANE) * SUBLANE

    def body(bg_ref, cg_ref, ha_ref, val_ref, gt_ref, dya_ref, dz_ref, wa_ref, wb_ref,
             dh_ref, dwa_ref, dwb_ref, base, sh, based, shd, tmp, wbc):
        zeros = lambda n: jnp.zeros((n, LANE), F32)
        base[pl.ds(0, CONV_HALO), :] = zeros(CONV_HALO)
        base[pl.ds(nr, SUBLANE), :] = zeros(SUBLANE)
        based[pl.ds(t_rows, CONV_HALO + SUBLANE), :] = zeros(CONV_HALO + SUBLANE)

        def fwd_slot(k_taps, k):
            return _tap_slot(CONV_HALO - (k_taps - 1) + k)

        def bwd_slot(k_taps, k):
            return _tap_slot(k_taps - 1 - k)

        def conv(w_ref, k_taps, src, slot, b, n):
            acc = None
            for k in range(k_taps):
                r, q = slot(k_taps, k)
                term = w_ref[pl.ds(k, 1), :] * _shifted_rows(src, r, b + q, n)
                acc = term if acc is None else acc + term
            return acc

        def by_residue(k_taps, slot):
            groups = {}
            for k in range(k_taps):
                r, q = slot(k_taps, k)
                groups.setdefault(r, []).append((k, q // SUBLANE))
            return groups

        def wgrad_loop(w_ref, k_taps):
            n_sub = WGRAD_ROWS // SUBLANE
            for k in range(k_taps):
                wbc[k] = jnp.broadcast_to(w_ref[pl.ds(k, 1), :], (SUBLANE, LANE))
            fwd, bwd = by_residue(k_taps, fwd_slot), by_residue(k_taps, bwd_slot)

            def window(src, r, taps, b):
                span = n_sub + max(qi for _, qi in taps)
                return [_shifted_rows(src, r, b + SUBLANE * i, SUBLANE) for i in range(span)]

            def step(b, accs):
                accs = list(accs)
                dv = [based[pl.ds(b + SUBLANE * j, SUBLANE), :] for j in range(n_sub)]
                for r, taps in fwd.items():
                    win = window((base, sh), r, taps, b)
                    for k, qi in taps:
                        t = dv[0] * win[qi]
                        for j in range(1, n_sub):
                            t = t + dv[j] * win[qi + j]
                        accs[k] = accs[k] + t
                outs = [None] * n_sub
                for r, taps in bwd.items():
                    win = window((based, shd), r, taps, b)
                    for k, qi in taps:
                        wk = wbc[k]
                        for j in range(n_sub):
                            term = wk * win[qi + j]
                            outs[j] = term if outs[j] is None else outs[j] + term
                for j in range(n_sub):
                    tmp[pl.ds(b + SUBLANE * j, SUBLANE), :] = outs[j]
                return tuple(accs)

            return _row_loop(t_rows, WGRAD_ROWS, step, tuple(zeros(SUBLANE) for _ in range(k_taps)))

        def store_taps(ref, accs, rows):
            for k, acc in enumerate(accs):
                ref[pl.ds(k, 1), :] = jnp.sum(acc, axis=0, keepdims=True)
            if rows > len(accs):
                ref[pl.ds(len(accs), rows - len(accs)), :] = zeros(rows - len(accs))

        def fill_a(b, c):
            sl = pl.ds(b, CONV_CHUNK)
            base[pl.ds(CONV_HALO + b, CONV_CHUNK), :] = cg_ref[sl, :] * ha_ref[sl, :]
            based[sl, :] = dya_ref[sl, :] * bg_ref[sl, :]
            return c

        _row_loop(t_rows, CONV_CHUNK, fill_a)
        _fill_shifted(sh, base, sorted({fwd_slot(ka, k)[0] for k in range(ka)}), nr)
        _fill_shifted(shd, based, sorted({bwd_slot(ka, k)[0] for k in range(ka)}), nr)

        def d_bgate(b, c):
            sl = pl.ds(b, CONV_CHUNK)
            dh_ref[0, sl, :] = (dya_ref[sl, :] * conv(wa_ref, ka, (base, sh), fwd_slot, b, CONV_CHUNK)).astype(BF16)
            return c

        _row_loop(t_rows, CONV_CHUNK, d_bgate)
        store_taps(dwa_ref, wgrad_loop(wa_ref, ka), SUBLANE)

        def d_ch(b, c):
            sl = pl.ds(b, CONV_CHUNK)
            dua = tmp[sl, :]
            dh_ref[1, sl, :] = (dua * ha_ref[sl, :]).astype(BF16)
            dh_ref[2, sl, :] = (dua * cg_ref[sl, :]).astype(BF16)
            return c

        _row_loop(t_rows, CONV_CHUNK, d_ch)

        def fill_b(b, c):
            sl = pl.ds(b, CONV_CHUNK)
            base[pl.ds(CONV_HALO + b, CONV_CHUNK), :] = val_ref[sl, :] * jax.nn.sigmoid(gt_ref[sl, :])
            based[sl, :] = dz_ref[sl, :]
            return c

        _row_loop(t_rows, CONV_CHUNK, fill_b)
        _fill_shifted(sh, base, range(SUBLANE), nr)
        _fill_shifted(shd, based, range(SUBLANE), nr)
        store_taps(dwb_ref, wgrad_loop(wb_ref, kb), kb_rows)

        def d_glu(b, c):
            sl = pl.ds(b, CONV_CHUNK)
            dgg = tmp[sl, :]
            sig = jax.nn.sigmoid(gt_ref[sl, :])
            dh_ref[3, sl, :] = (dgg * sig).astype(BF16)
            dh_ref[4, sl, :] = (dgg * val_ref[sl, :] * (sig * (1.0 - sig))).astype(BF16)
            return c

        _row_loop(t_rows, CONV_CHUNK, d_glu)

    def col(g):
        return pl.BlockSpec((t_rows, LANE), lambda i, g=g: (0, g * nt + i))

    tile = lambda rows: pl.BlockSpec((rows, LANE), lambda i: (0, i))
    return _host_call(
        body, grid=(nt,),
        in_specs=[col(0), col(1), col(2), col(3), col(4), tile(t_rows), tile(t_rows), tile(ka), tile(kb)],
        out_specs=[pl.BlockSpec((5, t_rows, LANE), lambda i: (0, 0, i)), tile(SUBLANE), tile(kb_rows)],
        out_shape=[jax.ShapeDtypeStruct((5, t_rows, wa_w), BF16), jax.ShapeDtypeStruct((SUBLANE, wa_w), F32),
                   jax.ShapeDtypeStruct((kb_rows, wa_w), F32)],
        scratch_shapes=[pltpu.VMEM((nr + SUBLANE, LANE), F32), pltpu.VMEM((SUBLANE, nr, LANE), F32),
                        pltpu.VMEM((nr + SUBLANE, LANE), F32), pltpu.VMEM((SUBLANE, nr, LANE), F32),
                        pltpu.VMEM((t_rows, LANE), F32), pltpu.VMEM((kb_rows, SUBLANE, LANE), F32)],
        args=(hin, hin, hin, hin, hin, dy, dz, wa, wb), name="mix_conv_bwd", comm=comm)


def _bwd_in_proj(dh5, win_t, h0, dh1, g1, tm, comm):
    t_rows, d = h0.shape
    groups, _, w = dh5.shape

    def body(dh_ref, w_ref, h0_ref, dh1_ref, g1_ref, dh0_ref, dg1_ref):
        dxn1 = None
        for g in range(groups):
            part = _dot_nn(dh_ref[g], w_ref[pl.ds(g * w, w), :])
            dxn1 = part if dxn1 is None else dxn1 + part
        h0 = h0_ref[...]
        r1 = _rstd(h0)
        h0h = h0 * r1
        _acc_rows(dg1_ref, dxn1 * h0h, pl.program_id(0) == 0)
        dh0_ref[...] = dh1_ref[...] + _rms_bwd(dxn1 * g1_ref[...], h0h, r1)

    blk = pl.BlockSpec((tm, d), lambda i: (i, 0))
    return _host_call(
        body, grid=(t_rows // tm,),
        in_specs=[pl.BlockSpec((groups, tm, w), lambda i: (0, i, 0)), _resident(win_t.shape), blk, blk, _full((1, d))],
        out_specs=[blk, _full((1, d))],
        out_shape=[jax.ShapeDtypeStruct((t_rows, d), F32), jax.ShapeDtypeStruct((1, d), F32)],
        args=(dh5, win_t, h0, dh1, g1), name="bwd_in_proj", comm=comm)


def _reduce_small(smalls, d, after):
    (dmeta, dg1, dg2, dg3, dg4, dbb, dlg, dlb, lossv, dwa, dwb) = smalls
    half = d // 2
    kb_rows = dwb.shape[0]

    def body(dmeta_ref, dg1_ref, dg2_ref, dg3_ref, dg4_ref, dbb_ref, dlg_ref, dlb_ref, loss_ref, dwa_ref, dwb_ref,
             ptot_ref, pbuf, psib, chip_p, ps_send, ps_recv, pc_send, pc_recv):
        x, y, c = _mesh_pos()
        pbuf[...] = jnp.zeros_like(pbuf)
        pbuf[pl.ds(0, N_META), :] = dmeta_ref[...]
        for row, ref in ((16, dg1_ref), (17, dg2_ref), (18, dg3_ref), (19, dg4_ref)):
            pbuf[pl.ds(row, 1), :] = ref[...]
        pbuf[pl.ds(20, 1), pl.ds(0, half)] = dbb_ref[...]
        pbuf[pl.ds(20, 1), pl.ds(half, half)] = dlg_ref[...]
        pbuf[pl.ds(21, 1), pl.ds(0, half)] = dlb_ref[...]
        lv = loss_ref[...]
        pbuf[pl.ds(21, 1), pl.ds(half, half)] = lv[:, :half] + lv[:, half:]
        pbuf[pl.ds(24, SUBLANE), pl.ds(0, half)] = dwa_ref[...]
        pbuf[pl.ds(32, kb_rows), pl.ds(0, half)] = dwb_ref[...]
        to_sib = _remote(pbuf, psib, ps_send.at[0], ps_recv.at[0], (x, y, 1 - c))
        to_sib.start()
        to_sib.wait_recv()
        my_chip = 2 * x + y
        chip_p[my_chip] = pbuf[...] + psib[...]
        to_sib.wait_send()
        slot = chip_p.at[my_chip]
        cps = [_remote(slot, slot, pc_send.at[k], pc_recv.at[k], (*chip, c)) for k, chip in enumerate(_other_chips(x, y))]
        for cp in cps:
            cp.start()
        for cp in cps:
            cp.wait_recv()
        ptot_ref[...] = ((chip_p[0] + chip_p[1]) + chip_p[2]) + chip_p[3]
        for cp in cps:
            cp.wait_send()

    return _host_call(
        body, grid=(), in_specs=[VMEM] * 11, out_specs=[VMEM], out_shape=[jax.ShapeDtypeStruct((SMALL_ROWS, d), F32)],
        scratch_shapes=[pltpu.VMEM((SMALL_ROWS, d), F32), pltpu.VMEM((SMALL_ROWS, d), F32), pltpu.VMEM((4, SMALL_ROWS, d), F32),
                        pltpu.SemaphoreType.DMA((1,)), pltpu.SemaphoreType.DMA((1,)),
                        pltpu.SemaphoreType.DMA((3,)), pltpu.SemaphoreType.DMA((3,))],
        args=smalls, name="reduce_small", after=after)


def _adamw(w, g, m, v):
    m = ADAM_B1 * m + (1.0 - ADAM_B1) * g
    v = ADAM_B2 * v + (1.0 - ADAM_B2) * jnp.square(g)
    m_hat = m / (1.0 - ADAM_B1 ** ADAM_STEP)
    v_hat = v / (1.0 - ADAM_B2 ** ADAM_STEP)
    delta = -ADAM_LR * (m_hat / (jnp.sqrt(v_hat) + ADAM_EPS) + ADAM_WD * w)
    return delta, m, v


def _adam_big(g, pair, part, w, m, v, name):
    r, d = w.shape
    cols = d // ADAM_COL_BLOCKS

    def body(me_ref, g_ref, pair_ref, part_ref, w_ref, m_ref, v_ref, go_ref, d_ref, mo_ref, vo_ref):
        g = g_ref[...].astype(F32) + pair_ref[...].astype(F32)
        for k in range(3):
            g = g + part_ref[k].astype(F32)
        go_ref[...] = g
        d_ref[...], mo_ref[...], vo_ref[...] = _adamw(w_ref[...], g, m_ref[...], v_ref[...])

    blk = pl.BlockSpec((r, cols), lambda i, me_ref: (0, i))
    grid_spec = pltpu.PrefetchScalarGridSpec(
        num_scalar_prefetch=1, grid=(ADAM_COL_BLOCKS,),
        in_specs=[pl.BlockSpec((r, cols), lambda i, me_ref: (me_ref[0], i)),
                  pl.BlockSpec((None, r, cols), lambda i, me_ref: (0, 0, i)),
                  pl.BlockSpec((3, r, cols), lambda i, me_ref: (0, 0, i)), blk, blk, blk],
        out_specs=[blk, blk, blk, blk])
    me = jnp.reshape(_dev_index(*_mesh_pos()), (1,)).astype(jnp.int32)
    return pl.pallas_call(body, out_shape=[jax.ShapeDtypeStruct((r, d), F32)] * 4, grid_spec=grid_spec, name=name,
                          compiler_params=_cparams(1))(me, g, pair, part, w, m, v)


def _adam_small(gs, ws, ms, vs):
    n = len(gs)

    def body(*refs):
        ins, outs = refs[:4 * n], refs[4 * n:]
        for i in range(n):
            g = ins[i][...]
            delta, m, v = _adamw(ins[n + i][...], g, ins[2 * n + i][...], ins[3 * n + i][...])
            outs[i][...] = delta
            outs[n + i][...] = m
            outs[2 * n + i][...] = v

    shapes = [jax.ShapeDtypeStruct(w.shape, F32) for w in ws]
    return pl.pallas_call(body, out_shape=shapes * 3, name="adam_small", compiler_params=_cparams())(*gs, *ws, *ms, *vs)


def kernel(x, meta_tokens, pre_mix_norm, w_in, conv_a_w, conv_b_w, conv_b_bias, ln_b_gain, ln_b_bias, w_out, post_mix_norm, pre_ffn_norm, w_gate, w_up, w_down, post_ffn_norm, loss_target, m_meta_tokens, m_pre_mix_norm, m_w_in, m_conv_a_w, m_conv_b_w, m_conv_b_bias, m_ln_b_gain, m_ln_b_bias, m_w_out, m_post_mix_norm, m_pre_ffn_norm, m_w_gate, m_w_up, m_w_down, m_post_ffn_norm, v_meta_tokens, v_pre_mix_norm, v_w_in, v_conv_a_w, v_conv_b_w, v_conv_b_bias, v_ln_b_gain, v_ln_b_bias, v_w_out, v_post_mix_norm, v_pre_ffn_norm, v_w_gate, v_w_up, v_w_down, v_post_ffn_norm):
    _, seq, d = x.shape
    ka, ca_loc = conv_a_w.shape[1:]
    kb, cb_loc = conv_b_w.shape[1:]
    wa_w = ca_loc * N_DEV
    assert cb_loc == ca_loc and wa_w % LANE == 0 and w_in.shape[2] * N_DEV == 5 * wa_w
    pad = (-(N_META + seq)) % ROW_ALIGN
    x0 = pad + N_META
    t_rows = x0 + seq
    assert t_rows % (N_ROW_BLOCKS * BF16_ROWS) == 0 and t_rows % CONV_CHUNK == 0 and d % LANE == 0
    tm = t_rows // N_ROW_BLOCKS
    me = _dev_index(*_mesh_pos())

    def as_rows(w_in_like, w_out_like, w_gate_like, w_up_like, w_down_like):
        return (w_in_like[0].T, w_out_like[0], w_gate_like[0].T, w_up_like[0].T, w_down_like[0])

    w_loc = as_rows(w_in, w_out, w_gate, w_up, w_down)
    rows = [w.shape[0] for w in w_loc]
    assert all(r % ADD_CHUNK == 0 for r in rows)
    P_IN, P_OUT, P_GATE, P_UP, P_DOWN = range(N_BIG)

    sm = jnp.zeros((SM_ROWS, LANE), F32)
    sm = sm.at[0:N_META, :].set(meta_tokens)
    sm = sm.at[16:16 + ka, 0:ca_loc].set(conv_a_w[0])
    sm = sm.at[24:24 + kb, 0:cb_loc].set(conv_b_w[0])
    wl, wfull, sm_all, h0, tgt = _gather_first(w_loc, sm, [(P_IN, 0, rows[P_IN])], x[0], loss_target[0], t_rows, x0)
    wa =jnp.transpose(sm_all[:, 16:16 + ka, 0:ca_loc], (1, 0, 2)).reshape(ka, wa_w)
    wb = jnp.transpose(sm_all[:, 24:24 + kb, 0:cb_loc], (1, 0, 2)).reshape(kb, wa_w)

    later = (P_OUT, P_GATE, P_UP, P_DOWN)
    sems, wl, started, _ = _gather_start(wl, wfull, later, rows, START_BARRIER_IDS[0])
    for p, arr in zip(later, started):
        wfull[p] = arr

    def arrived(p, after, name):
        nonlocal wl
        wl, wfull[p] = _gather_wait(wl, wfull[p], sems[later.index(p)], after, rows[p], name)
        return _forward_comm(wfull[p], rows[p])

    (xn1, hin), _ = _in_proj(h0, pre_mix_norm, wfull[P_IN], tm, None)
    (ya, z), (wfull[P_OUT],) = _mix_conv_fwd(hin, wa, wb, conv_b_bias, wa_w, arrived(P_OUT, hin, "gather_wait_out"))
    y, (wfull[P_GATE],) = _mix_ln_fwd(ya, z, ln_b_gain, ln_b_bias, tm, arrived(P_GATE, z, "gather_wait_gate"))
    (mix, h1, xn2), _ = _out_proj(y, wfull[P_OUT], h0, post_mix_norm, pre_ffn_norm, tm, None)
    arrived(P_UP, xn2, "gather_wait_up")
    wfull[P_UP] = _forward_now(wfull[P_UP], rows[P_UP], "forward_up")
    (a, u, s), _ = _gate_up(xn2, wfull[P_GATE], wfull[P_UP], tm, None)
    arrived(P_DOWN, s, "gather_wait_down")
    wfull[P_DOWN] = _forward_now(wfull[P_DOWN], rows[P_DOWN], "forward_down")
    dh2, dff, dg4, lossv = _down_loss(s, wfull[P_DOWN], h1, tgt, post_ffn_norm, tm, x0)

    gwd = _wgrad(s, dff, "wgrad_down")
    (da, du), (pair_d,) = _bwd_down(dff, wfull[P_DOWN], a, u, tm, _pair_comm(gwd, rows[P_DOWN]))
    (flight_d,), token = _chip_start([_pair_sum(gwd, pair_d, rows[P_DOWN], "pair_sum_down")], "chip_start_down", START_BARRIER_IDS[1])
    gwg = _wgrad(da, xn2, "wgrad_gate", [token])
    gwu = _wgrad(du, xn2, "wgrad_up")
    (dh1, dg3), (pair_g, pair_u) = _bwd_ffn_in(da, du, wfull[P_GATE], wfull[P_UP], h1, dh2, pre_ffn_norm, tm,
                                               _merge_comms([_pair_comm(gwg, rows[P_GATE]), _pair_comm(gwu, rows[P_UP])]))
    (flight_g, flight_u), token = _chip_start([_pair_sum(gwg, pair_g, rows[P_GATE], "pair_sum_gate"),
                                               _pair_sum(gwu, pair_u, rows[P_UP], "pair_sum_up")], "chip_start_gate_up",
                                              START_BARRIER_IDS[2])
    dmix, dy, dg2 = _bwd_out_proj(dh1, mix, wfull[P_OUT], post_mix_norm, tm, [token])
    gwo = _wgrad(y, dmix, "wgrad_out")
    dz, dlg, dlb, dbb = _mix_ln_bwd(z, dy, ln_b_gain, ln_b_bias, tm)
    (dh5, dwa, dwb), (pair_o,) = _mix_conv_bwd(hin, dy, dz, wa, wb, wa_w, _pair_comm(gwo, rows[P_OUT]))
    (flight_o,), token = _chip_start([_pair_sum(gwo, pair_o, rows[P_OUT], "pair_sum_out")], "chip_start_out", START_BARRIER_IDS[3])
    gwi = _wgrad(dh5, xn1, "wgrad_in", [token])
    (dh0, dg1), (pair_i,) = _bwd_in_proj(dh5, wfull[P_IN], h0, dh1, pre_mix_norm, tm, _pair_comm(gwi, rows[P_IN]))
    grad_x = dh0[x0:][None]
    dmeta = dh0[x0 - N_META:x0]
    (ptot,), _ = _reduce_small((dmeta, dg1, dg2, dg3, dg4, dbb, dlg, dlb, lossv, dwa, dwb), d, [])
    (flight_i,), token = _chip_start([_pair_sum(gwi, pair_i, rows[P_IN], "pair_sum_in", [ptot])], "chip_start_in",
                                     START_BARRIER_IDS[4])

    def landed(flight, after, tag):
        sems_p, sums, land = flight
        return _chip_wait(sums, land, sems_p, after, "chip_wait_" + tag)

    part_d = landed(flight_d, token, "down")
    part_g = landed(flight_g, token, "gate")
    part_u = landed(flight_u, token, "up")
    part_o = landed(flight_o, token, "out")

    half = d // 2
    loss = (0.5 / d) * jnp.sum(ptot[21, half:])
    g_meta = lax.dynamic_slice(ptot, (0, me * (d // N_DEV)), (N_META, d // N_DEV))
    g_small = [g_meta, ptot[16:17], lax.dynamic_slice(ptot, (24, me * ca_loc), (ka, ca_loc))[None],
               lax.dynamic_slice(ptot, (32, me * cb_loc), (kb, cb_loc))[None],
               ptot[20:21, :half], ptot[20:21, half:], ptot[21:22, :half], ptot[17:18], ptot[18:19], ptot[19:20]]
    w_small = [meta_tokens, pre_mix_norm, conv_a_w, conv_b_w, conv_b_bias, ln_b_gain, ln_b_bias, post_mix_norm,
               pre_ffn_norm, post_ffn_norm]
    m_small = [m_meta_tokens, m_pre_mix_norm, m_conv_a_w, m_conv_b_w, m_conv_b_bias, m_ln_b_gain, m_ln_b_bias,
               m_post_mix_norm, m_pre_ffn_norm, m_post_ffn_norm]
    v_small = [v_meta_tokens, v_pre_mix_norm, v_conv_a_w, v_conv_b_w, v_conv_b_bias, v_ln_b_gain, v_ln_b_bias,
               v_post_mix_norm, v_pre_ffn_norm, v_post_ffn_norm]
    small = _adam_small(g_small, w_small, m_small, v_small)
    n_small = len(w_small)
    d_small, nm_small, nv_small = small[:n_small], small[n_small:2 * n_small], small[2 * n_small:]

    m_loc = as_rows(m_w_in, m_w_out, m_w_gate, m_w_up, m_w_down)
    v_loc = as_rows(v_w_in, v_w_out, v_w_gate, v_w_up, v_w_down)
    full_grads = {P_IN: gwi, P_OUT: gwo, P_GATE: gwg, P_UP: gwu, P_DOWN: gwd}
    pairs = {P_IN: pair_i, P_OUT: pair_o, P_GATE: pair_g, P_UP: pair_u, P_DOWN: pair_d}
    parts = {P_OUT: part_o, P_GATE: part_g, P_UP: part_u, P_DOWN: part_d}
    names = {P_IN: "w_in", P_OUT: "w_out", P_GATE: "w_gate", P_UP: "w_up", P_DOWN: "w_down"}
    bigs = {}
    res = None
    for p in (P_DOWN, P_GATE, P_UP, P_OUT, P_IN):
        if p == P_IN:
            parts[p] = landed(flight_i, res[1], "in")
        res = _adam_big(full_grads[p], pairs[p], parts[p], w_loc[p], m_loc[p], v_loc[p], "adam_" + names[p])
        bigs[names[p]] = [(o.T if p in (P_IN, P_GATE, P_UP) else o)[None] for o in res]

    def ordered(pick_small, pick_big):
        sm_it = iter(range(n_small))
        out = []
        for name in ("s", "s", "w_in", "s", "s", "s", "s", "s", "w_out", "s", "s", "w_gate", "w_up", "w_down", "s"):
            out.append(pick_small(next(sm_it)) if name == "s" else pick_big(name))
        return out

    grads = ordered(lambda i: g_small[i], lambda n: bigs[n][0])
    deltas = ordered(lambda i: d_small[i], lambda n: bigs[n][1])
    new_m = ordered(lambda i: nm_small[i], lambda n: bigs[n][2])
    new_v = ordered(lambda i: nv_small[i], lambda n: bigs[n][3])
    return (loss, grad_x, *grads, *deltas, *new_m, *new_v)
```

```python
import jax
import jax.numpy as jnp
from jax import lax
from jax.experimental import pallas as pl
from jax.experimental.pallas import tpu as pltpu

F32 = jnp.float32
BF16 = jnp.bfloat16
MESH = pl.DeviceIdType.MESH

N_META = 16
N_DEV = 8
RMS_EPS = 1e-6
LN_EPS = 1e-5
ADAM_LR = 0.001
ADAM_B1 = 0.9
ADAM_B2 = 0.999
ADAM_EPS = 1e-08
ADAM_WD = 0.01
ADAM_STEP = 10

LANE = 128
SUBLANE = 8
BF16_ROWS = 16
ROW_ALIGN = 128
N_ROW_BLOCKS = 4
CONV_HALO = 32
CONV_CHUNK = 64
WGRAD_ROWS = 32
N_CHUNK = 512
WGRAD_TILE_MAX = 1408
ADD_CHUNK = 32
ADAM_COL_BLOCKS = 4
COPY_PIECES = 4
V7X_VMEM_BYTES = 64 * 1024 * 1024
VMEM_LIMIT = V7X_VMEM_BYTES - 6 * 1024 * 1024
SMALL_ROWS = 64
SM_ROWS = 56
N_BIG = 5

ANY = pl.BlockSpec(memory_space=pl.ANY)
VMEM = pl.BlockSpec(memory_space=pltpu.VMEM)


def _cparams(n_grid_axes=0):
    sem = ("arbitrary",) * n_grid_axes if n_grid_axes else None
    return pltpu.CompilerParams(dimension_semantics=sem, vmem_limit_bytes=VMEM_LIMIT)


def _mesh_pos():
    return lax.axis_index("x"), lax.axis_index("y"), lax.axis_index("c")


def _dev_index(px, py, pc):
    return 4 * px + 2 * py + pc


def _other_chips(x, y):
    return [(1 - x, y), (x, 1 - y), (1 - x, 1 - y)]


def _full(shape):
    return pl.BlockSpec(shape, lambda *_: (0,) * len(shape))


def _resident(shape):
    return pl.BlockSpec(shape, lambda *_: (0,) * len(shape), pipeline_mode=pl.Buffered(1))


def _dot_nt(a, w):
    return lax.dot_general(a, w, (((1,), (1,)), ((), ())), preferred_element_type=F32)


def _dot_nn(a, w):
    return jnp.dot(a, w, preferred_element_type=F32)


def _chunks(n, c):
    out, o = [], 0
    while o < n:
        out.append((o, min(c, n - o)))
        o += c
    return out


def _rstd(h):
    return lax.rsqrt(jnp.mean(h * h, axis=-1, keepdims=True) + RMS_EPS)


def _rms_bwd(dyh, yh, r):
    return r * (dyh - yh * jnp.mean(dyh * yh, axis=-1, keepdims=True))


def _silu_grad(a, sig):
    return sig * (1.0 + a * (1.0 - sig))


def _acc_rows(ref, val, first):
    s = jnp.sum(val, axis=0, keepdims=True)

    @pl.when(first)
    def _():
        ref[...] = s

    @pl.when(jnp.logical_not(first))
    def _():
        ref[...] += s


def _row_loop(t_rows, chunk, fn, carry=None):
    def step(i, c):
        return fn(pl.multiple_of(i * chunk, chunk), c)

    return lax.fori_loop(0, t_rows // chunk, step, carry)


def _remote(src, dst, send_sem, recv_sem, to):
    return pltpu.make_async_remote_copy(src_ref=src, dst_ref=dst, send_sem=send_sem, recv_sem=recv_sem,
                                        device_id=to, device_id_type=MESH)


class _SplitRemote:
    def __init__(self, src, dst, send_sem, recv_sem, to, rows, n_chunks):
        units = rows // BF16_ROWS
        n_chunks = max(1, min(n_chunks, units))
        sizes = [(units // n_chunks + (i < units % n_chunks)) * BF16_ROWS for i in range(n_chunks)]
        self.whole = _remote(src, dst, send_sem, recv_sem, to)
        self.parts, o = [], 0
        for n in sizes:
            self.parts.append(_remote(src.at[pl.ds(o, n), :], dst.at[pl.ds(o, n), :], send_sem, recv_sem, to))
            o += n

    def start(self):
        for cp in self.parts:
            cp.start()

    def wait_recv(self):
        self.whole.wait_recv()

    def wait_send(self):
        self.whole.wait_send()


class _Comm:
    def __init__(self, inputs, out_shapes, aliases, scratch, start, finish):
        self.inputs, self.out_shapes, self.aliases, self.scratch = list(inputs), list(out_shapes), dict(aliases), list(scratch)
        self.start, self.finish = start, finish


def _merge_comms(comms):
    inputs, out_shapes, aliases, scratch, spans = [], [], {}, [], []
    for cm in comms:
        spans.append((len(inputs), len(out_shapes), len(scratch), cm))
        aliases.update({len(inputs) + k: len(out_shapes) + v for k, v in cm.aliases.items()})
        inputs += cm.inputs
        out_shapes += cm.out_shapes
        scratch += cm.scratch

    def run(which):
        def fn(ins, outs, scr):
            for i0, o0, s0, cm in spans:
                getattr(cm, which)(ins[i0:i0 + len(cm.inputs)], outs[o0:o0 + len(cm.out_shapes)], scr[s0:s0 + len(cm.scratch)])
        return fn

    return _Comm(inputs, out_shapes, aliases, scratch, run("start"), run("finish"))


def _host_call(body, *, grid, in_specs, out_specs, out_shape, args, name, scratch_shapes=(), comm=None, after=()):
    talks = comm is not None
    if comm is None:
        comm = _Comm([], [], {}, [], lambda *_: None, lambda *_: None)
    n_in, n_out, n_scr = len(args), len(out_shape), len(scratch_shapes)
    c_in, c_out = len(comm.inputs), len(comm.out_shapes)
    n_after = len(after)

    def open_comm(c_ins, c_outs, c_scr):
        if talks:
            _pair_handshake()
        comm.start(c_ins, c_outs, c_scr)

    def hosted(*refs):
        ins, c_ins = refs[:n_in], refs[n_in:n_in + c_in]
        o0 = n_in + c_in + n_after
        outs, c_outs = refs[o0:o0 + n_out], refs[o0 + n_out:o0 + n_out + c_out]
        s0 = o0 + n_out + c_out
        scr, c_scr = refs[s0:s0 + n_scr], refs[s0 + n_scr:]
        if not grid:
            open_comm(c_ins, c_outs, c_scr)
            body(*ins, *outs, *scr)
            comm.finish(c_ins, c_outs, c_scr)
            return
        first = last = None
        for a, n in enumerate(grid):
            f, l = pl.program_id(a) == 0, pl.program_id(a) == n - 1
            first = f if first is None else jnp.logical_and(first, f)
            last = l if last is None else jnp.logical_and(last, l)

        @pl.when(first)
        def _():
            open_comm(c_ins, c_outs, c_scr)

        body(*ins, *outs, *scr)

        @pl.when(last)
        def _():
            comm.finish(c_ins, c_outs, c_scr)

    sem = ("arbitrary",) * len(grid) if grid else None
    params = pltpu.CompilerParams(dimension_semantics=sem, vmem_limit_bytes=VMEM_LIMIT,
                                  collective_id=PAIR_BARRIER_ID if talks else None)
    res = pl.pallas_call(
        hosted, grid=grid, in_specs=list(in_specs) + [ANY] * (c_in + n_after), out_specs=list(out_specs) + [ANY] * c_out,
        out_shape=list(out_shape) + comm.out_shapes, scratch_shapes=list(scratch_shapes) + comm.scratch,
        input_output_aliases={n_in + k: n_out + v for k, v in comm.aliases.items()},
        name=name, compiler_params=params)(*args, *comm.inputs, *after)
    return list(res[:n_out]), list(res[n_out:])


PAIR_BARRIER_ID = 0
START_BARRIER_IDS = (1, 2, 3, 4, 5)


def _chips_handshake():
    x, y, c = _mesh_pos()
    barrier = pltpu.get_barrier_semaphore()
    for chip in _other_chips(x, y):
        pl.semaphore_signal(barrier, inc=1, device_id=(*chip, c), device_id_type=MESH)
    pl.semaphore_wait(barrier, 3)


def _pair_handshake():
    x, y, c = _mesh_pos()
    barrier = pltpu.get_barrier_semaphore()
    pl.semaphore_signal(barrier, inc=1, device_id=(x, y, 1 - c), device_id_type=MESH)
    pl.semaphore_wait(barrier, 1)


GATHER_SEMS = 10
D2D_CHUNKS = 8


class _Gather:
    def __init__(self, jobs, rows, lo, src_ref, dests, send_sems, recv_sems):
        x, y, c = _mesh_pos()
        me, sib = (x, y, c), (x, y, 1 - c)
        nx, ny, dg = (1 - x, y, c), (x, 1 - y, c), (1 - x, 1 - y, c)
        self.relayed, self.direct, self.relay, self.to_sib, self.sib_fwd = [], [], [], [], []
        for n, (p, r0, nr) in enumerate(jobs):
            assert nr % (2 * BF16_ROWS) == 0
            half = nr // 2

            def rows_of(dev, h, p=p, r0=r0, nr=nr, half=half):
                off, cnt = (r0, nr) if h is None else (r0 + h * half, half)
                return dests[p].at[pl.ds(pl.multiple_of(_dev_index(*dev) * rows[p] + off, BF16_ROWS), cnt), :]

            def mine(h, p=p, r0=r0, nr=nr, half=half):
                off, cnt = (r0, nr) if h is None else (r0 + h * half, half)
                return src_ref.at[pl.ds(lo[p] + off, cnt), :]

            sem = lambda k, n=n: (send_sems.at[GATHER_SEMS * n + k], recv_sems.at[GATHER_SEMS * n + k])
            self.relayed.append([_remote(mine(0), rows_of(me, 0), *sem(0), nx), _remote(mine(1), rows_of(me, 1), *sem(3), ny)])
            self.direct.append([_remote(mine(1), rows_of(me, 1), *sem(1), nx), _remote(mine(0), rows_of(me, 0), *sem(2), ny)])
            self.relay.append([_remote(rows_of(nx, 0), rows_of(nx, 0), *sem(4), ny), _remote(rows_of(ny, 1), rows_of(ny, 1), *sem(5), nx)])
            self.to_sib.append(_SplitRemote(mine(None), rows_of(me, None), *sem(6), sib, nr, D2D_CHUNKS))
            self.sib_fwd.append([_SplitRemote(rows_of(dev, None), rows_of(dev, None), *sem(7 + i), sib, nr, D2D_CHUNKS)
                                 for i, dev in enumerate((nx, ny, dg))])

    def start(self):
        for group in (self.relayed, self.direct):
            for cps in group:
                for cp in cps:
                    cp.start()
        for cp in self.to_sib:
            cp.start()

    def mid(self):
        for first, relay in zip(self.relayed, self.relay):
            for arrived, onward in zip(first, relay):
                arrived.wait_recv()
                onward.start()

    def finish(self):
        for direct, relay, fwd in zip(self.direct, self.relay, self.sib_fwd):
            for k in range(2):
                direct[k].wait_recv()
                fwd[k].start()
            for cp in relay:
                cp.wait_recv()
            fwd[2].start()
        for n in range(len(self.to_sib)):
            self.to_sib[n].wait_recv()
            for cp in self.sib_fwd[n]:
                cp.wait_recv()
            for cp in self.relayed[n] + self.direct[n] + self.relay[n] + [self.to_sib[n]] + self.sib_fwd[n]:
                cp.wait_send()


HBM = pl.BlockSpec(memory_space=pltpu.HBM)
SEM = pl.BlockSpec(memory_space=pltpu.SEMAPHORE)
FLOWS = pltpu.SideEffectType.DATAFLOW_SIDE_EFFECTING


def _in_hbm(a):
    return pltpu.with_memory_space_constraint(a, pltpu.HBM)


def _gather_start(wl, dests, ps, rows, barrier_id):
    lo = [sum(rows[:p]) for p in range(N_BIG)]
    n = len(ps)

    def body(*refs):
        wl_ref, dest_refs = refs[0], refs[1:1 + n]
        sends, recvs = refs[1 + n:1 + 2 * n], refs[1 + 2 * n:1 + 3 * n]
        token = refs[-1]
        _chips_handshake()
        x, y, c = _mesh_pos()
        jme = _dev_index(x, y, c)
        for i, p in enumerate(ps):
            mine = dest_refs[i].at[pl.ds(pl.multiple_of(jme * rows[p], BF16_ROWS), rows[p]), :]
            for chip in _other_chips(x, y):
                _remote(wl_ref.at[pl.ds(lo[p], rows[p]), :], mine, sends[i], recvs[i], (*chip, c)).start()
        token[...] = jnp.zeros_like(token)

    thru = [pltpu.HBM(wl.shape, wl.dtype)] + [pltpu.HBM(dests[p].shape, BF16) for p in ps]
    res = pl.pallas_call(
        body, name="gather_start",
        out_shape=tuple([pltpu.SemaphoreType.DMA(())] * (2 * n) + thru + [jax.ShapeDtypeStruct((SUBLANE, LANE), F32)]),
        in_specs=[HBM] * (1 + n), out_specs=tuple([SEM] * (2 * n) + [HBM] * (1 + n) + [VMEM]),
        input_output_aliases={i: 2 * n + i for i in range(1 + n)},
        compiler_params=pltpu.CompilerParams(has_side_effects=FLOWS, collective_id=barrier_id))(
            _in_hbm(wl), *[_in_hbm(dests[p]) for p in ps])
    sems = [(res[i], res[n + i]) for i in range(n)]
    return sems, res[2 * n], list(res[2 * n + 1:3 * n + 1]), res[-1]


def _gather_wait(wl, dest, sems, after, r, name):
    def body(wl_ref, dest_ref, send_sem, recv_sem, after_ref, wl_out, dest_out):
        x, y, c = _mesh_pos()
        three = dest_ref.at[pl.ds(0, 3 * r), :]
        cp = _remote(three, three, send_sem, recv_sem, (x, y, 1 - c))
        cp.wait_send()
        cp.wait_recv()

    res = pl.pallas_call(
        body, name=name, out_shape=(pltpu.HBM(wl.shape, wl.dtype), pltpu.HBM(dest.shape, dest.dtype)),
        in_specs=[HBM, HBM, SEM, SEM, ANY], out_specs=(HBM, HBM), input_output_aliases={0: 0, 1: 1},
        compiler_params=pltpu.CompilerParams(has_side_effects=FLOWS))(wl, dest, sems[0], sems[1], after)
    return res[0], res[1]


def _forward_comm(dest, r):
    def descs(ins, outs, scr):
        x, y, c = _mesh_pos()
        cps = []
        for k, chip in enumerate(_other_chips(x, y)):
            blk = outs[0].at[pl.ds(pl.multiple_of(_dev_index(*chip, c) * r, BF16_ROWS), r), :]
            cps.append(_SplitRemote(blk, blk, scr[0].at[k], scr[1].at[k], (x, y, 1 - c), r, D2D_CHUNKS))
        return cps

    def start(ins, outs, scr):
        for cp in descs(ins, outs, scr):
            cp.start()

    def finish(ins, outs, scr):
        cps = descs(ins, outs, scr)
        for cp in cps:
            cp.wait_recv()
        for cp in cps:
            cp.wait_send()

    return _Comm([dest], [jax.ShapeDtypeStruct(dest.shape, dest.dtype)], {0: 0},
                 [pltpu.SemaphoreType.DMA((3,)), pltpu.SemaphoreType.DMA((3,))], start, finish)


def _forward_now(dest, r, name):
    _, (dest,) = _host_call(lambda: None, grid=(), in_specs=[], out_specs=[], out_shape=[], args=(), name=name,
                            comm=_forward_comm(dest, r))
    return dest


def _pair_comm(g, r):
    d = g.shape[1]

    def descs(ins, outs, scr):
        x, y, c = _mesh_pos()
        chips = [(x, y)] + _other_chips(x, y)
        return [_SplitRemote(ins[0].at[pl.ds(pl.multiple_of(_dev_index(*chip, 1 - c) * r, BF16_ROWS), r), :], outs[0].at[k],
                             scr[0].at[k], scr[1].at[k], (x, y, 1 - c), r, D2D_CHUNKS) for k, chip in enumerate(chips)]

    def start(ins, outs, scr):
        for cp in descs(ins, outs, scr):
            cp.start()

    def finish(ins, outs, scr):
        cps = descs(ins, outs, scr)
        for cp in cps:
            cp.wait_recv()
        for cp in cps:
            cp.wait_send()

    comm = _Comm([g], [jax.ShapeDtypeStruct((4, r, d), BF16)], {},
                 [pltpu.SemaphoreType.DMA((4,)), pltpu.SemaphoreType.DMA((4,))], start, finish)
    return comm


def _pair_sum(g, pair, r, name, after=()):
    d = g.shape[1]

    def body(g_ref, p_ref, *rest):
        o_ref, gbuf, pbuf, sems = rest[len(after):]
        x, y, c = _mesh_pos()
        loads = [pltpu.make_async_copy(p_ref.at[pl.ds(1, 3)], pbuf, sems.at[3])]
        for k, chip in enumerate(_other_chips(x, y)):
            j = _dev_index(*chip, c)
            loads.append(pltpu.make_async_copy(g_ref.at[pl.ds(pl.multiple_of(j * r, BF16_ROWS), r), :], gbuf.at[k], sems.at[k]))
        for cp in loads:
            cp.start()
        for cp in loads:
            cp.wait()
        for k in range(3):
            o_ref[k] = (gbuf[k].astype(F32) + pbuf[k].astype(F32)).astype(BF16)

    return pl.pallas_call(
        body, out_shape=jax.ShapeDtypeStruct((3, r, d), BF16), in_specs=[ANY] * (2 + len(after)), out_specs=VMEM,
        scratch_shapes=[pltpu.VMEM((3, r, d), BF16), pltpu.VMEM((3, r, d), BF16), pltpu.SemaphoreType.DMA((4,))],
        name=name, compiler_params=_cparams())(g, pair, *after)


def _chip_start(sums, name, barrier_id):
    n = len(sums)

    def body(*refs):
        srcs, lands = refs[:n], refs[n:2 * n]
        sends, recvs = refs[2 * n:3 * n], refs[3 * n:4 * n]
        _chips_handshake()
        x, y, c = _mesh_pos()
        for i in range(n):
            for k, chip in enumerate(_other_chips(x, y)):
                _remote(srcs[i].at[k], lands[i].at[k], sends[i], recvs[i], (*chip, c)).start()
        refs[-1][...] = jnp.zeros_like(refs[-1])

    zones = [pltpu.HBM(s.shape, s.dtype) for s in sums]
    res = pl.pallas_call(
        body, name=name,
        out_shape=tuple([pltpu.SemaphoreType.DMA(())] * (2 * n) + zones + zones + [jax.ShapeDtypeStruct((SUBLANE, LANE), F32)]),
        in_specs=[HBM] * (2 * n), out_specs=tuple([SEM] * (2 * n) + [HBM] * (2 * n) + [VMEM]),
        input_output_aliases={i: 2 * n + i for i in range(2 * n)},
        compiler_params=pltpu.CompilerParams(has_side_effects=FLOWS, collective_id=barrier_id))(
            *[_in_hbm(s) for s in sums], *[_in_hbm(lax.empty(s.shape, s.dtype)) for s in sums])
    flights = [((res[i], res[n + i]), res[2 * n + i], res[3 * n + i]) for i in range(n)]
    return flights, res[-1]


def _chip_wait(sums, land, sems, after, name):
    def body(sums_ref, land_ref, send_sem, recv_sem, after_ref, sums_out, land_out):
        x, y, c = _mesh_pos()
        cp = _remote(sums_ref, land_ref, send_sem, recv_sem, (x, y, 1 - c))
        cp.wait_send()
        cp.wait_recv()

    res = pl.pallas_call(
        body, name=name, out_shape=(pltpu.HBM(sums.shape, sums.dtype), pltpu.HBM(land.shape, land.dtype)),
        in_specs=[HBM, HBM, SEM, SEM, ANY], out_specs=(HBM, HBM), input_output_aliases={0: 0, 1: 1},
        compiler_params=pltpu.CompilerParams(has_side_effects=FLOWS))(sums, land, sems[0], sems[1], after)
    return res[1]


class _CopyThrough:
    def __init__(self, src_ref, dst_ref, dst_row0, n_rows, buf, sem_in, sem_out):
        rc = n_rows // COPY_PIECES
        piece = lambda ref, o: ref.at[pl.ds(o, rc), :]
        self.loads = [pltpu.make_async_copy(piece(src_ref, k * rc), piece(buf, k * rc), sem_in) for k in range(COPY_PIECES)]
        self.stores = [pltpu.make_async_copy(piece(buf, k * rc), piece(dst_ref, dst_row0 + k * rc), sem_out) for k in range(COPY_PIECES)]
        self.all_in = pltpu.make_async_copy(src_ref, buf, sem_in)
        self.all_out = pltpu.make_async_copy(buf, dst_ref.at[pl.ds(dst_row0, n_rows), :], sem_out)

    def load(self):
        for cp in self.loads:
            cp.start()

    def store(self):
        self.all_in.wait()
        for cp in self.stores:
            cp.start()

    def done(self):
        self.all_out.wait()


def _gather_first(shards, sm, jobs, x2, tgt2, t_rows, x0):
    d = shards[0].shape[1]
    rows = [w.shape[0] for w in shards]
    lo = [sum(rows[:p]) for p in range(N_BIG)]
    n_sems = GATHER_SEMS * len(jobs)
    seq = x2.shape[0]
    assert x0 == ROW_ALIGN and seq % ROW_ALIGN == 0 and d == N_DEV * LANE

    def body(s0, s1, s2, s3, s4, sm_ref, x_ref, tgt_ref, wl_ref, o0, o1, o2, o3, o4, sa_ref, h0_ref, tp_ref,
             wl_v, x_v, tgt_v, heads_v, sa_v, send_sems, recv_sems, ssend, srecv, local_sems, sems_in, sems_out):
        dests = (o0, o1, o2, o3, o4)
        x, y, c = _mesh_pos()
        me = (x, y, c)
        jme = _dev_index(*me)
        padded = [_CopyThrough(x_ref, h0_ref, x0, seq, x_v, sems_in.at[0], sems_out.at[0]),
                  _CopyThrough(tgt_ref, tp_ref, x0, seq, tgt_v, sems_in.at[1], sems_out.at[1])]
        for cp in padded:
            cp.load()
        shard_refs = (s0, s1, s2, s3, s4)
        first = sorted({j[0] for j in jobs})
        for p in first + [p for p in range(N_BIG) if p not in first]:
            wl_v[pl.ds(lo[p], rows[p]), :] = shard_refs[p][...].astype(BF16)
            if p == first[-1]:
                gather = _Gather(jobs, rows, lo, wl_v, dict(enumerate(dests)), send_sems, recv_sems)
                gather.start()
        peers = [(x, y, 1 - c)] + [(*chip, pc) for pc in (c, 1 - c) for chip in _other_chips(x, y)]
        smalls = [_remote(sm_ref, sa_ref.at[jme], ssend.at[k], srecv.at[k], to) for k, to in enumerate(peers)]
        for cp in smalls:
            cp.start()
        mine = [pltpu.make_async_copy(wl_v.at[pl.ds(lo[p], rows[p]), :],
                                      dests[p].at[pl.ds(pl.multiple_of(jme * rows[p], BF16_ROWS), rows[p]), :], local_sems.at[p])
                for p in range(N_BIG)]
        mine.append(pltpu.make_async_copy(wl_v, wl_ref, local_sems.at[N_BIG]))
        mine.append(pltpu.make_async_copy(sm_ref, sa_ref.at[jme], local_sems.at[N_BIG + 1]))
        for cp in mine:
            cp.start()
        later = [p for p in range(N_BIG) if p not in {j[0] for j in jobs}]
        own = [_SplitRemote(wl_v.at[pl.ds(lo[p], rows[p]), :],
                            dests[p].at[pl.ds(pl.multiple_of(jme * rows[p], BF16_ROWS), rows[p]), :],
                            ssend.at[7 + i], srecv.at[7 + i], (x, y, 1 - c), rows[p], D2D_CHUNKS) for i, p in enumerate(later)]
        for cp in own:
            cp.start()
        for cp in padded:
            cp.store()
        gather.mid()
        for cp in smalls + own:
            cp.wait_recv()
        mine[-1].wait()
        to_v = pltpu.make_async_copy(sa_ref, sa_v, local_sems.at[N_BIG + 1])
        to_v.start()
        to_v.wait()
        head, zeros = heads_v.at[0], heads_v.at[1]
        head[...] = jnp.zeros_like(head)
        zeros[...] = jnp.zeros_like(zeros)
        for j in range(N_DEV):
            head[pl.ds(x0 - N_META, N_META), pl.ds(j * LANE, LANE)] = sa_v[j, pl.ds(0, N_META), :]
        heads = [pltpu.make_async_copy(head, h0_ref.at[pl.ds(0, x0), :], local_sems.at[N_BIG + 1]),
                 pltpu.make_async_copy(zeros, tp_ref.at[pl.ds(0, x0), :], local_sems.at[N_BIG + 2])]
        for cp in heads:
            cp.start()
        gather.finish()
        for cp in smalls + own:
            cp.wait_send()
        for cp in mine[:-1] + heads:
            cp.wait()
        for cp in padded:
            cp.done()

    out_shape = [jax.ShapeDtypeStruct((sum(rows), d), BF16)]
    out_shape += [jax.ShapeDtypeStruct((N_DEV * r, d), BF16) for r in rows]
    out_shape.append(jax.ShapeDtypeStruct((N_DEV,) + sm.shape, F32))
    out_shape += [jax.ShapeDtypeStruct((t_rows, d), F32)] * 2
    res = pl.pallas_call(
        body, out_shape=out_shape, in_specs=[VMEM] * 6 + [ANY] * 2, out_specs=[ANY] * 9,
        scratch_shapes=[pltpu.VMEM((sum(rows), d), BF16), pltpu.VMEM((seq, d), F32), pltpu.VMEM((seq, d), F32),
                        pltpu.VMEM((2, ROW_ALIGN, d), F32), pltpu.VMEM((N_DEV,) + sm.shape, F32),
                        pltpu.SemaphoreType.DMA((n_sems,)), pltpu.SemaphoreType.DMA((n_sems,)),
                        pltpu.SemaphoreType.DMA((7 + N_BIG,)), pltpu.SemaphoreType.DMA((7 + N_BIG,)),
                        pltpu.SemaphoreType.DMA((N_BIG + 3,)), pltpu.SemaphoreType.DMA((2,)), pltpu.SemaphoreType.DMA((2,))],
        name="gather_first", compiler_params=_cparams())(*shards, sm, x2, tgt2)
    return res[0], list(res[1:1 + N_BIG]), res[1 + N_BIG], res[2 + N_BIG], res[3 + N_BIG]


def _in_proj(h0, g1, win_t, tm, comm):
    t_rows, d = h0.shape
    e = win_t.shape[0]

    def body(h_ref, g_ref, w_ref, xn_ref, hin_ref):
        h = h_ref[...]
        xn = ((h * _rstd(h)) * g_ref[...]).astype(BF16)
        xn_ref[...] = xn
        for o, n in _chunks(e, N_CHUNK):
            hin_ref[:, pl.ds(o, n)] = _dot_nt(xn, w_ref[pl.ds(o, n), :])

    return _host_call(
        body, grid=(t_rows // tm,),
        in_specs=[pl.BlockSpec((tm, d), lambda i: (i, 0)), _full((1, d)), _resident((e, d))],
        out_specs=[pl.BlockSpec((tm, d), lambda i: (i, 0)), pl.BlockSpec((tm, e), lambda i: (i, 0))],
        out_shape=[jax.ShapeDtypeStruct((t_rows, d), BF16), jax.ShapeDtypeStruct((t_rows, e), F32)],
        args=(h0, g1, win_t), name="in_proj", comm=comm)


def _tap_slot(off):
    return off % SUBLANE, (off // SUBLANE) * SUBLANE


def _fill_shifted(sh_ref, base_ref, residues, n_rows):
    for r in residues:
        if r:
            sh_ref[r] = base_ref[pl.ds(r, n_rows), :]


def _shifted_rows(pair, r, start, n):
    base_ref, sh_ref = pair
    return base_ref[pl.ds(start, n), :] if r == 0 else sh_ref[r, pl.ds(start, n), :]


def _mix_conv_fwd(hin, wa, wb, bb, wa_w, comm):
    t_rows = hin.shape[0]
    nt = wa_w // LANE
    ka, kb = wa.shape[0], wb.shape[0]
    nr = CONV_HALO + t_rows

    def body(bg_ref, cg_ref, ha_ref, val_ref, gt_ref, wa_ref, wb_ref, bb_ref, ya_ref, z_ref, base, sh):
        base[pl.ds(0, CONV_HALO), :] = jnp.zeros((CONV_HALO, LANE), F32)
        base[pl.ds(nr, SUBLANE), :] = jnp.zeros((SUBLANE, LANE), F32)

        def conv(w_ref, k_taps, b, n):
            acc = None
            for k in range(k_taps):
                r, q = _tap_slot(CONV_HALO - (k_taps - 1) + k)
                term = w_ref[pl.ds(k, 1), :] * _shifted_rows((base, sh), r, b + q, n)
                acc = term if acc is None else acc + term
            return acc

        def fill_a(b, c):
            base[pl.ds(CONV_HALO + b, CONV_CHUNK), :] = cg_ref[pl.ds(b, CONV_CHUNK), :] * ha_ref[pl.ds(b, CONV_CHUNK), :]
            return c

        _row_loop(t_rows, CONV_CHUNK, fill_a)
        _fill_shifted(sh, base, sorted({_tap_slot(CONV_HALO - (ka - 1) + k)[0] for k in range(ka)}), nr)

        def out_a(b, c):
            ya_ref[pl.ds(b, CONV_CHUNK), :] = (bg_ref[pl.ds(b, CONV_CHUNK), :] * conv(wa_ref, ka, b, CONV_CHUNK)).astype(BF16)
            return c

        _row_loop(t_rows, CONV_CHUNK, out_a)

        def fill_b(b, c):
            base[pl.ds(CONV_HALO + b, CONV_CHUNK), :] = (val_ref[pl.ds(b, CONV_CHUNK), :]
                                                          * jax.nn.sigmoid(gt_ref[pl.ds(b, CONV_CHUNK), :]))
            return c

        _row_loop(t_rows, CONV_CHUNK, fill_b)
        _fill_shifted(sh, base, range(SUBLANE), nr)

        def out_b(b, c):
            z_ref[pl.ds(b, CONV_CHUNK), :] = conv(wb_ref, kb, b, CONV_CHUNK) + bb_ref[...]
            return c

        _row_loop(t_rows, CONV_CHUNK, out_b)

    def col(g):
        return pl.BlockSpec((t_rows, LANE), lambda i, g=g: (0, g * nt + i))

    tile = lambda rows: pl.BlockSpec((rows, LANE), lambda i: (0, i))
    return _host_call(
        body, grid=(nt,),
        in_specs=[col(0), col(1), col(2), col(3), col(4), tile(ka), tile(kb), tile(1)],
        out_specs=[tile(t_rows), tile(t_rows)],
        out_shape=[jax.ShapeDtypeStruct((t_rows, wa_w), BF16), jax.ShapeDtypeStruct((t_rows, wa_w), F32)],
        scratch_shapes=[pltpu.VMEM((nr + SUBLANE, LANE), F32), pltpu.VMEM((SUBLANE, nr, LANE), F32)],
        args=(hin, hin, hin, hin, hin, wa, wb, bb), name="mix_conv_fwd", comm=comm)


def _ln_parts(z, lg, lb):
    mu = jnp.mean(z, axis=-1, keepdims=True)
    zc = z - mu
    rstd = lax.rsqrt(jnp.mean(zc * zc, axis=-1, keepdims=True) + LN_EPS)
    zh = zc * rstd
    return zh, rstd, zh * lg + lb


def _out_proj(ya, z, lg, lb, w_out, h0, g2, g3, tm, comm):
    t_rows, d = h0.shape
    w = z.shape[1]

    def body(ya_ref, z_ref, lg_ref, lb_ref, w_ref, h0_ref, g2_ref, g3_ref, y_ref, mix_ref, h1_ref, xn2_ref):
        _, _, ln = _ln_parts(z_ref[...], lg_ref[...], lb_ref[...])
        y_ref[:, pl.ds(0, w)] = ya_ref[...]
        y_ref[:, pl.ds(w, w)] = (ln * jax.nn.sigmoid(ln)).astype(BF16)
        mix = _dot_nn(y_ref[...], w_ref[...])
        mix_ref[...] = mix
        h1 = h0_ref[...] + (mix * _rstd(mix)) * g2_ref[...]
        h1_ref[...] = h1
        xn2_ref[...] = ((h1 * _rstd(h1)) * g3_ref[...]).astype(BF16)

    blk = pl.BlockSpec((tm, d), lambda i: (i, 0))
    half = pl.BlockSpec((tm, w), lambda i: (i, 0))
    return _host_call(
        body, grid=(t_rows // tm,),
        in_specs=[half, half, _full((1, w)), _full((1, w)), _resident(w_out.shape), blk, _full((1, d)), _full((1, d))],
        out_specs=[blk, blk, blk, blk],
        out_shape=[jax.ShapeDtypeStruct((t_rows, d), BF16), jax.ShapeDtypeStruct((t_rows, d), F32),
                   jax.ShapeDtypeStruct((t_rows, d), F32), jax.ShapeDtypeStruct((t_rows, d), BF16)],
        args=(ya, z, lg, lb, w_out, h0, g2, g3), name="out_proj", comm=comm)


def _gate_up(xn2, wg_t, wu_t, tm, comm):
    t_rows, d = xn2.shape
    f = wg_t.shape[0]

    def body(x_ref, wg_ref, wu_ref, a_ref, u_ref, s_ref):
        xn = x_ref[...]
        for o, n in _chunks(f, N_CHUNK):
            a = _dot_nt(xn, wg_ref[pl.ds(o, n), :])
            u = _dot_nt(xn, wu_ref[pl.ds(o, n), :])
            a_ref[:, pl.ds(o, n)] = a.astype(BF16)
            u_ref[:, pl.ds(o, n)] = u.astype(BF16)
            s_ref[:, pl.ds(o, n)] = ((a * jax.nn.sigmoid(a)) * u).astype(BF16)

    blk = pl.BlockSpec((tm, f), lambda i: (i, 0))
    return _host_call(
        body, grid=(t_rows // tm,),
        in_specs=[pl.BlockSpec((tm, d), lambda i: (i, 0)), _resident((f, d)), _resident((f, d))],
        out_specs=[blk, blk, blk], out_shape=[jax.ShapeDtypeStruct((t_rows, f), BF16)] * 3,
        args=(xn2, wg_t, wu_t), name="gate_up", comm=comm)


def _down_loss(s, wd, h1, tgt, g4, tm, x0):
    t_rows, d = h1.shape
    f = wd.shape[0]

    def body(s_ref, w_ref, h1_ref, tgt_ref, g4_ref, dh2_ref, dff_ref, dg4_ref, loss_ref):
        i = pl.program_id(0)
        ff = _dot_nn(s_ref[...], w_ref[...])
        r4 = _rstd(ff)
        fh = ff * r4
        g4 = g4_ref[...]
        h2 = h1_ref[...] + fh * g4
        row = i * tm + lax.broadcasted_iota(jnp.int32, (tm, 1), 0)
        diff = jnp.where(row >= x0, h2 - tgt_ref[...], 0.0)
        dh2 = diff / d
        dh2_ref[...] = dh2
        dff_ref[...] = _rms_bwd(dh2 * g4, fh, r4).astype(BF16)
        _acc_rows(dg4_ref, dh2 * fh, i == 0)
        _acc_rows(loss_ref, diff * diff, i == 0)

    blk = pl.BlockSpec((tm, d), lambda i: (i, 0))
    res, _ = _host_call(
        body, grid=(t_rows // tm,),
        in_specs=[pl.BlockSpec((tm, f), lambda i: (i, 0)), _resident((f, d)), blk, blk, _full((1, d))],
        out_specs=[blk, blk, _full((1, d)), _full((1, d))],
        out_shape=[jax.ShapeDtypeStruct((t_rows, d), F32), jax.ShapeDtypeStruct((t_rows, d), BF16),
                   jax.ShapeDtypeStruct((1, d), F32), jax.ShapeDtypeStruct((1, d), F32)],
        args=(s, wd, h1, tgt, g4), name="down_loss")
    return res


def _bwd_down(dff, wd, a, u, tm, comm):
    t_rows, d = dff.shape
    f = wd.shape[0]

    def body(dff_ref, w_ref, a_ref, u_ref, da_ref, du_ref):
        dff_v = dff_ref[...]
        for o, n in _chunks(f, N_CHUNK):
            ds = _dot_nt(dff_v, w_ref[pl.ds(o, n), :]).astype(BF16)
            av = a_ref[:, pl.ds(o, n)]
            uv = u_ref[:, pl.ds(o, n)]
            sig = jax.nn.sigmoid(av)
            da_ref[:, pl.ds(o, n)] = ds * uv * _silu_grad(av, sig)
            du_ref[:, pl.ds(o, n)] = ds * (av * sig)

    blk = pl.BlockSpec((tm, f), lambda i: (i, 0))
    return _host_call(
        body, grid=(t_rows // tm,),
        in_specs=[pl.BlockSpec((tm, d), lambda i: (i, 0)), _resident((f, d)), blk, blk],
        out_specs=[blk, blk], out_shape=[jax.ShapeDtypeStruct((t_rows, f), BF16)] * 2,
        args=(dff, wd, a, u), name="bwd_down", comm=comm)


def _wgrad(a, b, name, after=()):
    d = b.shape[1]
    t_rows = b.shape[0]
    stacked = a.ndim == 3
    n = a.shape[-1]
    groups = a.shape[0] if stacked else 1
    steps = 1 if stacked else 2
    tile = max(t for t in range(LANE, min(n // steps, WGRAD_TILE_MAX) + 1, LANE) if n % t == 0)
    tiles = n // tile

    def body(a_ref, b_ref, o_ref):
        o_ref[...] = lax.dot_general(a_ref[...], b_ref[...], (((0,), (0,)), ((), ())),
                                     preferred_element_type=F32).astype(BF16)

    if stacked:
        a_spec = pl.BlockSpec((None, t_rows, tile), lambda g, i: (g, 0, i))
    else:
        a_spec = pl.BlockSpec((t_rows, tile), lambda g, i: (0, i))
    res, _ = _host_call(
        body, grid=(groups, tiles), in_specs=[a_spec, _resident((t_rows, d))],
        out_specs=[pl.BlockSpec((tile, d), lambda g, i: (g * tiles + i, 0))],
        out_shape=[jax.ShapeDtypeStruct((groups * n, d), BF16)], args=(a, b), name=name, after=after)
    return res[0]


def _bwd_ffn_in(da, du, wg_t, wu_t, h1, dh2, g3, tm, comm):
    t_rows, d = h1.shape
    f = wg_t.shape[0]

    def body(da_ref, du_ref, wg_ref, wu_ref, h1_ref, dh2_ref, g3_ref, dh1_ref, dg3_ref):
        dxn2 = _dot_nn(da_ref[...], wg_ref[...]) + _dot_nn(du_ref[...], wu_ref[...])
        h1 = h1_ref[...]
        r3 = _rstd(h1)
        h1h = h1 * r3
        _acc_rows(dg3_ref, dxn2 * h1h, pl.program_id(0) == 0)
        dh1_ref[...] = dh2_ref[...] + _rms_bwd(dxn2 * g3_ref[...], h1h, r3)

    blk = pl.BlockSpec((tm, d), lambda i: (i, 0))
    blkf = pl.BlockSpec((tm, f), lambda i: (i, 0))
    return _host_call(
        body, grid=(t_rows // tm,),
        in_specs=[blkf, blkf, _resident((f, d)), _resident((f, d)), blk, blk, _full((1, d))],
        out_specs=[blk, _full((1, d))],
        out_shape=[jax.ShapeDtypeStruct((t_rows, d), F32), jax.ShapeDtypeStruct((1, d), F32)],
        args=(da, du, wg_t, wu_t, h1, dh2, g3), name="bwd_ffn_in", comm=comm)


def _bwd_out_proj(dh1, mix, w_out, g2, z, lg, lb, tm, after):
    t_rows, d = dh1.shape
    w = z.shape[1]

    def body(dh1_ref, mix_ref, w_ref, g2_ref, z_ref, lg_ref, lb_ref, dmix_ref, dya_ref, dz_ref, dg2_ref, dlg_ref, dlb_ref, dbb_ref):
        first = pl.program_id(0) == 0
        mix = mix_ref[...]
        r2 = _rstd(mix)
        mh = mix * r2
        dh1 = dh1_ref[...]
        _acc_rows(dg2_ref, dh1 * mh, first)
        dmix = _rms_bwd(dh1 * g2_ref[...], mh, r2).astype(BF16)
        dmix_ref[...] = dmix
        dy = _dot_nt(dmix, w_ref[...])
        dya_ref[...] = dy[:, :w]
        lg = lg_ref[...]
        zh, rstd, ln = _ln_parts(z_ref[...], lg, lb_ref[...])
        dln = dy[:, w:] * _silu_grad(ln, jax.nn.sigmoid(ln))
        _acc_rows(dlg_ref, dln * zh, first)
        _acc_rows(dlb_ref, dln, first)
        dzh = dln * lg
        dz = rstd * (dzh - jnp.mean(dzh, axis=-1, keepdims=True) - zh * jnp.mean(dzh * zh, axis=-1, keepdims=True))
        dz_ref[...] = dz
        _acc_rows(dbb_ref, dz, first)

    blk = pl.BlockSpec((tm, d), lambda i: (i, 0))
    half = pl.BlockSpec((tm, w), lambda i: (i, 0))
    vec = _full((1, w))
    res, _ = _host_call(
        body, grid=(t_rows // tm,), in_specs=[blk, blk, _resident(w_out.shape), _full((1, d)), half, vec, vec],
        out_specs=[blk, half, half, _full((1, d)), vec, vec, vec],
        out_shape=[jax.ShapeDtypeStruct((t_rows, d), BF16), jax.ShapeDtypeStruct((t_rows, w), F32),
                   jax.ShapeDtypeStruct((t_rows, w), F32), jax.ShapeDtypeStruct((1, d), F32)]
        + [jax.ShapeDtypeStruct((1, w), F32)] * 3,
        args=(dh1, mix, w_out, g2, z, lg, lb), name="bwd_out_proj", after=after)
    return res


def _mix_conv_bwd(hin, dy, dz, wa, wb, wa_w, comm):
    t_rows = hin.shape[0]
    nt = wa_w // LANE
    ka, kb = wa.shape[0], wb.shape[0]
    nr = CONV_HALO + t_rows
    kb_rows = -(-kb // SUBLANE) * SUBLANE

    def body(bg_ref, cg_ref, ha_ref, val_ref, gt_ref, dya_ref, dz_ref, wa_ref, wb_ref,
             dh_ref, dwa_ref, dwb_ref, base, sh, based, shd, tmp, wbc):
        zeros = lambda n: jnp.zeros((n, LANE), F32)
        base[pl.ds(0, CONV_HALO), :] = zeros(CONV_HALO)
        base[pl.ds(nr, SUBLANE), :] = zeros(SUBLANE)
        based[pl.ds(t_rows, CONV_HALO + SUBLANE), :] = zeros(CONV_HALO + SUBLANE)

        def fwd_slot(k_taps, k):
            return _tap_slot(CONV_HALO - (k_taps - 1) + k)

        def bwd_slot(k_taps, k):
            return _tap_slot(k_taps - 1 - k)

        def conv(w_ref, k_taps, src, slot, b, n):
            acc = None
            for k in range(k_taps):
                r, q = slot(k_taps, k)
                term = w_ref[pl.ds(k, 1), :] * _shifted_rows(src, r, b + q, n)
                acc = term if acc is None else acc + term
            return acc

        def by_residue(k_taps, slot):
            groups = {}
            for k in range(k_taps):
                r, q = slot(k_taps, k)
                groups.setdefault(r, []).append((k, q // SUBLANE))
            return groups

        def wgrad_loop(w_ref, k_taps):
            n_sub = WGRAD_ROWS // SUBLANE
            for k in range(k_taps):
                wbc[k] = jnp.broadcast_to(w_ref[pl.ds(k, 1), :], (SUBLANE, LANE))
            fwd, bwd = by_residue(k_taps, fwd_slot), by_residue(k_taps, bwd_slot)

            def window(src, r, taps, b):
                span = n_sub + max(qi for _, qi in taps)
                return [_shifted_rows(src, r, b + SUBLANE * i, SUBLANE) for i in range(span)]

            def step(b, accs):
                accs = list(accs)
                dv = [based[pl.ds(b + SUBLANE * j, SUBLANE), :] for j in range(n_sub)]
                for r, taps in fwd.items():
                    win = window((base, sh), r, taps, b)
                    for k, qi in taps:
                        t = dv[0] * win[qi]
                        for j in range(1, n_sub):
                            t = t + dv[j] * win[qi + j]
                        accs[k] = accs[k] + t
                outs = [None] * n_sub
                for r, taps in bwd.items():
                    win = window((based, shd), r, taps, b)
                    for k, qi in taps:
                        wk = wbc[k]
                        for j in range(n_sub):
                            term = wk * win[qi + j]
                            outs[j] = term if outs[j] is None else outs[j] + term
                for j in range(n_sub):
                    tmp[pl.ds(b + SUBLANE * j, SUBLANE), :] = outs[j]
                return tuple(accs)

            return _row_loop(t_rows, WGRAD_ROWS, step, tuple(zeros(SUBLANE) for _ in range(k_taps)))

        def store_taps(ref, accs, rows):
            for k, acc in enumerate(accs):
                ref[pl.ds(k, 1), :] = jnp.sum(acc, axis=0, keepdims=True)
            if rows > len(accs):
                ref[pl.ds(len(accs), rows - len(accs)), :] = zeros(rows - len(accs))

        def fill_a(b, c):
            sl = pl.ds(b, CONV_CHUNK)
            base[pl.ds(CONV_HALO + b, CONV_CHUNK), :] = cg_ref[sl, :] * ha_ref[sl, :]
            based[sl, :] = dya_ref[sl, :] * bg_ref[sl, :]
            return c

        _row_loop(t_rows, CONV_CHUNK, fill_a)
        _fill_shifted(sh, base, sorted({fwd_slot(ka, k)[0] for k in range(ka)}), nr)
        _fill_shifted(shd, based, sorted({bwd_slot(ka, k)[0] for k in range(ka)}), nr)

        def d_bgate(b, c):
            sl = pl.ds(b, CONV_CHUNK)
            dh_ref[0, sl, :] = (dya_ref[sl, :] * conv(wa_ref, ka, (base, sh), fwd_slot, b, CONV_CHUNK)).astype(BF16)
            return c

        _row_loop(t_rows, CONV_CHUNK, d_bgate)
        store_taps(dwa_ref, wgrad_loop(wa_ref, ka), SUBLANE)

        def d_ch(b, c):
            sl = pl.ds(b, CONV_CHUNK)
            dua = tmp[sl, :]
            dh_ref[1, sl, :] = (dua * ha_ref[sl, :]).astype(BF16)
            dh_ref[2, sl, :] = (dua * cg_ref[sl, :]).astype(BF16)
            return c

        _row_loop(t_rows, CONV_CHUNK, d_ch)

        def fill_b(b, c):
            sl = pl.ds(b, CONV_CHUNK)
            base[pl.ds(CONV_HALO + b, CONV_CHUNK), :] = val_ref[sl, :] * jax.nn.sigmoid(gt_ref[sl, :])
            based[sl, :] = dz_ref[sl, :]
            return c

        _row_loop(t_rows, CONV_CHUNK, fill_b)
        _fill_shifted(sh, base, range(SUBLANE), nr)
        _fill_shifted(shd, based, range(SUBLANE), nr)
        store_taps(dwb_ref, wgrad_loop(wb_ref, kb), kb_rows)

        def d_glu(b, c):
            sl = pl.ds(b, CONV_CHUNK)
            dgg = tmp[sl, :]
            sig = jax.nn.sigmoid(gt_ref[sl, :])
            dh_ref[3, sl, :] = (dgg * sig).astype(BF16)
            dh_ref[4, sl, :] = (dgg * val_ref[sl, :] * (sig * (1.0 - sig))).astype(BF16)
            return c

        _row_loop(t_rows, CONV_CHUNK, d_glu)

    def col(g):
        return pl.BlockSpec((t_rows, LANE), lambda i, g=g: (0, g * nt + i))

    tile = lambda rows: pl.BlockSpec((rows, LANE), lambda i: (0, i))
    return _host_call(
        body, grid=(nt,),
        in_specs=[col(0), col(1), col(2), col(3), col(4), tile(t_rows), tile(t_rows), tile(ka), tile(kb)],
        out_specs=[pl.BlockSpec((5, t_rows, LANE), lambda i: (0, 0, i)), tile(SUBLANE), tile(kb_rows)],
        out_shape=[jax.ShapeDtypeStruct((5, t_rows, wa_w), BF16), jax.ShapeDtypeStruct((SUBLANE, wa_w), F32),
                   jax.ShapeDtypeStruct((kb_rows, wa_w), F32)],
        scratch_shapes=[pltpu.VMEM((nr + SUBLANE, LANE), F32), pltpu.VMEM((SUBLANE, nr, LANE), F32),
                        pltpu.VMEM((nr + SUBLANE, LANE), F32), pltpu.VMEM((SUBLANE, nr, LANE), F32),
                        pltpu.VMEM((t_rows, LANE), F32), pltpu.VMEM((kb_rows, SUBLANE, LANE), F32)],
        args=(hin, hin, hin, hin, hin, dy, dz, wa, wb), name="mix_conv_bwd", comm=comm)


def _bwd_in_proj(dh5, win_t, h0, dh1, g1, tm, comm):
    t_rows, d = h0.shape
    groups, _, w = dh5.shape

    def body(dh_ref, w_ref, h0_ref, dh1_ref, g1_ref, dh0_ref, dg1_ref):
        dxn1 = None
        for g in range(groups):
            part = _dot_nn(dh_ref[g], w_ref[pl.ds(g * w, w), :])
            dxn1 = part if dxn1 is None else dxn1 + part
        h0 = h0_ref[...]
        r1 = _rstd(h0)
        h0h = h0 * r1
        _acc_rows(dg1_ref, dxn1 * h0h, pl.program_id(0) == 0)
        dh0_ref[...] = dh1_ref[...] + _rms_bwd(dxn1 * g1_ref[...], h0h, r1)

    blk = pl.BlockSpec((tm, d), lambda i: (i, 0))
    return _host_call(
        body, grid=(t_rows // tm,),
        in_specs=[pl.BlockSpec((groups, tm, w), lambda i: (0, i, 0)), _resident(win_t.shape), blk, blk, _full((1, d))],
        out_specs=[blk, _full((1, d))],
        out_shape=[jax.ShapeDtypeStruct((t_rows, d), F32), jax.ShapeDtypeStruct((1, d), F32)],
        args=(dh5, win_t, h0, dh1, g1), name="bwd_in_proj", comm=comm)


def _reduce_small(smalls, d, after):
    (dmeta, dg1, dg2, dg3, dg4, dbb, dlg, dlb, lossv, dwa, dwb) = smalls
    half = d // 2
    kb_rows = dwb.shape[0]

    def body(dmeta_ref, dg1_ref, dg2_ref, dg3_ref, dg4_ref, dbb_ref, dlg_ref, dlb_ref, loss_ref, dwa_ref, dwb_ref,
             ptot_ref, pbuf, psib, chip_p, ps_send, ps_recv, pc_send, pc_recv):
        x, y, c = _mesh_pos()
        pbuf[...] = jnp.zeros_like(pbuf)
        pbuf[pl.ds(0, N_META), :] = dmeta_ref[...]
        for row, ref in ((16, dg1_ref), (17, dg2_ref), (18, dg3_ref), (19, dg4_ref)):
            pbuf[pl.ds(row, 1), :] = ref[...]
        pbuf[pl.ds(20, 1), pl.ds(0, half)] = dbb_ref[...]
        pbuf[pl.ds(20, 1), pl.ds(half, half)] = dlg_ref[...]
        pbuf[pl.ds(21, 1), pl.ds(0, half)] = dlb_ref[...]
        lv = loss_ref[...]
        pbuf[pl.ds(21, 1), pl.ds(half, half)] = lv[:, :half] + lv[:, half:]
        pbuf[pl.ds(24, SUBLANE), pl.ds(0, half)] = dwa_ref[...]
        pbuf[pl.ds(32, kb_rows), pl.ds(0, half)] = dwb_ref[...]
        to_sib = _remote(pbuf, psib, ps_send.at[0], ps_recv.at[0], (x, y, 1 - c))
        to_sib.start()
        to_sib.wait_recv()
        my_chip = 2 * x + y
        chip_p[my_chip] = pbuf[...] + psib[...]
        to_sib.wait_send()
        slot = chip_p.at[my_chip]
        cps = [_remote(slot, slot, pc_send.at[k], pc_recv.at[k], (*chip, c)) for k, chip in enumerate(_other_chips(x, y))]
        for cp in cps:
            cp.start()
        for cp in cps:
            cp.wait_recv()
        ptot_ref[...] = ((chip_p[0] + chip_p[1]) + chip_p[2]) + chip_p[3]
        for cp in cps:
            cp.wait_send()

    return _host_call(
        body, grid=(), in_specs=[VMEM] * 11, out_specs=[VMEM], out_shape=[jax.ShapeDtypeStruct((SMALL_ROWS, d), F32)],
        scratch_shapes=[pltpu.VMEM((SMALL_ROWS, d), F32), pltpu.VMEM((SMALL_ROWS, d), F32), pltpu.VMEM((4, SMALL_ROWS, d), F32),
                        pltpu.SemaphoreType.DMA((1,)), pltpu.SemaphoreType.DMA((1,)),
                        pltpu.SemaphoreType.DMA((3,)), pltpu.SemaphoreType.DMA((3,))],
        args=smalls, name="reduce_small", after=after)


def _adamw(w, g, m, v):
    m = ADAM_B1 * m + (1.0 - ADAM_B1) * g
    v = ADAM_B2 * v + (1.0 - ADAM_B2) * jnp.square(g)
    m_hat = m / (1.0 - ADAM_B1 ** ADAM_STEP)
    v_hat = v / (1.0 - ADAM_B2 ** ADAM_STEP)
    delta = -ADAM_LR * (m_hat / (jnp.sqrt(v_hat) + ADAM_EPS) + ADAM_WD * w)
    return delta, m, v


def _adam_big(g, pair, part, w, m, v, name):
    r, d = w.shape
    cols = d // ADAM_COL_BLOCKS

    def body(me_ref, g_ref, pair_ref, part_ref, w_ref, m_ref, v_ref, go_ref, d_ref, mo_ref, vo_ref):
        g = g_ref[...].astype(F32) + pair_ref[...].astype(F32)
        for k in range(3):
            g = g + part_ref[k].astype(F32)
        go_ref[...] = g
        d_ref[...], mo_ref[...], vo_ref[...] = _adamw(w_ref[...], g, m_ref[...], v_ref[...])

    blk = pl.BlockSpec((r, cols), lambda i, me_ref: (0, i))
    grid_spec = pltpu.PrefetchScalarGridSpec(
        num_scalar_prefetch=1, grid=(ADAM_COL_BLOCKS,),
        in_specs=[pl.BlockSpec((r, cols), lambda i, me_ref: (me_ref[0], i)),
                  pl.BlockSpec((None, r, cols), lambda i, me_ref: (0, 0, i)),
                  pl.BlockSpec((3, r, cols), lambda i, me_ref: (0, 0, i)), blk, blk, blk],
        out_specs=[blk, blk, blk, blk])
    me = jnp.reshape(_dev_index(*_mesh_pos()), (1,)).astype(jnp.int32)
    return pl.pallas_call(body, out_shape=[jax.ShapeDtypeStruct((r, d), F32)] * 4, grid_spec=grid_spec, name=name,
                          compiler_params=_cparams(1))(me, g, pair, part, w, m, v)


def _adam_small(gs, ws, ms, vs):
    n = len(gs)

    def body(*refs):
        ins, outs = refs[:4 * n], refs[4 * n:]
        for i in range(n):
            g = ins[i][...]
            delta, m, v = _adamw(ins[n + i][...], g, ins[2 * n + i][...], ins[3 * n + i][...])
            outs[i][...] = delta
            outs[n + i][...] = m
            outs[2 * n + i][...] = v

    shapes = [jax.ShapeDtypeStruct(w.shape, F32) for w in ws]
    return pl.pallas_call(body, out_shape=shapes * 3, name="adam_small", compiler_params=_cparams())(*gs, *ws, *ms, *vs)


def kernel(x, meta_tokens, pre_mix_norm, w_in, conv_a_w, conv_b_w, conv_b_bias, ln_b_gain, ln_b_bias, w_out, post_mix_norm, pre_ffn_norm, w_gate, w_up, w_down, post_ffn_norm, loss_target, m_meta_tokens, m_pre_mix_norm, m_w_in, m_conv_a_w, m_conv_b_w, m_conv_b_bias, m_ln_b_gain, m_ln_b_bias, m_w_out, m_post_mix_norm, m_pre_ffn_norm, m_w_gate, m_w_up, m_w_down, m_post_ffn_norm, v_meta_tokens, v_pre_mix_norm, v_w_in, v_conv_a_w, v_conv_b_w, v_conv_b_bias, v_ln_b_gain, v_ln_b_bias, v_w_out, v_post_mix_norm, v_pre_ffn_norm, v_w_gate, v_w_up, v_w_down, v_post_ffn_norm):
    _, seq, d = x.shape
    ka, ca_loc = conv_a_w.shape[1:]
    kb, cb_loc = conv_b_w.shape[1:]
    wa_w = ca_loc * N_DEV
    assert cb_loc == ca_loc and wa_w % LANE == 0 and w_in.shape[2] * N_DEV == 5 * wa_w and 2 * wa_w == d
    pad = (-(N_META + seq)) % ROW_ALIGN
    x0 = pad + N_META
    t_rows = x0 + seq
    assert t_rows % (N_ROW_BLOCKS * BF16_ROWS) == 0 and t_rows % CONV_CHUNK == 0 and d % LANE == 0
    tm = t_rows // N_ROW_BLOCKS
    me = _dev_index(*_mesh_pos())

    def as_rows(w_in_like, w_out_like, w_gate_like, w_up_like, w_down_like):
        return (w_in_like[0].T, w_out_like[0], w_gate_like[0].T, w_up_like[0].T, w_down_like[0])

    w_loc = as_rows(w_in, w_out, w_gate, w_up, w_down)
    rows = [w.shape[0] for w in w_loc]
    assert all(r % ADD_CHUNK == 0 for r in rows)
    P_IN, P_OUT, P_GATE, P_UP, P_DOWN = range(N_BIG)

    sm = jnp.zeros((SM_ROWS, LANE), F32)
    sm = sm.at[0:N_META, :].set(meta_tokens)
    sm = sm.at[16:16 + ka, 0:ca_loc].set(conv_a_w[0])
    sm = sm.at[24:24 + kb, 0:cb_loc].set(conv_b_w[0])
    wl, wfull, sm_all, h0, tgt = _gather_first(w_loc, sm, [(P_IN, 0, rows[P_IN])], x[0], loss_target[0], t_rows, x0)
    wa =jnp.transpose(sm_all[:, 16:16 + ka, 0:ca_loc], (1, 0, 2)).reshape(ka, wa_w)
    wb = jnp.transpose(sm_all[:, 24:24 + kb, 0:cb_loc], (1, 0, 2)).reshape(kb, wa_w)

    later = (P_OUT, P_GATE, P_UP, P_DOWN)
    sems, wl, started, _ = _gather_start(wl, wfull, later, rows, START_BARRIER_IDS[0])
    for p, arr in zip(later, started):
        wfull[p] = arr

    def arrived(p, after, name):
        nonlocal wl
        wl, wfull[p] = _gather_wait(wl, wfull[p], sems[later.index(p)], after, rows[p], name)
        return _forward_comm(wfull[p], rows[p])

    (xn1, hin), _ = _in_proj(h0, pre_mix_norm, wfull[P_IN], tm, None)
    (ya, z), (wfull[P_OUT],) = _mix_conv_fwd(hin, wa, wb, conv_b_bias, wa_w, arrived(P_OUT, hin, "gather_wait_out"))
    (y, mix, h1, xn2), (wfull[P_GATE],) = _out_proj(ya, z, ln_b_gain, ln_b_bias, wfull[P_OUT], h0, post_mix_norm, pre_ffn_norm, tm,
                                                    arrived(P_GATE, z, "gather_wait_gate"))
    arrived(P_UP, xn2, "gather_wait_up")
    wfull[P_UP] = _forward_now(wfull[P_UP], rows[P_UP], "forward_up")
    (a, u, s), _ = _gate_up(xn2, wfull[P_GATE], wfull[P_UP], tm, None)
    arrived(P_DOWN, s, "gather_wait_down")
    wfull[P_DOWN] = _forward_now(wfull[P_DOWN], rows[P_DOWN], "forward_down")
    dh2, dff, dg4, lossv = _down_loss(s, wfull[P_DOWN], h1, tgt, post_ffn_norm, tm, x0)

    gwd = _wgrad(s, dff, "wgrad_down")
    (da, du), (pair_d,) = _bwd_down(dff, wfull[P_DOWN], a, u, tm, _pair_comm(gwd, rows[P_DOWN]))
    (flight_d,), token = _chip_start([_pair_sum(gwd, pair_d, rows[P_DOWN], "pair_sum_down")], "chip_start_down", START_BARRIER_IDS[1])
    gwg = _wgrad(da, xn2, "wgrad_gate", [token])
    gwu = _wgrad(du, xn2, "wgrad_up")
    (dh1, dg3), (pair_g, pair_u) = _bwd_ffn_in(da, du, wfull[P_GATE], wfull[P_UP], h1, dh2, pre_ffn_norm, tm,
                                               _merge_comms([_pair_comm(gwg, rows[P_GATE]), _pair_comm(gwu, rows[P_UP])]))
    (flight_g, flight_u), token = _chip_start([_pair_sum(gwg, pair_g, rows[P_GATE], "pair_sum_gate"),
                                               _pair_sum(gwu, pair_u, rows[P_UP], "pair_sum_up")], "chip_start_gate_up",
                                              START_BARRIER_IDS[2])
    dmix, dya, dz, dg2, dlg, dlb, dbb = _bwd_out_proj(dh1, mix, wfull[P_OUT], post_mix_norm, z, ln_b_gain, ln_b_bias, tm, [token])
    gwo = _wgrad(y, dmix, "wgrad_out")
    (dh5, dwa, dwb), (pair_o,) = _mix_conv_bwd(hin, dya, dz, wa, wb, wa_w, _pair_comm(gwo, rows[P_OUT]))
    (flight_o,), token = _chip_start([_pair_sum(gwo, pair_o, rows[P_OUT], "pair_sum_out")], "chip_start_out", START_BARRIER_IDS[3])
    gwi = _wgrad(dh5, xn1, "wgrad_in", [token])
    (dh0, dg1), (pair_i,) = _bwd_in_proj(dh5, wfull[P_IN], h0, dh1, pre_mix_norm, tm, _pair_comm(gwi, rows[P_IN]))
    grad_x = dh0[x0:][None]
    dmeta = dh0[x0 - N_META:x0]
    (ptot,), _ = _reduce_small((dmeta, dg1, dg2, dg3, dg4, dbb, dlg, dlb, lossv, dwa, dwb), d, [])
    (flight_i,), token = _chip_start([_pair_sum(gwi, pair_i, rows[P_IN], "pair_sum_in", [ptot])], "chip_start_in",
                                     START_BARRIER_IDS[4])

    def landed(flight, after, tag):
        sems_p, sums, land = flight
        return _chip_wait(sums, land, sems_p, after, "chip_wait_" + tag)

    part_d = landed(flight_d, token, "down")
    part_g = landed(flight_g, token, "gate")
    part_u = landed(flight_u, token, "up")
    part_o = landed(flight_o, token, "out")

    half = d // 2
    loss = (0.5 / d) * jnp.sum(ptot[21, half:])
    g_meta = lax.dynamic_slice(ptot, (0, me * (d // N_DEV)), (N_META, d // N_DEV))
    g_small = [g_meta, ptot[16:17], lax.dynamic_slice(ptot, (24, me * ca_loc), (ka, ca_loc))[None],
               lax.dynamic_slice(ptot, (32, me * cb_loc), (kb, cb_loc))[None],
               ptot[20:21, :half], ptot[20:21, half:], ptot[21:22, :half], ptot[17:18], ptot[18:19], ptot[19:20]]
    w_small = [meta_tokens, pre_mix_norm, conv_a_w, conv_b_w, conv_b_bias, ln_b_gain, ln_b_bias, post_mix_norm,
               pre_ffn_norm, post_ffn_norm]
    m_small = [m_meta_tokens, m_pre_mix_norm, m_conv_a_w, m_conv_b_w, m_conv_b_bias, m_ln_b_gain, m_ln_b_bias,
               m_post_mix_norm, m_pre_ffn_norm, m_post_ffn_norm]
    v_small = [v_meta_tokens, v_pre_mix_norm, v_conv_a_w, v_conv_b_w, v_conv_b_bias, v_ln_b_gain, v_ln_b_bias,
               v_post_mix_norm, v_pre_ffn_norm, v_post_ffn_norm]
    small = _adam_small(g_small, w_small, m_small, v_small)
    n_small = len(w_small)
    d_small, nm_small, nv_small = small[:n_small], small[n_small:2 * n_small], small[2 * n_small:]

    m_loc = as_rows(m_w_in, m_w_out, m_w_gate, m_w_up, m_w_down)
    v_loc = as_rows(v_w_in, v_w_out, v_w_gate, v_w_up, v_w_down)
    full_grads = {P_IN: gwi, P_OUT: gwo, P_GATE: gwg, P_UP: gwu, P_DOWN: gwd}
    pairs = {P_IN: pair_i, P_OUT: pair_o, P_GATE: pair_g, P_UP: pair_u, P_DOWN: pair_d}
    parts = {P_OUT: part_o, P_GATE: part_g, P_UP: part_u, P_DOWN: part_d}
    names = {P_IN: "w_in", P_OUT: "w_out", P_GATE: "w_gate", P_UP: "w_up", P_DOWN: "w_down"}
    bigs = {}
    res = None
    for p in (P_DOWN, P_GATE, P_UP, P_OUT, P_IN):
        if p == P_IN:
            parts[p] = landed(flight_i, res[1], "in")
        res = _adam_big(full_grads[p], pairs[p], parts[p], w_loc[p], m_loc[p], v_loc[p], "adam_" + names[p])
        bigs[names[p]] = [(o.T if p in (P_IN, P_GATE, P_UP) else o)[None] for o in res]

    def ordered(pick_small, pick_big):
        sm_it = iter(range(n_small))
        out = []
        for name in ("s", "s", "w_in", "s", "s", "s", "s", "s", "w_out", "s", "s", "w_gate", "w_up", "w_down", "s"):
            out.append(pick_small(next(sm_it)) if name == "s" else pick_big(name))
        return out

    grads = ordered(lambda i: g_small[i], lambda n: bigs[n][0])
    deltas = ordered(lambda i: d_small[i], lambda n: bigs[n][1])
    new_m = ordered(lambda i: nm_small[i], lambda n: bigs[n][2])
    new_v = ordered(lambda i: nv_small[i], lambda n: bigs[n][3])
    return (loss, grad_x, *grads, *deltas, *new_m, *new_v)
```

```python
import jax
import jax.numpy as jnp
from jax import lax
from jax.experimental import pallas as pl
from jax.experimental.pallas import tpu as pltpu

F32 = jnp.float32
BF16 = jnp.bfloat16
MESH = pl.DeviceIdType.MESH

N_META = 16
N_DEV = 8
RMS_EPS = 1e-6
LN_EPS = 1e-5
ADAM_LR = 0.001
ADAM_B1 = 0.9
ADAM_B2 = 0.999
ADAM_EPS = 1e-08
ADAM_WD = 0.01
ADAM_STEP = 10

LANE = 128
SUBLANE = 8
BF16_ROWS = 16
ROW_ALIGN = 128
N_ROW_BLOCKS = 4
CONV_HALO = 32
CONV_CHUNK = 64
WGRAD_ROWS = 32
N_CHUNK = 512
WGRAD_TILE_MAX = 1408
ADD_CHUNK = 32
ADAM_COL_BLOCKS = 4
COPY_PIECES = 4
V7X_VMEM_BYTES = 64 * 1024 * 1024
VMEM_LIMIT = V7X_VMEM_BYTES - 6 * 1024 * 1024
SMALL_ROWS = 64
SM_ROWS = 56
N_BIG = 5

ANY = pl.BlockSpec(memory_space=pl.ANY)
VMEM = pl.BlockSpec(memory_space=pltpu.VMEM)


def _cparams(n_grid_axes=0):
    sem = ("arbitrary",) * n_grid_axes if n_grid_axes else None
    return pltpu.CompilerParams(dimension_semantics=sem, vmem_limit_bytes=VMEM_LIMIT)


def _mesh_pos():
    return lax.axis_index("x"), lax.axis_index("y"), lax.axis_index("c")


def _dev_index(px, py, pc):
    return 4 * px + 2 * py + pc


def _other_chips(x, y):
    return [(1 - x, y), (x, 1 - y), (1 - x, 1 - y)]


def _full(shape):
    return pl.BlockSpec(shape, lambda *_: (0,) * len(shape))


def _resident(shape):
    return pl.BlockSpec(shape, lambda *_: (0,) * len(shape), pipeline_mode=pl.Buffered(1))


def _dot_nt(a, w):
    return lax.dot_general(a, w, (((1,), (1,)), ((), ())), preferred_element_type=F32)


def _dot_nn(a, w):
    return jnp.dot(a, w, preferred_element_type=F32)


def _chunks(n, c):
    out, o = [], 0
    while o < n:
        out.append((o, min(c, n - o)))
        o += c
    return out


def _rstd(h):
    return lax.rsqrt(jnp.mean(h * h, axis=-1, keepdims=True) + RMS_EPS)


def _rms_bwd(dyh, yh, r):
    return r * (dyh - yh * jnp.mean(dyh * yh, axis=-1, keepdims=True))


def _silu_grad(a, sig):
    return sig * (1.0 + a * (1.0 - sig))


def _acc_rows(ref, val, first):
    s = jnp.sum(val, axis=0, keepdims=True)

    @pl.when(first)
    def _():
        ref[...] = s

    @pl.when(jnp.logical_not(first))
    def _():
        ref[...] += s


def _row_loop(t_rows, chunk, fn, carry=None):
    def step(i, c):
        return fn(pl.multiple_of(i * chunk, chunk), c)

    return lax.fori_loop(0, t_rows // chunk, step, carry)


def _remote(src, dst, send_sem, recv_sem, to):
    return pltpu.make_async_remote_copy(src_ref=src, dst_ref=dst, send_sem=send_sem, recv_sem=recv_sem,
                                        device_id=to, device_id_type=MESH)


class _SplitRemote:
    def __init__(self, src, dst, send_sem, recv_sem, to, rows, n_chunks):
        units = rows // BF16_ROWS
        n_chunks = max(1, min(n_chunks, units))
        sizes = [(units // n_chunks + (i < units % n_chunks)) * BF16_ROWS for i in range(n_chunks)]
        self.whole = _remote(src, dst, send_sem, recv_sem, to)
        self.parts, o = [], 0
        for n in sizes:
            self.parts.append(_remote(src.at[pl.ds(o, n), :], dst.at[pl.ds(o, n), :], send_sem, recv_sem, to))
            o += n

    def start(self):
        for cp in self.parts:
            cp.start()

    def wait_recv(self):
        self.whole.wait_recv()

    def wait_send(self):
        self.whole.wait_send()


class _Comm:
    def __init__(self, inputs, out_shapes, aliases, scratch, start, finish):
        self.inputs, self.out_shapes, self.aliases, self.scratch = list(inputs), list(out_shapes), dict(aliases), list(scratch)
        self.start, self.finish = start, finish


def _merge_comms(comms):
    inputs, out_shapes, aliases, scratch, spans = [], [], {}, [], []
    for cm in comms:
        spans.append((len(inputs), len(out_shapes), len(scratch), cm))
        aliases.update({len(inputs) + k: len(out_shapes) + v for k, v in cm.aliases.items()})
        inputs += cm.inputs
        out_shapes += cm.out_shapes
        scratch += cm.scratch

    def run(which):
        def fn(ins, outs, scr):
            for i0, o0, s0, cm in spans:
                getattr(cm, which)(ins[i0:i0 + len(cm.inputs)], outs[o0:o0 + len(cm.out_shapes)], scr[s0:s0 + len(cm.scratch)])
        return fn

    return _Comm(inputs, out_shapes, aliases, scratch, run("start"), run("finish"))


def _host_call(body, *, grid, in_specs, out_specs, out_shape, args, name, scratch_shapes=(), comm=None, after=()):
    talks = comm is not None
    if comm is None:
        comm = _Comm([], [], {}, [], lambda *_: None, lambda *_: None)
    n_in, n_out, n_scr = len(args), len(out_shape), len(scratch_shapes)
    c_in, c_out = len(comm.inputs), len(comm.out_shapes)
    n_after = len(after)

    def open_comm(c_ins, c_outs, c_scr):
        if talks:
            _pair_handshake()
        comm.start(c_ins, c_outs, c_scr)

    def hosted(*refs):
        ins, c_ins = refs[:n_in], refs[n_in:n_in + c_in]
        o0 = n_in + c_in + n_after
        outs, c_outs = refs[o0:o0 + n_out], refs[o0 + n_out:o0 + n_out + c_out]
        s0 = o0 + n_out + c_out
        scr, c_scr = refs[s0:s0 + n_scr], refs[s0 + n_scr:]
        if not grid:
            open_comm(c_ins, c_outs, c_scr)
            body(*ins, *outs, *scr)
            comm.finish(c_ins, c_outs, c_scr)
            return
        first = last = None
        for a, n in enumerate(grid):
            f, l = pl.program_id(a) == 0, pl.program_id(a) == n - 1
            first = f if first is None else jnp.logical_and(first, f)
            last = l if last is None else jnp.logical_and(last, l)

        @pl.when(first)
        def _():
            open_comm(c_ins, c_outs, c_scr)

        body(*ins, *outs, *scr)

        @pl.when(last)
        def _():
            comm.finish(c_ins, c_outs, c_scr)

    sem = ("arbitrary",) * len(grid) if grid else None
    params = pltpu.CompilerParams(dimension_semantics=sem, vmem_limit_bytes=VMEM_LIMIT,
                                  collective_id=PAIR_BARRIER_ID if talks else None)
    res = pl.pallas_call(
        hosted, grid=grid, in_specs=list(in_specs) + [ANY] * (c_in + n_after), out_specs=list(out_specs) + [ANY] * c_out,
        out_shape=list(out_shape) + comm.out_shapes, scratch_shapes=list(scratch_shapes) + comm.scratch,
        input_output_aliases={n_in + k: n_out + v for k, v in comm.aliases.items()},
        name=name, compiler_params=params)(*args, *comm.inputs, *after)
    return list(res[:n_out]), list(res[n_out:])


PAIR_BARRIER_ID = 0
START_BARRIER_IDS = (1, 2, 3, 4, 5, 6)


def _chips_handshake():
    x, y, c = _mesh_pos()
    barrier = pltpu.get_barrier_semaphore()
    for chip in _other_chips(x, y):
        pl.semaphore_signal(barrier, inc=1, device_id=(*chip, c), device_id_type=MESH)
    pl.semaphore_wait(barrier, 3)


def _pair_handshake():
    x, y, c = _mesh_pos()
    barrier = pltpu.get_barrier_semaphore()
    pl.semaphore_signal(barrier, inc=1, device_id=(x, y, 1 - c), device_id_type=MESH)
    pl.semaphore_wait(barrier, 1)


GATHER_SEMS = 10
D2D_CHUNKS = 8


class _Gather:
    def __init__(self, jobs, rows, lo, src_ref, dests, send_sems, recv_sems):
        x, y, c = _mesh_pos()
        me, sib = (x, y, c), (x, y, 1 - c)
        nx, ny, dg = (1 - x, y, c), (x, 1 - y, c), (1 - x, 1 - y, c)
        self.relayed, self.direct, self.relay, self.to_sib, self.sib_fwd = [], [], [], [], []
        for n, (p, r0, nr) in enumerate(jobs):
            assert nr % (2 * BF16_ROWS) == 0
            half = nr // 2

            def rows_of(dev, h, p=p, r0=r0, nr=nr, half=half):
                off, cnt = (r0, nr) if h is None else (r0 + h * half, half)
                return dests[p].at[pl.ds(pl.multiple_of(_dev_index(*dev) * rows[p] + off, BF16_ROWS), cnt), :]

            def mine(h, p=p, r0=r0, nr=nr, half=half):
                off, cnt = (r0, nr) if h is None else (r0 + h * half, half)
                return src_ref.at[pl.ds(lo[p] + off, cnt), :]

            sem = lambda k, n=n: (send_sems.at[GATHER_SEMS * n + k], recv_sems.at[GATHER_SEMS * n + k])
            self.relayed.append([_remote(mine(0), rows_of(me, 0), *sem(0), nx), _remote(mine(1), rows_of(me, 1), *sem(3), ny)])
            self.direct.append([_remote(mine(1), rows_of(me, 1), *sem(1), nx), _remote(mine(0), rows_of(me, 0), *sem(2), ny)])
            self.relay.append([_remote(rows_of(nx, 0), rows_of(nx, 0), *sem(4), ny), _remote(rows_of(ny, 1), rows_of(ny, 1), *sem(5), nx)])
            self.to_sib.append(_SplitRemote(mine(None), rows_of(me, None), *sem(6), sib, nr, D2D_CHUNKS))
            self.sib_fwd.append([_SplitRemote(rows_of(dev, None), rows_of(dev, None), *sem(7 + i), sib, nr, D2D_CHUNKS)
                                 for i, dev in enumerate((nx, ny, dg))])

    def start(self):
        for group in (self.relayed, self.direct):
            for cps in group:
                for cp in cps:
                    cp.start()
        for cp in self.to_sib:
            cp.start()

    def mid(self):
        for first, relay in zip(self.relayed, self.relay):
            for arrived, onward in zip(first, relay):
                arrived.wait_recv()
                onward.start()

    def finish(self):
        for direct, relay, fwd in zip(self.direct, self.relay, self.sib_fwd):
            for k in range(2):
                direct[k].wait_recv()
                fwd[k].start()
            for cp in relay:
                cp.wait_recv()
            fwd[2].start()
        for n in range(len(self.to_sib)):
            self.to_sib[n].wait_recv()
            for cp in self.sib_fwd[n]:
                cp.wait_recv()
            for cp in self.relayed[n] + self.direct[n] + self.relay[n] + [self.to_sib[n]] + self.sib_fwd[n]:
                cp.wait_send()


HBM = pl.BlockSpec(memory_space=pltpu.HBM)
SEM = pl.BlockSpec(memory_space=pltpu.SEMAPHORE)
FLOWS = pltpu.SideEffectType.DATAFLOW_SIDE_EFFECTING


def _in_hbm(a):
    return pltpu.with_memory_space_constraint(a, pltpu.HBM)


def _gather_start(wl, dests, ps, rows, barrier_id):
    lo = [sum(rows[:p]) for p in range(N_BIG)]
    n = len(ps)

    def body(*refs):
        wl_ref, dest_refs = refs[0], refs[1:1 + n]
        sends, recvs = refs[1 + n:1 + 2 * n], refs[1 + 2 * n:1 + 3 * n]
        token = refs[-1]
        _chips_handshake()
        x, y, c = _mesh_pos()
        jme = _dev_index(x, y, c)
        for i, p in enumerate(ps):
            mine = dest_refs[i].at[pl.ds(pl.multiple_of(jme * rows[p], BF16_ROWS), rows[p]), :]
            for chip in _other_chips(x, y):
                _remote(wl_ref.at[pl.ds(lo[p], rows[p]), :], mine, sends[i], recvs[i], (*chip, c)).start()
        token[...] = jnp.zeros_like(token)

    thru = [pltpu.HBM(wl.shape, wl.dtype)] + [pltpu.HBM(dests[p].shape, BF16) for p in ps]
    res = pl.pallas_call(
        body, name="gather_start",
        out_shape=tuple([pltpu.SemaphoreType.DMA(())] * (2 * n) + thru + [jax.ShapeDtypeStruct((SUBLANE, LANE), F32)]),
        in_specs=[HBM] * (1 + n), out_specs=tuple([SEM] * (2 * n) + [HBM] * (1 + n) + [VMEM]),
        input_output_aliases={i: 2 * n + i for i in range(1 + n)},
        compiler_params=pltpu.CompilerParams(has_side_effects=FLOWS, collective_id=barrier_id))(
            _in_hbm(wl), *[_in_hbm(dests[p]) for p in ps])
    sems = [(res[i], res[n + i]) for i in range(n)]
    return sems, res[2 * n], list(res[2 * n + 1:3 * n + 1]), res[-1]


def _gather_wait(wl, dest, sems, after, r, name):
    def body(wl_ref, dest_ref, send_sem, recv_sem, after_ref, wl_out, dest_out):
        x, y, c = _mesh_pos()
        three = dest_ref.at[pl.ds(0, 3 * r), :]
        cp = _remote(three, three, send_sem, recv_sem, (x, y, 1 - c))
        cp.wait_send()
        cp.wait_recv()

    res = pl.pallas_call(
        body, name=name, out_shape=(pltpu.HBM(wl.shape, wl.dtype), pltpu.HBM(dest.shape, dest.dtype)),
        in_specs=[HBM, HBM, SEM, SEM, ANY], out_specs=(HBM, HBM), input_output_aliases={0: 0, 1: 1},
        compiler_params=pltpu.CompilerParams(has_side_effects=FLOWS))(wl, dest, sems[0], sems[1], after)
    return res[0], res[1]


def _forward_comm(dest, r):
    def descs(ins, outs, scr):
        x, y, c = _mesh_pos()
        cps = []
        for k, chip in enumerate(_other_chips(x, y)):
            blk = outs[0].at[pl.ds(pl.multiple_of(_dev_index(*chip, c) * r, BF16_ROWS), r), :]
            cps.append(_SplitRemote(blk, blk, scr[0].at[k], scr[1].at[k], (x, y, 1 - c), r, D2D_CHUNKS))
        return cps

    def start(ins, outs, scr):
        for cp in descs(ins, outs, scr):
            cp.start()

    def finish(ins, outs, scr):
        cps = descs(ins, outs, scr)
        for cp in cps:
            cp.wait_recv()
        for cp in cps:
            cp.wait_send()

    return _Comm([dest], [jax.ShapeDtypeStruct(dest.shape, dest.dtype)], {0: 0},
                 [pltpu.SemaphoreType.DMA((3,)), pltpu.SemaphoreType.DMA((3,))], start, finish)


def _forward_now(dest, r, name):
    _, (dest,) = _host_call(lambda: None, grid=(), in_specs=[], out_specs=[], out_shape=[], args=(), name=name,
                            comm=_forward_comm(dest, r))
    return dest


def _pair_comm(g, r):
    d = g.shape[1]

    def descs(ins, outs, scr):
        x, y, c = _mesh_pos()
        chips = [(x, y)] + _other_chips(x, y)
        return [_SplitRemote(ins[0].at[pl.ds(pl.multiple_of(_dev_index(*chip, 1 - c) * r, BF16_ROWS), r), :], outs[0].at[k],
                             scr[0].at[k], scr[1].at[k], (x, y, 1 - c), r, D2D_CHUNKS) for k, chip in enumerate(chips)]

    def start(ins, outs, scr):
        for cp in descs(ins, outs, scr):
            cp.start()

    def finish(ins, outs, scr):
        cps = descs(ins, outs, scr)
        for cp in cps:
            cp.wait_recv()
        for cp in cps:
            cp.wait_send()

    comm = _Comm([g], [jax.ShapeDtypeStruct((4, r, d), BF16)], {},
                 [pltpu.SemaphoreType.DMA((4,)), pltpu.SemaphoreType.DMA((4,))], start, finish)
    return comm


def _pair_sum(g, pair, r, name, after=()):
    d = g.shape[1]

    def body(g_ref, p_ref, *rest):
        o_ref, gbuf, pbuf, sems = rest[len(after):]
        x, y, c = _mesh_pos()
        loads = [pltpu.make_async_copy(p_ref.at[pl.ds(1, 3)], pbuf, sems.at[3])]
        for k, chip in enumerate(_other_chips(x, y)):
            j = _dev_index(*chip, c)
            loads.append(pltpu.make_async_copy(g_ref.at[pl.ds(pl.multiple_of(j * r, BF16_ROWS), r), :], gbuf.at[k], sems.at[k]))
        for cp in loads:
            cp.start()
        for cp in loads:
            cp.wait()
        for k in range(3):
            o_ref[k] = (gbuf[k].astype(F32) + pbuf[k].astype(F32)).astype(BF16)

    return pl.pallas_call(
        body, out_shape=jax.ShapeDtypeStruct((3, r, d), BF16), in_specs=[ANY] * (2 + len(after)), out_specs=VMEM,
        scratch_shapes=[pltpu.VMEM((3, r, d), BF16), pltpu.VMEM((3, r, d), BF16), pltpu.SemaphoreType.DMA((4,))],
        name=name, compiler_params=_cparams())(g, pair, *after)


def _chip_start(sums, name, barrier_id):
    n = len(sums)

    def body(*refs):
        srcs, lands = refs[:n], refs[n:2 * n]
        sends, recvs = refs[2 * n:3 * n], refs[3 * n:4 * n]
        _chips_handshake()
        x, y, c = _mesh_pos()
        for i in range(n):
            for k, chip in enumerate(_other_chips(x, y)):
                _remote(srcs[i].at[k], lands[i].at[k], sends[i], recvs[i], (*chip, c)).start()
        refs[-1][...] = jnp.zeros_like(refs[-1])

    zones = [pltpu.HBM(s.shape, s.dtype) for s in sums]
    res = pl.pallas_call(
        body, name=name,
        out_shape=tuple([pltpu.SemaphoreType.DMA(())] * (2 * n) + zones + zones + [jax.ShapeDtypeStruct((SUBLANE, LANE), F32)]),
        in_specs=[HBM] * (2 * n), out_specs=tuple([SEM] * (2 * n) + [HBM] * (2 * n) + [VMEM]),
        input_output_aliases={i: 2 * n + i for i in range(2 * n)},
        compiler_params=pltpu.CompilerParams(has_side_effects=FLOWS, collective_id=barrier_id))(
            *[_in_hbm(s) for s in sums], *[_in_hbm(lax.empty(s.shape, s.dtype)) for s in sums])
    flights = [((res[i], res[n + i]), res[2 * n + i], res[3 * n + i]) for i in range(n)]
    return flights, res[-1]


def _chip_wait(sums, land, sems, after, name):
    def body(sums_ref, land_ref, send_sem, recv_sem, after_ref, sums_out, land_out):
        x, y, c = _mesh_pos()
        cp = _remote(sums_ref, land_ref, send_sem, recv_sem, (x, y, 1 - c))
        cp.wait_send()
        cp.wait_recv()

    res = pl.pallas_call(
        body, name=name, out_shape=(pltpu.HBM(sums.shape, sums.dtype), pltpu.HBM(land.shape, land.dtype)),
        in_specs=[HBM, HBM, SEM, SEM, ANY], out_specs=(HBM, HBM), input_output_aliases={0: 0, 1: 1},
        compiler_params=pltpu.CompilerParams(has_side_effects=FLOWS))(sums, land, sems[0], sems[1], after)
    return res[1]


class _CopyThrough:
    def __init__(self, src_ref, dst_ref, dst_row0, n_rows, buf, sem_in, sem_out):
        rc = n_rows // COPY_PIECES
        piece = lambda ref, o: ref.at[pl.ds(o, rc), :]
        self.loads = [pltpu.make_async_copy(piece(src_ref, k * rc), piece(buf, k * rc), sem_in) for k in range(COPY_PIECES)]
        self.stores = [pltpu.make_async_copy(piece(buf, k * rc), piece(dst_ref, dst_row0 + k * rc), sem_out) for k in range(COPY_PIECES)]
        self.all_in = pltpu.make_async_copy(src_ref, buf, sem_in)
        self.all_out = pltpu.make_async_copy(buf, dst_ref.at[pl.ds(dst_row0, n_rows), :], sem_out)

    def load(self):
        for cp in self.loads:
            cp.start()

    def store(self):
        self.all_in.wait()
        for cp in self.stores:
            cp.start()

    def done(self):
        self.all_out.wait()


def _gather_first(shards, sm, jobs, x2, tgt2, t_rows, x0):
    d = shards[0].shape[1]
    rows = [w.shape[0] for w in shards]
    lo = [sum(rows[:p]) for p in range(N_BIG)]
    n_sems = GATHER_SEMS * len(jobs)
    seq = x2.shape[0]
    assert x0 == ROW_ALIGN and seq % ROW_ALIGN == 0 and d == N_DEV * LANE

    def body(s0, s1, s2, s3, s4, sm_ref, x_ref, tgt_ref, wl_ref, o0, o1, o2, o3, o4, sa_ref, h0_ref, tp_ref,
             wl_v, x_v, tgt_v, heads_v, sa_v, send_sems, recv_sems, ssend, srecv, local_sems, sems_in, sems_out):
        dests = (o0, o1, o2, o3, o4)
        x, y, c = _mesh_pos()
        me = (x, y, c)
        jme = _dev_index(*me)
        padded = [_CopyThrough(x_ref, h0_ref, x0, seq, x_v, sems_in.at[0], sems_out.at[0]),
                  _CopyThrough(tgt_ref, tp_ref, x0, seq, tgt_v, sems_in.at[1], sems_out.at[1])]
        for cp in padded:
            cp.load()
        shard_refs = (s0, s1, s2, s3, s4)
        first = sorted({j[0] for j in jobs})
        for p in first + [p for p in range(N_BIG) if p not in first]:
            wl_v[pl.ds(lo[p], rows[p]), :] = shard_refs[p][...].astype(BF16)
            if p == first[-1]:
                gather = _Gather(jobs, rows, lo, wl_v, dict(enumerate(dests)), send_sems, recv_sems)
                gather.start()
        peers = [(x, y, 1 - c)] + [(*chip, pc) for pc in (c, 1 - c) for chip in _other_chips(x, y)]
        smalls = [_remote(sm_ref, sa_ref.at[jme], ssend.at[k], srecv.at[k], to) for k, to in enumerate(peers)]
        for cp in smalls:
            cp.start()
        mine = [pltpu.make_async_copy(wl_v.at[pl.ds(lo[p], rows[p]), :],
                                      dests[p].at[pl.ds(pl.multiple_of(jme * rows[p], BF16_ROWS), rows[p]), :], local_sems.at[p])
                for p in range(N_BIG)]
        mine.append(pltpu.make_async_copy(wl_v, wl_ref, local_sems.at[N_BIG]))
        mine.append(pltpu.make_async_copy(sm_ref, sa_ref.at[jme], local_sems.at[N_BIG + 1]))
        for cp in mine:
            cp.start()
        later = [p for p in range(N_BIG) if p not in {j[0] for j in jobs}]
        own = [_SplitRemote(wl_v.at[pl.ds(lo[p], rows[p]), :],
                            dests[p].at[pl.ds(pl.multiple_of(jme * rows[p], BF16_ROWS), rows[p]), :],
                            ssend.at[7 + i], srecv.at[7 + i], (x, y, 1 - c), rows[p], D2D_CHUNKS) for i, p in enumerate(later)]
        for cp in own:
            cp.start()
        for cp in padded:
            cp.store()
        gather.mid()
        for cp in smalls + own:
            cp.wait_recv()
        mine[-1].wait()
        to_v = pltpu.make_async_copy(sa_ref, sa_v, local_sems.at[N_BIG + 1])
        to_v.start()
        to_v.wait()
        head, zeros = heads_v.at[0], heads_v.at[1]
        head[...] = jnp.zeros_like(head)
        zeros[...] = jnp.zeros_like(zeros)
        for j in range(N_DEV):
            head[pl.ds(x0 - N_META, N_META), pl.ds(j * LANE, LANE)] = sa_v[j, pl.ds(0, N_META), :]
        heads = [pltpu.make_async_copy(head, h0_ref.at[pl.ds(0, x0), :], local_sems.at[N_BIG + 1]),
                 pltpu.make_async_copy(zeros, tp_ref.at[pl.ds(0, x0), :], local_sems.at[N_BIG + 2])]
        for cp in heads:
            cp.start()
        gather.finish()
        for cp in smalls + own:
            cp.wait_send()
        for cp in mine[:-1] + heads:
            cp.wait()
        for cp in padded:
            cp.done()

    out_shape = [jax.ShapeDtypeStruct((sum(rows), d), BF16)]
    out_shape += [jax.ShapeDtypeStruct((N_DEV * r, d), BF16) for r in rows]
    out_shape.append(jax.ShapeDtypeStruct((N_DEV,) + sm.shape, F32))
    out_shape += [jax.ShapeDtypeStruct((t_rows, d), F32)] * 2
    res = pl.pallas_call(
        body, out_shape=out_shape, in_specs=[VMEM] * 6 + [ANY] * 2, out_specs=[ANY] * 9,
        scratch_shapes=[pltpu.VMEM((sum(rows), d), BF16), pltpu.VMEM((seq, d), F32), pltpu.VMEM((seq, d), F32),
                        pltpu.VMEM((2, ROW_ALIGN, d), F32), pltpu.VMEM((N_DEV,) + sm.shape, F32),
                        pltpu.SemaphoreType.DMA((n_sems,)), pltpu.SemaphoreType.DMA((n_sems,)),
                        pltpu.SemaphoreType.DMA((7 + N_BIG,)), pltpu.SemaphoreType.DMA((7 + N_BIG,)),
                        pltpu.SemaphoreType.DMA((N_BIG + 3,)), pltpu.SemaphoreType.DMA((2,)), pltpu.SemaphoreType.DMA((2,))],
        name="gather_first", compiler_params=_cparams())(*shards, sm, x2, tgt2)
    return res[0], list(res[1:1 + N_BIG]), res[1 + N_BIG], res[2 + N_BIG], res[3 + N_BIG]


def _in_proj(h0, g1, win_t, tm, comm):
    t_rows, d = h0.shape
    e = win_t.shape[0]

    def body(h_ref, g_ref, w_ref, xn_ref, hin_ref):
        h = h_ref[...]
        xn = ((h * _rstd(h)) * g_ref[...]).astype(BF16)
        xn_ref[...] = xn
        for o, n in _chunks(e, N_CHUNK):
            hin_ref[:, pl.ds(o, n)] = _dot_nt(xn, w_ref[pl.ds(o, n), :])

    return _host_call(
        body, grid=(t_rows // tm,),
        in_specs=[pl.BlockSpec((tm, d), lambda i: (i, 0)), _full((1, d)), _resident((e, d))],
        out_specs=[pl.BlockSpec((tm, d), lambda i: (i, 0)), pl.BlockSpec((tm, e), lambda i: (i, 0))],
        out_shape=[jax.ShapeDtypeStruct((t_rows, d), BF16), jax.ShapeDtypeStruct((t_rows, e), F32)],
        args=(h0, g1, win_t), name="in_proj", comm=comm)


def _tap_slot(off):
    return off % SUBLANE, (off // SUBLANE) * SUBLANE


def _fill_shifted(sh_ref, base_ref, residues, n_rows):
    for r in residues:
        if r:
            sh_ref[r] = base_ref[pl.ds(r, n_rows), :]


def _shifted_rows(pair, r, start, n):
    base_ref, sh_ref = pair
    return base_ref[pl.ds(start, n), :] if r == 0 else sh_ref[r, pl.ds(start, n), :]


def _mix_conv_fwd(hin, wa, wb, bb, wa_w, comm):
    t_rows = hin.shape[0]
    nt = wa_w // LANE
    ka, kb = wa.shape[0], wb.shape[0]
    nr = CONV_HALO + t_rows

    def body(bg_ref, cg_ref, ha_ref, val_ref, gt_ref, wa_ref, wb_ref, bb_ref, ya_ref, z_ref, base, sh):
        base[pl.ds(0, CONV_HALO), :] = jnp.zeros((CONV_HALO, LANE), F32)
        base[pl.ds(nr, SUBLANE), :] = jnp.zeros((SUBLANE, LANE), F32)

        def conv(w_ref, k_taps, b, n):
            acc = None
            for k in range(k_taps):
                r, q = _tap_slot(CONV_HALO - (k_taps - 1) + k)
                term = w_ref[pl.ds(k, 1), :] * _shifted_rows((base, sh), r, b + q, n)
                acc = term if acc is None else acc + term
            return acc

        def fill_a(b, c):
            base[pl.ds(CONV_HALO + b, CONV_CHUNK), :] = cg_ref[pl.ds(b, CONV_CHUNK), :] * ha_ref[pl.ds(b, CONV_CHUNK), :]
            return c

        _row_loop(t_rows, CONV_CHUNK, fill_a)
        _fill_shifted(sh, base, sorted({_tap_slot(CONV_HALO - (ka - 1) + k)[0] for k in range(ka)}), nr)

        def out_a(b, c):
            ya_ref[pl.ds(b, CONV_CHUNK), :] = (bg_ref[pl.ds(b, CONV_CHUNK), :] * conv(wa_ref, ka, b, CONV_CHUNK)).astype(BF16)
            return c

        _row_loop(t_rows, CONV_CHUNK, out_a)

        def fill_b(b, c):
            base[pl.ds(CONV_HALO + b, CONV_CHUNK), :] = (val_ref[pl.ds(b, CONV_CHUNK), :]
                                                          * jax.nn.sigmoid(gt_ref[pl.ds(b, CONV_CHUNK), :]))
            return c

        _row_loop(t_rows, CONV_CHUNK, fill_b)
        _fill_shifted(sh, base, range(SUBLANE), nr)

        def out_b(b, c):
            z_ref[pl.ds(b, CONV_CHUNK), :] = conv(wb_ref, kb, b, CONV_CHUNK) + bb_ref[...]
            return c

        _row_loop(t_rows, CONV_CHUNK, out_b)

    def col(g):
        return pl.BlockSpec((t_rows, LANE), lambda i, g=g: (0, g * nt + i))

    tile = lambda rows: pl.BlockSpec((rows, LANE), lambda i: (0, i))
    return _host_call(
        body, grid=(nt,),
        in_specs=[col(0), col(1), col(2), col(3), col(4), tile(ka), tile(kb), tile(1)],
        out_specs=[tile(t_rows), tile(t_rows)],
        out_shape=[jax.ShapeDtypeStruct((t_rows, wa_w), BF16), jax.ShapeDtypeStruct((t_rows, wa_w), F32)],
        scratch_shapes=[pltpu.VMEM((nr + SUBLANE, LANE), F32), pltpu.VMEM((SUBLANE, nr, LANE), F32)],
        args=(hin, hin, hin, hin, hin, wa, wb, bb), name="mix_conv_fwd", comm=comm)


def _ln_parts(z, lg, lb):
    mu = jnp.mean(z, axis=-1, keepdims=True)
    zc = z - mu
    rstd = lax.rsqrt(jnp.mean(zc * zc, axis=-1, keepdims=True) + LN_EPS)
    zh = zc * rstd
    return zh, rstd, zh * lg + lb


def _out_proj(ya, z, lg, lb, w_out, h0, g2, g3, tm, comm):
    t_rows, d = h0.shape
    w = z.shape[1]

    def body(ya_ref, z_ref, lg_ref, lb_ref, w_ref, h0_ref, g2_ref, g3_ref, y_ref, mix_ref, h1_ref, xn2_ref):
        _, _, ln = _ln_parts(z_ref[...], lg_ref[...], lb_ref[...])
        y_ref[:, pl.ds(0, w)] = ya_ref[...]
        y_ref[:, pl.ds(w, w)] = (ln * jax.nn.sigmoid(ln)).astype(BF16)
        mix = _dot_nn(y_ref[...], w_ref[...])
        mix_ref[...] = mix
        h1 = h0_ref[...] + (mix * _rstd(mix)) * g2_ref[...]
        h1_ref[...] = h1
        xn2_ref[...] = ((h1 * _rstd(h1)) * g3_ref[...]).astype(BF16)

    blk = pl.BlockSpec((tm, d), lambda i: (i, 0))
    half = pl.BlockSpec((tm, w), lambda i: (i, 0))
    return _host_call(
        body, grid=(t_rows // tm,),
        in_specs=[half, half, _full((1, w)), _full((1, w)), _resident(w_out.shape), blk, _full((1, d)), _full((1, d))],
        out_specs=[blk, blk, blk, blk],
        out_shape=[jax.ShapeDtypeStruct((t_rows, d), BF16), jax.ShapeDtypeStruct((t_rows, d), F32),
                   jax.ShapeDtypeStruct((t_rows, d), F32), jax.ShapeDtypeStruct((t_rows, d), BF16)],
        args=(ya, z, lg, lb, w_out, h0, g2, g3), name="out_proj", comm=comm)


def _gate_up(xn2, wg_t, wu_t, tm, comm):
    t_rows, d = xn2.shape
    f = wg_t.shape[0]

    def body(x_ref, wg_ref, wu_ref, a_ref, u_ref, s_ref):
        xn = x_ref[...]
        for o, n in _chunks(f, N_CHUNK):
            a = _dot_nt(xn, wg_ref[pl.ds(o, n), :])
            u = _dot_nt(xn, wu_ref[pl.ds(o, n), :])
            a_ref[:, pl.ds(o, n)] = a.astype(BF16)
            u_ref[:, pl.ds(o, n)] = u.astype(BF16)
            s_ref[:, pl.ds(o, n)] = ((a * jax.nn.sigmoid(a)) * u).astype(BF16)

    blk = pl.BlockSpec((tm, f), lambda i: (i, 0))
    return _host_call(
        body, grid=(t_rows // tm,),
        in_specs=[pl.BlockSpec((tm, d), lambda i: (i, 0)), _resident((f, d)), _resident((f, d))],
        out_specs=[blk, blk, blk], out_shape=[jax.ShapeDtypeStruct((t_rows, f), BF16)] * 3,
        args=(xn2, wg_t, wu_t), name="gate_up", comm=comm)


def _down_loss(s, wd, h1, tgt, g4, tm, x0):
    t_rows, d = h1.shape
    f = wd.shape[0]

    def body(s_ref, w_ref, h1_ref, tgt_ref, g4_ref, dh2_ref, dff_ref, dg4_ref, loss_ref):
        i = pl.program_id(0)
        ff = _dot_nn(s_ref[...], w_ref[...])
        r4 = _rstd(ff)
        fh = ff * r4
        g4 = g4_ref[...]
        h2 = h1_ref[...] + fh * g4
        row = i * tm + lax.broadcasted_iota(jnp.int32, (tm, 1), 0)
        diff = jnp.where(row >= x0, h2 - tgt_ref[...], 0.0)
        dh2 = diff / d
        dh2_ref[...] = dh2
        dff_ref[...] = _rms_bwd(dh2 * g4, fh, r4).astype(BF16)
        _acc_rows(dg4_ref, dh2 * fh, i == 0)
        _acc_rows(loss_ref, diff * diff, i == 0)

    blk = pl.BlockSpec((tm, d), lambda i: (i, 0))
    res, _ = _host_call(
        body, grid=(t_rows // tm,),
        in_specs=[pl.BlockSpec((tm, f), lambda i: (i, 0)), _resident((f, d)), blk, blk, _full((1, d))],
        out_specs=[blk, blk, _full((1, d)), _full((1, d))],
        out_shape=[jax.ShapeDtypeStruct((t_rows, d), F32), jax.ShapeDtypeStruct((t_rows, d), BF16),
                   jax.ShapeDtypeStruct((1, d), F32), jax.ShapeDtypeStruct((1, d), F32)],
        args=(s, wd, h1, tgt, g4), name="down_loss")
    return res


def _bwd_down(dff, wd, a, u, tm, comm):
    t_rows, d = dff.shape
    f = wd.shape[0]

    def body(dff_ref, w_ref, a_ref, u_ref, da_ref, du_ref):
        dff_v = dff_ref[...]
        for o, n in _chunks(f, N_CHUNK):
            ds = _dot_nt(dff_v, w_ref[pl.ds(o, n), :]).astype(BF16)
            av = a_ref[:, pl.ds(o, n)]
            uv = u_ref[:, pl.ds(o, n)]
            sig = jax.nn.sigmoid(av)
            da_ref[:, pl.ds(o, n)] = ds * uv * _silu_grad(av, sig)
            du_ref[:, pl.ds(o, n)] = ds * (av * sig)

    blk = pl.BlockSpec((tm, f), lambda i: (i, 0))
    return _host_call(
        body, grid=(t_rows // tm,),
        in_specs=[pl.BlockSpec((tm, d), lambda i: (i, 0)), _resident((f, d)), blk, blk],
        out_specs=[blk, blk], out_shape=[jax.ShapeDtypeStruct((t_rows, f), BF16)] * 2,
        args=(dff, wd, a, u), name="bwd_down", comm=comm)


def _wgrad(a, b, name, after=()):
    d = b.shape[1]
    t_rows = b.shape[0]
    stacked = a.ndim == 3
    n = a.shape[-1]
    groups = a.shape[0] if stacked else 1
    steps = 1 if stacked else 2
    tile = max(t for t in range(LANE, min(n // steps, WGRAD_TILE_MAX) + 1, LANE) if n % t == 0)
    tiles = n // tile

    def body(a_ref, b_ref, o_ref):
        o_ref[...] = lax.dot_general(a_ref[...], b_ref[...], (((0,), (0,)), ((), ())),
                                     preferred_element_type=F32).astype(BF16)

    if stacked:
        a_spec = pl.BlockSpec((None, t_rows, tile), lambda g, i: (g, 0, i))
    else:
        a_spec = pl.BlockSpec((t_rows, tile), lambda g, i: (0, i))
    res, _ = _host_call(
        body, grid=(groups, tiles), in_specs=[a_spec, _resident((t_rows, d))],
        out_specs=[pl.BlockSpec((tile, d), lambda g, i: (g * tiles + i, 0))],
        out_shape=[jax.ShapeDtypeStruct((groups * n, d), BF16)], args=(a, b), name=name, after=after)
    return res[0]


def _bwd_ffn_in(da, du, wg_t, wu_t, h1, dh2, g3, tm, comm):
    t_rows, d = h1.shape
    f = wg_t.shape[0]

    def body(da_ref, du_ref, wg_ref, wu_ref, h1_ref, dh2_ref, g3_ref, dh1_ref, dg3_ref):
        dxn2 = _dot_nn(da_ref[...], wg_ref[...]) + _dot_nn(du_ref[...], wu_ref[...])
        h1 = h1_ref[...]
        r3 = _rstd(h1)
        h1h = h1 * r3
        _acc_rows(dg3_ref, dxn2 * h1h, pl.program_id(0) == 0)
        dh1_ref[...] = dh2_ref[...] + _rms_bwd(dxn2 * g3_ref[...], h1h, r3)

    blk = pl.BlockSpec((tm, d), lambda i: (i, 0))
    blkf = pl.BlockSpec((tm, f), lambda i: (i, 0))
    return _host_call(
        body, grid=(t_rows // tm,),
        in_specs=[blkf, blkf, _resident((f, d)), _resident((f, d)), blk, blk, _full((1, d))],
        out_specs=[blk, _full((1, d))],
        out_shape=[jax.ShapeDtypeStruct((t_rows, d), F32), jax.ShapeDtypeStruct((1, d), F32)],
        args=(da, du, wg_t, wu_t, h1, dh2, g3), name="bwd_ffn_in", comm=comm)


def _bwd_out_proj(dh1, mix, w_out, g2, z, lg, lb, tm, after):
    t_rows, d = dh1.shape
    w = z.shape[1]

    def body(dh1_ref, mix_ref, w_ref, g2_ref, z_ref, lg_ref, lb_ref, dmix_ref, dya_ref, dz_ref, dg2_ref, dlg_ref, dlb_ref, dbb_ref):
        first = pl.program_id(0) == 0
        mix = mix_ref[...]
        r2 = _rstd(mix)
        mh = mix * r2
        dh1 = dh1_ref[...]
        _acc_rows(dg2_ref, dh1 * mh, first)
        dmix = _rms_bwd(dh1 * g2_ref[...], mh, r2).astype(BF16)
        dmix_ref[...] = dmix
        dy = _dot_nt(dmix, w_ref[...])
        dya_ref[...] = dy[:, :w]
        lg = lg_ref[...]
        zh, rstd, ln = _ln_parts(z_ref[...], lg, lb_ref[...])
        dln = dy[:, w:] * _silu_grad(ln, jax.nn.sigmoid(ln))
        _acc_rows(dlg_ref, dln * zh, first)
        _acc_rows(dlb_ref, dln, first)
        dzh = dln * lg
        dz = rstd * (dzh - jnp.mean(dzh, axis=-1, keepdims=True) - zh * jnp.mean(dzh * zh, axis=-1, keepdims=True))
        dz_ref[...] = dz
        _acc_rows(dbb_ref, dz, first)

    blk = pl.BlockSpec((tm, d), lambda i: (i, 0))
    half = pl.BlockSpec((tm, w), lambda i: (i, 0))
    vec = _full((1, w))
    res, _ = _host_call(
        body, grid=(t_rows // tm,), in_specs=[blk, blk, _resident(w_out.shape), _full((1, d)), half, vec, vec],
        out_specs=[blk, half, half, _full((1, d)), vec, vec, vec],
        out_shape=[jax.ShapeDtypeStruct((t_rows, d), BF16), jax.ShapeDtypeStruct((t_rows, w), F32),
                   jax.ShapeDtypeStruct((t_rows, w), F32), jax.ShapeDtypeStruct((1, d), F32)]
        + [jax.ShapeDtypeStruct((1, w), F32)] * 3,
        args=(dh1, mix, w_out, g2, z, lg, lb), name="bwd_out_proj", after=after)
    return res


def _mix_conv_bwd(hin, dy, dz, wa, wb, wa_w, comm):
    t_rows = hin.shape[0]
    nt = wa_w // LANE
    ka, kb = wa.shape[0], wb.shape[0]
    nr = CONV_HALO + t_rows
    kb_rows = -(-kb // SUBLANE) * SUBLANE

    def body(bg_ref, cg_ref, ha_ref, val_ref, gt_ref, dya_ref, dz_ref, wa_ref, wb_ref,
             dh_ref, dwa_ref, dwb_ref, base, sh, based, shd, tmp, wbc):
        zeros = lambda n: jnp.zeros((n, LANE), F32)
        base[pl.ds(0, CONV_HALO), :] = zeros(CONV_HALO)
        base[pl.ds(nr, SUBLANE), :] = zeros(SUBLANE)
        based[pl.ds(t_rows, CONV_HALO + SUBLANE), :] = zeros(CONV_HALO + SUBLANE)

        def fwd_slot(k_taps, k):
            return _tap_slot(CONV_HALO - (k_taps - 1) + k)

        def bwd_slot(k_taps, k):
            return _tap_slot(k_taps - 1 - k)

        def conv(w_ref, k_taps, src, slot, b, n):
            acc = None
            for k in range(k_taps):
                r, q = slot(k_taps, k)
                term = w_ref[pl.ds(k, 1), :] * _shifted_rows(src, r, b + q, n)
                acc = term if acc is None else acc + term
            return acc

        def by_residue(k_taps, slot):
            groups = {}
            for k in range(k_taps):
                r, q = slot(k_taps, k)
                groups.setdefault(r, []).append((k, q // SUBLANE))
            return groups

        def wgrad_loop(w_ref, k_taps):
            n_sub = WGRAD_ROWS // SUBLANE
            for k in range(k_taps):
                wbc[k] = jnp.broadcast_to(w_ref[pl.ds(k, 1), :], (SUBLANE, LANE))
            fwd, bwd = by_residue(k_taps, fwd_slot), by_residue(k_taps, bwd_slot)

            def window(src, r, taps, b):
                span = n_sub + max(qi for _, qi in taps)
                return [_shifted_rows(src, r, b + SUBLANE * i, SUBLANE) for i in range(span)]

            def step(b, accs):
                accs = list(accs)
                dv = [based[pl.ds(b + SUBLANE * j, SUBLANE), :] for j in range(n_sub)]
                for r, taps in fwd.items():
                    win = window((base, sh), r, taps, b)
                    for k, qi in taps:
                        t = dv[0] * win[qi]
                        for j in range(1, n_sub):
                            t = t + dv[j] * win[qi + j]
                        accs[k] = accs[k] + t
                outs = [None] * n_sub
                for r, taps in bwd.items():
                    win = window((based, shd), r, taps, b)
                    for k, qi in taps:
                        wk = wbc[k]
                        for j in range(n_sub):
                            term = wk * win[qi + j]
                            outs[j] = term if outs[j] is None else outs[j] + term
                for j in range(n_sub):
                    tmp[pl.ds(b + SUBLANE * j, SUBLANE), :] = outs[j]
                return tuple(accs)

            return _row_loop(t_rows, WGRAD_ROWS, step, tuple(zeros(SUBLANE) for _ in range(k_taps)))

        def store_taps(ref, accs, rows):
            for k, acc in enumerate(accs):
                ref[pl.ds(k, 1), :] = jnp.sum(acc, axis=0, keepdims=True)
            if rows > len(accs):
                ref[pl.ds(len(accs), rows - len(accs)), :] = zeros(rows - len(accs))

        def fill_a(b, c):
            sl = pl.ds(b, CONV_CHUNK)
            base[pl.ds(CONV_HALO + b, CONV_CHUNK), :] = cg_ref[sl, :] * ha_ref[sl, :]
            based[sl, :] = dya_ref[sl, :] * bg_ref[sl, :]
            return c

        _row_loop(t_rows, CONV_CHUNK, fill_a)
        _fill_shifted(sh, base, sorted({fwd_slot(ka, k)[0] for k in range(ka)}), nr)
        _fill_shifted(shd, based, sorted({bwd_slot(ka, k)[0] for k in range(ka)}), nr)

        def d_bgate(b, c):
            sl = pl.ds(b, CONV_CHUNK)
            dh_ref[0, sl, :] = (dya_ref[sl, :] * conv(wa_ref, ka, (base, sh), fwd_slot, b, CONV_CHUNK)).astype(BF16)
            return c

        _row_loop(t_rows, CONV_CHUNK, d_bgate)
        store_taps(dwa_ref, wgrad_loop(wa_ref, ka), SUBLANE)

        def d_ch(b, c):
            sl = pl.ds(b, CONV_CHUNK)
            dua = tmp[sl, :]
            dh_ref[1, sl, :] = (dua * ha_ref[sl, :]).astype(BF16)
            dh_ref[2, sl, :] = (dua * cg_ref[sl, :]).astype(BF16)
            return c

        _row_loop(t_rows, CONV_CHUNK, d_ch)

        def fill_b(b, c):
            sl = pl.ds(b, CONV_CHUNK)
            base[pl.ds(CONV_HALO + b, CONV_CHUNK), :] = val_ref[sl, :] * jax.nn.sigmoid(gt_ref[sl, :])
            based[sl, :] = dz_ref[sl, :]
            return c

        _row_loop(t_rows, CONV_CHUNK, fill_b)
        _fill_shifted(sh, base, range(SUBLANE), nr)
        _fill_shifted(shd, based, range(SUBLANE), nr)
        store_taps(dwb_ref, wgrad_loop(wb_ref, kb), kb_rows)

        def d_glu(b, c):
            sl = pl.ds(b, CONV_CHUNK)
            dgg = tmp[sl, :]
            sig = jax.nn.sigmoid(gt_ref[sl, :])
            dh_ref[3, sl, :] = (dgg * sig).astype(BF16)
            dh_ref[4, sl, :] = (dgg * val_ref[sl, :] * (sig * (1.0 - sig))).astype(BF16)
            return c

        _row_loop(t_rows, CONV_CHUNK, d_glu)

    def col(g):
        return pl.BlockSpec((t_rows, LANE), lambda i, g=g: (0, g * nt + i))

    tile = lambda rows: pl.BlockSpec((rows, LANE), lambda i: (0, i))
    return _host_call(
        body, grid=(nt,),
        in_specs=[col(0), col(1), col(2), col(3), col(4), tile(t_rows), tile(t_rows), tile(ka), tile(kb)],
        out_specs=[pl.BlockSpec((5, t_rows, LANE), lambda i: (0, 0, i)), tile(SUBLANE), tile(kb_rows)],
        out_shape=[jax.ShapeDtypeStruct((5, t_rows, wa_w), BF16), jax.ShapeDtypeStruct((SUBLANE, wa_w), F32),
                   jax.ShapeDtypeStruct((kb_rows, wa_w), F32)],
        scratch_shapes=[pltpu.VMEM((nr + SUBLANE, LANE), F32), pltpu.VMEM((SUBLANE, nr, LANE), F32),
                        pltpu.VMEM((nr + SUBLANE, LANE), F32), pltpu.VMEM((SUBLANE, nr, LANE), F32),
                        pltpu.VMEM((t_rows, LANE), F32), pltpu.VMEM((kb_rows, SUBLANE, LANE), F32)],
        args=(hin, hin, hin, hin, hin, dy, dz, wa, wb), name="mix_conv_bwd", comm=comm)


def _bwd_in_proj(dh5, win_t, h0, dh1, g1, tm, comm):
    t_rows, d = h0.shape
    groups, _, w = dh5.shape

    def body(dh_ref, w_ref, h0_ref, dh1_ref, g1_ref, dh0_ref, dg1_ref):
        dxn1 = None
        for g in range(groups):
            part = _dot_nn(dh_ref[g], w_ref[pl.ds(g * w, w), :])
            dxn1 = part if dxn1 is None else dxn1 + part
        h0 = h0_ref[...]
        r1 = _rstd(h0)
        h0h = h0 * r1
        _acc_rows(dg1_ref, dxn1 * h0h, pl.program_id(0) == 0)
        dh0_ref[...] = dh1_ref[...] + _rms_bwd(dxn1 * g1_ref[...], h0h, r1)

    blk = pl.BlockSpec((tm, d), lambda i: (i, 0))
    return _host_call(
        body, grid=(t_rows // tm,),
        in_specs=[pl.BlockSpec((groups, tm, w), lambda i: (0, i, 0)), _resident(win_t.shape), blk, blk, _full((1, d))],
        out_specs=[blk, _full((1, d))],
        out_shape=[jax.ShapeDtypeStruct((t_rows, d), F32), jax.ShapeDtypeStruct((1, d), F32)],
        args=(dh5, win_t, h0, dh1, g1), name="bwd_in_proj", comm=comm)


def _pair_small(smalls, d):
    (dmeta, dg1, dg2, dg3, dg4, dbb, dlg, dlb, lossv, dwa, dwb) = smalls
    half = d // 2
    kb_rows = dwb.shape[0]

    def body(dmeta_ref, dg1_ref, dg2_ref, dg3_ref, dg4_ref, dbb_ref, dlg_ref, dlb_ref, loss_ref, dwa_ref, dwb_ref,
             sums_ref, pbuf, psib, ps_send, ps_recv):
        x, y, c = _mesh_pos()
        _pair_handshake()
        pbuf[...] = jnp.zeros_like(pbuf)
        pbuf[pl.ds(0, N_META), :] = dmeta_ref[...]
        for row, ref in ((16, dg1_ref), (17, dg2_ref), (18, dg3_ref), (19, dg4_ref)):
            pbuf[pl.ds(row, 1), :] = ref[...]
        pbuf[pl.ds(20, 1), pl.ds(0, half)] = dbb_ref[...]
        pbuf[pl.ds(20, 1), pl.ds(half, half)] = dlg_ref[...]
        pbuf[pl.ds(21, 1), pl.ds(0, half)] = dlb_ref[...]
        lv = loss_ref[...]
        pbuf[pl.ds(21, 1), pl.ds(half, half)] = lv[:, :half] + lv[:, half:]
        pbuf[pl.ds(24, SUBLANE), pl.ds(0, half)] = dwa_ref[...]
        pbuf[pl.ds(32, kb_rows), pl.ds(0, half)] = dwb_ref[...]
        to_sib = _remote(pbuf, psib, ps_send.at[0], ps_recv.at[0], (x, y, 1 - c))
        to_sib.start()
        to_sib.wait_recv()
        s = pbuf[...] + psib[...]
        for k in range(3):
            sums_ref[k] = s
        to_sib.wait_send()

    return pl.pallas_call(
        body, out_shape=jax.ShapeDtypeStruct((3, SMALL_ROWS, d), F32), in_specs=[VMEM] * 11, out_specs=VMEM,
        scratch_shapes=[pltpu.VMEM((SMALL_ROWS, d), F32), pltpu.VMEM((SMALL_ROWS, d), F32),
                        pltpu.SemaphoreType.DMA((1,)), pltpu.SemaphoreType.DMA((1,))],
        name="pair_small", compiler_params=pltpu.CompilerParams(vmem_limit_bytes=VMEM_LIMIT, collective_id=PAIR_BARRIER_ID))(*smalls)


def _total_small(own, others):
    _, r, d = own.shape

    def body(own_ref, others_ref, tot_ref, chip_p):
        x, y, _ = _mesh_pos()
        chip_p[2 * x + y] = own_ref[0]
        for k, (cx, cy) in enumerate(_other_chips(x, y)):
            chip_p[2 * cx + cy] = others_ref[k]
        tot_ref[...] = ((chip_p[0] + chip_p[1]) + chip_p[2]) + chip_p[3]

    return pl.pallas_call(body, out_shape=jax.ShapeDtypeStruct((r, d), F32), scratch_shapes=[pltpu.VMEM((4, r, d), F32)],
                          name="total_small", compiler_params=_cparams())(own, others)


def _adamw(w, g, m, v):
    m = ADAM_B1 * m + (1.0 - ADAM_B1) * g
    v = ADAM_B2 * v + (1.0 - ADAM_B2) * jnp.square(g)
    m_hat = m / (1.0 - ADAM_B1 ** ADAM_STEP)
    v_hat = v / (1.0 - ADAM_B2 ** ADAM_STEP)
    delta = -ADAM_LR * (m_hat / (jnp.sqrt(v_hat) + ADAM_EPS) + ADAM_WD * w)
    return delta, m, v


def _adam_big(g, pair, part, w, m, v, name):
    r, d = w.shape
    cols = d // ADAM_COL_BLOCKS

    def body(me_ref, g_ref, pair_ref, part_ref, w_ref, m_ref, v_ref, go_ref, d_ref, mo_ref, vo_ref):
        g = g_ref[...].astype(F32) + pair_ref[...].astype(F32)
        for k in range(3):
            g = g + part_ref[k].astype(F32)
        go_ref[...] = g
        d_ref[...], mo_ref[...], vo_ref[...] = _adamw(w_ref[...], g, m_ref[...], v_ref[...])

    blk = pl.BlockSpec((r, cols), lambda i, me_ref: (0, i))
    grid_spec = pltpu.PrefetchScalarGridSpec(
        num_scalar_prefetch=1, grid=(ADAM_COL_BLOCKS,),
        in_specs=[pl.BlockSpec((r, cols), lambda i, me_ref: (me_ref[0], i)),
                  pl.BlockSpec((None, r, cols), lambda i, me_ref: (0, 0, i)),
                  pl.BlockSpec((3, r, cols), lambda i, me_ref: (0, 0, i)), blk, blk, blk],
        out_specs=[blk, blk, blk, blk])
    me = jnp.reshape(_dev_index(*_mesh_pos()), (1,)).astype(jnp.int32)
    return pl.pallas_call(body, out_shape=[jax.ShapeDtypeStruct((r, d), F32)] * 4, grid_spec=grid_spec, name=name,
                          compiler_params=_cparams(1))(me, g, pair, part, w, m, v)


def _adam_small(gs, ws, ms, vs):
    n = len(gs)

    def body(*refs):
        ins, outs = refs[:4 * n], refs[4 * n:]
        for i in range(n):
            g = ins[i][...]
            delta, m, v = _adamw(ins[n + i][...], g, ins[2 * n + i][...], ins[3 * n + i][...])
            outs[i][...] = delta
            outs[n + i][...] = m
            outs[2 * n + i][...] = v

    shapes = [jax.ShapeDtypeStruct(w.shape, F32) for w in ws]
    return pl.pallas_call(body, out_shape=shapes * 3, name="adam_small", compiler_params=_cparams())(*gs, *ws, *ms, *vs)


def kernel(x, meta_tokens, pre_mix_norm, w_in, conv_a_w, conv_b_w, conv_b_bias, ln_b_gain, ln_b_bias, w_out, post_mix_norm, pre_ffn_norm, w_gate, w_up, w_down, post_ffn_norm, loss_target, m_meta_tokens, m_pre_mix_norm, m_w_in, m_conv_a_w, m_conv_b_w, m_conv_b_bias, m_ln_b_gain, m_ln_b_bias, m_w_out, m_post_mix_norm, m_pre_ffn_norm, m_w_gate, m_w_up, m_w_down, m_post_ffn_norm, v_meta_tokens, v_pre_mix_norm, v_w_in, v_conv_a_w, v_conv_b_w, v_conv_b_bias, v_ln_b_gain, v_ln_b_bias, v_w_out, v_post_mix_norm, v_pre_ffn_norm, v_w_gate, v_w_up, v_w_down, v_post_ffn_norm):
    _, seq, d = x.shape
    ka, ca_loc = conv_a_w.shape[1:]
    kb, cb_loc = conv_b_w.shape[1:]
    wa_w = ca_loc * N_DEV
    assert cb_loc == ca_loc and wa_w % LANE == 0 and w_in.shape[2] * N_DEV == 5 * wa_w and 2 * wa_w == d
    pad = (-(N_META + seq)) % ROW_ALIGN
    x0 = pad + N_META
    t_rows = x0 + seq
    assert t_rows % (N_ROW_BLOCKS * BF16_ROWS) == 0 and t_rows % CONV_CHUNK == 0 and d % LANE == 0
    tm = t_rows // N_ROW_BLOCKS
    me = _dev_index(*_mesh_pos())

    def as_rows(w_in_like, w_out_like, w_gate_like, w_up_like, w_down_like):
        return (w_in_like[0].T, w_out_like[0], w_gate_like[0].T, w_up_like[0].T, w_down_like[0])

    w_loc = as_rows(w_in, w_out, w_gate, w_up, w_down)
    rows = [w.shape[0] for w in w_loc]
    assert all(r % ADD_CHUNK == 0 for r in rows)
    P_IN, P_OUT, P_GATE, P_UP, P_DOWN = range(N_BIG)

    sm = jnp.zeros((SM_ROWS, LANE), F32)
    sm = sm.at[0:N_META, :].set(meta_tokens)
    sm = sm.at[16:16 + ka, 0:ca_loc].set(conv_a_w[0])
    sm = sm.at[24:24 + kb, 0:cb_loc].set(conv_b_w[0])
    wl, wfull, sm_all, h0, tgt = _gather_first(w_loc, sm, [(P_IN, 0, rows[P_IN])], x[0], loss_target[0], t_rows, x0)
    wa =jnp.transpose(sm_all[:, 16:16 + ka, 0:ca_loc], (1, 0, 2)).reshape(ka, wa_w)
    wb = jnp.transpose(sm_all[:, 24:24 + kb, 0:cb_loc], (1, 0, 2)).reshape(kb, wa_w)

    later = (P_OUT, P_GATE, P_UP, P_DOWN)
    sems, wl, started, _ = _gather_start(wl, wfull, later, rows, START_BARRIER_IDS[0])
    for p, arr in zip(later, started):
        wfull[p] = arr

    def arrived(p, after, name):
        nonlocal wl
        wl, wfull[p] = _gather_wait(wl, wfull[p], sems[later.index(p)], after, rows[p], name)
        return _forward_comm(wfull[p], rows[p])

    (xn1, hin), _ = _in_proj(h0, pre_mix_norm, wfull[P_IN], tm, None)
    (ya, z), (wfull[P_OUT],) = _mix_conv_fwd(hin, wa, wb, conv_b_bias, wa_w, arrived(P_OUT, hin, "gather_wait_out"))
    (y, mix, h1, xn2), (wfull[P_GATE],) = _out_proj(ya, z, ln_b_gain, ln_b_bias, wfull[P_OUT], h0, post_mix_norm, pre_ffn_norm, tm,
                                                    arrived(P_GATE, z, "gather_wait_gate"))
    arrived(P_UP, xn2, "gather_wait_up")
    wfull[P_UP] = _forward_now(wfull[P_UP], rows[P_UP], "forward_up")
    (a, u, s), _ = _gate_up(xn2, wfull[P_GATE], wfull[P_UP], tm, None)
    arrived(P_DOWN, s, "gather_wait_down")
    wfull[P_DOWN] = _forward_now(wfull[P_DOWN], rows[P_DOWN], "forward_down")
    dh2, dff, dg4, lossv = _down_loss(s, wfull[P_DOWN], h1, tgt, post_ffn_norm, tm, x0)

    gwd = _wgrad(s, dff, "wgrad_down")
    (da, du), (pair_d,) = _bwd_down(dff, wfull[P_DOWN], a, u, tm, _pair_comm(gwd, rows[P_DOWN]))
    (flight_d,), token = _chip_start([_pair_sum(gwd, pair_d, rows[P_DOWN], "pair_sum_down")], "chip_start_down", START_BARRIER_IDS[1])
    gwg = _wgrad(da, xn2, "wgrad_gate", [token])
    gwu = _wgrad(du, xn2, "wgrad_up")
    (dh1, dg3), (pair_g, pair_u) = _bwd_ffn_in(da, du, wfull[P_GATE], wfull[P_UP], h1, dh2, pre_ffn_norm, tm,
                                               _merge_comms([_pair_comm(gwg, rows[P_GATE]), _pair_comm(gwu, rows[P_UP])]))
    (flight_g, flight_u), token = _chip_start([_pair_sum(gwg, pair_g, rows[P_GATE], "pair_sum_gate"),
                                               _pair_sum(gwu, pair_u, rows[P_UP], "pair_sum_up")], "chip_start_gate_up",
                                              START_BARRIER_IDS[2])
    dmix, dya, dz, dg2, dlg, dlb, dbb = _bwd_out_proj(dh1, mix, wfull[P_OUT], post_mix_norm, z, ln_b_gain, ln_b_bias, tm, [token])
    gwo = _wgrad(y, dmix, "wgrad_out")
    (dh5, dwa, dwb), (pair_o,) = _mix_conv_bwd(hin, dya, dz, wa, wb, wa_w, _pair_comm(gwo, rows[P_OUT]))
    (flight_o,), token = _chip_start([_pair_sum(gwo, pair_o, rows[P_OUT], "pair_sum_out")], "chip_start_out", START_BARRIER_IDS[3])
    gwi = _wgrad(dh5, xn1, "wgrad_in", [token])
    (dh0, dg1), (pair_i,) = _bwd_in_proj(dh5, wfull[P_IN], h0, dh1, pre_mix_norm, tm, _pair_comm(gwi, rows[P_IN]))
    grad_x = dh0[x0:][None]
    dmeta = dh0[x0 - N_META:x0]
    small_sums = _pair_small((dmeta, dg1, dg2, dg3, dg4, dbb, dlg, dlb, lossv, dwa, dwb), d)
    (flight_s,), token = _chip_start([small_sums], "chip_start_small", START_BARRIER_IDS[4])
    (flight_i,), token = _chip_start([_pair_sum(gwi, pair_i, rows[P_IN], "pair_sum_in", [token])], "chip_start_in",
                                     START_BARRIER_IDS[5])

    def landed(flight, after, tag):
        sems_p, sums, land = flight
        return _chip_wait(sums, land, sems_p, after, "chip_wait_" + tag)

    m_loc = as_rows(m_w_in, m_w_out, m_w_gate, m_w_up, m_w_down)
    v_loc = as_rows(v_w_in, v_w_out, v_w_gate, v_w_up, v_w_down)
    full_grads = {P_IN: gwi, P_OUT: gwo, P_GATE: gwg, P_UP: gwu, P_DOWN: gwd}
    pairs = {P_IN: pair_i, P_OUT: pair_o, P_GATE: pair_g, P_UP: pair_u, P_DOWN: pair_d}
    flights = {P_IN: flight_i, P_OUT: flight_o, P_GATE: flight_g, P_UP: flight_u, P_DOWN: flight_d}
    names = {P_IN: "w_in", P_OUT: "w_out", P_GATE: "w_gate", P_UP: "w_up", P_DOWN: "w_down"}
    bigs = {}

    def adam_big(p, after):
        part = landed(flights[p], after, names[p])
        res = _adam_big(full_grads[p], pairs[p], part, w_loc[p], m_loc[p], v_loc[p], "adam_" + names[p])
        bigs[names[p]] = [(o.T if p in (P_IN, P_GATE, P_UP) else o)[None] for o in res]
        return res[1]

    for p in (P_DOWN, P_GATE, P_UP, P_OUT):
        token = adam_big(p, token)
    ptot = _total_small(flight_s[1], landed(flight_s, token, "small"))
    half = d // 2
    loss = (0.5 / d) * jnp.sum(ptot[21, half:])
    g_meta = lax.dynamic_slice(ptot, (0, me * (d // N_DEV)), (N_META, d // N_DEV))
    g_small = [g_meta, ptot[16:17], lax.dynamic_slice(ptot, (24, me * ca_loc), (ka, ca_loc))[None],
               lax.dynamic_slice(ptot, (32, me * cb_loc), (kb, cb_loc))[None],
               ptot[20:21, :half], ptot[20:21, half:], ptot[21:22, :half], ptot[17:18], ptot[18:19], ptot[19:20]]
    w_small = [meta_tokens, pre_mix_norm, conv_a_w, conv_b_w, conv_b_bias, ln_b_gain, ln_b_bias, post_mix_norm,
               pre_ffn_norm, post_ffn_norm]
    m_small = [m_meta_tokens, m_pre_mix_norm, m_conv_a_w, m_conv_b_w, m_conv_b_bias, m_ln_b_gain, m_ln_b_bias,
               m_post_mix_norm, m_pre_ffn_norm, m_post_ffn_norm]
    v_small = [v_meta_tokens, v_pre_mix_norm, v_conv_a_w, v_conv_b_w, v_conv_b_bias, v_ln_b_gain, v_ln_b_bias,
               v_post_mix_norm, v_pre_ffn_norm, v_post_ffn_norm]
    small = _adam_small(g_small, w_small, m_small, v_small)
    n_small = len(w_small)
    d_small, nm_small, nv_small = small[:n_small], small[n_small:2 * n_small], small[2 * n_small:]

    adam_big(P_IN, small[0])

    def ordered(pick_small, pick_big):
        sm_it = iter(range(n_small))
        out = []
        for name in ("s", "s", "w_in", "s", "s", "s", "s", "s", "w_out", "s", "s", "w_gate", "w_up", "w_down", "s"):
            out.append(pick_small(next(sm_it)) if name == "s" else pick_big(name))
        return out

    grads = ordered(lambda i: g_small[i], lambda n: bigs[n][0])
    deltas = ordered(lambda i: d_small[i], lambda n: bigs[n][1])
    new_m = ordered(lambda i: nm_small[i], lambda n: bigs[n][2])
    new_v = ordered(lambda i: nv_small[i], lambda n: bigs[n][3])
    return (loss, grad_x, *grads, *deltas, *new_m, *new_v)
```

```python
import jax
import jax.numpy as jnp
from jax import lax
from jax.experimental import pallas as pl
from jax.experimental.pallas import tpu as pltpu

F32 = jnp.float32
BF16 = jnp.bfloat16
MESH = pl.DeviceIdType.MESH

N_META = 16
N_DEV = 8
RMS_EPS = 1e-6
LN_EPS = 1e-5
ADAM_LR = 0.001
ADAM_B1 = 0.9
ADAM_B2 = 0.999
ADAM_EPS = 1e-08
ADAM_WD = 0.01
ADAM_STEP = 10

LANE = 128
SUBLANE = 8
BF16_ROWS = 16
ROW_ALIGN = 128
N_ROW_BLOCKS = 4
CONV_HALO = 32
CONV_CHUNK = 64
WGRAD_ROWS = 32
N_CHUNK = 512
WGRAD_TILE_MAX = 1408
ADD_CHUNK = 32
ADAM_COL_BLOCKS = 4
COPY_PIECES = 4
V7X_VMEM_BYTES = 64 * 1024 * 1024
VMEM_LIMIT = V7X_VMEM_BYTES - 6 * 1024 * 1024
SMALL_ROWS = 64
SM_ROWS = 56
N_BIG = 5

ANY = pl.BlockSpec(memory_space=pl.ANY)
VMEM = pl.BlockSpec(memory_space=pltpu.VMEM)


def _cparams(n_grid_axes=0):
    sem = ("arbitrary",) * n_grid_axes if n_grid_axes else None
    return pltpu.CompilerParams(dimension_semantics=sem, vmem_limit_bytes=VMEM_LIMIT)


def _mesh_pos():
    return lax.axis_index("x"), lax.axis_index("y"), lax.axis_index("c")


def _dev_index(px, py, pc):
    return 4 * px + 2 * py + pc


def _other_chips(x, y):
    return [(1 - x, y), (x, 1 - y), (1 - x, 1 - y)]


def _full(shape):
    return pl.BlockSpec(shape, lambda *_: (0,) * len(shape))


def _resident(shape):
    return pl.BlockSpec(shape, lambda *_: (0,) * len(shape), pipeline_mode=pl.Buffered(1))


def _dot_nt(a, w):
    return lax.dot_general(a, w, (((1,), (1,)), ((), ())), preferred_element_type=F32)


def _dot_nn(a, w):
    return jnp.dot(a, w, preferred_element_type=F32)


def _chunks(n, c):
    out, o = [], 0
    while o < n:
        out.append((o, min(c, n - o)))
        o += c
    return out


def _rstd(h):
    return lax.rsqrt(jnp.mean(h * h, axis=-1, keepdims=True) + RMS_EPS)


def _rms_bwd(dyh, yh, r):
    return r * (dyh - yh * jnp.mean(dyh * yh, axis=-1, keepdims=True))


def _silu_grad(a, sig):
    return sig * (1.0 + a * (1.0 - sig))


def _acc_rows(ref, val, first):
    s = jnp.sum(val, axis=0, keepdims=True)

    @pl.when(first)
    def _():
        ref[...] = s

    @pl.when(jnp.logical_not(first))
    def _():
        ref[...] += s


def _row_loop(t_rows, chunk, fn, carry=None):
    def step(i, c):
        return fn(pl.multiple_of(i * chunk, chunk), c)

    return lax.fori_loop(0, t_rows // chunk, step, carry)


def _remote(src, dst, send_sem, recv_sem, to):
    return pltpu.make_async_remote_copy(src_ref=src, dst_ref=dst, send_sem=send_sem, recv_sem=recv_sem,
                                        device_id=to, device_id_type=MESH)


class _SplitRemote:
    def __init__(self, src, dst, send_sem, recv_sem, to, rows, n_chunks):
        units = rows // BF16_ROWS
        n_chunks = max(1, min(n_chunks, units))
        sizes = [(units // n_chunks + (i < units % n_chunks)) * BF16_ROWS for i in range(n_chunks)]
        self.whole = _remote(src, dst, send_sem, recv_sem, to)
        self.parts, o = [], 0
        for n in sizes:
            self.parts.append(_remote(src.at[pl.ds(o, n), :], dst.at[pl.ds(o, n), :], send_sem, recv_sem, to))
            o += n

    def start(self):
        for cp in self.parts:
            cp.start()

    def wait_recv(self):
        self.whole.wait_recv()

    def wait_send(self):
        self.whole.wait_send()


class _Comm:
    def __init__(self, inputs, out_shapes, aliases, scratch, start, finish):
        self.inputs, self.out_shapes, self.aliases, self.scratch = list(inputs), list(out_shapes), dict(aliases), list(scratch)
        self.start, self.finish = start, finish


def _merge_comms(comms):
    inputs, out_shapes, aliases, scratch, spans = [], [], {}, [], []
    for cm in comms:
        spans.append((len(inputs), len(out_shapes), len(scratch), cm))
        aliases.update({len(inputs) + k: len(out_shapes) + v for k, v in cm.aliases.items()})
        inputs += cm.inputs
        out_shapes += cm.out_shapes
        scratch += cm.scratch

    def run(which):
        def fn(ins, outs, scr):
            for i0, o0, s0, cm in spans:
                getattr(cm, which)(ins[i0:i0 + len(cm.inputs)], outs[o0:o0 + len(cm.out_shapes)], scr[s0:s0 + len(cm.scratch)])
        return fn

    return _Comm(inputs, out_shapes, aliases, scratch, run("start"), run("finish"))


def _host_call(body, *, grid, in_specs, out_specs, out_shape, args, name, scratch_shapes=(), comm=None, after=()):
    talks = comm is not None
    if comm is None:
        comm = _Comm([], [], {}, [], lambda *_: None, lambda *_: None)
    n_in, n_out, n_scr = len(args), len(out_shape), len(scratch_shapes)
    c_in, c_out = len(comm.inputs), len(comm.out_shapes)
    n_after = len(after)

    def open_comm(c_ins, c_outs, c_scr):
        if talks:
            _pair_handshake()
        comm.start(c_ins, c_outs, c_scr)

    def hosted(*refs):
        ins, c_ins = refs[:n_in], refs[n_in:n_in + c_in]
        o0 = n_in + c_in + n_after
        outs, c_outs = refs[o0:o0 + n_out], refs[o0 + n_out:o0 + n_out + c_out]
        s0 = o0 + n_out + c_out
        scr, c_scr = refs[s0:s0 + n_scr], refs[s0 + n_scr:]
        if not grid:
            open_comm(c_ins, c_outs, c_scr)
            body(*ins, *outs, *scr)
            comm.finish(c_ins, c_outs, c_scr)
            return
        first = last = None
        for a, n in enumerate(grid):
            f, l = pl.program_id(a) == 0, pl.program_id(a) == n - 1
            first = f if first is None else jnp.logical_and(first, f)
            last = l if last is None else jnp.logical_and(last, l)

        @pl.when(first)
        def _():
            open_comm(c_ins, c_outs, c_scr)

        body(*ins, *outs, *scr)

        @pl.when(last)
        def _():
            comm.finish(c_ins, c_outs, c_scr)

    sem = ("arbitrary",) * len(grid) if grid else None
    params = pltpu.CompilerParams(dimension_semantics=sem, vmem_limit_bytes=VMEM_LIMIT,
                                  collective_id=PAIR_BARRIER_ID if talks else None)
    res = pl.pallas_call(
        hosted, grid=grid, in_specs=list(in_specs) + [ANY] * (c_in + n_after), out_specs=list(out_specs) + [ANY] * c_out,
        out_shape=list(out_shape) + comm.out_shapes, scratch_shapes=list(scratch_shapes) + comm.scratch,
        input_output_aliases={n_in + k: n_out + v for k, v in comm.aliases.items()},
        name=name, compiler_params=params)(*args, *comm.inputs, *after)
    return list(res[:n_out]), list(res[n_out:])


PAIR_BARRIER_ID = 0
START_BARRIER_IDS = (1, 2, 3, 4, 5, 6)


def _chips_handshake():
    x, y, c = _mesh_pos()
    barrier = pltpu.get_barrier_semaphore()
    for chip in _other_chips(x, y):
        pl.semaphore_signal(barrier, inc=1, device_id=(*chip, c), device_id_type=MESH)
    pl.semaphore_wait(barrier, 3)


def _pair_handshake():
    x, y, c = _mesh_pos()
    barrier = pltpu.get_barrier_semaphore()
    pl.semaphore_signal(barrier, inc=1, device_id=(x, y, 1 - c), device_id_type=MESH)
    pl.semaphore_wait(barrier, 1)


GATHER_SEMS = 10
D2D_CHUNKS = 8


class _Gather:
    def __init__(self, jobs, rows, lo, src_ref, dests, send_sems, recv_sems):
        x, y, c = _mesh_pos()
        me, sib = (x, y, c), (x, y, 1 - c)
        nx, ny, dg = (1 - x, y, c), (x, 1 - y, c), (1 - x, 1 - y, c)
        self.relayed, self.direct, self.relay, self.to_sib, self.sib_fwd = [], [], [], [], []
        for n, (p, r0, nr) in enumerate(jobs):
            assert nr % (2 * BF16_ROWS) == 0
            half = nr // 2

            def rows_of(dev, h, p=p, r0=r0, nr=nr, half=half):
                off, cnt = (r0, nr) if h is None else (r0 + h * half, half)
                return dests[p].at[pl.ds(pl.multiple_of(_dev_index(*dev) * rows[p] + off, BF16_ROWS), cnt), :]

            def mine(h, p=p, r0=r0, nr=nr, half=half):
                off, cnt = (r0, nr) if h is None else (r0 + h * half, half)
                return src_ref.at[pl.ds(lo[p] + off, cnt), :]

            sem = lambda k, n=n: (send_sems.at[GATHER_SEMS * n + k], recv_sems.at[GATHER_SEMS * n + k])
            self.relayed.append([_remote(mine(0), rows_of(me, 0), *sem(0), nx), _remote(mine(1), rows_of(me, 1), *sem(3), ny)])
            self.direct.append([_remote(mine(1), rows_of(me, 1), *sem(1), nx), _remote(mine(0), rows_of(me, 0), *sem(2), ny)])
            self.relay.append([_remote(rows_of(nx, 0), rows_of(nx, 0), *sem(4), ny), _remote(rows_of(ny, 1), rows_of(ny, 1), *sem(5), nx)])
            self.to_sib.append(_SplitRemote(mine(None), rows_of(me, None), *sem(6), sib, nr, D2D_CHUNKS))
            self.sib_fwd.append([_SplitRemote(rows_of(dev, None), rows_of(dev, None), *sem(7 + i), sib, nr, D2D_CHUNKS)
                                 for i, dev in enumerate((nx, ny, dg))])

    def start(self):
        for group in (self.relayed, self.direct):
            for cps in group:
                for cp in cps:
                    cp.start()
        for cp in self.to_sib:
            cp.start()

    def mid(self):
        for first, relay in zip(self.relayed, self.relay):
            for arrived, onward in zip(first, relay):
                arrived.wait_recv()
                onward.start()

    def finish(self):
        for direct, relay, fwd in zip(self.direct, self.relay, self.sib_fwd):
            for k in range(2):
                direct[k].wait_recv()
                fwd[k].start()
            for cp in relay:
                cp.wait_recv()
            fwd[2].start()
        for n in range(len(self.to_sib)):
            self.to_sib[n].wait_recv()
            for cp in self.sib_fwd[n]:
                cp.wait_recv()
            for cp in self.relayed[n] + self.direct[n] + self.relay[n] + [self.to_sib[n]] + self.sib_fwd[n]:
                cp.wait_send()


HBM = pl.BlockSpec(memory_space=pltpu.HBM)
SEM = pl.BlockSpec(memory_space=pltpu.SEMAPHORE)
FLOWS = pltpu.SideEffectType.DATAFLOW_SIDE_EFFECTING


def _in_hbm(a):
    return pltpu.with_memory_space_constraint(a, pltpu.HBM)


def _gather_start(wl, dests, ps, rows, barrier_id):
    lo = [sum(rows[:p]) for p in range(N_BIG)]
    n = len(ps)

    def body(*refs):
        wl_ref, dest_refs = refs[0], refs[1:1 + n]
        sends, recvs = refs[1 + n:1 + 2 * n], refs[1 + 2 * n:1 + 3 * n]
        token = refs[-1]
        _chips_handshake()
        x, y, c = _mesh_pos()
        jme = _dev_index(x, y, c)
        for i, p in enumerate(ps):
            mine = dest_refs[i].at[pl.ds(pl.multiple_of(jme * rows[p], BF16_ROWS), rows[p]), :]
            for chip in _other_chips(x, y):
                _remote(wl_ref.at[pl.ds(lo[p], rows[p]), :], mine, sends[i], recvs[i], (*chip, c)).start()
        token[...] = jnp.zeros_like(token)

    thru = [pltpu.HBM(wl.shape, wl.dtype)] + [pltpu.HBM(dests[p].shape, BF16) for p in ps]
    res = pl.pallas_call(
        body, name="gather_start",
        out_shape=tuple([pltpu.SemaphoreType.DMA(())] * (2 * n) + thru + [jax.ShapeDtypeStruct((SUBLANE, LANE), F32)]),
        in_specs=[HBM] * (1 + n), out_specs=tuple([SEM] * (2 * n) + [HBM] * (1 + n) + [VMEM]),
        input_output_aliases={i: 2 * n + i for i in range(1 + n)},
        compiler_params=pltpu.CompilerParams(has_side_effects=FLOWS, collective_id=barrier_id))(
            _in_hbm(wl), *[_in_hbm(dests[p]) for p in ps])
    sems = [(res[i], res[n + i]) for i in range(n)]
    return sems, res[2 * n], list(res[2 * n + 1:3 * n + 1]), res[-1]


def _gather_wait(wl, dest, sems, after, r, name):
    def body(wl_ref, dest_ref, send_sem, recv_sem, after_ref, wl_out, dest_out):
        x, y, c = _mesh_pos()
        three = dest_ref.at[pl.ds(0, 3 * r), :]
        cp = _remote(three, three, send_sem, recv_sem, (x, y, 1 - c))
        cp.wait_send()
        cp.wait_recv()

    res = pl.pallas_call(
        body, name=name, out_shape=(pltpu.HBM(wl.shape, wl.dtype), pltpu.HBM(dest.shape, dest.dtype)),
        in_specs=[HBM, HBM, SEM, SEM, ANY], out_specs=(HBM, HBM), input_output_aliases={0: 0, 1: 1},
        compiler_params=pltpu.CompilerParams(has_side_effects=FLOWS))(wl, dest, sems[0], sems[1], after)
    return res[0], res[1]


def _forward_comm(dest, r):
    def descs(ins, outs, scr):
        x, y, c = _mesh_pos()
        cps = []
        for k, chip in enumerate(_other_chips(x, y)):
            blk = outs[0].at[pl.ds(pl.multiple_of(_dev_index(*chip, c) * r, BF16_ROWS), r), :]
            cps.append(_SplitRemote(blk, blk, scr[0].at[k], scr[1].at[k], (x, y, 1 - c), r, D2D_CHUNKS))
        return cps

    def start(ins, outs, scr):
        for cp in descs(ins, outs, scr):
            cp.start()

    def finish(ins, outs, scr):
        cps = descs(ins, outs, scr)
        for cp in cps:
            cp.wait_recv()
        for cp in cps:
            cp.wait_send()

    return _Comm([dest], [jax.ShapeDtypeStruct(dest.shape, dest.dtype)], {0: 0},
                 [pltpu.SemaphoreType.DMA((3,)), pltpu.SemaphoreType.DMA((3,))], start, finish)


def _forward_now(dest, r, name):
    _, (dest,) = _host_call(lambda: None, grid=(), in_specs=[], out_specs=[], out_shape=[], args=(), name=name,
                            comm=_forward_comm(dest, r))
    return dest


def _pair_comm(g, r):
    d = g.shape[1]

    def descs(ins, outs, scr):
        x, y, c = _mesh_pos()
        chips = [(x, y)] + _other_chips(x, y)
        return [_SplitRemote(ins[0].at[pl.ds(pl.multiple_of(_dev_index(*chip, 1 - c) * r, BF16_ROWS), r), :], outs[0].at[k],
                             scr[0].at[k], scr[1].at[k], (x, y, 1 - c), r, D2D_CHUNKS) for k, chip in enumerate(chips)]

    def start(ins, outs, scr):
        for cp in descs(ins, outs, scr):
            cp.start()

    def finish(ins, outs, scr):
        cps = descs(ins, outs, scr)
        for cp in cps:
            cp.wait_recv()
        for cp in cps:
            cp.wait_send()

    comm = _Comm([g], [jax.ShapeDtypeStruct((4, r, d), BF16)], {},
                 [pltpu.SemaphoreType.DMA((4,)), pltpu.SemaphoreType.DMA((4,))], start, finish)
    return comm


def _pair_sum(g, pair, r, name, after=()):
    d = g.shape[1]

    def body(g_ref, p_ref, *rest):
        o_ref, gbuf, pbuf, sems = rest[len(after):]
        x, y, c = _mesh_pos()
        loads = [pltpu.make_async_copy(p_ref.at[pl.ds(1, 3)], pbuf, sems.at[3])]
        for k, chip in enumerate(_other_chips(x, y)):
            j = _dev_index(*chip, c)
            loads.append(pltpu.make_async_copy(g_ref.at[pl.ds(pl.multiple_of(j * r, BF16_ROWS), r), :], gbuf.at[k], sems.at[k]))
        for cp in loads:
            cp.start()
        for cp in loads:
            cp.wait()
        for k in range(3):
            o_ref[k] = (gbuf[k].astype(F32) + pbuf[k].astype(F32)).astype(BF16)

    return pl.pallas_call(
        body, out_shape=jax.ShapeDtypeStruct((3, r, d), BF16), in_specs=[ANY] * (2 + len(after)), out_specs=VMEM,
        scratch_shapes=[pltpu.VMEM((3, r, d), BF16), pltpu.VMEM((3, r, d), BF16), pltpu.SemaphoreType.DMA((4,))],
        name=name, compiler_params=_cparams())(g, pair, *after)


def _chip_start(sums, name, barrier_id):
    n = len(sums)

    def body(*refs):
        srcs, lands = refs[:n], refs[n:2 * n]
        sends, recvs = refs[2 * n:3 * n], refs[3 * n:4 * n]
        _chips_handshake()
        x, y, c = _mesh_pos()
        for i in range(n):
            for k, chip in enumerate(_other_chips(x, y)):
                _remote(srcs[i].at[k], lands[i].at[k], sends[i], recvs[i], (*chip, c)).start()
        refs[-1][...] = jnp.zeros_like(refs[-1])

    zones = [pltpu.HBM(s.shape, s.dtype) for s in sums]
    res = pl.pallas_call(
        body, name=name,
        out_shape=tuple([pltpu.SemaphoreType.DMA(())] * (2 * n) + zones + zones + [jax.ShapeDtypeStruct((SUBLANE, LANE), F32)]),
        in_specs=[HBM] * (2 * n), out_specs=tuple([SEM] * (2 * n) + [HBM] * (2 * n) + [VMEM]),
        input_output_aliases={i: 2 * n + i for i in range(2 * n)},
        compiler_params=pltpu.CompilerParams(has_side_effects=FLOWS, collective_id=barrier_id))(
            *[_in_hbm(s) for s in sums], *[_in_hbm(lax.empty(s.shape, s.dtype)) for s in sums])
    flights = [((res[i], res[n + i]), res[2 * n + i], res[3 * n + i]) for i in range(n)]
    return flights, res[-1]


def _chip_wait(sums, land, sems, after, name):
    def body(sums_ref, land_ref, send_sem, recv_sem, after_ref, sums_out, land_out):
        x, y, c = _mesh_pos()
        cp = _remote(sums_ref, land_ref, send_sem, recv_sem, (x, y, 1 - c))
        cp.wait_send()
        cp.wait_recv()

    res = pl.pallas_call(
        body, name=name, out_shape=(pltpu.HBM(sums.shape, sums.dtype), pltpu.HBM(land.shape, land.dtype)),
        in_specs=[HBM, HBM, SEM, SEM, ANY], out_specs=(HBM, HBM), input_output_aliases={0: 0, 1: 1},
        compiler_params=pltpu.CompilerParams(has_side_effects=FLOWS))(sums, land, sems[0], sems[1], after)
    return res[1]


class _CopyThrough:
    def __init__(self, src_ref, dst_ref, dst_row0, n_rows, buf, sem_in, sem_out):
        rc = n_rows // COPY_PIECES
        piece = lambda ref, o: ref.at[pl.ds(o, rc), :]
        self.loads = [pltpu.make_async_copy(piece(src_ref, k * rc), piece(buf, k * rc), sem_in) for k in range(COPY_PIECES)]
        self.stores = [pltpu.make_async_copy(piece(buf, k * rc), piece(dst_ref, dst_row0 + k * rc), sem_out) for k in range(COPY_PIECES)]
        self.all_in = pltpu.make_async_copy(src_ref, buf, sem_in)
        self.all_out = pltpu.make_async_copy(buf, dst_ref.at[pl.ds(dst_row0, n_rows), :], sem_out)

    def load(self):
        for cp in self.loads:
            cp.start()

    def store(self):
        self.all_in.wait()
        for cp in self.stores:
            cp.start()

    def done(self):
        self.all_out.wait()


def _gather_first(shards, sm, jobs, x2, tgt2, t_rows, x0):
    d = shards[0].shape[1]
    rows = [w.shape[0] for w in shards]
    lo = [sum(rows[:p]) for p in range(N_BIG)]
    n_sems = GATHER_SEMS * len(jobs)
    seq = x2.shape[0]
    assert x0 == ROW_ALIGN and seq % ROW_ALIGN == 0 and d == N_DEV * LANE

    def body(s0, s1, s2, s3, s4, sm_ref, x_ref, tgt_ref, wl_ref, o0, o1, o2, o3, o4, sa_ref, h0_ref, tp_ref,
             wl_v, x_v, tgt_v, heads_v, sa_v, send_sems, recv_sems, ssend, srecv, local_sems, sems_in, sems_out):
        dests = (o0, o1, o2, o3, o4)
        x, y, c = _mesh_pos()
        me = (x, y, c)
        jme = _dev_index(*me)
        padded = [_CopyThrough(x_ref, h0_ref, x0, seq, x_v, sems_in.at[0], sems_out.at[0]),
                  _CopyThrough(tgt_ref, tp_ref, x0, seq, tgt_v, sems_in.at[1], sems_out.at[1])]
        for cp in padded:
            cp.load()
        shard_refs = (s0, s1, s2, s3, s4)
        first = sorted({j[0] for j in jobs})
        for p in first + [p for p in range(N_BIG) if p not in first]:
            wl_v[pl.ds(lo[p], rows[p]), :] = shard_refs[p][...].astype(BF16)
            if p == first[-1]:
                gather = _Gather(jobs, rows, lo, wl_v, dict(enumerate(dests)), send_sems, recv_sems)
                gather.start()
        peers = [(x, y, 1 - c)] + [(*chip, pc) for pc in (c, 1 - c) for chip in _other_chips(x, y)]
        smalls = [_remote(sm_ref, sa_ref.at[jme], ssend.at[k], srecv.at[k], to) for k, to in enumerate(peers)]
        for cp in smalls:
            cp.start()
        mine = [pltpu.make_async_copy(wl_v.at[pl.ds(lo[p], rows[p]), :],
                                      dests[p].at[pl.ds(pl.multiple_of(jme * rows[p], BF16_ROWS), rows[p]), :], local_sems.at[p])
                for p in range(N_BIG)]
        mine.append(pltpu.make_async_copy(wl_v, wl_ref, local_sems.at[N_BIG]))
        mine.append(pltpu.make_async_copy(sm_ref, sa_ref.at[jme], local_sems.at[N_BIG + 1]))
        for cp in mine:
            cp.start()
        later = [p for p in range(N_BIG) if p not in {j[0] for j in jobs}]
        own = [_SplitRemote(wl_v.at[pl.ds(lo[p], rows[p]), :],
                            dests[p].at[pl.ds(pl.multiple_of(jme * rows[p], BF16_ROWS), rows[p]), :],
                            ssend.at[7 + i], srecv.at[7 + i], (x, y, 1 - c), rows[p], D2D_CHUNKS) for i, p in enumerate(later)]
        for cp in own:
            cp.start()
        for cp in padded:
            cp.store()
        gather.mid()
        for cp in smalls + own:
            cp.wait_recv()
        mine[-1].wait()
        to_v = pltpu.make_async_copy(sa_ref, sa_v, local_sems.at[N_BIG + 1])
        to_v.start()
        to_v.wait()
        head, zeros = heads_v.at[0], heads_v.at[1]
        head[...] = jnp.zeros_like(head)
        zeros[...] = jnp.zeros_like(zeros)
        for j in range(N_DEV):
            head[pl.ds(x0 - N_META, N_META), pl.ds(j * LANE, LANE)] = sa_v[j, pl.ds(0, N_META), :]
        heads = [pltpu.make_async_copy(head, h0_ref.at[pl.ds(0, x0), :], local_sems.at[N_BIG + 1]),
                 pltpu.make_async_copy(zeros, tp_ref.at[pl.ds(0, x0), :], local_sems.at[N_BIG + 2])]
        for cp in heads:
            cp.start()
        gather.finish()
        for cp in smalls + own:
            cp.wait_send()
        for cp in mine[:-1] + heads:
            cp.wait()
        for cp in padded:
            cp.done()

    out_shape = [jax.ShapeDtypeStruct((sum(rows), d), BF16)]
    out_shape += [jax.ShapeDtypeStruct((N_DEV * r, d), BF16) for r in rows]
    out_shape.append(jax.ShapeDtypeStruct((N_DEV,) + sm.shape, F32))
    out_shape += [jax.ShapeDtypeStruct((t_rows, d), F32)] * 2
    res = pl.pallas_call(
        body, out_shape=out_shape, in_specs=[VMEM] * 6 + [ANY] * 2, out_specs=[ANY] * 9,
        scratch_shapes=[pltpu.VMEM((sum(rows), d), BF16), pltpu.VMEM((seq, d), F32), pltpu.VMEM((seq, d), F32),
                        pltpu.VMEM((2, ROW_ALIGN, d), F32), pltpu.VMEM((N_DEV,) + sm.shape, F32),
                        pltpu.SemaphoreType.DMA((n_sems,)), pltpu.SemaphoreType.DMA((n_sems,)),
                        pltpu.SemaphoreType.DMA((7 + N_BIG,)), pltpu.SemaphoreType.DMA((7 + N_BIG,)),
                        pltpu.SemaphoreType.DMA((N_BIG + 3,)), pltpu.SemaphoreType.DMA((2,)), pltpu.SemaphoreType.DMA((2,))],
        name="gather_first", compiler_params=_cparams())(*shards, sm, x2, tgt2)
    return res[0], list(res[1:1 + N_BIG]), res[1 + N_BIG], res[2 + N_BIG], res[3 + N_BIG]


def _in_proj(h0, g1, win_t, tm, comm):
    t_rows, d = h0.shape
    e = win_t.shape[0]

    def body(h_ref, g_ref, w_ref, xn_ref, hin_ref):
        h = h_ref[...]
        xn = ((h * _rstd(h)) * g_ref[...]).astype(BF16)
        xn_ref[...] = xn
        for o, n in _chunks(e, N_CHUNK):
            hin_ref[:, pl.ds(o, n)] = _dot_nt(xn, w_ref[pl.ds(o, n), :])

    return _host_call(
        body, grid=(t_rows // tm,),
        in_specs=[pl.BlockSpec((tm, d), lambda i: (i, 0)), _full((1, d)), _resident((e, d))],
        out_specs=[pl.BlockSpec((tm, d), lambda i: (i, 0)), pl.BlockSpec((tm, e), lambda i: (i, 0))],
        out_shape=[jax.ShapeDtypeStruct((t_rows, d), BF16), jax.ShapeDtypeStruct((t_rows, e), F32)],
        args=(h0, g1, win_t), name="in_proj", comm=comm)


def _tap_slot(off):
    return off % SUBLANE, (off // SUBLANE) * SUBLANE


def _fill_shifted(sh_ref, base_ref, residues, n_rows):
    for r in residues:
        if r:
            sh_ref[r] = base_ref[pl.ds(r, n_rows), :]


def _shifted_rows(pair, r, start, n):
    base_ref, sh_ref = pair
    return base_ref[pl.ds(start, n), :] if r == 0 else sh_ref[r, pl.ds(start, n), :]


def _mix_conv_fwd(hin, wa, wb, bb, wa_w, comm):
    t_rows = hin.shape[0]
    nt = wa_w // LANE
    ka, kb = wa.shape[0], wb.shape[0]
    nr = CONV_HALO + t_rows

    def body(bg_ref, cg_ref, ha_ref, val_ref, gt_ref, wa_ref, wb_ref, bb_ref, ya_ref, z_ref, base, sh):
        base[pl.ds(0, CONV_HALO), :] = jnp.zeros((CONV_HALO, LANE), F32)
        base[pl.ds(nr, SUBLANE), :] = jnp.zeros((SUBLANE, LANE), F32)

        def conv(w_ref, k_taps, b, n):
            acc = None
            for k in range(k_taps):
                r, q = _tap_slot(CONV_HALO - (k_taps - 1) + k)
                term = w_ref[pl.ds(k, 1), :] * _shifted_rows((base, sh), r, b + q, n)
                acc = term if acc is None else acc + term
            return acc

        def fill_a(b, c):
            base[pl.ds(CONV_HALO + b, CONV_CHUNK), :] = cg_ref[pl.ds(b, CONV_CHUNK), :] * ha_ref[pl.ds(b, CONV_CHUNK), :]
            return c

        _row_loop(t_rows, CONV_CHUNK, fill_a)
        _fill_shifted(sh, base, sorted({_tap_slot(CONV_HALO - (ka - 1) + k)[0] for k in range(ka)}), nr)

        def out_a(b, c):
            ya_ref[pl.ds(b, CONV_CHUNK), :] = (bg_ref[pl.ds(b, CONV_CHUNK), :] * conv(wa_ref, ka, b, CONV_CHUNK)).astype(BF16)
            return c

        _row_loop(t_rows, CONV_CHUNK, out_a)

        def fill_b(b, c):
            base[pl.ds(CONV_HALO + b, CONV_CHUNK), :] = (val_ref[pl.ds(b, CONV_CHUNK), :]
                                                          * jax.nn.sigmoid(gt_ref[pl.ds(b, CONV_CHUNK), :]))
            return c

        _row_loop(t_rows, CONV_CHUNK, fill_b)
        _fill_shifted(sh, base, range(SUBLANE), nr)

        def out_b(b, c):
            z_ref[pl.ds(b, CONV_CHUNK), :] = conv(wb_ref, kb, b, CONV_CHUNK) + bb_ref[...]
            return c

        _row_loop(t_rows, CONV_CHUNK, out_b)

    def col(g):
        return pl.BlockSpec((t_rows, LANE), lambda i, g=g: (0, g * nt + i))

    tile = lambda rows: pl.BlockSpec((rows, LANE), lambda i: (0, i))
    return _host_call(
        body, grid=(nt,),
        in_specs=[col(0), col(1), col(2), col(3), col(4), tile(ka), tile(kb), tile(1)],
        out_specs=[tile(t_rows), tile(t_rows)],
        out_shape=[jax.ShapeDtypeStruct((t_rows, wa_w), BF16), jax.ShapeDtypeStruct((t_rows, wa_w), F32)],
        scratch_shapes=[pltpu.VMEM((nr + SUBLANE, LANE), F32), pltpu.VMEM((SUBLANE, nr, LANE), F32)],
        args=(hin, hin, hin, hin, hin, wa, wb, bb), name="mix_conv_fwd", comm=comm)


def _ln_parts(z, lg, lb):
    mu = jnp.mean(z, axis=-1, keepdims=True)
    zc = z - mu
    rstd = lax.rsqrt(jnp.mean(zc * zc, axis=-1, keepdims=True) + LN_EPS)
    zh = zc * rstd
    return zh, rstd, zh * lg + lb


def _out_proj(ya, z, lg, lb, w_out, h0, g2, g3, tm, comm):
    t_rows, d = h0.shape
    w = z.shape[1]

    def body(ya_ref, z_ref, lg_ref, lb_ref, w_ref, h0_ref, g2_ref, g3_ref, y_ref, mix_ref, h1_ref, xn2_ref):
        _, _, ln = _ln_parts(z_ref[...], lg_ref[...], lb_ref[...])
        y_ref[:, pl.ds(0, w)] = ya_ref[...]
        y_ref[:, pl.ds(w, w)] = (ln * jax.nn.sigmoid(ln)).astype(BF16)
        mix = _dot_nn(y_ref[...], w_ref[...])
        mix_ref[...] = mix
        h1 = h0_ref[...] + (mix * _rstd(mix)) * g2_ref[...]
        h1_ref[...] = h1
        xn2_ref[...] = ((h1 * _rstd(h1)) * g3_ref[...]).astype(BF16)

    blk = pl.BlockSpec((tm, d), lambda i: (i, 0))
    half = pl.BlockSpec((tm, w), lambda i: (i, 0))
    return _host_call(
        body, grid=(t_rows // tm,),
        in_specs=[half, half, _full((1, w)), _full((1, w)), _resident(w_out.shape), blk, _full((1, d)), _full((1, d))],
        out_specs=[blk, blk, blk, blk],
        out_shape=[jax.ShapeDtypeStruct((t_rows, d), BF16), jax.ShapeDtypeStruct((t_rows, d), F32),
                   jax.ShapeDtypeStruct((t_rows, d), F32), jax.ShapeDtypeStruct((t_rows, d), BF16)],
        args=(ya, z, lg, lb, w_out, h0, g2, g3), name="out_proj", comm=comm)


def _gate_up(xn2, wg_t, wu_t, tm, comm):
    t_rows, d = xn2.shape
    f = wg_t.shape[0]

    def body(x_ref, wg_ref, wu_ref, ga_ref, gu_ref, s_ref):
        xn = x_ref[...]
        for o, n in _chunks(f, N_CHUNK):
            a = _dot_nt(xn, wg_ref[pl.ds(o, n), :])
            u = _dot_nt(xn, wu_ref[pl.ds(o, n), :])
            sig = jax.nn.sigmoid(a)
            silu = a * sig
            s = silu * u
            gu_ref[:, pl.ds(o, n)] = silu.astype(BF16)
            ga_ref[:, pl.ds(o, n)] = (u * sig + s * (1.0 - sig)).astype(BF16)
            s_ref[:, pl.ds(o, n)] = s.astype(BF16)

    blk = pl.BlockSpec((tm, f), lambda i: (i, 0))
    return _host_call(
        body, grid=(t_rows // tm,),
        in_specs=[pl.BlockSpec((tm, d), lambda i: (i, 0)), _resident((f, d)), _resident((f, d))],
        out_specs=[blk, blk, blk], out_shape=[jax.ShapeDtypeStruct((t_rows, f), BF16)] * 3,
        args=(xn2, wg_t, wu_t), name="gate_up", comm=comm)


def _down_loss(s, wd, h1, tgt, g4, tm, x0):
    t_rows, d = h1.shape
    f = wd.shape[0]

    def body(s_ref, w_ref, h1_ref, tgt_ref, g4_ref, dh2_ref, dff_ref, dg4_ref, loss_ref):
        i = pl.program_id(0)
        ff = _dot_nn(s_ref[...], w_ref[...])
        r4 = _rstd(ff)
        fh = ff * r4
        g4 = g4_ref[...]
        h2 = h1_ref[...] + fh * g4
        row = i * tm + lax.broadcasted_iota(jnp.int32, (tm, 1), 0)
        diff = jnp.where(row >= x0, h2 - tgt_ref[...], 0.0)
        dh2 = diff / d
        dh2_ref[...] = dh2
        dff_ref[...] = _rms_bwd(dh2 * g4, fh, r4).astype(BF16)
        _acc_rows(dg4_ref, dh2 * fh, i == 0)
        _acc_rows(loss_ref, diff * diff, i == 0)

    blk = pl.BlockSpec((tm, d), lambda i: (i, 0))
    res, _ = _host_call(
        body, grid=(t_rows // tm,),
        in_specs=[pl.BlockSpec((tm, f), lambda i: (i, 0)), _resident((f, d)), blk, blk, _full((1, d))],
        out_specs=[blk, blk, _full((1, d)), _full((1, d))],
        out_shape=[jax.ShapeDtypeStruct((t_rows, d), F32), jax.ShapeDtypeStruct((t_rows, d), BF16),
                   jax.ShapeDtypeStruct((1, d), F32), jax.ShapeDtypeStruct((1, d), F32)],
        args=(s, wd, h1, tgt, g4), name="down_loss")
    return res


def _bwd_down(dff, wd, ga, gu, tm, comm):
    t_rows, d = dff.shape
    f = wd.shape[0]

    def body(dff_ref, w_ref, ga_ref, gu_ref, da_ref, du_ref):
        dff_v = dff_ref[...]
        for o, n in _chunks(f, N_CHUNK):
            ds = _dot_nt(dff_v, w_ref[pl.ds(o, n), :]).astype(BF16)
            da_ref[:, pl.ds(o, n)] = ds * ga_ref[:, pl.ds(o, n)]
            du_ref[:, pl.ds(o, n)] = ds * gu_ref[:, pl.ds(o, n)]

    blk = pl.BlockSpec((tm, f), lambda i: (i, 0))
    return _host_call(
        body, grid=(t_rows // tm,),
        in_specs=[pl.BlockSpec((tm, d), lambda i: (i, 0)), _resident((f, d)), blk, blk],
        out_specs=[blk, blk], out_shape=[jax.ShapeDtypeStruct((t_rows, f), BF16)] * 2,
        args=(dff, wd, ga, gu), name="bwd_down", comm=comm)


def _wgrad(a, b, name, after=()):
    d = b.shape[1]
    t_rows = b.shape[0]
    stacked = a.ndim == 3
    n = a.shape[-1]
    groups = a.shape[0] if stacked else 1
    steps = 1 if stacked else 2
    tile = max(t for t in range(LANE, min(n // steps, WGRAD_TILE_MAX) + 1, LANE) if n % t == 0)
    tiles = n // tile

    def body(a_ref, b_ref, o_ref):
        o_ref[...] = lax.dot_general(a_ref[...], b_ref[...], (((0,), (0,)), ((), ())),
                                     preferred_element_type=F32).astype(BF16)

    if stacked:
        a_spec = pl.BlockSpec((None, t_rows, tile), lambda g, i: (g, 0, i))
    else:
        a_spec = pl.BlockSpec((t_rows, tile), lambda g, i: (0, i))
    res, _ = _host_call(
        body, grid=(groups, tiles), in_specs=[a_spec, _resident((t_rows, d))],
        out_specs=[pl.BlockSpec((tile, d), lambda g, i: (g * tiles + i, 0))],
        out_shape=[jax.ShapeDtypeStruct((groups * n, d), BF16)], args=(a, b), name=name, after=after)
    return res[0]


def _bwd_ffn_in(da, du, wg_t, wu_t, h1, dh2, g3, tm, comm):
    t_rows, d = h1.shape
    f = wg_t.shape[0]

    def body(da_ref, du_ref, wg_ref, wu_ref, h1_ref, dh2_ref, g3_ref, dh1_ref, dg3_ref):
        dxn2 = _dot_nn(da_ref[...], wg_ref[...]) + _dot_nn(du_ref[...], wu_ref[...])
        h1 = h1_ref[...]
        r3 = _rstd(h1)
        h1h = h1 * r3
        _acc_rows(dg3_ref, dxn2 * h1h, pl.program_id(0) == 0)
        dh1_ref[...] = dh2_ref[...] + _rms_bwd(dxn2 * g3_ref[...], h1h, r3)

    blk = pl.BlockSpec((tm, d), lambda i: (i, 0))
    blkf = pl.BlockSpec((tm, f), lambda i: (i, 0))
    return _host_call(
        body, grid=(t_rows // tm,),
        in_specs=[blkf, blkf, _resident((f, d)), _resident((f, d)), blk, blk, _full((1, d))],
        out_specs=[blk, _full((1, d))],
        out_shape=[jax.ShapeDtypeStruct((t_rows, d), F32), jax.ShapeDtypeStruct((1, d), F32)],
        args=(da, du, wg_t, wu_t, h1, dh2, g3), name="bwd_ffn_in", comm=comm)


def _bwd_out_proj(dh1, mix, w_out, g2, z, lg, lb, tm, after):
    t_rows, d = dh1.shape
    w = z.shape[1]

    def body(dh1_ref, mix_ref, w_ref, g2_ref, z_ref, lg_ref, lb_ref, dmix_ref, dya_ref, dz_ref, dg2_ref, dlg_ref, dlb_ref, dbb_ref):
        first = pl.program_id(0) == 0
        mix = mix_ref[...]
        r2 = _rstd(mix)
        mh = mix * r2
        dh1 = dh1_ref[...]
        _acc_rows(dg2_ref, dh1 * mh, first)
        dmix = _rms_bwd(dh1 * g2_ref[...], mh, r2).astype(BF16)
        dmix_ref[...] = dmix
        dy = _dot_nt(dmix, w_ref[...])
        dya_ref[...] = dy[:, :w]
        lg = lg_ref[...]
        zh, rstd, ln = _ln_parts(z_ref[...], lg, lb_ref[...])
        dln = dy[:, w:] * _silu_grad(ln, jax.nn.sigmoid(ln))
        _acc_rows(dlg_ref, dln * zh, first)
        _acc_rows(dlb_ref, dln, first)
        dzh = dln * lg
        dz = rstd * (dzh - jnp.mean(dzh, axis=-1, keepdims=True) - zh * jnp.mean(dzh * zh, axis=-1, keepdims=True))
        dz_ref[...] = dz
        _acc_rows(dbb_ref, dz, first)

    blk = pl.BlockSpec((tm, d), lambda i: (i, 0))
    half = pl.BlockSpec((tm, w), lambda i: (i, 0))
    vec = _full((1, w))
    res, _ = _host_call(
        body, grid=(t_rows // tm,), in_specs=[blk, blk, _resident(w_out.shape), _full((1, d)), half, vec, vec],
        out_specs=[blk, half, half, _full((1, d)), vec, vec, vec],
        out_shape=[jax.ShapeDtypeStruct((t_rows, d), BF16), jax.ShapeDtypeStruct((t_rows, w), F32),
                   jax.ShapeDtypeStruct((t_rows, w), F32), jax.ShapeDtypeStruct((1, d), F32)]
        + [jax.ShapeDtypeStruct((1, w), F32)] * 3,
        args=(dh1, mix, w_out, g2, z, lg, lb), name="bwd_out_proj", after=after)
    return res


def _mix_conv_bwd(hin, dy, dz, wa, wb, wa_w, comm):
    t_rows = hin.shape[0]
    nt = wa_w // LANE
    ka, kb = wa.shape[0], wb.shape[0]
    nr = CONV_HALO + t_rows
    kb_rows = -(-kb // SUBLANE) * SUBLANE

    def body(bg_ref, cg_ref, ha_ref, val_ref, gt_ref, dya_ref, dz_ref, wa_ref, wb_ref,
             dh_ref, dwa_ref, dwb_ref, base, sh, based, shd, tmp, wbc):
        zeros = lambda n: jnp.zeros((n, LANE), F32)
        base[pl.ds(0, CONV_HALO), :] = zeros(CONV_HALO)
        base[pl.ds(nr, SUBLANE), :] = zeros(SUBLANE)
        based[pl.ds(t_rows, CONV_HALO + SUBLANE), :] = zeros(CONV_HALO + SUBLANE)

        def fwd_slot(k_taps, k):
            return _tap_slot(CONV_HALO - (k_taps - 1) + k)

        def bwd_slot(k_taps, k):
            return _tap_slot(k_taps - 1 - k)

        def conv(w_ref, k_taps, src, slot, b, n):
            acc = None
            for k in range(k_taps):
                r, q = slot(k_taps, k)
                term = w_ref[pl.ds(k, 1), :] * _shifted_rows(src, r, b + q, n)
                acc = term if acc is None else acc + term
            return acc

        def by_residue(k_taps, slot):
            groups = {}
            for k in range(k_taps):
                r, q = slot(k_taps, k)
                groups.setdefault(r, []).append((k, q // SUBLANE))
            return groups

        def wgrad_loop(w_ref, k_taps):
            n_sub = WGRAD_ROWS // SUBLANE
            for k in range(k_taps):
                wbc[k] = jnp.broadcast_to(w_ref[pl.ds(k, 1), :], (SUBLANE, LANE))
            fwd, bwd = by_residue(k_taps, fwd_slot), by_residue(k_taps, bwd_slot)

            def window(src, r, taps, b):
                span = n_sub + max(qi for _, qi in taps)
                return [_shifted_rows(src, r, b + SUBLANE * i, SUBLANE) for i in range(span)]

            def step(b, accs):
                accs = list(accs)
                dv = [based[pl.ds(b + SUBLANE * j, SUBLANE), :] for j in range(n_sub)]
                for r, taps in fwd.items():
                    win = window((base, sh), r, taps, b)
                    for k, qi in taps:
                        t = dv[0] * win[qi]
                        for j in range(1, n_sub):
                            t = t + dv[j] * win[qi + j]
                        accs[k] = accs[k] + t
                outs = [None] * n_sub
                for r, taps in bwd.items():
                    win = window((based, shd), r, taps, b)
                    for k, qi in taps:
                        wk = wbc[k]
                        for j in range(n_sub):
                            term = wk * win[qi + j]
                            outs[j] = term if outs[j] is None else outs[j] + term
                for j in range(n_sub):
                    tmp[pl.ds(b + SUBLANE * j, SUBLANE), :] = outs[j]
                return tuple(accs)

            return _row_loop(t_rows, WGRAD_ROWS, step, tuple(zeros(SUBLANE) for _ in range(k_taps)))

        def store_taps(ref, accs, rows):
            for k, acc in enumerate(accs):
                ref[pl.ds(k, 1), :] = jnp.sum(acc, axis=0, keepdims=True)
            if rows > len(accs):
                ref[pl.ds(len(accs), rows - len(accs)), :] = zeros(rows - len(accs))

        def fill_a(b, c):
            sl = pl.ds(b, CONV_CHUNK)
            base[pl.ds(CONV_HALO + b, CONV_CHUNK), :] = cg_ref[sl, :] * ha_ref[sl, :]
            based[sl, :] = dya_ref[sl, :] * bg_ref[sl, :]
            return c

        _row_loop(t_rows, CONV_CHUNK, fill_a)
        _fill_shifted(sh, base, sorted({fwd_slot(ka, k)[0] for k in range(ka)}), nr)
        _fill_shifted(shd, based, sorted({bwd_slot(ka, k)[0] for k in range(ka)}), nr)

        def d_bgate(b, c):
            sl = pl.ds(b, CONV_CHUNK)
            dh_ref[0, sl, :] = (dya_ref[sl, :] * conv(wa_ref, ka, (base, sh), fwd_slot, b, CONV_CHUNK)).astype(BF16)
            return c

        _row_loop(t_rows, CONV_CHUNK, d_bgate)
        store_taps(dwa_ref, wgrad_loop(wa_ref, ka), SUBLANE)

        def d_ch(b, c):
            sl = pl.ds(b, CONV_CHUNK)
            dua = tmp[sl, :]
            dh_ref[1, sl, :] = (dua * ha_ref[sl, :]).astype(BF16)
            dh_ref[2, sl, :] = (dua * cg_ref[sl, :]).astype(BF16)
            return c

        _row_loop(t_rows, CONV_CHUNK, d_ch)

        def fill_b(b, c):
            sl = pl.ds(b, CONV_CHUNK)
            base[pl.ds(CONV_HALO + b, CONV_CHUNK), :] = val_ref[sl, :] * jax.nn.sigmoid(gt_ref[sl, :])
            based[sl, :] = dz_ref[sl, :]
            return c

        _row_loop(t_rows, CONV_CHUNK, fill_b)
        _fill_shifted(sh, base, range(SUBLANE), nr)
        _fill_shifted(shd, based, range(SUBLANE), nr)
        store_taps(dwb_ref, wgrad_loop(wb_ref, kb), kb_rows)

        def d_glu(b, c):
            sl = pl.ds(b, CONV_CHUNK)
            dgg = tmp[sl, :]
            sig = jax.nn.sigmoid(gt_ref[sl, :])
            dh_ref[3, sl, :] = (dgg * sig).astype(BF16)
            dh_ref[4, sl, :] = (dgg * val_ref[sl, :] * (sig * (1.0 - sig))).astype(BF16)
            return c

        _row_loop(t_rows, CONV_CHUNK, d_glu)

    def col(g):
        return pl.BlockSpec((t_rows, LANE), lambda i, g=g: (0, g * nt + i))

    tile = lambda rows: pl.BlockSpec((rows, LANE), lambda i: (0, i))
    return _host_call(
        body, grid=(nt,),
        in_specs=[col(0), col(1), col(2), col(3), col(4), tile(t_rows), tile(t_rows), tile(ka), tile(kb)],
        out_specs=[pl.BlockSpec((5, t_rows, LANE), lambda i: (0, 0, i)), tile(SUBLANE), tile(kb_rows)],
        out_shape=[jax.ShapeDtypeStruct((5, t_rows, wa_w), BF16), jax.ShapeDtypeStruct((SUBLANE, wa_w), F32),
                   jax.ShapeDtypeStruct((kb_rows, wa_w), F32)],
        scratch_shapes=[pltpu.VMEM((nr + SUBLANE, LANE), F32), pltpu.VMEM((SUBLANE, nr, LANE), F32),
                        pltpu.VMEM((nr + SUBLANE, LANE), F32), pltpu.VMEM((SUBLANE, nr, LANE), F32),
                        pltpu.VMEM((t_rows, LANE), F32), pltpu.VMEM((kb_rows, SUBLANE, LANE), F32)],
        args=(hin, hin, hin, hin, hin, dy, dz, wa, wb), name="mix_conv_bwd", comm=comm)


def _bwd_in_proj(dh5, win_t, h0, dh1, g1, tm, comm):
    t_rows, d = h0.shape
    groups, _, w = dh5.shape

    def body(dh_ref, w_ref, h0_ref, dh1_ref, g1_ref, dh0_ref, dg1_ref):
        dxn1 = None
        for g in range(groups):
            part = _dot_nn(dh_ref[g], w_ref[pl.ds(g * w, w), :])
            dxn1 = part if dxn1 is None else dxn1 + part
        h0 = h0_ref[...]
        r1 = _rstd(h0)
        h0h = h0 * r1
        _acc_rows(dg1_ref, dxn1 * h0h, pl.program_id(0) == 0)
        dh0_ref[...] = dh1_ref[...] + _rms_bwd(dxn1 * g1_ref[...], h0h, r1)

    blk = pl.BlockSpec((tm, d), lambda i: (i, 0))
    return _host_call(
        body, grid=(t_rows // tm,),
        in_specs=[pl.BlockSpec((groups, tm, w), lambda i: (0, i, 0)), _resident(win_t.shape), blk, blk, _full((1, d))],
        out_specs=[blk, _full((1, d))],
        out_shape=[jax.ShapeDtypeStruct((t_rows, d), F32), jax.ShapeDtypeStruct((1, d), F32)],
        args=(dh5, win_t, h0, dh1, g1), name="bwd_in_proj", comm=comm)


def _pair_small(smalls, d):
    (dmeta, dg1, dg2, dg3, dg4, dbb, dlg, dlb, lossv, dwa, dwb) = smalls
    half = d // 2
    kb_rows = dwb.shape[0]

    def body(dmeta_ref, dg1_ref, dg2_ref, dg3_ref, dg4_ref, dbb_ref, dlg_ref, dlb_ref, loss_ref, dwa_ref, dwb_ref,
             sums_ref, pbuf, psib, ps_send, ps_recv):
        x, y, c = _mesh_pos()
        _pair_handshake()
        pbuf[...] = jnp.zeros_like(pbuf)
        pbuf[pl.ds(0, N_META), :] = dmeta_ref[...]
        for row, ref in ((16, dg1_ref), (17, dg2_ref), (18, dg3_ref), (19, dg4_ref)):
            pbuf[pl.ds(row, 1), :] = ref[...]
        pbuf[pl.ds(20, 1), pl.ds(0, half)] = dbb_ref[...]
        pbuf[pl.ds(20, 1), pl.ds(half, half)] = dlg_ref[...]
        pbuf[pl.ds(21, 1), pl.ds(0, half)] = dlb_ref[...]
        lv = loss_ref[...]
        pbuf[pl.ds(21, 1), pl.ds(half, half)] = lv[:, :half] + lv[:, half:]
        pbuf[pl.ds(24, SUBLANE), pl.ds(0, half)] = dwa_ref[...]
        pbuf[pl.ds(32, kb_rows), pl.ds(0, half)] = dwb_ref[...]
        to_sib = _remote(pbuf, psib, ps_send.at[0], ps_recv.at[0], (x, y, 1 - c))
        to_sib.start()
        to_sib.wait_recv()
        s = pbuf[...] + psib[...]
        for k in range(3):
            sums_ref[k] = s
        to_sib.wait_send()

    return pl.pallas_call(
        body, out_shape=jax.ShapeDtypeStruct((3, SMALL_ROWS, d), F32), in_specs=[VMEM] * 11, out_specs=VMEM,
        scratch_shapes=[pltpu.VMEM((SMALL_ROWS, d), F32), pltpu.VMEM((SMALL_ROWS, d), F32),
                        pltpu.SemaphoreType.DMA((1,)), pltpu.SemaphoreType.DMA((1,))],
        name="pair_small", compiler_params=pltpu.CompilerParams(vmem_limit_bytes=VMEM_LIMIT, collective_id=PAIR_BARRIER_ID))(*smalls)


def _total_small(own, others):
    _, r, d = own.shape

    def body(own_ref, others_ref, tot_ref, chip_p):
        x, y, _ = _mesh_pos()
        chip_p[2 * x + y] = own_ref[0]
        for k, (cx, cy) in enumerate(_other_chips(x, y)):
            chip_p[2 * cx + cy] = others_ref[k]
        tot_ref[...] = ((chip_p[0] + chip_p[1]) + chip_p[2]) + chip_p[3]

    return pl.pallas_call(body, out_shape=jax.ShapeDtypeStruct((r, d), F32), scratch_shapes=[pltpu.VMEM((4, r, d), F32)],
                          name="total_small", compiler_params=_cparams())(own, others)


def _adamw(w, g, m, v):
    m = ADAM_B1 * m + (1.0 - ADAM_B1) * g
    v = ADAM_B2 * v + (1.0 - ADAM_B2) * jnp.square(g)
    m_hat = m / (1.0 - ADAM_B1 ** ADAM_STEP)
    v_hat = v / (1.0 - ADAM_B2 ** ADAM_STEP)
    delta = -ADAM_LR * (m_hat / (jnp.sqrt(v_hat) + ADAM_EPS) + ADAM_WD * w)
    return delta, m, v


def _adam_big(g, pair, part, w, m, v, name):
    r, d = w.shape
    cols = d // ADAM_COL_BLOCKS

    def body(me_ref, g_ref, pair_ref, part_ref, w_ref, m_ref, v_ref, go_ref, d_ref, mo_ref, vo_ref):
        g = g_ref[...].astype(F32) + pair_ref[...].astype(F32)
        for k in range(3):
            g = g + part_ref[k].astype(F32)
        go_ref[...] = g
        d_ref[...], mo_ref[...], vo_ref[...] = _adamw(w_ref[...], g, m_ref[...], v_ref[...])

    blk = pl.BlockSpec((r, cols), lambda i, me_ref: (0, i))
    grid_spec = pltpu.PrefetchScalarGridSpec(
        num_scalar_prefetch=1, grid=(ADAM_COL_BLOCKS,),
        in_specs=[pl.BlockSpec((r, cols), lambda i, me_ref: (me_ref[0], i)),
                  pl.BlockSpec((None, r, cols), lambda i, me_ref: (0, 0, i)),
                  pl.BlockSpec((3, r, cols), lambda i, me_ref: (0, 0, i)), blk, blk, blk],
        out_specs=[blk, blk, blk, blk])
    me = jnp.reshape(_dev_index(*_mesh_pos()), (1,)).astype(jnp.int32)
    return pl.pallas_call(body, out_shape=[jax.ShapeDtypeStruct((r, d), F32)] * 4, grid_spec=grid_spec, name=name,
                          compiler_params=_cparams(1))(me, g, pair, part, w, m, v)


def _adam_small(gs, ws, ms, vs):
    n = len(gs)

    def body(*refs):
        ins, outs = refs[:4 * n], refs[4 * n:]
        for i in range(n):
            g = ins[i][...]
            delta, m, v = _adamw(ins[n + i][...], g, ins[2 * n + i][...], ins[3 * n + i][...])
            outs[i][...] = delta
            outs[n + i][...] = m
            outs[2 * n + i][...] = v

    shapes = [jax.ShapeDtypeStruct(w.shape, F32) for w in ws]
    return pl.pallas_call(body, out_shape=shapes * 3, name="adam_small", compiler_params=_cparams())(*gs, *ws, *ms, *vs)


def kernel(x, meta_tokens, pre_mix_norm, w_in, conv_a_w, conv_b_w, conv_b_bias, ln_b_gain, ln_b_bias, w_out, post_mix_norm, pre_ffn_norm, w_gate, w_up, w_down, post_ffn_norm, loss_target, m_meta_tokens, m_pre_mix_norm, m_w_in, m_conv_a_w, m_conv_b_w, m_conv_b_bias, m_ln_b_gain, m_ln_b_bias, m_w_out, m_post_mix_norm, m_pre_ffn_norm, m_w_gate, m_w_up, m_w_down, m_post_ffn_norm, v_meta_tokens, v_pre_mix_norm, v_w_in, v_conv_a_w, v_conv_b_w, v_conv_b_bias, v_ln_b_gain, v_ln_b_bias, v_w_out, v_post_mix_norm, v_pre_ffn_norm, v_w_gate, v_w_up, v_w_down, v_post_ffn_norm):
    _, seq, d = x.shape
    ka, ca_loc = conv_a_w.shape[1:]
    kb, cb_loc = conv_b_w.shape[1:]
    wa_w = ca_loc * N_DEV
    assert cb_loc == ca_loc and wa_w % LANE == 0 and w_in.shape[2] * N_DEV == 5 * wa_w and 2 * wa_w == d
    pad = (-(N_META + seq)) % ROW_ALIGN
    x0 = pad + N_META
    t_rows = x0 + seq
    assert t_rows % (N_ROW_BLOCKS * BF16_ROWS) == 0 and t_rows % CONV_CHUNK == 0 and d % LANE == 0
    tm = t_rows // N_ROW_BLOCKS
    me = _dev_index(*_mesh_pos())

    def as_rows(w_in_like, w_out_like, w_gate_like, w_up_like, w_down_like):
        return (w_in_like[0].T, w_out_like[0], w_gate_like[0].T, w_up_like[0].T, w_down_like[0])

    w_loc = as_rows(w_in, w_out, w_gate, w_up, w_down)
    rows = [w.shape[0] for w in w_loc]
    assert all(r % ADD_CHUNK == 0 for r in rows)
    P_IN, P_OUT, P_GATE, P_UP, P_DOWN = range(N_BIG)

    sm = jnp.zeros((SM_ROWS, LANE), F32)
    sm = sm.at[0:N_META, :].set(meta_tokens)
    sm = sm.at[16:16 + ka, 0:ca_loc].set(conv_a_w[0])
    sm = sm.at[24:24 + kb, 0:cb_loc].set(conv_b_w[0])
    wl, wfull, sm_all, h0, tgt = _gather_first(w_loc, sm, [(P_IN, 0, rows[P_IN])], x[0], loss_target[0], t_rows, x0)
    wa =jnp.transpose(sm_all[:, 16:16 + ka, 0:ca_loc], (1, 0, 2)).reshape(ka, wa_w)
    wb = jnp.transpose(sm_all[:, 24:24 + kb, 0:cb_loc], (1, 0, 2)).reshape(kb, wa_w)

    later = (P_OUT, P_GATE, P_UP, P_DOWN)
    sems, wl, started, _ = _gather_start(wl, wfull, later, rows, START_BARRIER_IDS[0])
    for p, arr in zip(later, started):
        wfull[p] = arr

    def arrived(p, after, name):
        nonlocal wl
        wl, wfull[p] = _gather_wait(wl, wfull[p], sems[later.index(p)], after, rows[p], name)
        return _forward_comm(wfull[p], rows[p])

    (xn1, hin), _ = _in_proj(h0, pre_mix_norm, wfull[P_IN], tm, None)
    (ya, z), (wfull[P_OUT],) = _mix_conv_fwd(hin, wa, wb, conv_b_bias, wa_w, arrived(P_OUT, hin, "gather_wait_out"))
    (y, mix, h1, xn2), (wfull[P_GATE],) = _out_proj(ya, z, ln_b_gain, ln_b_bias, wfull[P_OUT], h0, post_mix_norm, pre_ffn_norm, tm,
                                                    arrived(P_GATE, z, "gather_wait_gate"))
    arrived(P_UP, xn2, "gather_wait_up")
    wfull[P_UP] = _forward_now(wfull[P_UP], rows[P_UP], "forward_up")
    (ga, gu, s), _ = _gate_up(xn2, wfull[P_GATE], wfull[P_UP], tm, None)
    arrived(P_DOWN, s, "gather_wait_down")
    wfull[P_DOWN] = _forward_now(wfull[P_DOWN], rows[P_DOWN], "forward_down")
    dh2, dff, dg4, lossv = _down_loss(s, wfull[P_DOWN], h1, tgt, post_ffn_norm, tm, x0)

    gwd = _wgrad(s, dff, "wgrad_down")
    (da, du), (pair_d,) = _bwd_down(dff, wfull[P_DOWN], ga, gu, tm, _pair_comm(gwd, rows[P_DOWN]))
    (flight_d,), token = _chip_start([_pair_sum(gwd, pair_d, rows[P_DOWN], "pair_sum_down")], "chip_start_down", START_BARRIER_IDS[1])
    gwg = _wgrad(da, xn2, "wgrad_gate", [token])
    gwu = _wgrad(du, xn2, "wgrad_up")
    (dh1, dg3), (pair_g, pair_u) = _bwd_ffn_in(da, du, wfull[P_GATE], wfull[P_UP], h1, dh2, pre_ffn_norm, tm,
                                               _merge_comms([_pair_comm(gwg, rows[P_GATE]), _pair_comm(gwu, rows[P_UP])]))
    (flight_g, flight_u), token = _chip_start([_pair_sum(gwg, pair_g, rows[P_GATE], "pair_sum_gate"),
                                               _pair_sum(gwu, pair_u, rows[P_UP], "pair_sum_up")], "chip_start_gate_up",
                                              START_BARRIER_IDS[2])
    dmix, dya, dz, dg2, dlg, dlb, dbb = _bwd_out_proj(dh1, mix, wfull[P_OUT], post_mix_norm, z, ln_b_gain, ln_b_bias, tm, [token])
    gwo = _wgrad(y, dmix, "wgrad_out")
    (dh5, dwa, dwb), (pair_o,) = _mix_conv_bwd(hin, dya, dz, wa, wb, wa_w, _pair_comm(gwo, rows[P_OUT]))
    (flight_o,), token = _chip_start([_pair_sum(gwo, pair_o, rows[P_OUT], "pair_sum_out")], "chip_start_out", START_BARRIER_IDS[3])
    gwi = _wgrad(dh5, xn1, "wgrad_in", [token])
    (dh0, dg1), (pair_i,) = _bwd_in_proj(dh5, wfull[P_IN], h0, dh1, pre_mix_norm, tm, _pair_comm(gwi, rows[P_IN]))
    grad_x = dh0[x0:][None]
    dmeta = dh0[x0 - N_META:x0]
    small_sums = _pair_small((dmeta, dg1, dg2, dg3, dg4, dbb, dlg, dlb, lossv, dwa, dwb), d)
    (flight_s,), token = _chip_start([small_sums], "chip_start_small", START_BARRIER_IDS[4])
    (flight_i,), token = _chip_start([_pair_sum(gwi, pair_i, rows[P_IN], "pair_sum_in", [token])], "chip_start_in",
                                     START_BARRIER_IDS[5])

    def landed(flight, after, tag):
        sems_p, sums, land = flight
        return _chip_wait(sums, land, sems_p, after, "chip_wait_" + tag)

    m_loc = as_rows(m_w_in, m_w_out, m_w_gate, m_w_up, m_w_down)
    v_loc = as_rows(v_w_in, v_w_out, v_w_gate, v_w_up, v_w_down)
    full_grads = {P_IN: gwi, P_OUT: gwo, P_GATE: gwg, P_UP: gwu, P_DOWN: gwd}
    pairs = {P_IN: pair_i, P_OUT: pair_o, P_GATE: pair_g, P_UP: pair_u, P_DOWN: pair_d}
    flights = {P_IN: flight_i, P_OUT: flight_o, P_GATE: flight_g, P_UP: flight_u, P_DOWN: flight_d}
    names = {P_IN: "w_in", P_OUT: "w_out", P_GATE: "w_gate", P_UP: "w_up", P_DOWN: "w_down"}
    bigs = {}

    def adam_big(p, after):
        part = landed(flights[p], after, names[p])
        res = _adam_big(full_grads[p], pairs[p], part, w_loc[p], m_loc[p], v_loc[p], "adam_" + names[p])
        bigs[names[p]] = [(o.T if p in (P_IN, P_GATE, P_UP) else o)[None] for o in res]
        return res[1]

    for p in (P_DOWN, P_GATE, P_UP, P_OUT):
        token = adam_big(p, token)
    ptot = _total_small(flight_s[1], landed(flight_s, token, "small"))
    half = d // 2
    loss = (0.5 / d) * jnp.sum(ptot[21, half:])
    g_meta = lax.dynamic_slice(ptot, (0, me * (d // N_DEV)), (N_META, d // N_DEV))
    g_small = [g_meta, ptot[16:17], lax.dynamic_slice(ptot, (24, me * ca_loc), (ka, ca_loc))[None],
               lax.dynamic_slice(ptot, (32, me * cb_loc), (kb, cb_loc))[None],
               ptot[20:21, :half], ptot[20:21, half:], ptot[21:22, :half], ptot[17:18], ptot[18:19], ptot[19:20]]
    w_small = [meta_tokens, pre_mix_norm, conv_a_w, conv_b_w, conv_b_bias, ln_b_gain, ln_b_bias, post_mix_norm,
               pre_ffn_norm, post_ffn_norm]
    m_small = [m_meta_tokens, m_pre_mix_norm, m_conv_a_w, m_conv_b_w, m_conv_b_bias, m_ln_b_gain, m_ln_b_bias,
               m_post_mix_norm, m_pre_ffn_norm, m_post_ffn_norm]
    v_small = [v_meta_tokens, v_pre_mix_norm, v_conv_a_w, v_conv_b_w, v_conv_b_bias, v_ln_b_gain, v_ln_b_bias,
               v_post_mix_norm, v_pre_ffn_norm, v_post_ffn_norm]
    small = _adam_small(g_small, w_small, m_small, v_small)
    n_small = len(w_small)
    d_small, nm_small, nv_small = small[:n_small], small[n_small:2 * n_small], small[2 * n_small:]

    adam_big(P_IN, small[0])

    def ordered(pick_small, pick_big):
        sm_it = iter(range(n_small))
        out = []
        for name in ("s", "s", "w_in", "s", "s", "s", "s", "s", "w_out", "s", "s", "w_gate", "w_up", "w_down", "s"):
            out.append(pick_small(next(sm_it)) if name == "s" else pick_big(name))
        return out

    grads = ordered(lambda i: g_small[i], lambda n: bigs[n][0])
    deltas = ordered(lambda i: d_small[i], lambda n: bigs[n][1])
    new_m = ordered(lambda i: nm_small[i], lambda n: bigs[n][2])
    new_v = ordered(lambda i: nv_small[i], lambda n: bigs[n][3])
    return (loss, grad_x, *grads, *deltas, *new_m, *new_v)
```

```python
import jax
import jax.numpy as jnp
from jax import lax
from jax.experimental import pallas as pl
from jax.experimental.pallas import tpu as pltpu

F32 = jnp.float32
BF16 = jnp.bfloat16
MESH = pl.DeviceIdType.MESH

N_META = 16
N_DEV = 8
RMS_EPS = 1e-6
LN_EPS = 1e-5
ADAM_LR = 0.001
ADAM_B1 = 0.9
ADAM_B2 = 0.999
ADAM_EPS = 1e-08
ADAM_WD = 0.01
ADAM_STEP = 10

LANE = 128
SUBLANE = 8
BF16_ROWS = 16
ROW_ALIGN = 128
N_ROW_BLOCKS = 4
CONV_HALO = 32
CONV_CHUNK = 64
WGRAD_ROWS = 32
N_CHUNK = 512
WGRAD_TILE_MAX = 1408
ADD_CHUNK = 32
ADAM_COL_BLOCKS = 2
COPY_PIECES = 4
V7X_VMEM_BYTES = 64 * 1024 * 1024
VMEM_LIMIT = V7X_VMEM_BYTES - 6 * 1024 * 1024
SMALL_ROWS = 64
SM_ROWS = 56
N_BIG = 5

ANY = pl.BlockSpec(memory_space=pl.ANY)
VMEM = pl.BlockSpec(memory_space=pltpu.VMEM)


def _cparams(n_grid_axes=0):
    sem = ("arbitrary",) * n_grid_axes if n_grid_axes else None
    return pltpu.CompilerParams(dimension_semantics=sem, vmem_limit_bytes=VMEM_LIMIT)


def _mesh_pos():
    return lax.axis_index("x"), lax.axis_index("y"), lax.axis_index("c")


def _dev_index(px, py, pc):
    return 4 * px + 2 * py + pc


def _other_chips(x, y):
    return [(1 - x, y), (x, 1 - y), (1 - x, 1 - y)]


def _full(shape):
    return pl.BlockSpec(shape, lambda *_: (0,) * len(shape))


def _resident(shape):
    return pl.BlockSpec(shape, lambda *_: (0,) * len(shape), pipeline_mode=pl.Buffered(1))


def _dot_nt(a, w):
    return lax.dot_general(a, w, (((1,), (1,)), ((), ())), preferred_element_type=F32)


def _dot_nn(a, w):
    return jnp.dot(a, w, preferred_element_type=F32)


def _chunks(n, c):
    out, o = [], 0
    while o < n:
        out.append((o, min(c, n - o)))
        o += c
    return out


def _rstd(h):
    return lax.rsqrt(jnp.mean(h * h, axis=-1, keepdims=True) + RMS_EPS)


def _rms_bwd(dyh, yh, r):
    return r * (dyh - yh * jnp.mean(dyh * yh, axis=-1, keepdims=True))


def _silu_grad(a, sig):
    return sig * (1.0 + a * (1.0 - sig))


def _acc_rows(ref, val, first):
    s = jnp.sum(val, axis=0, keepdims=True)

    @pl.when(first)
    def _():
        ref[...] = s

    @pl.when(jnp.logical_not(first))
    def _():
        ref[...] += s


def _row_loop(t_rows, chunk, fn, carry=None):
    def step(i, c):
        return fn(pl.multiple_of(i * chunk, chunk), c)

    return lax.fori_loop(0, t_rows // chunk, step, carry)


def _remote(src, dst, send_sem, recv_sem, to):
    return pltpu.make_async_remote_copy(src_ref=src, dst_ref=dst, send_sem=send_sem, recv_sem=recv_sem,
                                        device_id=to, device_id_type=MESH)


class _Comm:
    def __init__(self, inputs, out_shapes, aliases, scratch, start, finish):
        self.inputs, self.out_shapes, self.aliases, self.scratch = list(inputs), list(out_shapes), dict(aliases), list(scratch)
        self.start, self.finish = start, finish


def _merge_comms(comms):
    inputs, out_shapes, aliases, scratch, spans = [], [], {}, [], []
    for cm in comms:
        spans.append((len(inputs), len(out_shapes), len(scratch), cm))
        aliases.update({len(inputs) + k: len(out_shapes) + v for k, v in cm.aliases.items()})
        inputs += cm.inputs
        out_shapes += cm.out_shapes
        scratch += cm.scratch

    def run(which):
        def fn(ins, outs, scr):
            for i0, o0, s0, cm in spans:
                getattr(cm, which)(ins[i0:i0 + len(cm.inputs)], outs[o0:o0 + len(cm.out_shapes)], scr[s0:s0 + len(cm.scratch)])
        return fn

    return _Comm(inputs, out_shapes, aliases, scratch, run("start"), run("finish"))


def _host_call(body, *, grid, in_specs, out_specs, out_shape, args, name, scratch_shapes=(), comm=None, after=()):
    talks = comm is not None
    if comm is None:
        comm = _Comm([], [], {}, [], lambda *_: None, lambda *_: None)
    n_in, n_out, n_scr = len(args), len(out_shape), len(scratch_shapes)
    c_in, c_out = len(comm.inputs), len(comm.out_shapes)
    n_after = len(after)

    def open_comm(c_ins, c_outs, c_scr):
        if talks:
            _pair_handshake()
        comm.start(c_ins, c_outs, c_scr)

    def hosted(*refs):
        ins, c_ins = refs[:n_in], refs[n_in:n_in + c_in]
        o0 = n_in + c_in + n_after
        outs, c_outs = refs[o0:o0 + n_out], refs[o0 + n_out:o0 + n_out + c_out]
        s0 = o0 + n_out + c_out
        scr, c_scr = refs[s0:s0 + n_scr], refs[s0 + n_scr:]
        if not grid:
            open_comm(c_ins, c_outs, c_scr)
            body(*ins, *outs, *scr)
            comm.finish(c_ins, c_outs, c_scr)
            return
        first = last = None
        for a, n in enumerate(grid):
            f, l = pl.program_id(a) == 0, pl.program_id(a) == n - 1
            first = f if first is None else jnp.logical_and(first, f)
            last = l if last is None else jnp.logical_and(last, l)

        @pl.when(first)
        def _():
            open_comm(c_ins, c_outs, c_scr)

        body(*ins, *outs, *scr)

        @pl.when(last)
        def _():
            comm.finish(c_ins, c_outs, c_scr)

    sem = ("arbitrary",) * len(grid) if grid else None
    params = pltpu.CompilerParams(dimension_semantics=sem, vmem_limit_bytes=VMEM_LIMIT,
                                  collective_id=PAIR_BARRIER_ID if talks else None)
    res = pl.pallas_call(
        hosted, grid=grid, in_specs=list(in_specs) + [ANY] * (c_in + n_after), out_specs=list(out_specs) + [ANY] * c_out,
        out_shape=list(out_shape) + comm.out_shapes, scratch_shapes=list(scratch_shapes) + comm.scratch,
        input_output_aliases={n_in + k: n_out + v for k, v in comm.aliases.items()},
        name=name, compiler_params=params)(*args, *comm.inputs, *after)
    return list(res[:n_out]), list(res[n_out:])


PAIR_BARRIER_ID = 0
START_BARRIER_IDS = (1, 2, 3, 4, 5, 6)


def _chips_handshake():
    x, y, c = _mesh_pos()
    barrier = pltpu.get_barrier_semaphore()
    for chip in _other_chips(x, y):
        pl.semaphore_signal(barrier, inc=1, device_id=(*chip, c), device_id_type=MESH)
    pl.semaphore_wait(barrier, 3)


def _pair_handshake():
    x, y, c = _mesh_pos()
    barrier = pltpu.get_barrier_semaphore()
    pl.semaphore_signal(barrier, inc=1, device_id=(x, y, 1 - c), device_id_type=MESH)
    pl.semaphore_wait(barrier, 1)


GATHER_SEMS = 10


class _Gather:
    def __init__(self, jobs, rows, lo, src_ref, dests, send_sems, recv_sems):
        x, y, c = _mesh_pos()
        me, sib = (x, y, c), (x, y, 1 - c)
        nx, ny, dg = (1 - x, y, c), (x, 1 - y, c), (1 - x, 1 - y, c)
        self.relayed, self.direct, self.relay, self.to_sib, self.sib_fwd = [], [], [], [], []
        for n, (p, r0, nr) in enumerate(jobs):
            assert nr % (2 * BF16_ROWS) == 0
            half = nr // 2

            def rows_of(dev, h, p=p, r0=r0, nr=nr, half=half):
                off, cnt = (r0, nr) if h is None else (r0 + h * half, half)
                return dests[p].at[pl.ds(pl.multiple_of(_dev_index(*dev) * rows[p] + off, BF16_ROWS), cnt), :]

            def mine(h, p=p, r0=r0, nr=nr, half=half):
                off, cnt = (r0, nr) if h is None else (r0 + h * half, half)
                return src_ref.at[pl.ds(lo[p] + off, cnt), :]

            sem = lambda k, n=n: (send_sems.at[GATHER_SEMS * n + k], recv_sems.at[GATHER_SEMS * n + k])
            self.relayed.append([_remote(mine(0), rows_of(me, 0), *sem(0), nx), _remote(mine(1), rows_of(me, 1), *sem(3), ny)])
            self.direct.append([_remote(mine(1), rows_of(me, 1), *sem(1), nx), _remote(mine(0), rows_of(me, 0), *sem(2), ny)])
            self.relay.append([_remote(rows_of(nx, 0), rows_of(nx, 0), *sem(4), ny), _remote(rows_of(ny, 1), rows_of(ny, 1), *sem(5), nx)])
            self.to_sib.append(_remote(mine(None), rows_of(me, None), *sem(6), sib))
            self.sib_fwd.append([_remote(rows_of(dev, None), rows_of(dev, None), *sem(7 + i), sib) for i, dev in enumerate((nx, ny, dg))])

    def start(self):
        for group in (self.relayed, self.direct):
            for cps in group:
                for cp in cps:
                    cp.start()
        for cp in self.to_sib:
            cp.start()

    def mid(self):
        for first, relay in zip(self.relayed, self.relay):
            for arrived, onward in zip(first, relay):
                arrived.wait_recv()
                onward.start()

    def finish(self):
        for direct, relay, fwd in zip(self.direct, self.relay, self.sib_fwd):
            for k in range(2):
                direct[k].wait_recv()
                fwd[k].start()
            for cp in relay:
                cp.wait_recv()
            fwd[2].start()
        for n in range(len(self.to_sib)):
            self.to_sib[n].wait_recv()
            for cp in self.sib_fwd[n]:
                cp.wait_recv()
            for cp in self.relayed[n] + self.direct[n] + self.relay[n] + [self.to_sib[n]] + self.sib_fwd[n]:
                cp.wait_send()


HBM = pl.BlockSpec(memory_space=pltpu.HBM)
SEM = pl.BlockSpec(memory_space=pltpu.SEMAPHORE)
FLOWS = pltpu.SideEffectType.DATAFLOW_SIDE_EFFECTING


def _in_hbm(a):
    return pltpu.with_memory_space_constraint(a, pltpu.HBM)


def _gather_start(wl, dests, ps, rows, barrier_id):
    lo = [sum(rows[:p]) for p in range(N_BIG)]
    n = len(ps)

    def body(*refs):
        wl_ref, dest_refs = refs[0], refs[1:1 + n]
        sends, recvs = refs[1 + n:1 + 2 * n], refs[1 + 2 * n:1 + 3 * n]
        token = refs[-1]
        _chips_handshake()
        x, y, c = _mesh_pos()
        jme = _dev_index(x, y, c)
        for i, p in enumerate(ps):
            mine = dest_refs[i].at[pl.ds(pl.multiple_of(jme * rows[p], BF16_ROWS), rows[p]), :]
            for chip in _other_chips(x, y):
                _remote(wl_ref.at[pl.ds(lo[p], rows[p]), :], mine, sends[i], recvs[i], (*chip, c)).start()
        token[...] = jnp.zeros_like(token)

    thru = [pltpu.HBM(wl.shape, wl.dtype)] + [pltpu.HBM(dests[p].shape, BF16) for p in ps]
    res = pl.pallas_call(
        body, name="gather_start",
        out_shape=tuple([pltpu.SemaphoreType.DMA(())] * (2 * n) + thru + [jax.ShapeDtypeStruct((SUBLANE, LANE), F32)]),
        in_specs=[HBM] * (1 + n), out_specs=tuple([SEM] * (2 * n) + [HBM] * (1 + n) + [VMEM]),
        input_output_aliases={i: 2 * n + i for i in range(1 + n)},
        compiler_params=pltpu.CompilerParams(has_side_effects=FLOWS, collective_id=barrier_id))(
            _in_hbm(wl), *[_in_hbm(dests[p]) for p in ps])
    sems = [(res[i], res[n + i]) for i in range(n)]
    return sems, res[2 * n], list(res[2 * n + 1:3 * n + 1]), res[-1]


def _gather_wait(wl, dest, sems, after, r, name):
    def body(wl_ref, dest_ref, send_sem, recv_sem, after_ref, wl_out, dest_out):
        x, y, c = _mesh_pos()
        three = dest_ref.at[pl.ds(0, 3 * r), :]
        cp = _remote(three, three, send_sem, recv_sem, (x, y, 1 - c))
        cp.wait_send()
        cp.wait_recv()

    res = pl.pallas_call(
        body, name=name, out_shape=(pltpu.HBM(wl.shape, wl.dtype), pltpu.HBM(dest.shape, dest.dtype)),
        in_specs=[HBM, HBM, SEM, SEM, ANY], out_specs=(HBM, HBM), input_output_aliases={0: 0, 1: 1},
        compiler_params=pltpu.CompilerParams(has_side_effects=FLOWS))(wl, dest, sems[0], sems[1], after)
    return res[0], res[1]


def _forward_comm(dest, r):
    def descs(ins, outs, scr):
        x, y, c = _mesh_pos()
        cps = []
        for k, chip in enumerate(_other_chips(x, y)):
            blk = outs[0].at[pl.ds(pl.multiple_of(_dev_index(*chip, c) * r, BF16_ROWS), r), :]
            cps.append(_remote(blk, blk, scr[0].at[k], scr[1].at[k], (x, y, 1 - c)))
        return cps

    def start(ins, outs, scr):
        for cp in descs(ins, outs, scr):
            cp.start()

    def finish(ins, outs, scr):
        cps = descs(ins, outs, scr)
        for cp in cps:
            cp.wait_recv()
        for cp in cps:
            cp.wait_send()

    return _Comm([dest], [jax.ShapeDtypeStruct(dest.shape, dest.dtype)], {0: 0},
                 [pltpu.SemaphoreType.DMA((3,)), pltpu.SemaphoreType.DMA((3,))], start, finish)


def _forward_now(dest, r, name):
    _, (dest,) = _host_call(lambda: None, grid=(), in_specs=[], out_specs=[], out_shape=[], args=(), name=name,
                            comm=_forward_comm(dest, r))
    return dest


def _pair_comm(g, r):
    d = g.shape[1]

    def descs(ins, outs, scr):
        x, y, c = _mesh_pos()
        chips = [(x, y)] + _other_chips(x, y)
        return [_remote(ins[0].at[pl.ds(pl.multiple_of(_dev_index(*chip, 1 - c) * r, BF16_ROWS), r), :], outs[0].at[k],
                        scr[0].at[k], scr[1].at[k], (x, y, 1 - c)) for k, chip in enumerate(chips)]

    def start(ins, outs, scr):
        for cp in descs(ins, outs, scr):
            cp.start()

    def finish(ins, outs, scr):
        cps = descs(ins, outs, scr)
        for cp in cps:
            cp.wait_recv()
        for cp in cps:
            cp.wait_send()

    comm = _Comm([g], [jax.ShapeDtypeStruct((4, r, d), BF16)], {},
                 [pltpu.SemaphoreType.DMA((4,)), pltpu.SemaphoreType.DMA((4,))], start, finish)
    return comm


def _pair_sum(g, pair, r, name, after=()):
    d = g.shape[1]

    def body(g_ref, p_ref, *rest):
        o_ref, gbuf, pbuf, sems = rest[len(after):]
        x, y, c = _mesh_pos()
        loads = [pltpu.make_async_copy(p_ref.at[pl.ds(1, 3)], pbuf, sems.at[3])]
        for k, chip in enumerate(_other_chips(x, y)):
            j = _dev_index(*chip, c)
            loads.append(pltpu.make_async_copy(g_ref.at[pl.ds(pl.multiple_of(j * r, BF16_ROWS), r), :], gbuf.at[k], sems.at[k]))
        for cp in loads:
            cp.start()
        for cp in loads:
            cp.wait()
        for k in range(3):
            o_ref[k] = (gbuf[k].astype(F32) + pbuf[k].astype(F32)).astype(BF16)

    return pl.pallas_call(
        body, out_shape=jax.ShapeDtypeStruct((3, r, d), BF16), in_specs=[ANY] * (2 + len(after)), out_specs=VMEM,
        scratch_shapes=[pltpu.VMEM((3, r, d), BF16), pltpu.VMEM((3, r, d), BF16), pltpu.SemaphoreType.DMA((4,))],
        name=name, compiler_params=_cparams())(g, pair, *after)


def _chip_start(sums, name, barrier_id):
    n = len(sums)

    def body(*refs):
        srcs, lands = refs[:n], refs[n:2 * n]
        sends, recvs = refs[2 * n:3 * n], refs[3 * n:4 * n]
        _chips_handshake()
        x, y, c = _mesh_pos()
        for i in range(n):
            for k, chip in enumerate(_other_chips(x, y)):
                _remote(srcs[i].at[k], lands[i].at[k], sends[i], recvs[i], (*chip, c)).start()
        refs[-1][...] = jnp.zeros_like(refs[-1])

    zones = [pltpu.HBM(s.shape, s.dtype) for s in sums]
    res = pl.pallas_call(
        body, name=name,
        out_shape=tuple([pltpu.SemaphoreType.DMA(())] * (2 * n) + zones + zones + [jax.ShapeDtypeStruct((SUBLANE, LANE), F32)]),
        in_specs=[HBM] * (2 * n), out_specs=tuple([SEM] * (2 * n) + [HBM] * (2 * n) + [VMEM]),
        input_output_aliases={i: 2 * n + i for i in range(2 * n)},
        compiler_params=pltpu.CompilerParams(has_side_effects=FLOWS, collective_id=barrier_id))(
            *[_in_hbm(s) for s in sums], *[_in_hbm(lax.empty(s.shape, s.dtype)) for s in sums])
    flights = [((res[i], res[n + i]), res[2 * n + i], res[3 * n + i]) for i in range(n)]
    return flights, res[-1]


def _chip_wait(sums, land, sems, after, name):
    def body(sums_ref, land_ref, send_sem, recv_sem, after_ref, sums_out, land_out):
        x, y, c = _mesh_pos()
        cp = _remote(sums_ref, land_ref, send_sem, recv_sem, (x, y, 1 - c))
        cp.wait_send()
        cp.wait_recv()

    res = pl.pallas_call(
        body, name=name, out_shape=(pltpu.HBM(sums.shape, sums.dtype), pltpu.HBM(land.shape, land.dtype)),
        in_specs=[HBM, HBM, SEM, SEM, ANY], out_specs=(HBM, HBM), input_output_aliases={0: 0, 1: 1},
        compiler_params=pltpu.CompilerParams(has_side_effects=FLOWS))(sums, land, sems[0], sems[1], after)
    return res[1]


class _CopyThrough:
    def __init__(self, src_ref, dst_ref, dst_row0, n_rows, buf, sem_in, sem_out):
        rc = n_rows // COPY_PIECES
        piece = lambda ref, o: ref.at[pl.ds(o, rc), :]
        self.loads = [pltpu.make_async_copy(piece(src_ref, k * rc), piece(buf, k * rc), sem_in) for k in range(COPY_PIECES)]
        self.stores = [pltpu.make_async_copy(piece(buf, k * rc), piece(dst_ref, dst_row0 + k * rc), sem_out) for k in range(COPY_PIECES)]
        self.all_in = pltpu.make_async_copy(src_ref, buf, sem_in)
        self.all_out = pltpu.make_async_copy(buf, dst_ref.at[pl.ds(dst_row0, n_rows), :], sem_out)

    def load(self):
        for cp in self.loads:
            cp.start()

    def store(self):
        self.all_in.wait()
        for cp in self.stores:
            cp.start()

    def done(self):
        self.all_out.wait()


def _gather_first(shards, sm, jobs, x2, tgt2, t_rows, x0):
    d = shards[0].shape[1]
    rows = [w.shape[0] for w in shards]
    lo = [sum(rows[:p]) for p in range(N_BIG)]
    n_sems = GATHER_SEMS * len(jobs)
    seq = x2.shape[0]
    assert x0 == ROW_ALIGN and seq % ROW_ALIGN == 0 and d == N_DEV * LANE

    def body(s0, s1, s2, s3, s4, sm_ref, x_ref, tgt_ref, wl_ref, o0, o1, o2, o3, o4, sa_ref, h0_ref, tp_ref,
             wl_v, x_v, tgt_v, heads_v, sa_v, send_sems, recv_sems, ssend, srecv, local_sems, sems_in, sems_out):
        dests = (o0, o1, o2, o3, o4)
        x, y, c = _mesh_pos()
        me = (x, y, c)
        jme = _dev_index(*me)
        padded = [_CopyThrough(x_ref, h0_ref, x0, seq, x_v, sems_in.at[0], sems_out.at[0]),
                  _CopyThrough(tgt_ref, tp_ref, x0, seq, tgt_v, sems_in.at[1], sems_out.at[1])]
        for cp in padded:
            cp.load()
        shard_refs = (s0, s1, s2, s3, s4)
        first = sorted({j[0] for j in jobs})
        for p in first + [p for p in range(N_BIG) if p not in first]:
            wl_v[pl.ds(lo[p], rows[p]), :] = shard_refs[p][...].astype(BF16)
            if p == first[-1]:
                gather = _Gather(jobs, rows, lo, wl_v, dict(enumerate(dests)), send_sems, recv_sems)
                gather.start()
        peers = [(x, y, 1 - c)] + [(*chip, pc) for pc in (c, 1 - c) for chip in _other_chips(x, y)]
        smalls = [_remote(sm_ref, sa_ref.at[jme], ssend.at[k], srecv.at[k], to) for k, to in enumerate(peers)]
        for cp in smalls:
            cp.start()
        mine = [pltpu.make_async_copy(wl_v.at[pl.ds(lo[p], rows[p]), :],
                                      dests[p].at[pl.ds(pl.multiple_of(jme * rows[p], BF16_ROWS), rows[p]), :], local_sems.at[p])
                for p in range(N_BIG)]
        mine.append(pltpu.make_async_copy(wl_v, wl_ref, local_sems.at[N_BIG]))
        mine.append(pltpu.make_async_copy(sm_ref, sa_ref.at[jme], local_sems.at[N_BIG + 1]))
        for cp in mine:
            cp.start()
        later = [p for p in range(N_BIG) if p not in {j[0] for j in jobs}]
        own = [_remote(wl_v.at[pl.ds(lo[p], rows[p]), :], dests[p].at[pl.ds(pl.multiple_of(jme * rows[p], BF16_ROWS), rows[p]), :],
                       ssend.at[7 + i], srecv.at[7 + i], (x, y, 1 - c)) for i, p in enumerate(later)]
        for cp in own:
            cp.start()
        for cp in padded:
            cp.store()
        gather.mid()
        for cp in smalls + own:
            cp.wait_recv()
        mine[-1].wait()
        to_v = pltpu.make_async_copy(sa_ref, sa_v, local_sems.at[N_BIG + 1])
        to_v.start()
        to_v.wait()
        head, zeros = heads_v.at[0], heads_v.at[1]
        head[...] = jnp.zeros_like(head)
        zeros[...] = jnp.zeros_like(zeros)
        for j in range(N_DEV):
            head[pl.ds(x0 - N_META, N_META), pl.ds(j * LANE, LANE)] = sa_v[j, pl.ds(0, N_META), :]
        heads = [pltpu.make_async_copy(head, h0_ref.at[pl.ds(0, x0), :], local_sems.at[N_BIG + 1]),
                 pltpu.make_async_copy(zeros, tp_ref.at[pl.ds(0, x0), :], local_sems.at[N_BIG + 2])]
        for cp in heads:
            cp.start()
        gather.finish()
        for cp in smalls + own:
            cp.wait_send()
        for cp in mine[:-1] + heads:
            cp.wait()
        for cp in padded:
            cp.done()

    out_shape = [jax.ShapeDtypeStruct((sum(rows), d), BF16)]
    out_shape += [jax.ShapeDtypeStruct((N_DEV * r, d), BF16) for r in rows]
    out_shape.append(jax.ShapeDtypeStruct((N_DEV,) + sm.shape, F32))
    out_shape += [jax.ShapeDtypeStruct((t_rows, d), F32)] * 2
    res = pl.pallas_call(
        body, out_shape=out_shape, in_specs=[VMEM] * 6 + [ANY] * 2, out_specs=[ANY] * 9,
        scratch_shapes=[pltpu.VMEM((sum(rows), d), BF16), pltpu.VMEM((seq, d), F32), pltpu.VMEM((seq, d), F32),
                        pltpu.VMEM((2, ROW_ALIGN, d), F32), pltpu.VMEM((N_DEV,) + sm.shape, F32),
                        pltpu.SemaphoreType.DMA((n_sems,)), pltpu.SemaphoreType.DMA((n_sems,)),
                        pltpu.SemaphoreType.DMA((7 + N_BIG,)), pltpu.SemaphoreType.DMA((7 + N_BIG,)),
                        pltpu.SemaphoreType.DMA((N_BIG + 3,)), pltpu.SemaphoreType.DMA((2,)), pltpu.SemaphoreType.DMA((2,))],
        name="gather_first", compiler_params=_cparams())(*shards, sm, x2, tgt2)
    return res[0], list(res[1:1 + N_BIG]), res[1 + N_BIG], res[2 + N_BIG], res[3 + N_BIG]


def _in_proj(h0, g1, win_t, tm, comm):
    t_rows, d = h0.shape
    e = win_t.shape[0]

    def body(h_ref, g_ref, w_ref, xn_ref, hin_ref):
        h = h_ref[...]
        xn = ((h * _rstd(h)) * g_ref[...]).astype(BF16)
        xn_ref[...] = xn
        for o, n in _chunks(e, N_CHUNK):
            hin_ref[:, pl.ds(o, n)] = _dot_nt(xn, w_ref[pl.ds(o, n), :])

    return _host_call(
        body, grid=(t_rows // tm,),
        in_specs=[pl.BlockSpec((tm, d), lambda i: (i, 0)), _full((1, d)), _resident((e, d))],
        out_specs=[pl.BlockSpec((tm, d), lambda i: (i, 0)), pl.BlockSpec((tm, e), lambda i: (i, 0))],
        out_shape=[jax.ShapeDtypeStruct((t_rows, d), BF16), jax.ShapeDtypeStruct((t_rows, e), F32)],
        args=(h0, g1, win_t), name="in_proj", comm=comm)


def _tap_slot(off):
    return off % SUBLANE, (off // SUBLANE) * SUBLANE


def _fill_shifted(sh_ref, base_ref, residues, n_rows):
    for r in residues:
        if r:
            sh_ref[r] = base_ref[pl.ds(r, n_rows), :]


def _shifted_rows(pair, r, start, n):
    base_ref, sh_ref = pair
    return base_ref[pl.ds(start, n), :] if r == 0 else sh_ref[r, pl.ds(start, n), :]


def _mix_conv_fwd(hin, wa, wb, bb, wa_w, comm):
    t_rows = hin.shape[0]
    nt = wa_w // LANE
    ka, kb = wa.shape[0], wb.shape[0]
    nr = CONV_HALO + t_rows

    def body(bg_ref, cg_ref, ha_ref, val_ref, gt_ref, wa_ref, wb_ref, bb_ref, ya_ref, z_ref, base, sh):
        base[pl.ds(0, CONV_HALO), :] = jnp.zeros((CONV_HALO, LANE), F32)
        base[pl.ds(nr, SUBLANE), :] = jnp.zeros((SUBLANE, LANE), F32)

        def conv(w_ref, k_taps, b, n):
            acc = None
            for k in range(k_taps):
                r, q = _tap_slot(CONV_HALO - (k_taps - 1) + k)
                term = w_ref[pl.ds(k, 1), :] * _shifted_rows((base, sh), r, b + q, n)
                acc = term if acc is None else acc + term
            return acc

        def fill_a(b, c):
            base[pl.ds(CONV_HALO + b, CONV_CHUNK), :] = cg_ref[pl.ds(b, CONV_CHUNK), :] * ha_ref[pl.ds(b, CONV_CHUNK), :]
            return c

        _row_loop(t_rows, CONV_CHUNK, fill_a)
        _fill_shifted(sh, base, sorted({_tap_slot(CONV_HALO - (ka - 1) + k)[0] for k in range(ka)}), nr)

        def out_a(b, c):
            ya_ref[pl.ds(b, CONV_CHUNK), :] = (bg_ref[pl.ds(b, CONV_CHUNK), :] * conv(wa_ref, ka, b, CONV_CHUNK)).astype(BF16)
            return c

        _row_loop(t_rows, CONV_CHUNK, out_a)

        def fill_b(b, c):
            base[pl.ds(CONV_HALO + b, CONV_CHUNK), :] = (val_ref[pl.ds(b, CONV_CHUNK), :]
                                                          * jax.nn.sigmoid(gt_ref[pl.ds(b, CONV_CHUNK), :]))
            return c

        _row_loop(t_rows, CONV_CHUNK, fill_b)
        _fill_shifted(sh, base, range(SUBLANE), nr)

        def out_b(b, c):
            z_ref[pl.ds(b, CONV_CHUNK), :] = conv(wb_ref, kb, b, CONV_CHUNK) + bb_ref[...]
            return c

        _row_loop(t_rows, CONV_CHUNK, out_b)

    def col(g):
        return pl.BlockSpec((t_rows, LANE), lambda i, g=g: (0, g * nt + i))

    tile = lambda rows: pl.BlockSpec((rows, LANE), lambda i: (0, i))
    return _host_call(
        body, grid=(nt,),
        in_specs=[col(0), col(1), col(2), col(3), col(4), tile(ka), tile(kb), tile(1)],
        out_specs=[tile(t_rows), tile(t_rows)],
        out_shape=[jax.ShapeDtypeStruct((t_rows, wa_w), BF16), jax.ShapeDtypeStruct((t_rows, wa_w), F32)],
        scratch_shapes=[pltpu.VMEM((nr + SUBLANE, LANE), F32), pltpu.VMEM((SUBLANE, nr, LANE), F32)],
        args=(hin, hin, hin, hin, hin, wa, wb, bb), name="mix_conv_fwd", comm=comm)


def _ln_parts(z, lg, lb):
    mu = jnp.mean(z, axis=-1, keepdims=True)
    zc = z - mu
    rstd = lax.rsqrt(jnp.mean(zc * zc, axis=-1, keepdims=True) + LN_EPS)
    zh = zc * rstd
    return zh, rstd, zh * lg + lb


def _out_proj(ya, z, lg, lb, w_out, h0, g2, g3, tm, comm):
    t_rows, d = h0.shape
    w = z.shape[1]

    def body(ya_ref, z_ref, lg_ref, lb_ref, w_ref, h0_ref, g2_ref, g3_ref, y_ref, mix_ref, h1_ref, xn2_ref):
        _, _, ln = _ln_parts(z_ref[...], lg_ref[...], lb_ref[...])
        y_ref[:, pl.ds(0, w)] = ya_ref[...]
        y_ref[:, pl.ds(w, w)] = (ln * jax.nn.sigmoid(ln)).astype(BF16)
        mix = _dot_nn(y_ref[...], w_ref[...])
        mix_ref[...] = mix
        h1 = h0_ref[...] + (mix * _rstd(mix)) * g2_ref[...]
        h1_ref[...] = h1
        xn2_ref[...] = ((h1 * _rstd(h1)) * g3_ref[...]).astype(BF16)

    blk = pl.BlockSpec((tm, d), lambda i: (i, 0))
    half = pl.BlockSpec((tm, w), lambda i: (i, 0))
    return _host_call(
        body, grid=(t_rows // tm,),
        in_specs=[half, half, _full((1, w)), _full((1, w)), _resident(w_out.shape), blk, _full((1, d)), _full((1, d))],
        out_specs=[blk, blk, blk, blk],
        out_shape=[jax.ShapeDtypeStruct((t_rows, d), BF16), jax.ShapeDtypeStruct((t_rows, d), F32),
                   jax.ShapeDtypeStruct((t_rows, d), F32), jax.ShapeDtypeStruct((t_rows, d), BF16)],
        args=(ya, z, lg, lb, w_out, h0, g2, g3), name="out_proj", comm=comm)


def _gate_up(xn2, wg_t, wu_t, tm, comm):
    t_rows, d = xn2.shape
    f = wg_t.shape[0]

    def body(x_ref, wg_ref, wu_ref, ga_ref, gu_ref, s_ref):
        xn = x_ref[...]
        for o, n in _chunks(f, N_CHUNK):
            a = _dot_nt(xn, wg_ref[pl.ds(o, n), :])
            u = _dot_nt(xn, wu_ref[pl.ds(o, n), :])
            sig = jax.nn.sigmoid(a)
            silu = a * sig
            s = silu * u
            gu_ref[:, pl.ds(o, n)] = silu.astype(BF16)
            ga_ref[:, pl.ds(o, n)] = ((u - s) * sig + s).astype(BF16)
            s_ref[:, pl.ds(o, n)] = s.astype(BF16)

    blk = pl.BlockSpec((tm, f), lambda i: (i, 0))
    return _host_call(
        body, grid=(t_rows // tm,),
        in_specs=[pl.BlockSpec((tm, d), lambda i: (i, 0)), _resident((f, d)), _resident((f, d))],
        out_specs=[blk, blk, blk], out_shape=[jax.ShapeDtypeStruct((t_rows, f), BF16)] * 3,
        args=(xn2, wg_t, wu_t), name="gate_up", comm=comm)


def _down_loss(s, wd, h1, tgt, g4, tm, x0):
    t_rows, d = h1.shape
    f = wd.shape[0]

    def body(s_ref, w_ref, h1_ref, tgt_ref, g4_ref, dh2_ref, dff_ref, dg4_ref, loss_ref):
        i = pl.program_id(0)
        ff = _dot_nn(s_ref[...], w_ref[...])
        r4 = _rstd(ff)
        fh = ff * r4
        g4 = g4_ref[...]
        h2 = h1_ref[...] + fh * g4
        row = i * tm + lax.broadcasted_iota(jnp.int32, (tm, 1), 0)
        diff = jnp.where(row >= x0, h2 - tgt_ref[...], 0.0)
        dh2 = diff / d
        dh2_ref[...] = dh2
        dff_ref[...] = _rms_bwd(dh2 * g4, fh, r4).astype(BF16)
        _acc_rows(dg4_ref, dh2 * fh, i == 0)
        _acc_rows(loss_ref, diff * diff, i == 0)

    blk = pl.BlockSpec((tm, d), lambda i: (i, 0))
    res, _ = _host_call(
        body, grid=(t_rows // tm,),
        in_specs=[pl.BlockSpec((tm, f), lambda i: (i, 0)), _resident((f, d)), blk, blk, _full((1, d))],
        out_specs=[blk, blk, _full((1, d)), _full((1, d))],
        out_shape=[jax.ShapeDtypeStruct((t_rows, d), F32), jax.ShapeDtypeStruct((t_rows, d), BF16),
                   jax.ShapeDtypeStruct((1, d), F32), jax.ShapeDtypeStruct((1, d), F32)],
        args=(s, wd, h1, tgt, g4), name="down_loss")
    return res


def _bwd_down(dff, wd, ga, gu, tm, comm):
    t_rows, d = dff.shape
    f = wd.shape[0]

    def body(dff_ref, w_ref, ga_ref, gu_ref, da_ref, du_ref):
        dff_v = dff_ref[...]
        for o, n in _chunks(f, N_CHUNK):
            ds = _dot_nt(dff_v, w_ref[pl.ds(o, n), :]).astype(BF16)
            da_ref[:, pl.ds(o, n)] = ds * ga_ref[:, pl.ds(o, n)]
            du_ref[:, pl.ds(o, n)] = ds * gu_ref[:, pl.ds(o, n)]

    blk = pl.BlockSpec((tm, f), lambda i: (i, 0))
    return _host_call(
        body, grid=(t_rows // tm,),
        in_specs=[pl.BlockSpec((tm, d), lambda i: (i, 0)), _resident((f, d)), blk, blk],
        out_specs=[blk, blk], out_shape=[jax.ShapeDtypeStruct((t_rows, f), BF16)] * 2,
        args=(dff, wd, ga, gu), name="bwd_down", comm=comm)


def _wgrad(a, b, name, after=()):
    d = b.shape[1]
    t_rows = b.shape[0]
    stacked = a.ndim == 3
    n = a.shape[-1]
    groups = a.shape[0] if stacked else 1
    steps = 1 if stacked else 2
    tile = max(t for t in range(LANE, min(n // steps, WGRAD_TILE_MAX) + 1, LANE) if n % t == 0)
    tiles = n // tile

    def body(a_ref, b_ref, o_ref):
        o_ref[...] = lax.dot_general(a_ref[...], b_ref[...], (((0,), (0,)), ((), ())),
                                     preferred_element_type=F32).astype(BF16)

    if stacked:
        a_spec = pl.BlockSpec((None, t_rows, tile), lambda g, i: (g, 0, i))
    else:
        a_spec = pl.BlockSpec((t_rows, tile), lambda g, i: (0, i))
    res, _ = _host_call(
        body, grid=(groups, tiles), in_specs=[a_spec, _resident((t_rows, d))],
        out_specs=[pl.BlockSpec((tile, d), lambda g, i: (g * tiles + i, 0))],
        out_shape=[jax.ShapeDtypeStruct((groups * n, d), BF16)], args=(a, b), name=name, after=after)
    return res[0]


def _bwd_ffn_in(da, du, wg_t, wu_t, h1, dh2, g3, tm, comm):
    t_rows, d = h1.shape
    f = wg_t.shape[0]

    def body(da_ref, du_ref, wg_ref, wu_ref, h1_ref, dh2_ref, g3_ref, dh1_ref, dg3_ref):
        dxn2 = _dot_nn(da_ref[...], wg_ref[...]) + _dot_nn(du_ref[...], wu_ref[...])
        h1 = h1_ref[...]
        r3 = _rstd(h1)
        h1h = h1 * r3
        _acc_rows(dg3_ref, dxn2 * h1h, pl.program_id(0) == 0)
        dh1_ref[...] = dh2_ref[...] + _rms_bwd(dxn2 * g3_ref[...], h1h, r3)

    blk = pl.BlockSpec((tm, d), lambda i: (i, 0))
    blkf = pl.BlockSpec((tm, f), lambda i: (i, 0))
    return _host_call(
        body, grid=(t_rows // tm,),
        in_specs=[blkf, blkf, _resident((f, d)), _resident((f, d)), blk, blk, _full((1, d))],
        out_specs=[blk, _full((1, d))],
        out_shape=[jax.ShapeDtypeStruct((t_rows, d), F32), jax.ShapeDtypeStruct((1, d), F32)],
        args=(da, du, wg_t, wu_t, h1, dh2, g3), name="bwd_ffn_in", comm=comm)


def _bwd_out_proj(dh1, mix, w_out, g2, z, lg, lb, tm, after):
    t_rows, d = dh1.shape
    w = z.shape[1]

    def body(dh1_ref, mix_ref, w_ref, g2_ref, z_ref, lg_ref, lb_ref, dmix_ref, dya_ref, dz_ref, dg2_ref, dlg_ref, dlb_ref, dbb_ref):
        first = pl.program_id(0) == 0
        mix = mix_ref[...]
        r2 = _rstd(mix)
        mh = mix * r2
        dh1 = dh1_ref[...]
        _acc_rows(dg2_ref, dh1 * mh, first)
        dmix = _rms_bwd(dh1 * g2_ref[...], mh, r2).astype(BF16)
        dmix_ref[...] = dmix
        dy = _dot_nt(dmix, w_ref[...])
        dya_ref[...] = dy[:, :w]
        lg = lg_ref[...]
        zh, rstd, ln = _ln_parts(z_ref[...], lg, lb_ref[...])
        dln = dy[:, w:] * _silu_grad(ln, jax.nn.sigmoid(ln))
        _acc_rows(dlg_ref, dln * zh, first)
        _acc_rows(dlb_ref, dln, first)
        dzh = dln * lg
        dz = rstd * (dzh - jnp.mean(dzh, axis=-1, keepdims=True) - zh * jnp.mean(dzh * zh, axis=-1, keepdims=True))
        dz_ref[...] = dz
        _acc_rows(dbb_ref, dz, first)

    blk = pl.BlockSpec((tm, d), lambda i: (i, 0))
    half = pl.BlockSpec((tm, w), lambda i: (i, 0))
    vec = _full((1, w))
    res, _ = _host_call(
        body, grid=(t_rows // tm,), in_specs=[blk, blk, _resident(w_out.shape), _full((1, d)), half, vec, vec],
        out_specs=[blk, half, half, _full((1, d)), vec, vec, vec],
        out_shape=[jax.ShapeDtypeStruct((t_rows, d), BF16), jax.ShapeDtypeStruct((t_rows, w), F32),
                   jax.ShapeDtypeStruct((t_rows, w), F32), jax.ShapeDtypeStruct((1, d), F32)]
        + [jax.ShapeDtypeStruct((1, w), F32)] * 3,
        args=(dh1, mix, w_out, g2, z, lg, lb), name="bwd_out_proj", after=after)
    return res


def _mix_conv_bwd(hin, dy, dz, wa, wb, wa_w, comm):
    t_rows = hin.shape[0]
    nt = wa_w // LANE
    ka, kb = wa.shape[0], wb.shape[0]
    nr = CONV_HALO + t_rows
    kb_rows = -(-kb // SUBLANE) * SUBLANE

    def body(bg_ref, cg_ref, ha_ref, val_ref, gt_ref, dya_ref, dz_ref, wa_ref, wb_ref,
             dh_ref, dwa_ref, dwb_ref, base, sh, based, shd, tmp, wbc):
        zeros = lambda n: jnp.zeros((n, LANE), F32)
        base[pl.ds(0, CONV_HALO), :] = zeros(CONV_HALO)
        base[pl.ds(nr, SUBLANE), :] = zeros(SUBLANE)
        based[pl.ds(t_rows, CONV_HALO + SUBLANE), :] = zeros(CONV_HALO + SUBLANE)

        def fwd_slot(k_taps, k):
            return _tap_slot(CONV_HALO - (k_taps - 1) + k)

        def bwd_slot(k_taps, k):
            return _tap_slot(k_taps - 1 - k)

        def conv(w_ref, k_taps, src, slot, b, n):
            acc = None
            for k in range(k_taps):
                r, q = slot(k_taps, k)
                term = w_ref[pl.ds(k, 1), :] * _shifted_rows(src, r, b + q, n)
                acc = term if acc is None else acc + term
            return acc

        def by_residue(k_taps, slot):
            groups = {}
            for k in range(k_taps):
                r, q = slot(k_taps, k)
                groups.setdefault(r, []).append((k, q // SUBLANE))
            return groups

        def wgrad_loop(w_ref, k_taps):
            n_sub = WGRAD_ROWS // SUBLANE
            for k in range(k_taps):
                wbc[k] = jnp.broadcast_to(w_ref[pl.ds(k, 1), :], (SUBLANE, LANE))
            fwd, bwd = by_residue(k_taps, fwd_slot), by_residue(k_taps, bwd_slot)

            def window(src, r, taps, b):
                span = n_sub + max(qi for _, qi in taps)
                return [_shifted_rows(src, r, b + SUBLANE * i, SUBLANE) for i in range(span)]

            def step(b, accs):
                accs = list(accs)
                dv = [based[pl.ds(b + SUBLANE * j, SUBLANE), :] for j in range(n_sub)]
                for r, taps in fwd.items():
                    win = window((base, sh), r, taps, b)
                    for k, qi in taps:
                        t = dv[0] * win[qi]
                        for j in range(1, n_sub):
                            t = t + dv[j] * win[qi + j]
                        accs[k] = accs[k] + t
                outs = [None] * n_sub
                for r, taps in bwd.items():
                    win = window((based, shd), r, taps, b)
                    for k, qi in taps:
                        wk = wbc[k]
                        for j in range(n_sub):
                            term = wk * win[qi + j]
                            outs[j] = term if outs[j] is None else outs[j] + term
                for j in range(n_sub):
                    tmp[pl.ds(b + SUBLANE * j, SUBLANE), :] = outs[j]
                return tuple(accs)

            return _row_loop(t_rows, WGRAD_ROWS, step, tuple(zeros(SUBLANE) for _ in range(k_taps)))

        def store_taps(ref, accs, rows):
            for k, acc in enumerate(accs):
                ref[pl.ds(k, 1), :] = jnp.sum(acc, axis=0, keepdims=True)
            if rows > len(accs):
                ref[pl.ds(len(accs), rows - len(accs)), :] = zeros(rows - len(accs))

        def fill_a(b, c):
            sl = pl.ds(b, CONV_CHUNK)
            base[pl.ds(CONV_HALO + b, CONV_CHUNK), :] = cg_ref[sl, :] * ha_ref[sl, :]
            based[sl, :] = dya_ref[sl, :] * bg_ref[sl, :]
            return c

        _row_loop(t_rows, CONV_CHUNK, fill_a)
        _fill_shifted(sh, base, sorted({fwd_slot(ka, k)[0] for k in range(ka)}), nr)
        _fill_shifted(shd, based, sorted({bwd_slot(ka, k)[0] for k in range(ka)}), nr)

        def d_bgate(b, c):
            sl = pl.ds(b, CONV_CHUNK)
            dh_ref[0, sl, :] = (dya_ref[sl, :] * conv(wa_ref, ka, (base, sh), fwd_slot, b, CONV_CHUNK)).astype(BF16)
            return c

        _row_loop(t_rows, CONV_CHUNK, d_bgate)
        store_taps(dwa_ref, wgrad_loop(wa_ref, ka), SUBLANE)

        def d_ch(b, c):
            sl = pl.ds(b, CONV_CHUNK)
            dua = tmp[sl, :]
            dh_ref[1, sl, :] = (dua * ha_ref[sl, :]).astype(BF16)
            dh_ref[2, sl, :] = (dua * cg_ref[sl, :]).astype(BF16)
            return c

        _row_loop(t_rows, CONV_CHUNK, d_ch)

        def fill_b(b, c):
            sl = pl.ds(b, CONV_CHUNK)
            base[pl.ds(CONV_HALO + b, CONV_CHUNK), :] = val_ref[sl, :] * jax.nn.sigmoid(gt_ref[sl, :])
            based[sl, :] = dz_ref[sl, :]
            return c

        _row_loop(t_rows, CONV_CHUNK, fill_b)
        _fill_shifted(sh, base, range(SUBLANE), nr)
        _fill_shifted(shd, based, range(SUBLANE), nr)
        store_taps(dwb_ref, wgrad_loop(wb_ref, kb), kb_rows)

        def d_glu(b, c):
            sl = pl.ds(b, CONV_CHUNK)
            dgg = tmp[sl, :]
            sig = jax.nn.sigmoid(gt_ref[sl, :])
            dh_ref[3, sl, :] = (dgg * sig).astype(BF16)
            dh_ref[4, sl, :] = (dgg * val_ref[sl, :] * (sig * (1.0 - sig))).astype(BF16)
            return c

        _row_loop(t_rows, CONV_CHUNK, d_glu)

    def col(g):
        return pl.BlockSpec((t_rows, LANE), lambda i, g=g: (0, g * nt + i))

    tile = lambda rows: pl.BlockSpec((rows, LANE), lambda i: (0, i))
    return _host_call(
        body, grid=(nt,),
        in_specs=[col(0), col(1), col(2), col(3), col(4), tile(t_rows), tile(t_rows), tile(ka), tile(kb)],
        out_specs=[pl.BlockSpec((5, t_rows, LANE), lambda i: (0, 0, i)), tile(SUBLANE), tile(kb_rows)],
        out_shape=[jax.ShapeDtypeStruct((5, t_rows, wa_w), BF16), jax.ShapeDtypeStruct((SUBLANE, wa_w), F32),
                   jax.ShapeDtypeStruct((kb_rows, wa_w), F32)],
        scratch_shapes=[pltpu.VMEM((nr + SUBLANE, LANE), F32), pltpu.VMEM((SUBLANE, nr, LANE), F32),
                        pltpu.VMEM((nr + SUBLANE, LANE), F32), pltpu.VMEM((SUBLANE, nr, LANE), F32),
                        pltpu.VMEM((t_rows, LANE), F32), pltpu.VMEM((kb_rows, SUBLANE, LANE), F32)],
        args=(hin, hin, hin, hin, hin, dy, dz, wa, wb), name="mix_conv_bwd", comm=comm)


def _bwd_in_proj(dh5, win_t, h0, dh1, g1, tm, comm):
    t_rows, d = h0.shape
    groups, _, w = dh5.shape

    def body(dh_ref, w_ref, h0_ref, dh1_ref, g1_ref, dh0_ref, dg1_ref):
        dxn1 = None
        for g in range(groups):
            part = _dot_nn(dh_ref[g], w_ref[pl.ds(g * w, w), :])
            dxn1 = part if dxn1 is None else dxn1 + part
        h0 = h0_ref[...]
        r1 = _rstd(h0)
        h0h = h0 * r1
        _acc_rows(dg1_ref, dxn1 * h0h, pl.program_id(0) == 0)
        dh0_ref[...] = dh1_ref[...] + _rms_bwd(dxn1 * g1_ref[...], h0h, r1)

    blk = pl.BlockSpec((tm, d), lambda i: (i, 0))
    return _host_call(
        body, grid=(t_rows // tm,),
        in_specs=[pl.BlockSpec((groups, tm, w), lambda i: (0, i, 0)), _resident(win_t.shape), blk, blk, _full((1, d))],
        out_specs=[blk, _full((1, d))],
        out_shape=[jax.ShapeDtypeStruct((t_rows, d), F32), jax.ShapeDtypeStruct((1, d), F32)],
        args=(dh5, win_t, h0, dh1, g1), name="bwd_in_proj", comm=comm)


def _pair_small(smalls, d):
    (dmeta, dg1, dg2, dg3, dg4, dbb, dlg, dlb, lossv, dwa, dwb) = smalls
    half = d // 2
    kb_rows = dwb.shape[0]

    def body(dmeta_ref, dg1_ref, dg2_ref, dg3_ref, dg4_ref, dbb_ref, dlg_ref, dlb_ref, loss_ref, dwa_ref, dwb_ref,
             sums_ref, pbuf, psib, ps_send, ps_recv):
        x, y, c = _mesh_pos()
        _pair_handshake()
        pbuf[...] = jnp.zeros_like(pbuf)
        pbuf[pl.ds(0, N_META), :] = dmeta_ref[...]
        for row, ref in ((16, dg1_ref), (17, dg2_ref), (18, dg3_ref), (19, dg4_ref)):
            pbuf[pl.ds(row, 1), :] = ref[...]
        pbuf[pl.ds(20, 1), pl.ds(0, half)] = dbb_ref[...]
        pbuf[pl.ds(20, 1), pl.ds(half, half)] = dlg_ref[...]
        pbuf[pl.ds(21, 1), pl.ds(0, half)] = dlb_ref[...]
        lv = loss_ref[...]
        pbuf[pl.ds(21, 1), pl.ds(half, half)] = lv[:, :half] + lv[:, half:]
        pbuf[pl.ds(24, SUBLANE), pl.ds(0, half)] = dwa_ref[...]
        pbuf[pl.ds(32, kb_rows), pl.ds(0, half)] = dwb_ref[...]
        to_sib = _remote(pbuf, psib, ps_send.at[0], ps_recv.at[0], (x, y, 1 - c))
        to_sib.start()
        to_sib.wait_recv()
        s = pbuf[...] + psib[...]
        for k in range(3):
            sums_ref[k] = s
        to_sib.wait_send()

    return pl.pallas_call(
        body, out_shape=jax.ShapeDtypeStruct((3, SMALL_ROWS, d), F32), in_specs=[VMEM] * 11, out_specs=VMEM,
        scratch_shapes=[pltpu.VMEM((SMALL_ROWS, d), F32), pltpu.VMEM((SMALL_ROWS, d), F32),
                        pltpu.SemaphoreType.DMA((1,)), pltpu.SemaphoreType.DMA((1,))],
        name="pair_small", compiler_params=pltpu.CompilerParams(vmem_limit_bytes=VMEM_LIMIT, collective_id=PAIR_BARRIER_ID))(*smalls)


def _total_small(own, others):
    _, r, d = own.shape

    def body(own_ref, others_ref, tot_ref, chip_p):
        x, y, _ = _mesh_pos()
        chip_p[2 * x + y] = own_ref[0]
        for k, (cx, cy) in enumerate(_other_chips(x, y)):
            chip_p[2 * cx + cy] = others_ref[k]
        tot_ref[...] = ((chip_p[0] + chip_p[1]) + chip_p[2]) + chip_p[3]

    return pl.pallas_call(body, out_shape=jax.ShapeDtypeStruct((r, d), F32), scratch_shapes=[pltpu.VMEM((4, r, d), F32)],
                          name="total_small", compiler_params=_cparams())(own, others)


def _adamw(w, g, m, v):
    m = ADAM_B1 * m + (1.0 - ADAM_B1) * g
    v = ADAM_B2 * v + (1.0 - ADAM_B2) * jnp.square(g)
    m_hat = m / (1.0 - ADAM_B1 ** ADAM_STEP)
    v_hat = v / (1.0 - ADAM_B2 ** ADAM_STEP)
    delta = -ADAM_LR * (m_hat / (jnp.sqrt(v_hat) + ADAM_EPS) + ADAM_WD * w)
    return delta, m, v


def _adam_big(g, pair, part, w, m, v, name):
    r, d = w.shape
    cols = d // ADAM_COL_BLOCKS

    def body(me_ref, g_ref, pair_ref, part_ref, w_ref, m_ref, v_ref, go_ref, d_ref, mo_ref, vo_ref):
        g = g_ref[...].astype(F32) + pair_ref[...].astype(F32)
        for k in range(3):
            g = g + part_ref[k].astype(F32)
        go_ref[...] = g
        d_ref[...], mo_ref[...], vo_ref[...] = _adamw(w_ref[...], g, m_ref[...], v_ref[...])

    blk = pl.BlockSpec((r, cols), lambda i, me_ref: (0, i))
    grid_spec = pltpu.PrefetchScalarGridSpec(
        num_scalar_prefetch=1, grid=(ADAM_COL_BLOCKS,),
        in_specs=[pl.BlockSpec((r, cols), lambda i, me_ref: (me_ref[0], i)),
                  pl.BlockSpec((None, r, cols), lambda i, me_ref: (0, 0, i)),
                  pl.BlockSpec((3, r, cols), lambda i, me_ref: (0, 0, i)), blk, blk, blk],
        out_specs=[blk, blk, blk, blk])
    me = jnp.reshape(_dev_index(*_mesh_pos()), (1,)).astype(jnp.int32)
    return pl.pallas_call(body, out_shape=[jax.ShapeDtypeStruct((r, d), F32)] * 4, grid_spec=grid_spec, name=name,
                          compiler_params=_cparams(1))(me, g, pair, part, w, m, v)


def _adam_small(gs, ws, ms, vs):
    n = len(gs)

    def body(*refs):
        ins, outs = refs[:4 * n], refs[4 * n:]
        for i in range(n):
            g = ins[i][...]
            delta, m, v = _adamw(ins[n + i][...], g, ins[2 * n + i][...], ins[3 * n + i][...])
            outs[i][...] = delta
            outs[n + i][...] = m
            outs[2 * n + i][...] = v

    shapes = [jax.ShapeDtypeStruct(w.shape, F32) for w in ws]
    return pl.pallas_call(body, out_shape=shapes * 3, name="adam_small", compiler_params=_cparams())(*gs, *ws, *ms, *vs)


def kernel(x, meta_tokens, pre_mix_norm, w_in, conv_a_w, conv_b_w, conv_b_bias, ln_b_gain, ln_b_bias, w_out, post_mix_norm, pre_ffn_norm, w_gate, w_up, w_down, post_ffn_norm, loss_target, m_meta_tokens, m_pre_mix_norm, m_w_in, m_conv_a_w, m_conv_b_w, m_conv_b_bias, m_ln_b_gain, m_ln_b_bias, m_w_out, m_post_mix_norm, m_pre_ffn_norm, m_w_gate, m_w_up, m_w_down, m_post_ffn_norm, v_meta_tokens, v_pre_mix_norm, v_w_in, v_conv_a_w, v_conv_b_w, v_conv_b_bias, v_ln_b_gain, v_ln_b_bias, v_w_out, v_post_mix_norm, v_pre_ffn_norm, v_w_gate, v_w_up, v_w_down, v_post_ffn_norm):
    _, seq, d = x.shape
    ka, ca_loc = conv_a_w.shape[1:]
    kb, cb_loc = conv_b_w.shape[1:]
    wa_w = ca_loc * N_DEV
    assert cb_loc == ca_loc and wa_w % LANE == 0 and w_in.shape[2] * N_DEV == 5 * wa_w and 2 * wa_w == d
    pad = (-(N_META + seq)) % ROW_ALIGN
    x0 = pad + N_META
    t_rows = x0 + seq
    assert t_rows % (N_ROW_BLOCKS * BF16_ROWS) == 0 and t_rows % CONV_CHUNK == 0 and d % LANE == 0
    tm = t_rows // N_ROW_BLOCKS
    tm2 = t_rows // 2
    me = _dev_index(*_mesh_pos())

    def as_rows(w_in_like, w_out_like, w_gate_like, w_up_like, w_down_like):
        return (w_in_like[0].T, w_out_like[0], w_gate_like[0].T, w_up_like[0].T, w_down_like[0])

    w_loc = as_rows(w_in, w_out, w_gate, w_up, w_down)
    rows = [w.shape[0] for w in w_loc]
    assert all(r % ADD_CHUNK == 0 for r in rows)
    P_IN, P_OUT, P_GATE, P_UP, P_DOWN = range(N_BIG)

    sm = jnp.zeros((SM_ROWS, LANE), F32)
    sm = sm.at[0:N_META, :].set(meta_tokens)
    sm = sm.at[16:16 + ka, 0:ca_loc].set(conv_a_w[0])
    sm = sm.at[24:24 + kb, 0:cb_loc].set(conv_b_w[0])
    wl, wfull, sm_all, h0, tgt = _gather_first(w_loc, sm, [(P_IN, 0, rows[P_IN])], x[0], loss_target[0], t_rows, x0)
    wa =jnp.transpose(sm_all[:, 16:16 + ka, 0:ca_loc], (1, 0, 2)).reshape(ka, wa_w)
    wb = jnp.transpose(sm_all[:, 24:24 + kb, 0:cb_loc], (1, 0, 2)).reshape(kb, wa_w)

    later = (P_OUT, P_GATE, P_UP, P_DOWN)
    sems, wl, started, _ = _gather_start(wl, wfull, later, rows, START_BARRIER_IDS[0])
    for p, arr in zip(later, started):
        wfull[p] = arr

    def arrived(p, after, name):
        nonlocal wl
        wl, wfull[p] = _gather_wait(wl, wfull[p], sems[later.index(p)], after, rows[p], name)
        return _forward_comm(wfull[p], rows[p])

    (xn1, hin), _ = _in_proj(h0, pre_mix_norm, wfull[P_IN], tm2, None)
    (ya, z), (wfull[P_OUT],) = _mix_conv_fwd(hin, wa, wb, conv_b_bias, wa_w, arrived(P_OUT, hin, "gather_wait_out"))
    (y, mix, h1, xn2), (wfull[P_GATE],) = _out_proj(ya, z, ln_b_gain, ln_b_bias, wfull[P_OUT], h0, post_mix_norm, pre_ffn_norm, tm2,
                                                    arrived(P_GATE, z, "gather_wait_gate"))
    arrived(P_UP, xn2, "gather_wait_up")
    wfull[P_UP] = _forward_now(wfull[P_UP], rows[P_UP], "forward_up")
    (ga, gu, s), _ = _gate_up(xn2, wfull[P_GATE], wfull[P_UP], tm, None)
    arrived(P_DOWN, s, "gather_wait_down")
    wfull[P_DOWN] = _forward_now(wfull[P_DOWN], rows[P_DOWN], "forward_down")
    dh2, dff, dg4, lossv = _down_loss(s, wfull[P_DOWN], h1, tgt, post_ffn_norm, tm2, x0)

    gwd = _wgrad(s, dff, "wgrad_down")
    (da, du), (pair_d,) = _bwd_down(dff, wfull[P_DOWN], ga, gu, tm, _pair_comm(gwd, rows[P_DOWN]))
    (flight_d,), token = _chip_start([_pair_sum(gwd, pair_d, rows[P_DOWN], "pair_sum_down")], "chip_start_down", START_BARRIER_IDS[1])
    gwg = _wgrad(da, xn2, "wgrad_gate", [token])
    gwu = _wgrad(du, xn2, "wgrad_up")
    (dh1, dg3), (pair_g, pair_u) = _bwd_ffn_in(da, du, wfull[P_GATE], wfull[P_UP], h1, dh2, pre_ffn_norm, tm,
                                               _merge_comms([_pair_comm(gwg, rows[P_GATE]), _pair_comm(gwu, rows[P_UP])]))
    (flight_g, flight_u), token = _chip_start([_pair_sum(gwg, pair_g, rows[P_GATE], "pair_sum_gate"),
                                               _pair_sum(gwu, pair_u, rows[P_UP], "pair_sum_up")], "chip_start_gate_up",
                                              START_BARRIER_IDS[2])
    dmix, dya, dz, dg2, dlg, dlb, dbb = _bwd_out_proj(dh1, mix, wfull[P_OUT], post_mix_norm, z, ln_b_gain, ln_b_bias, tm2, [token])
    gwo = _wgrad(y, dmix, "wgrad_out")
    (dh5, dwa, dwb), (pair_o,) = _mix_conv_bwd(hin, dya, dz, wa, wb, wa_w, _pair_comm(gwo, rows[P_OUT]))
    (flight_o,), token = _chip_start([_pair_sum(gwo, pair_o, rows[P_OUT], "pair_sum_out")], "chip_start_out", START_BARRIER_IDS[3])
    gwi = _wgrad(dh5, xn1, "wgrad_in", [token])
    (dh0, dg1), (pair_i,) = _bwd_in_proj(dh5, wfull[P_IN], h0, dh1, pre_mix_norm, tm2, _pair_comm(gwi, rows[P_IN]))
    grad_x = dh0[x0:][None]
    dmeta = dh0[x0 - N_META:x0]
    small_sums = _pair_small((dmeta, dg1, dg2, dg3, dg4, dbb, dlg, dlb, lossv, dwa, dwb), d)
    (flight_s,), token = _chip_start([small_sums], "chip_start_small", START_BARRIER_IDS[4])
    (flight_i,), token = _chip_start([_pair_sum(gwi, pair_i, rows[P_IN], "pair_sum_in", [token])], "chip_start_in",
                                     START_BARRIER_IDS[5])

    def landed(flight, after, tag):
        sems_p, sums, land = flight
        return _chip_wait(sums, land, sems_p, after, "chip_wait_" + tag)

    m_loc = as_rows(m_w_in, m_w_out, m_w_gate, m_w_up, m_w_down)
    v_loc = as_rows(v_w_in, v_w_out, v_w_gate, v_w_up, v_w_down)
    full_grads = {P_IN: gwi, P_OUT: gwo, P_GATE: gwg, P_UP: gwu, P_DOWN: gwd}
    pairs = {P_IN: pair_i, P_OUT: pair_o, P_GATE: pair_g, P_UP: pair_u, P_DOWN: pair_d}
    flights = {P_IN: flight_i, P_OUT: flight_o, P_GATE: flight_g, P_UP: flight_u, P_DOWN: flight_d}
    names = {P_IN: "w_in", P_OUT: "w_out", P_GATE: "w_gate", P_UP: "w_up", P_DOWN: "w_down"}
    bigs = {}

    def adam_big(p, after):
        part = landed(flights[p], after, names[p])
        res = _adam_big(full_grads[p], pairs[p], part, w_loc[p], m_loc[p], v_loc[p], "adam_" + names[p])
        bigs[names[p]] = [(o.T if p in (P_IN, P_GATE, P_UP) else o)[None] for o in res]
        return res[1]

    for p in (P_DOWN, P_GATE, P_UP, P_OUT):
        token = adam_big(p, token)
    ptot = _total_small(flight_s[1], landed(flight_s, token, "small"))
    half = d // 2
    loss = (0.5 / d) * jnp.sum(ptot[21, half:])
    g_meta = lax.dynamic_slice(ptot, (0, me * (d // N_DEV)), (N_META, d // N_DEV))
    g_small = [g_meta, ptot[16:17], lax.dynamic_slice(ptot, (24, me * ca_loc), (ka, ca_loc))[None],
               lax.dynamic_slice(ptot, (32, me * cb_loc), (kb, cb_loc))[None],
               ptot[20:21, :half], ptot[20:21, half:], ptot[21:22, :half], ptot[17:18], ptot[18:19], ptot[19:20]]
    w_small = [meta_tokens, pre_mix_norm, conv_a_w, conv_b_w, conv_b_bias, ln_b_gain, ln_b_bias, post_mix_norm,
               pre_ffn_norm, post_ffn_norm]
    m_small = [m_meta_tokens, m_pre_mix_norm, m_conv_a_w, m_conv_b_w, m_conv_b_bias, m_ln_b_gain, m_ln_b_bias,
               m_post_mix_norm, m_pre_ffn_norm, m_post_ffn_norm]
    v_small = [v_meta_tokens, v_pre_mix_norm, v_conv_a_w, v_conv_b_w, v_conv_b_bias, v_ln_b_gain, v_ln_b_bias,
               v_post_mix_norm, v_pre_ffn_norm, v_post_ffn_norm]
    small = _adam_small(g_small, w_small, m_small, v_small)
    n_small = len(w_small)
    d_small, nm_small, nv_small = small[:n_small], small[n_small:2 * n_small], small[2 * n_small:]

    adam_big(P_IN, small[0])

    def ordered(pick_small, pick_big):
        sm_it = iter(range(n_small))
        out = []
        for name in ("s", "s", "w_in", "s", "s", "s", "s", "s", "w_out", "s", "s", "w_gate", "w_up", "w_down", "s"):
            out.append(pick_small(next(sm_it)) if name == "s" else pick_big(name))
        return out

    grads = ordered(lambda i: g_small[i], lambda n: bigs[n][0])
    deltas = ordered(lambda i: d_small[i], lambda n: bigs[n][1])
    new_m = ordered(lambda i: nm_small[i], lambda n: bigs[n][2])
    new_v = ordered(lambda i: nv_small[i], lambda n: bigs[n][3])
    return (loss, grad_x, *grads, *deltas, *new_m, *new_v)
```

```python
import jax
import jax.numpy as jnp
from jax import lax
from jax.experimental import pallas as pl
from jax.experimental.pallas import tpu as pltpu

F32 = jnp.float32
BF16 = jnp.bfloat16
MESH = pl.DeviceIdType.MESH

N_META = 16
N_DEV = 8
RMS_EPS = 1e-6
LN_EPS = 1e-5
ADAM_LR = 0.001
ADAM_B1 = 0.9
ADAM_B2 = 0.999
ADAM_EPS = 1e-08
ADAM_WD = 0.01
ADAM_STEP = 10

LANE = 128
SUBLANE = 8
BF16_ROWS = 16
ROW_ALIGN = 128
N_ROW_BLOCKS = 4
CONV_HALO = 32
CONV_CHUNK = 64
WGRAD_ROWS = 32
N_CHUNK = 512
WGRAD_TILE_MAX = 1408
ADD_CHUNK = 32
ADAM_COL_BLOCKS = 2
COPY_PIECES = 4
V7X_VMEM_BYTES = 64 * 1024 * 1024
VMEM_LIMIT = V7X_VMEM_BYTES - 6 * 1024 * 1024
SMALL_ROWS = 64
SM_ROWS = 56
N_BIG = 5

ANY = pl.BlockSpec(memory_space=pl.ANY)
VMEM = pl.BlockSpec(memory_space=pltpu.VMEM)


def _cparams(n_grid_axes=0):
    sem = ("arbitrary",) * n_grid_axes if n_grid_axes else None
    return pltpu.CompilerParams(dimension_semantics=sem, vmem_limit_bytes=VMEM_LIMIT)


def _mesh_pos():
    return lax.axis_index("x"), lax.axis_index("y"), lax.axis_index("c")


def _dev_index(px, py, pc):
    return 4 * px + 2 * py + pc


def _other_chips(x, y):
    return [(1 - x, y), (x, 1 - y), (1 - x, 1 - y)]


def _full(shape):
    return pl.BlockSpec(shape, lambda *_: (0,) * len(shape))


def _resident(shape):
    return pl.BlockSpec(shape, lambda *_: (0,) * len(shape), pipeline_mode=pl.Buffered(1))


def _dot_nt(a, w):
    return lax.dot_general(a, w, (((1,), (1,)), ((), ())), preferred_element_type=F32)


def _dot_nn(a, w):
    return jnp.dot(a, w, preferred_element_type=F32)


def _chunks(n, c):
    out, o = [], 0
    while o < n:
        out.append((o, min(c, n - o)))
        o += c
    return out


def _rstd(h):
    return lax.rsqrt(jnp.mean(h * h, axis=-1, keepdims=True) + RMS_EPS)


def _rms_bwd(dyh, yh, r):
    return r * (dyh - yh * jnp.mean(dyh * yh, axis=-1, keepdims=True))


def _silu_grad(a, sig):
    return sig * (1.0 + a * (1.0 - sig))


def _acc_rows(ref, val, first):
    s = jnp.sum(val, axis=0, keepdims=True)

    @pl.when(first)
    def _():
        ref[...] = s

    @pl.when(jnp.logical_not(first))
    def _():
        ref[...] += s


def _row_loop(t_rows, chunk, fn, carry=None):
    def step(i, c):
        return fn(pl.multiple_of(i * chunk, chunk), c)

    return lax.fori_loop(0, t_rows // chunk, step, carry)


def _remote(src, dst, send_sem, recv_sem, to):
    return pltpu.make_async_remote_copy(src_ref=src, dst_ref=dst, send_sem=send_sem, recv_sem=recv_sem,
                                        device_id=to, device_id_type=MESH)


class _Comm:
    def __init__(self, inputs, out_shapes, aliases, scratch, start, finish):
        self.inputs, self.out_shapes, self.aliases, self.scratch = list(inputs), list(out_shapes), dict(aliases), list(scratch)
        self.start, self.finish = start, finish


def _merge_comms(comms):
    inputs, out_shapes, aliases, scratch, spans = [], [], {}, [], []
    for cm in comms:
        spans.append((len(inputs), len(out_shapes), len(scratch), cm))
        aliases.update({len(inputs) + k: len(out_shapes) + v for k, v in cm.aliases.items()})
        inputs += cm.inputs
        out_shapes += cm.out_shapes
        scratch += cm.scratch

    def run(which):
        def fn(ins, outs, scr):
            for i0, o0, s0, cm in spans:
                getattr(cm, which)(ins[i0:i0 + len(cm.inputs)], outs[o0:o0 + len(cm.out_shapes)], scr[s0:s0 + len(cm.scratch)])
        return fn

    return _Comm(inputs, out_shapes, aliases, scratch, run("start"), run("finish"))


def _host_call(body, *, grid, in_specs, out_specs, out_shape, args, name, scratch_shapes=(), comm=None, after=()):
    talks = comm is not None
    if comm is None:
        comm = _Comm([], [], {}, [], lambda *_: None, lambda *_: None)
    n_in, n_out, n_scr = len(args), len(out_shape), len(scratch_shapes)
    c_in, c_out = len(comm.inputs), len(comm.out_shapes)
    n_after = len(after)

    def open_comm(c_ins, c_outs, c_scr):
        if talks:
            _pair_handshake()
        comm.start(c_ins, c_outs, c_scr)

    def hosted(*refs):
        ins, c_ins = refs[:n_in], refs[n_in:n_in + c_in]
        o0 = n_in + c_in + n_after
        outs, c_outs = refs[o0:o0 + n_out], refs[o0 + n_out:o0 + n_out + c_out]
        s0 = o0 + n_out + c_out
        scr, c_scr = refs[s0:s0 + n_scr], refs[s0 + n_scr:]
        if not grid:
            open_comm(c_ins, c_outs, c_scr)
            body(*ins, *outs, *scr)
            comm.finish(c_ins, c_outs, c_scr)
            return
        first = last = None
        for a, n in enumerate(grid):
            f, l = pl.program_id(a) == 0, pl.program_id(a) == n - 1
            first = f if first is None else jnp.logical_and(first, f)
            last = l if last is None else jnp.logical_and(last, l)

        @pl.when(first)
        def _():
            open_comm(c_ins, c_outs, c_scr)

        body(*ins, *outs, *scr)

        @pl.when(last)
        def _():
            comm.finish(c_ins, c_outs, c_scr)

    sem = ("arbitrary",) * len(grid) if grid else None
    params = pltpu.CompilerParams(dimension_semantics=sem, vmem_limit_bytes=VMEM_LIMIT,
                                  collective_id=PAIR_BARRIER_ID if talks else None)
    res = pl.pallas_call(
        hosted, grid=grid, in_specs=list(in_specs) + [ANY] * (c_in + n_after), out_specs=list(out_specs) + [ANY] * c_out,
        out_shape=list(out_shape) + comm.out_shapes, scratch_shapes=list(scratch_shapes) + comm.scratch,
        input_output_aliases={n_in + k: n_out + v for k, v in comm.aliases.items()},
        name=name, compiler_params=params)(*args, *comm.inputs, *after)
    return list(res[:n_out]), list(res[n_out:])


PAIR_BARRIER_ID = 0
START_BARRIER_IDS = (1, 2, 3, 4, 5, 6)


def _chips_handshake():
    x, y, c = _mesh_pos()
    barrier = pltpu.get_barrier_semaphore()
    for chip in _other_chips(x, y):
        pl.semaphore_signal(barrier, inc=1, device_id=(*chip, c), device_id_type=MESH)
    pl.semaphore_wait(barrier, 3)


def _pair_handshake():
    x, y, c = _mesh_pos()
    barrier = pltpu.get_barrier_semaphore()
    pl.semaphore_signal(barrier, inc=1, device_id=(x, y, 1 - c), device_id_type=MESH)
    pl.semaphore_wait(barrier, 1)


GATHER_SEMS = 10


class _Gather:
    def __init__(self, jobs, rows, lo, src_ref, dests, send_sems, recv_sems):
        x, y, c = _mesh_pos()
        me, sib = (x, y, c), (x, y, 1 - c)
        nx, ny, dg = (1 - x, y, c), (x, 1 - y, c), (1 - x, 1 - y, c)
        self.relayed, self.direct, self.relay, self.to_sib, self.sib_fwd = [], [], [], [], []
        for n, (p, r0, nr) in enumerate(jobs):
            assert nr % (2 * BF16_ROWS) == 0
            half = nr // 2

            def rows_of(dev, h, p=p, r0=r0, nr=nr, half=half):
                off, cnt = (r0, nr) if h is None else (r0 + h * half, half)
                return dests[p].at[pl.ds(pl.multiple_of(_dev_index(*dev) * rows[p] + off, BF16_ROWS), cnt), :]

            def mine(h, p=p, r0=r0, nr=nr, half=half):
                off, cnt = (r0, nr) if h is None else (r0 + h * half, half)
                return src_ref.at[pl.ds(lo[p] + off, cnt), :]

            sem = lambda k, n=n: (send_sems.at[GATHER_SEMS * n + k], recv_sems.at[GATHER_SEMS * n + k])
            self.relayed.append([_remote(mine(0), rows_of(me, 0), *sem(0), nx), _remote(mine(1), rows_of(me, 1), *sem(3), ny)])
            self.direct.append([_remote(mine(1), rows_of(me, 1), *sem(1), nx), _remote(mine(0), rows_of(me, 0), *sem(2), ny)])
            self.relay.append([_remote(rows_of(nx, 0), rows_of(nx, 0), *sem(4), ny), _remote(rows_of(ny, 1), rows_of(ny, 1), *sem(5), nx)])
            self.to_sib.append(_remote(mine(None), rows_of(me, None), *sem(6), sib))
            self.sib_fwd.append([_remote(rows_of(dev, None), rows_of(dev, None), *sem(7 + i), sib) for i, dev in enumerate((nx, ny, dg))])

    def start(self):
        for group in (self.relayed, self.direct):
            for cps in group:
                for cp in cps:
                    cp.start()
        for cp in self.to_sib:
            cp.start()

    def mid(self):
        for first, relay in zip(self.relayed, self.relay):
            for arrived, onward in zip(first, relay):
                arrived.wait_recv()
                onward.start()

    def finish(self):
        for direct, relay, fwd in zip(self.direct, self.relay, self.sib_fwd):
            for k in range(2):
                direct[k].wait_recv()
                fwd[k].start()
            for cp in relay:
                cp.wait_recv()
            fwd[2].start()
        for n in range(len(self.to_sib)):
            self.to_sib[n].wait_recv()
            for cp in self.sib_fwd[n]:
                cp.wait_recv()
            for cp in self.relayed[n] + self.direct[n] + self.relay[n] + [self.to_sib[n]] + self.sib_fwd[n]:
                cp.wait_send()


HBM = pl.BlockSpec(memory_space=pltpu.HBM)
SEM = pl.BlockSpec(memory_space=pltpu.SEMAPHORE)
FLOWS = pltpu.SideEffectType.DATAFLOW_SIDE_EFFECTING


def _in_hbm(a):
    return pltpu.with_memory_space_constraint(a, pltpu.HBM)


def _gather_start(wl, dests, ps, rows, barrier_id):
    lo = [sum(rows[:p]) for p in range(N_BIG)]
    n = len(ps)

    def body(*refs):
        wl_ref, dest_refs = refs[0], refs[1:1 + n]
        sends, recvs = refs[1 + n:1 + 2 * n], refs[1 + 2 * n:1 + 3 * n]
        token = refs[-1]
        _chips_handshake()
        x, y, c = _mesh_pos()
        jme = _dev_index(x, y, c)
        for i, p in enumerate(ps):
            mine = dest_refs[i].at[pl.ds(pl.multiple_of(jme * rows[p], BF16_ROWS), rows[p]), :]
            for chip in _other_chips(x, y):
                _remote(wl_ref.at[pl.ds(lo[p], rows[p]), :], mine, sends[i], recvs[i], (*chip, c)).start()
        token[...] = jnp.zeros_like(token)

    thru = [pltpu.HBM(wl.shape, wl.dtype)] + [pltpu.HBM(dests[p].shape, BF16) for p in ps]
    res = pl.pallas_call(
        body, name="gather_start",
        out_shape=tuple([pltpu.SemaphoreType.DMA(())] * (2 * n) + thru + [jax.ShapeDtypeStruct((SUBLANE, LANE), F32)]),
        in_specs=[HBM] * (1 + n), out_specs=tuple([SEM] * (2 * n) + [HBM] * (1 + n) + [VMEM]),
        input_output_aliases={i: 2 * n + i for i in range(1 + n)},
        compiler_params=pltpu.CompilerParams(has_side_effects=FLOWS, collective_id=barrier_id))(
            _in_hbm(wl), *[_in_hbm(dests[p]) for p in ps])
    sems = [(res[i], res[n + i]) for i in range(n)]
    return sems, res[2 * n], list(res[2 * n + 1:3 * n + 1]), res[-1]


def _gather_wait(wl, dest, sems, after, r, name):
    def body(wl_ref, dest_ref, send_sem, recv_sem, after_ref, wl_out, dest_out):
        x, y, c = _mesh_pos()
        three = dest_ref.at[pl.ds(0, 3 * r), :]
        cp = _remote(three, three, send_sem, recv_sem, (x, y, 1 - c))
        cp.wait_send()
        cp.wait_recv()

    res = pl.pallas_call(
        body, name=name, out_shape=(pltpu.HBM(wl.shape, wl.dtype), pltpu.HBM(dest.shape, dest.dtype)),
        in_specs=[HBM, HBM, SEM, SEM, ANY], out_specs=(HBM, HBM), input_output_aliases={0: 0, 1: 1},
        compiler_params=pltpu.CompilerParams(has_side_effects=FLOWS))(wl, dest, sems[0], sems[1], after)
    return res[0], res[1]


def _forward_comm(dest, r):
    def descs(ins, outs, scr):
        x, y, c = _mesh_pos()
        cps = []
        for k, chip in enumerate(_other_chips(x, y)):
            blk = outs[0].at[pl.ds(pl.multiple_of(_dev_index(*chip, c) * r, BF16_ROWS), r), :]
            cps.append(_remote(blk, blk, scr[0].at[k], scr[1].at[k], (x, y, 1 - c)))
        return cps

    def start(ins, outs, scr):
        for cp in descs(ins, outs, scr):
            cp.start()

    def finish(ins, outs, scr):
        cps = descs(ins, outs, scr)
        for cp in cps:
            cp.wait_recv()
        for cp in cps:
            cp.wait_send()

    return _Comm([dest], [jax.ShapeDtypeStruct(dest.shape, dest.dtype)], {0: 0},
                 [pltpu.SemaphoreType.DMA((3,)), pltpu.SemaphoreType.DMA((3,))], start, finish)


def _forward_now(dest, r, name):
    _, (dest,) = _host_call(lambda: None, grid=(), in_specs=[], out_specs=[], out_shape=[], args=(), name=name,
                            comm=_forward_comm(dest, r))
    return dest


def _pair_comm(g, r):
    d = g.shape[1]

    def descs(ins, outs, scr):
        x, y, c = _mesh_pos()
        chips = [(x, y)] + _other_chips(x, y)
        return [_remote(ins[0].at[pl.ds(pl.multiple_of(_dev_index(*chip, 1 - c) * r, BF16_ROWS), r), :], outs[0].at[k],
                        scr[0].at[k], scr[1].at[k], (x, y, 1 - c)) for k, chip in enumerate(chips)]

    def start(ins, outs, scr):
        for cp in descs(ins, outs, scr):
            cp.start()

    def finish(ins, outs, scr):
        cps = descs(ins, outs, scr)
        for cp in cps:
            cp.wait_recv()
        for cp in cps:
            cp.wait_send()

    comm = _Comm([g], [jax.ShapeDtypeStruct((4, r, d), BF16)], {},
                 [pltpu.SemaphoreType.DMA((4,)), pltpu.SemaphoreType.DMA((4,))], start, finish)
    return comm


def _pair_sum(g, pair, r, name, after=()):
    d = g.shape[1]

    def body(g_ref, p_ref, *rest):
        o_ref, gbuf, pbuf, sems = rest[len(after):]
        x, y, c = _mesh_pos()
        loads = [pltpu.make_async_copy(p_ref.at[pl.ds(1, 3)], pbuf, sems.at[3])]
        for k, chip in enumerate(_other_chips(x, y)):
            j = _dev_index(*chip, c)
            loads.append(pltpu.make_async_copy(g_ref.at[pl.ds(pl.multiple_of(j * r, BF16_ROWS), r), :], gbuf.at[k], sems.at[k]))
        for cp in loads:
            cp.start()
        for cp in loads:
            cp.wait()
        for k in range(3):
            o_ref[k] = (gbuf[k].astype(F32) + pbuf[k].astype(F32)).astype(BF16)

    return pl.pallas_call(
        body, out_shape=jax.ShapeDtypeStruct((3, r, d), BF16), in_specs=[ANY] * (2 + len(after)), out_specs=VMEM,
        scratch_shapes=[pltpu.VMEM((3, r, d), BF16), pltpu.VMEM((3, r, d), BF16), pltpu.SemaphoreType.DMA((4,))],
        name=name, compiler_params=_cparams())(g, pair, *after)


def _chip_start(sums, name, barrier_id):
    n = len(sums)

    def body(*refs):
        srcs, lands = refs[:n], refs[n:2 * n]
        sends, recvs = refs[2 * n:3 * n], refs[3 * n:4 * n]
        _chips_handshake()
        x, y, c = _mesh_pos()
        for i in range(n):
            for k, chip in enumerate(_other_chips(x, y)):
                _remote(srcs[i].at[k], lands[i].at[k], sends[i], recvs[i], (*chip, c)).start()
        refs[-1][...] = jnp.zeros_like(refs[-1])

    zones = [pltpu.HBM(s.shape, s.dtype) for s in sums]
    res = pl.pallas_call(
        body, name=name,
        out_shape=tuple([pltpu.SemaphoreType.DMA(())] * (2 * n) + zones + zones + [jax.ShapeDtypeStruct((SUBLANE, LANE), F32)]),
        in_specs=[HBM] * (2 * n), out_specs=tuple([SEM] * (2 * n) + [HBM] * (2 * n) + [VMEM]),
        input_output_aliases={i: 2 * n + i for i in range(2 * n)},
        compiler_params=pltpu.CompilerParams(has_side_effects=FLOWS, collective_id=barrier_id))(
            *[_in_hbm(s) for s in sums], *[_in_hbm(lax.empty(s.shape, s.dtype)) for s in sums])
    flights = [((res[i], res[n + i]), res[2 * n + i], res[3 * n + i]) for i in range(n)]
    return flights, res[-1]


def _chip_wait(sums, land, sems, after, name):
    def body(sums_ref, land_ref, send_sem, recv_sem, after_ref, sums_out, land_out):
        x, y, c = _mesh_pos()
        cp = _remote(sums_ref, land_ref, send_sem, recv_sem, (x, y, 1 - c))
        cp.wait_send()
        cp.wait_recv()

    res = pl.pallas_call(
        body, name=name, out_shape=(pltpu.HBM(sums.shape, sums.dtype), pltpu.HBM(land.shape, land.dtype)),
        in_specs=[HBM, HBM, SEM, SEM, ANY], out_specs=(HBM, HBM), input_output_aliases={0: 0, 1: 1},
        compiler_params=pltpu.CompilerParams(has_side_effects=FLOWS))(sums, land, sems[0], sems[1], after)
    return res[1]


class _CopyThrough:
    def __init__(self, src_ref, dst_ref, dst_row0, n_rows, buf, sem_in, sem_out):
        rc = n_rows // COPY_PIECES
        piece = lambda ref, o: ref.at[pl.ds(o, rc), :]
        self.loads = [pltpu.make_async_copy(piece(src_ref, k * rc), piece(buf, k * rc), sem_in) for k in range(COPY_PIECES)]
        self.stores = [pltpu.make_async_copy(piece(buf, k * rc), piece(dst_ref, dst_row0 + k * rc), sem_out) for k in range(COPY_PIECES)]
        self.all_in = pltpu.make_async_copy(src_ref, buf, sem_in)
        self.all_out = pltpu.make_async_copy(buf, dst_ref.at[pl.ds(dst_row0, n_rows), :], sem_out)

    def load(self):
        for cp in self.loads:
            cp.start()

    def store(self):
        self.all_in.wait()
        for cp in self.stores:
            cp.start()

    def done(self):
        self.all_out.wait()


def _gather_first(shards, sm, jobs, x2, tgt2, t_rows, x0):
    d = shards[0].shape[1]
    rows = [w.shape[0] for w in shards]
    lo = [sum(rows[:p]) for p in range(N_BIG)]
    n_sems = GATHER_SEMS * len(jobs)
    seq = x2.shape[0]
    assert x0 == ROW_ALIGN and seq % ROW_ALIGN == 0 and d == N_DEV * LANE

    def body(s0, s1, s2, s3, s4, sm_ref, x_ref, tgt_ref, wl_ref, o0, o1, o2, o3, o4, sa_ref, h0_ref, tp_ref,
             wl_v, x_v, tgt_v, heads_v, sa_v, send_sems, recv_sems, ssend, srecv, local_sems, sems_in, sems_out):
        dests = (o0, o1, o2, o3, o4)
        x, y, c = _mesh_pos()
        me = (x, y, c)
        jme = _dev_index(*me)
        padded = [_CopyThrough(x_ref, h0_ref, x0, seq, x_v, sems_in.at[0], sems_out.at[0]),
                  _CopyThrough(tgt_ref, tp_ref, x0, seq, tgt_v, sems_in.at[1], sems_out.at[1])]
        for cp in padded:
            cp.load()
        shard_refs = (s0, s1, s2, s3, s4)
        first = sorted({j[0] for j in jobs})
        for p in first + [p for p in range(N_BIG) if p not in first]:
            wl_v[pl.ds(lo[p], rows[p]), :] = shard_refs[p][...].astype(BF16)
            if p == first[-1]:
                gather = _Gather(jobs, rows, lo, wl_v, dict(enumerate(dests)), send_sems, recv_sems)
                gather.start()
        peers = [(x, y, 1 - c)] + [(*chip, pc) for pc in (c, 1 - c) for chip in _other_chips(x, y)]
        smalls = [_remote(sm_ref, sa_ref.at[jme], ssend.at[k], srecv.at[k], to) for k, to in enumerate(peers)]
        for cp in smalls:
            cp.start()
        mine = [pltpu.make_async_copy(wl_v.at[pl.ds(lo[p], rows[p]), :],
                                      dests[p].at[pl.ds(pl.multiple_of(jme * rows[p], BF16_ROWS), rows[p]), :], local_sems.at[p])
                for p in range(N_BIG)]
        mine.append(pltpu.make_async_copy(wl_v, wl_ref, local_sems.at[N_BIG]))
        mine.append(pltpu.make_async_copy(sm_ref, sa_ref.at[jme], local_sems.at[N_BIG + 1]))
        for cp in mine:
            cp.start()
        later = [p for p in range(N_BIG) if p not in {j[0] for j in jobs}]
        own = [_remote(wl_v.at[pl.ds(lo[p], rows[p]), :], dests[p].at[pl.ds(pl.multiple_of(jme * rows[p], BF16_ROWS), rows[p]), :],
                       ssend.at[7 + i], srecv.at[7 + i], (x, y, 1 - c)) for i, p in enumerate(later)]
        for cp in own:
            cp.start()
        for cp in padded:
            cp.store()
        gather.mid()
        for cp in smalls + own:
            cp.wait_recv()
        mine[-1].wait()
        to_v = pltpu.make_async_copy(sa_ref, sa_v, local_sems.at[N_BIG + 1])
        to_v.start()
        to_v.wait()
        head, zeros = heads_v.at[0], heads_v.at[1]
        head[...] = jnp.zeros_like(head)
        zeros[...] = jnp.zeros_like(zeros)
        for j in range(N_DEV):
            head[pl.ds(x0 - N_META, N_META), pl.ds(j * LANE, LANE)] = sa_v[j, pl.ds(0, N_META), :]
        heads = [pltpu.make_async_copy(head, h0_ref.at[pl.ds(0, x0), :], local_sems.at[N_BIG + 1]),
                 pltpu.make_async_copy(zeros, tp_ref.at[pl.ds(0, x0), :], local_sems.at[N_BIG + 2])]
        for cp in heads:
            cp.start()
        gather.finish()
        for cp in smalls + own:
            cp.wait_send()
        for cp in mine[:-1] + heads:
            cp.wait()
        for cp in padded:
            cp.done()

    out_shape = [jax.ShapeDtypeStruct((sum(rows), d), BF16)]
    out_shape += [jax.ShapeDtypeStruct((N_DEV * r, d), BF16) for r in rows]
    out_shape.append(jax.ShapeDtypeStruct((N_DEV,) + sm.shape, F32))
    out_shape += [jax.ShapeDtypeStruct((t_rows, d), F32)] * 2
    res = pl.pallas_call(
        body, out_shape=out_shape, in_specs=[VMEM] * 6 + [ANY] * 2, out_specs=[ANY] * 9,
        scratch_shapes=[pltpu.VMEM((sum(rows), d), BF16), pltpu.VMEM((seq, d), F32), pltpu.VMEM((seq, d), F32),
                        pltpu.VMEM((2, ROW_ALIGN, d), F32), pltpu.VMEM((N_DEV,) + sm.shape, F32),
                        pltpu.SemaphoreType.DMA((n_sems,)), pltpu.SemaphoreType.DMA((n_sems,)),
                        pltpu.SemaphoreType.DMA((7 + N_BIG,)), pltpu.SemaphoreType.DMA((7 + N_BIG,)),
                        pltpu.SemaphoreType.DMA((N_BIG + 3,)), pltpu.SemaphoreType.DMA((2,)), pltpu.SemaphoreType.DMA((2,))],
        name="gather_first", compiler_params=_cparams())(*shards, sm, x2, tgt2)
    return res[0], list(res[1:1 + N_BIG]), res[1 + N_BIG], res[2 + N_BIG], res[3 + N_BIG]


def _in_proj(h0, g1, win_t, tm, comm):
    t_rows, d = h0.shape
    e = win_t.shape[0]

    def body(h_ref, g_ref, w_ref, xn_ref, hin_ref):
        h = h_ref[...]
        xn = ((h * _rstd(h)) * g_ref[...]).astype(BF16)
        xn_ref[...] = xn
        for o, n in _chunks(e, N_CHUNK):
            hin_ref[:, pl.ds(o, n)] = _dot_nt(xn, w_ref[pl.ds(o, n), :])

    return _host_call(
        body, grid=(t_rows // tm,),
        in_specs=[pl.BlockSpec((tm, d), lambda i: (i, 0)), _full((1, d)), _resident((e, d))],
        out_specs=[pl.BlockSpec((tm, d), lambda i: (i, 0)), pl.BlockSpec((tm, e), lambda i: (i, 0))],
        out_shape=[jax.ShapeDtypeStruct((t_rows, d), BF16), jax.ShapeDtypeStruct((t_rows, e), F32)],
        args=(h0, g1, win_t), name="in_proj", comm=comm)


def _tap_slot(off):
    return off % SUBLANE, (off // SUBLANE) * SUBLANE


def _fill_shifted(sh_ref, base_ref, residues, n_rows):
    for r in residues:
        if r:
            sh_ref[r] = base_ref[pl.ds(r, n_rows), :]


def _shifted_rows(pair, r, start, n):
    base_ref, sh_ref = pair
    return base_ref[pl.ds(start, n), :] if r == 0 else sh_ref[r, pl.ds(start, n), :]


def _mix_conv_fwd(hin, wa, wb, bb, wa_w, comm):
    t_rows = hin.shape[0]
    nt = wa_w // LANE
    ka, kb = wa.shape[0], wb.shape[0]
    nr = CONV_HALO + t_rows

    def body(bg_ref, cg_ref, ha_ref, val_ref, gt_ref, wa_ref, wb_ref, bb_ref, ya_ref, z_ref, base, sh):
        base[pl.ds(0, CONV_HALO), :] = jnp.zeros((CONV_HALO, LANE), F32)
        base[pl.ds(nr, SUBLANE), :] = jnp.zeros((SUBLANE, LANE), F32)

        def conv(w_ref, k_taps, b, n):
            acc = None
            for k in range(k_taps):
                r, q = _tap_slot(CONV_HALO - (k_taps - 1) + k)
                term = w_ref[pl.ds(k, 1), :] * _shifted_rows((base, sh), r, b + q, n)
                acc = term if acc is None else acc + term
            return acc

        def fill_a(b, c):
            base[pl.ds(CONV_HALO + b, CONV_CHUNK), :] = cg_ref[pl.ds(b, CONV_CHUNK), :] * ha_ref[pl.ds(b, CONV_CHUNK), :]
            return c

        _row_loop(t_rows, CONV_CHUNK, fill_a)
        _fill_shifted(sh, base, sorted({_tap_slot(CONV_HALO - (ka - 1) + k)[0] for k in range(ka)}), nr)

        def out_a(b, c):
            ya_ref[pl.ds(b, CONV_CHUNK), :] = (bg_ref[pl.ds(b, CONV_CHUNK), :] * conv(wa_ref, ka, b, CONV_CHUNK)).astype(BF16)
            return c

        _row_loop(t_rows, CONV_CHUNK, out_a)

        def fill_b(b, c):
            base[pl.ds(CONV_HALO + b, CONV_CHUNK), :] = (val_ref[pl.ds(b, CONV_CHUNK), :]
                                                          * jax.nn.sigmoid(gt_ref[pl.ds(b, CONV_CHUNK), :]))
            return c

        _row_loop(t_rows, CONV_CHUNK, fill_b)
        _fill_shifted(sh, base, range(SUBLANE), nr)

        def out_b(b, c):
            z_ref[pl.ds(b, CONV_CHUNK), :] = conv(wb_ref, kb, b, CONV_CHUNK) + bb_ref[...]
            return c

        _row_loop(t_rows, CONV_CHUNK, out_b)

    def col(g):
        return pl.BlockSpec((t_rows, LANE), lambda i, g=g: (0, g * nt + i))

    tile = lambda rows: pl.BlockSpec((rows, LANE), lambda i: (0, i))
    return _host_call(
        body, grid=(nt,),
        in_specs=[col(0), col(1), col(2), col(3), col(4), tile(ka), tile(kb), tile(1)],
        out_specs=[tile(t_rows), tile(t_rows)],
        out_shape=[jax.ShapeDtypeStruct((t_rows, wa_w), BF16), jax.ShapeDtypeStruct((t_rows, wa_w), F32)],
        scratch_shapes=[pltpu.VMEM((nr + SUBLANE, LANE), F32), pltpu.VMEM((SUBLANE, nr, LANE), F32)],
        args=(hin, hin, hin, hin, hin, wa, wb, bb), name="mix_conv_fwd", comm=comm)


def _ln_parts(z, lg, lb):
    mu = jnp.mean(z, axis=-1, keepdims=True)
    zc = z - mu
    rstd = lax.rsqrt(jnp.mean(zc * zc, axis=-1, keepdims=True) + LN_EPS)
    zh = zc * rstd
    return zh, rstd, zh * lg + lb


def _out_proj(ya, z, lg, lb, w_out, h0, g2, g3, tm, comm):
    t_rows, d = h0.shape
    w = z.shape[1]

    def body(ya_ref, z_ref, lg_ref, lb_ref, w_ref, h0_ref, g2_ref, g3_ref, y_ref, mix_ref, h1_ref, xn2_ref):
        _, _, ln = _ln_parts(z_ref[...], lg_ref[...], lb_ref[...])
        y_ref[:, pl.ds(0, w)] = ya_ref[...]
        y_ref[:, pl.ds(w, w)] = (ln * jax.nn.sigmoid(ln)).astype(BF16)
        mix = _dot_nn(y_ref[...], w_ref[...])
        mix_ref[...] = mix
        h1 = h0_ref[...] + (mix * _rstd(mix)) * g2_ref[...]
        h1_ref[...] = h1
        xn2_ref[...] = ((h1 * _rstd(h1)) * g3_ref[...]).astype(BF16)

    blk = pl.BlockSpec((tm, d), lambda i: (i, 0))
    half = pl.BlockSpec((tm, w), lambda i: (i, 0))
    return _host_call(
        body, grid=(t_rows // tm,),
        in_specs=[half, half, _full((1, w)), _full((1, w)), _resident(w_out.shape), blk, _full((1, d)), _full((1, d))],
        out_specs=[blk, blk, blk, blk],
        out_shape=[jax.ShapeDtypeStruct((t_rows, d), BF16), jax.ShapeDtypeStruct((t_rows, d), F32),
                   jax.ShapeDtypeStruct((t_rows, d), F32), jax.ShapeDtypeStruct((t_rows, d), BF16)],
        args=(ya, z, lg, lb, w_out, h0, g2, g3), name="out_proj", comm=comm)


def _gate_up(xn2, wg_t, wu_t, tm, comm):
    t_rows, d = xn2.shape
    f = wg_t.shape[0]

    def body(x_ref, wg_ref, wu_ref, ga_ref, gu_ref, s_ref):
        xn = x_ref[...]
        for o, n in _chunks(f, N_CHUNK):
            a = _dot_nt(xn, wg_ref[pl.ds(o, n), :])
            u = _dot_nt(xn, wu_ref[pl.ds(o, n), :])
            sig = jax.nn.sigmoid(a)
            silu = a * sig
            s = silu * u
            gu_ref[:, pl.ds(o, n)] = silu.astype(BF16)
            ga_ref[:, pl.ds(o, n)] = ((u - s) * sig + s).astype(BF16)
            s_ref[:, pl.ds(o, n)] = s.astype(BF16)

    blk = pl.BlockSpec((tm, f), lambda i: (i, 0))
    return _host_call(
        body, grid=(t_rows // tm,),
        in_specs=[pl.BlockSpec((tm, d), lambda i: (i, 0)), _resident((f, d)), _resident((f, d))],
        out_specs=[blk, blk, blk], out_shape=[jax.ShapeDtypeStruct((t_rows, f), BF16)] * 3,
        args=(xn2, wg_t, wu_t), name="gate_up", comm=comm)


def _down_loss(s, wd, h1, tgt, g4, tm, x0):
    t_rows, d = h1.shape
    f = wd.shape[0]

    def body(s_ref, w_ref, h1_ref, tgt_ref, g4_ref, dh2_ref, dff_ref, dg4_ref, loss_ref):
        i = pl.program_id(0)
        ff = _dot_nn(s_ref[...], w_ref[...])
        r4 = _rstd(ff)
        fh = ff * r4
        g4 = g4_ref[...]
        h2 = h1_ref[...] + fh * g4
        row = i * tm + lax.broadcasted_iota(jnp.int32, (tm, 1), 0)
        diff = jnp.where(row >= x0, h2 - tgt_ref[...], 0.0)
        dh2 = diff / d
        dh2_ref[...] = dh2
        dff_ref[...] = _rms_bwd(dh2 * g4, fh, r4).astype(BF16)
        _acc_rows(dg4_ref, dh2 * fh, i == 0)
        _acc_rows(loss_ref, diff * diff, i == 0)

    blk = pl.BlockSpec((tm, d), lambda i: (i, 0))
    res, _ = _host_call(
        body, grid=(t_rows // tm,),
        in_specs=[pl.BlockSpec((tm, f), lambda i: (i, 0)), _resident((f, d)), blk, blk, _full((1, d))],
        out_specs=[blk, blk, _full((1, d)), _full((1, d))],
        out_shape=[jax.ShapeDtypeStruct((t_rows, d), F32), jax.ShapeDtypeStruct((t_rows, d), BF16),
                   jax.ShapeDtypeStruct((1, d), F32), jax.ShapeDtypeStruct((1, d), F32)],
        args=(s, wd, h1, tgt, g4), name="down_loss")
    return res


def _bwd_down(dff, wd, ga, gu, tm, comm):
    t_rows, d = dff.shape
    f = wd.shape[0]

    def body(dff_ref, w_ref, ga_ref, gu_ref, da_ref, du_ref):
        dff_v = dff_ref[...]
        for o, n in _chunks(f, N_CHUNK):
            ds = _dot_nt(dff_v, w_ref[pl.ds(o, n), :]).astype(BF16)
            da_ref[:, pl.ds(o, n)] = ds * ga_ref[:, pl.ds(o, n)]
            du_ref[:, pl.ds(o, n)] = ds * gu_ref[:, pl.ds(o, n)]

    blk = pl.BlockSpec((tm, f), lambda i: (i, 0))
    return _host_call(
        body, grid=(t_rows // tm,),
        in_specs=[pl.BlockSpec((tm, d), lambda i: (i, 0)), _resident((f, d)), blk, blk],
        out_specs=[blk, blk], out_shape=[jax.ShapeDtypeStruct((t_rows, f), BF16)] * 2,
        args=(dff, wd, ga, gu), name="bwd_down", comm=comm)


def _wgrad(a, b, name, after=()):
    d = b.shape[1]
    t_rows = b.shape[0]
    stacked = a.ndim == 3
    n = a.shape[-1]
    groups = a.shape[0] if stacked else 1
    steps = 1 if stacked else 2
    tile = max(t for t in range(LANE, min(n // steps, WGRAD_TILE_MAX) + 1, LANE) if n % t == 0)
    tiles = n // tile

    def body(a_ref, b_ref, o_ref):
        o_ref[...] = lax.dot_general(a_ref[...], b_ref[...], (((0,), (0,)), ((), ())),
                                     preferred_element_type=F32).astype(BF16)

    if stacked:
        a_spec = pl.BlockSpec((None, t_rows, tile), lambda g, i: (g, 0, i))
    else:
        a_spec = pl.BlockSpec((t_rows, tile), lambda g, i: (0, i))
    res, _ = _host_call(
        body, grid=(groups, tiles), in_specs=[a_spec, _resident((t_rows, d))],
        out_specs=[pl.BlockSpec((tile, d), lambda g, i: (g * tiles + i, 0))],
        out_shape=[jax.ShapeDtypeStruct((groups * n, d), BF16)], args=(a, b), name=name, after=after)
    return res[0]


def _bwd_ffn_in(da, du, wg_t, wu_t, h1, dh2, g3, tm, comm):
    t_rows, d = h1.shape
    f = wg_t.shape[0]

    def body(da_ref, du_ref, wg_ref, wu_ref, h1_ref, dh2_ref, g3_ref, dh1_ref, dg3_ref):
        dxn2 = _dot_nn(da_ref[...], wg_ref[...]) + _dot_nn(du_ref[...], wu_ref[...])
        h1 = h1_ref[...]
        r3 = _rstd(h1)
        h1h = h1 * r3
        _acc_rows(dg3_ref, dxn2 * h1h, pl.program_id(0) == 0)
        dh1_ref[...] = dh2_ref[...] + _rms_bwd(dxn2 * g3_ref[...], h1h, r3)

    blk = pl.BlockSpec((tm, d), lambda i: (i, 0))
    blkf = pl.BlockSpec((tm, f), lambda i: (i, 0))
    return _host_call(
        body, grid=(t_rows // tm,),
        in_specs=[blkf, blkf, _resident((f, d)), _resident((f, d)), blk, blk, _full((1, d))],
        out_specs=[blk, _full((1, d))],
        out_shape=[jax.ShapeDtypeStruct((t_rows, d), F32), jax.ShapeDtypeStruct((1, d), F32)],
        args=(da, du, wg_t, wu_t, h1, dh2, g3), name="bwd_ffn_in", comm=comm)


def _bwd_out_proj(dh1, mix, w_out, g2, z, lg, lb, tm, after):
    t_rows, d = dh1.shape
    w = z.shape[1]

    def body(dh1_ref, mix_ref, w_ref, g2_ref, z_ref, lg_ref, lb_ref, dmix_ref, dya_ref, dz_ref, dg2_ref, dlg_ref, dlb_ref, dbb_ref):
        first = pl.program_id(0) == 0
        mix = mix_ref[...]
        r2 = _rstd(mix)
        mh = mix * r2
        dh1 = dh1_ref[...]
        _acc_rows(dg2_ref, dh1 * mh, first)
        dmix = _rms_bwd(dh1 * g2_ref[...], mh, r2).astype(BF16)
        dmix_ref[...] = dmix
        dy = _dot_nt(dmix, w_ref[...])
        dya_ref[...] = dy[:, :w]
        lg = lg_ref[...]
        zh, rstd, ln = _ln_parts(z_ref[...], lg, lb_ref[...])
        dln = dy[:, w:] * _silu_grad(ln, jax.nn.sigmoid(ln))
        _acc_rows(dlg_ref, dln * zh, first)
        _acc_rows(dlb_ref, dln, first)
        dzh = dln * lg
        dz = rstd * (dzh - jnp.mean(dzh, axis=-1, keepdims=True) - zh * jnp.mean(dzh * zh, axis=-1, keepdims=True))
        dz_ref[...] = dz
        _acc_rows(dbb_ref, dz, first)

    blk = pl.BlockSpec((tm, d), lambda i: (i, 0))
    half = pl.BlockSpec((tm, w), lambda i: (i, 0))
    vec = _full((1, w))
    res, _ = _host_call(
        body, grid=(t_rows // tm,), in_specs=[blk, blk, _resident(w_out.shape), _full((1, d)), half, vec, vec],
        out_specs=[blk, half, half, _full((1, d)), vec, vec, vec],
        out_shape=[jax.ShapeDtypeStruct((t_rows, d), BF16), jax.ShapeDtypeStruct((t_rows, w), F32),
                   jax.ShapeDtypeStruct((t_rows, w), F32), jax.ShapeDtypeStruct((1, d), F32)]
        + [jax.ShapeDtypeStruct((1, w), F32)] * 3,
        args=(dh1, mix, w_out, g2, z, lg, lb), name="bwd_out_proj", after=after)
    return res


def _mix_conv_bwd(hin, dy, dz, wa, wb, wa_w, comm):
    t_rows = hin.shape[0]
    nt = wa_w // LANE
    ka, kb = wa.shape[0], wb.shape[0]
    nr = CONV_HALO + t_rows
    kb_rows = -(-kb // SUBLANE) * SUBLANE

    def body(bg_ref, cg_ref, ha_ref, val_ref, gt_ref, dya_ref, dz_ref, wa_ref, wb_ref,
             dh_ref, dwa_ref, dwb_ref, base, sh, based, shd, tmp, wbc):
        zeros = lambda n: jnp.zeros((n, LANE), F32)
        base[pl.ds(0, CONV_HALO), :] = zeros(CONV_HALO)
        base[pl.ds(nr, SUBLANE), :] = zeros(SUBLANE)
        based[pl.ds(t_rows, CONV_HALO + SUBLANE), :] = zeros(CONV_HALO + SUBLANE)

        def fwd_slot(k_taps, k):
            return _tap_slot(CONV_HALO - (k_taps - 1) + k)

        def bwd_slot(k_taps, k):
            return _tap_slot(k_taps - 1 - k)

        def conv(w_ref, k_taps, src, slot, b, n):
            acc = None
            for k in range(k_taps):
                r, q = slot(k_taps, k)
                term = w_ref[pl.ds(k, 1), :] * _shifted_rows(src, r, b + q, n)
                acc = term if acc is None else acc + term
            return acc

        def by_residue(k_taps, slot):
            groups = {}
            for k in range(k_taps):
                r, q = slot(k_taps, k)
                groups.setdefault(r, []).append((k, q // SUBLANE))
            return groups

        def wgrad_loop(w_ref, k_taps):
            n_sub = WGRAD_ROWS // SUBLANE
            for k in range(k_taps):
                wbc[k] = jnp.broadcast_to(w_ref[pl.ds(k, 1), :], (SUBLANE, LANE))
            fwd, bwd = by_residue(k_taps, fwd_slot), by_residue(k_taps, bwd_slot)

            def window(src, r, taps, b):
                span = n_sub + max(qi for _, qi in taps)
                return [_shifted_rows(src, r, b + SUBLANE * i, SUBLANE) for i in range(span)]

            def step(b, accs):
                accs = list(accs)
                dv = [based[pl.ds(b + SUBLANE * j, SUBLANE), :] for j in range(n_sub)]
                for r, taps in fwd.items():
                    win = window((base, sh), r, taps, b)
                    for k, qi in taps:
                        t = dv[0] * win[qi]
                        for j in range(1, n_sub):
                            t = t + dv[j] * win[qi + j]
                        accs[k] = accs[k] + t
                outs = [None] * n_sub
                for r, taps in bwd.items():
                    win = window((based, shd), r, taps, b)
                    for k, qi in taps:
                        wk = wbc[k]
                        for j in range(n_sub):
                            term = wk * win[qi + j]
                            outs[j] = term if outs[j] is None else outs[j] + term
                for j in range(n_sub):
                    tmp[pl.ds(b + SUBLANE * j, SUBLANE), :] = outs[j]
                return tuple(accs)

            return _row_loop(t_rows, WGRAD_ROWS, step, tuple(zeros(SUBLANE) for _ in range(k_taps)))

        def store_taps(ref, accs, rows):
            for k, acc in enumerate(accs):
                ref[pl.ds(k, 1), :] = jnp.sum(acc, axis=0, keepdims=True)
            if rows > len(accs):
                ref[pl.ds(len(accs), rows - len(accs)), :] = zeros(rows - len(accs))

        def fill_a(b, c):
            sl = pl.ds(b, CONV_CHUNK)
            base[pl.ds(CONV_HALO + b, CONV_CHUNK), :] = cg_ref[sl, :] * ha_ref[sl, :]
            based[sl, :] = dya_ref[sl, :] * bg_ref[sl, :]
            return c

        _row_loop(t_rows, CONV_CHUNK, fill_a)
        _fill_shifted(sh, base, sorted({fwd_slot(ka, k)[0] for k in range(ka)}), nr)
        _fill_shifted(shd, based, sorted({bwd_slot(ka, k)[0] for k in range(ka)}), nr)

        def d_bgate(b, c):
            sl = pl.ds(b, CONV_CHUNK)
            dh_ref[0, sl, :] = (dya_ref[sl, :] * conv(wa_ref, ka, (base, sh), fwd_slot, b, CONV_CHUNK)).astype(BF16)
            return c

        _row_loop(t_rows, CONV_CHUNK, d_bgate)
        store_taps(dwa_ref, wgrad_loop(wa_ref, ka), SUBLANE)

        def d_ch(b, c):
            sl = pl.ds(b, CONV_CHUNK)
            dua = tmp[sl, :]
            dh_ref[1, sl, :] = (dua * ha_ref[sl, :]).astype(BF16)
            dh_ref[2, sl, :] = (dua * cg_ref[sl, :]).astype(BF16)
            return c

        _row_loop(t_rows, CONV_CHUNK, d_ch)

        def fill_b(b, c):
            sl = pl.ds(b, CONV_CHUNK)
            base[pl.ds(CONV_HALO + b, CONV_CHUNK), :] = val_ref[sl, :] * jax.nn.sigmoid(gt_ref[sl, :])
            based[sl, :] = dz_ref[sl, :]
            return c

        _row_loop(t_rows, CONV_CHUNK, fill_b)
        _fill_shifted(sh, base, range(SUBLANE), nr)
        _fill_shifted(shd, based, range(SUBLANE), nr)
        store_taps(dwb_ref, wgrad_loop(wb_ref, kb), kb_rows)

        def d_glu(b, c):
            sl = pl.ds(b, CONV_CHUNK)
            dgg = tmp[sl, :]
            sig = jax.nn.sigmoid(gt_ref[sl, :])
            dh_ref[3, sl, :] = (dgg * sig).astype(BF16)
            dh_ref[4, sl, :] = (dgg * val_ref[sl, :] * (sig * (1.0 - sig))).astype(BF16)
            return c

        _row_loop(t_rows, CONV_CHUNK, d_glu)

    def col(g):
        return pl.BlockSpec((t_rows, LANE), lambda i, g=g: (0, g * nt + i))

    tile = lambda rows: pl.BlockSpec((rows, LANE), lambda i: (0, i))
    return _host_call(
        body, grid=(nt,),
        in_specs=[col(0), col(1), col(2), col(3), col(4), tile(t_rows), tile(t_rows), tile(ka), tile(kb)],
        out_specs=[pl.BlockSpec((5, t_rows, LANE), lambda i: (0, 0, i)), tile(SUBLANE), tile(kb_rows)],
        out_shape=[jax.ShapeDtypeStruct((5, t_rows, wa_w), BF16), jax.ShapeDtypeStruct((SUBLANE, wa_w), F32),
                   jax.ShapeDtypeStruct((kb_rows, wa_w), F32)],
        scratch_shapes=[pltpu.VMEM((nr + SUBLANE, LANE), F32), pltpu.VMEM((SUBLANE, nr, LANE), F32),
                        pltpu.VMEM((nr + SUBLANE, LANE), F32), pltpu.VMEM((SUBLANE, nr, LANE), F32),
                        pltpu.VMEM((t_rows, LANE), F32), pltpu.VMEM((kb_rows, SUBLANE, LANE), F32)],
        args=(hin, hin, hin, hin, hin, dy, dz, wa, wb), name="mix_conv_bwd", comm=comm)


def _bwd_in_proj(dh5, win_t, h0, dh1, g1, tm, comm):
    t_rows, d = h0.shape
    groups, _, w = dh5.shape

    def body(dh_ref, w_ref, h0_ref, dh1_ref, g1_ref, dh0_ref, dg1_ref):
        dxn1 = None
        for g in range(groups):
            part = _dot_nn(dh_ref[g], w_ref[pl.ds(g * w, w), :])
            dxn1 = part if dxn1 is None else dxn1 + part
        h0 = h0_ref[...]
        r1 = _rstd(h0)
        h0h = h0 * r1
        _acc_rows(dg1_ref, dxn1 * h0h, pl.program_id(0) == 0)
        dh0_ref[...] = dh1_ref[...] + _rms_bwd(dxn1 * g1_ref[...], h0h, r1)

    blk = pl.BlockSpec((tm, d), lambda i: (i, 0))
    return _host_call(
        body, grid=(t_rows // tm,),
        in_specs=[pl.BlockSpec((groups, tm, w), lambda i: (0, i, 0)), _resident(win_t.shape), blk, blk, _full((1, d))],
        out_specs=[blk, _full((1, d))],
        out_shape=[jax.ShapeDtypeStruct((t_rows, d), F32), jax.ShapeDtypeStruct((1, d), F32)],
        args=(dh5, win_t, h0, dh1, g1), name="bwd_in_proj", comm=comm)


def _pair_small(smalls, d):
    (dmeta, dg1, dg2, dg3, dg4, dbb, dlg, dlb, lossv, dwa, dwb) = smalls
    half = d // 2
    kb_rows = dwb.shape[0]

    def body(dmeta_ref, dg1_ref, dg2_ref, dg3_ref, dg4_ref, dbb_ref, dlg_ref, dlb_ref, loss_ref, dwa_ref, dwb_ref,
             sums_ref, pbuf, psib, ps_send, ps_recv):
        x, y, c = _mesh_pos()
        _pair_handshake()
        pbuf[...] = jnp.zeros_like(pbuf)
        pbuf[pl.ds(0, N_META), :] = dmeta_ref[...]
        for row, ref in ((16, dg1_ref), (17, dg2_ref), (18, dg3_ref), (19, dg4_ref)):
            pbuf[pl.ds(row, 1), :] = ref[...]
        pbuf[pl.ds(20, 1), pl.ds(0, half)] = dbb_ref[...]
        pbuf[pl.ds(20, 1), pl.ds(half, half)] = dlg_ref[...]
        pbuf[pl.ds(21, 1), pl.ds(0, half)] = dlb_ref[...]
        lv = loss_ref[...]
        pbuf[pl.ds(21, 1), pl.ds(half, half)] = lv[:, :half] + lv[:, half:]
        pbuf[pl.ds(24, SUBLANE), pl.ds(0, half)] = dwa_ref[...]
        pbuf[pl.ds(32, kb_rows), pl.ds(0, half)] = dwb_ref[...]
        to_sib = _remote(pbuf, psib, ps_send.at[0], ps_recv.at[0], (x, y, 1 - c))
        to_sib.start()
        to_sib.wait_recv()
        s = pbuf[...] + psib[...]
        for k in range(3):
            sums_ref[k] = s
        to_sib.wait_send()

    return pl.pallas_call(
        body, out_shape=jax.ShapeDtypeStruct((3, SMALL_ROWS, d), F32), in_specs=[VMEM] * 11, out_specs=VMEM,
        scratch_shapes=[pltpu.VMEM((SMALL_ROWS, d), F32), pltpu.VMEM((SMALL_ROWS, d), F32),
                        pltpu.SemaphoreType.DMA((1,)), pltpu.SemaphoreType.DMA((1,))],
        name="pair_small", compiler_params=pltpu.CompilerParams(vmem_limit_bytes=VMEM_LIMIT, collective_id=PAIR_BARRIER_ID))(*smalls)


def _total_small(own, others):
    _, r, d = own.shape

    def body(own_ref, others_ref, tot_ref, chip_p):
        x, y, _ = _mesh_pos()
        chip_p[2 * x + y] = own_ref[0]
        for k, (cx, cy) in enumerate(_other_chips(x, y)):
            chip_p[2 * cx + cy] = others_ref[k]
        tot_ref[...] = ((chip_p[0] + chip_p[1]) + chip_p[2]) + chip_p[3]

    return pl.pallas_call(body, out_shape=jax.ShapeDtypeStruct((r, d), F32), scratch_shapes=[pltpu.VMEM((4, r, d), F32)],
                          name="total_small", compiler_params=_cparams())(own, others)


def _adamw(w, g, m, v):
    m = ADAM_B1 * m + (1.0 - ADAM_B1) * g
    v = ADAM_B2 * v + (1.0 - ADAM_B2) * jnp.square(g)
    m_hat = m / (1.0 - ADAM_B1 ** ADAM_STEP)
    v_hat = v / (1.0 - ADAM_B2 ** ADAM_STEP)
    delta = -ADAM_LR * (m_hat / (jnp.sqrt(v_hat) + ADAM_EPS) + ADAM_WD * w)
    return delta, m, v


def _adam_big(g, pair, part, w, m, v, name):
    r, d = w.shape
    cols = d // ADAM_COL_BLOCKS

    def body(me_ref, g_ref, pair_ref, part_ref, w_ref, m_ref, v_ref, go_ref, d_ref, mo_ref, vo_ref):
        g = g_ref[...].astype(F32) + pair_ref[...].astype(F32)
        for k in range(3):
            g = g + part_ref[k].astype(F32)
        go_ref[...] = g
        d_ref[...], mo_ref[...], vo_ref[...] = _adamw(w_ref[...], g, m_ref[...], v_ref[...])

    blk = pl.BlockSpec((r, cols), lambda i, me_ref: (0, i))
    grid_spec = pltpu.PrefetchScalarGridSpec(
        num_scalar_prefetch=1, grid=(ADAM_COL_BLOCKS,),
        in_specs=[pl.BlockSpec((r, cols), lambda i, me_ref: (me_ref[0], i)),
                  pl.BlockSpec((None, r, cols), lambda i, me_ref: (0, 0, i)),
                  pl.BlockSpec((3, r, cols), lambda i, me_ref: (0, 0, i)), blk, blk, blk],
        out_specs=[blk, blk, blk, blk])
    me = jnp.reshape(_dev_index(*_mesh_pos()), (1,)).astype(jnp.int32)
    return pl.pallas_call(body, out_shape=[jax.ShapeDtypeStruct((r, d), F32)] * 4, grid_spec=grid_spec, name=name,
                          compiler_params=_cparams(1))(me, g, pair, part, w, m, v)


def _adam_small(gs, ws, ms, vs):
    n = len(gs)

    def body(*refs):
        ins, outs = refs[:4 * n], refs[4 * n:]
        for i in range(n):
            g = ins[i][...]
            delta, m, v = _adamw(ins[n + i][...], g, ins[2 * n + i][...], ins[3 * n + i][...])
            outs[i][...] = delta
            outs[n + i][...] = m
            outs[2 * n + i][...] = v

    shapes = [jax.ShapeDtypeStruct(w.shape, F32) for w in ws]
    return pl.pallas_call(body, out_shape=shapes * 3, name="adam_small", compiler_params=_cparams())(*gs, *ws, *ms, *vs)


def kernel(x, meta_tokens, pre_mix_norm, w_in, conv_a_w, conv_b_w, conv_b_bias, ln_b_gain, ln_b_bias, w_out, post_mix_norm, pre_ffn_norm, w_gate, w_up, w_down, post_ffn_norm, loss_target, m_meta_tokens, m_pre_mix_norm, m_w_in, m_conv_a_w, m_conv_b_w, m_conv_b_bias, m_ln_b_gain, m_ln_b_bias, m_w_out, m_post_mix_norm, m_pre_ffn_norm, m_w_gate, m_w_up, m_w_down, m_post_ffn_norm, v_meta_tokens, v_pre_mix_norm, v_w_in, v_conv_a_w, v_conv_b_w, v_conv_b_bias, v_ln_b_gain, v_ln_b_bias, v_w_out, v_post_mix_norm, v_pre_ffn_norm, v_w_gate, v_w_up, v_w_down, v_post_ffn_norm):
    _, seq, d = x.shape
    ka, ca_loc = conv_a_w.shape[1:]
    kb, cb_loc = conv_b_w.shape[1:]
    wa_w = ca_loc * N_DEV
    assert cb_loc == ca_loc and wa_w % LANE == 0 and w_in.shape[2] * N_DEV == 5 * wa_w and 2 * wa_w == d
    pad = (-(N_META + seq)) % ROW_ALIGN
    x0 = pad + N_META
    t_rows = x0 + seq
    assert t_rows % (N_ROW_BLOCKS * BF16_ROWS) == 0 and t_rows % CONV_CHUNK == 0 and d % LANE == 0
    tm = t_rows // N_ROW_BLOCKS
    tm2 = t_rows // 2
    me = _dev_index(*_mesh_pos())

    def as_rows(w_in_like, w_out_like, w_gate_like, w_up_like, w_down_like):
        return (w_in_like[0].T, w_out_like[0], w_gate_like[0].T, w_up_like[0].T, w_down_like[0])

    w_loc = as_rows(w_in, w_out, w_gate, w_up, w_down)
    rows = [w.shape[0] for w in w_loc]
    assert all(r % ADD_CHUNK == 0 for r in rows)
    P_IN, P_OUT, P_GATE, P_UP, P_DOWN = range(N_BIG)

    sm = jnp.zeros((SM_ROWS, LANE), F32)
    sm = sm.at[0:N_META, :].set(meta_tokens)
    sm = sm.at[16:16 + ka, 0:ca_loc].set(conv_a_w[0])
    sm = sm.at[24:24 + kb, 0:cb_loc].set(conv_b_w[0])
    wl, wfull, sm_all, h0, tgt = _gather_first(w_loc, sm, [(P_IN, 0, rows[P_IN])], x[0], loss_target[0], t_rows, x0)
    wa =jnp.transpose(sm_all[:, 16:16 + ka, 0:ca_loc], (1, 0, 2)).reshape(ka, wa_w)
    wb = jnp.transpose(sm_all[:, 24:24 + kb, 0:cb_loc], (1, 0, 2)).reshape(kb, wa_w)

    later = (P_OUT, P_GATE, P_UP, P_DOWN)
    sems, wl, started, _ = _gather_start(wl, wfull, later, rows, START_BARRIER_IDS[0])
    for p, arr in zip(later, started):
        wfull[p] = arr

    def arrived(p, after, name):
        nonlocal wl
        wl, wfull[p] = _gather_wait(wl, wfull[p], sems[later.index(p)], after, rows[p], name)
        return _forward_comm(wfull[p], rows[p])

    (xn1, hin), _ = _in_proj(h0, pre_mix_norm, wfull[P_IN], tm, None)
    (ya, z), (wfull[P_OUT],) = _mix_conv_fwd(hin, wa, wb, conv_b_bias, wa_w, arrived(P_OUT, hin, "gather_wait_out"))
    (y, mix, h1, xn2), (wfull[P_GATE],) = _out_proj(ya, z, ln_b_gain, ln_b_bias, wfull[P_OUT], h0, post_mix_norm, pre_ffn_norm, tm2,
                                                    arrived(P_GATE, z, "gather_wait_gate"))
    arrived(P_UP, xn2, "gather_wait_up")
    wfull[P_UP] = _forward_now(wfull[P_UP], rows[P_UP], "forward_up")
    (ga, gu, s), _ = _gate_up(xn2, wfull[P_GATE], wfull[P_UP], tm, None)
    arrived(P_DOWN, s, "gather_wait_down")
    wfull[P_DOWN] = _forward_now(wfull[P_DOWN], rows[P_DOWN], "forward_down")
    dh2, dff, dg4, lossv = _down_loss(s, wfull[P_DOWN], h1, tgt, post_ffn_norm, tm, x0)

    gwd = _wgrad(s, dff, "wgrad_down")
    (da, du), (pair_d,) = _bwd_down(dff, wfull[P_DOWN], ga, gu, tm, _pair_comm(gwd, rows[P_DOWN]))
    (flight_d,), token = _chip_start([_pair_sum(gwd, pair_d, rows[P_DOWN], "pair_sum_down")], "chip_start_down", START_BARRIER_IDS[1])
    gwg = _wgrad(da, xn2, "wgrad_gate", [token])
    gwu = _wgrad(du, xn2, "wgrad_up")
    (dh1, dg3), (pair_g, pair_u) = _bwd_ffn_in(da, du, wfull[P_GATE], wfull[P_UP], h1, dh2, pre_ffn_norm, tm,
                                               _merge_comms([_pair_comm(gwg, rows[P_GATE]), _pair_comm(gwu, rows[P_UP])]))
    (flight_g, flight_u), token = _chip_start([_pair_sum(gwg, pair_g, rows[P_GATE], "pair_sum_gate"),
                                               _pair_sum(gwu, pair_u, rows[P_UP], "pair_sum_up")], "chip_start_gate_up",
                                              START_BARRIER_IDS[2])
    dmix, dya, dz, dg2, dlg, dlb, dbb = _bwd_out_proj(dh1, mix, wfull[P_OUT], post_mix_norm, z, ln_b_gain, ln_b_bias, tm, [token])
    gwo = _wgrad(y, dmix, "wgrad_out")
    (dh5, dwa, dwb), (pair_o,) = _mix_conv_bwd(hin, dya, dz, wa, wb, wa_w, _pair_comm(gwo, rows[P_OUT]))
    (flight_o,), token = _chip_start([_pair_sum(gwo, pair_o, rows[P_OUT], "pair_sum_out")], "chip_start_out", START_BARRIER_IDS[3])
    gwi = _wgrad(dh5, xn1, "wgrad_in", [token])
    (dh0, dg1), (pair_i,) = _bwd_in_proj(dh5, wfull[P_IN], h0, dh1, pre_mix_norm, tm, _pair_comm(gwi, rows[P_IN]))
    grad_x = dh0[x0:][None]
    dmeta = dh0[x0 - N_META:x0]
    small_sums = _pair_small((dmeta, dg1, dg2, dg3, dg4, dbb, dlg, dlb, lossv, dwa, dwb), d)
    (flight_s,), token = _chip_start([small_sums], "chip_start_small", START_BARRIER_IDS[4])
    (flight_i,), token = _chip_start([_pair_sum(gwi, pair_i, rows[P_IN], "pair_sum_in", [token])], "chip_start_in",
                                     START_BARRIER_IDS[5])

    def landed(flight, after, tag):
        sems_p, sums, land = flight
        return _chip_wait(sums, land, sems_p, after, "chip_wait_" + tag)

    m_loc = as_rows(m_w_in, m_w_out, m_w_gate, m_w_up, m_w_down)
    v_loc = as_rows(v_w_in, v_w_out, v_w_gate, v_w_up, v_w_down)
    full_grads = {P_IN: gwi, P_OUT: gwo, P_GATE: gwg, P_UP: gwu, P_DOWN: gwd}
    pairs = {P_IN: pair_i, P_OUT: pair_o, P_GATE: pair_g, P_UP: pair_u, P_DOWN: pair_d}
    flights = {P_IN: flight_i, P_OUT: flight_o, P_GATE: flight_g, P_UP: flight_u, P_DOWN: flight_d}
    names = {P_IN: "w_in", P_OUT: "w_out", P_GATE: "w_gate", P_UP: "w_up", P_DOWN: "w_down"}
    bigs = {}

    def adam_big(p, after):
        part = landed(flights[p], after, names[p])
        res = _adam_big(full_grads[p], pairs[p], part, w_loc[p], m_loc[p], v_loc[p], "adam_" + names[p])
        bigs[names[p]] = [(o.T if p in (P_IN, P_GATE, P_UP) else o)[None] for o in res]
        return res[1]

    for p in (P_DOWN, P_GATE, P_UP, P_OUT):
        token = adam_big(p, token)
    ptot = _total_small(flight_s[1], landed(flight_s, token, "small"))
    half = d // 2
    loss = (0.5 / d) * jnp.sum(ptot[21, half:])
    g_meta = lax.dynamic_slice(ptot, (0, me * (d // N_DEV)), (N_META, d // N_DEV))
    g_small = [g_meta, ptot[16:17], lax.dynamic_slice(ptot, (24, me * ca_loc), (ka, ca_loc))[None],
               lax.dynamic_slice(ptot, (32, me * cb_loc), (kb, cb_loc))[None],
               ptot[20:21, :half], ptot[20:21, half:], ptot[21:22, :half], ptot[17:18], ptot[18:19], ptot[19:20]]
    w_small = [meta_tokens, pre_mix_norm, conv_a_w, conv_b_w, conv_b_bias, ln_b_gain, ln_b_bias, post_mix_norm,
               pre_ffn_norm, post_ffn_norm]
    m_small = [m_meta_tokens, m_pre_mix_norm, m_conv_a_w, m_conv_b_w, m_conv_b_bias, m_ln_b_gain, m_ln_b_bias,
               m_post_mix_norm, m_pre_ffn_norm, m_post_ffn_norm]
    v_small = [v_meta_tokens, v_pre_mix_norm, v_conv_a_w, v_conv_b_w, v_conv_b_bias, v_ln_b_gain, v_ln_b_bias,
               v_post_mix_norm, v_pre_ffn_norm, v_post_ffn_norm]
    small = _adam_small(g_small, w_small, m_small, v_small)
    n_small = len(w_small)
    d_small, nm_small, nv_small = small[:n_small], small[n_small:2 * n_small], small[2 * n_small:]

    adam_big(P_IN, small[0])

    def ordered(pick_small, pick_big):
        sm_it = iter(range(n_small))
        out = []
        for name in ("s", "s", "w_in", "s", "s", "s", "s", "s", "w_out", "s", "s", "w_gate", "w_up", "w_down", "s"):
            out.append(pick_small(next(sm_it)) if name == "s" else pick_big(name))
        return out

    grads = ordered(lambda i: g_small[i], lambda n: bigs[n][0])
    deltas = ordered(lambda i: d_small[i], lambda n: bigs[n][1])
    new_m = ordered(lambda i: nm_small[i], lambda n: bigs[n][2])
    new_v = ordered(lambda i: nv_small[i], lambda n: bigs[n][3])
    return (loss, grad_x, *grads, *deltas, *new_m, *new_v)
```

```python
import jax
import jax.numpy as jnp
from jax import lax
from jax.experimental import pallas as pl
from jax.experimental.pallas import tpu as pltpu

F32 = jnp.float32
BF16 = jnp.bfloat16
MESH = pl.DeviceIdType.MESH

N_META = 16
N_DEV = 8
RMS_EPS = 1e-6
LN_EPS = 1e-5
ADAM_LR = 0.001
ADAM_B1 = 0.9
ADAM_B2 = 0.999
ADAM_EPS = 1e-08
ADAM_WD = 0.01
ADAM_STEP = 10

LANE = 128
SUBLANE = 8
BF16_ROWS = 16
ROW_ALIGN = 128
N_ROW_BLOCKS = 4
CONV_HALO = 32
CONV_CHUNK = 64
WGRAD_ROWS = 32
N_CHUNK = 512
WGRAD_TILE_MAX = 1408
ADD_CHUNK = 32
ADAM_COL_BLOCKS = 2
COPY_PIECES = 4
V7X_VMEM_BYTES = 64 * 1024 * 1024
VMEM_LIMIT = V7X_VMEM_BYTES - 6 * 1024 * 1024
SMALL_ROWS = 64
SM_ROWS = 56
N_BIG = 5

ANY = pl.BlockSpec(memory_space=pl.ANY)
VMEM = pl.BlockSpec(memory_space=pltpu.VMEM)


def _cparams(n_grid_axes=0):
    sem = ("arbitrary",) * n_grid_axes if n_grid_axes else None
    return pltpu.CompilerParams(dimension_semantics=sem, vmem_limit_bytes=VMEM_LIMIT)


def _mesh_pos():
    return lax.axis_index("x"), lax.axis_index("y"), lax.axis_index("c")


def _dev_index(px, py, pc):
    return 4 * px + 2 * py + pc


def _other_chips(x, y):
    return [(1 - x, y), (x, 1 - y), (1 - x, 1 - y)]


def _full(shape):
    return pl.BlockSpec(shape, lambda *_: (0,) * len(shape))


def _resident(shape):
    return pl.BlockSpec(shape, lambda *_: (0,) * len(shape), pipeline_mode=pl.Buffered(1))


def _dot_nt(a, w):
    return lax.dot_general(a, w, (((1,), (1,)), ((), ())), preferred_element_type=F32)


def _dot_nn(a, w):
    return jnp.dot(a, w, preferred_element_type=F32)


def _chunks(n, c):
    out, o = [], 0
    while o < n:
        out.append((o, min(c, n - o)))
        o += c
    return out


def _rstd(h):
    return lax.rsqrt(jnp.mean(h * h, axis=-1, keepdims=True) + RMS_EPS)


def _rms_bwd(dyh, yh, r):
    return r * (dyh - yh * jnp.mean(dyh * yh, axis=-1, keepdims=True))


def _silu_grad(a, sig):
    return sig * (1.0 + a * (1.0 - sig))


def _acc_rows(ref, val, first):
    s = jnp.sum(val, axis=0, keepdims=True)

    @pl.when(first)
    def _():
        ref[...] = s

    @pl.when(jnp.logical_not(first))
    def _():
        ref[...] += s


def _row_loop(t_rows, chunk, fn, carry=None):
    def step(i, c):
        return fn(pl.multiple_of(i * chunk, chunk), c)

    return lax.fori_loop(0, t_rows // chunk, step, carry)


def _remote(src, dst, send_sem, recv_sem, to):
    return pltpu.make_async_remote_copy(src_ref=src, dst_ref=dst, send_sem=send_sem, recv_sem=recv_sem,
                                        device_id=to, device_id_type=MESH)


class _Comm:
    def __init__(self, inputs, out_shapes, aliases, scratch, start, finish):
        self.inputs, self.out_shapes, self.aliases, self.scratch = list(inputs), list(out_shapes), dict(aliases), list(scratch)
        self.start, self.finish = start, finish


def _merge_comms(comms):
    inputs, out_shapes, aliases, scratch, spans = [], [], {}, [], []
    for cm in comms:
        spans.append((len(inputs), len(out_shapes), len(scratch), cm))
        aliases.update({len(inputs) + k: len(out_shapes) + v for k, v in cm.aliases.items()})
        inputs += cm.inputs
        out_shapes += cm.out_shapes
        scratch += cm.scratch

    def run(which):
        def fn(ins, outs, scr):
            for i0, o0, s0, cm in spans:
                getattr(cm, which)(ins[i0:i0 + len(cm.inputs)], outs[o0:o0 + len(cm.out_shapes)], scr[s0:s0 + len(cm.scratch)])
        return fn

    return _Comm(inputs, out_shapes, aliases, scratch, run("start"), run("finish"))


def _host_call(body, *, grid, in_specs, out_specs, out_shape, args, name, scratch_shapes=(), comm=None, after=()):
    talks = comm is not None
    if comm is None:
        comm = _Comm([], [], {}, [], lambda *_: None, lambda *_: None)
    n_in, n_out, n_scr = len(args), len(out_shape), len(scratch_shapes)
    c_in, c_out = len(comm.inputs), len(comm.out_shapes)
    n_after = len(after)

    def open_comm(c_ins, c_outs, c_scr):
        if talks:
            _pair_handshake()
        comm.start(c_ins, c_outs, c_scr)

    def hosted(*refs):
        ins, c_ins = refs[:n_in], refs[n_in:n_in + c_in]
        o0 = n_in + c_in + n_after
        outs, c_outs = refs[o0:o0 + n_out], refs[o0 + n_out:o0 + n_out + c_out]
        s0 = o0 + n_out + c_out
        scr, c_scr = refs[s0:s0 + n_scr], refs[s0 + n_scr:]
        if not grid:
            open_comm(c_ins, c_outs, c_scr)
            body(*ins, *outs, *scr)
            comm.finish(c_ins, c_outs, c_scr)
            return
        first = last = None
        for a, n in enumerate(grid):
            f, l = pl.program_id(a) == 0, pl.program_id(a) == n - 1
            first = f if first is None else jnp.logical_and(first, f)
            last = l if last is None else jnp.logical_and(last, l)

        @pl.when(first)
        def _():
            open_comm(c_ins, c_outs, c_scr)

        body(*ins, *outs, *scr)

        @pl.when(last)
        def _():
            comm.finish(c_ins, c_outs, c_scr)

    sem = ("arbitrary",) * len(grid) if grid else None
    params = pltpu.CompilerParams(dimension_semantics=sem, vmem_limit_bytes=VMEM_LIMIT,
                                  collective_id=PAIR_BARRIER_ID if talks else None)
    res = pl.pallas_call(
        hosted, grid=grid, in_specs=list(in_specs) + [ANY] * (c_in + n_after), out_specs=list(out_specs) + [ANY] * c_out,
        out_shape=list(out_shape) + comm.out_shapes, scratch_shapes=list(scratch_shapes) + comm.scratch,
        input_output_aliases={n_in + k: n_out + v for k, v in comm.aliases.items()},
        name=name, compiler_params=params)(*args, *comm.inputs, *after)
    return list(res[:n_out]), list(res[n_out:])


PAIR_BARRIER_ID = 0
START_BARRIER_IDS = (1, 2, 3, 4, 5, 6)


def _chips_handshake():
    x, y, c = _mesh_pos()
    barrier = pltpu.get_barrier_semaphore()
    for chip in _other_chips(x, y):
        pl.semaphore_signal(barrier, inc=1, device_id=(*chip, c), device_id_type=MESH)
    pl.semaphore_wait(barrier, 3)


def _pair_handshake():
    x, y, c = _mesh_pos()
    barrier = pltpu.get_barrier_semaphore()
    pl.semaphore_signal(barrier, inc=1, device_id=(x, y, 1 - c), device_id_type=MESH)
    pl.semaphore_wait(barrier, 1)


GATHER_SEMS = 10


class _Gather:
    def __init__(self, jobs, rows, lo, src_ref, dests, send_sems, recv_sems):
        x, y, c = _mesh_pos()
        me, sib = (x, y, c), (x, y, 1 - c)
        nx, ny, dg = (1 - x, y, c), (x, 1 - y, c), (1 - x, 1 - y, c)
        self.relayed, self.direct, self.relay, self.to_sib, self.sib_fwd = [], [], [], [], []
        for n, (p, r0, nr) in enumerate(jobs):
            assert nr % (2 * BF16_ROWS) == 0
            half = nr // 2

            def rows_of(dev, h, p=p, r0=r0, nr=nr, half=half):
                off, cnt = (r0, nr) if h is None else (r0 + h * half, half)
                return dests[p].at[pl.ds(pl.multiple_of(_dev_index(*dev) * rows[p] + off, BF16_ROWS), cnt), :]

            def mine(h, p=p, r0=r0, nr=nr, half=half):
                off, cnt = (r0, nr) if h is None else (r0 + h * half, half)
                return src_ref.at[pl.ds(lo[p] + off, cnt), :]

            sem = lambda k, n=n: (send_sems.at[GATHER_SEMS * n + k], recv_sems.at[GATHER_SEMS * n + k])
            self.relayed.append([_remote(mine(0), rows_of(me, 0), *sem(0), nx), _remote(mine(1), rows_of(me, 1), *sem(3), ny)])
            self.direct.append([_remote(mine(1), rows_of(me, 1), *sem(1), nx), _remote(mine(0), rows_of(me, 0), *sem(2), ny)])
            self.relay.append([_remote(rows_of(nx, 0), rows_of(nx, 0), *sem(4), ny), _remote(rows_of(ny, 1), rows_of(ny, 1), *sem(5), nx)])
            self.to_sib.append(_remote(mine(None), rows_of(me, None), *sem(6), sib))
            self.sib_fwd.append([_remote(rows_of(dev, None), rows_of(dev, None), *sem(7 + i), sib) for i, dev in enumerate((nx, ny, dg))])

    def start(self):
        for group in (self.relayed, self.direct):
            for cps in group:
                for cp in cps:
                    cp.start()
        for cp in self.to_sib:
            cp.start()

    def mid(self):
        for first, relay in zip(self.relayed, self.relay):
            for arrived, onward in zip(first, relay):
                arrived.wait_recv()
                onward.start()

    def finish(self):
        for direct, relay, fwd in zip(self.direct, self.relay, self.sib_fwd):
            for k in range(2):
                direct[k].wait_recv()
                fwd[k].start()
            for cp in relay:
                cp.wait_recv()
            fwd[2].start()
        for n in range(len(self.to_sib)):
            self.to_sib[n].wait_recv()
            for cp in self.sib_fwd[n]:
                cp.wait_recv()
            for cp in self.relayed[n] + self.direct[n] + self.relay[n] + [self.to_sib[n]] + self.sib_fwd[n]:
                cp.wait_send()


HBM = pl.BlockSpec(memory_space=pltpu.HBM)
SEM = pl.BlockSpec(memory_space=pltpu.SEMAPHORE)
FLOWS = pltpu.SideEffectType.DATAFLOW_SIDE_EFFECTING


def _in_hbm(a):
    return pltpu.with_memory_space_constraint(a, pltpu.HBM)


def _gather_start(wl, dests, ps, rows, barrier_id):
    lo = [sum(rows[:p]) for p in range(N_BIG)]
    n = len(ps)

    def body(*refs):
        wl_ref, dest_refs = refs[0], refs[1:1 + n]
        sends, recvs = refs[1 + n:1 + 2 * n], refs[1 + 2 * n:1 + 3 * n]
        token = refs[-1]
        _chips_handshake()
        x, y, c = _mesh_pos()
        jme = _dev_index(x, y, c)
        for i, p in enumerate(ps):
            mine = dest_refs[i].at[pl.ds(pl.multiple_of(jme * rows[p], BF16_ROWS), rows[p]), :]
            for chip in _other_chips(x, y):
                _remote(wl_ref.at[pl.ds(lo[p], rows[p]), :], mine, sends[i], recvs[i], (*chip, c)).start()
        token[...] = jnp.zeros_like(token)

    thru = [pltpu.HBM(wl.shape, wl.dtype)] + [pltpu.HBM(dests[p].shape, BF16) for p in ps]
    res = pl.pallas_call(
        body, name="gather_start",
        out_shape=tuple([pltpu.SemaphoreType.DMA(())] * (2 * n) + thru + [jax.ShapeDtypeStruct((SUBLANE, LANE), F32)]),
        in_specs=[HBM] * (1 + n), out_specs=tuple([SEM] * (2 * n) + [HBM] * (1 + n) + [VMEM]),
        input_output_aliases={i: 2 * n + i for i in range(1 + n)},
        compiler_params=pltpu.CompilerParams(has_side_effects=FLOWS, collective_id=barrier_id))(
            _in_hbm(wl), *[_in_hbm(dests[p]) for p in ps])
    sems = [(res[i], res[n + i]) for i in range(n)]
    return sems, res[2 * n], list(res[2 * n + 1:3 * n + 1]), res[-1]


def _gather_wait(wl, dest, sems, after, r, name):
    def body(wl_ref, dest_ref, send_sem, recv_sem, after_ref, wl_out, dest_out):
        x, y, c = _mesh_pos()
        three = dest_ref.at[pl.ds(0, 3 * r), :]
        cp = _remote(three, three, send_sem, recv_sem, (x, y, 1 - c))
        cp.wait_send()
        cp.wait_recv()

    res = pl.pallas_call(
        body, name=name, out_shape=(pltpu.HBM(wl.shape, wl.dtype), pltpu.HBM(dest.shape, dest.dtype)),
        in_specs=[HBM, HBM, SEM, SEM, ANY], out_specs=(HBM, HBM), input_output_aliases={0: 0, 1: 1},
        compiler_params=pltpu.CompilerParams(has_side_effects=FLOWS))(wl, dest, sems[0], sems[1], after)
    return res[0], res[1]


def _forward_comm(dest, r):
    def descs(ins, outs, scr):
        x, y, c = _mesh_pos()
        cps = []
        for k, chip in enumerate(_other_chips(x, y)):
            blk = outs[0].at[pl.ds(pl.multiple_of(_dev_index(*chip, c) * r, BF16_ROWS), r), :]
            cps.append(_remote(blk, blk, scr[0].at[k], scr[1].at[k], (x, y, 1 - c)))
        return cps

    def start(ins, outs, scr):
        for cp in descs(ins, outs, scr):
            cp.start()

    def finish(ins, outs, scr):
        cps = descs(ins, outs, scr)
        for cp in cps:
            cp.wait_recv()
        for cp in cps:
            cp.wait_send()

    return _Comm([dest], [jax.ShapeDtypeStruct(dest.shape, dest.dtype)], {0: 0},
                 [pltpu.SemaphoreType.DMA((3,)), pltpu.SemaphoreType.DMA((3,))], start, finish)


def _forward_now(dest, r, name):
    _, (dest,) = _host_call(lambda: None, grid=(), in_specs=[], out_specs=[], out_shape=[], args=(), name=name,
                            comm=_forward_comm(dest, r))
    return dest


def _pair_comm(g, r):
    d = g.shape[1]

    def descs(ins, outs, scr):
        x, y, c = _mesh_pos()
        chips = [(x, y)] + _other_chips(x, y)
        return [_remote(ins[0].at[pl.ds(pl.multiple_of(_dev_index(*chip, 1 - c) * r, BF16_ROWS), r), :], outs[0].at[k],
                        scr[0].at[k], scr[1].at[k], (x, y, 1 - c)) for k, chip in enumerate(chips)]

    def start(ins, outs, scr):
        for cp in descs(ins, outs, scr):
            cp.start()

    def finish(ins, outs, scr):
        cps = descs(ins, outs, scr)
        for cp in cps:
            cp.wait_recv()
        for cp in cps:
            cp.wait_send()

    comm = _Comm([g], [jax.ShapeDtypeStruct((4, r, d), BF16)], {},
                 [pltpu.SemaphoreType.DMA((4,)), pltpu.SemaphoreType.DMA((4,))], start, finish)
    return comm


def _pair_sum(g, pair, r, name, after=()):
    d = g.shape[1]

    def body(g_ref, p_ref, *rest):
        o_ref, gbuf, pbuf, sems = rest[len(after):]
        x, y, c = _mesh_pos()
        loads = [pltpu.make_async_copy(p_ref.at[pl.ds(1, 3)], pbuf, sems.at[3])]
        for k, chip in enumerate(_other_chips(x, y)):
            j = _dev_index(*chip, c)
            loads.append(pltpu.make_async_copy(g_ref.at[pl.ds(pl.multiple_of(j * r, BF16_ROWS), r), :], gbuf.at[k], sems.at[k]))
        for cp in loads:
            cp.start()
        for cp in loads:
            cp.wait()
        for k in range(3):
            o_ref[k] = (gbuf[k].astype(F32) + pbuf[k].astype(F32)).astype(BF16)

    return pl.pallas_call(
        body, out_shape=jax.ShapeDtypeStruct((3, r, d), BF16), in_specs=[ANY] * (2 + len(after)), out_specs=VMEM,
        scratch_shapes=[pltpu.VMEM((3, r, d), BF16), pltpu.VMEM((3, r, d), BF16), pltpu.SemaphoreType.DMA((4,))],
        name=name, compiler_params=_cparams())(g, pair, *after)


def _chip_start(sums, name, barrier_id):
    n = len(sums)

    def body(*refs):
        srcs, lands = refs[:n], refs[n:2 * n]
        sends, recvs = refs[2 * n:3 * n], refs[3 * n:4 * n]
        _chips_handshake()
        x, y, c = _mesh_pos()
        for i in range(n):
            for k, chip in enumerate(_other_chips(x, y)):
                _remote(srcs[i].at[k], lands[i].at[k], sends[i], recvs[i], (*chip, c)).start()
        refs[-1][...] = jnp.zeros_like(refs[-1])

    zones = [pltpu.HBM(s.shape, s.dtype) for s in sums]
    res = pl.pallas_call(
        body, name=name,
        out_shape=tuple([pltpu.SemaphoreType.DMA(())] * (2 * n) + zones + zones + [jax.ShapeDtypeStruct((SUBLANE, LANE), F32)]),
        in_specs=[HBM] * (2 * n), out_specs=tuple([SEM] * (2 * n) + [HBM] * (2 * n) + [VMEM]),
        input_output_aliases={i: 2 * n + i for i in range(2 * n)},
        compiler_params=pltpu.CompilerParams(has_side_effects=FLOWS, collective_id=barrier_id))(
            *[_in_hbm(s) for s in sums], *[_in_hbm(lax.empty(s.shape, s.dtype)) for s in sums])
    flights = [((res[i], res[n + i]), res[2 * n + i], res[3 * n + i]) for i in range(n)]
    return flights, res[-1]


def _chip_wait(sums, land, sems, after, name):
    def body(sums_ref, land_ref, send_sem, recv_sem, after_ref, sums_out, land_out):
        x, y, c = _mesh_pos()
        cp = _remote(sums_ref, land_ref, send_sem, recv_sem, (x, y, 1 - c))
        cp.wait_send()
        cp.wait_recv()

    res = pl.pallas_call(
        body, name=name, out_shape=(pltpu.HBM(sums.shape, sums.dtype), pltpu.HBM(land.shape, land.dtype)),
        in_specs=[HBM, HBM, SEM, SEM, ANY], out_specs=(HBM, HBM), input_output_aliases={0: 0, 1: 1},
        compiler_params=pltpu.CompilerParams(has_side_effects=FLOWS))(sums, land, sems[0], sems[1], after)
    return res[1]


class _CopyThrough:
    def __init__(self, src_ref, dst_ref, dst_row0, n_rows, buf, sem_in, sem_out):
        rc = n_rows // COPY_PIECES
        piece = lambda ref, o: ref.at[pl.ds(o, rc), :]
        self.loads = [pltpu.make_async_copy(piece(src_ref, k * rc), piece(buf, k * rc), sem_in) for k in range(COPY_PIECES)]
        self.stores = [pltpu.make_async_copy(piece(buf, k * rc), piece(dst_ref, dst_row0 + k * rc), sem_out) for k in range(COPY_PIECES)]
        self.all_in = pltpu.make_async_copy(src_ref, buf, sem_in)
        self.all_out = pltpu.make_async_copy(buf, dst_ref.at[pl.ds(dst_row0, n_rows), :], sem_out)

    def load(self):
        for cp in self.loads:
            cp.start()

    def store(self):
        self.all_in.wait()
        for cp in self.stores:
            cp.start()

    def done(self):
        self.all_out.wait()


def _gather_first(shards, sm, jobs, x2, tgt2, t_rows, x0):
    d = shards[0].shape[1]
    rows = [w.shape[0] for w in shards]
    lo = [sum(rows[:p]) for p in range(N_BIG)]
    n_sems = GATHER_SEMS * len(jobs)
    seq = x2.shape[0]
    assert x0 == ROW_ALIGN and seq % ROW_ALIGN == 0 and d == N_DEV * LANE

    def body(s0, s1, s2, s3, s4, sm_ref, x_ref, tgt_ref, wl_ref, o0, o1, o2, o3, o4, sa_ref, h0_ref, tp_ref,
             wl_v, x_v, tgt_v, heads_v, sa_v, send_sems, recv_sems, ssend, srecv, local_sems, sems_in, sems_out):
        dests = (o0, o1, o2, o3, o4)
        x, y, c = _mesh_pos()
        me = (x, y, c)
        jme = _dev_index(*me)
        padded = [_CopyThrough(x_ref, h0_ref, x0, seq, x_v, sems_in.at[0], sems_out.at[0]),
                  _CopyThrough(tgt_ref, tp_ref, x0, seq, tgt_v, sems_in.at[1], sems_out.at[1])]
        for cp in padded:
            cp.load()
        shard_refs = (s0, s1, s2, s3, s4)
        first = sorted({j[0] for j in jobs})
        for p in first + [p for p in range(N_BIG) if p not in first]:
            wl_v[pl.ds(lo[p], rows[p]), :] = shard_refs[p][...].astype(BF16)
            if p == first[-1]:
                gather = _Gather(jobs, rows, lo, wl_v, dict(enumerate(dests)), send_sems, recv_sems)
                gather.start()
        peers = [(x, y, 1 - c)] + [(*chip, pc) for pc in (c, 1 - c) for chip in _other_chips(x, y)]
        smalls = [_remote(sm_ref, sa_ref.at[jme], ssend.at[k], srecv.at[k], to) for k, to in enumerate(peers)]
        for cp in smalls:
            cp.start()
        mine = [pltpu.make_async_copy(wl_v.at[pl.ds(lo[p], rows[p]), :],
                                      dests[p].at[pl.ds(pl.multiple_of(jme * rows[p], BF16_ROWS), rows[p]), :], local_sems.at[p])
                for p in range(N_BIG)]
        mine.append(pltpu.make_async_copy(wl_v, wl_ref, local_sems.at[N_BIG]))
        mine.append(pltpu.make_async_copy(sm_ref, sa_ref.at[jme], local_sems.at[N_BIG + 1]))
        for cp in mine:
            cp.start()
        later = [p for p in range(N_BIG) if p not in {j[0] for j in jobs}]
        own = [_remote(wl_v.at[pl.ds(lo[p], rows[p]), :], dests[p].at[pl.ds(pl.multiple_of(jme * rows[p], BF16_ROWS), rows[p]), :],
                       ssend.at[7 + i], srecv.at[7 + i], (x, y, 1 - c)) for i, p in enumerate(later)]
        for cp in own:
            cp.start()
        for cp in padded:
            cp.store()
        gather.mid()
        for cp in smalls + own:
            cp.wait_recv()
        mine[-1].wait()
        to_v = pltpu.make_async_copy(sa_ref, sa_v, local_sems.at[N_BIG + 1])
        to_v.start()
        to_v.wait()
        head, zeros = heads_v.at[0], heads_v.at[1]
        head[...] = jnp.zeros_like(head)
        zeros[...] = jnp.zeros_like(zeros)
        for j in range(N_DEV):
            head[pl.ds(x0 - N_META, N_META), pl.ds(j * LANE, LANE)] = sa_v[j, pl.ds(0, N_META), :]
        heads = [pltpu.make_async_copy(head, h0_ref.at[pl.ds(0, x0), :], local_sems.at[N_BIG + 1]),
                 pltpu.make_async_copy(zeros, tp_ref.at[pl.ds(0, x0), :], local_sems.at[N_BIG + 2])]
        for cp in heads:
            cp.start()
        gather.finish()
        for cp in smalls + own:
            cp.wait_send()
        for cp in mine[:-1] + heads:
            cp.wait()
        for cp in padded:
            cp.done()

    out_shape = [jax.ShapeDtypeStruct((sum(rows), d), BF16)]
    out_shape += [jax.ShapeDtypeStruct((N_DEV * r, d), BF16) for r in rows]
    out_shape.append(jax.ShapeDtypeStruct((N_DEV,) + sm.shape, F32))
    out_shape += [jax.ShapeDtypeStruct((t_rows, d), F32)] * 2
    res = pl.pallas_call(
        body, out_shape=out_shape, in_specs=[VMEM] * 6 + [ANY] * 2, out_specs=[ANY] * 9,
        scratch_shapes=[pltpu.VMEM((sum(rows), d), BF16), pltpu.VMEM((seq, d), F32), pltpu.VMEM((seq, d), F32),
                        pltpu.VMEM((2, ROW_ALIGN, d), F32), pltpu.VMEM((N_DEV,) + sm.shape, F32),
                        pltpu.SemaphoreType.DMA((n_sems,)), pltpu.SemaphoreType.DMA((n_sems,)),
                        pltpu.SemaphoreType.DMA((7 + N_BIG,)), pltpu.SemaphoreType.DMA((7 + N_BIG,)),
                        pltpu.SemaphoreType.DMA((N_BIG + 3,)), pltpu.SemaphoreType.DMA((2,)), pltpu.SemaphoreType.DMA((2,))],
        name="gather_first", compiler_params=_cparams())(*shards, sm, x2, tgt2)
    return res[0], list(res[1:1 + N_BIG]), res[1 + N_BIG], res[2 + N_BIG], res[3 + N_BIG]


def _in_proj(h0, g1, win_t, tm, comm):
    t_rows, d = h0.shape
    e = win_t.shape[0]

    def body(h_ref, g_ref, w_ref, xn_ref, hin_ref):
        h = h_ref[...]
        xn = ((h * _rstd(h)) * g_ref[...]).astype(BF16)
        xn_ref[...] = xn
        for o, n in _chunks(e, N_CHUNK):
            hin_ref[:, pl.ds(o, n)] = _dot_nt(xn, w_ref[pl.ds(o, n), :])

    return _host_call(
        body, grid=(t_rows // tm,),
        in_specs=[pl.BlockSpec((tm, d), lambda i: (i, 0)), _full((1, d)), _resident((e, d))],
        out_specs=[pl.BlockSpec((tm, d), lambda i: (i, 0)), pl.BlockSpec((tm, e), lambda i: (i, 0))],
        out_shape=[jax.ShapeDtypeStruct((t_rows, d), BF16), jax.ShapeDtypeStruct((t_rows, e), F32)],
        args=(h0, g1, win_t), name="in_proj", comm=comm)


def _tap_slot(off):
    return off % SUBLANE, (off // SUBLANE) * SUBLANE


def _fill_shifted(sh_ref, base_ref, residues, n_rows):
    for r in residues:
        if r:
            sh_ref[r] = base_ref[pl.ds(r, n_rows), :]


def _shifted_rows(pair, r, start, n):
    base_ref, sh_ref = pair
    return base_ref[pl.ds(start, n), :] if r == 0 else sh_ref[r, pl.ds(start, n), :]


def _mix_conv_fwd(hin, wa, wb, bb, wa_w, comm):
    t_rows = hin.shape[0]
    nt = wa_w // LANE
    ka, kb = wa.shape[0], wb.shape[0]
    nr = CONV_HALO + t_rows

    def body(bg_ref, cg_ref, ha_ref, val_ref, gt_ref, wa_ref, wb_ref, bb_ref, ya_ref, z_ref, base, sh):
        base[pl.ds(0, CONV_HALO), :] = jnp.zeros((CONV_HALO, LANE), F32)
        base[pl.ds(nr, SUBLANE), :] = jnp.zeros((SUBLANE, LANE), F32)

        def conv(w_ref, k_taps, b, n):
            acc = None
            for k in range(k_taps):
                r, q = _tap_slot(CONV_HALO - (k_taps - 1) + k)
                term = w_ref[pl.ds(k, 1), :] * _shifted_rows((base, sh), r, b + q, n)
                acc = term if acc is None else acc + term
            return acc

        def fill_a(b, c):
            base[pl.ds(CONV_HALO + b, CONV_CHUNK), :] = cg_ref[pl.ds(b, CONV_CHUNK), :] * ha_ref[pl.ds(b, CONV_CHUNK), :]
            return c

        _row_loop(t_rows, CONV_CHUNK, fill_a)
        _fill_shifted(sh, base, sorted({_tap_slot(CONV_HALO - (ka - 1) + k)[0] for k in range(ka)}), nr)

        def out_a(b, c):
            ya_ref[pl.ds(b, CONV_CHUNK), :] = (bg_ref[pl.ds(b, CONV_CHUNK), :] * conv(wa_ref, ka, b, CONV_CHUNK)).astype(BF16)
            return c

        _row_loop(t_rows, CONV_CHUNK, out_a)

        def fill_b(b, c):
            base[pl.ds(CONV_HALO + b, CONV_CHUNK), :] = (val_ref[pl.ds(b, CONV_CHUNK), :]
                                                          * jax.nn.sigmoid(gt_ref[pl.ds(b, CONV_CHUNK), :]))
            return c

        _row_loop(t_rows, CONV_CHUNK, fill_b)
        _fill_shifted(sh, base, range(SUBLANE), nr)

        def out_b(b, c):
            z_ref[pl.ds(b, CONV_CHUNK), :] = conv(wb_ref, kb, b, CONV_CHUNK) + bb_ref[...]
            return c

        _row_loop(t_rows, CONV_CHUNK, out_b)

    def col(g):
        return pl.BlockSpec((t_rows, LANE), lambda i, g=g: (0, g * nt + i))

    tile = lambda rows: pl.BlockSpec((rows, LANE), lambda i: (0, i))
    return _host_call(
        body, grid=(nt,),
        in_specs=[col(0), col(1), col(2), col(3), col(4), tile(ka), tile(kb), tile(1)],
        out_specs=[tile(t_rows), tile(t_rows)],
        out_shape=[jax.ShapeDtypeStruct((t_rows, wa_w), BF16), jax.ShapeDtypeStruct((t_rows, wa_w), F32)],
        scratch_shapes=[pltpu.VMEM((nr + SUBLANE, LANE), F32), pltpu.VMEM((SUBLANE, nr, LANE), F32)],
        args=(hin, hin, hin, hin, hin, wa, wb, bb), name="mix_conv_fwd", comm=comm)


def _ln_parts(z, lg, lb):
    mu = jnp.mean(z, axis=-1, keepdims=True)
    zc = z - mu
    rstd = lax.rsqrt(jnp.mean(zc * zc, axis=-1, keepdims=True) + LN_EPS)
    zh = zc * rstd
    return zh, rstd, zh * lg + lb


def _out_proj(ya, z, lg, lb, w_out, h0, g2, g3, tm, comm):
    t_rows, d = h0.shape
    w = z.shape[1]

    def body(ya_ref, z_ref, lg_ref, lb_ref, w_ref, h0_ref, g2_ref, g3_ref, y_ref, mix_ref, h1_ref, xn2_ref):
        _, _, ln = _ln_parts(z_ref[...], lg_ref[...], lb_ref[...])
        y_ref[:, pl.ds(0, w)] = ya_ref[...]
        y_ref[:, pl.ds(w, w)] = (ln * jax.nn.sigmoid(ln)).astype(BF16)
        mix = _dot_nn(y_ref[...], w_ref[...])
        mix_ref[...] = mix
        h1 = h0_ref[...] + (mix * _rstd(mix)) * g2_ref[...]
        h1_ref[...] = h1
        xn2_ref[...] = ((h1 * _rstd(h1)) * g3_ref[...]).astype(BF16)

    blk = pl.BlockSpec((tm, d), lambda i: (i, 0))
    half = pl.BlockSpec((tm, w), lambda i: (i, 0))
    return _host_call(
        body, grid=(t_rows // tm,),
        in_specs=[half, half, _full((1, w)), _full((1, w)), _resident(w_out.shape), blk, _full((1, d)), _full((1, d))],
        out_specs=[blk, blk, blk, blk],
        out_shape=[jax.ShapeDtypeStruct((t_rows, d), BF16), jax.ShapeDtypeStruct((t_rows, d), F32),
                   jax.ShapeDtypeStruct((t_rows, d), F32), jax.ShapeDtypeStruct((t_rows, d), BF16)],
        args=(ya, z, lg, lb, w_out, h0, g2, g3), name="out_proj", comm=comm)


def _gate_up(xn2, wg_t, wu_t, tm, comm):
    t_rows, d = xn2.shape
    f = wg_t.shape[0]

    def body(x_ref, wg_ref, wu_ref, ga_ref, gu_ref, s_ref):
        xn = x_ref[...]
        for o, n in _chunks(f, N_CHUNK):
            a = _dot_nt(xn, wg_ref[pl.ds(o, n), :])
            u = _dot_nt(xn, wu_ref[pl.ds(o, n), :])
            sig = jax.nn.sigmoid(a)
            silu = a * sig
            s = silu * u
            gu_ref[:, pl.ds(o, n)] = silu.astype(BF16)
            ga_ref[:, pl.ds(o, n)] = ((u - s) * sig + s).astype(BF16)
            s_ref[:, pl.ds(o, n)] = s.astype(BF16)

    blk = pl.BlockSpec((tm, f), lambda i: (i, 0))
    return _host_call(
        body, grid=(t_rows // tm,),
        in_specs=[pl.BlockSpec((tm, d), lambda i: (i, 0)), _resident((f, d)), _resident((f, d))],
        out_specs=[blk, blk, blk], out_shape=[jax.ShapeDtypeStruct((t_rows, f), BF16)] * 3,
        args=(xn2, wg_t, wu_t), name="gate_up", comm=comm)


def _down_loss(s, wd, h1, tgt, g4, tm, x0):
    t_rows, d = h1.shape
    f = wd.shape[0]

    def body(s_ref, w_ref, h1_ref, tgt_ref, g4_ref, dh2_ref, dff_ref, dg4_ref, loss_ref):
        i = pl.program_id(0)
        ff = _dot_nn(s_ref[...], w_ref[...])
        r4 = _rstd(ff)
        fh = ff * r4
        g4 = g4_ref[...]
        h2 = h1_ref[...] + fh * g4
        row = i * tm + lax.broadcasted_iota(jnp.int32, (tm, 1), 0)
        diff = jnp.where(row >= x0, h2 - tgt_ref[...], 0.0)
        dh2 = diff / d
        dh2_ref[...] = dh2
        dff_ref[...] = _rms_bwd(dh2 * g4, fh, r4).astype(BF16)
        _acc_rows(dg4_ref, dh2 * fh, i == 0)
        _acc_rows(loss_ref, diff * diff, i == 0)

    blk = pl.BlockSpec((tm, d), lambda i: (i, 0))
    res, _ = _host_call(
        body, grid=(t_rows // tm,),
        in_specs=[pl.BlockSpec((tm, f), lambda i: (i, 0)), _resident((f, d)), blk, blk, _full((1, d))],
        out_specs=[blk, blk, _full((1, d)), _full((1, d))],
        out_shape=[jax.ShapeDtypeStruct((t_rows, d), F32), jax.ShapeDtypeStruct((t_rows, d), BF16),
                   jax.ShapeDtypeStruct((1, d), F32), jax.ShapeDtypeStruct((1, d), F32)],
        args=(s, wd, h1, tgt, g4), name="down_loss")
    return res


def _bwd_down(dff, wd, ga, gu, tm, comm):
    t_rows, d = dff.shape
    f = wd.shape[0]

    def body(dff_ref, w_ref, ga_ref, gu_ref, da_ref, du_ref):
        dff_v = dff_ref[...]
        for o, n in _chunks(f, N_CHUNK):
            ds = _dot_nt(dff_v, w_ref[pl.ds(o, n), :]).astype(BF16)
            da_ref[:, pl.ds(o, n)] = ds * ga_ref[:, pl.ds(o, n)]
            du_ref[:, pl.ds(o, n)] = ds * gu_ref[:, pl.ds(o, n)]

    blk = pl.BlockSpec((tm, f), lambda i: (i, 0))
    return _host_call(
        body, grid=(t_rows // tm,),
        in_specs=[pl.BlockSpec((tm, d), lambda i: (i, 0)), _resident((f, d)), blk, blk],
        out_specs=[blk, blk], out_shape=[jax.ShapeDtypeStruct((t_rows, f), BF16)] * 2,
        args=(dff, wd, ga, gu), name="bwd_down", comm=comm)


def _wgrad(a, b, name, after=()):
    d = b.shape[1]
    t_rows = b.shape[0]
    stacked = a.ndim == 3
    n = a.shape[-1]
    groups = a.shape[0] if stacked else 1
    steps = 1 if stacked else 2
    tile = max(t for t in range(LANE, min(n // steps, WGRAD_TILE_MAX) + 1, LANE) if n % t == 0)
    tiles = n // tile

    def body(a_ref, b_ref, o_ref):
        o_ref[...] = lax.dot_general(a_ref[...], b_ref[...], (((0,), (0,)), ((), ())),
                                     preferred_element_type=F32).astype(BF16)

    if stacked:
        a_spec = pl.BlockSpec((None, t_rows, tile), lambda g, i: (g, 0, i))
    else:
        a_spec = pl.BlockSpec((t_rows, tile), lambda g, i: (0, i))
    res, _ = _host_call(
        body, grid=(groups, tiles), in_specs=[a_spec, _resident((t_rows, d))],
        out_specs=[pl.BlockSpec((tile, d), lambda g, i: (g * tiles + i, 0))],
        out_shape=[jax.ShapeDtypeStruct((groups * n, d), BF16)], args=(a, b), name=name, after=after)
    return res[0]


def _bwd_ffn_in(da, du, wg_t, wu_t, h1, dh2, g3, tm, comm):
    t_rows, d = h1.shape
    f = wg_t.shape[0]

    def body(da_ref, du_ref, wg_ref, wu_ref, h1_ref, dh2_ref, g3_ref, dh1_ref, dg3_ref):
        dxn2 = _dot_nn(da_ref[...], wg_ref[...]) + _dot_nn(du_ref[...], wu_ref[...])
        h1 = h1_ref[...]
        r3 = _rstd(h1)
        h1h = h1 * r3
        _acc_rows(dg3_ref, dxn2 * h1h, pl.program_id(0) == 0)
        dh1_ref[...] = dh2_ref[...] + _rms_bwd(dxn2 * g3_ref[...], h1h, r3)

    blk = pl.BlockSpec((tm, d), lambda i: (i, 0))
    blkf = pl.BlockSpec((tm, f), lambda i: (i, 0))
    return _host_call(
        body, grid=(t_rows // tm,),
        in_specs=[blkf, blkf, _resident((f, d)), _resident((f, d)), blk, blk, _full((1, d))],
        out_specs=[blk, _full((1, d))],
        out_shape=[jax.ShapeDtypeStruct((t_rows, d), F32), jax.ShapeDtypeStruct((1, d), F32)],
        args=(da, du, wg_t, wu_t, h1, dh2, g3), name="bwd_ffn_in", comm=comm)


def _bwd_out_proj(dh1, mix, w_out, g2, z, lg, lb, tm, after):
    t_rows, d = dh1.shape
    w = z.shape[1]

    def body(dh1_ref, mix_ref, w_ref, g2_ref, z_ref, lg_ref, lb_ref, dmix_ref, dya_ref, dz_ref, dg2_ref, dlg_ref, dlb_ref, dbb_ref):
        first = pl.program_id(0) == 0
        mix = mix_ref[...]
        r2 = _rstd(mix)
        mh = mix * r2
        dh1 = dh1_ref[...]
        _acc_rows(dg2_ref, dh1 * mh, first)
        dmix = _rms_bwd(dh1 * g2_ref[...], mh, r2).astype(BF16)
        dmix_ref[...] = dmix
        dy = _dot_nt(dmix, w_ref[...])
        dya_ref[...] = dy[:, :w]
        lg = lg_ref[...]
        zh, rstd, ln = _ln_parts(z_ref[...], lg, lb_ref[...])
        dln = dy[:, w:] * _silu_grad(ln, jax.nn.sigmoid(ln))
        _acc_rows(dlg_ref, dln * zh, first)
        _acc_rows(dlb_ref, dln, first)
        dzh = dln * lg
        dz = rstd * (dzh - jnp.mean(dzh, axis=-1, keepdims=True) - zh * jnp.mean(dzh * zh, axis=-1, keepdims=True))
        dz_ref[...] = dz
        _acc_rows(dbb_ref, dz, first)

    blk = pl.BlockSpec((tm, d), lambda i: (i, 0))
    half = pl.BlockSpec((tm, w), lambda i: (i, 0))
    vec = _full((1, w))
    res, _ = _host_call(
        body, grid=(t_rows // tm,), in_specs=[blk, blk, _resident(w_out.shape), _full((1, d)), half, vec, vec],
        out_specs=[blk, half, half, _full((1, d)), vec, vec, vec],
        out_shape=[jax.ShapeDtypeStruct((t_rows, d), BF16), jax.ShapeDtypeStruct((t_rows, w), F32),
                   jax.ShapeDtypeStruct((t_rows, w), F32), jax.ShapeDtypeStruct((1, d), F32)]
        + [jax.ShapeDtypeStruct((1, w), F32)] * 3,
        args=(dh1, mix, w_out, g2, z, lg, lb), name="bwd_out_proj", after=after)
    return res


def _mix_conv_bwd(hin, dy, dz, wa, wb, wa_w, comm):
    t_rows = hin.shape[0]
    nt = wa_w // LANE
    ka, kb = wa.shape[0], wb.shape[0]
    nr = CONV_HALO + t_rows
    kb_rows = -(-kb // SUBLANE) * SUBLANE

    def body(bg_ref, cg_ref, ha_ref, val_ref, gt_ref, dya_ref, dz_ref, wa_ref, wb_ref,
             dh_ref, dwa_ref, dwb_ref, base, sh, based, shd, tmp, wbc):
        zeros = lambda n: jnp.zeros((n, LANE), F32)
        base[pl.ds(0, CONV_HALO), :] = zeros(CONV_HALO)
        base[pl.ds(nr, SUBLANE), :] = zeros(SUBLANE)
        based[pl.ds(t_rows, CONV_HALO + SUBLANE), :] = zeros(CONV_HALO + SUBLANE)

        def fwd_slot(k_taps, k):
            return _tap_slot(CONV_HALO - (k_taps - 1) + k)

        def bwd_slot(k_taps, k):
            return _tap_slot(k_taps - 1 - k)

        def conv(w_ref, k_taps, src, slot, b, n):
            acc = None
            for k in range(k_taps):
                r, q = slot(k_taps, k)
                term = w_ref[pl.ds(k, 1), :] * _shifted_rows(src, r, b + q, n)
                acc = term if acc is None else acc + term
            return acc

        def by_residue(k_taps, slot):
            groups = {}
            for k in range(k_taps):
                r, q = slot(k_taps, k)
                groups.setdefault(r, []).append((k, q // SUBLANE))
            return groups

        def wgrad_loop(w_ref, k_taps):
            n_sub = WGRAD_ROWS // SUBLANE
            for k in range(k_taps):
                wbc[k] = jnp.broadcast_to(w_ref[pl.ds(k, 1), :], (SUBLANE, LANE))
            fwd, bwd = by_residue(k_taps, fwd_slot), by_residue(k_taps, bwd_slot)

            def window(src, r, taps, b):
                span = n_sub + max(qi for _, qi in taps)
                return [_shifted_rows(src, r, b + SUBLANE * i, SUBLANE) for i in range(span)]

            def step(b, accs):
                accs = list(accs)
                dv = [based[pl.ds(b + SUBLANE * j, SUBLANE), :] for j in range(n_sub)]
                for r, taps in fwd.items():
                    win = window((base, sh), r, taps, b)
                    for k, qi in taps:
                        t = dv[0] * win[qi]
                        for j in range(1, n_sub):
                            t = t + dv[j] * win[qi + j]
                        accs[k] = accs[k] + t
                outs = [None] * n_sub
                for r, taps in bwd.items():
                    win = window((based, shd), r, taps, b)
                    for k, qi in taps:
                        wk = wbc[k]
                        for j in range(n_sub):
                            term = wk * win[qi + j]
                            outs[j] = term if outs[j] is None else outs[j] + term
                for j in range(n_sub):
                    tmp[pl.ds(b + SUBLANE * j, SUBLANE), :] = outs[j]
                return tuple(accs)

            return _row_loop(t_rows, WGRAD_ROWS, step, tuple(zeros(SUBLANE) for _ in range(k_taps)))

        def store_taps(ref, accs, rows):
            for k, acc in enumerate(accs):
                ref[pl.ds(k, 1), :] = jnp.sum(acc, axis=0, keepdims=True)
            if rows > len(accs):
                ref[pl.ds(len(accs), rows - len(accs)), :] = zeros(rows - len(accs))

        def fill_a(b, c):
            sl = pl.ds(b, CONV_CHUNK)
            base[pl.ds(CONV_HALO + b, CONV_CHUNK), :] = cg_ref[sl, :] * ha_ref[sl, :]
            based[sl, :] = dya_ref[sl, :] * bg_ref[sl, :]
            return c

        _row_loop(t_rows, CONV_CHUNK, fill_a)
        _fill_shifted(sh, base, sorted({fwd_slot(ka, k)[0] for k in range(ka)}), nr)
        _fill_shifted(shd, based, sorted({bwd_slot(ka, k)[0] for k in range(ka)}), nr)

        def d_bgate(b, c):
            sl = pl.ds(b, CONV_CHUNK)
            dh_ref[0, sl, :] = (dya_ref[sl, :] * conv(wa_ref, ka, (base, sh), fwd_slot, b, CONV_CHUNK)).astype(BF16)
            return c

        _row_loop(t_rows, CONV_CHUNK, d_bgate)
        store_taps(dwa_ref, wgrad_loop(wa_ref, ka), SUBLANE)

        def d_ch(b, c):
            sl = pl.ds(b, CONV_CHUNK)
            dua = tmp[sl, :]
            dh_ref[1, sl, :] = (dua * ha_ref[sl, :]).astype(BF16)
            dh_ref[2, sl, :] = (dua * cg_ref[sl, :]).astype(BF16)
            return c

        _row_loop(t_rows, CONV_CHUNK, d_ch)

        def fill_b(b, c):
            sl = pl.ds(b, CONV_CHUNK)
            base[pl.ds(CONV_HALO + b, CONV_CHUNK), :] = val_ref[sl, :] * jax.nn.sigmoid(gt_ref[sl, :])
            based[sl, :] = dz_ref[sl, :]
            return c

        _row_loop(t_rows, CONV_CHUNK, fill_b)
        _fill_shifted(sh, base, range(SUBLANE), nr)
        _fill_shifted(shd, based, range(SUBLANE), nr)
        store_taps(dwb_ref, wgrad_loop(wb_ref, kb), kb_rows)

        def d_glu(b, c):
            sl = pl.ds(b, CONV_CHUNK)
            dgg = tmp[sl, :]
            sig = jax.nn.sigmoid(gt_ref[sl, :])
            dh_ref[3, sl, :] = (dgg * sig).astype(BF16)
            dh_ref[4, sl, :] = (dgg * val_ref[sl, :] * (sig * (1.0 - sig))).astype(BF16)
            return c

        _row_loop(t_rows, CONV_CHUNK, d_glu)

    def col(g):
        return pl.BlockSpec((t_rows, LANE), lambda i, g=g: (0, g * nt + i))

    tile = lambda rows: pl.BlockSpec((rows, LANE), lambda i: (0, i))
    return _host_call(
        body, grid=(nt,),
        in_specs=[col(0), col(1), col(2), col(3), col(4), tile(t_rows), tile(t_rows), tile(ka), tile(kb)],
        out_specs=[pl.BlockSpec((5, t_rows, LANE), lambda i: (0, 0, i)), tile(SUBLANE), tile(kb_rows)],
        out_shape=[jax.ShapeDtypeStruct((5, t_rows, wa_w), BF16), jax.ShapeDtypeStruct((SUBLANE, wa_w), F32),
                   jax.ShapeDtypeStruct((kb_rows, wa_w), F32)],
        scratch_shapes=[pltpu.VMEM((nr + SUBLANE, LANE), F32), pltpu.VMEM((SUBLANE, nr, LANE), F32),
                        pltpu.VMEM((nr + SUBLANE, LANE), F32), pltpu.VMEM((SUBLANE, nr, LANE), F32),
                        pltpu.VMEM((t_rows, LANE), F32), pltpu.VMEM((kb_rows, SUBLANE, LANE), F32)],
        args=(hin, hin, hin, hin, hin, dy, dz, wa, wb), name="mix_conv_bwd", comm=comm)


def _bwd_in_proj(dh5, win_t, h0, dh1, g1, tm, comm):
    t_rows, d = h0.shape
    groups, _, w = dh5.shape

    def body(dh_ref, w_ref, h0_ref, dh1_ref, g1_ref, dh0_ref, dg1_ref):
        dxn1 = None
        for g in range(groups):
            part = _dot_nn(dh_ref[g], w_ref[pl.ds(g * w, w), :])
            dxn1 = part if dxn1 is None else dxn1 + part
        h0 = h0_ref[...]
        r1 = _rstd(h0)
        h0h = h0 * r1
        _acc_rows(dg1_ref, dxn1 * h0h, pl.program_id(0) == 0)
        dh0_ref[...] = dh1_ref[...] + _rms_bwd(dxn1 * g1_ref[...], h0h, r1)

    blk = pl.BlockSpec((tm, d), lambda i: (i, 0))
    return _host_call(
        body, grid=(t_rows // tm,),
        in_specs=[pl.BlockSpec((groups, tm, w), lambda i: (0, i, 0)), _resident(win_t.shape), blk, blk, _full((1, d))],
        out_specs=[blk, _full((1, d))],
        out_shape=[jax.ShapeDtypeStruct((t_rows, d), F32), jax.ShapeDtypeStruct((1, d), F32)],
        args=(dh5, win_t, h0, dh1, g1), name="bwd_in_proj", comm=comm)


def _pair_small(smalls, d):
    (dmeta, dg1, dg2, dg3, dg4, dbb, dlg, dlb, lossv, dwa, dwb) = smalls
    half = d // 2
    kb_rows = dwb.shape[0]

    def body(dmeta_ref, dg1_ref, dg2_ref, dg3_ref, dg4_ref, dbb_ref, dlg_ref, dlb_ref, loss_ref, dwa_ref, dwb_ref,
             sums_ref, pbuf, psib, ps_send, ps_recv):
        x, y, c = _mesh_pos()
        _pair_handshake()
        pbuf[...] = jnp.zeros_like(pbuf)
        pbuf[pl.ds(0, N_META), :] = dmeta_ref[...]
        for row, ref in ((16, dg1_ref), (17, dg2_ref), (18, dg3_ref), (19, dg4_ref)):
            pbuf[pl.ds(row, 1), :] = ref[...]
        pbuf[pl.ds(20, 1), pl.ds(0, half)] = dbb_ref[...]
        pbuf[pl.ds(20, 1), pl.ds(half, half)] = dlg_ref[...]
        pbuf[pl.ds(21, 1), pl.ds(0, half)] = dlb_ref[...]
        lv = loss_ref[...]
        pbuf[pl.ds(21, 1), pl.ds(half, half)] = lv[:, :half] + lv[:, half:]
        pbuf[pl.ds(24, SUBLANE), pl.ds(0, half)] = dwa_ref[...]
        pbuf[pl.ds(32, kb_rows), pl.ds(0, half)] = dwb_ref[...]
        to_sib = _remote(pbuf, psib, ps_send.at[0], ps_recv.at[0], (x, y, 1 - c))
        to_sib.start()
        to_sib.wait_recv()
        s = pbuf[...] + psib[...]
        for k in range(3):
            sums_ref[k] = s
        to_sib.wait_send()

    return pl.pallas_call(
        body, out_shape=jax.ShapeDtypeStruct((3, SMALL_ROWS, d), F32), in_specs=[VMEM] * 11, out_specs=VMEM,
        scratch_shapes=[pltpu.VMEM((SMALL_ROWS, d), F32), pltpu.VMEM((SMALL_ROWS, d), F32),
                        pltpu.SemaphoreType.DMA((1,)), pltpu.SemaphoreType.DMA((1,))],
        name="pair_small", compiler_params=pltpu.CompilerParams(vmem_limit_bytes=VMEM_LIMIT, collective_id=PAIR_BARRIER_ID))(*smalls)


def _total_small(own, others):
    _, r, d = own.shape

    def body(own_ref, others_ref, tot_ref, chip_p):
        x, y, _ = _mesh_pos()
        chip_p[2 * x + y] = own_ref[0]
        for k, (cx, cy) in enumerate(_other_chips(x, y)):
            chip_p[2 * cx + cy] = others_ref[k]
        tot_ref[...] = ((chip_p[0] + chip_p[1]) + chip_p[2]) + chip_p[3]

    return pl.pallas_call(body, out_shape=jax.ShapeDtypeStruct((r, d), F32), scratch_shapes=[pltpu.VMEM((4, r, d), F32)],
                          name="total_small", compiler_params=_cparams())(own, others)


def _adamw(w, g, m, v):
    m = ADAM_B1 * m + (1.0 - ADAM_B1) * g
    v = ADAM_B2 * v + (1.0 - ADAM_B2) * jnp.square(g)
    m_hat = m / (1.0 - ADAM_B1 ** ADAM_STEP)
    v_hat = v / (1.0 - ADAM_B2 ** ADAM_STEP)
    delta = -ADAM_LR * (m_hat / (jnp.sqrt(v_hat) + ADAM_EPS) + ADAM_WD * w)
    return delta, m, v


def _adam_big(g, pair, part, w, m, v, name):
    r, d = w.shape
    cols = d // ADAM_COL_BLOCKS

    def body(me_ref, g_ref, pair_ref, part_ref, w_ref, m_ref, v_ref, go_ref, d_ref, mo_ref, vo_ref):
        g = g_ref[...].astype(F32) + pair_ref[...].astype(F32)
        for k in range(3):
            g = g + part_ref[k].astype(F32)
        go_ref[...] = g
        d_ref[...], mo_ref[...], vo_ref[...] = _adamw(w_ref[...], g, m_ref[...], v_ref[...])

    blk = pl.BlockSpec((r, cols), lambda i, me_ref: (0, i))
    grid_spec = pltpu.PrefetchScalarGridSpec(
        num_scalar_prefetch=1, grid=(ADAM_COL_BLOCKS,),
        in_specs=[pl.BlockSpec((r, cols), lambda i, me_ref: (me_ref[0], i)),
                  pl.BlockSpec((None, r, cols), lambda i, me_ref: (0, 0, i)),
                  pl.BlockSpec((3, r, cols), lambda i, me_ref: (0, 0, i)), blk, blk, blk],
        out_specs=[blk, blk, blk, blk])
    me = jnp.reshape(_dev_index(*_mesh_pos()), (1,)).astype(jnp.int32)
    return pl.pallas_call(body, out_shape=[jax.ShapeDtypeStruct((r, d), F32)] * 4, grid_spec=grid_spec, name=name,
                          compiler_params=_cparams(1))(me, g, pair, part, w, m, v)


def _adam_small(gs, ws, ms, vs):
    n = len(gs)

    def body(*refs):
        ins, outs = refs[:4 * n], refs[4 * n:]
        for i in range(n):
            g = ins[i][...]
            delta, m, v = _adamw(ins[n + i][...], g, ins[2 * n + i][...], ins[3 * n + i][...])
            outs[i][...] = delta
            outs[n + i][...] = m
            outs[2 * n + i][...] = v

    shapes = [jax.ShapeDtypeStruct(w.shape, F32) for w in ws]
    return pl.pallas_call(body, out_shape=shapes * 3, name="adam_small", compiler_params=_cparams())(*gs, *ws, *ms, *vs)


def kernel(x, meta_tokens, pre_mix_norm, w_in, conv_a_w, conv_b_w, conv_b_bias, ln_b_gain, ln_b_bias, w_out, post_mix_norm, pre_ffn_norm, w_gate, w_up, w_down, post_ffn_norm, loss_target, m_meta_tokens, m_pre_mix_norm, m_w_in, m_conv_a_w, m_conv_b_w, m_conv_b_bias, m_ln_b_gain, m_ln_b_bias, m_w_out, m_post_mix_norm, m_pre_ffn_norm, m_w_gate, m_w_up, m_w_down, m_post_ffn_norm, v_meta_tokens, v_pre_mix_norm, v_w_in, v_conv_a_w, v_conv_b_w, v_conv_b_bias, v_ln_b_gain, v_ln_b_bias, v_w_out, v_post_mix_norm, v_pre_ffn_norm, v_w_gate, v_w_up, v_w_down, v_post_ffn_norm):
    _, seq, d = x.shape
    ka, ca_loc = conv_a_w.shape[1:]
    kb, cb_loc = conv_b_w.shape[1:]
    wa_w = ca_loc * N_DEV
    assert cb_loc == ca_loc and wa_w % LANE == 0 and w_in.shape[2] * N_DEV == 5 * wa_w and 2 * wa_w == d
    pad = (-(N_META + seq)) % ROW_ALIGN
    x0 = pad + N_META
    t_rows = x0 + seq
    assert t_rows % (N_ROW_BLOCKS * BF16_ROWS) == 0 and t_rows % CONV_CHUNK == 0 and d % LANE == 0
    tm = t_rows // N_ROW_BLOCKS
    tm2 = t_rows // 2
    tm8 = t_rows // 8
    assert tm8 % BF16_ROWS == 0
    me = _dev_index(*_mesh_pos())

    def as_rows(w_in_like, w_out_like, w_gate_like, w_up_like, w_down_like):
        return (w_in_like[0].T, w_out_like[0], w_gate_like[0].T, w_up_like[0].T, w_down_like[0])

    w_loc = as_rows(w_in, w_out, w_gate, w_up, w_down)
    rows = [w.shape[0] for w in w_loc]
    assert all(r % ADD_CHUNK == 0 for r in rows)
    P_IN, P_OUT, P_GATE, P_UP, P_DOWN = range(N_BIG)

    sm = jnp.zeros((SM_ROWS, LANE), F32)
    sm = sm.at[0:N_META, :].set(meta_tokens)
    sm = sm.at[16:16 + ka, 0:ca_loc].set(conv_a_w[0])
    sm = sm.at[24:24 + kb, 0:cb_loc].set(conv_b_w[0])
    wl, wfull, sm_all, h0, tgt = _gather_first(w_loc, sm, [(P_IN, 0, rows[P_IN])], x[0], loss_target[0], t_rows, x0)
    wa =jnp.transpose(sm_all[:, 16:16 + ka, 0:ca_loc], (1, 0, 2)).reshape(ka, wa_w)
    wb = jnp.transpose(sm_all[:, 24:24 + kb, 0:cb_loc], (1, 0, 2)).reshape(kb, wa_w)

    later = (P_OUT, P_GATE, P_UP, P_DOWN)
    sems, wl, started, _ = _gather_start(wl, wfull, later, rows, START_BARRIER_IDS[0])
    for p, arr in zip(later, started):
        wfull[p] = arr

    def arrived(p, after, name):
        nonlocal wl
        wl, wfull[p] = _gather_wait(wl, wfull[p], sems[later.index(p)], after, rows[p], name)
        return _forward_comm(wfull[p], rows[p])

    (xn1, hin), _ = _in_proj(h0, pre_mix_norm, wfull[P_IN], tm8, None)
    (ya, z), (wfull[P_OUT],) = _mix_conv_fwd(hin, wa, wb, conv_b_bias, wa_w, arrived(P_OUT, hin, "gather_wait_out"))
    (y, mix, h1, xn2), (wfull[P_GATE],) = _out_proj(ya, z, ln_b_gain, ln_b_bias, wfull[P_OUT], h0, post_mix_norm, pre_ffn_norm, tm2,
                                                    arrived(P_GATE, z, "gather_wait_gate"))
    arrived(P_UP, xn2, "gather_wait_up")
    wfull[P_UP] = _forward_now(wfull[P_UP], rows[P_UP], "forward_up")
    (ga, gu, s), _ = _gate_up(xn2, wfull[P_GATE], wfull[P_UP], tm, None)
    arrived(P_DOWN, s, "gather_wait_down")
    wfull[P_DOWN] = _forward_now(wfull[P_DOWN], rows[P_DOWN], "forward_down")
    dh2, dff, dg4, lossv = _down_loss(s, wfull[P_DOWN], h1, tgt, post_ffn_norm, tm8, x0)

    gwd = _wgrad(s, dff, "wgrad_down")
    (da, du), (pair_d,) = _bwd_down(dff, wfull[P_DOWN], ga, gu, tm, _pair_comm(gwd, rows[P_DOWN]))
    (flight_d,), token = _chip_start([_pair_sum(gwd, pair_d, rows[P_DOWN], "pair_sum_down")], "chip_start_down", START_BARRIER_IDS[1])
    gwg = _wgrad(da, xn2, "wgrad_gate", [token])
    gwu = _wgrad(du, xn2, "wgrad_up")
    (dh1, dg3), (pair_g, pair_u) = _bwd_ffn_in(da, du, wfull[P_GATE], wfull[P_UP], h1, dh2, pre_ffn_norm, tm,
                                               _merge_comms([_pair_comm(gwg, rows[P_GATE]), _pair_comm(gwu, rows[P_UP])]))
    (flight_g, flight_u), token = _chip_start([_pair_sum(gwg, pair_g, rows[P_GATE], "pair_sum_gate"),
                                               _pair_sum(gwu, pair_u, rows[P_UP], "pair_sum_up")], "chip_start_gate_up",
                                              START_BARRIER_IDS[2])
    dmix, dya, dz, dg2, dlg, dlb, dbb = _bwd_out_proj(dh1, mix, wfull[P_OUT], post_mix_norm, z, ln_b_gain, ln_b_bias, tm8, [token])
    gwo = _wgrad(y, dmix, "wgrad_out")
    (dh5, dwa, dwb), (pair_o,) = _mix_conv_bwd(hin, dya, dz, wa, wb, wa_w, _pair_comm(gwo, rows[P_OUT]))
    (flight_o,), token = _chip_start([_pair_sum(gwo, pair_o, rows[P_OUT], "pair_sum_out")], "chip_start_out", START_BARRIER_IDS[3])
    gwi = _wgrad(dh5, xn1, "wgrad_in", [token])
    (dh0, dg1), (pair_i,) = _bwd_in_proj(dh5, wfull[P_IN], h0, dh1, pre_mix_norm, tm8, _pair_comm(gwi, rows[P_IN]))
    grad_x = dh0[x0:][None]
    dmeta = dh0[x0 - N_META:x0]
    small_sums = _pair_small((dmeta, dg1, dg2, dg3, dg4, dbb, dlg, dlb, lossv, dwa, dwb), d)
    (flight_s,), token = _chip_start([small_sums], "chip_start_small", START_BARRIER_IDS[4])
    (flight_i,), token = _chip_start([_pair_sum(gwi, pair_i, rows[P_IN], "pair_sum_in", [token])], "chip_start_in",
                                     START_BARRIER_IDS[5])

    def landed(flight, after, tag):
        sems_p, sums, land = flight
        return _chip_wait(sums, land, sems_p, after, "chip_wait_" + tag)

    m_loc = as_rows(m_w_in, m_w_out, m_w_gate, m_w_up, m_w_down)
    v_loc = as_rows(v_w_in, v_w_out, v_w_gate, v_w_up, v_w_down)
    full_grads = {P_IN: gwi, P_OUT: gwo, P_GATE: gwg, P_UP: gwu, P_DOWN: gwd}
    pairs = {P_IN: pair_i, P_OUT: pair_o, P_GATE: pair_g, P_UP: pair_u, P_DOWN: pair_d}
    flights = {P_IN: flight_i, P_OUT: flight_o, P_GATE: flight_g, P_UP: flight_u, P_DOWN: flight_d}
    names = {P_IN: "w_in", P_OUT: "w_out", P_GATE: "w_gate", P_UP: "w_up", P_DOWN: "w_down"}
    bigs = {}

    def adam_big(p, after):
        part = landed(flights[p], after, names[p])
        res = _adam_big(full_grads[p], pairs[p], part, w_loc[p], m_loc[p], v_loc[p], "adam_" + names[p])
        bigs[names[p]] = [(o.T if p in (P_IN, P_GATE, P_UP) else o)[None] for o in res]
        return res[1]

    for p in (P_DOWN, P_GATE, P_UP, P_OUT):
        token = adam_big(p, token)
    ptot = _total_small(flight_s[1], landed(flight_s, token, "small"))
    half = d // 2
    loss = (0.5 / d) * jnp.sum(ptot[21, half:])
    g_meta = lax.dynamic_slice(ptot, (0, me * (d // N_DEV)), (N_META, d // N_DEV))
    g_small = [g_meta, ptot[16:17], lax.dynamic_slice(ptot, (24, me * ca_loc), (ka, ca_loc))[None],
               lax.dynamic_slice(ptot, (32, me * cb_loc), (kb, cb_loc))[None],
               ptot[20:21, :half], ptot[20:21, half:], ptot[21:22, :half], ptot[17:18], ptot[18:19], ptot[19:20]]
    w_small = [meta_tokens, pre_mix_norm, conv_a_w, conv_b_w, conv_b_bias, ln_b_gain, ln_b_bias, post_mix_norm,
               pre_ffn_norm, post_ffn_norm]
    m_small = [m_meta_tokens, m_pre_mix_norm, m_conv_a_w, m_conv_b_w, m_conv_b_bias, m_ln_b_gain, m_ln_b_bias,
               m_post_mix_norm, m_pre_ffn_norm, m_post_ffn_norm]
    v_small = [v_meta_tokens, v_pre_mix_norm, v_conv_a_w, v_conv_b_w, v_conv_b_bias, v_ln_b_gain, v_ln_b_bias,
               v_post_mix_norm, v_pre_ffn_norm, v_post_ffn_norm]
    small = _adam_small(g_small, w_small, m_small, v_small)
    n_small = len(w_small)
    d_small, nm_small, nv_small = small[:n_small], small[n_small:2 * n_small], small[2 * n_small:]

    adam_big(P_IN, small[0])

    def ordered(pick_small, pick_big):
        sm_it = iter(range(n_small))
        out = []
        for name in ("s", "s", "w_in", "s", "s", "s", "s", "s", "w_out", "s", "s", "w_gate", "w_up", "w_down", "s"):
            out.append(pick_small(next(sm_it)) if name == "s" else pick_big(name))
        return out

    grads = ordered(lambda i: g_small[i], lambda n: bigs[n][0])
    deltas = ordered(lambda i: d_small[i], lambda n: bigs[n][1])
    new_m = ordered(lambda i: nm_small[i], lambda n: bigs[n][2])
    new_v = ordered(lambda i: nv_small[i], lambda n: bigs[n][3])
    return (loss, grad_x, *grads, *deltas, *new_m, *new_v)
```

```python
import jax
import jax.numpy as jnp
from jax import lax
from jax.experimental import pallas as pl
from jax.experimental.pallas import tpu as pltpu

F32 = jnp.float32
BF16 = jnp.bfloat16
MESH = pl.DeviceIdType.MESH

N_META = 16
N_DEV = 8
RMS_EPS = 1e-6
LN_EPS = 1e-5
ADAM_LR = 0.001
ADAM_B1 = 0.9
ADAM_B2 = 0.999
ADAM_EPS = 1e-08
ADAM_WD = 0.01
ADAM_STEP = 10

LANE = 128
SUBLANE = 8
BF16_ROWS = 16
ROW_ALIGN = 128
N_ROW_BLOCKS = 4
CONV_HALO = 32
CONV_CHUNK = 64
WGRAD_ROWS = 32
N_CHUNK = 512
WGRAD_TILE_MAX = 1408
ADD_CHUNK = 32
ADAM_COL_BLOCKS = 2
COPY_PIECES = 4
V7X_VMEM_BYTES = 64 * 1024 * 1024
VMEM_LIMIT = V7X_VMEM_BYTES - 6 * 1024 * 1024
SMALL_ROWS = 64
SM_ROWS = 56
N_BIG = 5

ANY = pl.BlockSpec(memory_space=pl.ANY)
VMEM = pl.BlockSpec(memory_space=pltpu.VMEM)


def _cparams(n_grid_axes=0):
    sem = ("arbitrary",) * n_grid_axes if n_grid_axes else None
    return pltpu.CompilerParams(dimension_semantics=sem, vmem_limit_bytes=VMEM_LIMIT)


def _mesh_pos():
    return lax.axis_index("x"), lax.axis_index("y"), lax.axis_index("c")


def _dev_index(px, py, pc):
    return 4 * px + 2 * py + pc


def _other_chips(x, y):
    return [(1 - x, y), (x, 1 - y), (1 - x, 1 - y)]


def _full(shape):
    return pl.BlockSpec(shape, lambda *_: (0,) * len(shape))


def _resident(shape):
    return pl.BlockSpec(shape, lambda *_: (0,) * len(shape), pipeline_mode=pl.Buffered(1))


def _dot_nt(a, w):
    return lax.dot_general(a, w, (((1,), (1,)), ((), ())), preferred_element_type=F32)


def _dot_nn(a, w):
    return jnp.dot(a, w, preferred_element_type=F32)


def _chunks(n, c):
    out, o = [], 0
    while o < n:
        out.append((o, min(c, n - o)))
        o += c
    return out


def _rstd(h):
    return lax.rsqrt(jnp.mean(h * h, axis=-1, keepdims=True) + RMS_EPS)


def _rms_bwd(dyh, yh, r):
    return r * (dyh - yh * jnp.mean(dyh * yh, axis=-1, keepdims=True))


def _silu_grad(a, sig):
    return sig * (1.0 + a * (1.0 - sig))


def _acc_rows(ref, val, first):
    s = jnp.sum(val, axis=0, keepdims=True)

    @pl.when(first)
    def _():
        ref[...] = s

    @pl.when(jnp.logical_not(first))
    def _():
        ref[...] += s


def _row_loop(t_rows, chunk, fn, carry=None):
    def step(i, c):
        return fn(pl.multiple_of(i * chunk, chunk), c)

    return lax.fori_loop(0, t_rows // chunk, step, carry)


def _remote(src, dst, send_sem, recv_sem, to):
    return pltpu.make_async_remote_copy(src_ref=src, dst_ref=dst, send_sem=send_sem, recv_sem=recv_sem,
                                        device_id=to, device_id_type=MESH)


class _Comm:
    def __init__(self, inputs, out_shapes, aliases, scratch, start, finish):
        self.inputs, self.out_shapes, self.aliases, self.scratch = list(inputs), list(out_shapes), dict(aliases), list(scratch)
        self.start, self.finish = start, finish


def _merge_comms(comms):
    inputs, out_shapes, aliases, scratch, spans = [], [], {}, [], []
    for cm in comms:
        spans.append((len(inputs), len(out_shapes), len(scratch), cm))
        aliases.update({len(inputs) + k: len(out_shapes) + v for k, v in cm.aliases.items()})
        inputs += cm.inputs
        out_shapes += cm.out_shapes
        scratch += cm.scratch

    def run(which):
        def fn(ins, outs, scr):
            for i0, o0, s0, cm in spans:
                getattr(cm, which)(ins[i0:i0 + len(cm.inputs)], outs[o0:o0 + len(cm.out_shapes)], scr[s0:s0 + len(cm.scratch)])
        return fn

    return _Comm(inputs, out_shapes, aliases, scratch, run("start"), run("finish"))


def _host_call(body, *, grid, in_specs, out_specs, out_shape, args, name, scratch_shapes=(), comm=None, after=()):
    talks = comm is not None
    if comm is None:
        comm = _Comm([], [], {}, [], lambda *_: None, lambda *_: None)
    n_in, n_out, n_scr = len(args), len(out_shape), len(scratch_shapes)
    c_in, c_out = len(comm.inputs), len(comm.out_shapes)
    n_after = len(after)

    def open_comm(c_ins, c_outs, c_scr):
        if talks:
            _pair_handshake()
        comm.start(c_ins, c_outs, c_scr)

    def hosted(*refs):
        ins, c_ins = refs[:n_in], refs[n_in:n_in + c_in]
        o0 = n_in + c_in + n_after
        outs, c_outs = refs[o0:o0 + n_out], refs[o0 + n_out:o0 + n_out + c_out]
        s0 = o0 + n_out + c_out
        scr, c_scr = refs[s0:s0 + n_scr], refs[s0 + n_scr:]
        if not grid:
            open_comm(c_ins, c_outs, c_scr)
            body(*ins, *outs, *scr)
            comm.finish(c_ins, c_outs, c_scr)
            return
        first = last = None
        for a, n in enumerate(grid):
            f, l = pl.program_id(a) == 0, pl.program_id(a) == n - 1
            first = f if first is None else jnp.logical_and(first, f)
            last = l if last is None else jnp.logical_and(last, l)

        @pl.when(first)
        def _():
            open_comm(c_ins, c_outs, c_scr)

        body(*ins, *outs, *scr)

        @pl.when(last)
        def _():
            comm.finish(c_ins, c_outs, c_scr)

    sem = ("arbitrary",) * len(grid) if grid else None
    params = pltpu.CompilerParams(dimension_semantics=sem, vmem_limit_bytes=VMEM_LIMIT,
                                  collective_id=PAIR_BARRIER_ID if talks else None)
    res = pl.pallas_call(
        hosted, grid=grid, in_specs=list(in_specs) + [ANY] * (c_in + n_after), out_specs=list(out_specs) + [ANY] * c_out,
        out_shape=list(out_shape) + comm.out_shapes, scratch_shapes=list(scratch_shapes) + comm.scratch,
        input_output_aliases={n_in + k: n_out + v for k, v in comm.aliases.items()},
        name=name, compiler_params=params)(*args, *comm.inputs, *after)
    return list(res[:n_out]), list(res[n_out:])


PAIR_BARRIER_ID = 0
START_BARRIER_IDS = (1, 2, 3, 4, 5, 6)
ALL_PEERS_BARRIER_ID = 7


def _chips_handshake():
    x, y, c = _mesh_pos()
    barrier = pltpu.get_barrier_semaphore()
    for chip in _other_chips(x, y):
        pl.semaphore_signal(barrier, inc=1, device_id=(*chip, c), device_id_type=MESH)
    pl.semaphore_wait(barrier, 3)


def _pair_handshake():
    x, y, c = _mesh_pos()
    barrier = pltpu.get_barrier_semaphore()
    pl.semaphore_signal(barrier, inc=1, device_id=(x, y, 1 - c), device_id_type=MESH)
    pl.semaphore_wait(barrier, 1)


GATHER_SEMS = 10


class _Gather:
    def __init__(self, jobs, rows, lo, src_ref, dests, send_sems, recv_sems):
        x, y, c = _mesh_pos()
        me, sib = (x, y, c), (x, y, 1 - c)
        nx, ny, dg = (1 - x, y, c), (x, 1 - y, c), (1 - x, 1 - y, c)
        self.relayed, self.direct, self.relay, self.to_sib, self.sib_fwd = [], [], [], [], []
        for n, (p, r0, nr) in enumerate(jobs):
            assert nr % (2 * BF16_ROWS) == 0
            half = nr // 2

            def rows_of(dev, h, p=p, r0=r0, nr=nr, half=half):
                off, cnt = (r0, nr) if h is None else (r0 + h * half, half)
                return dests[p].at[pl.ds(pl.multiple_of(_dev_index(*dev) * rows[p] + off, BF16_ROWS), cnt), :]

            def mine(h, p=p, r0=r0, nr=nr, half=half):
                off, cnt = (r0, nr) if h is None else (r0 + h * half, half)
                return src_ref.at[pl.ds(lo[p] + off, cnt), :]

            sem = lambda k, n=n: (send_sems.at[GATHER_SEMS * n + k], recv_sems.at[GATHER_SEMS * n + k])
            self.relayed.append([_remote(mine(0), rows_of(me, 0), *sem(0), nx), _remote(mine(1), rows_of(me, 1), *sem(3), ny)])
            self.direct.append([_remote(mine(1), rows_of(me, 1), *sem(1), nx), _remote(mine(0), rows_of(me, 0), *sem(2), ny)])
            self.relay.append([_remote(rows_of(nx, 0), rows_of(nx, 0), *sem(4), ny), _remote(rows_of(ny, 1), rows_of(ny, 1), *sem(5), nx)])
            self.to_sib.append(_remote(mine(None), rows_of(me, None), *sem(6), sib))
            self.sib_fwd.append([_remote(rows_of(dev, None), rows_of(dev, None), *sem(7 + i), sib) for i, dev in enumerate((nx, ny, dg))])

    def start(self):
        for group in (self.relayed, self.direct):
            for cps in group:
                for cp in cps:
                    cp.start()
        for cp in self.to_sib:
            cp.start()

    def mid(self):
        for first, relay in zip(self.relayed, self.relay):
            for arrived, onward in zip(first, relay):
                arrived.wait_recv()
                onward.start()

    def finish(self):
        for direct, relay, fwd in zip(self.direct, self.relay, self.sib_fwd):
            for k in range(2):
                direct[k].wait_recv()
                fwd[k].start()
            for cp in relay:
                cp.wait_recv()
            fwd[2].start()
        for n in range(len(self.to_sib)):
            self.to_sib[n].wait_recv()
            for cp in self.sib_fwd[n]:
                cp.wait_recv()
            for cp in self.relayed[n] + self.direct[n] + self.relay[n] + [self.to_sib[n]] + self.sib_fwd[n]:
                cp.wait_send()


HBM = pl.BlockSpec(memory_space=pltpu.HBM)
SEM = pl.BlockSpec(memory_space=pltpu.SEMAPHORE)
FLOWS = pltpu.SideEffectType.DATAFLOW_SIDE_EFFECTING


def _in_hbm(a):
    return pltpu.with_memory_space_constraint(a, pltpu.HBM)


def _gather_start(wl, dests, ps, rows, barrier_id):
    lo = [sum(rows[:p]) for p in range(N_BIG)]
    n = len(ps)

    def body(*refs):
        wl_ref, dest_refs = refs[0], refs[1:1 + n]
        sends, recvs = refs[1 + n:1 + 2 * n], refs[1 + 2 * n:1 + 3 * n]
        token = refs[-1]
        _chips_handshake()
        x, y, c = _mesh_pos()
        jme = _dev_index(x, y, c)
        for i, p in enumerate(ps):
            mine = dest_refs[i].at[pl.ds(pl.multiple_of(jme * rows[p], BF16_ROWS), rows[p]), :]
            for chip in _other_chips(x, y):
                _remote(wl_ref.at[pl.ds(lo[p], rows[p]), :], mine, sends[i], recvs[i], (*chip, c)).start()
        token[...] = jnp.zeros_like(token)

    thru = [pltpu.HBM(wl.shape, wl.dtype)] + [pltpu.HBM(dests[p].shape, BF16) for p in ps]
    res = pl.pallas_call(
        body, name="gather_start",
        out_shape=tuple([pltpu.SemaphoreType.DMA(())] * (2 * n) + thru + [jax.ShapeDtypeStruct((SUBLANE, LANE), F32)]),
        in_specs=[HBM] * (1 + n), out_specs=tuple([SEM] * (2 * n) + [HBM] * (1 + n) + [VMEM]),
        input_output_aliases={i: 2 * n + i for i in range(1 + n)},
        compiler_params=pltpu.CompilerParams(has_side_effects=FLOWS, collective_id=barrier_id))(
            _in_hbm(wl), *[_in_hbm(dests[p]) for p in ps])
    sems = [(res[i], res[n + i]) for i in range(n)]
    return sems, res[2 * n], list(res[2 * n + 1:3 * n + 1]), res[-1]


def _gather_wait(wl, dest, sems, after, r, name):
    def body(wl_ref, dest_ref, send_sem, recv_sem, after_ref, wl_out, dest_out):
        x, y, c = _mesh_pos()
        three = dest_ref.at[pl.ds(0, 3 * r), :]
        cp = _remote(three, three, send_sem, recv_sem, (x, y, 1 - c))
        cp.wait_send()
        cp.wait_recv()

    res = pl.pallas_call(
        body, name=name, out_shape=(pltpu.HBM(wl.shape, wl.dtype), pltpu.HBM(dest.shape, dest.dtype)),
        in_specs=[HBM, HBM, SEM, SEM, ANY], out_specs=(HBM, HBM), input_output_aliases={0: 0, 1: 1},
        compiler_params=pltpu.CompilerParams(has_side_effects=FLOWS))(wl, dest, sems[0], sems[1], after)
    return res[0], res[1]


def _forward_comm(dest, r):
    def descs(ins, outs, scr):
        x, y, c = _mesh_pos()
        cps = []
        for k, chip in enumerate(_other_chips(x, y)):
            blk = outs[0].at[pl.ds(pl.multiple_of(_dev_index(*chip, c) * r, BF16_ROWS), r), :]
            cps.append(_remote(blk, blk, scr[0].at[k], scr[1].at[k], (x, y, 1 - c)))
        return cps

    def start(ins, outs, scr):
        for cp in descs(ins, outs, scr):
            cp.start()

    def finish(ins, outs, scr):
        cps = descs(ins, outs, scr)
        for cp in cps:
            cp.wait_recv()
        for cp in cps:
            cp.wait_send()

    return _Comm([dest], [jax.ShapeDtypeStruct(dest.shape, dest.dtype)], {0: 0},
                 [pltpu.SemaphoreType.DMA((3,)), pltpu.SemaphoreType.DMA((3,))], start, finish)


def _forward_now(dest, r, name):
    _, (dest,) = _host_call(lambda: None, grid=(), in_specs=[], out_specs=[], out_shape=[], args=(), name=name,
                            comm=_forward_comm(dest, r))
    return dest


def _pair_comm(g, r, row0=0):
    d = g.shape[1]

    def descs(ins, outs, scr):
        x, y, c = _mesh_pos()
        chips = [(x, y)] + _other_chips(x, y)
        return [_remote(ins[0].at[pl.ds(pl.multiple_of(row0 + _dev_index(*chip, 1 - c) * r, BF16_ROWS), r), :], outs[0].at[k],
                        scr[0].at[k], scr[1].at[k], (x, y, 1 - c)) for k, chip in enumerate(chips)]

    def start(ins, outs, scr):
        for cp in descs(ins, outs, scr):
            cp.start()

    def finish(ins, outs, scr):
        cps = descs(ins, outs, scr)
        for cp in cps:
            cp.wait_recv()
        for cp in cps:
            cp.wait_send()

    comm = _Comm([g], [jax.ShapeDtypeStruct((4, r, d), BF16)], {},
                 [pltpu.SemaphoreType.DMA((4,)), pltpu.SemaphoreType.DMA((4,))], start, finish)
    return comm


def _pair_sum(g, pair, r, name, after=(), row0=0):
    d = g.shape[1]

    def body(g_ref, p_ref, *rest):
        o_ref, gbuf, pbuf, sems = rest[len(after):]
        x, y, c = _mesh_pos()
        loads = [pltpu.make_async_copy(p_ref.at[pl.ds(1, 3)], pbuf, sems.at[3])]
        for k, chip in enumerate(_other_chips(x, y)):
            j = _dev_index(*chip, c)
            loads.append(pltpu.make_async_copy(g_ref.at[pl.ds(pl.multiple_of(row0 + j * r, BF16_ROWS), r), :], gbuf.at[k], sems.at[k]))
        for cp in loads:
            cp.start()
        for cp in loads:
            cp.wait()
        for k in range(3):
            o_ref[k] = (gbuf[k].astype(F32) + pbuf[k].astype(F32)).astype(BF16)

    return pl.pallas_call(
        body, out_shape=jax.ShapeDtypeStruct((3, r, d), BF16), in_specs=[ANY] * (2 + len(after)), out_specs=VMEM,
        scratch_shapes=[pltpu.VMEM((3, r, d), BF16), pltpu.VMEM((3, r, d), BF16), pltpu.SemaphoreType.DMA((4,))],
        name=name, compiler_params=_cparams())(g, pair, *after)


def _chip_start(sums, name, barrier_id):
    n = len(sums)

    def body(*refs):
        srcs, lands = refs[:n], refs[n:2 * n]
        sends, recvs = refs[2 * n:3 * n], refs[3 * n:4 * n]
        _chips_handshake()
        x, y, c = _mesh_pos()
        for i in range(n):
            for k, chip in enumerate(_other_chips(x, y)):
                _remote(srcs[i].at[k], lands[i].at[k], sends[i], recvs[i], (*chip, c)).start()
        refs[-1][...] = jnp.zeros_like(refs[-1])

    zones = [pltpu.HBM(s.shape, s.dtype) for s in sums]
    res = pl.pallas_call(
        body, name=name,
        out_shape=tuple([pltpu.SemaphoreType.DMA(())] * (2 * n) + zones + zones + [jax.ShapeDtypeStruct((SUBLANE, LANE), F32)]),
        in_specs=[HBM] * (2 * n), out_specs=tuple([SEM] * (2 * n) + [HBM] * (2 * n) + [VMEM]),
        input_output_aliases={i: 2 * n + i for i in range(2 * n)},
        compiler_params=pltpu.CompilerParams(has_side_effects=FLOWS, collective_id=barrier_id))(
            *[_in_hbm(s) for s in sums], *[_in_hbm(lax.empty(s.shape, s.dtype)) for s in sums])
    flights = [((res[i], res[n + i]), res[2 * n + i], res[3 * n + i]) for i in range(n)]
    return flights, res[-1]


def _chip_wait(sums, land, sems, after, name):
    def body(sums_ref, land_ref, send_sem, recv_sem, after_ref, sums_out, land_out):
        x, y, c = _mesh_pos()
        cp = _remote(sums_ref, land_ref, send_sem, recv_sem, (x, y, 1 - c))
        cp.wait_send()
        cp.wait_recv()

    res = pl.pallas_call(
        body, name=name, out_shape=(pltpu.HBM(sums.shape, sums.dtype), pltpu.HBM(land.shape, land.dtype)),
        in_specs=[HBM, HBM, SEM, SEM, ANY], out_specs=(HBM, HBM), input_output_aliases={0: 0, 1: 1},
        compiler_params=pltpu.CompilerParams(has_side_effects=FLOWS))(sums, land, sems[0], sems[1], after)
    return res[1]


class _CopyThrough:
    def __init__(self, src_ref, dst_ref, dst_row0, n_rows, buf, sem_in, sem_out):
        rc = n_rows // COPY_PIECES
        piece = lambda ref, o: ref.at[pl.ds(o, rc), :]
        self.loads = [pltpu.make_async_copy(piece(src_ref, k * rc), piece(buf, k * rc), sem_in) for k in range(COPY_PIECES)]
        self.stores = [pltpu.make_async_copy(piece(buf, k * rc), piece(dst_ref, dst_row0 + k * rc), sem_out) for k in range(COPY_PIECES)]
        self.all_in = pltpu.make_async_copy(src_ref, buf, sem_in)
        self.all_out = pltpu.make_async_copy(buf, dst_ref.at[pl.ds(dst_row0, n_rows), :], sem_out)

    def load(self):
        for cp in self.loads:
            cp.start()

    def store(self):
        self.all_in.wait()
        for cp in self.stores:
            cp.start()

    def done(self):
        self.all_out.wait()


def _gather_first(shards, sm, jobs, x2, tgt2, t_rows, x0):
    d = shards[0].shape[1]
    rows = [w.shape[0] for w in shards]
    lo = [sum(rows[:p]) for p in range(N_BIG)]
    n_sems = GATHER_SEMS * len(jobs)
    seq = x2.shape[0]
    assert x0 == ROW_ALIGN and seq % ROW_ALIGN == 0 and d == N_DEV * LANE

    def body(s0, s1, s2, s3, s4, sm_ref, x_ref, tgt_ref, wl_ref, o0, o1, o2, o3, o4, sa_ref, h0_ref, tp_ref,
             wl_v, x_v, tgt_v, heads_v, sa_v, send_sems, recv_sems, ssend, srecv, local_sems, sems_in, sems_out):
        dests = (o0, o1, o2, o3, o4)
        x, y, c = _mesh_pos()
        me = (x, y, c)
        jme = _dev_index(*me)
        peers = [(x, y, 1 - c)] + [(*chip, pc) for pc in (c, 1 - c) for chip in _other_chips(x, y)]
        barrier = pltpu.get_barrier_semaphore()
        for to in peers:
            pl.semaphore_signal(barrier, inc=1, device_id=to, device_id_type=MESH)
        pl.semaphore_wait(barrier, len(peers))
        padded = [_CopyThrough(x_ref, h0_ref, x0, seq, x_v, sems_in.at[0], sems_out.at[0]),
                  _CopyThrough(tgt_ref, tp_ref, x0, seq, tgt_v, sems_in.at[1], sems_out.at[1])]
        for cp in padded:
            cp.load()
        shard_refs = (s0, s1, s2, s3, s4)
        first = sorted({j[0] for j in jobs})
        for p in first + [p for p in range(N_BIG) if p not in first]:
            wl_v[pl.ds(lo[p], rows[p]), :] = shard_refs[p][...].astype(BF16)
            if p == first[-1]:
                gather = _Gather(jobs, rows, lo, wl_v, dict(enumerate(dests)), send_sems, recv_sems)
                gather.start()
        smalls = [_remote(sm_ref, sa_ref.at[jme], ssend.at[k], srecv.at[k], to) for k, to in enumerate(peers)]
        for cp in smalls:
            cp.start()
        mine = [pltpu.make_async_copy(wl_v.at[pl.ds(lo[p], rows[p]), :],
                                      dests[p].at[pl.ds(pl.multiple_of(jme * rows[p], BF16_ROWS), rows[p]), :], local_sems.at[p])
                for p in range(N_BIG)]
        mine.append(pltpu.make_async_copy(wl_v, wl_ref, local_sems.at[N_BIG]))
        mine.append(pltpu.make_async_copy(sm_ref, sa_ref.at[jme], local_sems.at[N_BIG + 1]))
        for cp in mine:
            cp.start()
        later = [p for p in range(N_BIG) if p not in {j[0] for j in jobs}]
        own = [_remote(wl_v.at[pl.ds(lo[p], rows[p]), :], dests[p].at[pl.ds(pl.multiple_of(jme * rows[p], BF16_ROWS), rows[p]), :],
                       ssend.at[7 + i], srecv.at[7 + i], (x, y, 1 - c)) for i, p in enumerate(later)]
        for cp in own:
            cp.start()
        gather.mid()
        for cp in padded:
            cp.store()
        for cp in smalls + own:
            cp.wait_recv()
        mine[-1].wait()
        to_v = pltpu.make_async_copy(sa_ref, sa_v, local_sems.at[N_BIG + 1])
        to_v.start()
        to_v.wait()
        head, zeros = heads_v.at[0], heads_v.at[1]
        head[...] = jnp.zeros_like(head)
        zeros[...] = jnp.zeros_like(zeros)
        for j in range(N_DEV):
            head[pl.ds(x0 - N_META, N_META), pl.ds(j * LANE, LANE)] = sa_v[j, pl.ds(0, N_META), :]
        heads = [pltpu.make_async_copy(head, h0_ref.at[pl.ds(0, x0), :], local_sems.at[N_BIG + 1]),
                 pltpu.make_async_copy(zeros, tp_ref.at[pl.ds(0, x0), :], local_sems.at[N_BIG + 2])]
        for cp in heads:
            cp.start()
        gather.finish()
        for cp in smalls + own:
            cp.wait_send()
        for cp in mine[:-1] + heads:
            cp.wait()
        for cp in padded:
            cp.done()

    out_shape = [jax.ShapeDtypeStruct((sum(rows), d), BF16)]
    out_shape += [jax.ShapeDtypeStruct((N_DEV * r, d), BF16) for r in rows]
    out_shape.append(jax.ShapeDtypeStruct((N_DEV,) + sm.shape, F32))
    out_shape += [jax.ShapeDtypeStruct((t_rows, d), F32)] * 2
    res = pl.pallas_call(
        body, out_shape=out_shape, in_specs=[VMEM] * 6 + [ANY] * 2, out_specs=[ANY] * 9,
        scratch_shapes=[pltpu.VMEM((sum(rows), d), BF16), pltpu.VMEM((seq, d), F32), pltpu.VMEM((seq, d), F32),
                        pltpu.VMEM((2, ROW_ALIGN, d), F32), pltpu.VMEM((N_DEV,) + sm.shape, F32),
                        pltpu.SemaphoreType.DMA((n_sems,)), pltpu.SemaphoreType.DMA((n_sems,)),
                        pltpu.SemaphoreType.DMA((7 + N_BIG,)), pltpu.SemaphoreType.DMA((7 + N_BIG,)),
                        pltpu.SemaphoreType.DMA((N_BIG + 3,)), pltpu.SemaphoreType.DMA((2,)), pltpu.SemaphoreType.DMA((2,))],
        name="gather_first",
        compiler_params=pltpu.CompilerParams(vmem_limit_bytes=VMEM_LIMIT, collective_id=ALL_PEERS_BARRIER_ID))(*shards, sm, x2, tgt2)
    return res[0], list(res[1:1 + N_BIG]), res[1 + N_BIG], res[2 + N_BIG], res[3 + N_BIG]


def _in_proj(h0, g1, win_t, tm, comm):
    t_rows, d = h0.shape
    e = win_t.shape[0]

    def body(h_ref, g_ref, w_ref, xn_ref, hin_ref):
        h = h_ref[...]
        xn = ((h * _rstd(h)) * g_ref[...]).astype(BF16)
        xn_ref[...] = xn
        for o, n in _chunks(e, N_CHUNK):
            hin_ref[:, pl.ds(o, n)] = _dot_nt(xn, w_ref[pl.ds(o, n), :])

    return _host_call(
        body, grid=(t_rows // tm,),
        in_specs=[pl.BlockSpec((tm, d), lambda i: (i, 0)), _full((1, d)), _resident((e, d))],
        out_specs=[pl.BlockSpec((tm, d), lambda i: (i, 0)), pl.BlockSpec((tm, e), lambda i: (i, 0))],
        out_shape=[jax.ShapeDtypeStruct((t_rows, d), BF16), jax.ShapeDtypeStruct((t_rows, e), F32)],
        args=(h0, g1, win_t), name="in_proj", comm=comm)


def _tap_slot(off):
    return off % SUBLANE, (off // SUBLANE) * SUBLANE


def _fill_shifted(sh_ref, base_ref, residues, n_rows):
    for r in residues:
        if r:
            sh_ref[r] = base_ref[pl.ds(r, n_rows), :]


def _shifted_rows(pair, r, start, n):
    base_ref, sh_ref = pair
    return base_ref[pl.ds(start, n), :] if r == 0 else sh_ref[r, pl.ds(start, n), :]


def _mix_conv_fwd(hin, wa, wb, bb, wa_w, comm):
    t_rows = hin.shape[0]
    nt = wa_w // LANE
    ka, kb = wa.shape[0], wb.shape[0]
    nr = CONV_HALO + t_rows

    def body(bg_ref, cg_ref, ha_ref, val_ref, gt_ref, wa_ref, wb_ref, bb_ref, ya_ref, z_ref, base, sh):
        base[pl.ds(0, CONV_HALO), :] = jnp.zeros((CONV_HALO, LANE), F32)
        base[pl.ds(nr, SUBLANE), :] = jnp.zeros((SUBLANE, LANE), F32)

        def conv(w_ref, k_taps, b, n):
            acc = None
            for k in range(k_taps):
                r, q = _tap_slot(CONV_HALO - (k_taps - 1) + k)
                term = w_ref[pl.ds(k, 1), :] * _shifted_rows((base, sh), r, b + q, n)
                acc = term if acc is None else acc + term
            return acc

        def fill_a(b, c):
            base[pl.ds(CONV_HALO + b, CONV_CHUNK), :] = cg_ref[pl.ds(b, CONV_CHUNK), :] * ha_ref[pl.ds(b, CONV_CHUNK), :]
            return c

        _row_loop(t_rows, CONV_CHUNK, fill_a)
        _fill_shifted(sh, base, sorted({_tap_slot(CONV_HALO - (ka - 1) + k)[0] for k in range(ka)}), nr)

        def out_a(b, c):
            ya_ref[pl.ds(b, CONV_CHUNK), :] = (bg_ref[pl.ds(b, CONV_CHUNK), :] * conv(wa_ref, ka, b, CONV_CHUNK)).astype(BF16)
            return c

        _row_loop(t_rows, CONV_CHUNK, out_a)

        def fill_b(b, c):
            base[pl.ds(CONV_HALO + b, CONV_CHUNK), :] = (val_ref[pl.ds(b, CONV_CHUNK), :]
                                                          * jax.nn.sigmoid(gt_ref[pl.ds(b, CONV_CHUNK), :]))
            return c

        _row_loop(t_rows, CONV_CHUNK, fill_b)
        _fill_shifted(sh, base, range(SUBLANE), nr)

        def out_b(b, c):
            z_ref[pl.ds(b, CONV_CHUNK), :] = conv(wb_ref, kb, b, CONV_CHUNK) + bb_ref[...]
            return c

        _row_loop(t_rows, CONV_CHUNK, out_b)

    def col(g):
        return pl.BlockSpec((t_rows, LANE), lambda i, g=g: (0, g * nt + i))

    tile = lambda rows: pl.BlockSpec((rows, LANE), lambda i: (0, i))
    return _host_call(
        body, grid=(nt,),
        in_specs=[col(0), col(1), col(2), col(3), col(4), tile(ka), tile(kb), tile(1)],
        out_specs=[tile(t_rows), tile(t_rows)],
        out_shape=[jax.ShapeDtypeStruct((t_rows, wa_w), BF16), jax.ShapeDtypeStruct((t_rows, wa_w), F32)],
        scratch_shapes=[pltpu.VMEM((nr + SUBLANE, LANE), F32), pltpu.VMEM((SUBLANE, nr, LANE), F32)],
        args=(hin, hin, hin, hin, hin, wa, wb, bb), name="mix_conv_fwd", comm=comm)


def _ln_parts(z, lg, lb):
    mu = jnp.mean(z, axis=-1, keepdims=True)
    zc = z - mu
    rstd = lax.rsqrt(jnp.mean(zc * zc, axis=-1, keepdims=True) + LN_EPS)
    zh = zc * rstd
    return zh, rstd, zh * lg + lb


def _out_proj(ya, z, lg, lb, w_out, h0, g2, g3, tm, comm):
    t_rows, d = h0.shape
    w = z.shape[1]

    def body(ya_ref, z_ref, lg_ref, lb_ref, w_ref, h0_ref, g2_ref, g3_ref, y_ref, mix_ref, h1_ref, xn2_ref):
        _, _, ln = _ln_parts(z_ref[...], lg_ref[...], lb_ref[...])
        y_ref[:, pl.ds(0, w)] = ya_ref[...]
        y_ref[:, pl.ds(w, w)] = (ln * jax.nn.sigmoid(ln)).astype(BF16)
        mix = _dot_nn(y_ref[...], w_ref[...])
        mix_ref[...] = mix
        h1 = h0_ref[...] + (mix * _rstd(mix)) * g2_ref[...]
        h1_ref[...] = h1
        xn2_ref[...] = ((h1 * _rstd(h1)) * g3_ref[...]).astype(BF16)

    blk = pl.BlockSpec((tm, d), lambda i: (i, 0))
    half = pl.BlockSpec((tm, w), lambda i: (i, 0))
    return _host_call(
        body, grid=(t_rows // tm,),
        in_specs=[half, half, _full((1, w)), _full((1, w)), _resident(w_out.shape), blk, _full((1, d)), _full((1, d))],
        out_specs=[blk, blk, blk, blk],
        out_shape=[jax.ShapeDtypeStruct((t_rows, d), BF16), jax.ShapeDtypeStruct((t_rows, d), F32),
                   jax.ShapeDtypeStruct((t_rows, d), F32), jax.ShapeDtypeStruct((t_rows, d), BF16)],
        args=(ya, z, lg, lb, w_out, h0, g2, g3), name="out_proj", comm=comm)


def _gate_up(xn2, wg_t, wu_t, tm, comm):
    t_rows, d = xn2.shape
    f = wg_t.shape[0]

    def body(x_ref, wg_ref, wu_ref, ga_ref, gu_ref, s_ref):
        xn = x_ref[...]
        for o, n in _chunks(f, N_CHUNK):
            a = _dot_nt(xn, wg_ref[pl.ds(o, n), :])
            u = _dot_nt(xn, wu_ref[pl.ds(o, n), :])
            sig = jax.nn.sigmoid(a)
            silu = a * sig
            s = silu * u
            gu_ref[:, pl.ds(o, n)] = silu.astype(BF16)
            ga_ref[:, pl.ds(o, n)] = ((u - s) * sig + s).astype(BF16)
            s_ref[:, pl.ds(o, n)] = s.astype(BF16)

    blk = pl.BlockSpec((tm, f), lambda i: (i, 0))
    return _host_call(
        body, grid=(t_rows // tm,),
        in_specs=[pl.BlockSpec((tm, d), lambda i: (i, 0)), _resident((f, d)), _resident((f, d))],
        out_specs=[blk, blk, blk], out_shape=[jax.ShapeDtypeStruct((t_rows, f), BF16)] * 3,
        args=(xn2, wg_t, wu_t), name="gate_up", comm=comm)


def _down_loss(s, wd, h1, tgt, g4, tm, x0):
    t_rows, d = h1.shape
    f = wd.shape[0]

    def body(s_ref, w_ref, h1_ref, tgt_ref, g4_ref, dh2_ref, dff_ref, dg4_ref, loss_ref):
        i = pl.program_id(0)
        ff = _dot_nn(s_ref[...], w_ref[...])
        r4 = _rstd(ff)
        fh = ff * r4
        g4 = g4_ref[...]
        h2 = h1_ref[...] + fh * g4
        row = i * tm + lax.broadcasted_iota(jnp.int32, (tm, 1), 0)
        diff = jnp.where(row >= x0, h2 - tgt_ref[...], 0.0)
        dh2 = diff / d
        dh2_ref[...] = dh2
        dff_ref[...] = _rms_bwd(dh2 * g4, fh, r4).astype(BF16)
        _acc_rows(dg4_ref, dh2 * fh, i == 0)
        _acc_rows(loss_ref, diff * diff, i == 0)

    blk = pl.BlockSpec((tm, d), lambda i: (i, 0))
    res, _ = _host_call(
        body, grid=(t_rows // tm,),
        in_specs=[pl.BlockSpec((tm, f), lambda i: (i, 0)), _resident((f, d)), blk, blk, _full((1, d))],
        out_specs=[blk, blk, _full((1, d)), _full((1, d))],
        out_shape=[jax.ShapeDtypeStruct((t_rows, d), F32), jax.ShapeDtypeStruct((t_rows, d), BF16),
                   jax.ShapeDtypeStruct((1, d), F32), jax.ShapeDtypeStruct((1, d), F32)],
        args=(s, wd, h1, tgt, g4), name="down_loss")
    return res


def _bwd_down(dff, wd, ga, gu, tm, comm):
    t_rows, d = dff.shape
    f = wd.shape[0]

    def body(dff_ref, w_ref, ga_ref, gu_ref, dau_ref):
        dff_v = dff_ref[...]
        for o, n in _chunks(f, N_CHUNK):
            ds = _dot_nt(dff_v, w_ref[pl.ds(o, n), :]).astype(BF16)
            dau_ref[0, :, pl.ds(o, n)] = ds * ga_ref[:, pl.ds(o, n)]
            dau_ref[1, :, pl.ds(o, n)] = ds * gu_ref[:, pl.ds(o, n)]

    blk = pl.BlockSpec((tm, f), lambda i: (i, 0))
    (dau,), extra = _host_call(
        body, grid=(t_rows // tm,),
        in_specs=[pl.BlockSpec((tm, d), lambda i: (i, 0)), _resident((f, d)), blk, blk],
        out_specs=[pl.BlockSpec((2, tm, f), lambda i: (0, i, 0))], out_shape=[jax.ShapeDtypeStruct((2, t_rows, f), BF16)],
        args=(dff, wd, ga, gu), name="bwd_down", comm=comm)
    return dau, extra


def _wgrad(a, b, name, after=()):
    d = b.shape[1]
    t_rows = b.shape[0]
    stacked = a.ndim == 3
    n = a.shape[-1]
    groups = a.shape[0] if stacked else 1
    steps = 1 if stacked else 2
    tile = max(t for t in range(LANE, min(n // steps, WGRAD_TILE_MAX) + 1, LANE) if n % t == 0)
    tiles = n // tile

    def body(a_ref, b_ref, o_ref):
        o_ref[...] = lax.dot_general(a_ref[...], b_ref[...], (((0,), (0,)), ((), ())),
                                     preferred_element_type=F32).astype(BF16)

    if stacked:
        a_spec = pl.BlockSpec((None, t_rows, tile), lambda g, i: (g, 0, i))
    else:
        a_spec = pl.BlockSpec((t_rows, tile), lambda g, i: (0, i))
    res, _ = _host_call(
        body, grid=(groups, tiles), in_specs=[a_spec, _resident((t_rows, d))],
        out_specs=[pl.BlockSpec((tile, d), lambda g, i: (g * tiles + i, 0))],
        out_shape=[jax.ShapeDtypeStruct((groups * n, d), BF16)], args=(a, b), name=name, after=after)
    return res[0]


def _bwd_ffn_in(dau, wg_t, wu_t, h1, dh2, g3, tm, comm):
    t_rows, d = h1.shape
    f = wg_t.shape[0]

    def body(dau_ref, wg_ref, wu_ref, h1_ref, dh2_ref, g3_ref, dh1_ref, dg3_ref):
        dxn2 = _dot_nn(dau_ref[0], wg_ref[...]) + _dot_nn(dau_ref[1], wu_ref[...])
        h1 = h1_ref[...]
        r3 = _rstd(h1)
        h1h = h1 * r3
        _acc_rows(dg3_ref, dxn2 * h1h, pl.program_id(0) == 0)
        dh1_ref[...] = dh2_ref[...] + _rms_bwd(dxn2 * g3_ref[...], h1h, r3)

    blk = pl.BlockSpec((tm, d), lambda i: (i, 0))
    return _host_call(
        body, grid=(t_rows // tm,),
        in_specs=[pl.BlockSpec((2, tm, f), lambda i: (0, i, 0)), _resident((f, d)), _resident((f, d)), blk, blk, _full((1, d))],
        out_specs=[blk, _full((1, d))],
        out_shape=[jax.ShapeDtypeStruct((t_rows, d), F32), jax.ShapeDtypeStruct((1, d), F32)],
        args=(dau, wg_t, wu_t, h1, dh2, g3), name="bwd_ffn_in", comm=comm)


def _bwd_out_proj(dh1, mix, w_out, g2, z, lg, lb, tm, after):
    t_rows, d = dh1.shape
    w = z.shape[1]

    def body(dh1_ref, mix_ref, w_ref, g2_ref, z_ref, lg_ref, lb_ref, dmix_ref, dya_ref, dz_ref, dg2_ref, dlg_ref, dlb_ref, dbb_ref):
        first = pl.program_id(0) == 0
        mix = mix_ref[...]
        r2 = _rstd(mix)
        mh = mix * r2
        dh1 = dh1_ref[...]
        _acc_rows(dg2_ref, dh1 * mh, first)
        dmix = _rms_bwd(dh1 * g2_ref[...], mh, r2).astype(BF16)
        dmix_ref[...] = dmix
        dy = _dot_nt(dmix, w_ref[...])
        dya_ref[...] = dy[:, :w]
        lg = lg_ref[...]
        zh, rstd, ln = _ln_parts(z_ref[...], lg, lb_ref[...])
        dln = dy[:, w:] * _silu_grad(ln, jax.nn.sigmoid(ln))
        _acc_rows(dlg_ref, dln * zh, first)
        _acc_rows(dlb_ref, dln, first)
        dzh = dln * lg
        dz = rstd * (dzh - jnp.mean(dzh, axis=-1, keepdims=True) - zh * jnp.mean(dzh * zh, axis=-1, keepdims=True))
        dz_ref[...] = dz
        _acc_rows(dbb_ref, dz, first)

    blk = pl.BlockSpec((tm, d), lambda i: (i, 0))
    half = pl.BlockSpec((tm, w), lambda i: (i, 0))
    vec = _full((1, w))
    res, _ = _host_call(
        body, grid=(t_rows // tm,), in_specs=[blk, blk, _resident(w_out.shape), _full((1, d)), half, vec, vec],
        out_specs=[blk, half, half, _full((1, d)), vec, vec, vec],
        out_shape=[jax.ShapeDtypeStruct((t_rows, d), BF16), jax.ShapeDtypeStruct((t_rows, w), F32),
                   jax.ShapeDtypeStruct((t_rows, w), F32), jax.ShapeDtypeStruct((1, d), F32)]
        + [jax.ShapeDtypeStruct((1, w), F32)] * 3,
        args=(dh1, mix, w_out, g2, z, lg, lb), name="bwd_out_proj", after=after)
    return res


def _mix_conv_bwd(hin, dy, dz, wa, wb, wa_w, comm):
    t_rows = hin.shape[0]
    nt = wa_w // LANE
    ka, kb = wa.shape[0], wb.shape[0]
    nr = CONV_HALO + t_rows
    kb_rows = -(-kb // SUBLANE) * SUBLANE

    def body(bg_ref, cg_ref, ha_ref, val_ref, gt_ref, dya_ref, dz_ref, wa_ref, wb_ref,
             dh_ref, dwa_ref, dwb_ref, base, sh, based, shd, tmp, wbc):
        zeros = lambda n: jnp.zeros((n, LANE), F32)
        base[pl.ds(0, CONV_HALO), :] = zeros(CONV_HALO)
        base[pl.ds(nr, SUBLANE), :] = zeros(SUBLANE)
        based[pl.ds(t_rows, CONV_HALO + SUBLANE), :] = zeros(CONV_HALO + SUBLANE)

        def fwd_slot(k_taps, k):
            return _tap_slot(CONV_HALO - (k_taps - 1) + k)

        def bwd_slot(k_taps, k):
            return _tap_slot(k_taps - 1 - k)

        def conv(w_ref, k_taps, src, slot, b, n):
            acc = None
            for k in range(k_taps):
                r, q = slot(k_taps, k)
                term = w_ref[pl.ds(k, 1), :] * _shifted_rows(src, r, b + q, n)
                acc = term if acc is None else acc + term
            return acc

        def by_residue(k_taps, slot):
            groups = {}
            for k in range(k_taps):
                r, q = slot(k_taps, k)
                groups.setdefault(r, []).append((k, q // SUBLANE))
            return groups

        def wgrad_loop(w_ref, k_taps):
            n_sub = WGRAD_ROWS // SUBLANE
            for k in range(k_taps):
                wbc[k] = jnp.broadcast_to(w_ref[pl.ds(k, 1), :], (SUBLANE, LANE))
            fwd, bwd = by_residue(k_taps, fwd_slot), by_residue(k_taps, bwd_slot)

            def window(src, r, taps, b):
                span = n_sub + max(qi for _, qi in taps)
                return [_shifted_rows(src, r, b + SUBLANE * i, SUBLANE) for i in range(span)]

            def step(b, accs):
                accs = list(accs)
                dv = [based[pl.ds(b + SUBLANE * j, SUBLANE), :] for j in range(n_sub)]
                for r, taps in fwd.items():
                    win = window((base, sh), r, taps, b)
                    for k, qi in taps:
                        t = dv[0] * win[qi]
                        for j in range(1, n_sub):
                            t = t + dv[j] * win[qi + j]
                        accs[k] = accs[k] + t
                outs = [None] * n_sub
                for r, taps in bwd.items():
                    win = window((based, shd), r, taps, b)
                    for k, qi in taps:
                        wk = wbc[k]
                        for j in range(n_sub):
                            term = wk * win[qi + j]
                            outs[j] = term if outs[j] is None else outs[j] + term
                for j in range(n_sub):
                    tmp[pl.ds(b + SUBLANE * j, SUBLANE), :] = outs[j]
                return tuple(accs)

            return _row_loop(t_rows, WGRAD_ROWS, step, tuple(zeros(SUBLANE) for _ in range(k_taps)))

        def store_taps(ref, accs, rows):
            for k, acc in enumerate(accs):
                ref[pl.ds(k, 1), :] = jnp.sum(acc, axis=0, keepdims=True)
            if rows > len(accs):
                ref[pl.ds(len(accs), rows - len(accs)), :] = zeros(rows - len(accs))

        def fill_a(b, c):
            sl = pl.ds(b, CONV_CHUNK)
            base[pl.ds(CONV_HALO + b, CONV_CHUNK), :] = cg_ref[sl, :] * ha_ref[sl, :]
            based[sl, :] = dya_ref[sl, :] * bg_ref[sl, :]
            return c

        _row_loop(t_rows, CONV_CHUNK, fill_a)
        _fill_shifted(sh, base, sorted({fwd_slot(ka, k)[0] for k in range(ka)}), nr)
        _fill_shifted(shd, based, sorted({bwd_slot(ka, k)[0] for k in range(ka)}), nr)

        def d_bgate(b, c):
            sl = pl.ds(b, CONV_CHUNK)
            dh_ref[0, sl, :] = (dya_ref[sl, :] * conv(wa_ref, ka, (base, sh), fwd_slot, b, CONV_CHUNK)).astype(BF16)
            return c

        _row_loop(t_rows, CONV_CHUNK, d_bgate)
        store_taps(dwa_ref, wgrad_loop(wa_ref, ka), SUBLANE)

        def d_ch(b, c):
            sl = pl.ds(b, CONV_CHUNK)
            dua = tmp[sl, :]
            dh_ref[1, sl, :] = (dua * ha_ref[sl, :]).astype(BF16)
            dh_ref[2, sl, :] = (dua * cg_ref[sl, :]).astype(BF16)
            return c

        _row_loop(t_rows, CONV_CHUNK, d_ch)

        def fill_b(b, c):
            sl = pl.ds(b, CONV_CHUNK)
            base[pl.ds(CONV_HALO + b, CONV_CHUNK), :] = val_ref[sl, :] * jax.nn.sigmoid(gt_ref[sl, :])
            based[sl, :] = dz_ref[sl, :]
            return c

        _row_loop(t_rows, CONV_CHUNK, fill_b)
        _fill_shifted(sh, base, range(SUBLANE), nr)
        _fill_shifted(shd, based, range(SUBLANE), nr)
        store_taps(dwb_ref, wgrad_loop(wb_ref, kb), kb_rows)

        def d_glu(b, c):
            sl = pl.ds(b, CONV_CHUNK)
            dgg = tmp[sl, :]
            sig = jax.nn.sigmoid(gt_ref[sl, :])
            dh_ref[3, sl, :] = (dgg * sig).astype(BF16)
            dh_ref[4, sl, :] = (dgg * val_ref[sl, :] * (sig * (1.0 - sig))).astype(BF16)
            return c

        _row_loop(t_rows, CONV_CHUNK, d_glu)

    def col(g):
        return pl.BlockSpec((t_rows, LANE), lambda i, g=g: (0, g * nt + i))

    tile = lambda rows: pl.BlockSpec((rows, LANE), lambda i: (0, i))
    return _host_call(
        body, grid=(nt,),
        in_specs=[col(0), col(1), col(2), col(3), col(4), tile(t_rows), tile(t_rows), tile(ka), tile(kb)],
        out_specs=[pl.BlockSpec((5, t_rows, LANE), lambda i: (0, 0, i)), tile(SUBLANE), tile(kb_rows)],
        out_shape=[jax.ShapeDtypeStruct((5, t_rows, wa_w), BF16), jax.ShapeDtypeStruct((SUBLANE, wa_w), F32),
                   jax.ShapeDtypeStruct((kb_rows, wa_w), F32)],
        scratch_shapes=[pltpu.VMEM((nr + SUBLANE, LANE), F32), pltpu.VMEM((SUBLANE, nr, LANE), F32),
                        pltpu.VMEM((nr + SUBLANE, LANE), F32), pltpu.VMEM((SUBLANE, nr, LANE), F32),
                        pltpu.VMEM((t_rows, LANE), F32), pltpu.VMEM((kb_rows, SUBLANE, LANE), F32)],
        args=(hin, hin, hin, hin, hin, dy, dz, wa, wb), name="mix_conv_bwd", comm=comm)


def _bwd_in_proj(dh5, win_t, h0, dh1, g1, tm, comm):
    t_rows, d = h0.shape
    groups, _, w = dh5.shape

    def body(dh_ref, w_ref, h0_ref, dh1_ref, g1_ref, dh0_ref, dg1_ref):
        dxn1 = None
        for g in range(groups):
            part = _dot_nn(dh_ref[g], w_ref[pl.ds(g * w, w), :])
            dxn1 = part if dxn1 is None else dxn1 + part
        h0 = h0_ref[...]
        r1 = _rstd(h0)
        h0h = h0 * r1
        _acc_rows(dg1_ref, dxn1 * h0h, pl.program_id(0) == 0)
        dh0_ref[...] = dh1_ref[...] + _rms_bwd(dxn1 * g1_ref[...], h0h, r1)

    blk = pl.BlockSpec((tm, d), lambda i: (i, 0))
    return _host_call(
        body, grid=(t_rows // tm,),
        in_specs=[pl.BlockSpec((groups, tm, w), lambda i: (0, i, 0)), _resident(win_t.shape), blk, blk, _full((1, d))],
        out_specs=[blk, _full((1, d))],
        out_shape=[jax.ShapeDtypeStruct((t_rows, d), F32), jax.ShapeDtypeStruct((1, d), F32)],
        args=(dh5, win_t, h0, dh1, g1), name="bwd_in_proj", comm=comm)


def _pair_small(smalls, d):
    (dmeta, dg1, dg2, dg3, dg4, dbb, dlg, dlb, lossv, dwa, dwb) = smalls
    half = d // 2
    kb_rows = dwb.shape[0]

    def body(dmeta_ref, dg1_ref, dg2_ref, dg3_ref, dg4_ref, dbb_ref, dlg_ref, dlb_ref, loss_ref, dwa_ref, dwb_ref,
             sums_ref, pbuf, psib, ps_send, ps_recv):
        x, y, c = _mesh_pos()
        _pair_handshake()
        pbuf[...] = jnp.zeros_like(pbuf)
        pbuf[pl.ds(0, N_META), :] = dmeta_ref[...]
        for row, ref in ((16, dg1_ref), (17, dg2_ref), (18, dg3_ref), (19, dg4_ref)):
            pbuf[pl.ds(row, 1), :] = ref[...]
        pbuf[pl.ds(20, 1), pl.ds(0, half)] = dbb_ref[...]
        pbuf[pl.ds(20, 1), pl.ds(half, half)] = dlg_ref[...]
        pbuf[pl.ds(21, 1), pl.ds(0, half)] = dlb_ref[...]
        lv = loss_ref[...]
        pbuf[pl.ds(21, 1), pl.ds(half, half)] = lv[:, :half] + lv[:, half:]
        pbuf[pl.ds(24, SUBLANE), pl.ds(0, half)] = dwa_ref[...]
        pbuf[pl.ds(32, kb_rows), pl.ds(0, half)] = dwb_ref[...]
        to_sib = _remote(pbuf, psib, ps_send.at[0], ps_recv.at[0], (x, y, 1 - c))
        to_sib.start()
        to_sib.wait_recv()
        s = pbuf[...] + psib[...]
        for k in range(3):
            sums_ref[k] = s
        to_sib.wait_send()

    return pl.pallas_call(
        body, out_shape=jax.ShapeDtypeStruct((3, SMALL_ROWS, d), F32), in_specs=[VMEM] * 11, out_specs=VMEM,
        scratch_shapes=[pltpu.VMEM((SMALL_ROWS, d), F32), pltpu.VMEM((SMALL_ROWS, d), F32),
                        pltpu.SemaphoreType.DMA((1,)), pltpu.SemaphoreType.DMA((1,))],
        name="pair_small", compiler_params=pltpu.CompilerParams(vmem_limit_bytes=VMEM_LIMIT, collective_id=PAIR_BARRIER_ID))(*smalls)


def _total_small(own, others):
    _, r, d = own.shape

    def body(own_ref, others_ref, tot_ref, chip_p):
        x, y, _ = _mesh_pos()
        chip_p[2 * x + y] = own_ref[0]
        for k, (cx, cy) in enumerate(_other_chips(x, y)):
            chip_p[2 * cx + cy] = others_ref[k]
        tot_ref[...] = ((chip_p[0] + chip_p[1]) + chip_p[2]) + chip_p[3]

    return pl.pallas_call(body, out_shape=jax.ShapeDtypeStruct((r, d), F32), scratch_shapes=[pltpu.VMEM((4, r, d), F32)],
                          name="total_small", compiler_params=_cparams())(own, others)


def _adamw(w, g, m, v):
    m = ADAM_B1 * m + (1.0 - ADAM_B1) * g
    v = ADAM_B2 * v + (1.0 - ADAM_B2) * jnp.square(g)
    m_hat = m / (1.0 - ADAM_B1 ** ADAM_STEP)
    v_hat = v / (1.0 - ADAM_B2 ** ADAM_STEP)
    delta = -ADAM_LR * (m_hat / (jnp.sqrt(v_hat) + ADAM_EPS) + ADAM_WD * w)
    return delta, m, v


def _adam_big(g, pair, part, w, m, v, name, row0=0):
    r, d = w.shape
    cols = d // ADAM_COL_BLOCKS
    assert row0 % r == 0

    def body(me_ref, g_ref, pair_ref, part_ref, w_ref, m_ref, v_ref, go_ref, d_ref, mo_ref, vo_ref):
        g = g_ref[...].astype(F32) + pair_ref[...].astype(F32)
        for k in range(3):
            g = g + part_ref[k].astype(F32)
        go_ref[...] = g
        d_ref[...], mo_ref[...], vo_ref[...] = _adamw(w_ref[...], g, m_ref[...], v_ref[...])

    blk = pl.BlockSpec((r, cols), lambda i, me_ref: (0, i))
    grid_spec = pltpu.PrefetchScalarGridSpec(
        num_scalar_prefetch=1, grid=(ADAM_COL_BLOCKS,),
        in_specs=[pl.BlockSpec((r, cols), lambda i, me_ref: (me_ref[0], i)),
                  pl.BlockSpec((None, r, cols), lambda i, me_ref: (0, 0, i)),
                  pl.BlockSpec((3, r, cols), lambda i, me_ref: (0, 0, i)), blk, blk, blk],
        out_specs=[blk, blk, blk, blk])
    me = jnp.reshape(_dev_index(*_mesh_pos()) + row0 // r, (1,)).astype(jnp.int32)
    return pl.pallas_call(body, out_shape=[jax.ShapeDtypeStruct((r, d), F32)] * 4, grid_spec=grid_spec, name=name,
                          compiler_params=_cparams(1))(me, g, pair, part, w, m, v)


def _adam_small(gs, ws, ms, vs):
    n = len(gs)

    def body(*refs):
        ins, outs = refs[:4 * n], refs[4 * n:]
        for i in range(n):
            g = ins[i][...]
            delta, m, v = _adamw(ins[n + i][...], g, ins[2 * n + i][...], ins[3 * n + i][...])
            outs[i][...] = delta
            outs[n + i][...] = m
            outs[2 * n + i][...] = v

    shapes = [jax.ShapeDtypeStruct(w.shape, F32) for w in ws]
    return pl.pallas_call(body, out_shape=shapes * 3, name="adam_small", compiler_params=_cparams())(*gs, *ws, *ms, *vs)


def kernel(x, meta_tokens, pre_mix_norm, w_in, conv_a_w, conv_b_w, conv_b_bias, ln_b_gain, ln_b_bias, w_out, post_mix_norm, pre_ffn_norm, w_gate, w_up, w_down, post_ffn_norm, loss_target, m_meta_tokens, m_pre_mix_norm, m_w_in, m_conv_a_w, m_conv_b_w, m_conv_b_bias, m_ln_b_gain, m_ln_b_bias, m_w_out, m_post_mix_norm, m_pre_ffn_norm, m_w_gate, m_w_up, m_w_down, m_post_ffn_norm, v_meta_tokens, v_pre_mix_norm, v_w_in, v_conv_a_w, v_conv_b_w, v_conv_b_bias, v_ln_b_gain, v_ln_b_bias, v_w_out, v_post_mix_norm, v_pre_ffn_norm, v_w_gate, v_w_up, v_w_down, v_post_ffn_norm):
    _, seq, d = x.shape
    ka, ca_loc = conv_a_w.shape[1:]
    kb, cb_loc = conv_b_w.shape[1:]
    wa_w = ca_loc * N_DEV
    assert cb_loc == ca_loc and wa_w % LANE == 0 and w_in.shape[2] * N_DEV == 5 * wa_w and 2 * wa_w == d
    pad = (-(N_META + seq)) % ROW_ALIGN
    x0 = pad + N_META
    t_rows = x0 + seq
    assert t_rows % (N_ROW_BLOCKS * BF16_ROWS) == 0 and t_rows % CONV_CHUNK == 0 and d % LANE == 0
    tm = t_rows // N_ROW_BLOCKS
    tm2 = t_rows // 2
    me = _dev_index(*_mesh_pos())

    def as_rows(w_in_like, w_out_like, w_gate_like, w_up_like, w_down_like):
        return (w_in_like[0].T, w_out_like[0], w_gate_like[0].T, w_up_like[0].T, w_down_like[0])

    w_loc = as_rows(w_in, w_out, w_gate, w_up, w_down)
    rows = [w.shape[0] for w in w_loc]
    assert all(r % ADD_CHUNK == 0 for r in rows)
    P_IN, P_OUT, P_GATE, P_UP, P_DOWN = range(N_BIG)

    sm = jnp.zeros((SM_ROWS, LANE), F32)
    sm = sm.at[0:N_META, :].set(meta_tokens)
    sm = sm.at[16:16 + ka, 0:ca_loc].set(conv_a_w[0])
    sm = sm.at[24:24 + kb, 0:cb_loc].set(conv_b_w[0])
    wl, wfull, sm_all, h0, tgt = _gather_first(w_loc, sm, [(P_IN, 0, rows[P_IN])], x[0], loss_target[0], t_rows, x0)
    wa =jnp.transpose(sm_all[:, 16:16 + ka, 0:ca_loc], (1, 0, 2)).reshape(ka, wa_w)
    wb = jnp.transpose(sm_all[:, 24:24 + kb, 0:cb_loc], (1, 0, 2)).reshape(kb, wa_w)

    later = (P_OUT, P_GATE, P_UP, P_DOWN)
    sems, wl, started, _ = _gather_start(wl, wfull, later, rows, START_BARRIER_IDS[0])
    for p, arr in zip(later, started):
        wfull[p] = arr

    def arrived(p, after, name):
        nonlocal wl
        wl, wfull[p] = _gather_wait(wl, wfull[p], sems[later.index(p)], after, rows[p], name)
        return _forward_comm(wfull[p], rows[p])

    (xn1, hin), _ = _in_proj(h0, pre_mix_norm, wfull[P_IN], tm, None)
    (ya, z), (wfull[P_OUT],) = _mix_conv_fwd(hin, wa, wb, conv_b_bias, wa_w, arrived(P_OUT, hin, "gather_wait_out"))
    (y, mix, h1, xn2), (wfull[P_GATE],) = _out_proj(ya, z, ln_b_gain, ln_b_bias, wfull[P_OUT], h0, post_mix_norm, pre_ffn_norm, tm2,
                                                    arrived(P_GATE, z, "gather_wait_gate"))
    arrived(P_UP, xn2, "gather_wait_up")
    wfull[P_UP] = _forward_now(wfull[P_UP], rows[P_UP], "forward_up")
    (ga, gu, s), _ = _gate_up(xn2, wfull[P_GATE], wfull[P_UP], tm, None)
    arrived(P_DOWN, s, "gather_wait_down")
    wfull[P_DOWN] = _forward_now(wfull[P_DOWN], rows[P_DOWN], "forward_down")
    dh2, dff, dg4, lossv = _down_loss(s, wfull[P_DOWN], h1, tgt, post_ffn_norm, tm, x0)

    gwd = _wgrad(s, dff, "wgrad_down")
    dau, (pair_d,) = _bwd_down(dff, wfull[P_DOWN], ga, gu, tm, _pair_comm(gwd, rows[P_DOWN]))
    (flight_d,), token = _chip_start([_pair_sum(gwd, pair_d, rows[P_DOWN], "pair_sum_down")], "chip_start_down", START_BARRIER_IDS[1])
    gw_gu = _wgrad(dau, xn2, "wgrad_gate_up", [token])
    up0 = N_DEV * rows[P_GATE]
    (dh1, dg3), (pair_g, pair_u) = _bwd_ffn_in(dau, wfull[P_GATE], wfull[P_UP], h1, dh2, pre_ffn_norm, tm,
                                               _merge_comms([_pair_comm(gw_gu, rows[P_GATE]), _pair_comm(gw_gu, rows[P_UP], up0)]))
    (flight_g, flight_u), token = _chip_start([_pair_sum(gw_gu, pair_g, rows[P_GATE], "pair_sum_gate"),
                                               _pair_sum(gw_gu, pair_u, rows[P_UP], "pair_sum_up", row0=up0)], "chip_start_gate_up",
                                              START_BARRIER_IDS[2])
    dmix, dya, dz, dg2, dlg, dlb, dbb = _bwd_out_proj(dh1, mix, wfull[P_OUT], post_mix_norm, z, ln_b_gain, ln_b_bias, tm, [token])
    gwo = _wgrad(y, dmix, "wgrad_out")
    (dh5, dwa, dwb), (pair_o,) = _mix_conv_bwd(hin, dya, dz, wa, wb, wa_w, _pair_comm(gwo, rows[P_OUT]))
    (flight_o,), token = _chip_start([_pair_sum(gwo, pair_o, rows[P_OUT], "pair_sum_out")], "chip_start_out", START_BARRIER_IDS[3])
    gwi = _wgrad(dh5, xn1, "wgrad_in", [token])
    (dh0, dg1), (pair_i,) = _bwd_in_proj(dh5, wfull[P_IN], h0, dh1, pre_mix_norm, tm, _pair_comm(gwi, rows[P_IN]))
    grad_x = dh0[x0:][None]
    dmeta = dh0[x0 - N_META:x0]
    small_sums = _pair_small((dmeta, dg1, dg2, dg3, dg4, dbb, dlg, dlb, lossv, dwa, dwb), d)
    (flight_s,), token = _chip_start([small_sums], "chip_start_small", START_BARRIER_IDS[4])
    (flight_i,), token = _chip_start([_pair_sum(gwi, pair_i, rows[P_IN], "pair_sum_in", [token])], "chip_start_in",
                                     START_BARRIER_IDS[5])

    def landed(flight, after, tag):
        sems_p, sums, land = flight
        return _chip_wait(sums, land, sems_p, after, "chip_wait_" + tag)

    m_loc = as_rows(m_w_in, m_w_out, m_w_gate, m_w_up, m_w_down)
    v_loc = as_rows(v_w_in, v_w_out, v_w_gate, v_w_up, v_w_down)
    full_grads = {P_IN: gwi, P_OUT: gwo, P_GATE: gw_gu, P_UP: gw_gu, P_DOWN: gwd}
    pairs = {P_IN: pair_i, P_OUT: pair_o, P_GATE: pair_g, P_UP: pair_u, P_DOWN: pair_d}
    flights = {P_IN: flight_i, P_OUT: flight_o, P_GATE: flight_g, P_UP: flight_u, P_DOWN: flight_d}
    names = {P_IN: "w_in", P_OUT: "w_out", P_GATE: "w_gate", P_UP: "w_up", P_DOWN: "w_down"}
    bigs = {}

    def adam_big(p, after):
        part = landed(flights[p], after, names[p])
        res = _adam_big(full_grads[p], pairs[p], part, w_loc[p], m_loc[p], v_loc[p], "adam_" + names[p], up0 if p == P_UP else 0)
        bigs[names[p]] = [(o.T if p in (P_IN, P_GATE, P_UP) else o)[None] for o in res]
        return res[1]

    for p in (P_DOWN, P_GATE, P_UP, P_OUT):
        token = adam_big(p, token)
    ptot = _total_small(flight_s[1], landed(flight_s, token, "small"))
    half = d // 2
    loss = (0.5 / d) * jnp.sum(ptot[21, half:])
    g_meta = lax.dynamic_slice(ptot, (0, me * (d // N_DEV)), (N_META, d // N_DEV))
    g_small = [g_meta, ptot[16:17], lax.dynamic_slice(ptot, (24, me * ca_loc), (ka, ca_loc))[None],
               lax.dynamic_slice(ptot, (32, me * cb_loc), (kb, cb_loc))[None],
               ptot[20:21, :half], ptot[20:21, half:], ptot[21:22, :half], ptot[17:18], ptot[18:19], ptot[19:20]]
    w_small = [meta_tokens, pre_mix_norm, conv_a_w, conv_b_w, conv_b_bias, ln_b_gain, ln_b_bias, post_mix_norm,
               pre_ffn_norm, post_ffn_norm]
    m_small = [m_meta_tokens, m_pre_mix_norm, m_conv_a_w, m_conv_b_w, m_conv_b_bias, m_ln_b_gain, m_ln_b_bias,
               m_post_mix_norm, m_pre_ffn_norm, m_post_ffn_norm]
    v_small = [v_meta_tokens, v_pre_mix_norm, v_conv_a_w, v_conv_b_w, v_conv_b_bias, v_ln_b_gain, v_ln_b_bias,
               v_post_mix_norm, v_pre_ffn_norm, v_post_ffn_norm]
    small = _adam_small(g_small, w_small, m_small, v_small)
    n_small = len(w_small)
    d_small, nm_small, nv_small = small[:n_small], small[n_small:2 * n_small], small[2 * n_small:]

    adam_big(P_IN, small[0])

    def ordered(pick_small, pick_big):
        sm_it = iter(range(n_small))
        out = []
        for name in ("s", "s", "w_in", "s", "s", "s", "s", "s", "w_out", "s", "s", "w_gate", "w_up", "w_down", "s"):
            out.append(pick_small(next(sm_it)) if name == "s" else pick_big(name))
        return out

    grads = ordered(lambda i: g_small[i], lambda n: bigs[n][0])
    deltas = ordered(lambda i: d_small[i], lambda n: bigs[n][1])
    new_m = ordered(lambda i: nm_small[i], lambda n: bigs[n][2])
    new_v = ordered(lambda i: nv_small[i], lambda n: bigs[n][3])
    return (loss, grad_x, *grads, *deltas, *new_m, *new_v)
```

```python
import jax
import jax.numpy as jnp
from jax import lax
from jax.experimental import pallas as pl
from jax.experimental.pallas import tpu as pltpu

F32 = jnp.float32
BF16 = jnp.bfloat16
MESH = pl.DeviceIdType.MESH

N_META = 16
N_DEV = 8
RMS_EPS = 1e-6
LN_EPS = 1e-5
ADAM_LR = 0.001
ADAM_B1 = 0.9
ADAM_B2 = 0.999
ADAM_EPS = 1e-08
ADAM_WD = 0.01
ADAM_STEP = 10

LANE = 128
SUBLANE = 8
BF16_ROWS = 16
ROW_ALIGN = 128
N_ROW_BLOCKS = 4
CONV_HALO = 32
CONV_CHUNK = 64
WGRAD_ROWS = 32
N_CHUNK = 512
WGRAD_TILE_MAX = 1408
ADD_CHUNK = 32
ADAM_COL_BLOCKS = 2
COPY_PIECES = 4
V7X_VMEM_BYTES = 64 * 1024 * 1024
VMEM_LIMIT = V7X_VMEM_BYTES - 6 * 1024 * 1024
SMALL_ROWS = 64
SM_ROWS = 56
N_BIG = 5

ANY = pl.BlockSpec(memory_space=pl.ANY)
VMEM = pl.BlockSpec(memory_space=pltpu.VMEM)


def _cparams(n_grid_axes=0):
    sem = ("arbitrary",) * n_grid_axes if n_grid_axes else None
    return pltpu.CompilerParams(dimension_semantics=sem, vmem_limit_bytes=VMEM_LIMIT)


def _mesh_pos():
    return lax.axis_index("x"), lax.axis_index("y"), lax.axis_index("c")


def _dev_index(px, py, pc):
    return 4 * px + 2 * py + pc


def _other_chips(x, y):
    return [(1 - x, y), (x, 1 - y), (1 - x, 1 - y)]


def _full(shape):
    return pl.BlockSpec(shape, lambda *_: (0,) * len(shape))


def _resident(shape):
    return pl.BlockSpec(shape, lambda *_: (0,) * len(shape), pipeline_mode=pl.Buffered(1))


def _dot_nt(a, w):
    return lax.dot_general(a, w, (((1,), (1,)), ((), ())), preferred_element_type=F32)


def _dot_nn(a, w):
    return jnp.dot(a, w, preferred_element_type=F32)


def _chunks(n, c):
    out, o = [], 0
    while o < n:
        out.append((o, min(c, n - o)))
        o += c
    return out


def _rstd(h):
    return lax.rsqrt(jnp.mean(h * h, axis=-1, keepdims=True) + RMS_EPS)


def _rms_bwd(dyh, yh, r):
    return r * (dyh - yh * jnp.mean(dyh * yh, axis=-1, keepdims=True))


def _silu_grad(a, sig):
    return sig * (1.0 + a * (1.0 - sig))


def _acc_rows(ref, val, first):
    s = jnp.sum(val, axis=0, keepdims=True)

    @pl.when(first)
    def _():
        ref[...] = s

    @pl.when(jnp.logical_not(first))
    def _():
        ref[...] += s


def _row_loop(t_rows, chunk, fn, carry=None):
    def step(i, c):
        return fn(pl.multiple_of(i * chunk, chunk), c)

    return lax.fori_loop(0, t_rows // chunk, step, carry)


def _remote(src, dst, send_sem, recv_sem, to):
    return pltpu.make_async_remote_copy(src_ref=src, dst_ref=dst, send_sem=send_sem, recv_sem=recv_sem,
                                        device_id=to, device_id_type=MESH)


class _Comm:
    def __init__(self, inputs, out_shapes, aliases, scratch, start, finish):
        self.inputs, self.out_shapes, self.aliases, self.scratch = list(inputs), list(out_shapes), dict(aliases), list(scratch)
        self.start, self.finish = start, finish


def _merge_comms(comms):
    inputs, out_shapes, aliases, scratch, spans = [], [], {}, [], []
    for cm in comms:
        spans.append((len(inputs), len(out_shapes), len(scratch), cm))
        aliases.update({len(inputs) + k: len(out_shapes) + v for k, v in cm.aliases.items()})
        inputs += cm.inputs
        out_shapes += cm.out_shapes
        scratch += cm.scratch

    def run(which):
        def fn(ins, outs, scr):
            for i0, o0, s0, cm in spans:
                getattr(cm, which)(ins[i0:i0 + len(cm.inputs)], outs[o0:o0 + len(cm.out_shapes)], scr[s0:s0 + len(cm.scratch)])
        return fn

    return _Comm(inputs, out_shapes, aliases, scratch, run("start"), run("finish"))


def _host_call(body, *, grid, in_specs, out_specs, out_shape, args, name, scratch_shapes=(), comm=None, after=()):
    talks = comm is not None
    if comm is None:
        comm = _Comm([], [], {}, [], lambda *_: None, lambda *_: None)
    n_in, n_out, n_scr = len(args), len(out_shape), len(scratch_shapes)
    c_in, c_out = len(comm.inputs), len(comm.out_shapes)
    n_after = len(after)

    def open_comm(c_ins, c_outs, c_scr):
        if talks:
            _pair_handshake()
        comm.start(c_ins, c_outs, c_scr)

    def hosted(*refs):
        ins, c_ins = refs[:n_in], refs[n_in:n_in + c_in]
        o0 = n_in + c_in + n_after
        outs, c_outs = refs[o0:o0 + n_out], refs[o0 + n_out:o0 + n_out + c_out]
        s0 = o0 + n_out + c_out
        scr, c_scr = refs[s0:s0 + n_scr], refs[s0 + n_scr:]
        if not grid:
            open_comm(c_ins, c_outs, c_scr)
            body(*ins, *outs, *scr)
            comm.finish(c_ins, c_outs, c_scr)
            return
        first = last = None
        for a, n in enumerate(grid):
            f, l = pl.program_id(a) == 0, pl.program_id(a) == n - 1
            first = f if first is None else jnp.logical_and(first, f)
            last = l if last is None else jnp.logical_and(last, l)

        @pl.when(first)
        def _():
            open_comm(c_ins, c_outs, c_scr)

        body(*ins, *outs, *scr)

        @pl.when(last)
        def _():
            comm.finish(c_ins, c_outs, c_scr)

    sem = ("arbitrary",) * len(grid) if grid else None
    params = pltpu.CompilerParams(dimension_semantics=sem, vmem_limit_bytes=VMEM_LIMIT,
                                  collective_id=PAIR_BARRIER_ID if talks else None)
    res = pl.pallas_call(
        hosted, grid=grid, in_specs=list(in_specs) + [ANY] * (c_in + n_after), out_specs=list(out_specs) + [ANY] * c_out,
        out_shape=list(out_shape) + comm.out_shapes, scratch_shapes=list(scratch_shapes) + comm.scratch,
        input_output_aliases={n_in + k: n_out + v for k, v in comm.aliases.items()},
        name=name, compiler_params=params)(*args, *comm.inputs, *after)
    return list(res[:n_out]), list(res[n_out:])


PAIR_BARRIER_ID = 0
START_BARRIER_IDS = (1, 2, 3, 4, 5, 6)
ALL_PEERS_BARRIER_ID = 7


def _chips_handshake():
    x, y, c = _mesh_pos()
    barrier = pltpu.get_barrier_semaphore()
    for chip in _other_chips(x, y):
        pl.semaphore_signal(barrier, inc=1, device_id=(*chip, c), device_id_type=MESH)
    pl.semaphore_wait(barrier, 3)


def _pair_handshake():
    x, y, c = _mesh_pos()
    barrier = pltpu.get_barrier_semaphore()
    pl.semaphore_signal(barrier, inc=1, device_id=(x, y, 1 - c), device_id_type=MESH)
    pl.semaphore_wait(barrier, 1)


GATHER_SEMS = 10


class _Gather:
    def __init__(self, jobs, rows, lo, src_ref, dests, send_sems, recv_sems):
        x, y, c = _mesh_pos()
        me, sib = (x, y, c), (x, y, 1 - c)
        nx, ny, dg = (1 - x, y, c), (x, 1 - y, c), (1 - x, 1 - y, c)
        self.relayed, self.direct, self.relay, self.to_sib, self.sib_fwd = [], [], [], [], []
        for n, (p, r0, nr) in enumerate(jobs):
            assert nr % (2 * BF16_ROWS) == 0
            half = nr // 2

            def rows_of(dev, h, p=p, r0=r0, nr=nr, half=half):
                off, cnt = (r0, nr) if h is None else (r0 + h * half, half)
                return dests[p].at[pl.ds(pl.multiple_of(_dev_index(*dev) * rows[p] + off, BF16_ROWS), cnt), :]

            def mine(h, p=p, r0=r0, nr=nr, half=half):
                off, cnt = (r0, nr) if h is None else (r0 + h * half, half)
                return src_ref.at[pl.ds(lo[p] + off, cnt), :]

            sem = lambda k, n=n: (send_sems.at[GATHER_SEMS * n + k], recv_sems.at[GATHER_SEMS * n + k])
            self.relayed.append([_remote(mine(0), rows_of(me, 0), *sem(0), nx), _remote(mine(1), rows_of(me, 1), *sem(3), ny)])
            self.direct.append([_remote(mine(1), rows_of(me, 1), *sem(1), nx), _remote(mine(0), rows_of(me, 0), *sem(2), ny)])
            self.relay.append([_remote(rows_of(nx, 0), rows_of(nx, 0), *sem(4), ny), _remote(rows_of(ny, 1), rows_of(ny, 1), *sem(5), nx)])
            self.to_sib.append(_remote(mine(None), rows_of(me, None), *sem(6), sib))
            self.sib_fwd.append([_remote(rows_of(dev, None), rows_of(dev, None), *sem(7 + i), sib) for i, dev in enumerate((nx, ny, dg))])

    def start(self):
        for group in (self.relayed, self.direct):
            for cps in group:
                for cp in cps:
                    cp.start()
        for cp in self.to_sib:
            cp.start()

    def mid(self):
        for first, relay in zip(self.relayed, self.relay):
            for arrived, onward in zip(first, relay):
                arrived.wait_recv()
                onward.start()

    def finish(self):
        for direct, relay, fwd in zip(self.direct, self.relay, self.sib_fwd):
            for k in range(2):
                direct[k].wait_recv()
                fwd[k].start()
            for cp in relay:
                cp.wait_recv()
            fwd[2].start()
        for n in range(len(self.to_sib)):
            self.to_sib[n].wait_recv()
            for cp in self.sib_fwd[n]:
                cp.wait_recv()
            for cp in self.relayed[n] + self.direct[n] + self.relay[n] + [self.to_sib[n]] + self.sib_fwd[n]:
                cp.wait_send()


HBM = pl.BlockSpec(memory_space=pltpu.HBM)
SEM = pl.BlockSpec(memory_space=pltpu.SEMAPHORE)
FLOWS = pltpu.SideEffectType.DATAFLOW_SIDE_EFFECTING


def _in_hbm(a):
    return pltpu.with_memory_space_constraint(a, pltpu.HBM)


def _gather_start(wl, dests, ps, rows, barrier_id):
    lo = [sum(rows[:p]) for p in range(N_BIG)]
    n = len(ps)

    def body(*refs):
        wl_ref, dest_refs = refs[0], refs[1:1 + n]
        sends, recvs = refs[1 + n:1 + 2 * n], refs[1 + 2 * n:1 + 3 * n]
        token = refs[-1]
        _chips_handshake()
        x, y, c = _mesh_pos()
        jme = _dev_index(x, y, c)
        for i, p in enumerate(ps):
            mine = dest_refs[i].at[pl.ds(pl.multiple_of(jme * rows[p], BF16_ROWS), rows[p]), :]
            for chip in _other_chips(x, y):
                _remote(wl_ref.at[pl.ds(lo[p], rows[p]), :], mine, sends[i], recvs[i], (*chip, c)).start()
        token[...] = jnp.zeros_like(token)

    thru = [pltpu.HBM(wl.shape, wl.dtype)] + [pltpu.HBM(dests[p].shape, BF16) for p in ps]
    res = pl.pallas_call(
        body, name="gather_start",
        out_shape=tuple([pltpu.SemaphoreType.DMA(())] * (2 * n) + thru + [jax.ShapeDtypeStruct((SUBLANE, LANE), F32)]),
        in_specs=[HBM] * (1 + n), out_specs=tuple([SEM] * (2 * n) + [HBM] * (1 + n) + [VMEM]),
        input_output_aliases={i: 2 * n + i for i in range(1 + n)},
        compiler_params=pltpu.CompilerParams(has_side_effects=FLOWS, collective_id=barrier_id))(
            _in_hbm(wl), *[_in_hbm(dests[p]) for p in ps])
    sems = [(res[i], res[n + i]) for i in range(n)]
    return sems, res[2 * n], list(res[2 * n + 1:3 * n + 1]), res[-1]


def _gather_wait(wl, dest, sems, after, r, name):
    def body(wl_ref, dest_ref, send_sem, recv_sem, after_ref, wl_out, dest_out):
        x, y, c = _mesh_pos()
        three = dest_ref.at[pl.ds(0, 3 * r), :]
        cp = _remote(three, three, send_sem, recv_sem, (x, y, 1 - c))
        cp.wait_send()
        cp.wait_recv()

    res = pl.pallas_call(
        body, name=name, out_shape=(pltpu.HBM(wl.shape, wl.dtype), pltpu.HBM(dest.shape, dest.dtype)),
        in_specs=[HBM, HBM, SEM, SEM, ANY], out_specs=(HBM, HBM), input_output_aliases={0: 0, 1: 1},
        compiler_params=pltpu.CompilerParams(has_side_effects=FLOWS))(wl, dest, sems[0], sems[1], after)
    return res[0], res[1]


def _forward_comm(dest, r):
    def descs(ins, outs, scr):
        x, y, c = _mesh_pos()
        cps = []
        for k, chip in enumerate(_other_chips(x, y)):
            blk = outs[0].at[pl.ds(pl.multiple_of(_dev_index(*chip, c) * r, BF16_ROWS), r), :]
            cps.append(_remote(blk, blk, scr[0].at[k], scr[1].at[k], (x, y, 1 - c)))
        return cps

    def start(ins, outs, scr):
        for cp in descs(ins, outs, scr):
            cp.start()

    def finish(ins, outs, scr):
        cps = descs(ins, outs, scr)
        for cp in cps:
            cp.wait_recv()
        for cp in cps:
            cp.wait_send()

    return _Comm([dest], [jax.ShapeDtypeStruct(dest.shape, dest.dtype)], {0: 0},
                 [pltpu.SemaphoreType.DMA((3,)), pltpu.SemaphoreType.DMA((3,))], start, finish)


def _forward_now(dest, r, name):
    _, (dest,) = _host_call(lambda: None, grid=(), in_specs=[], out_specs=[], out_shape=[], args=(), name=name,
                            comm=_forward_comm(dest, r))
    return dest


def _pair_comm(g, r, row0=0):
    d = g.shape[1]

    def descs(ins, outs, scr):
        x, y, c = _mesh_pos()
        chips = [(x, y)] + _other_chips(x, y)
        return [_remote(ins[0].at[pl.ds(pl.multiple_of(row0 + _dev_index(*chip, 1 - c) * r, BF16_ROWS), r), :], outs[0].at[k],
                        scr[0].at[k], scr[1].at[k], (x, y, 1 - c)) for k, chip in enumerate(chips)]

    def start(ins, outs, scr):
        for cp in descs(ins, outs, scr):
            cp.start()

    def finish(ins, outs, scr):
        cps = descs(ins, outs, scr)
        for cp in cps:
            cp.wait_recv()
        for cp in cps:
            cp.wait_send()

    comm = _Comm([g], [jax.ShapeDtypeStruct((4, r, d), BF16)], {},
                 [pltpu.SemaphoreType.DMA((4,)), pltpu.SemaphoreType.DMA((4,))], start, finish)
    return comm


def _pair_sum(g, pair, r, name, after=(), row0=0):
    d = g.shape[1]

    def body(g_ref, p_ref, *rest):
        o_ref, gbuf, pbuf, sems = rest[len(after):]
        x, y, c = _mesh_pos()
        loads = [pltpu.make_async_copy(p_ref.at[pl.ds(1, 3)], pbuf, sems.at[3])]
        for k, chip in enumerate(_other_chips(x, y)):
            j = _dev_index(*chip, c)
            loads.append(pltpu.make_async_copy(g_ref.at[pl.ds(pl.multiple_of(row0 + j * r, BF16_ROWS), r), :], gbuf.at[k], sems.at[k]))
        for cp in loads:
            cp.start()
        for cp in loads:
            cp.wait()
        for k in range(3):
            o_ref[k] = (gbuf[k].astype(F32) + pbuf[k].astype(F32)).astype(BF16)

    return pl.pallas_call(
        body, out_shape=jax.ShapeDtypeStruct((3, r, d), BF16), in_specs=[ANY] * (2 + len(after)), out_specs=VMEM,
        scratch_shapes=[pltpu.VMEM((3, r, d), BF16), pltpu.VMEM((3, r, d), BF16), pltpu.SemaphoreType.DMA((4,))],
        name=name, compiler_params=_cparams())(g, pair, *after)


def _chip_start(sums, name, barrier_id):
    n = len(sums)

    def body(*refs):
        srcs, lands = refs[:n], refs[n:2 * n]
        sends, recvs = refs[2 * n:3 * n], refs[3 * n:4 * n]
        _chips_handshake()
        x, y, c = _mesh_pos()
        for i in range(n):
            for k, chip in enumerate(_other_chips(x, y)):
                _remote(srcs[i].at[k], lands[i].at[k], sends[i], recvs[i], (*chip, c)).start()
        refs[-1][...] = jnp.zeros_like(refs[-1])

    zones = [pltpu.HBM(s.shape, s.dtype) for s in sums]
    res = pl.pallas_call(
        body, name=name,
        out_shape=tuple([pltpu.SemaphoreType.DMA(())] * (2 * n) + zones + zones + [jax.ShapeDtypeStruct((SUBLANE, LANE), F32)]),
        in_specs=[HBM] * (2 * n), out_specs=tuple([SEM] * (2 * n) + [HBM] * (2 * n) + [VMEM]),
        input_output_aliases={i: 2 * n + i for i in range(2 * n)},
        compiler_params=pltpu.CompilerParams(has_side_effects=FLOWS, collective_id=barrier_id))(
            *[_in_hbm(s) for s in sums], *[_in_hbm(lax.empty(s.shape, s.dtype)) for s in sums])
    flights = [((res[i], res[n + i]), res[2 * n + i], res[3 * n + i]) for i in range(n)]
    return flights, res[-1]


def _chip_wait(sums, land, sems, after, name):
    def body(sums_ref, land_ref, send_sem, recv_sem, after_ref, sums_out, land_out):
        x, y, c = _mesh_pos()
        cp = _remote(sums_ref, land_ref, send_sem, recv_sem, (x, y, 1 - c))
        cp.wait_send()
        cp.wait_recv()

    res = pl.pallas_call(
        body, name=name, out_shape=(pltpu.HBM(sums.shape, sums.dtype), pltpu.HBM(land.shape, land.dtype)),
        in_specs=[HBM, HBM, SEM, SEM, ANY], out_specs=(HBM, HBM), input_output_aliases={0: 0, 1: 1},
        compiler_params=pltpu.CompilerParams(has_side_effects=FLOWS))(sums, land, sems[0], sems[1], after)
    return res[1]


class _CopyThrough:
    def __init__(self, src_ref, dst_ref, dst_row0, n_rows, buf, sem_in, sem_out):
        rc = n_rows // COPY_PIECES
        piece = lambda ref, o: ref.at[pl.ds(o, rc), :]
        self.loads = [pltpu.make_async_copy(piece(src_ref, k * rc), piece(buf, k * rc), sem_in) for k in range(COPY_PIECES)]
        self.stores = [pltpu.make_async_copy(piece(buf, k * rc), piece(dst_ref, dst_row0 + k * rc), sem_out) for k in range(COPY_PIECES)]
        self.all_in = pltpu.make_async_copy(src_ref, buf, sem_in)
        self.all_out = pltpu.make_async_copy(buf, dst_ref.at[pl.ds(dst_row0, n_rows), :], sem_out)

    def load(self):
        for cp in self.loads:
            cp.start()

    def store(self):
        self.all_in.wait()
        for cp in self.stores:
            cp.start()

    def done(self):
        self.all_out.wait()


def _gather_first(shards, sm, jobs, x2, tgt2, t_rows, x0):
    d = shards[0].shape[1]
    rows = [w.shape[0] for w in shards]
    lo = [sum(rows[:p]) for p in range(N_BIG)]
    n_sems = GATHER_SEMS * len(jobs)
    seq = x2.shape[0]
    assert x0 == ROW_ALIGN and seq % ROW_ALIGN == 0 and d == N_DEV * LANE

    def body(s0, s1, s2, s3, s4, sm_ref, x_ref, tgt_ref, wl_ref, o0, o1, o2, o3, o4, sa_ref, h0_ref, tp_ref,
             wl_v, x_v, tgt_v, heads_v, sa_v, send_sems, recv_sems, ssend, srecv, local_sems, sems_in, sems_out):
        dests = (o0, o1, o2, o3, o4)
        x, y, c = _mesh_pos()
        me = (x, y, c)
        jme = _dev_index(*me)
        peers = [(x, y, 1 - c)] + [(*chip, pc) for pc in (c, 1 - c) for chip in _other_chips(x, y)]
        barrier = pltpu.get_barrier_semaphore()
        for to in peers:
            pl.semaphore_signal(barrier, inc=1, device_id=to, device_id_type=MESH)
        pl.semaphore_wait(barrier, len(peers))
        padded = [_CopyThrough(x_ref, h0_ref, x0, seq, x_v, sems_in.at[0], sems_out.at[0]),
                  _CopyThrough(tgt_ref, tp_ref, x0, seq, tgt_v, sems_in.at[1], sems_out.at[1])]
        for cp in padded:
            cp.load()
        shard_refs = (s0, s1, s2, s3, s4)
        first = sorted({j[0] for j in jobs})
        for p in first + [p for p in range(N_BIG) if p not in first]:
            wl_v[pl.ds(lo[p], rows[p]), :] = shard_refs[p][...].astype(BF16)
            if p == first[-1]:
                gather = _Gather(jobs, rows, lo, wl_v, dict(enumerate(dests)), send_sems, recv_sems)
                gather.start()
        smalls = [_remote(sm_ref, sa_ref.at[jme], ssend.at[k], srecv.at[k], to) for k, to in enumerate(peers)]
        for cp in smalls:
            cp.start()
        mine = [pltpu.make_async_copy(wl_v.at[pl.ds(lo[p], rows[p]), :],
                                      dests[p].at[pl.ds(pl.multiple_of(jme * rows[p], BF16_ROWS), rows[p]), :], local_sems.at[p])
                for p in range(N_BIG)]
        mine.append(pltpu.make_async_copy(wl_v, wl_ref, local_sems.at[N_BIG]))
        mine.append(pltpu.make_async_copy(sm_ref, sa_ref.at[jme], local_sems.at[N_BIG + 1]))
        for cp in mine:
            cp.start()
        later = [p for p in range(N_BIG) if p not in {j[0] for j in jobs}]
        own = [_remote(wl_v.at[pl.ds(lo[p], rows[p]), :], dests[p].at[pl.ds(pl.multiple_of(jme * rows[p], BF16_ROWS), rows[p]), :],
                       ssend.at[7 + i], srecv.at[7 + i], (x, y, 1 - c)) for i, p in enumerate(later)]
        for cp in own:
            cp.start()
        gather.mid()
        for cp in padded:
            cp.store()
        for cp in smalls + own:
            cp.wait_recv()
        mine[-1].wait()
        to_v = pltpu.make_async_copy(sa_ref, sa_v, local_sems.at[N_BIG + 1])
        to_v.start()
        to_v.wait()
        head, zeros = heads_v.at[0], heads_v.at[1]
        head[...] = jnp.zeros_like(head)
        zeros[...] = jnp.zeros_like(zeros)
        for j in range(N_DEV):
            head[pl.ds(x0 - N_META, N_META), pl.ds(j * LANE, LANE)] = sa_v[j, pl.ds(0, N_META), :]
        heads = [pltpu.make_async_copy(head, h0_ref.at[pl.ds(0, x0), :], local_sems.at[N_BIG + 1]),
                 pltpu.make_async_copy(zeros, tp_ref.at[pl.ds(0, x0), :], local_sems.at[N_BIG + 2])]
        for cp in heads:
            cp.start()
        gather.finish()
        for cp in smalls + own:
            cp.wait_send()
        for cp in mine[:-1] + heads:
            cp.wait()
        for cp in padded:
            cp.done()

    out_shape = [jax.ShapeDtypeStruct((sum(rows), d), BF16)]
    out_shape += [jax.ShapeDtypeStruct((N_DEV * r, d), BF16) for r in rows]
    out_shape.append(jax.ShapeDtypeStruct((N_DEV,) + sm.shape, F32))
    out_shape += [jax.ShapeDtypeStruct((t_rows, d), F32)] * 2
    res = pl.pallas_call(
        body, out_shape=out_shape, in_specs=[VMEM] * 6 + [ANY] * 2, out_specs=[ANY] * 9,
        scratch_shapes=[pltpu.VMEM((sum(rows), d), BF16), pltpu.VMEM((seq, d), F32), pltpu.VMEM((seq, d), F32),
                        pltpu.VMEM((2, ROW_ALIGN, d), F32), pltpu.VMEM((N_DEV,) + sm.shape, F32),
                        pltpu.SemaphoreType.DMA((n_sems,)), pltpu.SemaphoreType.DMA((n_sems,)),
                        pltpu.SemaphoreType.DMA((7 + N_BIG,)), pltpu.SemaphoreType.DMA((7 + N_BIG,)),
                        pltpu.SemaphoreType.DMA((N_BIG + 3,)), pltpu.SemaphoreType.DMA((2,)), pltpu.SemaphoreType.DMA((2,))],
        name="gather_first",
        compiler_params=pltpu.CompilerParams(vmem_limit_bytes=VMEM_LIMIT, collective_id=ALL_PEERS_BARRIER_ID))(*shards, sm, x2, tgt2)
    return res[0], list(res[1:1 + N_BIG]), res[1 + N_BIG], res[2 + N_BIG], res[3 + N_BIG]


def _cast_place(shards):
    d = shards[0].shape[1]
    rows = [w.shape[0] for w in shards]
    lo = [sum(rows[:p]) for p in range(N_BIG)]

    def body(s0, s1, s2, s3, s4, wl_ref, o0, o1, o2, o3, o4, wl_v, send_sems, recv_sems, local_sems):
        dests = (o0, o1, o2, o3, o4)
        x, y, c = _mesh_pos()
        jme = _dev_index(x, y, c)
        _pair_handshake()
        for p, ref in enumerate((s0, s1, s2, s3, s4)):
            wl_v[pl.ds(lo[p], rows[p]), :] = ref[...].astype(BF16)
        mine_rows = lambda p: dests[p].at[pl.ds(pl.multiple_of(jme * rows[p], BF16_ROWS), rows[p]), :]
        local = [pltpu.make_async_copy(wl_v.at[pl.ds(lo[p], rows[p]), :], mine_rows(p), local_sems.at[p]) for p in range(N_BIG)]
        local.append(pltpu.make_async_copy(wl_v, wl_ref, local_sems.at[N_BIG]))
        to_sib = [_remote(wl_v.at[pl.ds(lo[p], rows[p]), :], mine_rows(p), send_sems.at[p], recv_sems.at[p], (x, y, 1 - c))
                  for p in range(N_BIG)]
        for cp in local + to_sib:
            cp.start()
        for cp in to_sib:
            cp.wait_recv()
        for cp in to_sib:
            cp.wait_send()
        for cp in local:
            cp.wait()

    out_shape = [jax.ShapeDtypeStruct((sum(rows), d), BF16)] + [jax.ShapeDtypeStruct((N_DEV * r, d), BF16) for r in rows]
    res = pl.pallas_call(
        body, out_shape=out_shape, in_specs=[VMEM] * N_BIG, out_specs=[ANY] * (1 + N_BIG),
        scratch_shapes=[pltpu.VMEM((sum(rows), d), BF16), pltpu.SemaphoreType.DMA((N_BIG,)), pltpu.SemaphoreType.DMA((N_BIG,)),
                        pltpu.SemaphoreType.DMA((N_BIG + 1,))],
        name="cast_place", compiler_params=pltpu.CompilerParams(vmem_limit_bytes=VMEM_LIMIT, collective_id=PAIR_BARRIER_ID))(*shards)
    return res[0], list(res[1:])


def _small_prep(sm, x2, tgt2, t_rows, x0, after):
    seq, d = x2.shape
    assert x0 == ROW_ALIGN and seq % ROW_ALIGN == 0 and d == N_DEV * LANE

    def body(sm_ref, x_ref, tgt_ref, *rest):
        sa_ref, h0_ref, tp_ref, x_v, tgt_v, heads_v, sa_v, ssend, srecv, local_sems, sems_in, sems_out = rest[len(after):]
        x, y, c = _mesh_pos()
        jme = _dev_index(x, y, c)
        peers = [(x, y, 1 - c)] + [(*chip, pc) for pc in (c, 1 - c) for chip in _other_chips(x, y)]
        barrier = pltpu.get_barrier_semaphore()
        for to in peers:
            pl.semaphore_signal(barrier, inc=1, device_id=to, device_id_type=MESH)
        pl.semaphore_wait(barrier, len(peers))
        padded = [_CopyThrough(x_ref, h0_ref, x0, seq, x_v, sems_in.at[0], sems_out.at[0]),
                  _CopyThrough(tgt_ref, tp_ref, x0, seq, tgt_v, sems_in.at[1], sems_out.at[1])]
        for cp in padded:
            cp.load()
        smalls = [_remote(sm_ref, sa_ref.at[jme], ssend.at[k], srecv.at[k], to) for k, to in enumerate(peers)]
        for cp in smalls:
            cp.start()
        own = pltpu.make_async_copy(sm_ref, sa_ref.at[jme], local_sems.at[0])
        own.start()
        for cp in padded:
            cp.store()
        for cp in smalls:
            cp.wait_recv()
        own.wait()
        to_v = pltpu.make_async_copy(sa_ref, sa_v, local_sems.at[0])
        to_v.start()
        to_v.wait()
        head, zeros = heads_v.at[0], heads_v.at[1]
        head[...] = jnp.zeros_like(head)
        zeros[...] = jnp.zeros_like(zeros)
        for j in range(N_DEV):
            head[pl.ds(x0 - N_META, N_META), pl.ds(j * LANE, LANE)] = sa_v[j, pl.ds(0, N_META), :]
        heads = [pltpu.make_async_copy(head, h0_ref.at[pl.ds(0, x0), :], local_sems.at[0]),
                 pltpu.make_async_copy(zeros, tp_ref.at[pl.ds(0, x0), :], local_sems.at[1])]
        for cp in heads:
            cp.start()
        for cp in smalls:
            cp.wait_send()
        for cp in heads:
            cp.wait()
        for cp in padded:
            cp.done()

    out_shape = [jax.ShapeDtypeStruct((N_DEV,) + sm.shape, F32)] + [jax.ShapeDtypeStruct((t_rows, d), F32)] * 2
    return pl.pallas_call(
        body, out_shape=out_shape, in_specs=[VMEM] + [ANY] * (2 + len(after)), out_specs=[ANY] * 3,
        scratch_shapes=[pltpu.VMEM((seq, d), F32), pltpu.VMEM((seq, d), F32), pltpu.VMEM((2, ROW_ALIGN, d), F32),
                        pltpu.VMEM((N_DEV,) + sm.shape, F32), pltpu.SemaphoreType.DMA((7,)), pltpu.SemaphoreType.DMA((7,)),
                        pltpu.SemaphoreType.DMA((2,)), pltpu.SemaphoreType.DMA((2,)), pltpu.SemaphoreType.DMA((2,))],
        name="small_prep",
        compiler_params=pltpu.CompilerParams(vmem_limit_bytes=VMEM_LIMIT, collective_id=ALL_PEERS_BARRIER_ID))(sm, x2, tgt2, *after)


def _in_proj(h0, g1, win_t, tm, comm):
    t_rows, d = h0.shape
    e = win_t.shape[0]

    def body(h_ref, g_ref, w_ref, xn_ref, hin_ref):
        h = h_ref[...]
        xn = ((h * _rstd(h)) * g_ref[...]).astype(BF16)
        xn_ref[...] = xn
        for o, n in _chunks(e, N_CHUNK):
            hin_ref[:, pl.ds(o, n)] = _dot_nt(xn, w_ref[pl.ds(o, n), :])

    return _host_call(
        body, grid=(t_rows // tm,),
        in_specs=[pl.BlockSpec((tm, d), lambda i: (i, 0)), _full((1, d)), _resident((e, d))],
        out_specs=[pl.BlockSpec((tm, d), lambda i: (i, 0)), pl.BlockSpec((tm, e), lambda i: (i, 0))],
        out_shape=[jax.ShapeDtypeStruct((t_rows, d), BF16), jax.ShapeDtypeStruct((t_rows, e), F32)],
        args=(h0, g1, win_t), name="in_proj", comm=comm)


def _tap_slot(off):
    return off % SUBLANE, (off // SUBLANE) * SUBLANE


def _fill_shifted(sh_ref, base_ref, residues, n_rows):
    for r in residues:
        if r:
            sh_ref[r] = base_ref[pl.ds(r, n_rows), :]


def _shifted_rows(pair, r, start, n):
    base_ref, sh_ref = pair
    return base_ref[pl.ds(start, n), :] if r == 0 else sh_ref[r, pl.ds(start, n), :]


def _mix_conv_fwd(hin, wa, wb, bb, wa_w, comm):
    t_rows = hin.shape[0]
    nt = wa_w // LANE
    ka, kb = wa.shape[0], wb.shape[0]
    nr = CONV_HALO + t_rows

    def body(bg_ref, cg_ref, ha_ref, val_ref, gt_ref, wa_ref, wb_ref, bb_ref, ya_ref, z_ref, base, sh):
        base[pl.ds(0, CONV_HALO), :] = jnp.zeros((CONV_HALO, LANE), F32)
        base[pl.ds(nr, SUBLANE), :] = jnp.zeros((SUBLANE, LANE), F32)

        def conv(w_ref, k_taps, b, n):
            acc = None
            for k in range(k_taps):
                r, q = _tap_slot(CONV_HALO - (k_taps - 1) + k)
                term = w_ref[pl.ds(k, 1), :] * _shifted_rows((base, sh), r, b + q, n)
                acc = term if acc is None else acc + term
            return acc

        def fill_a(b, c):
            base[pl.ds(CONV_HALO + b, CONV_CHUNK), :] = cg_ref[pl.ds(b, CONV_CHUNK), :] * ha_ref[pl.ds(b, CONV_CHUNK), :]
            return c

        _row_loop(t_rows, CONV_CHUNK, fill_a)
        _fill_shifted(sh, base, sorted({_tap_slot(CONV_HALO - (ka - 1) + k)[0] for k in range(ka)}), nr)

        def out_a(b, c):
            ya_ref[pl.ds(b, CONV_CHUNK), :] = (bg_ref[pl.ds(b, CONV_CHUNK), :] * conv(wa_ref, ka, b, CONV_CHUNK)).astype(BF16)
            return c

        _row_loop(t_rows, CONV_CHUNK, out_a)

        def fill_b(b, c):
            base[pl.ds(CONV_HALO + b, CONV_CHUNK), :] = (val_ref[pl.ds(b, CONV_CHUNK), :]
                                                          * jax.nn.sigmoid(gt_ref[pl.ds(b, CONV_CHUNK), :]))
            return c

        _row_loop(t_rows, CONV_CHUNK, fill_b)
        _fill_shifted(sh, base, range(SUBLANE), nr)

        def out_b(b, c):
            z_ref[pl.ds(b, CONV_CHUNK), :] = conv(wb_ref, kb, b, CONV_CHUNK) + bb_ref[...]
            return c

        _row_loop(t_rows, CONV_CHUNK, out_b)

    def col(g):
        return pl.BlockSpec((t_rows, LANE), lambda i, g=g: (0, g * nt + i))

    tile = lambda rows: pl.BlockSpec((rows, LANE), lambda i: (0, i))
    return _host_call(
        body, grid=(nt,),
        in_specs=[col(0), col(1), col(2), col(3), col(4), tile(ka), tile(kb), tile(1)],
        out_specs=[tile(t_rows), tile(t_rows)],
        out_shape=[jax.ShapeDtypeStruct((t_rows, wa_w), BF16), jax.ShapeDtypeStruct((t_rows, wa_w), F32)],
        scratch_shapes=[pltpu.VMEM((nr + SUBLANE, LANE), F32), pltpu.VMEM((SUBLANE, nr, LANE), F32)],
        args=(hin, hin, hin, hin, hin, wa, wb, bb), name="mix_conv_fwd", comm=comm)


def _ln_parts(z, lg, lb):
    mu = jnp.mean(z, axis=-1, keepdims=True)
    zc = z - mu
    rstd = lax.rsqrt(jnp.mean(zc * zc, axis=-1, keepdims=True) + LN_EPS)
    zh = zc * rstd
    return zh, rstd, zh * lg + lb


def _out_proj(ya, z, lg, lb, w_out, h0, g2, g3, tm, comm):
    t_rows, d = h0.shape
    w = z.shape[1]

    def body(ya_ref, z_ref, lg_ref, lb_ref, w_ref, h0_ref, g2_ref, g3_ref, y_ref, mix_ref, h1_ref, xn2_ref):
        _, _, ln = _ln_parts(z_ref[...], lg_ref[...], lb_ref[...])
        y_ref[:, pl.ds(0, w)] = ya_ref[...]
        y_ref[:, pl.ds(w, w)] = (ln * jax.nn.sigmoid(ln)).astype(BF16)
        mix = _dot_nn(y_ref[...], w_ref[...])
        mix_ref[...] = mix
        h1 = h0_ref[...] + (mix * _rstd(mix)) * g2_ref[...]
        h1_ref[...] = h1
        xn2_ref[...] = ((h1 * _rstd(h1)) * g3_ref[...]).astype(BF16)

    blk = pl.BlockSpec((tm, d), lambda i: (i, 0))
    half = pl.BlockSpec((tm, w), lambda i: (i, 0))
    return _host_call(
        body, grid=(t_rows // tm,),
        in_specs=[half, half, _full((1, w)), _full((1, w)), _resident(w_out.shape), blk, _full((1, d)), _full((1, d))],
        out_specs=[blk, blk, blk, blk],
        out_shape=[jax.ShapeDtypeStruct((t_rows, d), BF16), jax.ShapeDtypeStruct((t_rows, d), F32),
                   jax.ShapeDtypeStruct((t_rows, d), F32), jax.ShapeDtypeStruct((t_rows, d), BF16)],
        args=(ya, z, lg, lb, w_out, h0, g2, g3), name="out_proj", comm=comm)


def _gate_up(xn2, wg_t, wu_t, tm, comm):
    t_rows, d = xn2.shape
    f = wg_t.shape[0]

    def body(x_ref, wg_ref, wu_ref, ga_ref, gu_ref, s_ref):
        xn = x_ref[...]
        for o, n in _chunks(f, N_CHUNK):
            a = _dot_nt(xn, wg_ref[pl.ds(o, n), :])
            u = _dot_nt(xn, wu_ref[pl.ds(o, n), :])
            sig = jax.nn.sigmoid(a)
            silu = a * sig
            s = silu * u
            gu_ref[:, pl.ds(o, n)] = silu.astype(BF16)
            ga_ref[:, pl.ds(o, n)] = ((u - s) * sig + s).astype(BF16)
            s_ref[:, pl.ds(o, n)] = s.astype(BF16)

    blk = pl.BlockSpec((tm, f), lambda i: (i, 0))
    return _host_call(
        body, grid=(t_rows // tm,),
        in_specs=[pl.BlockSpec((tm, d), lambda i: (i, 0)), _resident((f, d)), _resident((f, d))],
        out_specs=[blk, blk, blk], out_shape=[jax.ShapeDtypeStruct((t_rows, f), BF16)] * 3,
        args=(xn2, wg_t, wu_t), name="gate_up", comm=comm)


def _down_loss(s, wd, h1, tgt, g4, tm, x0):
    t_rows, d = h1.shape
    f = wd.shape[0]

    def body(s_ref, w_ref, h1_ref, tgt_ref, g4_ref, dh2_ref, dff_ref, dg4_ref, loss_ref):
        i = pl.program_id(0)
        ff = _dot_nn(s_ref[...], w_ref[...])
        r4 = _rstd(ff)
        fh = ff * r4
        g4 = g4_ref[...]
        h2 = h1_ref[...] + fh * g4
        row = i * tm + lax.broadcasted_iota(jnp.int32, (tm, 1), 0)
        diff = jnp.where(row >= x0, h2 - tgt_ref[...], 0.0)
        dh2 = diff / d
        dh2_ref[...] = dh2
        dff_ref[...] = _rms_bwd(dh2 * g4, fh, r4).astype(BF16)
        _acc_rows(dg4_ref, dh2 * fh, i == 0)
        _acc_rows(loss_ref, diff * diff, i == 0)

    blk = pl.BlockSpec((tm, d), lambda i: (i, 0))
    res, _ = _host_call(
        body, grid=(t_rows // tm,),
        in_specs=[pl.BlockSpec((tm, f), lambda i: (i, 0)), _resident((f, d)), blk, blk, _full((1, d))],
        out_specs=[blk, blk, _full((1, d)), _full((1, d))],
        out_shape=[jax.ShapeDtypeStruct((t_rows, d), F32), jax.ShapeDtypeStruct((t_rows, d), BF16),
                   jax.ShapeDtypeStruct((1, d), F32), jax.ShapeDtypeStruct((1, d), F32)],
        args=(s, wd, h1, tgt, g4), name="down_loss")
    return res


def _bwd_down(dff, wd, ga, gu, tm, comm):
    t_rows, d = dff.shape
    f = wd.shape[0]

    def body(dff_ref, w_ref, ga_ref, gu_ref, dau_ref):
        dff_v = dff_ref[...]
        for o, n in _chunks(f, N_CHUNK):
            ds = _dot_nt(dff_v, w_ref[pl.ds(o, n), :]).astype(BF16)
            dau_ref[0, :, pl.ds(o, n)] = ds * ga_ref[:, pl.ds(o, n)]
            dau_ref[1, :, pl.ds(o, n)] = ds * gu_ref[:, pl.ds(o, n)]

    blk = pl.BlockSpec((tm, f), lambda i: (i, 0))
    (dau,), extra = _host_call(
        body, grid=(t_rows // tm,),
        in_specs=[pl.BlockSpec((tm, d), lambda i: (i, 0)), _resident((f, d)), blk, blk],
        out_specs=[pl.BlockSpec((2, tm, f), lambda i: (0, i, 0))], out_shape=[jax.ShapeDtypeStruct((2, t_rows, f), BF16)],
        args=(dff, wd, ga, gu), name="bwd_down", comm=comm)
    return dau, extra


def _wgrad(a, b, name, after=()):
    d = b.shape[1]
    t_rows = b.shape[0]
    stacked = a.ndim == 3
    n = a.shape[-1]
    groups = a.shape[0] if stacked else 1
    steps = 1 if stacked else 2
    tile = max(t for t in range(LANE, min(n // steps, WGRAD_TILE_MAX) + 1, LANE) if n % t == 0)
    tiles = n // tile

    def body(a_ref, b_ref, o_ref):
        o_ref[...] = lax.dot_general(a_ref[...], b_ref[...], (((0,), (0,)), ((), ())),
                                     preferred_element_type=F32).astype(BF16)

    if stacked:
        a_spec = pl.BlockSpec((None, t_rows, tile), lambda g, i: (g, 0, i))
    else:
        a_spec = pl.BlockSpec((t_rows, tile), lambda g, i: (0, i))
    res, _ = _host_call(
        body, grid=(groups, tiles), in_specs=[a_spec, _resident((t_rows, d))],
        out_specs=[pl.BlockSpec((tile, d), lambda g, i: (g * tiles + i, 0))],
        out_shape=[jax.ShapeDtypeStruct((groups * n, d), BF16)], args=(a, b), name=name, after=after)
    return res[0]


def _bwd_ffn_in(dau, wg_t, wu_t, h1, dh2, g3, tm, comm):
    t_rows, d = h1.shape
    f = wg_t.shape[0]

    def body(dau_ref, wg_ref, wu_ref, h1_ref, dh2_ref, g3_ref, dh1_ref, dg3_ref):
        dxn2 = _dot_nn(dau_ref[0], wg_ref[...]) + _dot_nn(dau_ref[1], wu_ref[...])
        h1 = h1_ref[...]
        r3 = _rstd(h1)
        h1h = h1 * r3
        _acc_rows(dg3_ref, dxn2 * h1h, pl.program_id(0) == 0)
        dh1_ref[...] = dh2_ref[...] + _rms_bwd(dxn2 * g3_ref[...], h1h, r3)

    blk = pl.BlockSpec((tm, d), lambda i: (i, 0))
    return _host_call(
        body, grid=(t_rows // tm,),
        in_specs=[pl.BlockSpec((2, tm, f), lambda i: (0, i, 0)), _resident((f, d)), _resident((f, d)), blk, blk, _full((1, d))],
        out_specs=[blk, _full((1, d))],
        out_shape=[jax.ShapeDtypeStruct((t_rows, d), F32), jax.ShapeDtypeStruct((1, d), F32)],
        args=(dau, wg_t, wu_t, h1, dh2, g3), name="bwd_ffn_in", comm=comm)


def _bwd_out_proj(dh1, mix, w_out, g2, z, lg, lb, tm, after):
    t_rows, d = dh1.shape
    w = z.shape[1]

    def body(dh1_ref, mix_ref, w_ref, g2_ref, z_ref, lg_ref, lb_ref, dmix_ref, dya_ref, dz_ref, dg2_ref, dlg_ref, dlb_ref, dbb_ref):
        first = pl.program_id(0) == 0
        mix = mix_ref[...]
        r2 = _rstd(mix)
        mh = mix * r2
        dh1 = dh1_ref[...]
        _acc_rows(dg2_ref, dh1 * mh, first)
        dmix = _rms_bwd(dh1 * g2_ref[...], mh, r2).astype(BF16)
        dmix_ref[...] = dmix
        dy = _dot_nt(dmix, w_ref[...])
        dya_ref[...] = dy[:, :w]
        lg = lg_ref[...]
        zh, rstd, ln = _ln_parts(z_ref[...], lg, lb_ref[...])
        dln = dy[:, w:] * _silu_grad(ln, jax.nn.sigmoid(ln))
        _acc_rows(dlg_ref, dln * zh, first)
        _acc_rows(dlb_ref, dln, first)
        dzh = dln * lg
        dz = rstd * (dzh - jnp.mean(dzh, axis=-1, keepdims=True) - zh * jnp.mean(dzh * zh, axis=-1, keepdims=True))
        dz_ref[...] = dz
        _acc_rows(dbb_ref, dz, first)

    blk = pl.BlockSpec((tm, d), lambda i: (i, 0))
    half = pl.BlockSpec((tm, w), lambda i: (i, 0))
    vec = _full((1, w))
    res, _ = _host_call(
        body, grid=(t_rows // tm,), in_specs=[blk, blk, _resident(w_out.shape), _full((1, d)), half, vec, vec],
        out_specs=[blk, half, half, _full((1, d)), vec, vec, vec],
        out_shape=[jax.ShapeDtypeStruct((t_rows, d), BF16), jax.ShapeDtypeStruct((t_rows, w), F32),
                   jax.ShapeDtypeStruct((t_rows, w), F32), jax.ShapeDtypeStruct((1, d), F32)]
        + [jax.ShapeDtypeStruct((1, w), F32)] * 3,
        args=(dh1, mix, w_out, g2, z, lg, lb), name="bwd_out_proj", after=after)
    return res


def _mix_conv_bwd(hin, dy, dz, wa, wb, wa_w, comm):
    t_rows = hin.shape[0]
    nt = wa_w // LANE
    ka, kb = wa.shape[0], wb.shape[0]
    nr = CONV_HALO + t_rows
    kb_rows = -(-kb // SUBLANE) * SUBLANE

    def body(bg_ref, cg_ref, ha_ref, val_ref, gt_ref, dya_ref, dz_ref, wa_ref, wb_ref,
             dh_ref, dwa_ref, dwb_ref, base, sh, based, shd, tmp, wbc):
        zeros = lambda n: jnp.zeros((n, LANE), F32)
        base[pl.ds(0, CONV_HALO), :] = zeros(CONV_HALO)
        base[pl.ds(nr, SUBLANE), :] = zeros(SUBLANE)
        based[pl.ds(t_rows, CONV_HALO + SUBLANE), :] = zeros(CONV_HALO + SUBLANE)

        def fwd_slot(k_taps, k):
            return _tap_slot(CONV_HALO - (k_taps - 1) + k)

        def bwd_slot(k_taps, k):
            return _tap_slot(k_taps - 1 - k)

        def conv(w_ref, k_taps, src, slot, b, n):
            acc = None
            for k in range(k_taps):
                r, q = slot(k_taps, k)
                term = w_ref[pl.ds(k, 1), :] * _shifted_rows(src, r, b + q, n)
                acc = term if acc is None else acc + term
            return acc

        def by_residue(k_taps, slot):
            groups = {}
            for k in range(k_taps):
                r, q = slot(k_taps, k)
                groups.setdefault(r, []).append((k, q // SUBLANE))
            return groups

        def wgrad_loop(w_ref, k_taps):
            n_sub = WGRAD_ROWS // SUBLANE
            for k in range(k_taps):
                wbc[k] = jnp.broadcast_to(w_ref[pl.ds(k, 1), :], (SUBLANE, LANE))
            fwd, bwd = by_residue(k_taps, fwd_slot), by_residue(k_taps, bwd_slot)

            def window(src, r, taps, b):
                span = n_sub + max(qi for _, qi in taps)
                return [_shifted_rows(src, r, b + SUBLANE * i, SUBLANE) for i in range(span)]

            def step(b, accs):
                accs = list(accs)
                dv = [based[pl.ds(b + SUBLANE * j, SUBLANE), :] for j in range(n_sub)]
                for r, taps in fwd.items():
                    win = window((base, sh), r, taps, b)
                    for k, qi in taps:
                        t = dv[0] * win[qi]
                        for j in range(1, n_sub):
                            t = t + dv[j] * win[qi + j]
                        accs[k] = accs[k] + t
                outs = [None] * n_sub
                for r, taps in bwd.items():
                    win = window((based, shd), r, taps, b)
                    for k, qi in taps:
                        wk = wbc[k]
                        for j in range(n_sub):
                            term = wk * win[qi + j]
                            outs[j] = term if outs[j] is None else outs[j] + term
                for j in range(n_sub):
                    tmp[pl.ds(b + SUBLANE * j, SUBLANE), :] = outs[j]
                return tuple(accs)

            return _row_loop(t_rows, WGRAD_ROWS, step, tuple(zeros(SUBLANE) for _ in range(k_taps)))

        def store_taps(ref, accs, rows):
            for k, acc in enumerate(accs):
                ref[pl.ds(k, 1), :] = jnp.sum(acc, axis=0, keepdims=True)
            if rows > len(accs):
                ref[pl.ds(len(accs), rows - len(accs)), :] = zeros(rows - len(accs))

        def fill_a(b, c):
            sl = pl.ds(b, CONV_CHUNK)
            base[pl.ds(CONV_HALO + b, CONV_CHUNK), :] = cg_ref[sl, :] * ha_ref[sl, :]
            based[sl, :] = dya_ref[sl, :] * bg_ref[sl, :]
            return c

        _row_loop(t_rows, CONV_CHUNK, fill_a)
        _fill_shifted(sh, base, sorted({fwd_slot(ka, k)[0] for k in range(ka)}), nr)
        _fill_shifted(shd, based, sorted({bwd_slot(ka, k)[0] for k in range(ka)}), nr)

        def d_bgate(b, c):
            sl = pl.ds(b, CONV_CHUNK)
            dh_ref[0, sl, :] = (dya_ref[sl, :] * conv(wa_ref, ka, (base, sh), fwd_slot, b, CONV_CHUNK)).astype(BF16)
            return c

        _row_loop(t_rows, CONV_CHUNK, d_bgate)
        store_taps(dwa_ref, wgrad_loop(wa_ref, ka), SUBLANE)

        def d_ch(b, c):
            sl = pl.ds(b, CONV_CHUNK)
            dua = tmp[sl, :]
            dh_ref[1, sl, :] = (dua * ha_ref[sl, :]).astype(BF16)
            dh_ref[2, sl, :] = (dua * cg_ref[sl, :]).astype(BF16)
            return c

        _row_loop(t_rows, CONV_CHUNK, d_ch)

        def fill_b(b, c):
            sl = pl.ds(b, CONV_CHUNK)
            base[pl.ds(CONV_HALO + b, CONV_CHUNK), :] = val_ref[sl, :] * jax.nn.sigmoid(gt_ref[sl, :])
            based[sl, :] = dz_ref[sl, :]
            return c

        _row_loop(t_rows, CONV_CHUNK, fill_b)
        _fill_shifted(sh, base, range(SUBLANE), nr)
        _fill_shifted(shd, based, range(SUBLANE), nr)
        store_taps(dwb_ref, wgrad_loop(wb_ref, kb), kb_rows)

        def d_glu(b, c):
            sl = pl.ds(b, CONV_CHUNK)
            dgg = tmp[sl, :]
            sig = jax.nn.sigmoid(gt_ref[sl, :])
            dh_ref[3, sl, :] = (dgg * sig).astype(BF16)
            dh_ref[4, sl, :] = (dgg * val_ref[sl, :] * (sig * (1.0 - sig))).astype(BF16)
            return c

        _row_loop(t_rows, CONV_CHUNK, d_glu)

    def col(g):
        return pl.BlockSpec((t_rows, LANE), lambda i, g=g: (0, g * nt + i))

    tile = lambda rows: pl.BlockSpec((rows, LANE), lambda i: (0, i))
    return _host_call(
        body, grid=(nt,),
        in_specs=[col(0), col(1), col(2), col(3), col(4), tile(t_rows), tile(t_rows), tile(ka), tile(kb)],
        out_specs=[pl.BlockSpec((5, t_rows, LANE), lambda i: (0, 0, i)), tile(SUBLANE), tile(kb_rows)],
        out_shape=[jax.ShapeDtypeStruct((5, t_rows, wa_w), BF16), jax.ShapeDtypeStruct((SUBLANE, wa_w), F32),
                   jax.ShapeDtypeStruct((kb_rows, wa_w), F32)],
        scratch_shapes=[pltpu.VMEM((nr + SUBLANE, LANE), F32), pltpu.VMEM((SUBLANE, nr, LANE), F32),
                        pltpu.VMEM((nr + SUBLANE, LANE), F32), pltpu.VMEM((SUBLANE, nr, LANE), F32),
                        pltpu.VMEM((t_rows, LANE), F32), pltpu.VMEM((kb_rows, SUBLANE, LANE), F32)],
        args=(hin, hin, hin, hin, hin, dy, dz, wa, wb), name="mix_conv_bwd", comm=comm)


def _bwd_in_proj(dh5, win_t, h0, dh1, g1, tm, comm):
    t_rows, d = h0.shape
    groups, _, w = dh5.shape

    def body(dh_ref, w_ref, h0_ref, dh1_ref, g1_ref, dh0_ref, dg1_ref):
        dxn1 = None
        for g in range(groups):
            part = _dot_nn(dh_ref[g], w_ref[pl.ds(g * w, w), :])
            dxn1 = part if dxn1 is None else dxn1 + part
        h0 = h0_ref[...]
        r1 = _rstd(h0)
        h0h = h0 * r1
        _acc_rows(dg1_ref, dxn1 * h0h, pl.program_id(0) == 0)
        dh0_ref[...] = dh1_ref[...] + _rms_bwd(dxn1 * g1_ref[...], h0h, r1)

    blk = pl.BlockSpec((tm, d), lambda i: (i, 0))
    return _host_call(
        body, grid=(t_rows // tm,),
        in_specs=[pl.BlockSpec((groups, tm, w), lambda i: (0, i, 0)), _resident(win_t.shape), blk, blk, _full((1, d))],
        out_specs=[blk, _full((1, d))],
        out_shape=[jax.ShapeDtypeStruct((t_rows, d), F32), jax.ShapeDtypeStruct((1, d), F32)],
        args=(dh5, win_t, h0, dh1, g1), name="bwd_in_proj", comm=comm)


def _pair_small(smalls, d):
    (dmeta, dg1, dg2, dg3, dg4, dbb, dlg, dlb, lossv, dwa, dwb) = smalls
    half = d // 2
    kb_rows = dwb.shape[0]

    def body(dmeta_ref, dg1_ref, dg2_ref, dg3_ref, dg4_ref, dbb_ref, dlg_ref, dlb_ref, loss_ref, dwa_ref, dwb_ref,
             sums_ref, pbuf, psib, ps_send, ps_recv):
        x, y, c = _mesh_pos()
        _pair_handshake()
        pbuf[...] = jnp.zeros_like(pbuf)
        pbuf[pl.ds(0, N_META), :] = dmeta_ref[...]
        for row, ref in ((16, dg1_ref), (17, dg2_ref), (18, dg3_ref), (19, dg4_ref)):
            pbuf[pl.ds(row, 1), :] = ref[...]
        pbuf[pl.ds(20, 1), pl.ds(0, half)] = dbb_ref[...]
        pbuf[pl.ds(20, 1), pl.ds(half, half)] = dlg_ref[...]
        pbuf[pl.ds(21, 1), pl.ds(0, half)] = dlb_ref[...]
        lv = loss_ref[...]
        pbuf[pl.ds(21, 1), pl.ds(half, half)] = lv[:, :half] + lv[:, half:]
        pbuf[pl.ds(24, SUBLANE), pl.ds(0, half)] = dwa_ref[...]
        pbuf[pl.ds(32, kb_rows), pl.ds(0, half)] = dwb_ref[...]
        to_sib = _remote(pbuf, psib, ps_send.at[0], ps_recv.at[0], (x, y, 1 - c))
        to_sib.start()
        to_sib.wait_recv()
        s = pbuf[...] + psib[...]
        for k in range(3):
            sums_ref[k] = s
        to_sib.wait_send()

    return pl.pallas_call(
        body, out_shape=jax.ShapeDtypeStruct((3, SMALL_ROWS, d), F32), in_specs=[VMEM] * 11, out_specs=VMEM,
        scratch_shapes=[pltpu.VMEM((SMALL_ROWS, d), F32), pltpu.VMEM((SMALL_ROWS, d), F32),
                        pltpu.SemaphoreType.DMA((1,)), pltpu.SemaphoreType.DMA((1,))],
        name="pair_small", compiler_params=pltpu.CompilerParams(vmem_limit_bytes=VMEM_LIMIT, collective_id=PAIR_BARRIER_ID))(*smalls)


def _total_small(own, others):
    _, r, d = own.shape

    def body(own_ref, others_ref, tot_ref, chip_p):
        x, y, _ = _mesh_pos()
        chip_p[2 * x + y] = own_ref[0]
        for k, (cx, cy) in enumerate(_other_chips(x, y)):
            chip_p[2 * cx + cy] = others_ref[k]
        tot_ref[...] = ((chip_p[0] + chip_p[1]) + chip_p[2]) + chip_p[3]

    return pl.pallas_call(body, out_shape=jax.ShapeDtypeStruct((r, d), F32), scratch_shapes=[pltpu.VMEM((4, r, d), F32)],
                          name="total_small", compiler_params=_cparams())(own, others)


def _adamw(w, g, m, v):
    m = ADAM_B1 * m + (1.0 - ADAM_B1) * g
    v = ADAM_B2 * v + (1.0 - ADAM_B2) * jnp.square(g)
    m_hat = m / (1.0 - ADAM_B1 ** ADAM_STEP)
    v_hat = v / (1.0 - ADAM_B2 ** ADAM_STEP)
    delta = -ADAM_LR * (m_hat / (jnp.sqrt(v_hat) + ADAM_EPS) + ADAM_WD * w)
    return delta, m, v


def _adam_big(g, pair, part, w, m, v, name, row0=0):
    r, d = w.shape
    cols = d // ADAM_COL_BLOCKS
    assert row0 % r == 0

    def body(me_ref, g_ref, pair_ref, part_ref, w_ref, m_ref, v_ref, go_ref, d_ref, mo_ref, vo_ref):
        g = g_ref[...].astype(F32) + pair_ref[...].astype(F32)
        for k in range(3):
            g = g + part_ref[k].astype(F32)
        go_ref[...] = g
        d_ref[...], mo_ref[...], vo_ref[...] = _adamw(w_ref[...], g, m_ref[...], v_ref[...])

    blk = pl.BlockSpec((r, cols), lambda i, me_ref: (0, i))
    grid_spec = pltpu.PrefetchScalarGridSpec(
        num_scalar_prefetch=1, grid=(ADAM_COL_BLOCKS,),
        in_specs=[pl.BlockSpec((r, cols), lambda i, me_ref: (me_ref[0], i)),
                  pl.BlockSpec((None, r, cols), lambda i, me_ref: (0, 0, i)),
                  pl.BlockSpec((3, r, cols), lambda i, me_ref: (0, 0, i)), blk, blk, blk],
        out_specs=[blk, blk, blk, blk])
    me = jnp.reshape(_dev_index(*_mesh_pos()) + row0 // r, (1,)).astype(jnp.int32)
    return pl.pallas_call(body, out_shape=[jax.ShapeDtypeStruct((r, d), F32)] * 4, grid_spec=grid_spec, name=name,
                          compiler_params=_cparams(1))(me, g, pair, part, w, m, v)


def _adam_small(gs, ws, ms, vs):
    n = len(gs)

    def body(*refs):
        ins, outs = refs[:4 * n], refs[4 * n:]
        for i in range(n):
            g = ins[i][...]
            delta, m, v = _adamw(ins[n + i][...], g, ins[2 * n + i][...], ins[3 * n + i][...])
            outs[i][...] = delta
            outs[n + i][...] = m
            outs[2 * n + i][...] = v

    shapes = [jax.ShapeDtypeStruct(w.shape, F32) for w in ws]
    return pl.pallas_call(body, out_shape=shapes * 3, name="adam_small", compiler_params=_cparams())(*gs, *ws, *ms, *vs)


def kernel(x, meta_tokens, pre_mix_norm, w_in, conv_a_w, conv_b_w, conv_b_bias, ln_b_gain, ln_b_bias, w_out, post_mix_norm, pre_ffn_norm, w_gate, w_up, w_down, post_ffn_norm, loss_target, m_meta_tokens, m_pre_mix_norm, m_w_in, m_conv_a_w, m_conv_b_w, m_conv_b_bias, m_ln_b_gain, m_ln_b_bias, m_w_out, m_post_mix_norm, m_pre_ffn_norm, m_w_gate, m_w_up, m_w_down, m_post_ffn_norm, v_meta_tokens, v_pre_mix_norm, v_w_in, v_conv_a_w, v_conv_b_w, v_conv_b_bias, v_ln_b_gain, v_ln_b_bias, v_w_out, v_post_mix_norm, v_pre_ffn_norm, v_w_gate, v_w_up, v_w_down, v_post_ffn_norm):
    _, seq, d = x.shape
    ka, ca_loc = conv_a_w.shape[1:]
    kb, cb_loc = conv_b_w.shape[1:]
    wa_w = ca_loc * N_DEV
    assert cb_loc == ca_loc and wa_w % LANE == 0 and w_in.shape[2] * N_DEV == 5 * wa_w and 2 * wa_w == d
    pad = (-(N_META + seq)) % ROW_ALIGN
    x0 = pad + N_META
    t_rows = x0 + seq
    assert t_rows % (N_ROW_BLOCKS * BF16_ROWS) == 0 and t_rows % CONV_CHUNK == 0 and d % LANE == 0
    tm = t_rows // N_ROW_BLOCKS
    tm2 = t_rows // 2
    me = _dev_index(*_mesh_pos())

    def as_rows(w_in_like, w_out_like, w_gate_like, w_up_like, w_down_like):
        return (w_in_like[0].T, w_out_like[0], w_gate_like[0].T, w_up_like[0].T, w_down_like[0])

    w_loc = as_rows(w_in, w_out, w_gate, w_up, w_down)
    rows = [w.shape[0] for w in w_loc]
    assert all(r % ADD_CHUNK == 0 for r in rows)
    P_IN, P_OUT, P_GATE, P_UP, P_DOWN = range(N_BIG)

    sm = jnp.zeros((SM_ROWS, LANE), F32)
    sm = sm.at[0:N_META, :].set(meta_tokens)
    sm = sm.at[16:16 + ka, 0:ca_loc].set(conv_a_w[0])
    sm = sm.at[24:24 + kb, 0:cb_loc].set(conv_b_w[0])
    wl, wfull = _cast_place(w_loc)
    order = (P_IN, P_OUT, P_GATE, P_UP, P_DOWN)
    sems, wl, started, token = _gather_start(wl, wfull, order, rows, START_BARRIER_IDS[0])
    for p, arr in zip(order, started):
        wfull[p] = arr
    sm_all, h0, tgt = _small_prep(sm, x[0], loss_target[0], t_rows, x0, [token])
    wa = jnp.transpose(sm_all[:, 16:16 + ka, 0:ca_loc], (1, 0, 2)).reshape(ka, wa_w)
    wb = jnp.transpose(sm_all[:, 24:24 + kb, 0:cb_loc], (1, 0, 2)).reshape(kb, wa_w)

    def arrived(p, after, name):
        nonlocal wl
        wl, wfull[p] = _gather_wait(wl, wfull[p], sems[order.index(p)], after, rows[p], name)
        return _forward_comm(wfull[p], rows[p])

    arrived(P_IN, h0, "gather_wait_in")
    wfull[P_IN] = _forward_now(wfull[P_IN], rows[P_IN], "forward_in")
    (xn1, hin), _ = _in_proj(h0, pre_mix_norm, wfull[P_IN], tm, None)
    (ya, z), (wfull[P_OUT],) = _mix_conv_fwd(hin, wa, wb, conv_b_bias, wa_w, arrived(P_OUT, hin, "gather_wait_out"))
    (y, mix, h1, xn2), (wfull[P_GATE],) = _out_proj(ya, z, ln_b_gain, ln_b_bias, wfull[P_OUT], h0, post_mix_norm, pre_ffn_norm, tm2,
                                                    arrived(P_GATE, z, "gather_wait_gate"))
    arrived(P_UP, xn2, "gather_wait_up")
    wfull[P_UP] = _forward_now(wfull[P_UP], rows[P_UP], "forward_up")
    (ga, gu, s), _ = _gate_up(xn2, wfull[P_GATE], wfull[P_UP], tm, None)
    arrived(P_DOWN, s, "gather_wait_down")
    wfull[P_DOWN] = _forward_now(wfull[P_DOWN], rows[P_DOWN], "forward_down")
    dh2, dff, dg4, lossv = _down_loss(s, wfull[P_DOWN], h1, tgt, post_ffn_norm, tm, x0)

    gwd = _wgrad(s, dff, "wgrad_down")
    dau, (pair_d,) = _bwd_down(dff, wfull[P_DOWN], ga, gu, tm, _pair_comm(gwd, rows[P_DOWN]))
    (flight_d,), token = _chip_start([_pair_sum(gwd, pair_d, rows[P_DOWN], "pair_sum_down")], "chip_start_down", START_BARRIER_IDS[1])
    gw_gu = _wgrad(dau, xn2, "wgrad_gate_up", [token])
    up0 = N_DEV * rows[P_GATE]
    (dh1, dg3), (pair_g, pair_u) = _bwd_ffn_in(dau, wfull[P_GATE], wfull[P_UP], h1, dh2, pre_ffn_norm, tm,
                                               _merge_comms([_pair_comm(gw_gu, rows[P_GATE]), _pair_comm(gw_gu, rows[P_UP], up0)]))
    (flight_g, flight_u), token = _chip_start([_pair_sum(gw_gu, pair_g, rows[P_GATE], "pair_sum_gate"),
                                               _pair_sum(gw_gu, pair_u, rows[P_UP], "pair_sum_up", row0=up0)], "chip_start_gate_up",
                                              START_BARRIER_IDS[2])
    dmix, dya, dz, dg2, dlg, dlb, dbb = _bwd_out_proj(dh1, mix, wfull[P_OUT], post_mix_norm, z, ln_b_gain, ln_b_bias, tm, [token])
    gwo = _wgrad(y, dmix, "wgrad_out")
    (dh5, dwa, dwb), (pair_o,) = _mix_conv_bwd(hin, dya, dz, wa, wb, wa_w, _pair_comm(gwo, rows[P_OUT]))
    (flight_o,), token = _chip_start([_pair_sum(gwo, pair_o, rows[P_OUT], "pair_sum_out")], "chip_start_out", START_BARRIER_IDS[3])
    gwi = _wgrad(dh5, xn1, "wgrad_in", [token])
    (dh0, dg1), (pair_i,) = _bwd_in_proj(dh5, wfull[P_IN], h0, dh1, pre_mix_norm, tm, _pair_comm(gwi, rows[P_IN]))
    grad_x = dh0[x0:][None]
    dmeta = dh0[x0 - N_META:x0]
    small_sums = _pair_small((dmeta, dg1, dg2, dg3, dg4, dbb, dlg, dlb, lossv, dwa, dwb), d)
    (flight_s,), token = _chip_start([small_sums], "chip_start_small", START_BARRIER_IDS[4])
    (flight_i,), token = _chip_start([_pair_sum(gwi, pair_i, rows[P_IN], "pair_sum_in", [token])], "chip_start_in",
                                     START_BARRIER_IDS[5])

    def landed(flight, after, tag):
        sems_p, sums, land = flight
        return _chip_wait(sums, land, sems_p, after, "chip_wait_" + tag)

    m_loc = as_rows(m_w_in, m_w_out, m_w_gate, m_w_up, m_w_down)
    v_loc = as_rows(v_w_in, v_w_out, v_w_gate, v_w_up, v_w_down)
    full_grads = {P_IN: gwi, P_OUT: gwo, P_GATE: gw_gu, P_UP: gw_gu, P_DOWN: gwd}
    pairs = {P_IN: pair_i, P_OUT: pair_o, P_GATE: pair_g, P_UP: pair_u, P_DOWN: pair_d}
    flights = {P_IN: flight_i, P_OUT: flight_o, P_GATE: flight_g, P_UP: flight_u, P_DOWN: flight_d}
    names = {P_IN: "w_in", P_OUT: "w_out", P_GATE: "w_gate", P_UP: "w_up", P_DOWN: "w_down"}
    bigs = {}

    def adam_big(p, after):
        part = landed(flights[p], after, names[p])
        res = _adam_big(full_grads[p], pairs[p], part, w_loc[p], m_loc[p], v_loc[p], "adam_" + names[p], up0 if p == P_UP else 0)
        bigs[names[p]] = [(o.T if p in (P_IN, P_GATE, P_UP) else o)[None] for o in res]
        return res[1]

    for p in (P_DOWN, P_GATE, P_UP, P_OUT):
        token = adam_big(p, token)
    ptot = _total_small(flight_s[1], landed(flight_s, token, "small"))
    half = d // 2
    loss = (0.5 / d) * jnp.sum(ptot[21, half:])
    g_meta = lax.dynamic_slice(ptot, (0, me * (d // N_DEV)), (N_META, d // N_DEV))
    g_small = [g_meta, ptot[16:17], lax.dynamic_slice(ptot, (24, me * ca_loc), (ka, ca_loc))[None],
               lax.dynamic_slice(ptot, (32, me * cb_loc), (kb, cb_loc))[None],
               ptot[20:21, :half], ptot[20:21, half:], ptot[21:22, :half], ptot[17:18], ptot[18:19], ptot[19:20]]
    w_small = [meta_tokens, pre_mix_norm, conv_a_w, conv_b_w, conv_b_bias, ln_b_gain, ln_b_bias, post_mix_norm,
               pre_ffn_norm, post_ffn_norm]
    m_small = [m_meta_tokens, m_pre_mix_norm, m_conv_a_w, m_conv_b_w, m_conv_b_bias, m_ln_b_gain, m_ln_b_bias,
               m_post_mix_norm, m_pre_ffn_norm, m_post_ffn_norm]
    v_small = [v_meta_tokens, v_pre_mix_norm, v_conv_a_w, v_conv_b_w, v_conv_b_bias, v_ln_b_gain, v_ln_b_bias,
               v_post_mix_norm, v_pre_ffn_norm, v_post_ffn_norm]
    small = _adam_small(g_small, w_small, m_small, v_small)
    n_small = len(w_small)
    d_small, nm_small, nv_small = small[:n_small], small[n_small:2 * n_small], small[2 * n_small:]

    adam_big(P_IN, small[0])

    def ordered(pick_small, pick_big):
        sm_it = iter(range(n_small))
        out = []
        for name in ("s", "s", "w_in", "s", "s", "s", "s", "s", "w_out", "s", "s", "w_gate", "w_up", "w_down", "s"):
            out.append(pick_small(next(sm_it)) if name == "s" else pick_big(name))
        return out

    grads = ordered(lambda i: g_small[i], lambda n: bigs[n][0])
    deltas = ordered(lambda i: d_small[i], lambda n: bigs[n][1])
    new_m = ordered(lambda i: nm_small[i], lambda n: bigs[n][2])
    new_v = ordered(lambda i: nv_small[i], lambda n: bigs[n][3])
    return (loss, grad_x, *grads, *deltas, *new_m, *new_v)
```

```python
import jax
import jax.numpy as jnp
from jax import lax
from jax.experimental import pallas as pl
from jax.experimental.pallas import tpu as pltpu

F32 = jnp.float32
BF16 = jnp.bfloat16
MESH = pl.DeviceIdType.MESH

N_META = 16
N_DEV = 8
RMS_EPS = 1e-6
LN_EPS = 1e-5
ADAM_LR = 0.001
ADAM_B1 = 0.9
ADAM_B2 = 0.999
ADAM_EPS = 1e-08
ADAM_WD = 0.01
ADAM_STEP = 10

LANE = 128
SUBLANE = 8
BF16_ROWS = 16
ROW_ALIGN = 128
N_ROW_BLOCKS = 4
CONV_HALO = 32
CONV_CHUNK = 64
WGRAD_ROWS = 32
N_CHUNK = 512
WGRAD_TILE_MAX = 1408
ADD_CHUNK = 32
ADAM_COL_BLOCKS = 2
COPY_PIECES = 4
V7X_VMEM_BYTES = 64 * 1024 * 1024
VMEM_LIMIT = V7X_VMEM_BYTES - 6 * 1024 * 1024
SMALL_ROWS = 64
SM_ROWS = 56
N_BIG = 5

ANY = pl.BlockSpec(memory_space=pl.ANY)
VMEM = pl.BlockSpec(memory_space=pltpu.VMEM)


def _cparams(n_grid_axes=0):
    sem = ("arbitrary",) * n_grid_axes if n_grid_axes else None
    return pltpu.CompilerParams(dimension_semantics=sem, vmem_limit_bytes=VMEM_LIMIT)


def _mesh_pos():
    return lax.axis_index("x"), lax.axis_index("y"), lax.axis_index("c")


def _dev_index(px, py, pc):
    return 4 * px + 2 * py + pc


def _other_chips(x, y):
    return [(1 - x, y), (x, 1 - y), (1 - x, 1 - y)]


def _full(shape):
    return pl.BlockSpec(shape, lambda *_: (0,) * len(shape))


def _resident(shape):
    return pl.BlockSpec(shape, lambda *_: (0,) * len(shape), pipeline_mode=pl.Buffered(1))


def _dot_nt(a, w):
    return lax.dot_general(a, w, (((1,), (1,)), ((), ())), preferred_element_type=F32)


def _dot_nn(a, w):
    return jnp.dot(a, w, preferred_element_type=F32)


def _chunks(n, c):
    out, o = [], 0
    while o < n:
        out.append((o, min(c, n - o)))
        o += c
    return out


def _rstd(h):
    return lax.rsqrt(jnp.mean(h * h, axis=-1, keepdims=True) + RMS_EPS)


def _rms_bwd(dyh, yh, r):
    return r * (dyh - yh * jnp.mean(dyh * yh, axis=-1, keepdims=True))


def _silu_grad(a, sig):
    return sig * (1.0 + a * (1.0 - sig))


def _acc_rows(ref, val, first):
    s = jnp.sum(val, axis=0, keepdims=True)

    @pl.when(first)
    def _():
        ref[...] = s

    @pl.when(jnp.logical_not(first))
    def _():
        ref[...] += s


def _row_loop(t_rows, chunk, fn, carry=None):
    def step(i, c):
        return fn(pl.multiple_of(i * chunk, chunk), c)

    return lax.fori_loop(0, t_rows // chunk, step, carry)


def _remote(src, dst, send_sem, recv_sem, to):
    return pltpu.make_async_remote_copy(src_ref=src, dst_ref=dst, send_sem=send_sem, recv_sem=recv_sem,
                                        device_id=to, device_id_type=MESH)


class _Comm:
    def __init__(self, inputs, out_shapes, aliases, scratch, start, finish):
        self.inputs, self.out_shapes, self.aliases, self.scratch = list(inputs), list(out_shapes), dict(aliases), list(scratch)
        self.start, self.finish = start, finish


def _merge_comms(comms):
    inputs, out_shapes, aliases, scratch, spans = [], [], {}, [], []
    for cm in comms:
        spans.append((len(inputs), len(out_shapes), len(scratch), cm))
        aliases.update({len(inputs) + k: len(out_shapes) + v for k, v in cm.aliases.items()})
        inputs += cm.inputs
        out_shapes += cm.out_shapes
        scratch += cm.scratch

    def run(which):
        def fn(ins, outs, scr):
            for i0, o0, s0, cm in spans:
                getattr(cm, which)(ins[i0:i0 + len(cm.inputs)], outs[o0:o0 + len(cm.out_shapes)], scr[s0:s0 + len(cm.scratch)])
        return fn

    return _Comm(inputs, out_shapes, aliases, scratch, run("start"), run("finish"))


def _host_call(body, *, grid, in_specs, out_specs, out_shape, args, name, scratch_shapes=(), comm=None, after=()):
    talks = comm is not None
    if comm is None:
        comm = _Comm([], [], {}, [], lambda *_: None, lambda *_: None)
    n_in, n_out, n_scr = len(args), len(out_shape), len(scratch_shapes)
    c_in, c_out = len(comm.inputs), len(comm.out_shapes)
    n_after = len(after)

    def open_comm(c_ins, c_outs, c_scr):
        if talks:
            _pair_handshake()
        comm.start(c_ins, c_outs, c_scr)

    def hosted(*refs):
        ins, c_ins = refs[:n_in], refs[n_in:n_in + c_in]
        o0 = n_in + c_in + n_after
        outs, c_outs = refs[o0:o0 + n_out], refs[o0 + n_out:o0 + n_out + c_out]
        s0 = o0 + n_out + c_out
        scr, c_scr = refs[s0:s0 + n_scr], refs[s0 + n_scr:]
        if not grid:
            open_comm(c_ins, c_outs, c_scr)
            body(*ins, *outs, *scr)
            comm.finish(c_ins, c_outs, c_scr)
            return
        first = last = None
        for a, n in enumerate(grid):
            f, l = pl.program_id(a) == 0, pl.program_id(a) == n - 1
            first = f if first is None else jnp.logical_and(first, f)
            last = l if last is None else jnp.logical_and(last, l)

        @pl.when(first)
        def _():
            open_comm(c_ins, c_outs, c_scr)

        body(*ins, *outs, *scr)

        @pl.when(last)
        def _():
            comm.finish(c_ins, c_outs, c_scr)

    sem = ("arbitrary",) * len(grid) if grid else None
    params = pltpu.CompilerParams(dimension_semantics=sem, vmem_limit_bytes=VMEM_LIMIT,
                                  collective_id=PAIR_BARRIER_ID if talks else None)
    res = pl.pallas_call(
        hosted, grid=grid, in_specs=list(in_specs) + [ANY] * (c_in + n_after), out_specs=list(out_specs) + [ANY] * c_out,
        out_shape=list(out_shape) + comm.out_shapes, scratch_shapes=list(scratch_shapes) + comm.scratch,
        input_output_aliases={n_in + k: n_out + v for k, v in comm.aliases.items()},
        name=name, compiler_params=params)(*args, *comm.inputs, *after)
    return list(res[:n_out]), list(res[n_out:])


PAIR_BARRIER_ID = 0
START_BARRIER_IDS = (1, 2, 3, 4, 5, 6)
ALL_PEERS_BARRIER_ID = 7


def _chips_handshake():
    x, y, c = _mesh_pos()
    barrier = pltpu.get_barrier_semaphore()
    for chip in _other_chips(x, y):
        pl.semaphore_signal(barrier, inc=1, device_id=(*chip, c), device_id_type=MESH)
    pl.semaphore_wait(barrier, 3)


def _pair_handshake():
    x, y, c = _mesh_pos()
    barrier = pltpu.get_barrier_semaphore()
    pl.semaphore_signal(barrier, inc=1, device_id=(x, y, 1 - c), device_id_type=MESH)
    pl.semaphore_wait(barrier, 1)


GATHER_SEMS = 10


class _Gather:
    def __init__(self, jobs, rows, lo, src_ref, dests, send_sems, recv_sems):
        x, y, c = _mesh_pos()
        me, sib = (x, y, c), (x, y, 1 - c)
        nx, ny, dg = (1 - x, y, c), (x, 1 - y, c), (1 - x, 1 - y, c)
        self.relayed, self.direct, self.relay, self.to_sib, self.sib_fwd = [], [], [], [], []
        for n, (p, r0, nr) in enumerate(jobs):
            assert nr % (2 * BF16_ROWS) == 0
            half = nr // 2

            def rows_of(dev, h, p=p, r0=r0, nr=nr, half=half):
                off, cnt = (r0, nr) if h is None else (r0 + h * half, half)
                return dests[p].at[pl.ds(pl.multiple_of(_dev_index(*dev) * rows[p] + off, BF16_ROWS), cnt), :]

            def mine(h, p=p, r0=r0, nr=nr, half=half):
                off, cnt = (r0, nr) if h is None else (r0 + h * half, half)
                return src_ref.at[pl.ds(lo[p] + off, cnt), :]

            sem = lambda k, n=n: (send_sems.at[GATHER_SEMS * n + k], recv_sems.at[GATHER_SEMS * n + k])
            self.relayed.append([_remote(mine(0), rows_of(me, 0), *sem(0), nx), _remote(mine(1), rows_of(me, 1), *sem(3), ny)])
            self.direct.append([_remote(mine(1), rows_of(me, 1), *sem(1), nx), _remote(mine(0), rows_of(me, 0), *sem(2), ny)])
            self.relay.append([_remote(rows_of(nx, 0), rows_of(nx, 0), *sem(4), ny), _remote(rows_of(ny, 1), rows_of(ny, 1), *sem(5), nx)])
            self.to_sib.append(_remote(mine(None), rows_of(me, None), *sem(6), sib))
            self.sib_fwd.append([_remote(rows_of(dev, None), rows_of(dev, None), *sem(7 + i), sib) for i, dev in enumerate((nx, ny, dg))])

    def start(self):
        for group in (self.relayed, self.direct):
            for cps in group:
                for cp in cps:
                    cp.start()
        for cp in self.to_sib:
            cp.start()

    def mid(self):
        for first, relay in zip(self.relayed, self.relay):
            for arrived, onward in zip(first, relay):
                arrived.wait_recv()
                onward.start()

    def finish(self):
        for direct, relay, fwd in zip(self.direct, self.relay, self.sib_fwd):
            for k in range(2):
                direct[k].wait_recv()
                fwd[k].start()
            for cp in relay:
                cp.wait_recv()
            fwd[2].start()
        for n in range(len(self.to_sib)):
            self.to_sib[n].wait_recv()
            for cp in self.sib_fwd[n]:
                cp.wait_recv()
            for cp in self.relayed[n] + self.direct[n] + self.relay[n] + [self.to_sib[n]] + self.sib_fwd[n]:
                cp.wait_send()


HBM = pl.BlockSpec(memory_space=pltpu.HBM)
SEM = pl.BlockSpec(memory_space=pltpu.SEMAPHORE)
FLOWS = pltpu.SideEffectType.DATAFLOW_SIDE_EFFECTING


def _in_hbm(a):
    return pltpu.with_memory_space_constraint(a, pltpu.HBM)


def _gather_start(wl, dests, ps, rows, barrier_id):
    lo = [sum(rows[:p]) for p in range(N_BIG)]
    n = len(ps)

    def body(*refs):
        wl_ref, dest_refs = refs[0], refs[1:1 + n]
        sends, recvs = refs[1 + n:1 + 2 * n], refs[1 + 2 * n:1 + 3 * n]
        token = refs[-1]
        _chips_handshake()
        x, y, c = _mesh_pos()
        jme = _dev_index(x, y, c)
        for i, p in enumerate(ps):
            mine = dest_refs[i].at[pl.ds(pl.multiple_of(jme * rows[p], BF16_ROWS), rows[p]), :]
            for chip in _other_chips(x, y):
                _remote(wl_ref.at[pl.ds(lo[p], rows[p]), :], mine, sends[i], recvs[i], (*chip, c)).start()
        token[...] = jnp.zeros_like(token)

    thru = [pltpu.HBM(wl.shape, wl.dtype)] + [pltpu.HBM(dests[p].shape, BF16) for p in ps]
    res = pl.pallas_call(
        body, name="gather_start",
        out_shape=tuple([pltpu.SemaphoreType.DMA(())] * (2 * n) + thru + [jax.ShapeDtypeStruct((SUBLANE, LANE), F32)]),
        in_specs=[HBM] * (1 + n), out_specs=tuple([SEM] * (2 * n) + [HBM] * (1 + n) + [VMEM]),
        input_output_aliases={i: 2 * n + i for i in range(1 + n)},
        compiler_params=pltpu.CompilerParams(has_side_effects=FLOWS, collective_id=barrier_id))(
            _in_hbm(wl), *[_in_hbm(dests[p]) for p in ps])
    sems = [(res[i], res[n + i]) for i in range(n)]
    return sems, res[2 * n], list(res[2 * n + 1:3 * n + 1]), res[-1]


def _gather_wait(wl, dest, sems, after, r, name):
    def body(wl_ref, dest_ref, send_sem, recv_sem, after_ref, wl_out, dest_out):
        x, y, c = _mesh_pos()
        three = dest_ref.at[pl.ds(0, 3 * r), :]
        cp = _remote(three, three, send_sem, recv_sem, (x, y, 1 - c))
        cp.wait_send()
        cp.wait_recv()

    res = pl.pallas_call(
        body, name=name, out_shape=(pltpu.HBM(wl.shape, wl.dtype), pltpu.HBM(dest.shape, dest.dtype)),
        in_specs=[HBM, HBM, SEM, SEM, ANY], out_specs=(HBM, HBM), input_output_aliases={0: 0, 1: 1},
        compiler_params=pltpu.CompilerParams(has_side_effects=FLOWS))(wl, dest, sems[0], sems[1], after)
    return res[0], res[1]


def _forward_comm(dest, r):
    def descs(ins, outs, scr):
        x, y, c = _mesh_pos()
        cps = []
        for k, chip in enumerate(_other_chips(x, y)):
            blk = outs[0].at[pl.ds(pl.multiple_of(_dev_index(*chip, c) * r, BF16_ROWS), r), :]
            cps.append(_remote(blk, blk, scr[0].at[k], scr[1].at[k], (x, y, 1 - c)))
        return cps

    def start(ins, outs, scr):
        for cp in descs(ins, outs, scr):
            cp.start()

    def finish(ins, outs, scr):
        cps = descs(ins, outs, scr)
        for cp in cps:
            cp.wait_recv()
        for cp in cps:
            cp.wait_send()

    return _Comm([dest], [jax.ShapeDtypeStruct(dest.shape, dest.dtype)], {0: 0},
                 [pltpu.SemaphoreType.DMA((3,)), pltpu.SemaphoreType.DMA((3,))], start, finish)


def _forward_now(dest, r, name):
    _, (dest,) = _host_call(lambda: None, grid=(), in_specs=[], out_specs=[], out_shape=[], args=(), name=name,
                            comm=_forward_comm(dest, r))
    return dest


def _pair_comm(g, r, row0=0):
    d = g.shape[1]

    def descs(ins, outs, scr):
        x, y, c = _mesh_pos()
        chips = [(x, y)] + _other_chips(x, y)
        return [_remote(ins[0].at[pl.ds(pl.multiple_of(row0 + _dev_index(*chip, 1 - c) * r, BF16_ROWS), r), :], outs[0].at[k],
                        scr[0].at[k], scr[1].at[k], (x, y, 1 - c)) for k, chip in enumerate(chips)]

    def start(ins, outs, scr):
        for cp in descs(ins, outs, scr):
            cp.start()

    def finish(ins, outs, scr):
        cps = descs(ins, outs, scr)
        for cp in cps:
            cp.wait_recv()
        for cp in cps:
            cp.wait_send()

    comm = _Comm([g], [jax.ShapeDtypeStruct((4, r, d), BF16)], {},
                 [pltpu.SemaphoreType.DMA((4,)), pltpu.SemaphoreType.DMA((4,))], start, finish)
    return comm


def _pair_sum(g, pair, r, name, after=(), row0=0):
    d = g.shape[1]

    def body(g_ref, p_ref, *rest):
        o_ref, gbuf, pbuf, sems = rest[len(after):]
        x, y, c = _mesh_pos()
        loads = [pltpu.make_async_copy(p_ref.at[pl.ds(1, 3)], pbuf, sems.at[3])]
        for k, chip in enumerate(_other_chips(x, y)):
            j = _dev_index(*chip, c)
            loads.append(pltpu.make_async_copy(g_ref.at[pl.ds(pl.multiple_of(row0 + j * r, BF16_ROWS), r), :], gbuf.at[k], sems.at[k]))
        for cp in loads:
            cp.start()
        for cp in loads:
            cp.wait()
        for k in range(3):
            o_ref[k] = (gbuf[k].astype(F32) + pbuf[k].astype(F32)).astype(BF16)

    return pl.pallas_call(
        body, out_shape=jax.ShapeDtypeStruct((3, r, d), BF16), in_specs=[ANY] * (2 + len(after)), out_specs=VMEM,
        scratch_shapes=[pltpu.VMEM((3, r, d), BF16), pltpu.VMEM((3, r, d), BF16), pltpu.SemaphoreType.DMA((4,))],
        name=name, compiler_params=_cparams())(g, pair, *after)


def _chip_start(sums, name, barrier_id):
    n = len(sums)

    def body(*refs):
        srcs, lands = refs[:n], refs[n:2 * n]
        sends, recvs = refs[2 * n:3 * n], refs[3 * n:4 * n]
        _chips_handshake()
        x, y, c = _mesh_pos()
        for i in range(n):
            for k, chip in enumerate(_other_chips(x, y)):
                _remote(srcs[i].at[k], lands[i].at[k], sends[i], recvs[i], (*chip, c)).start()
        refs[-1][...] = jnp.zeros_like(refs[-1])

    zones = [pltpu.HBM(s.shape, s.dtype) for s in sums]
    res = pl.pallas_call(
        body, name=name,
        out_shape=tuple([pltpu.SemaphoreType.DMA(())] * (2 * n) + zones + zones + [jax.ShapeDtypeStruct((SUBLANE, LANE), F32)]),
        in_specs=[HBM] * (2 * n), out_specs=tuple([SEM] * (2 * n) + [HBM] * (2 * n) + [VMEM]),
        input_output_aliases={i: 2 * n + i for i in range(2 * n)},
        compiler_params=pltpu.CompilerParams(has_side_effects=FLOWS, collective_id=barrier_id))(
            *[_in_hbm(s) for s in sums], *[_in_hbm(lax.empty(s.shape, s.dtype)) for s in sums])
    flights = [((res[i], res[n + i]), res[2 * n + i], res[3 * n + i]) for i in range(n)]
    return flights, res[-1]


def _chip_wait(sums, land, sems, after, name):
    def body(sums_ref, land_ref, send_sem, recv_sem, after_ref, sums_out, land_out):
        x, y, c = _mesh_pos()
        cp = _remote(sums_ref, land_ref, send_sem, recv_sem, (x, y, 1 - c))
        cp.wait_send()
        cp.wait_recv()

    res = pl.pallas_call(
        body, name=name, out_shape=(pltpu.HBM(sums.shape, sums.dtype), pltpu.HBM(land.shape, land.dtype)),
        in_specs=[HBM, HBM, SEM, SEM, ANY], out_specs=(HBM, HBM), input_output_aliases={0: 0, 1: 1},
        compiler_params=pltpu.CompilerParams(has_side_effects=FLOWS))(sums, land, sems[0], sems[1], after)
    return res[1]


class _CopyThrough:
    def __init__(self, src_ref, dst_ref, dst_row0, n_rows, buf, sem_in, sem_out):
        rc = n_rows // COPY_PIECES
        piece = lambda ref, o: ref.at[pl.ds(o, rc), :]
        self.loads = [pltpu.make_async_copy(piece(src_ref, k * rc), piece(buf, k * rc), sem_in) for k in range(COPY_PIECES)]
        self.stores = [pltpu.make_async_copy(piece(buf, k * rc), piece(dst_ref, dst_row0 + k * rc), sem_out) for k in range(COPY_PIECES)]
        self.all_in = pltpu.make_async_copy(src_ref, buf, sem_in)
        self.all_out = pltpu.make_async_copy(buf, dst_ref.at[pl.ds(dst_row0, n_rows), :], sem_out)

    def load(self):
        for cp in self.loads:
            cp.start()

    def store(self):
        self.all_in.wait()
        for cp in self.stores:
            cp.start()

    def done(self):
        self.all_out.wait()


def _gather_first(shards, sm, jobs, x2, tgt2, t_rows, x0):
    d = shards[0].shape[1]
    rows = [w.shape[0] for w in shards]
    lo = [sum(rows[:p]) for p in range(N_BIG)]
    n_sems = GATHER_SEMS * len(jobs)
    seq = x2.shape[0]
    assert x0 == ROW_ALIGN and seq % ROW_ALIGN == 0 and d == N_DEV * LANE

    def body(s0, s1, s2, s3, s4, sm_ref, x_ref, tgt_ref, wl_ref, o0, o1, o2, o3, o4, sa_ref, h0_ref, tp_ref,
             wl_v, x_v, tgt_v, heads_v, sa_v, send_sems, recv_sems, ssend, srecv, local_sems, sems_in, sems_out):
        dests = (o0, o1, o2, o3, o4)
        x, y, c = _mesh_pos()
        me = (x, y, c)
        jme = _dev_index(*me)
        peers = [(x, y, 1 - c)] + [(*chip, pc) for pc in (c, 1 - c) for chip in _other_chips(x, y)]
        barrier = pltpu.get_barrier_semaphore()
        for to in peers:
            pl.semaphore_signal(barrier, inc=1, device_id=to, device_id_type=MESH)
        pl.semaphore_wait(barrier, len(peers))
        padded = [_CopyThrough(x_ref, h0_ref, x0, seq, x_v, sems_in.at[0], sems_out.at[0]),
                  _CopyThrough(tgt_ref, tp_ref, x0, seq, tgt_v, sems_in.at[1], sems_out.at[1])]
        for cp in padded:
            cp.load()
        shard_refs = (s0, s1, s2, s3, s4)
        first = sorted({j[0] for j in jobs})
        for p in first + [p for p in range(N_BIG) if p not in first]:
            wl_v[pl.ds(lo[p], rows[p]), :] = shard_refs[p][...].astype(BF16)
            if p == first[-1]:
                gather = _Gather(jobs, rows, lo, wl_v, dict(enumerate(dests)), send_sems, recv_sems)
                gather.start()
        smalls = [_remote(sm_ref, sa_ref.at[jme], ssend.at[k], srecv.at[k], to) for k, to in enumerate(peers)]
        for cp in smalls:
            cp.start()
        mine = [pltpu.make_async_copy(wl_v.at[pl.ds(lo[p], rows[p]), :],
                                      dests[p].at[pl.ds(pl.multiple_of(jme * rows[p], BF16_ROWS), rows[p]), :], local_sems.at[p])
                for p in range(N_BIG)]
        mine.append(pltpu.make_async_copy(wl_v, wl_ref, local_sems.at[N_BIG]))
        mine.append(pltpu.make_async_copy(sm_ref, sa_ref.at[jme], local_sems.at[N_BIG + 1]))
        for cp in mine:
            cp.start()
        later = [p for p in range(N_BIG) if p not in {j[0] for j in jobs}]
        own = [_remote(wl_v.at[pl.ds(lo[p], rows[p]), :], dests[p].at[pl.ds(pl.multiple_of(jme * rows[p], BF16_ROWS), rows[p]), :],
                       ssend.at[7 + i], srecv.at[7 + i], (x, y, 1 - c)) for i, p in enumerate(later)]
        for cp in own:
            cp.start()
        gather.mid()
        for cp in padded:
            cp.store()
        for cp in smalls + own:
            cp.wait_recv()
        mine[-1].wait()
        to_v = pltpu.make_async_copy(sa_ref, sa_v, local_sems.at[N_BIG + 1])
        to_v.start()
        to_v.wait()
        head, zeros = heads_v.at[0], heads_v.at[1]
        head[...] = jnp.zeros_like(head)
        zeros[...] = jnp.zeros_like(zeros)
        for j in range(N_DEV):
            head[pl.ds(x0 - N_META, N_META), pl.ds(j * LANE, LANE)] = sa_v[j, pl.ds(0, N_META), :]
        heads = [pltpu.make_async_copy(head, h0_ref.at[pl.ds(0, x0), :], local_sems.at[N_BIG + 1]),
                 pltpu.make_async_copy(zeros, tp_ref.at[pl.ds(0, x0), :], local_sems.at[N_BIG + 2])]
        for cp in heads:
            cp.start()
        gather.finish()
        for cp in smalls + own:
            cp.wait_send()
        for cp in mine[:-1] + heads:
            cp.wait()
        for cp in padded:
            cp.done()

    out_shape = [jax.ShapeDtypeStruct((sum(rows), d), BF16)]
    out_shape += [jax.ShapeDtypeStruct((N_DEV * r, d), BF16) for r in rows]
    out_shape.append(jax.ShapeDtypeStruct((N_DEV,) + sm.shape, F32))
    out_shape += [jax.ShapeDtypeStruct((t_rows, d), F32)] * 2
    res = pl.pallas_call(
        body, out_shape=out_shape, in_specs=[VMEM] * 6 + [ANY] * 2, out_specs=[ANY] * 9,
        scratch_shapes=[pltpu.VMEM((sum(rows), d), BF16), pltpu.VMEM((seq, d), F32), pltpu.VMEM((seq, d), F32),
                        pltpu.VMEM((2, ROW_ALIGN, d), F32), pltpu.VMEM((N_DEV,) + sm.shape, F32),
                        pltpu.SemaphoreType.DMA((n_sems,)), pltpu.SemaphoreType.DMA((n_sems,)),
                        pltpu.SemaphoreType.DMA((7 + N_BIG,)), pltpu.SemaphoreType.DMA((7 + N_BIG,)),
                        pltpu.SemaphoreType.DMA((N_BIG + 3,)), pltpu.SemaphoreType.DMA((2,)), pltpu.SemaphoreType.DMA((2,))],
        name="gather_first",
        compiler_params=pltpu.CompilerParams(vmem_limit_bytes=VMEM_LIMIT, collective_id=ALL_PEERS_BARRIER_ID))(*shards, sm, x2, tgt2)
    return res[0], list(res[1:1 + N_BIG]), res[1 + N_BIG], res[2 + N_BIG], res[3 + N_BIG]


def _in_proj(h0, g1, win_t, tm, comm):
    t_rows, d = h0.shape
    e = win_t.shape[0]

    def body(h_ref, g_ref, w_ref, xn_ref, hin_ref):
        h = h_ref[...]
        xn = ((h * _rstd(h)) * g_ref[...]).astype(BF16)
        xn_ref[...] = xn
        for o, n in _chunks(e, N_CHUNK):
            hin_ref[:, pl.ds(o, n)] = _dot_nt(xn, w_ref[pl.ds(o, n), :])

    return _host_call(
        body, grid=(t_rows // tm,),
        in_specs=[pl.BlockSpec((tm, d), lambda i: (i, 0)), _full((1, d)), _resident((e, d))],
        out_specs=[pl.BlockSpec((tm, d), lambda i: (i, 0)), pl.BlockSpec((tm, e), lambda i: (i, 0))],
        out_shape=[jax.ShapeDtypeStruct((t_rows, d), BF16), jax.ShapeDtypeStruct((t_rows, e), F32)],
        args=(h0, g1, win_t), name="in_proj", comm=comm)


def _tap_slot(off):
    return off % SUBLANE, (off // SUBLANE) * SUBLANE


def _fill_shifted(sh_ref, base_ref, residues, n_rows):
    for r in residues:
        if r:
            sh_ref[r] = base_ref[pl.ds(r, n_rows), :]


def _shifted_rows(pair, r, start, n):
    base_ref, sh_ref = pair
    return base_ref[pl.ds(start, n), :] if r == 0 else sh_ref[r, pl.ds(start, n), :]


def _mix_conv_fwd(hin, wa, wb, bb, wa_w, comm):
    t_rows = hin.shape[0]
    nt = wa_w // LANE
    ka, kb = wa.shape[0], wb.shape[0]
    nr = CONV_HALO + t_rows

    def body(bg_ref, cg_ref, ha_ref, val_ref, gt_ref, wa_ref, wb_ref, bb_ref, ya_ref, z_ref, base, sh):
        base[pl.ds(0, CONV_HALO), :] = jnp.zeros((CONV_HALO, LANE), F32)
        base[pl.ds(nr, SUBLANE), :] = jnp.zeros((SUBLANE, LANE), F32)

        def conv(w_ref, k_taps, b, n):
            acc = None
            for k in range(k_taps):
                r, q = _tap_slot(CONV_HALO - (k_taps - 1) + k)
                term = w_ref[pl.ds(k, 1), :] * _shifted_rows((base, sh), r, b + q, n)
                acc = term if acc is None else acc + term
            return acc

        def fill_a(b, c):
            base[pl.ds(CONV_HALO + b, CONV_CHUNK), :] = cg_ref[pl.ds(b, CONV_CHUNK), :] * ha_ref[pl.ds(b, CONV_CHUNK), :]
            return c

        _row_loop(t_rows, CONV_CHUNK, fill_a)
        _fill_shifted(sh, base, sorted({_tap_slot(CONV_HALO - (ka - 1) + k)[0] for k in range(ka)}), nr)

        def out_a(b, c):
            ya_ref[pl.ds(b, CONV_CHUNK), :] = (bg_ref[pl.ds(b, CONV_CHUNK), :] * conv(wa_ref, ka, b, CONV_CHUNK)).astype(BF16)
            return c

        _row_loop(t_rows, CONV_CHUNK, out_a)

        def fill_b(b, c):
            base[pl.ds(CONV_HALO + b, CONV_CHUNK), :] = (val_ref[pl.ds(b, CONV_CHUNK), :]
                                                          * jax.nn.sigmoid(gt_ref[pl.ds(b, CONV_CHUNK), :]))
            return c

        _row_loop(t_rows, CONV_CHUNK, fill_b)
        _fill_shifted(sh, base, range(SUBLANE), nr)

        def out_b(b, c):
            z_ref[pl.ds(b, CONV_CHUNK), :] = conv(wb_ref, kb, b, CONV_CHUNK) + bb_ref[...]
            return c

        _row_loop(t_rows, CONV_CHUNK, out_b)

    def col(g):
        return pl.BlockSpec((t_rows, LANE), lambda i, g=g: (0, g * nt + i))

    tile = lambda rows: pl.BlockSpec((rows, LANE), lambda i: (0, i))
    return _host_call(
        body, grid=(nt,),
        in_specs=[col(0), col(1), col(2), col(3), col(4), tile(ka), tile(kb), tile(1)],
        out_specs=[tile(t_rows), tile(t_rows)],
        out_shape=[jax.ShapeDtypeStruct((t_rows, wa_w), BF16), jax.ShapeDtypeStruct((t_rows, wa_w), F32)],
        scratch_shapes=[pltpu.VMEM((nr + SUBLANE, LANE), F32), pltpu.VMEM((SUBLANE, nr, LANE), F32)],
        args=(hin, hin, hin, hin, hin, wa, wb, bb), name="mix_conv_fwd", comm=comm)


def _ln_parts(z, lg, lb):
    mu = jnp.mean(z, axis=-1, keepdims=True)
    zc = z - mu
    rstd = lax.rsqrt(jnp.mean(zc * zc, axis=-1, keepdims=True) + LN_EPS)
    zh = zc * rstd
    return zh, rstd, zh * lg + lb


def _out_proj(ya, z, lg, lb, w_out, h0, g2, g3, tm, comm):
    t_rows, d = h0.shape
    w = z.shape[1]

    def body(ya_ref, z_ref, lg_ref, lb_ref, w_ref, h0_ref, g2_ref, g3_ref, y_ref, mix_ref, h1_ref, xn2_ref):
        _, _, ln = _ln_parts(z_ref[...], lg_ref[...], lb_ref[...])
        y_ref[:, pl.ds(0, w)] = ya_ref[...]
        y_ref[:, pl.ds(w, w)] = (ln * jax.nn.sigmoid(ln)).astype(BF16)
        mix = _dot_nn(y_ref[...], w_ref[...])
        mix_ref[...] = mix
        h1 = h0_ref[...] + (mix * _rstd(mix)) * g2_ref[...]
        h1_ref[...] = h1
        xn2_ref[...] = ((h1 * _rstd(h1)) * g3_ref[...]).astype(BF16)

    blk = pl.BlockSpec((tm, d), lambda i: (i, 0))
    half = pl.BlockSpec((tm, w), lambda i: (i, 0))
    return _host_call(
        body, grid=(t_rows // tm,),
        in_specs=[half, half, _full((1, w)), _full((1, w)), _resident(w_out.shape), blk, _full((1, d)), _full((1, d))],
        out_specs=[blk, blk, blk, blk],
        out_shape=[jax.ShapeDtypeStruct((t_rows, d), BF16), jax.ShapeDtypeStruct((t_rows, d), F32),
                   jax.ShapeDtypeStruct((t_rows, d), F32), jax.ShapeDtypeStruct((t_rows, d), BF16)],
        args=(ya, z, lg, lb, w_out, h0, g2, g3), name="out_proj", comm=comm)


def _gate_up(xn2, wg_t, wu_t, tm, comm):
    t_rows, d = xn2.shape
    f = wg_t.shape[0]

    def body(x_ref, wg_ref, wu_ref, ga_ref, gu_ref, s_ref):
        xn = x_ref[...]
        for o, n in _chunks(f, N_CHUNK):
            a = _dot_nt(xn, wg_ref[pl.ds(o, n), :])
            u = _dot_nt(xn, wu_ref[pl.ds(o, n), :])
            sig = jax.nn.sigmoid(a)
            silu = a * sig
            s = silu * u
            gu_ref[:, pl.ds(o, n)] = silu.astype(BF16)
            ga_ref[:, pl.ds(o, n)] = ((u - s) * sig + s).astype(BF16)
            s_ref[:, pl.ds(o, n)] = s.astype(BF16)

    blk = pl.BlockSpec((tm, f), lambda i: (i, 0))
    return _host_call(
        body, grid=(t_rows // tm,),
        in_specs=[pl.BlockSpec((tm, d), lambda i: (i, 0)), _resident((f, d)), _resident((f, d))],
        out_specs=[blk, blk, blk], out_shape=[jax.ShapeDtypeStruct((t_rows, f), BF16)] * 3,
        args=(xn2, wg_t, wu_t), name="gate_up", comm=comm)


def _down_loss(s, wd, h1, tgt, g4, tm, x0):
    t_rows, d = h1.shape
    f = wd.shape[0]

    def body(s_ref, w_ref, h1_ref, tgt_ref, g4_ref, dh2_ref, dff_ref, dg4_ref, loss_ref):
        i = pl.program_id(0)
        ff = _dot_nn(s_ref[...], w_ref[...])
        r4 = _rstd(ff)
        fh = ff * r4
        g4 = g4_ref[...]
        h2 = h1_ref[...] + fh * g4
        row = i * tm + lax.broadcasted_iota(jnp.int32, (tm, 1), 0)
        diff = jnp.where(row >= x0, h2 - tgt_ref[...], 0.0)
        dh2 = diff / d
        dh2_ref[...] = dh2
        dff_ref[...] = _rms_bwd(dh2 * g4, fh, r4).astype(BF16)
        _acc_rows(dg4_ref, dh2 * fh, i == 0)
        _acc_rows(loss_ref, diff * diff, i == 0)

    blk = pl.BlockSpec((tm, d), lambda i: (i, 0))
    res, _ = _host_call(
        body, grid=(t_rows // tm,),
        in_specs=[pl.BlockSpec((tm, f), lambda i: (i, 0)), _resident((f, d)), blk, blk, _full((1, d))],
        out_specs=[blk, blk, _full((1, d)), _full((1, d))],
        out_shape=[jax.ShapeDtypeStruct((t_rows, d), F32), jax.ShapeDtypeStruct((t_rows, d), BF16),
                   jax.ShapeDtypeStruct((1, d), F32), jax.ShapeDtypeStruct((1, d), F32)],
        args=(s, wd, h1, tgt, g4), name="down_loss")
    return res


def _bwd_down(dff, wd, ga, gu, tm, comm):
    t_rows, d = dff.shape
    f = wd.shape[0]

    def body(dff_ref, w_ref, ga_ref, gu_ref, dau_ref):
        dff_v = dff_ref[...]
        for o, n in _chunks(f, N_CHUNK):
            ds = _dot_nt(dff_v, w_ref[pl.ds(o, n), :]).astype(BF16)
            dau_ref[0, :, pl.ds(o, n)] = ds * ga_ref[:, pl.ds(o, n)]
            dau_ref[1, :, pl.ds(o, n)] = ds * gu_ref[:, pl.ds(o, n)]

    blk = pl.BlockSpec((tm, f), lambda i: (i, 0))
    (dau,), extra = _host_call(
        body, grid=(t_rows // tm,),
        in_specs=[pl.BlockSpec((tm, d), lambda i: (i, 0)), _resident((f, d)), blk, blk],
        out_specs=[pl.BlockSpec((2, tm, f), lambda i: (0, i, 0))], out_shape=[jax.ShapeDtypeStruct((2, t_rows, f), BF16)],
        args=(dff, wd, ga, gu), name="bwd_down", comm=comm)
    return dau, extra


def _wgrad(a, b, name, after=()):
    d = b.shape[1]
    t_rows = b.shape[0]
    stacked = a.ndim == 3
    n = a.shape[-1]
    groups = a.shape[0] if stacked else 1
    steps = 1 if stacked else 2
    tile = max(t for t in range(LANE, min(n // steps, WGRAD_TILE_MAX) + 1, LANE) if n % t == 0)
    tiles = n // tile

    def body(a_ref, b_ref, o_ref):
        o_ref[...] = lax.dot_general(a_ref[...], b_ref[...], (((0,), (0,)), ((), ())),
                                     preferred_element_type=F32).astype(BF16)

    if stacked:
        a_spec = pl.BlockSpec((None, t_rows, tile), lambda g, i: (g, 0, i))
    else:
        a_spec = pl.BlockSpec((t_rows, tile), lambda g, i: (0, i))
    res, _ = _host_call(
        body, grid=(groups, tiles), in_specs=[a_spec, _resident((t_rows, d))],
        out_specs=[pl.BlockSpec((tile, d), lambda g, i: (g * tiles + i, 0))],
        out_shape=[jax.ShapeDtypeStruct((groups * n, d), BF16)], args=(a, b), name=name, after=after)
    return res[0]


def _bwd_ffn_in(dau, wg_t, wu_t, h1, dh2, g3, tm, comm):
    t_rows, d = h1.shape
    f = wg_t.shape[0]

    def body(dau_ref, wg_ref, wu_ref, h1_ref, dh2_ref, g3_ref, dh1_ref, dg3_ref):
        dxn2 = _dot_nn(dau_ref[0], wg_ref[...]) + _dot_nn(dau_ref[1], wu_ref[...])
        h1 = h1_ref[...]
        r3 = _rstd(h1)
        h1h = h1 * r3
        _acc_rows(dg3_ref, dxn2 * h1h, pl.program_id(0) == 0)
        dh1_ref[...] = dh2_ref[...] + _rms_bwd(dxn2 * g3_ref[...], h1h, r3)

    blk = pl.BlockSpec((tm, d), lambda i: (i, 0))
    return _host_call(
        body, grid=(t_rows // tm,),
        in_specs=[pl.BlockSpec((2, tm, f), lambda i: (0, i, 0)), _resident((f, d)), _resident((f, d)), blk, blk, _full((1, d))],
        out_specs=[blk, _full((1, d))],
        out_shape=[jax.ShapeDtypeStruct((t_rows, d), F32), jax.ShapeDtypeStruct((1, d), F32)],
        args=(dau, wg_t, wu_t, h1, dh2, g3), name="bwd_ffn_in", comm=comm)


def _bwd_out_proj(dh1, mix, w_out, g2, z, lg, lb, tm, after):
    t_rows, d = dh1.shape
    w = z.shape[1]

    def body(dh1_ref, mix_ref, w_ref, g2_ref, z_ref, lg_ref, lb_ref, dmix_ref, dya_ref, dz_ref, dg2_ref, dlg_ref, dlb_ref, dbb_ref):
        first = pl.program_id(0) == 0
        mix = mix_ref[...]
        r2 = _rstd(mix)
        mh = mix * r2
        dh1 = dh1_ref[...]
        _acc_rows(dg2_ref, dh1 * mh, first)
        dmix = _rms_bwd(dh1 * g2_ref[...], mh, r2).astype(BF16)
        dmix_ref[...] = dmix
        dy = _dot_nt(dmix, w_ref[...])
        dya_ref[...] = dy[:, :w]
        lg = lg_ref[...]
        zh, rstd, ln = _ln_parts(z_ref[...], lg, lb_ref[...])
        dln = dy[:, w:] * _silu_grad(ln, jax.nn.sigmoid(ln))
        _acc_rows(dlg_ref, dln * zh, first)
        _acc_rows(dlb_ref, dln, first)
        dzh = dln * lg
        dz = rstd * (dzh - jnp.mean(dzh, axis=-1, keepdims=True) - zh * jnp.mean(dzh * zh, axis=-1, keepdims=True))
        dz_ref[...] = dz
        _acc_rows(dbb_ref, dz, first)

    blk = pl.BlockSpec((tm, d), lambda i: (i, 0))
    half = pl.BlockSpec((tm, w), lambda i: (i, 0))
    vec = _full((1, w))
    res, _ = _host_call(
        body, grid=(t_rows // tm,), in_specs=[blk, blk, _resident(w_out.shape), _full((1, d)), half, vec, vec],
        out_specs=[blk, half, half, _full((1, d)), vec, vec, vec],
        out_shape=[jax.ShapeDtypeStruct((t_rows, d), BF16), jax.ShapeDtypeStruct((t_rows, w), F32),
                   jax.ShapeDtypeStruct((t_rows, w), F32), jax.ShapeDtypeStruct((1, d), F32)]
        + [jax.ShapeDtypeStruct((1, w), F32)] * 3,
        args=(dh1, mix, w_out, g2, z, lg, lb), name="bwd_out_proj", after=after)
    return res


def _mix_conv_bwd(hin, dy, dz, wa, wb, wa_w, comm):
    t_rows = hin.shape[0]
    nt = wa_w // LANE
    ka, kb = wa.shape[0], wb.shape[0]
    nr = CONV_HALO + t_rows
    kb_rows = -(-kb // SUBLANE) * SUBLANE

    def body(bg_ref, cg_ref, ha_ref, val_ref, gt_ref, dya_ref, dz_ref, wa_ref, wb_ref,
             dh_ref, dwa_ref, dwb_ref, base, sh, based, shd, tmp, wbc):
        zeros = lambda n: jnp.zeros((n, LANE), F32)
        base[pl.ds(0, CONV_HALO), :] = zeros(CONV_HALO)
        base[pl.ds(nr, SUBLANE), :] = zeros(SUBLANE)
        based[pl.ds(t_rows, CONV_HALO + SUBLANE), :] = zeros(CONV_HALO + SUBLANE)

        def fwd_slot(k_taps, k):
            return _tap_slot(CONV_HALO - (k_taps - 1) + k)

        def bwd_slot(k_taps, k):
            return _tap_slot(k_taps - 1 - k)

        def conv(w_ref, k_taps, src, slot, b, n):
            acc = None
            for k in range(k_taps):
                r, q = slot(k_taps, k)
                term = w_ref[pl.ds(k, 1), :] * _shifted_rows(src, r, b + q, n)
                acc = term if acc is None else acc + term
            return acc

        def by_residue(k_taps, slot):
            groups = {}
            for k in range(k_taps):
                r, q = slot(k_taps, k)
                groups.setdefault(r, []).append((k, q // SUBLANE))
            return groups

        def wgrad_loop(w_ref, k_taps):
            n_sub = WGRAD_ROWS // SUBLANE
            for k in range(k_taps):
                wbc[k] = jnp.broadcast_to(w_ref[pl.ds(k, 1), :], (SUBLANE, LANE))
            fwd, bwd = by_residue(k_taps, fwd_slot), by_residue(k_taps, bwd_slot)

            def window(src, r, taps, b):
                span = n_sub + max(qi for _, qi in taps)
                return [_shifted_rows(src, r, b + SUBLANE * i, SUBLANE) for i in range(span)]

            def step(b, accs):
                accs = list(accs)
                dv = [based[pl.ds(b + SUBLANE * j, SUBLANE), :] for j in range(n_sub)]
                for r, taps in fwd.items():
                    win = window((base, sh), r, taps, b)
                    for k, qi in taps:
                        t = dv[0] * win[qi]
                        for j in range(1, n_sub):
                            t = t + dv[j] * win[qi + j]
                        accs[k] = accs[k] + t
                outs = [None] * n_sub
                for r, taps in bwd.items():
                    win = window((based, shd), r, taps, b)
                    for k, qi in taps:
                        wk = wbc[k]
                        for j in range(n_sub):
                            term = wk * win[qi + j]
                            outs[j] = term if outs[j] is None else outs[j] + term
                for j in range(n_sub):
                    tmp[pl.ds(b + SUBLANE * j, SUBLANE), :] = outs[j]
                return tuple(accs)

            return _row_loop(t_rows, WGRAD_ROWS, step, tuple(zeros(SUBLANE) for _ in range(k_taps)))

        def store_taps(ref, accs, rows):
            for k, acc in enumerate(accs):
                ref[pl.ds(k, 1), :] = jnp.sum(acc, axis=0, keepdims=True)
            if rows > len(accs):
                ref[pl.ds(len(accs), rows - len(accs)), :] = zeros(rows - len(accs))

        def fill_a(b, c):
            sl = pl.ds(b, CONV_CHUNK)
            base[pl.ds(CONV_HALO + b, CONV_CHUNK), :] = cg_ref[sl, :] * ha_ref[sl, :]
            based[sl, :] = dya_ref[sl, :] * bg_ref[sl, :]
            return c

        _row_loop(t_rows, CONV_CHUNK, fill_a)
        _fill_shifted(sh, base, sorted({fwd_slot(ka, k)[0] for k in range(ka)}), nr)
        _fill_shifted(shd, based, sorted({bwd_slot(ka, k)[0] for k in range(ka)}), nr)

        def d_bgate(b, c):
            sl = pl.ds(b, CONV_CHUNK)
            dh_ref[0, sl, :] = (dya_ref[sl, :] * conv(wa_ref, ka, (base, sh), fwd_slot, b, CONV_CHUNK)).astype(BF16)
            return c

        _row_loop(t_rows, CONV_CHUNK, d_bgate)
        store_taps(dwa_ref, wgrad_loop(wa_ref, ka), SUBLANE)

        def d_ch(b, c):
            sl = pl.ds(b, CONV_CHUNK)
            dua = tmp[sl, :]
            dh_ref[1, sl, :] = (dua * ha_ref[sl, :]).astype(BF16)
            dh_ref[2, sl, :] = (dua * cg_ref[sl, :]).astype(BF16)
            return c

        _row_loop(t_rows, CONV_CHUNK, d_ch)

        def fill_b(b, c):
            sl = pl.ds(b, CONV_CHUNK)
            base[pl.ds(CONV_HALO + b, CONV_CHUNK), :] = val_ref[sl, :] * jax.nn.sigmoid(gt_ref[sl, :])
            based[sl, :] = dz_ref[sl, :]
            return c

        _row_loop(t_rows, CONV_CHUNK, fill_b)
        _fill_shifted(sh, base, range(SUBLANE), nr)
        _fill_shifted(shd, based, range(SUBLANE), nr)
        store_taps(dwb_ref, wgrad_loop(wb_ref, kb), kb_rows)

        def d_glu(b, c):
            sl = pl.ds(b, CONV_CHUNK)
            dgg = tmp[sl, :]
            sig = jax.nn.sigmoid(gt_ref[sl, :])
            dh_ref[3, sl, :] = (dgg * sig).astype(BF16)
            dh_ref[4, sl, :] = (dgg * val_ref[sl, :] * (sig * (1.0 - sig))).astype(BF16)
            return c

        _row_loop(t_rows, CONV_CHUNK, d_glu)

    def col(g):
        return pl.BlockSpec((t_rows, LANE), lambda i, g=g: (0, g * nt + i))

    tile = lambda rows: pl.BlockSpec((rows, LANE), lambda i: (0, i))
    return _host_call(
        body, grid=(nt,),
        in_specs=[col(0), col(1), col(2), col(3), col(4), tile(t_rows), tile(t_rows), tile(ka), tile(kb)],
        out_specs=[pl.BlockSpec((5, t_rows, LANE), lambda i: (0, 0, i)), tile(SUBLANE), tile(kb_rows)],
        out_shape=[jax.ShapeDtypeStruct((5, t_rows, wa_w), BF16), jax.ShapeDtypeStruct((SUBLANE, wa_w), F32),
                   jax.ShapeDtypeStruct((kb_rows, wa_w), F32)],
        scratch_shapes=[pltpu.VMEM((nr + SUBLANE, LANE), F32), pltpu.VMEM((SUBLANE, nr, LANE), F32),
                        pltpu.VMEM((nr + SUBLANE, LANE), F32), pltpu.VMEM((SUBLANE, nr, LANE), F32),
                        pltpu.VMEM((t_rows, LANE), F32), pltpu.VMEM((kb_rows, SUBLANE, LANE), F32)],
        args=(hin, hin, hin, hin, hin, dy, dz, wa, wb), name="mix_conv_bwd", comm=comm)


def _bwd_in_proj(dh5, win_t, h0, dh1, g1, tm, comm):
    t_rows, d = h0.shape
    groups, _, w = dh5.shape

    def body(dh_ref, w_ref, h0_ref, dh1_ref, g1_ref, dh0_ref, dg1_ref):
        dxn1 = None
        for g in range(groups):
            part = _dot_nn(dh_ref[g], w_ref[pl.ds(g * w, w), :])
            dxn1 = part if dxn1 is None else dxn1 + part
        h0 = h0_ref[...]
        r1 = _rstd(h0)
        h0h = h0 * r1
        _acc_rows(dg1_ref, dxn1 * h0h, pl.program_id(0) == 0)
        dh0_ref[...] = dh1_ref[...] + _rms_bwd(dxn1 * g1_ref[...], h0h, r1)

    blk = pl.BlockSpec((tm, d), lambda i: (i, 0))
    return _host_call(
        body, grid=(t_rows // tm,),
        in_specs=[pl.BlockSpec((groups, tm, w), lambda i: (0, i, 0)), _resident(win_t.shape), blk, blk, _full((1, d))],
        out_specs=[blk, _full((1, d))],
        out_shape=[jax.ShapeDtypeStruct((t_rows, d), F32), jax.ShapeDtypeStruct((1, d), F32)],
        args=(dh5, win_t, h0, dh1, g1), name="bwd_in_proj", comm=comm)


def _pair_small(smalls, d):
    (dmeta, dg1, dg2, dg3, dg4, dbb, dlg, dlb, lossv, dwa, dwb) = smalls
    half = d // 2
    kb_rows = dwb.shape[0]

    def body(dmeta_ref, dg1_ref, dg2_ref, dg3_ref, dg4_ref, dbb_ref, dlg_ref, dlb_ref, loss_ref, dwa_ref, dwb_ref,
             sums_ref, pbuf, psib, ps_send, ps_recv):
        x, y, c = _mesh_pos()
        _pair_handshake()
        pbuf[...] = jnp.zeros_like(pbuf)
        pbuf[pl.ds(0, N_META), :] = dmeta_ref[...]
        for row, ref in ((16, dg1_ref), (17, dg2_ref), (18, dg3_ref), (19, dg4_ref)):
            pbuf[pl.ds(row, 1), :] = ref[...]
        pbuf[pl.ds(20, 1), pl.ds(0, half)] = dbb_ref[...]
        pbuf[pl.ds(20, 1), pl.ds(half, half)] = dlg_ref[...]
        pbuf[pl.ds(21, 1), pl.ds(0, half)] = dlb_ref[...]
        lv = loss_ref[...]
        pbuf[pl.ds(21, 1), pl.ds(half, half)] = lv[:, :half] + lv[:, half:]
        pbuf[pl.ds(24, SUBLANE), pl.ds(0, half)] = dwa_ref[...]
        pbuf[pl.ds(32, kb_rows), pl.ds(0, half)] = dwb_ref[...]
        to_sib = _remote(pbuf, psib, ps_send.at[0], ps_recv.at[0], (x, y, 1 - c))
        to_sib.start()
        to_sib.wait_recv()
        s = pbuf[...] + psib[...]
        for k in range(3):
            sums_ref[k] = s
        to_sib.wait_send()

    return pl.pallas_call(
        body, out_shape=jax.ShapeDtypeStruct((3, SMALL_ROWS, d), F32), in_specs=[VMEM] * 11, out_specs=VMEM,
        scratch_shapes=[pltpu.VMEM((SMALL_ROWS, d), F32), pltpu.VMEM((SMALL_ROWS, d), F32),
                        pltpu.SemaphoreType.DMA((1,)), pltpu.SemaphoreType.DMA((1,))],
        name="pair_small", compiler_params=pltpu.CompilerParams(vmem_limit_bytes=VMEM_LIMIT, collective_id=PAIR_BARRIER_ID))(*smalls)


def _total_small(own, others):
    _, r, d = own.shape
    half, cols = d // 2, d // N_DEV

    def body(own_ref, others_ref, tot_ref, meta_ref, g1_ref, g2_ref, g3_ref, g4_ref, dbb_ref, dlg_ref, dlb_ref, loss_ref, chip_p):
        x, y, c = _mesh_pos()
        chip_p[2 * x + y] = own_ref[0]
        for k, (cx, cy) in enumerate(_other_chips(x, y)):
            chip_p[2 * cx + cy] = others_ref[k]
        tot_ref[...] = ((chip_p[0] + chip_p[1]) + chip_p[2]) + chip_p[3]
        meta_ref[...] = tot_ref[pl.ds(0, N_META), pl.ds(pl.multiple_of(_dev_index(x, y, c) * cols, LANE), cols)]
        for row, ref in ((16, g1_ref), (17, g2_ref), (18, g3_ref), (19, g4_ref)):
            ref[...] = tot_ref[pl.ds(row, 1), :]
        dbb_ref[...] = tot_ref[pl.ds(20, 1), pl.ds(0, half)]
        dlg_ref[...] = tot_ref[pl.ds(20, 1), pl.ds(half, half)]
        dlb_ref[...] = tot_ref[pl.ds(21, 1), pl.ds(0, half)]
        loss_ref[...] = (0.5 / d) * jnp.sum(tot_ref[pl.ds(21, 1), pl.ds(half, half)], axis=-1, keepdims=True)

    row = lambda n: jax.ShapeDtypeStruct((1, n), F32)
    return pl.pallas_call(
        body, out_shape=[jax.ShapeDtypeStruct((r, d), F32), jax.ShapeDtypeStruct((N_META, cols), F32), row(d), row(d), row(d), row(d),
                         row(half), row(half), row(half), row(1)],
        scratch_shapes=[pltpu.VMEM((4, r, d), F32)], name="total_small", compiler_params=_cparams())(own, others)


def _adamw(w, g, m, v):
    m = ADAM_B1 * m + (1.0 - ADAM_B1) * g
    v = ADAM_B2 * v + (1.0 - ADAM_B2) * jnp.square(g)
    m_hat = m / (1.0 - ADAM_B1 ** ADAM_STEP)
    v_hat = v / (1.0 - ADAM_B2 ** ADAM_STEP)
    delta = -ADAM_LR * (m_hat / (jnp.sqrt(v_hat) + ADAM_EPS) + ADAM_WD * w)
    return delta, m, v


def _adam_big(g, pair, part, w, m, v, name, row0=0):
    r, d = w.shape
    cols = d // ADAM_COL_BLOCKS
    assert row0 % r == 0

    def body(me_ref, g_ref, pair_ref, part_ref, w_ref, m_ref, v_ref, go_ref, d_ref, mo_ref, vo_ref):
        g = g_ref[...].astype(F32) + pair_ref[...].astype(F32)
        for k in range(3):
            g = g + part_ref[k].astype(F32)
        go_ref[...] = g
        d_ref[...], mo_ref[...], vo_ref[...] = _adamw(w_ref[...], g, m_ref[...], v_ref[...])

    blk = pl.BlockSpec((r, cols), lambda i, me_ref: (0, i))
    grid_spec = pltpu.PrefetchScalarGridSpec(
        num_scalar_prefetch=1, grid=(ADAM_COL_BLOCKS,),
        in_specs=[pl.BlockSpec((r, cols), lambda i, me_ref: (me_ref[0], i)),
                  pl.BlockSpec((None, r, cols), lambda i, me_ref: (0, 0, i)),
                  pl.BlockSpec((3, r, cols), lambda i, me_ref: (0, 0, i)), blk, blk, blk],
        out_specs=[blk, blk, blk, blk])
    me = jnp.reshape(_dev_index(*_mesh_pos()) + row0 // r, (1,)).astype(jnp.int32)
    return pl.pallas_call(body, out_shape=[jax.ShapeDtypeStruct((r, d), F32)] * 4, grid_spec=grid_spec, name=name,
                          compiler_params=_cparams(1))(me, g, pair, part, w, m, v)


def _adam_small(gs, ws, ms, vs):
    n = len(gs)

    def body(*refs):
        ins, outs = refs[:4 * n], refs[4 * n:]
        for i in range(n):
            g = ins[i][...]
            delta, m, v = _adamw(ins[n + i][...], g, ins[2 * n + i][...], ins[3 * n + i][...])
            outs[i][...] = delta
            outs[n + i][...] = m
            outs[2 * n + i][...] = v

    shapes = [jax.ShapeDtypeStruct(w.shape, F32) for w in ws]
    return pl.pallas_call(body, out_shape=shapes * 3, name="adam_small", compiler_params=_cparams())(*gs, *ws, *ms, *vs)


def kernel(x, meta_tokens, pre_mix_norm, w_in, conv_a_w, conv_b_w, conv_b_bias, ln_b_gain, ln_b_bias, w_out, post_mix_norm, pre_ffn_norm, w_gate, w_up, w_down, post_ffn_norm, loss_target, m_meta_tokens, m_pre_mix_norm, m_w_in, m_conv_a_w, m_conv_b_w, m_conv_b_bias, m_ln_b_gain, m_ln_b_bias, m_w_out, m_post_mix_norm, m_pre_ffn_norm, m_w_gate, m_w_up, m_w_down, m_post_ffn_norm, v_meta_tokens, v_pre_mix_norm, v_w_in, v_conv_a_w, v_conv_b_w, v_conv_b_bias, v_ln_b_gain, v_ln_b_bias, v_w_out, v_post_mix_norm, v_pre_ffn_norm, v_w_gate, v_w_up, v_w_down, v_post_ffn_norm):
    _, seq, d = x.shape
    ka, ca_loc = conv_a_w.shape[1:]
    kb, cb_loc = conv_b_w.shape[1:]
    wa_w = ca_loc * N_DEV
    assert cb_loc == ca_loc and wa_w % LANE == 0 and w_in.shape[2] * N_DEV == 5 * wa_w and 2 * wa_w == d
    pad = (-(N_META + seq)) % ROW_ALIGN
    x0 = pad + N_META
    t_rows = x0 + seq
    assert t_rows % (N_ROW_BLOCKS * BF16_ROWS) == 0 and t_rows % CONV_CHUNK == 0 and d % LANE == 0
    tm = t_rows // N_ROW_BLOCKS
    tm2 = t_rows // 2
    me = _dev_index(*_mesh_pos())

    def as_rows(w_in_like, w_out_like, w_gate_like, w_up_like, w_down_like):
        return (w_in_like[0].T, w_out_like[0], w_gate_like[0].T, w_up_like[0].T, w_down_like[0])

    w_loc = as_rows(w_in, w_out, w_gate, w_up, w_down)
    rows = [w.shape[0] for w in w_loc]
    assert all(r % ADD_CHUNK == 0 for r in rows)
    P_IN, P_OUT, P_GATE, P_UP, P_DOWN = range(N_BIG)

    sm = jnp.zeros((SM_ROWS, LANE), F32)
    sm = sm.at[0:N_META, :].set(meta_tokens)
    sm = sm.at[16:16 + ka, 0:ca_loc].set(conv_a_w[0])
    sm = sm.at[24:24 + kb, 0:cb_loc].set(conv_b_w[0])
    wl, wfull, sm_all, h0, tgt = _gather_first(w_loc, sm, [(P_IN, 0, rows[P_IN])], x[0], loss_target[0], t_rows, x0)
    wa =jnp.transpose(sm_all[:, 16:16 + ka, 0:ca_loc], (1, 0, 2)).reshape(ka, wa_w)
    wb = jnp.transpose(sm_all[:, 24:24 + kb, 0:cb_loc], (1, 0, 2)).reshape(kb, wa_w)

    later = (P_OUT, P_GATE, P_UP, P_DOWN)
    sems, wl, started, _ = _gather_start(wl, wfull, later, rows, START_BARRIER_IDS[0])
    for p, arr in zip(later, started):
        wfull[p] = arr

    def arrived(p, after, name):
        nonlocal wl
        wl, wfull[p] = _gather_wait(wl, wfull[p], sems[later.index(p)], after, rows[p], name)
        return _forward_comm(wfull[p], rows[p])

    (xn1, hin), _ = _in_proj(h0, pre_mix_norm, wfull[P_IN], tm, None)
    (ya, z), (wfull[P_OUT],) = _mix_conv_fwd(hin, wa, wb, conv_b_bias, wa_w, arrived(P_OUT, hin, "gather_wait_out"))
    (y, mix, h1, xn2), (wfull[P_GATE],) = _out_proj(ya, z, ln_b_gain, ln_b_bias, wfull[P_OUT], h0, post_mix_norm, pre_ffn_norm, tm2,
                                                    arrived(P_GATE, z, "gather_wait_gate"))
    arrived(P_UP, xn2, "gather_wait_up")
    wfull[P_UP] = _forward_now(wfull[P_UP], rows[P_UP], "forward_up")
    (ga, gu, s), _ = _gate_up(xn2, wfull[P_GATE], wfull[P_UP], tm, None)
    arrived(P_DOWN, s, "gather_wait_down")
    wfull[P_DOWN] = _forward_now(wfull[P_DOWN], rows[P_DOWN], "forward_down")
    dh2, dff, dg4, lossv = _down_loss(s, wfull[P_DOWN], h1, tgt, post_ffn_norm, tm, x0)

    gwd = _wgrad(s, dff, "wgrad_down")
    dau, (pair_d,) = _bwd_down(dff, wfull[P_DOWN], ga, gu, tm, _pair_comm(gwd, rows[P_DOWN]))
    (flight_d,), token = _chip_start([_pair_sum(gwd, pair_d, rows[P_DOWN], "pair_sum_down")], "chip_start_down", START_BARRIER_IDS[1])
    gw_gu = _wgrad(dau, xn2, "wgrad_gate_up", [token])
    up0 = N_DEV * rows[P_GATE]
    (dh1, dg3), (pair_g, pair_u) = _bwd_ffn_in(dau, wfull[P_GATE], wfull[P_UP], h1, dh2, pre_ffn_norm, tm,
                                               _merge_comms([_pair_comm(gw_gu, rows[P_GATE]), _pair_comm(gw_gu, rows[P_UP], up0)]))
    (flight_g, flight_u), token = _chip_start([_pair_sum(gw_gu, pair_g, rows[P_GATE], "pair_sum_gate"),
                                               _pair_sum(gw_gu, pair_u, rows[P_UP], "pair_sum_up", row0=up0)], "chip_start_gate_up",
                                              START_BARRIER_IDS[2])
    dmix, dya, dz, dg2, dlg, dlb, dbb = _bwd_out_proj(dh1, mix, wfull[P_OUT], post_mix_norm, z, ln_b_gain, ln_b_bias, tm, [token])
    gwo = _wgrad(y, dmix, "wgrad_out")
    (dh5, dwa, dwb), (pair_o,) = _mix_conv_bwd(hin, dya, dz, wa, wb, wa_w, _pair_comm(gwo, rows[P_OUT]))
    (flight_o,), token = _chip_start([_pair_sum(gwo, pair_o, rows[P_OUT], "pair_sum_out")], "chip_start_out", START_BARRIER_IDS[3])
    gwi = _wgrad(dh5, xn1, "wgrad_in", [token])
    (dh0, dg1), (pair_i,) = _bwd_in_proj(dh5, wfull[P_IN], h0, dh1, pre_mix_norm, tm, _pair_comm(gwi, rows[P_IN]))
    grad_x = dh0[x0:][None]
    dmeta = dh0[x0 - N_META:x0]
    small_sums = _pair_small((dmeta, dg1, dg2, dg3, dg4, dbb, dlg, dlb, lossv, dwa, dwb), d)
    (flight_s,), token = _chip_start([small_sums], "chip_start_small", START_BARRIER_IDS[4])
    (flight_i,), token = _chip_start([_pair_sum(gwi, pair_i, rows[P_IN], "pair_sum_in", [token])], "chip_start_in",
                                     START_BARRIER_IDS[5])

    def landed(flight, after, tag):
        sems_p, sums, land = flight
        return _chip_wait(sums, land, sems_p, after, "chip_wait_" + tag)

    m_loc = as_rows(m_w_in, m_w_out, m_w_gate, m_w_up, m_w_down)
    v_loc = as_rows(v_w_in, v_w_out, v_w_gate, v_w_up, v_w_down)
    full_grads = {P_IN: gwi, P_OUT: gwo, P_GATE: gw_gu, P_UP: gw_gu, P_DOWN: gwd}
    pairs = {P_IN: pair_i, P_OUT: pair_o, P_GATE: pair_g, P_UP: pair_u, P_DOWN: pair_d}
    flights = {P_IN: flight_i, P_OUT: flight_o, P_GATE: flight_g, P_UP: flight_u, P_DOWN: flight_d}
    names = {P_IN: "w_in", P_OUT: "w_out", P_GATE: "w_gate", P_UP: "w_up", P_DOWN: "w_down"}
    bigs = {}

    def adam_big(p, after):
        part = landed(flights[p], after, names[p])
        res = _adam_big(full_grads[p], pairs[p], part, w_loc[p], m_loc[p], v_loc[p], "adam_" + names[p], up0 if p == P_UP else 0)
        bigs[names[p]] = [(o.T if p in (P_IN, P_GATE, P_UP) else o)[None] for o in res]
        return res[1]

    for p in (P_DOWN, P_GATE, P_UP, P_OUT):
        token = adam_big(p, token)
    (ptot, g_meta, g_pre_mix, g_post_mix, g_pre_ffn, g_post_ffn, g_conv_bias, g_ln_gain, g_ln_bias, loss11) = _total_small(
        flight_s[1], landed(flight_s, token, "small"))
    loss = jnp.reshape(loss11, ())
    g_small = [g_meta, g_pre_mix, lax.dynamic_slice(ptot, (24, me * ca_loc), (ka, ca_loc))[None],
               lax.dynamic_slice(ptot, (32, me * cb_loc), (kb, cb_loc))[None],
               g_conv_bias, g_ln_gain, g_ln_bias, g_post_mix, g_pre_ffn, g_post_ffn]
    w_small = [meta_tokens, pre_mix_norm, conv_a_w, conv_b_w, conv_b_bias, ln_b_gain, ln_b_bias, post_mix_norm,
               pre_ffn_norm, post_ffn_norm]
    m_small = [m_meta_tokens, m_pre_mix_norm, m_conv_a_w, m_conv_b_w, m_conv_b_bias, m_ln_b_gain, m_ln_b_bias,
               m_post_mix_norm, m_pre_ffn_norm, m_post_ffn_norm]
    v_small = [v_meta_tokens, v_pre_mix_norm, v_conv_a_w, v_conv_b_w, v_conv_b_bias, v_ln_b_gain, v_ln_b_bias,
               v_post_mix_norm, v_pre_ffn_norm, v_post_ffn_norm]
    small = _adam_small(g_small, w_small, m_small, v_small)
    n_small = len(w_small)
    d_small, nm_small, nv_small = small[:n_small], small[n_small:2 * n_small], small[2 * n_small:]

    adam_big(P_IN, small[0])

    def ordered(pick_small, pick_big):
        sm_it = iter(range(n_small))
        out = []
        for name in ("s", "s", "w_in", "s", "s", "s", "s", "s", "w_out", "s", "s", "w_gate", "w_up", "w_down", "s"):
            out.append(pick_small(next(sm_it)) if name == "s" else pick_big(name))
        return out

    grads = ordered(lambda i: g_small[i], lambda n: bigs[n][0])
    deltas = ordered(lambda i: d_small[i], lambda n: bigs[n][1])
    new_m = ordered(lambda i: nm_small[i], lambda n: bigs[n][2])
    new_v = ordered(lambda i: nv_small[i], lambda n: bigs[n][3])
    return (loss, grad_x, *grads, *deltas, *new_m, *new_v)
```

```python
import jax
import jax.numpy as jnp
from jax import lax
from jax.experimental import pallas as pl
from jax.experimental.pallas import tpu as pltpu

F32 = jnp.float32
BF16 = jnp.bfloat16
MESH = pl.DeviceIdType.MESH

N_META = 16
N_DEV = 8
RMS_EPS = 1e-6
LN_EPS = 1e-5
ADAM_LR = 0.001
ADAM_B1 = 0.9
ADAM_B2 = 0.999
ADAM_EPS = 1e-08
ADAM_WD = 0.01
ADAM_STEP = 10

LANE = 128
SUBLANE = 8
BF16_ROWS = 16
ROW_ALIGN = 128
N_ROW_BLOCKS = 4
CONV_HALO = 32
CONV_CHUNK = 64
WGRAD_ROWS = 32
N_CHUNK = 512
WGRAD_TILE_MAX = 1408
ADD_CHUNK = 32
ADAM_COL_BLOCKS = 2
COPY_PIECES = 4
V7X_VMEM_BYTES = 64 * 1024 * 1024
VMEM_LIMIT = V7X_VMEM_BYTES - 6 * 1024 * 1024
SMALL_ROWS = 64
SM_ROWS = 56
N_BIG = 5

ANY = pl.BlockSpec(memory_space=pl.ANY)
VMEM = pl.BlockSpec(memory_space=pltpu.VMEM)


def _cparams(n_grid_axes=0):
    sem = ("arbitrary",) * n_grid_axes if n_grid_axes else None
    return pltpu.CompilerParams(dimension_semantics=sem, vmem_limit_bytes=VMEM_LIMIT)


def _mesh_pos():
    return lax.axis_index("x"), lax.axis_index("y"), lax.axis_index("c")


def _dev_index(px, py, pc):
    return 4 * px + 2 * py + pc


def _other_chips(x, y):
    return [(1 - x, y), (x, 1 - y), (1 - x, 1 - y)]


def _full(shape):
    return pl.BlockSpec(shape, lambda *_: (0,) * len(shape))


def _resident(shape):
    return pl.BlockSpec(shape, lambda *_: (0,) * len(shape), pipeline_mode=pl.Buffered(1))


def _dot_nt(a, w):
    return lax.dot_general(a, w, (((1,), (1,)), ((), ())), preferred_element_type=F32)


def _dot_nn(a, w):
    return jnp.dot(a, w, preferred_element_type=F32)


def _chunks(n, c):
    out, o = [], 0
    while o < n:
        out.append((o, min(c, n - o)))
        o += c
    return out


def _rstd(h):
    return lax.rsqrt(jnp.mean(h * h, axis=-1, keepdims=True) + RMS_EPS)


def _rms_bwd(dyh, yh, r):
    return r * (dyh - yh * jnp.mean(dyh * yh, axis=-1, keepdims=True))


def _silu_grad(a, sig):
    return sig * (1.0 + a * (1.0 - sig))


def _acc_rows(ref, val, first):
    s = jnp.sum(val, axis=0, keepdims=True)

    @pl.when(first)
    def _():
        ref[...] = s

    @pl.when(jnp.logical_not(first))
    def _():
        ref[...] += s


def _row_loop(t_rows, chunk, fn, carry=None):
    def step(i, c):
        return fn(pl.multiple_of(i * chunk, chunk), c)

    return lax.fori_loop(0, t_rows // chunk, step, carry)


def _remote(src, dst, send_sem, recv_sem, to):
    return pltpu.make_async_remote_copy(src_ref=src, dst_ref=dst, send_sem=send_sem, recv_sem=recv_sem,
                                        device_id=to, device_id_type=MESH)


class _Comm:
    def __init__(self, inputs, out_shapes, aliases, scratch, start, finish):
        self.inputs, self.out_shapes, self.aliases, self.scratch = list(inputs), list(out_shapes), dict(aliases), list(scratch)
        self.start, self.finish = start, finish


def _merge_comms(comms):
    inputs, out_shapes, aliases, scratch, spans = [], [], {}, [], []
    for cm in comms:
        spans.append((len(inputs), len(out_shapes), len(scratch), cm))
        aliases.update({len(inputs) + k: len(out_shapes) + v for k, v in cm.aliases.items()})
        inputs += cm.inputs
        out_shapes += cm.out_shapes
        scratch += cm.scratch

    def run(which):
        def fn(ins, outs, scr):
            for i0, o0, s0, cm in spans:
                getattr(cm, which)(ins[i0:i0 + len(cm.inputs)], outs[o0:o0 + len(cm.out_shapes)], scr[s0:s0 + len(cm.scratch)])
        return fn

    return _Comm(inputs, out_shapes, aliases, scratch, run("start"), run("finish"))


def _host_call(body, *, grid, in_specs, out_specs, out_shape, args, name, scratch_shapes=(), comm=None, after=()):
    talks = comm is not None
    if comm is None:
        comm = _Comm([], [], {}, [], lambda *_: None, lambda *_: None)
    n_in, n_out, n_scr = len(args), len(out_shape), len(scratch_shapes)
    c_in, c_out = len(comm.inputs), len(comm.out_shapes)
    n_after = len(after)

    def open_comm(c_ins, c_outs, c_scr):
        if talks:
            _pair_handshake()
        comm.start(c_ins, c_outs, c_scr)

    def hosted(*refs):
        ins, c_ins = refs[:n_in], refs[n_in:n_in + c_in]
        o0 = n_in + c_in + n_after
        outs, c_outs = refs[o0:o0 + n_out], refs[o0 + n_out:o0 + n_out + c_out]
        s0 = o0 + n_out + c_out
        scr, c_scr = refs[s0:s0 + n_scr], refs[s0 + n_scr:]
        if not grid:
            open_comm(c_ins, c_outs, c_scr)
            body(*ins, *outs, *scr)
            comm.finish(c_ins, c_outs, c_scr)
            return
        first = last = None
        for a, n in enumerate(grid):
            f, l = pl.program_id(a) == 0, pl.program_id(a) == n - 1
            first = f if first is None else jnp.logical_and(first, f)
            last = l if last is None else jnp.logical_and(last, l)

        @pl.when(first)
        def _():
            open_comm(c_ins, c_outs, c_scr)

        body(*ins, *outs, *scr)

        @pl.when(last)
        def _():
            comm.finish(c_ins, c_outs, c_scr)

    sem = ("arbitrary",) * len(grid) if grid else None
    params = pltpu.CompilerParams(dimension_semantics=sem, vmem_limit_bytes=VMEM_LIMIT,
                                  collective_id=PAIR_BARRIER_ID if talks else None)
    res = pl.pallas_call(
        hosted, grid=grid, in_specs=list(in_specs) + [ANY] * (c_in + n_after), out_specs=list(out_specs) + [ANY] * c_out,
        out_shape=list(out_shape) + comm.out_shapes, scratch_shapes=list(scratch_shapes) + comm.scratch,
        input_output_aliases={n_in + k: n_out + v for k, v in comm.aliases.items()},
        name=name, compiler_params=params)(*args, *comm.inputs, *after)
    return list(res[:n_out]), list(res[n_out:])


PAIR_BARRIER_ID = 0
START_BARRIER_IDS = (1, 2, 3, 4, 5, 6)
ALL_PEERS_BARRIER_ID = 7


def _chips_handshake():
    x, y, c = _mesh_pos()
    barrier = pltpu.get_barrier_semaphore()
    for chip in _other_chips(x, y):
        pl.semaphore_signal(barrier, inc=1, device_id=(*chip, c), device_id_type=MESH)
    pl.semaphore_wait(barrier, 3)


def _pair_handshake():
    x, y, c = _mesh_pos()
    barrier = pltpu.get_barrier_semaphore()
    pl.semaphore_signal(barrier, inc=1, device_id=(x, y, 1 - c), device_id_type=MESH)
    pl.semaphore_wait(barrier, 1)


GATHER_SEMS = 10


class _Gather:
    def __init__(self, jobs, rows, lo, src_ref, dests, send_sems, recv_sems):
        x, y, c = _mesh_pos()
        me, sib = (x, y, c), (x, y, 1 - c)
        nx, ny, dg = (1 - x, y, c), (x, 1 - y, c), (1 - x, 1 - y, c)
        self.relayed, self.direct, self.relay, self.to_sib, self.sib_fwd = [], [], [], [], []
        for n, (p, r0, nr) in enumerate(jobs):
            assert nr % (2 * BF16_ROWS) == 0
            half = nr // 2

            def rows_of(dev, h, p=p, r0=r0, nr=nr, half=half):
                off, cnt = (r0, nr) if h is None else (r0 + h * half, half)
                return dests[p].at[pl.ds(pl.multiple_of(_dev_index(*dev) * rows[p] + off, BF16_ROWS), cnt), :]

            def mine(h, p=p, r0=r0, nr=nr, half=half):
                off, cnt = (r0, nr) if h is None else (r0 + h * half, half)
                return src_ref.at[pl.ds(lo[p] + off, cnt), :]

            sem = lambda k, n=n: (send_sems.at[GATHER_SEMS * n + k], recv_sems.at[GATHER_SEMS * n + k])
            self.relayed.append([_remote(mine(0), rows_of(me, 0), *sem(0), nx), _remote(mine(1), rows_of(me, 1), *sem(3), ny)])
            self.direct.append([_remote(mine(1), rows_of(me, 1), *sem(1), nx), _remote(mine(0), rows_of(me, 0), *sem(2), ny)])
            self.relay.append([_remote(rows_of(nx, 0), rows_of(nx, 0), *sem(4), ny), _remote(rows_of(ny, 1), rows_of(ny, 1), *sem(5), nx)])
            self.to_sib.append(_remote(mine(None), rows_of(me, None), *sem(6), sib))
            self.sib_fwd.append([_remote(rows_of(dev, None), rows_of(dev, None), *sem(7 + i), sib) for i, dev in enumerate((nx, ny, dg))])

    def start(self):
        for group in (self.relayed, self.direct):
            for cps in group:
                for cp in cps:
                    cp.start()
        for cp in self.to_sib:
            cp.start()

    def mid(self):
        for first, relay in zip(self.relayed, self.relay):
            for arrived, onward in zip(first, relay):
                arrived.wait_recv()
                onward.start()

    def finish(self):
        for direct, relay, fwd in zip(self.direct, self.relay, self.sib_fwd):
            for k in range(2):
                direct[k].wait_recv()
                fwd[k].start()
            for cp in relay:
                cp.wait_recv()
            fwd[2].start()
        for n in range(len(self.to_sib)):
            self.to_sib[n].wait_recv()
            for cp in self.sib_fwd[n]:
                cp.wait_recv()
            for cp in self.relayed[n] + self.direct[n] + self.relay[n] + [self.to_sib[n]] + self.sib_fwd[n]:
                cp.wait_send()


HBM = pl.BlockSpec(memory_space=pltpu.HBM)
SEM = pl.BlockSpec(memory_space=pltpu.SEMAPHORE)
FLOWS = pltpu.SideEffectType.DATAFLOW_SIDE_EFFECTING


def _in_hbm(a):
    return pltpu.with_memory_space_constraint(a, pltpu.HBM)


def _gather_start(wl, dests, ps, rows, barrier_id):
    lo = [sum(rows[:p]) for p in range(N_BIG)]
    n = len(ps)

    def body(*refs):
        wl_ref, dest_refs = refs[0], refs[1:1 + n]
        sends, recvs = refs[1 + n:1 + 2 * n], refs[1 + 2 * n:1 + 3 * n]
        token = refs[-1]
        _chips_handshake()
        x, y, c = _mesh_pos()
        jme = _dev_index(x, y, c)
        for i, p in enumerate(ps):
            mine = dest_refs[i].at[pl.ds(pl.multiple_of(jme * rows[p], BF16_ROWS), rows[p]), :]
            for chip in _other_chips(x, y):
                _remote(wl_ref.at[pl.ds(lo[p], rows[p]), :], mine, sends[i], recvs[i], (*chip, c)).start()
        token[...] = jnp.zeros_like(token)

    thru = [pltpu.HBM(wl.shape, wl.dtype)] + [pltpu.HBM(dests[p].shape, BF16) for p in ps]
    res = pl.pallas_call(
        body, name="gather_start",
        out_shape=tuple([pltpu.SemaphoreType.DMA(())] * (2 * n) + thru + [jax.ShapeDtypeStruct((SUBLANE, LANE), F32)]),
        in_specs=[HBM] * (1 + n), out_specs=tuple([SEM] * (2 * n) + [HBM] * (1 + n) + [VMEM]),
        input_output_aliases={i: 2 * n + i for i in range(1 + n)},
        compiler_params=pltpu.CompilerParams(has_side_effects=FLOWS, collective_id=barrier_id))(
            _in_hbm(wl), *[_in_hbm(dests[p]) for p in ps])
    sems = [(res[i], res[n + i]) for i in range(n)]
    return sems, res[2 * n], list(res[2 * n + 1:3 * n + 1]), res[-1]


def _gather_wait(wl, dest, sems, after, r, name):
    def body(wl_ref, dest_ref, send_sem, recv_sem, after_ref, wl_out, dest_out):
        x, y, c = _mesh_pos()
        three = dest_ref.at[pl.ds(0, 3 * r), :]
        cp = _remote(three, three, send_sem, recv_sem, (x, y, 1 - c))
        cp.wait_send()
        cp.wait_recv()

    res = pl.pallas_call(
        body, name=name, out_shape=(pltpu.HBM(wl.shape, wl.dtype), pltpu.HBM(dest.shape, dest.dtype)),
        in_specs=[HBM, HBM, SEM, SEM, ANY], out_specs=(HBM, HBM), input_output_aliases={0: 0, 1: 1},
        compiler_params=pltpu.CompilerParams(has_side_effects=FLOWS))(wl, dest, sems[0], sems[1], after)
    return res[0], res[1]


def _forward_comm(dest, r):
    def descs(ins, outs, scr):
        x, y, c = _mesh_pos()
        cps = []
        for k, chip in enumerate(_other_chips(x, y)):
            blk = outs[0].at[pl.ds(pl.multiple_of(_dev_index(*chip, c) * r, BF16_ROWS), r), :]
            cps.append(_remote(blk, blk, scr[0].at[k], scr[1].at[k], (x, y, 1 - c)))
        return cps

    def start(ins, outs, scr):
        for cp in descs(ins, outs, scr):
            cp.start()

    def finish(ins, outs, scr):
        cps = descs(ins, outs, scr)
        for cp in cps:
            cp.wait_recv()
        for cp in cps:
            cp.wait_send()

    return _Comm([dest], [jax.ShapeDtypeStruct(dest.shape, dest.dtype)], {0: 0},
                 [pltpu.SemaphoreType.DMA((3,)), pltpu.SemaphoreType.DMA((3,))], start, finish)


def _forward_now(dest, r, name):
    _, (dest,) = _host_call(lambda: None, grid=(), in_specs=[], out_specs=[], out_shape=[], args=(), name=name,
                            comm=_forward_comm(dest, r))
    return dest


def _pair_comm(g, r, row0=0):
    d = g.shape[1]

    def descs(ins, outs, scr):
        x, y, c = _mesh_pos()
        chips = [(x, y)] + _other_chips(x, y)
        return [_remote(ins[0].at[pl.ds(pl.multiple_of(row0 + _dev_index(*chip, 1 - c) * r, BF16_ROWS), r), :], outs[0].at[k],
                        scr[0].at[k], scr[1].at[k], (x, y, 1 - c)) for k, chip in enumerate(chips)]

    def start(ins, outs, scr):
        for cp in descs(ins, outs, scr):
            cp.start()

    def finish(ins, outs, scr):
        cps = descs(ins, outs, scr)
        for cp in cps:
            cp.wait_recv()
        for cp in cps:
            cp.wait_send()

    comm = _Comm([g], [jax.ShapeDtypeStruct((4, r, d), BF16)], {},
                 [pltpu.SemaphoreType.DMA((4,)), pltpu.SemaphoreType.DMA((4,))], start, finish)
    return comm


def _pair_sum(g, pair, r, name, after=(), row0=0):
    d = g.shape[1]

    def body(g_ref, p_ref, *rest):
        o_ref, gbuf, pbuf, sems = rest[len(after):]
        x, y, c = _mesh_pos()
        loads = [pltpu.make_async_copy(p_ref.at[pl.ds(1, 3)], pbuf, sems.at[3])]
        for k, chip in enumerate(_other_chips(x, y)):
            j = _dev_index(*chip, c)
            loads.append(pltpu.make_async_copy(g_ref.at[pl.ds(pl.multiple_of(row0 + j * r, BF16_ROWS), r), :], gbuf.at[k], sems.at[k]))
        for cp in loads:
            cp.start()
        for cp in loads:
            cp.wait()
        for k in range(3):
            o_ref[k] = (gbuf[k].astype(F32) + pbuf[k].astype(F32)).astype(BF16)

    return pl.pallas_call(
        body, out_shape=jax.ShapeDtypeStruct((3, r, d), BF16), in_specs=[ANY] * (2 + len(after)), out_specs=VMEM,
        scratch_shapes=[pltpu.VMEM((3, r, d), BF16), pltpu.VMEM((3, r, d), BF16), pltpu.SemaphoreType.DMA((4,))],
        name=name, compiler_params=_cparams())(g, pair, *after)


def _chip_start(sums, name, barrier_id):
    n = len(sums)

    def body(*refs):
        srcs, lands = refs[:n], refs[n:2 * n]
        sends, recvs = refs[2 * n:3 * n], refs[3 * n:4 * n]
        _chips_handshake()
        x, y, c = _mesh_pos()
        for i in range(n):
            for k, chip in enumerate(_other_chips(x, y)):
                _remote(srcs[i].at[k], lands[i].at[k], sends[i], recvs[i], (*chip, c)).start()
        refs[-1][...] = jnp.zeros_like(refs[-1])

    zones = [pltpu.HBM(s.shape, s.dtype) for s in sums]
    res = pl.pallas_call(
        body, name=name,
        out_shape=tuple([pltpu.SemaphoreType.DMA(())] * (2 * n) + zones + zones + [jax.ShapeDtypeStruct((SUBLANE, LANE), F32)]),
        in_specs=[HBM] * (2 * n), out_specs=tuple([SEM] * (2 * n) + [HBM] * (2 * n) + [VMEM]),
        input_output_aliases={i: 2 * n + i for i in range(2 * n)},
        compiler_params=pltpu.CompilerParams(has_side_effects=FLOWS, collective_id=barrier_id))(
            *[_in_hbm(s) for s in sums], *[_in_hbm(lax.empty(s.shape, s.dtype)) for s in sums])
    flights = [((res[i], res[n + i]), res[2 * n + i], res[3 * n + i]) for i in range(n)]
    return flights, res[-1]


def _chip_wait(sums, land, sems, after, name):
    def body(sums_ref, land_ref, send_sem, recv_sem, after_ref, sums_out, land_out):
        x, y, c = _mesh_pos()
        cp = _remote(sums_ref, land_ref, send_sem, recv_sem, (x, y, 1 - c))
        cp.wait_send()
        cp.wait_recv()

    res = pl.pallas_call(
        body, name=name, out_shape=(pltpu.HBM(sums.shape, sums.dtype), pltpu.HBM(land.shape, land.dtype)),
        in_specs=[HBM, HBM, SEM, SEM, ANY], out_specs=(HBM, HBM), input_output_aliases={0: 0, 1: 1},
        compiler_params=pltpu.CompilerParams(has_side_effects=FLOWS))(sums, land, sems[0], sems[1], after)
    return res[1]


class _CopyThrough:
    def __init__(self, src_ref, dst_ref, dst_row0, n_rows, buf, sem_in, sem_out):
        rc = n_rows // COPY_PIECES
        piece = lambda ref, o: ref.at[pl.ds(o, rc), :]
        self.loads = [pltpu.make_async_copy(piece(src_ref, k * rc), piece(buf, k * rc), sem_in) for k in range(COPY_PIECES)]
        self.stores = [pltpu.make_async_copy(piece(buf, k * rc), piece(dst_ref, dst_row0 + k * rc), sem_out) for k in range(COPY_PIECES)]
        self.all_in = pltpu.make_async_copy(src_ref, buf, sem_in)
        self.all_out = pltpu.make_async_copy(buf, dst_ref.at[pl.ds(dst_row0, n_rows), :], sem_out)

    def load(self):
        for cp in self.loads:
            cp.start()

    def store(self):
        self.all_in.wait()
        for cp in self.stores:
            cp.start()

    def done(self):
        self.all_out.wait()


def _gather_first(shards, sm, jobs, x2, tgt2, t_rows, x0):
    d = shards[0].shape[1]
    rows = [w.shape[0] for w in shards]
    lo = [sum(rows[:p]) for p in range(N_BIG)]
    n_sems = GATHER_SEMS * len(jobs)
    seq = x2.shape[0]
    assert x0 == ROW_ALIGN and seq % ROW_ALIGN == 0 and d == N_DEV * LANE

    def body(s0, s1, s2, s3, s4, sm_ref, x_ref, tgt_ref, wl_ref, o0, o1, o2, o3, o4, sa_ref, h0_ref, tp_ref,
             wl_v, x_v, tgt_v, heads_v, sa_v, send_sems, recv_sems, ssend, srecv, local_sems, sems_in, sems_out):
        dests = (o0, o1, o2, o3, o4)
        x, y, c = _mesh_pos()
        me = (x, y, c)
        jme = _dev_index(*me)
        peers = [(x, y, 1 - c)] + [(*chip, pc) for pc in (c, 1 - c) for chip in _other_chips(x, y)]
        barrier = pltpu.get_barrier_semaphore()
        for to in peers:
            pl.semaphore_signal(barrier, inc=1, device_id=to, device_id_type=MESH)
        pl.semaphore_wait(barrier, len(peers))
        padded = [_CopyThrough(x_ref, h0_ref, x0, seq, x_v, sems_in.at[0], sems_out.at[0]),
                  _CopyThrough(tgt_ref, tp_ref, x0, seq, tgt_v, sems_in.at[1], sems_out.at[1])]
        for cp in padded:
            cp.load()
        shard_refs = (s0, s1, s2, s3, s4)
        first = sorted({j[0] for j in jobs})
        for p in first + [p for p in range(N_BIG) if p not in first]:
            wl_v[pl.ds(lo[p], rows[p]), :] = shard_refs[p][...].astype(BF16)
            if p == first[-1]:
                gather = _Gather(jobs, rows, lo, wl_v, dict(enumerate(dests)), send_sems, recv_sems)
                gather.start()
        smalls = [_remote(sm_ref, sa_ref.at[jme], ssend.at[k], srecv.at[k], to) for k, to in enumerate(peers)]
        for cp in smalls:
            cp.start()
        mine = [pltpu.make_async_copy(wl_v.at[pl.ds(lo[p], rows[p]), :],
                                      dests[p].at[pl.ds(pl.multiple_of(jme * rows[p], BF16_ROWS), rows[p]), :], local_sems.at[p])
                for p in range(N_BIG)]
        mine.append(pltpu.make_async_copy(wl_v, wl_ref, local_sems.at[N_BIG]))
        mine.append(pltpu.make_async_copy(sm_ref, sa_ref.at[jme], local_sems.at[N_BIG + 1]))
        for cp in mine:
            cp.start()
        later = [p for p in range(N_BIG) if p not in {j[0] for j in jobs}]
        own = [_remote(wl_v.at[pl.ds(lo[p], rows[p]), :], dests[p].at[pl.ds(pl.multiple_of(jme * rows[p], BF16_ROWS), rows[p]), :],
                       ssend.at[7 + i], srecv.at[7 + i], (x, y, 1 - c)) for i, p in enumerate(later)]
        for cp in own:
            cp.start()
        gather.mid()
        for cp in padded:
            cp.store()
        for cp in smalls + own:
            cp.wait_recv()
        mine[-1].wait()
        to_v = pltpu.make_async_copy(sa_ref, sa_v, local_sems.at[N_BIG + 1])
        to_v.start()
        to_v.wait()
        head, zeros = heads_v.at[0], heads_v.at[1]
        head[...] = jnp.zeros_like(head)
        zeros[...] = jnp.zeros_like(zeros)
        for j in range(N_DEV):
            head[pl.ds(x0 - N_META, N_META), pl.ds(j * LANE, LANE)] = sa_v[j, pl.ds(0, N_META), :]
        heads = [pltpu.make_async_copy(head, h0_ref.at[pl.ds(0, x0), :], local_sems.at[N_BIG + 1]),
                 pltpu.make_async_copy(zeros, tp_ref.at[pl.ds(0, x0), :], local_sems.at[N_BIG + 2])]
        for cp in heads:
            cp.start()
        gather.finish()
        for cp in smalls + own:
            cp.wait_send()
        for cp in mine[:-1] + heads:
            cp.wait()
        for cp in padded:
            cp.done()

    out_shape = [jax.ShapeDtypeStruct((sum(rows), d), BF16)]
    out_shape += [jax.ShapeDtypeStruct((N_DEV * r, d), BF16) for r in rows]
    out_shape.append(jax.ShapeDtypeStruct((N_DEV,) + sm.shape, F32))
    out_shape += [jax.ShapeDtypeStruct((t_rows, d), F32)] * 2
    res = pl.pallas_call(
        body, out_shape=out_shape, in_specs=[VMEM] * 6 + [ANY] * 2, out_specs=[ANY] * 9,
        scratch_shapes=[pltpu.VMEM((sum(rows), d), BF16), pltpu.VMEM((seq, d), F32), pltpu.VMEM((seq, d), F32),
                        pltpu.VMEM((2, ROW_ALIGN, d), F32), pltpu.VMEM((N_DEV,) + sm.shape, F32),
                        pltpu.SemaphoreType.DMA((n_sems,)), pltpu.SemaphoreType.DMA((n_sems,)),
                        pltpu.SemaphoreType.DMA((7 + N_BIG,)), pltpu.SemaphoreType.DMA((7 + N_BIG,)),
                        pltpu.SemaphoreType.DMA((N_BIG + 3,)), pltpu.SemaphoreType.DMA((2,)), pltpu.SemaphoreType.DMA((2,))],
        name="gather_first",
        compiler_params=pltpu.CompilerParams(vmem_limit_bytes=VMEM_LIMIT, collective_id=ALL_PEERS_BARRIER_ID))(*shards, sm, x2, tgt2)
    return res[0], list(res[1:1 + N_BIG]), res[1 + N_BIG], res[2 + N_BIG], res[3 + N_BIG]


def _in_proj(h0, g1, win_t, tm, comm):
    t_rows, d = h0.shape
    e = win_t.shape[0]

    def body(h_ref, g_ref, w_ref, xn_ref, hin_ref):
        h = h_ref[...]
        xn = ((h * _rstd(h)) * g_ref[...]).astype(BF16)
        xn_ref[...] = xn
        for o, n in _chunks(e, N_CHUNK):
            hin_ref[:, pl.ds(o, n)] = _dot_nt(xn, w_ref[pl.ds(o, n), :])

    return _host_call(
        body, grid=(t_rows // tm,),
        in_specs=[pl.BlockSpec((tm, d), lambda i: (i, 0)), _full((1, d)), _resident((e, d))],
        out_specs=[pl.BlockSpec((tm, d), lambda i: (i, 0)), pl.BlockSpec((tm, e), lambda i: (i, 0))],
        out_shape=[jax.ShapeDtypeStruct((t_rows, d), BF16), jax.ShapeDtypeStruct((t_rows, e), F32)],
        args=(h0, g1, win_t), name="in_proj", comm=comm)


def _tap_slot(off):
    return off % SUBLANE, (off // SUBLANE) * SUBLANE


def _fill_shifted(sh_ref, base_ref, residues, n_rows):
    for r in residues:
        if r:
            sh_ref[r] = base_ref[pl.ds(r, n_rows), :]


def _shifted_rows(pair, r, start, n):
    base_ref, sh_ref = pair
    return base_ref[pl.ds(start, n), :] if r == 0 else sh_ref[r, pl.ds(start, n), :]


def _mix_conv_fwd(hin, wa, wb, bb, wa_w, comm):
    t_rows = hin.shape[0]
    nt = wa_w // LANE
    ka, kb = wa.shape[0], wb.shape[0]
    nr = CONV_HALO + t_rows

    def body(bg_ref, cg_ref, ha_ref, val_ref, gt_ref, wa_ref, wb_ref, bb_ref, ya_ref, z_ref, base, sh):
        base[pl.ds(0, CONV_HALO), :] = jnp.zeros((CONV_HALO, LANE), F32)
        base[pl.ds(nr, SUBLANE), :] = jnp.zeros((SUBLANE, LANE), F32)

        def conv(w_ref, k_taps, b, n):
            acc = None
            for k in range(k_taps):
                r, q = _tap_slot(CONV_HALO - (k_taps - 1) + k)
                term = w_ref[pl.ds(k, 1), :] * _shifted_rows((base, sh), r, b + q, n)
                acc = term if acc is None else acc + term
            return acc

        def fill_a(b, c):
            base[pl.ds(CONV_HALO + b, CONV_CHUNK), :] = cg_ref[pl.ds(b, CONV_CHUNK), :] * ha_ref[pl.ds(b, CONV_CHUNK), :]
            return c

        _row_loop(t_rows, CONV_CHUNK, fill_a)
        _fill_shifted(sh, base, sorted({_tap_slot(CONV_HALO - (ka - 1) + k)[0] for k in range(ka)}), nr)

        def out_a(b, c):
            ya_ref[pl.ds(b, CONV_CHUNK), :] = (bg_ref[pl.ds(b, CONV_CHUNK), :] * conv(wa_ref, ka, b, CONV_CHUNK)).astype(BF16)
            return c

        _row_loop(t_rows, CONV_CHUNK, out_a)

        def fill_b(b, c):
            base[pl.ds(CONV_HALO + b, CONV_CHUNK), :] = (val_ref[pl.ds(b, CONV_CHUNK), :]
                                                          * jax.nn.sigmoid(gt_ref[pl.ds(b, CONV_CHUNK), :]))
            return c

        _row_loop(t_rows, CONV_CHUNK, fill_b)
        _fill_shifted(sh, base, range(SUBLANE), nr)

        def out_b(b, c):
            z_ref[pl.ds(b, CONV_CHUNK), :] = conv(wb_ref, kb, b, CONV_CHUNK) + bb_ref[...]
            return c

        _row_loop(t_rows, CONV_CHUNK, out_b)

    def col(g):
        return pl.BlockSpec((t_rows, LANE), lambda i, g=g: (0, g * nt + i))

    tile = lambda rows: pl.BlockSpec((rows, LANE), lambda i: (0, i))
    return _host_call(
        body, grid=(nt,),
        in_specs=[col(0), col(1), col(2), col(3), col(4), tile(ka), tile(kb), tile(1)],
        out_specs=[tile(t_rows), tile(t_rows)],
        out_shape=[jax.ShapeDtypeStruct((t_rows, wa_w), BF16), jax.ShapeDtypeStruct((t_rows, wa_w), F32)],
        scratch_shapes=[pltpu.VMEM((nr + SUBLANE, LANE), F32), pltpu.VMEM((SUBLANE, nr, LANE), F32)],
        args=(hin, hin, hin, hin, hin, wa, wb, bb), name="mix_conv_fwd", comm=comm)


def _ln_parts(z, lg, lb):
    mu = jnp.mean(z, axis=-1, keepdims=True)
    zc = z - mu
    rstd = lax.rsqrt(jnp.mean(zc * zc, axis=-1, keepdims=True) + LN_EPS)
    zh = zc * rstd
    return zh, rstd, zh * lg + lb


def _out_proj(ya, z, lg, lb, w_out, h0, g2, g3, tm, comm):
    t_rows, d = h0.shape
    w = z.shape[1]

    def body(ya_ref, z_ref, lg_ref, lb_ref, w_ref, h0_ref, g2_ref, g3_ref, y_ref, mix_ref, h1_ref, xn2_ref):
        _, _, ln = _ln_parts(z_ref[...], lg_ref[...], lb_ref[...])
        y_ref[:, pl.ds(0, w)] = ya_ref[...]
        y_ref[:, pl.ds(w, w)] = (ln * jax.nn.sigmoid(ln)).astype(BF16)
        mix = _dot_nn(y_ref[...], w_ref[...])
        mix_ref[...] = mix
        h1 = h0_ref[...] + (mix * _rstd(mix)) * g2_ref[...]
        h1_ref[...] = h1
        xn2_ref[...] = ((h1 * _rstd(h1)) * g3_ref[...]).astype(BF16)

    blk = pl.BlockSpec((tm, d), lambda i: (i, 0))
    half = pl.BlockSpec((tm, w), lambda i: (i, 0))
    return _host_call(
        body, grid=(t_rows // tm,),
        in_specs=[half, half, _full((1, w)), _full((1, w)), _resident(w_out.shape), blk, _full((1, d)), _full((1, d))],
        out_specs=[blk, blk, blk, blk],
        out_shape=[jax.ShapeDtypeStruct((t_rows, d), BF16), jax.ShapeDtypeStruct((t_rows, d), F32),
                   jax.ShapeDtypeStruct((t_rows, d), F32), jax.ShapeDtypeStruct((t_rows, d), BF16)],
        args=(ya, z, lg, lb, w_out, h0, g2, g3), name="out_proj", comm=comm)


def _gate_up(xn2, wg_t, wu_t, tm, comm):
    t_rows, d = xn2.shape
    f = wg_t.shape[0]

    def body(x_ref, wg_ref, wu_ref, ga_ref, gu_ref, s_ref):
        xn = x_ref[...]
        for o, n in _chunks(f, N_CHUNK):
            a = _dot_nt(xn, wg_ref[pl.ds(o, n), :])
            u = _dot_nt(xn, wu_ref[pl.ds(o, n), :])
            sig = jax.nn.sigmoid(a)
            silu = a * sig
            s = silu * u
            gu_ref[:, pl.ds(o, n)] = silu.astype(BF16)
            ga_ref[:, pl.ds(o, n)] = ((u - s) * sig + s).astype(BF16)
            s_ref[:, pl.ds(o, n)] = s.astype(BF16)

    blk = pl.BlockSpec((tm, f), lambda i: (i, 0))
    return _host_call(
        body, grid=(t_rows // tm,),
        in_specs=[pl.BlockSpec((tm, d), lambda i: (i, 0)), _resident((f, d)), _resident((f, d))],
        out_specs=[blk, blk, blk], out_shape=[jax.ShapeDtypeStruct((t_rows, f), BF16)] * 3,
        args=(xn2, wg_t, wu_t), name="gate_up", comm=comm)


def _down_loss(s, wd, h1, tgt, g4, tm, x0):
    t_rows, d = h1.shape
    f = wd.shape[0]

    def body(s_ref, w_ref, h1_ref, tgt_ref, g4_ref, dh2_ref, dff_ref, dg4_ref, loss_ref):
        i = pl.program_id(0)
        ff = _dot_nn(s_ref[...], w_ref[...])
        r4 = _rstd(ff)
        fh = ff * r4
        g4 = g4_ref[...]
        h2 = h1_ref[...] + fh * g4
        row = i * tm + lax.broadcasted_iota(jnp.int32, (tm, 1), 0)
        diff = jnp.where(row >= x0, h2 - tgt_ref[...], 0.0)
        dh2 = diff / d
        dh2_ref[...] = dh2
        dff_ref[...] = _rms_bwd(dh2 * g4, fh, r4).astype(BF16)
        _acc_rows(dg4_ref, dh2 * fh, i == 0)
        _acc_rows(loss_ref, diff * diff, i == 0)

    blk = pl.BlockSpec((tm, d), lambda i: (i, 0))
    res, _ = _host_call(
        body, grid=(t_rows // tm,),
        in_specs=[pl.BlockSpec((tm, f), lambda i: (i, 0)), _resident((f, d)), blk, blk, _full((1, d))],
        out_specs=[blk, blk, _full((1, d)), _full((1, d))],
        out_shape=[jax.ShapeDtypeStruct((t_rows, d), F32), jax.ShapeDtypeStruct((t_rows, d), BF16),
                   jax.ShapeDtypeStruct((1, d), F32), jax.ShapeDtypeStruct((1, d), F32)],
        args=(s, wd, h1, tgt, g4), name="down_loss")
    return res


def _bwd_down(dff, wd, ga, gu, tm, comm):
    t_rows, d = dff.shape
    f = wd.shape[0]

    def body(dff_ref, w_ref, ga_ref, gu_ref, dau_ref):
        dff_v = dff_ref[...]
        for o, n in _chunks(f, N_CHUNK):
            ds = _dot_nt(dff_v, w_ref[pl.ds(o, n), :]).astype(BF16)
            dau_ref[0, :, pl.ds(o, n)] = ds * ga_ref[:, pl.ds(o, n)]
            dau_ref[1, :, pl.ds(o, n)] = ds * gu_ref[:, pl.ds(o, n)]

    blk = pl.BlockSpec((tm, f), lambda i: (i, 0))
    (dau,), extra = _host_call(
        body, grid=(t_rows // tm,),
        in_specs=[pl.BlockSpec((tm, d), lambda i: (i, 0)), _resident((f, d)), blk, blk],
        out_specs=[pl.BlockSpec((2, tm, f), lambda i: (0, i, 0))], out_shape=[jax.ShapeDtypeStruct((2, t_rows, f), BF16)],
        args=(dff, wd, ga, gu), name="bwd_down", comm=comm)
    return dau, extra


def _wgrad(a, b, name, after=()):
    d = b.shape[1]
    t_rows = b.shape[0]
    stacked = a.ndim == 3
    n = a.shape[-1]
    groups = a.shape[0] if stacked else 1
    steps = 1 if stacked else 2
    tile = max(t for t in range(LANE, min(n // steps, WGRAD_TILE_MAX) + 1, LANE) if n % t == 0)
    tiles = n // tile

    def body(a_ref, b_ref, o_ref):
        o_ref[...] = lax.dot_general(a_ref[...], b_ref[...], (((0,), (0,)), ((), ())),
                                     preferred_element_type=F32).astype(BF16)

    if stacked:
        a_spec = pl.BlockSpec((None, t_rows, tile), lambda g, i: (g, 0, i))
    else:
        a_spec = pl.BlockSpec((t_rows, tile), lambda g, i: (0, i))
    res, _ = _host_call(
        body, grid=(groups, tiles), in_specs=[a_spec, _resident((t_rows, d))],
        out_specs=[pl.BlockSpec((tile, d), lambda g, i: (g * tiles + i, 0))],
        out_shape=[jax.ShapeDtypeStruct((groups * n, d), BF16)], args=(a, b), name=name, after=after)
    return res[0]


def _bwd_ffn_in(dau, wg_t, wu_t, h1, dh2, g3, tm, comm):
    t_rows, d = h1.shape
    f = wg_t.shape[0]

    def body(dau_ref, wg_ref, wu_ref, h1_ref, dh2_ref, g3_ref, dh1_ref, dg3_ref):
        dxn2 = _dot_nn(dau_ref[0], wg_ref[...]) + _dot_nn(dau_ref[1], wu_ref[...])
        h1 = h1_ref[...]
        r3 = _rstd(h1)
        h1h = h1 * r3
        _acc_rows(dg3_ref, dxn2 * h1h, pl.program_id(0) == 0)
        dh1_ref[...] = dh2_ref[...] + _rms_bwd(dxn2 * g3_ref[...], h1h, r3)

    blk = pl.BlockSpec((tm, d), lambda i: (i, 0))
    return _host_call(
        body, grid=(t_rows // tm,),
        in_specs=[pl.BlockSpec((2, tm, f), lambda i: (0, i, 0)), _resident((f, d)), _resident((f, d)), blk, blk, _full((1, d))],
        out_specs=[blk, _full((1, d))],
        out_shape=[jax.ShapeDtypeStruct((t_rows, d), F32), jax.ShapeDtypeStruct((1, d), F32)],
        args=(dau, wg_t, wu_t, h1, dh2, g3), name="bwd_ffn_in", comm=comm)


def _bwd_out_proj(dh1, mix, w_out, g2, z, lg, lb, tm, after):
    t_rows, d = dh1.shape
    w = z.shape[1]

    def body(dh1_ref, mix_ref, w_ref, g2_ref, z_ref, lg_ref, lb_ref, dmix_ref, dya_ref, dz_ref, dg2_ref, dlg_ref, dlb_ref, dbb_ref):
        first = pl.program_id(0) == 0
        mix = mix_ref[...]
        r2 = _rstd(mix)
        mh = mix * r2
        dh1 = dh1_ref[...]
        _acc_rows(dg2_ref, dh1 * mh, first)
        dmix = _rms_bwd(dh1 * g2_ref[...], mh, r2).astype(BF16)
        dmix_ref[...] = dmix
        dy = _dot_nt(dmix, w_ref[...])
        dya_ref[...] = dy[:, :w]
        lg = lg_ref[...]
        zh, rstd, ln = _ln_parts(z_ref[...], lg, lb_ref[...])
        dln = dy[:, w:] * _silu_grad(ln, jax.nn.sigmoid(ln))
        _acc_rows(dlg_ref, dln * zh, first)
        _acc_rows(dlb_ref, dln, first)
        dzh = dln * lg
        dz = rstd * (dzh - jnp.mean(dzh, axis=-1, keepdims=True) - zh * jnp.mean(dzh * zh, axis=-1, keepdims=True))
        dz_ref[...] = dz
        _acc_rows(dbb_ref, dz, first)

    blk = pl.BlockSpec((tm, d), lambda i: (i, 0))
    half = pl.BlockSpec((tm, w), lambda i: (i, 0))
    vec = _full((1, w))
    res, _ = _host_call(
        body, grid=(t_rows // tm,), in_specs=[blk, blk, _resident(w_out.shape), _full((1, d)), half, vec, vec],
        out_specs=[blk, half, half, _full((1, d)), vec, vec, vec],
        out_shape=[jax.ShapeDtypeStruct((t_rows, d), BF16), jax.ShapeDtypeStruct((t_rows, w), F32),
                   jax.ShapeDtypeStruct((t_rows, w), F32), jax.ShapeDtypeStruct((1, d), F32)]
        + [jax.ShapeDtypeStruct((1, w), F32)] * 3,
        args=(dh1, mix, w_out, g2, z, lg, lb), name="bwd_out_proj", after=after)
    return res


def _mix_conv_bwd(hin, dy, dz, wa, wb, wa_w, comm):
    t_rows = hin.shape[0]
    nt = wa_w // LANE
    ka, kb = wa.shape[0], wb.shape[0]
    nr = CONV_HALO + t_rows
    kb_rows = -(-kb // SUBLANE) * SUBLANE

    def body(bg_ref, cg_ref, ha_ref, val_ref, gt_ref, dya_ref, dz_ref, wa_ref, wb_ref,
             dh_ref, dwa_ref, dwb_ref, base, sh, based, shd, tmp, wbc):
        zeros = lambda n: jnp.zeros((n, LANE), F32)
        base[pl.ds(0, CONV_HALO), :] = zeros(CONV_HALO)
        base[pl.ds(nr, SUBLANE), :] = zeros(SUBLANE)
        based[pl.ds(t_rows, CONV_HALO + SUBLANE), :] = zeros(CONV_HALO + SUBLANE)

        def fwd_slot(k_taps, k):
            return _tap_slot(CONV_HALO - (k_taps - 1) + k)

        def bwd_slot(k_taps, k):
            return _tap_slot(k_taps - 1 - k)

        def conv(w_ref, k_taps, src, slot, b, n):
            acc = None
            for k in range(k_taps):
                r, q = slot(k_taps, k)
                term = w_ref[pl.ds(k, 1), :] * _shifted_rows(src, r, b + q, n)
                acc = term if acc is None else acc + term
            return acc

        def by_residue(k_taps, slot):
            groups = {}
            for k in range(k_taps):
                r, q = slot(k_taps, k)
                groups.setdefault(r, []).append((k, q // SUBLANE))
            return groups

        def wgrad_loop(w_ref, k_taps):
            n_sub = WGRAD_ROWS // SUBLANE
            for k in range(k_taps):
                wbc[k] = jnp.broadcast_to(w_ref[pl.ds(k, 1), :], (SUBLANE, LANE))
            fwd, bwd = by_residue(k_taps, fwd_slot), by_residue(k_taps, bwd_slot)

            def window(src, r, taps, b):
                span = n_sub + max(qi for _, qi in taps)
                return [_shifted_rows(src, r, b + SUBLANE * i, SUBLANE) for i in range(span)]

            def step(b, accs):
                accs = list(accs)
                dv = [based[pl.ds(b + SUBLANE * j, SUBLANE), :] for j in range(n_sub)]
                for r, taps in fwd.items():
                    win = window((base, sh), r, taps, b)
                    for k, qi in taps:
                        t = dv[0] * win[qi]
                        for j in range(1, n_sub):
                            t = t + dv[j] * win[qi + j]
                        accs[k] = accs[k] + t
                outs = [None] * n_sub
                for r, taps in bwd.items():
                    win = window((based, shd), r, taps, b)
                    for k, qi in taps:
                        wk = wbc[k]
                        for j in range(n_sub):
                            term = wk * win[qi + j]
                            outs[j] = term if outs[j] is None else outs[j] + term
                for j in range(n_sub):
                    tmp[pl.ds(b + SUBLANE * j, SUBLANE), :] = outs[j]
                return tuple(accs)

            return _row_loop(t_rows, WGRAD_ROWS, step, tuple(zeros(SUBLANE) for _ in range(k_taps)))

        def store_taps(ref, accs, rows):
            for k, acc in enumerate(accs):
                ref[pl.ds(k, 1), :] = jnp.sum(acc, axis=0, keepdims=True)
            if rows > len(accs):
                ref[pl.ds(len(accs), rows - len(accs)), :] = zeros(rows - len(accs))

        def fill_a(b, c):
            sl = pl.ds(b, CONV_CHUNK)
            base[pl.ds(CONV_HALO + b, CONV_CHUNK), :] = cg_ref[sl, :] * ha_ref[sl, :]
            based[sl, :] = dya_ref[sl, :] * bg_ref[sl, :]
            return c

        _row_loop(t_rows, CONV_CHUNK, fill_a)
        _fill_shifted(sh, base, sorted({fwd_slot(ka, k)[0] for k in range(ka)}), nr)
        _fill_shifted(shd, based, sorted({bwd_slot(ka, k)[0] for k in range(ka)}), nr)

        def d_bgate(b, c):
            sl = pl.ds(b, CONV_CHUNK)
            dh_ref[0, sl, :] = (dya_ref[sl, :] * conv(wa_ref, ka, (base, sh), fwd_slot, b, CONV_CHUNK)).astype(BF16)
            return c

        _row_loop(t_rows, CONV_CHUNK, d_bgate)
        store_taps(dwa_ref, wgrad_loop(wa_ref, ka), SUBLANE)

        def d_ch(b, c):
            sl = pl.ds(b, CONV_CHUNK)
            dua = tmp[sl, :]
            dh_ref[1, sl, :] = (dua * ha_ref[sl, :]).astype(BF16)
            dh_ref[2, sl, :] = (dua * cg_ref[sl, :]).astype(BF16)
            return c

        _row_loop(t_rows, CONV_CHUNK, d_ch)

        def fill_b(b, c):
            sl = pl.ds(b, CONV_CHUNK)
            base[pl.ds(CONV_HALO + b, CONV_CHUNK), :] = val_ref[sl, :] * jax.nn.sigmoid(gt_ref[sl, :])
            based[sl, :] = dz_ref[sl, :]
            return c

        _row_loop(t_rows, CONV_CHUNK, fill_b)
        _fill_shifted(sh, base, range(SUBLANE), nr)
        _fill_shifted(shd, based, range(SUBLANE), nr)
        store_taps(dwb_ref, wgrad_loop(wb_ref, kb), kb_rows)

        def d_glu(b, c):
            sl = pl.ds(b, CONV_CHUNK)
            dgg = tmp[sl, :]
            sig = jax.nn.sigmoid(gt_ref[sl, :])
            dh_ref[3, sl, :] = (dgg * sig).astype(BF16)
            dh_ref[4, sl, :] = (dgg * val_ref[sl, :] * (sig * (1.0 - sig))).astype(BF16)
            return c

        _row_loop(t_rows, CONV_CHUNK, d_glu)

    def col(g):
        return pl.BlockSpec((t_rows, LANE), lambda i, g=g: (0, g * nt + i))

    tile = lambda rows: pl.BlockSpec((rows, LANE), lambda i: (0, i))
    return _host_call(
        body, grid=(nt,),
        in_specs=[col(0), col(1), col(2), col(3), col(4), tile(t_rows), tile(t_rows), tile(ka), tile(kb)],
        out_specs=[pl.BlockSpec((5, t_rows, LANE), lambda i: (0, 0, i)), tile(SUBLANE), tile(kb_rows)],
        out_shape=[jax.ShapeDtypeStruct((5, t_rows, wa_w), BF16), jax.ShapeDtypeStruct((SUBLANE, wa_w), F32),
                   jax.ShapeDtypeStruct((kb_rows, wa_w), F32)],
        scratch_shapes=[pltpu.VMEM((nr + SUBLANE, LANE), F32), pltpu.VMEM((SUBLANE, nr, LANE), F32),
                        pltpu.VMEM((nr + SUBLANE, LANE), F32), pltpu.VMEM((SUBLANE, nr, LANE), F32),
                        pltpu.VMEM((t_rows, LANE), F32), pltpu.VMEM((kb_rows, SUBLANE, LANE), F32)],
        args=(hin, hin, hin, hin, hin, dy, dz, wa, wb), name="mix_conv_bwd", comm=comm)


def _bwd_in_proj(dh5, win_t, h0, dh1, g1, tm, x0, comm):
    t_rows, d = h0.shape
    groups, _, w = dh5.shape
    assert x0 <= tm

    def body(dh_ref, w_ref, h0_ref, dh1_ref, g1_ref, dh0_ref, dg1_ref, dmeta_ref):
        dxn1 = None
        for g in range(groups):
            part = _dot_nn(dh_ref[g], w_ref[pl.ds(g * w, w), :])
            dxn1 = part if dxn1 is None else dxn1 + part
        h0 = h0_ref[...]
        r1 = _rstd(h0)
        h0h = h0 * r1
        _acc_rows(dg1_ref, dxn1 * h0h, pl.program_id(0) == 0)
        dh0_ref[...] = dh1_ref[...] + _rms_bwd(dxn1 * g1_ref[...], h0h, r1)

        @pl.when(pl.program_id(0) == 0)
        def _():
            dmeta_ref[...] = dh0_ref[pl.ds(x0 - N_META, N_META), :]

    blk = pl.BlockSpec((tm, d), lambda i: (i, 0))
    return _host_call(
        body, grid=(t_rows // tm,),
        in_specs=[pl.BlockSpec((groups, tm, w), lambda i: (0, i, 0)), _resident(win_t.shape), blk, blk, _full((1, d))],
        out_specs=[blk, _full((1, d)), _full((N_META, d))],
        out_shape=[jax.ShapeDtypeStruct((t_rows, d), F32), jax.ShapeDtypeStruct((1, d), F32), jax.ShapeDtypeStruct((N_META, d), F32)],
        args=(dh5, win_t, h0, dh1, g1), name="bwd_in_proj", comm=comm)


def _pair_small(smalls, d):
    (dmeta, dg1, dg2, dg3, dg4, dbb, dlg, dlb, lossv, dwa, dwb) = smalls
    half = d // 2
    kb_rows = dwb.shape[0]

    def body(dmeta_ref, dg1_ref, dg2_ref, dg3_ref, dg4_ref, dbb_ref, dlg_ref, dlb_ref, loss_ref, dwa_ref, dwb_ref,
             sums_ref, pbuf, psib, ps_send, ps_recv):
        x, y, c = _mesh_pos()
        _pair_handshake()
        pbuf[...] = jnp.zeros_like(pbuf)
        pbuf[pl.ds(0, N_META), :] = dmeta_ref[...]
        for row, ref in ((16, dg1_ref), (17, dg2_ref), (18, dg3_ref), (19, dg4_ref)):
            pbuf[pl.ds(row, 1), :] = ref[...]
        pbuf[pl.ds(20, 1), pl.ds(0, half)] = dbb_ref[...]
        pbuf[pl.ds(20, 1), pl.ds(half, half)] = dlg_ref[...]
        pbuf[pl.ds(21, 1), pl.ds(0, half)] = dlb_ref[...]
        lv = loss_ref[...]
        pbuf[pl.ds(21, 1), pl.ds(half, half)] = lv[:, :half] + lv[:, half:]
        pbuf[pl.ds(24, SUBLANE), pl.ds(0, half)] = dwa_ref[...]
        pbuf[pl.ds(32, kb_rows), pl.ds(0, half)] = dwb_ref[...]
        to_sib = _remote(pbuf, psib, ps_send.at[0], ps_recv.at[0], (x, y, 1 - c))
        to_sib.start()
        to_sib.wait_recv()
        s = pbuf[...] + psib[...]
        for k in range(3):
            sums_ref[k] = s
        to_sib.wait_send()

    return pl.pallas_call(
        body, out_shape=jax.ShapeDtypeStruct((3, SMALL_ROWS, d), F32), in_specs=[VMEM] * 11, out_specs=VMEM,
        scratch_shapes=[pltpu.VMEM((SMALL_ROWS, d), F32), pltpu.VMEM((SMALL_ROWS, d), F32),
                        pltpu.SemaphoreType.DMA((1,)), pltpu.SemaphoreType.DMA((1,))],
        name="pair_small", compiler_params=pltpu.CompilerParams(vmem_limit_bytes=VMEM_LIMIT, collective_id=PAIR_BARRIER_ID))(*smalls)


def _total_small(own, others):
    _, r, d = own.shape
    half, cols = d // 2, d // N_DEV

    def body(own_ref, others_ref, tot_ref, meta_ref, g1_ref, g2_ref, g3_ref, g4_ref, dbb_ref, dlg_ref, dlb_ref, loss_ref, chip_p):
        x, y, c = _mesh_pos()
        chip_p[2 * x + y] = own_ref[0]
        for k, (cx, cy) in enumerate(_other_chips(x, y)):
            chip_p[2 * cx + cy] = others_ref[k]
        tot_ref[...] = ((chip_p[0] + chip_p[1]) + chip_p[2]) + chip_p[3]
        meta_ref[...] = tot_ref[pl.ds(0, N_META), pl.ds(pl.multiple_of(_dev_index(x, y, c) * cols, LANE), cols)]
        for row, ref in ((16, g1_ref), (17, g2_ref), (18, g3_ref), (19, g4_ref)):
            ref[...] = tot_ref[pl.ds(row, 1), :]
        dbb_ref[...] = tot_ref[pl.ds(20, 1), pl.ds(0, half)]
        dlg_ref[...] = tot_ref[pl.ds(20, 1), pl.ds(half, half)]
        dlb_ref[...] = tot_ref[pl.ds(21, 1), pl.ds(0, half)]
        loss_ref[...] = (0.5 / d) * jnp.sum(tot_ref[pl.ds(21, 1), pl.ds(half, half)], axis=-1, keepdims=True)

    row = lambda n: jax.ShapeDtypeStruct((1, n), F32)
    return pl.pallas_call(
        body, out_shape=[jax.ShapeDtypeStruct((r, d), F32), jax.ShapeDtypeStruct((N_META, cols), F32), row(d), row(d), row(d), row(d),
                         row(half), row(half), row(half), row(1)],
        scratch_shapes=[pltpu.VMEM((4, r, d), F32)], name="total_small", compiler_params=_cparams())(own, others)


def _adamw(w, g, m, v):
    m = ADAM_B1 * m + (1.0 - ADAM_B1) * g
    v = ADAM_B2 * v + (1.0 - ADAM_B2) * jnp.square(g)
    m_hat = m / (1.0 - ADAM_B1 ** ADAM_STEP)
    v_hat = v / (1.0 - ADAM_B2 ** ADAM_STEP)
    delta = -ADAM_LR * (m_hat / (jnp.sqrt(v_hat) + ADAM_EPS) + ADAM_WD * w)
    return delta, m, v


def _adam_big(g, pair, part, w, m, v, name, row0=0):
    r, d = w.shape
    cols = d // ADAM_COL_BLOCKS
    assert row0 % r == 0

    def body(me_ref, g_ref, pair_ref, part_ref, w_ref, m_ref, v_ref, go_ref, d_ref, mo_ref, vo_ref):
        g = g_ref[...].astype(F32) + pair_ref[...].astype(F32)
        for k in range(3):
            g = g + part_ref[k].astype(F32)
        go_ref[...] = g
        d_ref[...], mo_ref[...], vo_ref[...] = _adamw(w_ref[...], g, m_ref[...], v_ref[...])

    blk = pl.BlockSpec((r, cols), lambda i, me_ref: (0, i))
    grid_spec = pltpu.PrefetchScalarGridSpec(
        num_scalar_prefetch=1, grid=(ADAM_COL_BLOCKS,),
        in_specs=[pl.BlockSpec((r, cols), lambda i, me_ref: (me_ref[0], i)),
                  pl.BlockSpec((None, r, cols), lambda i, me_ref: (0, 0, i)),
                  pl.BlockSpec((3, r, cols), lambda i, me_ref: (0, 0, i)), blk, blk, blk],
        out_specs=[blk, blk, blk, blk])
    me = jnp.reshape(_dev_index(*_mesh_pos()) + row0 // r, (1,)).astype(jnp.int32)
    return pl.pallas_call(body, out_shape=[jax.ShapeDtypeStruct((r, d), F32)] * 4, grid_spec=grid_spec, name=name,
                          compiler_params=_cparams(1))(me, g, pair, part, w, m, v)


def _adam_small(gs, ws, ms, vs):
    n = len(gs)

    def body(*refs):
        ins, outs = refs[:4 * n], refs[4 * n:]
        for i in range(n):
            g = ins[i][...]
            delta, m, v = _adamw(ins[n + i][...], g, ins[2 * n + i][...], ins[3 * n + i][...])
            outs[i][...] = delta
            outs[n + i][...] = m
            outs[2 * n + i][...] = v

    shapes = [jax.ShapeDtypeStruct(w.shape, F32) for w in ws]
    return pl.pallas_call(body, out_shape=shapes * 3, name="adam_small", compiler_params=_cparams())(*gs, *ws, *ms, *vs)


def kernel(x, meta_tokens, pre_mix_norm, w_in, conv_a_w, conv_b_w, conv_b_bias, ln_b_gain, ln_b_bias, w_out, post_mix_norm, pre_ffn_norm, w_gate, w_up, w_down, post_ffn_norm, loss_target, m_meta_tokens, m_pre_mix_norm, m_w_in, m_conv_a_w, m_conv_b_w, m_conv_b_bias, m_ln_b_gain, m_ln_b_bias, m_w_out, m_post_mix_norm, m_pre_ffn_norm, m_w_gate, m_w_up, m_w_down, m_post_ffn_norm, v_meta_tokens, v_pre_mix_norm, v_w_in, v_conv_a_w, v_conv_b_w, v_conv_b_bias, v_ln_b_gain, v_ln_b_bias, v_w_out, v_post_mix_norm, v_pre_ffn_norm, v_w_gate, v_w_up, v_w_down, v_post_ffn_norm):
    _, seq, d = x.shape
    ka, ca_loc = conv_a_w.shape[1:]
    kb, cb_loc = conv_b_w.shape[1:]
    wa_w = ca_loc * N_DEV
    assert cb_loc == ca_loc and wa_w % LANE == 0 and w_in.shape[2] * N_DEV == 5 * wa_w and 2 * wa_w == d
    pad = (-(N_META + seq)) % ROW_ALIGN
    x0 = pad + N_META
    t_rows = x0 + seq
    assert t_rows % (N_ROW_BLOCKS * BF16_ROWS) == 0 and t_rows % CONV_CHUNK == 0 and d % LANE == 0
    tm = t_rows // N_ROW_BLOCKS
    tm2 = t_rows // 2
    me = _dev_index(*_mesh_pos())

    def as_rows(w_in_like, w_out_like, w_gate_like, w_up_like, w_down_like):
        return (w_in_like[0].T, w_out_like[0], w_gate_like[0].T, w_up_like[0].T, w_down_like[0])

    w_loc = as_rows(w_in, w_out, w_gate, w_up, w_down)
    rows = [w.shape[0] for w in w_loc]
    assert all(r % ADD_CHUNK == 0 for r in rows)
    P_IN, P_OUT, P_GATE, P_UP, P_DOWN = range(N_BIG)

    sm = jnp.zeros((SM_ROWS, LANE), F32)
    sm = sm.at[0:N_META, :].set(meta_tokens)
    sm = sm.at[16:16 + ka, 0:ca_loc].set(conv_a_w[0])
    sm = sm.at[24:24 + kb, 0:cb_loc].set(conv_b_w[0])
    wl, wfull, sm_all, h0, tgt = _gather_first(w_loc, sm, [(P_IN, 0, rows[P_IN])], x[0], loss_target[0], t_rows, x0)
    wa =jnp.transpose(sm_all[:, 16:16 + ka, 0:ca_loc], (1, 0, 2)).reshape(ka, wa_w)
    wb = jnp.transpose(sm_all[:, 24:24 + kb, 0:cb_loc], (1, 0, 2)).reshape(kb, wa_w)

    later = (P_OUT, P_GATE, P_UP, P_DOWN)
    sems, wl, started, _ = _gather_start(wl, wfull, later, rows, START_BARRIER_IDS[0])
    for p, arr in zip(later, started):
        wfull[p] = arr

    def arrived(p, after, name):
        nonlocal wl
        wl, wfull[p] = _gather_wait(wl, wfull[p], sems[later.index(p)], after, rows[p], name)
        return _forward_comm(wfull[p], rows[p])

    (xn1, hin), _ = _in_proj(h0, pre_mix_norm, wfull[P_IN], tm, None)
    (ya, z), (wfull[P_OUT],) = _mix_conv_fwd(hin, wa, wb, conv_b_bias, wa_w, arrived(P_OUT, hin, "gather_wait_out"))
    (y, mix, h1, xn2), (wfull[P_GATE],) = _out_proj(ya, z, ln_b_gain, ln_b_bias, wfull[P_OUT], h0, post_mix_norm, pre_ffn_norm, tm2,
                                                    arrived(P_GATE, z, "gather_wait_gate"))
    arrived(P_UP, xn2, "gather_wait_up")
    wfull[P_UP] = _forward_now(wfull[P_UP], rows[P_UP], "forward_up")
    (ga, gu, s), _ = _gate_up(xn2, wfull[P_GATE], wfull[P_UP], tm, None)
    arrived(P_DOWN, s, "gather_wait_down")
    wfull[P_DOWN] = _forward_now(wfull[P_DOWN], rows[P_DOWN], "forward_down")
    dh2, dff, dg4, lossv = _down_loss(s, wfull[P_DOWN], h1, tgt, post_ffn_norm, tm, x0)

    gwd = _wgrad(s, dff, "wgrad_down")
    dau, (pair_d,) = _bwd_down(dff, wfull[P_DOWN], ga, gu, tm, _pair_comm(gwd, rows[P_DOWN]))
    (flight_d,), token = _chip_start([_pair_sum(gwd, pair_d, rows[P_DOWN], "pair_sum_down")], "chip_start_down", START_BARRIER_IDS[1])
    gw_gu = _wgrad(dau, xn2, "wgrad_gate_up", [token])
    up0 = N_DEV * rows[P_GATE]
    (dh1, dg3), (pair_g, pair_u) = _bwd_ffn_in(dau, wfull[P_GATE], wfull[P_UP], h1, dh2, pre_ffn_norm, tm,
                                               _merge_comms([_pair_comm(gw_gu, rows[P_GATE]), _pair_comm(gw_gu, rows[P_UP], up0)]))
    (flight_g, flight_u), token = _chip_start([_pair_sum(gw_gu, pair_g, rows[P_GATE], "pair_sum_gate"),
                                               _pair_sum(gw_gu, pair_u, rows[P_UP], "pair_sum_up", row0=up0)], "chip_start_gate_up",
                                              START_BARRIER_IDS[2])
    dmix, dya, dz, dg2, dlg, dlb, dbb = _bwd_out_proj(dh1, mix, wfull[P_OUT], post_mix_norm, z, ln_b_gain, ln_b_bias, tm, [token])
    gwo = _wgrad(y, dmix, "wgrad_out")
    (dh5, dwa, dwb), (pair_o,) = _mix_conv_bwd(hin, dya, dz, wa, wb, wa_w, _pair_comm(gwo, rows[P_OUT]))
    (flight_o,), token = _chip_start([_pair_sum(gwo, pair_o, rows[P_OUT], "pair_sum_out")], "chip_start_out", START_BARRIER_IDS[3])
    gwi = _wgrad(dh5, xn1, "wgrad_in", [token])
    (dh0, dg1, dmeta), (pair_i,) = _bwd_in_proj(dh5, wfull[P_IN], h0, dh1, pre_mix_norm, tm, x0, _pair_comm(gwi, rows[P_IN]))
    grad_x = dh0[x0:][None]
    small_sums = _pair_small((dmeta, dg1, dg2, dg3, dg4, dbb, dlg, dlb, lossv, dwa, dwb), d)
    (flight_s,), token = _chip_start([small_sums], "chip_start_small", START_BARRIER_IDS[4])
    (flight_i,), token = _chip_start([_pair_sum(gwi, pair_i, rows[P_IN], "pair_sum_in", [token])], "chip_start_in",
                                     START_BARRIER_IDS[5])

    def landed(flight, after, tag):
        sems_p, sums, land = flight
        return _chip_wait(sums, land, sems_p, after, "chip_wait_" + tag)

    m_loc = as_rows(m_w_in, m_w_out, m_w_gate, m_w_up, m_w_down)
    v_loc = as_rows(v_w_in, v_w_out, v_w_gate, v_w_up, v_w_down)
    full_grads = {P_IN: gwi, P_OUT: gwo, P_GATE: gw_gu, P_UP: gw_gu, P_DOWN: gwd}
    pairs = {P_IN: pair_i, P_OUT: pair_o, P_GATE: pair_g, P_UP: pair_u, P_DOWN: pair_d}
    flights = {P_IN: flight_i, P_OUT: flight_o, P_GATE: flight_g, P_UP: flight_u, P_DOWN: flight_d}
    names = {P_IN: "w_in", P_OUT: "w_out", P_GATE: "w_gate", P_UP: "w_up", P_DOWN: "w_down"}
    bigs = {}

    def adam_big(p, after):
        part = landed(flights[p], after, names[p])
        res = _adam_big(full_grads[p], pairs[p], part, w_loc[p], m_loc[p], v_loc[p], "adam_" + names[p], up0 if p == P_UP else 0)
        bigs[names[p]] = [(o.T if p in (P_IN, P_GATE, P_UP) else o)[None] for o in res]
        return res[1]

    for p in (P_DOWN, P_GATE, P_UP, P_OUT):
        token = adam_big(p, token)
    (ptot, g_meta, g_pre_mix, g_post_mix, g_pre_ffn, g_post_ffn, g_conv_bias, g_ln_gain, g_ln_bias, loss11) = _total_small(
        flight_s[1], landed(flight_s, token, "small"))
    loss = jnp.reshape(loss11, ())
    taps_major = lambda a: jnp.transpose(a, (1, 0, 2))
    CONV = (2, 3)
    g_small = [g_meta, g_pre_mix, lax.dynamic_slice(ptot, (24, me * ca_loc), (ka, ca_loc))[:, None, :],
               lax.dynamic_slice(ptot, (32, me * cb_loc), (kb, cb_loc))[:, None, :],
               g_conv_bias, g_ln_gain, g_ln_bias, g_post_mix, g_pre_ffn, g_post_ffn]
    w_small = [meta_tokens, pre_mix_norm, taps_major(conv_a_w), taps_major(conv_b_w), conv_b_bias, ln_b_gain, ln_b_bias,
               post_mix_norm, pre_ffn_norm, post_ffn_norm]
    m_small = [m_meta_tokens, m_pre_mix_norm, taps_major(m_conv_a_w), taps_major(m_conv_b_w), m_conv_b_bias, m_ln_b_gain,
               m_ln_b_bias, m_post_mix_norm, m_pre_ffn_norm, m_post_ffn_norm]
    v_small = [v_meta_tokens, v_pre_mix_norm, taps_major(v_conv_a_w), taps_major(v_conv_b_w), v_conv_b_bias, v_ln_b_gain,
               v_ln_b_bias, v_post_mix_norm, v_pre_ffn_norm, v_post_ffn_norm]
    small = list(_adam_small(g_small, w_small, m_small, v_small))
    n_small = len(w_small)
    for i in CONV:
        g_small[i] = taps_major(g_small[i])
        for k in range(3):
            small[k * n_small + i] = taps_major(small[k * n_small + i])
    d_small, nm_small, nv_small = small[:n_small], small[n_small:2 * n_small], small[2 * n_small:]

    adam_big(P_IN, small[0])

    def ordered(pick_small, pick_big):
        sm_it = iter(range(n_small))
        out = []
        for name in ("s", "s", "w_in", "s", "s", "s", "s", "s", "w_out", "s", "s", "w_gate", "w_up", "w_down", "s"):
            out.append(pick_small(next(sm_it)) if name == "s" else pick_big(name))
        return out

    grads = ordered(lambda i: g_small[i], lambda n: bigs[n][0])
    deltas = ordered(lambda i: d_small[i], lambda n: bigs[n][1])
    new_m = ordered(lambda i: nm_small[i], lambda n: bigs[n][2])
    new_v = ordered(lambda i: nv_small[i], lambda n: bigs[n][3])
    return (loss, grad_x, *grads, *deltas, *new_m, *new_v)
```

```python
import jax
import jax.numpy as jnp
from jax import lax
from jax.experimental import pallas as pl
from jax.experimental.pallas import tpu as pltpu

F32 = jnp.float32
BF16 = jnp.bfloat16
MESH = pl.DeviceIdType.MESH

N_META = 16
N_DEV = 8
RMS_EPS = 1e-6
LN_EPS = 1e-5
ADAM_LR = 0.001
ADAM_B1 = 0.9
ADAM_B2 = 0.999
ADAM_EPS = 1e-08
ADAM_WD = 0.01
ADAM_STEP = 10

LANE = 128
SUBLANE = 8
BF16_ROWS = 16
ROW_ALIGN = 128
N_ROW_BLOCKS = 4
CONV_HALO = 32
CONV_CHUNK = 64
WGRAD_ROWS = 32
N_CHUNK = 512
WGRAD_TILE_MAX = 1408
ADD_CHUNK = 32
ADAM_COL_BLOCKS = 2
COPY_PIECES = 4
V7X_VMEM_BYTES = 64 * 1024 * 1024
VMEM_LIMIT = V7X_VMEM_BYTES - 6 * 1024 * 1024
SMALL_ROWS = 64
SM_ROWS = 56
N_BIG = 5

ANY = pl.BlockSpec(memory_space=pl.ANY)
VMEM = pl.BlockSpec(memory_space=pltpu.VMEM)


def _cparams(n_grid_axes=0):
    sem = ("arbitrary",) * n_grid_axes if n_grid_axes else None
    return pltpu.CompilerParams(dimension_semantics=sem, vmem_limit_bytes=VMEM_LIMIT)


def _mesh_pos():
    return lax.axis_index("x"), lax.axis_index("y"), lax.axis_index("c")


def _dev_index(px, py, pc):
    return 4 * px + 2 * py + pc


def _other_chips(x, y):
    return [(1 - x, y), (x, 1 - y), (1 - x, 1 - y)]


def _full(shape):
    return pl.BlockSpec(shape, lambda *_: (0,) * len(shape))


def _resident(shape):
    return pl.BlockSpec(shape, lambda *_: (0,) * len(shape), pipeline_mode=pl.Buffered(1))


def _dot_nt(a, w):
    return lax.dot_general(a, w, (((1,), (1,)), ((), ())), preferred_element_type=F32)


def _dot_nn(a, w):
    return jnp.dot(a, w, preferred_element_type=F32)


def _chunks(n, c):
    out, o = [], 0
    while o < n:
        out.append((o, min(c, n - o)))
        o += c
    return out


def _rstd(h):
    return lax.rsqrt(jnp.mean(h * h, axis=-1, keepdims=True) + RMS_EPS)


def _rms_bwd(dyh, yh, r):
    return r * (dyh - yh * jnp.mean(dyh * yh, axis=-1, keepdims=True))


def _silu_grad(a, sig):
    return sig * (1.0 + a * (1.0 - sig))


def _acc_rows(ref, val, first):
    s = jnp.sum(val, axis=0, keepdims=True)

    @pl.when(first)
    def _():
        ref[...] = s

    @pl.when(jnp.logical_not(first))
    def _():
        ref[...] += s


def _row_loop(t_rows, chunk, fn, carry=None):
    def step(i, c):
        return fn(pl.multiple_of(i * chunk, chunk), c)

    return lax.fori_loop(0, t_rows // chunk, step, carry)


def _remote(src, dst, send_sem, recv_sem, to):
    return pltpu.make_async_remote_copy(src_ref=src, dst_ref=dst, send_sem=send_sem, recv_sem=recv_sem,
                                        device_id=to, device_id_type=MESH)


class _Comm:
    def __init__(self, inputs, out_shapes, aliases, scratch, start, finish):
        self.inputs, self.out_shapes, self.aliases, self.scratch = list(inputs), list(out_shapes), dict(aliases), list(scratch)
        self.start, self.finish = start, finish


def _merge_comms(comms):
    inputs, out_shapes, aliases, scratch, spans = [], [], {}, [], []
    for cm in comms:
        spans.append((len(inputs), len(out_shapes), len(scratch), cm))
        aliases.update({len(inputs) + k: len(out_shapes) + v for k, v in cm.aliases.items()})
        inputs += cm.inputs
        out_shapes += cm.out_shapes
        scratch += cm.scratch

    def run(which):
        def fn(ins, outs, scr):
            for i0, o0, s0, cm in spans:
                getattr(cm, which)(ins[i0:i0 + len(cm.inputs)], outs[o0:o0 + len(cm.out_shapes)], scr[s0:s0 + len(cm.scratch)])
        return fn

    return _Comm(inputs, out_shapes, aliases, scratch, run("start"), run("finish"))


def _host_call(body, *, grid, in_specs, out_specs, out_shape, args, name, scratch_shapes=(), comm=None, after=()):
    talks = comm is not None
    if comm is None:
        comm = _Comm([], [], {}, [], lambda *_: None, lambda *_: None)
    n_in, n_out, n_scr = len(args), len(out_shape), len(scratch_shapes)
    c_in, c_out = len(comm.inputs), len(comm.out_shapes)
    n_after = len(after)

    def open_comm(c_ins, c_outs, c_scr):
        if talks:
            _pair_handshake()
        comm.start(c_ins, c_outs, c_scr)

    def hosted(*refs):
        ins, c_ins = refs[:n_in], refs[n_in:n_in + c_in]
        o0 = n_in + c_in + n_after
        outs, c_outs = refs[o0:o0 + n_out], refs[o0 + n_out:o0 + n_out + c_out]
        s0 = o0 + n_out + c_out
        scr, c_scr = refs[s0:s0 + n_scr], refs[s0 + n_scr:]
        if not grid:
            open_comm(c_ins, c_outs, c_scr)
            body(*ins, *outs, *scr)
            comm.finish(c_ins, c_outs, c_scr)
            return
        first = last = None
        for a, n in enumerate(grid):
            f, l = pl.program_id(a) == 0, pl.program_id(a) == n - 1
            first = f if first is None else jnp.logical_and(first, f)
            last = l if last is None else jnp.logical_and(last, l)

        @pl.when(first)
        def _():
            open_comm(c_ins, c_outs, c_scr)

        body(*ins, *outs, *scr)

        @pl.when(last)
        def _():
            comm.finish(c_ins, c_outs, c_scr)

    sem = ("arbitrary",) * len(grid) if grid else None
    params = pltpu.CompilerParams(dimension_semantics=sem, vmem_limit_bytes=VMEM_LIMIT,
                                  collective_id=PAIR_BARRIER_ID if talks else None)
    res = pl.pallas_call(
        hosted, grid=grid, in_specs=list(in_specs) + [ANY] * (c_in + n_after), out_specs=list(out_specs) + [ANY] * c_out,
        out_shape=list(out_shape) + comm.out_shapes, scratch_shapes=list(scratch_shapes) + comm.scratch,
        input_output_aliases={n_in + k: n_out + v for k, v in comm.aliases.items()},
        name=name, compiler_params=params)(*args, *comm.inputs, *after)
    return list(res[:n_out]), list(res[n_out:])


PAIR_BARRIER_ID = 0
START_BARRIER_IDS = (1, 2, 3, 4, 5, 6)
ALL_PEERS_BARRIER_ID = 7


def _chips_handshake():
    x, y, c = _mesh_pos()
    barrier = pltpu.get_barrier_semaphore()
    for chip in _other_chips(x, y):
        pl.semaphore_signal(barrier, inc=1, device_id=(*chip, c), device_id_type=MESH)
    pl.semaphore_wait(barrier, 3)


def _pair_handshake():
    x, y, c = _mesh_pos()
    barrier = pltpu.get_barrier_semaphore()
    pl.semaphore_signal(barrier, inc=1, device_id=(x, y, 1 - c), device_id_type=MESH)
    pl.semaphore_wait(barrier, 1)


GATHER_SEMS = 10


class _Gather:
    def __init__(self, jobs, rows, lo, src_ref, dests, send_sems, recv_sems):
        x, y, c = _mesh_pos()
        me, sib = (x, y, c), (x, y, 1 - c)
        nx, ny, dg = (1 - x, y, c), (x, 1 - y, c), (1 - x, 1 - y, c)
        self.relayed, self.direct, self.relay, self.to_sib, self.sib_fwd = [], [], [], [], []
        for n, (p, r0, nr) in enumerate(jobs):
            assert nr % (2 * BF16_ROWS) == 0
            half = nr // 2

            def rows_of(dev, h, p=p, r0=r0, nr=nr, half=half):
                off, cnt = (r0, nr) if h is None else (r0 + h * half, half)
                return dests[p].at[pl.ds(pl.multiple_of(_dev_index(*dev) * rows[p] + off, BF16_ROWS), cnt), :]

            def mine(h, p=p, r0=r0, nr=nr, half=half):
                off, cnt = (r0, nr) if h is None else (r0 + h * half, half)
                return src_ref.at[pl.ds(lo[p] + off, cnt), :]

            sem = lambda k, n=n: (send_sems.at[GATHER_SEMS * n + k], recv_sems.at[GATHER_SEMS * n + k])
            self.relayed.append([_remote(mine(0), rows_of(me, 0), *sem(0), nx), _remote(mine(1), rows_of(me, 1), *sem(3), ny)])
            self.direct.append([_remote(mine(1), rows_of(me, 1), *sem(1), nx), _remote(mine(0), rows_of(me, 0), *sem(2), ny)])
            self.relay.append([_remote(rows_of(nx, 0), rows_of(nx, 0), *sem(4), ny), _remote(rows_of(ny, 1), rows_of(ny, 1), *sem(5), nx)])
            self.to_sib.append(_remote(mine(None), rows_of(me, None), *sem(6), sib))
            self.sib_fwd.append([_remote(rows_of(dev, None), rows_of(dev, None), *sem(7 + i), sib) for i, dev in enumerate((nx, ny, dg))])

    def start(self):
        for group in (self.relayed, self.direct):
            for cps in group:
                for cp in cps:
                    cp.start()
        for cp in self.to_sib:
            cp.start()

    def mid(self):
        for first, relay in zip(self.relayed, self.relay):
            for arrived, onward in zip(first, relay):
                arrived.wait_recv()
                onward.start()

    def finish(self):
        for direct, relay, fwd in zip(self.direct, self.relay, self.sib_fwd):
            for k in range(2):
                direct[k].wait_recv()
                fwd[k].start()
            for cp in relay:
                cp.wait_recv()
            fwd[2].start()
        for n in range(len(self.to_sib)):
            self.to_sib[n].wait_recv()
            for cp in self.sib_fwd[n]:
                cp.wait_recv()
            for cp in self.relayed[n] + self.direct[n] + self.relay[n] + [self.to_sib[n]] + self.sib_fwd[n]:
                cp.wait_send()


HBM = pl.BlockSpec(memory_space=pltpu.HBM)
SEM = pl.BlockSpec(memory_space=pltpu.SEMAPHORE)
FLOWS = pltpu.SideEffectType.DATAFLOW_SIDE_EFFECTING


def _in_hbm(a):
    return pltpu.with_memory_space_constraint(a, pltpu.HBM)


def _gather_start(wl, dests, ps, rows, barrier_id):
    lo = [sum(rows[:p]) for p in range(N_BIG)]
    n = len(ps)

    def body(*refs):
        wl_ref, dest_refs = refs[0], refs[1:1 + n]
        sends, recvs = refs[1 + n:1 + 2 * n], refs[1 + 2 * n:1 + 3 * n]
        token = refs[-1]
        _chips_handshake()
        x, y, c = _mesh_pos()
        jme = _dev_index(x, y, c)
        for i, p in enumerate(ps):
            mine = dest_refs[i].at[pl.ds(pl.multiple_of(jme * rows[p], BF16_ROWS), rows[p]), :]
            for chip in _other_chips(x, y):
                _remote(wl_ref.at[pl.ds(lo[p], rows[p]), :], mine, sends[i], recvs[i], (*chip, c)).start()
        token[...] = jnp.zeros_like(token)

    thru = [pltpu.HBM(wl.shape, wl.dtype)] + [pltpu.HBM(dests[p].shape, BF16) for p in ps]
    res = pl.pallas_call(
        body, name="gather_start",
        out_shape=tuple([pltpu.SemaphoreType.DMA(())] * (2 * n) + thru + [jax.ShapeDtypeStruct((SUBLANE, LANE), F32)]),
        in_specs=[HBM] * (1 + n), out_specs=tuple([SEM] * (2 * n) + [HBM] * (1 + n) + [VMEM]),
        input_output_aliases={i: 2 * n + i for i in range(1 + n)},
        compiler_params=pltpu.CompilerParams(has_side_effects=FLOWS, collective_id=barrier_id))(
            _in_hbm(wl), *[_in_hbm(dests[p]) for p in ps])
    sems = [(res[i], res[n + i]) for i in range(n)]
    return sems, res[2 * n], list(res[2 * n + 1:3 * n + 1]), res[-1]


def _gather_wait(wl, dest, sems, after, r, name):
    def body(wl_ref, dest_ref, send_sem, recv_sem, after_ref, wl_out, dest_out):
        x, y, c = _mesh_pos()
        three = dest_ref.at[pl.ds(0, 3 * r), :]
        cp = _remote(three, three, send_sem, recv_sem, (x, y, 1 - c))
        cp.wait_send()
        cp.wait_recv()

    res = pl.pallas_call(
        body, name=name, out_shape=(pltpu.HBM(wl.shape, wl.dtype), pltpu.HBM(dest.shape, dest.dtype)),
        in_specs=[HBM, HBM, SEM, SEM, ANY], out_specs=(HBM, HBM), input_output_aliases={0: 0, 1: 1},
        compiler_params=pltpu.CompilerParams(has_side_effects=FLOWS))(wl, dest, sems[0], sems[1], after)
    return res[0], res[1]


def _forward_comm(dest, r):
    def descs(ins, outs, scr):
        x, y, c = _mesh_pos()
        cps = []
        for k, chip in enumerate(_other_chips(x, y)):
            blk = outs[0].at[pl.ds(pl.multiple_of(_dev_index(*chip, c) * r, BF16_ROWS), r), :]
            cps.append(_remote(blk, blk, scr[0].at[k], scr[1].at[k], (x, y, 1 - c)))
        return cps

    def start(ins, outs, scr):
        for cp in descs(ins, outs, scr):
            cp.start()

    def finish(ins, outs, scr):
        cps = descs(ins, outs, scr)
        for cp in cps:
            cp.wait_recv()
        for cp in cps:
            cp.wait_send()

    return _Comm([dest], [jax.ShapeDtypeStruct(dest.shape, dest.dtype)], {0: 0},
                 [pltpu.SemaphoreType.DMA((3,)), pltpu.SemaphoreType.DMA((3,))], start, finish)


def _forward_now(dest, r, name):
    _, (dest,) = _host_call(lambda: None, grid=(), in_specs=[], out_specs=[], out_shape=[], args=(), name=name,
                            comm=_forward_comm(dest, r))
    return dest


def _pair_comm(g, r, row0=0):
    d = g.shape[1]

    def descs(ins, outs, scr):
        x, y, c = _mesh_pos()
        chips = [(x, y)] + _other_chips(x, y)
        return [_remote(ins[0].at[pl.ds(pl.multiple_of(row0 + _dev_index(*chip, 1 - c) * r, BF16_ROWS), r), :], outs[0].at[k],
                        scr[0].at[k], scr[1].at[k], (x, y, 1 - c)) for k, chip in enumerate(chips)]

    def start(ins, outs, scr):
        for cp in descs(ins, outs, scr):
            cp.start()

    def finish(ins, outs, scr):
        cps = descs(ins, outs, scr)
        for cp in cps:
            cp.wait_recv()
        for cp in cps:
            cp.wait_send()

    comm = _Comm([g], [jax.ShapeDtypeStruct((4, r, d), BF16)], {},
                 [pltpu.SemaphoreType.DMA((4,)), pltpu.SemaphoreType.DMA((4,))], start, finish)
    return comm


def _pair_sum(g, pair, r, name, after=(), row0=0):
    d = g.shape[1]

    def body(g_ref, p_ref, *rest):
        o_ref, gbuf, pbuf, sems = rest[len(after):]
        x, y, c = _mesh_pos()
        loads = [pltpu.make_async_copy(p_ref.at[pl.ds(1, 3)], pbuf, sems.at[3])]
        for k, chip in enumerate(_other_chips(x, y)):
            j = _dev_index(*chip, c)
            loads.append(pltpu.make_async_copy(g_ref.at[pl.ds(pl.multiple_of(row0 + j * r, BF16_ROWS), r), :], gbuf.at[k], sems.at[k]))
        for cp in loads:
            cp.start()
        for cp in loads:
            cp.wait()
        for k in range(3):
            o_ref[k] = (gbuf[k].astype(F32) + pbuf[k].astype(F32)).astype(BF16)

    return pl.pallas_call(
        body, out_shape=jax.ShapeDtypeStruct((3, r, d), BF16), in_specs=[ANY] * (2 + len(after)), out_specs=VMEM,
        scratch_shapes=[pltpu.VMEM((3, r, d), BF16), pltpu.VMEM((3, r, d), BF16), pltpu.SemaphoreType.DMA((4,))],
        name=name, compiler_params=_cparams())(g, pair, *after)


def _chip_start(sums, name, barrier_id):
    n = len(sums)

    def body(*refs):
        srcs, lands = refs[:n], refs[n:2 * n]
        sends, recvs = refs[2 * n:3 * n], refs[3 * n:4 * n]
        _chips_handshake()
        x, y, c = _mesh_pos()
        for i in range(n):
            for k, chip in enumerate(_other_chips(x, y)):
                _remote(srcs[i].at[k], lands[i].at[k], sends[i], recvs[i], (*chip, c)).start()
        refs[-1][...] = jnp.zeros_like(refs[-1])

    zones = [pltpu.HBM(s.shape, s.dtype) for s in sums]
    res = pl.pallas_call(
        body, name=name,
        out_shape=tuple([pltpu.SemaphoreType.DMA(())] * (2 * n) + zones + zones + [jax.ShapeDtypeStruct((SUBLANE, LANE), F32)]),
        in_specs=[HBM] * (2 * n), out_specs=tuple([SEM] * (2 * n) + [HBM] * (2 * n) + [VMEM]),
        input_output_aliases={i: 2 * n + i for i in range(2 * n)},
        compiler_params=pltpu.CompilerParams(has_side_effects=FLOWS, collective_id=barrier_id))(
            *[_in_hbm(s) for s in sums], *[_in_hbm(lax.empty(s.shape, s.dtype)) for s in sums])
    flights = [((res[i], res[n + i]), res[2 * n + i], res[3 * n + i]) for i in range(n)]
    return flights, res[-1]


def _chip_wait(sums, land, sems, after, name):
    def body(sums_ref, land_ref, send_sem, recv_sem, after_ref, sums_out, land_out):
        x, y, c = _mesh_pos()
        cp = _remote(sums_ref, land_ref, send_sem, recv_sem, (x, y, 1 - c))
        cp.wait_send()
        cp.wait_recv()

    res = pl.pallas_call(
        body, name=name, out_shape=(pltpu.HBM(sums.shape, sums.dtype), pltpu.HBM(land.shape, land.dtype)),
        in_specs=[HBM, HBM, SEM, SEM, ANY], out_specs=(HBM, HBM), input_output_aliases={0: 0, 1: 1},
        compiler_params=pltpu.CompilerParams(has_side_effects=FLOWS))(sums, land, sems[0], sems[1], after)
    return res[1]


class _CopyThrough:
    def __init__(self, src_ref, dst_ref, dst_row0, n_rows, buf, sem_in, sem_out):
        rc = n_rows // COPY_PIECES
        piece = lambda ref, o: ref.at[pl.ds(o, rc), :]
        self.loads = [pltpu.make_async_copy(piece(src_ref, k * rc), piece(buf, k * rc), sem_in) for k in range(COPY_PIECES)]
        self.stores = [pltpu.make_async_copy(piece(buf, k * rc), piece(dst_ref, dst_row0 + k * rc), sem_out) for k in range(COPY_PIECES)]
        self.all_in = pltpu.make_async_copy(src_ref, buf, sem_in)
        self.all_out = pltpu.make_async_copy(buf, dst_ref.at[pl.ds(dst_row0, n_rows), :], sem_out)

    def load(self):
        for cp in self.loads:
            cp.start()

    def store(self):
        self.all_in.wait()
        for cp in self.stores:
            cp.start()

    def done(self):
        self.all_out.wait()


def _gather_first(shards, sm, jobs, x2, tgt2, t_rows, x0):
    d = shards[0].shape[1]
    rows = [w.shape[0] for w in shards]
    lo = [sum(rows[:p]) for p in range(N_BIG)]
    n_sems = GATHER_SEMS * len(jobs)
    seq = x2.shape[0]
    assert x0 == ROW_ALIGN and seq % ROW_ALIGN == 0 and d == N_DEV * LANE

    def body(s0, s1, s2, s3, s4, sm_ref, x_ref, tgt_ref, wl_ref, o0, o1, o2, o3, o4, sa_ref, h0_ref, tp_ref,
             wl_v, x_v, tgt_v, heads_v, sa_v, send_sems, recv_sems, ssend, srecv, local_sems, sems_in, sems_out):
        dests = (o0, o1, o2, o3, o4)
        x, y, c = _mesh_pos()
        me = (x, y, c)
        jme = _dev_index(*me)
        peers = [(x, y, 1 - c)] + [(*chip, pc) for pc in (c, 1 - c) for chip in _other_chips(x, y)]
        barrier = pltpu.get_barrier_semaphore()
        for to in peers:
            pl.semaphore_signal(barrier, inc=1, device_id=to, device_id_type=MESH)
        pl.semaphore_wait(barrier, len(peers))
        padded = [_CopyThrough(x_ref, h0_ref, x0, seq, x_v, sems_in.at[0], sems_out.at[0]),
                  _CopyThrough(tgt_ref, tp_ref, x0, seq, tgt_v, sems_in.at[1], sems_out.at[1])]
        for cp in padded:
            cp.load()
        shard_refs = (s0, s1, s2, s3, s4)
        first = sorted({j[0] for j in jobs})
        for p in first + [p for p in range(N_BIG) if p not in first]:
            wl_v[pl.ds(lo[p], rows[p]), :] = shard_refs[p][...].astype(BF16)
            if p == first[-1]:
                gather = _Gather(jobs, rows, lo, wl_v, dict(enumerate(dests)), send_sems, recv_sems)
                gather.start()
        smalls = [_remote(sm_ref, sa_ref.at[jme], ssend.at[k], srecv.at[k], to) for k, to in enumerate(peers)]
        for cp in smalls:
            cp.start()
        mine = [pltpu.make_async_copy(wl_v.at[pl.ds(lo[p], rows[p]), :],
                                      dests[p].at[pl.ds(pl.multiple_of(jme * rows[p], BF16_ROWS), rows[p]), :], local_sems.at[p])
                for p in range(N_BIG)]
        mine.append(pltpu.make_async_copy(wl_v, wl_ref, local_sems.at[N_BIG]))
        mine.append(pltpu.make_async_copy(sm_ref, sa_ref.at[jme], local_sems.at[N_BIG + 1]))
        for cp in mine:
            cp.start()
        later = [p for p in range(N_BIG) if p not in {j[0] for j in jobs}]
        own = [_remote(wl_v.at[pl.ds(lo[p], rows[p]), :], dests[p].at[pl.ds(pl.multiple_of(jme * rows[p], BF16_ROWS), rows[p]), :],
                       ssend.at[7 + i], srecv.at[7 + i], (x, y, 1 - c)) for i, p in enumerate(later)]
        for cp in own:
            cp.start()
        gather.mid()
        for cp in padded:
            cp.store()
        for cp in smalls + own:
            cp.wait_recv()
        mine[-1].wait()
        to_v = pltpu.make_async_copy(sa_ref, sa_v, local_sems.at[N_BIG + 1])
        to_v.start()
        to_v.wait()
        head, zeros = heads_v.at[0], heads_v.at[1]
        head[...] = jnp.zeros_like(head)
        zeros[...] = jnp.zeros_like(zeros)
        for j in range(N_DEV):
            head[pl.ds(x0 - N_META, N_META), pl.ds(j * LANE, LANE)] = sa_v[j, pl.ds(0, N_META), :]
        heads = [pltpu.make_async_copy(head, h0_ref.at[pl.ds(0, x0), :], local_sems.at[N_BIG + 1]),
                 pltpu.make_async_copy(zeros, tp_ref.at[pl.ds(0, x0), :], local_sems.at[N_BIG + 2])]
        for cp in heads:
            cp.start()
        gather.finish()
        for cp in smalls + own:
            cp.wait_send()
        for cp in mine[:-1] + heads:
            cp.wait()
        for cp in padded:
            cp.done()

    out_shape = [jax.ShapeDtypeStruct((sum(rows), d), BF16)]
    out_shape += [jax.ShapeDtypeStruct((N_DEV * r, d), BF16) for r in rows]
    out_shape.append(jax.ShapeDtypeStruct((N_DEV,) + sm.shape, F32))
    out_shape += [jax.ShapeDtypeStruct((t_rows, d), F32)] * 2
    res = pl.pallas_call(
        body, out_shape=out_shape, in_specs=[VMEM] * 6 + [ANY] * 2, out_specs=[ANY] * 9,
        scratch_shapes=[pltpu.VMEM((sum(rows), d), BF16), pltpu.VMEM((seq, d), F32), pltpu.VMEM((seq, d), F32),
                        pltpu.VMEM((2, ROW_ALIGN, d), F32), pltpu.VMEM((N_DEV,) + sm.shape, F32),
                        pltpu.SemaphoreType.DMA((n_sems,)), pltpu.SemaphoreType.DMA((n_sems,)),
                        pltpu.SemaphoreType.DMA((7 + N_BIG,)), pltpu.SemaphoreType.DMA((7 + N_BIG,)),
                        pltpu.SemaphoreType.DMA((N_BIG + 3,)), pltpu.SemaphoreType.DMA((2,)), pltpu.SemaphoreType.DMA((2,))],
        name="gather_first",
        compiler_params=pltpu.CompilerParams(vmem_limit_bytes=VMEM_LIMIT, collective_id=ALL_PEERS_BARRIER_ID))(*shards, sm, x2, tgt2)
    return res[0], list(res[1:1 + N_BIG]), res[1 + N_BIG], res[2 + N_BIG], res[3 + N_BIG]


def _in_proj(h0, g1, win_t, tm, comm):
    t_rows, d = h0.shape
    e = win_t.shape[0]

    def body(h_ref, g_ref, w_ref, xn_ref, hin_ref):
        h = h_ref[...]
        xn = ((h * _rstd(h)) * g_ref[...]).astype(BF16)
        xn_ref[...] = xn
        for o, n in _chunks(e, N_CHUNK):
            hin_ref[:, pl.ds(o, n)] = _dot_nt(xn, w_ref[pl.ds(o, n), :])

    return _host_call(
        body, grid=(t_rows // tm,),
        in_specs=[pl.BlockSpec((tm, d), lambda i: (i, 0)), _full((1, d)), _resident((e, d))],
        out_specs=[pl.BlockSpec((tm, d), lambda i: (i, 0)), pl.BlockSpec((tm, e), lambda i: (i, 0))],
        out_shape=[jax.ShapeDtypeStruct((t_rows, d), BF16), jax.ShapeDtypeStruct((t_rows, e), F32)],
        args=(h0, g1, win_t), name="in_proj", comm=comm)


def _tap_slot(off):
    return off % SUBLANE, (off // SUBLANE) * SUBLANE


def _fill_shifted(sh_ref, base_ref, residues, n_rows):
    for r in residues:
        if r:
            sh_ref[r] = base_ref[pl.ds(r, n_rows), :]


def _shifted_rows(pair, r, start, n):
    base_ref, sh_ref = pair
    return base_ref[pl.ds(start, n), :] if r == 0 else sh_ref[r, pl.ds(start, n), :]


def _mix_conv_fwd(hin, wa, wb, bb, wa_w, comm):
    t_rows = hin.shape[0]
    nt = wa_w // LANE
    ka, kb = wa.shape[0], wb.shape[0]
    nr = CONV_HALO + t_rows

    def body(bg_ref, cg_ref, ha_ref, val_ref, gt_ref, wa_ref, wb_ref, bb_ref, ya_ref, z_ref, base, sh):
        base[pl.ds(0, CONV_HALO), :] = jnp.zeros((CONV_HALO, LANE), F32)
        base[pl.ds(nr, SUBLANE), :] = jnp.zeros((SUBLANE, LANE), F32)

        def conv(w_ref, k_taps, b, n):
            acc = None
            for k in range(k_taps):
                r, q = _tap_slot(CONV_HALO - (k_taps - 1) + k)
                term = w_ref[pl.ds(k, 1), :] * _shifted_rows((base, sh), r, b + q, n)
                acc = term if acc is None else acc + term
            return acc

        def fill_a(b, c):
            base[pl.ds(CONV_HALO + b, CONV_CHUNK), :] = cg_ref[pl.ds(b, CONV_CHUNK), :] * ha_ref[pl.ds(b, CONV_CHUNK), :]
            return c

        _row_loop(t_rows, CONV_CHUNK, fill_a)
        _fill_shifted(sh, base, sorted({_tap_slot(CONV_HALO - (ka - 1) + k)[0] for k in range(ka)}), nr)

        def out_a(b, c):
            ya_ref[pl.ds(b, CONV_CHUNK), :] = (bg_ref[pl.ds(b, CONV_CHUNK), :] * conv(wa_ref, ka, b, CONV_CHUNK)).astype(BF16)
            return c

        _row_loop(t_rows, CONV_CHUNK, out_a)

        def fill_b(b, c):
            base[pl.ds(CONV_HALO + b, CONV_CHUNK), :] = (val_ref[pl.ds(b, CONV_CHUNK), :]
                                                          * jax.nn.sigmoid(gt_ref[pl.ds(b, CONV_CHUNK), :]))
            return c

        _row_loop(t_rows, CONV_CHUNK, fill_b)
        _fill_shifted(sh, base, range(SUBLANE), nr)

        def out_b(b, c):
            z_ref[pl.ds(b, CONV_CHUNK), :] = conv(wb_ref, kb, b, CONV_CHUNK) + bb_ref[...]
            return c

        _row_loop(t_rows, CONV_CHUNK, out_b)

    def col(g):
        return pl.BlockSpec((t_rows, LANE), lambda i, g=g: (0, g * nt + i))

    tile = lambda rows: pl.BlockSpec((rows, LANE), lambda i: (0, i))
    return _host_call(
        body, grid=(nt,),
        in_specs=[col(0), col(1), col(2), col(3), col(4), tile(ka), tile(kb), tile(1)],
        out_specs=[tile(t_rows), tile(t_rows)],
        out_shape=[jax.ShapeDtypeStruct((t_rows, wa_w), BF16), jax.ShapeDtypeStruct((t_rows, wa_w), F32)],
        scratch_shapes=[pltpu.VMEM((nr + SUBLANE, LANE), F32), pltpu.VMEM((SUBLANE, nr, LANE), F32)],
        args=(hin, hin, hin, hin, hin, wa, wb, bb), name="mix_conv_fwd", comm=comm)


def _ln_parts(z, lg, lb):
    mu = jnp.mean(z, axis=-1, keepdims=True)
    zc = z - mu
    rstd = lax.rsqrt(jnp.mean(zc * zc, axis=-1, keepdims=True) + LN_EPS)
    zh = zc * rstd
    return zh, rstd, zh * lg + lb


def _out_proj(ya, z, lg, lb, w_out, h0, g2, g3, tm, comm):
    t_rows, d = h0.shape
    w = z.shape[1]

    def body(ya_ref, z_ref, lg_ref, lb_ref, w_ref, h0_ref, g2_ref, g3_ref, y_ref, mix_ref, h1_ref, xn2_ref):
        _, _, ln = _ln_parts(z_ref[...], lg_ref[...], lb_ref[...])
        y_ref[:, pl.ds(0, w)] = ya_ref[...]
        y_ref[:, pl.ds(w, w)] = (ln * jax.nn.sigmoid(ln)).astype(BF16)
        mix = _dot_nn(y_ref[...], w_ref[...])
        mix_ref[...] = mix
        h1 = h0_ref[...] + (mix * _rstd(mix)) * g2_ref[...]
        h1_ref[...] = h1
        xn2_ref[...] = ((h1 * _rstd(h1)) * g3_ref[...]).astype(BF16)

    blk = pl.BlockSpec((tm, d), lambda i: (i, 0))
    half = pl.BlockSpec((tm, w), lambda i: (i, 0))
    return _host_call(
        body, grid=(t_rows // tm,),
        in_specs=[half, half, _full((1, w)), _full((1, w)), _resident(w_out.shape), blk, _full((1, d)), _full((1, d))],
        out_specs=[blk, blk, blk, blk],
        out_shape=[jax.ShapeDtypeStruct((t_rows, d), BF16), jax.ShapeDtypeStruct((t_rows, d), F32),
                   jax.ShapeDtypeStruct((t_rows, d), F32), jax.ShapeDtypeStruct((t_rows, d), BF16)],
        args=(ya, z, lg, lb, w_out, h0, g2, g3), name="out_proj", comm=comm)


def _gate_up(xn2, wg_t, wu_t, tm, comm):
    t_rows, d = xn2.shape
    f = wg_t.shape[0]

    def body(x_ref, wg_ref, wu_ref, ga_ref, gu_ref, s_ref):
        xn = x_ref[...]
        for o, n in _chunks(f, N_CHUNK):
            a = _dot_nt(xn, wg_ref[pl.ds(o, n), :])
            u = _dot_nt(xn, wu_ref[pl.ds(o, n), :])
            sig = jax.nn.sigmoid(a)
            silu = a * sig
            s = silu * u
            gu_ref[:, pl.ds(o, n)] = silu.astype(BF16)
            ga_ref[:, pl.ds(o, n)] = ((u - s) * sig + s).astype(BF16)
            s_ref[:, pl.ds(o, n)] = s.astype(BF16)

    blk = pl.BlockSpec((tm, f), lambda i: (i, 0))
    return _host_call(
        body, grid=(t_rows // tm,),
        in_specs=[pl.BlockSpec((tm, d), lambda i: (i, 0)), _resident((f, d)), _resident((f, d))],
        out_specs=[blk, blk, blk], out_shape=[jax.ShapeDtypeStruct((t_rows, f), BF16)] * 3,
        args=(xn2, wg_t, wu_t), name="gate_up", comm=comm)


def _down_loss(s, wd, h1, tgt, g4, tm, x0):
    t_rows, d = h1.shape
    f = wd.shape[0]

    def body(s_ref, w_ref, h1_ref, tgt_ref, g4_ref, dh2_ref, dff_ref, dg4_ref, loss_ref):
        i = pl.program_id(0)
        ff = _dot_nn(s_ref[...], w_ref[...])
        r4 = _rstd(ff)
        fh = ff * r4
        g4 = g4_ref[...]
        h2 = h1_ref[...] + fh * g4
        row = i * tm + lax.broadcasted_iota(jnp.int32, (tm, 1), 0)
        diff = jnp.where(row >= x0, h2 - tgt_ref[...], 0.0)
        dh2 = diff / d
        dh2_ref[...] = dh2
        dff_ref[...] = _rms_bwd(dh2 * g4, fh, r4).astype(BF16)
        _acc_rows(dg4_ref, dh2 * fh, i == 0)
        _acc_rows(loss_ref, diff * diff, i == 0)

    blk = pl.BlockSpec((tm, d), lambda i: (i, 0))
    res, _ = _host_call(
        body, grid=(t_rows // tm,),
        in_specs=[pl.BlockSpec((tm, f), lambda i: (i, 0)), _resident((f, d)), blk, blk, _full((1, d))],
        out_specs=[blk, blk, _full((1, d)), _full((1, d))],
        out_shape=[jax.ShapeDtypeStruct((t_rows, d), F32), jax.ShapeDtypeStruct((t_rows, d), BF16),
                   jax.ShapeDtypeStruct((1, d), F32), jax.ShapeDtypeStruct((1, d), F32)],
        args=(s, wd, h1, tgt, g4), name="down_loss")
    return res


def _bwd_down(dff, wd, ga, gu, tm, comm):
    t_rows, d = dff.shape
    f = wd.shape[0]

    def body(dff_ref, w_ref, ga_ref, gu_ref, dau_ref):
        dff_v = dff_ref[...]
        for o, n in _chunks(f, N_CHUNK):
            ds = _dot_nt(dff_v, w_ref[pl.ds(o, n), :]).astype(BF16)
            dau_ref[0, :, pl.ds(o, n)] = ds * ga_ref[:, pl.ds(o, n)]
            dau_ref[1, :, pl.ds(o, n)] = ds * gu_ref[:, pl.ds(o, n)]

    blk = pl.BlockSpec((tm, f), lambda i: (i, 0))
    (dau,), extra = _host_call(
        body, grid=(t_rows // tm,),
        in_specs=[pl.BlockSpec((tm, d), lambda i: (i, 0)), _resident((f, d)), blk, blk],
        out_specs=[pl.BlockSpec((2, tm, f), lambda i: (0, i, 0))], out_shape=[jax.ShapeDtypeStruct((2, t_rows, f), BF16)],
        args=(dff, wd, ga, gu), name="bwd_down", comm=comm)
    return dau, extra


def _wgrad(a, b, name, after=()):
    d = b.shape[1]
    t_rows = b.shape[0]
    stacked = a.ndim == 3
    n = a.shape[-1]
    groups = a.shape[0] if stacked else 1
    steps = 1 if stacked else 2
    tile = max(t for t in range(LANE, min(n // steps, WGRAD_TILE_MAX) + 1, LANE) if n % t == 0)
    tiles = n // tile

    def body(a_ref, b_ref, o_ref):
        o_ref[...] = lax.dot_general(a_ref[...], b_ref[...], (((0,), (0,)), ((), ())),
                                     preferred_element_type=F32).astype(BF16)

    if stacked:
        a_spec = pl.BlockSpec((None, t_rows, tile), lambda g, i: (g, 0, i))
    else:
        a_spec = pl.BlockSpec((t_rows, tile), lambda g, i: (0, i))
    res, _ = _host_call(
        body, grid=(groups, tiles), in_specs=[a_spec, _resident((t_rows, d))],
        out_specs=[pl.BlockSpec((tile, d), lambda g, i: (g * tiles + i, 0))],
        out_shape=[jax.ShapeDtypeStruct((groups * n, d), BF16)], args=(a, b), name=name, after=after)
    return res[0]


def _bwd_ffn_in(dau, wg_t, wu_t, h1, dh2, g3, tm, comm):
    t_rows, d = h1.shape
    f = wg_t.shape[0]

    def body(dau_ref, wg_ref, wu_ref, h1_ref, dh2_ref, g3_ref, dh1_ref, dg3_ref):
        dxn2 = _dot_nn(dau_ref[0], wg_ref[...]) + _dot_nn(dau_ref[1], wu_ref[...])
        h1 = h1_ref[...]
        r3 = _rstd(h1)
        h1h = h1 * r3
        _acc_rows(dg3_ref, dxn2 * h1h, pl.program_id(0) == 0)
        dh1_ref[...] = dh2_ref[...] + _rms_bwd(dxn2 * g3_ref[...], h1h, r3)

    blk = pl.BlockSpec((tm, d), lambda i: (i, 0))
    return _host_call(
        body, grid=(t_rows // tm,),
        in_specs=[pl.BlockSpec((2, tm, f), lambda i: (0, i, 0)), _resident((f, d)), _resident((f, d)), blk, blk, _full((1, d))],
        out_specs=[blk, _full((1, d))],
        out_shape=[jax.ShapeDtypeStruct((t_rows, d), F32), jax.ShapeDtypeStruct((1, d), F32)],
        args=(dau, wg_t, wu_t, h1, dh2, g3), name="bwd_ffn_in", comm=comm)


def _bwd_out_proj(dh1, mix, w_out, g2, z, lg, lb, tm, after):
    t_rows, d = dh1.shape
    w = z.shape[1]

    def body(dh1_ref, mix_ref, w_ref, g2_ref, z_ref, lg_ref, lb_ref, dmix_ref, dya_ref, dz_ref, dg2_ref, dlg_ref, dlb_ref, dbb_ref):
        first = pl.program_id(0) == 0
        mix = mix_ref[...]
        r2 = _rstd(mix)
        mh = mix * r2
        dh1 = dh1_ref[...]
        _acc_rows(dg2_ref, dh1 * mh, first)
        dmix = _rms_bwd(dh1 * g2_ref[...], mh, r2).astype(BF16)
        dmix_ref[...] = dmix
        dy = _dot_nt(dmix, w_ref[...])
        dya_ref[...] = dy[:, :w]
        lg = lg_ref[...]
        zh, rstd, ln = _ln_parts(z_ref[...], lg, lb_ref[...])
        dln = dy[:, w:] * _silu_grad(ln, jax.nn.sigmoid(ln))
        _acc_rows(dlg_ref, dln * zh, first)
        _acc_rows(dlb_ref, dln, first)
        dzh = dln * lg
        dz = rstd * (dzh - jnp.mean(dzh, axis=-1, keepdims=True) - zh * jnp.mean(dzh * zh, axis=-1, keepdims=True))
        dz_ref[...] = dz
        _acc_rows(dbb_ref, dz, first)

    blk = pl.BlockSpec((tm, d), lambda i: (i, 0))
    half = pl.BlockSpec((tm, w), lambda i: (i, 0))
    vec = _full((1, w))
    res, _ = _host_call(
        body, grid=(t_rows // tm,), in_specs=[blk, blk, _resident(w_out.shape), _full((1, d)), half, vec, vec],
        out_specs=[blk, half, half, _full((1, d)), vec, vec, vec],
        out_shape=[jax.ShapeDtypeStruct((t_rows, d), BF16), jax.ShapeDtypeStruct((t_rows, w), F32),
                   jax.ShapeDtypeStruct((t_rows, w), F32), jax.ShapeDtypeStruct((1, d), F32)]
        + [jax.ShapeDtypeStruct((1, w), F32)] * 3,
        args=(dh1, mix, w_out, g2, z, lg, lb), name="bwd_out_proj", after=after)
    return res


def _mix_conv_bwd(hin, dy, dz, wa, wb, wa_w, comm):
    t_rows = hin.shape[0]
    nt = wa_w // LANE
    ka, kb = wa.shape[0], wb.shape[0]
    nr = CONV_HALO + t_rows
    kb_rows = -(-kb // SUBLANE) * SUBLANE

    def body(bg_ref, cg_ref, ha_ref, val_ref, gt_ref, dya_ref, dz_ref, wa_ref, wb_ref,
             dh_ref, dwa_ref, dwb_ref, base, sh, based, shd, tmp, wbc):
        zeros = lambda n: jnp.zeros((n, LANE), F32)
        base[pl.ds(0, CONV_HALO), :] = zeros(CONV_HALO)
        base[pl.ds(nr, SUBLANE), :] = zeros(SUBLANE)
        based[pl.ds(t_rows, CONV_HALO + SUBLANE), :] = zeros(CONV_HALO + SUBLANE)

        def fwd_slot(k_taps, k):
            return _tap_slot(CONV_HALO - (k_taps - 1) + k)

        def bwd_slot(k_taps, k):
            return _tap_slot(k_taps - 1 - k)

        def conv(w_ref, k_taps, src, slot, b, n):
            acc = None
            for k in range(k_taps):
                r, q = slot(k_taps, k)
                term = w_ref[pl.ds(k, 1), :] * _shifted_rows(src, r, b + q, n)
                acc = term if acc is None else acc + term
            return acc

        def by_residue(k_taps, slot):
            groups = {}
            for k in range(k_taps):
                r, q = slot(k_taps, k)
                groups.setdefault(r, []).append((k, q // SUBLANE))
            return groups

        def wgrad_loop(w_ref, k_taps):
            n_sub = WGRAD_ROWS // SUBLANE
            for k in range(k_taps):
                wbc[k] = jnp.broadcast_to(w_ref[pl.ds(k, 1), :], (SUBLANE, LANE))
            fwd, bwd = by_residue(k_taps, fwd_slot), by_residue(k_taps, bwd_slot)

            def window(src, r, taps, b):
                span = n_sub + max(qi for _, qi in taps)
                return [_shifted_rows(src, r, b + SUBLANE * i, SUBLANE) for i in range(span)]

            def step(b, accs):
                accs = list(accs)
                dv = [based[pl.ds(b + SUBLANE * j, SUBLANE), :] for j in range(n_sub)]
                for r, taps in fwd.items():
                    win = window((base, sh), r, taps, b)
                    for k, qi in taps:
                        t = dv[0] * win[qi]
                        for j in range(1, n_sub):
                            t = t + dv[j] * win[qi + j]
                        accs[k] = accs[k] + t
                outs = [None] * n_sub
                for r, taps in bwd.items():
                    win = window((based, shd), r, taps, b)
                    for k, qi in taps:
                        wk = wbc[k]
                        for j in range(n_sub):
                            term = wk * win[qi + j]
                            outs[j] = term if outs[j] is None else outs[j] + term
                for j in range(n_sub):
                    tmp[pl.ds(b + SUBLANE * j, SUBLANE), :] = outs[j]
                return tuple(accs)

            return _row_loop(t_rows, WGRAD_ROWS, step, tuple(zeros(SUBLANE) for _ in range(k_taps)))

        def store_taps(ref, accs, rows):
            for k, acc in enumerate(accs):
                ref[pl.ds(k, 1), :] = jnp.sum(acc, axis=0, keepdims=True)
            if rows > len(accs):
                ref[pl.ds(len(accs), rows - len(accs)), :] = zeros(rows - len(accs))

        def fill_a(b, c):
            sl = pl.ds(b, CONV_CHUNK)
            base[pl.ds(CONV_HALO + b, CONV_CHUNK), :] = cg_ref[sl, :] * ha_ref[sl, :]
            based[sl, :] = dya_ref[sl, :] * bg_ref[sl, :]
            return c

        _row_loop(t_rows, CONV_CHUNK, fill_a)
        _fill_shifted(sh, base, sorted({fwd_slot(ka, k)[0] for k in range(ka)}), nr)
        _fill_shifted(shd, based, sorted({bwd_slot(ka, k)[0] for k in range(ka)}), nr)

        def d_bgate(b, c):
            sl = pl.ds(b, CONV_CHUNK)
            dh_ref[0, sl, :] = (dya_ref[sl, :] * conv(wa_ref, ka, (base, sh), fwd_slot, b, CONV_CHUNK)).astype(BF16)
            return c

        _row_loop(t_rows, CONV_CHUNK, d_bgate)
        store_taps(dwa_ref, wgrad_loop(wa_ref, ka), SUBLANE)

        def d_ch(b, c):
            sl = pl.ds(b, CONV_CHUNK)
            dua = tmp[sl, :]
            dh_ref[1, sl, :] = (dua * ha_ref[sl, :]).astype(BF16)
            dh_ref[2, sl, :] = (dua * cg_ref[sl, :]).astype(BF16)
            return c

        _row_loop(t_rows, CONV_CHUNK, d_ch)

        def fill_b(b, c):
            sl = pl.ds(b, CONV_CHUNK)
            base[pl.ds(CONV_HALO + b, CONV_CHUNK), :] = val_ref[sl, :] * jax.nn.sigmoid(gt_ref[sl, :])
            based[sl, :] = dz_ref[sl, :]
            return c

        _row_loop(t_rows, CONV_CHUNK, fill_b)
        _fill_shifted(sh, base, range(SUBLANE), nr)
        _fill_shifted(shd, based, range(SUBLANE), nr)
        store_taps(dwb_ref, wgrad_loop(wb_ref, kb), kb_rows)

        def d_glu(b, c):
            sl = pl.ds(b, CONV_CHUNK)
            dgg = tmp[sl, :]
            sig = jax.nn.sigmoid(gt_ref[sl, :])
            dh_ref[3, sl, :] = (dgg * sig).astype(BF16)
            dh_ref[4, sl, :] = (dgg * val_ref[sl, :] * (sig * (1.0 - sig))).astype(BF16)
            return c

        _row_loop(t_rows, CONV_CHUNK, d_glu)

    def col(g):
        return pl.BlockSpec((t_rows, LANE), lambda i, g=g: (0, g * nt + i))

    tile = lambda rows: pl.BlockSpec((rows, LANE), lambda i: (0, i))
    return _host_call(
        body, grid=(nt,),
        in_specs=[col(0), col(1), col(2), col(3), col(4), tile(t_rows), tile(t_rows), tile(ka), tile(kb)],
        out_specs=[pl.BlockSpec((5, t_rows, LANE), lambda i: (0, 0, i)), tile(SUBLANE), tile(kb_rows)],
        out_shape=[jax.ShapeDtypeStruct((5, t_rows, wa_w), BF16), jax.ShapeDtypeStruct((SUBLANE, wa_w), F32),
                   jax.ShapeDtypeStruct((kb_rows, wa_w), F32)],
        scratch_shapes=[pltpu.VMEM((nr + SUBLANE, LANE), F32), pltpu.VMEM((SUBLANE, nr, LANE), F32),
                        pltpu.VMEM((nr + SUBLANE, LANE), F32), pltpu.VMEM((SUBLANE, nr, LANE), F32),
                        pltpu.VMEM((t_rows, LANE), F32), pltpu.VMEM((kb_rows, SUBLANE, LANE), F32)],
        args=(hin, hin, hin, hin, hin, dy, dz, wa, wb), name="mix_conv_bwd", comm=comm)


def _bwd_in_proj(dh5, win_t, h0, dh1, g1, tm, x0, comm):
    t_rows, d = h0.shape
    groups, _, w = dh5.shape
    n_blk = t_rows // tm
    assert N_META <= x0 <= tm

    def body(dh_ref, w_ref, h0_ref, dh1_ref, g1_ref, gx_ref, dg1_ref, dmeta_ref, buf, sems):
        i = pl.program_id(0)

        def out_copy(b):
            lo = x0 if b == 0 else 0
            return pltpu.make_async_copy(buf.at[b, pl.ds(lo, tm - lo), :], gx_ref.at[pl.ds(b * tm + lo - x0, tm - lo), :],
                                         sems.at[b])

        dxn1 = None
        for g in range(groups):
            part = _dot_nn(dh_ref[g], w_ref[pl.ds(g * w, w), :])
            dxn1 = part if dxn1 is None else dxn1 + part
        h0 = h0_ref[...]
        r1 = _rstd(h0)
        h0h = h0 * r1
        _acc_rows(dg1_ref, dxn1 * h0h, i == 0)
        buf[i] = dh1_ref[...] + _rms_bwd(dxn1 * g1_ref[...], h0h, r1)
        for b in range(n_blk):
            @pl.when(i == b)
            def _(b=b):
                out_copy(b).start()

        @pl.when(i == 0)
        def _():
            dmeta_ref[...] = buf[0, pl.ds(x0 - N_META, N_META), :]

        @pl.when(i == n_blk - 1)
        def _():
            for b in range(n_blk):
                out_copy(b).wait()

    blk = pl.BlockSpec((tm, d), lambda i: (i, 0))
    return _host_call(
        body, grid=(n_blk,),
        in_specs=[pl.BlockSpec((groups, tm, w), lambda i: (0, i, 0)), _resident(win_t.shape), blk, blk, _full((1, d))],
        out_specs=[ANY, _full((1, d)), _full((N_META, d))],
        out_shape=[jax.ShapeDtypeStruct((t_rows - x0, d), F32), jax.ShapeDtypeStruct((1, d), F32), jax.ShapeDtypeStruct((N_META, d), F32)],
        scratch_shapes=[pltpu.VMEM((n_blk, tm, d), F32), pltpu.SemaphoreType.DMA((n_blk,))],
        args=(dh5, win_t, h0, dh1, g1), name="bwd_in_proj", comm=comm)


def _pair_small(smalls, d):
    (dmeta, dg1, dg2, dg3, dg4, dbb, dlg, dlb, lossv, dwa, dwb) = smalls
    half = d // 2
    kb_rows = dwb.shape[0]

    def body(dmeta_ref, dg1_ref, dg2_ref, dg3_ref, dg4_ref, dbb_ref, dlg_ref, dlb_ref, loss_ref, dwa_ref, dwb_ref,
             sums_ref, pbuf, psib, ps_send, ps_recv):
        x, y, c = _mesh_pos()
        _pair_handshake()
        pbuf[...] = jnp.zeros_like(pbuf)
        pbuf[pl.ds(0, N_META), :] = dmeta_ref[...]
        for row, ref in ((16, dg1_ref), (17, dg2_ref), (18, dg3_ref), (19, dg4_ref)):
            pbuf[pl.ds(row, 1), :] = ref[...]
        pbuf[pl.ds(20, 1), pl.ds(0, half)] = dbb_ref[...]
        pbuf[pl.ds(20, 1), pl.ds(half, half)] = dlg_ref[...]
        pbuf[pl.ds(21, 1), pl.ds(0, half)] = dlb_ref[...]
        lv = loss_ref[...]
        pbuf[pl.ds(21, 1), pl.ds(half, half)] = lv[:, :half] + lv[:, half:]
        pbuf[pl.ds(24, SUBLANE), pl.ds(0, half)] = dwa_ref[...]
        pbuf[pl.ds(32, kb_rows), pl.ds(0, half)] = dwb_ref[...]
        to_sib = _remote(pbuf, psib, ps_send.at[0], ps_recv.at[0], (x, y, 1 - c))
        to_sib.start()
        to_sib.wait_recv()
        s = pbuf[...] + psib[...]
        for k in range(3):
            sums_ref[k] = s
        to_sib.wait_send()

    return pl.pallas_call(
        body, out_shape=jax.ShapeDtypeStruct((3, SMALL_ROWS, d), F32), in_specs=[VMEM] * 11, out_specs=VMEM,
        scratch_shapes=[pltpu.VMEM((SMALL_ROWS, d), F32), pltpu.VMEM((SMALL_ROWS, d), F32),
                        pltpu.SemaphoreType.DMA((1,)), pltpu.SemaphoreType.DMA((1,))],
        name="pair_small", compiler_params=pltpu.CompilerParams(vmem_limit_bytes=VMEM_LIMIT, collective_id=PAIR_BARRIER_ID))(*smalls)


def _total_small(own, others):
    _, r, d = own.shape
    half, cols = d // 2, d // N_DEV

    def body(own_ref, others_ref, tot_ref, meta_ref, g1_ref, g2_ref, g3_ref, g4_ref, dbb_ref, dlg_ref, dlb_ref, loss_ref, chip_p):
        x, y, c = _mesh_pos()
        chip_p[2 * x + y] = own_ref[0]
        for k, (cx, cy) in enumerate(_other_chips(x, y)):
            chip_p[2 * cx + cy] = others_ref[k]
        tot_ref[...] = ((chip_p[0] + chip_p[1]) + chip_p[2]) + chip_p[3]
        meta_ref[...] = tot_ref[pl.ds(0, N_META), pl.ds(pl.multiple_of(_dev_index(x, y, c) * cols, LANE), cols)]
        for row, ref in ((16, g1_ref), (17, g2_ref), (18, g3_ref), (19, g4_ref)):
            ref[...] = tot_ref[pl.ds(row, 1), :]
        dbb_ref[...] = tot_ref[pl.ds(20, 1), pl.ds(0, half)]
        dlg_ref[...] = tot_ref[pl.ds(20, 1), pl.ds(half, half)]
        dlb_ref[...] = tot_ref[pl.ds(21, 1), pl.ds(0, half)]
        loss_ref[...] = (0.5 / d) * jnp.sum(tot_ref[pl.ds(21, 1), pl.ds(half, half)], axis=-1, keepdims=True)

    row = lambda n: jax.ShapeDtypeStruct((1, n), F32)
    return pl.pallas_call(
        body, out_shape=[jax.ShapeDtypeStruct((r, d), F32), jax.ShapeDtypeStruct((N_META, cols), F32), row(d), row(d), row(d), row(d),
                         row(half), row(half), row(half), row(1)],
        scratch_shapes=[pltpu.VMEM((4, r, d), F32)], name="total_small", compiler_params=_cparams())(own, others)


def _adamw(w, g, m, v):
    m = ADAM_B1 * m + (1.0 - ADAM_B1) * g
    v = ADAM_B2 * v + (1.0 - ADAM_B2) * jnp.square(g)
    m_hat = m / (1.0 - ADAM_B1 ** ADAM_STEP)
    v_hat = v / (1.0 - ADAM_B2 ** ADAM_STEP)
    delta = -ADAM_LR * (m_hat / (jnp.sqrt(v_hat) + ADAM_EPS) + ADAM_WD * w)
    return delta, m, v


def _adam_big(g, pair, part, w, m, v, name, row0=0):
    r, d = w.shape
    cols = d // ADAM_COL_BLOCKS
    assert row0 % r == 0

    def body(me_ref, g_ref, pair_ref, part_ref, w_ref, m_ref, v_ref, go_ref, d_ref, mo_ref, vo_ref):
        g = g_ref[...].astype(F32) + pair_ref[...].astype(F32)
        for k in range(3):
            g = g + part_ref[k].astype(F32)
        go_ref[...] = g
        d_ref[...], mo_ref[...], vo_ref[...] = _adamw(w_ref[...], g, m_ref[...], v_ref[...])

    blk = pl.BlockSpec((r, cols), lambda i, me_ref: (0, i))
    grid_spec = pltpu.PrefetchScalarGridSpec(
        num_scalar_prefetch=1, grid=(ADAM_COL_BLOCKS,),
        in_specs=[pl.BlockSpec((r, cols), lambda i, me_ref: (me_ref[0], i)),
                  pl.BlockSpec((None, r, cols), lambda i, me_ref: (0, 0, i)),
                  pl.BlockSpec((3, r, cols), lambda i, me_ref: (0, 0, i)), blk, blk, blk],
        out_specs=[blk, blk, blk, blk])
    me = jnp.reshape(_dev_index(*_mesh_pos()) + row0 // r, (1,)).astype(jnp.int32)
    return pl.pallas_call(body, out_shape=[jax.ShapeDtypeStruct((r, d), F32)] * 4, grid_spec=grid_spec, name=name,
                          compiler_params=_cparams(1))(me, g, pair, part, w, m, v)


def _adam_small(gs, ws, ms, vs):
    n = len(gs)

    def body(*refs):
        ins, outs = refs[:4 * n], refs[4 * n:]
        for i in range(n):
            g = ins[i][...]
            delta, m, v = _adamw(ins[n + i][...], g, ins[2 * n + i][...], ins[3 * n + i][...])
            outs[i][...] = delta
            outs[n + i][...] = m
            outs[2 * n + i][...] = v

    shapes = [jax.ShapeDtypeStruct(w.shape, F32) for w in ws]
    return pl.pallas_call(body, out_shape=shapes * 3, name="adam_small", compiler_params=_cparams())(*gs, *ws, *ms, *vs)


def kernel(x, meta_tokens, pre_mix_norm, w_in, conv_a_w, conv_b_w, conv_b_bias, ln_b_gain, ln_b_bias, w_out, post_mix_norm, pre_ffn_norm, w_gate, w_up, w_down, post_ffn_norm, loss_target, m_meta_tokens, m_pre_mix_norm, m_w_in, m_conv_a_w, m_conv_b_w, m_conv_b_bias, m_ln_b_gain, m_ln_b_bias, m_w_out, m_post_mix_norm, m_pre_ffn_norm, m_w_gate, m_w_up, m_w_down, m_post_ffn_norm, v_meta_tokens, v_pre_mix_norm, v_w_in, v_conv_a_w, v_conv_b_w, v_conv_b_bias, v_ln_b_gain, v_ln_b_bias, v_w_out, v_post_mix_norm, v_pre_ffn_norm, v_w_gate, v_w_up, v_w_down, v_post_ffn_norm):
    _, seq, d = x.shape
    ka, ca_loc = conv_a_w.shape[1:]
    kb, cb_loc = conv_b_w.shape[1:]
    wa_w = ca_loc * N_DEV
    assert cb_loc == ca_loc and wa_w % LANE == 0 and w_in.shape[2] * N_DEV == 5 * wa_w and 2 * wa_w == d
    pad = (-(N_META + seq)) % ROW_ALIGN
    x0 = pad + N_META
    t_rows = x0 + seq
    assert t_rows % (N_ROW_BLOCKS * BF16_ROWS) == 0 and t_rows % CONV_CHUNK == 0 and d % LANE == 0
    tm = t_rows // N_ROW_BLOCKS
    tm2 = t_rows // 2
    me = _dev_index(*_mesh_pos())

    def as_rows(w_in_like, w_out_like, w_gate_like, w_up_like, w_down_like):
        return (w_in_like[0].T, w_out_like[0], w_gate_like[0].T, w_up_like[0].T, w_down_like[0])

    w_loc = as_rows(w_in, w_out, w_gate, w_up, w_down)
    rows = [w.shape[0] for w in w_loc]
    assert all(r % ADD_CHUNK == 0 for r in rows)
    P_IN, P_OUT, P_GATE, P_UP, P_DOWN = range(N_BIG)

    sm = jnp.zeros((SM_ROWS, LANE), F32)
    sm = sm.at[0:N_META, :].set(meta_tokens)
    sm = sm.at[16:16 + ka, 0:ca_loc].set(conv_a_w[0])
    sm = sm.at[24:24 + kb, 0:cb_loc].set(conv_b_w[0])
    wl, wfull, sm_all, h0, tgt = _gather_first(w_loc, sm, [(P_IN, 0, rows[P_IN])], x[0], loss_target[0], t_rows, x0)
    wa =jnp.transpose(sm_all[:, 16:16 + ka, 0:ca_loc], (1, 0, 2)).reshape(ka, wa_w)
    wb = jnp.transpose(sm_all[:, 24:24 + kb, 0:cb_loc], (1, 0, 2)).reshape(kb, wa_w)

    later = (P_OUT, P_GATE, P_UP, P_DOWN)
    sems, wl, started, _ = _gather_start(wl, wfull, later, rows, START_BARRIER_IDS[0])
    for p, arr in zip(later, started):
        wfull[p] = arr

    def arrived(p, after, name):
        nonlocal wl
        wl, wfull[p] = _gather_wait(wl, wfull[p], sems[later.index(p)], after, rows[p], name)
        return _forward_comm(wfull[p], rows[p])

    (xn1, hin), _ = _in_proj(h0, pre_mix_norm, wfull[P_IN], tm, None)
    (ya, z), (wfull[P_OUT],) = _mix_conv_fwd(hin, wa, wb, conv_b_bias, wa_w, arrived(P_OUT, hin, "gather_wait_out"))
    (y, mix, h1, xn2), (wfull[P_GATE],) = _out_proj(ya, z, ln_b_gain, ln_b_bias, wfull[P_OUT], h0, post_mix_norm, pre_ffn_norm, tm2,
                                                    arrived(P_GATE, z, "gather_wait_gate"))
    arrived(P_UP, xn2, "gather_wait_up")
    wfull[P_UP] = _forward_now(wfull[P_UP], rows[P_UP], "forward_up")
    (ga, gu, s), _ = _gate_up(xn2, wfull[P_GATE], wfull[P_UP], tm, None)
    arrived(P_DOWN, s, "gather_wait_down")
    wfull[P_DOWN] = _forward_now(wfull[P_DOWN], rows[P_DOWN], "forward_down")
    dh2, dff, dg4, lossv = _down_loss(s, wfull[P_DOWN], h1, tgt, post_ffn_norm, tm, x0)

    gwd = _wgrad(s, dff, "wgrad_down")
    dau, (pair_d,) = _bwd_down(dff, wfull[P_DOWN], ga, gu, tm, _pair_comm(gwd, rows[P_DOWN]))
    (flight_d,), token = _chip_start([_pair_sum(gwd, pair_d, rows[P_DOWN], "pair_sum_down")], "chip_start_down", START_BARRIER_IDS[1])
    gw_gu = _wgrad(dau, xn2, "wgrad_gate_up", [token])
    up0 = N_DEV * rows[P_GATE]
    (dh1, dg3), (pair_g, pair_u) = _bwd_ffn_in(dau, wfull[P_GATE], wfull[P_UP], h1, dh2, pre_ffn_norm, tm,
                                               _merge_comms([_pair_comm(gw_gu, rows[P_GATE]), _pair_comm(gw_gu, rows[P_UP], up0)]))
    (flight_g, flight_u), token = _chip_start([_pair_sum(gw_gu, pair_g, rows[P_GATE], "pair_sum_gate"),
                                               _pair_sum(gw_gu, pair_u, rows[P_UP], "pair_sum_up", row0=up0)], "chip_start_gate_up",
                                              START_BARRIER_IDS[2])
    dmix, dya, dz, dg2, dlg, dlb, dbb = _bwd_out_proj(dh1, mix, wfull[P_OUT], post_mix_norm, z, ln_b_gain, ln_b_bias, tm, [token])
    gwo = _wgrad(y, dmix, "wgrad_out")
    (dh5, dwa, dwb), (pair_o,) = _mix_conv_bwd(hin, dya, dz, wa, wb, wa_w, _pair_comm(gwo, rows[P_OUT]))
    (flight_o,), token = _chip_start([_pair_sum(gwo, pair_o, rows[P_OUT], "pair_sum_out")], "chip_start_out", START_BARRIER_IDS[3])
    gwi = _wgrad(dh5, xn1, "wgrad_in", [token])
    (dx, dg1, dmeta), (pair_i,) = _bwd_in_proj(dh5, wfull[P_IN], h0, dh1, pre_mix_norm, tm, x0, _pair_comm(gwi, rows[P_IN]))
    grad_x = dx[None]
    small_sums = _pair_small((dmeta, dg1, dg2, dg3, dg4, dbb, dlg, dlb, lossv, dwa, dwb), d)
    (flight_s,), token = _chip_start([small_sums], "chip_start_small", START_BARRIER_IDS[4])
    (flight_i,), token = _chip_start([_pair_sum(gwi, pair_i, rows[P_IN], "pair_sum_in", [token])], "chip_start_in",
                                     START_BARRIER_IDS[5])

    def landed(flight, after, tag):
        sems_p, sums, land = flight
        return _chip_wait(sums, land, sems_p, after, "chip_wait_" + tag)

    m_loc = as_rows(m_w_in, m_w_out, m_w_gate, m_w_up, m_w_down)
    v_loc = as_rows(v_w_in, v_w_out, v_w_gate, v_w_up, v_w_down)
    full_grads = {P_IN: gwi, P_OUT: gwo, P_GATE: gw_gu, P_UP: gw_gu, P_DOWN: gwd}
    pairs = {P_IN: pair_i, P_OUT: pair_o, P_GATE: pair_g, P_UP: pair_u, P_DOWN: pair_d}
    flights = {P_IN: flight_i, P_OUT: flight_o, P_GATE: flight_g, P_UP: flight_u, P_DOWN: flight_d}
    names = {P_IN: "w_in", P_OUT: "w_out", P_GATE: "w_gate", P_UP: "w_up", P_DOWN: "w_down"}
    bigs = {}

    def adam_big(p, after):
        part = landed(flights[p], after, names[p])
        res = _adam_big(full_grads[p], pairs[p], part, w_loc[p], m_loc[p], v_loc[p], "adam_" + names[p], up0 if p == P_UP else 0)
        bigs[names[p]] = [(o.T if p in (P_IN, P_GATE, P_UP) else o)[None] for o in res]
        return res[1]

    for p in (P_DOWN, P_GATE, P_UP, P_OUT):
        token = adam_big(p, token)
    (ptot, g_meta, g_pre_mix, g_post_mix, g_pre_ffn, g_post_ffn, g_conv_bias, g_ln_gain, g_ln_bias, loss11) = _total_small(
        flight_s[1], landed(flight_s, token, "small"))
    loss = jnp.reshape(loss11, ())
    taps_major = lambda a: jnp.transpose(a, (1, 0, 2))
    CONV = (2, 3)
    g_small = [g_meta, g_pre_mix, lax.dynamic_slice(ptot, (24, me * ca_loc), (ka, ca_loc))[:, None, :],
               lax.dynamic_slice(ptot, (32, me * cb_loc), (kb, cb_loc))[:, None, :],
               g_conv_bias, g_ln_gain, g_ln_bias, g_post_mix, g_pre_ffn, g_post_ffn]
    w_small = [meta_tokens, pre_mix_norm, taps_major(conv_a_w), taps_major(conv_b_w), conv_b_bias, ln_b_gain, ln_b_bias,
               post_mix_norm, pre_ffn_norm, post_ffn_norm]
    m_small = [m_meta_tokens, m_pre_mix_norm, taps_major(m_conv_a_w), taps_major(m_conv_b_w), m_conv_b_bias, m_ln_b_gain,
               m_ln_b_bias, m_post_mix_norm, m_pre_ffn_norm, m_post_ffn_norm]
    v_small = [v_meta_tokens, v_pre_mix_norm, taps_major(v_conv_a_w), taps_major(v_conv_b_w), v_conv_b_bias, v_ln_b_gain,
               v_ln_b_bias, v_post_mix_norm, v_pre_ffn_norm, v_post_ffn_norm]
    small = list(_adam_small(g_small, w_small, m_small, v_small))
    n_small = len(w_small)
    for i in CONV:
        g_small[i] = taps_major(g_small[i])
        for k in range(3):
            small[k * n_small + i] = taps_major(small[k * n_small + i])
    d_small, nm_small, nv_small = small[:n_small], small[n_small:2 * n_small], small[2 * n_small:]

    adam_big(P_IN, small[0])

    def ordered(pick_small, pick_big):
        sm_it = iter(range(n_small))
        out = []
        for name in ("s", "s", "w_in", "s", "s", "s", "s", "s", "w_out", "s", "s", "w_gate", "w_up", "w_down", "s"):
            out.append(pick_small(next(sm_it)) if name == "s" else pick_big(name))
        return out

    grads = ordered(lambda i: g_small[i], lambda n: bigs[n][0])
    deltas = ordered(lambda i: d_small[i], lambda n: bigs[n][1])
    new_m = ordered(lambda i: nm_small[i], lambda n: bigs[n][2])
    new_v = ordered(lambda i: nv_small[i], lambda n: bigs[n][3])
    return (loss, grad_x, *grads, *deltas, *new_m, *new_v)
```

```python
import jax
import jax.numpy as jnp
from jax import lax
from jax.experimental import pallas as pl
from jax.experimental.pallas import tpu as pltpu

F32 = jnp.float32
BF16 = jnp.bfloat16
MESH = pl.DeviceIdType.MESH

N_META = 16
N_DEV = 8
RMS_EPS = 1e-6
LN_EPS = 1e-5
ADAM_LR = 0.001
ADAM_B1 = 0.9
ADAM_B2 = 0.999
ADAM_EPS = 1e-08
ADAM_WD = 0.01
ADAM_STEP = 10

LANE = 128
SUBLANE = 8
BF16_ROWS = 16
ROW_ALIGN = 128
N_ROW_BLOCKS = 4
CONV_HALO = 32
CONV_CHUNK = 64
WGRAD_ROWS = 32
N_CHUNK = 512
WGRAD_TILE_MAX = 1408
ADD_CHUNK = 32
ADAM_COL_BLOCKS = 2
COPY_PIECES = 4
V7X_VMEM_BYTES = 64 * 1024 * 1024
VMEM_LIMIT = V7X_VMEM_BYTES - 6 * 1024 * 1024
SMALL_ROWS = 64
SM_ROWS = 56
N_BIG = 5

ANY = pl.BlockSpec(memory_space=pl.ANY)
VMEM = pl.BlockSpec(memory_space=pltpu.VMEM)


def _cparams(n_grid_axes=0):
    sem = ("arbitrary",) * n_grid_axes if n_grid_axes else None
    return pltpu.CompilerParams(dimension_semantics=sem, vmem_limit_bytes=VMEM_LIMIT)


def _mesh_pos():
    return lax.axis_index("x"), lax.axis_index("y"), lax.axis_index("c")


def _dev_index(px, py, pc):
    return 4 * px + 2 * py + pc


def _other_chips(x, y):
    return [(1 - x, y), (x, 1 - y), (1 - x, 1 - y)]


def _full(shape):
    return pl.BlockSpec(shape, lambda *_: (0,) * len(shape))


def _resident(shape):
    return pl.BlockSpec(shape, lambda *_: (0,) * len(shape), pipeline_mode=pl.Buffered(1))


def _dot_nt(a, w):
    return lax.dot_general(a, w, (((1,), (1,)), ((), ())), preferred_element_type=F32)


def _dot_nn(a, w):
    return jnp.dot(a, w, preferred_element_type=F32)


def _chunks(n, c):
    out, o = [], 0
    while o < n:
        out.append((o, min(c, n - o)))
        o += c
    return out


def _rstd(h):
    return lax.rsqrt(jnp.mean(h * h, axis=-1, keepdims=True) + RMS_EPS)


def _rms_bwd(dyh, yh, r):
    return r * (dyh - yh * jnp.mean(dyh * yh, axis=-1, keepdims=True))


def _silu_grad(a, sig):
    return sig * (1.0 + a * (1.0 - sig))


def _acc_rows(ref, val, first):
    s = jnp.sum(val, axis=0, keepdims=True)

    @pl.when(first)
    def _():
        ref[...] = s

    @pl.when(jnp.logical_not(first))
    def _():
        ref[...] += s


def _row_loop(t_rows, chunk, fn, carry=None):
    def step(i, c):
        return fn(pl.multiple_of(i * chunk, chunk), c)

    return lax.fori_loop(0, t_rows // chunk, step, carry)


def _remote(src, dst, send_sem, recv_sem, to):
    return pltpu.make_async_remote_copy(src_ref=src, dst_ref=dst, send_sem=send_sem, recv_sem=recv_sem,
                                        device_id=to, device_id_type=MESH)


class _Comm:
    def __init__(self, inputs, out_shapes, aliases, scratch, start, finish):
        self.inputs, self.out_shapes, self.aliases, self.scratch = list(inputs), list(out_shapes), dict(aliases), list(scratch)
        self.start, self.finish = start, finish


def _merge_comms(comms):
    inputs, out_shapes, aliases, scratch, spans = [], [], {}, [], []
    for cm in comms:
        spans.append((len(inputs), len(out_shapes), len(scratch), cm))
        aliases.update({len(inputs) + k: len(out_shapes) + v for k, v in cm.aliases.items()})
        inputs += cm.inputs
        out_shapes += cm.out_shapes
        scratch += cm.scratch

    def run(which):
        def fn(ins, outs, scr):
            for i0, o0, s0, cm in spans:
                getattr(cm, which)(ins[i0:i0 + len(cm.inputs)], outs[o0:o0 + len(cm.out_shapes)], scr[s0:s0 + len(cm.scratch)])
        return fn

    return _Comm(inputs, out_shapes, aliases, scratch, run("start"), run("finish"))


def _host_call(body, *, grid, in_specs, out_specs, out_shape, args, name, scratch_shapes=(), comm=None, after=()):
    talks = comm is not None
    if comm is None:
        comm = _Comm([], [], {}, [], lambda *_: None, lambda *_: None)
    n_in, n_out, n_scr = len(args), len(out_shape), len(scratch_shapes)
    c_in, c_out = len(comm.inputs), len(comm.out_shapes)
    n_after = len(after)

    def open_comm(c_ins, c_outs, c_scr):
        if talks:
            _pair_handshake()
        comm.start(c_ins, c_outs, c_scr)

    def hosted(*refs):
        ins, c_ins = refs[:n_in], refs[n_in:n_in + c_in]
        o0 = n_in + c_in + n_after
        outs, c_outs = refs[o0:o0 + n_out], refs[o0 + n_out:o0 + n_out + c_out]
        s0 = o0 + n_out + c_out
        scr, c_scr = refs[s0:s0 + n_scr], refs[s0 + n_scr:]
        if not grid:
            open_comm(c_ins, c_outs, c_scr)
            body(*ins, *outs, *scr)
            comm.finish(c_ins, c_outs, c_scr)
            return
        first = last = None
        for a, n in enumerate(grid):
            f, l = pl.program_id(a) == 0, pl.program_id(a) == n - 1
            first = f if first is None else jnp.logical_and(first, f)
            last = l if last is None else jnp.logical_and(last, l)

        @pl.when(first)
        def _():
            open_comm(c_ins, c_outs, c_scr)

        body(*ins, *outs, *scr)

        @pl.when(last)
        def _():
            comm.finish(c_ins, c_outs, c_scr)

    sem = ("arbitrary",) * len(grid) if grid else None
    params = pltpu.CompilerParams(dimension_semantics=sem, vmem_limit_bytes=VMEM_LIMIT,
                                  collective_id=PAIR_BARRIER_ID if talks else None)
    res = pl.pallas_call(
        hosted, grid=grid, in_specs=list(in_specs) + [ANY] * (c_in + n_after), out_specs=list(out_specs) + [ANY] * c_out,
        out_shape=list(out_shape) + comm.out_shapes, scratch_shapes=list(scratch_shapes) + comm.scratch,
        input_output_aliases={n_in + k: n_out + v for k, v in comm.aliases.items()},
        name=name, compiler_params=params)(*args, *comm.inputs, *after)
    return list(res[:n_out]), list(res[n_out:])


PAIR_BARRIER_ID = 0
START_BARRIER_IDS = (1, 2, 3, 4, 5, 6)
ALL_PEERS_BARRIER_ID = 7


def _chips_handshake():
    x, y, c = _mesh_pos()
    barrier = pltpu.get_barrier_semaphore()
    for chip in _other_chips(x, y):
        pl.semaphore_signal(barrier, inc=1, device_id=(*chip, c), device_id_type=MESH)
    pl.semaphore_wait(barrier, 3)


def _pair_handshake():
    x, y, c = _mesh_pos()
    barrier = pltpu.get_barrier_semaphore()
    pl.semaphore_signal(barrier, inc=1, device_id=(x, y, 1 - c), device_id_type=MESH)
    pl.semaphore_wait(barrier, 1)


GATHER_SEMS = 10


class _Gather:
    def __init__(self, jobs, rows, lo, src_ref, dests, send_sems, recv_sems):
        x, y, c = _mesh_pos()
        me, sib = (x, y, c), (x, y, 1 - c)
        nx, ny, dg = (1 - x, y, c), (x, 1 - y, c), (1 - x, 1 - y, c)
        self.relayed, self.direct, self.relay, self.to_sib, self.sib_fwd = [], [], [], [], []
        for n, (p, r0, nr) in enumerate(jobs):
            assert nr % (2 * BF16_ROWS) == 0
            half = nr // 2

            def rows_of(dev, h, p=p, r0=r0, nr=nr, half=half):
                off, cnt = (r0, nr) if h is None else (r0 + h * half, half)
                return dests[p].at[pl.ds(pl.multiple_of(_dev_index(*dev) * rows[p] + off, BF16_ROWS), cnt), :]

            def mine(h, p=p, r0=r0, nr=nr, half=half):
                off, cnt = (r0, nr) if h is None else (r0 + h * half, half)
                return src_ref.at[pl.ds(lo[p] + off, cnt), :]

            sem = lambda k, n=n: (send_sems.at[GATHER_SEMS * n + k], recv_sems.at[GATHER_SEMS * n + k])
            self.relayed.append([_remote(mine(0), rows_of(me, 0), *sem(0), nx), _remote(mine(1), rows_of(me, 1), *sem(3), ny)])
            self.direct.append([_remote(mine(1), rows_of(me, 1), *sem(1), nx), _remote(mine(0), rows_of(me, 0), *sem(2), ny)])
            self.relay.append([_remote(rows_of(nx, 0), rows_of(nx, 0), *sem(4), ny), _remote(rows_of(ny, 1), rows_of(ny, 1), *sem(5), nx)])
            self.to_sib.append(_remote(mine(None), rows_of(me, None), *sem(6), sib))
            self.sib_fwd.append([_remote(rows_of(dev, None), rows_of(dev, None), *sem(7 + i), sib) for i, dev in enumerate((nx, ny, dg))])

    def start(self):
        for group in (self.relayed, self.direct):
            for cps in group:
                for cp in cps:
                    cp.start()
        for cp in self.to_sib:
            cp.start()

    def mid(self):
        for first, relay in zip(self.relayed, self.relay):
            for arrived, onward in zip(first, relay):
                arrived.wait_recv()
                onward.start()

    def finish(self):
        for direct, relay, fwd in zip(self.direct, self.relay, self.sib_fwd):
            for k in range(2):
                direct[k].wait_recv()
                fwd[k].start()
            for cp in relay:
                cp.wait_recv()
            fwd[2].start()
        for n in range(len(self.to_sib)):
            self.to_sib[n].wait_recv()
            for cp in self.sib_fwd[n]:
                cp.wait_recv()
            for cp in self.relayed[n] + self.direct[n] + self.relay[n] + [self.to_sib[n]] + self.sib_fwd[n]:
                cp.wait_send()


HBM = pl.BlockSpec(memory_space=pltpu.HBM)
SEM = pl.BlockSpec(memory_space=pltpu.SEMAPHORE)
FLOWS = pltpu.SideEffectType.DATAFLOW_SIDE_EFFECTING


def _in_hbm(a):
    return pltpu.with_memory_space_constraint(a, pltpu.HBM)


def _gather_start(wl, dests, ps, rows, barrier_id):
    lo = [sum(rows[:p]) for p in range(N_BIG)]
    n = len(ps)

    def body(*refs):
        wl_ref, dest_refs = refs[0], refs[1:1 + n]
        sends, recvs = refs[1 + n:1 + 2 * n], refs[1 + 2 * n:1 + 3 * n]
        token = refs[-1]
        _chips_handshake()
        x, y, c = _mesh_pos()
        jme = _dev_index(x, y, c)
        for i, p in enumerate(ps):
            mine = dest_refs[i].at[pl.ds(pl.multiple_of(jme * rows[p], BF16_ROWS), rows[p]), :]
            for chip in _other_chips(x, y):
                _remote(wl_ref.at[pl.ds(lo[p], rows[p]), :], mine, sends[i], recvs[i], (*chip, c)).start()
        token[...] = jnp.zeros_like(token)

    thru = [pltpu.HBM(wl.shape, wl.dtype)] + [pltpu.HBM(dests[p].shape, BF16) for p in ps]
    res = pl.pallas_call(
        body, name="gather_start",
        out_shape=tuple([pltpu.SemaphoreType.DMA(())] * (2 * n) + thru + [jax.ShapeDtypeStruct((SUBLANE, LANE), F32)]),
        in_specs=[HBM] * (1 + n), out_specs=tuple([SEM] * (2 * n) + [HBM] * (1 + n) + [VMEM]),
        input_output_aliases={i: 2 * n + i for i in range(1 + n)},
        compiler_params=pltpu.CompilerParams(has_side_effects=FLOWS, collective_id=barrier_id))(
            _in_hbm(wl), *[_in_hbm(dests[p]) for p in ps])
    sems = [(res[i], res[n + i]) for i in range(n)]
    return sems, res[2 * n], list(res[2 * n + 1:3 * n + 1]), res[-1]


def _gather_wait(wl, dest, sems, after, r, name):
    def body(wl_ref, dest_ref, send_sem, recv_sem, after_ref, wl_out, dest_out):
        x, y, c = _mesh_pos()
        three = dest_ref.at[pl.ds(0, 3 * r), :]
        cp = _remote(three, three, send_sem, recv_sem, (x, y, 1 - c))
        cp.wait_send()
        cp.wait_recv()

    res = pl.pallas_call(
        body, name=name, out_shape=(pltpu.HBM(wl.shape, wl.dtype), pltpu.HBM(dest.shape, dest.dtype)),
        in_specs=[HBM, HBM, SEM, SEM, ANY], out_specs=(HBM, HBM), input_output_aliases={0: 0, 1: 1},
        compiler_params=pltpu.CompilerParams(has_side_effects=FLOWS))(wl, dest, sems[0], sems[1], after)
    return res[0], res[1]


def _forward_comm(dest, r):
    def descs(ins, outs, scr):
        x, y, c = _mesh_pos()
        cps = []
        for k, chip in enumerate(_other_chips(x, y)):
            blk = outs[0].at[pl.ds(pl.multiple_of(_dev_index(*chip, c) * r, BF16_ROWS), r), :]
            cps.append(_remote(blk, blk, scr[0].at[k], scr[1].at[k], (x, y, 1 - c)))
        return cps

    def start(ins, outs, scr):
        for cp in descs(ins, outs, scr):
            cp.start()

    def finish(ins, outs, scr):
        cps = descs(ins, outs, scr)
        for cp in cps:
            cp.wait_recv()
        for cp in cps:
            cp.wait_send()

    return _Comm([dest], [jax.ShapeDtypeStruct(dest.shape, dest.dtype)], {0: 0},
                 [pltpu.SemaphoreType.DMA((3,)), pltpu.SemaphoreType.DMA((3,))], start, finish)


def _forward_now(dest, r, name):
    _, (dest,) = _host_call(lambda: None, grid=(), in_specs=[], out_specs=[], out_shape=[], args=(), name=name,
                            comm=_forward_comm(dest, r))
    return dest


def _pair_comm(g, r, row0=0):
    d = g.shape[1]

    def descs(ins, outs, scr):
        x, y, c = _mesh_pos()
        chips = [(x, y)] + _other_chips(x, y)
        return [_remote(ins[0].at[pl.ds(pl.multiple_of(row0 + _dev_index(*chip, 1 - c) * r, BF16_ROWS), r), :], outs[0].at[k],
                        scr[0].at[k], scr[1].at[k], (x, y, 1 - c)) for k, chip in enumerate(chips)]

    def start(ins, outs, scr):
        for cp in descs(ins, outs, scr):
            cp.start()

    def finish(ins, outs, scr):
        cps = descs(ins, outs, scr)
        for cp in cps:
            cp.wait_recv()
        for cp in cps:
            cp.wait_send()

    comm = _Comm([g], [jax.ShapeDtypeStruct((4, r, d), BF16)], {},
                 [pltpu.SemaphoreType.DMA((4,)), pltpu.SemaphoreType.DMA((4,))], start, finish)
    return comm


def _pair_sum(g, pair, r, name, after=(), row0=0):
    d = g.shape[1]

    def body(g_ref, p_ref, *rest):
        o_ref, gbuf, pbuf, sems = rest[len(after):]
        x, y, c = _mesh_pos()
        loads = [pltpu.make_async_copy(p_ref.at[pl.ds(1, 3)], pbuf, sems.at[3])]
        for k, chip in enumerate(_other_chips(x, y)):
            j = _dev_index(*chip, c)
            loads.append(pltpu.make_async_copy(g_ref.at[pl.ds(pl.multiple_of(row0 + j * r, BF16_ROWS), r), :], gbuf.at[k], sems.at[k]))
        for cp in loads:
            cp.start()
        for cp in loads:
            cp.wait()
        for k in range(3):
            o_ref[k] = (gbuf[k].astype(F32) + pbuf[k].astype(F32)).astype(BF16)

    return pl.pallas_call(
        body, out_shape=jax.ShapeDtypeStruct((3, r, d), BF16), in_specs=[ANY] * (2 + len(after)), out_specs=VMEM,
        scratch_shapes=[pltpu.VMEM((3, r, d), BF16), pltpu.VMEM((3, r, d), BF16), pltpu.SemaphoreType.DMA((4,))],
        name=name, compiler_params=_cparams())(g, pair, *after)


def _chip_start(sums, name, barrier_id):
    n = len(sums)

    def body(*refs):
        srcs, lands = refs[:n], refs[n:2 * n]
        sends, recvs = refs[2 * n:3 * n], refs[3 * n:4 * n]
        _chips_handshake()
        x, y, c = _mesh_pos()
        for i in range(n):
            for k, chip in enumerate(_other_chips(x, y)):
                _remote(srcs[i].at[k], lands[i].at[k], sends[i], recvs[i], (*chip, c)).start()
        refs[-1][...] = jnp.zeros_like(refs[-1])

    zones = [pltpu.HBM(s.shape, s.dtype) for s in sums]
    res = pl.pallas_call(
        body, name=name,
        out_shape=tuple([pltpu.SemaphoreType.DMA(())] * (2 * n) + zones + zones + [jax.ShapeDtypeStruct((SUBLANE, LANE), F32)]),
        in_specs=[HBM] * (2 * n), out_specs=tuple([SEM] * (2 * n) + [HBM] * (2 * n) + [VMEM]),
        input_output_aliases={i: 2 * n + i for i in range(2 * n)},
        compiler_params=pltpu.CompilerParams(has_side_effects=FLOWS, collective_id=barrier_id))(
            *[_in_hbm(s) for s in sums], *[_in_hbm(lax.empty(s.shape, s.dtype)) for s in sums])
    flights = [((res[i], res[n + i]), res[2 * n + i], res[3 * n + i]) for i in range(n)]
    return flights, res[-1]


def _chip_wait(sums, land, sems, after, name):
    def body(sums_ref, land_ref, send_sem, recv_sem, after_ref, sums_out, land_out):
        x, y, c = _mesh_pos()
        cp = _remote(sums_ref, land_ref, send_sem, recv_sem, (x, y, 1 - c))
        cp.wait_send()
        cp.wait_recv()

    res = pl.pallas_call(
        body, name=name, out_shape=(pltpu.HBM(sums.shape, sums.dtype), pltpu.HBM(land.shape, land.dtype)),
        in_specs=[HBM, HBM, SEM, SEM, ANY], out_specs=(HBM, HBM), input_output_aliases={0: 0, 1: 1},
        compiler_params=pltpu.CompilerParams(has_side_effects=FLOWS))(sums, land, sems[0], sems[1], after)
    return res[0], res[1]


class _CopyThrough:
    def __init__(self, src_ref, dst_ref, dst_row0, n_rows, buf, sem_in, sem_out):
        rc = n_rows // COPY_PIECES
        piece = lambda ref, o: ref.at[pl.ds(o, rc), :]
        self.loads = [pltpu.make_async_copy(piece(src_ref, k * rc), piece(buf, k * rc), sem_in) for k in range(COPY_PIECES)]
        self.stores = [pltpu.make_async_copy(piece(buf, k * rc), piece(dst_ref, dst_row0 + k * rc), sem_out) for k in range(COPY_PIECES)]
        self.all_in = pltpu.make_async_copy(src_ref, buf, sem_in)
        self.all_out = pltpu.make_async_copy(buf, dst_ref.at[pl.ds(dst_row0, n_rows), :], sem_out)

    def load(self):
        for cp in self.loads:
            cp.start()

    def store(self):
        self.all_in.wait()
        for cp in self.stores:
            cp.start()

    def done(self):
        self.all_out.wait()


def _gather_first(shards, sm, jobs, x2, tgt2, t_rows, x0):
    d = shards[0].shape[1]
    rows = [w.shape[0] for w in shards]
    lo = [sum(rows[:p]) for p in range(N_BIG)]
    n_sems = GATHER_SEMS * len(jobs)
    seq = x2.shape[0]
    assert x0 == ROW_ALIGN and seq % ROW_ALIGN == 0 and d == N_DEV * LANE

    def body(s0, s1, s2, s3, s4, sm_ref, x_ref, tgt_ref, wl_ref, o0, o1, o2, o3, o4, sa_ref, h0_ref, tp_ref,
             wl_v, x_v, tgt_v, heads_v, sa_v, send_sems, recv_sems, ssend, srecv, local_sems, sems_in, sems_out):
        dests = (o0, o1, o2, o3, o4)
        x, y, c = _mesh_pos()
        me = (x, y, c)
        jme = _dev_index(*me)
        peers = [(x, y, 1 - c)] + [(*chip, pc) for pc in (c, 1 - c) for chip in _other_chips(x, y)]
        barrier = pltpu.get_barrier_semaphore()
        for to in peers:
            pl.semaphore_signal(barrier, inc=1, device_id=to, device_id_type=MESH)
        pl.semaphore_wait(barrier, len(peers))
        padded = [_CopyThrough(x_ref, h0_ref, x0, seq, x_v, sems_in.at[0], sems_out.at[0]),
                  _CopyThrough(tgt_ref, tp_ref, x0, seq, tgt_v, sems_in.at[1], sems_out.at[1])]
        for cp in padded:
            cp.load()
        shard_refs = (s0, s1, s2, s3, s4)
        first = sorted({j[0] for j in jobs})
        for p in first + [p for p in range(N_BIG) if p not in first]:
            wl_v[pl.ds(lo[p], rows[p]), :] = shard_refs[p][...].astype(BF16)
            if p == first[-1]:
                gather = _Gather(jobs, rows, lo, wl_v, dict(enumerate(dests)), send_sems, recv_sems)
                gather.start()
        smalls = [_remote(sm_ref, sa_ref.at[jme], ssend.at[k], srecv.at[k], to) for k, to in enumerate(peers)]
        for cp in smalls:
            cp.start()
        mine = [pltpu.make_async_copy(wl_v.at[pl.ds(lo[p], rows[p]), :],
                                      dests[p].at[pl.ds(pl.multiple_of(jme * rows[p], BF16_ROWS), rows[p]), :], local_sems.at[p])
                for p in range(N_BIG)]
        mine.append(pltpu.make_async_copy(wl_v, wl_ref, local_sems.at[N_BIG]))
        mine.append(pltpu.make_async_copy(sm_ref, sa_ref.at[jme], local_sems.at[N_BIG + 1]))
        for cp in mine:
            cp.start()
        later = [p for p in range(N_BIG) if p not in {j[0] for j in jobs}]
        own = [_remote(wl_v.at[pl.ds(lo[p], rows[p]), :], dests[p].at[pl.ds(pl.multiple_of(jme * rows[p], BF16_ROWS), rows[p]), :],
                       ssend.at[7 + i], srecv.at[7 + i], (x, y, 1 - c)) for i, p in enumerate(later)]
        for cp in own:
            cp.start()
        gather.mid()
        for cp in padded:
            cp.store()
        for cp in smalls + own:
            cp.wait_recv()
        mine[-1].wait()
        to_v = pltpu.make_async_copy(sa_ref, sa_v, local_sems.at[N_BIG + 1])
        to_v.start()
        to_v.wait()
        head, zeros = heads_v.at[0], heads_v.at[1]
        head[...] = jnp.zeros_like(head)
        zeros[...] = jnp.zeros_like(zeros)
        for j in range(N_DEV):
            head[pl.ds(x0 - N_META, N_META), pl.ds(j * LANE, LANE)] = sa_v[j, pl.ds(0, N_META), :]
        heads = [pltpu.make_async_copy(head, h0_ref.at[pl.ds(0, x0), :], local_sems.at[N_BIG + 1]),
                 pltpu.make_async_copy(zeros, tp_ref.at[pl.ds(0, x0), :], local_sems.at[N_BIG + 2])]
        for cp in heads:
            cp.start()
        gather.finish()
        for cp in smalls + own:
            cp.wait_send()
        for cp in mine[:-1] + heads:
            cp.wait()
        for cp in padded:
            cp.done()

    out_shape = [jax.ShapeDtypeStruct((sum(rows), d), BF16)]
    out_shape += [jax.ShapeDtypeStruct((N_DEV * r, d), BF16) for r in rows]
    out_shape.append(jax.ShapeDtypeStruct((N_DEV,) + sm.shape, F32))
    out_shape += [jax.ShapeDtypeStruct((t_rows, d), F32)] * 2
    res = pl.pallas_call(
        body, out_shape=out_shape, in_specs=[VMEM] * 6 + [ANY] * 2, out_specs=[ANY] * 9,
        scratch_shapes=[pltpu.VMEM((sum(rows), d), BF16), pltpu.VMEM((seq, d), F32), pltpu.VMEM((seq, d), F32),
                        pltpu.VMEM((2, ROW_ALIGN, d), F32), pltpu.VMEM((N_DEV,) + sm.shape, F32),
                        pltpu.SemaphoreType.DMA((n_sems,)), pltpu.SemaphoreType.DMA((n_sems,)),
                        pltpu.SemaphoreType.DMA((7 + N_BIG,)), pltpu.SemaphoreType.DMA((7 + N_BIG,)),
                        pltpu.SemaphoreType.DMA((N_BIG + 3,)), pltpu.SemaphoreType.DMA((2,)), pltpu.SemaphoreType.DMA((2,))],
        name="gather_first",
        compiler_params=pltpu.CompilerParams(vmem_limit_bytes=VMEM_LIMIT, collective_id=ALL_PEERS_BARRIER_ID))(*shards, sm, x2, tgt2)
    return res[0], list(res[1:1 + N_BIG]), res[1 + N_BIG], res[2 + N_BIG], res[3 + N_BIG]


def _in_proj(h0, g1, win_t, tm, comm):
    t_rows, d = h0.shape
    e = win_t.shape[0]

    def body(h_ref, g_ref, w_ref, xn_ref, hin_ref):
        h = h_ref[...]
        xn = ((h * _rstd(h)) * g_ref[...]).astype(BF16)
        xn_ref[...] = xn
        for o, n in _chunks(e, N_CHUNK):
            hin_ref[:, pl.ds(o, n)] = _dot_nt(xn, w_ref[pl.ds(o, n), :])

    return _host_call(
        body, grid=(t_rows // tm,),
        in_specs=[pl.BlockSpec((tm, d), lambda i: (i, 0)), _full((1, d)), _resident((e, d))],
        out_specs=[pl.BlockSpec((tm, d), lambda i: (i, 0)), pl.BlockSpec((tm, e), lambda i: (i, 0))],
        out_shape=[jax.ShapeDtypeStruct((t_rows, d), BF16), jax.ShapeDtypeStruct((t_rows, e), F32)],
        args=(h0, g1, win_t), name="in_proj", comm=comm)


def _tap_slot(off):
    return off % SUBLANE, (off // SUBLANE) * SUBLANE


def _fill_shifted(sh_ref, base_ref, residues, n_rows):
    for r in residues:
        if r:
            sh_ref[r] = base_ref[pl.ds(r, n_rows), :]


def _shifted_rows(pair, r, start, n):
    base_ref, sh_ref = pair
    return base_ref[pl.ds(start, n), :] if r == 0 else sh_ref[r, pl.ds(start, n), :]


def _mix_conv_fwd(hin, wa, wb, bb, wa_w, comm):
    t_rows = hin.shape[0]
    nt = wa_w // LANE
    ka, kb = wa.shape[0], wb.shape[0]
    nr = CONV_HALO + t_rows

    def body(bg_ref, cg_ref, ha_ref, val_ref, gt_ref, wa_ref, wb_ref, bb_ref, ya_ref, z_ref, base, sh):
        base[pl.ds(0, CONV_HALO), :] = jnp.zeros((CONV_HALO, LANE), F32)
        base[pl.ds(nr, SUBLANE), :] = jnp.zeros((SUBLANE, LANE), F32)

        def conv(w_ref, k_taps, b, n):
            acc = None
            for k in range(k_taps):
                r, q = _tap_slot(CONV_HALO - (k_taps - 1) + k)
                term = w_ref[pl.ds(k, 1), :] * _shifted_rows((base, sh), r, b + q, n)
                acc = term if acc is None else acc + term
            return acc

        def fill_a(b, c):
            base[pl.ds(CONV_HALO + b, CONV_CHUNK), :] = cg_ref[pl.ds(b, CONV_CHUNK), :] * ha_ref[pl.ds(b, CONV_CHUNK), :]
            return c

        _row_loop(t_rows, CONV_CHUNK, fill_a)
        _fill_shifted(sh, base, sorted({_tap_slot(CONV_HALO - (ka - 1) + k)[0] for k in range(ka)}), nr)

        def out_a(b, c):
            ya_ref[pl.ds(b, CONV_CHUNK), :] = (bg_ref[pl.ds(b, CONV_CHUNK), :] * conv(wa_ref, ka, b, CONV_CHUNK)).astype(BF16)
            return c

        _row_loop(t_rows, CONV_CHUNK, out_a)

        def fill_b(b, c):
            base[pl.ds(CONV_HALO + b, CONV_CHUNK), :] = (val_ref[pl.ds(b, CONV_CHUNK), :]
                                                          * jax.nn.sigmoid(gt_ref[pl.ds(b, CONV_CHUNK), :]))
            return c

        _row_loop(t_rows, CONV_CHUNK, fill_b)
        _fill_shifted(sh, base, range(SUBLANE), nr)

        def out_b(b, c):
            z_ref[pl.ds(b, CONV_CHUNK), :] = conv(wb_ref, kb, b, CONV_CHUNK) + bb_ref[...]
            return c

        _row_loop(t_rows, CONV_CHUNK, out_b)

    def col(g):
        return pl.BlockSpec((t_rows, LANE), lambda i, g=g: (0, g * nt + i))

    tile = lambda rows: pl.BlockSpec((rows, LANE), lambda i: (0, i))
    return _host_call(
        body, grid=(nt,),
        in_specs=[col(0), col(1), col(2), col(3), col(4), tile(ka), tile(kb), tile(1)],
        out_specs=[tile(t_rows), tile(t_rows)],
        out_shape=[jax.ShapeDtypeStruct((t_rows, wa_w), BF16), jax.ShapeDtypeStruct((t_rows, wa_w), F32)],
        scratch_shapes=[pltpu.VMEM((nr + SUBLANE, LANE), F32), pltpu.VMEM((SUBLANE, nr, LANE), F32)],
        args=(hin, hin, hin, hin, hin, wa, wb, bb), name="mix_conv_fwd", comm=comm)


def _ln_parts(z, lg, lb):
    mu = jnp.mean(z, axis=-1, keepdims=True)
    zc = z - mu
    rstd = lax.rsqrt(jnp.mean(zc * zc, axis=-1, keepdims=True) + LN_EPS)
    zh = zc * rstd
    return zh, rstd, zh * lg + lb


def _out_proj(ya, z, lg, lb, w_out, h0, g2, g3, tm, comm):
    t_rows, d = h0.shape
    w = z.shape[1]

    def body(ya_ref, z_ref, lg_ref, lb_ref, w_ref, h0_ref, g2_ref, g3_ref, y_ref, mix_ref, h1_ref, xn2_ref):
        _, _, ln = _ln_parts(z_ref[...], lg_ref[...], lb_ref[...])
        y_ref[:, pl.ds(0, w)] = ya_ref[...]
        y_ref[:, pl.ds(w, w)] = (ln * jax.nn.sigmoid(ln)).astype(BF16)
        mix = _dot_nn(y_ref[...], w_ref[...])
        mix_ref[...] = mix
        h1 = h0_ref[...] + (mix * _rstd(mix)) * g2_ref[...]
        h1_ref[...] = h1
        xn2_ref[...] = ((h1 * _rstd(h1)) * g3_ref[...]).astype(BF16)

    blk = pl.BlockSpec((tm, d), lambda i: (i, 0))
    half = pl.BlockSpec((tm, w), lambda i: (i, 0))
    return _host_call(
        body, grid=(t_rows // tm,),
        in_specs=[half, half, _full((1, w)), _full((1, w)), _resident(w_out.shape), blk, _full((1, d)), _full((1, d))],
        out_specs=[blk, blk, blk, blk],
        out_shape=[jax.ShapeDtypeStruct((t_rows, d), BF16), jax.ShapeDtypeStruct((t_rows, d), F32),
                   jax.ShapeDtypeStruct((t_rows, d), F32), jax.ShapeDtypeStruct((t_rows, d), BF16)],
        args=(ya, z, lg, lb, w_out, h0, g2, g3), name="out_proj", comm=comm)


def _gate_up(xn2, wg_t, wu_t, tm, comm):
    t_rows, d = xn2.shape
    f = wg_t.shape[0]

    def body(x_ref, wg_ref, wu_ref, ga_ref, gu_ref, s_ref):
        xn = x_ref[...]
        for o, n in _chunks(f, N_CHUNK):
            a = _dot_nt(xn, wg_ref[pl.ds(o, n), :])
            u = _dot_nt(xn, wu_ref[pl.ds(o, n), :])
            sig = jax.nn.sigmoid(a)
            silu = a * sig
            s = silu * u
            gu_ref[:, pl.ds(o, n)] = silu.astype(BF16)
            ga_ref[:, pl.ds(o, n)] = ((u - s) * sig + s).astype(BF16)
            s_ref[:, pl.ds(o, n)] = s.astype(BF16)

    blk = pl.BlockSpec((tm, f), lambda i: (i, 0))
    return _host_call(
        body, grid=(t_rows // tm,),
        in_specs=[pl.BlockSpec((tm, d), lambda i: (i, 0)), _resident((f, d)), _resident((f, d))],
        out_specs=[blk, blk, blk], out_shape=[jax.ShapeDtypeStruct((t_rows, f), BF16)] * 3,
        args=(xn2, wg_t, wu_t), name="gate_up", comm=comm)


def _down_loss(s, wd, h1, tgt, g4, tm, x0):
    t_rows, d = h1.shape
    f = wd.shape[0]

    def body(s_ref, w_ref, h1_ref, tgt_ref, g4_ref, dh2_ref, dff_ref, dg4_ref, loss_ref):
        i = pl.program_id(0)
        ff = _dot_nn(s_ref[...], w_ref[...])
        r4 = _rstd(ff)
        fh = ff * r4
        g4 = g4_ref[...]
        h2 = h1_ref[...] + fh * g4
        row = i * tm + lax.broadcasted_iota(jnp.int32, (tm, 1), 0)
        diff = jnp.where(row >= x0, h2 - tgt_ref[...], 0.0)
        dh2 = diff / d
        dh2_ref[...] = dh2
        dff_ref[...] = _rms_bwd(dh2 * g4, fh, r4).astype(BF16)
        _acc_rows(dg4_ref, dh2 * fh, i == 0)
        _acc_rows(loss_ref, diff * diff, i == 0)

    blk = pl.BlockSpec((tm, d), lambda i: (i, 0))
    res, _ = _host_call(
        body, grid=(t_rows // tm,),
        in_specs=[pl.BlockSpec((tm, f), lambda i: (i, 0)), _resident((f, d)), blk, blk, _full((1, d))],
        out_specs=[blk, blk, _full((1, d)), _full((1, d))],
        out_shape=[jax.ShapeDtypeStruct((t_rows, d), F32), jax.ShapeDtypeStruct((t_rows, d), BF16),
                   jax.ShapeDtypeStruct((1, d), F32), jax.ShapeDtypeStruct((1, d), F32)],
        args=(s, wd, h1, tgt, g4), name="down_loss")
    return res


def _bwd_down(dff, wd, ga, gu, tm, comm):
    t_rows, d = dff.shape
    f = wd.shape[0]

    def body(dff_ref, w_ref, ga_ref, gu_ref, dau_ref):
        dff_v = dff_ref[...]
        for o, n in _chunks(f, N_CHUNK):
            ds = _dot_nt(dff_v, w_ref[pl.ds(o, n), :]).astype(BF16)
            dau_ref[0, :, pl.ds(o, n)] = ds * ga_ref[:, pl.ds(o, n)]
            dau_ref[1, :, pl.ds(o, n)] = ds * gu_ref[:, pl.ds(o, n)]

    blk = pl.BlockSpec((tm, f), lambda i: (i, 0))
    (dau,), extra = _host_call(
        body, grid=(t_rows // tm,),
        in_specs=[pl.BlockSpec((tm, d), lambda i: (i, 0)), _resident((f, d)), blk, blk],
        out_specs=[pl.BlockSpec((2, tm, f), lambda i: (0, i, 0))], out_shape=[jax.ShapeDtypeStruct((2, t_rows, f), BF16)],
        args=(dff, wd, ga, gu), name="bwd_down", comm=comm)
    return dau, extra


def _wgrad(a, b, name, after=()):
    d = b.shape[1]
    t_rows = b.shape[0]
    stacked = a.ndim == 3
    n = a.shape[-1]
    groups = a.shape[0] if stacked else 1
    steps = 1 if stacked else 2
    tile = max(t for t in range(LANE, min(n // steps, WGRAD_TILE_MAX) + 1, LANE) if n % t == 0)
    tiles = n // tile

    def body(a_ref, b_ref, o_ref):
        o_ref[...] = lax.dot_general(a_ref[...], b_ref[...], (((0,), (0,)), ((), ())),
                                     preferred_element_type=F32).astype(BF16)

    if stacked:
        a_spec = pl.BlockSpec((None, t_rows, tile), lambda g, i: (g, 0, i))
    else:
        a_spec = pl.BlockSpec((t_rows, tile), lambda g, i: (0, i))
    res, _ = _host_call(
        body, grid=(groups, tiles), in_specs=[a_spec, _resident((t_rows, d))],
        out_specs=[pl.BlockSpec((tile, d), lambda g, i: (g * tiles + i, 0))],
        out_shape=[jax.ShapeDtypeStruct((groups * n, d), BF16)], args=(a, b), name=name, after=after)
    return res[0]


def _bwd_ffn_in(dau, wg_t, wu_t, h1, dh2, g3, tm, comm):
    t_rows, d = h1.shape
    f = wg_t.shape[0]

    def body(dau_ref, wg_ref, wu_ref, h1_ref, dh2_ref, g3_ref, dh1_ref, dg3_ref):
        dxn2 = _dot_nn(dau_ref[0], wg_ref[...]) + _dot_nn(dau_ref[1], wu_ref[...])
        h1 = h1_ref[...]
        r3 = _rstd(h1)
        h1h = h1 * r3
        _acc_rows(dg3_ref, dxn2 * h1h, pl.program_id(0) == 0)
        dh1_ref[...] = dh2_ref[...] + _rms_bwd(dxn2 * g3_ref[...], h1h, r3)

    blk = pl.BlockSpec((tm, d), lambda i: (i, 0))
    return _host_call(
        body, grid=(t_rows // tm,),
        in_specs=[pl.BlockSpec((2, tm, f), lambda i: (0, i, 0)), _resident((f, d)), _resident((f, d)), blk, blk, _full((1, d))],
        out_specs=[blk, _full((1, d))],
        out_shape=[jax.ShapeDtypeStruct((t_rows, d), F32), jax.ShapeDtypeStruct((1, d), F32)],
        args=(dau, wg_t, wu_t, h1, dh2, g3), name="bwd_ffn_in", comm=comm)


def _bwd_out_proj(dh1, mix, w_out, g2, z, lg, lb, tm, after):
    t_rows, d = dh1.shape
    w = z.shape[1]

    def body(dh1_ref, mix_ref, w_ref, g2_ref, z_ref, lg_ref, lb_ref, dmix_ref, dya_ref, dz_ref, dg2_ref, dlg_ref, dlb_ref, dbb_ref):
        first = pl.program_id(0) == 0
        mix = mix_ref[...]
        r2 = _rstd(mix)
        mh = mix * r2
        dh1 = dh1_ref[...]
        _acc_rows(dg2_ref, dh1 * mh, first)
        dmix = _rms_bwd(dh1 * g2_ref[...], mh, r2).astype(BF16)
        dmix_ref[...] = dmix
        dy = _dot_nt(dmix, w_ref[...])
        dya_ref[...] = dy[:, :w]
        lg = lg_ref[...]
        zh, rstd, ln = _ln_parts(z_ref[...], lg, lb_ref[...])
        dln = dy[:, w:] * _silu_grad(ln, jax.nn.sigmoid(ln))
        _acc_rows(dlg_ref, dln * zh, first)
        _acc_rows(dlb_ref, dln, first)
        dzh = dln * lg
        dz = rstd * (dzh - jnp.mean(dzh, axis=-1, keepdims=True) - zh * jnp.mean(dzh * zh, axis=-1, keepdims=True))
        dz_ref[...] = dz
        _acc_rows(dbb_ref, dz, first)

    blk = pl.BlockSpec((tm, d), lambda i: (i, 0))
    half = pl.BlockSpec((tm, w), lambda i: (i, 0))
    vec = _full((1, w))
    res, _ = _host_call(
        body, grid=(t_rows // tm,), in_specs=[blk, blk, _resident(w_out.shape), _full((1, d)), half, vec, vec],
        out_specs=[blk, half, half, _full((1, d)), vec, vec, vec],
        out_shape=[jax.ShapeDtypeStruct((t_rows, d), BF16), jax.ShapeDtypeStruct((t_rows, w), F32),
                   jax.ShapeDtypeStruct((t_rows, w), F32), jax.ShapeDtypeStruct((1, d), F32)]
        + [jax.ShapeDtypeStruct((1, w), F32)] * 3,
        args=(dh1, mix, w_out, g2, z, lg, lb), name="bwd_out_proj", after=after)
    return res


def _mix_conv_bwd(hin, dy, dz, wa, wb, wa_w, comm):
    t_rows = hin.shape[0]
    nt = wa_w // LANE
    ka, kb = wa.shape[0], wb.shape[0]
    nr = CONV_HALO + t_rows
    kb_rows = -(-kb // SUBLANE) * SUBLANE

    def body(bg_ref, cg_ref, ha_ref, val_ref, gt_ref, dya_ref, dz_ref, wa_ref, wb_ref,
             dh_ref, dwa_ref, dwb_ref, base, sh, based, shd, tmp, wbc):
        zeros = lambda n: jnp.zeros((n, LANE), F32)
        base[pl.ds(0, CONV_HALO), :] = zeros(CONV_HALO)
        base[pl.ds(nr, SUBLANE), :] = zeros(SUBLANE)
        based[pl.ds(t_rows, CONV_HALO + SUBLANE), :] = zeros(CONV_HALO + SUBLANE)

        def fwd_slot(k_taps, k):
            return _tap_slot(CONV_HALO - (k_taps - 1) + k)

        def bwd_slot(k_taps, k):
            return _tap_slot(k_taps - 1 - k)

        def conv(w_ref, k_taps, src, slot, b, n):
            acc = None
            for k in range(k_taps):
                r, q = slot(k_taps, k)
                term = w_ref[pl.ds(k, 1), :] * _shifted_rows(src, r, b + q, n)
                acc = term if acc is None else acc + term
            return acc

        def by_residue(k_taps, slot):
            groups = {}
            for k in range(k_taps):
                r, q = slot(k_taps, k)
                groups.setdefault(r, []).append((k, q // SUBLANE))
            return groups

        def wgrad_loop(w_ref, k_taps):
            n_sub = WGRAD_ROWS // SUBLANE
            for k in range(k_taps):
                wbc[k] = jnp.broadcast_to(w_ref[pl.ds(k, 1), :], (SUBLANE, LANE))
            fwd, bwd = by_residue(k_taps, fwd_slot), by_residue(k_taps, bwd_slot)

            def window(src, r, taps, b):
                span = n_sub + max(qi for _, qi in taps)
                return [_shifted_rows(src, r, b + SUBLANE * i, SUBLANE) for i in range(span)]

            def step(b, accs):
                accs = list(accs)
                dv = [based[pl.ds(b + SUBLANE * j, SUBLANE), :] for j in range(n_sub)]
                for r, taps in fwd.items():
                    win = window((base, sh), r, taps, b)
                    for k, qi in taps:
                        t = dv[0] * win[qi]
                        for j in range(1, n_sub):
                            t = t + dv[j] * win[qi + j]
                        accs[k] = accs[k] + t
                outs = [None] * n_sub
                for r, taps in bwd.items():
                    win = window((based, shd), r, taps, b)
                    for k, qi in taps:
                        wk = wbc[k]
                        for j in range(n_sub):
                            term = wk * win[qi + j]
                            outs[j] = term if outs[j] is None else outs[j] + term
                for j in range(n_sub):
                    tmp[pl.ds(b + SUBLANE * j, SUBLANE), :] = outs[j]
                return tuple(accs)

            return _row_loop(t_rows, WGRAD_ROWS, step, tuple(zeros(SUBLANE) for _ in range(k_taps)))

        def store_taps(ref, accs, rows):
            for k, acc in enumerate(accs):
                ref[pl.ds(k, 1), :] = jnp.sum(acc, axis=0, keepdims=True)
            if rows > len(accs):
                ref[pl.ds(len(accs), rows - len(accs)), :] = zeros(rows - len(accs))

        def fill_a(b, c):
            sl = pl.ds(b, CONV_CHUNK)
            base[pl.ds(CONV_HALO + b, CONV_CHUNK), :] = cg_ref[sl, :] * ha_ref[sl, :]
            based[sl, :] = dya_ref[sl, :] * bg_ref[sl, :]
            return c

        _row_loop(t_rows, CONV_CHUNK, fill_a)
        _fill_shifted(sh, base, sorted({fwd_slot(ka, k)[0] for k in range(ka)}), nr)
        _fill_shifted(shd, based, sorted({bwd_slot(ka, k)[0] for k in range(ka)}), nr)

        def d_bgate(b, c):
            sl = pl.ds(b, CONV_CHUNK)
            dh_ref[0, sl, :] = (dya_ref[sl, :] * conv(wa_ref, ka, (base, sh), fwd_slot, b, CONV_CHUNK)).astype(BF16)
            return c

        _row_loop(t_rows, CONV_CHUNK, d_bgate)
        store_taps(dwa_ref, wgrad_loop(wa_ref, ka), SUBLANE)

        def d_ch(b, c):
            sl = pl.ds(b, CONV_CHUNK)
            dua = tmp[sl, :]
            dh_ref[1, sl, :] = (dua * ha_ref[sl, :]).astype(BF16)
            dh_ref[2, sl, :] = (dua * cg_ref[sl, :]).astype(BF16)
            return c

        _row_loop(t_rows, CONV_CHUNK, d_ch)

        def fill_b(b, c):
            sl = pl.ds(b, CONV_CHUNK)
            base[pl.ds(CONV_HALO + b, CONV_CHUNK), :] = val_ref[sl, :] * jax.nn.sigmoid(gt_ref[sl, :])
            based[sl, :] = dz_ref[sl, :]
            return c

        _row_loop(t_rows, CONV_CHUNK, fill_b)
        _fill_shifted(sh, base, range(SUBLANE), nr)
        _fill_shifted(shd, based, range(SUBLANE), nr)
        store_taps(dwb_ref, wgrad_loop(wb_ref, kb), kb_rows)

        def d_glu(b, c):
            sl = pl.ds(b, CONV_CHUNK)
            dgg = tmp[sl, :]
            sig = jax.nn.sigmoid(gt_ref[sl, :])
            dh_ref[3, sl, :] = (dgg * sig).astype(BF16)
            dh_ref[4, sl, :] = (dgg * val_ref[sl, :] * (sig * (1.0 - sig))).astype(BF16)
            return c

        _row_loop(t_rows, CONV_CHUNK, d_glu)

    def col(g):
        return pl.BlockSpec((t_rows, LANE), lambda i, g=g: (0, g * nt + i))

    tile = lambda rows: pl.BlockSpec((rows, LANE), lambda i: (0, i))
    return _host_call(
        body, grid=(nt,),
        in_specs=[col(0), col(1), col(2), col(3), col(4), tile(t_rows), tile(t_rows), tile(ka), tile(kb)],
        out_specs=[pl.BlockSpec((5, t_rows, LANE), lambda i: (0, 0, i)), tile(SUBLANE), tile(kb_rows)],
        out_shape=[jax.ShapeDtypeStruct((5, t_rows, wa_w), BF16), jax.ShapeDtypeStruct((SUBLANE, wa_w), F32),
                   jax.ShapeDtypeStruct((kb_rows, wa_w), F32)],
        scratch_shapes=[pltpu.VMEM((nr + SUBLANE, LANE), F32), pltpu.VMEM((SUBLANE, nr, LANE), F32),
                        pltpu.VMEM((nr + SUBLANE, LANE), F32), pltpu.VMEM((SUBLANE, nr, LANE), F32),
                        pltpu.VMEM((t_rows, LANE), F32), pltpu.VMEM((kb_rows, SUBLANE, LANE), F32)],
        args=(hin, hin, hin, hin, hin, dy, dz, wa, wb), name="mix_conv_bwd", comm=comm)


def _bwd_in_proj(dh5, win_t, h0, dh1, g1, tm, x0, comm):
    t_rows, d = h0.shape
    groups, _, w = dh5.shape
    assert x0 <= tm

    def body(dh_ref, w_ref, h0_ref, dh1_ref, g1_ref, dh0_ref, dg1_ref, dmeta_ref):
        dxn1 = None
        for g in range(groups):
            part = _dot_nn(dh_ref[g], w_ref[pl.ds(g * w, w), :])
            dxn1 = part if dxn1 is None else dxn1 + part
        h0 = h0_ref[...]
        r1 = _rstd(h0)
        h0h = h0 * r1
        _acc_rows(dg1_ref, dxn1 * h0h, pl.program_id(0) == 0)
        dh0_ref[...] = dh1_ref[...] + _rms_bwd(dxn1 * g1_ref[...], h0h, r1)

        @pl.when(pl.program_id(0) == 0)
        def _():
            dmeta_ref[...] = dh0_ref[pl.ds(x0 - N_META, N_META), :]

    blk = pl.BlockSpec((tm, d), lambda i: (i, 0))
    return _host_call(
        body, grid=(t_rows // tm,),
        in_specs=[pl.BlockSpec((groups, tm, w), lambda i: (0, i, 0)), _resident(win_t.shape), blk, blk, _full((1, d))],
        out_specs=[blk, _full((1, d)), _full((N_META, d))],
        out_shape=[jax.ShapeDtypeStruct((t_rows, d), F32), jax.ShapeDtypeStruct((1, d), F32), jax.ShapeDtypeStruct((N_META, d), F32)],
        args=(dh5, win_t, h0, dh1, g1), name="bwd_in_proj", comm=comm)


def _pair_small(smalls, d):
    (dmeta, dg1, dg2, dg3, dg4, dbb, dlg, dlb, lossv, dwa, dwb) = smalls
    half = d // 2
    kb_rows = dwb.shape[0]

    def body(dmeta_ref, dg1_ref, dg2_ref, dg3_ref, dg4_ref, dbb_ref, dlg_ref, dlb_ref, loss_ref, dwa_ref, dwb_ref,
             sums_ref, pbuf, psib, ps_send, ps_recv):
        x, y, c = _mesh_pos()
        _pair_handshake()
        pbuf[...] = jnp.zeros_like(pbuf)
        pbuf[pl.ds(0, N_META), :] = dmeta_ref[...]
        for row, ref in ((16, dg1_ref), (17, dg2_ref), (18, dg3_ref), (19, dg4_ref)):
            pbuf[pl.ds(row, 1), :] = ref[...]
        pbuf[pl.ds(20, 1), pl.ds(0, half)] = dbb_ref[...]
        pbuf[pl.ds(20, 1), pl.ds(half, half)] = dlg_ref[...]
        pbuf[pl.ds(21, 1), pl.ds(0, half)] = dlb_ref[...]
        lv = loss_ref[...]
        pbuf[pl.ds(21, 1), pl.ds(half, half)] = lv[:, :half] + lv[:, half:]
        pbuf[pl.ds(24, SUBLANE), pl.ds(0, half)] = dwa_ref[...]
        pbuf[pl.ds(32, kb_rows), pl.ds(0, half)] = dwb_ref[...]
        to_sib = _remote(pbuf, psib, ps_send.at[0], ps_recv.at[0], (x, y, 1 - c))
        to_sib.start()
        to_sib.wait_recv()
        s = pbuf[...] + psib[...]
        for k in range(3):
            sums_ref[k] = s
        to_sib.wait_send()

    return pl.pallas_call(
        body, out_shape=jax.ShapeDtypeStruct((3, SMALL_ROWS, d), F32), in_specs=[VMEM] * 11, out_specs=VMEM,
        scratch_shapes=[pltpu.VMEM((SMALL_ROWS, d), F32), pltpu.VMEM((SMALL_ROWS, d), F32),
                        pltpu.SemaphoreType.DMA((1,)), pltpu.SemaphoreType.DMA((1,))],
        name="pair_small", compiler_params=pltpu.CompilerParams(vmem_limit_bytes=VMEM_LIMIT, collective_id=PAIR_BARRIER_ID))(*smalls)


def _total_small(own, others):
    _, r, d = own.shape
    half, cols = d // 2, d // N_DEV

    def body(own_ref, others_ref, tot_ref, meta_ref, g1_ref, g2_ref, g3_ref, g4_ref, dbb_ref, dlg_ref, dlb_ref, loss_ref, chip_p):
        x, y, c = _mesh_pos()
        chip_p[2 * x + y] = own_ref[0]
        for k, (cx, cy) in enumerate(_other_chips(x, y)):
            chip_p[2 * cx + cy] = others_ref[k]
        tot_ref[...] = ((chip_p[0] + chip_p[1]) + chip_p[2]) + chip_p[3]
        meta_ref[...] = tot_ref[pl.ds(0, N_META), pl.ds(pl.multiple_of(_dev_index(x, y, c) * cols, LANE), cols)]
        for row, ref in ((16, g1_ref), (17, g2_ref), (18, g3_ref), (19, g4_ref)):
            ref[...] = tot_ref[pl.ds(row, 1), :]
        dbb_ref[...] = tot_ref[pl.ds(20, 1), pl.ds(0, half)]
        dlg_ref[...] = tot_ref[pl.ds(20, 1), pl.ds(half, half)]
        dlb_ref[...] = tot_ref[pl.ds(21, 1), pl.ds(0, half)]
        loss_ref[...] = (0.5 / d) * jnp.sum(tot_ref[pl.ds(21, 1), pl.ds(half, half)], axis=-1, keepdims=True)

    row = lambda n: jax.ShapeDtypeStruct((1, n), F32)
    return pl.pallas_call(
        body, out_shape=[jax.ShapeDtypeStruct((r, d), F32), jax.ShapeDtypeStruct((N_META, cols), F32), row(d), row(d), row(d), row(d),
                         row(half), row(half), row(half), row(1)],
        scratch_shapes=[pltpu.VMEM((4, r, d), F32)], name="total_small", compiler_params=_cparams())(own, others)


def _adamw(w, g, m, v):
    m = ADAM_B1 * m + (1.0 - ADAM_B1) * g
    v = ADAM_B2 * v + (1.0 - ADAM_B2) * jnp.square(g)
    m_hat = m / (1.0 - ADAM_B1 ** ADAM_STEP)
    v_hat = v / (1.0 - ADAM_B2 ** ADAM_STEP)
    delta = -ADAM_LR * (m_hat / (jnp.sqrt(v_hat) + ADAM_EPS) + ADAM_WD * w)
    return delta, m, v


def _adam_big(g, pair, part, w, m, v, name, row0=0):
    r, d = w.shape
    cols = d // ADAM_COL_BLOCKS
    assert row0 % r == 0

    def body(me_ref, g_ref, pair_ref, part_ref, w_ref, m_ref, v_ref, go_ref, d_ref, mo_ref, vo_ref):
        g = g_ref[...].astype(F32) + pair_ref[...].astype(F32)
        for k in range(3):
            g = g + part_ref[k].astype(F32)
        go_ref[...] = g
        d_ref[...], mo_ref[...], vo_ref[...] = _adamw(w_ref[...], g, m_ref[...], v_ref[...])

    blk = pl.BlockSpec((r, cols), lambda i, me_ref: (0, i))
    grid_spec = pltpu.PrefetchScalarGridSpec(
        num_scalar_prefetch=1, grid=(ADAM_COL_BLOCKS,),
        in_specs=[pl.BlockSpec((r, cols), lambda i, me_ref: (me_ref[0], i)),
                  pl.BlockSpec((None, r, cols), lambda i, me_ref: (0, 0, i)),
                  pl.BlockSpec((3, r, cols), lambda i, me_ref: (0, 0, i)), blk, blk, blk],
        out_specs=[blk, blk, blk, blk])
    me = jnp.reshape(_dev_index(*_mesh_pos()) + row0 // r, (1,)).astype(jnp.int32)
    return pl.pallas_call(body, out_shape=[jax.ShapeDtypeStruct((r, d), F32)] * 4, grid_spec=grid_spec, name=name,
                          compiler_params=_cparams(1))(me, g, pair, part, w, m, v)


def _adam_small(gs, ws, ms, vs):
    n = len(gs)

    def body(*refs):
        ins, outs = refs[:4 * n], refs[4 * n:]
        for i in range(n):
            g = ins[i][...]
            delta, m, v = _adamw(ins[n + i][...], g, ins[2 * n + i][...], ins[3 * n + i][...])
            outs[i][...] = delta
            outs[n + i][...] = m
            outs[2 * n + i][...] = v

    shapes = [jax.ShapeDtypeStruct(w.shape, F32) for w in ws]
    return pl.pallas_call(body, out_shape=shapes * 3, name="adam_small", compiler_params=_cparams())(*gs, *ws, *ms, *vs)


def kernel(x, meta_tokens, pre_mix_norm, w_in, conv_a_w, conv_b_w, conv_b_bias, ln_b_gain, ln_b_bias, w_out, post_mix_norm, pre_ffn_norm, w_gate, w_up, w_down, post_ffn_norm, loss_target, m_meta_tokens, m_pre_mix_norm, m_w_in, m_conv_a_w, m_conv_b_w, m_conv_b_bias, m_ln_b_gain, m_ln_b_bias, m_w_out, m_post_mix_norm, m_pre_ffn_norm, m_w_gate, m_w_up, m_w_down, m_post_ffn_norm, v_meta_tokens, v_pre_mix_norm, v_w_in, v_conv_a_w, v_conv_b_w, v_conv_b_bias, v_ln_b_gain, v_ln_b_bias, v_w_out, v_post_mix_norm, v_pre_ffn_norm, v_w_gate, v_w_up, v_w_down, v_post_ffn_norm):
    _, seq, d = x.shape
    ka, ca_loc = conv_a_w.shape[1:]
    kb, cb_loc = conv_b_w.shape[1:]
    wa_w = ca_loc * N_DEV
    assert cb_loc == ca_loc and wa_w % LANE == 0 and w_in.shape[2] * N_DEV == 5 * wa_w and 2 * wa_w == d
    pad = (-(N_META + seq)) % ROW_ALIGN
    x0 = pad + N_META
    t_rows = x0 + seq
    assert t_rows % (N_ROW_BLOCKS * BF16_ROWS) == 0 and t_rows % CONV_CHUNK == 0 and d % LANE == 0
    tm = t_rows // N_ROW_BLOCKS
    tm2 = t_rows // 2
    me = _dev_index(*_mesh_pos())

    def as_rows(w_in_like, w_out_like, w_gate_like, w_up_like, w_down_like):
        return (w_in_like[0].T, w_out_like[0], w_gate_like[0].T, w_up_like[0].T, w_down_like[0])

    w_loc = as_rows(w_in, w_out, w_gate, w_up, w_down)
    rows = [w.shape[0] for w in w_loc]
    assert all(r % ADD_CHUNK == 0 for r in rows)
    P_IN, P_OUT, P_GATE, P_UP, P_DOWN = range(N_BIG)

    sm = jnp.zeros((SM_ROWS, LANE), F32)
    sm = sm.at[0:N_META, :].set(meta_tokens)
    sm = sm.at[16:16 + ka, 0:ca_loc].set(conv_a_w[0])
    sm = sm.at[24:24 + kb, 0:cb_loc].set(conv_b_w[0])
    wl, wfull, sm_all, h0, tgt = _gather_first(w_loc, sm, [(P_IN, 0, rows[P_IN])], x[0], loss_target[0], t_rows, x0)
    wa =jnp.transpose(sm_all[:, 16:16 + ka, 0:ca_loc], (1, 0, 2)).reshape(ka, wa_w)
    wb = jnp.transpose(sm_all[:, 24:24 + kb, 0:cb_loc], (1, 0, 2)).reshape(kb, wa_w)

    later = (P_OUT, P_GATE, P_UP, P_DOWN)
    sems, wl, started, _ = _gather_start(wl, wfull, later, rows, START_BARRIER_IDS[0])
    for p, arr in zip(later, started):
        wfull[p] = arr

    def arrived(p, after, name):
        nonlocal wl
        wl, wfull[p] = _gather_wait(wl, wfull[p], sems[later.index(p)], after, rows[p], name)
        return _forward_comm(wfull[p], rows[p])

    (xn1, hin), _ = _in_proj(h0, pre_mix_norm, wfull[P_IN], tm, None)
    (ya, z), (wfull[P_OUT],) = _mix_conv_fwd(hin, wa, wb, conv_b_bias, wa_w, arrived(P_OUT, hin, "gather_wait_out"))
    (y, mix, h1, xn2), (wfull[P_GATE],) = _out_proj(ya, z, ln_b_gain, ln_b_bias, wfull[P_OUT], h0, post_mix_norm, pre_ffn_norm, tm2,
                                                    arrived(P_GATE, z, "gather_wait_gate"))
    arrived(P_UP, xn2, "gather_wait_up")
    wfull[P_UP] = _forward_now(wfull[P_UP], rows[P_UP], "forward_up")
    (ga, gu, s), _ = _gate_up(xn2, wfull[P_GATE], wfull[P_UP], tm, None)
    arrived(P_DOWN, s, "gather_wait_down")
    wfull[P_DOWN] = _forward_now(wfull[P_DOWN], rows[P_DOWN], "forward_down")
    dh2, dff, dg4, lossv = _down_loss(s, wfull[P_DOWN], h1, tgt, post_ffn_norm, tm, x0)

    gwd = _wgrad(s, dff, "wgrad_down")
    dau, (pair_d,) = _bwd_down(dff, wfull[P_DOWN], ga, gu, tm, _pair_comm(gwd, rows[P_DOWN]))
    (flight_d,), token = _chip_start([_pair_sum(gwd, pair_d, rows[P_DOWN], "pair_sum_down")], "chip_start_down", START_BARRIER_IDS[1])
    gw_gu = _wgrad(dau, xn2, "wgrad_gate_up", [token])
    up0 = N_DEV * rows[P_GATE]
    (dh1, dg3), (pair_g, pair_u) = _bwd_ffn_in(dau, wfull[P_GATE], wfull[P_UP], h1, dh2, pre_ffn_norm, tm,
                                               _merge_comms([_pair_comm(gw_gu, rows[P_GATE]), _pair_comm(gw_gu, rows[P_UP], up0)]))
    (flight_g, flight_u), token = _chip_start([_pair_sum(gw_gu, pair_g, rows[P_GATE], "pair_sum_gate"),
                                               _pair_sum(gw_gu, pair_u, rows[P_UP], "pair_sum_up", row0=up0)], "chip_start_gate_up",
                                              START_BARRIER_IDS[2])
    dmix, dya, dz, dg2, dlg, dlb, dbb = _bwd_out_proj(dh1, mix, wfull[P_OUT], post_mix_norm, z, ln_b_gain, ln_b_bias, tm, [token])
    gwo = _wgrad(y, dmix, "wgrad_out")
    (dh5, dwa, dwb), (pair_o,) = _mix_conv_bwd(hin, dya, dz, wa, wb, wa_w, _pair_comm(gwo, rows[P_OUT]))
    (flight_o,), token = _chip_start([_pair_sum(gwo, pair_o, rows[P_OUT], "pair_sum_out")], "chip_start_out", START_BARRIER_IDS[3])
    gwi = _wgrad(dh5, xn1, "wgrad_in", [token])
    (dh0, dg1, dmeta), (pair_i,) = _bwd_in_proj(dh5, wfull[P_IN], h0, dh1, pre_mix_norm, tm, x0, _pair_comm(gwi, rows[P_IN]))
    grad_x = dh0[x0:][None]
    small_sums = _pair_small((dmeta, dg1, dg2, dg3, dg4, dbb, dlg, dlb, lossv, dwa, dwb), d)
    (flight_s,), token = _chip_start([small_sums], "chip_start_small", START_BARRIER_IDS[4])
    (flight_i,), token = _chip_start([_pair_sum(gwi, pair_i, rows[P_IN], "pair_sum_in", [token])], "chip_start_in",
                                     START_BARRIER_IDS[5])

    def landed(flight, after, tag):
        sems_p, sums, land = flight
        return _chip_wait(sums, land, sems_p, after, "chip_wait_" + tag)

    m_loc = as_rows(m_w_in, m_w_out, m_w_gate, m_w_up, m_w_down)
    v_loc = as_rows(v_w_in, v_w_out, v_w_gate, v_w_up, v_w_down)
    full_grads = {P_IN: gwi, P_OUT: gwo, P_GATE: gw_gu, P_UP: gw_gu, P_DOWN: gwd}
    pairs = {P_IN: pair_i, P_OUT: pair_o, P_GATE: pair_g, P_UP: pair_u, P_DOWN: pair_d}
    flights = {P_IN: flight_i, P_OUT: flight_o, P_GATE: flight_g, P_UP: flight_u, P_DOWN: flight_d}
    names = {P_IN: "w_in", P_OUT: "w_out", P_GATE: "w_gate", P_UP: "w_up", P_DOWN: "w_down"}
    bigs = {}

    def adam_big(p, after):
        _, part = landed(flights[p], after, names[p])
        res = _adam_big(full_grads[p], pairs[p], part, w_loc[p], m_loc[p], v_loc[p], "adam_" + names[p], up0 if p == P_UP else 0)
        bigs[names[p]] = [(o.T if p in (P_IN, P_GATE, P_UP) else o)[None] for o in res]
        return res[1]

    for p in (P_DOWN, P_GATE, P_UP, P_OUT):
        token = adam_big(p, token)
    (ptot, g_meta, g_pre_mix, g_post_mix, g_pre_ffn, g_post_ffn, g_conv_bias, g_ln_gain, g_ln_bias, loss11) = _total_small(
        *landed(flight_s, token, "small"))
    loss = jnp.reshape(loss11, ())
    taps_major = lambda a: jnp.transpose(a, (1, 0, 2))
    CONV = (2, 3)
    g_small = [g_meta, g_pre_mix, lax.dynamic_slice(ptot, (24, me * ca_loc), (ka, ca_loc))[:, None, :],
               lax.dynamic_slice(ptot, (32, me * cb_loc), (kb, cb_loc))[:, None, :],
               g_conv_bias, g_ln_gain, g_ln_bias, g_post_mix, g_pre_ffn, g_post_ffn]
    w_small = [meta_tokens, pre_mix_norm, taps_major(conv_a_w), taps_major(conv_b_w), conv_b_bias, ln_b_gain, ln_b_bias,
               post_mix_norm, pre_ffn_norm, post_ffn_norm]
    m_small = [m_meta_tokens, m_pre_mix_norm, taps_major(m_conv_a_w), taps_major(m_conv_b_w), m_conv_b_bias, m_ln_b_gain,
               m_ln_b_bias, m_post_mix_norm, m_pre_ffn_norm, m_post_ffn_norm]
    v_small = [v_meta_tokens, v_pre_mix_norm, taps_major(v_conv_a_w), taps_major(v_conv_b_w), v_conv_b_bias, v_ln_b_gain,
               v_ln_b_bias, v_post_mix_norm, v_pre_ffn_norm, v_post_ffn_norm]
    small = list(_adam_small(g_small, w_small, m_small, v_small))
    n_small = len(w_small)
    for i in CONV:
        g_small[i] = taps_major(g_small[i])
        for k in range(3):
            small[k * n_small + i] = taps_major(small[k * n_small + i])
    d_small, nm_small, nv_small = small[:n_small], small[n_small:2 * n_small], small[2 * n_small:]

    adam_big(P_IN, small[0])

    def ordered(pick_small, pick_big):
        sm_it = iter(range(n_small))
        out = []
        for name in ("s", "s", "w_in", "s", "s", "s", "s", "s", "w_out", "s", "s", "w_gate", "w_up", "w_down", "s"):
            out.append(pick_small(next(sm_it)) if name == "s" else pick_big(name))
        return out

    grads = ordered(lambda i: g_small[i], lambda n: bigs[n][0])
    deltas = ordered(lambda i: d_small[i], lambda n: bigs[n][1])
    new_m = ordered(lambda i: nm_small[i], lambda n: bigs[n][2])
    new_v = ordered(lambda i: nv_small[i], lambda n: bigs[n][3])
    return (loss, grad_x, *grads, *deltas, *new_m, *new_v)
```

```python
import jax
import jax.numpy as jnp
from jax import lax
from jax.experimental import pallas as pl
from jax.experimental.pallas import tpu as pltpu

F32 = jnp.float32
BF16 = jnp.bfloat16
MESH = pl.DeviceIdType.MESH

N_META = 16
N_DEV = 8
RMS_EPS = 1e-6
LN_EPS = 1e-5
ADAM_LR = 0.001
ADAM_B1 = 0.9
ADAM_B2 = 0.999
ADAM_EPS = 1e-08
ADAM_WD = 0.01
ADAM_STEP = 10

LANE = 128
SUBLANE = 8
BF16_ROWS = 16
ROW_ALIGN = 128
N_ROW_BLOCKS = 4
CONV_HALO = 32
CONV_CHUNK = 64
WGRAD_ROWS = 32
N_CHUNK = 512
WGRAD_TILE_MAX = 1408
ADD_CHUNK = 32
ADAM_COL_BLOCKS = 2
COPY_PIECES = 4
V7X_VMEM_BYTES = 64 * 1024 * 1024
VMEM_LIMIT = V7X_VMEM_BYTES - 6 * 1024 * 1024
SMALL_ROWS = 64
SM_ROWS = 56
N_BIG = 5

ANY = pl.BlockSpec(memory_space=pl.ANY)
VMEM = pl.BlockSpec(memory_space=pltpu.VMEM)


def _cparams(n_grid_axes=0):
    sem = ("arbitrary",) * n_grid_axes if n_grid_axes else None
    return pltpu.CompilerParams(dimension_semantics=sem, vmem_limit_bytes=VMEM_LIMIT)


def _mesh_pos():
    return lax.axis_index("x"), lax.axis_index("y"), lax.axis_index("c")


def _dev_index(px, py, pc):
    return 4 * px + 2 * py + pc


def _other_chips(x, y):
    return [(1 - x, y), (x, 1 - y), (1 - x, 1 - y)]


def _full(shape):
    return pl.BlockSpec(shape, lambda *_: (0,) * len(shape))


def _resident(shape):
    return pl.BlockSpec(shape, lambda *_: (0,) * len(shape), pipeline_mode=pl.Buffered(1))


def _dot_nt(a, w):
    return lax.dot_general(a, w, (((1,), (1,)), ((), ())), preferred_element_type=F32)


def _dot_nn(a, w):
    return jnp.dot(a, w, preferred_element_type=F32)


def _chunks(n, c):
    out, o = [], 0
    while o < n:
        out.append((o, min(c, n - o)))
        o += c
    return out


def _rstd(h):
    return lax.rsqrt(jnp.mean(h * h, axis=-1, keepdims=True) + RMS_EPS)


def _rms_bwd(dyh, yh, r):
    return r * (dyh - yh * jnp.mean(dyh * yh, axis=-1, keepdims=True))


def _silu_grad(a, sig):
    return sig * (1.0 + a * (1.0 - sig))


def _acc_rows(ref, val, first):
    s = jnp.sum(val, axis=0, keepdims=True)

    @pl.when(first)
    def _():
        ref[...] = s

    @pl.when(jnp.logical_not(first))
    def _():
        ref[...] += s


def _row_loop(t_rows, chunk, fn, carry=None):
    def step(i, c):
        return fn(pl.multiple_of(i * chunk, chunk), c)

    return lax.fori_loop(0, t_rows // chunk, step, carry)


def _remote(src, dst, send_sem, recv_sem, to):
    return pltpu.make_async_remote_copy(src_ref=src, dst_ref=dst, send_sem=send_sem, recv_sem=recv_sem,
                                        device_id=to, device_id_type=MESH)


class _Comm:
    def __init__(self, inputs, out_shapes, aliases, scratch, start, finish):
        self.inputs, self.out_shapes, self.aliases, self.scratch = list(inputs), list(out_shapes), dict(aliases), list(scratch)
        self.start, self.finish = start, finish


def _merge_comms(comms):
    inputs, out_shapes, aliases, scratch, spans = [], [], {}, [], []
    for cm in comms:
        spans.append((len(inputs), len(out_shapes), len(scratch), cm))
        aliases.update({len(inputs) + k: len(out_shapes) + v for k, v in cm.aliases.items()})
        inputs += cm.inputs
        out_shapes += cm.out_shapes
        scratch += cm.scratch

    def run(which):
        def fn(ins, outs, scr):
            for i0, o0, s0, cm in spans:
                getattr(cm, which)(ins[i0:i0 + len(cm.inputs)], outs[o0:o0 + len(cm.out_shapes)], scr[s0:s0 + len(cm.scratch)])
        return fn

    return _Comm(inputs, out_shapes, aliases, scratch, run("start"), run("finish"))


def _host_call(body, *, grid, in_specs, out_specs, out_shape, args, name, scratch_shapes=(), comm=None, after=()):
    talks = comm is not None
    if comm is None:
        comm = _Comm([], [], {}, [], lambda *_: None, lambda *_: None)
    n_in, n_out, n_scr = len(args), len(out_shape), len(scratch_shapes)
    c_in, c_out = len(comm.inputs), len(comm.out_shapes)
    n_after = len(after)

    def open_comm(c_ins, c_outs, c_scr):
        if talks:
            _pair_handshake()
        comm.start(c_ins, c_outs, c_scr)

    def hosted(*refs):
        ins, c_ins = refs[:n_in], refs[n_in:n_in + c_in]
        o0 = n_in + c_in + n_after
        outs, c_outs = refs[o0:o0 + n_out], refs[o0 + n_out:o0 + n_out + c_out]
        s0 = o0 + n_out + c_out
        scr, c_scr = refs[s0:s0 + n_scr], refs[s0 + n_scr:]
        if not grid:
            open_comm(c_ins, c_outs, c_scr)
            body(*ins, *outs, *scr)
            comm.finish(c_ins, c_outs, c_scr)
            return
        first = last = None
        for a, n in enumerate(grid):
            f, l = pl.program_id(a) == 0, pl.program_id(a) == n - 1
            first = f if first is None else jnp.logical_and(first, f)
            last = l if last is None else jnp.logical_and(last, l)

        @pl.when(first)
        def _():
            open_comm(c_ins, c_outs, c_scr)

        body(*ins, *outs, *scr)

        @pl.when(last)
        def _():
            comm.finish(c_ins, c_outs, c_scr)

    sem = ("arbitrary",) * len(grid) if grid else None
    params = pltpu.CompilerParams(dimension_semantics=sem, vmem_limit_bytes=VMEM_LIMIT,
                                  collective_id=PAIR_BARRIER_ID if talks else None)
    res = pl.pallas_call(
        hosted, grid=grid, in_specs=list(in_specs) + [ANY] * (c_in + n_after), out_specs=list(out_specs) + [ANY] * c_out,
        out_shape=list(out_shape) + comm.out_shapes, scratch_shapes=list(scratch_shapes) + comm.scratch,
        input_output_aliases={n_in + k: n_out + v for k, v in comm.aliases.items()},
        name=name, compiler_params=params)(*args, *comm.inputs, *after)
    return list(res[:n_out]), list(res[n_out:])


PAIR_BARRIER_ID = 0
START_BARRIER_IDS = (1, 2, 3, 4, 5, 6)
ALL_PEERS_BARRIER_ID = 7


def _chips_handshake():
    x, y, c = _mesh_pos()
    barrier = pltpu.get_barrier_semaphore()
    for chip in _other_chips(x, y):
        pl.semaphore_signal(barrier, inc=1, device_id=(*chip, c), device_id_type=MESH)
    pl.semaphore_wait(barrier, 3)


def _pair_handshake():
    x, y, c = _mesh_pos()
    barrier = pltpu.get_barrier_semaphore()
    pl.semaphore_signal(barrier, inc=1, device_id=(x, y, 1 - c), device_id_type=MESH)
    pl.semaphore_wait(barrier, 1)


GATHER_SEMS = 10


class _Gather:
    def __init__(self, jobs, rows, lo, src_ref, dests, send_sems, recv_sems):
        x, y, c = _mesh_pos()
        me, sib = (x, y, c), (x, y, 1 - c)
        nx, ny, dg = (1 - x, y, c), (x, 1 - y, c), (1 - x, 1 - y, c)
        self.relayed, self.direct, self.relay, self.to_sib, self.sib_fwd = [], [], [], [], []
        for n, (p, r0, nr) in enumerate(jobs):
            assert nr % (2 * BF16_ROWS) == 0
            half = nr // 2

            def rows_of(dev, h, p=p, r0=r0, nr=nr, half=half):
                off, cnt = (r0, nr) if h is None else (r0 + h * half, half)
                return dests[p].at[pl.ds(pl.multiple_of(_dev_index(*dev) * rows[p] + off, BF16_ROWS), cnt), :]

            def mine(h, p=p, r0=r0, nr=nr, half=half):
                off, cnt = (r0, nr) if h is None else (r0 + h * half, half)
                return src_ref.at[pl.ds(lo[p] + off, cnt), :]

            sem = lambda k, n=n: (send_sems.at[GATHER_SEMS * n + k], recv_sems.at[GATHER_SEMS * n + k])
            self.relayed.append([_remote(mine(0), rows_of(me, 0), *sem(0), nx), _remote(mine(1), rows_of(me, 1), *sem(3), ny)])
            self.direct.append([_remote(mine(1), rows_of(me, 1), *sem(1), nx), _remote(mine(0), rows_of(me, 0), *sem(2), ny)])
            self.relay.append([_remote(rows_of(nx, 0), rows_of(nx, 0), *sem(4), ny), _remote(rows_of(ny, 1), rows_of(ny, 1), *sem(5), nx)])
            self.to_sib.append(_remote(mine(None), rows_of(me, None), *sem(6), sib))
            self.sib_fwd.append([_remote(rows_of(dev, None), rows_of(dev, None), *sem(7 + i), sib) for i, dev in enumerate((nx, ny, dg))])

    def start(self):
        for group in (self.relayed, self.direct):
            for cps in group:
                for cp in cps:
                    cp.start()
        for cp in self.to_sib:
            cp.start()

    def mid(self):
        for first, relay in zip(self.relayed, self.relay):
            for arrived, onward in zip(first, relay):
                arrived.wait_recv()
                onward.start()

    def finish(self):
        for direct, relay, fwd in zip(self.direct, self.relay, self.sib_fwd):
            for k in range(2):
                direct[k].wait_recv()
                fwd[k].start()
            for cp in relay:
                cp.wait_recv()
            fwd[2].start()
        for n in range(len(self.to_sib)):
            self.to_sib[n].wait_recv()
            for cp in self.sib_fwd[n]:
                cp.wait_recv()
            for cp in self.relayed[n] + self.direct[n] + self.relay[n] + [self.to_sib[n]] + self.sib_fwd[n]:
                cp.wait_send()


HBM = pl.BlockSpec(memory_space=pltpu.HBM)
SEM = pl.BlockSpec(memory_space=pltpu.SEMAPHORE)
FLOWS = pltpu.SideEffectType.DATAFLOW_SIDE_EFFECTING


def _in_hbm(a):
    return pltpu.with_memory_space_constraint(a, pltpu.HBM)


def _gather_start(wl, dests, ps, rows, barrier_id):
    lo = [sum(rows[:p]) for p in range(N_BIG)]
    n = len(ps)

    def body(*refs):
        wl_ref, dest_refs = refs[0], refs[1:1 + n]
        sends, recvs = refs[1 + n:1 + 2 * n], refs[1 + 2 * n:1 + 3 * n]
        token = refs[-1]
        _chips_handshake()
        x, y, c = _mesh_pos()
        jme = _dev_index(x, y, c)
        for i, p in enumerate(ps):
            mine = dest_refs[i].at[pl.ds(pl.multiple_of(jme * rows[p], BF16_ROWS), rows[p]), :]
            for chip in _other_chips(x, y):
                _remote(wl_ref.at[pl.ds(lo[p], rows[p]), :], mine, sends[i], recvs[i], (*chip, c)).start()
        token[...] = jnp.zeros_like(token)

    thru = [pltpu.HBM(wl.shape, wl.dtype)] + [pltpu.HBM(dests[p].shape, BF16) for p in ps]
    res = pl.pallas_call(
        body, name="gather_start",
        out_shape=tuple([pltpu.SemaphoreType.DMA(())] * (2 * n) + thru + [jax.ShapeDtypeStruct((SUBLANE, LANE), F32)]),
        in_specs=[HBM] * (1 + n), out_specs=tuple([SEM] * (2 * n) + [HBM] * (1 + n) + [VMEM]),
        input_output_aliases={i: 2 * n + i for i in range(1 + n)},
        compiler_params=pltpu.CompilerParams(has_side_effects=FLOWS, collective_id=barrier_id))(
            _in_hbm(wl), *[_in_hbm(dests[p]) for p in ps])
    sems = [(res[i], res[n + i]) for i in range(n)]
    return sems, res[2 * n], list(res[2 * n + 1:3 * n + 1]), res[-1]


def _gather_wait(wl, dest, sems, after, r, name):
    def body(wl_ref, dest_ref, send_sem, recv_sem, after_ref, wl_out, dest_out):
        x, y, c = _mesh_pos()
        three = dest_ref.at[pl.ds(0, 3 * r), :]
        cp = _remote(three, three, send_sem, recv_sem, (x, y, 1 - c))
        cp.wait_send()
        cp.wait_recv()

    res = pl.pallas_call(
        body, name=name, out_shape=(pltpu.HBM(wl.shape, wl.dtype), pltpu.HBM(dest.shape, dest.dtype)),
        in_specs=[HBM, HBM, SEM, SEM, ANY], out_specs=(HBM, HBM), input_output_aliases={0: 0, 1: 1},
        compiler_params=pltpu.CompilerParams(has_side_effects=FLOWS))(wl, dest, sems[0], sems[1], after)
    return res[0], res[1]


def _forward_comm(dest, r):
    def descs(ins, outs, scr):
        x, y, c = _mesh_pos()
        cps = []
        for k, chip in enumerate(_other_chips(x, y)):
            blk = outs[0].at[pl.ds(pl.multiple_of(_dev_index(*chip, c) * r, BF16_ROWS), r), :]
            cps.append(_remote(blk, blk, scr[0].at[k], scr[1].at[k], (x, y, 1 - c)))
        return cps

    def start(ins, outs, scr):
        for cp in descs(ins, outs, scr):
            cp.start()

    def finish(ins, outs, scr):
        cps = descs(ins, outs, scr)
        for cp in cps:
            cp.wait_recv()
        for cp in cps:
            cp.wait_send()

    return _Comm([dest], [jax.ShapeDtypeStruct(dest.shape, dest.dtype)], {0: 0},
                 [pltpu.SemaphoreType.DMA((3,)), pltpu.SemaphoreType.DMA((3,))], start, finish)


def _forward_now(dest, r, name):
    _, (dest,) = _host_call(lambda: None, grid=(), in_specs=[], out_specs=[], out_shape=[], args=(), name=name,
                            comm=_forward_comm(dest, r))
    return dest


def _pair_comm(g, r, row0=0):
    d = g.shape[1]

    def descs(ins, outs, scr):
        x, y, c = _mesh_pos()
        chips = [(x, y)] + _other_chips(x, y)
        return [_remote(ins[0].at[pl.ds(pl.multiple_of(row0 + _dev_index(*chip, 1 - c) * r, BF16_ROWS), r), :], outs[0].at[k],
                        scr[0].at[k], scr[1].at[k], (x, y, 1 - c)) for k, chip in enumerate(chips)]

    def start(ins, outs, scr):
        for cp in descs(ins, outs, scr):
            cp.start()

    def finish(ins, outs, scr):
        cps = descs(ins, outs, scr)
        for cp in cps:
            cp.wait_recv()
        for cp in cps:
            cp.wait_send()

    comm = _Comm([g], [jax.ShapeDtypeStruct((4, r, d), BF16)], {},
                 [pltpu.SemaphoreType.DMA((4,)), pltpu.SemaphoreType.DMA((4,))], start, finish)
    return comm


def _pair_sum(g, pair, r, name, after=(), row0=0):
    d = g.shape[1]

    def body(g_ref, p_ref, *rest):
        o_ref, gbuf, pbuf, sems = rest[len(after):]
        x, y, c = _mesh_pos()
        loads = [pltpu.make_async_copy(p_ref.at[pl.ds(1, 3)], pbuf, sems.at[3])]
        for k, chip in enumerate(_other_chips(x, y)):
            j = _dev_index(*chip, c)
            loads.append(pltpu.make_async_copy(g_ref.at[pl.ds(pl.multiple_of(row0 + j * r, BF16_ROWS), r), :], gbuf.at[k], sems.at[k]))
        for cp in loads:
            cp.start()
        for cp in loads:
            cp.wait()
        for k in range(3):
            o_ref[k] = (gbuf[k].astype(F32) + pbuf[k].astype(F32)).astype(BF16)

    return pl.pallas_call(
        body, out_shape=jax.ShapeDtypeStruct((3, r, d), BF16), in_specs=[ANY] * (2 + len(after)), out_specs=VMEM,
        scratch_shapes=[pltpu.VMEM((3, r, d), BF16), pltpu.VMEM((3, r, d), BF16), pltpu.SemaphoreType.DMA((4,))],
        name=name, compiler_params=_cparams())(g, pair, *after)


def _chip_start(sums, name, barrier_id):
    n = len(sums)

    def body(*refs):
        srcs, lands = refs[:n], refs[n:2 * n]
        sends, recvs = refs[2 * n:3 * n], refs[3 * n:4 * n]
        _chips_handshake()
        x, y, c = _mesh_pos()
        for i in range(n):
            for k, chip in enumerate(_other_chips(x, y)):
                _remote(srcs[i].at[k], lands[i].at[k], sends[i], recvs[i], (*chip, c)).start()
        refs[-1][...] = jnp.zeros_like(refs[-1])

    zones = [pltpu.HBM(s.shape, s.dtype) for s in sums]
    res = pl.pallas_call(
        body, name=name,
        out_shape=tuple([pltpu.SemaphoreType.DMA(())] * (2 * n) + zones + zones + [jax.ShapeDtypeStruct((SUBLANE, LANE), F32)]),
        in_specs=[HBM] * (2 * n), out_specs=tuple([SEM] * (2 * n) + [HBM] * (2 * n) + [VMEM]),
        input_output_aliases={i: 2 * n + i for i in range(2 * n)},
        compiler_params=pltpu.CompilerParams(has_side_effects=FLOWS, collective_id=barrier_id))(
            *[_in_hbm(s) for s in sums], *[_in_hbm(lax.empty(s.shape, s.dtype)) for s in sums])
    flights = [((res[i], res[n + i]), res[2 * n + i], res[3 * n + i]) for i in range(n)]
    return flights, res[-1]


def _chip_wait(sums, land, sems, after, name):
    def body(sums_ref, land_ref, send_sem, recv_sem, after_ref, sums_out, land_out):
        x, y, c = _mesh_pos()
        cp = _remote(sums_ref, land_ref, send_sem, recv_sem, (x, y, 1 - c))
        cp.wait_send()
        cp.wait_recv()

    res = pl.pallas_call(
        body, name=name, out_shape=(pltpu.HBM(sums.shape, sums.dtype), pltpu.HBM(land.shape, land.dtype)),
        in_specs=[HBM, HBM, SEM, SEM, ANY], out_specs=(HBM, HBM), input_output_aliases={0: 0, 1: 1},
        compiler_params=pltpu.CompilerParams(has_side_effects=FLOWS))(sums, land, sems[0], sems[1], after)
    return res[0], res[1]


class _CopyThrough:
    def __init__(self, src_ref, dst_ref, dst_row0, n_rows, buf, sem_in, sem_out):
        rc = n_rows // COPY_PIECES
        piece = lambda ref, o: ref.at[pl.ds(o, rc), :]
        self.loads = [pltpu.make_async_copy(piece(src_ref, k * rc), piece(buf, k * rc), sem_in) for k in range(COPY_PIECES)]
        self.stores = [pltpu.make_async_copy(piece(buf, k * rc), piece(dst_ref, dst_row0 + k * rc), sem_out) for k in range(COPY_PIECES)]
        self.all_in = pltpu.make_async_copy(src_ref, buf, sem_in)
        self.all_out = pltpu.make_async_copy(buf, dst_ref.at[pl.ds(dst_row0, n_rows), :], sem_out)

    def load(self):
        for cp in self.loads:
            cp.start()

    def store(self):
        self.all_in.wait()
        for cp in self.stores:
            cp.start()

    def done(self):
        self.all_out.wait()


def _gather_first(shards, sm, jobs, x2, tgt2, t_rows, x0):
    d = shards[0].shape[1]
    rows = [w.shape[0] for w in shards]
    lo = [sum(rows[:p]) for p in range(N_BIG)]
    n_sems = GATHER_SEMS * len(jobs)
    seq = x2.shape[0]
    assert x0 == ROW_ALIGN and seq % ROW_ALIGN == 0 and d == N_DEV * LANE

    def body(s0, s1, s2, s3, s4, sm_ref, x_ref, tgt_ref, wl_ref, o0, o1, o2, o3, o4, sa_ref, h0_ref, tp_ref,
             wl_v, x_v, tgt_v, heads_v, sa_v, send_sems, recv_sems, ssend, srecv, local_sems, sems_in, sems_out):
        dests = (o0, o1, o2, o3, o4)
        x, y, c = _mesh_pos()
        me = (x, y, c)
        jme = _dev_index(*me)
        peers = [(x, y, 1 - c)] + [(*chip, pc) for pc in (c, 1 - c) for chip in _other_chips(x, y)]
        barrier = pltpu.get_barrier_semaphore()
        for to in peers:
            pl.semaphore_signal(barrier, inc=1, device_id=to, device_id_type=MESH)
        pl.semaphore_wait(barrier, len(peers))
        padded = [_CopyThrough(x_ref, h0_ref, x0, seq, x_v, sems_in.at[0], sems_out.at[0]),
                  _CopyThrough(tgt_ref, tp_ref, x0, seq, tgt_v, sems_in.at[1], sems_out.at[1])]
        for cp in padded:
            cp.load()
        shard_refs = (s0, s1, s2, s3, s4)
        first = sorted({j[0] for j in jobs})
        for p in first + [p for p in range(N_BIG) if p not in first]:
            wl_v[pl.ds(lo[p], rows[p]), :] = shard_refs[p][...].astype(BF16)
            if p == first[-1]:
                gather = _Gather(jobs, rows, lo, wl_v, dict(enumerate(dests)), send_sems, recv_sems)
                gather.start()
        smalls = [_remote(sm_ref, sa_ref.at[jme], ssend.at[k], srecv.at[k], to) for k, to in enumerate(peers)]
        for cp in smalls:
            cp.start()
        mine = [pltpu.make_async_copy(wl_v.at[pl.ds(lo[p], rows[p]), :],
                                      dests[p].at[pl.ds(pl.multiple_of(jme * rows[p], BF16_ROWS), rows[p]), :], local_sems.at[p])
                for p in range(N_BIG)]
        mine.append(pltpu.make_async_copy(wl_v, wl_ref, local_sems.at[N_BIG]))
        mine.append(pltpu.make_async_copy(sm_ref, sa_ref.at[jme], local_sems.at[N_BIG + 1]))
        for cp in mine:
            cp.start()
        later = [p for p in range(N_BIG) if p not in {j[0] for j in jobs}]
        own = [_remote(wl_v.at[pl.ds(lo[p], rows[p]), :], dests[p].at[pl.ds(pl.multiple_of(jme * rows[p], BF16_ROWS), rows[p]), :],
                       ssend.at[7 + i], srecv.at[7 + i], (x, y, 1 - c)) for i, p in enumerate(later)]
        for cp in own:
            cp.start()
        gather.mid()
        for cp in padded:
            cp.store()
        for cp in smalls + own:
            cp.wait_recv()
        mine[-1].wait()
        to_v = pltpu.make_async_copy(sa_ref, sa_v, local_sems.at[N_BIG + 1])
        to_v.start()
        to_v.wait()
        head, zeros = heads_v.at[0], heads_v.at[1]
        head[...] = jnp.zeros_like(head)
        zeros[...] = jnp.zeros_like(zeros)
        for j in range(N_DEV):
            head[pl.ds(x0 - N_META, N_META), pl.ds(j * LANE, LANE)] = sa_v[j, pl.ds(0, N_META), :]
        heads = [pltpu.make_async_copy(head, h0_ref.at[pl.ds(0, x0), :], local_sems.at[N_BIG + 1]),
                 pltpu.make_async_copy(zeros, tp_ref.at[pl.ds(0, x0), :], local_sems.at[N_BIG + 2])]
        for cp in heads:
            cp.start()
        gather.finish()
        for cp in smalls + own:
            cp.wait_send()
        for cp in mine[:-1] + heads:
            cp.wait()
        for cp in padded:
            cp.done()

    out_shape = [jax.ShapeDtypeStruct((sum(rows), d), BF16)]
    out_shape += [jax.ShapeDtypeStruct((N_DEV * r, d), BF16) for r in rows]
    out_shape.append(jax.ShapeDtypeStruct((N_DEV,) + sm.shape, F32))
    out_shape += [jax.ShapeDtypeStruct((t_rows, d), F32)] * 2
    res = pl.pallas_call(
        body, out_shape=out_shape, in_specs=[VMEM] * 6 + [ANY] * 2, out_specs=[ANY] * 9,
        scratch_shapes=[pltpu.VMEM((sum(rows), d), BF16), pltpu.VMEM((seq, d), F32), pltpu.VMEM((seq, d), F32),
                        pltpu.VMEM((2, ROW_ALIGN, d), F32), pltpu.VMEM((N_DEV,) + sm.shape, F32),
                        pltpu.SemaphoreType.DMA((n_sems,)), pltpu.SemaphoreType.DMA((n_sems,)),
                        pltpu.SemaphoreType.DMA((7 + N_BIG,)), pltpu.SemaphoreType.DMA((7 + N_BIG,)),
                        pltpu.SemaphoreType.DMA((N_BIG + 3,)), pltpu.SemaphoreType.DMA((2,)), pltpu.SemaphoreType.DMA((2,))],
        name="gather_first",
        compiler_params=pltpu.CompilerParams(vmem_limit_bytes=VMEM_LIMIT, collective_id=ALL_PEERS_BARRIER_ID))(*shards, sm, x2, tgt2)
    return res[0], list(res[1:1 + N_BIG]), res[1 + N_BIG], res[2 + N_BIG], res[3 + N_BIG]


def _in_proj(h0, g1, win_t, tm, comm):
    t_rows, d = h0.shape
    e = win_t.shape[0]

    def body(h_ref, g_ref, w_ref, xn_ref, hin_ref):
        h = h_ref[...]
        xn = ((h * _rstd(h)) * g_ref[...]).astype(BF16)
        xn_ref[...] = xn
        for o, n in _chunks(e, N_CHUNK):
            hin_ref[:, pl.ds(o, n)] = _dot_nt(xn, w_ref[pl.ds(o, n), :])

    return _host_call(
        body, grid=(t_rows // tm,),
        in_specs=[pl.BlockSpec((tm, d), lambda i: (i, 0)), _full((1, d)), _resident((e, d))],
        out_specs=[pl.BlockSpec((tm, d), lambda i: (i, 0)), pl.BlockSpec((tm, e), lambda i: (i, 0))],
        out_shape=[jax.ShapeDtypeStruct((t_rows, d), BF16), jax.ShapeDtypeStruct((t_rows, e), F32)],
        args=(h0, g1, win_t), name="in_proj", comm=comm)


def _tap_slot(off):
    return off % SUBLANE, (off // SUBLANE) * SUBLANE


def _fill_shifted(sh_ref, base_ref, residues, n_rows):
    for r in residues:
        if r:
            sh_ref[r] = base_ref[pl.ds(r, n_rows), :]


def _shifted_rows(pair, r, start, n):
    base_ref, sh_ref = pair
    return base_ref[pl.ds(start, n), :] if r == 0 else sh_ref[r, pl.ds(start, n), :]


def _mix_conv_fwd(hin, wa, wb, bb, wa_w, comm):
    t_rows = hin.shape[0]
    nt = wa_w // LANE
    ka, kb = wa.shape[0], wb.shape[0]
    nr = CONV_HALO + t_rows

    def body(bg_ref, cg_ref, ha_ref, val_ref, gt_ref, wa_ref, wb_ref, bb_ref, ya_ref, z_ref, base, sh):
        base[pl.ds(0, CONV_HALO), :] = jnp.zeros((CONV_HALO, LANE), F32)
        base[pl.ds(nr, SUBLANE), :] = jnp.zeros((SUBLANE, LANE), F32)

        def conv(w_ref, k_taps, b, n):
            acc = None
            for k in range(k_taps):
                r, q = _tap_slot(CONV_HALO - (k_taps - 1) + k)
                term = w_ref[pl.ds(k, 1), :] * _shifted_rows((base, sh), r, b + q, n)
                acc = term if acc is None else acc + term
            return acc

        def fill_a(b, c):
            base[pl.ds(CONV_HALO + b, CONV_CHUNK), :] = cg_ref[pl.ds(b, CONV_CHUNK), :] * ha_ref[pl.ds(b, CONV_CHUNK), :]
            return c

        _row_loop(t_rows, CONV_CHUNK, fill_a)
        _fill_shifted(sh, base, sorted({_tap_slot(CONV_HALO - (ka - 1) + k)[0] for k in range(ka)}), nr)

        def out_a(b, c):
            ya_ref[pl.ds(b, CONV_CHUNK), :] = (bg_ref[pl.ds(b, CONV_CHUNK), :] * conv(wa_ref, ka, b, CONV_CHUNK)).astype(BF16)
            return c

        _row_loop(t_rows, CONV_CHUNK, out_a)

        def fill_b(b, c):
            base[pl.ds(CONV_HALO + b, CONV_CHUNK), :] = (val_ref[pl.ds(b, CONV_CHUNK), :]
                                                          * jax.nn.sigmoid(gt_ref[pl.ds(b, CONV_CHUNK), :]))
            return c

        _row_loop(t_rows, CONV_CHUNK, fill_b)
        _fill_shifted(sh, base, range(SUBLANE), nr)

        def out_b(b, c):
            z_ref[pl.ds(b, CONV_CHUNK), :] = conv(wb_ref, kb, b, CONV_CHUNK) + bb_ref[...]
            return c

        _row_loop(t_rows, CONV_CHUNK, out_b)

    def col(g):
        return pl.BlockSpec((t_rows, LANE), lambda i, g=g: (0, g * nt + i))

    tile = lambda rows: pl.BlockSpec((rows, LANE), lambda i: (0, i))
    return _host_call(
        body, grid=(nt,),
        in_specs=[col(0), col(1), col(2), col(3), col(4), tile(ka), tile(kb), tile(1)],
        out_specs=[tile(t_rows), tile(t_rows)],
        out_shape=[jax.ShapeDtypeStruct((t_rows, wa_w), BF16), jax.ShapeDtypeStruct((t_rows, wa_w), F32)],
        scratch_shapes=[pltpu.VMEM((nr + SUBLANE, LANE), F32), pltpu.VMEM((SUBLANE, nr, LANE), F32)],
        args=(hin, hin, hin, hin, hin, wa, wb, bb), name="mix_conv_fwd", comm=comm)


def _ln_parts(z, lg, lb):
    mu = jnp.mean(z, axis=-1, keepdims=True)
    zc = z - mu
    rstd = lax.rsqrt(jnp.mean(zc * zc, axis=-1, keepdims=True) + LN_EPS)
    zh = zc * rstd
    return zh, rstd, zh * lg + lb


def _out_proj(ya, z, lg, lb, w_out, h0, g2, g3, tm, comm):
    t_rows, d = h0.shape
    w = z.shape[1]

    def body(ya_ref, z_ref, lg_ref, lb_ref, w_ref, h0_ref, g2_ref, g3_ref, y_ref, mix_ref, h1_ref, xn2_ref):
        _, _, ln = _ln_parts(z_ref[...], lg_ref[...], lb_ref[...])
        y_ref[:, pl.ds(0, w)] = ya_ref[...]
        y_ref[:, pl.ds(w, w)] = (ln * jax.nn.sigmoid(ln)).astype(BF16)
        mix = _dot_nn(y_ref[...], w_ref[...])
        mix_ref[...] = mix
        h1 = h0_ref[...] + (mix * _rstd(mix)) * g2_ref[...]
        h1_ref[...] = h1
        xn2_ref[...] = ((h1 * _rstd(h1)) * g3_ref[...]).astype(BF16)

    blk = pl.BlockSpec((tm, d), lambda i: (i, 0))
    half = pl.BlockSpec((tm, w), lambda i: (i, 0))
    return _host_call(
        body, grid=(t_rows // tm,),
        in_specs=[half, half, _full((1, w)), _full((1, w)), _resident(w_out.shape), blk, _full((1, d)), _full((1, d))],
        out_specs=[blk, blk, blk, blk],
        out_shape=[jax.ShapeDtypeStruct((t_rows, d), BF16), jax.ShapeDtypeStruct((t_rows, d), F32),
                   jax.ShapeDtypeStruct((t_rows, d), F32), jax.ShapeDtypeStruct((t_rows, d), BF16)],
        args=(ya, z, lg, lb, w_out, h0, g2, g3), name="out_proj", comm=comm)


def _gate_up(xn2, wg_t, wu_t, tm, comm):
    t_rows, d = xn2.shape
    f = wg_t.shape[0]

    def body(x_ref, wg_ref, wu_ref, ga_ref, gu_ref, s_ref):
        xn = x_ref[...]
        for o, n in _chunks(f, N_CHUNK):
            a = _dot_nt(xn, wg_ref[pl.ds(o, n), :])
            u = _dot_nt(xn, wu_ref[pl.ds(o, n), :])
            sig = jax.nn.sigmoid(a)
            silu = a * sig
            s = silu * u
            gu_ref[:, pl.ds(o, n)] = silu.astype(BF16)
            ga_ref[:, pl.ds(o, n)] = ((u - s) * sig + s).astype(BF16)
            s_ref[:, pl.ds(o, n)] = s.astype(BF16)

    blk = pl.BlockSpec((tm, f), lambda i: (i, 0))
    return _host_call(
        body, grid=(t_rows // tm,),
        in_specs=[pl.BlockSpec((tm, d), lambda i: (i, 0)), _resident((f, d)), _resident((f, d))],
        out_specs=[blk, blk, blk], out_shape=[jax.ShapeDtypeStruct((t_rows, f), BF16)] * 3,
        args=(xn2, wg_t, wu_t), name="gate_up", comm=comm)


def _down_loss(s, wd, h1, tgt, g4, tm, x0):
    t_rows, d = h1.shape
    f = wd.shape[0]

    def body(s_ref, w_ref, h1_ref, tgt_ref, g4_ref, dh2_ref, dff_ref, dg4_ref, loss_ref):
        i = pl.program_id(0)
        ff = _dot_nn(s_ref[...], w_ref[...])
        r4 = _rstd(ff)
        fh = ff * r4
        g4 = g4_ref[...]
        h2 = h1_ref[...] + fh * g4
        row = i * tm + lax.broadcasted_iota(jnp.int32, (tm, 1), 0)
        diff = jnp.where(row >= x0, h2 - tgt_ref[...], 0.0)
        dh2 = diff / d
        dh2_ref[...] = dh2
        dff_ref[...] = _rms_bwd(dh2 * g4, fh, r4).astype(BF16)
        _acc_rows(dg4_ref, dh2 * fh, i == 0)
        _acc_rows(loss_ref, diff * diff, i == 0)

    blk = pl.BlockSpec((tm, d), lambda i: (i, 0))
    res, _ = _host_call(
        body, grid=(t_rows // tm,),
        in_specs=[pl.BlockSpec((tm, f), lambda i: (i, 0)), _resident((f, d)), blk, blk, _full((1, d))],
        out_specs=[blk, blk, _full((1, d)), _full((1, d))],
        out_shape=[jax.ShapeDtypeStruct((t_rows, d), F32), jax.ShapeDtypeStruct((t_rows, d), BF16),
                   jax.ShapeDtypeStruct((1, d), F32), jax.ShapeDtypeStruct((1, d), F32)],
        args=(s, wd, h1, tgt, g4), name="down_loss")
    return res


def _bwd_down(dff, wd, ga, gu, tm, comm):
    t_rows, d = dff.shape
    f = wd.shape[0]

    def body(dff_ref, w_ref, ga_ref, gu_ref, dau_ref):
        dff_v = dff_ref[...]
        for o, n in _chunks(f, N_CHUNK):
            ds = _dot_nt(dff_v, w_ref[pl.ds(o, n), :]).astype(BF16)
            dau_ref[0, :, pl.ds(o, n)] = ds * ga_ref[:, pl.ds(o, n)]
            dau_ref[1, :, pl.ds(o, n)] = ds * gu_ref[:, pl.ds(o, n)]

    blk = pl.BlockSpec((tm, f), lambda i: (i, 0))
    (dau,), extra = _host_call(
        body, grid=(t_rows // tm,),
        in_specs=[pl.BlockSpec((tm, d), lambda i: (i, 0)), _resident((f, d)), blk, blk],
        out_specs=[pl.BlockSpec((2, tm, f), lambda i: (0, i, 0))], out_shape=[jax.ShapeDtypeStruct((2, t_rows, f), BF16)],
        args=(dff, wd, ga, gu), name="bwd_down", comm=comm)
    return dau, extra


def _wgrad(a, b, name, after=()):
    d = b.shape[1]
    t_rows = b.shape[0]
    stacked = a.ndim == 3
    n = a.shape[-1]
    groups = a.shape[0] if stacked else 1
    steps = 1 if stacked else 2
    tile = max(t for t in range(LANE, min(n // steps, WGRAD_TILE_MAX) + 1, LANE) if n % t == 0)
    tiles = n // tile

    def body(a_ref, b_ref, o_ref):
        o_ref[...] = lax.dot_general(a_ref[...], b_ref[...], (((0,), (0,)), ((), ())),
                                     preferred_element_type=F32).astype(BF16)

    if stacked:
        a_spec = pl.BlockSpec((None, t_rows, tile), lambda g, i: (g, 0, i))
    else:
        a_spec = pl.BlockSpec((t_rows, tile), lambda g, i: (0, i))
    res, _ = _host_call(
        body, grid=(groups, tiles), in_specs=[a_spec, _resident((t_rows, d))],
        out_specs=[pl.BlockSpec((tile, d), lambda g, i: (g * tiles + i, 0))],
        out_shape=[jax.ShapeDtypeStruct((groups * n, d), BF16)], args=(a, b), name=name, after=after)
    return res[0]


def _bwd_ffn_in(dau, wg_t, wu_t, h1, dh2, g3, tm, comm):
    t_rows, d = h1.shape
    f = wg_t.shape[0]

    def body(dau_ref, wg_ref, wu_ref, h1_ref, dh2_ref, g3_ref, dh1_ref, dg3_ref):
        dxn2 = _dot_nn(dau_ref[0], wg_ref[...]) + _dot_nn(dau_ref[1], wu_ref[...])
        h1 = h1_ref[...]
        r3 = _rstd(h1)
        h1h = h1 * r3
        _acc_rows(dg3_ref, dxn2 * h1h, pl.program_id(0) == 0)
        dh1_ref[...] = dh2_ref[...] + _rms_bwd(dxn2 * g3_ref[...], h1h, r3)

    blk = pl.BlockSpec((tm, d), lambda i: (i, 0))
    return _host_call(
        body, grid=(t_rows // tm,),
        in_specs=[pl.BlockSpec((2, tm, f), lambda i: (0, i, 0)), _resident((f, d)), _resident((f, d)), blk, blk, _full((1, d))],
        out_specs=[blk, _full((1, d))],
        out_shape=[jax.ShapeDtypeStruct((t_rows, d), F32), jax.ShapeDtypeStruct((1, d), F32)],
        args=(dau, wg_t, wu_t, h1, dh2, g3), name="bwd_ffn_in", comm=comm)


def _bwd_out_proj(dh1, mix, w_out, g2, z, lg, lb, tm, after):
    t_rows, d = dh1.shape
    w = z.shape[1]

    def body(dh1_ref, mix_ref, w_ref, g2_ref, z_ref, lg_ref, lb_ref, dmix_ref, dya_ref, dz_ref, dg2_ref, dlg_ref, dlb_ref, dbb_ref):
        first = pl.program_id(0) == 0
        mix = mix_ref[...]
        r2 = _rstd(mix)
        mh = mix * r2
        dh1 = dh1_ref[...]
        _acc_rows(dg2_ref, dh1 * mh, first)
        dmix = _rms_bwd(dh1 * g2_ref[...], mh, r2).astype(BF16)
        dmix_ref[...] = dmix
        dy = _dot_nt(dmix, w_ref[...])
        dya_ref[...] = dy[:, :w]
        lg = lg_ref[...]
        zh, rstd, ln = _ln_parts(z_ref[...], lg, lb_ref[...])
        dln = dy[:, w:] * _silu_grad(ln, jax.nn.sigmoid(ln))
        _acc_rows(dlg_ref, dln * zh, first)
        _acc_rows(dlb_ref, dln, first)
        dzh = dln * lg
        dz = rstd * (dzh - jnp.mean(dzh, axis=-1, keepdims=True) - zh * jnp.mean(dzh * zh, axis=-1, keepdims=True))
        dz_ref[...] = dz
        _acc_rows(dbb_ref, dz, first)

    blk = pl.BlockSpec((tm, d), lambda i: (i, 0))
    half = pl.BlockSpec((tm, w), lambda i: (i, 0))
    vec = _full((1, w))
    res, _ = _host_call(
        body, grid=(t_rows // tm,), in_specs=[blk, blk, _resident(w_out.shape), _full((1, d)), half, vec, vec],
        out_specs=[blk, half, half, _full((1, d)), vec, vec, vec],
        out_shape=[jax.ShapeDtypeStruct((t_rows, d), BF16), jax.ShapeDtypeStruct((t_rows, w), F32),
                   jax.ShapeDtypeStruct((t_rows, w), F32), jax.ShapeDtypeStruct((1, d), F32)]
        + [jax.ShapeDtypeStruct((1, w), F32)] * 3,
        args=(dh1, mix, w_out, g2, z, lg, lb), name="bwd_out_proj", after=after)
    return res


def _mix_conv_bwd(hin, dy, dz, wa, wb, wa_w, comm):
    t_rows = hin.shape[0]
    nt = wa_w // LANE
    ka, kb = wa.shape[0], wb.shape[0]
    nr = CONV_HALO + t_rows
    kb_rows = -(-kb // SUBLANE) * SUBLANE

    def body(bg_ref, cg_ref, ha_ref, val_ref, gt_ref, dya_ref, dz_ref, wa_ref, wb_ref,
             dh_ref, dwa_ref, dwb_ref, base, sh, based, shd, tmp, wbc):
        zeros = lambda n: jnp.zeros((n, LANE), F32)
        base[pl.ds(0, CONV_HALO), :] = zeros(CONV_HALO)
        base[pl.ds(nr, SUBLANE), :] = zeros(SUBLANE)
        based[pl.ds(t_rows, CONV_HALO + SUBLANE), :] = zeros(CONV_HALO + SUBLANE)

        def fwd_slot(k_taps, k):
            return _tap_slot(CONV_HALO - (k_taps - 1) + k)

        def bwd_slot(k_taps, k):
            return _tap_slot(k_taps - 1 - k)

        def conv(w_ref, k_taps, src, slot, b, n):
            acc = None
            for k in range(k_taps):
                r, q = slot(k_taps, k)
                term = w_ref[pl.ds(k, 1), :] * _shifted_rows(src, r, b + q, n)
                acc = term if acc is None else acc + term
            return acc

        def by_residue(k_taps, slot):
            groups = {}
            for k in range(k_taps):
                r, q = slot(k_taps, k)
                groups.setdefault(r, []).append((k, q // SUBLANE))
            return groups

        def wgrad_loop(w_ref, k_taps):
            n_sub = WGRAD_ROWS // SUBLANE
            for k in range(k_taps):
                wbc[k] = jnp.broadcast_to(w_ref[pl.ds(k, 1), :], (SUBLANE, LANE))
            fwd, bwd = by_residue(k_taps, fwd_slot), by_residue(k_taps, bwd_slot)

            def window(src, r, taps, b):
                span = n_sub + max(qi for _, qi in taps)
                return [_shifted_rows(src, r, b + SUBLANE * i, SUBLANE) for i in range(span)]

            def step(b, accs):
                accs = list(accs)
                dv = [based[pl.ds(b + SUBLANE * j, SUBLANE), :] for j in range(n_sub)]
                for r, taps in fwd.items():
                    win = window((base, sh), r, taps, b)
                    for k, qi in taps:
                        t = dv[0] * win[qi]
                        for j in range(1, n_sub):
                            t = t + dv[j] * win[qi + j]
                        accs[k] = accs[k] + t
                outs = [None] * n_sub
                for r, taps in bwd.items():
                    win = window((based, shd), r, taps, b)
                    for k, qi in taps:
                        wk = wbc[k]
                        for j in range(n_sub):
                            term = wk * win[qi + j]
                            outs[j] = term if outs[j] is None else outs[j] + term
                for j in range(n_sub):
                    tmp[pl.ds(b + SUBLANE * j, SUBLANE), :] = outs[j]
                return tuple(accs)

            return _row_loop(t_rows, WGRAD_ROWS, step, tuple(zeros(SUBLANE) for _ in range(k_taps)))

        def store_taps(ref, accs, rows):
            for k, acc in enumerate(accs):
                ref[pl.ds(k, 1), :] = jnp.sum(acc, axis=0, keepdims=True)
            if rows > len(accs):
                ref[pl.ds(len(accs), rows - len(accs)), :] = zeros(rows - len(accs))

        def fill_a(b, c):
            sl = pl.ds(b, CONV_CHUNK)
            base[pl.ds(CONV_HALO + b, CONV_CHUNK), :] = cg_ref[sl, :] * ha_ref[sl, :]
            based[sl, :] = dya_ref[sl, :] * bg_ref[sl, :]
            return c

        _row_loop(t_rows, CONV_CHUNK, fill_a)
        _fill_shifted(sh, base, sorted({fwd_slot(ka, k)[0] for k in range(ka)}), nr)
        _fill_shifted(shd, based, sorted({bwd_slot(ka, k)[0] for k in range(ka)}), nr)

        def d_bgate(b, c):
            sl = pl.ds(b, CONV_CHUNK)
            dh_ref[0, sl, :] = (dya_ref[sl, :] * conv(wa_ref, ka, (base, sh), fwd_slot, b, CONV_CHUNK)).astype(BF16)
            return c

        _row_loop(t_rows, CONV_CHUNK, d_bgate)
        store_taps(dwa_ref, wgrad_loop(wa_ref, ka), SUBLANE)

        def d_ch(b, c):
            sl = pl.ds(b, CONV_CHUNK)
            dua = tmp[sl, :]
            dh_ref[1, sl, :] = (dua * ha_ref[sl, :]).astype(BF16)
            dh_ref[2, sl, :] = (dua * cg_ref[sl, :]).astype(BF16)
            return c

        _row_loop(t_rows, CONV_CHUNK, d_ch)

        def fill_b(b, c):
            sl = pl.ds(b, CONV_CHUNK)
            base[pl.ds(CONV_HALO + b, CONV_CHUNK), :] = val_ref[sl, :] * jax.nn.sigmoid(gt_ref[sl, :])
            based[sl, :] = dz_ref[sl, :]
            return c

        _row_loop(t_rows, CONV_CHUNK, fill_b)
        _fill_shifted(sh, base, range(SUBLANE), nr)
        _fill_shifted(shd, based, range(SUBLANE), nr)
        store_taps(dwb_ref, wgrad_loop(wb_ref, kb), kb_rows)

        def d_glu(b, c):
            sl = pl.ds(b, CONV_CHUNK)
            dgg = tmp[sl, :]
            sig = jax.nn.sigmoid(gt_ref[sl, :])
            dh_ref[3, sl, :] = (dgg * sig).astype(BF16)
            dh_ref[4, sl, :] = (dgg * val_ref[sl, :] * (sig * (1.0 - sig))).astype(BF16)
            return c

        _row_loop(t_rows, CONV_CHUNK, d_glu)

    def col(g):
        return pl.BlockSpec((t_rows, LANE), lambda i, g=g: (0, g * nt + i))

    tile = lambda rows: pl.BlockSpec((rows, LANE), lambda i: (0, i))
    return _host_call(
        body, grid=(nt,),
        in_specs=[col(0), col(1), col(2), col(3), col(4), tile(t_rows), tile(t_rows), tile(ka), tile(kb)],
        out_specs=[pl.BlockSpec((5, t_rows, LANE), lambda i: (0, 0, i)), tile(SUBLANE), tile(kb_rows)],
        out_shape=[jax.ShapeDtypeStruct((5, t_rows, wa_w), BF16), jax.ShapeDtypeStruct((SUBLANE, wa_w), F32),
                   jax.ShapeDtypeStruct((kb_rows, wa_w), F32)],
        scratch_shapes=[pltpu.VMEM((nr + SUBLANE, LANE), F32), pltpu.VMEM((SUBLANE, nr, LANE), F32),
                        pltpu.VMEM((nr + SUBLANE, LANE), F32), pltpu.VMEM((SUBLANE, nr, LANE), F32),
                        pltpu.VMEM((t_rows, LANE), F32), pltpu.VMEM((kb_rows, SUBLANE, LANE), F32)],
        args=(hin, hin, hin, hin, hin, dy, dz, wa, wb), name="mix_conv_bwd", comm=comm)


def _bwd_in_proj(dh5, win_t, h0, dh1, g1, tm, x0, comm):
    t_rows, d = h0.shape
    groups, _, w = dh5.shape
    assert x0 <= tm

    def body(dh_ref, w_ref, h0_ref, dh1_ref, g1_ref, dh0_ref, dg1_ref, dmeta_ref):
        dxn1 = None
        for g in range(groups):
            part = _dot_nn(dh_ref[g], w_ref[pl.ds(g * w, w), :])
            dxn1 = part if dxn1 is None else dxn1 + part
        h0 = h0_ref[...]
        r1 = _rstd(h0)
        h0h = h0 * r1
        _acc_rows(dg1_ref, dxn1 * h0h, pl.program_id(0) == 0)
        dh0_ref[...] = dh1_ref[...] + _rms_bwd(dxn1 * g1_ref[...], h0h, r1)

        @pl.when(pl.program_id(0) == 0)
        def _():
            dmeta_ref[...] = dh0_ref[pl.ds(x0 - N_META, N_META), :]

    blk = pl.BlockSpec((tm, d), lambda i: (i, 0))
    return _host_call(
        body, grid=(t_rows // tm,),
        in_specs=[pl.BlockSpec((groups, tm, w), lambda i: (0, i, 0)), _resident(win_t.shape), blk, blk, _full((1, d))],
        out_specs=[blk, _full((1, d)), _full((N_META, d))],
        out_shape=[jax.ShapeDtypeStruct((t_rows, d), F32), jax.ShapeDtypeStruct((1, d), F32), jax.ShapeDtypeStruct((N_META, d), F32)],
        args=(dh5, win_t, h0, dh1, g1), name="bwd_in_proj", comm=comm)


def _pair_small(smalls, d):
    (dmeta, dg1, dg2, dg3, dg4, dbb, dlg, dlb, lossv, dwa, dwb) = smalls
    half = d // 2
    kb_rows = dwb.shape[0]

    def body(dmeta_ref, dg1_ref, dg2_ref, dg3_ref, dg4_ref, dbb_ref, dlg_ref, dlb_ref, loss_ref, dwa_ref, dwb_ref,
             sums_ref, pbuf, psib, ps_send, ps_recv):
        x, y, c = _mesh_pos()
        _pair_handshake()
        pbuf[...] = jnp.zeros_like(pbuf)
        pbuf[pl.ds(0, N_META), :] = dmeta_ref[...]
        for row, ref in ((16, dg1_ref), (17, dg2_ref), (18, dg3_ref), (19, dg4_ref)):
            pbuf[pl.ds(row, 1), :] = ref[...]
        pbuf[pl.ds(20, 1), pl.ds(0, half)] = dbb_ref[...]
        pbuf[pl.ds(20, 1), pl.ds(half, half)] = dlg_ref[...]
        pbuf[pl.ds(21, 1), pl.ds(0, half)] = dlb_ref[...]
        lv = loss_ref[...]
        pbuf[pl.ds(21, 1), pl.ds(half, half)] = lv[:, :half] + lv[:, half:]
        pbuf[pl.ds(24, SUBLANE), pl.ds(0, half)] = dwa_ref[...]
        pbuf[pl.ds(32, kb_rows), pl.ds(0, half)] = dwb_ref[...]
        to_sib = _remote(pbuf, psib, ps_send.at[0], ps_recv.at[0], (x, y, 1 - c))
        to_sib.start()
        to_sib.wait_recv()
        s = pbuf[...] + psib[...]
        for k in range(3):
            sums_ref[k] = s
        to_sib.wait_send()

    return pl.pallas_call(
        body, out_shape=jax.ShapeDtypeStruct((3, SMALL_ROWS, d), F32), in_specs=[VMEM] * 11, out_specs=VMEM,
        scratch_shapes=[pltpu.VMEM((SMALL_ROWS, d), F32), pltpu.VMEM((SMALL_ROWS, d), F32),
                        pltpu.SemaphoreType.DMA((1,)), pltpu.SemaphoreType.DMA((1,))],
        name="pair_small", compiler_params=pltpu.CompilerParams(vmem_limit_bytes=VMEM_LIMIT, collective_id=PAIR_BARRIER_ID))(*smalls)


def _total_small(own, others):
    _, r, d = own.shape
    half, cols = d // 2, d // N_DEV

    def body(own_ref, others_ref, tot_ref, meta_ref, g1_ref, g2_ref, g3_ref, g4_ref, dbb_ref, dlg_ref, dlb_ref, loss_ref, chip_p):
        x, y, c = _mesh_pos()
        chip_p[2 * x + y] = own_ref[0]
        for k, (cx, cy) in enumerate(_other_chips(x, y)):
            chip_p[2 * cx + cy] = others_ref[k]
        tot_ref[...] = ((chip_p[0] + chip_p[1]) + chip_p[2]) + chip_p[3]
        meta_ref[...] = tot_ref[pl.ds(0, N_META), pl.ds(pl.multiple_of(_dev_index(x, y, c) * cols, LANE), cols)]
        for row, ref in ((16, g1_ref), (17, g2_ref), (18, g3_ref), (19, g4_ref)):
            ref[...] = tot_ref[pl.ds(row, 1), :]
        dbb_ref[...] = tot_ref[pl.ds(20, 1), pl.ds(0, half)]
        dlg_ref[...] = tot_ref[pl.ds(20, 1), pl.ds(half, half)]
        dlb_ref[...] = tot_ref[pl.ds(21, 1), pl.ds(0, half)]
        loss_ref[...] = (0.5 / d) * jnp.sum(tot_ref[pl.ds(21, 1), pl.ds(half, half)], axis=-1, keepdims=True)

    row = lambda n: jax.ShapeDtypeStruct((1, n), F32)
    return pl.pallas_call(
        body, out_shape=[jax.ShapeDtypeStruct((r, d), F32), jax.ShapeDtypeStruct((N_META, cols), F32), row(d), row(d), row(d), row(d),
                         row(half), row(half), row(half), row(1)],
        scratch_shapes=[pltpu.VMEM((4, r, d), F32)], name="total_small", compiler_params=_cparams())(own, others)


def _adamw(w, g, m, v):
    m = ADAM_B1 * m + (1.0 - ADAM_B1) * g
    v = ADAM_B2 * v + (1.0 - ADAM_B2) * jnp.square(g)
    m_hat = m / (1.0 - ADAM_B1 ** ADAM_STEP)
    v_hat = v / (1.0 - ADAM_B2 ** ADAM_STEP)
    delta = -ADAM_LR * (m_hat / (jnp.sqrt(v_hat) + ADAM_EPS) + ADAM_WD * w)
    return delta, m, v


def _adam_big(g, pair, part, w, m, v, name, row0=0):
    r, d = w.shape
    cols = d // ADAM_COL_BLOCKS
    assert row0 % r == 0

    def body(me_ref, g_ref, pair_ref, part_ref, w_ref, m_ref, v_ref, go_ref, d_ref, mo_ref, vo_ref):
        g = g_ref[...].astype(F32) + pair_ref[...].astype(F32)
        for k in range(3):
            g = g + part_ref[k].astype(F32)
        go_ref[...] = g
        d_ref[...], mo_ref[...], vo_ref[...] = _adamw(w_ref[...], g, m_ref[...], v_ref[...])

    blk = pl.BlockSpec((r, cols), lambda i, me_ref: (0, i))
    grid_spec = pltpu.PrefetchScalarGridSpec(
        num_scalar_prefetch=1, grid=(ADAM_COL_BLOCKS,),
        in_specs=[pl.BlockSpec((r, cols), lambda i, me_ref: (me_ref[0], i)),
                  pl.BlockSpec((None, r, cols), lambda i, me_ref: (0, 0, i)),
                  pl.BlockSpec((3, r, cols), lambda i, me_ref: (0, 0, i)), blk, blk, blk],
        out_specs=[blk, blk, blk, blk])
    me = jnp.reshape(_dev_index(*_mesh_pos()) + row0 // r, (1,)).astype(jnp.int32)
    return pl.pallas_call(body, out_shape=[jax.ShapeDtypeStruct((r, d), F32)] * 4, grid_spec=grid_spec, name=name,
                          compiler_params=_cparams(1))(me, g, pair, part, w, m, v)


def _adam_small(gs, ws, ms, vs):
    n = len(gs)

    def body(*refs):
        ins, outs = refs[:4 * n], refs[4 * n:]
        for i in range(n):
            g = ins[i][...]
            delta, m, v = _adamw(ins[n + i][...], g, ins[2 * n + i][...], ins[3 * n + i][...])
            outs[i][...] = delta
            outs[n + i][...] = m
            outs[2 * n + i][...] = v

    shapes = [jax.ShapeDtypeStruct(w.shape, F32) for w in ws]
    return pl.pallas_call(body, out_shape=shapes * 3, name="adam_small", compiler_params=_cparams())(*gs, *ws, *ms, *vs)


def kernel(x, meta_tokens, pre_mix_norm, w_in, conv_a_w, conv_b_w, conv_b_bias, ln_b_gain, ln_b_bias, w_out, post_mix_norm, pre_ffn_norm, w_gate, w_up, w_down, post_ffn_norm, loss_target, m_meta_tokens, m_pre_mix_norm, m_w_in, m_conv_a_w, m_conv_b_w, m_conv_b_bias, m_ln_b_gain, m_ln_b_bias, m_w_out, m_post_mix_norm, m_pre_ffn_norm, m_w_gate, m_w_up, m_w_down, m_post_ffn_norm, v_meta_tokens, v_pre_mix_norm, v_w_in, v_conv_a_w, v_conv_b_w, v_conv_b_bias, v_ln_b_gain, v_ln_b_bias, v_w_out, v_post_mix_norm, v_pre_ffn_norm, v_w_gate, v_w_up, v_w_down, v_post_ffn_norm):
    _, seq, d = x.shape
    ka, ca_loc = conv_a_w.shape[1:]
    kb, cb_loc = conv_b_w.shape[1:]
    wa_w = ca_loc * N_DEV
    assert cb_loc == ca_loc and wa_w % LANE == 0 and w_in.shape[2] * N_DEV == 5 * wa_w and 2 * wa_w == d
    pad = (-(N_META + seq)) % ROW_ALIGN
    x0 = pad + N_META
    t_rows = x0 + seq
    assert t_rows % (N_ROW_BLOCKS * BF16_ROWS) == 0 and t_rows % CONV_CHUNK == 0 and d % LANE == 0
    tm = t_rows // N_ROW_BLOCKS
    tm2 = t_rows // 2
    me = _dev_index(*_mesh_pos())

    def as_rows(w_in_like, w_out_like, w_gate_like, w_up_like, w_down_like):
        return (w_in_like[0].T, w_out_like[0], w_gate_like[0].T, w_up_like[0].T, w_down_like[0])

    w_loc = as_rows(w_in, w_out, w_gate, w_up, w_down)
    rows = [w.shape[0] for w in w_loc]
    assert all(r % ADD_CHUNK == 0 for r in rows)
    P_IN, P_OUT, P_GATE, P_UP, P_DOWN = range(N_BIG)

    assert meta_tokens.shape == (16, LANE) and ka <= 8 and 24 + kb <= SM_ROWS and ca_loc <= LANE
    rows_of = lambda a, n: jnp.pad(a, ((0, n - a.shape[0]), (0, LANE - a.shape[1])))
    sm = jnp.concatenate([meta_tokens, rows_of(conv_a_w[0], 8), rows_of(conv_b_w[0], SM_ROWS - 24)], axis=0)
    wl, wfull, sm_all, h0, tgt = _gather_first(w_loc, sm, [(P_IN, 0, rows[P_IN])], x[0], loss_target[0], t_rows, x0)
    wa =jnp.transpose(sm_all[:, 16:16 + ka, 0:ca_loc], (1, 0, 2)).reshape(ka, wa_w)
    wb = jnp.transpose(sm_all[:, 24:24 + kb, 0:cb_loc], (1, 0, 2)).reshape(kb, wa_w)

    later = (P_OUT, P_GATE, P_UP, P_DOWN)
    sems, wl, started, _ = _gather_start(wl, wfull, later, rows, START_BARRIER_IDS[0])
    for p, arr in zip(later, started):
        wfull[p] = arr

    def arrived(p, after, name):
        nonlocal wl
        wl, wfull[p] = _gather_wait(wl, wfull[p], sems[later.index(p)], after, rows[p], name)
        return _forward_comm(wfull[p], rows[p])

    (xn1, hin), _ = _in_proj(h0, pre_mix_norm, wfull[P_IN], tm, None)
    (ya, z), (wfull[P_OUT],) = _mix_conv_fwd(hin, wa, wb, conv_b_bias, wa_w, arrived(P_OUT, hin, "gather_wait_out"))
    (y, mix, h1, xn2), (wfull[P_GATE],) = _out_proj(ya, z, ln_b_gain, ln_b_bias, wfull[P_OUT], h0, post_mix_norm, pre_ffn_norm, tm2,
                                                    arrived(P_GATE, z, "gather_wait_gate"))
    arrived(P_UP, xn2, "gather_wait_up")
    wfull[P_UP] = _forward_now(wfull[P_UP], rows[P_UP], "forward_up")
    (ga, gu, s), _ = _gate_up(xn2, wfull[P_GATE], wfull[P_UP], tm, None)
    arrived(P_DOWN, s, "gather_wait_down")
    wfull[P_DOWN] = _forward_now(wfull[P_DOWN], rows[P_DOWN], "forward_down")
    dh2, dff, dg4, lossv = _down_loss(s, wfull[P_DOWN], h1, tgt, post_ffn_norm, tm, x0)

    gwd = _wgrad(s, dff, "wgrad_down")
    dau, (pair_d,) = _bwd_down(dff, wfull[P_DOWN], ga, gu, tm, _pair_comm(gwd, rows[P_DOWN]))
    (flight_d,), token = _chip_start([_pair_sum(gwd, pair_d, rows[P_DOWN], "pair_sum_down")], "chip_start_down", START_BARRIER_IDS[1])
    gw_gu = _wgrad(dau, xn2, "wgrad_gate_up", [token])
    up0 = N_DEV * rows[P_GATE]
    (dh1, dg3), (pair_g, pair_u) = _bwd_ffn_in(dau, wfull[P_GATE], wfull[P_UP], h1, dh2, pre_ffn_norm, tm,
                                               _merge_comms([_pair_comm(gw_gu, rows[P_GATE]), _pair_comm(gw_gu, rows[P_UP], up0)]))
    (flight_g, flight_u), token = _chip_start([_pair_sum(gw_gu, pair_g, rows[P_GATE], "pair_sum_gate"),
                                               _pair_sum(gw_gu, pair_u, rows[P_UP], "pair_sum_up", row0=up0)], "chip_start_gate_up",
                                              START_BARRIER_IDS[2])
    dmix, dya, dz, dg2, dlg, dlb, dbb = _bwd_out_proj(dh1, mix, wfull[P_OUT], post_mix_norm, z, ln_b_gain, ln_b_bias, tm, [token])
    gwo = _wgrad(y, dmix, "wgrad_out")
    (dh5, dwa, dwb), (pair_o,) = _mix_conv_bwd(hin, dya, dz, wa, wb, wa_w, _pair_comm(gwo, rows[P_OUT]))
    (flight_o,), token = _chip_start([_pair_sum(gwo, pair_o, rows[P_OUT], "pair_sum_out")], "chip_start_out", START_BARRIER_IDS[3])
    gwi = _wgrad(dh5, xn1, "wgrad_in", [token])
    (dh0, dg1, dmeta), (pair_i,) = _bwd_in_proj(dh5, wfull[P_IN], h0, dh1, pre_mix_norm, tm, x0, _pair_comm(gwi, rows[P_IN]))
    grad_x = dh0[x0:][None]
    small_sums = _pair_small((dmeta, dg1, dg2, dg3, dg4, dbb, dlg, dlb, lossv, dwa, dwb), d)
    (flight_s,), token = _chip_start([small_sums], "chip_start_small", START_BARRIER_IDS[4])
    (flight_i,), token = _chip_start([_pair_sum(gwi, pair_i, rows[P_IN], "pair_sum_in", [token])], "chip_start_in",
                                     START_BARRIER_IDS[5])

    def landed(flight, after, tag):
        sems_p, sums, land = flight
        return _chip_wait(sums, land, sems_p, after, "chip_wait_" + tag)

    m_loc = as_rows(m_w_in, m_w_out, m_w_gate, m_w_up, m_w_down)
    v_loc = as_rows(v_w_in, v_w_out, v_w_gate, v_w_up, v_w_down)
    full_grads = {P_IN: gwi, P_OUT: gwo, P_GATE: gw_gu, P_UP: gw_gu, P_DOWN: gwd}
    pairs = {P_IN: pair_i, P_OUT: pair_o, P_GATE: pair_g, P_UP: pair_u, P_DOWN: pair_d}
    flights = {P_IN: flight_i, P_OUT: flight_o, P_GATE: flight_g, P_UP: flight_u, P_DOWN: flight_d}
    names = {P_IN: "w_in", P_OUT: "w_out", P_GATE: "w_gate", P_UP: "w_up", P_DOWN: "w_down"}
    bigs = {}

    def adam_big(p, after):
        _, part = landed(flights[p], after, names[p])
        res = _adam_big(full_grads[p], pairs[p], part, w_loc[p], m_loc[p], v_loc[p], "adam_" + names[p], up0 if p == P_UP else 0)
        bigs[names[p]] = [(o.T if p in (P_IN, P_GATE, P_UP) else o)[None] for o in res]
        return res[1]

    for p in (P_DOWN, P_GATE, P_UP, P_OUT):
        token = adam_big(p, token)
    (ptot, g_meta, g_pre_mix, g_post_mix, g_pre_ffn, g_post_ffn, g_conv_bias, g_ln_gain, g_ln_bias, loss11) = _total_small(
        *landed(flight_s, token, "small"))
    loss = jnp.reshape(loss11, ())
    taps_major = lambda a: jnp.transpose(a, (1, 0, 2))
    CONV = (2, 3)
    g_small = [g_meta, g_pre_mix, lax.dynamic_slice(ptot, (24, me * ca_loc), (ka, ca_loc))[:, None, :],
               lax.dynamic_slice(ptot, (32, me * cb_loc), (kb, cb_loc))[:, None, :],
               g_conv_bias, g_ln_gain, g_ln_bias, g_post_mix, g_pre_ffn, g_post_ffn]
    w_small = [meta_tokens, pre_mix_norm, taps_major(conv_a_w), taps_major(conv_b_w), conv_b_bias, ln_b_gain, ln_b_bias,
               post_mix_norm, pre_ffn_norm, post_ffn_norm]
    m_small = [m_meta_tokens, m_pre_mix_norm, taps_major(m_conv_a_w), taps_major(m_conv_b_w), m_conv_b_bias, m_ln_b_gain,
               m_ln_b_bias, m_post_mix_norm, m_pre_ffn_norm, m_post_ffn_norm]
    v_small = [v_meta_tokens, v_pre_mix_norm, taps_major(v_conv_a_w), taps_major(v_conv_b_w), v_conv_b_bias, v_ln_b_gain,
               v_ln_b_bias, v_post_mix_norm, v_pre_ffn_norm, v_post_ffn_norm]
    small = list(_adam_small(g_small, w_small, m_small, v_small))
    n_small = len(w_small)
    for i in CONV:
        g_small[i] = taps_major(g_small[i])
        for k in range(3):
            small[k * n_small + i] = taps_major(small[k * n_small + i])
    d_small, nm_small, nv_small = small[:n_small], small[n_small:2 * n_small], small[2 * n_small:]

    adam_big(P_IN, small[0])

    def ordered(pick_small, pick_big):
        sm_it = iter(range(n_small))
        out = []
        for name in ("s", "s", "w_in", "s", "s", "s", "s", "s", "w_out", "s", "s", "w_gate", "w_up", "w_down", "s"):
            out.append(pick_small(next(sm_it)) if name == "s" else pick_big(name))
        return out

    grads = ordered(lambda i: g_small[i], lambda n: bigs[n][0])
    deltas = ordered(lambda i: d_small[i], lambda n: bigs[n][1])
    new_m = ordered(lambda i: nm_small[i], lambda n: bigs[n][2])
    new_v = ordered(lambda i: nv_small[i], lambda n: bigs[n][3])
    return (loss, grad_x, *grads, *deltas, *new_m, *new_v)
```

```python
import jax
import jax.numpy as jnp
from jax import lax
from jax.experimental import pallas as pl
from jax.experimental.pallas import tpu as pltpu

F32 = jnp.float32
BF16 = jnp.bfloat16
MESH = pl.DeviceIdType.MESH

N_META = 16
N_DEV = 8
RMS_EPS = 1e-6
LN_EPS = 1e-5
ADAM_LR = 0.001
ADAM_B1 = 0.9
ADAM_B2 = 0.999
ADAM_EPS = 1e-08
ADAM_WD = 0.01
ADAM_STEP = 10

LANE = 128
SUBLANE = 8
BF16_ROWS = 16
ROW_ALIGN = 128
N_ROW_BLOCKS = 4
CONV_HALO = 32
CONV_CHUNK = 64
WGRAD_ROWS = 32
N_CHUNK = 512
WGRAD_TILE_MAX = 1408
ADD_CHUNK = 32
ADAM_COL_BLOCKS = 2
COPY_PIECES = 4
V7X_VMEM_BYTES = 64 * 1024 * 1024
VMEM_LIMIT = V7X_VMEM_BYTES - 6 * 1024 * 1024
SMALL_ROWS = 64
SM_ROWS = 56
N_BIG = 5

ANY = pl.BlockSpec(memory_space=pl.ANY)
VMEM = pl.BlockSpec(memory_space=pltpu.VMEM)


def _cparams(n_grid_axes=0):
    sem = ("arbitrary",) * n_grid_axes if n_grid_axes else None
    return pltpu.CompilerParams(dimension_semantics=sem, vmem_limit_bytes=VMEM_LIMIT)


def _mesh_pos():
    return lax.axis_index("x"), lax.axis_index("y"), lax.axis_index("c")


def _dev_index(px, py, pc):
    return 4 * px + 2 * py + pc


def _other_chips(x, y):
    return [(1 - x, y), (x, 1 - y), (1 - x, 1 - y)]


def _full(shape):
    return pl.BlockSpec(shape, lambda *_: (0,) * len(shape))


def _resident(shape):
    return pl.BlockSpec(shape, lambda *_: (0,) * len(shape), pipeline_mode=pl.Buffered(1))


def _dot_nt(a, w):
    return lax.dot_general(a, w, (((1,), (1,)), ((), ())), preferred_element_type=F32)


def _dot_nn(a, w):
    return jnp.dot(a, w, preferred_element_type=F32)


def _chunks(n, c):
    out, o = [], 0
    while o < n:
        out.append((o, min(c, n - o)))
        o += c
    return out


def _rstd(h):
    return lax.rsqrt(jnp.mean(h * h, axis=-1, keepdims=True) + RMS_EPS)


def _rms_bwd(dyh, yh, r):
    return r * (dyh - yh * jnp.mean(dyh * yh, axis=-1, keepdims=True))


def _silu_grad(a, sig):
    return sig * (1.0 + a * (1.0 - sig))


def _acc_rows(ref, val, first):
    s = jnp.sum(val, axis=0, keepdims=True)

    @pl.when(first)
    def _():
        ref[...] = s

    @pl.when(jnp.logical_not(first))
    def _():
        ref[...] += s


def _row_loop(t_rows, chunk, fn, carry=None):
    def step(i, c):
        return fn(pl.multiple_of(i * chunk, chunk), c)

    return lax.fori_loop(0, t_rows // chunk, step, carry)


def _remote(src, dst, send_sem, recv_sem, to):
    return pltpu.make_async_remote_copy(src_ref=src, dst_ref=dst, send_sem=send_sem, recv_sem=recv_sem,
                                        device_id=to, device_id_type=MESH)


class _Comm:
    def __init__(self, inputs, out_shapes, aliases, scratch, start, finish):
        self.inputs, self.out_shapes, self.aliases, self.scratch = list(inputs), list(out_shapes), dict(aliases), list(scratch)
        self.start, self.finish = start, finish


def _merge_comms(comms):
    inputs, out_shapes, aliases, scratch, spans = [], [], {}, [], []
    for cm in comms:
        spans.append((len(inputs), len(out_shapes), len(scratch), cm))
        aliases.update({len(inputs) + k: len(out_shapes) + v for k, v in cm.aliases.items()})
        inputs += cm.inputs
        out_shapes += cm.out_shapes
        scratch += cm.scratch

    def run(which):
        def fn(ins, outs, scr):
            for i0, o0, s0, cm in spans:
                getattr(cm, which)(ins[i0:i0 + len(cm.inputs)], outs[o0:o0 + len(cm.out_shapes)], scr[s0:s0 + len(cm.scratch)])
        return fn

    return _Comm(inputs, out_shapes, aliases, scratch, run("start"), run("finish"))


def _host_call(body, *, grid, in_specs, out_specs, out_shape, args, name, scratch_shapes=(), comm=None, after=()):
    talks = comm is not None
    if comm is None:
        comm = _Comm([], [], {}, [], lambda *_: None, lambda *_: None)
    n_in, n_out, n_scr = len(args), len(out_shape), len(scratch_shapes)
    c_in, c_out = len(comm.inputs), len(comm.out_shapes)
    n_after = len(after)

    def open_comm(c_ins, c_outs, c_scr):
        if talks:
            _pair_handshake()
        comm.start(c_ins, c_outs, c_scr)

    def hosted(*refs):
        ins, c_ins = refs[:n_in], refs[n_in:n_in + c_in]
        o0 = n_in + c_in + n_after
        outs, c_outs = refs[o0:o0 + n_out], refs[o0 + n_out:o0 + n_out + c_out]
        s0 = o0 + n_out + c_out
        scr, c_scr = refs[s0:s0 + n_scr], refs[s0 + n_scr:]
        if not grid:
            open_comm(c_ins, c_outs, c_scr)
            body(*ins, *outs, *scr)
            comm.finish(c_ins, c_outs, c_scr)
            return
        first = last = None
        for a, n in enumerate(grid):
            f, l = pl.program_id(a) == 0, pl.program_id(a) == n - 1
            first = f if first is None else jnp.logical_and(first, f)
            last = l if last is None else jnp.logical_and(last, l)

        @pl.when(first)
        def _():
            open_comm(c_ins, c_outs, c_scr)

        body(*ins, *outs, *scr)

        @pl.when(last)
        def _():
            comm.finish(c_ins, c_outs, c_scr)

    sem = ("arbitrary",) * len(grid) if grid else None
    params = pltpu.CompilerParams(dimension_semantics=sem, vmem_limit_bytes=VMEM_LIMIT,
                                  collective_id=PAIR_BARRIER_ID if talks else None)
    res = pl.pallas_call(
        hosted, grid=grid, in_specs=list(in_specs) + [ANY] * (c_in + n_after), out_specs=list(out_specs) + [ANY] * c_out,
        out_shape=list(out_shape) + comm.out_shapes, scratch_shapes=list(scratch_shapes) + comm.scratch,
        input_output_aliases={n_in + k: n_out + v for k, v in comm.aliases.items()},
        name=name, compiler_params=params)(*args, *comm.inputs, *after)
    return list(res[:n_out]), list(res[n_out:])


PAIR_BARRIER_ID = 0
START_BARRIER_IDS = (1, 2, 3, 4, 5, 6)
ALL_PEERS_BARRIER_ID = 7


def _chips_handshake():
    x, y, c = _mesh_pos()
    barrier = pltpu.get_barrier_semaphore()
    for chip in _other_chips(x, y):
        pl.semaphore_signal(barrier, inc=1, device_id=(*chip, c), device_id_type=MESH)
    pl.semaphore_wait(barrier, 3)


def _pair_handshake():
    x, y, c = _mesh_pos()
    barrier = pltpu.get_barrier_semaphore()
    pl.semaphore_signal(barrier, inc=1, device_id=(x, y, 1 - c), device_id_type=MESH)
    pl.semaphore_wait(barrier, 1)


GATHER_SEMS = 10


class _Gather:
    def __init__(self, jobs, rows, lo, src_ref, dests, send_sems, recv_sems):
        x, y, c = _mesh_pos()
        me, sib = (x, y, c), (x, y, 1 - c)
        nx, ny, dg = (1 - x, y, c), (x, 1 - y, c), (1 - x, 1 - y, c)
        self.relayed, self.direct, self.relay, self.to_sib, self.sib_fwd = [], [], [], [], []
        for n, (p, r0, nr) in enumerate(jobs):
            assert nr % (2 * BF16_ROWS) == 0
            half = nr // 2

            def rows_of(dev, h, p=p, r0=r0, nr=nr, half=half):
                off, cnt = (r0, nr) if h is None else (r0 + h * half, half)
                return dests[p].at[pl.ds(pl.multiple_of(_dev_index(*dev) * rows[p] + off, BF16_ROWS), cnt), :]

            def mine(h, p=p, r0=r0, nr=nr, half=half):
                off, cnt = (r0, nr) if h is None else (r0 + h * half, half)
                return src_ref.at[pl.ds(lo[p] + off, cnt), :]

            sem = lambda k, n=n: (send_sems.at[GATHER_SEMS * n + k], recv_sems.at[GATHER_SEMS * n + k])
            self.relayed.append([_remote(mine(0), rows_of(me, 0), *sem(0), nx), _remote(mine(1), rows_of(me, 1), *sem(3), ny)])
            self.direct.append([_remote(mine(1), rows_of(me, 1), *sem(1), nx), _remote(mine(0), rows_of(me, 0), *sem(2), ny)])
            self.relay.append([_remote(rows_of(nx, 0), rows_of(nx, 0), *sem(4), ny), _remote(rows_of(ny, 1), rows_of(ny, 1), *sem(5), nx)])
            self.to_sib.append(_remote(mine(None), rows_of(me, None), *sem(6), sib))
            self.sib_fwd.append([_remote(rows_of(dev, None), rows_of(dev, None), *sem(7 + i), sib) for i, dev in enumerate((nx, ny, dg))])

    def start(self):
        for group in (self.relayed, self.direct):
            for cps in group:
                for cp in cps:
                    cp.start()
        for cp in self.to_sib:
            cp.start()

    def mid(self):
        for first, relay in zip(self.relayed, self.relay):
            for arrived, onward in zip(first, relay):
                arrived.wait_recv()
                onward.start()

    def finish(self):
        for direct, relay, fwd in zip(self.direct, self.relay, self.sib_fwd):
            for k in range(2):
                direct[k].wait_recv()
                fwd[k].start()
            for cp in relay:
                cp.wait_recv()
            fwd[2].start()
        for n in range(len(self.to_sib)):
            self.to_sib[n].wait_recv()
            for cp in self.sib_fwd[n]:
                cp.wait_recv()
            for cp in self.relayed[n] + self.direct[n] + self.relay[n] + [self.to_sib[n]] + self.sib_fwd[n]:
                cp.wait_send()


HBM = pl.BlockSpec(memory_space=pltpu.HBM)
SEM = pl.BlockSpec(memory_space=pltpu.SEMAPHORE)
FLOWS = pltpu.SideEffectType.DATAFLOW_SIDE_EFFECTING


def _in_hbm(a):
    return pltpu.with_memory_space_constraint(a, pltpu.HBM)


def _gather_start(wl, dests, ps, rows, barrier_id):
    lo = [sum(rows[:p]) for p in range(N_BIG)]
    n = len(ps)

    def body(*refs):
        wl_ref, dest_refs = refs[0], refs[1:1 + n]
        sends, recvs = refs[1 + n:1 + 2 * n], refs[1 + 2 * n:1 + 3 * n]
        token = refs[-1]
        _chips_handshake()
        x, y, c = _mesh_pos()
        jme = _dev_index(x, y, c)
        for i, p in enumerate(ps):
            mine = dest_refs[i].at[pl.ds(pl.multiple_of(jme * rows[p], BF16_ROWS), rows[p]), :]
            for chip in _other_chips(x, y):
                _remote(wl_ref.at[pl.ds(lo[p], rows[p]), :], mine, sends[i], recvs[i], (*chip, c)).start()
        token[...] = jnp.zeros_like(token)

    thru = [pltpu.HBM(wl.shape, wl.dtype)] + [pltpu.HBM(dests[p].shape, BF16) for p in ps]
    res = pl.pallas_call(
        body, name="gather_start",
        out_shape=tuple([pltpu.SemaphoreType.DMA(())] * (2 * n) + thru + [jax.ShapeDtypeStruct((SUBLANE, LANE), F32)]),
        in_specs=[HBM] * (1 + n), out_specs=tuple([SEM] * (2 * n) + [HBM] * (1 + n) + [VMEM]),
        input_output_aliases={i: 2 * n + i for i in range(1 + n)},
        compiler_params=pltpu.CompilerParams(has_side_effects=FLOWS, collective_id=barrier_id))(
            _in_hbm(wl), *[_in_hbm(dests[p]) for p in ps])
    sems = [(res[i], res[n + i]) for i in range(n)]
    return sems, res[2 * n], list(res[2 * n + 1:3 * n + 1]), res[-1]


def _gather_wait(wl, dest, sems, after, r, name):
    def body(wl_ref, dest_ref, send_sem, recv_sem, after_ref, wl_out, dest_out):
        x, y, c = _mesh_pos()
        three = dest_ref.at[pl.ds(0, 3 * r), :]
        cp = _remote(three, three, send_sem, recv_sem, (x, y, 1 - c))
        cp.wait_send()
        cp.wait_recv()

    res = pl.pallas_call(
        body, name=name, out_shape=(pltpu.HBM(wl.shape, wl.dtype), pltpu.HBM(dest.shape, dest.dtype)),
        in_specs=[HBM, HBM, SEM, SEM, ANY], out_specs=(HBM, HBM), input_output_aliases={0: 0, 1: 1},
        compiler_params=pltpu.CompilerParams(has_side_effects=FLOWS))(wl, dest, sems[0], sems[1], after)
    return res[0], res[1]


def _forward_comm(dest, r):
    def descs(ins, outs, scr):
        x, y, c = _mesh_pos()
        cps = []
        for k, chip in enumerate(_other_chips(x, y)):
            blk = outs[0].at[pl.ds(pl.multiple_of(_dev_index(*chip, c) * r, BF16_ROWS), r), :]
            cps.append(_remote(blk, blk, scr[0].at[k], scr[1].at[k], (x, y, 1 - c)))
        return cps

    def start(ins, outs, scr):
        for cp in descs(ins, outs, scr):
            cp.start()

    def finish(ins, outs, scr):
        cps = descs(ins, outs, scr)
        for cp in cps:
            cp.wait_recv()
        for cp in cps:
            cp.wait_send()

    return _Comm([dest], [jax.ShapeDtypeStruct(dest.shape, dest.dtype)], {0: 0},
                 [pltpu.SemaphoreType.DMA((3,)), pltpu.SemaphoreType.DMA((3,))], start, finish)


def _forward_now(dest, r, name):
    _, (dest,) = _host_call(lambda: None, grid=(), in_specs=[], out_specs=[], out_shape=[], args=(), name=name,
                            comm=_forward_comm(dest, r))
    return dest


def _pair_comm(g, r, row0=0):
    d = g.shape[1]

    def descs(ins, outs, scr):
        x, y, c = _mesh_pos()
        chips = [(x, y)] + _other_chips(x, y)
        return [_remote(ins[0].at[pl.ds(pl.multiple_of(row0 + _dev_index(*chip, 1 - c) * r, BF16_ROWS), r), :], outs[0].at[k],
                        scr[0].at[k], scr[1].at[k], (x, y, 1 - c)) for k, chip in enumerate(chips)]

    def start(ins, outs, scr):
        for cp in descs(ins, outs, scr):
            cp.start()

    def finish(ins, outs, scr):
        cps = descs(ins, outs, scr)
        for cp in cps:
            cp.wait_recv()
        for cp in cps:
            cp.wait_send()

    comm = _Comm([g], [jax.ShapeDtypeStruct((4, r, d), BF16)], {},
                 [pltpu.SemaphoreType.DMA((4,)), pltpu.SemaphoreType.DMA((4,))], start, finish)
    return comm


def _pair_sum(g, pair, r, name, after=(), row0=0):
    d = g.shape[1]

    def body(g_ref, p_ref, *rest):
        o_ref, gbuf, pbuf, sems = rest[len(after):]
        x, y, c = _mesh_pos()
        loads = [pltpu.make_async_copy(p_ref.at[pl.ds(1, 3)], pbuf, sems.at[3])]
        for k, chip in enumerate(_other_chips(x, y)):
            j = _dev_index(*chip, c)
            loads.append(pltpu.make_async_copy(g_ref.at[pl.ds(pl.multiple_of(row0 + j * r, BF16_ROWS), r), :], gbuf.at[k], sems.at[k]))
        for cp in loads:
            cp.start()
        for cp in loads:
            cp.wait()
        for k in range(3):
            o_ref[k] = (gbuf[k].astype(F32) + pbuf[k].astype(F32)).astype(BF16)

    return pl.pallas_call(
        body, out_shape=jax.ShapeDtypeStruct((3, r, d), BF16), in_specs=[ANY] * (2 + len(after)), out_specs=VMEM,
        scratch_shapes=[pltpu.VMEM((3, r, d), BF16), pltpu.VMEM((3, r, d), BF16), pltpu.SemaphoreType.DMA((4,))],
        name=name, compiler_params=_cparams())(g, pair, *after)


def _chip_start(sums, name, barrier_id):
    n = len(sums)

    def body(*refs):
        srcs, lands = refs[:n], refs[n:2 * n]
        sends, recvs = refs[2 * n:3 * n], refs[3 * n:4 * n]
        _chips_handshake()
        x, y, c = _mesh_pos()
        for i in range(n):
            for k, chip in enumerate(_other_chips(x, y)):
                _remote(srcs[i].at[k], lands[i].at[k], sends[i], recvs[i], (*chip, c)).start()
        refs[-1][...] = jnp.zeros_like(refs[-1])

    zones = [pltpu.HBM(s.shape, s.dtype) for s in sums]
    res = pl.pallas_call(
        body, name=name,
        out_shape=tuple([pltpu.SemaphoreType.DMA(())] * (2 * n) + zones + zones + [jax.ShapeDtypeStruct((SUBLANE, LANE), F32)]),
        in_specs=[HBM] * (2 * n), out_specs=tuple([SEM] * (2 * n) + [HBM] * (2 * n) + [VMEM]),
        input_output_aliases={i: 2 * n + i for i in range(2 * n)},
        compiler_params=pltpu.CompilerParams(has_side_effects=FLOWS, collective_id=barrier_id))(
            *[_in_hbm(s) for s in sums], *[_in_hbm(lax.empty(s.shape, s.dtype)) for s in sums])
    flights = [((res[i], res[n + i]), res[2 * n + i], res[3 * n + i]) for i in range(n)]
    return flights, res[-1]


def _chip_wait(sums, land, sems, after, name):
    def body(sums_ref, land_ref, send_sem, recv_sem, after_ref, sums_out, land_out):
        x, y, c = _mesh_pos()
        cp = _remote(sums_ref, land_ref, send_sem, recv_sem, (x, y, 1 - c))
        cp.wait_send()
        cp.wait_recv()

    res = pl.pallas_call(
        body, name=name, out_shape=(pltpu.HBM(sums.shape, sums.dtype), pltpu.HBM(land.shape, land.dtype)),
        in_specs=[HBM, HBM, SEM, SEM, ANY], out_specs=(HBM, HBM), input_output_aliases={0: 0, 1: 1},
        compiler_params=pltpu.CompilerParams(has_side_effects=FLOWS))(sums, land, sems[0], sems[1], after)
    return res[0], res[1]


class _CopyThrough:
    def __init__(self, src_ref, dst_ref, dst_row0, n_rows, buf, sem_in, sem_out):
        rc = n_rows // COPY_PIECES
        piece = lambda ref, o: ref.at[pl.ds(o, rc), :]
        self.loads = [pltpu.make_async_copy(piece(src_ref, k * rc), piece(buf, k * rc), sem_in) for k in range(COPY_PIECES)]
        self.stores = [pltpu.make_async_copy(piece(buf, k * rc), piece(dst_ref, dst_row0 + k * rc), sem_out) for k in range(COPY_PIECES)]
        self.all_in = pltpu.make_async_copy(src_ref, buf, sem_in)
        self.all_out = pltpu.make_async_copy(buf, dst_ref.at[pl.ds(dst_row0, n_rows), :], sem_out)

    def load(self):
        for cp in self.loads:
            cp.start()

    def store(self):
        self.all_in.wait()
        for cp in self.stores:
            cp.start()

    def done(self):
        self.all_out.wait()


def _gather_first(shards, sm, jobs, x2, tgt2, t_rows, x0):
    d = shards[0].shape[1]
    rows = [w.shape[0] for w in shards]
    lo = [sum(rows[:p]) for p in range(N_BIG)]
    n_sems = GATHER_SEMS * len(jobs)
    seq = x2.shape[0]
    assert x0 == ROW_ALIGN and seq % ROW_ALIGN == 0 and d == N_DEV * LANE

    def body(s0, s1, s2, s3, s4, sm_ref, x_ref, tgt_ref, wl_ref, o0, o1, o2, o3, o4, sa_ref, h0_ref, tp_ref,
             wl_v, x_v, tgt_v, heads_v, sa_v, send_sems, recv_sems, ssend, srecv, local_sems, sems_in, sems_out):
        dests = (o0, o1, o2, o3, o4)
        x, y, c = _mesh_pos()
        me = (x, y, c)
        jme = _dev_index(*me)
        peers = [(x, y, 1 - c)] + [(*chip, pc) for pc in (c, 1 - c) for chip in _other_chips(x, y)]
        barrier = pltpu.get_barrier_semaphore()
        for to in peers:
            pl.semaphore_signal(barrier, inc=1, device_id=to, device_id_type=MESH)
        pl.semaphore_wait(barrier, len(peers))
        padded = [_CopyThrough(x_ref, h0_ref, x0, seq, x_v, sems_in.at[0], sems_out.at[0]),
                  _CopyThrough(tgt_ref, tp_ref, x0, seq, tgt_v, sems_in.at[1], sems_out.at[1])]
        for cp in padded:
            cp.load()
        shard_refs = (s0, s1, s2, s3, s4)
        first = sorted({j[0] for j in jobs})
        for p in first + [p for p in range(N_BIG) if p not in first]:
            wl_v[pl.ds(lo[p], rows[p]), :] = shard_refs[p][...].astype(BF16)
            if p == first[-1]:
                gather = _Gather(jobs, rows, lo, wl_v, dict(enumerate(dests)), send_sems, recv_sems)
                gather.start()
        smalls = [_remote(sm_ref, sa_ref.at[jme], ssend.at[k], srecv.at[k], to) for k, to in enumerate(peers)]
        for cp in smalls:
            cp.start()
        mine = [pltpu.make_async_copy(wl_v.at[pl.ds(lo[p], rows[p]), :],
                                      dests[p].at[pl.ds(pl.multiple_of(jme * rows[p], BF16_ROWS), rows[p]), :], local_sems.at[p])
                for p in range(N_BIG)]
        mine.append(pltpu.make_async_copy(wl_v, wl_ref, local_sems.at[N_BIG]))
        mine.append(pltpu.make_async_copy(sm_ref, sa_ref.at[jme], local_sems.at[N_BIG + 1]))
        for cp in mine:
            cp.start()
        later = [p for p in range(N_BIG) if p not in {j[0] for j in jobs}]
        own = [_remote(wl_v.at[pl.ds(lo[p], rows[p]), :], dests[p].at[pl.ds(pl.multiple_of(jme * rows[p], BF16_ROWS), rows[p]), :],
                       ssend.at[7 + i], srecv.at[7 + i], (x, y, 1 - c)) for i, p in enumerate(later)]
        for cp in own:
            cp.start()
        gather.mid()
        for cp in padded:
            cp.store()
        for cp in smalls + own:
            cp.wait_recv()
        mine[-1].wait()
        to_v = pltpu.make_async_copy(sa_ref, sa_v, local_sems.at[N_BIG + 1])
        to_v.start()
        to_v.wait()
        head, zeros = heads_v.at[0], heads_v.at[1]
        head[...] = jnp.zeros_like(head)
        zeros[...] = jnp.zeros_like(zeros)
        for j in range(N_DEV):
            head[pl.ds(x0 - N_META, N_META), pl.ds(j * LANE, LANE)] = sa_v[j, pl.ds(0, N_META), :]
        heads = [pltpu.make_async_copy(head, h0_ref.at[pl.ds(0, x0), :], local_sems.at[N_BIG + 1]),
                 pltpu.make_async_copy(zeros, tp_ref.at[pl.ds(0, x0), :], local_sems.at[N_BIG + 2])]
        for cp in heads:
            cp.start()
        gather.finish()
        for cp in smalls + own:
            cp.wait_send()
        for cp in mine[:-1] + heads:
            cp.wait()
        for cp in padded:
            cp.done()

    out_shape = [jax.ShapeDtypeStruct((sum(rows), d), BF16)]
    out_shape += [jax.ShapeDtypeStruct((N_DEV * r, d), BF16) for r in rows]
    out_shape.append(jax.ShapeDtypeStruct((N_DEV,) + sm.shape, F32))
    out_shape += [jax.ShapeDtypeStruct((t_rows, d), F32)] * 2
    res = pl.pallas_call(
        body, out_shape=out_shape, in_specs=[VMEM] * 6 + [ANY] * 2, out_specs=[ANY] * 9,
        scratch_shapes=[pltpu.VMEM((sum(rows), d), BF16), pltpu.VMEM((seq, d), F32), pltpu.VMEM((seq, d), F32),
                        pltpu.VMEM((2, ROW_ALIGN, d), F32), pltpu.VMEM((N_DEV,) + sm.shape, F32),
                        pltpu.SemaphoreType.DMA((n_sems,)), pltpu.SemaphoreType.DMA((n_sems,)),
                        pltpu.SemaphoreType.DMA((7 + N_BIG,)), pltpu.SemaphoreType.DMA((7 + N_BIG,)),
                        pltpu.SemaphoreType.DMA((N_BIG + 3,)), pltpu.SemaphoreType.DMA((2,)), pltpu.SemaphoreType.DMA((2,))],
        name="gather_first",
        compiler_params=pltpu.CompilerParams(vmem_limit_bytes=VMEM_LIMIT, collective_id=ALL_PEERS_BARRIER_ID))(*shards, sm, x2, tgt2)
    return res[0], list(res[1:1 + N_BIG]), res[1 + N_BIG], res[2 + N_BIG], res[3 + N_BIG]


def _in_proj(h0, g1, win_t, tm, comm):
    t_rows, d = h0.shape
    e = win_t.shape[0]

    def body(h_ref, g_ref, w_ref, xn_ref, hin_ref):
        h = h_ref[...]
        xn = ((h * _rstd(h)) * g_ref[...]).astype(BF16)
        xn_ref[...] = xn
        for o, n in _chunks(e, N_CHUNK):
            hin_ref[:, pl.ds(o, n)] = _dot_nt(xn, w_ref[pl.ds(o, n), :])

    return _host_call(
        body, grid=(t_rows // tm,),
        in_specs=[pl.BlockSpec((tm, d), lambda i: (i, 0)), _full((1, d)), _resident((e, d))],
        out_specs=[pl.BlockSpec((tm, d), lambda i: (i, 0)), pl.BlockSpec((tm, e), lambda i: (i, 0))],
        out_shape=[jax.ShapeDtypeStruct((t_rows, d), BF16), jax.ShapeDtypeStruct((t_rows, e), F32)],
        args=(h0, g1, win_t), name="in_proj", comm=comm)


def _tap_slot(off):
    return off % SUBLANE, (off // SUBLANE) * SUBLANE


def _fill_shifted(sh_ref, base_ref, residues, n_rows):
    for r in residues:
        if r:
            sh_ref[r] = base_ref[pl.ds(r, n_rows), :]


def _shifted_rows(pair, r, start, n):
    base_ref, sh_ref = pair
    return base_ref[pl.ds(start, n), :] if r == 0 else sh_ref[r, pl.ds(start, n), :]


def _mix_conv_fwd(hin, wa, wb, bb, wa_w, comm):
    t_rows = hin.shape[0]
    nt = wa_w // LANE
    ka, kb = wa.shape[0], wb.shape[0]
    nr = CONV_HALO + t_rows

    def body(bg_ref, cg_ref, ha_ref, val_ref, gt_ref, wa_ref, wb_ref, bb_ref, ya_ref, z_ref, base, sh):
        base[pl.ds(0, CONV_HALO), :] = jnp.zeros((CONV_HALO, LANE), F32)
        base[pl.ds(nr, SUBLANE), :] = jnp.zeros((SUBLANE, LANE), F32)

        def conv(w_ref, k_taps, b, n):
            acc = None
            for k in range(k_taps):
                r, q = _tap_slot(CONV_HALO - (k_taps - 1) + k)
                term = w_ref[pl.ds(k, 1), :] * _shifted_rows((base, sh), r, b + q, n)
                acc = term if acc is None else acc + term
            return acc

        def fill_a(b, c):
            base[pl.ds(CONV_HALO + b, CONV_CHUNK), :] = cg_ref[pl.ds(b, CONV_CHUNK), :] * ha_ref[pl.ds(b, CONV_CHUNK), :]
            return c

        _row_loop(t_rows, CONV_CHUNK, fill_a)
        _fill_shifted(sh, base, sorted({_tap_slot(CONV_HALO - (ka - 1) + k)[0] for k in range(ka)}), nr)

        def out_a(b, c):
            ya_ref[pl.ds(b, CONV_CHUNK), :] = (bg_ref[pl.ds(b, CONV_CHUNK), :] * conv(wa_ref, ka, b, CONV_CHUNK)).astype(BF16)
            return c

        _row_loop(t_rows, CONV_CHUNK, out_a)

        def fill_b(b, c):
            base[pl.ds(CONV_HALO + b, CONV_CHUNK), :] = (val_ref[pl.ds(b, CONV_CHUNK), :]
                                                          * jax.nn.sigmoid(gt_ref[pl.ds(b, CONV_CHUNK), :]))
            return c

        _row_loop(t_rows, CONV_CHUNK, fill_b)
        _fill_shifted(sh, base, range(SUBLANE), nr)

        def out_b(b, c):
            z_ref[pl.ds(b, CONV_CHUNK), :] = conv(wb_ref, kb, b, CONV_CHUNK) + bb_ref[...]
            return c

        _row_loop(t_rows, CONV_CHUNK, out_b)

    def col(g):
        return pl.BlockSpec((t_rows, LANE), lambda i, g=g: (0, g * nt + i))

    tile = lambda rows: pl.BlockSpec((rows, LANE), lambda i: (0, i))
    return _host_call(
        body, grid=(nt,),
        in_specs=[col(0), col(1), col(2), col(3), col(4), tile(ka), tile(kb), tile(1)],
        out_specs=[tile(t_rows), tile(t_rows)],
        out_shape=[jax.ShapeDtypeStruct((t_rows, wa_w), BF16), jax.ShapeDtypeStruct((t_rows, wa_w), F32)],
        scratch_shapes=[pltpu.VMEM((nr + SUBLANE, LANE), F32), pltpu.VMEM((SUBLANE, nr, LANE), F32)],
        args=(hin, hin, hin, hin, hin, wa, wb, bb), name="mix_conv_fwd", comm=comm)


def _ln_parts(z, lg, lb):
    mu = jnp.mean(z, axis=-1, keepdims=True)
    zc = z - mu
    rstd = lax.rsqrt(jnp.mean(zc * zc, axis=-1, keepdims=True) + LN_EPS)
    zh = zc * rstd
    return zh, rstd, zh * lg + lb


def _out_proj(ya, z, lg, lb, w_out, h0, g2, g3, tm, comm):
    t_rows, d = h0.shape
    w = z.shape[1]

    def body(ya_ref, z_ref, lg_ref, lb_ref, w_ref, h0_ref, g2_ref, g3_ref, y_ref, mix_ref, h1_ref, xn2_ref):
        _, _, ln = _ln_parts(z_ref[...], lg_ref[...], lb_ref[...])
        y_ref[:, pl.ds(0, w)] = ya_ref[...]
        y_ref[:, pl.ds(w, w)] = (ln * jax.nn.sigmoid(ln)).astype(BF16)
        mix = _dot_nn(y_ref[...], w_ref[...])
        mix_ref[...] = mix
        h1 = h0_ref[...] + (mix * _rstd(mix)) * g2_ref[...]
        h1_ref[...] = h1
        xn2_ref[...] = ((h1 * _rstd(h1)) * g3_ref[...]).astype(BF16)

    blk = pl.BlockSpec((tm, d), lambda i: (i, 0))
    half = pl.BlockSpec((tm, w), lambda i: (i, 0))
    return _host_call(
        body, grid=(t_rows // tm,),
        in_specs=[half, half, _full((1, w)), _full((1, w)), _resident(w_out.shape), blk, _full((1, d)), _full((1, d))],
        out_specs=[blk, blk, blk, blk],
        out_shape=[jax.ShapeDtypeStruct((t_rows, d), BF16), jax.ShapeDtypeStruct((t_rows, d), F32),
                   jax.ShapeDtypeStruct((t_rows, d), F32), jax.ShapeDtypeStruct((t_rows, d), BF16)],
        args=(ya, z, lg, lb, w_out, h0, g2, g3), name="out_proj", comm=comm)


def _gate_up(xn2, wg_t, wu_t, tm, comm):
    t_rows, d = xn2.shape
    f = wg_t.shape[0]

    def body(x_ref, wg_ref, wu_ref, ga_ref, gu_ref, s_ref):
        xn = x_ref[...]
        for o, n in _chunks(f, N_CHUNK):
            a = _dot_nt(xn, wg_ref[pl.ds(o, n), :])
            u = _dot_nt(xn, wu_ref[pl.ds(o, n), :])
            sig = jax.nn.sigmoid(a)
            silu = a * sig
            s = silu * u
            gu_ref[:, pl.ds(o, n)] = silu.astype(BF16)
            ga_ref[:, pl.ds(o, n)] = ((u - s) * sig + s).astype(BF16)
            s_ref[:, pl.ds(o, n)] = s.astype(BF16)

    blk = pl.BlockSpec((tm, f), lambda i: (i, 0))
    return _host_call(
        body, grid=(t_rows // tm,),
        in_specs=[pl.BlockSpec((tm, d), lambda i: (i, 0)), _resident((f, d)), _resident((f, d))],
        out_specs=[blk, blk, blk], out_shape=[jax.ShapeDtypeStruct((t_rows, f), BF16)] * 3,
        args=(xn2, wg_t, wu_t), name="gate_up", comm=comm)


def _down_loss(s, wd, h1, tgt, g4, tm, x0):
    t_rows, d = h1.shape
    f = wd.shape[0]

    def body(s_ref, w_ref, h1_ref, tgt_ref, g4_ref, dh2_ref, dff_ref, dg4_ref, loss_ref):
        i = pl.program_id(0)
        ff = _dot_nn(s_ref[...], w_ref[...])
        r4 = _rstd(ff)
        fh = ff * r4
        g4 = g4_ref[...]
        h2 = h1_ref[...] + fh * g4
        row = i * tm + lax.broadcasted_iota(jnp.int32, (tm, 1), 0)
        diff = jnp.where(row >= x0, h2 - tgt_ref[...], 0.0)
        dh2 = diff / d
        dh2_ref[...] = dh2
        dff_ref[...] = _rms_bwd(dh2 * g4, fh, r4).astype(BF16)
        _acc_rows(dg4_ref, dh2 * fh, i == 0)
        _acc_rows(loss_ref, diff * diff, i == 0)

    blk = pl.BlockSpec((tm, d), lambda i: (i, 0))
    res, _ = _host_call(
        body, grid=(t_rows // tm,),
        in_specs=[pl.BlockSpec((tm, f), lambda i: (i, 0)), _resident((f, d)), blk, blk, _full((1, d))],
        out_specs=[blk, blk, _full((1, d)), _full((1, d))],
        out_shape=[jax.ShapeDtypeStruct((t_rows, d), F32), jax.ShapeDtypeStruct((t_rows, d), BF16),
                   jax.ShapeDtypeStruct((1, d), F32), jax.ShapeDtypeStruct((1, d), F32)],
        args=(s, wd, h1, tgt, g4), name="down_loss")
    return res


def _bwd_down(dff, wd, ga, gu, tm, comm):
    t_rows, d = dff.shape
    f = wd.shape[0]

    def body(dff_ref, w_ref, ga_ref, gu_ref, dau_ref):
        dff_v = dff_ref[...]
        for o, n in _chunks(f, N_CHUNK):
            ds = _dot_nt(dff_v, w_ref[pl.ds(o, n), :]).astype(BF16)
            dau_ref[0, :, pl.ds(o, n)] = ds * ga_ref[:, pl.ds(o, n)]
            dau_ref[1, :, pl.ds(o, n)] = ds * gu_ref[:, pl.ds(o, n)]

    blk = pl.BlockSpec((tm, f), lambda i: (i, 0))
    (dau,), extra = _host_call(
        body, grid=(t_rows // tm,),
        in_specs=[pl.BlockSpec((tm, d), lambda i: (i, 0)), _resident((f, d)), blk, blk],
        out_specs=[pl.BlockSpec((2, tm, f), lambda i: (0, i, 0))], out_shape=[jax.ShapeDtypeStruct((2, t_rows, f), BF16)],
        args=(dff, wd, ga, gu), name="bwd_down", comm=comm)
    return dau, extra


def _wgrad(a, b, name, after=()):
    d = b.shape[1]
    t_rows = b.shape[0]
    stacked = a.ndim == 3
    n = a.shape[-1]
    groups = a.shape[0] if stacked else 1
    steps = 1 if stacked else 2
    tile = max(t for t in range(LANE, min(n // steps, WGRAD_TILE_MAX) + 1, LANE) if n % t == 0)
    tiles = n // tile

    def body(a_ref, b_ref, o_ref):
        o_ref[...] = lax.dot_general(a_ref[...], b_ref[...], (((0,), (0,)), ((), ())),
                                     preferred_element_type=F32).astype(BF16)

    if stacked:
        a_spec = pl.BlockSpec((None, t_rows, tile), lambda g, i: (g, 0, i))
    else:
        a_spec = pl.BlockSpec((t_rows, tile), lambda g, i: (0, i))
    res, _ = _host_call(
        body, grid=(groups, tiles), in_specs=[a_spec, _resident((t_rows, d))],
        out_specs=[pl.BlockSpec((tile, d), lambda g, i: (g * tiles + i, 0))],
        out_shape=[jax.ShapeDtypeStruct((groups * n, d), BF16)], args=(a, b), name=name, after=after)
    return res[0]


def _bwd_ffn_in(dau, wg_t, wu_t, h1, dh2, g3, tm, comm):
    t_rows, d = h1.shape
    f = wg_t.shape[0]

    def body(dau_ref, wg_ref, wu_ref, h1_ref, dh2_ref, g3_ref, dh1_ref, dg3_ref):
        dxn2 = _dot_nn(dau_ref[0], wg_ref[...]) + _dot_nn(dau_ref[1], wu_ref[...])
        h1 = h1_ref[...]
        r3 = _rstd(h1)
        h1h = h1 * r3
        _acc_rows(dg3_ref, dxn2 * h1h, pl.program_id(0) == 0)
        dh1_ref[...] = dh2_ref[...] + _rms_bwd(dxn2 * g3_ref[...], h1h, r3)

    blk = pl.BlockSpec((tm, d), lambda i: (i, 0))
    return _host_call(
        body, grid=(t_rows // tm,),
        in_specs=[pl.BlockSpec((2, tm, f), lambda i: (0, i, 0)), _resident((f, d)), _resident((f, d)), blk, blk, _full((1, d))],
        out_specs=[blk, _full((1, d))],
        out_shape=[jax.ShapeDtypeStruct((t_rows, d), F32), jax.ShapeDtypeStruct((1, d), F32)],
        args=(dau, wg_t, wu_t, h1, dh2, g3), name="bwd_ffn_in", comm=comm)


def _bwd_out_proj(dh1, mix, w_out, g2, z, lg, lb, tm, after):
    t_rows, d = dh1.shape
    w = z.shape[1]

    def body(dh1_ref, mix_ref, w_ref, g2_ref, z_ref, lg_ref, lb_ref, dmix_ref, dya_ref, dz_ref, dg2_ref, dlg_ref, dlb_ref, dbb_ref):
        first = pl.program_id(0) == 0
        mix = mix_ref[...]
        r2 = _rstd(mix)
        mh = mix * r2
        dh1 = dh1_ref[...]
        _acc_rows(dg2_ref, dh1 * mh, first)
        dmix = _rms_bwd(dh1 * g2_ref[...], mh, r2).astype(BF16)
        dmix_ref[...] = dmix
        dy = _dot_nt(dmix, w_ref[...])
        dya_ref[...] = dy[:, :w]
        lg = lg_ref[...]
        zh, rstd, ln = _ln_parts(z_ref[...], lg, lb_ref[...])
        dln = dy[:, w:] * _silu_grad(ln, jax.nn.sigmoid(ln))
        _acc_rows(dlg_ref, dln * zh, first)
        _acc_rows(dlb_ref, dln, first)
        dzh = dln * lg
        dz = rstd * (dzh - jnp.mean(dzh, axis=-1, keepdims=True) - zh * jnp.mean(dzh * zh, axis=-1, keepdims=True))
        dz_ref[...] = dz
        _acc_rows(dbb_ref, dz, first)

    blk = pl.BlockSpec((tm, d), lambda i: (i, 0))
    half = pl.BlockSpec((tm, w), lambda i: (i, 0))
    vec = _full((1, w))
    res, _ = _host_call(
        body, grid=(t_rows // tm,), in_specs=[blk, blk, _resident(w_out.shape), _full((1, d)), half, vec, vec],
        out_specs=[blk, half, half, _full((1, d)), vec, vec, vec],
        out_shape=[jax.ShapeDtypeStruct((t_rows, d), BF16), jax.ShapeDtypeStruct((t_rows, w), F32),
                   jax.ShapeDtypeStruct((t_rows, w), F32), jax.ShapeDtypeStruct((1, d), F32)]
        + [jax.ShapeDtypeStruct((1, w), F32)] * 3,
        args=(dh1, mix, w_out, g2, z, lg, lb), name="bwd_out_proj", after=after)
    return res


def _mix_conv_bwd(hin, dy, dz, wa, wb, wa_w, comm):
    t_rows = hin.shape[0]
    nt = wa_w // LANE
    ka, kb = wa.shape[0], wb.shape[0]
    nr = CONV_HALO + t_rows
    kb_rows = -(-kb // SUBLANE) * SUBLANE

    def body(bg_ref, cg_ref, ha_ref, val_ref, gt_ref, dya_ref, dz_ref, wa_ref, wb_ref,
             dh_ref, dwa_ref, dwb_ref, base, sh, based, shd, tmp, wbc):
        zeros = lambda n: jnp.zeros((n, LANE), F32)
        base[pl.ds(0, CONV_HALO), :] = zeros(CONV_HALO)
        base[pl.ds(nr, SUBLANE), :] = zeros(SUBLANE)
        based[pl.ds(t_rows, CONV_HALO + SUBLANE), :] = zeros(CONV_HALO + SUBLANE)

        def fwd_slot(k_taps, k):
            return _tap_slot(CONV_HALO - (k_taps - 1) + k)

        def bwd_slot(k_taps, k):
            return _tap_slot(k_taps - 1 - k)

        def conv(w_ref, k_taps, src, slot, b, n):
            acc = None
            for k in range(k_taps):
                r, q = slot(k_taps, k)
                term = w_ref[pl.ds(k, 1), :] * _shifted_rows(src, r, b + q, n)
                acc = term if acc is None else acc + term
            return acc

        def by_residue(k_taps, slot):
            groups = {}
            for k in range(k_taps):
                r, q = slot(k_taps, k)
                groups.setdefault(r, []).append((k, q // SUBLANE))
            return groups

        def wgrad_loop(w_ref, k_taps):
            n_sub = WGRAD_ROWS // SUBLANE
            for k in range(k_taps):
                wbc[k] = jnp.broadcast_to(w_ref[pl.ds(k, 1), :], (SUBLANE, LANE))
            fwd, bwd = by_residue(k_taps, fwd_slot), by_residue(k_taps, bwd_slot)

            def window(src, r, taps, b):
                span = n_sub + max(qi for _, qi in taps)
                return [_shifted_rows(src, r, b + SUBLANE * i, SUBLANE) for i in range(span)]

            def step(b, accs):
                accs = list(accs)
                dv = [based[pl.ds(b + SUBLANE * j, SUBLANE), :] for j in range(n_sub)]
                for r, taps in fwd.items():
                    win = window((base, sh), r, taps, b)
                    for k, qi in taps:
                        t = dv[0] * win[qi]
                        for j in range(1, n_sub):
                            t = t + dv[j] * win[qi + j]
                        accs[k] = accs[k] + t
                outs = [None] * n_sub
                for r, taps in bwd.items():
                    win = window((based, shd), r, taps, b)
                    for k, qi in taps:
                        wk = wbc[k]
                        for j in range(n_sub):
                            term = wk * win[qi + j]
                            outs[j] = term if outs[j] is None else outs[j] + term
                for j in range(n_sub):
                    tmp[pl.ds(b + SUBLANE * j, SUBLANE), :] = outs[j]
                return tuple(accs)

            return _row_loop(t_rows, WGRAD_ROWS, step, tuple(zeros(SUBLANE) for _ in range(k_taps)))

        def store_taps(ref, accs, rows):
            for k, acc in enumerate(accs):
                ref[pl.ds(k, 1), :] = jnp.sum(acc, axis=0, keepdims=True)
            if rows > len(accs):
                ref[pl.ds(len(accs), rows - len(accs)), :] = zeros(rows - len(accs))

        def fill_a(b, c):
            sl = pl.ds(b, CONV_CHUNK)
            base[pl.ds(CONV_HALO + b, CONV_CHUNK), :] = cg_ref[sl, :] * ha_ref[sl, :]
            based[sl, :] = dya_ref[sl, :] * bg_ref[sl, :]
            return c

        _row_loop(t_rows, CONV_CHUNK, fill_a)
        _fill_shifted(sh, base, sorted({fwd_slot(ka, k)[0] for k in range(ka)}), nr)
        _fill_shifted(shd, based, sorted({bwd_slot(ka, k)[0] for k in range(ka)}), nr)

        def d_bgate(b, c):
            sl = pl.ds(b, CONV_CHUNK)
            dh_ref[0, sl, :] = (dya_ref[sl, :] * conv(wa_ref, ka, (base, sh), fwd_slot, b, CONV_CHUNK)).astype(BF16)
            return c

        _row_loop(t_rows, CONV_CHUNK, d_bgate)
        store_taps(dwa_ref, wgrad_loop(wa_ref, ka), SUBLANE)

        def d_ch(b, c):
            sl = pl.ds(b, CONV_CHUNK)
            dua = tmp[sl, :]
            dh_ref[1, sl, :] = (dua * ha_ref[sl, :]).astype(BF16)
            dh_ref[2, sl, :] = (dua * cg_ref[sl, :]).astype(BF16)
            return c

        _row_loop(t_rows, CONV_CHUNK, d_ch)

        def fill_b(b, c):
            sl = pl.ds(b, CONV_CHUNK)
            base[pl.ds(CONV_HALO + b, CONV_CHUNK), :] = val_ref[sl, :] * jax.nn.sigmoid(gt_ref[sl, :])
            based[sl, :] = dz_ref[sl, :]
            return c

        _row_loop(t_rows, CONV_CHUNK, fill_b)
        _fill_shifted(sh, base, range(SUBLANE), nr)
        _fill_shifted(shd, based, range(SUBLANE), nr)
        store_taps(dwb_ref, wgrad_loop(wb_ref, kb), kb_rows)

        def d_glu(b, c):
            sl = pl.ds(b, CONV_CHUNK)
            dgg = tmp[sl, :]
            sig = jax.nn.sigmoid(gt_ref[sl, :])
            dh_ref[3, sl, :] = (dgg * sig).astype(BF16)
            dh_ref[4, sl, :] = (dgg * val_ref[sl, :] * (sig * (1.0 - sig))).astype(BF16)
            return c

        _row_loop(t_rows, CONV_CHUNK, d_glu)

    def col(g):
        return pl.BlockSpec((t_rows, LANE), lambda i, g=g: (0, g * nt + i))

    tile = lambda rows: pl.BlockSpec((rows, LANE), lambda i: (0, i))
    return _host_call(
        body, grid=(nt,),
        in_specs=[col(0), col(1), col(2), col(3), col(4), tile(t_rows), tile(t_rows), tile(ka), tile(kb)],
        out_specs=[pl.BlockSpec((5, t_rows, LANE), lambda i: (0, 0, i)), tile(SUBLANE), tile(kb_rows)],
        out_shape=[jax.ShapeDtypeStruct((5, t_rows, wa_w), BF16), jax.ShapeDtypeStruct((SUBLANE, wa_w), F32),
                   jax.ShapeDtypeStruct((kb_rows, wa_w), F32)],
        scratch_shapes=[pltpu.VMEM((nr + SUBLANE, LANE), F32), pltpu.VMEM((SUBLANE, nr, LANE), F32),
                        pltpu.VMEM((nr + SUBLANE, LANE), F32), pltpu.VMEM((SUBLANE, nr, LANE), F32),
                        pltpu.VMEM((t_rows, LANE), F32), pltpu.VMEM((kb_rows, SUBLANE, LANE), F32)],
        args=(hin, hin, hin, hin, hin, dy, dz, wa, wb), name="mix_conv_bwd", comm=comm)


def _bwd_in_proj(dh5, win_t, h0, dh1, g1, tm, x0, comm):
    t_rows, d = h0.shape
    groups, _, w = dh5.shape
    assert x0 <= tm

    def body(dh_ref, w_ref, h0_ref, dh1_ref, g1_ref, dh0_ref, dg1_ref, dmeta_ref):
        dxn1 = None
        for g in range(groups):
            part = _dot_nn(dh_ref[g], w_ref[pl.ds(g * w, w), :])
            dxn1 = part if dxn1 is None else dxn1 + part
        h0 = h0_ref[...]
        r1 = _rstd(h0)
        h0h = h0 * r1
        _acc_rows(dg1_ref, dxn1 * h0h, pl.program_id(0) == 0)
        dh0_ref[...] = dh1_ref[...] + _rms_bwd(dxn1 * g1_ref[...], h0h, r1)

        @pl.when(pl.program_id(0) == 0)
        def _():
            dmeta_ref[...] = dh0_ref[pl.ds(x0 - N_META, N_META), :]

    blk = pl.BlockSpec((tm, d), lambda i: (i, 0))
    return _host_call(
        body, grid=(t_rows // tm,),
        in_specs=[pl.BlockSpec((groups, tm, w), lambda i: (0, i, 0)), _resident(win_t.shape), blk, blk, _full((1, d))],
        out_specs=[blk, _full((1, d)), _full((N_META, d))],
        out_shape=[jax.ShapeDtypeStruct((t_rows, d), F32), jax.ShapeDtypeStruct((1, d), F32), jax.ShapeDtypeStruct((N_META, d), F32)],
        args=(dh5, win_t, h0, dh1, g1), name="bwd_in_proj", comm=comm)


def _pair_small(smalls, d):
    (dmeta, dg1, dg2, dg3, dg4, dbb, dlg, dlb, lossv, dwa, dwb) = smalls
    half = d // 2
    kb_rows = dwb.shape[0]

    def body(dmeta_ref, dg1_ref, dg2_ref, dg3_ref, dg4_ref, dbb_ref, dlg_ref, dlb_ref, loss_ref, dwa_ref, dwb_ref,
             sums_ref, pbuf, psib, ps_send, ps_recv):
        x, y, c = _mesh_pos()
        _pair_handshake()
        pbuf[...] = jnp.zeros_like(pbuf)
        pbuf[pl.ds(0, N_META), :] = dmeta_ref[...]
        for row, ref in ((16, dg1_ref), (17, dg2_ref), (18, dg3_ref), (19, dg4_ref)):
            pbuf[pl.ds(row, 1), :] = ref[...]
        pbuf[pl.ds(20, 1), pl.ds(0, half)] = dbb_ref[...]
        pbuf[pl.ds(20, 1), pl.ds(half, half)] = dlg_ref[...]
        pbuf[pl.ds(21, 1), pl.ds(0, half)] = dlb_ref[...]
        lv = loss_ref[...]
        pbuf[pl.ds(21, 1), pl.ds(half, half)] = lv[:, :half] + lv[:, half:]
        pbuf[pl.ds(24, SUBLANE), pl.ds(0, half)] = dwa_ref[...]
        pbuf[pl.ds(32, kb_rows), pl.ds(0, half)] = dwb_ref[...]
        to_sib = _remote(pbuf, psib, ps_send.at[0], ps_recv.at[0], (x, y, 1 - c))
        to_sib.start()
        to_sib.wait_recv()
        s = pbuf[...] + psib[...]
        for k in range(3):
            sums_ref[k] = s
        to_sib.wait_send()

    return pl.pallas_call(
        body, out_shape=jax.ShapeDtypeStruct((3, SMALL_ROWS, d), F32), in_specs=[VMEM] * 11, out_specs=VMEM,
        scratch_shapes=[pltpu.VMEM((SMALL_ROWS, d), F32), pltpu.VMEM((SMALL_ROWS, d), F32),
                        pltpu.SemaphoreType.DMA((1,)), pltpu.SemaphoreType.DMA((1,))],
        name="pair_small", compiler_params=pltpu.CompilerParams(vmem_limit_bytes=VMEM_LIMIT, collective_id=PAIR_BARRIER_ID))(*smalls)


def _total_small(own, others, ka, kb, ca):
    _, r, d = own.shape
    half, cols = d // 2, d // N_DEV

    def body(own_ref, others_ref, tot_ref, meta_ref, g1_ref, g2_ref, g3_ref, g4_ref, dbb_ref, dlg_ref, dlb_ref, loss_ref,
             ga_ref, gb_ref, chip_p):
        x, y, c = _mesh_pos()
        chip_p[2 * x + y] = own_ref[0]
        for k, (cx, cy) in enumerate(_other_chips(x, y)):
            chip_p[2 * cx + cy] = others_ref[k]
        tot_ref[...] = ((chip_p[0] + chip_p[1]) + chip_p[2]) + chip_p[3]
        meta_ref[...] = tot_ref[pl.ds(0, N_META), pl.ds(pl.multiple_of(_dev_index(x, y, c) * cols, LANE), cols)]
        for row, ref in ((16, g1_ref), (17, g2_ref), (18, g3_ref), (19, g4_ref)):
            ref[...] = tot_ref[pl.ds(row, 1), :]
        dbb_ref[...] = tot_ref[pl.ds(20, 1), pl.ds(0, half)]
        dlg_ref[...] = tot_ref[pl.ds(20, 1), pl.ds(half, half)]
        dlb_ref[...] = tot_ref[pl.ds(21, 1), pl.ds(0, half)]
        loss_ref[...] = (0.5 / d) * jnp.sum(tot_ref[pl.ds(21, 1), pl.ds(half, half)], axis=-1, keepdims=True)
        for j in range(N_DEV):
            @pl.when(_dev_index(x, y, c) == j)
            def _(j=j):
                for row0, n, ref in ((24, ka, ga_ref), (32, kb, gb_ref)):
                    for k in range(n):
                        ref[k] = tot_ref[pl.ds(row0 + k, 1), pl.ds(j * ca, ca)]

    row = lambda n: jax.ShapeDtypeStruct((1, n), F32)
    return pl.pallas_call(
        body, out_shape=[jax.ShapeDtypeStruct((r, d), F32), jax.ShapeDtypeStruct((N_META, cols), F32), row(d), row(d), row(d), row(d),
                         row(half), row(half), row(half), row(1),
                         jax.ShapeDtypeStruct((ka, 1, ca), F32), jax.ShapeDtypeStruct((kb, 1, ca), F32)],
        scratch_shapes=[pltpu.VMEM((4, r, d), F32)], name="total_small", compiler_params=_cparams())(own, others)


def _adamw(w, g, m, v):
    m = ADAM_B1 * m + (1.0 - ADAM_B1) * g
    v = ADAM_B2 * v + (1.0 - ADAM_B2) * jnp.square(g)
    m_hat = m / (1.0 - ADAM_B1 ** ADAM_STEP)
    v_hat = v / (1.0 - ADAM_B2 ** ADAM_STEP)
    delta = -ADAM_LR * (m_hat / (jnp.sqrt(v_hat) + ADAM_EPS) + ADAM_WD * w)
    return delta, m, v


def _adam_big(g, pair, part, w, m, v, name, row0=0):
    r, d = w.shape
    cols = d // ADAM_COL_BLOCKS
    assert row0 % r == 0

    def body(me_ref, g_ref, pair_ref, part_ref, w_ref, m_ref, v_ref, go_ref, d_ref, mo_ref, vo_ref):
        g = g_ref[...].astype(F32) + pair_ref[...].astype(F32)
        for k in range(3):
            g = g + part_ref[k].astype(F32)
        go_ref[...] = g
        d_ref[...], mo_ref[...], vo_ref[...] = _adamw(w_ref[...], g, m_ref[...], v_ref[...])

    blk = pl.BlockSpec((r, cols), lambda i, me_ref: (0, i))
    grid_spec = pltpu.PrefetchScalarGridSpec(
        num_scalar_prefetch=1, grid=(ADAM_COL_BLOCKS,),
        in_specs=[pl.BlockSpec((r, cols), lambda i, me_ref: (me_ref[0], i)),
                  pl.BlockSpec((None, r, cols), lambda i, me_ref: (0, 0, i)),
                  pl.BlockSpec((3, r, cols), lambda i, me_ref: (0, 0, i)), blk, blk, blk],
        out_specs=[blk, blk, blk, blk])
    me = jnp.reshape(_dev_index(*_mesh_pos()) + row0 // r, (1,)).astype(jnp.int32)
    return pl.pallas_call(body, out_shape=[jax.ShapeDtypeStruct((r, d), F32)] * 4, grid_spec=grid_spec, name=name,
                          compiler_params=_cparams(1))(me, g, pair, part, w, m, v)


def _adam_small(gs, ws, ms, vs):
    n = len(gs)

    def body(*refs):
        ins, outs = refs[:4 * n], refs[4 * n:]
        for i in range(n):
            g = ins[i][...]
            delta, m, v = _adamw(ins[n + i][...], g, ins[2 * n + i][...], ins[3 * n + i][...])
            outs[i][...] = delta
            outs[n + i][...] = m
            outs[2 * n + i][...] = v

    shapes = [jax.ShapeDtypeStruct(w.shape, F32) for w in ws]
    return pl.pallas_call(body, out_shape=shapes * 3, name="adam_small", compiler_params=_cparams())(*gs, *ws, *ms, *vs)


def kernel(x, meta_tokens, pre_mix_norm, w_in, conv_a_w, conv_b_w, conv_b_bias, ln_b_gain, ln_b_bias, w_out, post_mix_norm, pre_ffn_norm, w_gate, w_up, w_down, post_ffn_norm, loss_target, m_meta_tokens, m_pre_mix_norm, m_w_in, m_conv_a_w, m_conv_b_w, m_conv_b_bias, m_ln_b_gain, m_ln_b_bias, m_w_out, m_post_mix_norm, m_pre_ffn_norm, m_w_gate, m_w_up, m_w_down, m_post_ffn_norm, v_meta_tokens, v_pre_mix_norm, v_w_in, v_conv_a_w, v_conv_b_w, v_conv_b_bias, v_ln_b_gain, v_ln_b_bias, v_w_out, v_post_mix_norm, v_pre_ffn_norm, v_w_gate, v_w_up, v_w_down, v_post_ffn_norm):
    _, seq, d = x.shape
    ka, ca_loc = conv_a_w.shape[1:]
    kb, cb_loc = conv_b_w.shape[1:]
    wa_w = ca_loc * N_DEV
    assert cb_loc == ca_loc and wa_w % LANE == 0 and w_in.shape[2] * N_DEV == 5 * wa_w and 2 * wa_w == d
    pad = (-(N_META + seq)) % ROW_ALIGN
    x0 = pad + N_META
    t_rows = x0 + seq
    assert t_rows % (N_ROW_BLOCKS * BF16_ROWS) == 0 and t_rows % CONV_CHUNK == 0 and d % LANE == 0
    tm = t_rows // N_ROW_BLOCKS
    tm2 = t_rows // 2
    me = _dev_index(*_mesh_pos())

    def as_rows(w_in_like, w_out_like, w_gate_like, w_up_like, w_down_like):
        return (w_in_like[0].T, w_out_like[0], w_gate_like[0].T, w_up_like[0].T, w_down_like[0])

    w_loc = as_rows(w_in, w_out, w_gate, w_up, w_down)
    rows = [w.shape[0] for w in w_loc]
    assert all(r % ADD_CHUNK == 0 for r in rows)
    P_IN, P_OUT, P_GATE, P_UP, P_DOWN = range(N_BIG)

    assert meta_tokens.shape == (16, LANE) and ka <= 8 and 24 + kb <= SM_ROWS and ca_loc <= LANE
    rows_of = lambda a, n: jnp.pad(a, ((0, n - a.shape[0]), (0, LANE - a.shape[1])))
    sm = jnp.concatenate([meta_tokens, rows_of(conv_a_w[0], 8), rows_of(conv_b_w[0], SM_ROWS - 24)], axis=0)
    wl, wfull, sm_all, h0, tgt = _gather_first(w_loc, sm, [(P_IN, 0, rows[P_IN])], x[0], loss_target[0], t_rows, x0)
    wa =jnp.transpose(sm_all[:, 16:16 + ka, 0:ca_loc], (1, 0, 2)).reshape(ka, wa_w)
    wb = jnp.transpose(sm_all[:, 24:24 + kb, 0:cb_loc], (1, 0, 2)).reshape(kb, wa_w)

    later = (P_OUT, P_GATE, P_UP, P_DOWN)
    sems, wl, started, _ = _gather_start(wl, wfull, later, rows, START_BARRIER_IDS[0])
    for p, arr in zip(later, started):
        wfull[p] = arr

    def arrived(p, after, name):
        nonlocal wl
        wl, wfull[p] = _gather_wait(wl, wfull[p], sems[later.index(p)], after, rows[p], name)
        return _forward_comm(wfull[p], rows[p])

    (xn1, hin), _ = _in_proj(h0, pre_mix_norm, wfull[P_IN], tm, None)
    (ya, z), (wfull[P_OUT],) = _mix_conv_fwd(hin, wa, wb, conv_b_bias, wa_w, arrived(P_OUT, hin, "gather_wait_out"))
    (y, mix, h1, xn2), (wfull[P_GATE],) = _out_proj(ya, z, ln_b_gain, ln_b_bias, wfull[P_OUT], h0, post_mix_norm, pre_ffn_norm, tm2,
                                                    arrived(P_GATE, z, "gather_wait_gate"))
    arrived(P_UP, xn2, "gather_wait_up")
    wfull[P_UP] = _forward_now(wfull[P_UP], rows[P_UP], "forward_up")
    (ga, gu, s), _ = _gate_up(xn2, wfull[P_GATE], wfull[P_UP], tm, None)
    arrived(P_DOWN, s, "gather_wait_down")
    wfull[P_DOWN] = _forward_now(wfull[P_DOWN], rows[P_DOWN], "forward_down")
    dh2, dff, dg4, lossv = _down_loss(s, wfull[P_DOWN], h1, tgt, post_ffn_norm, tm, x0)

    gwd = _wgrad(s, dff, "wgrad_down")
    dau, (pair_d,) = _bwd_down(dff, wfull[P_DOWN], ga, gu, tm, _pair_comm(gwd, rows[P_DOWN]))
    (flight_d,), token = _chip_start([_pair_sum(gwd, pair_d, rows[P_DOWN], "pair_sum_down")], "chip_start_down", START_BARRIER_IDS[1])
    gw_gu = _wgrad(dau, xn2, "wgrad_gate_up", [token])
    up0 = N_DEV * rows[P_GATE]
    (dh1, dg3), (pair_g, pair_u) = _bwd_ffn_in(dau, wfull[P_GATE], wfull[P_UP], h1, dh2, pre_ffn_norm, tm,
                                               _merge_comms([_pair_comm(gw_gu, rows[P_GATE]), _pair_comm(gw_gu, rows[P_UP], up0)]))
    (flight_g, flight_u), token = _chip_start([_pair_sum(gw_gu, pair_g, rows[P_GATE], "pair_sum_gate"),
                                               _pair_sum(gw_gu, pair_u, rows[P_UP], "pair_sum_up", row0=up0)], "chip_start_gate_up",
                                              START_BARRIER_IDS[2])
    dmix, dya, dz, dg2, dlg, dlb, dbb = _bwd_out_proj(dh1, mix, wfull[P_OUT], post_mix_norm, z, ln_b_gain, ln_b_bias, tm, [token])
    gwo = _wgrad(y, dmix, "wgrad_out")
    (dh5, dwa, dwb), (pair_o,) = _mix_conv_bwd(hin, dya, dz, wa, wb, wa_w, _pair_comm(gwo, rows[P_OUT]))
    (flight_o,), token = _chip_start([_pair_sum(gwo, pair_o, rows[P_OUT], "pair_sum_out")], "chip_start_out", START_BARRIER_IDS[3])
    gwi = _wgrad(dh5, xn1, "wgrad_in", [token])
    (dh0, dg1, dmeta), (pair_i,) = _bwd_in_proj(dh5, wfull[P_IN], h0, dh1, pre_mix_norm, tm, x0, _pair_comm(gwi, rows[P_IN]))
    grad_x = dh0[x0:][None]
    small_sums = _pair_small((dmeta, dg1, dg2, dg3, dg4, dbb, dlg, dlb, lossv, dwa, dwb), d)
    (flight_s,), token = _chip_start([small_sums], "chip_start_small", START_BARRIER_IDS[4])
    (flight_i,), token = _chip_start([_pair_sum(gwi, pair_i, rows[P_IN], "pair_sum_in", [token])], "chip_start_in",
                                     START_BARRIER_IDS[5])

    def landed(flight, after, tag):
        sems_p, sums, land = flight
        return _chip_wait(sums, land, sems_p, after, "chip_wait_" + tag)

    m_loc = as_rows(m_w_in, m_w_out, m_w_gate, m_w_up, m_w_down)
    v_loc = as_rows(v_w_in, v_w_out, v_w_gate, v_w_up, v_w_down)
    full_grads = {P_IN: gwi, P_OUT: gwo, P_GATE: gw_gu, P_UP: gw_gu, P_DOWN: gwd}
    pairs = {P_IN: pair_i, P_OUT: pair_o, P_GATE: pair_g, P_UP: pair_u, P_DOWN: pair_d}
    flights = {P_IN: flight_i, P_OUT: flight_o, P_GATE: flight_g, P_UP: flight_u, P_DOWN: flight_d}
    names = {P_IN: "w_in", P_OUT: "w_out", P_GATE: "w_gate", P_UP: "w_up", P_DOWN: "w_down"}
    bigs = {}

    def adam_big(p, after):
        _, part = landed(flights[p], after, names[p])
        res = _adam_big(full_grads[p], pairs[p], part, w_loc[p], m_loc[p], v_loc[p], "adam_" + names[p], up0 if p == P_UP else 0)
        bigs[names[p]] = [(o.T if p in (P_IN, P_GATE, P_UP) else o)[None] for o in res]
        return res[1]

    for p in (P_DOWN, P_GATE, P_UP, P_OUT):
        token = adam_big(p, token)
    (_, g_meta, g_pre_mix, g_post_mix, g_pre_ffn, g_post_ffn, g_conv_bias, g_ln_gain, g_ln_bias, loss11, g_conv_a, g_conv_b) = _total_small(
        *landed(flight_s, token, "small"), ka, kb, ca_loc)
    loss = jnp.reshape(loss11, ())
    taps_major = lambda a: jnp.transpose(a, (1, 0, 2))
    CONV = (2, 3)
    g_small = [g_meta, g_pre_mix, g_conv_a, g_conv_b, g_conv_bias, g_ln_gain, g_ln_bias, g_post_mix, g_pre_ffn, g_post_ffn]
    w_small = [meta_tokens, pre_mix_norm, taps_major(conv_a_w), taps_major(conv_b_w), conv_b_bias, ln_b_gain, ln_b_bias,
               post_mix_norm, pre_ffn_norm, post_ffn_norm]
    m_small = [m_meta_tokens, m_pre_mix_norm, taps_major(m_conv_a_w), taps_major(m_conv_b_w), m_conv_b_bias, m_ln_b_gain,
               m_ln_b_bias, m_post_mix_norm, m_pre_ffn_norm, m_post_ffn_norm]
    v_small = [v_meta_tokens, v_pre_mix_norm, taps_major(v_conv_a_w), taps_major(v_conv_b_w), v_conv_b_bias, v_ln_b_gain,
               v_ln_b_bias, v_post_mix_norm, v_pre_ffn_norm, v_post_ffn_norm]
    small = list(_adam_small(g_small, w_small, m_small, v_small))
    n_small = len(w_small)
    for i in CONV:
        g_small[i] = taps_major(g_small[i])
        for k in range(3):
            small[k * n_small + i] = taps_major(small[k * n_small + i])
    d_small, nm_small, nv_small = small[:n_small], small[n_small:2 * n_small], small[2 * n_small:]

    adam_big(P_IN, small[0])

    def ordered(pick_small, pick_big):
        sm_it = iter(range(n_small))
        out = []
        for name in ("s", "s", "w_in", "s", "s", "s", "s", "s", "w_out", "s", "s", "w_gate", "w_up", "w_down", "s"):
            out.append(pick_small(next(sm_it)) if name == "s" else pick_big(name))
        return out

    grads = ordered(lambda i: g_small[i], lambda n: bigs[n][0])
    deltas = ordered(lambda i: d_small[i], lambda n: bigs[n][1])
    new_m = ordered(lambda i: nm_small[i], lambda n: bigs[n][2])
    new_v = ordered(lambda i: nv_small[i], lambda n: bigs[n][3])
    return (loss, grad_x, *grads, *deltas, *new_m, *new_v)
```

```python
import jax
import jax.numpy as jnp
from jax import lax
from jax.experimental import pallas as pl
from jax.experimental.pallas import tpu as pltpu

F32 = jnp.float32
BF16 = jnp.bfloat16
MESH = pl.DeviceIdType.MESH

N_META = 16
N_DEV = 8
RMS_EPS = 1e-6
LN_EPS = 1e-5
ADAM_LR = 0.001
ADAM_B1 = 0.9
ADAM_B2 = 0.999
ADAM_EPS = 1e-08
ADAM_WD = 0.01
ADAM_STEP = 10

LANE = 128
SUBLANE = 8
BF16_ROWS = 16
ROW_ALIGN = 128
N_ROW_BLOCKS = 4
CONV_HALO = 32
CONV_CHUNK = 64
WGRAD_ROWS = 32
N_CHUNK = 512
WGRAD_TILE_MAX = 1408
ADD_CHUNK = 32
ADAM_COL_BLOCKS = 2
COPY_PIECES = 4
V7X_VMEM_BYTES = 64 * 1024 * 1024
VMEM_LIMIT = V7X_VMEM_BYTES - 6 * 1024 * 1024
SMALL_ROWS = 64
SM_ROWS = 56
N_BIG = 5

ANY = pl.BlockSpec(memory_space=pl.ANY)
VMEM = pl.BlockSpec(memory_space=pltpu.VMEM)


def _cparams(n_grid_axes=0):
    sem = ("arbitrary",) * n_grid_axes if n_grid_axes else None
    return pltpu.CompilerParams(dimension_semantics=sem, vmem_limit_bytes=VMEM_LIMIT)


def _mesh_pos():
    return lax.axis_index("x"), lax.axis_index("y"), lax.axis_index("c")


def _dev_index(px, py, pc):
    return 4 * px + 2 * py + pc


def _other_chips(x, y):
    return [(1 - x, y), (x, 1 - y), (1 - x, 1 - y)]


def _full(shape):
    return pl.BlockSpec(shape, lambda *_: (0,) * len(shape))


def _resident(shape):
    return pl.BlockSpec(shape, lambda *_: (0,) * len(shape), pipeline_mode=pl.Buffered(1))


def _dot_nt(a, w):
    return lax.dot_general(a, w, (((1,), (1,)), ((), ())), preferred_element_type=F32)


def _dot_nn(a, w):
    return jnp.dot(a, w, preferred_element_type=F32)


def _chunks(n, c):
    out, o = [], 0
    while o < n:
        out.append((o, min(c, n - o)))
        o += c
    return out


def _rstd(h):
    return lax.rsqrt(jnp.mean(h * h, axis=-1, keepdims=True) + RMS_EPS)


def _rms_bwd(dyh, yh, r):
    return r * (dyh - yh * jnp.mean(dyh * yh, axis=-1, keepdims=True))


def _silu_grad(a, sig):
    return sig * (1.0 + a * (1.0 - sig))


def _acc_rows(ref, val, first):
    s = jnp.sum(val, axis=0, keepdims=True)

    @pl.when(first)
    def _():
        ref[...] = s

    @pl.when(jnp.logical_not(first))
    def _():
        ref[...] += s


def _row_loop(t_rows, chunk, fn, carry=None):
    def step(i, c):
        return fn(pl.multiple_of(i * chunk, chunk), c)

    return lax.fori_loop(0, t_rows // chunk, step, carry)


def _remote(src, dst, send_sem, recv_sem, to):
    return pltpu.make_async_remote_copy(src_ref=src, dst_ref=dst, send_sem=send_sem, recv_sem=recv_sem,
                                        device_id=to, device_id_type=MESH)


class _Comm:
    def __init__(self, inputs, out_shapes, aliases, scratch, start, finish):
        self.inputs, self.out_shapes, self.aliases, self.scratch = list(inputs), list(out_shapes), dict(aliases), list(scratch)
        self.start, self.finish = start, finish


def _merge_comms(comms):
    inputs, out_shapes, aliases, scratch, spans = [], [], {}, [], []
    for cm in comms:
        spans.append((len(inputs), len(out_shapes), len(scratch), cm))
        aliases.update({len(inputs) + k: len(out_shapes) + v for k, v in cm.aliases.items()})
        inputs += cm.inputs
        out_shapes += cm.out_shapes
        scratch += cm.scratch

    def run(which):
        def fn(ins, outs, scr):
            for i0, o0, s0, cm in spans:
                getattr(cm, which)(ins[i0:i0 + len(cm.inputs)], outs[o0:o0 + len(cm.out_shapes)], scr[s0:s0 + len(cm.scratch)])
        return fn

    return _Comm(inputs, out_shapes, aliases, scratch, run("start"), run("finish"))


def _host_call(body, *, grid, in_specs, out_specs, out_shape, args, name, scratch_shapes=(), comm=None, after=()):
    talks = comm is not None
    if comm is None:
        comm = _Comm([], [], {}, [], lambda *_: None, lambda *_: None)
    n_in, n_out, n_scr = len(args), len(out_shape), len(scratch_shapes)
    c_in, c_out = len(comm.inputs), len(comm.out_shapes)
    n_after = len(after)

    def open_comm(c_ins, c_outs, c_scr):
        if talks:
            _pair_handshake()
        comm.start(c_ins, c_outs, c_scr)

    def hosted(*refs):
        ins, c_ins = refs[:n_in], refs[n_in:n_in + c_in]
        o0 = n_in + c_in + n_after
        outs, c_outs = refs[o0:o0 + n_out], refs[o0 + n_out:o0 + n_out + c_out]
        s0 = o0 + n_out + c_out
        scr, c_scr = refs[s0:s0 + n_scr], refs[s0 + n_scr:]
        if not grid:
            open_comm(c_ins, c_outs, c_scr)
            body(*ins, *outs, *scr)
            comm.finish(c_ins, c_outs, c_scr)
            return
        first = last = None
        for a, n in enumerate(grid):
            f, l = pl.program_id(a) == 0, pl.program_id(a) == n - 1
            first = f if first is None else jnp.logical_and(first, f)
            last = l if last is None else jnp.logical_and(last, l)

        @pl.when(first)
        def _():
            open_comm(c_ins, c_outs, c_scr)

        body(*ins, *outs, *scr)

        @pl.when(last)
        def _():
            comm.finish(c_ins, c_outs, c_scr)

    sem = ("arbitrary",) * len(grid) if grid else None
    params = pltpu.CompilerParams(dimension_semantics=sem, vmem_limit_bytes=VMEM_LIMIT,
                                  collective_id=PAIR_BARRIER_ID if talks else None)
    res = pl.pallas_call(
        hosted, grid=grid, in_specs=list(in_specs) + [ANY] * (c_in + n_after), out_specs=list(out_specs) + [ANY] * c_out,
        out_shape=list(out_shape) + comm.out_shapes, scratch_shapes=list(scratch_shapes) + comm.scratch,
        input_output_aliases={n_in + k: n_out + v for k, v in comm.aliases.items()},
        name=name, compiler_params=params)(*args, *comm.inputs, *after)
    return list(res[:n_out]), list(res[n_out:])


PAIR_BARRIER_ID = 0
START_BARRIER_IDS = (1, 2, 3, 4, 5, 6)
ALL_PEERS_BARRIER_ID = 7


def _chips_handshake():
    x, y, c = _mesh_pos()
    barrier = pltpu.get_barrier_semaphore()
    for chip in _other_chips(x, y):
        pl.semaphore_signal(barrier, inc=1, device_id=(*chip, c), device_id_type=MESH)
    pl.semaphore_wait(barrier, 3)


def _pair_handshake():
    x, y, c = _mesh_pos()
    barrier = pltpu.get_barrier_semaphore()
    pl.semaphore_signal(barrier, inc=1, device_id=(x, y, 1 - c), device_id_type=MESH)
    pl.semaphore_wait(barrier, 1)


GATHER_SEMS = 10


class _Gather:
    def __init__(self, jobs, rows, lo, src_ref, dests, send_sems, recv_sems):
        x, y, c = _mesh_pos()
        me, sib = (x, y, c), (x, y, 1 - c)
        nx, ny, dg = (1 - x, y, c), (x, 1 - y, c), (1 - x, 1 - y, c)
        self.relayed, self.direct, self.relay, self.to_sib, self.sib_fwd = [], [], [], [], []
        for n, (p, r0, nr) in enumerate(jobs):
            assert nr % (2 * BF16_ROWS) == 0
            half = nr // 2

            def rows_of(dev, h, p=p, r0=r0, nr=nr, half=half):
                off, cnt = (r0, nr) if h is None else (r0 + h * half, half)
                return dests[p].at[pl.ds(pl.multiple_of(_dev_index(*dev) * rows[p] + off, BF16_ROWS), cnt), :]

            def mine(h, p=p, r0=r0, nr=nr, half=half):
                off, cnt = (r0, nr) if h is None else (r0 + h * half, half)
                return src_ref.at[pl.ds(lo[p] + off, cnt), :]

            sem = lambda k, n=n: (send_sems.at[GATHER_SEMS * n + k], recv_sems.at[GATHER_SEMS * n + k])
            self.relayed.append([_remote(mine(0), rows_of(me, 0), *sem(0), nx), _remote(mine(1), rows_of(me, 1), *sem(3), ny)])
            self.direct.append([_remote(mine(1), rows_of(me, 1), *sem(1), nx), _remote(mine(0), rows_of(me, 0), *sem(2), ny)])
            self.relay.append([_remote(rows_of(nx, 0), rows_of(nx, 0), *sem(4), ny), _remote(rows_of(ny, 1), rows_of(ny, 1), *sem(5), nx)])
            self.to_sib.append(_remote(mine(None), rows_of(me, None), *sem(6), sib))
            self.sib_fwd.append([_remote(rows_of(dev, None), rows_of(dev, None), *sem(7 + i), sib) for i, dev in enumerate((nx, ny, dg))])

    def start(self):
        for group in (self.relayed, self.direct):
            for cps in group:
                for cp in cps:
                    cp.start()
        for cp in self.to_sib:
            cp.start()

    def mid(self):
        for first, relay in zip(self.relayed, self.relay):
            for arrived, onward in zip(first, relay):
                arrived.wait_recv()
                onward.start()

    def finish(self):
        for direct, relay, fwd in zip(self.direct, self.relay, self.sib_fwd):
            for k in range(2):
                direct[k].wait_recv()
                fwd[k].start()
            for cp in relay:
                cp.wait_recv()
            fwd[2].start()
        for n in range(len(self.to_sib)):
            self.to_sib[n].wait_recv()
            for cp in self.sib_fwd[n]:
                cp.wait_recv()
            for cp in self.relayed[n] + self.direct[n] + self.relay[n] + [self.to_sib[n]] + self.sib_fwd[n]:
                cp.wait_send()


HBM = pl.BlockSpec(memory_space=pltpu.HBM)
SEM = pl.BlockSpec(memory_space=pltpu.SEMAPHORE)
FLOWS = pltpu.SideEffectType.DATAFLOW_SIDE_EFFECTING


def _in_hbm(a):
    return pltpu.with_memory_space_constraint(a, pltpu.HBM)


def _gather_start(wl, dests, ps, rows, barrier_id):
    lo = [sum(rows[:p]) for p in range(N_BIG)]
    n = len(ps)

    def body(*refs):
        wl_ref, dest_refs = refs[0], refs[1:1 + n]
        sends, recvs = refs[1 + n:1 + 2 * n], refs[1 + 2 * n:1 + 3 * n]
        token = refs[-1]
        _chips_handshake()
        x, y, c = _mesh_pos()
        jme = _dev_index(x, y, c)
        for i, p in enumerate(ps):
            mine = dest_refs[i].at[pl.ds(pl.multiple_of(jme * rows[p], BF16_ROWS), rows[p]), :]
            for chip in _other_chips(x, y):
                _remote(wl_ref.at[pl.ds(lo[p], rows[p]), :], mine, sends[i], recvs[i], (*chip, c)).start()
        token[...] = jnp.zeros_like(token)

    thru = [pltpu.HBM(wl.shape, wl.dtype)] + [pltpu.HBM(dests[p].shape, BF16) for p in ps]
    res = pl.pallas_call(
        body, name="gather_start",
        out_shape=tuple([pltpu.SemaphoreType.DMA(())] * (2 * n) + thru + [jax.ShapeDtypeStruct((SUBLANE, LANE), F32)]),
        in_specs=[HBM] * (1 + n), out_specs=tuple([SEM] * (2 * n) + [HBM] * (1 + n) + [VMEM]),
        input_output_aliases={i: 2 * n + i for i in range(1 + n)},
        compiler_params=pltpu.CompilerParams(has_side_effects=FLOWS, collective_id=barrier_id))(
            _in_hbm(wl), *[_in_hbm(dests[p]) for p in ps])
    sems = [(res[i], res[n + i]) for i in range(n)]
    return sems, res[2 * n], list(res[2 * n + 1:3 * n + 1]), res[-1]


def _gather_wait(wl, dest, sems, after, r, name):
    def body(wl_ref, dest_ref, send_sem, recv_sem, after_ref, wl_out, dest_out):
        x, y, c = _mesh_pos()
        three = dest_ref.at[pl.ds(0, 3 * r), :]
        cp = _remote(three, three, send_sem, recv_sem, (x, y, 1 - c))
        cp.wait_send()
        cp.wait_recv()

    res = pl.pallas_call(
        body, name=name, out_shape=(pltpu.HBM(wl.shape, wl.dtype), pltpu.HBM(dest.shape, dest.dtype)),
        in_specs=[HBM, HBM, SEM, SEM, ANY], out_specs=(HBM, HBM), input_output_aliases={0: 0, 1: 1},
        compiler_params=pltpu.CompilerParams(has_side_effects=FLOWS))(wl, dest, sems[0], sems[1], after)
    return res[0], res[1]


def _forward_comm(dest, r):
    def descs(ins, outs, scr):
        x, y, c = _mesh_pos()
        cps = []
        for k, chip in enumerate(_other_chips(x, y)):
            blk = outs[0].at[pl.ds(pl.multiple_of(_dev_index(*chip, c) * r, BF16_ROWS), r), :]
            cps.append(_remote(blk, blk, scr[0].at[k], scr[1].at[k], (x, y, 1 - c)))
        return cps

    def start(ins, outs, scr):
        for cp in descs(ins, outs, scr):
            cp.start()

    def finish(ins, outs, scr):
        cps = descs(ins, outs, scr)
        for cp in cps:
            cp.wait_recv()
        for cp in cps:
            cp.wait_send()

    return _Comm([dest], [jax.ShapeDtypeStruct(dest.shape, dest.dtype)], {0: 0},
                 [pltpu.SemaphoreType.DMA((3,)), pltpu.SemaphoreType.DMA((3,))], start, finish)


def _forward_now(dest, r, name):
    _, (dest,) = _host_call(lambda: None, grid=(), in_specs=[], out_specs=[], out_shape=[], args=(), name=name,
                            comm=_forward_comm(dest, r))
    return dest


def _pair_comm(g, r, row0=0):
    d = g.shape[1]

    def descs(ins, outs, scr):
        x, y, c = _mesh_pos()
        chips = [(x, y)] + _other_chips(x, y)
        return [_remote(ins[0].at[pl.ds(pl.multiple_of(row0 + _dev_index(*chip, 1 - c) * r, BF16_ROWS), r), :], outs[0].at[k],
                        scr[0].at[k], scr[1].at[k], (x, y, 1 - c)) for k, chip in enumerate(chips)]

    def start(ins, outs, scr):
        for cp in descs(ins, outs, scr):
            cp.start()

    def finish(ins, outs, scr):
        cps = descs(ins, outs, scr)
        for cp in cps:
            cp.wait_recv()
        for cp in cps:
            cp.wait_send()

    comm = _Comm([g], [jax.ShapeDtypeStruct((4, r, d), BF16)], {},
                 [pltpu.SemaphoreType.DMA((4,)), pltpu.SemaphoreType.DMA((4,))], start, finish)
    return comm


def _pair_sum(g, pair, r, name, after=(), row0=0):
    d = g.shape[1]

    def body(g_ref, p_ref, *rest):
        o_ref, gbuf, pbuf, sems = rest[len(after):]
        x, y, c = _mesh_pos()
        loads = [pltpu.make_async_copy(p_ref.at[pl.ds(1, 3)], pbuf, sems.at[3])]
        for k, chip in enumerate(_other_chips(x, y)):
            j = _dev_index(*chip, c)
            loads.append(pltpu.make_async_copy(g_ref.at[pl.ds(pl.multiple_of(row0 + j * r, BF16_ROWS), r), :], gbuf.at[k], sems.at[k]))
        for cp in loads:
            cp.start()
        for cp in loads:
            cp.wait()
        for k in range(3):
            o_ref[k] = (gbuf[k].astype(F32) + pbuf[k].astype(F32)).astype(BF16)

    return pl.pallas_call(
        body, out_shape=jax.ShapeDtypeStruct((3, r, d), BF16), in_specs=[ANY] * (2 + len(after)), out_specs=VMEM,
        scratch_shapes=[pltpu.VMEM((3, r, d), BF16), pltpu.VMEM((3, r, d), BF16), pltpu.SemaphoreType.DMA((4,))],
        name=name, compiler_params=_cparams())(g, pair, *after)


def _chip_start(sums, name, barrier_id):
    n = len(sums)

    def body(*refs):
        srcs, lands = refs[:n], refs[n:2 * n]
        sends, recvs = refs[2 * n:3 * n], refs[3 * n:4 * n]
        _chips_handshake()
        x, y, c = _mesh_pos()
        for i in range(n):
            for k, chip in enumerate(_other_chips(x, y)):
                _remote(srcs[i].at[k], lands[i].at[k], sends[i], recvs[i], (*chip, c)).start()
        refs[-1][...] = jnp.zeros_like(refs[-1])

    zones = [pltpu.HBM(s.shape, s.dtype) for s in sums]
    res = pl.pallas_call(
        body, name=name,
        out_shape=tuple([pltpu.SemaphoreType.DMA(())] * (2 * n) + zones + zones + [jax.ShapeDtypeStruct((SUBLANE, LANE), F32)]),
        in_specs=[HBM] * (2 * n), out_specs=tuple([SEM] * (2 * n) + [HBM] * (2 * n) + [VMEM]),
        input_output_aliases={i: 2 * n + i for i in range(2 * n)},
        compiler_params=pltpu.CompilerParams(has_side_effects=FLOWS, collective_id=barrier_id))(
            *[_in_hbm(s) for s in sums], *[_in_hbm(lax.empty(s.shape, s.dtype)) for s in sums])
    flights = [((res[i], res[n + i]), res[2 * n + i], res[3 * n + i]) for i in range(n)]
    return flights, res[-1]


def _chip_wait(sums, land, sems, after, name):
    def body(sums_ref, land_ref, send_sem, recv_sem, after_ref, sums_out, land_out):
        x, y, c = _mesh_pos()
        cp = _remote(sums_ref, land_ref, send_sem, recv_sem, (x, y, 1 - c))
        cp.wait_send()
        cp.wait_recv()

    res = pl.pallas_call(
        body, name=name, out_shape=(pltpu.HBM(sums.shape, sums.dtype), pltpu.HBM(land.shape, land.dtype)),
        in_specs=[HBM, HBM, SEM, SEM, ANY], out_specs=(HBM, HBM), input_output_aliases={0: 0, 1: 1},
        compiler_params=pltpu.CompilerParams(has_side_effects=FLOWS))(sums, land, sems[0], sems[1], after)
    return res[0], res[1]


class _CopyThrough:
    def __init__(self, src_ref, dst_ref, dst_row0, n_rows, buf, sem_in, sem_out):
        rc = n_rows // COPY_PIECES
        piece = lambda ref, o: ref.at[pl.ds(o, rc), :]
        self.loads = [pltpu.make_async_copy(piece(src_ref, k * rc), piece(buf, k * rc), sem_in) for k in range(COPY_PIECES)]
        self.stores = [pltpu.make_async_copy(piece(buf, k * rc), piece(dst_ref, dst_row0 + k * rc), sem_out) for k in range(COPY_PIECES)]
        self.all_in = pltpu.make_async_copy(src_ref, buf, sem_in)
        self.all_out = pltpu.make_async_copy(buf, dst_ref.at[pl.ds(dst_row0, n_rows), :], sem_out)

    def load(self):
        for cp in self.loads:
            cp.start()

    def store(self):
        self.all_in.wait()
        for cp in self.stores:
            cp.start()

    def done(self):
        self.all_out.wait()


def _gather_first(shards, sm, jobs, x2, tgt2, t_rows, x0):
    d = shards[0].shape[1]
    rows = [w.shape[0] for w in shards]
    lo = [sum(rows[:p]) for p in range(N_BIG)]
    n_sems = GATHER_SEMS * len(jobs)
    seq = x2.shape[0]
    assert x0 == ROW_ALIGN and seq % ROW_ALIGN == 0 and d == N_DEV * LANE

    def body(s0, s1, s2, s3, s4, sm_ref, x_ref, tgt_ref, wl_ref, o0, o1, o2, o3, o4, sa_ref, h0_ref, tp_ref,
             wl_v, x_v, tgt_v, heads_v, sa_v, send_sems, recv_sems, ssend, srecv, local_sems, sems_in, sems_out):
        dests = (o0, o1, o2, o3, o4)
        x, y, c = _mesh_pos()
        me = (x, y, c)
        jme = _dev_index(*me)
        peers = [(x, y, 1 - c)] + [(*chip, pc) for pc in (c, 1 - c) for chip in _other_chips(x, y)]
        barrier = pltpu.get_barrier_semaphore()
        for to in peers:
            pl.semaphore_signal(barrier, inc=1, device_id=to, device_id_type=MESH)
        pl.semaphore_wait(barrier, len(peers))
        padded = [_CopyThrough(x_ref, h0_ref, x0, seq, x_v, sems_in.at[0], sems_out.at[0]),
                  _CopyThrough(tgt_ref, tp_ref, x0, seq, tgt_v, sems_in.at[1], sems_out.at[1])]
        for cp in padded:
            cp.load()
        shard_refs = (s0, s1, s2, s3, s4)
        first = sorted({j[0] for j in jobs})
        for p in first + [p for p in range(N_BIG) if p not in first]:
            wl_v[pl.ds(lo[p], rows[p]), :] = shard_refs[p][...].astype(BF16)
            if p == first[-1]:
                gather = _Gather(jobs, rows, lo, wl_v, dict(enumerate(dests)), send_sems, recv_sems)
                gather.start()
        smalls = [_remote(sm_ref, sa_ref.at[jme], ssend.at[k], srecv.at[k], to) for k, to in enumerate(peers)]
        for cp in smalls:
            cp.start()
        mine = [pltpu.make_async_copy(wl_v.at[pl.ds(lo[p], rows[p]), :],
                                      dests[p].at[pl.ds(pl.multiple_of(jme * rows[p], BF16_ROWS), rows[p]), :], local_sems.at[p])
                for p in range(N_BIG)]
        mine.append(pltpu.make_async_copy(wl_v, wl_ref, local_sems.at[N_BIG]))
        mine.append(pltpu.make_async_copy(sm_ref, sa_ref.at[jme], local_sems.at[N_BIG + 1]))
        for cp in mine:
            cp.start()
        later = [p for p in range(N_BIG) if p not in {j[0] for j in jobs}]
        own = [_remote(wl_v.at[pl.ds(lo[p], rows[p]), :], dests[p].at[pl.ds(pl.multiple_of(jme * rows[p], BF16_ROWS), rows[p]), :],
                       ssend.at[7 + i], srecv.at[7 + i], (x, y, 1 - c)) for i, p in enumerate(later)]
        for cp in own:
            cp.start()
        gather.mid()
        for cp in padded:
            cp.store()
        for cp in smalls + own:
            cp.wait_recv()
        mine[-1].wait()
        to_v = pltpu.make_async_copy(sa_ref, sa_v, local_sems.at[N_BIG + 1])
        to_v.start()
        to_v.wait()
        head, zeros = heads_v.at[0], heads_v.at[1]
        head[...] = jnp.zeros_like(head)
        zeros[...] = jnp.zeros_like(zeros)
        for j in range(N_DEV):
            head[pl.ds(x0 - N_META, N_META), pl.ds(j * LANE, LANE)] = sa_v[j, pl.ds(0, N_META), :]
        heads = [pltpu.make_async_copy(head, h0_ref.at[pl.ds(0, x0), :], local_sems.at[N_BIG + 1]),
                 pltpu.make_async_copy(zeros, tp_ref.at[pl.ds(0, x0), :], local_sems.at[N_BIG + 2])]
        for cp in heads:
            cp.start()
        gather.finish()
        for cp in smalls + own:
            cp.wait_send()
        for cp in mine[:-1] + heads:
            cp.wait()
        for cp in padded:
            cp.done()

    out_shape = [jax.ShapeDtypeStruct((sum(rows), d), BF16)]
    out_shape += [jax.ShapeDtypeStruct((N_DEV * r, d), BF16) for r in rows]
    out_shape.append(jax.ShapeDtypeStruct((N_DEV,) + sm.shape, F32))
    out_shape += [jax.ShapeDtypeStruct((t_rows, d), F32)] * 2
    res = pl.pallas_call(
        body, out_shape=out_shape, in_specs=[VMEM] * 6 + [ANY] * 2, out_specs=[ANY] * 9,
        scratch_shapes=[pltpu.VMEM((sum(rows), d), BF16), pltpu.VMEM((seq, d), F32), pltpu.VMEM((seq, d), F32),
                        pltpu.VMEM((2, ROW_ALIGN, d), F32), pltpu.VMEM((N_DEV,) + sm.shape, F32),
                        pltpu.SemaphoreType.DMA((n_sems,)), pltpu.SemaphoreType.DMA((n_sems,)),
                        pltpu.SemaphoreType.DMA((7 + N_BIG,)), pltpu.SemaphoreType.DMA((7 + N_BIG,)),
                        pltpu.SemaphoreType.DMA((N_BIG + 3,)), pltpu.SemaphoreType.DMA((2,)), pltpu.SemaphoreType.DMA((2,))],
        name="gather_first",
        compiler_params=pltpu.CompilerParams(vmem_limit_bytes=VMEM_LIMIT, collective_id=ALL_PEERS_BARRIER_ID))(*shards, sm, x2, tgt2)
    return res[0], list(res[1:1 + N_BIG]), res[1 + N_BIG], res[2 + N_BIG], res[3 + N_BIG]


def _in_proj(h0, g1, win_t, tm, comm):
    t_rows, d = h0.shape
    e = win_t.shape[0]

    def body(h_ref, g_ref, w_ref, xn_ref, hin_ref):
        h = h_ref[...]
        xn = ((h * _rstd(h)) * g_ref[...]).astype(BF16)
        xn_ref[...] = xn
        for o, n in _chunks(e, N_CHUNK):
            hin_ref[:, pl.ds(o, n)] = _dot_nt(xn, w_ref[pl.ds(o, n), :])

    return _host_call(
        body, grid=(t_rows // tm,),
        in_specs=[pl.BlockSpec((tm, d), lambda i: (i, 0)), _full((1, d)), _resident((e, d))],
        out_specs=[pl.BlockSpec((tm, d), lambda i: (i, 0)), pl.BlockSpec((tm, e), lambda i: (i, 0))],
        out_shape=[jax.ShapeDtypeStruct((t_rows, d), BF16), jax.ShapeDtypeStruct((t_rows, e), F32)],
        args=(h0, g1, win_t), name="in_proj", comm=comm)


def _tap_slot(off):
    return off % SUBLANE, (off // SUBLANE) * SUBLANE


def _fill_shifted(sh_ref, base_ref, residues, n_rows):
    for r in residues:
        if r:
            sh_ref[r] = base_ref[pl.ds(r, n_rows), :]


def _shifted_rows(pair, r, start, n):
    base_ref, sh_ref = pair
    return base_ref[pl.ds(start, n), :] if r == 0 else sh_ref[r, pl.ds(start, n), :]


def _mix_conv_fwd(hin, wa, wb, bb, wa_w, comm):
    t_rows = hin.shape[0]
    nt = wa_w // LANE
    ka, kb = wa.shape[0], wb.shape[0]
    nr = CONV_HALO + t_rows

    def body(bg_ref, cg_ref, ha_ref, val_ref, gt_ref, wa_ref, wb_ref, bb_ref, ya_ref, z_ref, base, sh):
        base[pl.ds(0, CONV_HALO), :] = jnp.zeros((CONV_HALO, LANE), F32)
        base[pl.ds(nr, SUBLANE), :] = jnp.zeros((SUBLANE, LANE), F32)

        def conv(w_ref, k_taps, b, n):
            acc = None
            for k in range(k_taps):
                r, q = _tap_slot(CONV_HALO - (k_taps - 1) + k)
                term = w_ref[pl.ds(k, 1), :] * _shifted_rows((base, sh), r, b + q, n)
                acc = term if acc is None else acc + term
            return acc

        def fill_a(b, c):
            base[pl.ds(CONV_HALO + b, CONV_CHUNK), :] = cg_ref[pl.ds(b, CONV_CHUNK), :] * ha_ref[pl.ds(b, CONV_CHUNK), :]
            return c

        _row_loop(t_rows, CONV_CHUNK, fill_a)
        _fill_shifted(sh, base, sorted({_tap_slot(CONV_HALO - (ka - 1) + k)[0] for k in range(ka)}), nr)

        def out_a(b, c):
            ya_ref[pl.ds(b, CONV_CHUNK), :] = (bg_ref[pl.ds(b, CONV_CHUNK), :] * conv(wa_ref, ka, b, CONV_CHUNK)).astype(BF16)
            return c

        _row_loop(t_rows, CONV_CHUNK, out_a)

        def fill_b(b, c):
            base[pl.ds(CONV_HALO + b, CONV_CHUNK), :] = (val_ref[pl.ds(b, CONV_CHUNK), :]
                                                          * jax.nn.sigmoid(gt_ref[pl.ds(b, CONV_CHUNK), :]))
            return c

        _row_loop(t_rows, CONV_CHUNK, fill_b)
        _fill_shifted(sh, base, range(SUBLANE), nr)

        def out_b(b, c):
            z_ref[pl.ds(b, CONV_CHUNK), :] = conv(wb_ref, kb, b, CONV_CHUNK) + bb_ref[...]
            return c

        _row_loop(t_rows, CONV_CHUNK, out_b)

    def col(g):
        return pl.BlockSpec((t_rows, LANE), lambda i, g=g: (0, g * nt + i))

    tile = lambda rows: pl.BlockSpec((rows, LANE), lambda i: (0, i))
    return _host_call(
        body, grid=(nt,),
        in_specs=[col(0), col(1), col(2), col(3), col(4), tile(ka), tile(kb), tile(1)],
        out_specs=[tile(t_rows), tile(t_rows)],
        out_shape=[jax.ShapeDtypeStruct((t_rows, wa_w), BF16), jax.ShapeDtypeStruct((t_rows, wa_w), F32)],
        scratch_shapes=[pltpu.VMEM((nr + SUBLANE, LANE), F32), pltpu.VMEM((SUBLANE, nr, LANE), F32)],
        args=(hin, hin, hin, hin, hin, wa, wb, bb), name="mix_conv_fwd", comm=comm)


def _ln_parts(z, lg, lb):
    mu = jnp.mean(z, axis=-1, keepdims=True)
    zc = z - mu
    rstd = lax.rsqrt(jnp.mean(zc * zc, axis=-1, keepdims=True) + LN_EPS)
    zh = zc * rstd
    return zh, rstd, zh * lg + lb


def _out_proj(ya, z, lg, lb, w_out, h0, g2, g3, tm, comm):
    t_rows, d = h0.shape
    w = z.shape[1]

    def body(ya_ref, z_ref, lg_ref, lb_ref, w_ref, h0_ref, g2_ref, g3_ref, y_ref, mix_ref, h1_ref, xn2_ref):
        _, _, ln = _ln_parts(z_ref[...], lg_ref[...], lb_ref[...])
        y_ref[:, pl.ds(0, w)] = ya_ref[...]
        y_ref[:, pl.ds(w, w)] = (ln * jax.nn.sigmoid(ln)).astype(BF16)
        mix = _dot_nn(y_ref[...], w_ref[...])
        mix_ref[...] = mix
        h1 = h0_ref[...] + (mix * _rstd(mix)) * g2_ref[...]
        h1_ref[...] = h1
        xn2_ref[...] = ((h1 * _rstd(h1)) * g3_ref[...]).astype(BF16)

    blk = pl.BlockSpec((tm, d), lambda i: (i, 0))
    half = pl.BlockSpec((tm, w), lambda i: (i, 0))
    return _host_call(
        body, grid=(t_rows // tm,),
        in_specs=[half, half, _full((1, w)), _full((1, w)), _resident(w_out.shape), blk, _full((1, d)), _full((1, d))],
        out_specs=[blk, blk, blk, blk],
        out_shape=[jax.ShapeDtypeStruct((t_rows, d), BF16), jax.ShapeDtypeStruct((t_rows, d), F32),
                   jax.ShapeDtypeStruct((t_rows, d), F32), jax.ShapeDtypeStruct((t_rows, d), BF16)],
        args=(ya, z, lg, lb, w_out, h0, g2, g3), name="out_proj", comm=comm)


def _gate_up(xn2, wg_t, wu_t, tm, comm):
    t_rows, d = xn2.shape
    f = wg_t.shape[0]

    def body(x_ref, wg_ref, wu_ref, ga_ref, gu_ref, s_ref):
        xn = x_ref[...]
        for o, n in _chunks(f, N_CHUNK):
            a = _dot_nt(xn, wg_ref[pl.ds(o, n), :])
            u = _dot_nt(xn, wu_ref[pl.ds(o, n), :])
            sig = jax.nn.sigmoid(a)
            silu = a * sig
            s = silu * u
            gu_ref[:, pl.ds(o, n)] = silu.astype(BF16)
            ga_ref[:, pl.ds(o, n)] = ((u - s) * sig + s).astype(BF16)
            s_ref[:, pl.ds(o, n)] = s.astype(BF16)

    blk = pl.BlockSpec((tm, f), lambda i: (i, 0))
    return _host_call(
        body, grid=(t_rows // tm,),
        in_specs=[pl.BlockSpec((tm, d), lambda i: (i, 0)), _resident((f, d)), _resident((f, d))],
        out_specs=[blk, blk, blk], out_shape=[jax.ShapeDtypeStruct((t_rows, f), BF16)] * 3,
        args=(xn2, wg_t, wu_t), name="gate_up", comm=comm)


def _down_loss(s, wd, h1, tgt, g4, tm, x0):
    t_rows, d = h1.shape
    f = wd.shape[0]

    def body(s_ref, w_ref, h1_ref, tgt_ref, g4_ref, dh2_ref, dff_ref, dg4_ref, loss_ref):
        i = pl.program_id(0)
        ff = _dot_nn(s_ref[...], w_ref[...])
        r4 = _rstd(ff)
        fh = ff * r4
        g4 = g4_ref[...]
        h2 = h1_ref[...] + fh * g4
        row = i * tm + lax.broadcasted_iota(jnp.int32, (tm, 1), 0)
        diff = jnp.where(row >= x0, h2 - tgt_ref[...], 0.0)
        dh2 = diff / d
        dh2_ref[...] = dh2
        dff_ref[...] = _rms_bwd(dh2 * g4, fh, r4).astype(BF16)
        _acc_rows(dg4_ref, dh2 * fh, i == 0)
        _acc_rows(loss_ref, diff * diff, i == 0)

    blk = pl.BlockSpec((tm, d), lambda i: (i, 0))
    res, _ = _host_call(
        body, grid=(t_rows // tm,),
        in_specs=[pl.BlockSpec((tm, f), lambda i: (i, 0)), _resident((f, d)), blk, blk, _full((1, d))],
        out_specs=[blk, blk, _full((1, d)), _full((1, d))],
        out_shape=[jax.ShapeDtypeStruct((t_rows, d), F32), jax.ShapeDtypeStruct((t_rows, d), BF16),
                   jax.ShapeDtypeStruct((1, d), F32), jax.ShapeDtypeStruct((1, d), F32)],
        args=(s, wd, h1, tgt, g4), name="down_loss")
    return res


def _bwd_down(dff, wd, ga, gu, tm, comm):
    t_rows, d = dff.shape
    f = wd.shape[0]

    def body(dff_ref, w_ref, ga_ref, gu_ref, dau_ref):
        dff_v = dff_ref[...]
        for o, n in _chunks(f, N_CHUNK):
            ds = _dot_nt(dff_v, w_ref[pl.ds(o, n), :]).astype(BF16)
            dau_ref[0, :, pl.ds(o, n)] = ds * ga_ref[:, pl.ds(o, n)]
            dau_ref[1, :, pl.ds(o, n)] = ds * gu_ref[:, pl.ds(o, n)]

    blk = pl.BlockSpec((tm, f), lambda i: (i, 0))
    (dau,), extra = _host_call(
        body, grid=(t_rows // tm,),
        in_specs=[pl.BlockSpec((tm, d), lambda i: (i, 0)), _resident((f, d)), blk, blk],
        out_specs=[pl.BlockSpec((2, tm, f), lambda i: (0, i, 0))], out_shape=[jax.ShapeDtypeStruct((2, t_rows, f), BF16)],
        args=(dff, wd, ga, gu), name="bwd_down", comm=comm)
    return dau, extra


def _wgrad(a, b, name, after=(), comm=None):
    d = b.shape[1]
    t_rows = b.shape[0]
    stacked = a.ndim == 3
    n = a.shape[-1]
    groups = a.shape[0] if stacked else 1
    steps = 1 if stacked else 2
    tile = max(t for t in range(LANE, min(n // steps, WGRAD_TILE_MAX) + 1, LANE) if n % t == 0)
    tiles = n // tile

    def body(a_ref, b_ref, o_ref):
        o_ref[...] = lax.dot_general(a_ref[...], b_ref[...], (((0,), (0,)), ((), ())),
                                     preferred_element_type=F32).astype(BF16)

    if stacked:
        a_spec = pl.BlockSpec((None, t_rows, tile), lambda g, i: (g, 0, i))
    else:
        a_spec = pl.BlockSpec((t_rows, tile), lambda g, i: (0, i))
    res, hosted = _host_call(
        body, grid=(groups, tiles), in_specs=[a_spec, _resident((t_rows, d))],
        out_specs=[pl.BlockSpec((tile, d), lambda g, i: (g * tiles + i, 0))],
        out_shape=[jax.ShapeDtypeStruct((groups * n, d), BF16)], args=(a, b), name=name, comm=comm, after=after)
    return res[0] if comm is None else (res[0], hosted)


def _bwd_ffn_in(dau, wg_t, wu_t, h1, dh2, g3, tm, comm):
    t_rows, d = h1.shape
    f = wg_t.shape[0]

    def body(dau_ref, wg_ref, wu_ref, h1_ref, dh2_ref, g3_ref, dh1_ref, dg3_ref):
        dxn2 = _dot_nn(dau_ref[0], wg_ref[...]) + _dot_nn(dau_ref[1], wu_ref[...])
        h1 = h1_ref[...]
        r3 = _rstd(h1)
        h1h = h1 * r3
        _acc_rows(dg3_ref, dxn2 * h1h, pl.program_id(0) == 0)
        dh1_ref[...] = dh2_ref[...] + _rms_bwd(dxn2 * g3_ref[...], h1h, r3)

    blk = pl.BlockSpec((tm, d), lambda i: (i, 0))
    return _host_call(
        body, grid=(t_rows // tm,),
        in_specs=[pl.BlockSpec((2, tm, f), lambda i: (0, i, 0)), _resident((f, d)), _resident((f, d)), blk, blk, _full((1, d))],
        out_specs=[blk, _full((1, d))],
        out_shape=[jax.ShapeDtypeStruct((t_rows, d), F32), jax.ShapeDtypeStruct((1, d), F32)],
        args=(dau, wg_t, wu_t, h1, dh2, g3), name="bwd_ffn_in", comm=comm)


def _bwd_out_proj(dh1, mix, w_out, g2, z, lg, lb, tm, after):
    t_rows, d = dh1.shape
    w = z.shape[1]

    def body(dh1_ref, mix_ref, w_ref, g2_ref, z_ref, lg_ref, lb_ref, dmix_ref, dya_ref, dz_ref, dg2_ref, dlg_ref, dlb_ref, dbb_ref):
        first = pl.program_id(0) == 0
        mix = mix_ref[...]
        r2 = _rstd(mix)
        mh = mix * r2
        dh1 = dh1_ref[...]
        _acc_rows(dg2_ref, dh1 * mh, first)
        dmix = _rms_bwd(dh1 * g2_ref[...], mh, r2).astype(BF16)
        dmix_ref[...] = dmix
        dy = _dot_nt(dmix, w_ref[...])
        dya_ref[...] = dy[:, :w]
        lg = lg_ref[...]
        zh, rstd, ln = _ln_parts(z_ref[...], lg, lb_ref[...])
        dln = dy[:, w:] * _silu_grad(ln, jax.nn.sigmoid(ln))
        _acc_rows(dlg_ref, dln * zh, first)
        _acc_rows(dlb_ref, dln, first)
        dzh = dln * lg
        dz = rstd * (dzh - jnp.mean(dzh, axis=-1, keepdims=True) - zh * jnp.mean(dzh * zh, axis=-1, keepdims=True))
        dz_ref[...] = dz
        _acc_rows(dbb_ref, dz, first)

    blk = pl.BlockSpec((tm, d), lambda i: (i, 0))
    half = pl.BlockSpec((tm, w), lambda i: (i, 0))
    vec = _full((1, w))
    res, _ = _host_call(
        body, grid=(t_rows // tm,), in_specs=[blk, blk, _resident(w_out.shape), _full((1, d)), half, vec, vec],
        out_specs=[blk, half, half, _full((1, d)), vec, vec, vec],
        out_shape=[jax.ShapeDtypeStruct((t_rows, d), BF16), jax.ShapeDtypeStruct((t_rows, w), F32),
                   jax.ShapeDtypeStruct((t_rows, w), F32), jax.ShapeDtypeStruct((1, d), F32)]
        + [jax.ShapeDtypeStruct((1, w), F32)] * 3,
        args=(dh1, mix, w_out, g2, z, lg, lb), name="bwd_out_proj", after=after)
    return res


def _mix_conv_bwd(hin, dy, dz, wa, wb, wa_w, comm):
    t_rows = hin.shape[0]
    nt = wa_w // LANE
    ka, kb = wa.shape[0], wb.shape[0]
    nr = CONV_HALO + t_rows
    kb_rows = -(-kb // SUBLANE) * SUBLANE

    def body(bg_ref, cg_ref, ha_ref, val_ref, gt_ref, dya_ref, dz_ref, wa_ref, wb_ref,
             dh_ref, dwa_ref, dwb_ref, base, sh, based, shd, tmp, wbc):
        zeros = lambda n: jnp.zeros((n, LANE), F32)
        base[pl.ds(0, CONV_HALO), :] = zeros(CONV_HALO)
        base[pl.ds(nr, SUBLANE), :] = zeros(SUBLANE)
        based[pl.ds(t_rows, CONV_HALO + SUBLANE), :] = zeros(CONV_HALO + SUBLANE)

        def fwd_slot(k_taps, k):
            return _tap_slot(CONV_HALO - (k_taps - 1) + k)

        def bwd_slot(k_taps, k):
            return _tap_slot(k_taps - 1 - k)

        def conv(w_ref, k_taps, src, slot, b, n):
            acc = None
            for k in range(k_taps):
                r, q = slot(k_taps, k)
                term = w_ref[pl.ds(k, 1), :] * _shifted_rows(src, r, b + q, n)
                acc = term if acc is None else acc + term
            return acc

        def by_residue(k_taps, slot):
            groups = {}
            for k in range(k_taps):
                r, q = slot(k_taps, k)
                groups.setdefault(r, []).append((k, q // SUBLANE))
            return groups

        def wgrad_loop(w_ref, k_taps):
            n_sub = WGRAD_ROWS // SUBLANE
            for k in range(k_taps):
                wbc[k] = jnp.broadcast_to(w_ref[pl.ds(k, 1), :], (SUBLANE, LANE))
            fwd, bwd = by_residue(k_taps, fwd_slot), by_residue(k_taps, bwd_slot)

            def window(src, r, taps, b):
                span = n_sub + max(qi for _, qi in taps)
                return [_shifted_rows(src, r, b + SUBLANE * i, SUBLANE) for i in range(span)]

            def step(b, accs):
                accs = list(accs)
                dv = [based[pl.ds(b + SUBLANE * j, SUBLANE), :] for j in range(n_sub)]
                for r, taps in fwd.items():
                    win = window((base, sh), r, taps, b)
                    for k, qi in taps:
                        t = dv[0] * win[qi]
                        for j in range(1, n_sub):
                            t = t + dv[j] * win[qi + j]
                        accs[k] = accs[k] + t
                outs = [None] * n_sub
                for r, taps in bwd.items():
                    win = window((based, shd), r, taps, b)
                    for k, qi in taps:
                        wk = wbc[k]
                        for j in range(n_sub):
                            term = wk * win[qi + j]
                            outs[j] = term if outs[j] is None else outs[j] + term
                for j in range(n_sub):
                    tmp[pl.ds(b + SUBLANE * j, SUBLANE), :] = outs[j]
                return tuple(accs)

            return _row_loop(t_rows, WGRAD_ROWS, step, tuple(zeros(SUBLANE) for _ in range(k_taps)))

        def store_taps(ref, accs, rows):
            for k, acc in enumerate(accs):
                ref[pl.ds(k, 1), :] = jnp.sum(acc, axis=0, keepdims=True)
            if rows > len(accs):
                ref[pl.ds(len(accs), rows - len(accs)), :] = zeros(rows - len(accs))

        def fill_a(b, c):
            sl = pl.ds(b, CONV_CHUNK)
            base[pl.ds(CONV_HALO + b, CONV_CHUNK), :] = cg_ref[sl, :] * ha_ref[sl, :]
            based[sl, :] = dya_ref[sl, :] * bg_ref[sl, :]
            return c

        _row_loop(t_rows, CONV_CHUNK, fill_a)
        _fill_shifted(sh, base, sorted({fwd_slot(ka, k)[0] for k in range(ka)}), nr)
        _fill_shifted(shd, based, sorted({bwd_slot(ka, k)[0] for k in range(ka)}), nr)

        def d_bgate(b, c):
            sl = pl.ds(b, CONV_CHUNK)
            dh_ref[0, sl, :] = (dya_ref[sl, :] * conv(wa_ref, ka, (base, sh), fwd_slot, b, CONV_CHUNK)).astype(BF16)
            return c

        _row_loop(t_rows, CONV_CHUNK, d_bgate)
        store_taps(dwa_ref, wgrad_loop(wa_ref, ka), SUBLANE)

        def d_ch(b, c):
            sl = pl.ds(b, CONV_CHUNK)
            dua = tmp[sl, :]
            dh_ref[1, sl, :] = (dua * ha_ref[sl, :]).astype(BF16)
            dh_ref[2, sl, :] = (dua * cg_ref[sl, :]).astype(BF16)
            return c

        _row_loop(t_rows, CONV_CHUNK, d_ch)

        def fill_b(b, c):
            sl = pl.ds(b, CONV_CHUNK)
            base[pl.ds(CONV_HALO + b, CONV_CHUNK), :] = val_ref[sl, :] * jax.nn.sigmoid(gt_ref[sl, :])
            based[sl, :] = dz_ref[sl, :]
            return c

        _row_loop(t_rows, CONV_CHUNK, fill_b)
        _fill_shifted(sh, base, range(SUBLANE), nr)
        _fill_shifted(shd, based, range(SUBLANE), nr)
        store_taps(dwb_ref, wgrad_loop(wb_ref, kb), kb_rows)

        def d_glu(b, c):
            sl = pl.ds(b, CONV_CHUNK)
            dgg = tmp[sl, :]
            sig = jax.nn.sigmoid(gt_ref[sl, :])
            dh_ref[3, sl, :] = (dgg * sig).astype(BF16)
            dh_ref[4, sl, :] = (dgg * val_ref[sl, :] * (sig * (1.0 - sig))).astype(BF16)
            return c

        _row_loop(t_rows, CONV_CHUNK, d_glu)

    def col(g):
        return pl.BlockSpec((t_rows, LANE), lambda i, g=g: (0, g * nt + i))

    tile = lambda rows: pl.BlockSpec((rows, LANE), lambda i: (0, i))
    return _host_call(
        body, grid=(nt,),
        in_specs=[col(0), col(1), col(2), col(3), col(4), tile(t_rows), tile(t_rows), tile(ka), tile(kb)],
        out_specs=[pl.BlockSpec((5, t_rows, LANE), lambda i: (0, 0, i)), tile(SUBLANE), tile(kb_rows)],
        out_shape=[jax.ShapeDtypeStruct((5, t_rows, wa_w), BF16), jax.ShapeDtypeStruct((SUBLANE, wa_w), F32),
                   jax.ShapeDtypeStruct((kb_rows, wa_w), F32)],
        scratch_shapes=[pltpu.VMEM((nr + SUBLANE, LANE), F32), pltpu.VMEM((SUBLANE, nr, LANE), F32),
                        pltpu.VMEM((nr + SUBLANE, LANE), F32), pltpu.VMEM((SUBLANE, nr, LANE), F32),
                        pltpu.VMEM((t_rows, LANE), F32), pltpu.VMEM((kb_rows, SUBLANE, LANE), F32)],
        args=(hin, hin, hin, hin, hin, dy, dz, wa, wb), name="mix_conv_bwd", comm=comm)


def _bwd_in_proj(dh5, win_t, h0, dh1, g1, tm, x0, comm, after=()):
    t_rows, d = h0.shape
    groups, _, w = dh5.shape
    assert x0 <= tm

    def body(dh_ref, w_ref, h0_ref, dh1_ref, g1_ref, dh0_ref, dg1_ref, dmeta_ref):
        dxn1 = None
        for g in range(groups):
            part = _dot_nn(dh_ref[g], w_ref[pl.ds(g * w, w), :])
            dxn1 = part if dxn1 is None else dxn1 + part
        h0 = h0_ref[...]
        r1 = _rstd(h0)
        h0h = h0 * r1
        _acc_rows(dg1_ref, dxn1 * h0h, pl.program_id(0) == 0)
        dh0_ref[...] = dh1_ref[...] + _rms_bwd(dxn1 * g1_ref[...], h0h, r1)

        @pl.when(pl.program_id(0) == 0)
        def _():
            dmeta_ref[...] = dh0_ref[pl.ds(x0 - N_META, N_META), :]

    blk = pl.BlockSpec((tm, d), lambda i: (i, 0))
    return _host_call(
        body, grid=(t_rows // tm,),
        in_specs=[pl.BlockSpec((groups, tm, w), lambda i: (0, i, 0)), _resident(win_t.shape), blk, blk, _full((1, d))],
        out_specs=[blk, _full((1, d)), _full((N_META, d))],
        out_shape=[jax.ShapeDtypeStruct((t_rows, d), F32), jax.ShapeDtypeStruct((1, d), F32), jax.ShapeDtypeStruct((N_META, d), F32)],
        args=(dh5, win_t, h0, dh1, g1), name="bwd_in_proj", comm=comm, after=after)


def _pair_small(smalls, d):
    (dmeta, dg1, dg2, dg3, dg4, dbb, dlg, dlb, lossv, dwa, dwb) = smalls
    half = d // 2
    kb_rows = dwb.shape[0]

    def body(dmeta_ref, dg1_ref, dg2_ref, dg3_ref, dg4_ref, dbb_ref, dlg_ref, dlb_ref, loss_ref, dwa_ref, dwb_ref,
             sums_ref, pbuf, psib, ps_send, ps_recv):
        x, y, c = _mesh_pos()
        _pair_handshake()
        pbuf[...] = jnp.zeros_like(pbuf)
        pbuf[pl.ds(0, N_META), :] = dmeta_ref[...]
        for row, ref in ((16, dg1_ref), (17, dg2_ref), (18, dg3_ref), (19, dg4_ref)):
            pbuf[pl.ds(row, 1), :] = ref[...]
        pbuf[pl.ds(20, 1), pl.ds(0, half)] = dbb_ref[...]
        pbuf[pl.ds(20, 1), pl.ds(half, half)] = dlg_ref[...]
        pbuf[pl.ds(21, 1), pl.ds(0, half)] = dlb_ref[...]
        lv = loss_ref[...]
        pbuf[pl.ds(21, 1), pl.ds(half, half)] = lv[:, :half] + lv[:, half:]
        pbuf[pl.ds(24, SUBLANE), pl.ds(0, half)] = dwa_ref[...]
        pbuf[pl.ds(32, kb_rows), pl.ds(0, half)] = dwb_ref[...]
        to_sib = _remote(pbuf, psib, ps_send.at[0], ps_recv.at[0], (x, y, 1 - c))
        to_sib.start()
        to_sib.wait_recv()
        s = pbuf[...] + psib[...]
        for k in range(3):
            sums_ref[k] = s
        to_sib.wait_send()

    return pl.pallas_call(
        body, out_shape=jax.ShapeDtypeStruct((3, SMALL_ROWS, d), F32), in_specs=[VMEM] * 11, out_specs=VMEM,
        scratch_shapes=[pltpu.VMEM((SMALL_ROWS, d), F32), pltpu.VMEM((SMALL_ROWS, d), F32),
                        pltpu.SemaphoreType.DMA((1,)), pltpu.SemaphoreType.DMA((1,))],
        name="pair_small", compiler_params=pltpu.CompilerParams(vmem_limit_bytes=VMEM_LIMIT, collective_id=PAIR_BARRIER_ID))(*smalls)


def _total_small(own, others, ka, kb, ca):
    _, r, d = own.shape
    half, cols = d // 2, d // N_DEV

    def body(own_ref, others_ref, tot_ref, meta_ref, g1_ref, g2_ref, g3_ref, g4_ref, dbb_ref, dlg_ref, dlb_ref, loss_ref,
             ga_ref, gb_ref, chip_p):
        x, y, c = _mesh_pos()
        chip_p[2 * x + y] = own_ref[0]
        for k, (cx, cy) in enumerate(_other_chips(x, y)):
            chip_p[2 * cx + cy] = others_ref[k]
        tot_ref[...] = ((chip_p[0] + chip_p[1]) + chip_p[2]) + chip_p[3]
        meta_ref[...] = tot_ref[pl.ds(0, N_META), pl.ds(pl.multiple_of(_dev_index(x, y, c) * cols, LANE), cols)]
        for row, ref in ((16, g1_ref), (17, g2_ref), (18, g3_ref), (19, g4_ref)):
            ref[...] = tot_ref[pl.ds(row, 1), :]
        dbb_ref[...] = tot_ref[pl.ds(20, 1), pl.ds(0, half)]
        dlg_ref[...] = tot_ref[pl.ds(20, 1), pl.ds(half, half)]
        dlb_ref[...] = tot_ref[pl.ds(21, 1), pl.ds(0, half)]
        loss_ref[...] = (0.5 / d) * jnp.sum(tot_ref[pl.ds(21, 1), pl.ds(half, half)], axis=-1, keepdims=True)
        for j in range(N_DEV):
            @pl.when(_dev_index(x, y, c) == j)
            def _(j=j):
                for row0, n, ref in ((24, ka, ga_ref), (32, kb, gb_ref)):
                    for k in range(n):
                        ref[k] = tot_ref[pl.ds(row0 + k, 1), pl.ds(j * ca, ca)]

    row = lambda n: jax.ShapeDtypeStruct((1, n), F32)
    return pl.pallas_call(
        body, out_shape=[jax.ShapeDtypeStruct((r, d), F32), jax.ShapeDtypeStruct((N_META, cols), F32), row(d), row(d), row(d), row(d),
                         row(half), row(half), row(half), row(1),
                         jax.ShapeDtypeStruct((ka, 1, ca), F32), jax.ShapeDtypeStruct((kb, 1, ca), F32)],
        scratch_shapes=[pltpu.VMEM((4, r, d), F32)], name="total_small", compiler_params=_cparams())(own, others)


def _adamw(w, g, m, v):
    m = ADAM_B1 * m + (1.0 - ADAM_B1) * g
    v = ADAM_B2 * v + (1.0 - ADAM_B2) * jnp.square(g)
    m_hat = m / (1.0 - ADAM_B1 ** ADAM_STEP)
    v_hat = v / (1.0 - ADAM_B2 ** ADAM_STEP)
    delta = -ADAM_LR * (m_hat / (jnp.sqrt(v_hat) + ADAM_EPS) + ADAM_WD * w)
    return delta, m, v


def _adam_big(g, pair, part, w, m, v, name, row0=0):
    r, d = w.shape
    cols = d // ADAM_COL_BLOCKS
    assert row0 % r == 0

    def body(me_ref, g_ref, pair_ref, part_ref, w_ref, m_ref, v_ref, go_ref, d_ref, mo_ref, vo_ref):
        g = g_ref[...].astype(F32) + pair_ref[...].astype(F32)
        for k in range(3):
            g = g + part_ref[k].astype(F32)
        go_ref[...] = g
        d_ref[...], mo_ref[...], vo_ref[...] = _adamw(w_ref[...], g, m_ref[...], v_ref[...])

    blk = pl.BlockSpec((r, cols), lambda i, me_ref: (0, i))
    grid_spec = pltpu.PrefetchScalarGridSpec(
        num_scalar_prefetch=1, grid=(ADAM_COL_BLOCKS,),
        in_specs=[pl.BlockSpec((r, cols), lambda i, me_ref: (me_ref[0], i)),
                  pl.BlockSpec((None, r, cols), lambda i, me_ref: (0, 0, i)),
                  pl.BlockSpec((3, r, cols), lambda i, me_ref: (0, 0, i)), blk, blk, blk],
        out_specs=[blk, blk, blk, blk])
    me = jnp.reshape(_dev_index(*_mesh_pos()) + row0 // r, (1,)).astype(jnp.int32)
    return pl.pallas_call(body, out_shape=[jax.ShapeDtypeStruct((r, d), F32)] * 4, grid_spec=grid_spec, name=name,
                          compiler_params=_cparams(1))(me, g, pair, part, w, m, v)


def _adam_small(gs, ws, ms, vs):
    n = len(gs)

    def body(*refs):
        ins, outs = refs[:4 * n], refs[4 * n:]
        for i in range(n):
            g = ins[i][...]
            delta, m, v = _adamw(ins[n + i][...], g, ins[2 * n + i][...], ins[3 * n + i][...])
            outs[i][...] = delta
            outs[n + i][...] = m
            outs[2 * n + i][...] = v

    shapes = [jax.ShapeDtypeStruct(w.shape, F32) for w in ws]
    return pl.pallas_call(body, out_shape=shapes * 3, name="adam_small", compiler_params=_cparams())(*gs, *ws, *ms, *vs)


def kernel(x, meta_tokens, pre_mix_norm, w_in, conv_a_w, conv_b_w, conv_b_bias, ln_b_gain, ln_b_bias, w_out, post_mix_norm, pre_ffn_norm, w_gate, w_up, w_down, post_ffn_norm, loss_target, m_meta_tokens, m_pre_mix_norm, m_w_in, m_conv_a_w, m_conv_b_w, m_conv_b_bias, m_ln_b_gain, m_ln_b_bias, m_w_out, m_post_mix_norm, m_pre_ffn_norm, m_w_gate, m_w_up, m_w_down, m_post_ffn_norm, v_meta_tokens, v_pre_mix_norm, v_w_in, v_conv_a_w, v_conv_b_w, v_conv_b_bias, v_ln_b_gain, v_ln_b_bias, v_w_out, v_post_mix_norm, v_pre_ffn_norm, v_w_gate, v_w_up, v_w_down, v_post_ffn_norm):
    _, seq, d = x.shape
    ka, ca_loc = conv_a_w.shape[1:]
    kb, cb_loc = conv_b_w.shape[1:]
    wa_w = ca_loc * N_DEV
    assert cb_loc == ca_loc and wa_w % LANE == 0 and w_in.shape[2] * N_DEV == 5 * wa_w and 2 * wa_w == d
    pad = (-(N_META + seq)) % ROW_ALIGN
    x0 = pad + N_META
    t_rows = x0 + seq
    assert t_rows % (N_ROW_BLOCKS * BF16_ROWS) == 0 and t_rows % CONV_CHUNK == 0 and d % LANE == 0
    tm = t_rows // N_ROW_BLOCKS
    tm2 = t_rows // 2
    me = _dev_index(*_mesh_pos())

    def as_rows(w_in_like, w_out_like, w_gate_like, w_up_like, w_down_like):
        return (w_in_like[0].T, w_out_like[0], w_gate_like[0].T, w_up_like[0].T, w_down_like[0])

    w_loc = as_rows(w_in, w_out, w_gate, w_up, w_down)
    rows = [w.shape[0] for w in w_loc]
    assert all(r % ADD_CHUNK == 0 for r in rows)
    P_IN, P_OUT, P_GATE, P_UP, P_DOWN = range(N_BIG)

    assert meta_tokens.shape == (16, LANE) and ka <= 8 and 24 + kb <= SM_ROWS and ca_loc <= LANE
    rows_of = lambda a, n: jnp.pad(a, ((0, n - a.shape[0]), (0, LANE - a.shape[1])))
    sm = jnp.concatenate([meta_tokens, rows_of(conv_a_w[0], 8), rows_of(conv_b_w[0], SM_ROWS - 24)], axis=0)
    wl, wfull, sm_all, h0, tgt = _gather_first(w_loc, sm, [(P_IN, 0, rows[P_IN])], x[0], loss_target[0], t_rows, x0)
    wa =jnp.transpose(sm_all[:, 16:16 + ka, 0:ca_loc], (1, 0, 2)).reshape(ka, wa_w)
    wb = jnp.transpose(sm_all[:, 24:24 + kb, 0:cb_loc], (1, 0, 2)).reshape(kb, wa_w)

    later = (P_OUT, P_GATE, P_UP, P_DOWN)
    sems, wl, started, _ = _gather_start(wl, wfull, later, rows, START_BARRIER_IDS[0])
    for p, arr in zip(later, started):
        wfull[p] = arr

    def arrived(p, after, name):
        nonlocal wl
        wl, wfull[p] = _gather_wait(wl, wfull[p], sems[later.index(p)], after, rows[p], name)
        return _forward_comm(wfull[p], rows[p])

    (xn1, hin), _ = _in_proj(h0, pre_mix_norm, wfull[P_IN], tm, None)
    (ya, z), (wfull[P_OUT],) = _mix_conv_fwd(hin, wa, wb, conv_b_bias, wa_w, arrived(P_OUT, hin, "gather_wait_out"))
    (y, mix, h1, xn2), (wfull[P_GATE],) = _out_proj(ya, z, ln_b_gain, ln_b_bias, wfull[P_OUT], h0, post_mix_norm, pre_ffn_norm, tm2,
                                                    arrived(P_GATE, z, "gather_wait_gate"))
    arrived(P_UP, xn2, "gather_wait_up")
    wfull[P_UP] = _forward_now(wfull[P_UP], rows[P_UP], "forward_up")
    (ga, gu, s), _ = _gate_up(xn2, wfull[P_GATE], wfull[P_UP], tm, None)
    arrived(P_DOWN, s, "gather_wait_down")
    wfull[P_DOWN] = _forward_now(wfull[P_DOWN], rows[P_DOWN], "forward_down")
    dh2, dff, dg4, lossv = _down_loss(s, wfull[P_DOWN], h1, tgt, post_ffn_norm, tm, x0)

    gwd = _wgrad(s, dff, "wgrad_down")
    dau, (pair_d,) = _bwd_down(dff, wfull[P_DOWN], ga, gu, tm, _pair_comm(gwd, rows[P_DOWN]))
    (flight_d,), token = _chip_start([_pair_sum(gwd, pair_d, rows[P_DOWN], "pair_sum_down")], "chip_start_down", START_BARRIER_IDS[1])
    gw_gu = _wgrad(dau, xn2, "wgrad_gate_up", [token])
    up0 = N_DEV * rows[P_GATE]
    (dh1, dg3), (pair_g, pair_u) = _bwd_ffn_in(dau, wfull[P_GATE], wfull[P_UP], h1, dh2, pre_ffn_norm, tm,
                                               _merge_comms([_pair_comm(gw_gu, rows[P_GATE]), _pair_comm(gw_gu, rows[P_UP], up0)]))
    (flight_g, flight_u), token = _chip_start([_pair_sum(gw_gu, pair_g, rows[P_GATE], "pair_sum_gate"),
                                               _pair_sum(gw_gu, pair_u, rows[P_UP], "pair_sum_up", row0=up0)], "chip_start_gate_up",
                                              START_BARRIER_IDS[2])
    dmix, dya, dz, dg2, dlg, dlb, dbb = _bwd_out_proj(dh1, mix, wfull[P_OUT], post_mix_norm, z, ln_b_gain, ln_b_bias, tm, [token])
    (dh5, dwa, dwb), _ = _mix_conv_bwd(hin, dya, dz, wa, wb, wa_w, None)
    gwi = _wgrad(dh5, xn1, "wgrad_in", [token])
    gwo, (pair_i,) = _wgrad(y, dmix, "wgrad_out", comm=_pair_comm(gwi, rows[P_IN]))
    (flight_i,), token = _chip_start([_pair_sum(gwi, pair_i, rows[P_IN], "pair_sum_in")], "chip_start_in", START_BARRIER_IDS[5])
    (dh0, dg1, dmeta), (pair_o,) = _bwd_in_proj(dh5, wfull[P_IN], h0, dh1, pre_mix_norm, tm, x0, _pair_comm(gwo, rows[P_OUT]),
                                                [token])
    grad_x = dh0[x0:][None]
    small_sums = _pair_small((dmeta, dg1, dg2, dg3, dg4, dbb, dlg, dlb, lossv, dwa, dwb), d)
    (flight_s,), token = _chip_start([small_sums], "chip_start_small", START_BARRIER_IDS[4])
    (flight_o,), token = _chip_start([_pair_sum(gwo, pair_o, rows[P_OUT], "pair_sum_out", [token])], "chip_start_out",
                                     START_BARRIER_IDS[3])

    def landed(flight, after, tag):
        sems_p, sums, land = flight
        return _chip_wait(sums, land, sems_p, after, "chip_wait_" + tag)

    m_loc = as_rows(m_w_in, m_w_out, m_w_gate, m_w_up, m_w_down)
    v_loc = as_rows(v_w_in, v_w_out, v_w_gate, v_w_up, v_w_down)
    full_grads = {P_IN: gwi, P_OUT: gwo, P_GATE: gw_gu, P_UP: gw_gu, P_DOWN: gwd}
    pairs = {P_IN: pair_i, P_OUT: pair_o, P_GATE: pair_g, P_UP: pair_u, P_DOWN: pair_d}
    flights = {P_IN: flight_i, P_OUT: flight_o, P_GATE: flight_g, P_UP: flight_u, P_DOWN: flight_d}
    names = {P_IN: "w_in", P_OUT: "w_out", P_GATE: "w_gate", P_UP: "w_up", P_DOWN: "w_down"}
    bigs = {}

    def adam_big(p, after):
        _, part = landed(flights[p], after, names[p])
        res = _adam_big(full_grads[p], pairs[p], part, w_loc[p], m_loc[p], v_loc[p], "adam_" + names[p], up0 if p == P_UP else 0)
        bigs[names[p]] = [(o.T if p in (P_IN, P_GATE, P_UP) else o)[None] for o in res]
        return res[1]

    for p in (P_DOWN, P_GATE, P_UP, P_IN):
        token = adam_big(p, token)
    (_, g_meta, g_pre_mix, g_post_mix, g_pre_ffn, g_post_ffn, g_conv_bias, g_ln_gain, g_ln_bias, loss11, g_conv_a, g_conv_b) = _total_small(
        *landed(flight_s, token, "small"), ka, kb, ca_loc)
    loss = jnp.reshape(loss11, ())
    taps_major = lambda a: jnp.transpose(a, (1, 0, 2))
    CONV = (2, 3)
    g_small = [g_meta, g_pre_mix, g_conv_a, g_conv_b, g_conv_bias, g_ln_gain, g_ln_bias, g_post_mix, g_pre_ffn, g_post_ffn]
    w_small = [meta_tokens, pre_mix_norm, taps_major(conv_a_w), taps_major(conv_b_w), conv_b_bias, ln_b_gain, ln_b_bias,
               post_mix_norm, pre_ffn_norm, post_ffn_norm]
    m_small = [m_meta_tokens, m_pre_mix_norm, taps_major(m_conv_a_w), taps_major(m_conv_b_w), m_conv_b_bias, m_ln_b_gain,
               m_ln_b_bias, m_post_mix_norm, m_pre_ffn_norm, m_post_ffn_norm]
    v_small = [v_meta_tokens, v_pre_mix_norm, taps_major(v_conv_a_w), taps_major(v_conv_b_w), v_conv_b_bias, v_ln_b_gain,
               v_ln_b_bias, v_post_mix_norm, v_pre_ffn_norm, v_post_ffn_norm]
    small = list(_adam_small(g_small, w_small, m_small, v_small))
    n_small = len(w_small)
    for i in CONV:
        g_small[i] = taps_major(g_small[i])
        for k in range(3):
            small[k * n_small + i] = taps_major(small[k * n_small + i])
    d_small, nm_small, nv_small = small[:n_small], small[n_small:2 * n_small], small[2 * n_small:]

    adam_big(P_OUT, small[0])

    def ordered(pick_small, pick_big):
        sm_it = iter(range(n_small))
        out = []
        for name in ("s", "s", "w_in", "s", "s", "s", "s", "s", "w_out", "s", "s", "w_gate", "w_up", "w_down", "s"):
            out.append(pick_small(next(sm_it)) if name == "s" else pick_big(name))
        return out

    grads = ordered(lambda i: g_small[i], lambda n: bigs[n][0])
    deltas = ordered(lambda i: d_small[i], lambda n: bigs[n][1])
    new_m = ordered(lambda i: nm_small[i], lambda n: bigs[n][2])
    new_v = ordered(lambda i: nv_small[i], lambda n: bigs[n][3])
    return (loss, grad_x, *grads, *deltas, *new_m, *new_v)
```

```python
import jax
import jax.numpy as jnp
from jax import lax
from jax.experimental import pallas as pl
from jax.experimental.pallas import tpu as pltpu

F32 = jnp.float32
BF16 = jnp.bfloat16
MESH = pl.DeviceIdType.MESH

N_META = 16
N_DEV = 8
RMS_EPS = 1e-6
LN_EPS = 1e-5
ADAM_LR = 0.001
ADAM_B1 = 0.9
ADAM_B2 = 0.999
ADAM_EPS = 1e-08
ADAM_WD = 0.01
ADAM_STEP = 10

LANE = 128
SUBLANE = 8
BF16_ROWS = 16
ROW_ALIGN = 128
N_ROW_BLOCKS = 4
CONV_HALO = 32
CONV_CHUNK = 64
WGRAD_ROWS = 32
N_CHUNK = 512
WGRAD_TILE_MAX = 1408
ADD_CHUNK = 32
ADAM_COL_BLOCKS = 2
COPY_PIECES = 4
V7X_VMEM_BYTES = 64 * 1024 * 1024
VMEM_LIMIT = V7X_VMEM_BYTES - 6 * 1024 * 1024
SMALL_ROWS = 64
SM_ROWS = 56
N_BIG = 5

ANY = pl.BlockSpec(memory_space=pl.ANY)
VMEM = pl.BlockSpec(memory_space=pltpu.VMEM)


def _cparams(n_grid_axes=0):
    sem = ("arbitrary",) * n_grid_axes if n_grid_axes else None
    return pltpu.CompilerParams(dimension_semantics=sem, vmem_limit_bytes=VMEM_LIMIT)


def _mesh_pos():
    return lax.axis_index("x"), lax.axis_index("y"), lax.axis_index("c")


def _dev_index(px, py, pc):
    return 4 * px + 2 * py + pc


def _other_chips(x, y):
    return [(1 - x, y), (x, 1 - y), (1 - x, 1 - y)]


def _full(shape):
    return pl.BlockSpec(shape, lambda *_: (0,) * len(shape))


def _resident(shape):
    return pl.BlockSpec(shape, lambda *_: (0,) * len(shape), pipeline_mode=pl.Buffered(1))


def _dot_nt(a, w):
    return lax.dot_general(a, w, (((1,), (1,)), ((), ())), preferred_element_type=F32)


def _dot_nn(a, w):
    return jnp.dot(a, w, preferred_element_type=F32)


def _chunks(n, c):
    out, o = [], 0
    while o < n:
        out.append((o, min(c, n - o)))
        o += c
    return out


def _rstd(h):
    return lax.rsqrt(jnp.mean(h * h, axis=-1, keepdims=True) + RMS_EPS)


def _rms_bwd(dyh, yh, r):
    return r * (dyh - yh * jnp.mean(dyh * yh, axis=-1, keepdims=True))


def _silu_grad(a, sig):
    return sig * (1.0 + a * (1.0 - sig))


def _acc_rows(ref, val, first):
    s = jnp.sum(val, axis=0, keepdims=True)

    @pl.when(first)
    def _():
        ref[...] = s

    @pl.when(jnp.logical_not(first))
    def _():
        ref[...] += s


def _row_loop(t_rows, chunk, fn, carry=None):
    def step(i, c):
        return fn(pl.multiple_of(i * chunk, chunk), c)

    return lax.fori_loop(0, t_rows // chunk, step, carry)


def _remote(src, dst, send_sem, recv_sem, to):
    return pltpu.make_async_remote_copy(src_ref=src, dst_ref=dst, send_sem=send_sem, recv_sem=recv_sem,
                                        device_id=to, device_id_type=MESH)


class _Comm:
    def __init__(self, inputs, out_shapes, aliases, scratch, start, finish):
        self.inputs, self.out_shapes, self.aliases, self.scratch = list(inputs), list(out_shapes), dict(aliases), list(scratch)
        self.start, self.finish = start, finish


def _merge_comms(comms):
    inputs, out_shapes, aliases, scratch, spans = [], [], {}, [], []
    for cm in comms:
        spans.append((len(inputs), len(out_shapes), len(scratch), cm))
        aliases.update({len(inputs) + k: len(out_shapes) + v for k, v in cm.aliases.items()})
        inputs += cm.inputs
        out_shapes += cm.out_shapes
        scratch += cm.scratch

    def run(which):
        def fn(ins, outs, scr):
            for i0, o0, s0, cm in spans:
                getattr(cm, which)(ins[i0:i0 + len(cm.inputs)], outs[o0:o0 + len(cm.out_shapes)], scr[s0:s0 + len(cm.scratch)])
        return fn

    return _Comm(inputs, out_shapes, aliases, scratch, run("start"), run("finish"))


def _host_call(body, *, grid, in_specs, out_specs, out_shape, args, name, scratch_shapes=(), comm=None, after=()):
    talks = comm is not None
    if comm is None:
        comm = _Comm([], [], {}, [], lambda *_: None, lambda *_: None)
    n_in, n_out, n_scr = len(args), len(out_shape), len(scratch_shapes)
    c_in, c_out = len(comm.inputs), len(comm.out_shapes)
    n_after = len(after)

    def open_comm(c_ins, c_outs, c_scr):
        if talks:
            _pair_handshake()
        comm.start(c_ins, c_outs, c_scr)

    def hosted(*refs):
        ins, c_ins = refs[:n_in], refs[n_in:n_in + c_in]
        o0 = n_in + c_in + n_after
        outs, c_outs = refs[o0:o0 + n_out], refs[o0 + n_out:o0 + n_out + c_out]
        s0 = o0 + n_out + c_out
        scr, c_scr = refs[s0:s0 + n_scr], refs[s0 + n_scr:]
        if not grid:
            open_comm(c_ins, c_outs, c_scr)
            body(*ins, *outs, *scr)
            comm.finish(c_ins, c_outs, c_scr)
            return
        first = last = None
        for a, n in enumerate(grid):
            f, l = pl.program_id(a) == 0, pl.program_id(a) == n - 1
            first = f if first is None else jnp.logical_and(first, f)
            last = l if last is None else jnp.logical_and(last, l)

        @pl.when(first)
        def _():
            open_comm(c_ins, c_outs, c_scr)

        body(*ins, *outs, *scr)

        @pl.when(last)
        def _():
            comm.finish(c_ins, c_outs, c_scr)

    sem = ("arbitrary",) * len(grid) if grid else None
    params = pltpu.CompilerParams(dimension_semantics=sem, vmem_limit_bytes=VMEM_LIMIT,
                                  collective_id=PAIR_BARRIER_ID if talks else None)
    res = pl.pallas_call(
        hosted, grid=grid, in_specs=list(in_specs) + [ANY] * (c_in + n_after), out_specs=list(out_specs) + [ANY] * c_out,
        out_shape=list(out_shape) + comm.out_shapes, scratch_shapes=list(scratch_shapes) + comm.scratch,
        input_output_aliases={n_in + k: n_out + v for k, v in comm.aliases.items()},
        name=name, compiler_params=params)(*args, *comm.inputs, *after)
    return list(res[:n_out]), list(res[n_out:])


PAIR_BARRIER_ID = 0
START_BARRIER_IDS = (1, 2, 3, 4, 5, 6)
ALL_PEERS_BARRIER_ID = 7


def _chips_handshake():
    x, y, c = _mesh_pos()
    barrier = pltpu.get_barrier_semaphore()
    for chip in _other_chips(x, y):
        pl.semaphore_signal(barrier, inc=1, device_id=(*chip, c), device_id_type=MESH)
    pl.semaphore_wait(barrier, 3)


def _pair_handshake():
    x, y, c = _mesh_pos()
    barrier = pltpu.get_barrier_semaphore()
    pl.semaphore_signal(barrier, inc=1, device_id=(x, y, 1 - c), device_id_type=MESH)
    pl.semaphore_wait(barrier, 1)


GATHER_SEMS = 10


class _Gather:
    def __init__(self, jobs, rows, lo, src_ref, dests, send_sems, recv_sems):
        x, y, c = _mesh_pos()
        me, sib = (x, y, c), (x, y, 1 - c)
        nx, ny, dg = (1 - x, y, c), (x, 1 - y, c), (1 - x, 1 - y, c)
        self.relayed, self.direct, self.relay, self.to_sib, self.sib_fwd = [], [], [], [], []
        for n, (p, r0, nr) in enumerate(jobs):
            assert nr % (2 * BF16_ROWS) == 0
            half = nr // 2

            def rows_of(dev, h, p=p, r0=r0, nr=nr, half=half):
                off, cnt = (r0, nr) if h is None else (r0 + h * half, half)
                return dests[p].at[pl.ds(pl.multiple_of(_dev_index(*dev) * rows[p] + off, BF16_ROWS), cnt), :]

            def mine(h, p=p, r0=r0, nr=nr, half=half):
                off, cnt = (r0, nr) if h is None else (r0 + h * half, half)
                return src_ref.at[pl.ds(lo[p] + off, cnt), :]

            sem = lambda k, n=n: (send_sems.at[GATHER_SEMS * n + k], recv_sems.at[GATHER_SEMS * n + k])
            self.relayed.append([_remote(mine(0), rows_of(me, 0), *sem(0), nx), _remote(mine(1), rows_of(me, 1), *sem(3), ny)])
            self.direct.append([_remote(mine(1), rows_of(me, 1), *sem(1), nx), _remote(mine(0), rows_of(me, 0), *sem(2), ny)])
            self.relay.append([_remote(rows_of(nx, 0), rows_of(nx, 0), *sem(4), ny), _remote(rows_of(ny, 1), rows_of(ny, 1), *sem(5), nx)])
            self.to_sib.append(_remote(mine(None), rows_of(me, None), *sem(6), sib))
            self.sib_fwd.append([_remote(rows_of(dev, None), rows_of(dev, None), *sem(7 + i), sib) for i, dev in enumerate((nx, ny, dg))])

    def start(self):
        for group in (self.relayed, self.direct):
            for cps in group:
                for cp in cps:
                    cp.start()
        for cp in self.to_sib:
            cp.start()

    def mid(self):
        for first, relay in zip(self.relayed, self.relay):
            for arrived, onward in zip(first, relay):
                arrived.wait_recv()
                onward.start()

    def finish(self):
        for direct, relay, fwd in zip(self.direct, self.relay, self.sib_fwd):
            for k in range(2):
                direct[k].wait_recv()
                fwd[k].start()
            for cp in relay:
                cp.wait_recv()
            fwd[2].start()
        for n in range(len(self.to_sib)):
            self.to_sib[n].wait_recv()
            for cp in self.sib_fwd[n]:
                cp.wait_recv()
            for cp in self.relayed[n] + self.direct[n] + self.relay[n] + [self.to_sib[n]] + self.sib_fwd[n]:
                cp.wait_send()


HBM = pl.BlockSpec(memory_space=pltpu.HBM)
SEM = pl.BlockSpec(memory_space=pltpu.SEMAPHORE)
FLOWS = pltpu.SideEffectType.DATAFLOW_SIDE_EFFECTING


def _in_hbm(a):
    return pltpu.with_memory_space_constraint(a, pltpu.HBM)


def _gather_start(wl, dests, ps, rows, barrier_id):
    lo = [sum(rows[:p]) for p in range(N_BIG)]
    n = len(ps)

    def body(*refs):
        wl_ref, dest_refs = refs[0], refs[1:1 + n]
        sends, recvs = refs[1 + n:1 + 2 * n], refs[1 + 2 * n:1 + 3 * n]
        token = refs[-1]
        _chips_handshake()
        x, y, c = _mesh_pos()
        jme = _dev_index(x, y, c)
        for i, p in enumerate(ps):
            mine = dest_refs[i].at[pl.ds(pl.multiple_of(jme * rows[p], BF16_ROWS), rows[p]), :]
            for chip in _other_chips(x, y):
                _remote(wl_ref.at[pl.ds(lo[p], rows[p]), :], mine, sends[i], recvs[i], (*chip, c)).start()
        token[...] = jnp.zeros_like(token)

    thru = [pltpu.HBM(wl.shape, wl.dtype)] + [pltpu.HBM(dests[p].shape, BF16) for p in ps]
    res = pl.pallas_call(
        body, name="gather_start",
        out_shape=tuple([pltpu.SemaphoreType.DMA(())] * (2 * n) + thru + [jax.ShapeDtypeStruct((SUBLANE, LANE), F32)]),
        in_specs=[HBM] * (1 + n), out_specs=tuple([SEM] * (2 * n) + [HBM] * (1 + n) + [VMEM]),
        input_output_aliases={i: 2 * n + i for i in range(1 + n)},
        compiler_params=pltpu.CompilerParams(has_side_effects=FLOWS, collective_id=barrier_id))(
            _in_hbm(wl), *[_in_hbm(dests[p]) for p in ps])
    sems = [(res[i], res[n + i]) for i in range(n)]
    return sems, res[2 * n], list(res[2 * n + 1:3 * n + 1]), res[-1]


def _gather_wait(wl, dest, sems, after, r, name):
    def body(wl_ref, dest_ref, send_sem, recv_sem, after_ref, wl_out, dest_out):
        x, y, c = _mesh_pos()
        three = dest_ref.at[pl.ds(0, 3 * r), :]
        cp = _remote(three, three, send_sem, recv_sem, (x, y, 1 - c))
        cp.wait_send()
        cp.wait_recv()

    res = pl.pallas_call(
        body, name=name, out_shape=(pltpu.HBM(wl.shape, wl.dtype), pltpu.HBM(dest.shape, dest.dtype)),
        in_specs=[HBM, HBM, SEM, SEM, ANY], out_specs=(HBM, HBM), input_output_aliases={0: 0, 1: 1},
        compiler_params=pltpu.CompilerParams(has_side_effects=FLOWS))(wl, dest, sems[0], sems[1], after)
    return res[0], res[1]


def _forward_comm(dest, r):
    def descs(ins, outs, scr):
        x, y, c = _mesh_pos()
        cps = []
        for k, chip in enumerate(_other_chips(x, y)):
            blk = outs[0].at[pl.ds(pl.multiple_of(_dev_index(*chip, c) * r, BF16_ROWS), r), :]
            cps.append(_remote(blk, blk, scr[0].at[k], scr[1].at[k], (x, y, 1 - c)))
        return cps

    def start(ins, outs, scr):
        for cp in descs(ins, outs, scr):
            cp.start()

    def finish(ins, outs, scr):
        cps = descs(ins, outs, scr)
        for cp in cps:
            cp.wait_recv()
        for cp in cps:
            cp.wait_send()

    return _Comm([dest], [jax.ShapeDtypeStruct(dest.shape, dest.dtype)], {0: 0},
                 [pltpu.SemaphoreType.DMA((3,)), pltpu.SemaphoreType.DMA((3,))], start, finish)


def _forward_now(dest, r, name):
    _, (dest,) = _host_call(lambda: None, grid=(), in_specs=[], out_specs=[], out_shape=[], args=(), name=name,
                            comm=_forward_comm(dest, r))
    return dest


def _pair_comm(g, r, row0=0):
    d = g.shape[1]

    def descs(ins, outs, scr):
        x, y, c = _mesh_pos()
        chips = [(x, y)] + _other_chips(x, y)
        return [_remote(ins[0].at[pl.ds(pl.multiple_of(row0 + _dev_index(*chip, 1 - c) * r, BF16_ROWS), r), :], outs[0].at[k],
                        scr[0].at[k], scr[1].at[k], (x, y, 1 - c)) for k, chip in enumerate(chips)]

    def start(ins, outs, scr):
        for cp in descs(ins, outs, scr):
            cp.start()

    def finish(ins, outs, scr):
        cps = descs(ins, outs, scr)
        for cp in cps:
            cp.wait_recv()
        for cp in cps:
            cp.wait_send()

    comm = _Comm([g], [jax.ShapeDtypeStruct((4, r, d), BF16)], {},
                 [pltpu.SemaphoreType.DMA((4,)), pltpu.SemaphoreType.DMA((4,))], start, finish)
    return comm


def _pair_sum(g, pair, r, name, after=(), row0=0):
    d = g.shape[1]

    def body(g_ref, p_ref, *rest):
        o_ref, gbuf, pbuf, sems = rest[len(after):]
        x, y, c = _mesh_pos()
        loads = [pltpu.make_async_copy(p_ref.at[pl.ds(1, 3)], pbuf, sems.at[3])]
        for k, chip in enumerate(_other_chips(x, y)):
            j = _dev_index(*chip, c)
            loads.append(pltpu.make_async_copy(g_ref.at[pl.ds(pl.multiple_of(row0 + j * r, BF16_ROWS), r), :], gbuf.at[k], sems.at[k]))
        for cp in loads:
            cp.start()
        for cp in loads:
            cp.wait()
        for k in range(3):
            o_ref[k] = (gbuf[k].astype(F32) + pbuf[k].astype(F32)).astype(BF16)

    return pl.pallas_call(
        body, out_shape=jax.ShapeDtypeStruct((3, r, d), BF16), in_specs=[ANY] * (2 + len(after)), out_specs=VMEM,
        scratch_shapes=[pltpu.VMEM((3, r, d), BF16), pltpu.VMEM((3, r, d), BF16), pltpu.SemaphoreType.DMA((4,))],
        name=name, compiler_params=_cparams())(g, pair, *after)


def _chip_start(sums, name, barrier_id):
    n = len(sums)

    def body(*refs):
        srcs, lands = refs[:n], refs[n:2 * n]
        sends, recvs = refs[2 * n:3 * n], refs[3 * n:4 * n]
        _chips_handshake()
        x, y, c = _mesh_pos()
        for i in range(n):
            for k, chip in enumerate(_other_chips(x, y)):
                _remote(srcs[i].at[k], lands[i].at[k], sends[i], recvs[i], (*chip, c)).start()
        refs[-1][...] = jnp.zeros_like(refs[-1])

    zones = [pltpu.HBM(s.shape, s.dtype) for s in sums]
    res = pl.pallas_call(
        body, name=name,
        out_shape=tuple([pltpu.SemaphoreType.DMA(())] * (2 * n) + zones + zones + [jax.ShapeDtypeStruct((SUBLANE, LANE), F32)]),
        in_specs=[HBM] * (2 * n), out_specs=tuple([SEM] * (2 * n) + [HBM] * (2 * n) + [VMEM]),
        input_output_aliases={i: 2 * n + i for i in range(2 * n)},
        compiler_params=pltpu.CompilerParams(has_side_effects=FLOWS, collective_id=barrier_id))(
            *[_in_hbm(s) for s in sums], *[_in_hbm(lax.empty(s.shape, s.dtype)) for s in sums])
    flights = [((res[i], res[n + i]), res[2 * n + i], res[3 * n + i]) for i in range(n)]
    return flights, res[-1]


def _chip_wait(sums, land, sems, after, name):
    def body(sums_ref, land_ref, send_sem, recv_sem, after_ref, sums_out, land_out):
        x, y, c = _mesh_pos()
        cp = _remote(sums_ref, land_ref, send_sem, recv_sem, (x, y, 1 - c))
        cp.wait_send()
        cp.wait_recv()

    res = pl.pallas_call(
        body, name=name, out_shape=(pltpu.HBM(sums.shape, sums.dtype), pltpu.HBM(land.shape, land.dtype)),
        in_specs=[HBM, HBM, SEM, SEM, ANY], out_specs=(HBM, HBM), input_output_aliases={0: 0, 1: 1},
        compiler_params=pltpu.CompilerParams(has_side_effects=FLOWS))(sums, land, sems[0], sems[1], after)
    return res[0], res[1]


class _CopyThrough:
    def __init__(self, src_ref, dst_ref, dst_row0, n_rows, buf, sem_in, sem_out):
        rc = n_rows // COPY_PIECES
        piece = lambda ref, o: ref.at[pl.ds(o, rc), :]
        self.loads = [pltpu.make_async_copy(piece(src_ref, k * rc), piece(buf, k * rc), sem_in) for k in range(COPY_PIECES)]
        self.stores = [pltpu.make_async_copy(piece(buf, k * rc), piece(dst_ref, dst_row0 + k * rc), sem_out) for k in range(COPY_PIECES)]
        self.all_in = pltpu.make_async_copy(src_ref, buf, sem_in)
        self.all_out = pltpu.make_async_copy(buf, dst_ref.at[pl.ds(dst_row0, n_rows), :], sem_out)

    def load(self):
        for cp in self.loads:
            cp.start()

    def store(self):
        self.all_in.wait()
        for cp in self.stores:
            cp.start()

    def done(self):
        self.all_out.wait()


def _gather_first(shards, sm, jobs, x2, tgt2, t_rows, x0):
    d = shards[0].shape[1]
    rows = [w.shape[0] for w in shards]
    lo = [sum(rows[:p]) for p in range(N_BIG)]
    n_sems = GATHER_SEMS * len(jobs)
    seq = x2.shape[0]
    assert x0 == ROW_ALIGN and seq % ROW_ALIGN == 0 and d == N_DEV * LANE

    def body(s0, s1, s2, s3, s4, sm_ref, x_ref, tgt_ref, wl_ref, o0, o1, o2, o3, o4, sa_ref, h0_ref, tp_ref,
             wl_v, x_v, tgt_v, heads_v, sa_v, send_sems, recv_sems, ssend, srecv, local_sems, sems_in, sems_out):
        dests = (o0, o1, o2, o3, o4)
        x, y, c = _mesh_pos()
        me = (x, y, c)
        jme = _dev_index(*me)
        peers = [(x, y, 1 - c)] + [(*chip, pc) for pc in (c, 1 - c) for chip in _other_chips(x, y)]
        barrier = pltpu.get_barrier_semaphore()
        for to in peers:
            pl.semaphore_signal(barrier, inc=1, device_id=to, device_id_type=MESH)
        pl.semaphore_wait(barrier, len(peers))
        padded = [_CopyThrough(x_ref, h0_ref, x0, seq, x_v, sems_in.at[0], sems_out.at[0]),
                  _CopyThrough(tgt_ref, tp_ref, x0, seq, tgt_v, sems_in.at[1], sems_out.at[1])]
        for cp in padded:
            cp.load()
        shard_refs = (s0, s1, s2, s3, s4)
        first = sorted({j[0] for j in jobs})
        for p in first + [p for p in range(N_BIG) if p not in first]:
            wl_v[pl.ds(lo[p], rows[p]), :] = shard_refs[p][...].astype(BF16)
            if p == first[-1]:
                gather = _Gather(jobs, rows, lo, wl_v, dict(enumerate(dests)), send_sems, recv_sems)
                gather.start()
        smalls = [_remote(sm_ref, sa_ref.at[jme], ssend.at[k], srecv.at[k], to) for k, to in enumerate(peers)]
        for cp in smalls:
            cp.start()
        mine = [pltpu.make_async_copy(wl_v.at[pl.ds(lo[p], rows[p]), :],
                                      dests[p].at[pl.ds(pl.multiple_of(jme * rows[p], BF16_ROWS), rows[p]), :], local_sems.at[p])
                for p in range(N_BIG)]
        mine.append(pltpu.make_async_copy(wl_v, wl_ref, local_sems.at[N_BIG]))
        mine.append(pltpu.make_async_copy(sm_ref, sa_ref.at[jme], local_sems.at[N_BIG + 1]))
        for cp in mine:
            cp.start()
        later = [p for p in range(N_BIG) if p not in {j[0] for j in jobs}]
        own = [_remote(wl_v.at[pl.ds(lo[p], rows[p]), :], dests[p].at[pl.ds(pl.multiple_of(jme * rows[p], BF16_ROWS), rows[p]), :],
                       ssend.at[7 + i], srecv.at[7 + i], (x, y, 1 - c)) for i, p in enumerate(later)]
        for cp in own:
            cp.start()
        gather.mid()
        for cp in padded:
            cp.store()
        for cp in smalls + own:
            cp.wait_recv()
        mine[-1].wait()
        to_v = pltpu.make_async_copy(sa_ref, sa_v, local_sems.at[N_BIG + 1])
        to_v.start()
        to_v.wait()
        head, zeros = heads_v.at[0], heads_v.at[1]
        head[...] = jnp.zeros_like(head)
        zeros[...] = jnp.zeros_like(zeros)
        for j in range(N_DEV):
            head[pl.ds(x0 - N_META, N_META), pl.ds(j * LANE, LANE)] = sa_v[j, pl.ds(0, N_META), :]
        heads = [pltpu.make_async_copy(head, h0_ref.at[pl.ds(0, x0), :], local_sems.at[N_BIG + 1]),
                 pltpu.make_async_copy(zeros, tp_ref.at[pl.ds(0, x0), :], local_sems.at[N_BIG + 2])]
        for cp in heads:
            cp.start()
        gather.finish()
        for cp in smalls + own:
            cp.wait_send()
        for cp in mine[:-1] + heads:
            cp.wait()
        for cp in padded:
            cp.done()

    out_shape = [jax.ShapeDtypeStruct((sum(rows), d), BF16)]
    out_shape += [jax.ShapeDtypeStruct((N_DEV * r, d), BF16) for r in rows]
    out_shape.append(jax.ShapeDtypeStruct((N_DEV,) + sm.shape, F32))
    out_shape += [jax.ShapeDtypeStruct((t_rows, d), F32)] * 2
    res = pl.pallas_call(
        body, out_shape=out_shape, in_specs=[VMEM] * 6 + [ANY] * 2, out_specs=[ANY] * 9,
        scratch_shapes=[pltpu.VMEM((sum(rows), d), BF16), pltpu.VMEM((seq, d), F32), pltpu.VMEM((seq, d), F32),
                        pltpu.VMEM((2, ROW_ALIGN, d), F32), pltpu.VMEM((N_DEV,) + sm.shape, F32),
                        pltpu.SemaphoreType.DMA((n_sems,)), pltpu.SemaphoreType.DMA((n_sems,)),
                        pltpu.SemaphoreType.DMA((7 + N_BIG,)), pltpu.SemaphoreType.DMA((7 + N_BIG,)),
                        pltpu.SemaphoreType.DMA((N_BIG + 3,)), pltpu.SemaphoreType.DMA((2,)), pltpu.SemaphoreType.DMA((2,))],
        name="gather_first",
        compiler_params=pltpu.CompilerParams(vmem_limit_bytes=VMEM_LIMIT, collective_id=ALL_PEERS_BARRIER_ID))(*shards, sm, x2, tgt2)
    return res[0], list(res[1:1 + N_BIG]), res[1 + N_BIG], res[2 + N_BIG], res[3 + N_BIG]


def _in_proj(h0, g1, win_t, tm, comm):
    t_rows, d = h0.shape
    e = win_t.shape[0]

    def body(h_ref, g_ref, w_ref, xn_ref, hin_ref):
        h = h_ref[...]
        xn = ((h * _rstd(h)) * g_ref[...]).astype(BF16)
        xn_ref[...] = xn
        for o, n in _chunks(e, N_CHUNK):
            hin_ref[:, pl.ds(o, n)] = _dot_nt(xn, w_ref[pl.ds(o, n), :])

    return _host_call(
        body, grid=(t_rows // tm,),
        in_specs=[pl.BlockSpec((tm, d), lambda i: (i, 0)), _full((1, d)), _resident((e, d))],
        out_specs=[pl.BlockSpec((tm, d), lambda i: (i, 0)), pl.BlockSpec((tm, e), lambda i: (i, 0))],
        out_shape=[jax.ShapeDtypeStruct((t_rows, d), BF16), jax.ShapeDtypeStruct((t_rows, e), F32)],
        args=(h0, g1, win_t), name="in_proj", comm=comm)


def _tap_slot(off):
    return off % SUBLANE, (off // SUBLANE) * SUBLANE


def _fill_shifted(sh_ref, base_ref, residues, n_rows):
    for r in residues:
        if r:
            sh_ref[r] = base_ref[pl.ds(r, n_rows), :]


def _shifted_rows(pair, r, start, n):
    base_ref, sh_ref = pair
    return base_ref[pl.ds(start, n), :] if r == 0 else sh_ref[r, pl.ds(start, n), :]


def _mix_conv_fwd(hin, wa, wb, bb, wa_w, comm):
    t_rows = hin.shape[0]
    nt = wa_w // LANE
    ka, kb = wa.shape[0], wb.shape[0]
    nr = CONV_HALO + t_rows

    def body(bg_ref, cg_ref, ha_ref, val_ref, gt_ref, wa_ref, wb_ref, bb_ref, ya_ref, z_ref, base, sh):
        base[pl.ds(0, CONV_HALO), :] = jnp.zeros((CONV_HALO, LANE), F32)
        base[pl.ds(nr, SUBLANE), :] = jnp.zeros((SUBLANE, LANE), F32)

        def conv(w_ref, k_taps, b, n):
            acc = None
            for k in range(k_taps):
                r, q = _tap_slot(CONV_HALO - (k_taps - 1) + k)
                term = w_ref[pl.ds(k, 1), :] * _shifted_rows((base, sh), r, b + q, n)
                acc = term if acc is None else acc + term
            return acc

        def fill_a(b, c):
            base[pl.ds(CONV_HALO + b, CONV_CHUNK), :] = cg_ref[pl.ds(b, CONV_CHUNK), :] * ha_ref[pl.ds(b, CONV_CHUNK), :]
            return c

        _row_loop(t_rows, CONV_CHUNK, fill_a)
        _fill_shifted(sh, base, sorted({_tap_slot(CONV_HALO - (ka - 1) + k)[0] for k in range(ka)}), nr)

        def out_a(b, c):
            ya_ref[pl.ds(b, CONV_CHUNK), :] = (bg_ref[pl.ds(b, CONV_CHUNK), :] * conv(wa_ref, ka, b, CONV_CHUNK)).astype(BF16)
            return c

        _row_loop(t_rows, CONV_CHUNK, out_a)

        def fill_b(b, c):
            base[pl.ds(CONV_HALO + b, CONV_CHUNK), :] = (val_ref[pl.ds(b, CONV_CHUNK), :]
                                                          * jax.nn.sigmoid(gt_ref[pl.ds(b, CONV_CHUNK), :]))
            return c

        _row_loop(t_rows, CONV_CHUNK, fill_b)
        _fill_shifted(sh, base, range(SUBLANE), nr)

        def out_b(b, c):
            z_ref[pl.ds(b, CONV_CHUNK), :] = conv(wb_ref, kb, b, CONV_CHUNK) + bb_ref[...]
            return c

        _row_loop(t_rows, CONV_CHUNK, out_b)

    def col(g):
        return pl.BlockSpec((t_rows, LANE), lambda i, g=g: (0, g * nt + i))

    tile = lambda rows: pl.BlockSpec((rows, LANE), lambda i: (0, i))
    return _host_call(
        body, grid=(nt,),
        in_specs=[col(0), col(1), col(2), col(3), col(4), tile(ka), tile(kb), tile(1)],
        out_specs=[tile(t_rows), tile(t_rows)],
        out_shape=[jax.ShapeDtypeStruct((t_rows, wa_w), BF16), jax.ShapeDtypeStruct((t_rows, wa_w), F32)],
        scratch_shapes=[pltpu.VMEM((nr + SUBLANE, LANE), F32), pltpu.VMEM((SUBLANE, nr, LANE), F32)],
        args=(hin, hin, hin, hin, hin, wa, wb, bb), name="mix_conv_fwd", comm=comm)


def _ln_parts(z, lg, lb):
    mu = jnp.mean(z, axis=-1, keepdims=True)
    zc = z - mu
    rstd = lax.rsqrt(jnp.mean(zc * zc, axis=-1, keepdims=True) + LN_EPS)
    zh = zc * rstd
    return zh, rstd, zh * lg + lb


def _out_proj(ya, z, lg, lb, w_out, h0, g2, g3, tm, comm):
    t_rows, d = h0.shape
    w = z.shape[1]

    def body(ya_ref, z_ref, lg_ref, lb_ref, w_ref, h0_ref, g2_ref, g3_ref, y_ref, mix_ref, h1_ref, xn2_ref):
        _, _, ln = _ln_parts(z_ref[...], lg_ref[...], lb_ref[...])
        y_ref[:, pl.ds(0, w)] = ya_ref[...]
        y_ref[:, pl.ds(w, w)] = (ln * jax.nn.sigmoid(ln)).astype(BF16)
        mix = _dot_nn(y_ref[...], w_ref[...])
        mix_ref[...] = mix
        h1 = h0_ref[...] + (mix * _rstd(mix)) * g2_ref[...]
        h1_ref[...] = h1
        xn2_ref[...] = ((h1 * _rstd(h1)) * g3_ref[...]).astype(BF16)

    blk = pl.BlockSpec((tm, d), lambda i: (i, 0))
    half = pl.BlockSpec((tm, w), lambda i: (i, 0))
    return _host_call(
        body, grid=(t_rows // tm,),
        in_specs=[half, half, _full((1, w)), _full((1, w)), _resident(w_out.shape), blk, _full((1, d)), _full((1, d))],
        out_specs=[blk, blk, blk, blk],
        out_shape=[jax.ShapeDtypeStruct((t_rows, d), BF16), jax.ShapeDtypeStruct((t_rows, d), F32),
                   jax.ShapeDtypeStruct((t_rows, d), F32), jax.ShapeDtypeStruct((t_rows, d), BF16)],
        args=(ya, z, lg, lb, w_out, h0, g2, g3), name="out_proj", comm=comm)


def _gate_up(xn2, wg_t, wu_t, tm, comm):
    t_rows, d = xn2.shape
    f = wg_t.shape[0]

    def body(x_ref, wg_ref, wu_ref, ga_ref, gu_ref, s_ref):
        xn = x_ref[...]
        for o, n in _chunks(f, N_CHUNK):
            a = _dot_nt(xn, wg_ref[pl.ds(o, n), :])
            u = _dot_nt(xn, wu_ref[pl.ds(o, n), :])
            sig = jax.nn.sigmoid(a)
            silu = a * sig
            s = silu * u
            gu_ref[:, pl.ds(o, n)] = silu.astype(BF16)
            ga_ref[:, pl.ds(o, n)] = ((u - s) * sig + s).astype(BF16)
            s_ref[:, pl.ds(o, n)] = s.astype(BF16)

    blk = pl.BlockSpec((tm, f), lambda i: (i, 0))
    return _host_call(
        body, grid=(t_rows // tm,),
        in_specs=[pl.BlockSpec((tm, d), lambda i: (i, 0)), _resident((f, d)), _resident((f, d))],
        out_specs=[blk, blk, blk], out_shape=[jax.ShapeDtypeStruct((t_rows, f), BF16)] * 3,
        args=(xn2, wg_t, wu_t), name="gate_up", comm=comm)


def _down_loss(s, wd, h1, tgt, g4, tm, x0):
    t_rows, d = h1.shape
    f = wd.shape[0]

    def body(s_ref, w_ref, h1_ref, tgt_ref, g4_ref, dh2_ref, dff_ref, dg4_ref, loss_ref):
        i = pl.program_id(0)
        ff = _dot_nn(s_ref[...], w_ref[...])
        r4 = _rstd(ff)
        fh = ff * r4
        g4 = g4_ref[...]
        h2 = h1_ref[...] + fh * g4
        row = i * tm + lax.broadcasted_iota(jnp.int32, (tm, 1), 0)
        diff = jnp.where(row >= x0, h2 - tgt_ref[...], 0.0)
        dh2 = diff / d
        dh2_ref[...] = dh2
        dff_ref[...] = _rms_bwd(dh2 * g4, fh, r4).astype(BF16)
        _acc_rows(dg4_ref, dh2 * fh, i == 0)
        _acc_rows(loss_ref, diff * diff, i == 0)

    blk = pl.BlockSpec((tm, d), lambda i: (i, 0))
    res, _ = _host_call(
        body, grid=(t_rows // tm,),
        in_specs=[pl.BlockSpec((tm, f), lambda i: (i, 0)), _resident((f, d)), blk, blk, _full((1, d))],
        out_specs=[blk, blk, _full((1, d)), _full((1, d))],
        out_shape=[jax.ShapeDtypeStruct((t_rows, d), F32), jax.ShapeDtypeStruct((t_rows, d), BF16),
                   jax.ShapeDtypeStruct((1, d), F32), jax.ShapeDtypeStruct((1, d), F32)],
        args=(s, wd, h1, tgt, g4), name="down_loss")
    return res


def _bwd_down(dff, wd, ga, gu, tm, comm):
    t_rows, d = dff.shape
    f = wd.shape[0]

    def body(dff_ref, w_ref, ga_ref, gu_ref, dau_ref):
        dff_v = dff_ref[...]
        for o, n in _chunks(f, N_CHUNK):
            ds = _dot_nt(dff_v, w_ref[pl.ds(o, n), :]).astype(BF16)
            dau_ref[0, :, pl.ds(o, n)] = ds * ga_ref[:, pl.ds(o, n)]
            dau_ref[1, :, pl.ds(o, n)] = ds * gu_ref[:, pl.ds(o, n)]

    blk = pl.BlockSpec((tm, f), lambda i: (i, 0))
    (dau,), extra = _host_call(
        body, grid=(t_rows // tm,),
        in_specs=[pl.BlockSpec((tm, d), lambda i: (i, 0)), _resident((f, d)), blk, blk],
        out_specs=[pl.BlockSpec((2, tm, f), lambda i: (0, i, 0))], out_shape=[jax.ShapeDtypeStruct((2, t_rows, f), BF16)],
        args=(dff, wd, ga, gu), name="bwd_down", comm=comm)
    return dau, extra


def _wgrad(a, b, name, after=(), comm=None):
    d = b.shape[1]
    t_rows = b.shape[0]
    stacked = a.ndim == 3
    n = a.shape[-1]
    groups = a.shape[0] if stacked else 1
    steps = 1 if stacked else 2
    tile = max(t for t in range(LANE, min(n // steps, WGRAD_TILE_MAX) + 1, LANE) if n % t == 0)
    tiles = n // tile

    def body(a_ref, b_ref, o_ref):
        o_ref[...] = lax.dot_general(a_ref[...], b_ref[...], (((0,), (0,)), ((), ())),
                                     preferred_element_type=F32).astype(BF16)

    if stacked:
        a_spec = pl.BlockSpec((None, t_rows, tile), lambda g, i: (g, 0, i))
    else:
        a_spec = pl.BlockSpec((t_rows, tile), lambda g, i: (0, i))
    res, hosted = _host_call(
        body, grid=(groups, tiles), in_specs=[a_spec, _resident((t_rows, d))],
        out_specs=[pl.BlockSpec((tile, d), lambda g, i: (g * tiles + i, 0))],
        out_shape=[jax.ShapeDtypeStruct((groups * n, d), BF16)], args=(a, b), name=name, comm=comm, after=after)
    return res[0] if comm is None else (res[0], hosted)


def _bwd_ffn_in(dau, wg_t, wu_t, h1, dh2, g3, tm, comm):
    t_rows, d = h1.shape
    f = wg_t.shape[0]

    def body(dau_ref, wg_ref, wu_ref, h1_ref, dh2_ref, g3_ref, dh1_ref, dg3_ref):
        dxn2 = _dot_nn(dau_ref[0], wg_ref[...]) + _dot_nn(dau_ref[1], wu_ref[...])
        h1 = h1_ref[...]
        r3 = _rstd(h1)
        h1h = h1 * r3
        _acc_rows(dg3_ref, dxn2 * h1h, pl.program_id(0) == 0)
        dh1_ref[...] = dh2_ref[...] + _rms_bwd(dxn2 * g3_ref[...], h1h, r3)

    blk = pl.BlockSpec((tm, d), lambda i: (i, 0))
    return _host_call(
        body, grid=(t_rows // tm,),
        in_specs=[pl.BlockSpec((2, tm, f), lambda i: (0, i, 0)), _resident((f, d)), _resident((f, d)), blk, blk, _full((1, d))],
        out_specs=[blk, _full((1, d))],
        out_shape=[jax.ShapeDtypeStruct((t_rows, d), F32), jax.ShapeDtypeStruct((1, d), F32)],
        args=(dau, wg_t, wu_t, h1, dh2, g3), name="bwd_ffn_in", comm=comm)


def _bwd_out_proj(dh1, mix, w_out, g2, z, lg, lb, tm, after):
    t_rows, d = dh1.shape
    w = z.shape[1]

    def body(dh1_ref, mix_ref, w_ref, g2_ref, z_ref, lg_ref, lb_ref, dmix_ref, dya_ref, dz_ref, dg2_ref, dlg_ref, dlb_ref, dbb_ref):
        first = pl.program_id(0) == 0
        mix = mix_ref[...]
        r2 = _rstd(mix)
        mh = mix * r2
        dh1 = dh1_ref[...]
        _acc_rows(dg2_ref, dh1 * mh, first)
        dmix = _rms_bwd(dh1 * g2_ref[...], mh, r2).astype(BF16)
        dmix_ref[...] = dmix
        dy = _dot_nt(dmix, w_ref[...])
        dya_ref[...] = dy[:, :w]
        lg = lg_ref[...]
        zh, rstd, ln = _ln_parts(z_ref[...], lg, lb_ref[...])
        dln = dy[:, w:] * _silu_grad(ln, jax.nn.sigmoid(ln))
        _acc_rows(dlg_ref, dln * zh, first)
        _acc_rows(dlb_ref, dln, first)
        dzh = dln * lg
        dz = rstd * (dzh - jnp.mean(dzh, axis=-1, keepdims=True) - zh * jnp.mean(dzh * zh, axis=-1, keepdims=True))
        dz_ref[...] = dz
        _acc_rows(dbb_ref, dz, first)

    blk = pl.BlockSpec((tm, d), lambda i: (i, 0))
    half = pl.BlockSpec((tm, w), lambda i: (i, 0))
    vec = _full((1, w))
    res, _ = _host_call(
        body, grid=(t_rows // tm,), in_specs=[blk, blk, _resident(w_out.shape), _full((1, d)), half, vec, vec],
        out_specs=[blk, half, half, _full((1, d)), vec, vec, vec],
        out_shape=[jax.ShapeDtypeStruct((t_rows, d), BF16), jax.ShapeDtypeStruct((t_rows, w), F32),
                   jax.ShapeDtypeStruct((t_rows, w), F32), jax.ShapeDtypeStruct((1, d), F32)]
        + [jax.ShapeDtypeStruct((1, w), F32)] * 3,
        args=(dh1, mix, w_out, g2, z, lg, lb), name="bwd_out_proj", after=after)
    return res


def _mix_conv_bwd(hin, dy, dz, wa, wb, wa_w, comm):
    t_rows = hin.shape[0]
    nt = wa_w // LANE
    ka, kb = wa.shape[0], wb.shape[0]
    nr = CONV_HALO + t_rows
    kb_rows = -(-kb // SUBLANE) * SUBLANE

    def body(bg_ref, cg_ref, ha_ref, val_ref, gt_ref, dya_ref, dz_ref, wa_ref, wb_ref,
             dh_ref, dwa_ref, dwb_ref, base, sh, based, shd, tmp, wbc):
        zeros = lambda n: jnp.zeros((n, LANE), F32)
        base[pl.ds(0, CONV_HALO), :] = zeros(CONV_HALO)
        base[pl.ds(nr, SUBLANE), :] = zeros(SUBLANE)
        based[pl.ds(t_rows, CONV_HALO + SUBLANE), :] = zeros(CONV_HALO + SUBLANE)

        def fwd_slot(k_taps, k):
            return _tap_slot(CONV_HALO - (k_taps - 1) + k)

        def bwd_slot(k_taps, k):
            return _tap_slot(k_taps - 1 - k)

        def conv(w_ref, k_taps, src, slot, b, n):
            acc = None
            for k in range(k_taps):
                r, q = slot(k_taps, k)
                term = w_ref[pl.ds(k, 1), :] * _shifted_rows(src, r, b + q, n)
                acc = term if acc is None else acc + term
            return acc

        def by_residue(k_taps, slot):
            groups = {}
            for k in range(k_taps):
                r, q = slot(k_taps, k)
                groups.setdefault(r, []).append((k, q // SUBLANE))
            return groups

        def wgrad_loop(w_ref, k_taps):
            n_sub = WGRAD_ROWS // SUBLANE
            for k in range(k_taps):
                wbc[k] = jnp.broadcast_to(w_ref[pl.ds(k, 1), :], (SUBLANE, LANE))
            fwd, bwd = by_residue(k_taps, fwd_slot), by_residue(k_taps, bwd_slot)

            def window(src, r, taps, b):
                span = n_sub + max(qi for _, qi in taps)
                return [_shifted_rows(src, r, b + SUBLANE * i, SUBLANE) for i in range(span)]

            def step(b, accs):
                accs = list(accs)
                dv = [based[pl.ds(b + SUBLANE * j, SUBLANE), :] for j in range(n_sub)]
                for r, taps in fwd.items():
                    win = window((base, sh), r, taps, b)
                    for k, qi in taps:
                        t = dv[0] * win[qi]
                        for j in range(1, n_sub):
                            t = t + dv[j] * win[qi + j]
                        accs[k] = accs[k] + t
                outs = [None] * n_sub
                for r, taps in bwd.items():
                    win = window((based, shd), r, taps, b)
                    for k, qi in taps:
                        wk = wbc[k]
                        for j in range(n_sub):
                            term = wk * win[qi + j]
                            outs[j] = term if outs[j] is None else outs[j] + term
                for j in range(n_sub):
                    tmp[pl.ds(b + SUBLANE * j, SUBLANE), :] = outs[j]
                return tuple(accs)

            return _row_loop(t_rows, WGRAD_ROWS, step, tuple(zeros(SUBLANE) for _ in range(k_taps)))

        def store_taps(ref, accs, rows):
            for k, acc in enumerate(accs):
                ref[pl.ds(k, 1), :] = jnp.sum(acc, axis=0, keepdims=True)
            if rows > len(accs):
                ref[pl.ds(len(accs), rows - len(accs)), :] = zeros(rows - len(accs))

        def fill_a(b, c):
            sl = pl.ds(b, CONV_CHUNK)
            base[pl.ds(CONV_HALO + b, CONV_CHUNK), :] = cg_ref[sl, :] * ha_ref[sl, :]
            based[sl, :] = dya_ref[sl, :] * bg_ref[sl, :]
            return c

        _row_loop(t_rows, CONV_CHUNK, fill_a)
        _fill_shifted(sh, base, sorted({fwd_slot(ka, k)[0] for k in range(ka)}), nr)
        _fill_shifted(shd, based, sorted({bwd_slot(ka, k)[0] for k in range(ka)}), nr)

        def d_bgate(b, c):
            sl = pl.ds(b, CONV_CHUNK)
            dh_ref[0, sl, :] = (dya_ref[sl, :] * conv(wa_ref, ka, (base, sh), fwd_slot, b, CONV_CHUNK)).astype(BF16)
            return c

        _row_loop(t_rows, CONV_CHUNK, d_bgate)
        store_taps(dwa_ref, wgrad_loop(wa_ref, ka), SUBLANE)

        def d_ch(b, c):
            sl = pl.ds(b, CONV_CHUNK)
            dua = tmp[sl, :]
            dh_ref[1, sl, :] = (dua * ha_ref[sl, :]).astype(BF16)
            dh_ref[2, sl, :] = (dua * cg_ref[sl, :]).astype(BF16)
            return c

        _row_loop(t_rows, CONV_CHUNK, d_ch)

        def fill_b(b, c):
            sl = pl.ds(b, CONV_CHUNK)
            base[pl.ds(CONV_HALO + b, CONV_CHUNK), :] = val_ref[sl, :] * jax.nn.sigmoid(gt_ref[sl, :])
            based[sl, :] = dz_ref[sl, :]
            return c

        _row_loop(t_rows, CONV_CHUNK, fill_b)
        _fill_shifted(sh, base, range(SUBLANE), nr)
        _fill_shifted(shd, based, range(SUBLANE), nr)
        store_taps(dwb_ref, wgrad_loop(wb_ref, kb), kb_rows)

        def d_glu(b, c):
            sl = pl.ds(b, CONV_CHUNK)
            dgg = tmp[sl, :]
            sig = jax.nn.sigmoid(gt_ref[sl, :])
            dh_ref[3, sl, :] = (dgg * sig).astype(BF16)
            dh_ref[4, sl, :] = (dgg * val_ref[sl, :] * (sig * (1.0 - sig))).astype(BF16)
            return c

        _row_loop(t_rows, CONV_CHUNK, d_glu)

    def col(g):
        return pl.BlockSpec((t_rows, LANE), lambda i, g=g: (0, g * nt + i))

    tile = lambda rows: pl.BlockSpec((rows, LANE), lambda i: (0, i))
    return _host_call(
        body, grid=(nt,),
        in_specs=[col(0), col(1), col(2), col(3), col(4), tile(t_rows), tile(t_rows), tile(ka), tile(kb)],
        out_specs=[pl.BlockSpec((5, t_rows, LANE), lambda i: (0, 0, i)), tile(SUBLANE), tile(kb_rows)],
        out_shape=[jax.ShapeDtypeStruct((5, t_rows, wa_w), BF16), jax.ShapeDtypeStruct((SUBLANE, wa_w), F32),
                   jax.ShapeDtypeStruct((kb_rows, wa_w), F32)],
        scratch_shapes=[pltpu.VMEM((nr + SUBLANE, LANE), F32), pltpu.VMEM((SUBLANE, nr, LANE), F32),
                        pltpu.VMEM((nr + SUBLANE, LANE), F32), pltpu.VMEM((SUBLANE, nr, LANE), F32),
                        pltpu.VMEM((t_rows, LANE), F32), pltpu.VMEM((kb_rows, SUBLANE, LANE), F32)],
        args=(hin, hin, hin, hin, hin, dy, dz, wa, wb), name="mix_conv_bwd", comm=comm)


def _bwd_in_proj(dh5, win_t, h0, dh1, g1, tm, x0, comm, after=()):
    t_rows, d = h0.shape
    groups, _, w = dh5.shape
    assert x0 <= tm

    def body(dh_ref, w_ref, h0_ref, dh1_ref, g1_ref, dh0_ref, dg1_ref, dmeta_ref):
        dxn1 = None
        for g in range(groups):
            part = _dot_nn(dh_ref[g], w_ref[pl.ds(g * w, w), :])
            dxn1 = part if dxn1 is None else dxn1 + part
        h0 = h0_ref[...]
        r1 = _rstd(h0)
        h0h = h0 * r1
        _acc_rows(dg1_ref, dxn1 * h0h, pl.program_id(0) == 0)
        dh0_ref[...] = dh1_ref[...] + _rms_bwd(dxn1 * g1_ref[...], h0h, r1)

        @pl.when(pl.program_id(0) == 0)
        def _():
            dmeta_ref[...] = dh0_ref[pl.ds(x0 - N_META, N_META), :]

    blk = pl.BlockSpec((tm, d), lambda i: (i, 0))
    return _host_call(
        body, grid=(t_rows // tm,),
        in_specs=[pl.BlockSpec((groups, tm, w), lambda i: (0, i, 0)), _resident(win_t.shape), blk, blk, _full((1, d))],
        out_specs=[blk, _full((1, d)), _full((N_META, d))],
        out_shape=[jax.ShapeDtypeStruct((t_rows, d), F32), jax.ShapeDtypeStruct((1, d), F32), jax.ShapeDtypeStruct((N_META, d), F32)],
        args=(dh5, win_t, h0, dh1, g1), name="bwd_in_proj", comm=comm, after=after)


def _pair_small(smalls, d):
    (dmeta, dg1, dg2, dg3, dg4, dbb, dlg, dlb, lossv, dwa, dwb) = smalls
    half = d // 2
    kb_rows = dwb.shape[0]

    def body(dmeta_ref, dg1_ref, dg2_ref, dg3_ref, dg4_ref, dbb_ref, dlg_ref, dlb_ref, loss_ref, dwa_ref, dwb_ref,
             sums_ref, pbuf, psib, ps_send, ps_recv):
        x, y, c = _mesh_pos()
        _pair_handshake()
        pbuf[...] = jnp.zeros_like(pbuf)
        pbuf[pl.ds(0, N_META), :] = dmeta_ref[...]
        for row, ref in ((16, dg1_ref), (17, dg2_ref), (18, dg3_ref), (19, dg4_ref)):
            pbuf[pl.ds(row, 1), :] = ref[...]
        pbuf[pl.ds(20, 1), pl.ds(0, half)] = dbb_ref[...]
        pbuf[pl.ds(20, 1), pl.ds(half, half)] = dlg_ref[...]
        pbuf[pl.ds(21, 1), pl.ds(0, half)] = dlb_ref[...]
        lv = loss_ref[...]
        pbuf[pl.ds(21, 1), pl.ds(half, half)] = lv[:, :half] + lv[:, half:]
        pbuf[pl.ds(24, SUBLANE), pl.ds(0, half)] = dwa_ref[...]
        pbuf[pl.ds(32, kb_rows), pl.ds(0, half)] = dwb_ref[...]
        to_sib = _remote(pbuf, psib, ps_send.at[0], ps_recv.at[0], (x, y, 1 - c))
        to_sib.start()
        to_sib.wait_recv()
        s = pbuf[...] + psib[...]
        for k in range(3):
            sums_ref[k] = s
        to_sib.wait_send()

    return pl.pallas_call(
        body, out_shape=jax.ShapeDtypeStruct((3, SMALL_ROWS, d), F32), in_specs=[VMEM] * 11, out_specs=VMEM,
        scratch_shapes=[pltpu.VMEM((SMALL_ROWS, d), F32), pltpu.VMEM((SMALL_ROWS, d), F32),
                        pltpu.SemaphoreType.DMA((1,)), pltpu.SemaphoreType.DMA((1,))],
        name="pair_small", compiler_params=pltpu.CompilerParams(vmem_limit_bytes=VMEM_LIMIT, collective_id=PAIR_BARRIER_ID))(*smalls)


def _total_small(own, others, ka, kb, ca):
    _, r, d = own.shape
    half, cols = d // 2, d // N_DEV

    def body(own_ref, others_ref, tot_ref, meta_ref, g1_ref, g2_ref, g3_ref, g4_ref, dbb_ref, dlg_ref, dlb_ref, loss_ref,
             ga_ref, gb_ref, chip_p):
        x, y, c = _mesh_pos()
        chip_p[2 * x + y] = own_ref[0]
        for k, (cx, cy) in enumerate(_other_chips(x, y)):
            chip_p[2 * cx + cy] = others_ref[k]
        tot_ref[...] = ((chip_p[0] + chip_p[1]) + chip_p[2]) + chip_p[3]
        meta_ref[...] = tot_ref[pl.ds(0, N_META), pl.ds(pl.multiple_of(_dev_index(x, y, c) * cols, LANE), cols)]
        for row, ref in ((16, g1_ref), (17, g2_ref), (18, g3_ref), (19, g4_ref)):
            ref[...] = tot_ref[pl.ds(row, 1), :]
        dbb_ref[...] = tot_ref[pl.ds(20, 1), pl.ds(0, half)]
        dlg_ref[...] = tot_ref[pl.ds(20, 1), pl.ds(half, half)]
        dlb_ref[...] = tot_ref[pl.ds(21, 1), pl.ds(0, half)]
        loss_ref[...] = (0.5 / d) * jnp.sum(tot_ref[pl.ds(21, 1), pl.ds(half, half)], axis=-1, keepdims=True)
        for j in range(N_DEV):
            @pl.when(_dev_index(x, y, c) == j)
            def _(j=j):
                for row0, n, ref in ((24, ka, ga_ref), (32, kb, gb_ref)):
                    for k in range(n):
                        ref[k] = tot_ref[pl.ds(row0 + k, 1), pl.ds(j * ca, ca)]

    row = lambda n: jax.ShapeDtypeStruct((1, n), F32)
    return pl.pallas_call(
        body, out_shape=[jax.ShapeDtypeStruct((r, d), F32), jax.ShapeDtypeStruct((N_META, cols), F32), row(d), row(d), row(d), row(d),
                         row(half), row(half), row(half), row(1),
                         jax.ShapeDtypeStruct((ka, 1, ca), F32), jax.ShapeDtypeStruct((kb, 1, ca), F32)],
        scratch_shapes=[pltpu.VMEM((4, r, d), F32)], name="total_small", compiler_params=_cparams())(own, others)


def _adamw(w, g, m, v):
    m = ADAM_B1 * m + (1.0 - ADAM_B1) * g
    v = ADAM_B2 * v + (1.0 - ADAM_B2) * jnp.square(g)
    m_hat = m / (1.0 - ADAM_B1 ** ADAM_STEP)
    v_hat = v / (1.0 - ADAM_B2 ** ADAM_STEP)
    delta = -ADAM_LR * (m_hat / (jnp.sqrt(v_hat) + ADAM_EPS) + ADAM_WD * w)
    return delta, m, v


def _adam_big(g, pair, part, w, m, v, name, row0=0):
    r, d = w.shape
    cols = d // ADAM_COL_BLOCKS
    assert row0 % r == 0

    def body(me_ref, g_ref, pair_ref, part_ref, w_ref, m_ref, v_ref, go_ref, d_ref, mo_ref, vo_ref):
        g = g_ref[...].astype(F32) + pair_ref[...].astype(F32)
        for k in range(3):
            g = g + part_ref[k].astype(F32)
        go_ref[...] = g
        d_ref[...], mo_ref[...], vo_ref[...] = _adamw(w_ref[...], g, m_ref[...], v_ref[...])

    blk = pl.BlockSpec((r, cols), lambda i, me_ref: (0, i))
    grid_spec = pltpu.PrefetchScalarGridSpec(
        num_scalar_prefetch=1, grid=(ADAM_COL_BLOCKS,),
        in_specs=[pl.BlockSpec((r, cols), lambda i, me_ref: (me_ref[0], i)),
                  pl.BlockSpec((None, r, cols), lambda i, me_ref: (0, 0, i)),
                  pl.BlockSpec((3, r, cols), lambda i, me_ref: (0, 0, i)), blk, blk, blk],
        out_specs=[blk, blk, blk, blk])
    me = jnp.reshape(_dev_index(*_mesh_pos()) + row0 // r, (1,)).astype(jnp.int32)
    return pl.pallas_call(body, out_shape=[jax.ShapeDtypeStruct((r, d), F32)] * 4, grid_spec=grid_spec, name=name,
                          compiler_params=_cparams(1))(me, g, pair, part, w, m, v)


def _adam_small(gs, ws, ms, vs):
    n = len(gs)

    def body(*refs):
        ins, outs = refs[:4 * n], refs[4 * n:]
        for i in range(n):
            g = ins[i][...]
            delta, m, v = _adamw(ins[n + i][...], g, ins[2 * n + i][...], ins[3 * n + i][...])
            outs[i][...] = delta
            outs[n + i][...] = m
            outs[2 * n + i][...] = v

    shapes = [jax.ShapeDtypeStruct(w.shape, F32) for w in ws]
    return pl.pallas_call(body, out_shape=shapes * 3, name="adam_small", compiler_params=_cparams())(*gs, *ws, *ms, *vs)


def kernel(x, meta_tokens, pre_mix_norm, w_in, conv_a_w, conv_b_w, conv_b_bias, ln_b_gain, ln_b_bias, w_out, post_mix_norm, pre_ffn_norm, w_gate, w_up, w_down, post_ffn_norm, loss_target, m_meta_tokens, m_pre_mix_norm, m_w_in, m_conv_a_w, m_conv_b_w, m_conv_b_bias, m_ln_b_gain, m_ln_b_bias, m_w_out, m_post_mix_norm, m_pre_ffn_norm, m_w_gate, m_w_up, m_w_down, m_post_ffn_norm, v_meta_tokens, v_pre_mix_norm, v_w_in, v_conv_a_w, v_conv_b_w, v_conv_b_bias, v_ln_b_gain, v_ln_b_bias, v_w_out, v_post_mix_norm, v_pre_ffn_norm, v_w_gate, v_w_up, v_w_down, v_post_ffn_norm):
    _, seq, d = x.shape
    ka, ca_loc = conv_a_w.shape[1:]
    kb, cb_loc = conv_b_w.shape[1:]
    wa_w = ca_loc * N_DEV
    assert cb_loc == ca_loc and wa_w % LANE == 0 and w_in.shape[2] * N_DEV == 5 * wa_w and 2 * wa_w == d
    pad = (-(N_META + seq)) % ROW_ALIGN
    x0 = pad + N_META
    t_rows = x0 + seq
    assert t_rows % (N_ROW_BLOCKS * BF16_ROWS) == 0 and t_rows % CONV_CHUNK == 0 and d % LANE == 0
    tm = t_rows // N_ROW_BLOCKS
    tm2 = t_rows // 2
    me = _dev_index(*_mesh_pos())

    def as_rows(w_in_like, w_out_like, w_gate_like, w_up_like, w_down_like):
        return (w_in_like[0].T, w_out_like[0], w_gate_like[0].T, w_up_like[0].T, w_down_like[0])

    w_loc = as_rows(w_in, w_out, w_gate, w_up, w_down)
    rows = [w.shape[0] for w in w_loc]
    assert all(r % ADD_CHUNK == 0 for r in rows)
    P_IN, P_OUT, P_GATE, P_UP, P_DOWN = range(N_BIG)

    assert meta_tokens.shape == (16, LANE) and ka <= 8 and 24 + kb <= SM_ROWS and ca_loc <= LANE
    rows_of = lambda a, n: jnp.pad(a, ((0, n - a.shape[0]), (0, LANE - a.shape[1])))
    sm = jnp.concatenate([meta_tokens, rows_of(conv_a_w[0], 8), rows_of(conv_b_w[0], SM_ROWS - 24)], axis=0)
    wl, wfull, sm_all, h0, tgt = _gather_first(w_loc, sm, [(P_IN, 0, rows[P_IN])], x[0], loss_target[0], t_rows, x0)
    wa =jnp.transpose(sm_all[:, 16:16 + ka, 0:ca_loc], (1, 0, 2)).reshape(ka, wa_w)
    wb = jnp.transpose(sm_all[:, 24:24 + kb, 0:cb_loc], (1, 0, 2)).reshape(kb, wa_w)

    later = (P_OUT, P_GATE, P_UP, P_DOWN)
    sems, wl, started, _ = _gather_start(wl, wfull, later, rows, START_BARRIER_IDS[0])
    for p, arr in zip(later, started):
        wfull[p] = arr

    def arrived(p, after, name):
        nonlocal wl
        wl, wfull[p] = _gather_wait(wl, wfull[p], sems[later.index(p)], after, rows[p], name)
        return _forward_comm(wfull[p], rows[p])

    (xn1, hin), _ = _in_proj(h0, pre_mix_norm, wfull[P_IN], tm, None)
    (ya, z), (wfull[P_OUT],) = _mix_conv_fwd(hin, wa, wb, conv_b_bias, wa_w, arrived(P_OUT, hin, "gather_wait_out"))
    (y, mix, h1, xn2), (wfull[P_GATE],) = _out_proj(ya, z, ln_b_gain, ln_b_bias, wfull[P_OUT], h0, post_mix_norm, pre_ffn_norm, tm2,
                                                    arrived(P_GATE, z, "gather_wait_gate"))
    arrived(P_UP, xn2, "gather_wait_up")
    wfull[P_UP] = _forward_now(wfull[P_UP], rows[P_UP], "forward_up")
    (ga, gu, s), _ = _gate_up(xn2, wfull[P_GATE], wfull[P_UP], tm, None)
    arrived(P_DOWN, s, "gather_wait_down")
    wfull[P_DOWN] = _forward_now(wfull[P_DOWN], rows[P_DOWN], "forward_down")
    dh2, dff, dg4, lossv = _down_loss(s, wfull[P_DOWN], h1, tgt, post_ffn_norm, tm, x0)

    gwd = _wgrad(s, dff, "wgrad_down")
    dau, (pair_d,) = _bwd_down(dff, wfull[P_DOWN], ga, gu, tm, _pair_comm(gwd, rows[P_DOWN]))
    (flight_d,), token = _chip_start([_pair_sum(gwd, pair_d, rows[P_DOWN], "pair_sum_down")], "chip_start_down", START_BARRIER_IDS[1])
    gw_gu = _wgrad(dau, xn2, "wgrad_gate_up", [token])
    up0 = N_DEV * rows[P_GATE]
    (dh1, dg3), (pair_g, pair_u) = _bwd_ffn_in(dau, wfull[P_GATE], wfull[P_UP], h1, dh2, pre_ffn_norm, tm,
                                               _merge_comms([_pair_comm(gw_gu, rows[P_GATE]), _pair_comm(gw_gu, rows[P_UP], up0)]))
    (flight_g, flight_u), token = _chip_start([_pair_sum(gw_gu, pair_g, rows[P_GATE], "pair_sum_gate"),
                                               _pair_sum(gw_gu, pair_u, rows[P_UP], "pair_sum_up", row0=up0)], "chip_start_gate_up",
                                              START_BARRIER_IDS[2])
    dmix, dya, dz, dg2, dlg, dlb, dbb = _bwd_out_proj(dh1, mix, wfull[P_OUT], post_mix_norm, z, ln_b_gain, ln_b_bias, tm, [token])
    (dh5, dwa, dwb), _ = _mix_conv_bwd(hin, dya, dz, wa, wb, wa_w, None)
    gwi = _wgrad(dh5, xn1, "wgrad_in", [token])
    gwo, (pair_i,) = _wgrad(y, dmix, "wgrad_out", comm=_pair_comm(gwi, rows[P_IN]))
    (flight_i,), token = _chip_start([_pair_sum(gwi, pair_i, rows[P_IN], "pair_sum_in")], "chip_start_in", START_BARRIER_IDS[5])
    (dh0, dg1, dmeta), (pair_o,) = _bwd_in_proj(dh5, wfull[P_IN], h0, dh1, pre_mix_norm, tm, x0, _pair_comm(gwo, rows[P_OUT]),
                                                [token])
    grad_x = dh0[x0:][None]
    small_sums = _pair_small((dmeta, dg1, dg2, dg3, dg4, dbb, dlg, dlb, lossv, dwa, dwb), d)
    (flight_s, flight_o), token = _chip_start([small_sums, _pair_sum(gwo, pair_o, rows[P_OUT], "pair_sum_out")],
                                              "chip_start_small_out", START_BARRIER_IDS[4])

    def landed(flight, after, tag):
        sems_p, sums, land = flight
        return _chip_wait(sums, land, sems_p, after, "chip_wait_" + tag)

    m_loc = as_rows(m_w_in, m_w_out, m_w_gate, m_w_up, m_w_down)
    v_loc = as_rows(v_w_in, v_w_out, v_w_gate, v_w_up, v_w_down)
    full_grads = {P_IN: gwi, P_OUT: gwo, P_GATE: gw_gu, P_UP: gw_gu, P_DOWN: gwd}
    pairs = {P_IN: pair_i, P_OUT: pair_o, P_GATE: pair_g, P_UP: pair_u, P_DOWN: pair_d}
    flights = {P_IN: flight_i, P_OUT: flight_o, P_GATE: flight_g, P_UP: flight_u, P_DOWN: flight_d}
    names = {P_IN: "w_in", P_OUT: "w_out", P_GATE: "w_gate", P_UP: "w_up", P_DOWN: "w_down"}
    bigs = {}

    def adam_big(p, after):
        _, part = landed(flights[p], after, names[p])
        res = _adam_big(full_grads[p], pairs[p], part, w_loc[p], m_loc[p], v_loc[p], "adam_" + names[p], up0 if p == P_UP else 0)
        bigs[names[p]] = [(o.T if p in (P_IN, P_GATE, P_UP) else o)[None] for o in res]
        return res[1]

    for p in (P_DOWN, P_GATE, P_UP, P_IN):
        token = adam_big(p, token)
    (_, g_meta, g_pre_mix, g_post_mix, g_pre_ffn, g_post_ffn, g_conv_bias, g_ln_gain, g_ln_bias, loss11, g_conv_a, g_conv_b) = _total_small(
        *landed(flight_s, token, "small"), ka, kb, ca_loc)
    loss = jnp.reshape(loss11, ())
    taps_major = lambda a: jnp.transpose(a, (1, 0, 2))
    CONV = (2, 3)
    g_small = [g_meta, g_pre_mix, g_conv_a, g_conv_b, g_conv_bias, g_ln_gain, g_ln_bias, g_post_mix, g_pre_ffn, g_post_ffn]
    w_small = [meta_tokens, pre_mix_norm, taps_major(conv_a_w), taps_major(conv_b_w), conv_b_bias, ln_b_gain, ln_b_bias,
               post_mix_norm, pre_ffn_norm, post_ffn_norm]
    m_small = [m_meta_tokens, m_pre_mix_norm, taps_major(m_conv_a_w), taps_major(m_conv_b_w), m_conv_b_bias, m_ln_b_gain,
               m_ln_b_bias, m_post_mix_norm, m_pre_ffn_norm, m_post_ffn_norm]
    v_small = [v_meta_tokens, v_pre_mix_norm, taps_major(v_conv_a_w), taps_major(v_conv_b_w), v_conv_b_bias, v_ln_b_gain,
               v_ln_b_bias, v_post_mix_norm, v_pre_ffn_norm, v_post_ffn_norm]
    small = list(_adam_small(g_small, w_small, m_small, v_small))
    n_small = len(w_small)
    for i in CONV:
        g_small[i] = taps_major(g_small[i])
        for k in range(3):
            small[k * n_small + i] = taps_major(small[k * n_small + i])
    d_small, nm_small, nv_small = small[:n_small], small[n_small:2 * n_small], small[2 * n_small:]

    adam_big(P_OUT, small[0])

    def ordered(pick_small, pick_big):
        sm_it = iter(range(n_small))
        out = []
        for name in ("s", "s", "w_in", "s", "s", "s", "s", "s", "w_out", "s", "s", "w_gate", "w_up", "w_down", "s"):
            out.append(pick_small(next(sm_it)) if name == "s" else pick_big(name))
        return out

    grads = ordered(lambda i: g_small[i], lambda n: bigs[n][0])
    deltas = ordered(lambda i: d_small[i], lambda n: bigs[n][1])
    new_m = ordered(lambda i: nm_small[i], lambda n: bigs[n][2])
    new_v = ordered(lambda i: nv_small[i], lambda n: bigs[n][3])
    return (loss, grad_x, *grads, *deltas, *new_m, *new_v)
```
